```python
import numpy as np
import jax, jax.numpy as jnp
from jax import lax

D_MODEL = 1024
BATCH = 8
SEQ = 8192
DEPTH = 1

D_CONV = D_MODEL
CONV_K = 3
HEAD_DIM = 64
N_HEADS = D_MODEL // HEAD_DIM
N_KV_HEADS = 4
GROUP = N_HEADS // N_KV_HEADS
D_ATTN = N_HEADS * HEAD_DIM
D_KV = N_KV_HEADS * HEAD_DIM
WINDOW = 128
BLOCK = 128
ROPE_THETA = 10000.0
D_FF = 2816
N_MOD = 9
EPS = 1e-6
ADA_INIT = 0.5
NEG_INF = -1e30

IN_SPLITS = [D_CONV, D_CONV, D_CONV, D_ATTN, D_KV, D_KV, D_MODEL, D_MODEL]
IN_OFFSETS = [int(o) for o in np.cumsum(IN_SPLITS)[:-1]]
D_IN = int(sum(IN_SPLITS))

kernel_name = "macaron_conv_swa_sink_hybrid"


def rms_norm(x, g):
    xf = x.astype(jnp.float32)
    y = xf * lax.rsqrt(jnp.mean(xf * xf, axis=-1, keepdims=True) + EPS)
    return (y * g.astype(jnp.float32)).astype(x.dtype)


def modulate(h, shift, scale):
    return h * (1 + scale[:, None, :]) + shift[:, None, :]


def swiglu(h, w_gu, w_down):
    a, b = jnp.split(h @ w_gu, 2, axis=-1)
    return (jax.nn.silu(a) * b) @ w_down


def rope_tables(seq):
    inv = 1.0 / (ROPE_THETA ** (jnp.arange(0, HEAD_DIM, 2, dtype=jnp.float32) / HEAD_DIM))
    ang = jnp.arange(seq, dtype=jnp.float32)[:, None] * inv[None, :]
    return jnp.cos(ang), jnp.sin(ang)


def apply_rope(t, cos, sin):
    t1, t2 = jnp.split(t.astype(jnp.float32), 2, axis=-1)
    c = cos[None, :, None, :]
    s = sin[None, :, None, :]
    return jnp.concatenate([t1 * c - t2 * s, t2 * c + t1 * s], axis=-1).astype(t.dtype)


def short_conv(u, w):
    s = u.shape[1]
    up = jnp.pad(u, ((0, 0), (CONV_K - 1, 0), (0, 0)))
    out = up[:, 0:s] * w[0]
    for j in range(1, CONV_K):
        out = out + up[:, j:j + s] * w[j]
    return out


def sliding_window_attention(q, k, v, sinks):
    bsz, s, _, _ = q.shape
    nb = s // BLOCK
    qb = q.reshape(bsz, nb, BLOCK, N_KV_HEADS, GROUP, HEAD_DIM)

    def band(t):
        tp = jnp.pad(t, ((0, 0), (BLOCK, 0), (0, 0), (0, 0)))
        tp = tp.reshape(bsz, nb + 1, BLOCK, N_KV_HEADS, HEAD_DIM)
        return jnp.concatenate([tp[:, :-1], tp[:, 1:]], axis=2)

    kb, vb = band(k), band(v)
    scores = jnp.einsum('bnqhgd,bnkhd->bnhgqk', qb, kb,
                        preferred_element_type=jnp.float32) * (HEAD_DIM ** -0.5)
    qi = jnp.arange(BLOCK)[:, None]
    kj = jnp.arange(2 * BLOCK)[None, :]
    diff = BLOCK + qi - kj
    in_window = (diff >= 0) & (diff < WINDOW)
    k_pos = (jnp.arange(nb)[:, None] - 1) * BLOCK + jnp.arange(2 * BLOCK)[None, :]
    valid = in_window[None] & (k_pos >= 0)[:, None, :]
    scores = jnp.where(valid[None, :, None, None], scores, NEG_INF)
    sink = sinks.astype(jnp.float32).reshape(N_KV_HEADS, GROUP)[None, None, :, :, None, None]
    m = jnp.maximum(jnp.max(scores, axis=-1, keepdims=True), sink)
    p = jnp.exp(scores - m)
    denom = jnp.sum(p, axis=-1, keepdims=True) + jnp.exp(sink - m)
    probs = (p / denom).astype(v.dtype)
    out = jnp.einsum('bnhgqk,bnkhd->bnqhgd', probs, vb)
    return out.reshape(bsz, s, D_ATTN)


def _fwd_setup_inputs(seed: int = 0) -> dict:
    key = jax.random.key(seed)
    ks = jax.random.split(key, 24)
    f32 = jnp.float32

    def nrm(k, shape, fan_in, mult=1.0):
        return jax.random.normal(k, shape, f32) * (mult * fan_in ** -0.5)

    def gain(k, shape):
        return 1.0 + 0.05 * jax.random.normal(k, shape, f32)

    L = DEPTH
    return {
        "x": jax.random.normal(ks[0], (BATCH, SEQ, D_MODEL), f32),
        "c": jax.random.normal(ks[1], (BATCH, D_MODEL), f32),
        "w_ada": nrm(ks[2], (L, D_MODEL, N_MOD * D_MODEL), D_MODEL, ADA_INIT),
        "b_ada": 0.02 * jax.random.normal(ks[3], (L, N_MOD * D_MODEL), f32),
        "g_ffn1": gain(ks[4], (L, D_MODEL)),
        "w1_gu": nrm(ks[5], (L, D_MODEL, 2 * D_FF), D_MODEL),
        "w1_down": nrm(ks[6], (L, D_FF, D_MODEL), D_FF),
        "g_mix": gain(ks[7], (L, D_MODEL)),
        "w_in": nrm(ks[8], (L, D_MODEL, D_IN), D_MODEL),
        "conv_w": nrm(ks[9], (L, CONV_K, D_CONV), CONV_K),
        "w_conv_proj": nrm(ks[10], (L, D_CONV, D_MODEL), D_CONV),
        "w_attn_proj": nrm(ks[11], (L, D_ATTN, D_MODEL), D_ATTN),
        "sinks": jax.random.normal(ks[12], (L, N_HEADS), f32),
        "w_out": nrm(ks[13], (L, D_MODEL, D_MODEL), D_MODEL),
        "g_ffn2": gain(ks[14], (L, D_MODEL)),
        "w2_gu": nrm(ks[15], (L, D_MODEL, 2 * D_FF), D_MODEL),
        "w2_down": nrm(ks[16], (L, D_FF, D_MODEL), D_FF),
        "g_final": gain(ks[17], (D_MODEL,)),
    }


def _fwd_reference(x, c, w_ada, b_ada, g_ffn1, w1_gu, w1_down, g_mix, w_in, conv_w,
              w_conv_proj, w_attn_proj, sinks, w_out, g_ffn2, w2_gu, w2_down, g_final):
    bsz, s, _ = x.shape
    cos, sin = rope_tables(s)
    c_act = jax.nn.silu(c)
    for l in range(DEPTH):
        mods = jnp.split(c_act @ w_ada[l] + b_ada[l], N_MOD, axis=-1)
        sh1, sc1, gt1, sh2, sc2, gt2, sh3, sc3, gt3 = mods

        h = modulate(rms_norm(x, g_ffn1[l]), sh1, sc1)
        x = x + 0.5 * gt1[:, None, :] * swiglu(h, w1_gu[l], w1_down[l])

        h = modulate(rms_norm(x, g_mix[l]), sh2, sc2)
        proj = h @ w_in[l]
        b_g, c_g, u, q, k, v, z_conv, z_attn = jnp.split(proj, IN_OFFSETS, axis=-1)

        y_conv = (b_g * short_conv(c_g * u, conv_w[l])) @ w_conv_proj[l]

        q = apply_rope(q.reshape(bsz, s, N_HEADS, HEAD_DIM), cos, sin)
        k = apply_rope(k.reshape(bsz, s, N_KV_HEADS, HEAD_DIM), cos, sin)
        v = v.reshape(bsz, s, N_KV_HEADS, HEAD_DIM)
        y_attn = sliding_window_attention(q, k, v, sinks[l]) @ w_attn_proj[l]

        merged = jax.nn.sigmoid(z_conv) * y_conv + jax.nn.sigmoid(z_attn) * y_attn
        x = x + gt2[:, None, :] * (merged @ w_out[l])

        h = modulate(rms_norm(x, g_ffn2[l]), sh3, sc3)
        x = x + 0.5 * gt3[:, None, :] * swiglu(h, w2_gu[l], w2_down[l])

    return rms_norm(x, g_final)


import jax as _jax
import jax.numpy as _jnp

TWIN_FORMAT = 'train_step'
FWD_PARAMS = ['x', 'c', 'w_ada', 'b_ada', 'g_ffn1', 'w1_gu', 'w1_down', 'g_mix', 'w_in', 'conv_w', 'w_conv_proj', 'w_attn_proj', 'sinks', 'w_out', 'g_ffn2', 'w2_gu', 'w2_down', 'g_final']
TWIN_WEIGHTS = ['w_ada', 'b_ada', 'g_ffn1', 'w1_gu', 'w1_down', 'g_mix', 'w_in', 'conv_w', 'w_conv_proj', 'w_attn_proj', 'sinks', 'w_out', 'g_ffn2', 'w2_gu', 'w2_down', 'g_final']
TWIN_DIFF_INPUT = 'x'
TWIN_INPUTS = ['x', 'c', 'w_ada', 'b_ada', 'g_ffn1', 'w1_gu', 'w1_down', 'g_mix', 'w_in', 'conv_w', 'w_conv_proj', 'w_attn_proj', 'sinks', 'w_out', 'g_ffn2', 'w2_gu', 'w2_down', 'g_final', 'loss_target', 'm_w_ada', 'm_b_ada', 'm_g_ffn1', 'm_w1_gu', 'm_w1_down', 'm_g_mix', 'm_w_in', 'm_conv_w', 'm_w_conv_proj', 'm_w_attn_proj', 'm_sinks', 'm_w_out', 'm_g_ffn2', 'm_w2_gu', 'm_w2_down', 'm_g_final', 'v_w_ada', 'v_b_ada', 'v_g_ffn1', 'v_w1_gu', 'v_w1_down', 'v_g_mix', 'v_w_in', 'v_conv_w', 'v_w_conv_proj', 'v_w_attn_proj', 'v_sinks', 'v_w_out', 'v_g_ffn2', 'v_w2_gu', 'v_w2_down', 'v_g_final']
TWIN_OUTPUTS = ['loss', 'grad_x', 'grad_w_ada', 'grad_b_ada', 'grad_g_ffn1', 'grad_w1_gu', 'grad_w1_down', 'grad_g_mix', 'grad_w_in', 'grad_conv_w', 'grad_w_conv_proj', 'grad_w_attn_proj', 'grad_sinks', 'grad_w_out', 'grad_g_ffn2', 'grad_w2_gu', 'grad_w2_down', 'grad_g_final', 'delta_w_ada', 'delta_b_ada', 'delta_g_ffn1', 'delta_w1_gu', 'delta_w1_down', 'delta_g_mix', 'delta_w_in', 'delta_conv_w', 'delta_w_conv_proj', 'delta_w_attn_proj', 'delta_sinks', 'delta_w_out', 'delta_g_ffn2', 'delta_w2_gu', 'delta_w2_down', 'delta_g_final', 'new_m_w_ada', 'new_m_b_ada', 'new_m_g_ffn1', 'new_m_w1_gu', 'new_m_w1_down', 'new_m_g_mix', 'new_m_w_in', 'new_m_conv_w', 'new_m_w_conv_proj', 'new_m_w_attn_proj', 'new_m_sinks', 'new_m_w_out', 'new_m_g_ffn2', 'new_m_w2_gu', 'new_m_w2_down', 'new_m_g_final', 'new_v_w_ada', 'new_v_b_ada', 'new_v_g_ffn1', 'new_v_w1_gu', 'new_v_w1_down', 'new_v_g_mix', 'new_v_w_in', 'new_v_conv_w', 'new_v_w_conv_proj', 'new_v_w_attn_proj', 'new_v_sinks', 'new_v_w_out', 'new_v_g_ffn2', 'new_v_w2_gu', 'new_v_w2_down', 'new_v_g_final']
TWIN_LEAF_KINDS = {'loss': 'loss', 'grad_x': 'grad_x', 'grad_w_ada': 'grad_w', 'grad_b_ada': 'grad_w', 'grad_g_ffn1': 'grad_w', 'grad_w1_gu': 'grad_w', 'grad_w1_down': 'grad_w', 'grad_g_mix': 'grad_w', 'grad_w_in': 'grad_w', 'grad_conv_w': 'grad_w', 'grad_w_conv_proj': 'grad_w', 'grad_w_attn_proj': 'grad_w', 'grad_sinks': 'grad_w', 'grad_w_out': 'grad_w', 'grad_g_ffn2': 'grad_w', 'grad_w2_gu': 'grad_w', 'grad_w2_down': 'grad_w', 'grad_g_final': 'grad_w', 'delta_w_ada': 'delta_w', 'delta_b_ada': 'delta_w', 'delta_g_ffn1': 'delta_w', 'delta_w1_gu': 'delta_w', 'delta_w1_down': 'delta_w', 'delta_g_mix': 'delta_w', 'delta_w_in': 'delta_w', 'delta_conv_w': 'delta_w', 'delta_w_conv_proj': 'delta_w', 'delta_w_attn_proj': 'delta_w', 'delta_sinks': 'delta_w', 'delta_w_out': 'delta_w', 'delta_g_ffn2': 'delta_w', 'delta_w2_gu': 'delta_w', 'delta_w2_down': 'delta_w', 'delta_g_final': 'delta_w', 'new_m_w_ada': 'new_m', 'new_m_b_ada': 'new_m', 'new_m_g_ffn1': 'new_m', 'new_m_w1_gu': 'new_m', 'new_m_w1_down': 'new_m', 'new_m_g_mix': 'new_m', 'new_m_w_in': 'new_m', 'new_m_conv_w': 'new_m', 'new_m_w_conv_proj': 'new_m', 'new_m_w_attn_proj': 'new_m', 'new_m_sinks': 'new_m', 'new_m_w_out': 'new_m', 'new_m_g_ffn2': 'new_m', 'new_m_w2_gu': 'new_m', 'new_m_w2_down': 'new_m', 'new_m_g_final': 'new_m', 'new_v_w_ada': 'new_v', 'new_v_b_ada': 'new_v', 'new_v_g_ffn1': 'new_v', 'new_v_w1_gu': 'new_v', 'new_v_w1_down': 'new_v', 'new_v_g_mix': 'new_v', 'new_v_w_in': 'new_v', 'new_v_conv_w': 'new_v', 'new_v_w_conv_proj': 'new_v', 'new_v_w_attn_proj': 'new_v', 'new_v_sinks': 'new_v', 'new_v_w_out': 'new_v', 'new_v_g_ffn2': 'new_v', 'new_v_w2_gu': 'new_v', 'new_v_w2_down': 'new_v', 'new_v_g_final': 'new_v'}


def _forward(args):
    return _fwd_reference(*[args[k] for k in FWD_PARAMS])


def _output_shape():
    def fwd():
        inp = _fwd_setup_inputs(0)
        return _fwd_reference(*[inp[k] for k in FWD_PARAMS])
    out = _jax.eval_shape(fwd)
    return out.shape, out.dtype

N_MICROBATCH = 1
ADAM_LR = 0.001
ADAM_B1 = 0.9
ADAM_B2 = 0.999
ADAM_EPS = 1e-08
ADAM_WD = 0.01
ADAM_STEP = 10
PER_EXAMPLE_BATCH_AXIS = {'x': 0, 'c': 0, 'loss_target': 0}
SHARED_INPUTS = []
_WEIGHT_DTYPES = {'w_ada': _jnp.float32, 'b_ada': _jnp.float32, 'g_ffn1': _jnp.float32, 'w1_gu': _jnp.float32, 'w1_down': _jnp.float32, 'g_mix': _jnp.float32, 'w_in': _jnp.float32, 'conv_w': _jnp.float32, 'w_conv_proj': _jnp.float32, 'w_attn_proj': _jnp.float32, 'sinks': _jnp.float32, 'w_out': _jnp.float32, 'g_ffn2': _jnp.float32, 'w2_gu': _jnp.float32, 'w2_down': _jnp.float32, 'g_final': _jnp.float32}
MOMENT_SCALE = {'w_ada': 6.923506e-02, 'b_ada': 1.265608e-01, 'g_ffn1': 3.848294e-02, 'w1_gu': 1.702337e-02, 'w1_down': 2.785117e-02, 'g_mix': 9.236385e-02, 'w_in': 3.792435e-02, 'conv_w': 5.286036e-02, 'w_conv_proj': 5.361240e-02, 'w_attn_proj': 1.553934e-02, 'sinks': 8.148770e-03, 'w_out': 5.578680e-02, 'g_ffn2': 3.678322e-02, 'w2_gu': 1.624773e-02, 'w2_down': 2.653263e-02, 'g_final': 6.420134e+01}


def _to_microbatches(a, axis):
    t = _jnp.moveaxis(a, axis, 0)
    t = t.reshape((N_MICROBATCH, t.shape[0] // N_MICROBATCH) + t.shape[1:])
    return _jnp.moveaxis(t, 1, axis + 1)


def setup_inputs(seed: int = 0) -> dict:
    inp = _fwd_setup_inputs(seed)
    key = _jax.random.fold_in(_jax.random.key(seed), 7919)
    shape, _ = _output_shape()
    out = dict(inp)
    out["loss_target"] = _jax.random.normal(_jax.random.fold_in(key, 0), shape, _jnp.float32)
    for i, name in enumerate(TWIN_WEIGHTS):
        w = inp[name].astype(_jnp.float32)
        if MOMENT_SCALE is None:
            s = _jnp.sqrt(_jnp.mean(_jnp.square(w)) + 1e-30)
        else:
            s = MOMENT_SCALE[name]
        km, kv = _jax.random.split(_jax.random.fold_in(key, i + 1))
        out[name] = w
        out["m_" + name] = s * _jax.random.normal(km, w.shape, _jnp.float32)
        out["v_" + name] = (s * s) * _jax.random.uniform(kv, w.shape, _jnp.float32, 0.5, 1.5)
    if N_MICROBATCH > 1:
        for name, axis in PER_EXAMPLE_BATCH_AXIS.items():
            out[name] = _to_microbatches(out[name], axis)
    return {'x': out['x'], 'c': out['c'], 'w_ada': out['w_ada'], 'b_ada': out['b_ada'], 'g_ffn1': out['g_ffn1'], 'w1_gu': out['w1_gu'], 'w1_down': out['w1_down'], 'g_mix': out['g_mix'], 'w_in': out['w_in'], 'conv_w': out['conv_w'], 'w_conv_proj': out['w_conv_proj'], 'w_attn_proj': out['w_attn_proj'], 'sinks': out['sinks'], 'w_out': out['w_out'], 'g_ffn2': out['g_ffn2'], 'w2_gu': out['w2_gu'], 'w2_down': out['w2_down'], 'g_final': out['g_final'], 'loss_target': out['loss_target'], 'm_w_ada': out['m_w_ada'], 'm_b_ada': out['m_b_ada'], 'm_g_ffn1': out['m_g_ffn1'], 'm_w1_gu': out['m_w1_gu'], 'm_w1_down': out['m_w1_down'], 'm_g_mix': out['m_g_mix'], 'm_w_in': out['m_w_in'], 'm_conv_w': out['m_conv_w'], 'm_w_conv_proj': out['m_w_conv_proj'], 'm_w_attn_proj': out['m_w_attn_proj'], 'm_sinks': out['m_sinks'], 'm_w_out': out['m_w_out'], 'm_g_ffn2': out['m_g_ffn2'], 'm_w2_gu': out['m_w2_gu'], 'm_w2_down': out['m_w2_down'], 'm_g_final': out['m_g_final'], 'v_w_ada': out['v_w_ada'], 'v_b_ada': out['v_b_ada'], 'v_g_ffn1': out['v_g_ffn1'], 'v_w1_gu': out['v_w1_gu'], 'v_w1_down': out['v_w1_down'], 'v_g_mix': out['v_g_mix'], 'v_w_in': out['v_w_in'], 'v_conv_w': out['v_conv_w'], 'v_w_conv_proj': out['v_w_conv_proj'], 'v_w_attn_proj': out['v_w_attn_proj'], 'v_sinks': out['v_sinks'], 'v_w_out': out['v_w_out'], 'v_g_ffn2': out['v_g_ffn2'], 'v_w2_gu': out['v_w2_gu'], 'v_w2_down': out['v_w2_down'], 'v_g_final': out['v_g_final']}


def _loss(weights, diff, rest, loss_target):
    with _jax.named_scope("forward"):
        args = {**rest, TWIN_DIFF_INPUT: diff, **{k: w.astype(_WEIGHT_DTYPES[k]) for k, w in weights.items()}}
        y = _forward(args)
    with _jax.named_scope("loss_head"):
        err = _jnp.square(y.astype(_jnp.float32) - loss_target)
        return 0.5 * _jnp.sum(_jnp.mean(err, axis=-1)) if err.ndim else 0.5 * err


def _adamw(w, g, m, v):
    m = ADAM_B1 * m + (1.0 - ADAM_B1) * g
    v = ADAM_B2 * v + (1.0 - ADAM_B2) * _jnp.square(g)
    m_hat = m / (1.0 - ADAM_B1 ** ADAM_STEP)
    v_hat = v / (1.0 - ADAM_B2 ** ADAM_STEP)
    delta = -ADAM_LR * (m_hat / (_jnp.sqrt(v_hat) + ADAM_EPS) + ADAM_WD * w)
    return delta, m, v


def reference(x, c, w_ada, b_ada, g_ffn1, w1_gu, w1_down, g_mix, w_in, conv_w, w_conv_proj, w_attn_proj, sinks, w_out, g_ffn2, w2_gu, w2_down, g_final, loss_target, m_w_ada, m_b_ada, m_g_ffn1, m_w1_gu, m_w1_down, m_g_mix, m_w_in, m_conv_w, m_w_conv_proj, m_w_attn_proj, m_sinks, m_w_out, m_g_ffn2, m_w2_gu, m_w2_down, m_g_final, v_w_ada, v_b_ada, v_g_ffn1, v_w1_gu, v_w1_down, v_g_mix, v_w_in, v_conv_w, v_w_conv_proj, v_w_attn_proj, v_sinks, v_w_out, v_g_ffn2, v_w2_gu, v_w2_down, v_g_final):
    given = dict(x=x, c=c, w_ada=w_ada, b_ada=b_ada, g_ffn1=g_ffn1, w1_gu=w1_gu, w1_down=w1_down, g_mix=g_mix, w_in=w_in, conv_w=conv_w, w_conv_proj=w_conv_proj, w_attn_proj=w_attn_proj, sinks=sinks, w_out=w_out, g_ffn2=g_ffn2, w2_gu=w2_gu, w2_down=w2_down, g_final=g_final, loss_target=loss_target, m_w_ada=m_w_ada, m_b_ada=m_b_ada, m_g_ffn1=m_g_ffn1, m_w1_gu=m_w1_gu, m_w1_down=m_w1_down, m_g_mix=m_g_mix, m_w_in=m_w_in, m_conv_w=m_conv_w, m_w_conv_proj=m_w_conv_proj, m_w_attn_proj=m_w_attn_proj, m_sinks=m_sinks, m_w_out=m_w_out, m_g_ffn2=m_g_ffn2, m_w2_gu=m_w2_gu, m_w2_down=m_w2_down, m_g_final=m_g_final, v_w_ada=v_w_ada, v_b_ada=v_b_ada, v_g_ffn1=v_g_ffn1, v_w1_gu=v_w1_gu, v_w1_down=v_w1_down, v_g_mix=v_g_mix, v_w_in=v_w_in, v_conv_w=v_conv_w, v_w_conv_proj=v_w_conv_proj, v_w_attn_proj=v_w_attn_proj, v_sinks=v_sinks, v_w_out=v_w_out, v_g_ffn2=v_g_ffn2, v_w2_gu=v_w2_gu, v_w2_down=v_w2_down, v_g_final=v_g_final)
    weights = {n: given[n] for n in TWIN_WEIGHTS}
    shared = {n: given[n] for n in SHARED_INPUTS}
    per_example = {n: given[n] for n in ['x', 'c']}
    grad_fn = _jax.value_and_grad(_loss, argnums=(0, 1))

    def one_microbatch(ex, loss_target):
        ex = dict(ex)
        diff = ex.pop(TWIN_DIFF_INPUT)
        return grad_fn(weights, diff, {**shared, **ex}, loss_target)

    if N_MICROBATCH == 1:
        loss, (grad_w, grad_x) = one_microbatch(per_example, given["loss_target"])
    else:
        def body(carry, xs):
            loss_sum, grad_sum = carry
            l_k, (gw_k, gx_k) = one_microbatch(xs[0], xs[1])
            with _jax.named_scope("update"):
                return (loss_sum + l_k, _jax.tree.map(_jnp.add, grad_sum, gw_k)), gx_k

        init = (_jnp.zeros((), _jnp.float32), _jax.tree.map(_jnp.zeros_like, weights))
        (loss, grad_w), grad_x = _jax.lax.scan(body, init, (per_example, given["loss_target"]))
    with _jax.named_scope("update"):
        delta_w, new_m, new_v = {}, {}, {}
        for n in TWIN_WEIGHTS:
            delta_w[n], new_m[n], new_v[n] = _adamw(weights[n], grad_w[n], given["m_" + n], given["v_" + n])
    return (loss, grad_x, *[grad_w[n] for n in TWIN_WEIGHTS], *[delta_w[n] for n in TWIN_WEIGHTS],
            *[new_m[n] for n in TWIN_WEIGHTS], *[new_v[n] for n in TWIN_WEIGHTS])
```

```python
import functools

import jax
import jax.numpy as jnp
from jax import lax
from jax.experimental import pallas as pl
from jax.experimental.pallas import tpu as pltpu

D = 1024
F = 2816
NIN = 6656
N_HEADS = 16
N_KV = 4
HEAD_DIM = 64
BLK = 128
N_MOD = 9
N_DEV = 8
EPS = 1e-6
NEG_INF = -1e30
ROPE_THETA = 10000.0
O_BG, O_CG, O_U, O_Q, O_K, O_V, O_ZC, O_ZA = 0, 1024, 2048, 3072, 4096, 4352, 4608, 5632

ADAM_LR = 0.001
ADAM_B1 = 0.9
ADAM_B2 = 0.999
ADAM_EPS = 1e-08
ADAM_WD = 0.01
ADAM_STEP = 10

BF = jnp.bfloat16
F32 = jnp.float32
VMEM_LIMIT = 56 * 1024 * 1024
MESH = pl.DeviceIdType.MESH

NT = (((1,), (1,)), ((), ()))
TN = (((0,), (0,)), ((), ()))


def _cp(sem=None):
    return pltpu.CompilerParams(dimension_semantics=sem, vmem_limit_bytes=VMEM_LIMIT)


def _tile(n, pref):
    if n <= pref:
        return n
    for t in range(pref - pref % 16, 15, -16):
        if n % t == 0:
            return t
    raise ValueError((n, pref))


def _row(i):
    return (i, 0)


def _const2(*_):
    return (0, 0)


def _norm_proj(x, g, sc, sh, wt, *, tm, tn, name):
    T, N = x.shape[0], wt.shape[0]
    tm, tn = _tile(T, tm), _tile(N, tn)

    def body(x_ref, g_ref, sc_ref, sh_ref, w_ref, h_ref, o_ref, hs):
        @pl.when(pl.program_id(1) == 0)
        def _():
            xv = x_ref[...]
            r = lax.rsqrt(jnp.mean(xv * xv, axis=-1, keepdims=True) + EPS)
            hb = ((xv * r) * g_ref[...] * (1.0 + sc_ref[...]) + sh_ref[...]).astype(BF)
            hs[...] = hb
            h_ref[...] = hb
        o_ref[...] = lax.dot_general(hs[...], w_ref[...], NT, preferred_element_type=F32).astype(BF)

    vec = pl.BlockSpec((1, D), _const2)
    return pl.pallas_call(
        body, name=name, grid=(T // tm, N // tn),
        in_specs=[pl.BlockSpec((tm, D), lambda i, j: (i, 0)), vec, vec, vec,
                  pl.BlockSpec((tn, D), lambda i, j: (j, 0))],
        out_specs=[pl.BlockSpec((tm, D), lambda i, j: (i, 0)), pl.BlockSpec((tm, tn), lambda i, j: (i, j))],
        out_shape=[jax.ShapeDtypeStruct((T, D), BF), jax.ShapeDtypeStruct((T, N), BF)],
        scratch_shapes=[pltpu.VMEM((tm, D), BF)],
        compiler_params=_cp(("parallel", "arbitrary")),
    )(x, g, sc, sh, wt)


def _ffn_down_fwd(ab, wd, x, gt, *, tm, name):
    T = x.shape[0]
    tm = _tile(T, tm)

    def body(a_ref, b_ref, wd_ref, x_ref, gt_ref, xo_ref, y_ref):
        a = a_ref[...].astype(F32)
        act = (a * jax.nn.sigmoid(a) * b_ref[...].astype(F32)).astype(BF)
        y = jnp.dot(act, wd_ref[...], preferred_element_type=F32)
        y_ref[...] = y.astype(BF)
        xo_ref[...] = x_ref[...] + (0.5 * gt_ref[...]) * y

    return pl.pallas_call(
        body, name=name, grid=(T // tm,),
        in_specs=[pl.BlockSpec((tm, F), lambda i: (i, 0)), pl.BlockSpec((tm, F), lambda i: (i, 1)),
                  pl.BlockSpec((F, D), _const2), pl.BlockSpec((tm, D), _row), pl.BlockSpec((1, D), _const2)],
        out_specs=[pl.BlockSpec((tm, D), _row), pl.BlockSpec((tm, D), _row)],
        out_shape=[jax.ShapeDtypeStruct((T, D), F32), jax.ShapeDtypeStruct((T, D), BF)],
        compiler_params=_cp(("parallel",)),
    )(ab, ab, wd, x, gt)


def _final_fwd_bwd(x, tgt, g, *, tm, name):
    T = x.shape[0]
    tm = _tile(T, tm)

    def body(x_ref, t_ref, g_ref, dx_ref, ls_ref, dg_ref):
        @pl.when(pl.program_id(0) == 0)
        def _():
            ls_ref[...] = jnp.zeros_like(ls_ref)
            dg_ref[...] = jnp.zeros_like(dg_ref)
        xv = x_ref[...]
        gv = g_ref[...]
        r = lax.rsqrt(jnp.mean(xv * xv, axis=-1, keepdims=True) + EPS)
        xh = xv * r
        e = xh * gv - t_ref[...]
        ls_ref[...] += jnp.sum(e * e, axis=0, keepdims=True)
        dy = e * (1.0 / D)
        dg_ref[...] += jnp.sum(dy * xh, axis=0, keepdims=True)
        dxh = dy * gv
        dx_ref[...] = r * (dxh - xh * jnp.mean(dxh * xh, axis=-1, keepdims=True))

    vec = pl.BlockSpec((1, D), _const2)
    return pl.pallas_call(
        body, name=name, grid=(T // tm,),
        in_specs=[pl.BlockSpec((tm, D), _row), pl.BlockSpec((tm, D), _row), vec],
        out_specs=[pl.BlockSpec((tm, D), _row), vec, vec],
        out_shape=[jax.ShapeDtypeStruct((T, D), F32), jax.ShapeDtypeStruct((1, D), F32),
                   jax.ShapeDtypeStruct((1, D), F32)],
        compiler_params=_cp(("arbitrary",)),
    )(x, tgt, g)


def _ffn_down_bwd(dxo, y, gt, ab, wd, *, tm, tn, name):
    T = dxo.shape[0]
    tm, tn = _tile(T, tm), _tile(F, tn)
    nj = F // tn

    def body(dxo_ref, y_ref, gt_ref, a_ref, b_ref, wd_ref, dy_ref, dab_ref, dgt_ref, dys):
        i, j = pl.program_id(0), pl.program_id(1)

        @pl.when(jnp.logical_and(i == 0, j == 0))
        def _():
            dgt_ref[...] = jnp.zeros_like(dgt_ref)

        @pl.when(j == 0)
        def _():
            dxv = dxo_ref[...]
            dgt_ref[...] += 0.5 * jnp.sum(dxv * y_ref[...].astype(F32), axis=0, keepdims=True)
            dyb = ((0.5 * gt_ref[...]) * dxv).astype(BF)
            dys[...] = dyb
            dy_ref[...] = dyb

        dact = lax.dot_general(dys[...], wd_ref[...], NT, preferred_element_type=F32)
        a = a_ref[...].astype(F32)
        b = b_ref[...].astype(F32)
        s = jax.nn.sigmoid(a)
        silu = a * s
        dab_ref[0] = (dact * b * (s * (1.0 + a * (1.0 - s)))).astype(BF)
        dab_ref[1] = (dact * silu).astype(BF)

    vec = pl.BlockSpec((1, D), _const2)
    return pl.pallas_call(
        body, name=name, grid=(T // tm, nj),
        in_specs=[pl.BlockSpec((tm, D), lambda i, j: (i, 0)), pl.BlockSpec((tm, D), lambda i, j: (i, 0)), vec,
                  pl.BlockSpec((tm, tn), lambda i, j: (i, j)), pl.BlockSpec((tm, tn), lambda i, j: (i, j + nj)),
                  pl.BlockSpec((tn, D), lambda i, j: (j, 0))],
        out_specs=[pl.BlockSpec((tm, D), lambda i, j: (i, 0)), pl.BlockSpec((2, tm, tn), lambda i, j: (0, i, j)), vec],
        out_shape=[jax.ShapeDtypeStruct((T, D), BF), jax.ShapeDtypeStruct((2, T, F), BF),
                   jax.ShapeDtypeStruct((1, D), F32)],
        scratch_shapes=[pltpu.VMEM((tm, D), BF)],
        compiler_params=_cp(("arbitrary", "arbitrary")),
    )(dxo, y, gt, ab, ab, wd)


def _tn_matmul(a, b, *, tn, tk, name):
    S, T, Ns = a.shape
    tn, tk = _tile(Ns, tn), _tile(T, tk)
    nk, njs = T // tk, Ns // tn

    def body(a_ref, b_ref, o_ref, acc):
        k = pl.program_id(1)

        @pl.when(k == 0)
        def _():
            acc[...] = jnp.zeros_like(acc)
        acc[...] += lax.dot_general(a_ref[0], b_ref[...], TN, preferred_element_type=F32)

        @pl.when(k == nk - 1)
        def _():
            o_ref[...] = acc[...].astype(BF)

    return pl.pallas_call(
        body, name=name, grid=(S * njs, nk),
        in_specs=[pl.BlockSpec((1, tk, tn), lambda j, k: (j // njs, k, j % njs)),
                  pl.BlockSpec((tk, D), lambda j, k: (k, 0))],
        out_specs=pl.BlockSpec((tn, D), lambda j, k: (j, 0)),
        out_shape=jax.ShapeDtypeStruct((S * Ns, D), BF),
        scratch_shapes=[pltpu.VMEM((tn, D), F32)],
        compiler_params=_cp(("parallel", "arbitrary")),
    )(a, b)


def _tn_matmul_swiglu(ab, b, *, tn, tk, name):
    T = ab.shape[0]
    tn, tk = _tile(F, tn), _tile(T, tk)
    nk, nj = T // tk, F // tn

    def body(a_ref, g_ref, b_ref, o_ref, acc):
        k = pl.program_id(1)

        @pl.when(k == 0)
        def _():
            acc[...] = jnp.zeros_like(acc)
        a = a_ref[...].astype(F32)
        act = (a * jax.nn.sigmoid(a) * g_ref[...].astype(F32)).astype(BF)
        acc[...] += lax.dot_general(act, b_ref[...], TN, preferred_element_type=F32)

        @pl.when(k == nk - 1)
        def _():
            o_ref[...] = acc[...].astype(BF)

    return pl.pallas_call(
        body, name=name, grid=(nj, nk),
        in_specs=[pl.BlockSpec((tk, tn), lambda j, k: (k, j)), pl.BlockSpec((tk, tn), lambda j, k: (k, j + nj)),
                  pl.BlockSpec((tk, D), lambda j, k: (k, 0))],
        out_specs=pl.BlockSpec((tn, D), lambda j, k: (j, 0)),
        out_shape=jax.ShapeDtypeStruct((F, D), BF),
        scratch_shapes=[pltpu.VMEM((tn, D), F32)],
        compiler_params=_cp(("parallel", "arbitrary")),
    )(ab, ab, b)


def _nn_bwd_norm(da, w, x, g, sc, dxo, *, tm, tk, name):
    S, T, Ks = da.shape
    tm, tk = _tile(T, tm), _tile(Ks, tk)
    nks = Ks // tk
    nk = S * nks

    def body(da_ref, w_ref, x_ref, g_ref, sc_ref, dxo_ref, dx_ref, dsh_ref, dsc_ref, dg_ref, acc):
        i, k = pl.program_id(0), pl.program_id(1)

        @pl.when(jnp.logical_and(i == 0, k == 0))
        def _():
            dsh_ref[...] = jnp.zeros_like(dsh_ref)
            dsc_ref[...] = jnp.zeros_like(dsc_ref)
            dg_ref[...] = jnp.zeros_like(dg_ref)

        @pl.when(k == 0)
        def _():
            acc[...] = jnp.zeros_like(acc)
        acc[...] += jnp.dot(da_ref[0], w_ref[...], preferred_element_type=F32)

        @pl.when(k == nk - 1)
        def _():
            u = acc[...]
            xv = x_ref[...]
            gv = g_ref[...]
            sc1 = 1.0 + sc_ref[...]
            r = lax.rsqrt(jnp.mean(xv * xv, axis=-1, keepdims=True) + EPS)
            xh = xv * r
            dsh_ref[...] += jnp.sum(u, axis=0, keepdims=True)
            dsc_ref[...] += jnp.sum(u * (xh * gv), axis=0, keepdims=True)
            us = u * sc1
            dg_ref[...] += jnp.sum(us * xh, axis=0, keepdims=True)
            dxh = us * gv
            dx_ref[...] = dxo_ref[...] + r * (dxh - xh * jnp.mean(dxh * xh, axis=-1, keepdims=True))

    vec = pl.BlockSpec((1, D), _const2)
    return pl.pallas_call(
        body, name=name, grid=(T // tm, nk),
        in_specs=[pl.BlockSpec((1, tm, tk), lambda i, k: (k // nks, i, k % nks)),
                  pl.BlockSpec((tk, D), lambda i, k: (k, 0)),
                  pl.BlockSpec((tm, D), lambda i, k: (i, 0)), vec, vec,
                  pl.BlockSpec((tm, D), lambda i, k: (i, 0))],
        out_specs=[pl.BlockSpec((tm, D), lambda i, k: (i, 0)), vec, vec, vec],
        out_shape=[jax.ShapeDtypeStruct((T, D), F32)] + [jax.ShapeDtypeStruct((1, D), F32)] * 3,
        scratch_shapes=[pltpu.VMEM((tm, D), F32)],
        compiler_params=_cp(("arbitrary", "arbitrary")),
    )(da, w, x, g, sc, dxo)


def _rope(t, cos, sin_signed, lt32, inverse=False):
    sel = jnp.where(lt32, pltpu.roll(t, 96, 1), pltpu.roll(t, 32, 1))
    return t * cos - sel * sin_signed if inverse else t * cos + sel * sin_signed


def _rope_tables(T):
    inv = 1.0 / (ROPE_THETA ** (jnp.arange(0, HEAD_DIM, 2, dtype=F32) / HEAD_DIM))
    ang = jnp.arange(T, dtype=F32)[:, None] * inv[None, :]
    cos, sin = jnp.cos(ang), jnp.sin(ang)
    cos128 = jnp.tile(cos, (1, 4))
    sin128 = jnp.tile(jnp.concatenate([-sin, sin], axis=1), (1, 2))
    return cos128, sin128


def _attn_specs(nb):
    qspec = pl.BlockSpec((BLK, D), lambda n: (n, O_Q // D))
    kc = pl.BlockSpec((BLK, 256), lambda n: (n, O_K // 256))
    kp = pl.BlockSpec((BLK, 256), lambda n: (jnp.maximum(n - 1, 0), O_K // 256))
    vc = pl.BlockSpec((BLK, 256), lambda n: (n, O_V // 256))
    vp = pl.BlockSpec((BLK, 256), lambda n: (jnp.maximum(n - 1, 0), O_V // 256))
    tc = pl.BlockSpec((BLK, 128), lambda n: (n, 0))
    tp = pl.BlockSpec((BLK, 128), lambda n: (jnp.maximum(n - 1, 0), 0))
    return [qspec, kc, kp, vc, vp, tc, tc, tp, tp, pl.BlockSpec(memory_space=pltpu.SMEM)]


def _attn_common(q_ref, kc_ref, kp_ref, vc_ref, vp_ref, cc_ref, sc_ref, cp_ref, sp_ref):
    n = pl.program_id(0)
    lane = lax.broadcasted_iota(jnp.int32, (BLK, 128), 1)
    lt32 = (lane % HEAD_DIM) < (HEAD_DIM // 2)
    cc, sc, cp, sp = cc_ref[...], sc_ref[...], cp_ref[...], sp_ref[...]
    kr, vr = [], []
    for r in range(2):
        cols = slice(r * 128, (r + 1) * 128)
        kcur = _rope(kc_ref[:, cols].astype(F32), cc, sc, lt32)
        kprev = _rope(kp_ref[:, cols].astype(F32), cp, sp, lt32)
        kr.append(jnp.concatenate([kprev, kcur], axis=0).astype(BF))
        vr.append(jnp.concatenate([vp_ref[:, cols], vc_ref[:, cols]], axis=0))
    qr = [_rope(q_ref[:, p * 128:(p + 1) * 128].astype(F32), cc, sc, lt32) for p in range(8)]
    qi = lax.broadcasted_iota(jnp.int32, (4 * BLK, 2 * BLK), 0) % BLK
    kj = lax.broadcasted_iota(jnp.int32, (4 * BLK, 2 * BLK), 1)
    valid = (kj > qi) & (kj <= qi + BLK) & ((kj >= BLK) | (n > 0))
    halves = [lane < HEAD_DIM, lane >= HEAD_DIM]
    return qr, kr, vr, valid, halves, lt32, (cc, sc, cp, sp)


def _stack_heads(chunks, g, halves):
    half = g % 2
    parts = []
    for hh in range(4):
        h = 4 * g + hh
        t = chunks[h // 2]
        if h % 2 != half:
            t = pltpu.roll(t, HEAD_DIM, 1)
        parts.append(jnp.where(halves[half], t, 0.0))
    return jnp.concatenate(parts, axis=0)


def _softmax_sink(s, valid, sink_ref, g):
    s = jnp.where(valid, s * (HEAD_DIM ** -0.5), NEG_INF)
    sink = jnp.concatenate([jnp.full((BLK, 1), sink_ref[4 * g + hh], F32) for hh in range(4)], axis=0)
    m = jnp.maximum(jnp.max(s, axis=-1, keepdims=True), sink)
    p = jnp.exp(s - m)
    ps = jnp.exp(sink - m)
    inv = 1.0 / (jnp.sum(p, axis=-1, keepdims=True) + ps)
    return p * inv, ps * inv


def _attn_fwd(proj, cos, sin, sinks, *, name):
    T = proj.shape[0]
    nb = T // BLK

    def body(q_ref, kc_ref, kp_ref, vc_ref, vp_ref, cc_ref, sc_ref, cp_ref, sp_ref, sink_ref, o_ref):
        qr, kr, vr, valid, halves, _, _ = _attn_common(q_ref, kc_ref, kp_ref, vc_ref, vp_ref,
                                                        cc_ref, sc_ref, cp_ref, sp_ref)
        outs = [jnp.zeros((BLK, 128), F32) for _ in range(8)]
        for g in range(N_KV):
            r, half = g // 2, g % 2
            qs = _stack_heads(qr, g, halves).astype(BF)
            s = lax.dot_general(qs, kr[r], NT, preferred_element_type=F32)
            p, _ = _softmax_sink(s, valid, sink_ref, g)
            o = jnp.dot(p.astype(BF), vr[r], preferred_element_type=F32)
            for hh in range(4):
                h = 4 * g + hh
                oh = jnp.where(halves[half], o[hh * BLK:(hh + 1) * BLK], 0.0)
                if h % 2 != half:
                    oh = pltpu.roll(oh, HEAD_DIM, 1)
                outs[h // 2] = outs[h // 2] + oh
        o_ref[...] = jnp.concatenate(outs, axis=1).astype(BF)

    return pl.pallas_call(
        body, name=name, grid=(nb,),
        in_specs=_attn_specs(nb),
        out_specs=pl.BlockSpec((BLK, D), _row),
        out_shape=jax.ShapeDtypeStruct((T, D), BF),
        compiler_params=_cp(("parallel",)),
    )(proj, proj, proj, proj, proj, cos, sin, cos, sin, sinks)


def _attn_bwd(proj, cos, sin, sinks, o, do, *, name):
    T = proj.shape[0]
    nb = T // BLK

    def body(q_ref, kc_ref, kp_ref, vc_ref, vp_ref, cc_ref, sc_ref, cp_ref, sp_ref, sink_ref, o_ref, do_ref,
             dq_ref, dkc_ref, dkp_ref, dvc_ref, dvp_ref, dsink_ref):
        @pl.when(pl.program_id(0) == 0)
        def _():
            dsink_ref[...] = jnp.zeros_like(dsink_ref)
        qr, kr, vr, valid, halves, lt32, (cc, sc, cp, sp) = _attn_common(
            q_ref, kc_ref, kp_ref, vc_ref, vp_ref, cc_ref, sc_ref, cp_ref, sp_ref)
        oc = [o_ref[:, p * 128:(p + 1) * 128].astype(F32) for p in range(8)]
        doc = [do_ref[:, p * 128:(p + 1) * 128].astype(F32) for p in range(8)]
        dqs = [jnp.zeros((BLK, 128), F32) for _ in range(8)]
        dkr = [jnp.zeros((2 * BLK, 128), F32) for _ in range(2)]
        dvr = [jnp.zeros((2 * BLK, 128), F32) for _ in range(2)]
        lane1 = lax.broadcasted_iota(jnp.int32, (1, 128), 1)
        dsink = jnp.zeros((1, 128), F32)
        for g in range(N_KV):
            r, half = g // 2, g % 2
            qs = _stack_heads(qr, g, halves).astype(BF)
            dos = _stack_heads(doc, g, halves)
            os_ = _stack_heads(oc, g, halves)
            s = lax.dot_general(qs, kr[r], NT, preferred_element_type=F32)
            p, ps = _softmax_sink(s, valid, sink_ref, g)
            dosb = dos.astype(BF)
            dp = lax.dot_general(dosb, vr[r], NT, preferred_element_type=F32)
            delta = jnp.sum(dos * os_, axis=-1, keepdims=True)
            ds = (p * (dp - delta) * (HEAD_DIM ** -0.5)).astype(BF)
            dsk = -ps * delta
            for hh in range(4):
                val = jnp.sum(dsk[hh * BLK:(hh + 1) * BLK], axis=0, keepdims=True)
                dsink = dsink + jnp.where(lane1 == 4 * g + hh, val, 0.0)
            dvr[r] = dvr[r] + lax.dot_general(p.astype(BF), dosb, TN, preferred_element_type=F32)
            dkr[r] = dkr[r] + lax.dot_general(ds, qs, TN, preferred_element_type=F32)
            dq = jnp.dot(ds, kr[r], preferred_element_type=F32)
            for hh in range(4):
                h = 4 * g + hh
                dqh = jnp.where(halves[half], dq[hh * BLK:(hh + 1) * BLK], 0.0)
                if h % 2 != half:
                    dqh = pltpu.roll(dqh, HEAD_DIM, 1)
                dqs[h // 2] = dqs[h // 2] + dqh
        dsink_ref[...] += dsink
        dq_ref[...] = jnp.concatenate([_rope(t, cc, sc, lt32, inverse=True) for t in dqs], axis=1).astype(BF)
        dkp_ref[...] = jnp.concatenate([_rope(t[:BLK], cp, sp, lt32, inverse=True) for t in dkr], axis=1)
        dkc_ref[...] = jnp.concatenate([_rope(t[BLK:], cc, sc, lt32, inverse=True) for t in dkr], axis=1)
        dvp_ref[...] = jnp.concatenate([t[:BLK] for t in dvr], axis=1)
        dvc_ref[...] = jnp.concatenate([t[BLK:] for t in dvr], axis=1)

    kv = pl.BlockSpec((BLK, 256), _row)
    return pl.pallas_call(
        body, name=name, grid=(nb,),
        in_specs=_attn_specs(nb) + [pl.BlockSpec((BLK, D), _row), pl.BlockSpec((BLK, D), _row)],
        out_specs=[pl.BlockSpec((BLK, D), _row), kv, kv, kv, kv, pl.BlockSpec((1, 128), _const2)],
        out_shape=[jax.ShapeDtypeStruct((T, D), BF)] + [jax.ShapeDtypeStruct((T, 256), F32)] * 4
        + [jax.ShapeDtypeStruct((1, 128), F32)],
        compiler_params=_cp(("arbitrary",)),
    )(proj, proj, proj, proj, proj, cos, sin, cos, sin, sinks, o, do)


def _dkv_combine(dkc, dkp, dvc, dvp, *, name):
    T = dkc.shape[0]
    nb = T // BLK

    def body(dkc_ref, dkp_ref, dvc_ref, dvp_ref, o_ref):
        last = (pl.program_id(0) == nb - 1)
        keep = jnp.where(last, 0.0, 1.0)
        o_ref[:, 0:256] = (dkc_ref[...] + keep * dkp_ref[...]).astype(BF)
        o_ref[:, 256:512] = (dvc_ref[...] + keep * dvp_ref[...]).astype(BF)

    cur = pl.BlockSpec((BLK, 256), _row)
    nxt = pl.BlockSpec((BLK, 256), lambda n: (jnp.minimum(n + 1, nb - 1), 0))
    return pl.pallas_call(
        body, name=name, grid=(nb,),
        in_specs=[cur, nxt, cur, nxt],
        out_specs=pl.BlockSpec((BLK, 512), _row),
        out_shape=jax.ShapeDtypeStruct((T, 512), BF),
        compiler_params=_cp(("parallel",)),
    )(dkc, dkp, dvc, dvp)


HALO = 16


def _conv_shifts(cu, hprev, tm):
    row = lax.broadcasted_iota(jnp.int32, cu.shape, 0)
    h1 = hprev[HALO - 1:HALO, :]
    h2 = hprev[HALO - 2:HALO - 1, :]
    m1 = jnp.where(row == 0, h1, pltpu.roll(cu, 1, 0))
    m2 = jnp.where(row == 0, h2, jnp.where(row == 1, h1, pltpu.roll(cu, 2, 0)))
    return m1, m2


def _mixer_mid_fwd(proj, attn, wcp, wap, wout, convw, x, gt, *, tm, name):
    T = x.shape[0]
    tm = _tile(T, tm)
    hb = tm // HALO

    def body(bg_ref, cg_ref, u_ref, hcg_ref, hu_ref, zc0_ref, zc1_ref, za0_ref, za1_ref, at_ref,
             wcp_ref, wap_ref, wout_ref, cw_ref, x_ref, gt_ref,
             x2_ref, gc_ref, yc_ref, ya_ref, mg_ref, o_ref):
        first = jnp.where(pl.program_id(0) == 0, 0.0, 1.0)
        cu = cg_ref[...].astype(F32) * u_ref[...].astype(F32)
        hprev = first * (hcg_ref[...].astype(F32) * hu_ref[...].astype(F32))
        m1, m2 = _conv_shifts(cu, hprev, tm)
        cv = cw_ref[0:1, :] * m2 + cw_ref[1:2, :] * m1 + cw_ref[2:3, :] * cu
        gc = (bg_ref[...].astype(F32) * cv).astype(BF)
        gc_ref[...] = gc
        yc = jnp.dot(gc, wcp_ref[...], preferred_element_type=F32)
        ya = jnp.dot(at_ref[...], wap_ref[...], preferred_element_type=F32)
        yc_ref[...] = yc.astype(BF)
        ya_ref[...] = ya.astype(BF)
        zc = jnp.concatenate([zc0_ref[...], zc1_ref[...]], axis=1).astype(F32)
        za = jnp.concatenate([za0_ref[...], za1_ref[...]], axis=1).astype(F32)
        mg = (jax.nn.sigmoid(zc) * yc + jax.nn.sigmoid(za) * ya).astype(BF)
        mg_ref[...] = mg
        o = jnp.dot(mg, wout_ref[...], preferred_element_type=F32)
        o_ref[...] = o.astype(BF)
        x2_ref[...] = x_ref[...] + gt_ref[...] * o

    wspec = pl.BlockSpec((D, D), _const2)
    rowspec = pl.BlockSpec((tm, D), _row)
    return pl.pallas_call(
        body, name=name, grid=(T // tm,),
        in_specs=[_col(tm, O_BG), _col(tm, O_CG), _col(tm, O_U), _halo_prev(hb, O_CG), _halo_prev(hb, O_U),
                  _col(tm, O_ZC, 512), _col(tm, O_ZC + 512, 512), _col(tm, O_ZA, 512), _col(tm, O_ZA + 512, 512),
                  rowspec, wspec, wspec, wspec, pl.BlockSpec((8, D), _const2), rowspec, pl.BlockSpec((1, D), _const2)],
        out_specs=[rowspec] * 6,
        out_shape=[jax.ShapeDtypeStruct((T, D), F32)] + [jax.ShapeDtypeStruct((T, D), BF)] * 5,
        compiler_params=_cp(("parallel",)),
    )(proj, proj, proj, proj, proj, proj, proj, proj, proj, attn, wcp, wap, wout, convw, x, gt)


def _col(tm, c, w=D):
    assert c % w == 0
    return pl.BlockSpec((tm, w), lambda i: (i, c // w))


def _halo_prev(hb, c):
    return pl.BlockSpec((HALO, D), lambda i: (jnp.maximum(i * hb - 1, 0), c // D))


def _halo_next(hb, nblk, c=0):
    return pl.BlockSpec((HALO, D), lambda i: (jnp.minimum((i + 1) * hb, nblk - 1), c // D))


def _mixer_mid_bwd(dx2, gt, o, proj, yc, ya, wout, wcp, wap, *, tm, name):
    T = dx2.shape[0]
    tm = _tile(T, tm)

    def body(dx_ref, gt_ref, o_ref, zc0_ref, zc1_ref, za0_ref, za1_ref, yc_ref, ya_ref, wout_ref, wcp_ref, wap_ref,
             dout_ref, dyc_ref, dya_ref, dgc_ref, dat_ref, dz_ref, dgt_ref):
        @pl.when(pl.program_id(0) == 0)
        def _():
            dgt_ref[...] = jnp.zeros_like(dgt_ref)
        dxv = dx_ref[...]
        dgt_ref[...] += jnp.sum(dxv * o_ref[...].astype(F32), axis=0, keepdims=True)
        dout = (gt_ref[...] * dxv).astype(BF)
        dout_ref[...] = dout
        dmg = lax.dot_general(dout, wout_ref[...], NT, preferred_element_type=F32)
        sc = jax.nn.sigmoid(jnp.concatenate([zc0_ref[...], zc1_ref[...]], axis=1).astype(F32))
        sa = jax.nn.sigmoid(jnp.concatenate([za0_ref[...], za1_ref[...]], axis=1).astype(F32))
        dyc = (dmg * sc).astype(BF)
        dya = (dmg * sa).astype(BF)
        dyc_ref[...] = dyc
        dya_ref[...] = dya
        dz_ref[:, 0:D] = (dmg * yc_ref[...].astype(F32) * (sc * (1.0 - sc))).astype(BF)
        dz_ref[:, D:2 * D] = (dmg * ya_ref[...].astype(F32) * (sa * (1.0 - sa))).astype(BF)
        dgc_ref[...] = lax.dot_general(dyc, wcp_ref[...], NT, preferred_element_type=F32).astype(BF)
        dat_ref[...] = lax.dot_general(dya, wap_ref[...], NT, preferred_element_type=F32).astype(BF)

    wspec = pl.BlockSpec((D, D), _const2)
    rowspec = pl.BlockSpec((tm, D), _row)
    vec = pl.BlockSpec((1, D), _const2)
    return pl.pallas_call(
        body, name=name, grid=(T // tm,),
        in_specs=[rowspec, vec, rowspec,
                  _col(tm, O_ZC, 512), _col(tm, O_ZC + 512, 512), _col(tm, O_ZA, 512), _col(tm, O_ZA + 512, 512),
                  rowspec, rowspec, wspec, wspec, wspec],
        out_specs=[rowspec] * 5 + [pl.BlockSpec((tm, 2 * D), _row), vec],
        out_shape=[jax.ShapeDtypeStruct((T, D), BF)] * 5 + [jax.ShapeDtypeStruct((T, 2 * D), BF),
                                                            jax.ShapeDtypeStruct((1, D), F32)],
        compiler_params=_cp(("arbitrary",)),
    )(dx2, gt, o, proj, proj, proj, proj, yc, ya, wout, wcp, wap)


def _conv_bwd(dgc, proj, convw, *, tm, name):
    T = dgc.shape[0]
    tm = _tile(T, tm)
    hb = tm // HALO
    nblk = T // HALO
    nt = T // tm

    def body(dgc_ref, ndgc_ref, bg_ref, nbg_ref, cg_ref, u_ref, hcg_ref, hu_ref, cw_ref, dp_ref, dcw_ref):
        i = pl.program_id(0)

        @pl.when(i == 0)
        def _():
            dcw_ref[...] = jnp.zeros_like(dcw_ref)
        first = jnp.where(i == 0, 0.0, 1.0)
        last = jnp.where(i == nt - 1, 0.0, 1.0)
        cg = cg_ref[...].astype(F32)
        u = u_ref[...].astype(F32)
        bg = bg_ref[...].astype(F32)
        dg = dgc_ref[...].astype(F32)
        cu = cg * u
        hprev = first * (hcg_ref[...].astype(F32) * hu_ref[...].astype(F32))
        m1, m2 = _conv_shifts(cu, hprev, tm)
        w0, w1, w2 = cw_ref[0:1, :], cw_ref[1:2, :], cw_ref[2:3, :]
        cv = w0 * m2 + w1 * m1 + w2 * cu
        dcv = dg * bg
        nxt = last * (ndgc_ref[...].astype(F32) * nbg_ref[...].astype(F32))
        n0, n1 = nxt[0:1, :], nxt[1:2, :]
        row = lax.broadcasted_iota(jnp.int32, dcv.shape, 0)
        p1 = jnp.where(row == tm - 1, n0, pltpu.roll(dcv, tm - 1, 0))
        p2 = jnp.where(row == tm - 1, n1, jnp.where(row == tm - 2, n0, pltpu.roll(dcv, tm - 2, 0)))
        dcu = w2 * dcv + w1 * p1 + w0 * p2
        dp_ref[:, 0:D] = (dg * cv).astype(BF)
        dp_ref[:, D:2 * D] = (dcu * u).astype(BF)
        dp_ref[:, 2 * D:3 * D] = (dcu * cg).astype(BF)
        dcw_ref[0:1, :] += jnp.sum(dcv * m2, axis=0, keepdims=True)
        dcw_ref[1:2, :] += jnp.sum(dcv * m1, axis=0, keepdims=True)
        dcw_ref[2:3, :] += jnp.sum(dcv * cu, axis=0, keepdims=True)

    rowspec = pl.BlockSpec((tm, D), _row)
    cw = pl.BlockSpec((8, D), _const2)
    return pl.pallas_call(
        body, name=name, grid=(nt,),
        in_specs=[rowspec, _halo_next(hb, nblk), _col(tm, O_BG), _halo_next(hb, nblk, O_BG),
                  _col(tm, O_CG), _col(tm, O_U), _halo_prev(hb, O_CG), _halo_prev(hb, O_U), cw],
        out_specs=[pl.BlockSpec((tm, 3 * D), _row), cw],
        out_shape=[jax.ShapeDtypeStruct((T, 3 * D), BF), jax.ShapeDtypeStruct((8, D), F32)],
        compiler_params=_cp(("arbitrary",)),
    )(dgc, dgc, proj, proj, proj, proj, proj, proj, convw)


def _gsum(parts, *, tm, name):
    _, R, C = parts.shape
    tm = _tile(R, tm)

    def body(p_ref, o_ref):
        acc = p_ref[0].astype(F32)
        for s in range(1, N_DEV):
            acc = acc + p_ref[s].astype(F32)
        o_ref[...] = acc

    return pl.pallas_call(
        body, name=name, grid=(R // tm,),
        in_specs=[pl.BlockSpec((N_DEV, tm, C), lambda i: (0, i, 0))],
        out_specs=pl.BlockSpec((tm, C), _row),
        out_shape=jax.ShapeDtypeStruct((R, C), F32),
        compiler_params=_cp(("parallel",)),
    )(parts)


def _adam(w, g, m, v, *, tm, name):
    R, C = w.shape
    tm = _tile(R, tm)
    c1 = 1.0 - ADAM_B1
    c2 = 1.0 - ADAM_B2
    bc1 = 1.0 - ADAM_B1 ** ADAM_STEP
    bc2 = 1.0 - ADAM_B2 ** ADAM_STEP

    def body(w_ref, g_ref, m_ref, v_ref, d_ref, nm_ref, nv_ref):
        gv = g_ref[...]
        nm = ADAM_B1 * m_ref[...] + c1 * gv
        nv = ADAM_B2 * v_ref[...] + c2 * (gv * gv)
        nm_ref[...] = nm
        nv_ref[...] = nv
        d_ref[...] = -ADAM_LR * ((nm / bc1) / (jnp.sqrt(nv / bc2) + ADAM_EPS) + ADAM_WD * w_ref[...])

    spec = pl.BlockSpec((tm, C), _row)
    return pl.pallas_call(
        body, name=name, grid=(R // tm,),
        in_specs=[spec] * 4, out_specs=[spec] * 3,
        out_shape=[jax.ShapeDtypeStruct((R, C), F32)] * 3,
        compiler_params=_cp(("parallel",)),
    )(w, g, m, v)


def _mods_part(c_all, w_ada, b_ada, *, name):
    C = w_ada.shape[1]

    def body(c_ref, w_ref, b_ref, o_ref):
        cv = c_ref[...]
        ca = cv * jax.nn.sigmoid(cv)
        o_ref[...] = jnp.dot(ca, w_ref[...], preferred_element_type=F32,
                             precision=lax.Precision.HIGHEST) + b_ref[...]

    return pl.pallas_call(
        body, name=name,
        out_shape=jax.ShapeDtypeStruct((N_DEV, C), F32),
        compiler_params=_cp(),
    )(c_all, w_ada, b_ada)


def _wada_grad(c_all_t, gm, *, name):
    C = gm.shape[1]

    def body(c_ref, g_ref, o_ref):
        cv = c_ref[...]
        ca = cv * jax.nn.sigmoid(cv)
        acc = ca[:, 0:1] * g_ref[0:1, :]
        for b in range(1, N_DEV):
            acc = acc + ca[:, b:b + 1] * g_ref[b:b + 1, :]
        o_ref[...] = acc

    return pl.pallas_call(
        body, name=name,
        out_shape=jax.ShapeDtypeStruct((D, C), F32),
        compiler_params=_cp(),
    )(c_all_t, gm)


def _peer(x, y, c, d):
    px = lax.rem(x + ((d >> 2) & 1), 2)
    py = lax.rem(y + ((d >> 1) & 1), 2)
    pc = lax.rem(c + (d & 1), 2)
    return (px, py, pc), 4 * px + 2 * py + pc


def _exchange(xs, *, scatter, name):
    n = len(xs)
    nsem = n * (N_DEV - 1)

    def body(*refs):
        ins, outs = refs[:n], refs[n:2 * n]
        send_sems, recv_sems, local_sems = refs[2 * n:]
        x, y, c = lax.axis_index("x"), lax.axis_index("y"), lax.axis_index("c")
        me = 4 * x + 2 * y + c

        def src(t, idx):
            return ins[t].at[idx] if scatter else ins[t]

        local = [pltpu.make_async_copy(src(t, me), outs[t].at[me], local_sems.at[t]) for t in range(n)]
        for cp in local:
            cp.start()
        remote = []
        for t in range(n):
            for d in range(1, N_DEV):
                peer, pidx = _peer(x, y, c, d)
                k = t * (N_DEV - 1) + d - 1
                send = pltpu.make_async_remote_copy(src_ref=src(t, pidx), dst_ref=outs[t].at[me],
                                                    send_sem=send_sems.at[k], recv_sem=recv_sems.at[k],
                                                    device_id=peer, device_id_type=MESH)
                recv = pltpu.make_async_remote_copy(src_ref=src(t, pidx), dst_ref=outs[t].at[pidx],
                                                    send_sem=send_sems.at[k], recv_sem=recv_sems.at[k],
                                                    device_id=peer, device_id_type=MESH)
                send.start()
                remote.append((send, recv))
        for cp in local:
            cp.wait()
        for send, recv in remote:
            send.wait_send()
            recv.wait_recv()

    anyspec = pl.BlockSpec(memory_space=pl.ANY)
    out_shape = [jax.ShapeDtypeStruct(a.shape if scatter else (N_DEV,) + a.shape, a.dtype) for a in xs]
    return pl.pallas_call(
        body, name=name,
        in_specs=[anyspec] * n, out_specs=[anyspec] * n, out_shape=out_shape,
        scratch_shapes=[pltpu.SemaphoreType.DMA((nsem,)), pltpu.SemaphoreType.DMA((nsem,)),
                        pltpu.SemaphoreType.DMA((n,))],
    )(*xs)


def _sum8(parts, *, name):
    _, R, C = parts.shape

    def body(p_ref, o_ref):
        acc = p_ref[0]
        for s in range(1, N_DEV):
            acc = acc + p_ref[s]
        o_ref[...] = acc

    return pl.pallas_call(body, name=name, out_shape=jax.ShapeDtypeStruct((R, C), F32),
                          compiler_params=_cp())(parts)


TM_PROJ, TN_PROJ = 1024, 512
TM_ROW = 512
TM_NN, TK_NN = 1024, 512
TN_TN, TK_TN = 512, 1024


def _tn(a, b, name, tn=TN_TN):
    if a.ndim == 2:
        a = a[None]
    return _tn_matmul(a, b, tn=tn, tk=TK_TN, name=name)


def _local_step(x, tgt, mods, g1, gm, g2, gf, convw8, sinks, w):
    T = x.shape[0]
    sh1, sc1, gt1, sh2, sc2, gt2, sh3, sc3, gt3 = [mods[i:i + 1] for i in range(N_MOD)]
    cos, sin = _rope_tables(T)

    h1, ab1 = _norm_proj(x, g1, sc1, sh1, w["gu1"], tm=TM_PROJ, tn=TN_PROJ, name="ffn1_up")
    x1, y1 = _ffn_down_fwd(ab1, w["d1"], x, gt1, tm=TM_ROW, name="ffn1_down")
    h2, proj = _norm_proj(x1, gm, sc2, sh2, w["win"], tm=TM_PROJ, tn=TN_PROJ, name="mix_in")
    attn = _attn_fwd(proj, cos, sin, sinks, name="attn_fwd")
    x2, gc, yc, ya, mg, o = _mixer_mid_fwd(proj, attn, w["cp"], w["ap"], w["out"], convw8, x1, gt2,
                                           tm=TM_ROW, name="mix_mid")
    h3, ab2 = _norm_proj(x2, g2, sc3, sh3, w["gu2"], tm=TM_PROJ, tn=TN_PROJ, name="ffn2_up")
    x3, y2 = _ffn_down_fwd(ab2, w["d2"], x2, gt3, tm=TM_ROW, name="ffn2_down")
    dx3, lsum, dgf = _final_fwd_bwd(x3, tgt, gf, tm=TM_ROW, name="final")

    dy2, dab2, dgt3 = _ffn_down_bwd(dx3, y2, gt3, ab2, w["d2"], tm=TM_PROJ, tn=256, name="ffn2_down_bwd")
    g_d2 = _tn_matmul_swiglu(ab2, dy2, tn=256, tk=TK_TN, name="ffn2_down_dw")
    dx2, dsh3, dsc3, dg2 = _nn_bwd_norm(dab2, w["gu2"], x2, g2, sc3, dx3, tm=TM_NN, tk=256, name="ffn2_up_bwd")
    g_gu2 = _tn(dab2, h3, "ffn2_up_dw", tn=256)

    dout, dyc, dya, dgc, dat, dz, dgt2 = _mixer_mid_bwd(dx2, gt2, o, proj, yc, ya, w["out"], w["cp"], w["ap"],
                                                        tm=TM_ROW, name="mix_mid_bwd")
    g_out = _tn(mg, dout, "mix_out_dw")
    g_cp = _tn(gc, dyc, "mix_cp_dw")
    g_ap = _tn(attn, dya, "mix_ap_dw")
    dq, dkc, dkp, dvc, dvp, dsink = _attn_bwd(proj, cos, sin, sinks, attn, dat, name="attn_bwd")
    dkv = _dkv_combine(dkc, dkp, dvc, dvp, name="attn_dkv")
    dp1, dcw = _conv_bwd(dgc, proj, convw8, tm=TM_ROW, name="conv_bwd")
    dproj = jnp.concatenate([dp1, dq, dkv, dz], axis=1)
    g_in = _tn(dproj, h2, "mix_in_dw")
    dx1, dsh2, dsc2, dgm = _nn_bwd_norm(dproj[None], w["win"], x1, gm, sc2, dx2, tm=TM_NN, tk=TK_NN,
                                        name="mix_in_bwd")

    dy1, dab1, dgt1 = _ffn_down_bwd(dx1, y1, gt1, ab1, w["d1"], tm=TM_PROJ, tn=256, name="ffn1_down_bwd")
    g_d1 = _tn_matmul_swiglu(ab1, dy1, tn=256, tk=TK_TN, name="ffn1_down_dw")
    dx0, dsh1, dsc1, dg1 = _nn_bwd_norm(dab1, w["gu1"], x, g1, sc1, dx1, tm=TM_NN, tk=256, name="ffn1_up_bwd")
    g_gu1 = _tn(dab1, h1, "ffn1_up_dw", tn=256)

    big = dict(gu1=g_gu1, d1=g_d1, win=g_in, cp=g_cp, ap=g_ap, out=g_out, gu2=g_gu2, d2=g_d2)
    small = dict(mods=jnp.concatenate([dsh1, dsc1, dgt1, dsh2, dsc2, dgt2, dsh3, dsc3, dgt3], axis=0),
                 g1=dg1, gm=dgm, g2=dg2, gf=dgf, convw=dcw[0:3], sinks=dsink[:, 0:N_HEADS])
    return lsum, dx0, big, small


BIG = ("gu1", "d1", "win", "cp", "ap", "out", "gu2", "d2")
TRANSPOSED = ("gu1", "win", "gu2")
SMALL_ROWS = 24
R_MODS, R_G1, R_GM, R_G2, R_GF, R_CONV, R_SINK = 0, 9, 10, 11, 12, 13, 16


def _pad_to(a, rows, cols):
    return jnp.pad(a, ((0, rows - a.shape[0]), (0, cols - a.shape[1])))


def _pack_small(b_ada, g1, gm, g2, gf, conv, sinks):
    rows = [b_ada.reshape(N_MOD, D), g1.reshape(1, D), gm.reshape(1, D), g2.reshape(1, D), gf.reshape(1, D),
            _pad_to(conv.reshape(3, -1), 3, D), _pad_to(sinks.reshape(1, N_HEADS), 1, D)]
    return _pad_to(jnp.concatenate(rows, axis=0), SMALL_ROWS, D)


def _unpack_small(p, conv_cols):
    return dict(b_ada=p[R_MODS:R_MODS + N_MOD].reshape(1, N_MOD * D), g_ffn1=p[R_G1:R_G1 + 1],
                g_mix=p[R_GM:R_GM + 1], g_ffn2=p[R_G2:R_G2 + 1], g_final=p[R_GF],
                conv_w=p[R_CONV:R_CONV + 3, 0:conv_cols][None], sinks=p[R_SINK:R_SINK + 1, 0:N_HEADS])


def kernel(x, c, w_ada, b_ada, g_ffn1, w1_gu, w1_down, g_mix, w_in, conv_w, w_conv_proj, w_attn_proj, sinks, w_out, g_ffn2, w2_gu, w2_down, g_final, loss_target, m_w_ada, m_b_ada, m_g_ffn1, m_w1_gu, m_w1_down, m_g_mix, m_w_in, m_conv_w, m_w_conv_proj, m_w_attn_proj, m_sinks, m_w_out, m_g_ffn2, m_w2_gu, m_w2_down, m_g_final, v_w_ada, v_b_ada, v_g_ffn1, v_w1_gu, v_w1_down, v_g_mix, v_w_in, v_conv_w, v_w_conv_proj, v_w_attn_proj, v_sinks, v_w_out, v_g_ffn2, v_w2_gu, v_w2_down, v_g_final):
    me = 4 * lax.axis_index("x") + 2 * lax.axis_index("y") + lax.axis_index("c")
    ada_cols = w_ada.shape[2]
    conv_cols = conv_w.shape[2]

    native = dict(gu1=w1_gu[0], d1=w1_down[0], win=w_in[0], cp=w_conv_proj[0], ap=w_attn_proj[0], out=w_out[0],
                  gu2=w2_gu[0], d2=w2_down[0])
    shards = [(native[n].T if n in TRANSPOSED else native[n]).astype(BF) for n in BIG]
    gathered = _exchange(shards, scatter=False, name="gather_weights")
    w = {n: g.reshape(-1, D) for n, g in zip(BIG, gathered)}
    c_all, conv_all = _exchange([c, _pad_to(conv_w[0], 8, conv_cols)], scatter=False, name="gather_cond")
    c_all = c_all.reshape(N_DEV, D)
    conv_full = conv_all[:, 0:3, :].transpose(1, 0, 2).reshape(3, D)

    b_cols = lax.dynamic_slice(b_ada, (0, me * ada_cols), (1, ada_cols))
    mods_cols = _mods_part(c_all, w_ada[0], b_cols, name="ada_mods")
    (mods_all,) = _exchange([mods_cols], scatter=False, name="gather_mods")
    mods = lax.dynamic_index_in_dim(mods_all, me, axis=1, keepdims=False).reshape(N_MOD, D)

    lsum, grad_x, big, small = _local_step(x[0], loss_target[0], mods, g_ffn1, g_mix, g_ffn2, g_final[None],
                                           _pad_to(conv_full, 8, D), sinks[0], w)
    loss = lax.psum((0.5 / D) * jnp.sum(lsum), ("x", "y", "c"))

    parts = _exchange([big[n].reshape(N_DEV, -1, D) for n in BIG], scatter=True, name="scatter_grads")
    grads = {}
    for n, p in zip(BIG, parts):
        g = _gsum(p, tm=128, name="gsum_" + n)
        grads[n] = g.T if n in TRANSPOSED else g

    packed = _pack_small(small["mods"], small["g1"], small["gm"], small["g2"], small["gf"], small["convw"],
                         small["sinks"])
    (packed_all,) = _exchange([packed], scatter=False, name="gather_small")
    gsmall = _sum8(packed_all, name="sum_small")
    gm_cols = lax.dynamic_slice(packed_all[:, R_MODS:R_MODS + N_MOD, :].reshape(N_DEV, N_MOD * D),
                                (0, me * ada_cols), (N_DEV, ada_cols))
    grads["ada"] = _wada_grad(c_all.T, gm_cols, name="ada_dw")
    conv_g = lax.dynamic_slice(gsmall[R_CONV:R_CONV + 3], (0, me * conv_cols), (3, conv_cols))
    gsmall_own = gsmall.at[R_CONV:R_CONV + 3].set(_pad_to(conv_g, 3, D))
    g_small = _unpack_small(gsmall_own, conv_cols)

    w_of = dict(ada=w_ada, gu1=w1_gu, d1=w1_down, win=w_in, cp=w_conv_proj, ap=w_attn_proj, out=w_out, gu2=w2_gu,
                d2=w2_down)
    m_of = dict(ada=m_w_ada, gu1=m_w1_gu, d1=m_w1_down, win=m_w_in, cp=m_w_conv_proj, ap=m_w_attn_proj, out=m_w_out,
                gu2=m_w2_gu, d2=m_w2_down)
    v_of = dict(ada=v_w_ada, gu1=v_w1_gu, d1=v_w1_down, win=v_w_in, cp=v_w_conv_proj, ap=v_w_attn_proj, out=v_w_out,
                gu2=v_w2_gu, d2=v_w2_down)
    upd = {n: _adam(w_of[n][0], grads[n], m_of[n][0], v_of[n][0], tm=128, name="adam_" + n) for n in w_of}
    small_upd = _adam(_pack_small(b_ada, g_ffn1, g_mix, g_ffn2, g_final, conv_w, sinks), gsmall_own,
                      _pack_small(m_b_ada, m_g_ffn1, m_g_mix, m_g_ffn2, m_g_final, m_conv_w, m_sinks),
                      _pack_small(v_b_ada, v_g_ffn1, v_g_mix, v_g_ffn2, v_g_final, v_conv_w, v_sinks),
                      tm=SMALL_ROWS, name="adam_small")
    small_out = [g_small] + [_unpack_small(p, conv_cols) for p in small_upd]

    big_name = dict(w_ada="ada", w1_gu="gu1", w1_down="d1", w_in="win", w_conv_proj="cp", w_attn_proj="ap",
                    w_out="out", w2_gu="gu2", w2_down="d2")
    order = ("w_ada", "b_ada", "g_ffn1", "w1_gu", "w1_down", "g_mix", "w_in", "conv_w", "w_conv_proj", "w_attn_proj",
             "sinks", "w_out", "g_ffn2", "w2_gu", "w2_down", "g_final")
    outs = [loss, grad_x[None]]
    for kind in range(4):
        for n in order:
            if n in big_name:
                t = grads[big_name[n]] if kind == 0 else upd[big_name[n]][kind - 1]
                outs.append(t[None])
            else:
                outs.append(small_out[kind][n])
    return tuple(outs)
```

```python
import functools

import jax
import jax.numpy as jnp
from jax import lax
from jax.experimental import pallas as pl
from jax.experimental.pallas import tpu as pltpu

D = 1024
F = 2816
NIN = 6656
N_HEADS = 16
N_KV = 4
HEAD_DIM = 64
BLK = 128
N_MOD = 9
N_DEV = 8
EPS = 1e-6
NEG_INF = -1e30
ROPE_THETA = 10000.0
O_BG, O_CG, O_U, O_Q, O_K, O_V, O_ZC, O_ZA = 0, 1024, 2048, 3072, 4096, 4352, 4608, 5632

ADAM_LR = 0.001
ADAM_B1 = 0.9
ADAM_B2 = 0.999
ADAM_EPS = 1e-08
ADAM_WD = 0.01
ADAM_STEP = 10

BF = jnp.bfloat16
F32 = jnp.float32
VMEM_LIMIT = 56 * 1024 * 1024
MESH = pl.DeviceIdType.MESH

NT = (((1,), (1,)), ((), ()))
TN = (((0,), (0,)), ((), ()))


def _cp(sem=None):
    return pltpu.CompilerParams(dimension_semantics=sem, vmem_limit_bytes=VMEM_LIMIT)


def _tile(n, pref):
    if n <= pref:
        return n
    for t in range(pref - pref % 16, 15, -16):
        if n % t == 0:
            return t
    raise ValueError((n, pref))


def _row(i):
    return (i, 0)


def _const2(*_):
    return (0, 0)


def _norm_proj(x, g, sc, sh, wt, *, tm, tn, name):
    T, N = x.shape[0], wt.shape[0]
    tm, tn = _tile(T, tm), _tile(N, tn)

    def body(x_ref, g_ref, sc_ref, sh_ref, w_ref, h_ref, o_ref, hs):
        @pl.when(pl.program_id(1) == 0)
        def _():
            xv = x_ref[...]
            r = lax.rsqrt(jnp.mean(xv * xv, axis=-1, keepdims=True) + EPS)
            hb = ((xv * r) * g_ref[...] * (1.0 + sc_ref[...]) + sh_ref[...]).astype(BF)
            hs[...] = hb
            h_ref[...] = hb
        o_ref[...] = lax.dot_general(hs[...], w_ref[...], NT, preferred_element_type=F32).astype(BF)

    vec = pl.BlockSpec((1, D), _const2)
    return pl.pallas_call(
        body, name=name, grid=(T // tm, N // tn),
        in_specs=[pl.BlockSpec((tm, D), lambda i, j: (i, 0)), vec, vec, vec,
                  pl.BlockSpec((tn, D), lambda i, j: (j, 0))],
        out_specs=[pl.BlockSpec((tm, D), lambda i, j: (i, 0)), pl.BlockSpec((tm, tn), lambda i, j: (i, j))],
        out_shape=[jax.ShapeDtypeStruct((T, D), BF), jax.ShapeDtypeStruct((T, N), BF)],
        scratch_shapes=[pltpu.VMEM((tm, D), BF)],
        compiler_params=_cp(("parallel", "arbitrary")),
    )(x, g, sc, sh, wt)


def _ffn_down_fwd(ab, wd, x, gt, *, tm, name):
    T = x.shape[0]
    tm = _tile(T, tm)

    def body(a_ref, b_ref, wd_ref, x_ref, gt_ref, xo_ref, y_ref):
        a = a_ref[...].astype(F32)
        act = (a * jax.nn.sigmoid(a) * b_ref[...].astype(F32)).astype(BF)
        y = jnp.dot(act, wd_ref[...], preferred_element_type=F32)
        y_ref[...] = y.astype(BF)
        xo_ref[...] = x_ref[...] + (0.5 * gt_ref[...]) * y

    return pl.pallas_call(
        body, name=name, grid=(T // tm,),
        in_specs=[pl.BlockSpec((tm, F), lambda i: (i, 0)), pl.BlockSpec((tm, F), lambda i: (i, 1)),
                  pl.BlockSpec((F, D), _const2), pl.BlockSpec((tm, D), _row), pl.BlockSpec((1, D), _const2)],
        out_specs=[pl.BlockSpec((tm, D), _row), pl.BlockSpec((tm, D), _row)],
        out_shape=[jax.ShapeDtypeStruct((T, D), F32), jax.ShapeDtypeStruct((T, D), BF)],
        compiler_params=_cp(("parallel",)),
    )(ab, ab, wd, x, gt)


def _final_fwd_bwd(x, tgt, g, *, tm, name):
    T = x.shape[0]
    tm = _tile(T, tm)

    def body(x_ref, t_ref, g_ref, dx_ref, ls_ref, dg_ref):
        @pl.when(pl.program_id(0) == 0)
        def _():
            ls_ref[...] = jnp.zeros_like(ls_ref)
            dg_ref[...] = jnp.zeros_like(dg_ref)
        xv = x_ref[...]
        gv = g_ref[...]
        r = lax.rsqrt(jnp.mean(xv * xv, axis=-1, keepdims=True) + EPS)
        xh = xv * r
        e = xh * gv - t_ref[...]
        ls_ref[...] += jnp.sum(e * e, axis=0, keepdims=True)
        dy = e * (1.0 / D)
        dg_ref[...] += jnp.sum(dy * xh, axis=0, keepdims=True)
        dxh = dy * gv
        dx_ref[...] = r * (dxh - xh * jnp.mean(dxh * xh, axis=-1, keepdims=True))

    vec = pl.BlockSpec((1, D), _const2)
    return pl.pallas_call(
        body, name=name, grid=(T // tm,),
        in_specs=[pl.BlockSpec((tm, D), _row), pl.BlockSpec((tm, D), _row), vec],
        out_specs=[pl.BlockSpec((tm, D), _row), vec, vec],
        out_shape=[jax.ShapeDtypeStruct((T, D), F32), jax.ShapeDtypeStruct((1, D), F32),
                   jax.ShapeDtypeStruct((1, D), F32)],
        compiler_params=_cp(("arbitrary",)),
    )(x, tgt, g)


def _ffn_down_bwd(dxo, y, gt, ab, wd, *, tm, tn, name):
    T = dxo.shape[0]
    tm, tn = _tile(T, tm), _tile(F, tn)
    nj = F // tn

    def body(dxo_ref, y_ref, gt_ref, a_ref, b_ref, wd_ref, dy_ref, dab_ref, dgt_ref, dys):
        i, j = pl.program_id(0), pl.program_id(1)

        @pl.when(jnp.logical_and(i == 0, j == 0))
        def _():
            dgt_ref[...] = jnp.zeros_like(dgt_ref)

        @pl.when(j == 0)
        def _():
            dxv = dxo_ref[...]
            dgt_ref[...] += 0.5 * jnp.sum(dxv * y_ref[...].astype(F32), axis=0, keepdims=True)
            dyb = ((0.5 * gt_ref[...]) * dxv).astype(BF)
            dys[...] = dyb
            dy_ref[...] = dyb

        dact = lax.dot_general(dys[...], wd_ref[...], NT, preferred_element_type=F32)
        a = a_ref[...].astype(F32)
        b = b_ref[...].astype(F32)
        s = jax.nn.sigmoid(a)
        silu = a * s
        dab_ref[0] = (dact * b * (s * (1.0 + a * (1.0 - s)))).astype(BF)
        dab_ref[1] = (dact * silu).astype(BF)

    vec = pl.BlockSpec((1, D), _const2)
    return pl.pallas_call(
        body, name=name, grid=(T // tm, nj),
        in_specs=[pl.BlockSpec((tm, D), lambda i, j: (i, 0)), pl.BlockSpec((tm, D), lambda i, j: (i, 0)), vec,
                  pl.BlockSpec((tm, tn), lambda i, j: (i, j)), pl.BlockSpec((tm, tn), lambda i, j: (i, j + nj)),
                  pl.BlockSpec((tn, D), lambda i, j: (j, 0))],
        out_specs=[pl.BlockSpec((tm, D), lambda i, j: (i, 0)), pl.BlockSpec((2, tm, tn), lambda i, j: (0, i, j)), vec],
        out_shape=[jax.ShapeDtypeStruct((T, D), BF), jax.ShapeDtypeStruct((2, T, F), BF),
                   jax.ShapeDtypeStruct((1, D), F32)],
        scratch_shapes=[pltpu.VMEM((tm, D), BF)],
        compiler_params=_cp(("arbitrary", "arbitrary")),
    )(dxo, y, gt, ab, ab, wd)


def _tn_matmul(a, b, *, tn, tk, name):
    S, T, Ns = a.shape
    tn, tk = _tile(Ns, tn), _tile(T, tk)
    nk, njs = T // tk, Ns // tn

    def body(a_ref, b_ref, o_ref, acc):
        k = pl.program_id(1)

        @pl.when(k == 0)
        def _():
            acc[...] = jnp.zeros_like(acc)
        acc[...] += lax.dot_general(a_ref[0], b_ref[...], TN, preferred_element_type=F32)

        @pl.when(k == nk - 1)
        def _():
            o_ref[...] = acc[...].astype(BF)

    return pl.pallas_call(
        body, name=name, grid=(S * njs, nk),
        in_specs=[pl.BlockSpec((1, tk, tn), lambda j, k: (j // njs, k, j % njs)),
                  pl.BlockSpec((tk, D), lambda j, k: (k, 0))],
        out_specs=pl.BlockSpec((tn, D), lambda j, k: (j, 0)),
        out_shape=jax.ShapeDtypeStruct((S * Ns, D), BF),
        scratch_shapes=[pltpu.VMEM((tn, D), F32)],
        compiler_params=_cp(("parallel", "arbitrary")),
    )(a, b)


def _tn_matmul_swiglu(ab, b, *, tn, tk, name):
    T = ab.shape[0]
    tn, tk = _tile(F, tn), _tile(T, tk)
    nk, nj = T // tk, F // tn

    def body(a_ref, g_ref, b_ref, o_ref, acc):
        k = pl.program_id(1)

        @pl.when(k == 0)
        def _():
            acc[...] = jnp.zeros_like(acc)
        a = a_ref[...].astype(F32)
        act = (a * jax.nn.sigmoid(a) * g_ref[...].astype(F32)).astype(BF)
        acc[...] += lax.dot_general(act, b_ref[...], TN, preferred_element_type=F32)

        @pl.when(k == nk - 1)
        def _():
            o_ref[...] = acc[...].astype(BF)

    return pl.pallas_call(
        body, name=name, grid=(nj, nk),
        in_specs=[pl.BlockSpec((tk, tn), lambda j, k: (k, j)), pl.BlockSpec((tk, tn), lambda j, k: (k, j + nj)),
                  pl.BlockSpec((tk, D), lambda j, k: (k, 0))],
        out_specs=pl.BlockSpec((tn, D), lambda j, k: (j, 0)),
        out_shape=jax.ShapeDtypeStruct((F, D), BF),
        scratch_shapes=[pltpu.VMEM((tn, D), F32)],
        compiler_params=_cp(("parallel", "arbitrary")),
    )(ab, ab, b)


def _nn_bwd_norm(da, w, x, g, sc, dxo, *, tm, tk, name):
    S, T, Ks = da.shape
    tm, tk = _tile(T, tm), _tile(Ks, tk)
    nks = Ks // tk
    nk = S * nks

    def body(da_ref, w_ref, x_ref, g_ref, sc_ref, dxo_ref, dx_ref, dsh_ref, dsc_ref, dg_ref, acc):
        i, k = pl.program_id(0), pl.program_id(1)

        @pl.when(jnp.logical_and(i == 0, k == 0))
        def _():
            dsh_ref[...] = jnp.zeros_like(dsh_ref)
            dsc_ref[...] = jnp.zeros_like(dsc_ref)
            dg_ref[...] = jnp.zeros_like(dg_ref)

        @pl.when(k == 0)
        def _():
            acc[...] = jnp.zeros_like(acc)
        acc[...] += jnp.dot(da_ref[0], w_ref[...], preferred_element_type=F32)

        @pl.when(k == nk - 1)
        def _():
            u = acc[...]
            xv = x_ref[...]
            gv = g_ref[...]
            sc1 = 1.0 + sc_ref[...]
            r = lax.rsqrt(jnp.mean(xv * xv, axis=-1, keepdims=True) + EPS)
            xh = xv * r
            dsh_ref[...] += jnp.sum(u, axis=0, keepdims=True)
            dsc_ref[...] += jnp.sum(u * (xh * gv), axis=0, keepdims=True)
            us = u * sc1
            dg_ref[...] += jnp.sum(us * xh, axis=0, keepdims=True)
            dxh = us * gv
            dx_ref[...] = dxo_ref[...] + r * (dxh - xh * jnp.mean(dxh * xh, axis=-1, keepdims=True))

    vec = pl.BlockSpec((1, D), _const2)
    return pl.pallas_call(
        body, name=name, grid=(T // tm, nk),
        in_specs=[pl.BlockSpec((1, tm, tk), lambda i, k: (k // nks, i, k % nks)),
                  pl.BlockSpec((tk, D), lambda i, k: (k, 0)),
                  pl.BlockSpec((tm, D), lambda i, k: (i, 0)), vec, vec,
                  pl.BlockSpec((tm, D), lambda i, k: (i, 0))],
        out_specs=[pl.BlockSpec((tm, D), lambda i, k: (i, 0)), vec, vec, vec],
        out_shape=[jax.ShapeDtypeStruct((T, D), F32)] + [jax.ShapeDtypeStruct((1, D), F32)] * 3,
        scratch_shapes=[pltpu.VMEM((tm, D), F32)],
        compiler_params=_cp(("arbitrary", "arbitrary")),
    )(da, w, x, g, sc, dxo)


def _rope(t, cos, sin_signed, lt32, inverse=False):
    sel = jnp.where(lt32, pltpu.roll(t, 96, 1), pltpu.roll(t, 32, 1))
    return t * cos - sel * sin_signed if inverse else t * cos + sel * sin_signed


def _rope_tables(T):
    inv = 1.0 / (ROPE_THETA ** (jnp.arange(0, HEAD_DIM, 2, dtype=F32) / HEAD_DIM))
    ang = jnp.arange(T, dtype=F32)[:, None] * inv[None, :]
    cos, sin = jnp.cos(ang), jnp.sin(ang)
    cos128 = jnp.tile(cos, (1, 4))
    sin128 = jnp.tile(jnp.concatenate([-sin, sin], axis=1), (1, 2))
    return cos128, sin128


def _attn_specs(nb):
    qspec = pl.BlockSpec((BLK, D), lambda n: (n, O_Q // D))
    kc = pl.BlockSpec((BLK, 256), lambda n: (n, O_K // 256))
    kp = pl.BlockSpec((BLK, 256), lambda n: (jnp.maximum(n - 1, 0), O_K // 256))
    vc = pl.BlockSpec((BLK, 256), lambda n: (n, O_V // 256))
    vp = pl.BlockSpec((BLK, 256), lambda n: (jnp.maximum(n - 1, 0), O_V // 256))
    tc = pl.BlockSpec((BLK, 128), lambda n: (n, 0))
    tp = pl.BlockSpec((BLK, 128), lambda n: (jnp.maximum(n - 1, 0), 0))
    return [qspec, kc, kp, vc, vp, tc, tc, tp, tp, pl.BlockSpec(memory_space=pltpu.SMEM)]


def _attn_common(q_ref, kc_ref, kp_ref, vc_ref, vp_ref, cc_ref, sc_ref, cp_ref, sp_ref):
    n = pl.program_id(0)
    lane = lax.broadcasted_iota(jnp.int32, (BLK, 128), 1)
    lt32 = (lane % HEAD_DIM) < (HEAD_DIM // 2)
    cc, sc, cp, sp = cc_ref[...], sc_ref[...], cp_ref[...], sp_ref[...]
    kr, vr = [], []
    for r in range(2):
        cols = slice(r * 128, (r + 1) * 128)
        kcur = _rope(kc_ref[:, cols].astype(F32), cc, sc, lt32)
        kprev = _rope(kp_ref[:, cols].astype(F32), cp, sp, lt32)
        kr.append(jnp.concatenate([kprev, kcur], axis=0).astype(BF))
        vr.append(jnp.concatenate([vp_ref[:, cols], vc_ref[:, cols]], axis=0))
    qr = [_rope(q_ref[:, p * 128:(p + 1) * 128].astype(F32), cc, sc, lt32) for p in range(8)]
    qi = lax.broadcasted_iota(jnp.int32, (4 * BLK, 2 * BLK), 0) % BLK
    kj = lax.broadcasted_iota(jnp.int32, (4 * BLK, 2 * BLK), 1)
    valid = (kj > qi) & (kj <= qi + BLK) & ((kj >= BLK) | (n > 0))
    halves = [lane < HEAD_DIM, lane >= HEAD_DIM]
    return qr, kr, vr, valid, halves, lt32, (cc, sc, cp, sp)


def _stack_heads(chunks, g, halves):
    half = g % 2
    parts = []
    for hh in range(4):
        h = 4 * g + hh
        t = chunks[h // 2]
        if h % 2 != half:
            t = pltpu.roll(t, HEAD_DIM, 1)
        parts.append(jnp.where(halves[half], t, 0.0))
    return jnp.concatenate(parts, axis=0)


def _softmax_sink(s, valid, sink_ref, g):
    s = jnp.where(valid, s * (HEAD_DIM ** -0.5), NEG_INF)
    sink = jnp.concatenate([jnp.full((BLK, 1), sink_ref[4 * g + hh], F32) for hh in range(4)], axis=0)
    m = jnp.maximum(jnp.max(s, axis=-1, keepdims=True), sink)
    p = jnp.exp(s - m)
    ps = jnp.exp(sink - m)
    inv = 1.0 / (jnp.sum(p, axis=-1, keepdims=True) + ps)
    return p * inv, ps * inv


def _attn_fwd(proj, cos, sin, sinks, *, name):
    T = proj.shape[0]
    nb = T // BLK

    def body(q_ref, kc_ref, kp_ref, vc_ref, vp_ref, cc_ref, sc_ref, cp_ref, sp_ref, sink_ref, o_ref):
        qr, kr, vr, valid, halves, _, _ = _attn_common(q_ref, kc_ref, kp_ref, vc_ref, vp_ref,
                                                        cc_ref, sc_ref, cp_ref, sp_ref)
        outs = [jnp.zeros((BLK, 128), F32) for _ in range(8)]
        for g in range(N_KV):
            r, half = g // 2, g % 2
            qs = _stack_heads(qr, g, halves).astype(BF)
            s = lax.dot_general(qs, kr[r], NT, preferred_element_type=F32)
            p, _ = _softmax_sink(s, valid, sink_ref, g)
            o = jnp.dot(p.astype(BF), vr[r], preferred_element_type=F32)
            for hh in range(4):
                h = 4 * g + hh
                oh = jnp.where(halves[half], o[hh * BLK:(hh + 1) * BLK], 0.0)
                if h % 2 != half:
                    oh = pltpu.roll(oh, HEAD_DIM, 1)
                outs[h // 2] = outs[h // 2] + oh
        o_ref[...] = jnp.concatenate(outs, axis=1).astype(BF)

    return pl.pallas_call(
        body, name=name, grid=(nb,),
        in_specs=_attn_specs(nb),
        out_specs=pl.BlockSpec((BLK, D), _row),
        out_shape=jax.ShapeDtypeStruct((T, D), BF),
        compiler_params=_cp(("parallel",)),
    )(proj, proj, proj, proj, proj, cos, sin, cos, sin, sinks)


def _attn_bwd(proj, cos, sin, sinks, o, do, *, name):
    T = proj.shape[0]
    nb = T // BLK

    def body(q_ref, kc_ref, kp_ref, vc_ref, vp_ref, cc_ref, sc_ref, cp_ref, sp_ref, sink_ref, o_ref, do_ref,
             dq_ref, dkc_ref, dkp_ref, dvc_ref, dvp_ref, dsink_ref):
        @pl.when(pl.program_id(0) == 0)
        def _():
            dsink_ref[...] = jnp.zeros_like(dsink_ref)
        qr, kr, vr, valid, halves, lt32, (cc, sc, cp, sp) = _attn_common(
            q_ref, kc_ref, kp_ref, vc_ref, vp_ref, cc_ref, sc_ref, cp_ref, sp_ref)
        oc = [o_ref[:, p * 128:(p + 1) * 128].astype(F32) for p in range(8)]
        doc = [do_ref[:, p * 128:(p + 1) * 128].astype(F32) for p in range(8)]
        dqs = [jnp.zeros((BLK, 128), F32) for _ in range(8)]
        dkr = [jnp.zeros((2 * BLK, 128), F32) for _ in range(2)]
        dvr = [jnp.zeros((2 * BLK, 128), F32) for _ in range(2)]
        lane1 = lax.broadcasted_iota(jnp.int32, (1, 128), 1)
        dsink = jnp.zeros((1, 128), F32)
        for g in range(N_KV):
            r, half = g // 2, g % 2
            qs = _stack_heads(qr, g, halves).astype(BF)
            dos = _stack_heads(doc, g, halves)
            os_ = _stack_heads(oc, g, halves)
            s = lax.dot_general(qs, kr[r], NT, preferred_element_type=F32)
            p, ps = _softmax_sink(s, valid, sink_ref, g)
            dosb = dos.astype(BF)
            dp = lax.dot_general(dosb, vr[r], NT, preferred_element_type=F32)
            delta = jnp.sum(dos * os_, axis=-1, keepdims=True)
            ds = (p * (dp - delta) * (HEAD_DIM ** -0.5)).astype(BF)
            dsk = -ps * delta
            for hh in range(4):
                val = jnp.sum(dsk[hh * BLK:(hh + 1) * BLK], axis=0, keepdims=True)
                dsink = dsink + jnp.where(lane1 == 4 * g + hh, val, 0.0)
            dvr[r] = dvr[r] + lax.dot_general(p.astype(BF), dosb, TN, preferred_element_type=F32)
            dkr[r] = dkr[r] + lax.dot_general(ds, qs, TN, preferred_element_type=F32)
            dq = jnp.dot(ds, kr[r], preferred_element_type=F32)
            for hh in range(4):
                h = 4 * g + hh
                dqh = jnp.where(halves[half], dq[hh * BLK:(hh + 1) * BLK], 0.0)
                if h % 2 != half:
                    dqh = pltpu.roll(dqh, HEAD_DIM, 1)
                dqs[h // 2] = dqs[h // 2] + dqh
        dsink_ref[...] += dsink
        dq_ref[...] = jnp.concatenate([_rope(t, cc, sc, lt32, inverse=True) for t in dqs], axis=1).astype(BF)
        dkp_ref[...] = jnp.concatenate([_rope(t[:BLK], cp, sp, lt32, inverse=True) for t in dkr], axis=1)
        dkc_ref[...] = jnp.concatenate([_rope(t[BLK:], cc, sc, lt32, inverse=True) for t in dkr], axis=1)
        dvp_ref[...] = jnp.concatenate([t[:BLK] for t in dvr], axis=1)
        dvc_ref[...] = jnp.concatenate([t[BLK:] for t in dvr], axis=1)

    kv = pl.BlockSpec((BLK, 256), _row)
    return pl.pallas_call(
        body, name=name, grid=(nb,),
        in_specs=_attn_specs(nb) + [pl.BlockSpec((BLK, D), _row), pl.BlockSpec((BLK, D), _row)],
        out_specs=[pl.BlockSpec((BLK, D), _row), kv, kv, kv, kv, pl.BlockSpec((1, 128), _const2)],
        out_shape=[jax.ShapeDtypeStruct((T, D), BF)] + [jax.ShapeDtypeStruct((T, 256), F32)] * 4
        + [jax.ShapeDtypeStruct((1, 128), F32)],
        compiler_params=_cp(("arbitrary",)),
    )(proj, proj, proj, proj, proj, cos, sin, cos, sin, sinks, o, do)


def _dkv_combine(dkc, dkp, dvc, dvp, *, name):
    T = dkc.shape[0]
    nb = T // BLK

    def body(dkc_ref, dkp_ref, dvc_ref, dvp_ref, o_ref):
        last = (pl.program_id(0) == nb - 1)
        keep = jnp.where(last, 0.0, 1.0)
        o_ref[:, 0:256] = (dkc_ref[...] + keep * dkp_ref[...]).astype(BF)
        o_ref[:, 256:512] = (dvc_ref[...] + keep * dvp_ref[...]).astype(BF)

    cur = pl.BlockSpec((BLK, 256), _row)
    nxt = pl.BlockSpec((BLK, 256), lambda n: (jnp.minimum(n + 1, nb - 1), 0))
    return pl.pallas_call(
        body, name=name, grid=(nb,),
        in_specs=[cur, nxt, cur, nxt],
        out_specs=pl.BlockSpec((BLK, 512), _row),
        out_shape=jax.ShapeDtypeStruct((T, 512), BF),
        compiler_params=_cp(("parallel",)),
    )(dkc, dkp, dvc, dvp)


HALO = 16


def _conv_shifts(cu, hprev, tm):
    row = lax.broadcasted_iota(jnp.int32, cu.shape, 0)
    h1 = hprev[HALO - 1:HALO, :]
    h2 = hprev[HALO - 2:HALO - 1, :]
    m1 = jnp.where(row == 0, h1, pltpu.roll(cu, 1, 0))
    m2 = jnp.where(row == 0, h2, jnp.where(row == 1, h1, pltpu.roll(cu, 2, 0)))
    return m1, m2


def _mixer_mid_fwd(proj, attn, wcp, wap, wout, convw, x, gt, *, tm, name):
    T = x.shape[0]
    tm = _tile(T, tm)
    hb = tm // HALO

    def body(bg_ref, cg_ref, u_ref, hcg_ref, hu_ref, zc0_ref, zc1_ref, za0_ref, za1_ref, at_ref,
             wcp_ref, wap_ref, wout_ref, cw_ref, x_ref, gt_ref,
             x2_ref, gc_ref, yc_ref, ya_ref, mg_ref, o_ref):
        first = jnp.where(pl.program_id(0) == 0, 0.0, 1.0)
        cu = cg_ref[...].astype(F32) * u_ref[...].astype(F32)
        hprev = first * (hcg_ref[...].astype(F32) * hu_ref[...].astype(F32))
        m1, m2 = _conv_shifts(cu, hprev, tm)
        cv = cw_ref[0:1, :] * m2 + cw_ref[1:2, :] * m1 + cw_ref[2:3, :] * cu
        gc = (bg_ref[...].astype(F32) * cv).astype(BF)
        gc_ref[...] = gc
        yc = jnp.dot(gc, wcp_ref[...], preferred_element_type=F32)
        ya = jnp.dot(at_ref[...], wap_ref[...], preferred_element_type=F32)
        yc_ref[...] = yc.astype(BF)
        ya_ref[...] = ya.astype(BF)
        zc = jnp.concatenate([zc0_ref[...], zc1_ref[...]], axis=1).astype(F32)
        za = jnp.concatenate([za0_ref[...], za1_ref[...]], axis=1).astype(F32)
        mg = (jax.nn.sigmoid(zc) * yc + jax.nn.sigmoid(za) * ya).astype(BF)
        mg_ref[...] = mg
        o = jnp.dot(mg, wout_ref[...], preferred_element_type=F32)
        o_ref[...] = o.astype(BF)
        x2_ref[...] = x_ref[...] + gt_ref[...] * o

    wspec = pl.BlockSpec((D, D), _const2)
    rowspec = pl.BlockSpec((tm, D), _row)
    return pl.pallas_call(
        body, name=name, grid=(T // tm,),
        in_specs=[_col(tm, O_BG), _col(tm, O_CG), _col(tm, O_U), _halo_prev(hb, O_CG), _halo_prev(hb, O_U),
                  _col(tm, O_ZC, 512), _col(tm, O_ZC + 512, 512), _col(tm, O_ZA, 512), _col(tm, O_ZA + 512, 512),
                  rowspec, wspec, wspec, wspec, pl.BlockSpec((8, D), _const2), rowspec, pl.BlockSpec((1, D), _const2)],
        out_specs=[rowspec] * 6,
        out_shape=[jax.ShapeDtypeStruct((T, D), F32)] + [jax.ShapeDtypeStruct((T, D), BF)] * 5,
        compiler_params=_cp(("parallel",)),
    )(proj, proj, proj, proj, proj, proj, proj, proj, proj, attn, wcp, wap, wout, convw, x, gt)


def _col(tm, c, w=D):
    assert c % w == 0
    return pl.BlockSpec((tm, w), lambda i: (i, c // w))


def _halo_prev(hb, c):
    return pl.BlockSpec((HALO, D), lambda i: (jnp.maximum(i * hb - 1, 0), c // D))


def _halo_next(hb, nblk, c=0):
    return pl.BlockSpec((HALO, D), lambda i: (jnp.minimum((i + 1) * hb, nblk - 1), c // D))


def _mixer_mid_bwd(dx2, gt, o, proj, yc, ya, wout, wcp, wap, *, tm, name):
    T = dx2.shape[0]
    tm = _tile(T, tm)

    def body(dx_ref, gt_ref, o_ref, zc0_ref, zc1_ref, za0_ref, za1_ref, yc_ref, ya_ref, wout_ref, wcp_ref, wap_ref,
             dout_ref, dyc_ref, dya_ref, dgc_ref, dat_ref, dz_ref, dgt_ref):
        @pl.when(pl.program_id(0) == 0)
        def _():
            dgt_ref[...] = jnp.zeros_like(dgt_ref)
        dxv = dx_ref[...]
        dgt_ref[...] += jnp.sum(dxv * o_ref[...].astype(F32), axis=0, keepdims=True)
        dout = (gt_ref[...] * dxv).astype(BF)
        dout_ref[...] = dout
        dmg = lax.dot_general(dout, wout_ref[...], NT, preferred_element_type=F32)
        sc = jax.nn.sigmoid(jnp.concatenate([zc0_ref[...], zc1_ref[...]], axis=1).astype(F32))
        sa = jax.nn.sigmoid(jnp.concatenate([za0_ref[...], za1_ref[...]], axis=1).astype(F32))
        dyc = (dmg * sc).astype(BF)
        dya = (dmg * sa).astype(BF)
        dyc_ref[...] = dyc
        dya_ref[...] = dya
        dz_ref[:, 0:D] = (dmg * yc_ref[...].astype(F32) * (sc * (1.0 - sc))).astype(BF)
        dz_ref[:, D:2 * D] = (dmg * ya_ref[...].astype(F32) * (sa * (1.0 - sa))).astype(BF)
        dgc_ref[...] = lax.dot_general(dyc, wcp_ref[...], NT, preferred_element_type=F32).astype(BF)
        dat_ref[...] = lax.dot_general(dya, wap_ref[...], NT, preferred_element_type=F32).astype(BF)

    wspec = pl.BlockSpec((D, D), _const2)
    rowspec = pl.BlockSpec((tm, D), _row)
    vec = pl.BlockSpec((1, D), _const2)
    return pl.pallas_call(
        body, name=name, grid=(T // tm,),
        in_specs=[rowspec, vec, rowspec,
                  _col(tm, O_ZC, 512), _col(tm, O_ZC + 512, 512), _col(tm, O_ZA, 512), _col(tm, O_ZA + 512, 512),
                  rowspec, rowspec, wspec, wspec, wspec],
        out_specs=[rowspec] * 5 + [pl.BlockSpec((tm, 2 * D), _row), vec],
        out_shape=[jax.ShapeDtypeStruct((T, D), BF)] * 5 + [jax.ShapeDtypeStruct((T, 2 * D), BF),
                                                            jax.ShapeDtypeStruct((1, D), F32)],
        compiler_params=_cp(("arbitrary",)),
    )(dx2, gt, o, proj, proj, proj, proj, yc, ya, wout, wcp, wap)


def _conv_bwd(dgc, proj, convw, *, tm, name):
    T = dgc.shape[0]
    tm = _tile(T, tm)
    hb = tm // HALO
    nblk = T // HALO
    nt = T // tm

    def body(dgc_ref, ndgc_ref, bg_ref, nbg_ref, cg_ref, u_ref, hcg_ref, hu_ref, cw_ref, dp_ref, dcw_ref):
        i = pl.program_id(0)

        @pl.when(i == 0)
        def _():
            dcw_ref[...] = jnp.zeros_like(dcw_ref)
        first = jnp.where(i == 0, 0.0, 1.0)
        last = jnp.where(i == nt - 1, 0.0, 1.0)
        cg = cg_ref[...].astype(F32)
        u = u_ref[...].astype(F32)
        bg = bg_ref[...].astype(F32)
        dg = dgc_ref[...].astype(F32)
        cu = cg * u
        hprev = first * (hcg_ref[...].astype(F32) * hu_ref[...].astype(F32))
        m1, m2 = _conv_shifts(cu, hprev, tm)
        w0, w1, w2 = cw_ref[0:1, :], cw_ref[1:2, :], cw_ref[2:3, :]
        cv = w0 * m2 + w1 * m1 + w2 * cu
        dcv = dg * bg
        nxt = last * (ndgc_ref[...].astype(F32) * nbg_ref[...].astype(F32))
        n0, n1 = nxt[0:1, :], nxt[1:2, :]
        row = lax.broadcasted_iota(jnp.int32, dcv.shape, 0)
        p1 = jnp.where(row == tm - 1, n0, pltpu.roll(dcv, tm - 1, 0))
        p2 = jnp.where(row == tm - 1, n1, jnp.where(row == tm - 2, n0, pltpu.roll(dcv, tm - 2, 0)))
        dcu = w2 * dcv + w1 * p1 + w0 * p2
        dp_ref[:, 0:D] = (dg * cv).astype(BF)
        dp_ref[:, D:2 * D] = (dcu * u).astype(BF)
        dp_ref[:, 2 * D:3 * D] = (dcu * cg).astype(BF)
        dcw_ref[0:1, :] += jnp.sum(dcv * m2, axis=0, keepdims=True)
        dcw_ref[1:2, :] += jnp.sum(dcv * m1, axis=0, keepdims=True)
        dcw_ref[2:3, :] += jnp.sum(dcv * cu, axis=0, keepdims=True)

    rowspec = pl.BlockSpec((tm, D), _row)
    cw = pl.BlockSpec((8, D), _const2)
    return pl.pallas_call(
        body, name=name, grid=(nt,),
        in_specs=[rowspec, _halo_next(hb, nblk), _col(tm, O_BG), _halo_next(hb, nblk, O_BG),
                  _col(tm, O_CG), _col(tm, O_U), _halo_prev(hb, O_CG), _halo_prev(hb, O_U), cw],
        out_specs=[pl.BlockSpec((tm, 3 * D), _row), cw],
        out_shape=[jax.ShapeDtypeStruct((T, 3 * D), BF), jax.ShapeDtypeStruct((8, D), F32)],
        compiler_params=_cp(("arbitrary",)),
    )(dgc, dgc, proj, proj, proj, proj, proj, proj, convw)


def _gsum(parts, *, tm, name):
    _, R, C = parts.shape
    tm = _tile(R, tm)

    def body(p_ref, o_ref):
        acc = p_ref[0].astype(F32)
        for s in range(1, N_DEV):
            acc = acc + p_ref[s].astype(F32)
        o_ref[...] = acc

    return pl.pallas_call(
        body, name=name, grid=(R // tm,),
        in_specs=[pl.BlockSpec((N_DEV, tm, C), lambda i: (0, i, 0))],
        out_specs=pl.BlockSpec((tm, C), _row),
        out_shape=jax.ShapeDtypeStruct((R, C), F32),
        compiler_params=_cp(("parallel",)),
    )(parts)


def _adam(w, g, m, v, *, tm, name):
    R, C = w.shape
    tm = _tile(R, tm)
    c1 = 1.0 - ADAM_B1
    c2 = 1.0 - ADAM_B2
    bc1 = 1.0 - ADAM_B1 ** ADAM_STEP
    bc2 = 1.0 - ADAM_B2 ** ADAM_STEP

    def body(w_ref, g_ref, m_ref, v_ref, d_ref, nm_ref, nv_ref):
        gv = g_ref[...]
        nm = ADAM_B1 * m_ref[...] + c1 * gv
        nv = ADAM_B2 * v_ref[...] + c2 * (gv * gv)
        nm_ref[...] = nm
        nv_ref[...] = nv
        d_ref[...] = -ADAM_LR * ((nm / bc1) / (jnp.sqrt(nv / bc2) + ADAM_EPS) + ADAM_WD * w_ref[...])

    spec = pl.BlockSpec((tm, C), _row)
    return pl.pallas_call(
        body, name=name, grid=(R // tm,),
        in_specs=[spec] * 4, out_specs=[spec] * 3,
        out_shape=[jax.ShapeDtypeStruct((R, C), F32)] * 3,
        compiler_params=_cp(("parallel",)),
    )(w, g, m, v)


def _mods_part(c_all, w_ada, b_ada, *, name):
    C = w_ada.shape[1]

    def body(c_ref, w_ref, b_ref, o_ref):
        cv = c_ref[...]
        ca = cv * jax.nn.sigmoid(cv)
        o_ref[...] = jnp.dot(ca, w_ref[...], preferred_element_type=F32,
                             precision=lax.Precision.HIGHEST) + b_ref[...]

    return pl.pallas_call(
        body, name=name,
        out_shape=jax.ShapeDtypeStruct((N_DEV, C), F32),
        compiler_params=_cp(),
    )(c_all, w_ada, b_ada)


def _wada_grad(c_all_t, gm, *, name):
    C = gm.shape[1]

    def body(c_ref, g_ref, o_ref):
        cv = c_ref[...]
        ca = cv * jax.nn.sigmoid(cv)
        acc = ca[:, 0:1] * g_ref[0:1, :]
        for b in range(1, N_DEV):
            acc = acc + ca[:, b:b + 1] * g_ref[b:b + 1, :]
        o_ref[...] = acc

    return pl.pallas_call(
        body, name=name,
        out_shape=jax.ShapeDtypeStruct((D, C), F32),
        compiler_params=_cp(),
    )(c_all_t, gm)


def _peer(x, y, c, d):
    px = lax.rem(x + ((d >> 2) & 1), 2)
    py = lax.rem(y + ((d >> 1) & 1), 2)
    pc = lax.rem(c + (d & 1), 2)
    return (px, py, pc), 4 * px + 2 * py + pc


def _exchange(xs, *, scatter, name):
    n = len(xs)
    nsem = n * (N_DEV - 1)

    def body(*refs):
        ins, outs = refs[:n], refs[n:2 * n]
        send_sems, recv_sems, local_sems = refs[2 * n:]
        x, y, c = lax.axis_index("x"), lax.axis_index("y"), lax.axis_index("c")
        me = 4 * x + 2 * y + c

        def src(t, idx):
            return ins[t].at[idx] if scatter else ins[t]

        local = [pltpu.make_async_copy(src(t, me), outs[t].at[me], local_sems.at[t]) for t in range(n)]
        for cp in local:
            cp.start()
        remote = []
        for t in range(n):
            for d in range(1, N_DEV):
                peer, pidx = _peer(x, y, c, d)
                k = t * (N_DEV - 1) + d - 1
                send = pltpu.make_async_remote_copy(src_ref=src(t, pidx), dst_ref=outs[t].at[me],
                                                    send_sem=send_sems.at[k], recv_sem=recv_sems.at[k],
                                                    device_id=peer, device_id_type=MESH)
                recv = pltpu.make_async_remote_copy(src_ref=src(t, pidx), dst_ref=outs[t].at[pidx],
                                                    send_sem=send_sems.at[k], recv_sem=recv_sems.at[k],
                                                    device_id=peer, device_id_type=MESH)
                send.start()
                remote.append((send, recv))
        for cp in local:
            cp.wait()
        for send, recv in remote:
            send.wait_send()
            recv.wait_recv()

    anyspec = pl.BlockSpec(memory_space=pl.ANY)
    out_shape = [jax.ShapeDtypeStruct(a.shape if scatter else (N_DEV,) + a.shape, a.dtype) for a in xs]
    return pl.pallas_call(
        body, name=name,
        in_specs=[anyspec] * n, out_specs=[anyspec] * n, out_shape=out_shape,
        scratch_shapes=[pltpu.SemaphoreType.DMA((nsem,)), pltpu.SemaphoreType.DMA((nsem,)),
                        pltpu.SemaphoreType.DMA((n,))],
    )(*xs)


def _sum8(parts, *, name):
    _, R, C = parts.shape

    def body(p_ref, o_ref):
        acc = p_ref[0]
        for s in range(1, N_DEV):
            acc = acc + p_ref[s]
        o_ref[...] = acc

    return pl.pallas_call(body, name=name, out_shape=jax.ShapeDtypeStruct((R, C), F32),
                          compiler_params=_cp())(parts)


HBM_SPEC = pl.BlockSpec(memory_space=pltpu.HBM)
SEM_SPEC = pl.BlockSpec(memory_space=pltpu.SEMAPHORE)
N_PEER = N_DEV - 1


def _split_copies(src_refs, land_refs, send_sems, recv_sems, scatter):
    x, y, c = lax.axis_index("x"), lax.axis_index("y"), lax.axis_index("c")
    me = 4 * x + 2 * y + c
    pairs = []
    for j, (src, land) in enumerate(zip(src_refs, land_refs)):
        for d in range(1, N_DEV):
            peer, pidx = _peer(x, y, c, d)
            k = j * N_PEER + d - 1
            s = src.at[pidx] if scatter else src
            send = pltpu.make_async_remote_copy(src_ref=s, dst_ref=land.at[me], send_sem=send_sems.at[k],
                                                recv_sem=recv_sems.at[k], device_id=peer, device_id_type=MESH)
            recv = pltpu.make_async_remote_copy(src_ref=s, dst_ref=land.at[pidx], send_sem=send_sems.at[k],
                                                recv_sem=recv_sems.at[k], device_id=peer, device_id_type=MESH)
            pairs.append((send, recv))
    return pairs


def _own_slot(block, me):
    land = lax.empty((N_DEV,) + block.shape, block.dtype)
    return lax.dynamic_update_slice(land, block[None], (me, 0, 0))


def _split_start(srcs, lands, groups, *, scatter, name):
    n, ng = len(srcs), len(groups)

    def body(*refs):
        src_refs, land_refs = refs[:n], refs[n:2 * n]
        sems = refs[2 * n:2 * n + 2 * ng]
        token = refs[-1]
        for gi, g in enumerate(groups):
            pairs = _split_copies([src_refs[t] for t in g], [land_refs[t] for t in g], sems[2 * gi],
                                  sems[2 * gi + 1], scatter)
            for send, _ in pairs:
                send.start()
        token[...] = jnp.zeros_like(token)

    sem_shapes = []
    for g in groups:
        sem_shapes += [pltpu.SemaphoreType.DMA((len(g) * N_PEER,))] * 2
    thru = [pltpu.HBM(a.shape, a.dtype) for a in list(srcs) + list(lands)]
    outs = pl.pallas_call(
        body, name=name,
        out_shape=tuple(sem_shapes + thru + [jax.ShapeDtypeStruct((8, 128), F32)]),
        in_specs=[HBM_SPEC] * (2 * n),
        out_specs=tuple([SEM_SPEC] * (2 * ng) + [HBM_SPEC] * (2 * n) + [pl.BlockSpec(memory_space=pltpu.VMEM)]),
        input_output_aliases={i: 2 * ng + i for i in range(2 * n)},
        compiler_params=pltpu.CompilerParams(has_side_effects=pltpu.SideEffectType.DATAFLOW_SIDE_EFFECTING),
    )(*[pltpu.with_memory_space_constraint(a, pltpu.HBM) for a in list(srcs) + list(lands)])
    sems = [(outs[2 * gi], outs[2 * gi + 1]) for gi in range(ng)]
    return sems, outs[2 * ng:2 * ng + n], outs[2 * ng + n:2 * ng + 2 * n], outs[-1][0, 0]


def _split_wait(srcs, lands, sems, after, *, scatter, name):
    m = len(srcs)

    def body(*refs):
        src_refs, land_refs = refs[:m], refs[m:2 * m]
        send_sems, recv_sems = refs[2 * m], refs[2 * m + 1]
        for send, recv in _split_copies(src_refs, land_refs, send_sems, recv_sems, scatter):
            send.wait_send()
            recv.wait_recv()

    outs = pl.pallas_call(
        body, name=name,
        out_shape=tuple(pltpu.HBM(a.shape, a.dtype) for a in list(srcs) + list(lands)),
        in_specs=[HBM_SPEC] * (2 * m) + [SEM_SPEC, SEM_SPEC, pl.BlockSpec(memory_space=pl.ANY)],
        out_specs=tuple([HBM_SPEC] * (2 * m)),
        input_output_aliases={i: i for i in range(2 * m)},
        compiler_params=pltpu.CompilerParams(has_side_effects=pltpu.SideEffectType.DATAFLOW_SIDE_EFFECTING),
    )(*srcs, *lands, sems[0], sems[1], after)
    return outs[m:]


TM_PROJ, TN_PROJ = 1024, 512
TM_ROW = 512
TM_NN, TK_NN = 1024, 512
TN_TN, TK_TN = 512, 1024


def _tn(a, b, name, tn=TN_TN):
    if a.ndim == 2:
        a = a[None]
    return _tn_matmul(a, b, tn=tn, tk=TK_TN, name=name)


def _local_step(x, tgt, mods, g1, gm, g2, gf, convw8, sinks, w_get, g_put):
    T = x.shape[0]
    sh1, sc1, gt1, sh2, sc2, gt2, sh3, sc3, gt3 = [mods[i:i + 1] for i in range(N_MOD)]
    cos, sin = _rope_tables(T)

    def behind(v, tok):
        return v if tok is None else v + tok

    w = dict(w_get("gu1", mods))
    h1, ab1 = _norm_proj(x, g1, sc1, sh1, w["gu1"], tm=TM_PROJ, tn=TN_PROJ, name="ffn1_up")
    w.update(w_get("d1", ab1))
    x1, y1 = _ffn_down_fwd(ab1, w["d1"], x, gt1, tm=TM_ROW, name="ffn1_down")
    w.update(w_get("mix", x1))
    h2, proj = _norm_proj(x1, gm, sc2, sh2, w["win"], tm=TM_PROJ, tn=TN_PROJ, name="mix_in")
    attn = _attn_fwd(proj, cos, sin, sinks, name="attn_fwd")
    x2, gc, yc, ya, mg, o = _mixer_mid_fwd(proj, attn, w["cp"], w["ap"], w["out"], convw8, x1, gt2,
                                           tm=TM_ROW, name="mix_mid")
    w.update(w_get("ffn2", x2))
    h3, ab2 = _norm_proj(x2, g2, sc3, sh3, w["gu2"], tm=TM_PROJ, tn=TN_PROJ, name="ffn2_up")
    x3, y2 = _ffn_down_fwd(ab2, w["d2"], x2, gt3, tm=TM_ROW, name="ffn2_down")
    dx3, lsum, dgf = _final_fwd_bwd(x3, tgt, gf, tm=TM_ROW, name="final")

    dy2, dab2, dgt3 = _ffn_down_bwd(dx3, y2, gt3, ab2, w["d2"], tm=TM_PROJ, tn=256, name="ffn2_down_bwd")
    g_d2 = _tn_matmul_swiglu(ab2, dy2, tn=256, tk=TK_TN, name="ffn2_down_dw")
    dx2, dsh3, dsc3, dg2 = _nn_bwd_norm(dab2, w["gu2"], x2, g2, sc3, dx3, tm=TM_NN, tk=256, name="ffn2_up_bwd")
    g_gu2 = _tn(dab2, h3, "ffn2_up_dw", tn=256)
    tok = g_put(dict(gu2=g_gu2, d2=g_d2))

    dout, dyc, dya, dgc, dat, dz, dgt2 = _mixer_mid_bwd(dx2, behind(gt2, tok), o, proj, yc, ya, w["out"], w["cp"],
                                                        w["ap"], tm=TM_ROW, name="mix_mid_bwd")
    g_out = _tn(mg, dout, "mix_out_dw")
    g_cp = _tn(gc, dyc, "mix_cp_dw")
    g_ap = _tn(attn, dya, "mix_ap_dw")
    dq, dkc, dkp, dvc, dvp, dsink = _attn_bwd(proj, cos, sin, sinks, attn, dat, name="attn_bwd")
    dkv = _dkv_combine(dkc, dkp, dvc, dvp, name="attn_dkv")
    dp1, dcw = _conv_bwd(dgc, proj, convw8, tm=TM_ROW, name="conv_bwd")
    dproj = jnp.concatenate([dp1, dq, dkv, dz], axis=1)
    g_in = _tn(dproj, h2, "mix_in_dw")
    tok = g_put(dict(win=g_in, cp=g_cp, ap=g_ap, out=g_out))
    dx1, dsh2, dsc2, dgm = _nn_bwd_norm(dproj[None], w["win"], x1, gm, behind(sc2, tok), dx2, tm=TM_NN, tk=TK_NN,
                                        name="mix_in_bwd")

    dy1, dab1, dgt1 = _ffn_down_bwd(dx1, y1, gt1, ab1, w["d1"], tm=TM_PROJ, tn=256, name="ffn1_down_bwd")
    g_d1 = _tn_matmul_swiglu(ab1, dy1, tn=256, tk=TK_TN, name="ffn1_down_dw")
    tok = g_put(dict(d1=g_d1))
    dx0, dsh1, dsc1, dg1 = _nn_bwd_norm(dab1, w["gu1"], x, g1, behind(sc1, tok), dx1, tm=TM_NN, tk=256,
                                        name="ffn1_up_bwd")
    g_gu1 = _tn(dab1, h1, "ffn1_up_dw", tn=256)
    g_put(dict(gu1=g_gu1))

    small = dict(mods=jnp.concatenate([dsh1, dsc1, dgt1, dsh2, dsc2, dgt2, dsh3, dsc3, dgt3], axis=0),
                 g1=dg1, gm=dgm, g2=dg2, gf=dgf, convw=dcw[0:3], sinks=dsink[:, 0:N_HEADS])
    return lsum, dx0, small


BIG = ("gu1", "d1", "win", "cp", "ap", "out", "gu2", "d2")
TRANSPOSED = ("gu1", "win", "gu2")
SMALL_ROWS = 24
R_MODS, R_G1, R_GM, R_G2, R_GF, R_CONV, R_SINK = 0, 9, 10, 11, 12, 13, 16


def _pad_to(a, rows, cols):
    return jnp.pad(a, ((0, rows - a.shape[0]), (0, cols - a.shape[1])))


def _pack_small(b_ada, g1, gm, g2, gf, conv, sinks):
    rows = [b_ada.reshape(N_MOD, D), g1.reshape(1, D), gm.reshape(1, D), g2.reshape(1, D), gf.reshape(1, D),
            _pad_to(conv.reshape(3, -1), 3, D), _pad_to(sinks.reshape(1, N_HEADS), 1, D)]
    return _pad_to(jnp.concatenate(rows, axis=0), SMALL_ROWS, D)


def _unpack_small(p, conv_cols):
    return dict(b_ada=p[R_MODS:R_MODS + N_MOD].reshape(1, N_MOD * D), g_ffn1=p[R_G1:R_G1 + 1],
                g_mix=p[R_GM:R_GM + 1], g_ffn2=p[R_G2:R_G2 + 1], g_final=p[R_GF],
                conv_w=p[R_CONV:R_CONV + 3, 0:conv_cols][None], sinks=p[R_SINK:R_SINK + 1, 0:N_HEADS])


def kernel(x, c, w_ada, b_ada, g_ffn1, w1_gu, w1_down, g_mix, w_in, conv_w, w_conv_proj, w_attn_proj, sinks, w_out, g_ffn2, w2_gu, w2_down, g_final, loss_target, m_w_ada, m_b_ada, m_g_ffn1, m_w1_gu, m_w1_down, m_g_mix, m_w_in, m_conv_w, m_w_conv_proj, m_w_attn_proj, m_sinks, m_w_out, m_g_ffn2, m_w2_gu, m_w2_down, m_g_final, v_w_ada, v_b_ada, v_g_ffn1, v_w1_gu, v_w1_down, v_g_mix, v_w_in, v_conv_w, v_w_conv_proj, v_w_attn_proj, v_sinks, v_w_out, v_g_ffn2, v_w2_gu, v_w2_down, v_g_final):
    me = 4 * lax.axis_index("x") + 2 * lax.axis_index("y") + lax.axis_index("c")
    ada_cols = w_ada.shape[2]
    conv_cols = conv_w.shape[2]

    native = dict(gu1=w1_gu[0], d1=w1_down[0], win=w_in[0], cp=w_conv_proj[0], ap=w_attn_proj[0], out=w_out[0],
                  gu2=w2_gu[0], d2=w2_down[0])
    shards = [(native[n].T if n in TRANSPOSED else native[n]).astype(BF) for n in BIG]
    groups = dict(gu1=("gu1",), d1=("d1",), mix=("win", "cp", "ap", "out"), ffn2=("gu2", "d2"))
    group_idx = [[BIG.index(n) for n in names] for names in groups.values()]
    w_sems, w_srcs, w_lands, tok = _split_start(shards, [_own_slot(s, me) for s in shards], group_idx,
                                                scatter=False, name="gather_weights_start")

    def w_get(group, after):
        idx = group_idx[list(groups).index(group)]
        landed = _split_wait([w_srcs[t] for t in idx], [w_lands[t] for t in idx], w_sems[list(groups).index(group)],
                             after, scatter=False, name="gather_weights_wait_" + group)
        return {BIG[t]: a.reshape(-1, D) for t, a in zip(idx, landed)}

    pending = []

    def g_put(gs):
        names = tuple(gs)
        srcs = [gs[n].reshape(N_DEV, -1, D) for n in names]
        lands = [_own_slot(lax.dynamic_index_in_dim(s, me, axis=0, keepdims=False), me) for s in srcs]
        sems, srcs, lands, tok = _split_start(srcs, lands, [list(range(len(names)))], scatter=True,
                                              name="scatter_grads_start_" + names[0])
        pending.append((names, sems[0], srcs, lands))
        return tok

    c_all, conv_all = _exchange([c + tok, _pad_to(conv_w[0], 8, conv_cols)], scatter=False, name="gather_cond")
    c_all = c_all.reshape(N_DEV, D)
    conv_full = conv_all[:, 0:3, :].transpose(1, 0, 2).reshape(3, D)

    b_cols = lax.dynamic_slice(b_ada, (0, me * ada_cols), (1, ada_cols))
    mods_cols = _mods_part(c_all, w_ada[0], b_cols, name="ada_mods")
    (mods_all,) = _exchange([mods_cols], scatter=False, name="gather_mods")
    mods = lax.dynamic_index_in_dim(mods_all, me, axis=1, keepdims=False).reshape(N_MOD, D)

    lsum, grad_x, small = _local_step(x[0], loss_target[0], mods, g_ffn1, g_mix, g_ffn2, g_final[None],
                                      _pad_to(conv_full, 8, D), sinks[0], w_get, g_put)
    loss = lax.psum((0.5 / D) * jnp.sum(lsum), ("x", "y", "c"))

    packed = _pack_small(small["mods"], small["g1"], small["gm"], small["g2"], small["gf"], small["convw"],
                         small["sinks"])
    (packed_all,) = _exchange([packed], scatter=False, name="gather_small")
    gsmall = _sum8(packed_all, name="sum_small")

    grads = {}
    after = gsmall
    for names, sems, srcs, lands in pending:
        parts = _split_wait(srcs, lands, sems, after, scatter=True, name="scatter_grads_wait_" + names[0])
        for n, p in zip(names, parts):
            g = _gsum(p, tm=128, name="gsum_" + n)
            grads[n] = g.T if n in TRANSPOSED else g
        after = g

    gm_cols = lax.dynamic_slice(packed_all[:, R_MODS:R_MODS + N_MOD, :].reshape(N_DEV, N_MOD * D),
                                (0, me * ada_cols), (N_DEV, ada_cols))
    grads["ada"] = _wada_grad(c_all.T, gm_cols, name="ada_dw")
    conv_g = lax.dynamic_slice(gsmall[R_CONV:R_CONV + 3], (0, me * conv_cols), (3, conv_cols))
    gsmall_own = gsmall.at[R_CONV:R_CONV + 3].set(_pad_to(conv_g, 3, D))
    g_small = _unpack_small(gsmall_own, conv_cols)

    w_of = dict(ada=w_ada, gu1=w1_gu, d1=w1_down, win=w_in, cp=w_conv_proj, ap=w_attn_proj, out=w_out, gu2=w2_gu,
                d2=w2_down)
    m_of = dict(ada=m_w_ada, gu1=m_w1_gu, d1=m_w1_down, win=m_w_in, cp=m_w_conv_proj, ap=m_w_attn_proj, out=m_w_out,
                gu2=m_w2_gu, d2=m_w2_down)
    v_of = dict(ada=v_w_ada, gu1=v_w1_gu, d1=v_w1_down, win=v_w_in, cp=v_w_conv_proj, ap=v_w_attn_proj, out=v_w_out,
                gu2=v_w2_gu, d2=v_w2_down)
    upd = {n: _adam(w_of[n][0], grads[n], m_of[n][0], v_of[n][0], tm=128, name="adam_" + n) for n in w_of}
    small_upd = _adam(_pack_small(b_ada, g_ffn1, g_mix, g_ffn2, g_final, conv_w, sinks), gsmall_own,
                      _pack_small(m_b_ada, m_g_ffn1, m_g_mix, m_g_ffn2, m_g_final, m_conv_w, m_sinks),
                      _pack_small(v_b_ada, v_g_ffn1, v_g_mix, v_g_ffn2, v_g_final, v_conv_w, v_sinks),
                      tm=SMALL_ROWS, name="adam_small")
    small_out = [g_small] + [_unpack_small(p, conv_cols) for p in small_upd]

    big_name = dict(w_ada="ada", w1_gu="gu1", w1_down="d1", w_in="win", w_conv_proj="cp", w_attn_proj="ap",
                    w_out="out", w2_gu="gu2", w2_down="d2")
    order = ("w_ada", "b_ada", "g_ffn1", "w1_gu", "w1_down", "g_mix", "w_in", "conv_w", "w_conv_proj", "w_attn_proj",
             "sinks", "w_out", "g_ffn2", "w2_gu", "w2_down", "g_final")
    outs = [loss, grad_x[None]]
    for kind in range(4):
        for n in order:
            if n in big_name:
                t = grads[big_name[n]] if kind == 0 else upd[big_name[n]][kind - 1]
                outs.append(t[None])
            else:
                outs.append(small_out[kind][n])
    return tuple(outs)
```

```python
import functools

import jax
import jax.numpy as jnp
from jax import lax
from jax.experimental import pallas as pl
from jax.experimental.pallas import tpu as pltpu

D = 1024
F = 2816
NIN = 6656
N_HEADS = 16
N_KV = 4
HEAD_DIM = 64
BLK = 128
N_MOD = 9
N_DEV = 8
EPS = 1e-6
NEG_INF = -1e30
ROPE_THETA = 10000.0
O_BG, O_CG, O_U, O_Q, O_K, O_V, O_ZC, O_ZA = 0, 1024, 2048, 3072, 4096, 4352, 4608, 5632

ADAM_LR = 0.001
ADAM_B1 = 0.9
ADAM_B2 = 0.999
ADAM_EPS = 1e-08
ADAM_WD = 0.01
ADAM_STEP = 10

BF = jnp.bfloat16
F32 = jnp.float32
VMEM_LIMIT = 56 * 1024 * 1024
MESH = pl.DeviceIdType.MESH

NT = (((1,), (1,)), ((), ()))
TN = (((0,), (0,)), ((), ()))


def _cp(sem=None):
    return pltpu.CompilerParams(dimension_semantics=sem, vmem_limit_bytes=VMEM_LIMIT)


def _tile(n, pref):
    if n <= pref:
        return n
    for t in range(pref - pref % 16, 15, -16):
        if n % t == 0:
            return t
    raise ValueError((n, pref))


def _row(i):
    return (i, 0)


def _const2(*_):
    return (0, 0)


def _norm_proj(x, g, sc, sh, wt, *, tm, tn, name):
    T, N = x.shape[0], wt.shape[0]
    tm, tn = _tile(T, tm), _tile(N, tn)

    def body(x_ref, g_ref, sc_ref, sh_ref, w_ref, h_ref, o_ref, hs):
        @pl.when(pl.program_id(1) == 0)
        def _():
            xv = x_ref[...]
            r = lax.rsqrt(jnp.mean(xv * xv, axis=-1, keepdims=True) + EPS)
            hb = ((xv * r) * g_ref[...] * (1.0 + sc_ref[...]) + sh_ref[...]).astype(BF)
            hs[...] = hb
            h_ref[...] = hb
        o_ref[...] = lax.dot_general(hs[...], w_ref[...], NT, preferred_element_type=F32).astype(BF)

    vec = pl.BlockSpec((1, D), _const2)
    return pl.pallas_call(
        body, name=name, grid=(T // tm, N // tn),
        in_specs=[pl.BlockSpec((tm, D), lambda i, j: (i, 0)), vec, vec, vec,
                  pl.BlockSpec((tn, D), lambda i, j: (j, 0))],
        out_specs=[pl.BlockSpec((tm, D), lambda i, j: (i, 0)), pl.BlockSpec((tm, tn), lambda i, j: (i, j))],
        out_shape=[jax.ShapeDtypeStruct((T, D), BF), jax.ShapeDtypeStruct((T, N), BF)],
        scratch_shapes=[pltpu.VMEM((tm, D), BF)],
        compiler_params=_cp(("parallel", "arbitrary")),
    )(x, g, sc, sh, wt)


def _ffn_down_fwd(ab, wd, x, gt, *, tm, name):
    T = x.shape[0]
    tm = _tile(T, tm)

    def body(a_ref, b_ref, wd_ref, x_ref, gt_ref, xo_ref, y_ref):
        a = a_ref[...].astype(F32)
        act = (a * jax.nn.sigmoid(a) * b_ref[...].astype(F32)).astype(BF)
        y = jnp.dot(act, wd_ref[...], preferred_element_type=F32)
        y_ref[...] = y.astype(BF)
        xo_ref[...] = x_ref[...] + (0.5 * gt_ref[...]) * y

    return pl.pallas_call(
        body, name=name, grid=(T // tm,),
        in_specs=[pl.BlockSpec((tm, F), lambda i: (i, 0)), pl.BlockSpec((tm, F), lambda i: (i, 1)),
                  pl.BlockSpec((F, D), _const2), pl.BlockSpec((tm, D), _row), pl.BlockSpec((1, D), _const2)],
        out_specs=[pl.BlockSpec((tm, D), _row), pl.BlockSpec((tm, D), _row)],
        out_shape=[jax.ShapeDtypeStruct((T, D), F32), jax.ShapeDtypeStruct((T, D), BF)],
        compiler_params=_cp(("parallel",)),
    )(ab, ab, wd, x, gt)


def _final_fwd_bwd(x, tgt, g, *, tm, name):
    T = x.shape[0]
    tm = _tile(T, tm)

    def body(x_ref, t_ref, g_ref, dx_ref, ls_ref, dg_ref):
        @pl.when(pl.program_id(0) == 0)
        def _():
            ls_ref[...] = jnp.zeros_like(ls_ref)
            dg_ref[...] = jnp.zeros_like(dg_ref)
        xv = x_ref[...]
        gv = g_ref[...]
        r = lax.rsqrt(jnp.mean(xv * xv, axis=-1, keepdims=True) + EPS)
        xh = xv * r
        e = xh * gv - t_ref[...]
        ls_ref[...] += jnp.sum(e * e, axis=0, keepdims=True)
        dy = e * (1.0 / D)
        dg_ref[...] += jnp.sum(dy * xh, axis=0, keepdims=True)
        dxh = dy * gv
        dx_ref[...] = r * (dxh - xh * jnp.mean(dxh * xh, axis=-1, keepdims=True))

    vec = pl.BlockSpec((1, D), _const2)
    return pl.pallas_call(
        body, name=name, grid=(T // tm,),
        in_specs=[pl.BlockSpec((tm, D), _row), pl.BlockSpec((tm, D), _row), vec],
        out_specs=[pl.BlockSpec((tm, D), _row), vec, vec],
        out_shape=[jax.ShapeDtypeStruct((T, D), F32), jax.ShapeDtypeStruct((1, D), F32),
                   jax.ShapeDtypeStruct((1, D), F32)],
        compiler_params=_cp(("arbitrary",)),
    )(x, tgt, g)


def _ffn_down_bwd(dxo, y, gt, ab, wd, *, tm, tn, name):
    T = dxo.shape[0]
    tm, tn = _tile(T, tm), _tile(F, tn)
    nj = F // tn

    def body(dxo_ref, y_ref, gt_ref, a_ref, b_ref, wd_ref, dy_ref, dab_ref, dgt_ref, dys):
        i, j = pl.program_id(0), pl.program_id(1)

        @pl.when(jnp.logical_and(i == 0, j == 0))
        def _():
            dgt_ref[...] = jnp.zeros_like(dgt_ref)

        @pl.when(j == 0)
        def _():
            dxv = dxo_ref[...]
            dgt_ref[...] += 0.5 * jnp.sum(dxv * y_ref[...].astype(F32), axis=0, keepdims=True)
            dyb = ((0.5 * gt_ref[...]) * dxv).astype(BF)
            dys[...] = dyb
            dy_ref[...] = dyb

        dact = lax.dot_general(dys[...], wd_ref[...], NT, preferred_element_type=F32)
        a = a_ref[...].astype(F32)
        b = b_ref[...].astype(F32)
        s = jax.nn.sigmoid(a)
        silu = a * s
        dab_ref[0] = (dact * b * (s * (1.0 + a * (1.0 - s)))).astype(BF)
        dab_ref[1] = (dact * silu).astype(BF)

    vec = pl.BlockSpec((1, D), _const2)
    return pl.pallas_call(
        body, name=name, grid=(T // tm, nj),
        in_specs=[pl.BlockSpec((tm, D), lambda i, j: (i, 0)), pl.BlockSpec((tm, D), lambda i, j: (i, 0)), vec,
                  pl.BlockSpec((tm, tn), lambda i, j: (i, j)), pl.BlockSpec((tm, tn), lambda i, j: (i, j + nj)),
                  pl.BlockSpec((tn, D), lambda i, j: (j, 0))],
        out_specs=[pl.BlockSpec((tm, D), lambda i, j: (i, 0)), pl.BlockSpec((2, tm, tn), lambda i, j: (0, i, j)), vec],
        out_shape=[jax.ShapeDtypeStruct((T, D), BF), jax.ShapeDtypeStruct((2, T, F), BF),
                   jax.ShapeDtypeStruct((1, D), F32)],
        scratch_shapes=[pltpu.VMEM((tm, D), BF)],
        compiler_params=_cp(("arbitrary", "arbitrary")),
    )(dxo, y, gt, ab, ab, wd)


def _tn_matmul(a, b, *, tn, tk, name):
    S, T, Ns = a.shape
    tn, tk = _tile(Ns, tn), _tile(T, tk)
    nk, njs = T // tk, Ns // tn

    def body(a_ref, b_ref, o_ref, acc):
        k = pl.program_id(1)

        @pl.when(k == 0)
        def _():
            acc[...] = jnp.zeros_like(acc)
        acc[...] += lax.dot_general(a_ref[0], b_ref[...], TN, preferred_element_type=F32)

        @pl.when(k == nk - 1)
        def _():
            o_ref[...] = acc[...].astype(BF)

    return pl.pallas_call(
        body, name=name, grid=(S * njs, nk),
        in_specs=[pl.BlockSpec((1, tk, tn), lambda j, k: (j // njs, k, j % njs)),
                  pl.BlockSpec((tk, D), lambda j, k: (k, 0))],
        out_specs=pl.BlockSpec((tn, D), lambda j, k: (j, 0)),
        out_shape=jax.ShapeDtypeStruct((S * Ns, D), BF),
        scratch_shapes=[pltpu.VMEM((tn, D), F32)],
        compiler_params=_cp(("parallel", "arbitrary")),
    )(a, b)


def _tn_matmul_swiglu(ab, b, *, tn, tk, name):
    T = ab.shape[0]
    tn, tk = _tile(F, tn), _tile(T, tk)
    nk, nj = T // tk, F // tn

    def body(a_ref, g_ref, b_ref, o_ref, acc):
        k = pl.program_id(1)

        @pl.when(k == 0)
        def _():
            acc[...] = jnp.zeros_like(acc)
        a = a_ref[...].astype(F32)
        act = (a * jax.nn.sigmoid(a) * g_ref[...].astype(F32)).astype(BF)
        acc[...] += lax.dot_general(act, b_ref[...], TN, preferred_element_type=F32)

        @pl.when(k == nk - 1)
        def _():
            o_ref[...] = acc[...].astype(BF)

    return pl.pallas_call(
        body, name=name, grid=(nj, nk),
        in_specs=[pl.BlockSpec((tk, tn), lambda j, k: (k, j)), pl.BlockSpec((tk, tn), lambda j, k: (k, j + nj)),
                  pl.BlockSpec((tk, D), lambda j, k: (k, 0))],
        out_specs=pl.BlockSpec((tn, D), lambda j, k: (j, 0)),
        out_shape=jax.ShapeDtypeStruct((F, D), BF),
        scratch_shapes=[pltpu.VMEM((tn, D), F32)],
        compiler_params=_cp(("parallel", "arbitrary")),
    )(ab, ab, b)


def _nn_bwd_norm(da, w, x, g, sc, dxo, *, tm, tk, name):
    S, T, Ks = da.shape
    tm, tk = _tile(T, tm), _tile(Ks, tk)
    nks = Ks // tk
    nk = S * nks

    def body(da_ref, w_ref, x_ref, g_ref, sc_ref, dxo_ref, dx_ref, dsh_ref, dsc_ref, dg_ref, acc):
        i, k = pl.program_id(0), pl.program_id(1)

        @pl.when(jnp.logical_and(i == 0, k == 0))
        def _():
            dsh_ref[...] = jnp.zeros_like(dsh_ref)
            dsc_ref[...] = jnp.zeros_like(dsc_ref)
            dg_ref[...] = jnp.zeros_like(dg_ref)

        @pl.when(k == 0)
        def _():
            acc[...] = jnp.zeros_like(acc)
        acc[...] += jnp.dot(da_ref[0], w_ref[...], preferred_element_type=F32)

        @pl.when(k == nk - 1)
        def _():
            u = acc[...]
            xv = x_ref[...]
            gv = g_ref[...]
            sc1 = 1.0 + sc_ref[...]
            r = lax.rsqrt(jnp.mean(xv * xv, axis=-1, keepdims=True) + EPS)
            xh = xv * r
            dsh_ref[...] += jnp.sum(u, axis=0, keepdims=True)
            dsc_ref[...] += jnp.sum(u * (xh * gv), axis=0, keepdims=True)
            us = u * sc1
            dg_ref[...] += jnp.sum(us * xh, axis=0, keepdims=True)
            dxh = us * gv
            dx_ref[...] = dxo_ref[...] + r * (dxh - xh * jnp.mean(dxh * xh, axis=-1, keepdims=True))

    vec = pl.BlockSpec((1, D), _const2)
    return pl.pallas_call(
        body, name=name, grid=(T // tm, nk),
        in_specs=[pl.BlockSpec((1, tm, tk), lambda i, k: (k // nks, i, k % nks)),
                  pl.BlockSpec((tk, D), lambda i, k: (k, 0)),
                  pl.BlockSpec((tm, D), lambda i, k: (i, 0)), vec, vec,
                  pl.BlockSpec((tm, D), lambda i, k: (i, 0))],
        out_specs=[pl.BlockSpec((tm, D), lambda i, k: (i, 0)), vec, vec, vec],
        out_shape=[jax.ShapeDtypeStruct((T, D), F32)] + [jax.ShapeDtypeStruct((1, D), F32)] * 3,
        scratch_shapes=[pltpu.VMEM((tm, D), F32)],
        compiler_params=_cp(("arbitrary", "arbitrary")),
    )(da, w, x, g, sc, dxo)


def _rope(t, cos, sin_signed, lt32, inverse=False):
    sel = jnp.where(lt32, pltpu.roll(t, 96, 1), pltpu.roll(t, 32, 1))
    return t * cos - sel * sin_signed if inverse else t * cos + sel * sin_signed


def _rope_tables(T):
    inv = 1.0 / (ROPE_THETA ** (jnp.arange(0, HEAD_DIM, 2, dtype=F32) / HEAD_DIM))
    ang = jnp.arange(T, dtype=F32)[:, None] * inv[None, :]
    cos, sin = jnp.cos(ang), jnp.sin(ang)
    cos128 = jnp.tile(cos, (1, 4))
    sin128 = jnp.tile(jnp.concatenate([-sin, sin], axis=1), (1, 2))
    return cos128, sin128


def _attn_specs(nb):
    qspec = pl.BlockSpec((BLK, D), lambda n: (n, O_Q // D))
    kc = pl.BlockSpec((BLK, 256), lambda n: (n, O_K // 256))
    kp = pl.BlockSpec((BLK, 256), lambda n: (jnp.maximum(n - 1, 0), O_K // 256))
    vc = pl.BlockSpec((BLK, 256), lambda n: (n, O_V // 256))
    vp = pl.BlockSpec((BLK, 256), lambda n: (jnp.maximum(n - 1, 0), O_V // 256))
    tc = pl.BlockSpec((BLK, 128), lambda n: (n, 0))
    tp = pl.BlockSpec((BLK, 128), lambda n: (jnp.maximum(n - 1, 0), 0))
    return [qspec, kc, kp, vc, vp, tc, tc, tp, tp, pl.BlockSpec(memory_space=pltpu.SMEM)]


def _attn_common(q_ref, kc_ref, kp_ref, vc_ref, vp_ref, cc_ref, sc_ref, cp_ref, sp_ref):
    n = pl.program_id(0)
    lane = lax.broadcasted_iota(jnp.int32, (BLK, 128), 1)
    lt32 = (lane % HEAD_DIM) < (HEAD_DIM // 2)
    cc, sc, cp, sp = cc_ref[...], sc_ref[...], cp_ref[...], sp_ref[...]
    kr, vr = [], []
    for r in range(2):
        cols = slice(r * 128, (r + 1) * 128)
        kcur = _rope(kc_ref[:, cols].astype(F32), cc, sc, lt32)
        kprev = _rope(kp_ref[:, cols].astype(F32), cp, sp, lt32)
        kr.append(jnp.concatenate([kprev, kcur], axis=0).astype(BF))
        vr.append(jnp.concatenate([vp_ref[:, cols], vc_ref[:, cols]], axis=0))
    qr = [_rope(q_ref[:, p * 128:(p + 1) * 128].astype(F32), cc, sc, lt32) for p in range(8)]
    qi = lax.broadcasted_iota(jnp.int32, (4 * BLK, 2 * BLK), 0) % BLK
    kj = lax.broadcasted_iota(jnp.int32, (4 * BLK, 2 * BLK), 1)
    valid = (kj > qi) & (kj <= qi + BLK) & ((kj >= BLK) | (n > 0))
    halves = [lane < HEAD_DIM, lane >= HEAD_DIM]
    return qr, kr, vr, valid, halves, lt32, (cc, sc, cp, sp)


def _stack_heads(chunks, g, halves):
    half = g % 2
    parts = []
    for hh in range(4):
        h = 4 * g + hh
        t = chunks[h // 2]
        if h % 2 != half:
            t = pltpu.roll(t, HEAD_DIM, 1)
        parts.append(jnp.where(halves[half], t, 0.0))
    return jnp.concatenate(parts, axis=0)


def _softmax_sink(s, valid, sink_ref, g):
    s = jnp.where(valid, s * (HEAD_DIM ** -0.5), NEG_INF)
    sink = jnp.concatenate([jnp.full((BLK, 1), sink_ref[4 * g + hh], F32) for hh in range(4)], axis=0)
    m = jnp.maximum(jnp.max(s, axis=-1, keepdims=True), sink)
    p = jnp.exp(s - m)
    ps = jnp.exp(sink - m)
    inv = 1.0 / (jnp.sum(p, axis=-1, keepdims=True) + ps)
    return p * inv, ps * inv


def _attn_fwd(proj, cos, sin, sinks, *, name):
    T = proj.shape[0]
    nb = T // BLK

    def body(q_ref, kc_ref, kp_ref, vc_ref, vp_ref, cc_ref, sc_ref, cp_ref, sp_ref, sink_ref, o_ref):
        qr, kr, vr, valid, halves, _, _ = _attn_common(q_ref, kc_ref, kp_ref, vc_ref, vp_ref,
                                                        cc_ref, sc_ref, cp_ref, sp_ref)
        outs = [jnp.zeros((BLK, 128), F32) for _ in range(8)]
        for g in range(N_KV):
            r, half = g // 2, g % 2
            qs = _stack_heads(qr, g, halves).astype(BF)
            s = lax.dot_general(qs, kr[r], NT, preferred_element_type=F32)
            p, _ = _softmax_sink(s, valid, sink_ref, g)
            o = jnp.dot(p.astype(BF), vr[r], preferred_element_type=F32)
            for hh in range(4):
                h = 4 * g + hh
                oh = jnp.where(halves[half], o[hh * BLK:(hh + 1) * BLK], 0.0)
                if h % 2 != half:
                    oh = pltpu.roll(oh, HEAD_DIM, 1)
                outs[h // 2] = outs[h // 2] + oh
        o_ref[...] = jnp.concatenate(outs, axis=1).astype(BF)

    return pl.pallas_call(
        body, name=name, grid=(nb,),
        in_specs=_attn_specs(nb),
        out_specs=pl.BlockSpec((BLK, D), _row),
        out_shape=jax.ShapeDtypeStruct((T, D), BF),
        compiler_params=_cp(("parallel",)),
    )(proj, proj, proj, proj, proj, cos, sin, cos, sin, sinks)


def _attn_bwd(proj, cos, sin, sinks, o, do, *, name):
    T = proj.shape[0]
    nb = T // BLK

    def body(q_ref, kc_ref, kp_ref, vc_ref, vp_ref, cc_ref, sc_ref, cp_ref, sp_ref, sink_ref, o_ref, do_ref,
             dq_ref, dkc_ref, dkp_ref, dvc_ref, dvp_ref, dsink_ref):
        @pl.when(pl.program_id(0) == 0)
        def _():
            dsink_ref[...] = jnp.zeros_like(dsink_ref)
        qr, kr, vr, valid, halves, lt32, (cc, sc, cp, sp) = _attn_common(
            q_ref, kc_ref, kp_ref, vc_ref, vp_ref, cc_ref, sc_ref, cp_ref, sp_ref)
        oc = [o_ref[:, p * 128:(p + 1) * 128].astype(F32) for p in range(8)]
        doc = [do_ref[:, p * 128:(p + 1) * 128].astype(F32) for p in range(8)]
        dqs = [jnp.zeros((BLK, 128), F32) for _ in range(8)]
        dkr = [jnp.zeros((2 * BLK, 128), F32) for _ in range(2)]
        dvr = [jnp.zeros((2 * BLK, 128), F32) for _ in range(2)]
        lane1 = lax.broadcasted_iota(jnp.int32, (1, 128), 1)
        dsink = jnp.zeros((1, 128), F32)
        for g in range(N_KV):
            r, half = g // 2, g % 2
            qs = _stack_heads(qr, g, halves).astype(BF)
            dos = _stack_heads(doc, g, halves)
            os_ = _stack_heads(oc, g, halves)
            s = lax.dot_general(qs, kr[r], NT, preferred_element_type=F32)
            p, ps = _softmax_sink(s, valid, sink_ref, g)
            dosb = dos.astype(BF)
            dp = lax.dot_general(dosb, vr[r], NT, preferred_element_type=F32)
            delta = jnp.sum(dos * os_, axis=-1, keepdims=True)
            ds = (p * (dp - delta) * (HEAD_DIM ** -0.5)).astype(BF)
            dsk = -ps * delta
            for hh in range(4):
                val = jnp.sum(dsk[hh * BLK:(hh + 1) * BLK], axis=0, keepdims=True)
                dsink = dsink + jnp.where(lane1 == 4 * g + hh, val, 0.0)
            dvr[r] = dvr[r] + lax.dot_general(p.astype(BF), dosb, TN, preferred_element_type=F32)
            dkr[r] = dkr[r] + lax.dot_general(ds, qs, TN, preferred_element_type=F32)
            dq = jnp.dot(ds, kr[r], preferred_element_type=F32)
            for hh in range(4):
                h = 4 * g + hh
                dqh = jnp.where(halves[half], dq[hh * BLK:(hh + 1) * BLK], 0.0)
                if h % 2 != half:
                    dqh = pltpu.roll(dqh, HEAD_DIM, 1)
                dqs[h // 2] = dqs[h // 2] + dqh
        dsink_ref[...] += dsink
        dq_ref[...] = jnp.concatenate([_rope(t, cc, sc, lt32, inverse=True) for t in dqs], axis=1).astype(BF)
        dkp_ref[...] = jnp.concatenate([_rope(t[:BLK], cp, sp, lt32, inverse=True) for t in dkr], axis=1)
        dkc_ref[...] = jnp.concatenate([_rope(t[BLK:], cc, sc, lt32, inverse=True) for t in dkr], axis=1)
        dvp_ref[...] = jnp.concatenate([t[:BLK] for t in dvr], axis=1)
        dvc_ref[...] = jnp.concatenate([t[BLK:] for t in dvr], axis=1)

    kv = pl.BlockSpec((BLK, 256), _row)
    return pl.pallas_call(
        body, name=name, grid=(nb,),
        in_specs=_attn_specs(nb) + [pl.BlockSpec((BLK, D), _row), pl.BlockSpec((BLK, D), _row)],
        out_specs=[pl.BlockSpec((BLK, D), _row), kv, kv, kv, kv, pl.BlockSpec((1, 128), _const2)],
        out_shape=[jax.ShapeDtypeStruct((T, D), BF)] + [jax.ShapeDtypeStruct((T, 256), F32)] * 4
        + [jax.ShapeDtypeStruct((1, 128), F32)],
        compiler_params=_cp(("arbitrary",)),
    )(proj, proj, proj, proj, proj, cos, sin, cos, sin, sinks, o, do)


def _dkv_combine(dkc, dkp, dvc, dvp, *, name):
    T = dkc.shape[0]
    nb = T // BLK

    def body(dkc_ref, dkp_ref, dvc_ref, dvp_ref, o_ref):
        last = (pl.program_id(0) == nb - 1)
        keep = jnp.where(last, 0.0, 1.0)
        o_ref[:, 0:256] = (dkc_ref[...] + keep * dkp_ref[...]).astype(BF)
        o_ref[:, 256:512] = (dvc_ref[...] + keep * dvp_ref[...]).astype(BF)

    cur = pl.BlockSpec((BLK, 256), _row)
    nxt = pl.BlockSpec((BLK, 256), lambda n: (jnp.minimum(n + 1, nb - 1), 0))
    return pl.pallas_call(
        body, name=name, grid=(nb,),
        in_specs=[cur, nxt, cur, nxt],
        out_specs=pl.BlockSpec((BLK, 512), _row),
        out_shape=jax.ShapeDtypeStruct((T, 512), BF),
        compiler_params=_cp(("parallel",)),
    )(dkc, dkp, dvc, dvp)


HALO = 16


def _conv_shifts(cu, hprev, tm):
    row = lax.broadcasted_iota(jnp.int32, cu.shape, 0)
    h1 = hprev[HALO - 1:HALO, :]
    h2 = hprev[HALO - 2:HALO - 1, :]
    m1 = jnp.where(row == 0, h1, pltpu.roll(cu, 1, 0))
    m2 = jnp.where(row == 0, h2, jnp.where(row == 1, h1, pltpu.roll(cu, 2, 0)))
    return m1, m2


def _mixer_mid_fwd(proj, attn, wcp, wap, wout, convw, x, gt, *, tm, name):
    T = x.shape[0]
    tm = _tile(T, tm)
    hb = tm // HALO

    def body(bg_ref, cg_ref, u_ref, hcg_ref, hu_ref, zc0_ref, zc1_ref, za0_ref, za1_ref, at_ref,
             wcp_ref, wap_ref, wout_ref, cw_ref, x_ref, gt_ref,
             x2_ref, gc_ref, yc_ref, ya_ref, mg_ref, o_ref):
        first = jnp.where(pl.program_id(0) == 0, 0.0, 1.0)
        cu = cg_ref[...].astype(F32) * u_ref[...].astype(F32)
        hprev = first * (hcg_ref[...].astype(F32) * hu_ref[...].astype(F32))
        m1, m2 = _conv_shifts(cu, hprev, tm)
        cv = cw_ref[0:1, :] * m2 + cw_ref[1:2, :] * m1 + cw_ref[2:3, :] * cu
        gc = (bg_ref[...].astype(F32) * cv).astype(BF)
        gc_ref[...] = gc
        yc = jnp.dot(gc, wcp_ref[...], preferred_element_type=F32)
        ya = jnp.dot(at_ref[...], wap_ref[...], preferred_element_type=F32)
        yc_ref[...] = yc.astype(BF)
        ya_ref[...] = ya.astype(BF)
        zc = jnp.concatenate([zc0_ref[...], zc1_ref[...]], axis=1).astype(F32)
        za = jnp.concatenate([za0_ref[...], za1_ref[...]], axis=1).astype(F32)
        mg = (jax.nn.sigmoid(zc) * yc + jax.nn.sigmoid(za) * ya).astype(BF)
        mg_ref[...] = mg
        o = jnp.dot(mg, wout_ref[...], preferred_element_type=F32)
        o_ref[...] = o.astype(BF)
        x2_ref[...] = x_ref[...] + gt_ref[...] * o

    wspec = pl.BlockSpec((D, D), _const2)
    rowspec = pl.BlockSpec((tm, D), _row)
    return pl.pallas_call(
        body, name=name, grid=(T // tm,),
        in_specs=[_col(tm, O_BG), _col(tm, O_CG), _col(tm, O_U), _halo_prev(hb, O_CG), _halo_prev(hb, O_U),
                  _col(tm, O_ZC, 512), _col(tm, O_ZC + 512, 512), _col(tm, O_ZA, 512), _col(tm, O_ZA + 512, 512),
                  rowspec, wspec, wspec, wspec, pl.BlockSpec((8, D), _const2), rowspec, pl.BlockSpec((1, D), _const2)],
        out_specs=[rowspec] * 6,
        out_shape=[jax.ShapeDtypeStruct((T, D), F32)] + [jax.ShapeDtypeStruct((T, D), BF)] * 5,
        compiler_params=_cp(("parallel",)),
    )(proj, proj, proj, proj, proj, proj, proj, proj, proj, attn, wcp, wap, wout, convw, x, gt)


def _col(tm, c, w=D):
    assert c % w == 0
    return pl.BlockSpec((tm, w), lambda i: (i, c // w))


def _halo_prev(hb, c):
    return pl.BlockSpec((HALO, D), lambda i: (jnp.maximum(i * hb - 1, 0), c // D))


def _halo_next(hb, nblk, c=0):
    return pl.BlockSpec((HALO, D), lambda i: (jnp.minimum((i + 1) * hb, nblk - 1), c // D))


def _mixer_mid_bwd(dx2, gt, o, proj, yc, ya, wout, wcp, wap, *, tm, name):
    T = dx2.shape[0]
    tm = _tile(T, tm)

    def body(dx_ref, gt_ref, o_ref, zc0_ref, zc1_ref, za0_ref, za1_ref, yc_ref, ya_ref, wout_ref, wcp_ref, wap_ref,
             dout_ref, dyc_ref, dya_ref, dgc_ref, dat_ref, dz_ref, dgt_ref):
        @pl.when(pl.program_id(0) == 0)
        def _():
            dgt_ref[...] = jnp.zeros_like(dgt_ref)
        dxv = dx_ref[...]
        dgt_ref[...] += jnp.sum(dxv * o_ref[...].astype(F32), axis=0, keepdims=True)
        dout = (gt_ref[...] * dxv).astype(BF)
        dout_ref[...] = dout
        dmg = lax.dot_general(dout, wout_ref[...], NT, preferred_element_type=F32)
        sc = jax.nn.sigmoid(jnp.concatenate([zc0_ref[...], zc1_ref[...]], axis=1).astype(F32))
        sa = jax.nn.sigmoid(jnp.concatenate([za0_ref[...], za1_ref[...]], axis=1).astype(F32))
        dyc = (dmg * sc).astype(BF)
        dya = (dmg * sa).astype(BF)
        dyc_ref[...] = dyc
        dya_ref[...] = dya
        dz_ref[:, 0:D] = (dmg * yc_ref[...].astype(F32) * (sc * (1.0 - sc))).astype(BF)
        dz_ref[:, D:2 * D] = (dmg * ya_ref[...].astype(F32) * (sa * (1.0 - sa))).astype(BF)
        dgc_ref[...] = lax.dot_general(dyc, wcp_ref[...], NT, preferred_element_type=F32).astype(BF)
        dat_ref[...] = lax.dot_general(dya, wap_ref[...], NT, preferred_element_type=F32).astype(BF)

    wspec = pl.BlockSpec((D, D), _const2)
    rowspec = pl.BlockSpec((tm, D), _row)
    vec = pl.BlockSpec((1, D), _const2)
    return pl.pallas_call(
        body, name=name, grid=(T // tm,),
        in_specs=[rowspec, vec, rowspec,
                  _col(tm, O_ZC, 512), _col(tm, O_ZC + 512, 512), _col(tm, O_ZA, 512), _col(tm, O_ZA + 512, 512),
                  rowspec, rowspec, wspec, wspec, wspec],
        out_specs=[rowspec] * 5 + [pl.BlockSpec((tm, 2 * D), _row), vec],
        out_shape=[jax.ShapeDtypeStruct((T, D), BF)] * 5 + [jax.ShapeDtypeStruct((T, 2 * D), BF),
                                                            jax.ShapeDtypeStruct((1, D), F32)],
        compiler_params=_cp(("arbitrary",)),
    )(dx2, gt, o, proj, proj, proj, proj, yc, ya, wout, wcp, wap)


def _conv_bwd(dgc, proj, convw, *, tm, name):
    T = dgc.shape[0]
    tm = _tile(T, tm)
    hb = tm // HALO
    nblk = T // HALO
    nt = T // tm

    def body(dgc_ref, ndgc_ref, bg_ref, nbg_ref, cg_ref, u_ref, hcg_ref, hu_ref, cw_ref, dp_ref, dcw_ref):
        i = pl.program_id(0)

        @pl.when(i == 0)
        def _():
            dcw_ref[...] = jnp.zeros_like(dcw_ref)
        first = jnp.where(i == 0, 0.0, 1.0)
        last = jnp.where(i == nt - 1, 0.0, 1.0)
        cg = cg_ref[...].astype(F32)
        u = u_ref[...].astype(F32)
        bg = bg_ref[...].astype(F32)
        dg = dgc_ref[...].astype(F32)
        cu = cg * u
        hprev = first * (hcg_ref[...].astype(F32) * hu_ref[...].astype(F32))
        m1, m2 = _conv_shifts(cu, hprev, tm)
        w0, w1, w2 = cw_ref[0:1, :], cw_ref[1:2, :], cw_ref[2:3, :]
        cv = w0 * m2 + w1 * m1 + w2 * cu
        dcv = dg * bg
        nxt = last * (ndgc_ref[...].astype(F32) * nbg_ref[...].astype(F32))
        n0, n1 = nxt[0:1, :], nxt[1:2, :]
        row = lax.broadcasted_iota(jnp.int32, dcv.shape, 0)
        p1 = jnp.where(row == tm - 1, n0, pltpu.roll(dcv, tm - 1, 0))
        p2 = jnp.where(row == tm - 1, n1, jnp.where(row == tm - 2, n0, pltpu.roll(dcv, tm - 2, 0)))
        dcu = w2 * dcv + w1 * p1 + w0 * p2
        dp_ref[:, 0:D] = (dg * cv).astype(BF)
        dp_ref[:, D:2 * D] = (dcu * u).astype(BF)
        dp_ref[:, 2 * D:3 * D] = (dcu * cg).astype(BF)
        dcw_ref[0:1, :] += jnp.sum(dcv * m2, axis=0, keepdims=True)
        dcw_ref[1:2, :] += jnp.sum(dcv * m1, axis=0, keepdims=True)
        dcw_ref[2:3, :] += jnp.sum(dcv * cu, axis=0, keepdims=True)

    rowspec = pl.BlockSpec((tm, D), _row)
    cw = pl.BlockSpec((8, D), _const2)
    return pl.pallas_call(
        body, name=name, grid=(nt,),
        in_specs=[rowspec, _halo_next(hb, nblk), _col(tm, O_BG), _halo_next(hb, nblk, O_BG),
                  _col(tm, O_CG), _col(tm, O_U), _halo_prev(hb, O_CG), _halo_prev(hb, O_U), cw],
        out_specs=[pl.BlockSpec((tm, 3 * D), _row), cw],
        out_shape=[jax.ShapeDtypeStruct((T, 3 * D), BF), jax.ShapeDtypeStruct((8, D), F32)],
        compiler_params=_cp(("arbitrary",)),
    )(dgc, dgc, proj, proj, proj, proj, proj, proj, convw)


def _gsum(parts, *, tm, name):
    _, R, C = parts.shape
    tm = _tile(R, tm)

    def body(p_ref, o_ref):
        acc = p_ref[0].astype(F32)
        for s in range(1, N_DEV):
            acc = acc + p_ref[s].astype(F32)
        o_ref[...] = acc

    return pl.pallas_call(
        body, name=name, grid=(R // tm,),
        in_specs=[pl.BlockSpec((N_DEV, tm, C), lambda i: (0, i, 0))],
        out_specs=pl.BlockSpec((tm, C), _row),
        out_shape=jax.ShapeDtypeStruct((R, C), F32),
        compiler_params=_cp(("parallel",)),
    )(parts)


def _adam(w, g, m, v, *, tm, name):
    R, C = w.shape
    tm = _tile(R, tm)
    c1 = 1.0 - ADAM_B1
    c2 = 1.0 - ADAM_B2
    bc1 = 1.0 - ADAM_B1 ** ADAM_STEP
    bc2 = 1.0 - ADAM_B2 ** ADAM_STEP

    def body(w_ref, g_ref, m_ref, v_ref, d_ref, nm_ref, nv_ref):
        gv = g_ref[...]
        nm = ADAM_B1 * m_ref[...] + c1 * gv
        nv = ADAM_B2 * v_ref[...] + c2 * (gv * gv)
        nm_ref[...] = nm
        nv_ref[...] = nv
        d_ref[...] = -ADAM_LR * ((nm / bc1) / (jnp.sqrt(nv / bc2) + ADAM_EPS) + ADAM_WD * w_ref[...])

    spec = pl.BlockSpec((tm, C), _row)
    return pl.pallas_call(
        body, name=name, grid=(R // tm,),
        in_specs=[spec] * 4, out_specs=[spec] * 3,
        out_shape=[jax.ShapeDtypeStruct((R, C), F32)] * 3,
        compiler_params=_cp(("parallel",)),
    )(w, g, m, v)


def _mods_part(c_all, w_ada, b_ada, *, name):
    C = w_ada.shape[1]

    def body(c_ref, w_ref, b_ref, o_ref):
        cv = c_ref[...]
        ca = cv * jax.nn.sigmoid(cv)
        o_ref[...] = jnp.dot(ca, w_ref[...], preferred_element_type=F32,
                             precision=lax.Precision.HIGHEST) + b_ref[...]

    return pl.pallas_call(
        body, name=name,
        out_shape=jax.ShapeDtypeStruct((N_DEV, C), F32),
        compiler_params=_cp(),
    )(c_all, w_ada, b_ada)


def _wada_grad(c_all_t, gm, *, name):
    C = gm.shape[1]

    def body(c_ref, g_ref, o_ref):
        cv = c_ref[...]
        ca = cv * jax.nn.sigmoid(cv)
        acc = ca[:, 0:1] * g_ref[0:1, :]
        for b in range(1, N_DEV):
            acc = acc + ca[:, b:b + 1] * g_ref[b:b + 1, :]
        o_ref[...] = acc

    return pl.pallas_call(
        body, name=name,
        out_shape=jax.ShapeDtypeStruct((D, C), F32),
        compiler_params=_cp(),
    )(c_all_t, gm)


def _peer(x, y, c, d):
    px = lax.rem(x + ((d >> 2) & 1), 2)
    py = lax.rem(y + ((d >> 1) & 1), 2)
    pc = lax.rem(c + (d & 1), 2)
    return (px, py, pc), 4 * px + 2 * py + pc


def _exchange(xs, *, scatter, name):
    n = len(xs)
    nsem = n * (N_DEV - 1)

    def body(*refs):
        ins, outs = refs[:n], refs[n:2 * n]
        send_sems, recv_sems, local_sems = refs[2 * n:]
        x, y, c = lax.axis_index("x"), lax.axis_index("y"), lax.axis_index("c")
        me = 4 * x + 2 * y + c

        def src(t, idx):
            return ins[t].at[idx] if scatter else ins[t]

        local = [pltpu.make_async_copy(src(t, me), outs[t].at[me], local_sems.at[t]) for t in range(n)]
        for cp in local:
            cp.start()
        remote = []
        for t in range(n):
            for d in range(1, N_DEV):
                peer, pidx = _peer(x, y, c, d)
                k = t * (N_DEV - 1) + d - 1
                send = pltpu.make_async_remote_copy(src_ref=src(t, pidx), dst_ref=outs[t].at[me],
                                                    send_sem=send_sems.at[k], recv_sem=recv_sems.at[k],
                                                    device_id=peer, device_id_type=MESH)
                recv = pltpu.make_async_remote_copy(src_ref=src(t, pidx), dst_ref=outs[t].at[pidx],
                                                    send_sem=send_sems.at[k], recv_sem=recv_sems.at[k],
                                                    device_id=peer, device_id_type=MESH)
                send.start()
                remote.append((send, recv))
        for cp in local:
            cp.wait()
        for send, recv in remote:
            send.wait_send()
            recv.wait_recv()

    anyspec = pl.BlockSpec(memory_space=pl.ANY)
    out_shape = [jax.ShapeDtypeStruct(a.shape if scatter else (N_DEV,) + a.shape, a.dtype) for a in xs]
    return pl.pallas_call(
        body, name=name,
        in_specs=[anyspec] * n, out_specs=[anyspec] * n, out_shape=out_shape,
        scratch_shapes=[pltpu.SemaphoreType.DMA((nsem,)), pltpu.SemaphoreType.DMA((nsem,)),
                        pltpu.SemaphoreType.DMA((n,))],
    )(*xs)


def _sum8(parts, *, name):
    _, R, C = parts.shape

    def body(p_ref, o_ref):
        acc = p_ref[0]
        for s in range(1, N_DEV):
            acc = acc + p_ref[s]
        o_ref[...] = acc

    return pl.pallas_call(body, name=name, out_shape=jax.ShapeDtypeStruct((R, C), F32),
                          compiler_params=_cp())(parts)


HBM_SPEC = pl.BlockSpec(memory_space=pltpu.HBM)
SEM_SPEC = pl.BlockSpec(memory_space=pltpu.SEMAPHORE)
N_PEER = N_DEV - 1


def _split_copies(src_refs, land_refs, send_sems, recv_sems, scatter):
    x, y, c = lax.axis_index("x"), lax.axis_index("y"), lax.axis_index("c")
    me = 4 * x + 2 * y + c
    pairs = []
    for j, (src, land) in enumerate(zip(src_refs, land_refs)):
        for d in range(1, N_DEV):
            peer, pidx = _peer(x, y, c, d)
            k = j * N_PEER + d - 1
            s = src.at[pidx] if scatter else src
            send = pltpu.make_async_remote_copy(src_ref=s, dst_ref=land.at[me], send_sem=send_sems.at[k],
                                                recv_sem=recv_sems.at[k], device_id=peer, device_id_type=MESH)
            recv = pltpu.make_async_remote_copy(src_ref=s, dst_ref=land.at[pidx], send_sem=send_sems.at[k],
                                                recv_sem=recv_sems.at[k], device_id=peer, device_id_type=MESH)
            pairs.append((send, recv))
    return pairs


def _own_slot(block, me):
    land = lax.empty((N_DEV,) + block.shape, block.dtype)
    return lax.dynamic_update_slice(land, block[None], (me, 0, 0))


def _split_start(srcs, lands, groups, after, *, scatter, name):
    n, ng = len(srcs), len(groups)

    def body(*refs):
        src_refs, land_refs = refs[:n], refs[n:2 * n]
        sems = refs[2 * n + 1:2 * n + 1 + 2 * ng]
        token = refs[-1]
        for gi, g in enumerate(groups):
            pairs = _split_copies([src_refs[t] for t in g], [land_refs[t] for t in g], sems[2 * gi],
                                  sems[2 * gi + 1], scatter)
            for send, _ in pairs:
                send.start()
        token[...] = jnp.zeros_like(token)

    sem_shapes = []
    for g in groups:
        sem_shapes += [pltpu.SemaphoreType.DMA((len(g) * N_PEER,))] * 2
    thru = [pltpu.HBM(a.shape, a.dtype) for a in list(srcs) + list(lands)]
    outs = pl.pallas_call(
        body, name=name,
        out_shape=tuple(sem_shapes + thru + [jax.ShapeDtypeStruct((8, 128), F32)]),
        in_specs=[HBM_SPEC] * (2 * n) + [pl.BlockSpec(memory_space=pl.ANY)],
        out_specs=tuple([SEM_SPEC] * (2 * ng) + [HBM_SPEC] * (2 * n) + [pl.BlockSpec(memory_space=pltpu.VMEM)]),
        input_output_aliases={i: 2 * ng + i for i in range(2 * n)},
        compiler_params=pltpu.CompilerParams(has_side_effects=pltpu.SideEffectType.DATAFLOW_SIDE_EFFECTING),
    )(*[pltpu.with_memory_space_constraint(a, pltpu.HBM) for a in list(srcs) + list(lands)], after)
    sems = [(outs[2 * gi], outs[2 * gi + 1]) for gi in range(ng)]
    return sems, outs[2 * ng:2 * ng + n], outs[2 * ng + n:2 * ng + 2 * n], outs[-1][0, 0]


def _split_wait(srcs, lands, sems, after, *, scatter, name):
    m = len(srcs)

    def body(*refs):
        src_refs, land_refs = refs[:m], refs[m:2 * m]
        send_sems, recv_sems = refs[2 * m], refs[2 * m + 1]
        for send, recv in _split_copies(src_refs, land_refs, send_sems, recv_sems, scatter):
            send.wait_send()
            recv.wait_recv()

    outs = pl.pallas_call(
        body, name=name,
        out_shape=tuple(pltpu.HBM(a.shape, a.dtype) for a in list(srcs) + list(lands)),
        in_specs=[HBM_SPEC] * (2 * m) + [SEM_SPEC, SEM_SPEC, pl.BlockSpec(memory_space=pl.ANY)],
        out_specs=tuple([HBM_SPEC] * (2 * m)),
        input_output_aliases={i: i for i in range(2 * m)},
        compiler_params=pltpu.CompilerParams(has_side_effects=pltpu.SideEffectType.DATAFLOW_SIDE_EFFECTING),
    )(*srcs, *lands, sems[0], sems[1], after)
    return outs[m:]


TM_PROJ = 1024
TM_ROW = 512
TM_NN = 512
TK_TN = 512
TN_FFN = F // 2
TN_IN = NIN // 4


def _tn(a, b, name, tn):
    if a.ndim == 2:
        a = a[None]
    return _tn_matmul(a, b, tn=tn, tk=TK_TN, name=name)


def _local_step(x, tgt, mods, g1, gm, g2, gf, convw8, sinks, w_get, g_put):
    T = x.shape[0]
    sh1, sc1, gt1, sh2, sc2, gt2, sh3, sc3, gt3 = [mods[i:i + 1] for i in range(N_MOD)]
    cos, sin = _rope_tables(T)

    def behind(v, tok):
        return v if tok is None else v + tok

    w = dict(w_get("gu1", mods))
    h1, ab1 = _norm_proj(x, g1, sc1, sh1, w["gu1"], tm=TM_PROJ, tn=TN_FFN, name="ffn1_up")
    w.update(w_get("d1", ab1))
    x1, y1 = _ffn_down_fwd(ab1, w["d1"], x, gt1, tm=TM_ROW, name="ffn1_down")
    w.update(w_get("mix", x1))
    h2, proj = _norm_proj(x1, gm, sc2, sh2, w["win"], tm=TM_PROJ, tn=TN_IN, name="mix_in")
    attn = _attn_fwd(proj, cos, sin, sinks, name="attn_fwd")
    x2, gc, yc, ya, mg, o = _mixer_mid_fwd(proj, attn, w["cp"], w["ap"], w["out"], convw8, x1, gt2,
                                           tm=TM_ROW, name="mix_mid")
    w.update(w_get("ffn2", x2))
    h3, ab2 = _norm_proj(x2, g2, sc3, sh3, w["gu2"], tm=TM_PROJ, tn=TN_FFN, name="ffn2_up")
    x3, y2 = _ffn_down_fwd(ab2, w["d2"], x2, gt3, tm=TM_ROW, name="ffn2_down")
    dx3, lsum, dgf = _final_fwd_bwd(x3, tgt, gf, tm=TM_ROW, name="final")

    dy2, dab2, dgt3 = _ffn_down_bwd(dx3, y2, gt3, ab2, w["d2"], tm=TM_ROW, tn=TN_FFN, name="ffn2_down_bwd")
    g_d2 = _tn_matmul_swiglu(ab2, dy2, tn=TN_FFN, tk=TK_TN, name="ffn2_down_dw")
    dx2, dsh3, dsc3, dg2 = _nn_bwd_norm(dab2, w["gu2"], x2, g2, sc3, dx3, tm=TM_NN, tk=TN_FFN, name="ffn2_up_bwd")
    g_gu2 = _tn(dab2, h3, "ffn2_up_dw", TN_FFN)
    tok = g_put(dict(gu2=g_gu2, d2=g_d2))

    dout, dyc, dya, dgc, dat, dz, dgt2 = _mixer_mid_bwd(dx2, behind(gt2, tok), o, proj, yc, ya, w["out"], w["cp"],
                                                        w["ap"], tm=TM_ROW, name="mix_mid_bwd")
    g_out = _tn(mg, dout, "mix_out_dw", D)
    g_cp = _tn(gc, dyc, "mix_cp_dw", D)
    g_ap = _tn(attn, dya, "mix_ap_dw", D)
    dq, dkc, dkp, dvc, dvp, dsink = _attn_bwd(proj, cos, sin, sinks, attn, dat, name="attn_bwd")
    dkv = _dkv_combine(dkc, dkp, dvc, dvp, name="attn_dkv")
    dp1, dcw = _conv_bwd(dgc, proj, convw8, tm=TM_ROW, name="conv_bwd")
    dproj = jnp.concatenate([dp1, dq, dkv, dz], axis=1)
    g_in = _tn(dproj, h2, "mix_in_dw", TN_IN)
    tok = g_put(dict(win=g_in, cp=g_cp, ap=g_ap, out=g_out))
    dx1, dsh2, dsc2, dgm = _nn_bwd_norm(dproj[None], w["win"], x1, gm, behind(sc2, tok), dx2, tm=TM_NN, tk=TN_IN,
                                        name="mix_in_bwd")

    dy1, dab1, dgt1 = _ffn_down_bwd(dx1, y1, gt1, ab1, w["d1"], tm=TM_ROW, tn=TN_FFN, name="ffn1_down_bwd")
    g_gu1 = _tn(dab1, h1, "ffn1_up_dw", TN_FFN)
    g_put(dict(gu1=g_gu1))
    g_d1 = _tn_matmul_swiglu(ab1, dy1, tn=TN_FFN, tk=TK_TN, name="ffn1_down_dw")
    tok = g_put(dict(d1=g_d1))
    dx0, dsh1, dsc1, dg1 = _nn_bwd_norm(dab1, w["gu1"], x, g1, behind(sc1, tok), dx1, tm=TM_NN, tk=TN_FFN,
                                        name="ffn1_up_bwd")

    small = dict(mods=jnp.concatenate([dsh1, dsc1, dgt1, dsh2, dsc2, dgt2, dsh3, dsc3, dgt3], axis=0),
                 g1=dg1, gm=dgm, g2=dg2, gf=dgf, convw=dcw[0:3], sinks=dsink[:, 0:N_HEADS])
    return lsum, dx0, small


BIG = ("gu1", "d1", "win", "cp", "ap", "out", "gu2", "d2")
TRANSPOSED = ("gu1", "win", "gu2")
SMALL_ROWS = 24
R_MODS, R_G1, R_GM, R_G2, R_GF, R_CONV, R_SINK = 0, 9, 10, 11, 12, 13, 16


def _pad_to(a, rows, cols):
    return jnp.pad(a, ((0, rows - a.shape[0]), (0, cols - a.shape[1])))


def _pack_small(b_ada, g1, gm, g2, gf, conv, sinks):
    rows = [b_ada.reshape(N_MOD, D), g1.reshape(1, D), gm.reshape(1, D), g2.reshape(1, D), gf.reshape(1, D),
            _pad_to(conv.reshape(3, -1), 3, D), _pad_to(sinks.reshape(1, N_HEADS), 1, D)]
    return _pad_to(jnp.concatenate(rows, axis=0), SMALL_ROWS, D)


def _unpack_small(p, conv_cols):
    return dict(b_ada=p[R_MODS:R_MODS + N_MOD].reshape(1, N_MOD * D), g_ffn1=p[R_G1:R_G1 + 1],
                g_mix=p[R_GM:R_GM + 1], g_ffn2=p[R_G2:R_G2 + 1], g_final=p[R_GF],
                conv_w=p[R_CONV:R_CONV + 3, 0:conv_cols][None], sinks=p[R_SINK:R_SINK + 1, 0:N_HEADS])


def kernel(x, c, w_ada, b_ada, g_ffn1, w1_gu, w1_down, g_mix, w_in, conv_w, w_conv_proj, w_attn_proj, sinks, w_out, g_ffn2, w2_gu, w2_down, g_final, loss_target, m_w_ada, m_b_ada, m_g_ffn1, m_w1_gu, m_w1_down, m_g_mix, m_w_in, m_conv_w, m_w_conv_proj, m_w_attn_proj, m_sinks, m_w_out, m_g_ffn2, m_w2_gu, m_w2_down, m_g_final, v_w_ada, v_b_ada, v_g_ffn1, v_w1_gu, v_w1_down, v_g_mix, v_w_in, v_conv_w, v_w_conv_proj, v_w_attn_proj, v_sinks, v_w_out, v_g_ffn2, v_w2_gu, v_w2_down, v_g_final):
    me = 4 * lax.axis_index("x") + 2 * lax.axis_index("y") + lax.axis_index("c")
    ada_cols = w_ada.shape[2]
    conv_cols = conv_w.shape[2]

    native = dict(gu1=w1_gu[0], d1=w1_down[0], win=w_in[0], cp=w_conv_proj[0], ap=w_attn_proj[0], out=w_out[0],
                  gu2=w2_gu[0], d2=w2_down[0])
    shards = [(native[n].T if n in TRANSPOSED else native[n]).astype(BF) for n in BIG]

    c_all, conv_all = _exchange([c, _pad_to(conv_w[0], 8, conv_cols)], scatter=False, name="gather_cond")
    c_all = c_all.reshape(N_DEV, D)
    conv_full = conv_all[:, 0:3, :].transpose(1, 0, 2).reshape(3, D)

    b_cols = lax.dynamic_slice(b_ada, (0, me * ada_cols), (1, ada_cols))
    mods_cols = _mods_part(c_all, w_ada[0], b_cols, name="ada_mods")
    (mods_all,) = _exchange([mods_cols], scatter=False, name="gather_mods")
    mods = lax.dynamic_index_in_dim(mods_all, me, axis=1, keepdims=False).reshape(N_MOD, D)

    groups = dict(gu1=("gu1",), d1=("d1",), mix=("win", "cp", "ap", "out"), ffn2=("gu2", "d2"))
    group_idx = [[BIG.index(n) for n in names] for names in groups.values()]
    w_sems, w_srcs, w_lands, _ = _split_start(shards, [_own_slot(s, me) for s in shards], group_idx, mods_all,
                                              scatter=False, name="gather_weights_start")

    def w_get(group, after):
        idx = group_idx[list(groups).index(group)]
        landed = _split_wait([w_srcs[t] for t in idx], [w_lands[t] for t in idx], w_sems[list(groups).index(group)],
                             after, scatter=False, name="gather_weights_wait_" + group)
        return {BIG[t]: a.reshape(-1, D) for t, a in zip(idx, landed)}

    pending = []

    def g_put(gs):
        names = tuple(gs)
        srcs = [gs[n].reshape(N_DEV, -1, D) for n in names]
        lands = [_own_slot(lax.dynamic_index_in_dim(s, me, axis=0, keepdims=False), me) for s in srcs]
        sems, srcs, lands, tok = _split_start(srcs, lands, [list(range(len(names)))], srcs[0], scatter=True,
                                              name="scatter_grads_start_" + names[0])
        pending.append((names, sems[0], srcs, lands))
        return tok

    lsum, grad_x, small = _local_step(x[0], loss_target[0], mods, g_ffn1, g_mix, g_ffn2, g_final[None],
                                      _pad_to(conv_full, 8, D), sinks[0], w_get, g_put)
    loss = lax.psum((0.5 / D) * jnp.sum(lsum), ("x", "y", "c"))

    packed = _pack_small(small["mods"], small["g1"], small["gm"], small["g2"], small["gf"], small["convw"],
                         small["sinks"])
    (packed_all,) = _exchange([packed], scatter=False, name="gather_small")
    gsmall = _sum8(packed_all, name="sum_small")

    grads = {}
    after = gsmall
    for names, sems, srcs, lands in pending:
        parts = _split_wait(srcs, lands, sems, after, scatter=True, name="scatter_grads_wait_" + names[0])
        for n, p in zip(names, parts):
            g = _gsum(p, tm=128, name="gsum_" + n)
            grads[n] = g.T if n in TRANSPOSED else g
        after = g

    gm_cols = lax.dynamic_slice(packed_all[:, R_MODS:R_MODS + N_MOD, :].reshape(N_DEV, N_MOD * D),
                                (0, me * ada_cols), (N_DEV, ada_cols))
    grads["ada"] = _wada_grad(c_all.T, gm_cols, name="ada_dw")
    conv_g = lax.dynamic_slice(gsmall[R_CONV:R_CONV + 3], (0, me * conv_cols), (3, conv_cols))
    gsmall_own = gsmall.at[R_CONV:R_CONV + 3].set(_pad_to(conv_g, 3, D))
    g_small = _unpack_small(gsmall_own, conv_cols)

    w_of = dict(ada=w_ada, gu1=w1_gu, d1=w1_down, win=w_in, cp=w_conv_proj, ap=w_attn_proj, out=w_out, gu2=w2_gu,
                d2=w2_down)
    m_of = dict(ada=m_w_ada, gu1=m_w1_gu, d1=m_w1_down, win=m_w_in, cp=m_w_conv_proj, ap=m_w_attn_proj, out=m_w_out,
                gu2=m_w2_gu, d2=m_w2_down)
    v_of = dict(ada=v_w_ada, gu1=v_w1_gu, d1=v_w1_down, win=v_w_in, cp=v_w_conv_proj, ap=v_w_attn_proj, out=v_w_out,
                gu2=v_w2_gu, d2=v_w2_down)
    upd = {n: _adam(w_of[n][0], grads[n], m_of[n][0], v_of[n][0], tm=128, name="adam_" + n) for n in w_of}
    small_upd = _adam(_pack_small(b_ada, g_ffn1, g_mix, g_ffn2, g_final, conv_w, sinks), gsmall_own,
                      _pack_small(m_b_ada, m_g_ffn1, m_g_mix, m_g_ffn2, m_g_final, m_conv_w, m_sinks),
                      _pack_small(v_b_ada, v_g_ffn1, v_g_mix, v_g_ffn2, v_g_final, v_conv_w, v_sinks),
                      tm=SMALL_ROWS, name="adam_small")
    small_out = [g_small] + [_unpack_small(p, conv_cols) for p in small_upd]

    big_name = dict(w_ada="ada", w1_gu="gu1", w1_down="d1", w_in="win", w_conv_proj="cp", w_attn_proj="ap",
                    w_out="out", w2_gu="gu2", w2_down="d2")
    order = ("w_ada", "b_ada", "g_ffn1", "w1_gu", "w1_down", "g_mix", "w_in", "conv_w", "w_conv_proj", "w_attn_proj",
             "sinks", "w_out", "g_ffn2", "w2_gu", "w2_down", "g_final")
    outs = [loss, grad_x[None]]
    for kind in range(4):
        for n in order:
            if n in big_name:
                t = grads[big_name[n]] if kind == 0 else upd[big_name[n]][kind - 1]
                outs.append(t[None])
            else:
                outs.append(small_out[kind][n])
    return tuple(outs)
```

```python
import functools

import jax
import jax.numpy as jnp
from jax import lax
from jax.experimental import pallas as pl
from jax.experimental.pallas import tpu as pltpu

D = 1024
F = 2816
NIN = 6656
N_HEADS = 16
N_KV = 4
HEAD_DIM = 64
BLK = 128
N_MOD = 9
N_DEV = 8
EPS = 1e-6
NEG_INF = -1e30
ROPE_THETA = 10000.0
O_BG, O_CG, O_U, O_Q, O_K, O_V, O_ZC, O_ZA = 0, 1024, 2048, 3072, 4096, 4352, 4608, 5632

ADAM_LR = 0.001
ADAM_B1 = 0.9
ADAM_B2 = 0.999
ADAM_EPS = 1e-08
ADAM_WD = 0.01
ADAM_STEP = 10

BF = jnp.bfloat16
F32 = jnp.float32
VMEM_LIMIT = 56 * 1024 * 1024
MESH = pl.DeviceIdType.MESH

NT = (((1,), (1,)), ((), ()))
TN = (((0,), (0,)), ((), ()))


def _cp(sem=None):
    return pltpu.CompilerParams(dimension_semantics=sem, vmem_limit_bytes=VMEM_LIMIT)


def _tile(n, pref):
    if n <= pref:
        return n
    for t in range(pref - pref % 16, 15, -16):
        if n % t == 0:
            return t
    raise ValueError((n, pref))


def _sigmoid(v):
    return 0.5 * jnp.tanh(0.5 * v) + 0.5


def _row(i):
    return (i, 0)


def _const2(*_):
    return (0, 0)


def _norm_proj(x, g, sc, sh, wt, *, tm, tn, name):
    T, N = x.shape[0], wt.shape[0]
    tm, tn = _tile(T, tm), _tile(N, tn)

    def body(x_ref, g_ref, sc_ref, sh_ref, w_ref, h_ref, o_ref, hs):
        @pl.when(pl.program_id(1) == 0)
        def _():
            xv = x_ref[...]
            r = lax.rsqrt(jnp.mean(xv * xv, axis=-1, keepdims=True) + EPS)
            hb = ((xv * r) * g_ref[...] * (1.0 + sc_ref[...]) + sh_ref[...]).astype(BF)
            hs[...] = hb
            h_ref[...] = hb
        o_ref[...] = lax.dot_general(hs[...], w_ref[...], NT, preferred_element_type=F32).astype(BF)

    vec = pl.BlockSpec((1, D), _const2)
    return pl.pallas_call(
        body, name=name, grid=(T // tm, N // tn),
        in_specs=[pl.BlockSpec((tm, D), lambda i, j: (i, 0)), vec, vec, vec,
                  pl.BlockSpec((tn, D), lambda i, j: (j, 0))],
        out_specs=[pl.BlockSpec((tm, D), lambda i, j: (i, 0)), pl.BlockSpec((tm, tn), lambda i, j: (i, j))],
        out_shape=[jax.ShapeDtypeStruct((T, D), BF), jax.ShapeDtypeStruct((T, N), BF)],
        scratch_shapes=[pltpu.VMEM((tm, D), BF)],
        compiler_params=_cp(("parallel", "arbitrary")),
    )(x, g, sc, sh, wt)


def _ffn_down_fwd(ab, wd, x, gt, *, tm, name):
    T = x.shape[0]
    tm = _tile(T, tm)

    def body(a_ref, b_ref, wd_ref, x_ref, gt_ref, xo_ref, y_ref):
        a = a_ref[...].astype(F32)
        act = (a * _sigmoid(a) * b_ref[...].astype(F32)).astype(BF)
        y = jnp.dot(act, wd_ref[...], preferred_element_type=F32)
        y_ref[...] = y.astype(BF)
        xo_ref[...] = x_ref[...] + (0.5 * gt_ref[...]) * y

    return pl.pallas_call(
        body, name=name, grid=(T // tm,),
        in_specs=[pl.BlockSpec((tm, F), lambda i: (i, 0)), pl.BlockSpec((tm, F), lambda i: (i, 1)),
                  pl.BlockSpec((F, D), _const2), pl.BlockSpec((tm, D), _row), pl.BlockSpec((1, D), _const2)],
        out_specs=[pl.BlockSpec((tm, D), _row), pl.BlockSpec((tm, D), _row)],
        out_shape=[jax.ShapeDtypeStruct((T, D), F32), jax.ShapeDtypeStruct((T, D), BF)],
        compiler_params=_cp(("parallel",)),
    )(ab, ab, wd, x, gt)


def _final_fwd_bwd(x, tgt, g, *, tm, name):
    T = x.shape[0]
    tm = _tile(T, tm)

    def body(x_ref, t_ref, g_ref, dx_ref, ls_ref, dg_ref):
        @pl.when(pl.program_id(0) == 0)
        def _():
            ls_ref[...] = jnp.zeros_like(ls_ref)
            dg_ref[...] = jnp.zeros_like(dg_ref)
        xv = x_ref[...]
        gv = g_ref[...]
        r = lax.rsqrt(jnp.mean(xv * xv, axis=-1, keepdims=True) + EPS)
        xh = xv * r
        e = xh * gv - t_ref[...]
        ls_ref[...] += jnp.sum(e * e, axis=0, keepdims=True)
        dy = e * (1.0 / D)
        dg_ref[...] += jnp.sum(dy * xh, axis=0, keepdims=True)
        dxh = dy * gv
        dx_ref[...] = r * (dxh - xh * jnp.mean(dxh * xh, axis=-1, keepdims=True))

    vec = pl.BlockSpec((1, D), _const2)
    return pl.pallas_call(
        body, name=name, grid=(T // tm,),
        in_specs=[pl.BlockSpec((tm, D), _row), pl.BlockSpec((tm, D), _row), vec],
        out_specs=[pl.BlockSpec((tm, D), _row), vec, vec],
        out_shape=[jax.ShapeDtypeStruct((T, D), F32), jax.ShapeDtypeStruct((1, D), F32),
                   jax.ShapeDtypeStruct((1, D), F32)],
        compiler_params=_cp(("arbitrary",)),
    )(x, tgt, g)


def _ffn_down_bwd(dxo, y, gt, ab, wd, *, tm, tn, name):
    T = dxo.shape[0]
    tm, tn = _tile(T, tm), _tile(F, tn)
    nj = F // tn

    def body(dxo_ref, y_ref, gt_ref, a_ref, b_ref, wd_ref, dy_ref, dab_ref, dgt_ref, dys):
        i, j = pl.program_id(0), pl.program_id(1)

        @pl.when(jnp.logical_and(i == 0, j == 0))
        def _():
            dgt_ref[...] = jnp.zeros_like(dgt_ref)

        @pl.when(j == 0)
        def _():
            dxv = dxo_ref[...]
            dgt_ref[...] += 0.5 * jnp.sum(dxv * y_ref[...].astype(F32), axis=0, keepdims=True)
            dyb = ((0.5 * gt_ref[...]) * dxv).astype(BF)
            dys[...] = dyb
            dy_ref[...] = dyb

        dact = lax.dot_general(dys[...], wd_ref[...], NT, preferred_element_type=F32)
        a = a_ref[...].astype(F32)
        b = b_ref[...].astype(F32)
        s = _sigmoid(a)
        silu = a * s
        dab_ref[0] = (dact * b * (s * (1.0 + a * (1.0 - s)))).astype(BF)
        dab_ref[1] = (dact * silu).astype(BF)

    vec = pl.BlockSpec((1, D), _const2)
    return pl.pallas_call(
        body, name=name, grid=(T // tm, nj),
        in_specs=[pl.BlockSpec((tm, D), lambda i, j: (i, 0)), pl.BlockSpec((tm, D), lambda i, j: (i, 0)), vec,
                  pl.BlockSpec((tm, tn), lambda i, j: (i, j)), pl.BlockSpec((tm, tn), lambda i, j: (i, j + nj)),
                  pl.BlockSpec((tn, D), lambda i, j: (j, 0))],
        out_specs=[pl.BlockSpec((tm, D), lambda i, j: (i, 0)), pl.BlockSpec((2, tm, tn), lambda i, j: (0, i, j)), vec],
        out_shape=[jax.ShapeDtypeStruct((T, D), BF), jax.ShapeDtypeStruct((2, T, F), BF),
                   jax.ShapeDtypeStruct((1, D), F32)],
        scratch_shapes=[pltpu.VMEM((tm, D), BF)],
        compiler_params=_cp(("arbitrary", "arbitrary")),
    )(dxo, y, gt, ab, ab, wd)


def _tn_matmul(a, b, *, tn, tk, name):
    S, T, Ns = a.shape
    tn, tk = _tile(Ns, tn), _tile(T, tk)
    nk, njs = T // tk, Ns // tn

    def body(a_ref, b_ref, o_ref, acc):
        k = pl.program_id(1)

        @pl.when(k == 0)
        def _():
            acc[...] = jnp.zeros_like(acc)
        acc[...] += lax.dot_general(a_ref[0], b_ref[...], TN, preferred_element_type=F32)

        @pl.when(k == nk - 1)
        def _():
            o_ref[...] = acc[...].astype(BF)

    return pl.pallas_call(
        body, name=name, grid=(S * njs, nk),
        in_specs=[pl.BlockSpec((1, tk, tn), lambda j, k: (j // njs, k, j % njs)),
                  pl.BlockSpec((tk, D), lambda j, k: (k, 0))],
        out_specs=pl.BlockSpec((tn, D), lambda j, k: (j, 0)),
        out_shape=jax.ShapeDtypeStruct((S * Ns, D), BF),
        scratch_shapes=[pltpu.VMEM((tn, D), F32)],
        compiler_params=_cp(("parallel", "arbitrary")),
    )(a, b)


def _tn_matmul_swiglu(ab, b, *, tn, tk, name):
    T = ab.shape[0]
    tn, tk = _tile(F, tn), _tile(T, tk)
    nk, nj = T // tk, F // tn

    def body(a_ref, g_ref, b_ref, o_ref, acc):
        k = pl.program_id(1)

        @pl.when(k == 0)
        def _():
            acc[...] = jnp.zeros_like(acc)
        a = a_ref[...].astype(F32)
        act = (a * _sigmoid(a) * g_ref[...].astype(F32)).astype(BF)
        acc[...] += lax.dot_general(act, b_ref[...], TN, preferred_element_type=F32)

        @pl.when(k == nk - 1)
        def _():
            o_ref[...] = acc[...].astype(BF)

    return pl.pallas_call(
        body, name=name, grid=(nj, nk),
        in_specs=[pl.BlockSpec((tk, tn), lambda j, k: (k, j)), pl.BlockSpec((tk, tn), lambda j, k: (k, j + nj)),
                  pl.BlockSpec((tk, D), lambda j, k: (k, 0))],
        out_specs=pl.BlockSpec((tn, D), lambda j, k: (j, 0)),
        out_shape=jax.ShapeDtypeStruct((F, D), BF),
        scratch_shapes=[pltpu.VMEM((tn, D), F32)],
        compiler_params=_cp(("parallel", "arbitrary")),
    )(ab, ab, b)


def _nn_bwd_norm(da, w, x, g, sc, dxo, *, tm, tk, name):
    S, T, Ks = da.shape
    tm, tk = _tile(T, tm), _tile(Ks, tk)
    nks = Ks // tk
    nk = S * nks

    def body(da_ref, w_ref, x_ref, g_ref, sc_ref, dxo_ref, dx_ref, dsh_ref, dsc_ref, dg_ref, acc):
        i, k = pl.program_id(0), pl.program_id(1)

        @pl.when(jnp.logical_and(i == 0, k == 0))
        def _():
            dsh_ref[...] = jnp.zeros_like(dsh_ref)
            dsc_ref[...] = jnp.zeros_like(dsc_ref)
            dg_ref[...] = jnp.zeros_like(dg_ref)

        @pl.when(k == 0)
        def _():
            acc[...] = jnp.zeros_like(acc)
        acc[...] += jnp.dot(da_ref[0], w_ref[...], preferred_element_type=F32)

        @pl.when(k == nk - 1)
        def _():
            u = acc[...]
            xv = x_ref[...]
            gv = g_ref[...]
            sc1 = 1.0 + sc_ref[...]
            r = lax.rsqrt(jnp.mean(xv * xv, axis=-1, keepdims=True) + EPS)
            xh = xv * r
            dsh_ref[...] += jnp.sum(u, axis=0, keepdims=True)
            dsc_ref[...] += jnp.sum(u * (xh * gv), axis=0, keepdims=True)
            us = u * sc1
            dg_ref[...] += jnp.sum(us * xh, axis=0, keepdims=True)
            dxh = us * gv
            dx_ref[...] = dxo_ref[...] + r * (dxh - xh * jnp.mean(dxh * xh, axis=-1, keepdims=True))

    vec = pl.BlockSpec((1, D), _const2)
    return pl.pallas_call(
        body, name=name, grid=(T // tm, nk),
        in_specs=[pl.BlockSpec((1, tm, tk), lambda i, k: (k // nks, i, k % nks)),
                  pl.BlockSpec((tk, D), lambda i, k: (k, 0)),
                  pl.BlockSpec((tm, D), lambda i, k: (i, 0)), vec, vec,
                  pl.BlockSpec((tm, D), lambda i, k: (i, 0))],
        out_specs=[pl.BlockSpec((tm, D), lambda i, k: (i, 0)), vec, vec, vec],
        out_shape=[jax.ShapeDtypeStruct((T, D), F32)] + [jax.ShapeDtypeStruct((1, D), F32)] * 3,
        scratch_shapes=[pltpu.VMEM((tm, D), F32)],
        compiler_params=_cp(("arbitrary", "arbitrary")),
    )(da, w, x, g, sc, dxo)


def _rope(t, cos, sin_signed, lt32, inverse=False):
    sel = jnp.where(lt32, pltpu.roll(t, 96, 1), pltpu.roll(t, 32, 1))
    return t * cos - sel * sin_signed if inverse else t * cos + sel * sin_signed


def _rope_tables(T):
    inv = 1.0 / (ROPE_THETA ** (jnp.arange(0, HEAD_DIM, 2, dtype=F32) / HEAD_DIM))
    ang = jnp.arange(T, dtype=F32)[:, None] * inv[None, :]
    cos, sin = jnp.cos(ang), jnp.sin(ang)
    cos128 = jnp.tile(cos, (1, 4))
    sin128 = jnp.tile(jnp.concatenate([-sin, sin], axis=1), (1, 2))
    return cos128, sin128


def _attn_specs(nb):
    qspec = pl.BlockSpec((BLK, D), lambda n: (n, O_Q // D))
    kc = pl.BlockSpec((BLK, 256), lambda n: (n, O_K // 256))
    kp = pl.BlockSpec((BLK, 256), lambda n: (jnp.maximum(n - 1, 0), O_K // 256))
    vc = pl.BlockSpec((BLK, 256), lambda n: (n, O_V // 256))
    vp = pl.BlockSpec((BLK, 256), lambda n: (jnp.maximum(n - 1, 0), O_V // 256))
    tc = pl.BlockSpec((BLK, 128), lambda n: (n, 0))
    tp = pl.BlockSpec((BLK, 128), lambda n: (jnp.maximum(n - 1, 0), 0))
    return [qspec, kc, kp, vc, vp, tc, tc, tp, tp, pl.BlockSpec(memory_space=pltpu.SMEM)]


def _attn_common(q_ref, kc_ref, kp_ref, vc_ref, vp_ref, cc_ref, sc_ref, cp_ref, sp_ref):
    n = pl.program_id(0)
    lane = lax.broadcasted_iota(jnp.int32, (BLK, 128), 1)
    lt32 = (lane % HEAD_DIM) < (HEAD_DIM // 2)
    cc, sc, cp, sp = cc_ref[...], sc_ref[...], cp_ref[...], sp_ref[...]
    kr, vr = [], []
    for r in range(2):
        cols = slice(r * 128, (r + 1) * 128)
        kcur = _rope(kc_ref[:, cols].astype(F32), cc, sc, lt32)
        kprev = _rope(kp_ref[:, cols].astype(F32), cp, sp, lt32)
        kr.append(jnp.concatenate([kprev, kcur], axis=0).astype(BF))
        vr.append(jnp.concatenate([vp_ref[:, cols], vc_ref[:, cols]], axis=0))
    qr = [_rope(q_ref[:, p * 128:(p + 1) * 128].astype(F32), cc, sc, lt32) for p in range(8)]
    qi = lax.broadcasted_iota(jnp.int32, (4 * BLK, 2 * BLK), 0) % BLK
    kj = lax.broadcasted_iota(jnp.int32, (4 * BLK, 2 * BLK), 1)
    valid = (kj > qi) & (kj <= qi + BLK) & ((kj >= BLK) | (n > 0))
    halves = [lane < HEAD_DIM, lane >= HEAD_DIM]
    return qr, kr, vr, valid, halves, lt32, (cc, sc, cp, sp)


def _stack_heads(chunks, g, halves):
    half = g % 2
    parts = []
    for hh in range(4):
        h = 4 * g + hh
        t = chunks[h // 2]
        if h % 2 != half:
            t = pltpu.roll(t, HEAD_DIM, 1)
        parts.append(jnp.where(halves[half], t, 0.0))
    return jnp.concatenate(parts, axis=0)


def _softmax_sink(s, valid, sink_ref, g):
    s = jnp.where(valid, s * (HEAD_DIM ** -0.5), NEG_INF)
    sink = jnp.concatenate([jnp.full((BLK, 1), sink_ref[4 * g + hh], F32) for hh in range(4)], axis=0)
    m = jnp.maximum(jnp.max(s, axis=-1, keepdims=True), sink)
    p = jnp.exp(s - m)
    ps = jnp.exp(sink - m)
    inv = 1.0 / (jnp.sum(p, axis=-1, keepdims=True) + ps)
    return p * inv, ps * inv


def _attn_fwd(proj, cos, sin, sinks, *, name):
    T = proj.shape[0]
    nb = T // BLK

    def body(q_ref, kc_ref, kp_ref, vc_ref, vp_ref, cc_ref, sc_ref, cp_ref, sp_ref, sink_ref, o_ref):
        qr, kr, vr, valid, halves, _, _ = _attn_common(q_ref, kc_ref, kp_ref, vc_ref, vp_ref,
                                                        cc_ref, sc_ref, cp_ref, sp_ref)
        outs = [jnp.zeros((BLK, 128), F32) for _ in range(8)]
        for g in range(N_KV):
            r, half = g // 2, g % 2
            qs = _stack_heads(qr, g, halves).astype(BF)
            s = lax.dot_general(qs, kr[r], NT, preferred_element_type=F32)
            p, _ = _softmax_sink(s, valid, sink_ref, g)
            o = jnp.dot(p.astype(BF), vr[r], preferred_element_type=F32)
            for hh in range(4):
                h = 4 * g + hh
                oh = jnp.where(halves[half], o[hh * BLK:(hh + 1) * BLK], 0.0)
                if h % 2 != half:
                    oh = pltpu.roll(oh, HEAD_DIM, 1)
                outs[h // 2] = outs[h // 2] + oh
        o_ref[...] = jnp.concatenate(outs, axis=1).astype(BF)

    return pl.pallas_call(
        body, name=name, grid=(nb,),
        in_specs=_attn_specs(nb),
        out_specs=pl.BlockSpec((BLK, D), _row),
        out_shape=jax.ShapeDtypeStruct((T, D), BF),
        compiler_params=_cp(("parallel",)),
    )(proj, proj, proj, proj, proj, cos, sin, cos, sin, sinks)


def _attn_bwd(proj, cos, sin, sinks, o, do, *, name):
    T = proj.shape[0]
    nb = T // BLK

    def body(q_ref, kc_ref, kp_ref, vc_ref, vp_ref, cc_ref, sc_ref, cp_ref, sp_ref, sink_ref, o_ref, do_ref,
             dq_ref, dkc_ref, dkp_ref, dvc_ref, dvp_ref, dsink_ref):
        @pl.when(pl.program_id(0) == 0)
        def _():
            dsink_ref[...] = jnp.zeros_like(dsink_ref)
        qr, kr, vr, valid, halves, lt32, (cc, sc, cp, sp) = _attn_common(
            q_ref, kc_ref, kp_ref, vc_ref, vp_ref, cc_ref, sc_ref, cp_ref, sp_ref)
        oc = [o_ref[:, p * 128:(p + 1) * 128].astype(F32) for p in range(8)]
        doc = [do_ref[:, p * 128:(p + 1) * 128].astype(F32) for p in range(8)]
        dqs = [jnp.zeros((BLK, 128), F32) for _ in range(8)]
        dkr = [jnp.zeros((2 * BLK, 128), F32) for _ in range(2)]
        dvr = [jnp.zeros((2 * BLK, 128), F32) for _ in range(2)]
        lane1 = lax.broadcasted_iota(jnp.int32, (1, 128), 1)
        dsink = jnp.zeros((1, 128), F32)
        for g in range(N_KV):
            r, half = g // 2, g % 2
            qs = _stack_heads(qr, g, halves).astype(BF)
            dos = _stack_heads(doc, g, halves)
            os_ = _stack_heads(oc, g, halves)
            s = lax.dot_general(qs, kr[r], NT, preferred_element_type=F32)
            p, ps = _softmax_sink(s, valid, sink_ref, g)
            dosb = dos.astype(BF)
            dp = lax.dot_general(dosb, vr[r], NT, preferred_element_type=F32)
            delta = jnp.sum(dos * os_, axis=-1, keepdims=True)
            ds = (p * (dp - delta) * (HEAD_DIM ** -0.5)).astype(BF)
            dsk = -ps * delta
            for hh in range(4):
                val = jnp.sum(dsk[hh * BLK:(hh + 1) * BLK], axis=0, keepdims=True)
                dsink = dsink + jnp.where(lane1 == 4 * g + hh, val, 0.0)
            dvr[r] = dvr[r] + lax.dot_general(p.astype(BF), dosb, TN, preferred_element_type=F32)
            dkr[r] = dkr[r] + lax.dot_general(ds, qs, TN, preferred_element_type=F32)
            dq = jnp.dot(ds, kr[r], preferred_element_type=F32)
            for hh in range(4):
                h = 4 * g + hh
                dqh = jnp.where(halves[half], dq[hh * BLK:(hh + 1) * BLK], 0.0)
                if h % 2 != half:
                    dqh = pltpu.roll(dqh, HEAD_DIM, 1)
                dqs[h // 2] = dqs[h // 2] + dqh
        dsink_ref[...] += dsink
        dq_ref[...] = jnp.concatenate([_rope(t, cc, sc, lt32, inverse=True) for t in dqs], axis=1).astype(BF)
        dkp_ref[...] = jnp.concatenate([_rope(t[:BLK], cp, sp, lt32, inverse=True) for t in dkr], axis=1)
        dkc_ref[...] = jnp.concatenate([_rope(t[BLK:], cc, sc, lt32, inverse=True) for t in dkr], axis=1)
        dvp_ref[...] = jnp.concatenate([t[:BLK] for t in dvr], axis=1)
        dvc_ref[...] = jnp.concatenate([t[BLK:] for t in dvr], axis=1)

    kv = pl.BlockSpec((BLK, 256), _row)
    return pl.pallas_call(
        body, name=name, grid=(nb,),
        in_specs=_attn_specs(nb) + [pl.BlockSpec((BLK, D), _row), pl.BlockSpec((BLK, D), _row)],
        out_specs=[pl.BlockSpec((BLK, D), _row), kv, kv, kv, kv, pl.BlockSpec((1, 128), _const2)],
        out_shape=[jax.ShapeDtypeStruct((T, D), BF)] + [jax.ShapeDtypeStruct((T, 256), F32)] * 4
        + [jax.ShapeDtypeStruct((1, 128), F32)],
        compiler_params=_cp(("arbitrary",)),
    )(proj, proj, proj, proj, proj, cos, sin, cos, sin, sinks, o, do)


def _dkv_combine(dkc, dkp, dvc, dvp, *, name):
    T = dkc.shape[0]
    nb = T // BLK

    def body(dkc_ref, dkp_ref, dvc_ref, dvp_ref, o_ref):
        last = (pl.program_id(0) == nb - 1)
        keep = jnp.where(last, 0.0, 1.0)
        o_ref[:, 0:256] = (dkc_ref[...] + keep * dkp_ref[...]).astype(BF)
        o_ref[:, 256:512] = (dvc_ref[...] + keep * dvp_ref[...]).astype(BF)

    cur = pl.BlockSpec((BLK, 256), _row)
    nxt = pl.BlockSpec((BLK, 256), lambda n: (jnp.minimum(n + 1, nb - 1), 0))
    return pl.pallas_call(
        body, name=name, grid=(nb,),
        in_specs=[cur, nxt, cur, nxt],
        out_specs=pl.BlockSpec((BLK, 512), _row),
        out_shape=jax.ShapeDtypeStruct((T, 512), BF),
        compiler_params=_cp(("parallel",)),
    )(dkc, dkp, dvc, dvp)


HALO = 16


def _conv_shifts(cu, hprev, tm):
    row = lax.broadcasted_iota(jnp.int32, cu.shape, 0)
    h1 = hprev[HALO - 1:HALO, :]
    h2 = hprev[HALO - 2:HALO - 1, :]
    m1 = jnp.where(row == 0, h1, pltpu.roll(cu, 1, 0))
    m2 = jnp.where(row == 0, h2, jnp.where(row == 1, h1, pltpu.roll(cu, 2, 0)))
    return m1, m2


def _mixer_mid_fwd(proj, attn, wcp, wap, wout, convw, x, gt, *, tm, name):
    T = x.shape[0]
    tm = _tile(T, tm)
    hb = tm // HALO

    def body(bg_ref, cg_ref, u_ref, hcg_ref, hu_ref, zc0_ref, zc1_ref, za0_ref, za1_ref, at_ref,
             wcp_ref, wap_ref, wout_ref, cw_ref, x_ref, gt_ref,
             x2_ref, gc_ref, yc_ref, ya_ref, mg_ref, o_ref):
        first = jnp.where(pl.program_id(0) == 0, 0.0, 1.0)
        cu = cg_ref[...].astype(F32) * u_ref[...].astype(F32)
        hprev = first * (hcg_ref[...].astype(F32) * hu_ref[...].astype(F32))
        m1, m2 = _conv_shifts(cu, hprev, tm)
        cv = cw_ref[0:1, :] * m2 + cw_ref[1:2, :] * m1 + cw_ref[2:3, :] * cu
        gc = (bg_ref[...].astype(F32) * cv).astype(BF)
        gc_ref[...] = gc
        yc = jnp.dot(gc, wcp_ref[...], preferred_element_type=F32)
        ya = jnp.dot(at_ref[...], wap_ref[...], preferred_element_type=F32)
        yc_ref[...] = yc.astype(BF)
        ya_ref[...] = ya.astype(BF)
        zc = jnp.concatenate([zc0_ref[...], zc1_ref[...]], axis=1).astype(F32)
        za = jnp.concatenate([za0_ref[...], za1_ref[...]], axis=1).astype(F32)
        mg = (_sigmoid(zc) * yc + _sigmoid(za) * ya).astype(BF)
        mg_ref[...] = mg
        o = jnp.dot(mg, wout_ref[...], preferred_element_type=F32)
        o_ref[...] = o.astype(BF)
        x2_ref[...] = x_ref[...] + gt_ref[...] * o

    wspec = pl.BlockSpec((D, D), _const2)
    rowspec = pl.BlockSpec((tm, D), _row)
    return pl.pallas_call(
        body, name=name, grid=(T // tm,),
        in_specs=[_col(tm, O_BG), _col(tm, O_CG), _col(tm, O_U), _halo_prev(hb, O_CG), _halo_prev(hb, O_U),
                  _col(tm, O_ZC, 512), _col(tm, O_ZC + 512, 512), _col(tm, O_ZA, 512), _col(tm, O_ZA + 512, 512),
                  rowspec, wspec, wspec, wspec, pl.BlockSpec((8, D), _const2), rowspec, pl.BlockSpec((1, D), _const2)],
        out_specs=[rowspec] * 6,
        out_shape=[jax.ShapeDtypeStruct((T, D), F32)] + [jax.ShapeDtypeStruct((T, D), BF)] * 5,
        compiler_params=_cp(("parallel",)),
    )(proj, proj, proj, proj, proj, proj, proj, proj, proj, attn, wcp, wap, wout, convw, x, gt)


def _col(tm, c, w=D):
    assert c % w == 0
    return pl.BlockSpec((tm, w), lambda i: (i, c // w))


def _halo_prev(hb, c):
    return pl.BlockSpec((HALO, D), lambda i: (jnp.maximum(i * hb - 1, 0), c // D))


def _halo_next(hb, nblk, c=0):
    return pl.BlockSpec((HALO, D), lambda i: (jnp.minimum((i + 1) * hb, nblk - 1), c // D))


def _mixer_mid_bwd(dx2, gt, o, proj, yc, ya, wout, wcp, wap, *, tm, name):
    T = dx2.shape[0]
    tm = _tile(T, tm)

    def body(dx_ref, gt_ref, o_ref, zc0_ref, zc1_ref, za0_ref, za1_ref, yc_ref, ya_ref, wout_ref, wcp_ref, wap_ref,
             dout_ref, dyc_ref, dya_ref, dgc_ref, dat_ref, dz_ref, dgt_ref):
        @pl.when(pl.program_id(0) == 0)
        def _():
            dgt_ref[...] = jnp.zeros_like(dgt_ref)
        dxv = dx_ref[...]
        dgt_ref[...] += jnp.sum(dxv * o_ref[...].astype(F32), axis=0, keepdims=True)
        dout = (gt_ref[...] * dxv).astype(BF)
        dout_ref[...] = dout
        dmg = lax.dot_general(dout, wout_ref[...], NT, preferred_element_type=F32)
        sc = _sigmoid(jnp.concatenate([zc0_ref[...], zc1_ref[...]], axis=1).astype(F32))
        sa = _sigmoid(jnp.concatenate([za0_ref[...], za1_ref[...]], axis=1).astype(F32))
        dyc = (dmg * sc).astype(BF)
        dya = (dmg * sa).astype(BF)
        dyc_ref[...] = dyc
        dya_ref[...] = dya
        dz_ref[:, 0:D] = (dmg * yc_ref[...].astype(F32) * (sc * (1.0 - sc))).astype(BF)
        dz_ref[:, D:2 * D] = (dmg * ya_ref[...].astype(F32) * (sa * (1.0 - sa))).astype(BF)
        dgc_ref[...] = lax.dot_general(dyc, wcp_ref[...], NT, preferred_element_type=F32).astype(BF)
        dat_ref[...] = lax.dot_general(dya, wap_ref[...], NT, preferred_element_type=F32).astype(BF)

    wspec = pl.BlockSpec((D, D), _const2)
    rowspec = pl.BlockSpec((tm, D), _row)
    vec = pl.BlockSpec((1, D), _const2)
    return pl.pallas_call(
        body, name=name, grid=(T // tm,),
        in_specs=[rowspec, vec, rowspec,
                  _col(tm, O_ZC, 512), _col(tm, O_ZC + 512, 512), _col(tm, O_ZA, 512), _col(tm, O_ZA + 512, 512),
                  rowspec, rowspec, wspec, wspec, wspec],
        out_specs=[rowspec] * 5 + [pl.BlockSpec((tm, 2 * D), _row), vec],
        out_shape=[jax.ShapeDtypeStruct((T, D), BF)] * 5 + [jax.ShapeDtypeStruct((T, 2 * D), BF),
                                                            jax.ShapeDtypeStruct((1, D), F32)],
        compiler_params=_cp(("arbitrary",)),
    )(dx2, gt, o, proj, proj, proj, proj, yc, ya, wout, wcp, wap)


def _conv_bwd(dgc, proj, convw, *, tm, name):
    T = dgc.shape[0]
    tm = _tile(T, tm)
    hb = tm // HALO
    nblk = T // HALO
    nt = T // tm

    def body(dgc_ref, ndgc_ref, bg_ref, nbg_ref, cg_ref, u_ref, hcg_ref, hu_ref, cw_ref, dp_ref, dcw_ref):
        i = pl.program_id(0)

        @pl.when(i == 0)
        def _():
            dcw_ref[...] = jnp.zeros_like(dcw_ref)
        first = jnp.where(i == 0, 0.0, 1.0)
        last = jnp.where(i == nt - 1, 0.0, 1.0)
        cg = cg_ref[...].astype(F32)
        u = u_ref[...].astype(F32)
        bg = bg_ref[...].astype(F32)
        dg = dgc_ref[...].astype(F32)
        cu = cg * u
        hprev = first * (hcg_ref[...].astype(F32) * hu_ref[...].astype(F32))
        m1, m2 = _conv_shifts(cu, hprev, tm)
        w0, w1, w2 = cw_ref[0:1, :], cw_ref[1:2, :], cw_ref[2:3, :]
        cv = w0 * m2 + w1 * m1 + w2 * cu
        dcv = dg * bg
        nxt = last * (ndgc_ref[...].astype(F32) * nbg_ref[...].astype(F32))
        n0, n1 = nxt[0:1, :], nxt[1:2, :]
        row = lax.broadcasted_iota(jnp.int32, dcv.shape, 0)
        p1 = jnp.where(row == tm - 1, n0, pltpu.roll(dcv, tm - 1, 0))
        p2 = jnp.where(row == tm - 1, n1, jnp.where(row == tm - 2, n0, pltpu.roll(dcv, tm - 2, 0)))
        dcu = w2 * dcv + w1 * p1 + w0 * p2
        dp_ref[:, 0:D] = (dg * cv).astype(BF)
        dp_ref[:, D:2 * D] = (dcu * u).astype(BF)
        dp_ref[:, 2 * D:3 * D] = (dcu * cg).astype(BF)
        dcw_ref[0:1, :] += jnp.sum(dcv * m2, axis=0, keepdims=True)
        dcw_ref[1:2, :] += jnp.sum(dcv * m1, axis=0, keepdims=True)
        dcw_ref[2:3, :] += jnp.sum(dcv * cu, axis=0, keepdims=True)

    rowspec = pl.BlockSpec((tm, D), _row)
    cw = pl.BlockSpec((8, D), _const2)
    return pl.pallas_call(
        body, name=name, grid=(nt,),
        in_specs=[rowspec, _halo_next(hb, nblk), _col(tm, O_BG), _halo_next(hb, nblk, O_BG),
                  _col(tm, O_CG), _col(tm, O_U), _halo_prev(hb, O_CG), _halo_prev(hb, O_U), cw],
        out_specs=[pl.BlockSpec((tm, 3 * D), _row), cw],
        out_shape=[jax.ShapeDtypeStruct((T, 3 * D), BF), jax.ShapeDtypeStruct((8, D), F32)],
        compiler_params=_cp(("arbitrary",)),
    )(dgc, dgc, proj, proj, proj, proj, proj, proj, convw)


def _gsum(parts, *, tm, name):
    _, R, C = parts.shape
    tm = _tile(R, tm)

    def body(p_ref, o_ref):
        acc = p_ref[0].astype(F32)
        for s in range(1, N_DEV):
            acc = acc + p_ref[s].astype(F32)
        o_ref[...] = acc

    return pl.pallas_call(
        body, name=name, grid=(R // tm,),
        in_specs=[pl.BlockSpec((N_DEV, tm, C), lambda i: (0, i, 0))],
        out_specs=pl.BlockSpec((tm, C), _row),
        out_shape=jax.ShapeDtypeStruct((R, C), F32),
        compiler_params=_cp(("parallel",)),
    )(parts)


def _adam(w, g, m, v, *, tm, name):
    R, C = w.shape
    tm = _tile(R, tm)
    c1 = 1.0 - ADAM_B1
    c2 = 1.0 - ADAM_B2
    bc1 = 1.0 - ADAM_B1 ** ADAM_STEP
    bc2 = 1.0 - ADAM_B2 ** ADAM_STEP

    def body(w_ref, g_ref, m_ref, v_ref, d_ref, nm_ref, nv_ref):
        gv = g_ref[...]
        nm = ADAM_B1 * m_ref[...] + c1 * gv
        nv = ADAM_B2 * v_ref[...] + c2 * (gv * gv)
        nm_ref[...] = nm
        nv_ref[...] = nv
        d_ref[...] = -ADAM_LR * ((nm / bc1) / (jnp.sqrt(nv / bc2) + ADAM_EPS) + ADAM_WD * w_ref[...])

    spec = pl.BlockSpec((tm, C), _row)
    return pl.pallas_call(
        body, name=name, grid=(R // tm,),
        in_specs=[spec] * 4, out_specs=[spec] * 3,
        out_shape=[jax.ShapeDtypeStruct((R, C), F32)] * 3,
        compiler_params=_cp(("parallel",)),
    )(w, g, m, v)


def _mods_part(c_all, w_ada, b_ada, *, name):
    C = w_ada.shape[1]

    def body(c_ref, w_ref, b_ref, o_ref):
        cv = c_ref[...]
        ca = cv * jax.nn.sigmoid(cv)
        o_ref[...] = jnp.dot(ca, w_ref[...], preferred_element_type=F32,
                             precision=lax.Precision.HIGHEST) + b_ref[...]

    return pl.pallas_call(
        body, name=name,
        out_shape=jax.ShapeDtypeStruct((N_DEV, C), F32),
        compiler_params=_cp(),
    )(c_all, w_ada, b_ada)


def _wada_grad(c_all_t, gm, *, name):
    C = gm.shape[1]

    def body(c_ref, g_ref, o_ref):
        cv = c_ref[...]
        ca = cv * jax.nn.sigmoid(cv)
        acc = ca[:, 0:1] * g_ref[0:1, :]
        for b in range(1, N_DEV):
            acc = acc + ca[:, b:b + 1] * g_ref[b:b + 1, :]
        o_ref[...] = acc

    return pl.pallas_call(
        body, name=name,
        out_shape=jax.ShapeDtypeStruct((D, C), F32),
        compiler_params=_cp(),
    )(c_all_t, gm)


def _peer(x, y, c, d):
    px = lax.rem(x + ((d >> 2) & 1), 2)
    py = lax.rem(y + ((d >> 1) & 1), 2)
    pc = lax.rem(c + (d & 1), 2)
    return (px, py, pc), 4 * px + 2 * py + pc


def _exchange(xs, *, scatter, name):
    n = len(xs)
    nsem = n * (N_DEV - 1)

    def body(*refs):
        ins, outs = refs[:n], refs[n:2 * n]
        send_sems, recv_sems, local_sems = refs[2 * n:]
        x, y, c = lax.axis_index("x"), lax.axis_index("y"), lax.axis_index("c")
        me = 4 * x + 2 * y + c

        def src(t, idx):
            return ins[t].at[idx] if scatter else ins[t]

        local = [pltpu.make_async_copy(src(t, me), outs[t].at[me], local_sems.at[t]) for t in range(n)]
        for cp in local:
            cp.start()
        remote = []
        for t in range(n):
            for d in range(1, N_DEV):
                peer, pidx = _peer(x, y, c, d)
                k = t * (N_DEV - 1) + d - 1
                send = pltpu.make_async_remote_copy(src_ref=src(t, pidx), dst_ref=outs[t].at[me],
                                                    send_sem=send_sems.at[k], recv_sem=recv_sems.at[k],
                                                    device_id=peer, device_id_type=MESH)
                recv = pltpu.make_async_remote_copy(src_ref=src(t, pidx), dst_ref=outs[t].at[pidx],
                                                    send_sem=send_sems.at[k], recv_sem=recv_sems.at[k],
                                                    device_id=peer, device_id_type=MESH)
                send.start()
                remote.append((send, recv))
        for cp in local:
            cp.wait()
        for send, recv in remote:
            send.wait_send()
            recv.wait_recv()

    anyspec = pl.BlockSpec(memory_space=pl.ANY)
    out_shape = [jax.ShapeDtypeStruct(a.shape if scatter else (N_DEV,) + a.shape, a.dtype) for a in xs]
    return pl.pallas_call(
        body, name=name,
        in_specs=[anyspec] * n, out_specs=[anyspec] * n, out_shape=out_shape,
        scratch_shapes=[pltpu.SemaphoreType.DMA((nsem,)), pltpu.SemaphoreType.DMA((nsem,)),
                        pltpu.SemaphoreType.DMA((n,))],
    )(*xs)


def _sum8(parts, *, name):
    _, R, C = parts.shape

    def body(p_ref, o_ref):
        acc = p_ref[0]
        for s in range(1, N_DEV):
            acc = acc + p_ref[s]
        o_ref[...] = acc

    return pl.pallas_call(body, name=name, out_shape=jax.ShapeDtypeStruct((R, C), F32),
                          compiler_params=_cp())(parts)


HBM_SPEC = pl.BlockSpec(memory_space=pltpu.HBM)
SEM_SPEC = pl.BlockSpec(memory_space=pltpu.SEMAPHORE)
N_PEER = N_DEV - 1


def _split_copies(src_refs, land_refs, send_sems, recv_sems, scatter):
    x, y, c = lax.axis_index("x"), lax.axis_index("y"), lax.axis_index("c")
    me = 4 * x + 2 * y + c
    pairs = []
    for j, (src, land) in enumerate(zip(src_refs, land_refs)):
        for d in range(1, N_DEV):
            peer, pidx = _peer(x, y, c, d)
            k = j * N_PEER + d - 1
            s = src.at[pidx] if scatter else src
            send = pltpu.make_async_remote_copy(src_ref=s, dst_ref=land.at[me], send_sem=send_sems.at[k],
                                                recv_sem=recv_sems.at[k], device_id=peer, device_id_type=MESH)
            recv = pltpu.make_async_remote_copy(src_ref=s, dst_ref=land.at[pidx], send_sem=send_sems.at[k],
                                                recv_sem=recv_sems.at[k], device_id=peer, device_id_type=MESH)
            pairs.append((send, recv))
    return pairs


def _own_slot(block, me):
    land = lax.empty((N_DEV,) + block.shape, block.dtype)
    return lax.dynamic_update_slice(land, block[None], (me, 0, 0))


def _split_start(srcs, lands, groups, *, scatter, name):
    n, ng = len(srcs), len(groups)

    def body(*refs):
        src_refs, land_refs = refs[:n], refs[n:2 * n]
        sems = refs[2 * n:2 * n + 2 * ng]
        token = refs[-1]
        for gi, g in enumerate(groups):
            pairs = _split_copies([src_refs[t] for t in g], [land_refs[t] for t in g], sems[2 * gi],
                                  sems[2 * gi + 1], scatter)
            for send, _ in pairs:
                send.start()
        token[...] = jnp.zeros_like(token)

    sem_shapes = []
    for g in groups:
        sem_shapes += [pltpu.SemaphoreType.DMA((len(g) * N_PEER,))] * 2
    thru = [pltpu.HBM(a.shape, a.dtype) for a in list(srcs) + list(lands)]
    outs = pl.pallas_call(
        body, name=name,
        out_shape=tuple(sem_shapes + thru + [jax.ShapeDtypeStruct((8, 128), F32)]),
        in_specs=[HBM_SPEC] * (2 * n),
        out_specs=tuple([SEM_SPEC] * (2 * ng) + [HBM_SPEC] * (2 * n) + [pl.BlockSpec(memory_space=pltpu.VMEM)]),
        input_output_aliases={i: 2 * ng + i for i in range(2 * n)},
        compiler_params=pltpu.CompilerParams(has_side_effects=pltpu.SideEffectType.DATAFLOW_SIDE_EFFECTING),
    )(*[pltpu.with_memory_space_constraint(a, pltpu.HBM) for a in list(srcs) + list(lands)])
    sems = [(outs[2 * gi], outs[2 * gi + 1]) for gi in range(ng)]
    return sems, outs[2 * ng:2 * ng + n], outs[2 * ng + n:2 * ng + 2 * n], outs[-1]


def _behind(v, token):
    if token is None:
        return v
    return lax.optimization_barrier((v, token))[0]


def _split_wait(srcs, lands, sems, after, *, scatter, name):
    m = len(srcs)

    def body(*refs):
        src_refs, land_refs = refs[:m], refs[m:2 * m]
        send_sems, recv_sems = refs[2 * m], refs[2 * m + 1]
        for send, recv in _split_copies(src_refs, land_refs, send_sems, recv_sems, scatter):
            send.wait_send()
            recv.wait_recv()

    outs = pl.pallas_call(
        body, name=name,
        out_shape=tuple(pltpu.HBM(a.shape, a.dtype) for a in list(srcs) + list(lands)),
        in_specs=[HBM_SPEC] * (2 * m) + [SEM_SPEC, SEM_SPEC, pl.BlockSpec(memory_space=pl.ANY)],
        out_specs=tuple([HBM_SPEC] * (2 * m)),
        input_output_aliases={i: i for i in range(2 * m)},
        compiler_params=pltpu.CompilerParams(has_side_effects=pltpu.SideEffectType.DATAFLOW_SIDE_EFFECTING),
    )(*srcs, *lands, sems[0], sems[1], after)
    return outs[m:]


TM_PROJ = 1024
TM_ROW = 512
TM_NN = 512
TK_TN = 512
TN_FFN = F // 2
TN_IN = NIN // 4


def _tn(a, b, name, tn):
    if a.ndim == 2:
        a = a[None]
    return _tn_matmul(a, b, tn=tn, tk=TK_TN, name=name)


def _local_step(x, tgt, mods, g1, gm, g2, gf, convw8, sinks, w_get, g_put):
    T = x.shape[0]
    sh1, sc1, gt1, sh2, sc2, gt2, sh3, sc3, gt3 = [mods[i:i + 1] for i in range(N_MOD)]
    cos, sin = _rope_tables(T)
    behind = _behind

    w = dict(w_get("gu1", mods))
    h1, ab1 = _norm_proj(x, g1, sc1, sh1, w["gu1"], tm=TM_PROJ, tn=TN_FFN, name="ffn1_up")
    w.update(w_get("d1", ab1))
    x1, y1 = _ffn_down_fwd(ab1, w["d1"], x, gt1, tm=TM_ROW, name="ffn1_down")
    w.update(w_get("mix", x1))
    h2, proj = _norm_proj(x1, gm, sc2, sh2, w["win"], tm=TM_PROJ, tn=TN_IN, name="mix_in")
    attn = _attn_fwd(proj, cos, sin, sinks, name="attn_fwd")
    x2, gc, yc, ya, mg, o = _mixer_mid_fwd(proj, attn, w["cp"], w["ap"], w["out"], convw8, x1, gt2,
                                           tm=TM_ROW, name="mix_mid")
    w.update(w_get("ffn2", x2))
    h3, ab2 = _norm_proj(x2, g2, sc3, sh3, w["gu2"], tm=TM_PROJ, tn=TN_FFN, name="ffn2_up")
    x3, y2 = _ffn_down_fwd(ab2, w["d2"], x2, gt3, tm=TM_ROW, name="ffn2_down")
    dx3, lsum, dgf = _final_fwd_bwd(x3, tgt, gf, tm=TM_ROW, name="final")

    dy2, dab2, dgt3 = _ffn_down_bwd(dx3, y2, gt3, ab2, w["d2"], tm=TM_ROW, tn=TN_FFN, name="ffn2_down_bwd")
    g_d2 = _tn_matmul_swiglu(ab2, dy2, tn=TN_FFN, tk=TK_TN, name="ffn2_down_dw")
    dx2, dsh3, dsc3, dg2 = _nn_bwd_norm(dab2, w["gu2"], x2, g2, sc3, dx3, tm=TM_NN, tk=TN_FFN, name="ffn2_up_bwd")
    g_gu2 = _tn(dab2, h3, "ffn2_up_dw", TN_FFN)
    tok = g_put(dict(gu2=g_gu2, d2=g_d2))

    dout, dyc, dya, dgc, dat, dz, dgt2 = _mixer_mid_bwd(dx2, behind(gt2, tok), o, proj, yc, ya, w["out"], w["cp"],
                                                        w["ap"], tm=TM_ROW, name="mix_mid_bwd")
    g_out = _tn(mg, dout, "mix_out_dw", D)
    g_cp = _tn(gc, dyc, "mix_cp_dw", D)
    g_ap = _tn(attn, dya, "mix_ap_dw", D)
    dq, dkc, dkp, dvc, dvp, dsink = _attn_bwd(proj, cos, sin, sinks, attn, dat, name="attn_bwd")
    dkv = _dkv_combine(dkc, dkp, dvc, dvp, name="attn_dkv")
    dp1, dcw = _conv_bwd(dgc, proj, convw8, tm=TM_ROW, name="conv_bwd")
    dproj = jnp.concatenate([dp1, dq, dkv, dz], axis=1)
    g_in = _tn(dproj, h2, "mix_in_dw", TN_IN)
    tok = g_put(dict(win=g_in, cp=g_cp, ap=g_ap, out=g_out))
    dx1, dsh2, dsc2, dgm = _nn_bwd_norm(dproj[None], w["win"], x1, gm, behind(sc2, tok), dx2, tm=TM_NN, tk=TN_IN,
                                        name="mix_in_bwd")

    dy1, dab1, dgt1 = _ffn_down_bwd(dx1, y1, gt1, ab1, w["d1"], tm=TM_ROW, tn=TN_FFN, name="ffn1_down_bwd")
    g_gu1 = _tn(dab1, h1, "ffn1_up_dw", TN_FFN)
    tok = g_put(dict(gu1=g_gu1))
    g_d1 = _tn_matmul_swiglu(ab1, behind(dy1, tok), tn=TN_FFN, tk=TK_TN, name="ffn1_down_dw")
    tok = g_put(dict(d1=g_d1))
    dx0, dsh1, dsc1, dg1 = _nn_bwd_norm(dab1, w["gu1"], x, g1, behind(sc1, tok), dx1, tm=TM_NN, tk=TN_FFN,
                                        name="ffn1_up_bwd")

    small = dict(mods=jnp.concatenate([dsh1, dsc1, dgt1, dsh2, dsc2, dgt2, dsh3, dsc3, dgt3], axis=0),
                 g1=dg1, gm=dgm, g2=dg2, gf=dgf, convw=dcw[0:3], sinks=dsink[:, 0:N_HEADS])
    return lsum, dx0, small


BIG = ("gu1", "d1", "win", "cp", "ap", "out", "gu2", "d2")
TRANSPOSED = ("gu1", "win", "gu2")
SMALL_ROWS = 24
R_MODS, R_G1, R_GM, R_G2, R_GF, R_CONV, R_SINK = 0, 9, 10, 11, 12, 13, 16


def _pad_to(a, rows, cols):
    return jnp.pad(a, ((0, rows - a.shape[0]), (0, cols - a.shape[1])))


def _pack_small(b_ada, g1, gm, g2, gf, conv, sinks):
    rows = [b_ada.reshape(N_MOD, D), g1.reshape(1, D), gm.reshape(1, D), g2.reshape(1, D), gf.reshape(1, D),
            _pad_to(conv.reshape(3, -1), 3, D), _pad_to(sinks.reshape(1, N_HEADS), 1, D)]
    return _pad_to(jnp.concatenate(rows, axis=0), SMALL_ROWS, D)


def _unpack_small(p, conv_cols):
    return dict(b_ada=p[R_MODS:R_MODS + N_MOD].reshape(1, N_MOD * D), g_ffn1=p[R_G1:R_G1 + 1],
                g_mix=p[R_GM:R_GM + 1], g_ffn2=p[R_G2:R_G2 + 1], g_final=p[R_GF],
                conv_w=p[R_CONV:R_CONV + 3, 0:conv_cols][None], sinks=p[R_SINK:R_SINK + 1, 0:N_HEADS])


def kernel(x, c, w_ada, b_ada, g_ffn1, w1_gu, w1_down, g_mix, w_in, conv_w, w_conv_proj, w_attn_proj, sinks, w_out, g_ffn2, w2_gu, w2_down, g_final, loss_target, m_w_ada, m_b_ada, m_g_ffn1, m_w1_gu, m_w1_down, m_g_mix, m_w_in, m_conv_w, m_w_conv_proj, m_w_attn_proj, m_sinks, m_w_out, m_g_ffn2, m_w2_gu, m_w2_down, m_g_final, v_w_ada, v_b_ada, v_g_ffn1, v_w1_gu, v_w1_down, v_g_mix, v_w_in, v_conv_w, v_w_conv_proj, v_w_attn_proj, v_sinks, v_w_out, v_g_ffn2, v_w2_gu, v_w2_down, v_g_final):
    me = 4 * lax.axis_index("x") + 2 * lax.axis_index("y") + lax.axis_index("c")
    ada_cols = w_ada.shape[2]
    conv_cols = conv_w.shape[2]

    native = dict(gu1=w1_gu[0], d1=w1_down[0], win=w_in[0], cp=w_conv_proj[0], ap=w_attn_proj[0], out=w_out[0],
                  gu2=w2_gu[0], d2=w2_down[0])

    def shard(n, token):
        a = _behind(native[n], token)
        return (a.T if n in TRANSPOSED else a).astype(BF)

    c_all, conv_all = _exchange([c, _pad_to(conv_w[0], 8, conv_cols)], scatter=False, name="gather_cond")
    c_all = c_all.reshape(N_DEV, D)
    conv_full = conv_all[:, 0:3, :].transpose(1, 0, 2).reshape(3, D)

    b_cols = lax.dynamic_slice(b_ada, (0, me * ada_cols), (1, ada_cols))
    mods_cols = _mods_part(c_all, w_ada[0], b_cols, name="ada_mods")
    (mods_all,) = _exchange([mods_cols], scatter=False, name="gather_mods")
    mods = lax.dynamic_index_in_dim(mods_all, me, axis=1, keepdims=False).reshape(N_MOD, D)

    groups = dict(gu1=("gu1",), d1=("d1",), mix=("win", "cp", "ap", "out"), ffn2=("gu2", "d2"))
    in_flight = {}
    first = [shard("gu1", mods_all)]
    sems, srcs, lands, token = _split_start(first, [_own_slot(s, me) for s in first], [[0]], scatter=False,
                                            name="gather_weights_start_gu1")
    in_flight["gu1"] = (sems[0], srcs, lands)
    rest = [n for n in BIG if n != "gu1"]
    shards = [shard(n, token) for n in rest]
    rest_groups = [[rest.index(n) for n in names] for g, names in groups.items() if g != "gu1"]
    sems, srcs, lands, _ = _split_start(shards, [_own_slot(s, me) for s in shards], rest_groups, scatter=False,
                                        name="gather_weights_start_rest")
    for (g, names), gsems, idx in zip([kv for kv in groups.items() if kv[0] != "gu1"], sems, rest_groups):
        in_flight[g] = (gsems, [srcs[t] for t in idx], [lands[t] for t in idx])

    def w_get(group, after):
        gsems, gsrcs, glands = in_flight[group]
        landed = _split_wait(gsrcs, glands, gsems, after, scatter=False, name="gather_weights_wait_" + group)
        return {n: a.reshape(-1, D) for n, a in zip(groups[group], landed)}

    pending = []

    def g_put(gs):
        names = tuple(gs)
        srcs = [gs[n].reshape(N_DEV, -1, D) for n in names]
        lands = [_own_slot(lax.dynamic_index_in_dim(s, me, axis=0, keepdims=False), me) for s in srcs]
        sems, srcs, lands, tok = _split_start(srcs, lands, [list(range(len(names)))], scatter=True,
                                              name="scatter_grads_start_" + names[0])
        pending.append((names, sems[0], srcs, lands))
        return tok

    lsum, grad_x, small = _local_step(x[0], loss_target[0], mods, g_ffn1, g_mix, g_ffn2, g_final[None],
                                      _pad_to(conv_full, 8, D), sinks[0], w_get, g_put)
    loss = lax.psum((0.5 / D) * jnp.sum(lsum), ("x", "y", "c"))

    packed = _pack_small(small["mods"], small["g1"], small["gm"], small["g2"], small["gf"], small["convw"],
                         small["sinks"])
    (packed_all,) = _exchange([packed], scatter=False, name="gather_small")
    gsmall = _sum8(packed_all, name="sum_small")

    grads = {}
    after = gsmall
    for names, sems, srcs, lands in pending:
        parts = _split_wait(srcs, lands, sems, after, scatter=True, name="scatter_grads_wait_" + names[0])
        for n, p in zip(names, parts):
            g = _gsum(p, tm=128, name="gsum_" + n)
            grads[n] = g.T if n in TRANSPOSED else g
        after = g

    gm_cols = lax.dynamic_slice(packed_all[:, R_MODS:R_MODS + N_MOD, :].reshape(N_DEV, N_MOD * D),
                                (0, me * ada_cols), (N_DEV, ada_cols))
    grads["ada"] = _wada_grad(c_all.T, gm_cols, name="ada_dw")
    conv_g = lax.dynamic_slice(gsmall[R_CONV:R_CONV + 3], (0, me * conv_cols), (3, conv_cols))
    gsmall_own = gsmall.at[R_CONV:R_CONV + 3].set(_pad_to(conv_g, 3, D))
    g_small = _unpack_small(gsmall_own, conv_cols)

    w_of = dict(ada=w_ada, gu1=w1_gu, d1=w1_down, win=w_in, cp=w_conv_proj, ap=w_attn_proj, out=w_out, gu2=w2_gu,
                d2=w2_down)
    m_of = dict(ada=m_w_ada, gu1=m_w1_gu, d1=m_w1_down, win=m_w_in, cp=m_w_conv_proj, ap=m_w_attn_proj, out=m_w_out,
                gu2=m_w2_gu, d2=m_w2_down)
    v_of = dict(ada=v_w_ada, gu1=v_w1_gu, d1=v_w1_down, win=v_w_in, cp=v_w_conv_proj, ap=v_w_attn_proj, out=v_w_out,
                gu2=v_w2_gu, d2=v_w2_down)
    upd = {n: _adam(w_of[n][0], grads[n], m_of[n][0], v_of[n][0], tm=128, name="adam_" + n) for n in w_of}
    small_upd = _adam(_pack_small(b_ada, g_ffn1, g_mix, g_ffn2, g_final, conv_w, sinks), gsmall_own,
                      _pack_small(m_b_ada, m_g_ffn1, m_g_mix, m_g_ffn2, m_g_final, m_conv_w, m_sinks),
                      _pack_small(v_b_ada, v_g_ffn1, v_g_mix, v_g_ffn2, v_g_final, v_conv_w, v_sinks),
                      tm=SMALL_ROWS, name="adam_small")
    small_out = [g_small] + [_unpack_small(p, conv_cols) for p in small_upd]

    big_name = dict(w_ada="ada", w1_gu="gu1", w1_down="d1", w_in="win", w_conv_proj="cp", w_attn_proj="ap",
                    w_out="out", w2_gu="gu2", w2_down="d2")
    order = ("w_ada", "b_ada", "g_ffn1", "w1_gu", "w1_down", "g_mix", "w_in", "conv_w", "w_conv_proj", "w_attn_proj",
             "sinks", "w_out", "g_ffn2", "w2_gu", "w2_down", "g_final")
    outs = [loss, grad_x[None]]
    for kind in range(4):
        for n in order:
            if n in big_name:
                t = grads[big_name[n]] if kind == 0 else upd[big_name[n]][kind - 1]
                outs.append(t[None])
            else:
                outs.append(small_out[kind][n])
    return tuple(outs)
```

```python
import functools

import jax
import jax.numpy as jnp
from jax import lax
from jax.experimental import pallas as pl
from jax.experimental.pallas import tpu as pltpu

D = 1024
F = 2816
NIN = 6656
N_HEADS = 16
N_KV = 4
HEAD_DIM = 64
BLK = 128
N_MOD = 9
N_DEV = 8
EPS = 1e-6
NEG_INF = -1e30
ROPE_THETA = 10000.0
O_BG, O_CG, O_U, O_Q, O_K, O_V, O_ZC, O_ZA = 0, 1024, 2048, 3072, 4096, 4352, 4608, 5632

ADAM_LR = 0.001
ADAM_B1 = 0.9
ADAM_B2 = 0.999
ADAM_EPS = 1e-08
ADAM_WD = 0.01
ADAM_STEP = 10

BF = jnp.bfloat16
F32 = jnp.float32
VMEM_LIMIT = 56 * 1024 * 1024
MESH = pl.DeviceIdType.MESH

NT = (((1,), (1,)), ((), ()))
TN = (((0,), (0,)), ((), ()))


def _cp(sem=None):
    return pltpu.CompilerParams(dimension_semantics=sem, vmem_limit_bytes=VMEM_LIMIT)


def _tile(n, pref):
    if n <= pref:
        return n
    for t in range(pref - pref % 16, 15, -16):
        if n % t == 0:
            return t
    raise ValueError((n, pref))


def _sigmoid(v):
    return 0.5 * jnp.tanh(0.5 * v) + 0.5


def _row(i):
    return (i, 0)


def _const2(*_):
    return (0, 0)


def _norm_proj(x, g, sc, sh, wt, *, tm, tn, name):
    T, N = x.shape[0], wt.shape[0]
    tm, tn = _tile(T, tm), _tile(N, tn)

    def body(x_ref, g_ref, sc_ref, sh_ref, w_ref, h_ref, o_ref, hs):
        @pl.when(pl.program_id(1) == 0)
        def _():
            xv = x_ref[...]
            r = lax.rsqrt(jnp.mean(xv * xv, axis=-1, keepdims=True) + EPS)
            hb = ((xv * r) * g_ref[...] * (1.0 + sc_ref[...]) + sh_ref[...]).astype(BF)
            hs[...] = hb
            h_ref[...] = hb
        o_ref[...] = lax.dot_general(hs[...], w_ref[...], NT, preferred_element_type=F32).astype(BF)

    vec = pl.BlockSpec((1, D), _const2)
    return pl.pallas_call(
        body, name=name, grid=(T // tm, N // tn),
        in_specs=[pl.BlockSpec((tm, D), lambda i, j: (i, 0)), vec, vec, vec,
                  pl.BlockSpec((tn, D), lambda i, j: (j, 0))],
        out_specs=[pl.BlockSpec((tm, D), lambda i, j: (i, 0)), pl.BlockSpec((tm, tn), lambda i, j: (i, j))],
        out_shape=[jax.ShapeDtypeStruct((T, D), BF), jax.ShapeDtypeStruct((T, N), BF)],
        scratch_shapes=[pltpu.VMEM((tm, D), BF)],
        compiler_params=_cp(("parallel", "arbitrary")),
    )(x, g, sc, sh, wt)


def _ffn_down_fwd(ab, wd, x, gt, *, tm, name):
    T = x.shape[0]
    tm = _tile(T, tm)

    def body(a_ref, b_ref, wd_ref, x_ref, gt_ref, xo_ref, y_ref):
        a = a_ref[...].astype(F32)
        act = (a * _sigmoid(a) * b_ref[...].astype(F32)).astype(BF)
        y = jnp.dot(act, wd_ref[...], preferred_element_type=F32)
        y_ref[...] = y.astype(BF)
        xo_ref[...] = x_ref[...] + (0.5 * gt_ref[...]) * y

    return pl.pallas_call(
        body, name=name, grid=(T // tm,),
        in_specs=[pl.BlockSpec((tm, F), lambda i: (i, 0)), pl.BlockSpec((tm, F), lambda i: (i, 1)),
                  pl.BlockSpec((F, D), _const2), pl.BlockSpec((tm, D), _row), pl.BlockSpec((1, D), _const2)],
        out_specs=[pl.BlockSpec((tm, D), _row), pl.BlockSpec((tm, D), _row)],
        out_shape=[jax.ShapeDtypeStruct((T, D), F32), jax.ShapeDtypeStruct((T, D), BF)],
        compiler_params=_cp(("parallel",)),
    )(ab, ab, wd, x, gt)


def _final_fwd_bwd(x, tgt, g, *, tm, name):
    T = x.shape[0]
    tm = _tile(T, tm)

    def body(x_ref, t_ref, g_ref, dx_ref, ls_ref, dg_ref):
        @pl.when(pl.program_id(0) == 0)
        def _():
            ls_ref[...] = jnp.zeros_like(ls_ref)
            dg_ref[...] = jnp.zeros_like(dg_ref)
        xv = x_ref[...]
        gv = g_ref[...]
        r = lax.rsqrt(jnp.mean(xv * xv, axis=-1, keepdims=True) + EPS)
        xh = xv * r
        e = xh * gv - t_ref[...]
        ls_ref[...] += jnp.sum(e * e, axis=0, keepdims=True)
        dy = e * (1.0 / D)
        dg_ref[...] += jnp.sum(dy * xh, axis=0, keepdims=True)
        dxh = dy * gv
        dx_ref[...] = r * (dxh - xh * jnp.mean(dxh * xh, axis=-1, keepdims=True))

    vec = pl.BlockSpec((1, D), _const2)
    return pl.pallas_call(
        body, name=name, grid=(T // tm,),
        in_specs=[pl.BlockSpec((tm, D), _row), pl.BlockSpec((tm, D), _row), vec],
        out_specs=[pl.BlockSpec((tm, D), _row), vec, vec],
        out_shape=[jax.ShapeDtypeStruct((T, D), F32), jax.ShapeDtypeStruct((1, D), F32),
                   jax.ShapeDtypeStruct((1, D), F32)],
        compiler_params=_cp(("arbitrary",)),
    )(x, tgt, g)


def _ffn_down_bwd(dxo, y, gt, ab, wd, *, tm, tn, name):
    T = dxo.shape[0]
    tm, tn = _tile(T, tm), _tile(F, tn)
    nj = F // tn

    def body(dxo_ref, y_ref, gt_ref, a_ref, b_ref, wd_ref, dy_ref, dab_ref, dgt_ref, dys):
        i, j = pl.program_id(0), pl.program_id(1)

        @pl.when(jnp.logical_and(i == 0, j == 0))
        def _():
            dgt_ref[...] = jnp.zeros_like(dgt_ref)

        @pl.when(j == 0)
        def _():
            dxv = dxo_ref[...]
            dgt_ref[...] += 0.5 * jnp.sum(dxv * y_ref[...].astype(F32), axis=0, keepdims=True)
            dyb = ((0.5 * gt_ref[...]) * dxv).astype(BF)
            dys[...] = dyb
            dy_ref[...] = dyb

        dact = lax.dot_general(dys[...], wd_ref[...], NT, preferred_element_type=F32)
        a = a_ref[...].astype(F32)
        b = b_ref[...].astype(F32)
        s = _sigmoid(a)
        silu = a * s
        dab_ref[0] = (dact * b * (s * (1.0 + a * (1.0 - s)))).astype(BF)
        dab_ref[1] = (dact * silu).astype(BF)

    vec = pl.BlockSpec((1, D), _const2)
    return pl.pallas_call(
        body, name=name, grid=(T // tm, nj),
        in_specs=[pl.BlockSpec((tm, D), lambda i, j: (i, 0)), pl.BlockSpec((tm, D), lambda i, j: (i, 0)), vec,
                  pl.BlockSpec((tm, tn), lambda i, j: (i, j)), pl.BlockSpec((tm, tn), lambda i, j: (i, j + nj)),
                  pl.BlockSpec((tn, D), lambda i, j: (j, 0))],
        out_specs=[pl.BlockSpec((tm, D), lambda i, j: (i, 0)), pl.BlockSpec((2, tm, tn), lambda i, j: (0, i, j)), vec],
        out_shape=[jax.ShapeDtypeStruct((T, D), BF), jax.ShapeDtypeStruct((2, T, F), BF),
                   jax.ShapeDtypeStruct((1, D), F32)],
        scratch_shapes=[pltpu.VMEM((tm, D), BF)],
        compiler_params=_cp(("arbitrary", "arbitrary")),
    )(dxo, y, gt, ab, ab, wd)


def _tn_matmul(a, b, *, tn, tk, name):
    S, T, Ns = a.shape
    tn, tk = _tile(Ns, tn), _tile(T, tk)
    nk, njs = T // tk, Ns // tn

    def body(a_ref, b_ref, o_ref, acc):
        k = pl.program_id(1)

        @pl.when(k == 0)
        def _():
            acc[...] = jnp.zeros_like(acc)
        acc[...] += lax.dot_general(a_ref[0], b_ref[...], TN, preferred_element_type=F32)

        @pl.when(k == nk - 1)
        def _():
            o_ref[...] = acc[...].astype(BF)

    return pl.pallas_call(
        body, name=name, grid=(S * njs, nk),
        in_specs=[pl.BlockSpec((1, tk, tn), lambda j, k: (j // njs, k, j % njs)),
                  pl.BlockSpec((tk, D), lambda j, k: (k, 0))],
        out_specs=pl.BlockSpec((tn, D), lambda j, k: (j, 0)),
        out_shape=jax.ShapeDtypeStruct((S * Ns, D), BF),
        scratch_shapes=[pltpu.VMEM((tn, D), F32)],
        compiler_params=_cp(("parallel", "arbitrary")),
    )(a, b)


def _tn_matmul_swiglu(ab, b, token, *, tn, tk, name):
    T = ab.shape[0]
    tn, tk = _tile(F, tn), _tile(T, tk)
    nk, nj = T // tk, F // tn
    deps = [] if token is None else [token]

    def body(a_ref, g_ref, b_ref, *rest):
        o_ref, acc = rest[len(deps):]
        k = pl.program_id(1)

        @pl.when(k == 0)
        def _():
            acc[...] = jnp.zeros_like(acc)
        a = a_ref[...].astype(F32)
        act = (a * _sigmoid(a) * g_ref[...].astype(F32)).astype(BF)
        acc[...] += lax.dot_general(act, b_ref[...], TN, preferred_element_type=F32)

        @pl.when(k == nk - 1)
        def _():
            o_ref[...] = acc[...].astype(BF)

    return pl.pallas_call(
        body, name=name, grid=(nj, nk),
        in_specs=[pl.BlockSpec((tk, tn), lambda j, k: (k, j)), pl.BlockSpec((tk, tn), lambda j, k: (k, j + nj)),
                  pl.BlockSpec((tk, D), lambda j, k: (k, 0))] + [pl.BlockSpec(memory_space=pl.ANY)] * len(deps),
        out_specs=pl.BlockSpec((tn, D), lambda j, k: (j, 0)),
        out_shape=jax.ShapeDtypeStruct((F, D), BF),
        scratch_shapes=[pltpu.VMEM((tn, D), F32)],
        compiler_params=_cp(("parallel", "arbitrary")),
    )(ab, ab, b, *deps)


def _nn_bwd_norm(da, w, x, g, sc, dxo, *, tm, tk, name):
    S, T, Ks = da.shape
    tm, tk = _tile(T, tm), _tile(Ks, tk)
    nks = Ks // tk
    nk = S * nks

    def body(da_ref, w_ref, x_ref, g_ref, sc_ref, dxo_ref, dx_ref, dsh_ref, dsc_ref, dg_ref, acc):
        i, k = pl.program_id(0), pl.program_id(1)

        @pl.when(jnp.logical_and(i == 0, k == 0))
        def _():
            dsh_ref[...] = jnp.zeros_like(dsh_ref)
            dsc_ref[...] = jnp.zeros_like(dsc_ref)
            dg_ref[...] = jnp.zeros_like(dg_ref)

        @pl.when(k == 0)
        def _():
            acc[...] = jnp.zeros_like(acc)
        acc[...] += jnp.dot(da_ref[0], w_ref[...], preferred_element_type=F32)

        @pl.when(k == nk - 1)
        def _():
            u = acc[...]
            xv = x_ref[...]
            gv = g_ref[...]
            sc1 = 1.0 + sc_ref[...]
            r = lax.rsqrt(jnp.mean(xv * xv, axis=-1, keepdims=True) + EPS)
            xh = xv * r
            dsh_ref[...] += jnp.sum(u, axis=0, keepdims=True)
            dsc_ref[...] += jnp.sum(u * (xh * gv), axis=0, keepdims=True)
            us = u * sc1
            dg_ref[...] += jnp.sum(us * xh, axis=0, keepdims=True)
            dxh = us * gv
            dx_ref[...] = dxo_ref[...] + r * (dxh - xh * jnp.mean(dxh * xh, axis=-1, keepdims=True))

    vec = pl.BlockSpec((1, D), _const2)
    return pl.pallas_call(
        body, name=name, grid=(T // tm, nk),
        in_specs=[pl.BlockSpec((1, tm, tk), lambda i, k: (k // nks, i, k % nks)),
                  pl.BlockSpec((tk, D), lambda i, k: (k, 0)),
                  pl.BlockSpec((tm, D), lambda i, k: (i, 0)), vec, vec,
                  pl.BlockSpec((tm, D), lambda i, k: (i, 0))],
        out_specs=[pl.BlockSpec((tm, D), lambda i, k: (i, 0)), vec, vec, vec],
        out_shape=[jax.ShapeDtypeStruct((T, D), F32)] + [jax.ShapeDtypeStruct((1, D), F32)] * 3,
        scratch_shapes=[pltpu.VMEM((tm, D), F32)],
        compiler_params=_cp(("arbitrary", "arbitrary")),
    )(da, w, x, g, sc, dxo)


def _rope(t, cos, sin_signed, lt32, inverse=False):
    sel = jnp.where(lt32, pltpu.roll(t, 96, 1), pltpu.roll(t, 32, 1))
    return t * cos - sel * sin_signed if inverse else t * cos + sel * sin_signed


def _rope_tables(T):
    inv = 1.0 / (ROPE_THETA ** (jnp.arange(0, HEAD_DIM, 2, dtype=F32) / HEAD_DIM))
    ang = jnp.arange(T, dtype=F32)[:, None] * inv[None, :]
    cos, sin = jnp.cos(ang), jnp.sin(ang)
    cos128 = jnp.tile(cos, (1, 4))
    sin128 = jnp.tile(jnp.concatenate([-sin, sin], axis=1), (1, 2))
    return cos128, sin128


def _attn_specs(nb):
    qspec = pl.BlockSpec((BLK, D), lambda n: (n, O_Q // D))
    kc = pl.BlockSpec((BLK, 256), lambda n: (n, O_K // 256))
    kp = pl.BlockSpec((BLK, 256), lambda n: (jnp.maximum(n - 1, 0), O_K // 256))
    vc = pl.BlockSpec((BLK, 256), lambda n: (n, O_V // 256))
    vp = pl.BlockSpec((BLK, 256), lambda n: (jnp.maximum(n - 1, 0), O_V // 256))
    tc = pl.BlockSpec((BLK, 128), lambda n: (n, 0))
    tp = pl.BlockSpec((BLK, 128), lambda n: (jnp.maximum(n - 1, 0), 0))
    return [qspec, kc, kp, vc, vp, tc, tc, tp, tp, pl.BlockSpec(memory_space=pltpu.SMEM)]


def _attn_common(q_ref, kc_ref, kp_ref, vc_ref, vp_ref, cc_ref, sc_ref, cp_ref, sp_ref):
    n = pl.program_id(0)
    lane = lax.broadcasted_iota(jnp.int32, (BLK, 128), 1)
    lt32 = (lane % HEAD_DIM) < (HEAD_DIM // 2)
    cc, sc, cp, sp = cc_ref[...], sc_ref[...], cp_ref[...], sp_ref[...]
    kr, vr = [], []
    for r in range(2):
        cols = slice(r * 128, (r + 1) * 128)
        kcur = _rope(kc_ref[:, cols].astype(F32), cc, sc, lt32)
        kprev = _rope(kp_ref[:, cols].astype(F32), cp, sp, lt32)
        kr.append(jnp.concatenate([kprev, kcur], axis=0).astype(BF))
        vr.append(jnp.concatenate([vp_ref[:, cols], vc_ref[:, cols]], axis=0))
    qr = [_rope(q_ref[:, p * 128:(p + 1) * 128].astype(F32), cc, sc, lt32) for p in range(8)]
    qi = lax.broadcasted_iota(jnp.int32, (4 * BLK, 2 * BLK), 0) % BLK
    kj = lax.broadcasted_iota(jnp.int32, (4 * BLK, 2 * BLK), 1)
    valid = (kj > qi) & (kj <= qi + BLK) & ((kj >= BLK) | (n > 0))
    halves = [lane < HEAD_DIM, lane >= HEAD_DIM]
    return qr, kr, vr, valid, halves, lt32, (cc, sc, cp, sp)


def _stack_heads(chunks, g, halves):
    half = g % 2
    parts = []
    for hh in range(4):
        h = 4 * g + hh
        t = chunks[h // 2]
        if h % 2 != half:
            t = pltpu.roll(t, HEAD_DIM, 1)
        parts.append(jnp.where(halves[half], t, 0.0))
    return jnp.concatenate(parts, axis=0)


def _softmax_sink(s, valid, sink_ref, g):
    s = jnp.where(valid, s * (HEAD_DIM ** -0.5), NEG_INF)
    sink = jnp.concatenate([jnp.full((BLK, 1), sink_ref[4 * g + hh], F32) for hh in range(4)], axis=0)
    m = jnp.maximum(jnp.max(s, axis=-1, keepdims=True), sink)
    p = jnp.exp(s - m)
    ps = jnp.exp(sink - m)
    inv = 1.0 / (jnp.sum(p, axis=-1, keepdims=True) + ps)
    return p * inv, ps * inv


def _attn_fwd(proj, cos, sin, sinks, *, name):
    T = proj.shape[0]
    nb = T // BLK

    def body(q_ref, kc_ref, kp_ref, vc_ref, vp_ref, cc_ref, sc_ref, cp_ref, sp_ref, sink_ref, o_ref):
        qr, kr, vr, valid, halves, _, _ = _attn_common(q_ref, kc_ref, kp_ref, vc_ref, vp_ref,
                                                        cc_ref, sc_ref, cp_ref, sp_ref)
        outs = [jnp.zeros((BLK, 128), F32) for _ in range(8)]
        for g in range(N_KV):
            r, half = g // 2, g % 2
            qs = _stack_heads(qr, g, halves).astype(BF)
            s = lax.dot_general(qs, kr[r], NT, preferred_element_type=F32)
            p, _ = _softmax_sink(s, valid, sink_ref, g)
            o = jnp.dot(p.astype(BF), vr[r], preferred_element_type=F32)
            for hh in range(4):
                h = 4 * g + hh
                oh = jnp.where(halves[half], o[hh * BLK:(hh + 1) * BLK], 0.0)
                if h % 2 != half:
                    oh = pltpu.roll(oh, HEAD_DIM, 1)
                outs[h // 2] = outs[h // 2] + oh
        o_ref[...] = jnp.concatenate(outs, axis=1).astype(BF)

    return pl.pallas_call(
        body, name=name, grid=(nb,),
        in_specs=_attn_specs(nb),
        out_specs=pl.BlockSpec((BLK, D), _row),
        out_shape=jax.ShapeDtypeStruct((T, D), BF),
        compiler_params=_cp(("parallel",)),
    )(proj, proj, proj, proj, proj, cos, sin, cos, sin, sinks)


def _attn_bwd(proj, cos, sin, sinks, o, do, *, name):
    T = proj.shape[0]
    nb = T // BLK

    def body(q_ref, kc_ref, kp_ref, vc_ref, vp_ref, cc_ref, sc_ref, cp_ref, sp_ref, sink_ref, o_ref, do_ref,
             dq_ref, dkc_ref, dkp_ref, dvc_ref, dvp_ref, dsink_ref):
        @pl.when(pl.program_id(0) == 0)
        def _():
            dsink_ref[...] = jnp.zeros_like(dsink_ref)
        qr, kr, vr, valid, halves, lt32, (cc, sc, cp, sp) = _attn_common(
            q_ref, kc_ref, kp_ref, vc_ref, vp_ref, cc_ref, sc_ref, cp_ref, sp_ref)
        oc = [o_ref[:, p * 128:(p + 1) * 128].astype(F32) for p in range(8)]
        doc = [do_ref[:, p * 128:(p + 1) * 128].astype(F32) for p in range(8)]
        dqs = [jnp.zeros((BLK, 128), F32) for _ in range(8)]
        dkr = [jnp.zeros((2 * BLK, 128), F32) for _ in range(2)]
        dvr = [jnp.zeros((2 * BLK, 128), F32) for _ in range(2)]
        lane1 = lax.broadcasted_iota(jnp.int32, (1, 128), 1)
        dsink = jnp.zeros((1, 128), F32)
        for g in range(N_KV):
            r, half = g // 2, g % 2
            qs = _stack_heads(qr, g, halves).astype(BF)
            dos = _stack_heads(doc, g, halves)
            os_ = _stack_heads(oc, g, halves)
            s = lax.dot_general(qs, kr[r], NT, preferred_element_type=F32)
            p, ps = _softmax_sink(s, valid, sink_ref, g)
            dosb = dos.astype(BF)
            dp = lax.dot_general(dosb, vr[r], NT, preferred_element_type=F32)
            delta = jnp.sum(dos * os_, axis=-1, keepdims=True)
            ds = (p * (dp - delta) * (HEAD_DIM ** -0.5)).astype(BF)
            dsk = -ps * delta
            for hh in range(4):
                val = jnp.sum(dsk[hh * BLK:(hh + 1) * BLK], axis=0, keepdims=True)
                dsink = dsink + jnp.where(lane1 == 4 * g + hh, val, 0.0)
            dvr[r] = dvr[r] + lax.dot_general(p.astype(BF), dosb, TN, preferred_element_type=F32)
            dkr[r] = dkr[r] + lax.dot_general(ds, qs, TN, preferred_element_type=F32)
            dq = jnp.dot(ds, kr[r], preferred_element_type=F32)
            for hh in range(4):
                h = 4 * g + hh
                dqh = jnp.where(halves[half], dq[hh * BLK:(hh + 1) * BLK], 0.0)
                if h % 2 != half:
                    dqh = pltpu.roll(dqh, HEAD_DIM, 1)
                dqs[h // 2] = dqs[h // 2] + dqh
        dsink_ref[...] += dsink
        dq_ref[...] = jnp.concatenate([_rope(t, cc, sc, lt32, inverse=True) for t in dqs], axis=1).astype(BF)
        dkp_ref[...] = jnp.concatenate([_rope(t[:BLK], cp, sp, lt32, inverse=True) for t in dkr], axis=1)
        dkc_ref[...] = jnp.concatenate([_rope(t[BLK:], cc, sc, lt32, inverse=True) for t in dkr], axis=1)
        dvp_ref[...] = jnp.concatenate([t[:BLK] for t in dvr], axis=1)
        dvc_ref[...] = jnp.concatenate([t[BLK:] for t in dvr], axis=1)

    kv = pl.BlockSpec((BLK, 256), _row)
    return pl.pallas_call(
        body, name=name, grid=(nb,),
        in_specs=_attn_specs(nb) + [pl.BlockSpec((BLK, D), _row), pl.BlockSpec((BLK, D), _row)],
        out_specs=[pl.BlockSpec((BLK, D), _row), kv, kv, kv, kv, pl.BlockSpec((1, 128), _const2)],
        out_shape=[jax.ShapeDtypeStruct((T, D), BF)] + [jax.ShapeDtypeStruct((T, 256), F32)] * 4
        + [jax.ShapeDtypeStruct((1, 128), F32)],
        compiler_params=_cp(("arbitrary",)),
    )(proj, proj, proj, proj, proj, cos, sin, cos, sin, sinks, o, do)


def _dkv_combine(dkc, dkp, dvc, dvp, *, name):
    T = dkc.shape[0]
    nb = T // BLK

    def body(dkc_ref, dkp_ref, dvc_ref, dvp_ref, o_ref):
        last = (pl.program_id(0) == nb - 1)
        keep = jnp.where(last, 0.0, 1.0)
        o_ref[:, 0:256] = (dkc_ref[...] + keep * dkp_ref[...]).astype(BF)
        o_ref[:, 256:512] = (dvc_ref[...] + keep * dvp_ref[...]).astype(BF)

    cur = pl.BlockSpec((BLK, 256), _row)
    nxt = pl.BlockSpec((BLK, 256), lambda n: (jnp.minimum(n + 1, nb - 1), 0))
    return pl.pallas_call(
        body, name=name, grid=(nb,),
        in_specs=[cur, nxt, cur, nxt],
        out_specs=pl.BlockSpec((BLK, 512), _row),
        out_shape=jax.ShapeDtypeStruct((T, 512), BF),
        compiler_params=_cp(("parallel",)),
    )(dkc, dkp, dvc, dvp)


HALO = 16


def _conv_shifts(cu, hprev, tm):
    row = lax.broadcasted_iota(jnp.int32, cu.shape, 0)
    h1 = hprev[HALO - 1:HALO, :]
    h2 = hprev[HALO - 2:HALO - 1, :]
    m1 = jnp.where(row == 0, h1, pltpu.roll(cu, 1, 0))
    m2 = jnp.where(row == 0, h2, jnp.where(row == 1, h1, pltpu.roll(cu, 2, 0)))
    return m1, m2


def _mixer_mid_fwd(proj, attn, wcp, wap, wout, convw, x, gt, *, tm, name):
    T = x.shape[0]
    tm = _tile(T, tm)
    hb = tm // HALO

    def body(bg_ref, cg_ref, u_ref, hcg_ref, hu_ref, zc0_ref, zc1_ref, za0_ref, za1_ref, at_ref,
             wcp_ref, wap_ref, wout_ref, cw_ref, x_ref, gt_ref,
             x2_ref, gc_ref, yc_ref, ya_ref, mg_ref, o_ref):
        first = jnp.where(pl.program_id(0) == 0, 0.0, 1.0)
        cu = cg_ref[...].astype(F32) * u_ref[...].astype(F32)
        hprev = first * (hcg_ref[...].astype(F32) * hu_ref[...].astype(F32))
        m1, m2 = _conv_shifts(cu, hprev, tm)
        cv = cw_ref[0:1, :] * m2 + cw_ref[1:2, :] * m1 + cw_ref[2:3, :] * cu
        gc = (bg_ref[...].astype(F32) * cv).astype(BF)
        gc_ref[...] = gc
        yc = jnp.dot(gc, wcp_ref[...], preferred_element_type=F32)
        ya = jnp.dot(at_ref[...], wap_ref[...], preferred_element_type=F32)
        yc_ref[...] = yc.astype(BF)
        ya_ref[...] = ya.astype(BF)
        zc = jnp.concatenate([zc0_ref[...], zc1_ref[...]], axis=1).astype(F32)
        za = jnp.concatenate([za0_ref[...], za1_ref[...]], axis=1).astype(F32)
        mg = (_sigmoid(zc) * yc + _sigmoid(za) * ya).astype(BF)
        mg_ref[...] = mg
        o = jnp.dot(mg, wout_ref[...], preferred_element_type=F32)
        o_ref[...] = o.astype(BF)
        x2_ref[...] = x_ref[...] + gt_ref[...] * o

    wspec = pl.BlockSpec((D, D), _const2)
    rowspec = pl.BlockSpec((tm, D), _row)
    return pl.pallas_call(
        body, name=name, grid=(T // tm,),
        in_specs=[_col(tm, O_BG), _col(tm, O_CG), _col(tm, O_U), _halo_prev(hb, O_CG), _halo_prev(hb, O_U),
                  _col(tm, O_ZC, 512), _col(tm, O_ZC + 512, 512), _col(tm, O_ZA, 512), _col(tm, O_ZA + 512, 512),
                  rowspec, wspec, wspec, wspec, pl.BlockSpec((8, D), _const2), rowspec, pl.BlockSpec((1, D), _const2)],
        out_specs=[rowspec] * 6,
        out_shape=[jax.ShapeDtypeStruct((T, D), F32)] + [jax.ShapeDtypeStruct((T, D), BF)] * 5,
        compiler_params=_cp(("parallel",)),
    )(proj, proj, proj, proj, proj, proj, proj, proj, proj, attn, wcp, wap, wout, convw, x, gt)


def _col(tm, c, w=D):
    assert c % w == 0
    return pl.BlockSpec((tm, w), lambda i: (i, c // w))


def _halo_prev(hb, c):
    return pl.BlockSpec((HALO, D), lambda i: (jnp.maximum(i * hb - 1, 0), c // D))


def _halo_next(hb, nblk, c=0):
    return pl.BlockSpec((HALO, D), lambda i: (jnp.minimum((i + 1) * hb, nblk - 1), c // D))


def _mixer_mid_bwd(dx2, gt, o, proj, yc, ya, wout, wcp, wap, *, tm, name):
    T = dx2.shape[0]
    tm = _tile(T, tm)

    def body(dx_ref, gt_ref, o_ref, zc0_ref, zc1_ref, za0_ref, za1_ref, yc_ref, ya_ref, wout_ref, wcp_ref, wap_ref,
             dout_ref, dyc_ref, dya_ref, dgc_ref, dat_ref, dz_ref, dgt_ref):
        @pl.when(pl.program_id(0) == 0)
        def _():
            dgt_ref[...] = jnp.zeros_like(dgt_ref)
        dxv = dx_ref[...]
        dgt_ref[...] += jnp.sum(dxv * o_ref[...].astype(F32), axis=0, keepdims=True)
        dout = (gt_ref[...] * dxv).astype(BF)
        dout_ref[...] = dout
        dmg = lax.dot_general(dout, wout_ref[...], NT, preferred_element_type=F32)
        sc = _sigmoid(jnp.concatenate([zc0_ref[...], zc1_ref[...]], axis=1).astype(F32))
        sa = _sigmoid(jnp.concatenate([za0_ref[...], za1_ref[...]], axis=1).astype(F32))
        dyc = (dmg * sc).astype(BF)
        dya = (dmg * sa).astype(BF)
        dyc_ref[...] = dyc
        dya_ref[...] = dya
        dz_ref[:, 0:D] = (dmg * yc_ref[...].astype(F32) * (sc * (1.0 - sc))).astype(BF)
        dz_ref[:, D:2 * D] = (dmg * ya_ref[...].astype(F32) * (sa * (1.0 - sa))).astype(BF)
        dgc_ref[...] = lax.dot_general(dyc, wcp_ref[...], NT, preferred_element_type=F32).astype(BF)
        dat_ref[...] = lax.dot_general(dya, wap_ref[...], NT, preferred_element_type=F32).astype(BF)

    wspec = pl.BlockSpec((D, D), _const2)
    rowspec = pl.BlockSpec((tm, D), _row)
    vec = pl.BlockSpec((1, D), _const2)
    return pl.pallas_call(
        body, name=name, grid=(T // tm,),
        in_specs=[rowspec, vec, rowspec,
                  _col(tm, O_ZC, 512), _col(tm, O_ZC + 512, 512), _col(tm, O_ZA, 512), _col(tm, O_ZA + 512, 512),
                  rowspec, rowspec, wspec, wspec, wspec],
        out_specs=[rowspec] * 5 + [pl.BlockSpec((tm, 2 * D), _row), vec],
        out_shape=[jax.ShapeDtypeStruct((T, D), BF)] * 5 + [jax.ShapeDtypeStruct((T, 2 * D), BF),
                                                            jax.ShapeDtypeStruct((1, D), F32)],
        compiler_params=_cp(("arbitrary",)),
    )(dx2, gt, o, proj, proj, proj, proj, yc, ya, wout, wcp, wap)


def _conv_bwd(dgc, proj, convw, *, tm, name):
    T = dgc.shape[0]
    tm = _tile(T, tm)
    hb = tm // HALO
    nblk = T // HALO
    nt = T // tm

    def body(dgc_ref, ndgc_ref, bg_ref, nbg_ref, cg_ref, u_ref, hcg_ref, hu_ref, cw_ref, dp_ref, dcw_ref):
        i = pl.program_id(0)

        @pl.when(i == 0)
        def _():
            dcw_ref[...] = jnp.zeros_like(dcw_ref)
        first = jnp.where(i == 0, 0.0, 1.0)
        last = jnp.where(i == nt - 1, 0.0, 1.0)
        cg = cg_ref[...].astype(F32)
        u = u_ref[...].astype(F32)
        bg = bg_ref[...].astype(F32)
        dg = dgc_ref[...].astype(F32)
        cu = cg * u
        hprev = first * (hcg_ref[...].astype(F32) * hu_ref[...].astype(F32))
        m1, m2 = _conv_shifts(cu, hprev, tm)
        w0, w1, w2 = cw_ref[0:1, :], cw_ref[1:2, :], cw_ref[2:3, :]
        cv = w0 * m2 + w1 * m1 + w2 * cu
        dcv = dg * bg
        nxt = last * (ndgc_ref[...].astype(F32) * nbg_ref[...].astype(F32))
        n0, n1 = nxt[0:1, :], nxt[1:2, :]
        row = lax.broadcasted_iota(jnp.int32, dcv.shape, 0)
        p1 = jnp.where(row == tm - 1, n0, pltpu.roll(dcv, tm - 1, 0))
        p2 = jnp.where(row == tm - 1, n1, jnp.where(row == tm - 2, n0, pltpu.roll(dcv, tm - 2, 0)))
        dcu = w2 * dcv + w1 * p1 + w0 * p2
        dp_ref[:, 0:D] = (dg * cv).astype(BF)
        dp_ref[:, D:2 * D] = (dcu * u).astype(BF)
        dp_ref[:, 2 * D:3 * D] = (dcu * cg).astype(BF)
        dcw_ref[0:1, :] += jnp.sum(dcv * m2, axis=0, keepdims=True)
        dcw_ref[1:2, :] += jnp.sum(dcv * m1, axis=0, keepdims=True)
        dcw_ref[2:3, :] += jnp.sum(dcv * cu, axis=0, keepdims=True)

    rowspec = pl.BlockSpec((tm, D), _row)
    cw = pl.BlockSpec((8, D), _const2)
    return pl.pallas_call(
        body, name=name, grid=(nt,),
        in_specs=[rowspec, _halo_next(hb, nblk), _col(tm, O_BG), _halo_next(hb, nblk, O_BG),
                  _col(tm, O_CG), _col(tm, O_U), _halo_prev(hb, O_CG), _halo_prev(hb, O_U), cw],
        out_specs=[pl.BlockSpec((tm, 3 * D), _row), cw],
        out_shape=[jax.ShapeDtypeStruct((T, 3 * D), BF), jax.ShapeDtypeStruct((8, D), F32)],
        compiler_params=_cp(("arbitrary",)),
    )(dgc, dgc, proj, proj, proj, proj, proj, proj, convw)


def _gsum(parts, *, tm, name):
    _, R, C = parts.shape
    tm = _tile(R, tm)

    def body(p_ref, o_ref):
        acc = p_ref[0].astype(F32)
        for s in range(1, N_DEV):
            acc = acc + p_ref[s].astype(F32)
        o_ref[...] = acc

    return pl.pallas_call(
        body, name=name, grid=(R // tm,),
        in_specs=[pl.BlockSpec((N_DEV, tm, C), lambda i: (0, i, 0))],
        out_specs=pl.BlockSpec((tm, C), _row),
        out_shape=jax.ShapeDtypeStruct((R, C), F32),
        compiler_params=_cp(("parallel",)),
    )(parts)


def _adam(w, g, m, v, *, tm, name):
    R, C = w.shape
    tm = _tile(R, tm)
    c1 = 1.0 - ADAM_B1
    c2 = 1.0 - ADAM_B2
    bc1 = 1.0 - ADAM_B1 ** ADAM_STEP
    bc2 = 1.0 - ADAM_B2 ** ADAM_STEP

    def body(w_ref, g_ref, m_ref, v_ref, d_ref, nm_ref, nv_ref):
        gv = g_ref[...]
        nm = ADAM_B1 * m_ref[...] + c1 * gv
        nv = ADAM_B2 * v_ref[...] + c2 * (gv * gv)
        nm_ref[...] = nm
        nv_ref[...] = nv
        d_ref[...] = -ADAM_LR * ((nm / bc1) / (jnp.sqrt(nv / bc2) + ADAM_EPS) + ADAM_WD * w_ref[...])

    spec = pl.BlockSpec((tm, C), _row)
    return pl.pallas_call(
        body, name=name, grid=(R // tm,),
        in_specs=[spec] * 4, out_specs=[spec] * 3,
        out_shape=[jax.ShapeDtypeStruct((R, C), F32)] * 3,
        compiler_params=_cp(("parallel",)),
    )(w, g, m, v)


def _mods_part(c_all, w_ada, b_ada, *, name):
    C = w_ada.shape[1]

    def body(c_ref, w_ref, b_ref, o_ref):
        cv = c_ref[...]
        ca = cv * jax.nn.sigmoid(cv)
        o_ref[...] = jnp.dot(ca, w_ref[...], preferred_element_type=F32,
                             precision=lax.Precision.HIGHEST) + b_ref[...]

    return pl.pallas_call(
        body, name=name,
        out_shape=jax.ShapeDtypeStruct((N_DEV, C), F32),
        compiler_params=_cp(),
    )(c_all, w_ada, b_ada)


def _wada_grad(c_all_t, gm, *, name):
    C = gm.shape[1]

    def body(c_ref, g_ref, o_ref):
        cv = c_ref[...]
        ca = cv * jax.nn.sigmoid(cv)
        acc = ca[:, 0:1] * g_ref[0:1, :]
        for b in range(1, N_DEV):
            acc = acc + ca[:, b:b + 1] * g_ref[b:b + 1, :]
        o_ref[...] = acc

    return pl.pallas_call(
        body, name=name,
        out_shape=jax.ShapeDtypeStruct((D, C), F32),
        compiler_params=_cp(),
    )(c_all_t, gm)


def _peer(x, y, c, d):
    px = lax.rem(x + ((d >> 2) & 1), 2)
    py = lax.rem(y + ((d >> 1) & 1), 2)
    pc = lax.rem(c + (d & 1), 2)
    return (px, py, pc), 4 * px + 2 * py + pc


def _exchange(xs, *, scatter, name):
    n = len(xs)
    nsem = n * (N_DEV - 1)

    def body(*refs):
        ins, outs = refs[:n], refs[n:2 * n]
        token, send_sems, recv_sems, local_sems = refs[2 * n:]
        x, y, c = lax.axis_index("x"), lax.axis_index("y"), lax.axis_index("c")
        me = 4 * x + 2 * y + c
        token[...] = jnp.zeros_like(token)

        def src(t, idx):
            return ins[t].at[idx] if scatter else ins[t]

        local = [pltpu.make_async_copy(src(t, me), outs[t].at[me], local_sems.at[t]) for t in range(n)]
        for cp in local:
            cp.start()
        remote = []
        for t in range(n):
            for d in range(1, N_DEV):
                peer, pidx = _peer(x, y, c, d)
                k = t * (N_DEV - 1) + d - 1
                send = pltpu.make_async_remote_copy(src_ref=src(t, pidx), dst_ref=outs[t].at[me],
                                                    send_sem=send_sems.at[k], recv_sem=recv_sems.at[k],
                                                    device_id=peer, device_id_type=MESH)
                recv = pltpu.make_async_remote_copy(src_ref=src(t, pidx), dst_ref=outs[t].at[pidx],
                                                    send_sem=send_sems.at[k], recv_sem=recv_sems.at[k],
                                                    device_id=peer, device_id_type=MESH)
                send.start()
                remote.append((send, recv))
        for cp in local:
            cp.wait()
        for send, recv in remote:
            send.wait_send()
            recv.wait_recv()

    anyspec = pl.BlockSpec(memory_space=pl.ANY)
    out_shape = [jax.ShapeDtypeStruct(a.shape if scatter else (N_DEV,) + a.shape, a.dtype) for a in xs]
    out_shape.append(jax.ShapeDtypeStruct((8, 128), F32))
    return pl.pallas_call(
        body, name=name,
        in_specs=[anyspec] * n, out_specs=[anyspec] * n + [pl.BlockSpec(memory_space=pltpu.VMEM)],
        out_shape=out_shape,
        scratch_shapes=[pltpu.SemaphoreType.DMA((nsem,)), pltpu.SemaphoreType.DMA((nsem,)),
                        pltpu.SemaphoreType.DMA((n,))],
    )(*xs)


def _sum8(parts, *, name):
    _, R, C = parts.shape

    def body(p_ref, o_ref):
        acc = p_ref[0]
        for s in range(1, N_DEV):
            acc = acc + p_ref[s]
        o_ref[...] = acc

    return pl.pallas_call(body, name=name, out_shape=jax.ShapeDtypeStruct((R, C), F32),
                          compiler_params=_cp())(parts)


HBM_SPEC = pl.BlockSpec(memory_space=pltpu.HBM)
SEM_SPEC = pl.BlockSpec(memory_space=pltpu.SEMAPHORE)
N_PEER = N_DEV - 1


def _split_copies(src_refs, land_refs, send_sems, recv_sems, scatter):
    x, y, c = lax.axis_index("x"), lax.axis_index("y"), lax.axis_index("c")
    me = 4 * x + 2 * y + c
    pairs = []
    for j, (src, land) in enumerate(zip(src_refs, land_refs)):
        for d in range(1, N_DEV):
            peer, pidx = _peer(x, y, c, d)
            k = j * N_PEER + d - 1
            s = src.at[pidx] if scatter else src
            send = pltpu.make_async_remote_copy(src_ref=s, dst_ref=land.at[me], send_sem=send_sems.at[k],
                                                recv_sem=recv_sems.at[k], device_id=peer, device_id_type=MESH)
            recv = pltpu.make_async_remote_copy(src_ref=s, dst_ref=land.at[pidx], send_sem=send_sems.at[k],
                                                recv_sem=recv_sems.at[k], device_id=peer, device_id_type=MESH)
            pairs.append((send, recv))
    return pairs


def _own_slot(block, me):
    land = lax.empty((N_DEV,) + block.shape, block.dtype)
    return lax.dynamic_update_slice(land, block[None], (me, 0, 0))


def _split_start(srcs, lands, groups, *, scatter, name):
    n, ng = len(srcs), len(groups)

    def body(*refs):
        src_refs, land_refs = refs[:n], refs[n:2 * n]
        sems = refs[2 * n:2 * n + 2 * ng]
        token = refs[-1]
        for gi, g in enumerate(groups):
            pairs = _split_copies([src_refs[t] for t in g], [land_refs[t] for t in g], sems[2 * gi],
                                  sems[2 * gi + 1], scatter)
            for send, _ in pairs:
                send.start()
        token[...] = jnp.zeros_like(token)

    sem_shapes = []
    for g in groups:
        sem_shapes += [pltpu.SemaphoreType.DMA((len(g) * N_PEER,))] * 2
    thru = [pltpu.HBM(a.shape, a.dtype) for a in list(srcs) + list(lands)]
    outs = pl.pallas_call(
        body, name=name,
        out_shape=tuple(sem_shapes + thru + [jax.ShapeDtypeStruct((8, 128), F32)]),
        in_specs=[HBM_SPEC] * (2 * n),
        out_specs=tuple([SEM_SPEC] * (2 * ng) + [HBM_SPEC] * (2 * n) + [pl.BlockSpec(memory_space=pltpu.VMEM)]),
        input_output_aliases={i: 2 * ng + i for i in range(2 * n)},
        compiler_params=pltpu.CompilerParams(has_side_effects=pltpu.SideEffectType.DATAFLOW_SIDE_EFFECTING),
    )(*[pltpu.with_memory_space_constraint(a, pltpu.HBM) for a in list(srcs) + list(lands)])
    sems = [(outs[2 * gi], outs[2 * gi + 1]) for gi in range(ng)]
    return sems, outs[2 * ng:2 * ng + n], outs[2 * ng + n:2 * ng + 2 * n], outs[-1]


def _behind(v, token):
    if token is None:
        return v
    return v + token[0, 0].astype(v.dtype)


def _split_wait(srcs, lands, sems, after, *, scatter, name):
    m = len(srcs)

    def body(*refs):
        src_refs, land_refs = refs[:m], refs[m:2 * m]
        send_sems, recv_sems = refs[2 * m], refs[2 * m + 1]
        for send, recv in _split_copies(src_refs, land_refs, send_sems, recv_sems, scatter):
            send.wait_send()
            recv.wait_recv()

    outs = pl.pallas_call(
        body, name=name,
        out_shape=tuple(pltpu.HBM(a.shape, a.dtype) for a in list(srcs) + list(lands)),
        in_specs=[HBM_SPEC] * (2 * m) + [SEM_SPEC, SEM_SPEC, pl.BlockSpec(memory_space=pl.ANY)],
        out_specs=tuple([HBM_SPEC] * (2 * m)),
        input_output_aliases={i: i for i in range(2 * m)},
        compiler_params=pltpu.CompilerParams(has_side_effects=pltpu.SideEffectType.DATAFLOW_SIDE_EFFECTING),
    )(*srcs, *lands, sems[0], sems[1], after)
    return outs[m:]


TM_PROJ = 1024
TM_ROW = 512
TM_NN = 512
TK_TN = 512
TN_FFN = F // 2
TN_IN = NIN // 4


def _tn(a, b, name, tn):
    if a.ndim == 2:
        a = a[None]
    return _tn_matmul(a, b, tn=tn, tk=TK_TN, name=name)


def _local_step(x, tgt, mods, g1, gm, g2, gf, convw8, sinks, w_get, g_put):
    T = x.shape[0]
    sh1, sc1, gt1, sh2, sc2, gt2, sh3, sc3, gt3 = [mods[i:i + 1] for i in range(N_MOD)]
    cos, sin = _rope_tables(T)
    behind = _behind

    w = dict(w_get("gu1", mods))
    h1, ab1 = _norm_proj(x, g1, sc1, sh1, w["gu1"], tm=TM_PROJ, tn=TN_FFN, name="ffn1_up")
    w.update(w_get("d1", ab1))
    x1, y1 = _ffn_down_fwd(ab1, w["d1"], x, gt1, tm=TM_ROW, name="ffn1_down")
    w.update(w_get("mix", x1))
    h2, proj = _norm_proj(x1, gm, sc2, sh2, w["win"], tm=TM_PROJ, tn=TN_IN, name="mix_in")
    attn = _attn_fwd(proj, cos, sin, sinks, name="attn_fwd")
    x2, gc, yc, ya, mg, o = _mixer_mid_fwd(proj, attn, w["cp"], w["ap"], w["out"], convw8, x1, gt2,
                                           tm=TM_ROW, name="mix_mid")
    w.update(w_get("ffn2", x2))
    h3, ab2 = _norm_proj(x2, g2, sc3, sh3, w["gu2"], tm=TM_PROJ, tn=TN_FFN, name="ffn2_up")
    x3, y2 = _ffn_down_fwd(ab2, w["d2"], x2, gt3, tm=TM_ROW, name="ffn2_down")
    dx3, lsum, dgf = _final_fwd_bwd(x3, tgt, gf, tm=TM_ROW, name="final")

    dy2, dab2, dgt3 = _ffn_down_bwd(dx3, y2, gt3, ab2, w["d2"], tm=TM_ROW, tn=TN_FFN, name="ffn2_down_bwd")
    g_d2 = _tn_matmul_swiglu(ab2, dy2, None, tn=TN_FFN, tk=TK_TN, name="ffn2_down_dw")
    dx2, dsh3, dsc3, dg2 = _nn_bwd_norm(dab2, w["gu2"], x2, g2, sc3, dx3, tm=TM_NN, tk=TN_FFN, name="ffn2_up_bwd")
    g_gu2 = _tn(dab2, h3, "ffn2_up_dw", TN_FFN)
    tok = g_put(dict(gu2=g_gu2, d2=g_d2))

    dout, dyc, dya, dgc, dat, dz, dgt2 = _mixer_mid_bwd(dx2, behind(gt2, tok), o, proj, yc, ya, w["out"], w["cp"],
                                                        w["ap"], tm=TM_ROW, name="mix_mid_bwd")
    g_out = _tn(mg, dout, "mix_out_dw", D)
    g_cp = _tn(gc, dyc, "mix_cp_dw", D)
    g_ap = _tn(attn, dya, "mix_ap_dw", D)
    dq, dkc, dkp, dvc, dvp, dsink = _attn_bwd(proj, cos, sin, sinks, attn, dat, name="attn_bwd")
    dkv = _dkv_combine(dkc, dkp, dvc, dvp, name="attn_dkv")
    dp1, dcw = _conv_bwd(dgc, proj, convw8, tm=TM_ROW, name="conv_bwd")
    dproj = jnp.concatenate([dp1, dq, dkv, dz], axis=1)
    g_in = _tn(dproj, h2, "mix_in_dw", TN_IN)
    tok = g_put(dict(win=g_in, cp=g_cp, ap=g_ap, out=g_out))
    dx1, dsh2, dsc2, dgm = _nn_bwd_norm(dproj[None], w["win"], x1, gm, behind(sc2, tok), dx2, tm=TM_NN, tk=TN_IN,
                                        name="mix_in_bwd")

    dy1, dab1, dgt1 = _ffn_down_bwd(dx1, y1, gt1, ab1, w["d1"], tm=TM_ROW, tn=TN_FFN, name="ffn1_down_bwd")
    g_gu1 = _tn(dab1, h1, "ffn1_up_dw", TN_FFN)
    tok = g_put(dict(gu1=g_gu1))
    g_d1 = _tn_matmul_swiglu(ab1, dy1, tok, tn=TN_FFN, tk=TK_TN, name="ffn1_down_dw")
    tok = g_put(dict(d1=g_d1))
    dx0, dsh1, dsc1, dg1 = _nn_bwd_norm(dab1, w["gu1"], x, g1, behind(sc1, tok), dx1, tm=TM_NN, tk=TN_FFN,
                                        name="ffn1_up_bwd")

    small = dict(mods=jnp.concatenate([dsh1, dsc1, dgt1, dsh2, dsc2, dgt2, dsh3, dsc3, dgt3], axis=0),
                 g1=dg1, gm=dgm, g2=dg2, gf=dgf, convw=dcw[0:3], sinks=dsink[:, 0:N_HEADS])
    return lsum, dx0, small


BIG = ("gu1", "d1", "win", "cp", "ap", "out", "gu2", "d2")
TRANSPOSED = ("gu1", "win", "gu2")
SMALL_ROWS = 24
R_MODS, R_G1, R_GM, R_G2, R_GF, R_CONV, R_SINK = 0, 9, 10, 11, 12, 13, 16


def _pad_to(a, rows, cols):
    return jnp.pad(a, ((0, rows - a.shape[0]), (0, cols - a.shape[1])))


def _pack_small(b_ada, g1, gm, g2, gf, conv, sinks):
    rows = [b_ada.reshape(N_MOD, D), g1.reshape(1, D), gm.reshape(1, D), g2.reshape(1, D), gf.reshape(1, D),
            _pad_to(conv.reshape(3, -1), 3, D), _pad_to(sinks.reshape(1, N_HEADS), 1, D)]
    return _pad_to(jnp.concatenate(rows, axis=0), SMALL_ROWS, D)


def _unpack_small(p, conv_cols):
    return dict(b_ada=p[R_MODS:R_MODS + N_MOD].reshape(1, N_MOD * D), g_ffn1=p[R_G1:R_G1 + 1],
                g_mix=p[R_GM:R_GM + 1], g_ffn2=p[R_G2:R_G2 + 1], g_final=p[R_GF],
                conv_w=p[R_CONV:R_CONV + 3, 0:conv_cols][None], sinks=p[R_SINK:R_SINK + 1, 0:N_HEADS])


def kernel(x, c, w_ada, b_ada, g_ffn1, w1_gu, w1_down, g_mix, w_in, conv_w, w_conv_proj, w_attn_proj, sinks, w_out, g_ffn2, w2_gu, w2_down, g_final, loss_target, m_w_ada, m_b_ada, m_g_ffn1, m_w1_gu, m_w1_down, m_g_mix, m_w_in, m_conv_w, m_w_conv_proj, m_w_attn_proj, m_sinks, m_w_out, m_g_ffn2, m_w2_gu, m_w2_down, m_g_final, v_w_ada, v_b_ada, v_g_ffn1, v_w1_gu, v_w1_down, v_g_mix, v_w_in, v_conv_w, v_w_conv_proj, v_w_attn_proj, v_sinks, v_w_out, v_g_ffn2, v_w2_gu, v_w2_down, v_g_final):
    me = 4 * lax.axis_index("x") + 2 * lax.axis_index("y") + lax.axis_index("c")
    ada_cols = w_ada.shape[2]
    conv_cols = conv_w.shape[2]

    native = dict(gu1=w1_gu[0], d1=w1_down[0], win=w_in[0], cp=w_conv_proj[0], ap=w_attn_proj[0], out=w_out[0],
                  gu2=w2_gu[0], d2=w2_down[0])

    def shard(n, token):
        a = _behind(native[n], token)
        return (a.T if n in TRANSPOSED else a).astype(BF)

    c_all, conv_all, _ = _exchange([c, _pad_to(conv_w[0], 8, conv_cols)], scatter=False, name="gather_cond")
    c_all = c_all.reshape(N_DEV, D)
    conv_full = conv_all[:, 0:3, :].transpose(1, 0, 2).reshape(3, D)

    b_cols = lax.dynamic_slice(b_ada, (0, me * ada_cols), (1, ada_cols))
    mods_cols = _mods_part(c_all, w_ada[0], b_cols, name="ada_mods")
    mods_all, mods_token = _exchange([mods_cols], scatter=False, name="gather_mods")
    mods = lax.dynamic_index_in_dim(mods_all, me, axis=1, keepdims=False).reshape(N_MOD, D)

    groups = dict(gu1=("gu1",), d1=("d1",), mix=("win", "cp", "ap", "out"), ffn2=("gu2", "d2"))
    in_flight = {}
    first = [shard("gu1", mods_token)]
    sems, srcs, lands, token = _split_start(first, [_own_slot(s, me) for s in first], [[0]], scatter=False,
                                            name="gather_weights_start_gu1")
    in_flight["gu1"] = (sems[0], srcs, lands)
    rest = [n for n in BIG if n != "gu1"]
    shards = [shard(n, token) for n in rest]
    rest_groups = [[rest.index(n) for n in names] for g, names in groups.items() if g != "gu1"]
    sems, srcs, lands, _ = _split_start(shards, [_own_slot(s, me) for s in shards], rest_groups, scatter=False,
                                        name="gather_weights_start_rest")
    for (g, names), gsems, idx in zip([kv for kv in groups.items() if kv[0] != "gu1"], sems, rest_groups):
        in_flight[g] = (gsems, [srcs[t] for t in idx], [lands[t] for t in idx])

    def w_get(group, after):
        gsems, gsrcs, glands = in_flight[group]
        landed = _split_wait(gsrcs, glands, gsems, after, scatter=False, name="gather_weights_wait_" + group)
        return {n: a.reshape(-1, D) for n, a in zip(groups[group], landed)}

    pending = []

    def g_put(gs):
        names = tuple(gs)
        srcs = [gs[n].reshape(N_DEV, -1, D) for n in names]
        lands = [_own_slot(lax.dynamic_index_in_dim(s, me, axis=0, keepdims=False), me) for s in srcs]
        sems, srcs, lands, tok = _split_start(srcs, lands, [list(range(len(names)))], scatter=True,
                                              name="scatter_grads_start_" + names[0])
        pending.append((names, sems[0], srcs, lands))
        return tok

    lsum, grad_x, small = _local_step(x[0], loss_target[0], mods, g_ffn1, g_mix, g_ffn2, g_final[None],
                                      _pad_to(conv_full, 8, D), sinks[0], w_get, g_put)
    loss = lax.psum((0.5 / D) * jnp.sum(lsum), ("x", "y", "c"))

    packed = _pack_small(small["mods"], small["g1"], small["gm"], small["g2"], small["gf"], small["convw"],
                         small["sinks"])
    packed_all, _ = _exchange([packed], scatter=False, name="gather_small")
    gsmall = _sum8(packed_all, name="sum_small")

    grads = {}
    after = gsmall
    for names, sems, srcs, lands in pending:
        parts = _split_wait(srcs, lands, sems, after, scatter=True, name="scatter_grads_wait_" + names[0])
        for n, p in zip(names, parts):
            g = _gsum(p, tm=128, name="gsum_" + n)
            grads[n] = g.T if n in TRANSPOSED else g
        after = g

    gm_cols = lax.dynamic_slice(packed_all[:, R_MODS:R_MODS + N_MOD, :].reshape(N_DEV, N_MOD * D),
                                (0, me * ada_cols), (N_DEV, ada_cols))
    grads["ada"] = _wada_grad(c_all.T, gm_cols, name="ada_dw")
    conv_g = lax.dynamic_slice(gsmall[R_CONV:R_CONV + 3], (0, me * conv_cols), (3, conv_cols))
    gsmall_own = gsmall.at[R_CONV:R_CONV + 3].set(_pad_to(conv_g, 3, D))
    g_small = _unpack_small(gsmall_own, conv_cols)

    w_of = dict(ada=w_ada, gu1=w1_gu, d1=w1_down, win=w_in, cp=w_conv_proj, ap=w_attn_proj, out=w_out, gu2=w2_gu,
                d2=w2_down)
    m_of = dict(ada=m_w_ada, gu1=m_w1_gu, d1=m_w1_down, win=m_w_in, cp=m_w_conv_proj, ap=m_w_attn_proj, out=m_w_out,
                gu2=m_w2_gu, d2=m_w2_down)
    v_of = dict(ada=v_w_ada, gu1=v_w1_gu, d1=v_w1_down, win=v_w_in, cp=v_w_conv_proj, ap=v_w_attn_proj, out=v_w_out,
                gu2=v_w2_gu, d2=v_w2_down)
    upd = {n: _adam(w_of[n][0], grads[n], m_of[n][0], v_of[n][0], tm=128, name="adam_" + n) for n in w_of}
    small_upd = _adam(_pack_small(b_ada, g_ffn1, g_mix, g_ffn2, g_final, conv_w, sinks), gsmall_own,
                      _pack_small(m_b_ada, m_g_ffn1, m_g_mix, m_g_ffn2, m_g_final, m_conv_w, m_sinks),
                      _pack_small(v_b_ada, v_g_ffn1, v_g_mix, v_g_ffn2, v_g_final, v_conv_w, v_sinks),
                      tm=SMALL_ROWS, name="adam_small")
    small_out = [g_small] + [_unpack_small(p, conv_cols) for p in small_upd]

    big_name = dict(w_ada="ada", w1_gu="gu1", w1_down="d1", w_in="win", w_conv_proj="cp", w_attn_proj="ap",
                    w_out="out", w2_gu="gu2", w2_down="d2")
    order = ("w_ada", "b_ada", "g_ffn1", "w1_gu", "w1_down", "g_mix", "w_in", "conv_w", "w_conv_proj", "w_attn_proj",
             "sinks", "w_out", "g_ffn2", "w2_gu", "w2_down", "g_final")
    outs = [loss, grad_x[None]]
    for kind in range(4):
        for n in order:
            if n in big_name:
                t = grads[big_name[n]] if kind == 0 else upd[big_name[n]][kind - 1]
                outs.append(t[None])
            else:
                outs.append(small_out[kind][n])
    return tuple(outs)
```

```python
import functools

import jax
import jax.numpy as jnp
from jax import lax
from jax.experimental import pallas as pl
from jax.experimental.pallas import tpu as pltpu

D = 1024
F = 2816
NIN = 6656
N_HEADS = 16
N_KV = 4
HEAD_DIM = 64
BLK = 128
N_MOD = 9
N_DEV = 8
EPS = 1e-6
NEG_INF = -1e30
ROPE_THETA = 10000.0
O_BG, O_CG, O_U, O_Q, O_K, O_V, O_ZC, O_ZA = 0, 1024, 2048, 3072, 4096, 4352, 4608, 5632

ADAM_LR = 0.001
ADAM_B1 = 0.9
ADAM_B2 = 0.999
ADAM_EPS = 1e-08
ADAM_WD = 0.01
ADAM_STEP = 10

BF = jnp.bfloat16
F32 = jnp.float32
VMEM_LIMIT = 56 * 1024 * 1024
MXU_N = 256
MESH = pl.DeviceIdType.MESH

NT = (((1,), (1,)), ((), ()))
TN = (((0,), (0,)), ((), ()))


def _cp(sem=None):
    return pltpu.CompilerParams(dimension_semantics=sem, vmem_limit_bytes=VMEM_LIMIT)


def _tile(n, pref):
    if n <= pref:
        return n
    for t in range(pref - pref % 16, 15, -16):
        if n % t == 0:
            return t
    raise ValueError((n, pref))


def _sigmoid(v):
    return 0.5 * jnp.tanh(0.5 * v) + 0.5


def _row(i):
    return (i, 0)


def _const2(*_):
    return (0, 0)


def _norm_proj(x, g, sc, sh, wt, *, tm, tn, name):
    T, N = x.shape[0], wt.shape[0]
    tm, tn = _tile(T, tm), _tile(N, tn)

    def body(x_ref, g_ref, sc_ref, sh_ref, w_ref, h_ref, o_ref, hs):
        @pl.when(pl.program_id(1) == 0)
        def _():
            xv = x_ref[...]
            r = lax.rsqrt(jnp.mean(xv * xv, axis=-1, keepdims=True) + EPS)
            hb = ((xv * r) * g_ref[...] * (1.0 + sc_ref[...]) + sh_ref[...]).astype(BF)
            hs[...] = hb
            h_ref[...] = hb
        o_ref[...] = lax.dot_general(hs[...], w_ref[...], NT, preferred_element_type=F32).astype(BF)

    vec = pl.BlockSpec((1, D), _const2)
    return pl.pallas_call(
        body, name=name, grid=(T // tm, N // tn),
        in_specs=[pl.BlockSpec((tm, D), lambda i, j: (i, 0)), vec, vec, vec,
                  pl.BlockSpec((tn, D), lambda i, j: (j, 0))],
        out_specs=[pl.BlockSpec((tm, D), lambda i, j: (i, 0)), pl.BlockSpec((tm, tn), lambda i, j: (i, j))],
        out_shape=[jax.ShapeDtypeStruct((T, D), BF), jax.ShapeDtypeStruct((T, N), BF)],
        scratch_shapes=[pltpu.VMEM((tm, D), BF)],
        compiler_params=_cp(("parallel", "arbitrary")),
    )(x, g, sc, sh, wt)


def _ffn_down_fwd(ab, wd, x, gt, *, tm, name):
    T = x.shape[0]
    tm = _tile(T, tm)

    def body(a_ref, b_ref, wd_ref, x_ref, gt_ref, xo_ref, y_ref):
        y = None
        for c0 in range(0, F, MXU_N):
            cols = pl.ds(c0, MXU_N)
            a = a_ref[:, cols].astype(F32)
            act = (a * _sigmoid(a) * b_ref[:, cols].astype(F32)).astype(BF)
            part = jnp.dot(act, wd_ref[cols, :], preferred_element_type=F32)
            y = part if y is None else y + part
        y_ref[...] = y.astype(BF)
        xo_ref[...] = x_ref[...] + (0.5 * gt_ref[...]) * y

    return pl.pallas_call(
        body, name=name, grid=(T // tm,),
        in_specs=[pl.BlockSpec((tm, F), lambda i: (i, 0)), pl.BlockSpec((tm, F), lambda i: (i, 1)),
                  pl.BlockSpec((F, D), _const2), pl.BlockSpec((tm, D), _row), pl.BlockSpec((1, D), _const2)],
        out_specs=[pl.BlockSpec((tm, D), _row), pl.BlockSpec((tm, D), _row)],
        out_shape=[jax.ShapeDtypeStruct((T, D), F32), jax.ShapeDtypeStruct((T, D), BF)],
        compiler_params=_cp(("parallel",)),
    )(ab, ab, wd, x, gt)


def _final_fwd_bwd(x, tgt, g, *, tm, name):
    T = x.shape[0]
    tm = _tile(T, tm)

    def body(x_ref, t_ref, g_ref, dx_ref, ls_ref, dg_ref):
        @pl.when(pl.program_id(0) == 0)
        def _():
            ls_ref[...] = jnp.zeros_like(ls_ref)
            dg_ref[...] = jnp.zeros_like(dg_ref)
        xv = x_ref[...]
        gv = g_ref[...]
        r = lax.rsqrt(jnp.mean(xv * xv, axis=-1, keepdims=True) + EPS)
        xh = xv * r
        e = xh * gv - t_ref[...]
        ls_ref[...] += jnp.sum(e * e, axis=0, keepdims=True)
        dy = e * (1.0 / D)
        dg_ref[...] += jnp.sum(dy * xh, axis=0, keepdims=True)
        dxh = dy * gv
        dx_ref[...] = r * (dxh - xh * jnp.mean(dxh * xh, axis=-1, keepdims=True))

    vec = pl.BlockSpec((1, D), _const2)
    return pl.pallas_call(
        body, name=name, grid=(T // tm,),
        in_specs=[pl.BlockSpec((tm, D), _row), pl.BlockSpec((tm, D), _row), vec],
        out_specs=[pl.BlockSpec((tm, D), _row), vec, vec],
        out_shape=[jax.ShapeDtypeStruct((T, D), F32), jax.ShapeDtypeStruct((1, D), F32),
                   jax.ShapeDtypeStruct((1, D), F32)],
        compiler_params=_cp(("arbitrary",)),
    )(x, tgt, g)


def _ffn_down_bwd(dxo, y, gt, ab, wd, *, tm, tn, name):
    T = dxo.shape[0]
    tm = _tile(T, tm)

    def body(dxo_ref, y_ref, gt_ref, a_ref, b_ref, wd_ref, dy_ref, dab_ref, dgt_ref):
        @pl.when(pl.program_id(0) == 0)
        def _():
            dgt_ref[...] = jnp.zeros_like(dgt_ref)

        dxv = dxo_ref[...]
        dgt_ref[...] += 0.5 * jnp.sum(dxv * y_ref[...].astype(F32), axis=0, keepdims=True)
        dy = ((0.5 * gt_ref[...]) * dxv).astype(BF)
        dy_ref[...] = dy
        for c0 in range(0, F, tn):
            cols = pl.ds(c0, tn)
            dact = lax.dot_general(dy, wd_ref[cols, :], NT, preferred_element_type=F32)
            a = a_ref[:, cols].astype(F32)
            b = b_ref[:, cols].astype(F32)
            s = _sigmoid(a)
            dab_ref[0, :, cols] = (dact * b * (s * (1.0 + a * (1.0 - s)))).astype(BF)
            dab_ref[1, :, cols] = (dact * (a * s)).astype(BF)

    vec = pl.BlockSpec((1, D), _const2)
    rowspec = pl.BlockSpec((tm, D), _row)
    return pl.pallas_call(
        body, name=name, grid=(T // tm,),
        in_specs=[rowspec, rowspec, vec, pl.BlockSpec((tm, F), lambda i: (i, 0)),
                  pl.BlockSpec((tm, F), lambda i: (i, 1)), pl.BlockSpec((F, D), _const2)],
        out_specs=[rowspec, pl.BlockSpec((2, tm, F), lambda i: (0, i, 0)), vec],
        out_shape=[jax.ShapeDtypeStruct((T, D), BF), jax.ShapeDtypeStruct((2, T, F), BF),
                   jax.ShapeDtypeStruct((1, D), F32)],
        compiler_params=_cp(("arbitrary",)),
    )(dxo, y, gt, ab, ab, wd)


def _tn_matmul(a, b, *, tn, tk, name):
    S, T, Ns = a.shape
    tn, tk = _tile(Ns, tn), _tile(T, tk)
    nk, njs = T // tk, Ns // tn

    def body(a_ref, b_ref, o_ref, acc):
        k = pl.program_id(1)

        @pl.when(k == 0)
        def _():
            acc[...] = jnp.zeros_like(acc)
        acc[...] += lax.dot_general(a_ref[0], b_ref[...], TN, preferred_element_type=F32)

        @pl.when(k == nk - 1)
        def _():
            o_ref[...] = acc[...].astype(BF)

    return pl.pallas_call(
        body, name=name, grid=(S * njs, nk),
        in_specs=[pl.BlockSpec((1, tk, tn), lambda j, k: (j // njs, k, j % njs)),
                  pl.BlockSpec((tk, D), lambda j, k: (k, 0))],
        out_specs=pl.BlockSpec((tn, D), lambda j, k: (j, 0)),
        out_shape=jax.ShapeDtypeStruct((S * Ns, D), BF),
        scratch_shapes=[pltpu.VMEM((tn, D), F32)],
        compiler_params=_cp(("parallel", "arbitrary")),
    )(a, b)


def _tn_matmul_swiglu(ab, b, token, *, tn, tk, name):
    T = ab.shape[0]
    tn, tk = _tile(F, tn), _tile(T, tk)
    nk, nj = T // tk, F // tn
    deps = [] if token is None else [token]

    def body(a_ref, g_ref, b_ref, *rest):
        o_ref, acc = rest[len(deps):]
        k = pl.program_id(1)

        @pl.when(k == 0)
        def _():
            acc[...] = jnp.zeros_like(acc)
        bv = b_ref[...]
        for c0 in range(0, tn, MXU_N):
            cw = min(MXU_N, tn - c0)
            cols = pl.ds(c0, cw)
            a = a_ref[:, cols].astype(F32)
            act = (a * _sigmoid(a) * g_ref[:, cols].astype(F32)).astype(BF)
            acc[cols, :] += lax.dot_general(act, bv, TN, preferred_element_type=F32)

        @pl.when(k == nk - 1)
        def _():
            o_ref[...] = acc[...].astype(BF)

    return pl.pallas_call(
        body, name=name, grid=(nj, nk),
        in_specs=[pl.BlockSpec((tk, tn), lambda j, k: (k, j)), pl.BlockSpec((tk, tn), lambda j, k: (k, j + nj)),
                  pl.BlockSpec((tk, D), lambda j, k: (k, 0))] + [pl.BlockSpec(memory_space=pl.ANY)] * len(deps),
        out_specs=pl.BlockSpec((tn, D), lambda j, k: (j, 0)),
        out_shape=jax.ShapeDtypeStruct((F, D), BF),
        scratch_shapes=[pltpu.VMEM((tn, D), F32)],
        compiler_params=_cp(("parallel", "arbitrary")),
    )(ab, ab, b, *deps)


def _nn_bwd_norm(da, w, x, g, sc, dxo, *, tm, tk, name):
    S, T, Ks = da.shape
    tm, tk = _tile(T, tm), _tile(Ks, tk)
    nks = Ks // tk
    nk = S * nks
    rc = _tile(tm, 256)

    def body(da_ref, w_ref, x_ref, g_ref, sc_ref, dxo_ref, dx_ref, dsh_ref, dsc_ref, dg_ref, acc):
        i, k = pl.program_id(0), pl.program_id(1)

        @pl.when(jnp.logical_and(i == 0, k == 0))
        def _():
            dsh_ref[...] = jnp.zeros_like(dsh_ref)
            dsc_ref[...] = jnp.zeros_like(dsc_ref)
            dg_ref[...] = jnp.zeros_like(dg_ref)

        d = jnp.dot(da_ref[0], w_ref[...], preferred_element_type=F32)

        @pl.when(k == 0)
        def _():
            acc[...] = d

        @pl.when(k > 0)
        def _():
            acc[...] += d

        @pl.when(k == nk - 1)
        def _():
            gv = g_ref[...]
            sc1 = 1.0 + sc_ref[...]
            dsh = jnp.zeros((1, D), F32)
            dsc = jnp.zeros((1, D), F32)
            dg = jnp.zeros((1, D), F32)
            for r0 in range(0, tm, rc):
                rows = pl.ds(r0, rc)
                u = acc[rows, :]
                xv = x_ref[rows, :]
                r = lax.rsqrt(jnp.mean(xv * xv, axis=-1, keepdims=True) + EPS)
                xh = xv * r
                dsh = dsh + jnp.sum(u, axis=0, keepdims=True)
                dsc = dsc + jnp.sum(u * (xh * gv), axis=0, keepdims=True)
                us = u * sc1
                dg = dg + jnp.sum(us * xh, axis=0, keepdims=True)
                dxh = us * gv
                dx_ref[rows, :] = dxo_ref[rows, :] + r * (dxh - xh * jnp.mean(dxh * xh, axis=-1, keepdims=True))
            dsh_ref[...] += dsh
            dsc_ref[...] += dsc
            dg_ref[...] += dg

    vec = pl.BlockSpec((1, D), _const2)
    return pl.pallas_call(
        body, name=name, grid=(T // tm, nk),
        in_specs=[pl.BlockSpec((1, tm, tk), lambda i, k: (k // nks, i, k % nks)),
                  pl.BlockSpec((tk, D), lambda i, k: (k, 0)),
                  pl.BlockSpec((tm, D), lambda i, k: (i, 0)), vec, vec,
                  pl.BlockSpec((tm, D), lambda i, k: (i, 0))],
        out_specs=[pl.BlockSpec((tm, D), lambda i, k: (i, 0)), vec, vec, vec],
        out_shape=[jax.ShapeDtypeStruct((T, D), F32)] + [jax.ShapeDtypeStruct((1, D), F32)] * 3,
        scratch_shapes=[pltpu.VMEM((tm, D), F32)],
        compiler_params=_cp(("arbitrary", "arbitrary")),
    )(da, w, x, g, sc, dxo)


def _rope(t, cos, sin_signed, lt32, inverse=False):
    sel = jnp.where(lt32, pltpu.roll(t, 96, 1), pltpu.roll(t, 32, 1))
    return t * cos - sel * sin_signed if inverse else t * cos + sel * sin_signed


def _rope_tables(T):
    inv = 1.0 / (ROPE_THETA ** (jnp.arange(0, HEAD_DIM, 2, dtype=F32) / HEAD_DIM))
    ang = jnp.arange(T, dtype=F32)[:, None] * inv[None, :]
    cos, sin = jnp.cos(ang), jnp.sin(ang)
    cos128 = jnp.tile(cos, (1, 4))
    sin128 = jnp.tile(jnp.concatenate([-sin, sin], axis=1), (1, 2))
    return cos128, sin128


def _attn_specs(nb):
    qspec = pl.BlockSpec((BLK, D), lambda n: (n, O_Q // D))
    kc = pl.BlockSpec((BLK, 256), lambda n: (n, O_K // 256))
    kp = pl.BlockSpec((BLK, 256), lambda n: (jnp.maximum(n - 1, 0), O_K // 256))
    vc = pl.BlockSpec((BLK, 256), lambda n: (n, O_V // 256))
    vp = pl.BlockSpec((BLK, 256), lambda n: (jnp.maximum(n - 1, 0), O_V // 256))
    tc = pl.BlockSpec((BLK, 128), lambda n: (n, 0))
    tp = pl.BlockSpec((BLK, 128), lambda n: (jnp.maximum(n - 1, 0), 0))
    return [qspec, kc, kp, vc, vp, tc, tc, tp, tp, pl.BlockSpec(memory_space=pltpu.SMEM)]


def _attn_common(q_ref, kc_ref, kp_ref, vc_ref, vp_ref, cc_ref, sc_ref, cp_ref, sp_ref):
    n = pl.program_id(0)
    lane = lax.broadcasted_iota(jnp.int32, (BLK, 128), 1)
    lt32 = (lane % HEAD_DIM) < (HEAD_DIM // 2)
    cc, sc, cp, sp = cc_ref[...], sc_ref[...], cp_ref[...], sp_ref[...]
    kr, vr = [], []
    for r in range(2):
        cols = slice(r * 128, (r + 1) * 128)
        kcur = _rope(kc_ref[:, cols].astype(F32), cc, sc, lt32)
        kprev = _rope(kp_ref[:, cols].astype(F32), cp, sp, lt32)
        kr.append(jnp.concatenate([kprev, kcur], axis=0).astype(BF))
        vr.append(jnp.concatenate([vp_ref[:, cols], vc_ref[:, cols]], axis=0))
    qr = [_rope(q_ref[:, p * 128:(p + 1) * 128].astype(F32), cc, sc, lt32) for p in range(8)]
    qi = lax.broadcasted_iota(jnp.int32, (4 * BLK, 2 * BLK), 0) % BLK
    kj = lax.broadcasted_iota(jnp.int32, (4 * BLK, 2 * BLK), 1)
    valid = (kj > qi) & (kj <= qi + BLK) & ((kj >= BLK) | (n > 0))
    halves = [lane < HEAD_DIM, lane >= HEAD_DIM]
    return qr, kr, vr, valid, halves, lt32, (cc, sc, cp, sp)


def _stack_heads(chunks, g, halves):
    half = g % 2
    parts = []
    for hh in range(4):
        h = 4 * g + hh
        t = chunks[h // 2]
        if h % 2 != half:
            t = pltpu.roll(t, HEAD_DIM, 1)
        parts.append(jnp.where(halves[half], t, 0.0))
    return jnp.concatenate(parts, axis=0)


def _softmax_sink(s, valid, sink_ref, g):
    s = jnp.where(valid, s * (HEAD_DIM ** -0.5), NEG_INF)
    sink = jnp.concatenate([jnp.full((BLK, 1), sink_ref[4 * g + hh], F32) for hh in range(4)], axis=0)
    m = jnp.maximum(jnp.max(s, axis=-1, keepdims=True), sink)
    p = jnp.exp(s - m)
    ps = jnp.exp(sink - m)
    inv = 1.0 / (jnp.sum(p, axis=-1, keepdims=True) + ps)
    return p * inv, ps * inv


def _attn_fwd(proj, cos, sin, sinks, *, name):
    T = proj.shape[0]
    nb = T // BLK

    def body(q_ref, kc_ref, kp_ref, vc_ref, vp_ref, cc_ref, sc_ref, cp_ref, sp_ref, sink_ref, o_ref):
        qr, kr, vr, valid, halves, _, _ = _attn_common(q_ref, kc_ref, kp_ref, vc_ref, vp_ref,
                                                        cc_ref, sc_ref, cp_ref, sp_ref)
        outs = [jnp.zeros((BLK, 128), F32) for _ in range(8)]
        for g in range(N_KV):
            r, half = g // 2, g % 2
            qs = _stack_heads(qr, g, halves).astype(BF)
            s = lax.dot_general(qs, kr[r], NT, preferred_element_type=F32)
            p, _ = _softmax_sink(s, valid, sink_ref, g)
            o = jnp.dot(p.astype(BF), vr[r], preferred_element_type=F32)
            for hh in range(4):
                h = 4 * g + hh
                oh = jnp.where(halves[half], o[hh * BLK:(hh + 1) * BLK], 0.0)
                if h % 2 != half:
                    oh = pltpu.roll(oh, HEAD_DIM, 1)
                outs[h // 2] = outs[h // 2] + oh
        o_ref[...] = jnp.concatenate(outs, axis=1).astype(BF)

    return pl.pallas_call(
        body, name=name, grid=(nb,),
        in_specs=_attn_specs(nb),
        out_specs=pl.BlockSpec((BLK, D), _row),
        out_shape=jax.ShapeDtypeStruct((T, D), BF),
        compiler_params=_cp(("parallel",)),
    )(proj, proj, proj, proj, proj, cos, sin, cos, sin, sinks)


def _attn_bwd(proj, cos, sin, sinks, o, do, dproj, *, name):
    T = proj.shape[0]
    nb = T // BLK

    def body(q_ref, kc_ref, kp_ref, vc_ref, vp_ref, cc_ref, sc_ref, cp_ref, sp_ref, sink_ref, o_ref, do_ref,
             dproj_ref, dq_ref, dkc_ref, dkp_ref, dvc_ref, dvp_ref, dsink_ref):
        @pl.when(pl.program_id(0) == 0)
        def _():
            dsink_ref[...] = jnp.zeros_like(dsink_ref)
        qr, kr, vr, valid, halves, lt32, (cc, sc, cp, sp) = _attn_common(
            q_ref, kc_ref, kp_ref, vc_ref, vp_ref, cc_ref, sc_ref, cp_ref, sp_ref)
        oc = [o_ref[:, p * 128:(p + 1) * 128].astype(F32) for p in range(8)]
        doc = [do_ref[:, p * 128:(p + 1) * 128].astype(F32) for p in range(8)]
        dqs = [jnp.zeros((BLK, 128), F32) for _ in range(8)]
        dkr = [jnp.zeros((2 * BLK, 128), F32) for _ in range(2)]
        dvr = [jnp.zeros((2 * BLK, 128), F32) for _ in range(2)]
        lane1 = lax.broadcasted_iota(jnp.int32, (1, 128), 1)
        dsink = jnp.zeros((1, 128), F32)
        for g in range(N_KV):
            r, half = g // 2, g % 2
            qs = _stack_heads(qr, g, halves).astype(BF)
            dos = _stack_heads(doc, g, halves)
            os_ = _stack_heads(oc, g, halves)
            s = lax.dot_general(qs, kr[r], NT, preferred_element_type=F32)
            p, ps = _softmax_sink(s, valid, sink_ref, g)
            dosb = dos.astype(BF)
            dp = lax.dot_general(dosb, vr[r], NT, preferred_element_type=F32)
            delta = jnp.sum(dos * os_, axis=-1, keepdims=True)
            ds = (p * (dp - delta) * (HEAD_DIM ** -0.5)).astype(BF)
            dsk = -ps * delta
            for hh in range(4):
                val = jnp.sum(dsk[hh * BLK:(hh + 1) * BLK], axis=0, keepdims=True)
                dsink = dsink + jnp.where(lane1 == 4 * g + hh, val, 0.0)
            dvr[r] = dvr[r] + lax.dot_general(p.astype(BF), dosb, TN, preferred_element_type=F32)
            dkr[r] = dkr[r] + lax.dot_general(ds, qs, TN, preferred_element_type=F32)
            dq = jnp.dot(ds, kr[r], preferred_element_type=F32)
            for hh in range(4):
                h = 4 * g + hh
                dqh = jnp.where(halves[half], dq[hh * BLK:(hh + 1) * BLK], 0.0)
                if h % 2 != half:
                    dqh = pltpu.roll(dqh, HEAD_DIM, 1)
                dqs[h // 2] = dqs[h // 2] + dqh
        dsink_ref[...] += dsink
        dq_ref[...] = jnp.concatenate([_rope(t, cc, sc, lt32, inverse=True) for t in dqs], axis=1).astype(BF)
        dkp_ref[...] = jnp.concatenate([_rope(t[:BLK], cp, sp, lt32, inverse=True) for t in dkr], axis=1)
        dkc_ref[...] = jnp.concatenate([_rope(t[BLK:], cc, sc, lt32, inverse=True) for t in dkr], axis=1)
        dvp_ref[...] = jnp.concatenate([t[:BLK] for t in dvr], axis=1)
        dvc_ref[...] = jnp.concatenate([t[BLK:] for t in dvr], axis=1)

    kv = pl.BlockSpec((BLK, 256), _row)
    return pl.pallas_call(
        body, name=name, grid=(nb,),
        in_specs=_attn_specs(nb) + [pl.BlockSpec((BLK, D), _row), pl.BlockSpec((BLK, D), _row),
                                    pl.BlockSpec(memory_space=pl.ANY)],
        out_specs=[pl.BlockSpec((BLK, D), lambda n: (n, O_Q // D)), kv, kv, kv, kv, pl.BlockSpec((1, 128), _const2)],
        out_shape=[jax.ShapeDtypeStruct(dproj.shape, BF)] + [jax.ShapeDtypeStruct((T, 256), F32)] * 4
        + [jax.ShapeDtypeStruct((1, 128), F32)],
        input_output_aliases={12: 0},
        compiler_params=_cp(("arbitrary",)),
    )(proj, proj, proj, proj, proj, cos, sin, cos, sin, sinks, o, do, dproj)


def _dkv_combine(dkc, dkp, dvc, dvp, dproj, *, name):
    T = dkc.shape[0]
    nb = T // BLK

    def body(dkc_ref, dkp_ref, dvc_ref, dvp_ref, dproj_ref, o_ref):
        last = (pl.program_id(0) == nb - 1)
        keep = jnp.where(last, 0.0, 1.0)
        o_ref[:, 0:256] = (dkc_ref[...] + keep * dkp_ref[...]).astype(BF)
        o_ref[:, 256:512] = (dvc_ref[...] + keep * dvp_ref[...]).astype(BF)

    cur = pl.BlockSpec((BLK, 256), _row)
    nxt = pl.BlockSpec((BLK, 256), lambda n: (jnp.minimum(n + 1, nb - 1), 0))
    return pl.pallas_call(
        body, name=name, grid=(nb,),
        in_specs=[cur, nxt, cur, nxt, pl.BlockSpec(memory_space=pl.ANY)],
        out_specs=pl.BlockSpec((BLK, 512), lambda n: (n, O_K // 512)),
        out_shape=jax.ShapeDtypeStruct(dproj.shape, BF),
        input_output_aliases={4: 0},
        compiler_params=_cp(("parallel",)),
    )(dkc, dkp, dvc, dvp, dproj)


HALO = 16


def _conv_shifts(cu, hprev, tm):
    row = lax.broadcasted_iota(jnp.int32, cu.shape, 0)
    h1 = hprev[HALO - 1:HALO, :]
    h2 = hprev[HALO - 2:HALO - 1, :]
    m1 = jnp.where(row == 0, h1, pltpu.roll(cu, 1, 0))
    m2 = jnp.where(row == 0, h2, jnp.where(row == 1, h1, pltpu.roll(cu, 2, 0)))
    return m1, m2


def _mixer_mid_fwd(proj, attn, wcp, wap, wout, convw, x, gt, *, tm, name):
    T = x.shape[0]
    tm = _tile(T, tm)
    hb = tm // HALO

    def body(bg_ref, cg_ref, u_ref, hcg_ref, hu_ref, zc0_ref, zc1_ref, za0_ref, za1_ref, at_ref,
             wcp_ref, wap_ref, wout_ref, cw_ref, x_ref, gt_ref,
             x2_ref, gc_ref, yc_ref, ya_ref, mg_ref, o_ref):
        first = jnp.where(pl.program_id(0) == 0, 0.0, 1.0)
        cu = cg_ref[...].astype(F32) * u_ref[...].astype(F32)
        hprev = first * (hcg_ref[...].astype(F32) * hu_ref[...].astype(F32))
        m1, m2 = _conv_shifts(cu, hprev, tm)
        cv = cw_ref[0:1, :] * m2 + cw_ref[1:2, :] * m1 + cw_ref[2:3, :] * cu
        gc = (bg_ref[...].astype(F32) * cv).astype(BF)
        gc_ref[...] = gc
        yc = jnp.dot(gc, wcp_ref[...], preferred_element_type=F32)
        ya = jnp.dot(at_ref[...], wap_ref[...], preferred_element_type=F32)
        yc_ref[...] = yc.astype(BF)
        ya_ref[...] = ya.astype(BF)
        zc = jnp.concatenate([zc0_ref[...], zc1_ref[...]], axis=1).astype(F32)
        za = jnp.concatenate([za0_ref[...], za1_ref[...]], axis=1).astype(F32)
        mg = (_sigmoid(zc) * yc + _sigmoid(za) * ya).astype(BF)
        mg_ref[...] = mg
        o = jnp.dot(mg, wout_ref[...], preferred_element_type=F32)
        o_ref[...] = o.astype(BF)
        x2_ref[...] = x_ref[...] + gt_ref[...] * o

    wspec = pl.BlockSpec((D, D), _const2)
    rowspec = pl.BlockSpec((tm, D), _row)
    return pl.pallas_call(
        body, name=name, grid=(T // tm,),
        in_specs=[_col(tm, O_BG), _col(tm, O_CG), _col(tm, O_U), _halo_prev(hb, O_CG), _halo_prev(hb, O_U),
                  _col(tm, O_ZC, 512), _col(tm, O_ZC + 512, 512), _col(tm, O_ZA, 512), _col(tm, O_ZA + 512, 512),
                  rowspec, wspec, wspec, wspec, pl.BlockSpec((8, D), _const2), rowspec, pl.BlockSpec((1, D), _const2)],
        out_specs=[rowspec] * 6,
        out_shape=[jax.ShapeDtypeStruct((T, D), F32)] + [jax.ShapeDtypeStruct((T, D), BF)] * 5,
        compiler_params=_cp(("parallel",)),
    )(proj, proj, proj, proj, proj, proj, proj, proj, proj, attn, wcp, wap, wout, convw, x, gt)


def _col(tm, c, w=D):
    assert c % w == 0
    return pl.BlockSpec((tm, w), lambda i: (i, c // w))


def _halo_prev(hb, c):
    return pl.BlockSpec((HALO, D), lambda i: (jnp.maximum(i * hb - 1, 0), c // D))


def _halo_next(hb, nblk, c=0):
    return pl.BlockSpec((HALO, D), lambda i: (jnp.minimum((i + 1) * hb, nblk - 1), c // D))


def _mixer_mid_bwd(dx2, gt, o, proj, yc, ya, wout, wcp, wap, *, tm, name):
    T = dx2.shape[0]
    tm = _tile(T, tm)
    zw = 512
    nz = 2 * D // zw

    def body(dx_ref, gt_ref, o_ref, zc0_ref, zc1_ref, za0_ref, za1_ref, yc_ref, ya_ref, wout_ref, wcp_ref, wap_ref,
             dout_ref, dyc_ref, dya_ref, dgc_ref, dat_ref, dz_ref, dgt_ref, dzs):
        i, j = pl.program_id(0), pl.program_id(1)

        @pl.when(jnp.logical_and(i == 0, j == 0))
        def _():
            dgt_ref[...] = jnp.zeros_like(dgt_ref)

        @pl.when(j == 0)
        def _():
            dxv = dx_ref[...]
            dgt_ref[...] += jnp.sum(dxv * o_ref[...].astype(F32), axis=0, keepdims=True)
            dout = (gt_ref[...] * dxv).astype(BF)
            dout_ref[...] = dout
            dmg = lax.dot_general(dout, wout_ref[...], NT, preferred_element_type=F32)
            sc = _sigmoid(jnp.concatenate([zc0_ref[...], zc1_ref[...]], axis=1).astype(F32))
            sa = _sigmoid(jnp.concatenate([za0_ref[...], za1_ref[...]], axis=1).astype(F32))
            dyc = (dmg * sc).astype(BF)
            dya = (dmg * sa).astype(BF)
            dyc_ref[...] = dyc
            dya_ref[...] = dya
            dzs[:, 0:D] = (dmg * yc_ref[...].astype(F32) * (sc * (1.0 - sc))).astype(BF)
            dzs[:, D:2 * D] = (dmg * ya_ref[...].astype(F32) * (sa * (1.0 - sa))).astype(BF)
            dgc_ref[...] = lax.dot_general(dyc, wcp_ref[...], NT, preferred_element_type=F32).astype(BF)
            dat_ref[...] = lax.dot_general(dya, wap_ref[...], NT, preferred_element_type=F32).astype(BF)

        for jj in range(nz):
            @pl.when(j == jj)
            def _(jj=jj):
                dz_ref[...] = dzs[:, jj * zw:(jj + 1) * zw]

    def zcol(c):
        return pl.BlockSpec((tm, zw), lambda i, j: (i, c // zw))

    wspec = pl.BlockSpec((D, D), _const2)
    rowspec = pl.BlockSpec((tm, D), lambda i, j: (i, 0))
    vec = pl.BlockSpec((1, D), _const2)
    return pl.pallas_call(
        body, name=name, grid=(T // tm, nz),
        in_specs=[rowspec, vec, rowspec, zcol(O_ZC), zcol(O_ZC + zw), zcol(O_ZA), zcol(O_ZA + zw),
                  rowspec, rowspec, wspec, wspec, wspec],
        out_specs=[rowspec] * 5 + [pl.BlockSpec((tm, zw), lambda i, j: (i, O_ZC // zw + j)), vec],
        out_shape=[jax.ShapeDtypeStruct((T, D), BF)] * 5 + [jax.ShapeDtypeStruct((T, NIN), BF),
                                                            jax.ShapeDtypeStruct((1, D), F32)],
        scratch_shapes=[pltpu.VMEM((tm, 2 * D), BF)],
        compiler_params=_cp(("arbitrary", "arbitrary")),
    )(dx2, gt, o, proj, proj, proj, proj, yc, ya, wout, wcp, wap)


def _conv_bwd(dgc, proj, convw, dproj, *, tm, name):
    T = dgc.shape[0]
    tm = _tile(T, tm)
    hb = tm // HALO
    nblk = T // HALO
    nt = T // tm

    def body(dgc_ref, ndgc_ref, bg_ref, nbg_ref, cg_ref, u_ref, hcg_ref, hu_ref, cw_ref, dproj_ref, dp_ref, dcw_ref):
        i = pl.program_id(0)

        @pl.when(i == 0)
        def _():
            dcw_ref[...] = jnp.zeros_like(dcw_ref)
        first = jnp.where(i == 0, 0.0, 1.0)
        last = jnp.where(i == nt - 1, 0.0, 1.0)
        cg = cg_ref[...].astype(F32)
        u = u_ref[...].astype(F32)
        bg = bg_ref[...].astype(F32)
        dg = dgc_ref[...].astype(F32)
        cu = cg * u
        hprev = first * (hcg_ref[...].astype(F32) * hu_ref[...].astype(F32))
        m1, m2 = _conv_shifts(cu, hprev, tm)
        w0, w1, w2 = cw_ref[0:1, :], cw_ref[1:2, :], cw_ref[2:3, :]
        cv = w0 * m2 + w1 * m1 + w2 * cu
        dcv = dg * bg
        nxt = last * (ndgc_ref[...].astype(F32) * nbg_ref[...].astype(F32))
        n0, n1 = nxt[0:1, :], nxt[1:2, :]
        row = lax.broadcasted_iota(jnp.int32, dcv.shape, 0)
        p1 = jnp.where(row == tm - 1, n0, pltpu.roll(dcv, tm - 1, 0))
        p2 = jnp.where(row == tm - 1, n1, jnp.where(row == tm - 2, n0, pltpu.roll(dcv, tm - 2, 0)))
        dcu = w2 * dcv + w1 * p1 + w0 * p2
        dp_ref[:, 0:D] = (dg * cv).astype(BF)
        dp_ref[:, D:2 * D] = (dcu * u).astype(BF)
        dp_ref[:, 2 * D:3 * D] = (dcu * cg).astype(BF)
        dcw_ref[0:1, :] += jnp.sum(dcv * m2, axis=0, keepdims=True)
        dcw_ref[1:2, :] += jnp.sum(dcv * m1, axis=0, keepdims=True)
        dcw_ref[2:3, :] += jnp.sum(dcv * cu, axis=0, keepdims=True)

    rowspec = pl.BlockSpec((tm, D), _row)
    cw = pl.BlockSpec((8, D), _const2)
    return pl.pallas_call(
        body, name=name, grid=(nt,),
        in_specs=[rowspec, _halo_next(hb, nblk), _col(tm, O_BG), _halo_next(hb, nblk, O_BG),
                  _col(tm, O_CG), _col(tm, O_U), _halo_prev(hb, O_CG), _halo_prev(hb, O_U), cw,
                  pl.BlockSpec(memory_space=pl.ANY)],
        out_specs=[pl.BlockSpec((tm, 3 * D), _row), cw],
        out_shape=[jax.ShapeDtypeStruct(dproj.shape, BF), jax.ShapeDtypeStruct((8, D), F32)],
        input_output_aliases={9: 0},
        compiler_params=_cp(("arbitrary",)),
    )(dgc, dgc, proj, proj, proj, proj, proj, proj, convw, dproj)


def _gsum(parts, *, tm, name):
    _, R, C = parts.shape
    tm = _tile(R, tm)

    def body(p_ref, o_ref):
        acc = p_ref[0].astype(F32)
        for s in range(1, N_DEV):
            acc = acc + p_ref[s].astype(F32)
        o_ref[...] = acc

    return pl.pallas_call(
        body, name=name, grid=(R // tm,),
        in_specs=[pl.BlockSpec((N_DEV, tm, C), lambda i: (0, i, 0))],
        out_specs=pl.BlockSpec((tm, C), _row),
        out_shape=jax.ShapeDtypeStruct((R, C), F32),
        compiler_params=_cp(("parallel",)),
    )(parts)


def _adam(w, g, m, v, *, tm, name):
    _, R, C = w.shape
    tm = _tile(R, tm)
    parts = g.ndim == 3
    c1 = 1.0 - ADAM_B1
    c2 = 1.0 - ADAM_B2
    bc1 = 1.0 - ADAM_B1 ** ADAM_STEP
    bc2 = 1.0 - ADAM_B2 ** ADAM_STEP

    def body(w_ref, g_ref, m_ref, v_ref, go_ref, d_ref, nm_ref, nv_ref):
        if parts:
            gv = g_ref[0].astype(F32)
            for s in range(1, N_DEV):
                gv = gv + g_ref[s].astype(F32)
        else:
            gv = g_ref[...]
        go_ref[0] = gv
        nm = ADAM_B1 * m_ref[0] + c1 * gv
        nv = ADAM_B2 * v_ref[0] + c2 * (gv * gv)
        nm_ref[0] = nm
        nv_ref[0] = nv
        d_ref[0] = -ADAM_LR * ((nm / bc1) / (jnp.sqrt(nv / bc2) + ADAM_EPS) + ADAM_WD * w_ref[0])

    spec = pl.BlockSpec((1, tm, C), lambda i: (0, i, 0))
    gspec = pl.BlockSpec((N_DEV, tm, C), lambda i: (0, i, 0)) if parts else pl.BlockSpec((tm, C), _row)
    return pl.pallas_call(
        body, name=name, grid=(R // tm,),
        in_specs=[spec, gspec, spec, spec], out_specs=[spec] * 4,
        out_shape=[jax.ShapeDtypeStruct((1, R, C), F32)] * 4,
        compiler_params=_cp(("parallel",)),
    )(w, g, m, v)


def _mods_part(c_all, w_ada, b_ada, *, name):
    C = w_ada.shape[1]

    def body(c_ref, w_ref, b_ref, o_ref):
        cv = c_ref[...]
        ca = cv * jax.nn.sigmoid(cv)
        o_ref[...] = jnp.dot(ca, w_ref[...], preferred_element_type=F32,
                             precision=lax.Precision.HIGHEST) + b_ref[...]

    return pl.pallas_call(
        body, name=name,
        out_shape=jax.ShapeDtypeStruct((N_DEV, C), F32),
        compiler_params=_cp(),
    )(c_all, w_ada, b_ada)


def _wada_grad(c_all_t, gm, *, name):
    C = gm.shape[1]

    def body(c_ref, g_ref, o_ref):
        cv = c_ref[...]
        ca = cv * jax.nn.sigmoid(cv)
        acc = ca[:, 0:1] * g_ref[0:1, :]
        for b in range(1, N_DEV):
            acc = acc + ca[:, b:b + 1] * g_ref[b:b + 1, :]
        o_ref[...] = acc

    return pl.pallas_call(
        body, name=name,
        out_shape=jax.ShapeDtypeStruct((D, C), F32),
        compiler_params=_cp(),
    )(c_all_t, gm)


def _peer(x, y, c, d):
    px = lax.rem(x + ((d >> 2) & 1), 2)
    py = lax.rem(y + ((d >> 1) & 1), 2)
    pc = lax.rem(c + (d & 1), 2)
    return (px, py, pc), 4 * px + 2 * py + pc


def _exchange(xs, *, scatter, name):
    n = len(xs)
    nsem = n * (N_DEV - 1)

    def body(*refs):
        ins, outs = refs[:n], refs[n:2 * n]
        token, send_sems, recv_sems, local_sems = refs[2 * n:]
        x, y, c = lax.axis_index("x"), lax.axis_index("y"), lax.axis_index("c")
        me = 4 * x + 2 * y + c
        token[...] = jnp.zeros_like(token)

        def src(t, idx):
            return ins[t].at[idx] if scatter else ins[t]

        local = [pltpu.make_async_copy(src(t, me), outs[t].at[me], local_sems.at[t]) for t in range(n)]
        for cp in local:
            cp.start()
        remote = []
        for t in range(n):
            for d in range(1, N_DEV):
                peer, pidx = _peer(x, y, c, d)
                k = t * (N_DEV - 1) + d - 1
                send = pltpu.make_async_remote_copy(src_ref=src(t, pidx), dst_ref=outs[t].at[me],
                                                    send_sem=send_sems.at[k], recv_sem=recv_sems.at[k],
                                                    device_id=peer, device_id_type=MESH)
                recv = pltpu.make_async_remote_copy(src_ref=src(t, pidx), dst_ref=outs[t].at[pidx],
                                                    send_sem=send_sems.at[k], recv_sem=recv_sems.at[k],
                                                    device_id=peer, device_id_type=MESH)
                send.start()
                remote.append((send, recv))
        for cp in local:
            cp.wait()
        for send, recv in remote:
            send.wait_send()
            recv.wait_recv()

    anyspec = pl.BlockSpec(memory_space=pl.ANY)
    out_shape = [jax.ShapeDtypeStruct(a.shape if scatter else (N_DEV,) + a.shape, a.dtype) for a in xs]
    out_shape.append(jax.ShapeDtypeStruct((8, 128), F32))
    return pl.pallas_call(
        body, name=name,
        in_specs=[anyspec] * n, out_specs=[anyspec] * n + [pl.BlockSpec(memory_space=pltpu.VMEM)],
        out_shape=out_shape,
        scratch_shapes=[pltpu.SemaphoreType.DMA((nsem,)), pltpu.SemaphoreType.DMA((nsem,)),
                        pltpu.SemaphoreType.DMA((n,))],
    )(*xs)


def _sum8(parts, *, name):
    _, R, C = parts.shape

    def body(p_ref, o_ref):
        acc = p_ref[0]
        for s in range(1, N_DEV):
            acc = acc + p_ref[s]
        o_ref[...] = acc

    return pl.pallas_call(body, name=name, out_shape=jax.ShapeDtypeStruct((R, C), F32),
                          compiler_params=_cp())(parts)


HBM_SPEC = pl.BlockSpec(memory_space=pltpu.HBM)
SEM_SPEC = pl.BlockSpec(memory_space=pltpu.SEMAPHORE)
N_PEER = N_DEV - 1


def _split_copies(src_refs, land_refs, send_sems, recv_sems, scatter):
    x, y, c = lax.axis_index("x"), lax.axis_index("y"), lax.axis_index("c")
    me = 4 * x + 2 * y + c
    pairs = []
    for j, (src, land) in enumerate(zip(src_refs, land_refs)):
        for d in range(1, N_DEV):
            peer, pidx = _peer(x, y, c, d)
            k = j * N_PEER + d - 1
            s = src.at[pidx] if scatter else src
            send = pltpu.make_async_remote_copy(src_ref=s, dst_ref=land.at[me], send_sem=send_sems.at[k],
                                                recv_sem=recv_sems.at[k], device_id=peer, device_id_type=MESH)
            recv = pltpu.make_async_remote_copy(src_ref=s, dst_ref=land.at[pidx], send_sem=send_sems.at[k],
                                                recv_sem=recv_sems.at[k], device_id=peer, device_id_type=MESH)
            pairs.append((send, recv))
    return pairs


def _own_slot(block, me):
    land = lax.empty((N_DEV,) + block.shape, block.dtype)
    return lax.dynamic_update_slice(land, block[None], (me, 0, 0))


def _split_start(srcs, lands, groups, *, scatter, name):
    n, ng = len(srcs), len(groups)

    def body(*refs):
        src_refs, land_refs = refs[:n], refs[n:2 * n]
        sems = refs[2 * n:2 * n + 2 * ng]
        token = refs[-1]
        for gi, g in enumerate(groups):
            pairs = _split_copies([src_refs[t] for t in g], [land_refs[t] for t in g], sems[2 * gi],
                                  sems[2 * gi + 1], scatter)
            for send, _ in pairs:
                send.start()
        token[...] = jnp.zeros_like(token)

    sem_shapes = []
    for g in groups:
        sem_shapes += [pltpu.SemaphoreType.DMA((len(g) * N_PEER,))] * 2
    thru = [pltpu.HBM(a.shape, a.dtype) for a in list(srcs) + list(lands)]
    outs = pl.pallas_call(
        body, name=name,
        out_shape=tuple(sem_shapes + thru + [jax.ShapeDtypeStruct((8, 128), F32)]),
        in_specs=[HBM_SPEC] * (2 * n),
        out_specs=tuple([SEM_SPEC] * (2 * ng) + [HBM_SPEC] * (2 * n) + [pl.BlockSpec(memory_space=pltpu.VMEM)]),
        input_output_aliases={i: 2 * ng + i for i in range(2 * n)},
        compiler_params=pltpu.CompilerParams(has_side_effects=pltpu.SideEffectType.DATAFLOW_SIDE_EFFECTING),
    )(*[pltpu.with_memory_space_constraint(a, pltpu.HBM) for a in list(srcs) + list(lands)])
    sems = [(outs[2 * gi], outs[2 * gi + 1]) for gi in range(ng)]
    return sems, outs[2 * ng:2 * ng + n], outs[2 * ng + n:2 * ng + 2 * n], outs[-1]


def _behind(v, token):
    if token is None:
        return v
    return v + token[0, 0].astype(v.dtype)


def _split_wait(srcs, lands, sems, after, *, scatter, name):
    m = len(srcs)

    def body(*refs):
        src_refs, land_refs = refs[:m], refs[m:2 * m]
        send_sems, recv_sems = refs[2 * m], refs[2 * m + 1]
        for send, recv in _split_copies(src_refs, land_refs, send_sems, recv_sems, scatter):
            send.wait_send()
            recv.wait_recv()

    outs = pl.pallas_call(
        body, name=name,
        out_shape=tuple(pltpu.HBM(a.shape, a.dtype) for a in list(srcs) + list(lands)),
        in_specs=[HBM_SPEC] * (2 * m) + [SEM_SPEC, SEM_SPEC, pl.BlockSpec(memory_space=pl.ANY)],
        out_specs=tuple([HBM_SPEC] * (2 * m)),
        input_output_aliases={i: i for i in range(2 * m)},
        compiler_params=pltpu.CompilerParams(has_side_effects=pltpu.SideEffectType.DATAFLOW_SIDE_EFFECTING),
    )(*srcs, *lands, sems[0], sems[1], after)
    return outs[m:]


TM_PROJ = 1024
TM_ROW = 512
TM_NN = 1024
TK_TN = 512
TN_FFN = F // 2
TN_IN = NIN // 4


def _tn(a, b, name, tn):
    if a.ndim == 2:
        a = a[None]
    return _tn_matmul(a, b, tn=tn, tk=TK_TN, name=name)


def _local_step(x, tgt, mods, g1, gm, g2, gf, convw8, sinks, w_get, g_put):
    T = x.shape[0]
    sh1, sc1, gt1, sh2, sc2, gt2, sh3, sc3, gt3 = [mods[i:i + 1] for i in range(N_MOD)]
    cos, sin = _rope_tables(T)
    behind = _behind

    w = dict(w_get("gu1", mods))
    h1, ab1 = _norm_proj(x, g1, sc1, sh1, w["gu1"], tm=TM_PROJ, tn=TN_FFN, name="ffn1_up")
    w.update(w_get("d1", ab1))
    x1, y1 = _ffn_down_fwd(ab1, w["d1"], x, gt1, tm=TM_ROW, name="ffn1_down")
    w.update(w_get("mix", x1))
    h2, proj = _norm_proj(x1, gm, sc2, sh2, w["win"], tm=TM_PROJ, tn=TN_IN, name="mix_in")
    attn = _attn_fwd(proj, cos, sin, sinks, name="attn_fwd")
    x2, gc, yc, ya, mg, o = _mixer_mid_fwd(proj, attn, w["cp"], w["ap"], w["out"], convw8, x1, gt2,
                                           tm=TM_ROW, name="mix_mid")
    w.update(w_get("ffn2", x2))
    h3, ab2 = _norm_proj(x2, g2, sc3, sh3, w["gu2"], tm=TM_PROJ, tn=TN_FFN, name="ffn2_up")
    x3, y2 = _ffn_down_fwd(ab2, w["d2"], x2, gt3, tm=TM_ROW, name="ffn2_down")
    dx3, lsum, dgf = _final_fwd_bwd(x3, tgt, gf, tm=TM_ROW, name="final")

    dy2, dab2, dgt3 = _ffn_down_bwd(dx3, y2, gt3, ab2, w["d2"], tm=TM_ROW, tn=MXU_N, name="ffn2_down_bwd")
    g_d2 = _tn_matmul_swiglu(ab2, dy2, None, tn=TN_FFN, tk=TK_TN, name="ffn2_down_dw")
    dx2, dsh3, dsc3, dg2 = _nn_bwd_norm(dab2, w["gu2"], x2, g2, sc3, dx3, tm=TM_NN, tk=TN_FFN, name="ffn2_up_bwd")
    g_gu2 = _tn(dab2, h3, "ffn2_up_dw", TN_FFN)
    tok = g_put(dict(gu2=g_gu2, d2=g_d2))

    dout, dyc, dya, dgc, dat, dproj, dgt2 = _mixer_mid_bwd(dx2, behind(gt2, tok), o, proj, yc, ya, w["out"], w["cp"],
                                                           w["ap"], tm=TM_ROW, name="mix_mid_bwd")
    g_out = _tn(mg, dout, "mix_out_dw", D)
    g_cp = _tn(gc, dyc, "mix_cp_dw", D)
    g_ap = _tn(attn, dya, "mix_ap_dw", D)
    dproj, dkc, dkp, dvc, dvp, dsink = _attn_bwd(proj, cos, sin, sinks, attn, dat, dproj, name="attn_bwd")
    dproj = _dkv_combine(dkc, dkp, dvc, dvp, dproj, name="attn_dkv")
    dproj, dcw = _conv_bwd(dgc, proj, convw8, dproj, tm=TM_ROW, name="conv_bwd")
    g_in = _tn(dproj, h2, "mix_in_dw", TN_IN)
    tok = g_put(dict(win=g_in, cp=g_cp, ap=g_ap, out=g_out))
    dx1, dsh2, dsc2, dgm = _nn_bwd_norm(dproj[None], w["win"], x1, gm, behind(sc2, tok), dx2, tm=TM_NN, tk=TN_IN,
                                        name="mix_in_bwd")

    dy1, dab1, dgt1 = _ffn_down_bwd(dx1, y1, gt1, ab1, w["d1"], tm=TM_ROW, tn=MXU_N, name="ffn1_down_bwd")
    g_gu1 = _tn(dab1, h1, "ffn1_up_dw", TN_FFN)
    tok = g_put(dict(gu1=g_gu1))
    g_d1 = _tn_matmul_swiglu(ab1, dy1, tok, tn=TN_FFN, tk=TK_TN, name="ffn1_down_dw")
    tok = g_put(dict(d1=g_d1))
    dx0, dsh1, dsc1, dg1 = _nn_bwd_norm(dab1, w["gu1"], x, g1, behind(sc1, tok), dx1, tm=TM_NN, tk=TN_FFN,
                                        name="ffn1_up_bwd")

    small = dict(mods=jnp.concatenate([dsh1, dsc1, dgt1, dsh2, dsc2, dgt2, dsh3, dsc3, dgt3], axis=0),
                 g1=dg1, gm=dgm, g2=dg2, gf=dgf, convw=dcw[0:3], sinks=dsink[:, 0:N_HEADS])
    return lsum, dx0, small


BIG = ("gu1", "d1", "win", "cp", "ap", "out", "gu2", "d2")
TRANSPOSED = ("gu1", "win", "gu2")
SMALL_ROWS = 24
R_MODS, R_G1, R_GM, R_G2, R_GF, R_CONV, R_SINK = 0, 9, 10, 11, 12, 13, 16


def _pad_to(a, rows, cols):
    return jnp.pad(a, ((0, rows - a.shape[0]), (0, cols - a.shape[1])))


def _pack_small(b_ada, g1, gm, g2, gf, conv, sinks):
    rows = [b_ada.reshape(N_MOD, D), g1.reshape(1, D), gm.reshape(1, D), g2.reshape(1, D), gf.reshape(1, D),
            _pad_to(conv.reshape(3, -1), 3, D), _pad_to(sinks.reshape(1, N_HEADS), 1, D)]
    return _pad_to(jnp.concatenate(rows, axis=0), SMALL_ROWS, D)


def _unpack_small(p, conv_cols):
    return dict(b_ada=p[R_MODS:R_MODS + N_MOD].reshape(1, N_MOD * D), g_ffn1=p[R_G1:R_G1 + 1],
                g_mix=p[R_GM:R_GM + 1], g_ffn2=p[R_G2:R_G2 + 1], g_final=p[R_GF],
                conv_w=p[R_CONV:R_CONV + 3, 0:conv_cols][None], sinks=p[R_SINK:R_SINK + 1, 0:N_HEADS])


def kernel(x, c, w_ada, b_ada, g_ffn1, w1_gu, w1_down, g_mix, w_in, conv_w, w_conv_proj, w_attn_proj, sinks, w_out, g_ffn2, w2_gu, w2_down, g_final, loss_target, m_w_ada, m_b_ada, m_g_ffn1, m_w1_gu, m_w1_down, m_g_mix, m_w_in, m_conv_w, m_w_conv_proj, m_w_attn_proj, m_sinks, m_w_out, m_g_ffn2, m_w2_gu, m_w2_down, m_g_final, v_w_ada, v_b_ada, v_g_ffn1, v_w1_gu, v_w1_down, v_g_mix, v_w_in, v_conv_w, v_w_conv_proj, v_w_attn_proj, v_sinks, v_w_out, v_g_ffn2, v_w2_gu, v_w2_down, v_g_final):
    me = 4 * lax.axis_index("x") + 2 * lax.axis_index("y") + lax.axis_index("c")
    ada_cols = w_ada.shape[2]
    conv_cols = conv_w.shape[2]

    native = dict(gu1=w1_gu[0], d1=w1_down[0], win=w_in[0], cp=w_conv_proj[0], ap=w_attn_proj[0], out=w_out[0],
                  gu2=w2_gu[0], d2=w2_down[0])

    def shard(n, token):
        a = _behind(native[n], token)
        return (a.T if n in TRANSPOSED else a).astype(BF)

    c_all, conv_all, _ = _exchange([c, _pad_to(conv_w[0], 8, conv_cols)], scatter=False, name="gather_cond")
    c_all = c_all.reshape(N_DEV, D)
    conv_full = conv_all[:, 0:3, :].transpose(1, 0, 2).reshape(3, D)

    b_cols = lax.dynamic_slice(b_ada, (0, me * ada_cols), (1, ada_cols))
    mods_cols = _mods_part(c_all, w_ada[0], b_cols, name="ada_mods")
    mods_all, mods_token = _exchange([mods_cols], scatter=False, name="gather_mods")
    mods = lax.dynamic_index_in_dim(mods_all, me, axis=1, keepdims=False).reshape(N_MOD, D)

    groups = dict(gu1=("gu1",), d1=("d1",), mix=("win", "cp", "ap", "out"), ffn2=("gu2", "d2"))
    in_flight = {}
    first = [shard("gu1", mods_token)]
    sems, srcs, lands, token = _split_start(first, [_own_slot(s, me) for s in first], [[0]], scatter=False,
                                            name="gather_weights_start_gu1")
    in_flight["gu1"] = (sems[0], srcs, lands)
    rest = [n for n in BIG if n != "gu1"]
    shards = [shard(n, token) for n in rest]
    rest_groups = [[rest.index(n) for n in names] for g, names in groups.items() if g != "gu1"]
    sems, srcs, lands, _ = _split_start(shards, [_own_slot(s, me) for s in shards], rest_groups, scatter=False,
                                        name="gather_weights_start_rest")
    for (g, names), gsems, idx in zip([kv for kv in groups.items() if kv[0] != "gu1"], sems, rest_groups):
        in_flight[g] = (gsems, [srcs[t] for t in idx], [lands[t] for t in idx])

    def w_get(group, after):
        gsems, gsrcs, glands = in_flight[group]
        landed = _split_wait(gsrcs, glands, gsems, after, scatter=False, name="gather_weights_wait_" + group)
        return {n: a.reshape(-1, D) for n, a in zip(groups[group], landed)}

    pending = []

    def g_put(gs):
        names = tuple(gs)
        srcs = [gs[n].reshape(N_DEV, -1, D) for n in names]
        lands = [_own_slot(lax.dynamic_index_in_dim(s, me, axis=0, keepdims=False), me) for s in srcs]
        sems, srcs, lands, tok = _split_start(srcs, lands, [list(range(len(names)))], scatter=True,
                                              name="scatter_grads_start_" + names[0])
        pending.append((names, sems[0], srcs, lands))
        return tok

    lsum, grad_x, small = _local_step(x[0], loss_target[0], mods, g_ffn1, g_mix, g_ffn2, g_final[None],
                                      _pad_to(conv_full, 8, D), sinks[0], w_get, g_put)
    loss = lax.psum((0.5 / D) * jnp.sum(lsum), ("x", "y", "c"))

    packed = _pack_small(small["mods"], small["g1"], small["gm"], small["g2"], small["gf"], small["convw"],
                         small["sinks"])
    packed_all, _ = _exchange([packed], scatter=False, name="gather_small")
    gsmall = _sum8(packed_all, name="sum_small")

    w_of = dict(ada=w_ada, gu1=w1_gu, d1=w1_down, win=w_in, cp=w_conv_proj, ap=w_attn_proj, out=w_out, gu2=w2_gu,
                d2=w2_down)
    m_of = dict(ada=m_w_ada, gu1=m_w1_gu, d1=m_w1_down, win=m_w_in, cp=m_w_conv_proj, ap=m_w_attn_proj, out=m_w_out,
                gu2=m_w2_gu, d2=m_w2_down)
    v_of = dict(ada=v_w_ada, gu1=v_w1_gu, d1=v_w1_down, win=v_w_in, cp=v_w_conv_proj, ap=v_w_attn_proj, out=v_w_out,
                gu2=v_w2_gu, d2=v_w2_down)
    upd = {}
    after = gsmall
    for names, sems, srcs, lands in pending:
        parts = _split_wait(srcs, lands, sems, after, scatter=True, name="scatter_grads_wait_" + names[0])
        for n, p in zip(names, parts):
            g = _gsum(p, tm=128, name="gsum_" + n).T if n in TRANSPOSED else p
            upd[n] = _adam(w_of[n], g, m_of[n], v_of[n], tm=128, name="adam_" + n)
        after = upd[names[-1]][1]

    gm_cols = lax.dynamic_slice(packed_all[:, R_MODS:R_MODS + N_MOD, :].reshape(N_DEV, N_MOD * D),
                                (0, me * ada_cols), (N_DEV, ada_cols))
    upd["ada"] = _adam(w_ada, _wada_grad(c_all.T, gm_cols, name="ada_dw"), m_w_ada, v_w_ada, tm=128, name="adam_ada")
    conv_g = lax.dynamic_slice(gsmall[R_CONV:R_CONV + 3], (0, me * conv_cols), (3, conv_cols))
    gsmall_own = gsmall.at[R_CONV:R_CONV + 3].set(_pad_to(conv_g, 3, D))
    small_upd = _adam(_pack_small(b_ada, g_ffn1, g_mix, g_ffn2, g_final, conv_w, sinks)[None], gsmall_own,
                      _pack_small(m_b_ada, m_g_ffn1, m_g_mix, m_g_ffn2, m_g_final, m_conv_w, m_sinks)[None],
                      _pack_small(v_b_ada, v_g_ffn1, v_g_mix, v_g_ffn2, v_g_final, v_conv_w, v_sinks)[None],
                      tm=SMALL_ROWS, name="adam_small")
    small_out = [_unpack_small(p[0], conv_cols) for p in small_upd]

    big_name = dict(w_ada="ada", w1_gu="gu1", w1_down="d1", w_in="win", w_conv_proj="cp", w_attn_proj="ap",
                    w_out="out", w2_gu="gu2", w2_down="d2")
    order = ("w_ada", "b_ada", "g_ffn1", "w1_gu", "w1_down", "g_mix", "w_in", "conv_w", "w_conv_proj", "w_attn_proj",
             "sinks", "w_out", "g_ffn2", "w2_gu", "w2_down", "g_final")
    outs = [loss, grad_x[None]]
    for kind in range(4):
        for n in order:
            outs.append(upd[big_name[n]][kind] if n in big_name else small_out[kind][n])
    return tuple(outs)
```

```python
import functools

import jax
import jax.numpy as jnp
from jax import lax
from jax.experimental import pallas as pl
from jax.experimental.pallas import tpu as pltpu

D = 1024
F = 2816
NIN = 6656
N_HEADS = 16
N_KV = 4
HEAD_DIM = 64
BLK = 128
N_MOD = 9
N_DEV = 8
EPS = 1e-6
NEG_INF = -1e30
ROPE_THETA = 10000.0
O_BG, O_CG, O_U, O_Q, O_K, O_V, O_ZC, O_ZA = 0, 1024, 2048, 3072, 4096, 4352, 4608, 5632

ADAM_LR = 0.001
ADAM_B1 = 0.9
ADAM_B2 = 0.999
ADAM_EPS = 1e-08
ADAM_WD = 0.01
ADAM_STEP = 10

BF = jnp.bfloat16
F32 = jnp.float32
VMEM_LIMIT = 56 * 1024 * 1024
MXU_N = 256
MESH = pl.DeviceIdType.MESH

NT = (((1,), (1,)), ((), ()))
TN = (((0,), (0,)), ((), ()))


def _cp(sem=None):
    return pltpu.CompilerParams(dimension_semantics=sem, vmem_limit_bytes=VMEM_LIMIT)


def _tile(n, pref):
    if n <= pref:
        return n
    for t in range(pref - pref % 16, 15, -16):
        if n % t == 0:
            return t
    raise ValueError((n, pref))


def _sigmoid(v):
    return 0.5 * jnp.tanh(0.5 * v) + 0.5


def _row(i):
    return (i, 0)


def _const2(*_):
    return (0, 0)


def _norm_proj(x, g, sc, sh, wt, *, tm, tn, name):
    T, N = x.shape[0], wt.shape[0]
    tm, tn = _tile(T, tm), _tile(N, tn)

    def body(x_ref, g_ref, sc_ref, sh_ref, w_ref, h_ref, o_ref, hs):
        @pl.when(pl.program_id(1) == 0)
        def _():
            xv = x_ref[...]
            r = lax.rsqrt(jnp.mean(xv * xv, axis=-1, keepdims=True) + EPS)
            hb = ((xv * r) * g_ref[...] * (1.0 + sc_ref[...]) + sh_ref[...]).astype(BF)
            hs[...] = hb
            h_ref[...] = hb
        o_ref[...] = lax.dot_general(hs[...], w_ref[...], NT, preferred_element_type=F32).astype(BF)

    vec = pl.BlockSpec((1, D), _const2)
    return pl.pallas_call(
        body, name=name, grid=(T // tm, N // tn),
        in_specs=[pl.BlockSpec((tm, D), lambda i, j: (i, 0)), vec, vec, vec,
                  pl.BlockSpec((tn, D), lambda i, j: (j, 0))],
        out_specs=[pl.BlockSpec((tm, D), lambda i, j: (i, 0)), pl.BlockSpec((tm, tn), lambda i, j: (i, j))],
        out_shape=[jax.ShapeDtypeStruct((T, D), BF), jax.ShapeDtypeStruct((T, N), BF)],
        scratch_shapes=[pltpu.VMEM((tm, D), BF)],
        compiler_params=_cp(("parallel", "arbitrary")),
    )(x, g, sc, sh, wt)


def _ffn_down_fwd(ab, wd, x, gt, *, tm, name):
    T = x.shape[0]
    tm = _tile(T, tm)

    def body(a_ref, b_ref, wd_ref, x_ref, gt_ref, xo_ref, y_ref):
        y = None
        for c0 in range(0, F, MXU_N):
            cols = pl.ds(c0, MXU_N)
            a = a_ref[:, cols].astype(F32)
            act = (a * _sigmoid(a) * b_ref[:, cols].astype(F32)).astype(BF)
            part = jnp.dot(act, wd_ref[cols, :], preferred_element_type=F32)
            y = part if y is None else y + part
        y_ref[...] = y.astype(BF)
        xo_ref[...] = x_ref[...] + (0.5 * gt_ref[...]) * y

    return pl.pallas_call(
        body, name=name, grid=(T // tm,),
        in_specs=[pl.BlockSpec((tm, F), lambda i: (i, 0)), pl.BlockSpec((tm, F), lambda i: (i, 1)),
                  pl.BlockSpec((F, D), _const2), pl.BlockSpec((tm, D), _row), pl.BlockSpec((1, D), _const2)],
        out_specs=[pl.BlockSpec((tm, D), _row), pl.BlockSpec((tm, D), _row)],
        out_shape=[jax.ShapeDtypeStruct((T, D), F32), jax.ShapeDtypeStruct((T, D), BF)],
        compiler_params=_cp(("parallel",)),
    )(ab, ab, wd, x, gt)


def _final_fwd_bwd(x, tgt, g, *, tm, name):
    T = x.shape[0]
    tm = _tile(T, tm)

    def body(x_ref, t_ref, g_ref, dx_ref, ls_ref, dg_ref):
        @pl.when(pl.program_id(0) == 0)
        def _():
            ls_ref[...] = jnp.zeros_like(ls_ref)
            dg_ref[...] = jnp.zeros_like(dg_ref)
        xv = x_ref[...]
        gv = g_ref[...]
        r = lax.rsqrt(jnp.mean(xv * xv, axis=-1, keepdims=True) + EPS)
        xh = xv * r
        e = xh * gv - t_ref[...]
        ls_ref[...] += jnp.sum(e * e, axis=0, keepdims=True)
        dy = e * (1.0 / D)
        dg_ref[...] += jnp.sum(dy * xh, axis=0, keepdims=True)
        dxh = dy * gv
        dx_ref[...] = r * (dxh - xh * jnp.mean(dxh * xh, axis=-1, keepdims=True))

    vec = pl.BlockSpec((1, D), _const2)
    return pl.pallas_call(
        body, name=name, grid=(T // tm,),
        in_specs=[pl.BlockSpec((tm, D), _row), pl.BlockSpec((tm, D), _row), vec],
        out_specs=[pl.BlockSpec((tm, D), _row), vec, vec],
        out_shape=[jax.ShapeDtypeStruct((T, D), F32), jax.ShapeDtypeStruct((1, D), F32),
                   jax.ShapeDtypeStruct((1, D), F32)],
        compiler_params=_cp(("arbitrary",)),
    )(x, tgt, g)


def _ffn_down_bwd(dxo, y, gt, ab, wd, *, tm, tn, name):
    T = dxo.shape[0]
    tm = _tile(T, tm)

    def body(dxo_ref, y_ref, gt_ref, a_ref, b_ref, wd_ref, dy_ref, dab_ref, dgt_ref):
        @pl.when(pl.program_id(0) == 0)
        def _():
            dgt_ref[...] = jnp.zeros_like(dgt_ref)

        dxv = dxo_ref[...]
        dgt_ref[...] += 0.5 * jnp.sum(dxv * y_ref[...].astype(F32), axis=0, keepdims=True)
        dy = ((0.5 * gt_ref[...]) * dxv).astype(BF)
        dy_ref[...] = dy
        for c0 in range(0, F, tn):
            cols = pl.ds(c0, tn)
            dact = lax.dot_general(dy, wd_ref[cols, :], NT, preferred_element_type=F32)
            a = a_ref[:, cols].astype(F32)
            b = b_ref[:, cols].astype(F32)
            s = _sigmoid(a)
            dab_ref[0, :, cols] = (dact * b * (s * (1.0 + a * (1.0 - s)))).astype(BF)
            dab_ref[1, :, cols] = (dact * (a * s)).astype(BF)

    vec = pl.BlockSpec((1, D), _const2)
    rowspec = pl.BlockSpec((tm, D), _row)
    return pl.pallas_call(
        body, name=name, grid=(T // tm,),
        in_specs=[rowspec, rowspec, vec, pl.BlockSpec((tm, F), lambda i: (i, 0)),
                  pl.BlockSpec((tm, F), lambda i: (i, 1)), pl.BlockSpec((F, D), _const2)],
        out_specs=[rowspec, pl.BlockSpec((2, tm, F), lambda i: (0, i, 0)), vec],
        out_shape=[jax.ShapeDtypeStruct((T, D), BF), jax.ShapeDtypeStruct((2, T, F), BF),
                   jax.ShapeDtypeStruct((1, D), F32)],
        compiler_params=_cp(("arbitrary",)),
    )(dxo, y, gt, ab, ab, wd)


def _tn_matmul(a, b, *, tn, tk, name):
    S, T, Ns = a.shape
    tn, tk = _tile(Ns, tn), _tile(T, tk)
    nk, njs = T // tk, Ns // tn

    def body(a_ref, b_ref, o_ref, acc):
        k = pl.program_id(1)

        @pl.when(k == 0)
        def _():
            acc[...] = jnp.zeros_like(acc)
        acc[...] += lax.dot_general(a_ref[0], b_ref[...], TN, preferred_element_type=F32)

        @pl.when(k == nk - 1)
        def _():
            o_ref[...] = acc[...].astype(BF)

    return pl.pallas_call(
        body, name=name, grid=(S * njs, nk),
        in_specs=[pl.BlockSpec((1, tk, tn), lambda j, k: (j // njs, k, j % njs)),
                  pl.BlockSpec((tk, D), lambda j, k: (k, 0))],
        out_specs=pl.BlockSpec((tn, D), lambda j, k: (j, 0)),
        out_shape=jax.ShapeDtypeStruct((S * Ns, D), BF),
        scratch_shapes=[pltpu.VMEM((tn, D), F32)],
        compiler_params=_cp(("parallel", "arbitrary")),
    )(a, b)


def _tn_matmul_swiglu(ab, b, token, *, tn, tk, name):
    T = ab.shape[0]
    tn, tk = _tile(F, tn), _tile(T, tk)
    nk, nj = T // tk, F // tn
    deps = [] if token is None else [token]

    def body(a_ref, g_ref, b_ref, *rest):
        o_ref, acc = rest[len(deps):]
        k = pl.program_id(1)

        @pl.when(k == 0)
        def _():
            acc[...] = jnp.zeros_like(acc)
        bv = b_ref[...]
        for c0 in range(0, tn, MXU_N):
            cw = min(MXU_N, tn - c0)
            cols = pl.ds(c0, cw)
            a = a_ref[:, cols].astype(F32)
            act = (a * _sigmoid(a) * g_ref[:, cols].astype(F32)).astype(BF)
            acc[cols, :] += lax.dot_general(act, bv, TN, preferred_element_type=F32)

        @pl.when(k == nk - 1)
        def _():
            o_ref[...] = acc[...].astype(BF)

    return pl.pallas_call(
        body, name=name, grid=(nj, nk),
        in_specs=[pl.BlockSpec((tk, tn), lambda j, k: (k, j)), pl.BlockSpec((tk, tn), lambda j, k: (k, j + nj)),
                  pl.BlockSpec((tk, D), lambda j, k: (k, 0))] + [pl.BlockSpec(memory_space=pl.ANY)] * len(deps),
        out_specs=pl.BlockSpec((tn, D), lambda j, k: (j, 0)),
        out_shape=jax.ShapeDtypeStruct((F, D), BF),
        scratch_shapes=[pltpu.VMEM((tn, D), F32)],
        compiler_params=_cp(("parallel", "arbitrary")),
    )(ab, ab, b, *deps)


def _nn_bwd_norm(da, w, x, g, sc, dxo, *, tm, tk, name):
    S, T, Ks = da.shape
    tm, tk = _tile(T, tm), _tile(Ks, tk)
    nks = Ks // tk
    nk = S * nks
    rc = _tile(tm, 256)

    def body(da_ref, w_ref, x_ref, g_ref, sc_ref, dxo_ref, dx_ref, dsh_ref, dsc_ref, dg_ref, acc):
        i, k = pl.program_id(0), pl.program_id(1)

        @pl.when(jnp.logical_and(i == 0, k == 0))
        def _():
            dsh_ref[...] = jnp.zeros_like(dsh_ref)
            dsc_ref[...] = jnp.zeros_like(dsc_ref)
            dg_ref[...] = jnp.zeros_like(dg_ref)

        d = jnp.dot(da_ref[0], w_ref[...], preferred_element_type=F32)

        @pl.when(k == 0)
        def _():
            acc[...] = d

        @pl.when(k > 0)
        def _():
            acc[...] += d

        @pl.when(k == nk - 1)
        def _():
            gv = g_ref[...]
            sc1 = 1.0 + sc_ref[...]
            dsh = jnp.zeros((1, D), F32)
            dsc = jnp.zeros((1, D), F32)
            dg = jnp.zeros((1, D), F32)
            for r0 in range(0, tm, rc):
                rows = pl.ds(r0, rc)
                u = acc[rows, :]
                xv = x_ref[rows, :]
                r = lax.rsqrt(jnp.mean(xv * xv, axis=-1, keepdims=True) + EPS)
                xh = xv * r
                dsh = dsh + jnp.sum(u, axis=0, keepdims=True)
                dsc = dsc + jnp.sum(u * (xh * gv), axis=0, keepdims=True)
                us = u * sc1
                dg = dg + jnp.sum(us * xh, axis=0, keepdims=True)
                dxh = us * gv
                dx_ref[rows, :] = dxo_ref[rows, :] + r * (dxh - xh * jnp.mean(dxh * xh, axis=-1, keepdims=True))
            dsh_ref[...] += dsh
            dsc_ref[...] += dsc
            dg_ref[...] += dg

    vec = pl.BlockSpec((1, D), _const2)
    return pl.pallas_call(
        body, name=name, grid=(T // tm, nk),
        in_specs=[pl.BlockSpec((1, tm, tk), lambda i, k: (k // nks, i, k % nks)),
                  pl.BlockSpec((tk, D), lambda i, k: (k, 0)),
                  pl.BlockSpec((tm, D), lambda i, k: (i, 0)), vec, vec,
                  pl.BlockSpec((tm, D), lambda i, k: (i, 0))],
        out_specs=[pl.BlockSpec((tm, D), lambda i, k: (i, 0)), vec, vec, vec],
        out_shape=[jax.ShapeDtypeStruct((T, D), F32)] + [jax.ShapeDtypeStruct((1, D), F32)] * 3,
        scratch_shapes=[pltpu.VMEM((tm, D), F32)],
        compiler_params=_cp(("arbitrary", "arbitrary")),
    )(da, w, x, g, sc, dxo)


def _rope(t, cos, sin_signed, lt32, inverse=False):
    sel = jnp.where(lt32, pltpu.roll(t, 96, 1), pltpu.roll(t, 32, 1))
    return t * cos - sel * sin_signed if inverse else t * cos + sel * sin_signed


def _rope_tables(T):
    inv = 1.0 / (ROPE_THETA ** (jnp.arange(0, HEAD_DIM, 2, dtype=F32) / HEAD_DIM))
    ang = jnp.arange(T, dtype=F32)[:, None] * inv[None, :]
    cos, sin = jnp.cos(ang), jnp.sin(ang)
    cos128 = jnp.tile(cos, (1, 4))
    sin128 = jnp.tile(jnp.concatenate([-sin, sin], axis=1), (1, 2))
    return cos128, sin128


def _attn_specs(nb):
    qspec = pl.BlockSpec((BLK, D), lambda n: (n, O_Q // D))
    kc = pl.BlockSpec((BLK, 256), lambda n: (n, O_K // 256))
    kp = pl.BlockSpec((BLK, 256), lambda n: (jnp.maximum(n - 1, 0), O_K // 256))
    vc = pl.BlockSpec((BLK, 256), lambda n: (n, O_V // 256))
    vp = pl.BlockSpec((BLK, 256), lambda n: (jnp.maximum(n - 1, 0), O_V // 256))
    tc = pl.BlockSpec((BLK, 128), lambda n: (n, 0))
    tp = pl.BlockSpec((BLK, 128), lambda n: (jnp.maximum(n - 1, 0), 0))
    return [qspec, kc, kp, vc, vp, tc, tc, tp, tp, pl.BlockSpec(memory_space=pltpu.SMEM)]


def _attn_common(q_ref, kc_ref, kp_ref, vc_ref, vp_ref, cc_ref, sc_ref, cp_ref, sp_ref):
    n = pl.program_id(0)
    lane = lax.broadcasted_iota(jnp.int32, (BLK, 128), 1)
    lt32 = (lane % HEAD_DIM) < (HEAD_DIM // 2)
    cc, sc, cp, sp = cc_ref[...], sc_ref[...], cp_ref[...], sp_ref[...]
    kr, vr = [], []
    for r in range(2):
        cols = slice(r * 128, (r + 1) * 128)
        kcur = _rope(kc_ref[:, cols].astype(F32), cc, sc, lt32)
        kprev = _rope(kp_ref[:, cols].astype(F32), cp, sp, lt32)
        kr.append(jnp.concatenate([kprev, kcur], axis=0).astype(BF))
        vr.append(jnp.concatenate([vp_ref[:, cols], vc_ref[:, cols]], axis=0))
    qr = [_rope(q_ref[:, p * 128:(p + 1) * 128].astype(F32), cc, sc, lt32) for p in range(8)]
    qi = lax.broadcasted_iota(jnp.int32, (4 * BLK, 2 * BLK), 0) % BLK
    kj = lax.broadcasted_iota(jnp.int32, (4 * BLK, 2 * BLK), 1)
    valid = (kj > qi) & (kj <= qi + BLK) & ((kj >= BLK) | (n > 0))
    halves = [lane < HEAD_DIM, lane >= HEAD_DIM]
    return qr, kr, vr, valid, halves, lt32, (cc, sc, cp, sp)


def _stack_heads(chunks, g, halves):
    half = g % 2
    parts = []
    for hh in range(4):
        h = 4 * g + hh
        t = chunks[h // 2]
        if h % 2 != half:
            t = pltpu.roll(t, HEAD_DIM, 1)
        parts.append(jnp.where(halves[half], t, 0.0))
    return jnp.concatenate(parts, axis=0)


def _softmax_sink(s, valid, sink_ref, g):
    s = jnp.where(valid, s * (HEAD_DIM ** -0.5), NEG_INF)
    sink = jnp.concatenate([jnp.full((BLK, 1), sink_ref[4 * g + hh], F32) for hh in range(4)], axis=0)
    m = jnp.maximum(jnp.max(s, axis=-1, keepdims=True), sink)
    p = jnp.exp(s - m)
    ps = jnp.exp(sink - m)
    inv = 1.0 / (jnp.sum(p, axis=-1, keepdims=True) + ps)
    return p * inv, ps * inv


def _attn_fwd(proj, cos, sin, sinks, *, name):
    T = proj.shape[0]
    nb = T // BLK

    def body(q_ref, kc_ref, kp_ref, vc_ref, vp_ref, cc_ref, sc_ref, cp_ref, sp_ref, sink_ref, o_ref):
        qr, kr, vr, valid, halves, _, _ = _attn_common(q_ref, kc_ref, kp_ref, vc_ref, vp_ref,
                                                        cc_ref, sc_ref, cp_ref, sp_ref)
        outs = [jnp.zeros((BLK, 128), F32) for _ in range(8)]
        for g in range(N_KV):
            r, half = g // 2, g % 2
            qs = _stack_heads(qr, g, halves).astype(BF)
            s = lax.dot_general(qs, kr[r], NT, preferred_element_type=F32)
            p, _ = _softmax_sink(s, valid, sink_ref, g)
            o = jnp.dot(p.astype(BF), vr[r], preferred_element_type=F32)
            for hh in range(4):
                h = 4 * g + hh
                oh = jnp.where(halves[half], o[hh * BLK:(hh + 1) * BLK], 0.0)
                if h % 2 != half:
                    oh = pltpu.roll(oh, HEAD_DIM, 1)
                outs[h // 2] = outs[h // 2] + oh
        o_ref[...] = jnp.concatenate(outs, axis=1).astype(BF)

    return pl.pallas_call(
        body, name=name, grid=(nb,),
        in_specs=_attn_specs(nb),
        out_specs=pl.BlockSpec((BLK, D), _row),
        out_shape=jax.ShapeDtypeStruct((T, D), BF),
        compiler_params=_cp(("parallel",)),
    )(proj, proj, proj, proj, proj, cos, sin, cos, sin, sinks)


def _attn_bwd(proj, cos, sin, sinks, o, do, dproj, *, name):
    T = proj.shape[0]
    nb = T // BLK

    def body(q_ref, kc_ref, kp_ref, vc_ref, vp_ref, cc_ref, sc_ref, cp_ref, sp_ref, sink_ref, o_ref, do_ref,
             dproj_ref, dq_ref, dkc_ref, dkp_ref, dvc_ref, dvp_ref, dsink_ref):
        @pl.when(pl.program_id(0) == 0)
        def _():
            dsink_ref[...] = jnp.zeros_like(dsink_ref)
        qr, kr, vr, valid, halves, lt32, (cc, sc, cp, sp) = _attn_common(
            q_ref, kc_ref, kp_ref, vc_ref, vp_ref, cc_ref, sc_ref, cp_ref, sp_ref)
        oc = [o_ref[:, p * 128:(p + 1) * 128].astype(F32) for p in range(8)]
        doc = [do_ref[:, p * 128:(p + 1) * 128].astype(F32) for p in range(8)]
        dqs = [jnp.zeros((BLK, 128), F32) for _ in range(8)]
        dkr = [jnp.zeros((2 * BLK, 128), F32) for _ in range(2)]
        dvr = [jnp.zeros((2 * BLK, 128), F32) for _ in range(2)]
        lane1 = lax.broadcasted_iota(jnp.int32, (1, 128), 1)
        dsink = jnp.zeros((1, 128), F32)
        for g in range(N_KV):
            r, half = g // 2, g % 2
            qs = _stack_heads(qr, g, halves).astype(BF)
            dos = _stack_heads(doc, g, halves)
            os_ = _stack_heads(oc, g, halves)
            s = lax.dot_general(qs, kr[r], NT, preferred_element_type=F32)
            p, ps = _softmax_sink(s, valid, sink_ref, g)
            dosb = dos.astype(BF)
            dp = lax.dot_general(dosb, vr[r], NT, preferred_element_type=F32)
            delta = jnp.sum(dos * os_, axis=-1, keepdims=True)
            ds = (p * (dp - delta) * (HEAD_DIM ** -0.5)).astype(BF)
            dsk = -ps * delta
            for hh in range(4):
                val = jnp.sum(dsk[hh * BLK:(hh + 1) * BLK], axis=0, keepdims=True)
                dsink = dsink + jnp.where(lane1 == 4 * g + hh, val, 0.0)
            dvr[r] = dvr[r] + lax.dot_general(p.astype(BF), dosb, TN, preferred_element_type=F32)
            dkr[r] = dkr[r] + lax.dot_general(ds, qs, TN, preferred_element_type=F32)
            dq = jnp.dot(ds, kr[r], preferred_element_type=F32)
            for hh in range(4):
                h = 4 * g + hh
                dqh = jnp.where(halves[half], dq[hh * BLK:(hh + 1) * BLK], 0.0)
                if h % 2 != half:
                    dqh = pltpu.roll(dqh, HEAD_DIM, 1)
                dqs[h // 2] = dqs[h // 2] + dqh
        dsink_ref[...] += dsink
        dq_ref[...] = jnp.concatenate([_rope(t, cc, sc, lt32, inverse=True) for t in dqs], axis=1).astype(BF)
        dkp_ref[...] = jnp.concatenate([_rope(t[:BLK], cp, sp, lt32, inverse=True) for t in dkr], axis=1)
        dkc_ref[...] = jnp.concatenate([_rope(t[BLK:], cc, sc, lt32, inverse=True) for t in dkr], axis=1)
        dvp_ref[...] = jnp.concatenate([t[:BLK] for t in dvr], axis=1)
        dvc_ref[...] = jnp.concatenate([t[BLK:] for t in dvr], axis=1)

    kv = pl.BlockSpec((BLK, 256), _row)
    return pl.pallas_call(
        body, name=name, grid=(nb,),
        in_specs=_attn_specs(nb) + [pl.BlockSpec((BLK, D), _row), pl.BlockSpec((BLK, D), _row),
                                    pl.BlockSpec(memory_space=pl.ANY)],
        out_specs=[pl.BlockSpec((BLK, D), lambda n: (n, O_Q // D)), kv, kv, kv, kv, pl.BlockSpec((1, 128), _const2)],
        out_shape=[jax.ShapeDtypeStruct(dproj.shape, BF)] + [jax.ShapeDtypeStruct((T, 256), F32)] * 4
        + [jax.ShapeDtypeStruct((1, 128), F32)],
        input_output_aliases={12: 0},
        compiler_params=_cp(("arbitrary",)),
    )(proj, proj, proj, proj, proj, cos, sin, cos, sin, sinks, o, do, dproj)


def _dkv_combine(dkc, dkp, dvc, dvp, dproj, *, name):
    T = dkc.shape[0]
    nb = T // BLK

    def body(dkc_ref, dkp_ref, dvc_ref, dvp_ref, dproj_ref, o_ref):
        last = (pl.program_id(0) == nb - 1)
        keep = jnp.where(last, 0.0, 1.0)
        o_ref[:, 0:256] = (dkc_ref[...] + keep * dkp_ref[...]).astype(BF)
        o_ref[:, 256:512] = (dvc_ref[...] + keep * dvp_ref[...]).astype(BF)

    cur = pl.BlockSpec((BLK, 256), _row)
    nxt = pl.BlockSpec((BLK, 256), lambda n: (jnp.minimum(n + 1, nb - 1), 0))
    return pl.pallas_call(
        body, name=name, grid=(nb,),
        in_specs=[cur, nxt, cur, nxt, pl.BlockSpec(memory_space=pl.ANY)],
        out_specs=pl.BlockSpec((BLK, 512), lambda n: (n, O_K // 512)),
        out_shape=jax.ShapeDtypeStruct(dproj.shape, BF),
        input_output_aliases={4: 0},
        compiler_params=_cp(("parallel",)),
    )(dkc, dkp, dvc, dvp, dproj)


HALO = 16


def _conv_shifts(cu, hprev, tm):
    row = lax.broadcasted_iota(jnp.int32, cu.shape, 0)
    h1 = hprev[HALO - 1:HALO, :]
    h2 = hprev[HALO - 2:HALO - 1, :]
    m1 = jnp.where(row == 0, h1, pltpu.roll(cu, 1, 0))
    m2 = jnp.where(row == 0, h2, jnp.where(row == 1, h1, pltpu.roll(cu, 2, 0)))
    return m1, m2


def _mixer_mid_fwd(proj, attn, wcp, wap, wout, convw, x, gt, *, tm, name):
    T = x.shape[0]
    tm = _tile(T, tm)
    hb = tm // HALO

    def body(bg_ref, cg_ref, u_ref, hcg_ref, hu_ref, zc0_ref, zc1_ref, za0_ref, za1_ref, at_ref,
             wcp_ref, wap_ref, wout_ref, cw_ref, x_ref, gt_ref,
             x2_ref, gc_ref, yc_ref, ya_ref, mg_ref, o_ref):
        first = jnp.where(pl.program_id(0) == 0, 0.0, 1.0)
        cu = cg_ref[...].astype(F32) * u_ref[...].astype(F32)
        hprev = first * (hcg_ref[...].astype(F32) * hu_ref[...].astype(F32))
        m1, m2 = _conv_shifts(cu, hprev, tm)
        cv = cw_ref[0:1, :] * m2 + cw_ref[1:2, :] * m1 + cw_ref[2:3, :] * cu
        gc = (bg_ref[...].astype(F32) * cv).astype(BF)
        gc_ref[...] = gc
        yc = jnp.dot(gc, wcp_ref[...], preferred_element_type=F32)
        ya = jnp.dot(at_ref[...], wap_ref[...], preferred_element_type=F32)
        yc_ref[...] = yc.astype(BF)
        ya_ref[...] = ya.astype(BF)
        zc = jnp.concatenate([zc0_ref[...], zc1_ref[...]], axis=1).astype(F32)
        za = jnp.concatenate([za0_ref[...], za1_ref[...]], axis=1).astype(F32)
        mg = (_sigmoid(zc) * yc + _sigmoid(za) * ya).astype(BF)
        mg_ref[...] = mg
        o = jnp.dot(mg, wout_ref[...], preferred_element_type=F32)
        o_ref[...] = o.astype(BF)
        x2_ref[...] = x_ref[...] + gt_ref[...] * o

    wspec = pl.BlockSpec((D, D), _const2)
    rowspec = pl.BlockSpec((tm, D), _row)
    return pl.pallas_call(
        body, name=name, grid=(T // tm,),
        in_specs=[_col(tm, O_BG), _col(tm, O_CG), _col(tm, O_U), _halo_prev(hb, O_CG), _halo_prev(hb, O_U),
                  _col(tm, O_ZC, 512), _col(tm, O_ZC + 512, 512), _col(tm, O_ZA, 512), _col(tm, O_ZA + 512, 512),
                  rowspec, wspec, wspec, wspec, pl.BlockSpec((8, D), _const2), rowspec, pl.BlockSpec((1, D), _const2)],
        out_specs=[rowspec] * 6,
        out_shape=[jax.ShapeDtypeStruct((T, D), F32)] + [jax.ShapeDtypeStruct((T, D), BF)] * 5,
        compiler_params=_cp(("parallel",)),
    )(proj, proj, proj, proj, proj, proj, proj, proj, proj, attn, wcp, wap, wout, convw, x, gt)


def _col(tm, c, w=D):
    assert c % w == 0
    return pl.BlockSpec((tm, w), lambda i: (i, c // w))


def _halo_prev(hb, c):
    return pl.BlockSpec((HALO, D), lambda i: (jnp.maximum(i * hb - 1, 0), c // D))


def _halo_next(hb, nblk, c=0):
    return pl.BlockSpec((HALO, D), lambda i: (jnp.minimum((i + 1) * hb, nblk - 1), c // D))


def _mixer_mid_bwd(dx2, gt, o, proj, yc, ya, wout, wcp, wap, *, tm, name):
    T = dx2.shape[0]
    tm = _tile(T, tm)
    zw = 512
    nz = 2 * D // zw

    def body(dx_ref, gt_ref, o_ref, zc0_ref, zc1_ref, za0_ref, za1_ref, yc_ref, ya_ref, wout_ref, wcp_ref, wap_ref,
             dout_ref, dyc_ref, dya_ref, dgc_ref, dat_ref, dz_ref, dgt_ref, dzs):
        i, j = pl.program_id(0), pl.program_id(1)

        @pl.when(jnp.logical_and(i == 0, j == 0))
        def _():
            dgt_ref[...] = jnp.zeros_like(dgt_ref)

        @pl.when(j == 0)
        def _():
            dxv = dx_ref[...]
            dgt_ref[...] += jnp.sum(dxv * o_ref[...].astype(F32), axis=0, keepdims=True)
            dout = (gt_ref[...] * dxv).astype(BF)
            dout_ref[...] = dout
            dmg = lax.dot_general(dout, wout_ref[...], NT, preferred_element_type=F32)
            sc = _sigmoid(jnp.concatenate([zc0_ref[...], zc1_ref[...]], axis=1).astype(F32))
            sa = _sigmoid(jnp.concatenate([za0_ref[...], za1_ref[...]], axis=1).astype(F32))
            dyc = (dmg * sc).astype(BF)
            dya = (dmg * sa).astype(BF)
            dyc_ref[...] = dyc
            dya_ref[...] = dya
            dzs[:, 0:D] = (dmg * yc_ref[...].astype(F32) * (sc * (1.0 - sc))).astype(BF)
            dzs[:, D:2 * D] = (dmg * ya_ref[...].astype(F32) * (sa * (1.0 - sa))).astype(BF)
            dgc_ref[...] = lax.dot_general(dyc, wcp_ref[...], NT, preferred_element_type=F32).astype(BF)
            dat_ref[...] = lax.dot_general(dya, wap_ref[...], NT, preferred_element_type=F32).astype(BF)

        for jj in range(nz):
            @pl.when(j == jj)
            def _(jj=jj):
                dz_ref[...] = dzs[:, jj * zw:(jj + 1) * zw]

    nt = T // tm

    def ahead(i, j):
        return jnp.minimum(i + jnp.minimum(j, 1), nt - 1)

    def zcol(c):
        return pl.BlockSpec((tm, zw), lambda i, j: (ahead(i, j), c // zw))

    wspec = pl.BlockSpec((D, D), _const2)
    rowin = pl.BlockSpec((tm, D), lambda i, j: (ahead(i, j), 0))
    rowspec = pl.BlockSpec((tm, D), lambda i, j: (i, 0))
    vec = pl.BlockSpec((1, D), _const2)
    return pl.pallas_call(
        body, name=name, grid=(nt, nz),
        in_specs=[rowin, vec, rowin, zcol(O_ZC), zcol(O_ZC + zw), zcol(O_ZA), zcol(O_ZA + zw),
                  rowin, rowin, wspec, wspec, wspec],
        out_specs=[rowspec] * 5 + [pl.BlockSpec((tm, zw), lambda i, j: (i, O_ZC // zw + j)), vec],
        out_shape=[jax.ShapeDtypeStruct((T, D), BF)] * 5 + [jax.ShapeDtypeStruct((T, NIN), BF),
                                                            jax.ShapeDtypeStruct((1, D), F32)],
        scratch_shapes=[pltpu.VMEM((tm, 2 * D), BF)],
        compiler_params=_cp(("arbitrary", "arbitrary")),
    )(dx2, gt, o, proj, proj, proj, proj, yc, ya, wout, wcp, wap)


def _conv_bwd(dgc, proj, convw, dproj, *, tm, name):
    T = dgc.shape[0]
    tm = _tile(T, tm)
    hb = tm // HALO
    nblk = T // HALO
    nt = T // tm

    def body(dgc_ref, ndgc_ref, bg_ref, nbg_ref, cg_ref, u_ref, hcg_ref, hu_ref, cw_ref, dproj_ref, dp_ref, dcw_ref):
        i = pl.program_id(0)

        @pl.when(i == 0)
        def _():
            dcw_ref[...] = jnp.zeros_like(dcw_ref)
        first = jnp.where(i == 0, 0.0, 1.0)
        last = jnp.where(i == nt - 1, 0.0, 1.0)
        cg = cg_ref[...].astype(F32)
        u = u_ref[...].astype(F32)
        bg = bg_ref[...].astype(F32)
        dg = dgc_ref[...].astype(F32)
        cu = cg * u
        hprev = first * (hcg_ref[...].astype(F32) * hu_ref[...].astype(F32))
        m1, m2 = _conv_shifts(cu, hprev, tm)
        w0, w1, w2 = cw_ref[0:1, :], cw_ref[1:2, :], cw_ref[2:3, :]
        cv = w0 * m2 + w1 * m1 + w2 * cu
        dcv = dg * bg
        nxt = last * (ndgc_ref[...].astype(F32) * nbg_ref[...].astype(F32))
        n0, n1 = nxt[0:1, :], nxt[1:2, :]
        row = lax.broadcasted_iota(jnp.int32, dcv.shape, 0)
        p1 = jnp.where(row == tm - 1, n0, pltpu.roll(dcv, tm - 1, 0))
        p2 = jnp.where(row == tm - 1, n1, jnp.where(row == tm - 2, n0, pltpu.roll(dcv, tm - 2, 0)))
        dcu = w2 * dcv + w1 * p1 + w0 * p2
        dp_ref[:, 0:D] = (dg * cv).astype(BF)
        dp_ref[:, D:2 * D] = (dcu * u).astype(BF)
        dp_ref[:, 2 * D:3 * D] = (dcu * cg).astype(BF)
        dcw_ref[0:1, :] += jnp.sum(dcv * m2, axis=0, keepdims=True)
        dcw_ref[1:2, :] += jnp.sum(dcv * m1, axis=0, keepdims=True)
        dcw_ref[2:3, :] += jnp.sum(dcv * cu, axis=0, keepdims=True)

    rowspec = pl.BlockSpec((tm, D), _row)
    cw = pl.BlockSpec((8, D), _const2)
    return pl.pallas_call(
        body, name=name, grid=(nt,),
        in_specs=[rowspec, _halo_next(hb, nblk), _col(tm, O_BG), _halo_next(hb, nblk, O_BG),
                  _col(tm, O_CG), _col(tm, O_U), _halo_prev(hb, O_CG), _halo_prev(hb, O_U), cw,
                  pl.BlockSpec(memory_space=pl.ANY)],
        out_specs=[pl.BlockSpec((tm, 3 * D), _row), cw],
        out_shape=[jax.ShapeDtypeStruct(dproj.shape, BF), jax.ShapeDtypeStruct((8, D), F32)],
        input_output_aliases={9: 0},
        compiler_params=_cp(("arbitrary",)),
    )(dgc, dgc, proj, proj, proj, proj, proj, proj, convw, dproj)


def _adam(w, g, m, v, *, tm, name):
    _, R, C = w.shape
    tm = _tile(R, tm)
    parts = g.ndim == 3
    c1 = 1.0 - ADAM_B1
    c2 = 1.0 - ADAM_B2
    bc1 = 1.0 - ADAM_B1 ** ADAM_STEP
    bc2 = 1.0 - ADAM_B2 ** ADAM_STEP

    def body(w_ref, g_ref, m_ref, v_ref, go_ref, d_ref, nm_ref, nv_ref):
        if parts:
            gv = g_ref[0].astype(F32)
            for s in range(1, N_DEV):
                gv = gv + g_ref[s].astype(F32)
        else:
            gv = g_ref[...]
        go_ref[0] = gv
        nm = ADAM_B1 * m_ref[0] + c1 * gv
        nv = ADAM_B2 * v_ref[0] + c2 * (gv * gv)
        nm_ref[0] = nm
        nv_ref[0] = nv
        d_ref[0] = -ADAM_LR * ((nm / bc1) / (jnp.sqrt(nv / bc2) + ADAM_EPS) + ADAM_WD * w_ref[0])

    spec = pl.BlockSpec((1, tm, C), lambda i: (0, i, 0))
    gspec = pl.BlockSpec((N_DEV, tm, C), lambda i: (0, i, 0)) if parts else pl.BlockSpec((tm, C), _row)
    return pl.pallas_call(
        body, name=name, grid=(R // tm,),
        in_specs=[spec, gspec, spec, spec], out_specs=[spec] * 4,
        out_shape=[jax.ShapeDtypeStruct((1, R, C), F32)] * 4,
        compiler_params=_cp(("parallel",)),
    )(w, g, m, v)


def _mods_part(c_all, w_ada, b_ada, *, name):
    C = w_ada.shape[1]

    def body(c_ref, w_ref, b_ref, o_ref):
        cv = c_ref[...]
        ca = cv * jax.nn.sigmoid(cv)
        o_ref[...] = jnp.dot(ca, w_ref[...], preferred_element_type=F32,
                             precision=lax.Precision.HIGHEST) + b_ref[...]

    return pl.pallas_call(
        body, name=name,
        out_shape=jax.ShapeDtypeStruct((N_DEV, C), F32),
        compiler_params=_cp(),
    )(c_all, w_ada, b_ada)


def _wada_grad(c_all_t, gm, *, name):
    C = gm.shape[1]

    def body(c_ref, g_ref, o_ref):
        cv = c_ref[...]
        ca = cv * jax.nn.sigmoid(cv)
        acc = ca[:, 0:1] * g_ref[0:1, :]
        for b in range(1, N_DEV):
            acc = acc + ca[:, b:b + 1] * g_ref[b:b + 1, :]
        o_ref[...] = acc

    return pl.pallas_call(
        body, name=name,
        out_shape=jax.ShapeDtypeStruct((D, C), F32),
        compiler_params=_cp(),
    )(c_all_t, gm)


def _peer(x, y, c, d):
    px = lax.rem(x + ((d >> 2) & 1), 2)
    py = lax.rem(y + ((d >> 1) & 1), 2)
    pc = lax.rem(c + (d & 1), 2)
    return (px, py, pc), 4 * px + 2 * py + pc


def _exchange(xs, *, scatter, name):
    n = len(xs)
    nsem = n * (N_DEV - 1)

    def body(*refs):
        ins, outs = refs[:n], refs[n:2 * n]
        token, send_sems, recv_sems, local_sems = refs[2 * n:]
        x, y, c = lax.axis_index("x"), lax.axis_index("y"), lax.axis_index("c")
        me = 4 * x + 2 * y + c
        token[...] = jnp.zeros_like(token)

        def src(t, idx):
            return ins[t].at[idx] if scatter else ins[t]

        local = [pltpu.make_async_copy(src(t, me), outs[t].at[me], local_sems.at[t]) for t in range(n)]
        for cp in local:
            cp.start()
        remote = []
        for t in range(n):
            for d in range(1, N_DEV):
                peer, pidx = _peer(x, y, c, d)
                k = t * (N_DEV - 1) + d - 1
                send = pltpu.make_async_remote_copy(src_ref=src(t, pidx), dst_ref=outs[t].at[me],
                                                    send_sem=send_sems.at[k], recv_sem=recv_sems.at[k],
                                                    device_id=peer, device_id_type=MESH)
                recv = pltpu.make_async_remote_copy(src_ref=src(t, pidx), dst_ref=outs[t].at[pidx],
                                                    send_sem=send_sems.at[k], recv_sem=recv_sems.at[k],
                                                    device_id=peer, device_id_type=MESH)
                send.start()
                remote.append((send, recv))
        for cp in local:
            cp.wait()
        for send, recv in remote:
            send.wait_send()
            recv.wait_recv()

    anyspec = pl.BlockSpec(memory_space=pl.ANY)
    out_shape = [jax.ShapeDtypeStruct(a.shape if scatter else (N_DEV,) + a.shape, a.dtype) for a in xs]
    out_shape.append(jax.ShapeDtypeStruct((8, 128), F32))
    return pl.pallas_call(
        body, name=name,
        in_specs=[anyspec] * n, out_specs=[anyspec] * n + [pl.BlockSpec(memory_space=pltpu.VMEM)],
        out_shape=out_shape,
        scratch_shapes=[pltpu.SemaphoreType.DMA((nsem,)), pltpu.SemaphoreType.DMA((nsem,)),
                        pltpu.SemaphoreType.DMA((n,))],
    )(*xs)


def _sum8(parts, *, name):
    _, R, C = parts.shape

    def body(p_ref, o_ref):
        acc = p_ref[0]
        for s in range(1, N_DEV):
            acc = acc + p_ref[s]
        o_ref[...] = acc

    return pl.pallas_call(body, name=name, out_shape=jax.ShapeDtypeStruct((R, C), F32),
                          compiler_params=_cp())(parts)


HBM_SPEC = pl.BlockSpec(memory_space=pltpu.HBM)
SEM_SPEC = pl.BlockSpec(memory_space=pltpu.SEMAPHORE)
N_PEER = N_DEV - 1


def _split_copies(src_refs, land_refs, send_sems, recv_sems, scatter):
    x, y, c = lax.axis_index("x"), lax.axis_index("y"), lax.axis_index("c")
    me = 4 * x + 2 * y + c
    pairs = []
    for j, (src, land) in enumerate(zip(src_refs, land_refs)):
        for d in range(1, N_DEV):
            peer, pidx = _peer(x, y, c, d)
            k = j * N_PEER + d - 1
            s = src.at[pidx] if scatter else src
            send = pltpu.make_async_remote_copy(src_ref=s, dst_ref=land.at[me], send_sem=send_sems.at[k],
                                                recv_sem=recv_sems.at[k], device_id=peer, device_id_type=MESH)
            recv = pltpu.make_async_remote_copy(src_ref=s, dst_ref=land.at[pidx], send_sem=send_sems.at[k],
                                                recv_sem=recv_sems.at[k], device_id=peer, device_id_type=MESH)
            pairs.append((send, recv))
    return pairs


def _own_slot(block, me):
    land = lax.empty((N_DEV,) + block.shape, block.dtype)
    return lax.dynamic_update_slice(land, block[None], (me, 0, 0))


def _split_start(srcs, lands, groups, *, scatter, name):
    n, ng = len(srcs), len(groups)

    def body(*refs):
        src_refs, land_refs = refs[:n], refs[n:2 * n]
        sems = refs[2 * n:2 * n + 2 * ng]
        token = refs[-1]
        for gi, g in enumerate(groups):
            pairs = _split_copies([src_refs[t] for t in g], [land_refs[t] for t in g], sems[2 * gi],
                                  sems[2 * gi + 1], scatter)
            for send, _ in pairs:
                send.start()
        token[...] = jnp.zeros_like(token)

    sem_shapes = []
    for g in groups:
        sem_shapes += [pltpu.SemaphoreType.DMA((len(g) * N_PEER,))] * 2
    thru = [pltpu.HBM(a.shape, a.dtype) for a in list(srcs) + list(lands)]
    outs = pl.pallas_call(
        body, name=name,
        out_shape=tuple(sem_shapes + thru + [jax.ShapeDtypeStruct((8, 128), F32)]),
        in_specs=[HBM_SPEC] * (2 * n),
        out_specs=tuple([SEM_SPEC] * (2 * ng) + [HBM_SPEC] * (2 * n) + [pl.BlockSpec(memory_space=pltpu.VMEM)]),
        input_output_aliases={i: 2 * ng + i for i in range(2 * n)},
        compiler_params=pltpu.CompilerParams(has_side_effects=pltpu.SideEffectType.DATAFLOW_SIDE_EFFECTING),
    )(*[pltpu.with_memory_space_constraint(a, pltpu.HBM) for a in list(srcs) + list(lands)])
    sems = [(outs[2 * gi], outs[2 * gi + 1]) for gi in range(ng)]
    return sems, outs[2 * ng:2 * ng + n], outs[2 * ng + n:2 * ng + 2 * n], outs[-1]


def _behind(v, token):
    if token is None:
        return v
    return v + token[0, 0].astype(v.dtype)


def _split_wait(srcs, lands, sems, after, *, scatter, name):
    m = len(srcs)

    def body(*refs):
        src_refs, land_refs = refs[:m], refs[m:2 * m]
        send_sems, recv_sems = refs[2 * m], refs[2 * m + 1]
        for send, recv in _split_copies(src_refs, land_refs, send_sems, recv_sems, scatter):
            send.wait_send()
            recv.wait_recv()

    outs = pl.pallas_call(
        body, name=name,
        out_shape=tuple(pltpu.HBM(a.shape, a.dtype) for a in list(srcs) + list(lands)),
        in_specs=[HBM_SPEC] * (2 * m) + [SEM_SPEC, SEM_SPEC, pl.BlockSpec(memory_space=pl.ANY)],
        out_specs=tuple([HBM_SPEC] * (2 * m)),
        input_output_aliases={i: i for i in range(2 * m)},
        compiler_params=pltpu.CompilerParams(has_side_effects=pltpu.SideEffectType.DATAFLOW_SIDE_EFFECTING),
    )(*srcs, *lands, sems[0], sems[1], after)
    return outs[m:]


TM_PROJ = 1024
TM_ROW = 512
TM_NN = 1024
TK_TN = 512
TN_FFN = F // 2
TN_IN = NIN // 4


def _tn(a, b, name, tn):
    if a.ndim == 2:
        a = a[None]
    return _tn_matmul(a, b, tn=tn, tk=TK_TN, name=name)


def _local_step(x, tgt, mods, g1, gm, g2, gf, convw8, sinks, w_get, g_put):
    T = x.shape[0]
    sh1, sc1, gt1, sh2, sc2, gt2, sh3, sc3, gt3 = [mods[i:i + 1] for i in range(N_MOD)]
    cos, sin = _rope_tables(T)
    behind = _behind

    w = dict(w_get("gu1", mods))
    h1, ab1 = _norm_proj(x, g1, sc1, sh1, w["gu1"], tm=TM_PROJ, tn=TN_FFN, name="ffn1_up")
    w.update(w_get("d1", ab1))
    x1, y1 = _ffn_down_fwd(ab1, w["d1"], x, gt1, tm=TM_ROW, name="ffn1_down")
    w.update(w_get("mix", x1))
    h2, proj = _norm_proj(x1, gm, sc2, sh2, w["win"], tm=TM_PROJ, tn=TN_IN, name="mix_in")
    attn = _attn_fwd(proj, cos, sin, sinks, name="attn_fwd")
    x2, gc, yc, ya, mg, o = _mixer_mid_fwd(proj, attn, w["cp"], w["ap"], w["out"], convw8, x1, gt2,
                                           tm=TM_ROW, name="mix_mid")
    w.update(w_get("ffn2", x2))
    h3, ab2 = _norm_proj(x2, g2, sc3, sh3, w["gu2"], tm=TM_PROJ, tn=TN_FFN, name="ffn2_up")
    x3, y2 = _ffn_down_fwd(ab2, w["d2"], x2, gt3, tm=TM_ROW, name="ffn2_down")
    dx3, lsum, dgf = _final_fwd_bwd(x3, tgt, gf, tm=TM_ROW, name="final")

    dy2, dab2, dgt3 = _ffn_down_bwd(dx3, y2, gt3, ab2, w["d2"], tm=TM_ROW, tn=MXU_N, name="ffn2_down_bwd")
    g_d2 = _tn_matmul_swiglu(ab2, dy2, None, tn=TN_FFN, tk=TK_TN, name="ffn2_down_dw")
    dx2, dsh3, dsc3, dg2 = _nn_bwd_norm(dab2, w["gu2"], x2, g2, sc3, dx3, tm=TM_NN, tk=TN_FFN, name="ffn2_up_bwd")
    g_gu2 = _tn(dab2, h3, "ffn2_up_dw", TN_FFN)
    tok = g_put(dict(gu2=g_gu2, d2=g_d2))

    dout, dyc, dya, dgc, dat, dproj, dgt2 = _mixer_mid_bwd(dx2, behind(gt2, tok), o, proj, yc, ya, w["out"], w["cp"],
                                                           w["ap"], tm=TM_ROW, name="mix_mid_bwd")
    g_out = _tn(mg, dout, "mix_out_dw", D)
    g_cp = _tn(gc, dyc, "mix_cp_dw", D)
    g_ap = _tn(attn, dya, "mix_ap_dw", D)
    dproj, dkc, dkp, dvc, dvp, dsink = _attn_bwd(proj, cos, sin, sinks, attn, dat, dproj, name="attn_bwd")
    dproj = _dkv_combine(dkc, dkp, dvc, dvp, dproj, name="attn_dkv")
    dproj, dcw = _conv_bwd(dgc, proj, convw8, dproj, tm=TM_ROW, name="conv_bwd")
    g_in = _tn(dproj, h2, "mix_in_dw", TN_IN)
    tok = g_put(dict(win=g_in, cp=g_cp, ap=g_ap, out=g_out))
    dx1, dsh2, dsc2, dgm = _nn_bwd_norm(dproj[None], w["win"], x1, gm, behind(sc2, tok), dx2, tm=TM_NN, tk=TN_IN,
                                        name="mix_in_bwd")

    dy1, dab1, dgt1 = _ffn_down_bwd(dx1, y1, gt1, ab1, w["d1"], tm=TM_ROW, tn=MXU_N, name="ffn1_down_bwd")
    g_gu1 = _tn(dab1, h1, "ffn1_up_dw", TN_FFN)
    tok = g_put(dict(gu1=g_gu1))
    g_d1 = _tn_matmul_swiglu(ab1, dy1, tok, tn=TN_FFN, tk=TK_TN, name="ffn1_down_dw")
    tok = g_put(dict(d1=g_d1))
    dx0, dsh1, dsc1, dg1 = _nn_bwd_norm(dab1, w["gu1"], x, g1, behind(sc1, tok), dx1, tm=TM_NN, tk=TN_FFN,
                                        name="ffn1_up_bwd")

    small = dict(mods=jnp.concatenate([dsh1, dsc1, dgt1, dsh2, dsc2, dgt2, dsh3, dsc3, dgt3], axis=0),
                 g1=dg1, gm=dgm, g2=dg2, gf=dgf, convw=dcw[0:3], sinks=dsink[:, 0:N_HEADS])
    return lsum, dx0, small


BIG = ("gu1", "d1", "win", "cp", "ap", "out", "gu2", "d2")
TRANSPOSED = ("gu1", "win", "gu2")
SMALL_ROWS = 24
R_MODS, R_G1, R_GM, R_G2, R_GF, R_CONV, R_SINK = 0, 9, 10, 11, 12, 13, 16


def _pad_to(a, rows, cols):
    return jnp.pad(a, ((0, rows - a.shape[0]), (0, cols - a.shape[1])))


def _pack_small(b_ada, g1, gm, g2, gf, conv, sinks):
    rows = [b_ada.reshape(N_MOD, D), g1.reshape(1, D), gm.reshape(1, D), g2.reshape(1, D), gf.reshape(1, D),
            _pad_to(conv.reshape(3, -1), 3, D), _pad_to(sinks.reshape(1, N_HEADS), 1, D)]
    return _pad_to(jnp.concatenate(rows, axis=0), SMALL_ROWS, D)


def _unpack_small(p, conv_cols):
    return dict(b_ada=p[R_MODS:R_MODS + N_MOD].reshape(1, N_MOD * D), g_ffn1=p[R_G1:R_G1 + 1],
                g_mix=p[R_GM:R_GM + 1], g_ffn2=p[R_G2:R_G2 + 1], g_final=p[R_GF],
                conv_w=p[R_CONV:R_CONV + 3, 0:conv_cols][None], sinks=p[R_SINK:R_SINK + 1, 0:N_HEADS])


def kernel(x, c, w_ada, b_ada, g_ffn1, w1_gu, w1_down, g_mix, w_in, conv_w, w_conv_proj, w_attn_proj, sinks, w_out, g_ffn2, w2_gu, w2_down, g_final, loss_target, m_w_ada, m_b_ada, m_g_ffn1, m_w1_gu, m_w1_down, m_g_mix, m_w_in, m_conv_w, m_w_conv_proj, m_w_attn_proj, m_sinks, m_w_out, m_g_ffn2, m_w2_gu, m_w2_down, m_g_final, v_w_ada, v_b_ada, v_g_ffn1, v_w1_gu, v_w1_down, v_g_mix, v_w_in, v_conv_w, v_w_conv_proj, v_w_attn_proj, v_sinks, v_w_out, v_g_ffn2, v_w2_gu, v_w2_down, v_g_final):
    me = 4 * lax.axis_index("x") + 2 * lax.axis_index("y") + lax.axis_index("c")
    ada_cols = w_ada.shape[2]
    conv_cols = conv_w.shape[2]

    native = dict(gu1=w1_gu[0], d1=w1_down[0], win=w_in[0], cp=w_conv_proj[0], ap=w_attn_proj[0], out=w_out[0],
                  gu2=w2_gu[0], d2=w2_down[0])

    def shard(n, token):
        a = _behind(native[n], token)
        return (a.T if n in TRANSPOSED else a).astype(BF)

    c_all, conv_all, _ = _exchange([c, _pad_to(conv_w[0], 8, conv_cols)], scatter=False, name="gather_cond")
    c_all = c_all.reshape(N_DEV, D)
    conv_full = conv_all[:, 0:3, :].transpose(1, 0, 2).reshape(3, D)

    b_cols = lax.dynamic_slice(b_ada, (0, me * ada_cols), (1, ada_cols))
    mods_cols = _mods_part(c_all, w_ada[0], b_cols, name="ada_mods")
    mods_all, mods_token = _exchange([mods_cols], scatter=False, name="gather_mods")
    mods = lax.dynamic_index_in_dim(mods_all, me, axis=1, keepdims=False).reshape(N_MOD, D)

    groups = dict(gu1=("gu1",), d1=("d1",), mix=("win", "cp", "ap", "out"), ffn2=("gu2", "d2"))
    in_flight = {}
    first = [shard("gu1", mods_token)]
    sems, srcs, lands, token = _split_start(first, [_own_slot(s, me) for s in first], [[0]], scatter=False,
                                            name="gather_weights_start_gu1")
    in_flight["gu1"] = (sems[0], srcs, lands)
    rest = [n for n in BIG if n != "gu1"]
    shards = [shard(n, token) for n in rest]
    rest_groups = [[rest.index(n) for n in names] for g, names in groups.items() if g != "gu1"]
    sems, srcs, lands, rest_token = _split_start(shards, [_own_slot(s, me) for s in shards], rest_groups,
                                                 scatter=False, name="gather_weights_start_rest")
    for (g, names), gsems, idx in zip([kv for kv in groups.items() if kv[0] != "gu1"], sems, rest_groups):
        in_flight[g] = (gsems, [srcs[t] for t in idx], [lands[t] for t in idx])

    def w_get(group, after):
        if group == "gu1":
            after = rest_token
        gsems, gsrcs, glands = in_flight[group]
        landed = _split_wait(gsrcs, glands, gsems, after, scatter=False, name="gather_weights_wait_" + group)
        return {n: a.reshape(-1, D) for n, a in zip(groups[group], landed)}

    pending = []

    def g_put(gs):
        names = tuple(gs)
        srcs = [gs[n].reshape(N_DEV, -1, D) for n in names]
        lands = [_own_slot(lax.dynamic_index_in_dim(s, me, axis=0, keepdims=False), me) for s in srcs]
        sems, srcs, lands, tok = _split_start(srcs, lands, [list(range(len(names)))], scatter=True,
                                              name="scatter_grads_start_" + names[0])
        pending.append((names, sems[0], srcs, lands))
        return tok

    lsum, grad_x, small = _local_step(x[0], loss_target[0], mods, g_ffn1, g_mix, g_ffn2, g_final[None],
                                      _pad_to(conv_full, 8, D), sinks[0], w_get, g_put)
    loss = lax.psum((0.5 / D) * jnp.sum(lsum), ("x", "y", "c"))

    packed = _pack_small(small["mods"], small["g1"], small["gm"], small["g2"], small["gf"], small["convw"],
                         small["sinks"])
    packed_all, _ = _exchange([packed], scatter=False, name="gather_small")
    gsmall = _sum8(packed_all, name="sum_small")

    w_of = dict(ada=w_ada, gu1=w1_gu, d1=w1_down, win=w_in, cp=w_conv_proj, ap=w_attn_proj, out=w_out, gu2=w2_gu,
                d2=w2_down)
    m_of = dict(ada=m_w_ada, gu1=m_w1_gu, d1=m_w1_down, win=m_w_in, cp=m_w_conv_proj, ap=m_w_attn_proj, out=m_w_out,
                gu2=m_w2_gu, d2=m_w2_down)
    v_of = dict(ada=v_w_ada, gu1=v_w1_gu, d1=v_w1_down, win=v_w_in, cp=v_w_conv_proj, ap=v_w_attn_proj, out=v_w_out,
                gu2=v_w2_gu, d2=v_w2_down)
    upd = {}
    after = gsmall
    for names, sems, srcs, lands in pending:
        parts = _split_wait(srcs, lands, sems, after, scatter=True, name="scatter_grads_wait_" + names[0])
        for n, p in zip(names, parts):
            if n in TRANSPOSED:
                res = _adam(jnp.swapaxes(w_of[n], 1, 2), p, jnp.swapaxes(m_of[n], 1, 2), jnp.swapaxes(v_of[n], 1, 2),
                            tm=128, name="adam_" + n)
                upd[n] = [jnp.swapaxes(t, 1, 2) for t in res]
            else:
                upd[n] = _adam(w_of[n], p, m_of[n], v_of[n], tm=128, name="adam_" + n)
        after = upd[names[-1]][1]

    gm_cols = lax.dynamic_slice(packed_all[:, R_MODS:R_MODS + N_MOD, :].reshape(N_DEV, N_MOD * D),
                                (0, me * ada_cols), (N_DEV, ada_cols))
    upd["ada"] = _adam(w_ada, _wada_grad(c_all.T, gm_cols, name="ada_dw"), m_w_ada, v_w_ada, tm=128, name="adam_ada")
    conv_g = lax.dynamic_slice(gsmall[R_CONV:R_CONV + 3], (0, me * conv_cols), (3, conv_cols))
    gsmall_own = gsmall.at[R_CONV:R_CONV + 3].set(_pad_to(conv_g, 3, D))
    small_upd = _adam(_pack_small(b_ada, g_ffn1, g_mix, g_ffn2, g_final, conv_w, sinks)[None], gsmall_own,
                      _pack_small(m_b_ada, m_g_ffn1, m_g_mix, m_g_ffn2, m_g_final, m_conv_w, m_sinks)[None],
                      _pack_small(v_b_ada, v_g_ffn1, v_g_mix, v_g_ffn2, v_g_final, v_conv_w, v_sinks)[None],
                      tm=SMALL_ROWS, name="adam_small")
    small_out = [_unpack_small(p[0], conv_cols) for p in small_upd]

    big_name = dict(w_ada="ada", w1_gu="gu1", w1_down="d1", w_in="win", w_conv_proj="cp", w_attn_proj="ap",
                    w_out="out", w2_gu="gu2", w2_down="d2")
    order = ("w_ada", "b_ada", "g_ffn1", "w1_gu", "w1_down", "g_mix", "w_in", "conv_w", "w_conv_proj", "w_attn_proj",
             "sinks", "w_out", "g_ffn2", "w2_gu", "w2_down", "g_final")
    outs = [loss, grad_x[None]]
    for kind in range(4):
        for n in order:
            outs.append(upd[big_name[n]][kind] if n in big_name else small_out[kind][n])
    return tuple(outs)
```

```python
import functools

import jax
import jax.numpy as jnp
from jax import lax
from jax.experimental import pallas as pl
from jax.experimental.pallas import tpu as pltpu

D = 1024
F = 2816
NIN = 6656
N_HEADS = 16
N_KV = 4
HEAD_DIM = 64
BLK = 128
N_MOD = 9
N_DEV = 8
EPS = 1e-6
NEG_INF = -1e30
ROPE_THETA = 10000.0
O_BG, O_CG, O_U, O_Q, O_K, O_V, O_ZC, O_ZA = 0, 1024, 2048, 3072, 4096, 4352, 4608, 5632

ADAM_LR = 0.001
ADAM_B1 = 0.9
ADAM_B2 = 0.999
ADAM_EPS = 1e-08
ADAM_WD = 0.01
ADAM_STEP = 10

BF = jnp.bfloat16
F32 = jnp.float32
VMEM_LIMIT = 56 * 1024 * 1024
MXU_N = 256
MESH = pl.DeviceIdType.MESH

NT = (((1,), (1,)), ((), ()))
TN = (((0,), (0,)), ((), ()))


def _cp(sem=None):
    return pltpu.CompilerParams(dimension_semantics=sem, vmem_limit_bytes=VMEM_LIMIT)


def _tile(n, pref):
    if n <= pref:
        return n
    for t in range(pref - pref % 16, 15, -16):
        if n % t == 0:
            return t
    raise ValueError((n, pref))


def _sigmoid(v):
    return 0.5 * jnp.tanh(0.5 * v) + 0.5


def _row(i):
    return (i, 0)


def _const2(*_):
    return (0, 0)


def _norm_proj(x, g, sc, sh, wt, *, tm, tn, name):
    T, N = x.shape[0], wt.shape[0]
    tm, tn = _tile(T, tm), _tile(N, tn)

    def body(x_ref, g_ref, sc_ref, sh_ref, w_ref, h_ref, o_ref, hs):
        @pl.when(pl.program_id(1) == 0)
        def _():
            xv = x_ref[...]
            r = lax.rsqrt(jnp.mean(xv * xv, axis=-1, keepdims=True) + EPS)
            hb = ((xv * r) * g_ref[...] * (1.0 + sc_ref[...]) + sh_ref[...]).astype(BF)
            hs[...] = hb
            h_ref[...] = hb
        o_ref[...] = lax.dot_general(hs[...], w_ref[...], NT, preferred_element_type=F32).astype(BF)

    vec = pl.BlockSpec((1, D), _const2)
    return pl.pallas_call(
        body, name=name, grid=(T // tm, N // tn),
        in_specs=[pl.BlockSpec((tm, D), lambda i, j: (i, 0)), vec, vec, vec,
                  pl.BlockSpec((tn, D), lambda i, j: (j, 0))],
        out_specs=[pl.BlockSpec((tm, D), lambda i, j: (i, 0)), pl.BlockSpec((tm, tn), lambda i, j: (i, j))],
        out_shape=[jax.ShapeDtypeStruct((T, D), BF), jax.ShapeDtypeStruct((T, N), BF)],
        scratch_shapes=[pltpu.VMEM((tm, D), BF)],
        compiler_params=_cp(("parallel", "arbitrary")),
    )(x, g, sc, sh, wt)


def _ffn_down_fwd(ab, wd, x, gt, *, tm, name):
    T = x.shape[0]
    tm = _tile(T, tm)

    def body(a_ref, b_ref, wd_ref, x_ref, gt_ref, xo_ref, y_ref):
        y = None
        for c0 in range(0, F, MXU_N):
            cols = pl.ds(c0, MXU_N)
            a = a_ref[:, cols].astype(F32)
            act = (a * _sigmoid(a) * b_ref[:, cols].astype(F32)).astype(BF)
            part = jnp.dot(act, wd_ref[cols, :], preferred_element_type=F32)
            y = part if y is None else y + part
        y_ref[...] = y.astype(BF)
        xo_ref[...] = x_ref[...] + (0.5 * gt_ref[...]) * y

    return pl.pallas_call(
        body, name=name, grid=(T // tm,),
        in_specs=[pl.BlockSpec((tm, F), lambda i: (i, 0)), pl.BlockSpec((tm, F), lambda i: (i, 1)),
                  pl.BlockSpec((F, D), _const2), pl.BlockSpec((tm, D), _row), pl.BlockSpec((1, D), _const2)],
        out_specs=[pl.BlockSpec((tm, D), _row), pl.BlockSpec((tm, D), _row)],
        out_shape=[jax.ShapeDtypeStruct((T, D), F32), jax.ShapeDtypeStruct((T, D), BF)],
        compiler_params=_cp(("parallel",)),
    )(ab, ab, wd, x, gt)


def _final_fwd_bwd(x, tgt, g, *, tm, name):
    T = x.shape[0]
    tm = _tile(T, tm)

    def body(x_ref, t_ref, g_ref, dx_ref, ls_ref, dg_ref):
        @pl.when(pl.program_id(0) == 0)
        def _():
            ls_ref[...] = jnp.zeros_like(ls_ref)
            dg_ref[...] = jnp.zeros_like(dg_ref)
        xv = x_ref[...]
        gv = g_ref[...]
        r = lax.rsqrt(jnp.mean(xv * xv, axis=-1, keepdims=True) + EPS)
        xh = xv * r
        e = xh * gv - t_ref[...]
        ls_ref[...] += jnp.sum(e * e, axis=0, keepdims=True)
        dy = e * (1.0 / D)
        dg_ref[...] += jnp.sum(dy * xh, axis=0, keepdims=True)
        dxh = dy * gv
        dx_ref[...] = r * (dxh - xh * jnp.mean(dxh * xh, axis=-1, keepdims=True))

    vec = pl.BlockSpec((1, D), _const2)
    return pl.pallas_call(
        body, name=name, grid=(T // tm,),
        in_specs=[pl.BlockSpec((tm, D), _row), pl.BlockSpec((tm, D), _row), vec],
        out_specs=[pl.BlockSpec((tm, D), _row), vec, vec],
        out_shape=[jax.ShapeDtypeStruct((T, D), F32), jax.ShapeDtypeStruct((1, D), F32),
                   jax.ShapeDtypeStruct((1, D), F32)],
        compiler_params=_cp(("arbitrary",)),
    )(x, tgt, g)


def _ffn_down_bwd(dxo, y, gt, ab, wd, *, tm, tn, name):
    T = dxo.shape[0]
    tm = _tile(T, tm)

    def body(dxo_ref, y_ref, gt_ref, a_ref, b_ref, wd_ref, dy_ref, dab_ref, dgt_ref):
        @pl.when(pl.program_id(0) == 0)
        def _():
            dgt_ref[...] = jnp.zeros_like(dgt_ref)

        dxv = dxo_ref[...]
        dgt_ref[...] += 0.5 * jnp.sum(dxv * y_ref[...].astype(F32), axis=0, keepdims=True)
        dy = ((0.5 * gt_ref[...]) * dxv).astype(BF)
        dy_ref[...] = dy
        for c0 in range(0, F, tn):
            cols = pl.ds(c0, tn)
            dact = lax.dot_general(dy, wd_ref[cols, :], NT, preferred_element_type=F32)
            a = a_ref[:, cols].astype(F32)
            b = b_ref[:, cols].astype(F32)
            s = _sigmoid(a)
            dab_ref[0, :, cols] = (dact * b * (s * (1.0 + a * (1.0 - s)))).astype(BF)
            dab_ref[1, :, cols] = (dact * (a * s)).astype(BF)

    vec = pl.BlockSpec((1, D), _const2)
    rowspec = pl.BlockSpec((tm, D), _row)
    return pl.pallas_call(
        body, name=name, grid=(T // tm,),
        in_specs=[rowspec, rowspec, vec, pl.BlockSpec((tm, F), lambda i: (i, 0)),
                  pl.BlockSpec((tm, F), lambda i: (i, 1)), pl.BlockSpec((F, D), _const2)],
        out_specs=[rowspec, pl.BlockSpec((2, tm, F), lambda i: (0, i, 0)), vec],
        out_shape=[jax.ShapeDtypeStruct((T, D), BF), jax.ShapeDtypeStruct((2, T, F), BF),
                   jax.ShapeDtypeStruct((1, D), F32)],
        compiler_params=_cp(("arbitrary",)),
    )(dxo, y, gt, ab, ab, wd)


def _tn_matmul(a, b, *, tn, tk, name):
    S, T, Ns = a.shape
    tn, tk = _tile(Ns, tn), _tile(T, tk)
    nk, njs = T // tk, Ns // tn

    def body(a_ref, b_ref, o_ref, acc):
        k = pl.program_id(1)

        @pl.when(k == 0)
        def _():
            acc[...] = jnp.zeros_like(acc)
        acc[...] += lax.dot_general(a_ref[0], b_ref[...], TN, preferred_element_type=F32)

        @pl.when(k == nk - 1)
        def _():
            o_ref[...] = acc[...].astype(BF)

    return pl.pallas_call(
        body, name=name, grid=(S * njs, nk),
        in_specs=[pl.BlockSpec((1, tk, tn), lambda j, k: (j // njs, k, j % njs)),
                  pl.BlockSpec((tk, D), lambda j, k: (k, 0))],
        out_specs=pl.BlockSpec((tn, D), lambda j, k: (j, 0)),
        out_shape=jax.ShapeDtypeStruct((S * Ns, D), BF),
        scratch_shapes=[pltpu.VMEM((tn, D), F32)],
        compiler_params=_cp(("parallel", "arbitrary")),
    )(a, b)


def _tn_matmul_swiglu(ab, b, token, *, tn, tk, name):
    T = ab.shape[0]
    tn, tk = _tile(F, tn), _tile(T, tk)
    nk, nj = T // tk, F // tn
    deps = [] if token is None else [token]

    def body(a_ref, g_ref, b_ref, *rest):
        o_ref, acc = rest[len(deps):]
        k = pl.program_id(1)

        @pl.when(k == 0)
        def _():
            acc[...] = jnp.zeros_like(acc)
        bv = b_ref[...]
        for c0 in range(0, tn, MXU_N):
            cw = min(MXU_N, tn - c0)
            cols = pl.ds(c0, cw)
            a = a_ref[:, cols].astype(F32)
            act = (a * _sigmoid(a) * g_ref[:, cols].astype(F32)).astype(BF)
            acc[cols, :] += lax.dot_general(act, bv, TN, preferred_element_type=F32)

        @pl.when(k == nk - 1)
        def _():
            o_ref[...] = acc[...].astype(BF)

    return pl.pallas_call(
        body, name=name, grid=(nj, nk),
        in_specs=[pl.BlockSpec((tk, tn), lambda j, k: (k, j)), pl.BlockSpec((tk, tn), lambda j, k: (k, j + nj)),
                  pl.BlockSpec((tk, D), lambda j, k: (k, 0))] + [pl.BlockSpec(memory_space=pl.ANY)] * len(deps),
        out_specs=pl.BlockSpec((tn, D), lambda j, k: (j, 0)),
        out_shape=jax.ShapeDtypeStruct((F, D), BF),
        scratch_shapes=[pltpu.VMEM((tn, D), F32)],
        compiler_params=_cp(("parallel", "arbitrary")),
    )(ab, ab, b, *deps)


def _nn_bwd_norm(da, w, x, g, sc, dxo, *, tm, tk, name):
    S, T, Ks = da.shape
    tm, tk = _tile(T, tm), _tile(Ks, tk)
    nks = Ks // tk
    nk = S * nks
    rc = _tile(tm, 256)

    def body(da_ref, w_ref, x_ref, g_ref, sc_ref, dxo_ref, dx_ref, dsh_ref, dsc_ref, dg_ref, acc):
        i, k = pl.program_id(0), pl.program_id(1)

        @pl.when(jnp.logical_and(i == 0, k == 0))
        def _():
            dsh_ref[...] = jnp.zeros_like(dsh_ref)
            dsc_ref[...] = jnp.zeros_like(dsc_ref)
            dg_ref[...] = jnp.zeros_like(dg_ref)

        d = jnp.dot(da_ref[0], w_ref[...], preferred_element_type=F32)

        @pl.when(k == 0)
        def _():
            acc[...] = d

        @pl.when(k > 0)
        def _():
            acc[...] += d

        @pl.when(k == nk - 1)
        def _():
            gv = g_ref[...]
            sc1 = 1.0 + sc_ref[...]
            dsh = jnp.zeros((1, D), F32)
            dsc = jnp.zeros((1, D), F32)
            dg = jnp.zeros((1, D), F32)
            for r0 in range(0, tm, rc):
                rows = pl.ds(r0, rc)
                u = acc[rows, :]
                xv = x_ref[rows, :]
                r = lax.rsqrt(jnp.mean(xv * xv, axis=-1, keepdims=True) + EPS)
                xh = xv * r
                dsh = dsh + jnp.sum(u, axis=0, keepdims=True)
                dsc = dsc + jnp.sum(u * (xh * gv), axis=0, keepdims=True)
                us = u * sc1
                dg = dg + jnp.sum(us * xh, axis=0, keepdims=True)
                dxh = us * gv
                dx_ref[rows, :] = dxo_ref[rows, :] + r * (dxh - xh * jnp.mean(dxh * xh, axis=-1, keepdims=True))
            dsh_ref[...] += dsh
            dsc_ref[...] += dsc
            dg_ref[...] += dg

    vec = pl.BlockSpec((1, D), _const2)
    return pl.pallas_call(
        body, name=name, grid=(T // tm, nk),
        in_specs=[pl.BlockSpec((1, tm, tk), lambda i, k: (k // nks, i, k % nks)),
                  pl.BlockSpec((tk, D), lambda i, k: (k, 0)),
                  pl.BlockSpec((tm, D), lambda i, k: (i, 0)), vec, vec,
                  pl.BlockSpec((tm, D), lambda i, k: (i, 0))],
        out_specs=[pl.BlockSpec((tm, D), lambda i, k: (i, 0)), vec, vec, vec],
        out_shape=[jax.ShapeDtypeStruct((T, D), F32)] + [jax.ShapeDtypeStruct((1, D), F32)] * 3,
        scratch_shapes=[pltpu.VMEM((tm, D), F32)],
        compiler_params=_cp(("arbitrary", "arbitrary")),
    )(da, w, x, g, sc, dxo)


def _rope(t, cos, sin_signed, lt32, inverse=False):
    sel = jnp.where(lt32, pltpu.roll(t, 96, 1), pltpu.roll(t, 32, 1))
    return t * cos - sel * sin_signed if inverse else t * cos + sel * sin_signed


def _rope_tables(T):
    inv = 1.0 / (ROPE_THETA ** (jnp.arange(0, HEAD_DIM, 2, dtype=F32) / HEAD_DIM))
    ang = jnp.arange(T, dtype=F32)[:, None] * inv[None, :]
    cos, sin = jnp.cos(ang), jnp.sin(ang)
    cos128 = jnp.tile(cos, (1, 4))
    sin128 = jnp.tile(jnp.concatenate([-sin, sin], axis=1), (1, 2))
    return cos128, sin128


QSCALE = HEAD_DIM ** -0.5


def _lane_masks(rows):
    lane = lax.broadcasted_iota(jnp.int32, (rows, 128), 1)
    return (lane % HEAD_DIM) < (HEAD_DIM // 2), [lane < HEAD_DIM, lane >= HEAD_DIM]


def _attn_bias():
    qi = lax.broadcasted_iota(jnp.int32, (4 * BLK, 2 * BLK), 0) % BLK
    kj = lax.broadcasted_iota(jnp.int32, (4 * BLK, 2 * BLK), 1)
    band = (kj > qi) & (kj <= qi + BLK)
    return jnp.stack([jnp.where(band & (kj >= BLK), 0.0, NEG_INF), jnp.where(band, 0.0, NEG_INF)]).astype(F32)


def _attn_prep(proj, cos, sin, *, name):
    T = proj.shape[0]
    nb = T // BLK

    def body(q_ref, k_ref, c_ref, s_ref, qs_ref, kr_ref):
        lt32, halves = _lane_masks(BLK)
        cc, sc = c_ref[...], s_ref[...]
        qr = [_rope(q_ref[:, p * 128:(p + 1) * 128].astype(F32), cc, sc, lt32) * QSCALE for p in range(8)]
        for g in range(N_KV):
            qs_ref[g] = _stack_heads(qr, g, halves).astype(BF)
        kr_ref[...] = jnp.concatenate([_rope(k_ref[:, r * 128:(r + 1) * 128].astype(F32), cc, sc, lt32)
                                       for r in range(2)], axis=1).astype(BF)

    tab = pl.BlockSpec((BLK, 128), _row)
    return pl.pallas_call(
        body, name=name, grid=(nb,),
        in_specs=[pl.BlockSpec((BLK, D), lambda n: (n, O_Q // D)), pl.BlockSpec((BLK, 256), lambda n: (n, O_K // 256)),
                  tab, tab],
        out_specs=[pl.BlockSpec((N_KV, 4 * BLK, 128), lambda n: (0, n, 0)), pl.BlockSpec((BLK, 256), _row)],
        out_shape=[jax.ShapeDtypeStruct((N_KV, 4 * T, 128), BF), jax.ShapeDtypeStruct((T, 256), BF)],
        compiler_params=_cp(("parallel",)),
    )(proj, proj, cos, sin)


def _attn_specs():
    prev = lambda n: jnp.maximum(n - 1, 0)
    return [pl.BlockSpec((N_KV, 4 * BLK, 128), lambda n: (0, n, 0)),
            pl.BlockSpec((BLK, 256), _row), pl.BlockSpec((BLK, 256), lambda n: (prev(n), 0)),
            pl.BlockSpec((BLK, 256), lambda n: (n, O_V // 256)),
            pl.BlockSpec((BLK, 256), lambda n: (prev(n), O_V // 256)),
            pl.BlockSpec((1, 4 * BLK, 2 * BLK), lambda n: (jnp.minimum(n, 1), 0, 0)),
            pl.BlockSpec(memory_space=pltpu.SMEM)]


def _bands(kc_ref, kp_ref, vc_ref, vp_ref):
    kb, vb = [], []
    for r in range(2):
        cols = slice(r * 128, (r + 1) * 128)
        kb.append(jnp.concatenate([kp_ref[:, cols], kc_ref[:, cols]], axis=0))
        vb.append(jnp.concatenate([vp_ref[:, cols], vc_ref[:, cols]], axis=0))
    return kb, vb


def _sink_col(sink_ref, g):
    return jnp.concatenate([jnp.full((BLK, 1), sink_ref[4 * g + hh], F32) for hh in range(4)], axis=0)


def _unstack_heads(t, g, halves, acc):
    half = g % 2
    for hh in range(4):
        h = 4 * g + hh
        th = jnp.where(halves[half], t[hh * BLK:(hh + 1) * BLK], 0.0)
        if h % 2 != half:
            th = pltpu.roll(th, HEAD_DIM, 1)
        acc[h // 2] = acc[h // 2] + th


def _stack_heads(chunks, g, halves):
    half = g % 2
    parts = []
    for hh in range(4):
        h = 4 * g + hh
        t = chunks[h // 2]
        if h % 2 != half:
            t = pltpu.roll(t, HEAD_DIM, 1)
        parts.append(jnp.where(halves[half], t, 0.0))
    return jnp.concatenate(parts, axis=0)


def _attn_fwd(qs, kr, proj, bias, sinks, *, name):
    T = proj.shape[0]
    nb = T // BLK

    def body(qs_ref, kc_ref, kp_ref, vc_ref, vp_ref, bias_ref, sink_ref, o_ref, lse_ref):
        _, h128 = _lane_masks(BLK)
        _, h256 = _lane_masks(2 * BLK)
        _, h512 = _lane_masks(4 * BLK)
        kb, vb = _bands(kc_ref, kp_ref, vc_ref, vp_ref)
        outs = [jnp.zeros((BLK, 128), F32) for _ in range(8)]
        for g in range(N_KV):
            r, half = g // 2, g % 2
            s = lax.dot_general(qs_ref[g], kb[r], NT, preferred_element_type=F32) + bias_ref[0]
            sink = _sink_col(sink_ref, g)
            m = jnp.maximum(jnp.max(s, axis=-1, keepdims=True), sink)
            p = jnp.exp(s - m).astype(BF)
            vg = jnp.where(h256[half], vb[r].astype(F32), 1.0).astype(BF)
            o = jnp.dot(p, vg, preferred_element_type=F32)
            rowsum = jnp.where(h512[half], pltpu.roll(o, HEAD_DIM, 1), o)
            denom = rowsum + jnp.exp(sink - m)
            lse_ref[g] = m + jnp.log(denom)
            _unstack_heads(o * (1.0 / denom), g, h128, outs)
        o_ref[...] = jnp.concatenate(outs, axis=1).astype(BF)

    return pl.pallas_call(
        body, name=name, grid=(nb,),
        in_specs=_attn_specs(),
        out_specs=[pl.BlockSpec((BLK, D), _row), pl.BlockSpec((N_KV, 4 * BLK, 128), lambda n: (0, n, 0))],
        out_shape=[jax.ShapeDtypeStruct((T, D), BF), jax.ShapeDtypeStruct((N_KV, 4 * T, 128), F32)],
        compiler_params=_cp(("parallel",)),
    )(qs, kr, kr, proj, proj, bias, sinks)


def _attn_bwd(qs, kr, proj, bias, sinks, lse, o, do, cos, sin, dproj, *, name):
    T = proj.shape[0]
    nb = T // BLK

    def body(qs_ref, kc_ref, kp_ref, vc_ref, vp_ref, bias_ref, sink_ref, lse_ref, o_ref, do_ref,
             cc_ref, sc_ref, cp_ref, sp_ref, dproj_ref, dq_ref, dkc_ref, dkp_ref, dvc_ref, dvp_ref, dsink_ref):
        @pl.when(pl.program_id(0) == 0)
        def _():
            dsink_ref[...] = jnp.zeros_like(dsink_ref)
        lt32, h128 = _lane_masks(BLK)
        kb, vb = _bands(kc_ref, kp_ref, vc_ref, vp_ref)
        oc = [o_ref[:, p * 128:(p + 1) * 128].astype(F32) for p in range(8)]
        doc = [do_ref[:, p * 128:(p + 1) * 128].astype(F32) for p in range(8)]
        dqs = [jnp.zeros((BLK, 128), F32) for _ in range(8)]
        dkr = [jnp.zeros((2 * BLK, 128), F32) for _ in range(2)]
        dvr = [jnp.zeros((2 * BLK, 128), F32) for _ in range(2)]
        lane1 = lax.broadcasted_iota(jnp.int32, (1, 128), 1)
        dsink = jnp.zeros((1, 128), F32)
        for g in range(N_KV):
            r = g // 2
            q = qs_ref[g]
            lse_g = lse_ref[g]
            s = lax.dot_general(q, kb[r], NT, preferred_element_type=F32) + bias_ref[0]
            p = jnp.exp(s - jnp.concatenate([lse_g, lse_g], axis=1))
            dos = _stack_heads(doc, g, h128)
            dosb = dos.astype(BF)
            dp = lax.dot_general(dosb, vb[r], NT, preferred_element_type=F32)
            delta = jnp.sum(dos * _stack_heads(oc, g, h128), axis=-1, keepdims=True)
            ds = (p * (dp - delta)).astype(BF)
            dsk = -jnp.exp(_sink_col(sink_ref, g) - lse_g[:, 0:1]) * delta
            for hh in range(4):
                val = jnp.sum(dsk[hh * BLK:(hh + 1) * BLK], axis=0, keepdims=True)
                dsink = dsink + jnp.where(lane1 == 4 * g + hh, val, 0.0)
            dvr[r] = dvr[r] + lax.dot_general(p.astype(BF), dosb, TN, preferred_element_type=F32)
            dkr[r] = dkr[r] + lax.dot_general(ds, q, TN, preferred_element_type=F32)
            _unstack_heads(jnp.dot(ds, kb[r], preferred_element_type=F32) * QSCALE, g, h128, dqs)
        cc, sc, cp, sp = cc_ref[...], sc_ref[...], cp_ref[...], sp_ref[...]
        dsink_ref[...] += dsink
        dq_ref[...] = jnp.concatenate([_rope(t, cc, sc, lt32, inverse=True) for t in dqs], axis=1).astype(BF)
        dkp_ref[...] = jnp.concatenate([_rope(t[:BLK], cp, sp, lt32, inverse=True) for t in dkr], axis=1)
        dkc_ref[...] = jnp.concatenate([_rope(t[BLK:], cc, sc, lt32, inverse=True) for t in dkr], axis=1)
        dvp_ref[...] = jnp.concatenate([t[:BLK] for t in dvr], axis=1)
        dvc_ref[...] = jnp.concatenate([t[BLK:] for t in dvr], axis=1)

    kv = pl.BlockSpec((BLK, 256), _row)
    tc = pl.BlockSpec((BLK, 128), _row)
    tp = pl.BlockSpec((BLK, 128), lambda n: (jnp.maximum(n - 1, 0), 0))
    return pl.pallas_call(
        body, name=name, grid=(nb,),
        in_specs=_attn_specs() + [pl.BlockSpec((N_KV, 4 * BLK, 128), lambda n: (0, n, 0)),
                                  pl.BlockSpec((BLK, D), _row), pl.BlockSpec((BLK, D), _row), tc, tc, tp, tp,
                                  pl.BlockSpec(memory_space=pl.ANY)],
        out_specs=[pl.BlockSpec((BLK, D), lambda n: (n, O_Q // D)), kv, kv, kv, kv, pl.BlockSpec((1, 128), _const2)],
        out_shape=[jax.ShapeDtypeStruct(dproj.shape, BF)] + [jax.ShapeDtypeStruct((T, 256), F32)] * 4
        + [jax.ShapeDtypeStruct((1, 128), F32)],
        input_output_aliases={14: 0},
        compiler_params=_cp(("arbitrary",)),
    )(qs, kr, kr, proj, proj, bias, sinks, lse, o, do, cos, sin, cos, sin, dproj)


def _dkv_combine(dkc, dkp, dvc, dvp, dproj, *, name):
    T = dkc.shape[0]
    nb = T // BLK

    def body(dkc_ref, dkp_ref, dvc_ref, dvp_ref, dproj_ref, o_ref):
        last = (pl.program_id(0) == nb - 1)
        keep = jnp.where(last, 0.0, 1.0)
        o_ref[:, 0:256] = (dkc_ref[...] + keep * dkp_ref[...]).astype(BF)
        o_ref[:, 256:512] = (dvc_ref[...] + keep * dvp_ref[...]).astype(BF)

    cur = pl.BlockSpec((BLK, 256), _row)
    nxt = pl.BlockSpec((BLK, 256), lambda n: (jnp.minimum(n + 1, nb - 1), 0))
    return pl.pallas_call(
        body, name=name, grid=(nb,),
        in_specs=[cur, nxt, cur, nxt, pl.BlockSpec(memory_space=pl.ANY)],
        out_specs=pl.BlockSpec((BLK, 512), lambda n: (n, O_K // 512)),
        out_shape=jax.ShapeDtypeStruct(dproj.shape, BF),
        input_output_aliases={4: 0},
        compiler_params=_cp(("parallel",)),
    )(dkc, dkp, dvc, dvp, dproj)


HALO = 16


def _conv_shifts(cu, hprev, tm):
    row = lax.broadcasted_iota(jnp.int32, cu.shape, 0)
    h1 = hprev[HALO - 1:HALO, :]
    h2 = hprev[HALO - 2:HALO - 1, :]
    m1 = jnp.where(row == 0, h1, pltpu.roll(cu, 1, 0))
    m2 = jnp.where(row == 0, h2, jnp.where(row == 1, h1, pltpu.roll(cu, 2, 0)))
    return m1, m2


def _mixer_mid_fwd(proj, attn, wcp, wap, wout, convw, x, gt, *, tm, name):
    T = x.shape[0]
    tm = _tile(T, tm)
    hb = tm // HALO

    def body(bg_ref, cg_ref, u_ref, hcg_ref, hu_ref, zc0_ref, zc1_ref, za0_ref, za1_ref, at_ref,
             wcp_ref, wap_ref, wout_ref, cw_ref, x_ref, gt_ref,
             x2_ref, gc_ref, yc_ref, ya_ref, mg_ref, o_ref):
        first = jnp.where(pl.program_id(0) == 0, 0.0, 1.0)
        cu = cg_ref[...].astype(F32) * u_ref[...].astype(F32)
        hprev = first * (hcg_ref[...].astype(F32) * hu_ref[...].astype(F32))
        m1, m2 = _conv_shifts(cu, hprev, tm)
        cv = cw_ref[0:1, :] * m2 + cw_ref[1:2, :] * m1 + cw_ref[2:3, :] * cu
        gc = (bg_ref[...].astype(F32) * cv).astype(BF)
        gc_ref[...] = gc
        yc = jnp.dot(gc, wcp_ref[...], preferred_element_type=F32)
        ya = jnp.dot(at_ref[...], wap_ref[...], preferred_element_type=F32)
        yc_ref[...] = yc.astype(BF)
        ya_ref[...] = ya.astype(BF)
        zc = jnp.concatenate([zc0_ref[...], zc1_ref[...]], axis=1).astype(F32)
        za = jnp.concatenate([za0_ref[...], za1_ref[...]], axis=1).astype(F32)
        mg = (_sigmoid(zc) * yc + _sigmoid(za) * ya).astype(BF)
        mg_ref[...] = mg
        o = jnp.dot(mg, wout_ref[...], preferred_element_type=F32)
        o_ref[...] = o.astype(BF)
        x2_ref[...] = x_ref[...] + gt_ref[...] * o

    wspec = pl.BlockSpec((D, D), _const2)
    rowspec = pl.BlockSpec((tm, D), _row)
    return pl.pallas_call(
        body, name=name, grid=(T // tm,),
        in_specs=[_col(tm, O_BG), _col(tm, O_CG), _col(tm, O_U), _halo_prev(hb, O_CG), _halo_prev(hb, O_U),
                  _col(tm, O_ZC, 512), _col(tm, O_ZC + 512, 512), _col(tm, O_ZA, 512), _col(tm, O_ZA + 512, 512),
                  rowspec, wspec, wspec, wspec, pl.BlockSpec((8, D), _const2), rowspec, pl.BlockSpec((1, D), _const2)],
        out_specs=[rowspec] * 6,
        out_shape=[jax.ShapeDtypeStruct((T, D), F32)] + [jax.ShapeDtypeStruct((T, D), BF)] * 5,
        compiler_params=_cp(("parallel",)),
    )(proj, proj, proj, proj, proj, proj, proj, proj, proj, attn, wcp, wap, wout, convw, x, gt)


def _col(tm, c, w=D):
    assert c % w == 0
    return pl.BlockSpec((tm, w), lambda i: (i, c // w))


def _halo_prev(hb, c):
    return pl.BlockSpec((HALO, D), lambda i: (jnp.maximum(i * hb - 1, 0), c // D))


def _halo_next(hb, nblk, c=0):
    return pl.BlockSpec((HALO, D), lambda i: (jnp.minimum((i + 1) * hb, nblk - 1), c // D))


def _mixer_mid_bwd(dx2, gt, o, proj, yc, ya, wout, wcp, wap, *, tm, name):
    T = dx2.shape[0]
    tm = _tile(T, tm)
    zw = 512
    nz = 2 * D // zw

    def body(dx_ref, gt_ref, o_ref, zc0_ref, zc1_ref, za0_ref, za1_ref, yc_ref, ya_ref, wout_ref, wcp_ref, wap_ref,
             dout_ref, dyc_ref, dya_ref, dgc_ref, dat_ref, dz_ref, dgt_ref, dzs):
        i, j = pl.program_id(0), pl.program_id(1)

        @pl.when(jnp.logical_and(i == 0, j == 0))
        def _():
            dgt_ref[...] = jnp.zeros_like(dgt_ref)

        @pl.when(j == 0)
        def _():
            dxv = dx_ref[...]
            dgt_ref[...] += jnp.sum(dxv * o_ref[...].astype(F32), axis=0, keepdims=True)
            dout = (gt_ref[...] * dxv).astype(BF)
            dout_ref[...] = dout
            dmg = lax.dot_general(dout, wout_ref[...], NT, preferred_element_type=F32)
            sc = _sigmoid(jnp.concatenate([zc0_ref[...], zc1_ref[...]], axis=1).astype(F32))
            sa = _sigmoid(jnp.concatenate([za0_ref[...], za1_ref[...]], axis=1).astype(F32))
            dyc = (dmg * sc).astype(BF)
            dya = (dmg * sa).astype(BF)
            dyc_ref[...] = dyc
            dya_ref[...] = dya
            dzs[:, 0:D] = (dmg * yc_ref[...].astype(F32) * (sc * (1.0 - sc))).astype(BF)
            dzs[:, D:2 * D] = (dmg * ya_ref[...].astype(F32) * (sa * (1.0 - sa))).astype(BF)
            dgc_ref[...] = lax.dot_general(dyc, wcp_ref[...], NT, preferred_element_type=F32).astype(BF)
            dat_ref[...] = lax.dot_general(dya, wap_ref[...], NT, preferred_element_type=F32).astype(BF)

        for jj in range(nz):
            @pl.when(j == jj)
            def _(jj=jj):
                dz_ref[...] = dzs[:, jj * zw:(jj + 1) * zw]

    nt = T // tm

    def ahead(i, j):
        return jnp.minimum(i + jnp.minimum(j, 1), nt - 1)

    def zcol(c):
        return pl.BlockSpec((tm, zw), lambda i, j: (ahead(i, j), c // zw))

    wspec = pl.BlockSpec((D, D), _const2)
    rowin = pl.BlockSpec((tm, D), lambda i, j: (ahead(i, j), 0))
    rowspec = pl.BlockSpec((tm, D), lambda i, j: (i, 0))
    vec = pl.BlockSpec((1, D), _const2)
    return pl.pallas_call(
        body, name=name, grid=(nt, nz),
        in_specs=[rowin, vec, rowin, zcol(O_ZC), zcol(O_ZC + zw), zcol(O_ZA), zcol(O_ZA + zw),
                  rowin, rowin, wspec, wspec, wspec],
        out_specs=[rowspec] * 5 + [pl.BlockSpec((tm, zw), lambda i, j: (i, O_ZC // zw + j)), vec],
        out_shape=[jax.ShapeDtypeStruct((T, D), BF)] * 5 + [jax.ShapeDtypeStruct((T, NIN), BF),
                                                            jax.ShapeDtypeStruct((1, D), F32)],
        scratch_shapes=[pltpu.VMEM((tm, 2 * D), BF)],
        compiler_params=_cp(("arbitrary", "arbitrary")),
    )(dx2, gt, o, proj, proj, proj, proj, yc, ya, wout, wcp, wap)


def _conv_bwd(dgc, proj, convw, dproj, *, tm, name):
    T = dgc.shape[0]
    tm = _tile(T, tm)
    hb = tm // HALO
    nblk = T // HALO
    nt = T // tm

    def body(dgc_ref, ndgc_ref, bg_ref, nbg_ref, cg_ref, u_ref, hcg_ref, hu_ref, cw_ref, dproj_ref, dp_ref, dcw_ref):
        i = pl.program_id(0)

        @pl.when(i == 0)
        def _():
            dcw_ref[...] = jnp.zeros_like(dcw_ref)
        first = jnp.where(i == 0, 0.0, 1.0)
        last = jnp.where(i == nt - 1, 0.0, 1.0)
        cg = cg_ref[...].astype(F32)
        u = u_ref[...].astype(F32)
        bg = bg_ref[...].astype(F32)
        dg = dgc_ref[...].astype(F32)
        cu = cg * u
        hprev = first * (hcg_ref[...].astype(F32) * hu_ref[...].astype(F32))
        m1, m2 = _conv_shifts(cu, hprev, tm)
        w0, w1, w2 = cw_ref[0:1, :], cw_ref[1:2, :], cw_ref[2:3, :]
        cv = w0 * m2 + w1 * m1 + w2 * cu
        dcv = dg * bg
        nxt = last * (ndgc_ref[...].astype(F32) * nbg_ref[...].astype(F32))
        n0, n1 = nxt[0:1, :], nxt[1:2, :]
        row = lax.broadcasted_iota(jnp.int32, dcv.shape, 0)
        p1 = jnp.where(row == tm - 1, n0, pltpu.roll(dcv, tm - 1, 0))
        p2 = jnp.where(row == tm - 1, n1, jnp.where(row == tm - 2, n0, pltpu.roll(dcv, tm - 2, 0)))
        dcu = w2 * dcv + w1 * p1 + w0 * p2
        dp_ref[:, 0:D] = (dg * cv).astype(BF)
        dp_ref[:, D:2 * D] = (dcu * u).astype(BF)
        dp_ref[:, 2 * D:3 * D] = (dcu * cg).astype(BF)
        dcw_ref[0:1, :] += jnp.sum(dcv * m2, axis=0, keepdims=True)
        dcw_ref[1:2, :] += jnp.sum(dcv * m1, axis=0, keepdims=True)
        dcw_ref[2:3, :] += jnp.sum(dcv * cu, axis=0, keepdims=True)

    rowspec = pl.BlockSpec((tm, D), _row)
    cw = pl.BlockSpec((8, D), _const2)
    return pl.pallas_call(
        body, name=name, grid=(nt,),
        in_specs=[rowspec, _halo_next(hb, nblk), _col(tm, O_BG), _halo_next(hb, nblk, O_BG),
                  _col(tm, O_CG), _col(tm, O_U), _halo_prev(hb, O_CG), _halo_prev(hb, O_U), cw,
                  pl.BlockSpec(memory_space=pl.ANY)],
        out_specs=[pl.BlockSpec((tm, 3 * D), _row), cw],
        out_shape=[jax.ShapeDtypeStruct(dproj.shape, BF), jax.ShapeDtypeStruct((8, D), F32)],
        input_output_aliases={9: 0},
        compiler_params=_cp(("arbitrary",)),
    )(dgc, dgc, proj, proj, proj, proj, proj, proj, convw, dproj)


def _adam(w, g, m, v, *, tm, name):
    _, R, C = w.shape
    tm = _tile(R, tm)
    parts = g.ndim == 3
    c1 = 1.0 - ADAM_B1
    c2 = 1.0 - ADAM_B2
    bc1 = 1.0 - ADAM_B1 ** ADAM_STEP
    bc2 = 1.0 - ADAM_B2 ** ADAM_STEP

    def body(w_ref, g_ref, m_ref, v_ref, go_ref, d_ref, nm_ref, nv_ref):
        if parts:
            gv = g_ref[0].astype(F32)
            for s in range(1, N_DEV):
                gv = gv + g_ref[s].astype(F32)
        else:
            gv = g_ref[...]
        go_ref[0] = gv
        nm = ADAM_B1 * m_ref[0] + c1 * gv
        nv = ADAM_B2 * v_ref[0] + c2 * (gv * gv)
        nm_ref[0] = nm
        nv_ref[0] = nv
        d_ref[0] = -ADAM_LR * ((nm / bc1) / (jnp.sqrt(nv / bc2) + ADAM_EPS) + ADAM_WD * w_ref[0])

    spec = pl.BlockSpec((1, tm, C), lambda i: (0, i, 0))
    gspec = pl.BlockSpec((N_DEV, tm, C), lambda i: (0, i, 0)) if parts else pl.BlockSpec((tm, C), _row)
    return pl.pallas_call(
        body, name=name, grid=(R // tm,),
        in_specs=[spec, gspec, spec, spec], out_specs=[spec] * 4,
        out_shape=[jax.ShapeDtypeStruct((1, R, C), F32)] * 4,
        compiler_params=_cp(("parallel",)),
    )(w, g, m, v)


def _mods_part(c_all, w_ada, b_ada, *, name):
    C = w_ada.shape[1]

    def body(c_ref, w_ref, b_ref, o_ref):
        cv = c_ref[...]
        ca = cv * jax.nn.sigmoid(cv)
        o_ref[...] = jnp.dot(ca, w_ref[...], preferred_element_type=F32,
                             precision=lax.Precision.HIGHEST) + b_ref[...]

    return pl.pallas_call(
        body, name=name,
        out_shape=jax.ShapeDtypeStruct((N_DEV, C), F32),
        compiler_params=_cp(),
    )(c_all, w_ada, b_ada)


def _wada_grad(c_all_t, gm, *, name):
    C = gm.shape[1]

    def body(c_ref, g_ref, o_ref):
        cv = c_ref[...]
        ca = cv * jax.nn.sigmoid(cv)
        acc = ca[:, 0:1] * g_ref[0:1, :]
        for b in range(1, N_DEV):
            acc = acc + ca[:, b:b + 1] * g_ref[b:b + 1, :]
        o_ref[...] = acc

    return pl.pallas_call(
        body, name=name,
        out_shape=jax.ShapeDtypeStruct((D, C), F32),
        compiler_params=_cp(),
    )(c_all_t, gm)


def _peer(x, y, c, d):
    px = lax.rem(x + ((d >> 2) & 1), 2)
    py = lax.rem(y + ((d >> 1) & 1), 2)
    pc = lax.rem(c + (d & 1), 2)
    return (px, py, pc), 4 * px + 2 * py + pc


def _exchange(xs, *, scatter, name):
    n = len(xs)
    nsem = n * (N_DEV - 1)

    def body(*refs):
        ins, outs = refs[:n], refs[n:2 * n]
        token, send_sems, recv_sems, local_sems = refs[2 * n:]
        x, y, c = lax.axis_index("x"), lax.axis_index("y"), lax.axis_index("c")
        me = 4 * x + 2 * y + c
        token[...] = jnp.zeros_like(token)

        def src(t, idx):
            return ins[t].at[idx] if scatter else ins[t]

        local = [pltpu.make_async_copy(src(t, me), outs[t].at[me], local_sems.at[t]) for t in range(n)]
        for cp in local:
            cp.start()
        remote = []
        for t in range(n):
            for d in range(1, N_DEV):
                peer, pidx = _peer(x, y, c, d)
                k = t * (N_DEV - 1) + d - 1
                send = pltpu.make_async_remote_copy(src_ref=src(t, pidx), dst_ref=outs[t].at[me],
                                                    send_sem=send_sems.at[k], recv_sem=recv_sems.at[k],
                                                    device_id=peer, device_id_type=MESH)
                recv = pltpu.make_async_remote_copy(src_ref=src(t, pidx), dst_ref=outs[t].at[pidx],
                                                    send_sem=send_sems.at[k], recv_sem=recv_sems.at[k],
                                                    device_id=peer, device_id_type=MESH)
                send.start()
                remote.append((send, recv))
        for cp in local:
            cp.wait()
        for send, recv in remote:
            send.wait_send()
            recv.wait_recv()

    anyspec = pl.BlockSpec(memory_space=pl.ANY)
    out_shape = [jax.ShapeDtypeStruct(a.shape if scatter else (N_DEV,) + a.shape, a.dtype) for a in xs]
    out_shape.append(jax.ShapeDtypeStruct((8, 128), F32))
    return pl.pallas_call(
        body, name=name,
        in_specs=[anyspec] * n, out_specs=[anyspec] * n + [pl.BlockSpec(memory_space=pltpu.VMEM)],
        out_shape=out_shape,
        scratch_shapes=[pltpu.SemaphoreType.DMA((nsem,)), pltpu.SemaphoreType.DMA((nsem,)),
                        pltpu.SemaphoreType.DMA((n,))],
    )(*xs)


def _sum8(parts, *, name):
    _, R, C = parts.shape

    def body(p_ref, o_ref):
        acc = p_ref[0]
        for s in range(1, N_DEV):
            acc = acc + p_ref[s]
        o_ref[...] = acc

    return pl.pallas_call(body, name=name, out_shape=jax.ShapeDtypeStruct((R, C), F32),
                          compiler_params=_cp())(parts)


HBM_SPEC = pl.BlockSpec(memory_space=pltpu.HBM)
SEM_SPEC = pl.BlockSpec(memory_space=pltpu.SEMAPHORE)
N_PEER = N_DEV - 1


def _split_copies(src_refs, land_refs, send_sems, recv_sems, scatter):
    x, y, c = lax.axis_index("x"), lax.axis_index("y"), lax.axis_index("c")
    me = 4 * x + 2 * y + c
    pairs = []
    for j, (src, land) in enumerate(zip(src_refs, land_refs)):
        for d in range(1, N_DEV):
            peer, pidx = _peer(x, y, c, d)
            k = j * N_PEER + d - 1
            s = src.at[pidx] if scatter else src
            send = pltpu.make_async_remote_copy(src_ref=s, dst_ref=land.at[me], send_sem=send_sems.at[k],
                                                recv_sem=recv_sems.at[k], device_id=peer, device_id_type=MESH)
            recv = pltpu.make_async_remote_copy(src_ref=s, dst_ref=land.at[pidx], send_sem=send_sems.at[k],
                                                recv_sem=recv_sems.at[k], device_id=peer, device_id_type=MESH)
            pairs.append((send, recv))
    return pairs


def _own_slot(block, me):
    land = lax.empty((N_DEV,) + block.shape, block.dtype)
    return lax.dynamic_update_slice(land, block[None], (me, 0, 0))


def _split_start(srcs, lands, groups, *, scatter, name):
    n, ng = len(srcs), len(groups)

    def body(*refs):
        src_refs, land_refs = refs[:n], refs[n:2 * n]
        sems = refs[2 * n:2 * n + 2 * ng]
        token = refs[-1]
        for gi, g in enumerate(groups):
            pairs = _split_copies([src_refs[t] for t in g], [land_refs[t] for t in g], sems[2 * gi],
                                  sems[2 * gi + 1], scatter)
            for send, _ in pairs:
                send.start()
        token[...] = jnp.zeros_like(token)

    sem_shapes = []
    for g in groups:
        sem_shapes += [pltpu.SemaphoreType.DMA((len(g) * N_PEER,))] * 2
    thru = [pltpu.HBM(a.shape, a.dtype) for a in list(srcs) + list(lands)]
    outs = pl.pallas_call(
        body, name=name,
        out_shape=tuple(sem_shapes + thru + [jax.ShapeDtypeStruct((8, 128), F32)]),
        in_specs=[HBM_SPEC] * (2 * n),
        out_specs=tuple([SEM_SPEC] * (2 * ng) + [HBM_SPEC] * (2 * n) + [pl.BlockSpec(memory_space=pltpu.VMEM)]),
        input_output_aliases={i: 2 * ng + i for i in range(2 * n)},
        compiler_params=pltpu.CompilerParams(has_side_effects=pltpu.SideEffectType.DATAFLOW_SIDE_EFFECTING),
    )(*[pltpu.with_memory_space_constraint(a, pltpu.HBM) for a in list(srcs) + list(lands)])
    sems = [(outs[2 * gi], outs[2 * gi + 1]) for gi in range(ng)]
    return sems, outs[2 * ng:2 * ng + n], outs[2 * ng + n:2 * ng + 2 * n], outs[-1]


def _behind(v, token):
    if token is None:
        return v
    return v + token[0, 0].astype(v.dtype)


def _split_wait(srcs, lands, sems, after, *, scatter, name):
    m = len(srcs)

    def body(*refs):
        src_refs, land_refs = refs[:m], refs[m:2 * m]
        send_sems, recv_sems = refs[2 * m], refs[2 * m + 1]
        for send, recv in _split_copies(src_refs, land_refs, send_sems, recv_sems, scatter):
            send.wait_send()
            recv.wait_recv()

    outs = pl.pallas_call(
        body, name=name,
        out_shape=tuple(pltpu.HBM(a.shape, a.dtype) for a in list(srcs) + list(lands)),
        in_specs=[HBM_SPEC] * (2 * m) + [SEM_SPEC, SEM_SPEC, pl.BlockSpec(memory_space=pl.ANY)],
        out_specs=tuple([HBM_SPEC] * (2 * m)),
        input_output_aliases={i: i for i in range(2 * m)},
        compiler_params=pltpu.CompilerParams(has_side_effects=pltpu.SideEffectType.DATAFLOW_SIDE_EFFECTING),
    )(*srcs, *lands, sems[0], sems[1], after)
    return outs[m:]


TM_PROJ = 1024
TM_ROW = 512
TM_NN = 1024
TK_TN = 512
TN_FFN = F // 2
TN_IN = NIN // 4


def _tn(a, b, name, tn):
    if a.ndim == 2:
        a = a[None]
    return _tn_matmul(a, b, tn=tn, tk=TK_TN, name=name)


def _local_step(x, tgt, mods, g1, gm, g2, gf, convw8, sinks, w_get, g_put):
    T = x.shape[0]
    sh1, sc1, gt1, sh2, sc2, gt2, sh3, sc3, gt3 = [mods[i:i + 1] for i in range(N_MOD)]
    cos, sin = _rope_tables(T)
    behind = _behind

    w = dict(w_get("gu1", mods))
    h1, ab1 = _norm_proj(x, g1, sc1, sh1, w["gu1"], tm=TM_PROJ, tn=TN_FFN, name="ffn1_up")
    w.update(w_get("d1", ab1))
    x1, y1 = _ffn_down_fwd(ab1, w["d1"], x, gt1, tm=TM_ROW, name="ffn1_down")
    w.update(w_get("mix", x1))
    h2, proj = _norm_proj(x1, gm, sc2, sh2, w["win"], tm=TM_PROJ, tn=TN_IN, name="mix_in")
    qs, kr = _attn_prep(proj, cos, sin, name="attn_prep")
    bias = _attn_bias()
    attn, lse = _attn_fwd(qs, kr, proj, bias, sinks, name="attn_fwd")
    x2, gc, yc, ya, mg, o = _mixer_mid_fwd(proj, attn, w["cp"], w["ap"], w["out"], convw8, x1, gt2,
                                           tm=TM_ROW, name="mix_mid")
    w.update(w_get("ffn2", x2))
    h3, ab2 = _norm_proj(x2, g2, sc3, sh3, w["gu2"], tm=TM_PROJ, tn=TN_FFN, name="ffn2_up")
    x3, y2 = _ffn_down_fwd(ab2, w["d2"], x2, gt3, tm=TM_ROW, name="ffn2_down")
    dx3, lsum, dgf = _final_fwd_bwd(x3, tgt, gf, tm=TM_ROW, name="final")

    dy2, dab2, dgt3 = _ffn_down_bwd(dx3, y2, gt3, ab2, w["d2"], tm=TM_ROW, tn=MXU_N, name="ffn2_down_bwd")
    g_d2 = _tn_matmul_swiglu(ab2, dy2, None, tn=TN_FFN, tk=TK_TN, name="ffn2_down_dw")
    dx2, dsh3, dsc3, dg2 = _nn_bwd_norm(dab2, w["gu2"], x2, g2, sc3, dx3, tm=TM_NN, tk=TN_FFN, name="ffn2_up_bwd")
    g_gu2 = _tn(dab2, h3, "ffn2_up_dw", TN_FFN)
    tok = g_put(dict(gu2=g_gu2, d2=g_d2))

    dout, dyc, dya, dgc, dat, dproj, dgt2 = _mixer_mid_bwd(dx2, behind(gt2, tok), o, proj, yc, ya, w["out"], w["cp"],
                                                           w["ap"], tm=TM_ROW, name="mix_mid_bwd")
    g_out = _tn(mg, dout, "mix_out_dw", D)
    g_cp = _tn(gc, dyc, "mix_cp_dw", D)
    g_ap = _tn(attn, dya, "mix_ap_dw", D)
    dproj, dkc, dkp, dvc, dvp, dsink = _attn_bwd(qs, kr, proj, bias, sinks, lse, attn, dat, cos, sin, dproj,
                                                 name="attn_bwd")
    dproj = _dkv_combine(dkc, dkp, dvc, dvp, dproj, name="attn_dkv")
    dproj, dcw = _conv_bwd(dgc, proj, convw8, dproj, tm=TM_ROW, name="conv_bwd")
    g_in = _tn(dproj, h2, "mix_in_dw", TN_IN)
    tok = g_put(dict(win=g_in, cp=g_cp, ap=g_ap, out=g_out))
    dx1, dsh2, dsc2, dgm = _nn_bwd_norm(dproj[None], w["win"], x1, gm, behind(sc2, tok), dx2, tm=TM_NN, tk=TN_IN,
                                        name="mix_in_bwd")

    dy1, dab1, dgt1 = _ffn_down_bwd(dx1, y1, gt1, ab1, w["d1"], tm=TM_ROW, tn=MXU_N, name="ffn1_down_bwd")
    g_gu1 = _tn(dab1, h1, "ffn1_up_dw", TN_FFN)
    tok = g_put(dict(gu1=g_gu1))
    g_d1 = _tn_matmul_swiglu(ab1, dy1, tok, tn=TN_FFN, tk=TK_TN, name="ffn1_down_dw")
    tok = g_put(dict(d1=g_d1))
    dx0, dsh1, dsc1, dg1 = _nn_bwd_norm(dab1, w["gu1"], x, g1, behind(sc1, tok), dx1, tm=TM_NN, tk=TN_FFN,
                                        name="ffn1_up_bwd")

    small = dict(mods=jnp.concatenate([dsh1, dsc1, dgt1, dsh2, dsc2, dgt2, dsh3, dsc3, dgt3], axis=0),
                 g1=dg1, gm=dgm, g2=dg2, gf=dgf, convw=dcw[0:3], sinks=dsink[:, 0:N_HEADS])
    return lsum, dx0, small


BIG = ("gu1", "d1", "win", "cp", "ap", "out", "gu2", "d2")
TRANSPOSED = ("gu1", "win", "gu2")
SMALL_ROWS = 24
R_MODS, R_G1, R_GM, R_G2, R_GF, R_CONV, R_SINK = 0, 9, 10, 11, 12, 13, 16


def _pad_to(a, rows, cols):
    return jnp.pad(a, ((0, rows - a.shape[0]), (0, cols - a.shape[1])))


def _pack_small(b_ada, g1, gm, g2, gf, conv, sinks):
    rows = [b_ada.reshape(N_MOD, D), g1.reshape(1, D), gm.reshape(1, D), g2.reshape(1, D), gf.reshape(1, D),
            _pad_to(conv.reshape(3, -1), 3, D), _pad_to(sinks.reshape(1, N_HEADS), 1, D)]
    return _pad_to(jnp.concatenate(rows, axis=0), SMALL_ROWS, D)


def _unpack_small(p, conv_cols):
    return dict(b_ada=p[R_MODS:R_MODS + N_MOD].reshape(1, N_MOD * D), g_ffn1=p[R_G1:R_G1 + 1],
                g_mix=p[R_GM:R_GM + 1], g_ffn2=p[R_G2:R_G2 + 1], g_final=p[R_GF],
                conv_w=p[R_CONV:R_CONV + 3, 0:conv_cols][None], sinks=p[R_SINK:R_SINK + 1, 0:N_HEADS])


def kernel(x, c, w_ada, b_ada, g_ffn1, w1_gu, w1_down, g_mix, w_in, conv_w, w_conv_proj, w_attn_proj, sinks, w_out, g_ffn2, w2_gu, w2_down, g_final, loss_target, m_w_ada, m_b_ada, m_g_ffn1, m_w1_gu, m_w1_down, m_g_mix, m_w_in, m_conv_w, m_w_conv_proj, m_w_attn_proj, m_sinks, m_w_out, m_g_ffn2, m_w2_gu, m_w2_down, m_g_final, v_w_ada, v_b_ada, v_g_ffn1, v_w1_gu, v_w1_down, v_g_mix, v_w_in, v_conv_w, v_w_conv_proj, v_w_attn_proj, v_sinks, v_w_out, v_g_ffn2, v_w2_gu, v_w2_down, v_g_final):
    me = 4 * lax.axis_index("x") + 2 * lax.axis_index("y") + lax.axis_index("c")
    ada_cols = w_ada.shape[2]
    conv_cols = conv_w.shape[2]

    native = dict(gu1=w1_gu[0], d1=w1_down[0], win=w_in[0], cp=w_conv_proj[0], ap=w_attn_proj[0], out=w_out[0],
                  gu2=w2_gu[0], d2=w2_down[0])

    def shard(n, token):
        a = _behind(native[n], token)
        return (a.T if n in TRANSPOSED else a).astype(BF)

    c_all, conv_all, _ = _exchange([c, _pad_to(conv_w[0], 8, conv_cols)], scatter=False, name="gather_cond")
    c_all = c_all.reshape(N_DEV, D)
    conv_full = conv_all[:, 0:3, :].transpose(1, 0, 2).reshape(3, D)

    b_cols = lax.dynamic_slice(b_ada, (0, me * ada_cols), (1, ada_cols))
    mods_cols = _mods_part(c_all, w_ada[0], b_cols, name="ada_mods")
    mods_all, mods_token = _exchange([mods_cols], scatter=False, name="gather_mods")
    mods = lax.dynamic_index_in_dim(mods_all, me, axis=1, keepdims=False).reshape(N_MOD, D)

    groups = dict(gu1=("gu1",), d1=("d1",), mix=("win", "cp", "ap", "out"), ffn2=("gu2", "d2"))
    in_flight = {}
    first = [shard("gu1", mods_token)]
    sems, srcs, lands, token = _split_start(first, [_own_slot(s, me) for s in first], [[0]], scatter=False,
                                            name="gather_weights_start_gu1")
    in_flight["gu1"] = (sems[0], srcs, lands)
    rest = [n for n in BIG if n != "gu1"]
    shards = [shard(n, token) for n in rest]
    rest_groups = [[rest.index(n) for n in names] for g, names in groups.items() if g != "gu1"]
    sems, srcs, lands, rest_token = _split_start(shards, [_own_slot(s, me) for s in shards], rest_groups,
                                                 scatter=False, name="gather_weights_start_rest")
    for (g, names), gsems, idx in zip([kv for kv in groups.items() if kv[0] != "gu1"], sems, rest_groups):
        in_flight[g] = (gsems, [srcs[t] for t in idx], [lands[t] for t in idx])

    def w_get(group, after):
        if group == "gu1":
            after = rest_token
        gsems, gsrcs, glands = in_flight[group]
        landed = _split_wait(gsrcs, glands, gsems, after, scatter=False, name="gather_weights_wait_" + group)
        return {n: a.reshape(-1, D) for n, a in zip(groups[group], landed)}

    pending = []

    def g_put(gs):
        names = tuple(gs)
        srcs = [gs[n].reshape(N_DEV, -1, D) for n in names]
        lands = [_own_slot(lax.dynamic_index_in_dim(s, me, axis=0, keepdims=False), me) for s in srcs]
        sems, srcs, lands, tok = _split_start(srcs, lands, [list(range(len(names)))], scatter=True,
                                              name="scatter_grads_start_" + names[0])
        pending.append((names, sems[0], srcs, lands))
        return tok

    lsum, grad_x, small = _local_step(x[0], loss_target[0], mods, g_ffn1, g_mix, g_ffn2, g_final[None],
                                      _pad_to(conv_full, 8, D), sinks[0], w_get, g_put)
    loss = lax.psum((0.5 / D) * jnp.sum(lsum), ("x", "y", "c"))

    packed = _pack_small(small["mods"], small["g1"], small["gm"], small["g2"], small["gf"], small["convw"],
                         small["sinks"])
    packed_all, _ = _exchange([packed], scatter=False, name="gather_small")
    gsmall = _sum8(packed_all, name="sum_small")

    w_of = dict(ada=w_ada, gu1=w1_gu, d1=w1_down, win=w_in, cp=w_conv_proj, ap=w_attn_proj, out=w_out, gu2=w2_gu,
                d2=w2_down)
    m_of = dict(ada=m_w_ada, gu1=m_w1_gu, d1=m_w1_down, win=m_w_in, cp=m_w_conv_proj, ap=m_w_attn_proj, out=m_w_out,
                gu2=m_w2_gu, d2=m_w2_down)
    v_of = dict(ada=v_w_ada, gu1=v_w1_gu, d1=v_w1_down, win=v_w_in, cp=v_w_conv_proj, ap=v_w_attn_proj, out=v_w_out,
                gu2=v_w2_gu, d2=v_w2_down)
    upd = {}
    after = gsmall
    for names, sems, srcs, lands in pending:
        parts = _split_wait(srcs, lands, sems, after, scatter=True, name="scatter_grads_wait_" + names[0])
        for n, p in zip(names, parts):
            if n in TRANSPOSED:
                res = _adam(jnp.swapaxes(w_of[n], 1, 2), p, jnp.swapaxes(m_of[n], 1, 2), jnp.swapaxes(v_of[n], 1, 2),
                            tm=128, name="adam_" + n)
                upd[n] = [jnp.swapaxes(t, 1, 2) for t in res]
            else:
                upd[n] = _adam(w_of[n], p, m_of[n], v_of[n], tm=128, name="adam_" + n)
        after = upd[names[-1]][1]

    gm_cols = lax.dynamic_slice(packed_all[:, R_MODS:R_MODS + N_MOD, :].reshape(N_DEV, N_MOD * D),
                                (0, me * ada_cols), (N_DEV, ada_cols))
    upd["ada"] = _adam(w_ada, _wada_grad(c_all.T, gm_cols, name="ada_dw"), m_w_ada, v_w_ada, tm=128, name="adam_ada")
    conv_g = lax.dynamic_slice(gsmall[R_CONV:R_CONV + 3], (0, me * conv_cols), (3, conv_cols))
    gsmall_own = gsmall.at[R_CONV:R_CONV + 3].set(_pad_to(conv_g, 3, D))
    small_upd = _adam(_pack_small(b_ada, g_ffn1, g_mix, g_ffn2, g_final, conv_w, sinks)[None], gsmall_own,
                      _pack_small(m_b_ada, m_g_ffn1, m_g_mix, m_g_ffn2, m_g_final, m_conv_w, m_sinks)[None],
                      _pack_small(v_b_ada, v_g_ffn1, v_g_mix, v_g_ffn2, v_g_final, v_conv_w, v_sinks)[None],
                      tm=SMALL_ROWS, name="adam_small")
    small_out = [_unpack_small(p[0], conv_cols) for p in small_upd]

    big_name = dict(w_ada="ada", w1_gu="gu1", w1_down="d1", w_in="win", w_conv_proj="cp", w_attn_proj="ap",
                    w_out="out", w2_gu="gu2", w2_down="d2")
    order = ("w_ada", "b_ada", "g_ffn1", "w1_gu", "w1_down", "g_mix", "w_in", "conv_w", "w_conv_proj", "w_attn_proj",
             "sinks", "w_out", "g_ffn2", "w2_gu", "w2_down", "g_final")
    outs = [loss, grad_x[None]]
    for kind in range(4):
        for n in order:
            outs.append(upd[big_name[n]][kind] if n in big_name else small_out[kind][n])
    return tuple(outs)
```

```python
import functools

import jax
import jax.numpy as jnp
from jax import lax
from jax.experimental import pallas as pl
from jax.experimental.pallas import tpu as pltpu

D = 1024
F = 2816
NIN = 6656
N_HEADS = 16
N_KV = 4
HEAD_DIM = 64
BLK = 128
N_MOD = 9
N_DEV = 8
EPS = 1e-6
NEG_INF = -1e30
ROPE_THETA = 10000.0
O_BG, O_CG, O_U, O_Q, O_K, O_V, O_ZC, O_ZA = 0, 1024, 2048, 3072, 4096, 4352, 4608, 5632

ADAM_LR = 0.001
ADAM_B1 = 0.9
ADAM_B2 = 0.999
ADAM_EPS = 1e-08
ADAM_WD = 0.01
ADAM_STEP = 10

BF = jnp.bfloat16
F32 = jnp.float32
VMEM_LIMIT = 56 * 1024 * 1024
MXU_N = 256
MESH = pl.DeviceIdType.MESH

NT = (((1,), (1,)), ((), ()))
TN = (((0,), (0,)), ((), ()))


def _cp(sem=None):
    return pltpu.CompilerParams(dimension_semantics=sem, vmem_limit_bytes=VMEM_LIMIT)


def _tile(n, pref):
    if n <= pref:
        return n
    for t in range(pref - pref % 16, 15, -16):
        if n % t == 0:
            return t
    raise ValueError((n, pref))


def _sigmoid(v):
    return 0.5 * jnp.tanh(0.5 * v) + 0.5


def _row(i):
    return (i, 0)


def _const2(*_):
    return (0, 0)


def _norm_proj(x, g, sc, sh, wt, *, tm, tn, name):
    T, N = x.shape[0], wt.shape[0]
    tm, tn = _tile(T, tm), _tile(N, tn)

    def body(x_ref, g_ref, sc_ref, sh_ref, w_ref, h_ref, o_ref, hs):
        @pl.when(pl.program_id(1) == 0)
        def _():
            xv = x_ref[...]
            r = lax.rsqrt(jnp.mean(xv * xv, axis=-1, keepdims=True) + EPS)
            hb = ((xv * r) * g_ref[...] * (1.0 + sc_ref[...]) + sh_ref[...]).astype(BF)
            hs[...] = hb
            h_ref[...] = hb
        o_ref[...] = lax.dot_general(hs[...], w_ref[...], NT, preferred_element_type=F32).astype(BF)

    vec = pl.BlockSpec((1, D), _const2)
    return pl.pallas_call(
        body, name=name, grid=(T // tm, N // tn),
        in_specs=[pl.BlockSpec((tm, D), lambda i, j: (i, 0)), vec, vec, vec,
                  pl.BlockSpec((tn, D), lambda i, j: (j, 0))],
        out_specs=[pl.BlockSpec((tm, D), lambda i, j: (i, 0)), pl.BlockSpec((tm, tn), lambda i, j: (i, j))],
        out_shape=[jax.ShapeDtypeStruct((T, D), BF), jax.ShapeDtypeStruct((T, N), BF)],
        scratch_shapes=[pltpu.VMEM((tm, D), BF)],
        compiler_params=_cp(("parallel", "arbitrary")),
    )(x, g, sc, sh, wt)


def _ffn_down_fwd(ab, wd, x, gt, *, tm, name):
    T = x.shape[0]
    tm = _tile(T, tm)

    def body(a_ref, b_ref, wd_ref, x_ref, gt_ref, xo_ref, y_ref):
        y = None
        for c0 in range(0, F, MXU_N):
            cols = pl.ds(c0, MXU_N)
            a = a_ref[:, cols].astype(F32)
            act = (a * _sigmoid(a) * b_ref[:, cols].astype(F32)).astype(BF)
            part = jnp.dot(act, wd_ref[cols, :], preferred_element_type=F32)
            y = part if y is None else y + part
        y_ref[...] = y.astype(BF)
        xo_ref[...] = x_ref[...] + (0.5 * gt_ref[...]) * y

    return pl.pallas_call(
        body, name=name, grid=(T // tm,),
        in_specs=[pl.BlockSpec((tm, F), lambda i: (i, 0)), pl.BlockSpec((tm, F), lambda i: (i, 1)),
                  pl.BlockSpec((F, D), _const2), pl.BlockSpec((tm, D), _row), pl.BlockSpec((1, D), _const2)],
        out_specs=[pl.BlockSpec((tm, D), _row), pl.BlockSpec((tm, D), _row)],
        out_shape=[jax.ShapeDtypeStruct((T, D), F32), jax.ShapeDtypeStruct((T, D), BF)],
        compiler_params=_cp(("parallel",)),
    )(ab, ab, wd, x, gt)


def _final_fwd_bwd(x, tgt, g, *, tm, name):
    T = x.shape[0]
    tm = _tile(T, tm)

    def body(x_ref, t_ref, g_ref, dx_ref, ls_ref, dg_ref):
        @pl.when(pl.program_id(0) == 0)
        def _():
            ls_ref[...] = jnp.zeros_like(ls_ref)
            dg_ref[...] = jnp.zeros_like(dg_ref)
        xv = x_ref[...]
        gv = g_ref[...]
        r = lax.rsqrt(jnp.mean(xv * xv, axis=-1, keepdims=True) + EPS)
        xh = xv * r
        e = xh * gv - t_ref[...]
        ls_ref[...] += jnp.sum(e * e, axis=0, keepdims=True)
        dy = e * (1.0 / D)
        dg_ref[...] += jnp.sum(dy * xh, axis=0, keepdims=True)
        dxh = dy * gv
        dx_ref[...] = r * (dxh - xh * jnp.mean(dxh * xh, axis=-1, keepdims=True))

    vec = pl.BlockSpec((1, D), _const2)
    return pl.pallas_call(
        body, name=name, grid=(T // tm,),
        in_specs=[pl.BlockSpec((tm, D), _row), pl.BlockSpec((tm, D), _row), vec],
        out_specs=[pl.BlockSpec((tm, D), _row), vec, vec],
        out_shape=[jax.ShapeDtypeStruct((T, D), F32), jax.ShapeDtypeStruct((1, D), F32),
                   jax.ShapeDtypeStruct((1, D), F32)],
        compiler_params=_cp(("arbitrary",)),
    )(x, tgt, g)


def _ffn_down_bwd(dxo, y, gt, ab, wd, *, tm, tn, name):
    T = dxo.shape[0]
    tm = _tile(T, tm)

    def body(dxo_ref, y_ref, gt_ref, a_ref, b_ref, wd_ref, dy_ref, dab_ref, dgt_ref):
        @pl.when(pl.program_id(0) == 0)
        def _():
            dgt_ref[...] = jnp.zeros_like(dgt_ref)

        dxv = dxo_ref[...]
        dgt_ref[...] += 0.5 * jnp.sum(dxv * y_ref[...].astype(F32), axis=0, keepdims=True)
        dy = ((0.5 * gt_ref[...]) * dxv).astype(BF)
        dy_ref[...] = dy
        for c0 in range(0, F, tn):
            cols = pl.ds(c0, tn)
            dact = lax.dot_general(dy, wd_ref[cols, :], NT, preferred_element_type=F32)
            a = a_ref[:, cols].astype(F32)
            b = b_ref[:, cols].astype(F32)
            s = _sigmoid(a)
            dab_ref[0, :, cols] = (dact * b * (s * (1.0 + a * (1.0 - s)))).astype(BF)
            dab_ref[1, :, cols] = (dact * (a * s)).astype(BF)

    vec = pl.BlockSpec((1, D), _const2)
    rowspec = pl.BlockSpec((tm, D), _row)
    return pl.pallas_call(
        body, name=name, grid=(T // tm,),
        in_specs=[rowspec, rowspec, vec, pl.BlockSpec((tm, F), lambda i: (i, 0)),
                  pl.BlockSpec((tm, F), lambda i: (i, 1)), pl.BlockSpec((F, D), _const2)],
        out_specs=[rowspec, pl.BlockSpec((2, tm, F), lambda i: (0, i, 0)), vec],
        out_shape=[jax.ShapeDtypeStruct((T, D), BF), jax.ShapeDtypeStruct((2, T, F), BF),
                   jax.ShapeDtypeStruct((1, D), F32)],
        compiler_params=_cp(("arbitrary",)),
    )(dxo, y, gt, ab, ab, wd)


def _tn_matmul(a, b, *, tn, tk, name):
    S, T, Ns = a.shape
    tn, tk = _tile(Ns, tn), _tile(T, tk)
    nk, njs = T // tk, Ns // tn

    def body(a_ref, b_ref, o_ref, acc):
        k = pl.program_id(1)

        @pl.when(k == 0)
        def _():
            acc[...] = jnp.zeros_like(acc)
        acc[...] += lax.dot_general(a_ref[0], b_ref[...], TN, preferred_element_type=F32)

        @pl.when(k == nk - 1)
        def _():
            o_ref[...] = acc[...].astype(BF)

    return pl.pallas_call(
        body, name=name, grid=(S * njs, nk),
        in_specs=[pl.BlockSpec((1, tk, tn), lambda j, k: (j // njs, k, j % njs)),
                  pl.BlockSpec((tk, D), lambda j, k: (k, 0))],
        out_specs=pl.BlockSpec((tn, D), lambda j, k: (j, 0)),
        out_shape=jax.ShapeDtypeStruct((S * Ns, D), BF),
        scratch_shapes=[pltpu.VMEM((tn, D), F32)],
        compiler_params=_cp(("parallel", "arbitrary")),
    )(a, b)


def _tn_matmul_swiglu(ab, b, token, *, tn, tk, name):
    T = ab.shape[0]
    tn, tk = _tile(F, tn), _tile(T, tk)
    nk, nj = T // tk, F // tn
    deps = [] if token is None else [token]

    def body(a_ref, g_ref, b_ref, *rest):
        o_ref, acc = rest[len(deps):]
        k = pl.program_id(1)

        @pl.when(k == 0)
        def _():
            acc[...] = jnp.zeros_like(acc)
        bv = b_ref[...]
        for c0 in range(0, tn, MXU_N):
            cw = min(MXU_N, tn - c0)
            cols = pl.ds(c0, cw)
            a = a_ref[:, cols].astype(F32)
            act = (a * _sigmoid(a) * g_ref[:, cols].astype(F32)).astype(BF)
            acc[cols, :] += lax.dot_general(act, bv, TN, preferred_element_type=F32)

        @pl.when(k == nk - 1)
        def _():
            o_ref[...] = acc[...].astype(BF)

    return pl.pallas_call(
        body, name=name, grid=(nj, nk),
        in_specs=[pl.BlockSpec((tk, tn), lambda j, k: (k, j)), pl.BlockSpec((tk, tn), lambda j, k: (k, j + nj)),
                  pl.BlockSpec((tk, D), lambda j, k: (k, 0))] + [pl.BlockSpec(memory_space=pl.ANY)] * len(deps),
        out_specs=pl.BlockSpec((tn, D), lambda j, k: (j, 0)),
        out_shape=jax.ShapeDtypeStruct((F, D), BF),
        scratch_shapes=[pltpu.VMEM((tn, D), F32)],
        compiler_params=_cp(("parallel", "arbitrary")),
    )(ab, ab, b, *deps)


def _nn_bwd_norm(da, w, x, g, sc, dxo, *, tm, tk, name):
    S, T, Ks = da.shape
    tm, tk = _tile(T, tm), _tile(Ks, tk)
    nks = Ks // tk
    nk = S * nks
    rc = _tile(tm, 256)

    def body(da_ref, w_ref, x_ref, g_ref, sc_ref, dxo_ref, dx_ref, dsh_ref, dsc_ref, dg_ref, acc):
        i, k = pl.program_id(0), pl.program_id(1)

        @pl.when(jnp.logical_and(i == 0, k == 0))
        def _():
            dsh_ref[...] = jnp.zeros_like(dsh_ref)
            dsc_ref[...] = jnp.zeros_like(dsc_ref)
            dg_ref[...] = jnp.zeros_like(dg_ref)

        d = jnp.dot(da_ref[0], w_ref[...], preferred_element_type=F32)

        @pl.when(k == 0)
        def _():
            acc[...] = d

        @pl.when(k > 0)
        def _():
            acc[...] += d

        @pl.when(k == nk - 1)
        def _():
            gv = g_ref[...]
            sc1 = 1.0 + sc_ref[...]
            dsh = jnp.zeros((1, D), F32)
            dsc = jnp.zeros((1, D), F32)
            dg = jnp.zeros((1, D), F32)
            for r0 in range(0, tm, rc):
                rows = pl.ds(r0, rc)
                u = acc[rows, :]
                xv = x_ref[rows, :]
                r = lax.rsqrt(jnp.mean(xv * xv, axis=-1, keepdims=True) + EPS)
                xh = xv * r
                dsh = dsh + jnp.sum(u, axis=0, keepdims=True)
                dsc = dsc + jnp.sum(u * (xh * gv), axis=0, keepdims=True)
                us = u * sc1
                dg = dg + jnp.sum(us * xh, axis=0, keepdims=True)
                dxh = us * gv
                dx_ref[rows, :] = dxo_ref[rows, :] + r * (dxh - xh * jnp.mean(dxh * xh, axis=-1, keepdims=True))
            dsh_ref[...] += dsh
            dsc_ref[...] += dsc
            dg_ref[...] += dg

    vec = pl.BlockSpec((1, D), _const2)
    return pl.pallas_call(
        body, name=name, grid=(T // tm, nk),
        in_specs=[pl.BlockSpec((1, tm, tk), lambda i, k: (k // nks, i, k % nks)),
                  pl.BlockSpec((tk, D), lambda i, k: (k, 0)),
                  pl.BlockSpec((tm, D), lambda i, k: (i, 0)), vec, vec,
                  pl.BlockSpec((tm, D), lambda i, k: (i, 0))],
        out_specs=[pl.BlockSpec((tm, D), lambda i, k: (i, 0)), vec, vec, vec],
        out_shape=[jax.ShapeDtypeStruct((T, D), F32)] + [jax.ShapeDtypeStruct((1, D), F32)] * 3,
        scratch_shapes=[pltpu.VMEM((tm, D), F32)],
        compiler_params=_cp(("arbitrary", "arbitrary")),
    )(da, w, x, g, sc, dxo)


def _rope(t, cos, sin_signed, lt32, inverse=False):
    sel = jnp.where(lt32, pltpu.roll(t, 96, 1), pltpu.roll(t, 32, 1))
    return t * cos - sel * sin_signed if inverse else t * cos + sel * sin_signed


def _rope_tables(T):
    inv = 1.0 / (ROPE_THETA ** (jnp.arange(0, HEAD_DIM, 2, dtype=F32) / HEAD_DIM))
    ang = jnp.arange(T, dtype=F32)[:, None] * inv[None, :]
    cos, sin = jnp.cos(ang), jnp.sin(ang)
    cos128 = jnp.tile(cos, (1, 4))
    sin128 = jnp.tile(jnp.concatenate([-sin, sin], axis=1), (1, 2))
    return cos128, sin128


QSCALE = HEAD_DIM ** -0.5


def _lane_masks(rows):
    lane = lax.broadcasted_iota(jnp.int32, (rows, 128), 1)
    return (lane % HEAD_DIM) < (HEAD_DIM // 2), [lane < HEAD_DIM, lane >= HEAD_DIM]


def _attn_bias():
    qi = lax.broadcasted_iota(jnp.int32, (4 * BLK, 2 * BLK), 0) % BLK
    kj = lax.broadcasted_iota(jnp.int32, (4 * BLK, 2 * BLK), 1)
    band = (kj > qi) & (kj <= qi + BLK)
    return jnp.stack([jnp.where(band & (kj >= BLK), 0.0, NEG_INF), jnp.where(band, 0.0, NEG_INF)]).astype(F32)


def _attn_prep(proj, cos, sin, *, name):
    T = proj.shape[0]
    tm = _tile(T, 4 * BLK)

    def body(q_ref, k_ref, c_ref, s_ref, qs_ref, kr_ref):
        lt32, halves = _lane_masks(BLK)
        for b in range(tm // BLK):
            rows = pl.ds(b * BLK, BLK)
            cc, sc = c_ref[rows, :], s_ref[rows, :]
            qr = [_rope(q_ref[rows, p * 128:(p + 1) * 128].astype(F32), cc, sc, lt32) * QSCALE for p in range(8)]
            for g in range(N_KV):
                qs_ref[g, pl.ds(4 * b * BLK, 4 * BLK), :] = _stack_heads(qr, g, halves).astype(BF)
            kr_ref[rows, :] = jnp.concatenate([_rope(k_ref[rows, r * 128:(r + 1) * 128].astype(F32), cc, sc, lt32)
                                               for r in range(2)], axis=1).astype(BF)

    tab = pl.BlockSpec((tm, 128), _row)
    return pl.pallas_call(
        body, name=name, grid=(T // tm,),
        in_specs=[pl.BlockSpec((tm, D), lambda n: (n, O_Q // D)), pl.BlockSpec((tm, 256), lambda n: (n, O_K // 256)),
                  tab, tab],
        out_specs=[pl.BlockSpec((N_KV, 4 * tm, 128), lambda n: (0, n, 0)), pl.BlockSpec((tm, 256), _row)],
        out_shape=[jax.ShapeDtypeStruct((N_KV, 4 * T, 128), BF), jax.ShapeDtypeStruct((T, 256), BF)],
        compiler_params=_cp(("parallel",)),
    )(proj, proj, cos, sin)


def _attn_specs():
    prev = lambda n: jnp.maximum(n - 1, 0)
    return [pl.BlockSpec((N_KV, 4 * BLK, 128), lambda n: (0, n, 0)),
            pl.BlockSpec((BLK, 256), _row), pl.BlockSpec((BLK, 256), lambda n: (prev(n), 0)),
            pl.BlockSpec((BLK, 256), lambda n: (n, O_V // 256)),
            pl.BlockSpec((BLK, 256), lambda n: (prev(n), O_V // 256)),
            pl.BlockSpec((1, 4 * BLK, 2 * BLK), lambda n: (jnp.minimum(n, 1), 0, 0)),
            pl.BlockSpec(memory_space=pltpu.SMEM)]


def _bands(kc_ref, kp_ref, vc_ref, vp_ref):
    kb, vb = [], []
    for r in range(2):
        cols = slice(r * 128, (r + 1) * 128)
        kb.append(jnp.concatenate([kp_ref[:, cols], kc_ref[:, cols]], axis=0))
        vb.append(jnp.concatenate([vp_ref[:, cols], vc_ref[:, cols]], axis=0))
    return kb, vb


def _sink_col(sink_ref, g):
    return jnp.concatenate([jnp.full((BLK, 1), sink_ref[4 * g + hh], F32) for hh in range(4)], axis=0)


def _unstack_heads(t, g, halves, acc):
    half = g % 2
    for hh in range(4):
        h = 4 * g + hh
        th = jnp.where(halves[half], t[hh * BLK:(hh + 1) * BLK], 0.0)
        if h % 2 != half:
            th = pltpu.roll(th, HEAD_DIM, 1)
        acc[h // 2] = acc[h // 2] + th


def _stack_heads(chunks, g, halves):
    half = g % 2
    parts = []
    for hh in range(4):
        h = 4 * g + hh
        t = chunks[h // 2]
        if h % 2 != half:
            t = pltpu.roll(t, HEAD_DIM, 1)
        parts.append(jnp.where(halves[half], t, 0.0))
    return jnp.concatenate(parts, axis=0)


def _attn_fwd(qs, kr, proj, bias, sinks, *, name):
    T = proj.shape[0]
    nb = T // BLK

    def body(qs_ref, kc_ref, kp_ref, vc_ref, vp_ref, bias_ref, sink_ref, o_ref, lse_ref):
        _, h128 = _lane_masks(BLK)
        _, h256 = _lane_masks(2 * BLK)
        _, h512 = _lane_masks(4 * BLK)
        kb, vb = _bands(kc_ref, kp_ref, vc_ref, vp_ref)
        outs = [jnp.zeros((BLK, 128), F32) for _ in range(8)]
        groups = range(N_KV)
        bias = bias_ref[0]
        sink = [_sink_col(sink_ref, g) for g in groups]
        s = [lax.dot_general(qs_ref[g], kb[g // 2], NT, preferred_element_type=F32) + bias for g in groups]
        m = [jnp.maximum(jnp.max(s[g], axis=-1, keepdims=True), sink[g]) for g in groups]
        p = [jnp.exp(s[g] - m[g]).astype(BF) for g in groups]
        vg = [jnp.where(h256[g % 2], vb[g // 2].astype(F32), 1.0).astype(BF) for g in groups]
        o = [jnp.dot(p[g], vg[g], preferred_element_type=F32) for g in groups]
        denom = [jnp.where(h512[g % 2], pltpu.roll(o[g], HEAD_DIM, 1), o[g]) + jnp.exp(sink[g] - m[g]) for g in groups]
        for g in groups:
            lse_ref[g] = m[g] + jnp.log(denom[g])
            _unstack_heads(o[g] * (1.0 / denom[g]), g, h128, outs)
        o_ref[...] = jnp.concatenate(outs, axis=1).astype(BF)

    return pl.pallas_call(
        body, name=name, grid=(nb,),
        in_specs=_attn_specs(),
        out_specs=[pl.BlockSpec((BLK, D), _row), pl.BlockSpec((N_KV, 4 * BLK, 128), lambda n: (0, n, 0))],
        out_shape=[jax.ShapeDtypeStruct((T, D), BF), jax.ShapeDtypeStruct((N_KV, 4 * T, 128), F32)],
        compiler_params=_cp(("parallel",)),
    )(qs, kr, kr, proj, proj, bias, sinks)


def _attn_bwd(qs, kr, proj, bias, sinks, lse, o, do, cos, sin, dproj, *, name):
    T = proj.shape[0]
    nb = T // BLK

    def body(qs_ref, kc_ref, kp_ref, vc_ref, vp_ref, bias_ref, sink_ref, lse_ref, o_ref, do_ref,
             cc_ref, sc_ref, cp_ref, sp_ref, dproj_ref, dq_ref, dkc_ref, dkp_ref, dvc_ref, dvp_ref, dsink_ref):
        @pl.when(pl.program_id(0) == 0)
        def _():
            dsink_ref[...] = jnp.zeros_like(dsink_ref)
        lt32, h128 = _lane_masks(BLK)
        kb, vb = _bands(kc_ref, kp_ref, vc_ref, vp_ref)
        oc = [o_ref[:, p * 128:(p + 1) * 128].astype(F32) for p in range(8)]
        doc = [do_ref[:, p * 128:(p + 1) * 128].astype(F32) for p in range(8)]
        dqs = [jnp.zeros((BLK, 128), F32) for _ in range(8)]
        lane1 = lax.broadcasted_iota(jnp.int32, (1, 128), 1)
        dsink = jnp.zeros((1, 128), F32)
        groups = range(N_KV)
        bias = bias_ref[0]
        q = [qs_ref[g] for g in groups]
        lse_g = [lse_ref[g] for g in groups]
        s = [lax.dot_general(q[g], kb[g // 2], NT, preferred_element_type=F32) + bias for g in groups]
        dos = [_stack_heads(doc, g, h128) for g in groups]
        dosb = [t.astype(BF) for t in dos]
        dp = [lax.dot_general(dosb[g], vb[g // 2], NT, preferred_element_type=F32) for g in groups]
        delta = [jnp.sum(dos[g] * _stack_heads(oc, g, h128), axis=-1, keepdims=True) for g in groups]
        p = [jnp.exp(s[g] - jnp.concatenate([lse_g[g], lse_g[g]], axis=1)) for g in groups]
        ds = [(p[g] * (dp[g] - delta[g])).astype(BF) for g in groups]
        pb = [t.astype(BF) for t in p]
        dvg = [lax.dot_general(pb[g], dosb[g], TN, preferred_element_type=F32) for g in groups]
        dkg = [lax.dot_general(ds[g], q[g], TN, preferred_element_type=F32) for g in groups]
        dqg = [jnp.dot(ds[g], kb[g // 2], preferred_element_type=F32) * QSCALE for g in groups]
        dvr = [dvg[0] + dvg[1], dvg[2] + dvg[3]]
        dkr = [dkg[0] + dkg[1], dkg[2] + dkg[3]]
        for g in groups:
            _unstack_heads(dqg[g], g, h128, dqs)
            dsk = -jnp.exp(_sink_col(sink_ref, g) - lse_g[g][:, 0:1]) * delta[g]
            for hh in range(4):
                val = jnp.sum(dsk[hh * BLK:(hh + 1) * BLK], axis=0, keepdims=True)
                dsink = dsink + jnp.where(lane1 == 4 * g + hh, val, 0.0)
        cc, sc, cp, sp = cc_ref[...], sc_ref[...], cp_ref[...], sp_ref[...]
        dsink_ref[...] += dsink
        dq_ref[...] = jnp.concatenate([_rope(t, cc, sc, lt32, inverse=True) for t in dqs], axis=1).astype(BF)
        dkp_ref[...] = jnp.concatenate([_rope(t[:BLK], cp, sp, lt32, inverse=True) for t in dkr], axis=1)
        dkc_ref[...] = jnp.concatenate([_rope(t[BLK:], cc, sc, lt32, inverse=True) for t in dkr], axis=1)
        dvp_ref[...] = jnp.concatenate([t[:BLK] for t in dvr], axis=1)
        dvc_ref[...] = jnp.concatenate([t[BLK:] for t in dvr], axis=1)

    kv = pl.BlockSpec((BLK, 256), _row)
    tc = pl.BlockSpec((BLK, 128), _row)
    tp = pl.BlockSpec((BLK, 128), lambda n: (jnp.maximum(n - 1, 0), 0))
    return pl.pallas_call(
        body, name=name, grid=(nb,),
        in_specs=_attn_specs() + [pl.BlockSpec((N_KV, 4 * BLK, 128), lambda n: (0, n, 0)),
                                  pl.BlockSpec((BLK, D), _row), pl.BlockSpec((BLK, D), _row), tc, tc, tp, tp,
                                  pl.BlockSpec(memory_space=pl.ANY)],
        out_specs=[pl.BlockSpec((BLK, D), lambda n: (n, O_Q // D)), kv, kv, kv, kv, pl.BlockSpec((1, 128), _const2)],
        out_shape=[jax.ShapeDtypeStruct(dproj.shape, BF)] + [jax.ShapeDtypeStruct((T, 256), F32)] * 4
        + [jax.ShapeDtypeStruct((1, 128), F32)],
        input_output_aliases={14: 0},
        compiler_params=_cp(("arbitrary",)),
    )(qs, kr, kr, proj, proj, bias, sinks, lse, o, do, cos, sin, cos, sin, dproj)


def _dkv_combine(dkc, dkp, dvc, dvp, dproj, *, name):
    T = dkc.shape[0]
    nb = T // BLK
    tm = _tile(T, 4 * BLK)
    bpt = tm // BLK
    nt = T // tm

    def body(dkc_ref, dkp_ref, dkn_ref, dvc_ref, dvp_ref, dvn_ref, dproj_ref, o_ref):
        keep = jnp.where(pl.program_id(0) == nt - 1, 0.0, 1.0)

        def shifted(prev_ref, next_ref):
            nxt = keep * next_ref[...]
            return nxt if bpt == 1 else jnp.concatenate([prev_ref[BLK:, :], nxt], axis=0)

        o_ref[:, 0:256] = (dkc_ref[...] + shifted(dkp_ref, dkn_ref)).astype(BF)
        o_ref[:, 256:512] = (dvc_ref[...] + shifted(dvp_ref, dvn_ref)).astype(BF)

    cur = pl.BlockSpec((tm, 256), _row)
    nxt = pl.BlockSpec((BLK, 256), lambda i: (jnp.minimum((i + 1) * bpt, nb - 1), 0))
    return pl.pallas_call(
        body, name=name, grid=(nt,),
        in_specs=[cur, cur, nxt, cur, cur, nxt, pl.BlockSpec(memory_space=pl.ANY)],
        out_specs=pl.BlockSpec((tm, 512), lambda i: (i, O_K // 512)),
        out_shape=jax.ShapeDtypeStruct(dproj.shape, BF),
        input_output_aliases={6: 0},
        compiler_params=_cp(("parallel",)),
    )(dkc, dkp, dkp, dvc, dvp, dvp, dproj)


HALO = 16


def _conv_shifts(cu, hprev, tm):
    row = lax.broadcasted_iota(jnp.int32, cu.shape, 0)
    h1 = hprev[HALO - 1:HALO, :]
    h2 = hprev[HALO - 2:HALO - 1, :]
    m1 = jnp.where(row == 0, h1, pltpu.roll(cu, 1, 0))
    m2 = jnp.where(row == 0, h2, jnp.where(row == 1, h1, pltpu.roll(cu, 2, 0)))
    return m1, m2


def _mixer_mid_fwd(proj, attn, wcp, wap, wout, convw, x, gt, *, tm, name):
    T = x.shape[0]
    tm = _tile(T, tm)
    hb = tm // HALO

    def body(bg_ref, cg_ref, u_ref, hcg_ref, hu_ref, zc0_ref, zc1_ref, za0_ref, za1_ref, at_ref,
             wcp_ref, wap_ref, wout_ref, cw_ref, x_ref, gt_ref,
             x2_ref, gc_ref, yc_ref, ya_ref, mg_ref, o_ref):
        first = jnp.where(pl.program_id(0) == 0, 0.0, 1.0)
        cu = cg_ref[...].astype(F32) * u_ref[...].astype(F32)
        hprev = first * (hcg_ref[...].astype(F32) * hu_ref[...].astype(F32))
        m1, m2 = _conv_shifts(cu, hprev, tm)
        cv = cw_ref[0:1, :] * m2 + cw_ref[1:2, :] * m1 + cw_ref[2:3, :] * cu
        gc = (bg_ref[...].astype(F32) * cv).astype(BF)
        gc_ref[...] = gc
        yc = jnp.dot(gc, wcp_ref[...], preferred_element_type=F32)
        ya = jnp.dot(at_ref[...], wap_ref[...], preferred_element_type=F32)
        yc_ref[...] = yc.astype(BF)
        ya_ref[...] = ya.astype(BF)
        zc = jnp.concatenate([zc0_ref[...], zc1_ref[...]], axis=1).astype(F32)
        za = jnp.concatenate([za0_ref[...], za1_ref[...]], axis=1).astype(F32)
        mg = (_sigmoid(zc) * yc + _sigmoid(za) * ya).astype(BF)
        mg_ref[...] = mg
        o = jnp.dot(mg, wout_ref[...], preferred_element_type=F32)
        o_ref[...] = o.astype(BF)
        x2_ref[...] = x_ref[...] + gt_ref[...] * o

    wspec = pl.BlockSpec((D, D), _const2)
    rowspec = pl.BlockSpec((tm, D), _row)
    return pl.pallas_call(
        body, name=name, grid=(T // tm,),
        in_specs=[_col(tm, O_BG), _col(tm, O_CG), _col(tm, O_U), _halo_prev(hb, O_CG), _halo_prev(hb, O_U),
                  _col(tm, O_ZC, 512), _col(tm, O_ZC + 512, 512), _col(tm, O_ZA, 512), _col(tm, O_ZA + 512, 512),
                  rowspec, wspec, wspec, wspec, pl.BlockSpec((8, D), _const2), rowspec, pl.BlockSpec((1, D), _const2)],
        out_specs=[rowspec] * 6,
        out_shape=[jax.ShapeDtypeStruct((T, D), F32)] + [jax.ShapeDtypeStruct((T, D), BF)] * 5,
        compiler_params=_cp(("parallel",)),
    )(proj, proj, proj, proj, proj, proj, proj, proj, proj, attn, wcp, wap, wout, convw, x, gt)


def _col(tm, c, w=D):
    assert c % w == 0
    return pl.BlockSpec((tm, w), lambda i: (i, c // w))


def _halo_prev(hb, c):
    return pl.BlockSpec((HALO, D), lambda i: (jnp.maximum(i * hb - 1, 0), c // D))


def _halo_next(hb, nblk, c=0):
    return pl.BlockSpec((HALO, D), lambda i: (jnp.minimum((i + 1) * hb, nblk - 1), c // D))


def _mixer_mid_bwd(dx2, gt, o, proj, yc, ya, wout, wcp, wap, *, tm, name):
    T = dx2.shape[0]
    tm = _tile(T, tm)
    zw = 512
    nz = 2 * D // zw

    def body(dx_ref, gt_ref, o_ref, zc0_ref, zc1_ref, za0_ref, za1_ref, yc_ref, ya_ref, wout_ref, wcp_ref, wap_ref,
             dout_ref, dyc_ref, dya_ref, dgc_ref, dat_ref, dz_ref, dgt_ref, dzs):
        i, j = pl.program_id(0), pl.program_id(1)

        @pl.when(jnp.logical_and(i == 0, j == 0))
        def _():
            dgt_ref[...] = jnp.zeros_like(dgt_ref)

        @pl.when(j == 0)
        def _():
            dxv = dx_ref[...]
            dgt_ref[...] += jnp.sum(dxv * o_ref[...].astype(F32), axis=0, keepdims=True)
            dout = (gt_ref[...] * dxv).astype(BF)
            dout_ref[...] = dout
            dmg = lax.dot_general(dout, wout_ref[...], NT, preferred_element_type=F32)
            sc = _sigmoid(jnp.concatenate([zc0_ref[...], zc1_ref[...]], axis=1).astype(F32))
            sa = _sigmoid(jnp.concatenate([za0_ref[...], za1_ref[...]], axis=1).astype(F32))
            dyc = (dmg * sc).astype(BF)
            dya = (dmg * sa).astype(BF)
            dyc_ref[...] = dyc
            dya_ref[...] = dya
            dzs[:, 0:D] = (dmg * yc_ref[...].astype(F32) * (sc * (1.0 - sc))).astype(BF)
            dzs[:, D:2 * D] = (dmg * ya_ref[...].astype(F32) * (sa * (1.0 - sa))).astype(BF)
            dgc_ref[...] = lax.dot_general(dyc, wcp_ref[...], NT, preferred_element_type=F32).astype(BF)
            dat_ref[...] = lax.dot_general(dya, wap_ref[...], NT, preferred_element_type=F32).astype(BF)

        for jj in range(nz):
            @pl.when(j == jj)
            def _(jj=jj):
                dz_ref[...] = dzs[:, jj * zw:(jj + 1) * zw]

    nt = T // tm

    def ahead(i, j):
        return jnp.minimum(i + jnp.minimum(j, 1), nt - 1)

    def zcol(c):
        return pl.BlockSpec((tm, zw), lambda i, j: (ahead(i, j), c // zw))

    wspec = pl.BlockSpec((D, D), _const2)
    rowin = pl.BlockSpec((tm, D), lambda i, j: (ahead(i, j), 0))
    rowspec = pl.BlockSpec((tm, D), lambda i, j: (i, 0))
    vec = pl.BlockSpec((1, D), _const2)
    return pl.pallas_call(
        body, name=name, grid=(nt, nz),
        in_specs=[rowin, vec, rowin, zcol(O_ZC), zcol(O_ZC + zw), zcol(O_ZA), zcol(O_ZA + zw),
                  rowin, rowin, wspec, wspec, wspec],
        out_specs=[rowspec] * 5 + [pl.BlockSpec((tm, zw), lambda i, j: (i, O_ZC // zw + j)), vec],
        out_shape=[jax.ShapeDtypeStruct((T, D), BF)] * 5 + [jax.ShapeDtypeStruct((T, NIN), BF),
                                                            jax.ShapeDtypeStruct((1, D), F32)],
        scratch_shapes=[pltpu.VMEM((tm, 2 * D), BF)],
        compiler_params=_cp(("arbitrary", "arbitrary")),
    )(dx2, gt, o, proj, proj, proj, proj, yc, ya, wout, wcp, wap)


def _conv_bwd(dgc, proj, convw, dproj, *, tm, name):
    T = dgc.shape[0]
    tm = _tile(T, tm)
    hb = tm // HALO
    nblk = T // HALO
    nt = T // tm

    def body(dgc_ref, ndgc_ref, bg_ref, nbg_ref, cg_ref, u_ref, hcg_ref, hu_ref, cw_ref, dproj_ref, dp_ref, dcw_ref):
        i = pl.program_id(0)

        @pl.when(i == 0)
        def _():
            dcw_ref[...] = jnp.zeros_like(dcw_ref)
        first = jnp.where(i == 0, 0.0, 1.0)
        last = jnp.where(i == nt - 1, 0.0, 1.0)
        cg = cg_ref[...].astype(F32)
        u = u_ref[...].astype(F32)
        bg = bg_ref[...].astype(F32)
        dg = dgc_ref[...].astype(F32)
        cu = cg * u
        hprev = first * (hcg_ref[...].astype(F32) * hu_ref[...].astype(F32))
        m1, m2 = _conv_shifts(cu, hprev, tm)
        w0, w1, w2 = cw_ref[0:1, :], cw_ref[1:2, :], cw_ref[2:3, :]
        cv = w0 * m2 + w1 * m1 + w2 * cu
        dcv = dg * bg
        nxt = last * (ndgc_ref[...].astype(F32) * nbg_ref[...].astype(F32))
        n0, n1 = nxt[0:1, :], nxt[1:2, :]
        row = lax.broadcasted_iota(jnp.int32, dcv.shape, 0)
        p1 = jnp.where(row == tm - 1, n0, pltpu.roll(dcv, tm - 1, 0))
        p2 = jnp.where(row == tm - 1, n1, jnp.where(row == tm - 2, n0, pltpu.roll(dcv, tm - 2, 0)))
        dcu = w2 * dcv + w1 * p1 + w0 * p2
        dp_ref[:, 0:D] = (dg * cv).astype(BF)
        dp_ref[:, D:2 * D] = (dcu * u).astype(BF)
        dp_ref[:, 2 * D:3 * D] = (dcu * cg).astype(BF)
        dcw_ref[0:1, :] += jnp.sum(dcv * m2, axis=0, keepdims=True)
        dcw_ref[1:2, :] += jnp.sum(dcv * m1, axis=0, keepdims=True)
        dcw_ref[2:3, :] += jnp.sum(dcv * cu, axis=0, keepdims=True)

    rowspec = pl.BlockSpec((tm, D), _row)
    cw = pl.BlockSpec((8, D), _const2)
    return pl.pallas_call(
        body, name=name, grid=(nt,),
        in_specs=[rowspec, _halo_next(hb, nblk), _col(tm, O_BG), _halo_next(hb, nblk, O_BG),
                  _col(tm, O_CG), _col(tm, O_U), _halo_prev(hb, O_CG), _halo_prev(hb, O_U), cw,
                  pl.BlockSpec(memory_space=pl.ANY)],
        out_specs=[pl.BlockSpec((tm, 3 * D), _row), cw],
        out_shape=[jax.ShapeDtypeStruct(dproj.shape, BF), jax.ShapeDtypeStruct((8, D), F32)],
        input_output_aliases={9: 0},
        compiler_params=_cp(("arbitrary",)),
    )(dgc, dgc, proj, proj, proj, proj, proj, proj, convw, dproj)


def _adam(w, g, m, v, *, tm, name):
    _, R, C = w.shape
    tm = _tile(R, tm)
    parts = g.ndim == 3
    c1 = 1.0 - ADAM_B1
    c2 = 1.0 - ADAM_B2
    bc1 = 1.0 - ADAM_B1 ** ADAM_STEP
    bc2 = 1.0 - ADAM_B2 ** ADAM_STEP

    def body(w_ref, g_ref, m_ref, v_ref, go_ref, d_ref, nm_ref, nv_ref):
        if parts:
            gv = g_ref[0].astype(F32)
            for s in range(1, N_DEV):
                gv = gv + g_ref[s].astype(F32)
        else:
            gv = g_ref[...]
        go_ref[0] = gv
        nm = ADAM_B1 * m_ref[0] + c1 * gv
        nv = ADAM_B2 * v_ref[0] + c2 * (gv * gv)
        nm_ref[0] = nm
        nv_ref[0] = nv
        d_ref[0] = -ADAM_LR * ((nm / bc1) / (jnp.sqrt(nv / bc2) + ADAM_EPS) + ADAM_WD * w_ref[0])

    spec = pl.BlockSpec((1, tm, C), lambda i: (0, i, 0))
    gspec = pl.BlockSpec((N_DEV, tm, C), lambda i: (0, i, 0)) if parts else pl.BlockSpec((tm, C), _row)
    return pl.pallas_call(
        body, name=name, grid=(R // tm,),
        in_specs=[spec, gspec, spec, spec], out_specs=[spec] * 4,
        out_shape=[jax.ShapeDtypeStruct((1, R, C), F32)] * 4,
        compiler_params=_cp(("parallel",)),
    )(w, g, m, v)


def _mods_part(c_all, w_ada, b_ada, *, name):
    C = w_ada.shape[1]

    def body(c_ref, w_ref, b_ref, o_ref):
        cv = c_ref[...]
        ca = cv * jax.nn.sigmoid(cv)
        o_ref[...] = jnp.dot(ca, w_ref[...], preferred_element_type=F32,
                             precision=lax.Precision.HIGHEST) + b_ref[...]

    return pl.pallas_call(
        body, name=name,
        out_shape=jax.ShapeDtypeStruct((N_DEV, C), F32),
        compiler_params=_cp(),
    )(c_all, w_ada, b_ada)


def _wada_grad(c_all_t, gm, *, name):
    C = gm.shape[1]

    def body(c_ref, g_ref, o_ref):
        cv = c_ref[...]
        ca = cv * jax.nn.sigmoid(cv)
        acc = ca[:, 0:1] * g_ref[0:1, :]
        for b in range(1, N_DEV):
            acc = acc + ca[:, b:b + 1] * g_ref[b:b + 1, :]
        o_ref[...] = acc

    return pl.pallas_call(
        body, name=name,
        out_shape=jax.ShapeDtypeStruct((D, C), F32),
        compiler_params=_cp(),
    )(c_all_t, gm)


def _peer(x, y, c, d):
    px = lax.rem(x + ((d >> 2) & 1), 2)
    py = lax.rem(y + ((d >> 1) & 1), 2)
    pc = lax.rem(c + (d & 1), 2)
    return (px, py, pc), 4 * px + 2 * py + pc


def _exchange(xs, *, scatter, name):
    n = len(xs)
    nsem = n * (N_DEV - 1)

    def body(*refs):
        ins, outs = refs[:n], refs[n:2 * n]
        token, send_sems, recv_sems, local_sems = refs[2 * n:]
        x, y, c = lax.axis_index("x"), lax.axis_index("y"), lax.axis_index("c")
        me = 4 * x + 2 * y + c
        token[...] = jnp.zeros_like(token)

        def src(t, idx):
            return ins[t].at[idx] if scatter else ins[t]

        local = [pltpu.make_async_copy(src(t, me), outs[t].at[me], local_sems.at[t]) for t in range(n)]
        for cp in local:
            cp.start()
        remote = []
        for t in range(n):
            for d in range(1, N_DEV):
                peer, pidx = _peer(x, y, c, d)
                k = t * (N_DEV - 1) + d - 1
                send = pltpu.make_async_remote_copy(src_ref=src(t, pidx), dst_ref=outs[t].at[me],
                                                    send_sem=send_sems.at[k], recv_sem=recv_sems.at[k],
                                                    device_id=peer, device_id_type=MESH)
                recv = pltpu.make_async_remote_copy(src_ref=src(t, pidx), dst_ref=outs[t].at[pidx],
                                                    send_sem=send_sems.at[k], recv_sem=recv_sems.at[k],
                                                    device_id=peer, device_id_type=MESH)
                send.start()
                remote.append((send, recv))
        for cp in local:
            cp.wait()
        for send, recv in remote:
            send.wait_send()
            recv.wait_recv()

    anyspec = pl.BlockSpec(memory_space=pl.ANY)
    out_shape = [jax.ShapeDtypeStruct(a.shape if scatter else (N_DEV,) + a.shape, a.dtype) for a in xs]
    out_shape.append(jax.ShapeDtypeStruct((8, 128), F32))
    return pl.pallas_call(
        body, name=name,
        in_specs=[anyspec] * n, out_specs=[anyspec] * n + [pl.BlockSpec(memory_space=pltpu.VMEM)],
        out_shape=out_shape,
        scratch_shapes=[pltpu.SemaphoreType.DMA((nsem,)), pltpu.SemaphoreType.DMA((nsem,)),
                        pltpu.SemaphoreType.DMA((n,))],
    )(*xs)


def _sum8(parts, *, name):
    _, R, C = parts.shape

    def body(p_ref, o_ref):
        acc = p_ref[0]
        for s in range(1, N_DEV):
            acc = acc + p_ref[s]
        o_ref[...] = acc

    return pl.pallas_call(body, name=name, out_shape=jax.ShapeDtypeStruct((R, C), F32),
                          compiler_params=_cp())(parts)


HBM_SPEC = pl.BlockSpec(memory_space=pltpu.HBM)
SEM_SPEC = pl.BlockSpec(memory_space=pltpu.SEMAPHORE)
N_PEER = N_DEV - 1


def _split_copies(src_refs, land_refs, send_sems, recv_sems, scatter):
    x, y, c = lax.axis_index("x"), lax.axis_index("y"), lax.axis_index("c")
    me = 4 * x + 2 * y + c
    pairs = []
    for j, (src, land) in enumerate(zip(src_refs, land_refs)):
        for d in range(1, N_DEV):
            peer, pidx = _peer(x, y, c, d)
            k = j * N_PEER + d - 1
            s = src.at[pidx] if scatter else src
            send = pltpu.make_async_remote_copy(src_ref=s, dst_ref=land.at[me], send_sem=send_sems.at[k],
                                                recv_sem=recv_sems.at[k], device_id=peer, device_id_type=MESH)
            recv = pltpu.make_async_remote_copy(src_ref=s, dst_ref=land.at[pidx], send_sem=send_sems.at[k],
                                                recv_sem=recv_sems.at[k], device_id=peer, device_id_type=MESH)
            pairs.append((send, recv))
    return pairs


def _own_slot(block, me):
    land = lax.empty((N_DEV,) + block.shape, block.dtype)
    return lax.dynamic_update_slice(land, block[None], (me, 0, 0))


def _split_start(srcs, lands, groups, *, scatter, name):
    n, ng = len(srcs), len(groups)

    def body(*refs):
        src_refs, land_refs = refs[:n], refs[n:2 * n]
        sems = refs[2 * n:2 * n + 2 * ng]
        token = refs[-1]
        for gi, g in enumerate(groups):
            pairs = _split_copies([src_refs[t] for t in g], [land_refs[t] for t in g], sems[2 * gi],
                                  sems[2 * gi + 1], scatter)
            for send, _ in pairs:
                send.start()
        token[...] = jnp.zeros_like(token)

    sem_shapes = []
    for g in groups:
        sem_shapes += [pltpu.SemaphoreType.DMA((len(g) * N_PEER,))] * 2
    thru = [pltpu.HBM(a.shape, a.dtype) for a in list(srcs) + list(lands)]
    outs = pl.pallas_call(
        body, name=name,
        out_shape=tuple(sem_shapes + thru + [jax.ShapeDtypeStruct((8, 128), F32)]),
        in_specs=[HBM_SPEC] * (2 * n),
        out_specs=tuple([SEM_SPEC] * (2 * ng) + [HBM_SPEC] * (2 * n) + [pl.BlockSpec(memory_space=pltpu.VMEM)]),
        input_output_aliases={i: 2 * ng + i for i in range(2 * n)},
        compiler_params=pltpu.CompilerParams(has_side_effects=pltpu.SideEffectType.DATAFLOW_SIDE_EFFECTING),
    )(*[pltpu.with_memory_space_constraint(a, pltpu.HBM) for a in list(srcs) + list(lands)])
    sems = [(outs[2 * gi], outs[2 * gi + 1]) for gi in range(ng)]
    return sems, outs[2 * ng:2 * ng + n], outs[2 * ng + n:2 * ng + 2 * n], outs[-1]


def _behind(v, token):
    if token is None:
        return v
    return v + token[0, 0].astype(v.dtype)


def _split_wait(srcs, lands, sems, after, *, scatter, name):
    m = len(srcs)

    def body(*refs):
        src_refs, land_refs = refs[:m], refs[m:2 * m]
        send_sems, recv_sems = refs[2 * m], refs[2 * m + 1]
        for send, recv in _split_copies(src_refs, land_refs, send_sems, recv_sems, scatter):
            send.wait_send()
            recv.wait_recv()

    outs = pl.pallas_call(
        body, name=name,
        out_shape=tuple(pltpu.HBM(a.shape, a.dtype) for a in list(srcs) + list(lands)),
        in_specs=[HBM_SPEC] * (2 * m) + [SEM_SPEC, SEM_SPEC, pl.BlockSpec(memory_space=pl.ANY)],
        out_specs=tuple([HBM_SPEC] * (2 * m)),
        input_output_aliases={i: i for i in range(2 * m)},
        compiler_params=pltpu.CompilerParams(has_side_effects=pltpu.SideEffectType.DATAFLOW_SIDE_EFFECTING),
    )(*srcs, *lands, sems[0], sems[1], after)
    return outs[m:]


TM_PROJ = 1024
TM_ROW = 512
TM_NN = 1024
TK_TN = 512
TN_FFN = F // 2
TN_IN = NIN // 4


def _tn(a, b, name, tn):
    if a.ndim == 2:
        a = a[None]
    return _tn_matmul(a, b, tn=tn, tk=TK_TN, name=name)


def _local_step(x, tgt, mods, g1, gm, g2, gf, convw8, sinks, w_get, g_put):
    T = x.shape[0]
    sh1, sc1, gt1, sh2, sc2, gt2, sh3, sc3, gt3 = [mods[i:i + 1] for i in range(N_MOD)]
    cos, sin = _rope_tables(T)
    behind = _behind

    w = dict(w_get("gu1", mods))
    h1, ab1 = _norm_proj(x, g1, sc1, sh1, w["gu1"], tm=TM_PROJ, tn=TN_FFN, name="ffn1_up")
    w.update(w_get("d1", ab1))
    x1, y1 = _ffn_down_fwd(ab1, w["d1"], x, gt1, tm=TM_ROW, name="ffn1_down")
    w.update(w_get("mix", x1))
    h2, proj = _norm_proj(x1, gm, sc2, sh2, w["win"], tm=TM_PROJ, tn=TN_IN, name="mix_in")
    qs, kr = _attn_prep(proj, cos, sin, name="attn_prep")
    bias = _attn_bias()
    attn, lse = _attn_fwd(qs, kr, proj, bias, sinks, name="attn_fwd")
    x2, gc, yc, ya, mg, o = _mixer_mid_fwd(proj, attn, w["cp"], w["ap"], w["out"], convw8, x1, gt2,
                                           tm=TM_ROW, name="mix_mid")
    w.update(w_get("ffn2", x2))
    h3, ab2 = _norm_proj(x2, g2, sc3, sh3, w["gu2"], tm=TM_PROJ, tn=TN_FFN, name="ffn2_up")
    x3, y2 = _ffn_down_fwd(ab2, w["d2"], x2, gt3, tm=TM_ROW, name="ffn2_down")
    dx3, lsum, dgf = _final_fwd_bwd(x3, tgt, gf, tm=TM_ROW, name="final")

    dy2, dab2, dgt3 = _ffn_down_bwd(dx3, y2, gt3, ab2, w["d2"], tm=TM_ROW, tn=MXU_N, name="ffn2_down_bwd")
    g_d2 = _tn_matmul_swiglu(ab2, dy2, None, tn=TN_FFN, tk=TK_TN, name="ffn2_down_dw")
    dx2, dsh3, dsc3, dg2 = _nn_bwd_norm(dab2, w["gu2"], x2, g2, sc3, dx3, tm=TM_NN, tk=TN_FFN, name="ffn2_up_bwd")
    g_gu2 = _tn(dab2, h3, "ffn2_up_dw", TN_FFN)
    tok = g_put(dict(gu2=g_gu2, d2=g_d2))

    dout, dyc, dya, dgc, dat, dproj, dgt2 = _mixer_mid_bwd(dx2, behind(gt2, tok), o, proj, yc, ya, w["out"], w["cp"],
                                                           w["ap"], tm=TM_ROW, name="mix_mid_bwd")
    g_out = _tn(mg, dout, "mix_out_dw", D)
    g_cp = _tn(gc, dyc, "mix_cp_dw", D)
    g_ap = _tn(attn, dya, "mix_ap_dw", D)
    dproj, dkc, dkp, dvc, dvp, dsink = _attn_bwd(qs, kr, proj, bias, sinks, lse, attn, dat, cos, sin, dproj,
                                                 name="attn_bwd")
    dproj = _dkv_combine(dkc, dkp, dvc, dvp, dproj, name="attn_dkv")
    dproj, dcw = _conv_bwd(dgc, proj, convw8, dproj, tm=TM_ROW, name="conv_bwd")
    g_in = _tn(dproj, h2, "mix_in_dw", TN_IN)
    tok = g_put(dict(win=g_in, cp=g_cp, ap=g_ap, out=g_out))
    dx1, dsh2, dsc2, dgm = _nn_bwd_norm(dproj[None], w["win"], x1, gm, behind(sc2, tok), dx2, tm=TM_NN, tk=TN_IN,
                                        name="mix_in_bwd")

    dy1, dab1, dgt1 = _ffn_down_bwd(dx1, y1, gt1, ab1, w["d1"], tm=TM_ROW, tn=MXU_N, name="ffn1_down_bwd")
    g_gu1 = _tn(dab1, h1, "ffn1_up_dw", TN_FFN)
    tok = g_put(dict(gu1=g_gu1))
    g_d1 = _tn_matmul_swiglu(ab1, dy1, tok, tn=TN_FFN, tk=TK_TN, name="ffn1_down_dw")
    tok = g_put(dict(d1=g_d1))
    dx0, dsh1, dsc1, dg1 = _nn_bwd_norm(dab1, w["gu1"], x, g1, behind(sc1, tok), dx1, tm=TM_NN, tk=TN_FFN,
                                        name="ffn1_up_bwd")

    small = dict(mods=jnp.concatenate([dsh1, dsc1, dgt1, dsh2, dsc2, dgt2, dsh3, dsc3, dgt3], axis=0),
                 g1=dg1, gm=dgm, g2=dg2, gf=dgf, convw=dcw[0:3], sinks=dsink[:, 0:N_HEADS])
    return lsum, dx0, small


BIG = ("gu1", "d1", "win", "cp", "ap", "out", "gu2", "d2")
TRANSPOSED = ("gu1", "win", "gu2")
SMALL_ROWS = 24
R_MODS, R_G1, R_GM, R_G2, R_GF, R_CONV, R_SINK = 0, 9, 10, 11, 12, 13, 16


def _pad_to(a, rows, cols):
    return jnp.pad(a, ((0, rows - a.shape[0]), (0, cols - a.shape[1])))


def _pack_small(b_ada, g1, gm, g2, gf, conv, sinks):
    rows = [b_ada.reshape(N_MOD, D), g1.reshape(1, D), gm.reshape(1, D), g2.reshape(1, D), gf.reshape(1, D),
            _pad_to(conv.reshape(3, -1), 3, D), _pad_to(sinks.reshape(1, N_HEADS), 1, D)]
    return _pad_to(jnp.concatenate(rows, axis=0), SMALL_ROWS, D)


def _unpack_small(p, conv_cols):
    return dict(b_ada=p[R_MODS:R_MODS + N_MOD].reshape(1, N_MOD * D), g_ffn1=p[R_G1:R_G1 + 1],
                g_mix=p[R_GM:R_GM + 1], g_ffn2=p[R_G2:R_G2 + 1], g_final=p[R_GF],
                conv_w=p[R_CONV:R_CONV + 3, 0:conv_cols][None], sinks=p[R_SINK:R_SINK + 1, 0:N_HEADS])


def kernel(x, c, w_ada, b_ada, g_ffn1, w1_gu, w1_down, g_mix, w_in, conv_w, w_conv_proj, w_attn_proj, sinks, w_out, g_ffn2, w2_gu, w2_down, g_final, loss_target, m_w_ada, m_b_ada, m_g_ffn1, m_w1_gu, m_w1_down, m_g_mix, m_w_in, m_conv_w, m_w_conv_proj, m_w_attn_proj, m_sinks, m_w_out, m_g_ffn2, m_w2_gu, m_w2_down, m_g_final, v_w_ada, v_b_ada, v_g_ffn1, v_w1_gu, v_w1_down, v_g_mix, v_w_in, v_conv_w, v_w_conv_proj, v_w_attn_proj, v_sinks, v_w_out, v_g_ffn2, v_w2_gu, v_w2_down, v_g_final):
    me = 4 * lax.axis_index("x") + 2 * lax.axis_index("y") + lax.axis_index("c")
    ada_cols = w_ada.shape[2]
    conv_cols = conv_w.shape[2]

    native = dict(gu1=w1_gu[0], d1=w1_down[0], win=w_in[0], cp=w_conv_proj[0], ap=w_attn_proj[0], out=w_out[0],
                  gu2=w2_gu[0], d2=w2_down[0])

    def shard(n, token):
        a = _behind(native[n], token)
        return (a.T if n in TRANSPOSED else a).astype(BF)

    c_all, conv_all, _ = _exchange([c, _pad_to(conv_w[0], 8, conv_cols)], scatter=False, name="gather_cond")
    c_all = c_all.reshape(N_DEV, D)
    conv_full = conv_all[:, 0:3, :].transpose(1, 0, 2).reshape(3, D)

    b_cols = lax.dynamic_slice(b_ada, (0, me * ada_cols), (1, ada_cols))
    mods_cols = _mods_part(c_all, w_ada[0], b_cols, name="ada_mods")
    mods_all, mods_token = _exchange([mods_cols], scatter=False, name="gather_mods")
    mods = lax.dynamic_index_in_dim(mods_all, me, axis=1, keepdims=False).reshape(N_MOD, D)

    groups = dict(gu1=("gu1",), d1=("d1",), mix=("win", "cp", "ap", "out"), ffn2=("gu2", "d2"))
    in_flight = {}
    first = [shard("gu1", mods_token)]
    sems, srcs, lands, token = _split_start(first, [_own_slot(s, me) for s in first], [[0]], scatter=False,
                                            name="gather_weights_start_gu1")
    in_flight["gu1"] = (sems[0], srcs, lands)
    rest = [n for n in BIG if n != "gu1"]
    shards = [shard(n, token) for n in rest]
    rest_groups = [[rest.index(n) for n in names] for g, names in groups.items() if g != "gu1"]
    sems, srcs, lands, rest_token = _split_start(shards, [_own_slot(s, me) for s in shards], rest_groups,
                                                 scatter=False, name="gather_weights_start_rest")
    for (g, names), gsems, idx in zip([kv for kv in groups.items() if kv[0] != "gu1"], sems, rest_groups):
        in_flight[g] = (gsems, [srcs[t] for t in idx], [lands[t] for t in idx])

    def w_get(group, after):
        if group == "gu1":
            after = rest_token
        gsems, gsrcs, glands = in_flight[group]
        landed = _split_wait(gsrcs, glands, gsems, after, scatter=False, name="gather_weights_wait_" + group)
        return {n: a.reshape(-1, D) for n, a in zip(groups[group], landed)}

    pending = []

    def g_put(gs):
        names = tuple(gs)
        srcs = [gs[n].reshape(N_DEV, -1, D) for n in names]
        lands = [_own_slot(lax.dynamic_index_in_dim(s, me, axis=0, keepdims=False), me) for s in srcs]
        sems, srcs, lands, tok = _split_start(srcs, lands, [list(range(len(names)))], scatter=True,
                                              name="scatter_grads_start_" + names[0])
        pending.append((names, sems[0], srcs, lands))
        return tok

    lsum, grad_x, small = _local_step(x[0], loss_target[0], mods, g_ffn1, g_mix, g_ffn2, g_final[None],
                                      _pad_to(conv_full, 8, D), sinks[0], w_get, g_put)
    loss = lax.psum((0.5 / D) * jnp.sum(lsum), ("x", "y", "c"))

    packed = _pack_small(small["mods"], small["g1"], small["gm"], small["g2"], small["gf"], small["convw"],
                         small["sinks"])
    packed_all, _ = _exchange([packed], scatter=False, name="gather_small")
    gsmall = _sum8(packed_all, name="sum_small")

    w_of = dict(ada=w_ada, gu1=w1_gu, d1=w1_down, win=w_in, cp=w_conv_proj, ap=w_attn_proj, out=w_out, gu2=w2_gu,
                d2=w2_down)
    m_of = dict(ada=m_w_ada, gu1=m_w1_gu, d1=m_w1_down, win=m_w_in, cp=m_w_conv_proj, ap=m_w_attn_proj, out=m_w_out,
                gu2=m_w2_gu, d2=m_w2_down)
    v_of = dict(ada=v_w_ada, gu1=v_w1_gu, d1=v_w1_down, win=v_w_in, cp=v_w_conv_proj, ap=v_w_attn_proj, out=v_w_out,
                gu2=v_w2_gu, d2=v_w2_down)
    upd = {}
    after = gsmall
    for names, sems, srcs, lands in pending:
        parts = _split_wait(srcs, lands, sems, after, scatter=True, name="scatter_grads_wait_" + names[0])
        for n, p in zip(names, parts):
            if n in TRANSPOSED:
                res = _adam(jnp.swapaxes(w_of[n], 1, 2), p, jnp.swapaxes(m_of[n], 1, 2), jnp.swapaxes(v_of[n], 1, 2),
                            tm=128, name="adam_" + n)
                upd[n] = [jnp.swapaxes(t, 1, 2) for t in res]
            else:
                upd[n] = _adam(w_of[n], p, m_of[n], v_of[n], tm=128, name="adam_" + n)
        after = upd[names[-1]][1]

    gm_cols = lax.dynamic_slice(packed_all[:, R_MODS:R_MODS + N_MOD, :].reshape(N_DEV, N_MOD * D),
                                (0, me * ada_cols), (N_DEV, ada_cols))
    upd["ada"] = _adam(w_ada, _wada_grad(c_all.T, gm_cols, name="ada_dw"), m_w_ada, v_w_ada, tm=128, name="adam_ada")
    conv_g = lax.dynamic_slice(gsmall[R_CONV:R_CONV + 3], (0, me * conv_cols), (3, conv_cols))
    gsmall_own = gsmall.at[R_CONV:R_CONV + 3].set(_pad_to(conv_g, 3, D))
    small_upd = _adam(_pack_small(b_ada, g_ffn1, g_mix, g_ffn2, g_final, conv_w, sinks)[None], gsmall_own,
                      _pack_small(m_b_ada, m_g_ffn1, m_g_mix, m_g_ffn2, m_g_final, m_conv_w, m_sinks)[None],
                      _pack_small(v_b_ada, v_g_ffn1, v_g_mix, v_g_ffn2, v_g_final, v_conv_w, v_sinks)[None],
                      tm=SMALL_ROWS, name="adam_small")
    small_out = [_unpack_small(p[0], conv_cols) for p in small_upd]

    big_name = dict(w_ada="ada", w1_gu="gu1", w1_down="d1", w_in="win", w_conv_proj="cp", w_attn_proj="ap",
                    w_out="out", w2_gu="gu2", w2_down="d2")
    order = ("w_ada", "b_ada", "g_ffn1", "w1_gu", "w1_down", "g_mix", "w_in", "conv_w", "w_conv_proj", "w_attn_proj",
             "sinks", "w_out", "g_ffn2", "w2_gu", "w2_down", "g_final")
    outs = [loss, grad_x[None]]
    for kind in range(4):
        for n in order:
            outs.append(upd[big_name[n]][kind] if n in big_name else small_out[kind][n])
    return tuple(outs)
```

```python
import functools

import jax
import jax.numpy as jnp
from jax import lax
from jax.experimental import pallas as pl
from jax.experimental.pallas import tpu as pltpu

D = 1024
F = 2816
NIN = 6656
N_HEADS = 16
N_KV = 4
HEAD_DIM = 64
BLK = 128
N_MOD = 9
N_DEV = 8
EPS = 1e-6
NEG_INF = -1e30
ROPE_THETA = 10000.0
O_BG, O_CG, O_U, O_Q, O_K, O_V, O_ZC, O_ZA = 0, 1024, 2048, 3072, 4096, 4352, 4608, 5632

ADAM_LR = 0.001
ADAM_B1 = 0.9
ADAM_B2 = 0.999
ADAM_EPS = 1e-08
ADAM_WD = 0.01
ADAM_STEP = 10

BF = jnp.bfloat16
F32 = jnp.float32
VMEM_LIMIT = 56 * 1024 * 1024
MXU_N = 256
MESH = pl.DeviceIdType.MESH

NT = (((1,), (1,)), ((), ()))
TN = (((0,), (0,)), ((), ()))


def _cp(sem=None):
    return pltpu.CompilerParams(dimension_semantics=sem, vmem_limit_bytes=VMEM_LIMIT)


def _tile(n, pref):
    if n <= pref:
        return n
    for t in range(pref - pref % 16, 15, -16):
        if n % t == 0:
            return t
    raise ValueError((n, pref))


def _sigmoid(v):
    return 0.5 * jnp.tanh(0.5 * v) + 0.5


def _row(i):
    return (i, 0)


def _const2(*_):
    return (0, 0)


def _resident(shape):
    return pl.BlockSpec(shape, lambda *_: (0,) * len(shape), pipeline_mode=pl.Buffered(1))


def _norm_proj(x, g, sc, sh, wt, *, tm, tn, name):
    T, N = x.shape[0], wt.shape[0]
    tm = _tile(T, tm)

    def body(x_ref, g_ref, sc_ref, sh_ref, w_ref, h_ref, o_ref):
        xv = x_ref[...]
        r = lax.rsqrt(jnp.mean(xv * xv, axis=-1, keepdims=True) + EPS)
        hb = ((xv * r) * g_ref[...] * (1.0 + sc_ref[...]) + sh_ref[...]).astype(BF)
        h_ref[...] = hb
        for c0 in range(0, N, tn):
            cols = pl.ds(c0, tn)
            o_ref[:, cols] = lax.dot_general(hb, w_ref[cols, :], NT, preferred_element_type=F32).astype(BF)

    vec = pl.BlockSpec((1, D), _const2)
    return pl.pallas_call(
        body, name=name, grid=(T // tm,),
        in_specs=[pl.BlockSpec((tm, D), _row), vec, vec, vec, _resident((N, D))],
        out_specs=[pl.BlockSpec((tm, D), _row), pl.BlockSpec((tm, N), _row)],
        out_shape=[jax.ShapeDtypeStruct((T, D), BF), jax.ShapeDtypeStruct((T, N), BF)],
        compiler_params=_cp(("parallel",)),
    )(x, g, sc, sh, wt)


def _ffn_down_fwd(ab, wd, x, gt, *, tm, name):
    T = x.shape[0]
    tm = _tile(T, tm)

    def body(a_ref, b_ref, wd_ref, x_ref, gt_ref, xo_ref, y_ref):
        y = None
        for c0 in range(0, F, MXU_N):
            cols = pl.ds(c0, MXU_N)
            a = a_ref[:, cols].astype(F32)
            act = (a * _sigmoid(a) * b_ref[:, cols].astype(F32)).astype(BF)
            part = jnp.dot(act, wd_ref[cols, :], preferred_element_type=F32)
            y = part if y is None else y + part
        y_ref[...] = y.astype(BF)
        xo_ref[...] = x_ref[...] + (0.5 * gt_ref[...]) * y

    return pl.pallas_call(
        body, name=name, grid=(T // tm,),
        in_specs=[pl.BlockSpec((tm, F), lambda i: (i, 0)), pl.BlockSpec((tm, F), lambda i: (i, 1)),
                  pl.BlockSpec((F, D), _const2), pl.BlockSpec((tm, D), _row), pl.BlockSpec((1, D), _const2)],
        out_specs=[pl.BlockSpec((tm, D), _row), pl.BlockSpec((tm, D), _row)],
        out_shape=[jax.ShapeDtypeStruct((T, D), F32), jax.ShapeDtypeStruct((T, D), BF)],
        compiler_params=_cp(("parallel",)),
    )(ab, ab, wd, x, gt)


def _final_fwd_bwd(x, tgt, g, *, tm, name):
    T = x.shape[0]
    tm = _tile(T, tm)

    def body(x_ref, t_ref, g_ref, dx_ref, ls_ref, dg_ref):
        @pl.when(pl.program_id(0) == 0)
        def _():
            ls_ref[...] = jnp.zeros_like(ls_ref)
            dg_ref[...] = jnp.zeros_like(dg_ref)
        xv = x_ref[...]
        gv = g_ref[...]
        r = lax.rsqrt(jnp.mean(xv * xv, axis=-1, keepdims=True) + EPS)
        xh = xv * r
        e = xh * gv - t_ref[...]
        ls_ref[...] += jnp.sum(e * e, axis=0, keepdims=True)
        dy = e * (1.0 / D)
        dg_ref[...] += jnp.sum(dy * xh, axis=0, keepdims=True)
        dxh = dy * gv
        dx_ref[...] = r * (dxh - xh * jnp.mean(dxh * xh, axis=-1, keepdims=True))

    vec = pl.BlockSpec((1, D), _const2)
    return pl.pallas_call(
        body, name=name, grid=(T // tm,),
        in_specs=[pl.BlockSpec((tm, D), _row), pl.BlockSpec((tm, D), _row), vec],
        out_specs=[pl.BlockSpec((tm, D), _row), vec, vec],
        out_shape=[jax.ShapeDtypeStruct((T, D), F32), jax.ShapeDtypeStruct((1, D), F32),
                   jax.ShapeDtypeStruct((1, D), F32)],
        compiler_params=_cp(("arbitrary",)),
    )(x, tgt, g)


def _ffn_down_bwd(dxo, y, gt, ab, wd, *, tm, tn, name):
    T = dxo.shape[0]
    tm = _tile(T, tm)

    def body(dxo_ref, y_ref, gt_ref, a_ref, b_ref, wd_ref, dy_ref, dab_ref, dgt_ref):
        @pl.when(pl.program_id(0) == 0)
        def _():
            dgt_ref[...] = jnp.zeros_like(dgt_ref)

        dxv = dxo_ref[...]
        dgt_ref[...] += 0.5 * jnp.sum(dxv * y_ref[...].astype(F32), axis=0, keepdims=True)
        dy = ((0.5 * gt_ref[...]) * dxv).astype(BF)
        dy_ref[...] = dy
        for c0 in range(0, F, tn):
            cols = pl.ds(c0, tn)
            dact = lax.dot_general(dy, wd_ref[cols, :], NT, preferred_element_type=F32)
            a = a_ref[:, cols].astype(F32)
            b = b_ref[:, cols].astype(F32)
            s = _sigmoid(a)
            dab_ref[0, :, cols] = (dact * b * (s * (1.0 + a * (1.0 - s)))).astype(BF)
            dab_ref[1, :, cols] = (dact * (a * s)).astype(BF)

    vec = pl.BlockSpec((1, D), _const2)
    rowspec = pl.BlockSpec((tm, D), _row)
    return pl.pallas_call(
        body, name=name, grid=(T // tm,),
        in_specs=[rowspec, rowspec, vec, pl.BlockSpec((tm, F), lambda i: (i, 0)),
                  pl.BlockSpec((tm, F), lambda i: (i, 1)), pl.BlockSpec((F, D), _const2)],
        out_specs=[rowspec, pl.BlockSpec((2, tm, F), lambda i: (0, i, 0)), vec],
        out_shape=[jax.ShapeDtypeStruct((T, D), BF), jax.ShapeDtypeStruct((2, T, F), BF),
                   jax.ShapeDtypeStruct((1, D), F32)],
        compiler_params=_cp(("arbitrary",)),
    )(dxo, y, gt, ab, ab, wd)


def _tn_matmul(a, b, *, tn, tk, name):
    S, T, Ns = a.shape
    tn, tk = _tile(Ns, tn), _tile(T, tk)
    nk, njs = T // tk, Ns // tn

    def body(a_ref, b_ref, o_ref, acc):
        k = pl.program_id(1)

        @pl.when(k == 0)
        def _():
            acc[...] = jnp.zeros_like(acc)
        acc[...] += lax.dot_general(a_ref[0], b_ref[...], TN, preferred_element_type=F32)

        @pl.when(k == nk - 1)
        def _():
            o_ref[...] = acc[...].astype(BF)

    return pl.pallas_call(
        body, name=name, grid=(S * njs, nk),
        in_specs=[pl.BlockSpec((1, tk, tn), lambda j, k: (j // njs, k, j % njs)),
                  pl.BlockSpec((tk, D), lambda j, k: (k, 0))],
        out_specs=pl.BlockSpec((tn, D), lambda j, k: (j, 0)),
        out_shape=jax.ShapeDtypeStruct((S * Ns, D), BF),
        scratch_shapes=[pltpu.VMEM((tn, D), F32)],
        compiler_params=_cp(("parallel", "arbitrary")),
    )(a, b)


def _tn_matmul_swiglu(ab, b, token, *, tn, tk, name):
    T = ab.shape[0]
    tn, tk = _tile(F, tn), _tile(T, tk)
    nk, nj = T // tk, F // tn
    deps = [] if token is None else [token]

    def body(a_ref, g_ref, b_ref, *rest):
        o_ref, acc = rest[len(deps):]
        k = pl.program_id(1)

        @pl.when(k == 0)
        def _():
            acc[...] = jnp.zeros_like(acc)
        bv = b_ref[...]
        for c0 in range(0, tn, MXU_N):
            cw = min(MXU_N, tn - c0)
            cols = pl.ds(c0, cw)
            a = a_ref[:, cols].astype(F32)
            act = (a * _sigmoid(a) * g_ref[:, cols].astype(F32)).astype(BF)
            acc[cols, :] += lax.dot_general(act, bv, TN, preferred_element_type=F32)

        @pl.when(k == nk - 1)
        def _():
            o_ref[...] = acc[...].astype(BF)

    return pl.pallas_call(
        body, name=name, grid=(nj, nk),
        in_specs=[pl.BlockSpec((tk, tn), lambda j, k: (k, j)), pl.BlockSpec((tk, tn), lambda j, k: (k, j + nj)),
                  pl.BlockSpec((tk, D), lambda j, k: (k, 0))] + [pl.BlockSpec(memory_space=pl.ANY)] * len(deps),
        out_specs=pl.BlockSpec((tn, D), lambda j, k: (j, 0)),
        out_shape=jax.ShapeDtypeStruct((F, D), BF),
        scratch_shapes=[pltpu.VMEM((tn, D), F32)],
        compiler_params=_cp(("parallel", "arbitrary")),
    )(ab, ab, b, *deps)


def _nn_bwd_norm(da, w, x, g, sc, dxo, *, tm, name):
    S, T, Ks = da.shape
    tm = _tile(T, tm)
    rc = _tile(tm, 256)

    def body(da_ref, w_ref, x_ref, g_ref, sc_ref, dxo_ref, dx_ref, dsh_ref, dsc_ref, dg_ref, acc):
        @pl.when(pl.program_id(0) == 0)
        def _():
            dsh_ref[...] = jnp.zeros_like(dsh_ref)
            dsc_ref[...] = jnp.zeros_like(dsc_ref)
            dg_ref[...] = jnp.zeros_like(dg_ref)

        d = jnp.dot(da_ref[0], w_ref[0:Ks, :], preferred_element_type=F32)
        for s in range(1, S):
            d = d + jnp.dot(da_ref[s], w_ref[s * Ks:(s + 1) * Ks, :], preferred_element_type=F32)
        acc[...] = d
        gv = g_ref[...]
        sc1 = 1.0 + sc_ref[...]
        dsh = jnp.zeros((1, D), F32)
        dsc = jnp.zeros((1, D), F32)
        dg = jnp.zeros((1, D), F32)
        for r0 in range(0, tm, rc):
            rows = pl.ds(r0, rc)
            u = acc[rows, :]
            xv = x_ref[rows, :]
            r = lax.rsqrt(jnp.mean(xv * xv, axis=-1, keepdims=True) + EPS)
            xh = xv * r
            dsh = dsh + jnp.sum(u, axis=0, keepdims=True)
            dsc = dsc + jnp.sum(u * (xh * gv), axis=0, keepdims=True)
            us = u * sc1
            dg = dg + jnp.sum(us * xh, axis=0, keepdims=True)
            dxh = us * gv
            dx_ref[rows, :] = dxo_ref[rows, :] + r * (dxh - xh * jnp.mean(dxh * xh, axis=-1, keepdims=True))
        dsh_ref[...] += dsh
        dsc_ref[...] += dsc
        dg_ref[...] += dg

    vec = pl.BlockSpec((1, D), _const2)
    rowspec = pl.BlockSpec((tm, D), _row)
    return pl.pallas_call(
        body, name=name, grid=(T // tm,),
        in_specs=[pl.BlockSpec((S, tm, Ks), lambda i: (0, i, 0)), _resident((S * Ks, D)), rowspec, vec, vec, rowspec],
        out_specs=[rowspec, vec, vec, vec],
        out_shape=[jax.ShapeDtypeStruct((T, D), F32)] + [jax.ShapeDtypeStruct((1, D), F32)] * 3,
        scratch_shapes=[pltpu.VMEM((tm, D), F32)],
        compiler_params=_cp(("arbitrary",)),
    )(da, w, x, g, sc, dxo)


def _rope(t, cos, sin_signed, lt32, inverse=False):
    sel = jnp.where(lt32, pltpu.roll(t, 96, 1), pltpu.roll(t, 32, 1))
    return t * cos - sel * sin_signed if inverse else t * cos + sel * sin_signed


def _rope_tables(T):
    inv = 1.0 / (ROPE_THETA ** (jnp.arange(0, HEAD_DIM, 2, dtype=F32) / HEAD_DIM))
    ang = jnp.arange(T, dtype=F32)[:, None] * inv[None, :]
    cos, sin = jnp.cos(ang), jnp.sin(ang)
    cos128 = jnp.tile(cos, (1, 4))
    sin128 = jnp.tile(jnp.concatenate([-sin, sin], axis=1), (1, 2))
    return cos128, sin128


QSCALE = HEAD_DIM ** -0.5


def _lane_masks(rows):
    lane = lax.broadcasted_iota(jnp.int32, (rows, 128), 1)
    return (lane % HEAD_DIM) < (HEAD_DIM // 2), [lane < HEAD_DIM, lane >= HEAD_DIM]


def _attn_bias():
    qi = lax.broadcasted_iota(jnp.int32, (4 * BLK, 2 * BLK), 0) % BLK
    kj = lax.broadcasted_iota(jnp.int32, (4 * BLK, 2 * BLK), 1)
    band = (kj > qi) & (kj <= qi + BLK)
    return jnp.stack([jnp.where(band & (kj >= BLK), 0.0, NEG_INF), jnp.where(band, 0.0, NEG_INF)]).astype(F32)


def _attn_prep(proj, cos, sin, *, name):
    T = proj.shape[0]
    tm = _tile(T, 4 * BLK)

    def body(q_ref, k_ref, c_ref, s_ref, qs_ref, kr_ref):
        lt32, halves = _lane_masks(BLK)
        for b in range(tm // BLK):
            rows = pl.ds(b * BLK, BLK)
            cc, sc = c_ref[rows, :], s_ref[rows, :]
            qr = [_rope(q_ref[rows, p * 128:(p + 1) * 128].astype(F32), cc, sc, lt32) * QSCALE for p in range(8)]
            for g in range(N_KV):
                qs_ref[g, pl.ds(4 * b * BLK, 4 * BLK), :] = _stack_heads(qr, g, halves).astype(BF)
            kr_ref[rows, :] = jnp.concatenate([_rope(k_ref[rows, r * 128:(r + 1) * 128].astype(F32), cc, sc, lt32)
                                               for r in range(2)], axis=1).astype(BF)

    tab = pl.BlockSpec((tm, 128), _row)
    return pl.pallas_call(
        body, name=name, grid=(T // tm,),
        in_specs=[pl.BlockSpec((tm, D), lambda n: (n, O_Q // D)), pl.BlockSpec((tm, 256), lambda n: (n, O_K // 256)),
                  tab, tab],
        out_specs=[pl.BlockSpec((N_KV, 4 * tm, 128), lambda n: (0, n, 0)), pl.BlockSpec((tm, 256), _row)],
        out_shape=[jax.ShapeDtypeStruct((N_KV, 4 * T, 128), BF), jax.ShapeDtypeStruct((T, 256), BF)],
        compiler_params=_cp(("parallel",)),
    )(proj, proj, cos, sin)


def _attn_specs():
    prev = lambda n: jnp.maximum(n - 1, 0)
    return [pl.BlockSpec((N_KV, 4 * BLK, 128), lambda n: (0, n, 0)),
            pl.BlockSpec((BLK, 256), _row), pl.BlockSpec((BLK, 256), lambda n: (prev(n), 0)),
            pl.BlockSpec((BLK, 256), lambda n: (n, O_V // 256)),
            pl.BlockSpec((BLK, 256), lambda n: (prev(n), O_V // 256)),
            pl.BlockSpec((1, 4 * BLK, 2 * BLK), lambda n: (jnp.minimum(n, 1), 0, 0)),
            pl.BlockSpec(memory_space=pltpu.SMEM)]


def _bands(kc_ref, kp_ref, vc_ref, vp_ref):
    kb, vb = [], []
    for r in range(2):
        cols = slice(r * 128, (r + 1) * 128)
        kb.append(jnp.concatenate([kp_ref[:, cols], kc_ref[:, cols]], axis=0))
        vb.append(jnp.concatenate([vp_ref[:, cols], vc_ref[:, cols]], axis=0))
    return kb, vb


def _sink_col(sink_ref, g):
    return jnp.concatenate([jnp.full((BLK, 1), sink_ref[4 * g + hh], F32) for hh in range(4)], axis=0)


def _unstack_heads(t, g, halves, acc):
    half = g % 2
    for hh in range(4):
        h = 4 * g + hh
        th = jnp.where(halves[half], t[hh * BLK:(hh + 1) * BLK], 0.0)
        if h % 2 != half:
            th = pltpu.roll(th, HEAD_DIM, 1)
        acc[h // 2] = acc[h // 2] + th


def _stack_heads(chunks, g, halves):
    half = g % 2
    parts = []
    for hh in range(4):
        h = 4 * g + hh
        t = chunks[h // 2]
        if h % 2 != half:
            t = pltpu.roll(t, HEAD_DIM, 1)
        parts.append(jnp.where(halves[half], t, 0.0))
    return jnp.concatenate(parts, axis=0)


def _attn_fwd(qs, kr, proj, bias, sinks, *, name):
    T = proj.shape[0]
    nb = T // BLK

    def body(qs_ref, kc_ref, kp_ref, vc_ref, vp_ref, bias_ref, sink_ref, o_ref, lse_ref):
        _, h128 = _lane_masks(BLK)
        _, h256 = _lane_masks(2 * BLK)
        _, h512 = _lane_masks(4 * BLK)
        kb, vb = _bands(kc_ref, kp_ref, vc_ref, vp_ref)
        outs = [jnp.zeros((BLK, 128), F32) for _ in range(8)]
        groups = range(N_KV)
        bias = bias_ref[0]
        sink = [_sink_col(sink_ref, g) for g in groups]
        s = [lax.dot_general(qs_ref[g], kb[g // 2], NT, preferred_element_type=F32) + bias for g in groups]
        m = [jnp.maximum(jnp.max(s[g], axis=-1, keepdims=True), sink[g]) for g in groups]
        p = [jnp.exp(s[g] - m[g]).astype(BF) for g in groups]
        vg = [jnp.where(h256[g % 2], vb[g // 2].astype(F32), 1.0).astype(BF) for g in groups]
        o = [jnp.dot(p[g], vg[g], preferred_element_type=F32) for g in groups]
        denom = [jnp.where(h512[g % 2], pltpu.roll(o[g], HEAD_DIM, 1), o[g]) + jnp.exp(sink[g] - m[g]) for g in groups]
        for g in groups:
            lse_ref[g] = m[g] + jnp.log(denom[g])
            _unstack_heads(o[g] * (1.0 / denom[g]), g, h128, outs)
        o_ref[...] = jnp.concatenate(outs, axis=1).astype(BF)

    return pl.pallas_call(
        body, name=name, grid=(nb,),
        in_specs=_attn_specs(),
        out_specs=[pl.BlockSpec((BLK, D), _row), pl.BlockSpec((N_KV, 4 * BLK, 128), lambda n: (0, n, 0))],
        out_shape=[jax.ShapeDtypeStruct((T, D), BF), jax.ShapeDtypeStruct((N_KV, 4 * T, 128), F32)],
        compiler_params=_cp(("parallel",)),
    )(qs, kr, kr, proj, proj, bias, sinks)


def _attn_bwd(qs, kr, proj, bias, sinks, lse, o, do, cos, sin, dproj, *, name):
    T = proj.shape[0]
    nb = T // BLK

    def body(qs_ref, kc_ref, kp_ref, vc_ref, vp_ref, bias_ref, sink_ref, lse_ref, o_ref, do_ref,
             cc_ref, sc_ref, cp_ref, sp_ref, dproj_ref, dq_ref, dkc_ref, dkp_ref, dvc_ref, dvp_ref, dsink_ref):
        @pl.when(pl.program_id(0) == 0)
        def _():
            dsink_ref[...] = jnp.zeros_like(dsink_ref)
        lt32, h128 = _lane_masks(BLK)
        kb, vb = _bands(kc_ref, kp_ref, vc_ref, vp_ref)
        oc = [o_ref[:, p * 128:(p + 1) * 128].astype(F32) for p in range(8)]
        doc = [do_ref[:, p * 128:(p + 1) * 128].astype(F32) for p in range(8)]
        dqs = [jnp.zeros((BLK, 128), F32) for _ in range(8)]
        lane1 = lax.broadcasted_iota(jnp.int32, (1, 128), 1)
        dsink = jnp.zeros((1, 128), F32)
        groups = range(N_KV)
        bias = bias_ref[0]
        q = [qs_ref[g] for g in groups]
        lse_g = [lse_ref[g] for g in groups]
        s = [lax.dot_general(q[g], kb[g // 2], NT, preferred_element_type=F32) + bias for g in groups]
        dos = [_stack_heads(doc, g, h128) for g in groups]
        dosb = [t.astype(BF) for t in dos]
        dp = [lax.dot_general(dosb[g], vb[g // 2], NT, preferred_element_type=F32) for g in groups]
        delta = [jnp.sum(dos[g] * _stack_heads(oc, g, h128), axis=-1, keepdims=True) for g in groups]
        p = [jnp.exp(s[g] - jnp.concatenate([lse_g[g], lse_g[g]], axis=1)) for g in groups]
        ds = [(p[g] * (dp[g] - delta[g])).astype(BF) for g in groups]
        pb = [t.astype(BF) for t in p]
        dvg = [lax.dot_general(pb[g], dosb[g], TN, preferred_element_type=F32) for g in groups]
        dkg = [lax.dot_general(ds[g], q[g], TN, preferred_element_type=F32) for g in groups]
        dqg = [jnp.dot(ds[g], kb[g // 2], preferred_element_type=F32) * QSCALE for g in groups]
        dvr = [dvg[0] + dvg[1], dvg[2] + dvg[3]]
        dkr = [dkg[0] + dkg[1], dkg[2] + dkg[3]]
        for g in groups:
            _unstack_heads(dqg[g], g, h128, dqs)
            dsk = -jnp.exp(_sink_col(sink_ref, g) - lse_g[g][:, 0:1]) * delta[g]
            for hh in range(4):
                val = jnp.sum(dsk[hh * BLK:(hh + 1) * BLK], axis=0, keepdims=True)
                dsink = dsink + jnp.where(lane1 == 4 * g + hh, val, 0.0)
        cc, sc, cp, sp = cc_ref[...], sc_ref[...], cp_ref[...], sp_ref[...]
        dsink_ref[...] += dsink
        dq_ref[...] = jnp.concatenate([_rope(t, cc, sc, lt32, inverse=True) for t in dqs], axis=1).astype(BF)
        dkp_ref[...] = jnp.concatenate([_rope(t[:BLK], cp, sp, lt32, inverse=True) for t in dkr], axis=1)
        dkc_ref[...] = jnp.concatenate([_rope(t[BLK:], cc, sc, lt32, inverse=True) for t in dkr], axis=1)
        dvp_ref[...] = jnp.concatenate([t[:BLK] for t in dvr], axis=1)
        dvc_ref[...] = jnp.concatenate([t[BLK:] for t in dvr], axis=1)

    kv = pl.BlockSpec((BLK, 256), _row)
    tc = pl.BlockSpec((BLK, 128), _row)
    tp = pl.BlockSpec((BLK, 128), lambda n: (jnp.maximum(n - 1, 0), 0))
    return pl.pallas_call(
        body, name=name, grid=(nb,),
        in_specs=_attn_specs() + [pl.BlockSpec((N_KV, 4 * BLK, 128), lambda n: (0, n, 0)),
                                  pl.BlockSpec((BLK, D), _row), pl.BlockSpec((BLK, D), _row), tc, tc, tp, tp,
                                  pl.BlockSpec(memory_space=pl.ANY)],
        out_specs=[pl.BlockSpec((BLK, D), lambda n: (n, O_Q // D)), kv, kv, kv, kv, pl.BlockSpec((1, 128), _const2)],
        out_shape=[jax.ShapeDtypeStruct(dproj.shape, BF)] + [jax.ShapeDtypeStruct((T, 256), F32)] * 4
        + [jax.ShapeDtypeStruct((1, 128), F32)],
        input_output_aliases={14: 0},
        compiler_params=_cp(("arbitrary",)),
    )(qs, kr, kr, proj, proj, bias, sinks, lse, o, do, cos, sin, cos, sin, dproj)


def _dkv_combine(dkc, dkp, dvc, dvp, dproj, *, name):
    T = dkc.shape[0]
    nb = T // BLK
    tm = _tile(T, 4 * BLK)
    bpt = tm // BLK
    nt = T // tm

    def body(dkc_ref, dkp_ref, dkn_ref, dvc_ref, dvp_ref, dvn_ref, dproj_ref, o_ref):
        keep = jnp.where(pl.program_id(0) == nt - 1, 0.0, 1.0)

        def shifted(prev_ref, next_ref):
            nxt = keep * next_ref[...]
            return nxt if bpt == 1 else jnp.concatenate([prev_ref[BLK:, :], nxt], axis=0)

        o_ref[:, 0:256] = (dkc_ref[...] + shifted(dkp_ref, dkn_ref)).astype(BF)
        o_ref[:, 256:512] = (dvc_ref[...] + shifted(dvp_ref, dvn_ref)).astype(BF)

    cur = pl.BlockSpec((tm, 256), _row)
    nxt = pl.BlockSpec((BLK, 256), lambda i: (jnp.minimum((i + 1) * bpt, nb - 1), 0))
    return pl.pallas_call(
        body, name=name, grid=(nt,),
        in_specs=[cur, cur, nxt, cur, cur, nxt, pl.BlockSpec(memory_space=pl.ANY)],
        out_specs=pl.BlockSpec((tm, 512), lambda i: (i, O_K // 512)),
        out_shape=jax.ShapeDtypeStruct(dproj.shape, BF),
        input_output_aliases={6: 0},
        compiler_params=_cp(("parallel",)),
    )(dkc, dkp, dkp, dvc, dvp, dvp, dproj)


HALO = 16


def _conv_shifts(cu, hprev, tm):
    row = lax.broadcasted_iota(jnp.int32, cu.shape, 0)
    h1 = hprev[HALO - 1:HALO, :]
    h2 = hprev[HALO - 2:HALO - 1, :]
    m1 = jnp.where(row == 0, h1, pltpu.roll(cu, 1, 0))
    m2 = jnp.where(row == 0, h2, jnp.where(row == 1, h1, pltpu.roll(cu, 2, 0)))
    return m1, m2


def _mixer_mid_fwd(proj, attn, wcp, wap, wout, convw, x, gt, *, tm, name):
    T = x.shape[0]
    tm = _tile(T, tm)
    hb = tm // HALO

    def body(bg_ref, cg_ref, u_ref, hcg_ref, hu_ref, zc0_ref, zc1_ref, za0_ref, za1_ref, at_ref,
             wcp_ref, wap_ref, wout_ref, cw_ref, x_ref, gt_ref,
             x2_ref, gc_ref, yc_ref, ya_ref, mg_ref, o_ref):
        first = jnp.where(pl.program_id(0) == 0, 0.0, 1.0)
        cu = cg_ref[...].astype(F32) * u_ref[...].astype(F32)
        hprev = first * (hcg_ref[...].astype(F32) * hu_ref[...].astype(F32))
        m1, m2 = _conv_shifts(cu, hprev, tm)
        cv = cw_ref[0:1, :] * m2 + cw_ref[1:2, :] * m1 + cw_ref[2:3, :] * cu
        gc = (bg_ref[...].astype(F32) * cv).astype(BF)
        gc_ref[...] = gc
        yc = jnp.dot(gc, wcp_ref[...], preferred_element_type=F32)
        ya = jnp.dot(at_ref[...], wap_ref[...], preferred_element_type=F32)
        yc_ref[...] = yc.astype(BF)
        ya_ref[...] = ya.astype(BF)
        zc = jnp.concatenate([zc0_ref[...], zc1_ref[...]], axis=1).astype(F32)
        za = jnp.concatenate([za0_ref[...], za1_ref[...]], axis=1).astype(F32)
        mg = (_sigmoid(zc) * yc + _sigmoid(za) * ya).astype(BF)
        mg_ref[...] = mg
        o = jnp.dot(mg, wout_ref[...], preferred_element_type=F32)
        o_ref[...] = o.astype(BF)
        x2_ref[...] = x_ref[...] + gt_ref[...] * o

    wspec = pl.BlockSpec((D, D), _const2)
    rowspec = pl.BlockSpec((tm, D), _row)
    return pl.pallas_call(
        body, name=name, grid=(T // tm,),
        in_specs=[_col(tm, O_BG), _col(tm, O_CG), _col(tm, O_U), _halo_prev(hb, O_CG), _halo_prev(hb, O_U),
                  _col(tm, O_ZC, 512), _col(tm, O_ZC + 512, 512), _col(tm, O_ZA, 512), _col(tm, O_ZA + 512, 512),
                  rowspec, wspec, wspec, wspec, pl.BlockSpec((8, D), _const2), rowspec, pl.BlockSpec((1, D), _const2)],
        out_specs=[rowspec] * 6,
        out_shape=[jax.ShapeDtypeStruct((T, D), F32)] + [jax.ShapeDtypeStruct((T, D), BF)] * 5,
        compiler_params=_cp(("parallel",)),
    )(proj, proj, proj, proj, proj, proj, proj, proj, proj, attn, wcp, wap, wout, convw, x, gt)


def _col(tm, c, w=D):
    assert c % w == 0
    return pl.BlockSpec((tm, w), lambda i: (i, c // w))


def _halo_prev(hb, c):
    return pl.BlockSpec((HALO, D), lambda i: (jnp.maximum(i * hb - 1, 0), c // D))


def _halo_next(hb, nblk, c=0):
    return pl.BlockSpec((HALO, D), lambda i: (jnp.minimum((i + 1) * hb, nblk - 1), c // D))


def _mixer_mid_bwd(dx2, gt, o, proj, yc, ya, wout, wcp, wap, *, tm, name):
    T = dx2.shape[0]
    tm = _tile(T, tm)
    zw = 512
    nz = 2 * D // zw

    def body(dx_ref, gt_ref, o_ref, zc0_ref, zc1_ref, za0_ref, za1_ref, yc_ref, ya_ref, wout_ref, wcp_ref, wap_ref,
             dout_ref, dyc_ref, dya_ref, dgc_ref, dat_ref, dz_ref, dgt_ref, dzs):
        i, j = pl.program_id(0), pl.program_id(1)

        @pl.when(jnp.logical_and(i == 0, j == 0))
        def _():
            dgt_ref[...] = jnp.zeros_like(dgt_ref)

        @pl.when(j == 0)
        def _():
            dxv = dx_ref[...]
            dgt_ref[...] += jnp.sum(dxv * o_ref[...].astype(F32), axis=0, keepdims=True)
            dout = (gt_ref[...] * dxv).astype(BF)
            dout_ref[...] = dout
            dmg = lax.dot_general(dout, wout_ref[...], NT, preferred_element_type=F32)
            sc = _sigmoid(jnp.concatenate([zc0_ref[...], zc1_ref[...]], axis=1).astype(F32))
            sa = _sigmoid(jnp.concatenate([za0_ref[...], za1_ref[...]], axis=1).astype(F32))
            dyc = (dmg * sc).astype(BF)
            dya = (dmg * sa).astype(BF)
            dyc_ref[...] = dyc
            dya_ref[...] = dya
            dzs[:, 0:D] = (dmg * yc_ref[...].astype(F32) * (sc * (1.0 - sc))).astype(BF)
            dzs[:, D:2 * D] = (dmg * ya_ref[...].astype(F32) * (sa * (1.0 - sa))).astype(BF)
            dgc_ref[...] = lax.dot_general(dyc, wcp_ref[...], NT, preferred_element_type=F32).astype(BF)
            dat_ref[...] = lax.dot_general(dya, wap_ref[...], NT, preferred_element_type=F32).astype(BF)

        for jj in range(nz):
            @pl.when(j == jj)
            def _(jj=jj):
                dz_ref[...] = dzs[:, jj * zw:(jj + 1) * zw]

    nt = T // tm

    def ahead(i, j):
        return jnp.minimum(i + jnp.minimum(j, 1), nt - 1)

    def zcol(c):
        return pl.BlockSpec((tm, zw), lambda i, j: (ahead(i, j), c // zw))

    wspec = pl.BlockSpec((D, D), _const2)
    rowin = pl.BlockSpec((tm, D), lambda i, j: (ahead(i, j), 0))
    rowspec = pl.BlockSpec((tm, D), lambda i, j: (i, 0))
    vec = pl.BlockSpec((1, D), _const2)
    return pl.pallas_call(
        body, name=name, grid=(nt, nz),
        in_specs=[rowin, vec, rowin, zcol(O_ZC), zcol(O_ZC + zw), zcol(O_ZA), zcol(O_ZA + zw),
                  rowin, rowin, wspec, wspec, wspec],
        out_specs=[rowspec] * 5 + [pl.BlockSpec((tm, zw), lambda i, j: (i, O_ZC // zw + j)), vec],
        out_shape=[jax.ShapeDtypeStruct((T, D), BF)] * 5 + [jax.ShapeDtypeStruct((T, NIN), BF),
                                                            jax.ShapeDtypeStruct((1, D), F32)],
        scratch_shapes=[pltpu.VMEM((tm, 2 * D), BF)],
        compiler_params=_cp(("arbitrary", "arbitrary")),
    )(dx2, gt, o, proj, proj, proj, proj, yc, ya, wout, wcp, wap)


def _conv_bwd(dgc, proj, convw, dproj, *, tm, name):
    T = dgc.shape[0]
    tm = _tile(T, tm)
    hb = tm // HALO
    nblk = T // HALO
    nt = T // tm

    def body(dgc_ref, ndgc_ref, bg_ref, nbg_ref, cg_ref, u_ref, hcg_ref, hu_ref, cw_ref, dproj_ref, dp_ref, dcw_ref):
        i = pl.program_id(0)

        @pl.when(i == 0)
        def _():
            dcw_ref[...] = jnp.zeros_like(dcw_ref)
        first = jnp.where(i == 0, 0.0, 1.0)
        last = jnp.where(i == nt - 1, 0.0, 1.0)
        cg = cg_ref[...].astype(F32)
        u = u_ref[...].astype(F32)
        bg = bg_ref[...].astype(F32)
        dg = dgc_ref[...].astype(F32)
        cu = cg * u
        hprev = first * (hcg_ref[...].astype(F32) * hu_ref[...].astype(F32))
        m1, m2 = _conv_shifts(cu, hprev, tm)
        w0, w1, w2 = cw_ref[0:1, :], cw_ref[1:2, :], cw_ref[2:3, :]
        cv = w0 * m2 + w1 * m1 + w2 * cu
        dcv = dg * bg
        nxt = last * (ndgc_ref[...].astype(F32) * nbg_ref[...].astype(F32))
        n0, n1 = nxt[0:1, :], nxt[1:2, :]
        row = lax.broadcasted_iota(jnp.int32, dcv.shape, 0)
        p1 = jnp.where(row == tm - 1, n0, pltpu.roll(dcv, tm - 1, 0))
        p2 = jnp.where(row == tm - 1, n1, jnp.where(row == tm - 2, n0, pltpu.roll(dcv, tm - 2, 0)))
        dcu = w2 * dcv + w1 * p1 + w0 * p2
        dp_ref[:, 0:D] = (dg * cv).astype(BF)
        dp_ref[:, D:2 * D] = (dcu * u).astype(BF)
        dp_ref[:, 2 * D:3 * D] = (dcu * cg).astype(BF)
        dcw_ref[0:1, :] += jnp.sum(dcv * m2, axis=0, keepdims=True)
        dcw_ref[1:2, :] += jnp.sum(dcv * m1, axis=0, keepdims=True)
        dcw_ref[2:3, :] += jnp.sum(dcv * cu, axis=0, keepdims=True)

    rowspec = pl.BlockSpec((tm, D), _row)
    cw = pl.BlockSpec((8, D), _const2)
    return pl.pallas_call(
        body, name=name, grid=(nt,),
        in_specs=[rowspec, _halo_next(hb, nblk), _col(tm, O_BG), _halo_next(hb, nblk, O_BG),
                  _col(tm, O_CG), _col(tm, O_U), _halo_prev(hb, O_CG), _halo_prev(hb, O_U), cw,
                  pl.BlockSpec(memory_space=pl.ANY)],
        out_specs=[pl.BlockSpec((tm, 3 * D), _row), cw],
        out_shape=[jax.ShapeDtypeStruct(dproj.shape, BF), jax.ShapeDtypeStruct((8, D), F32)],
        input_output_aliases={9: 0},
        compiler_params=_cp(("arbitrary",)),
    )(dgc, dgc, proj, proj, proj, proj, proj, proj, convw, dproj)


def _adam(w, g, m, v, *, tm, name):
    _, R, C = w.shape
    tm = _tile(R, tm)
    parts = g.ndim == 3
    c1 = 1.0 - ADAM_B1
    c2 = 1.0 - ADAM_B2
    bc1 = 1.0 - ADAM_B1 ** ADAM_STEP
    bc2 = 1.0 - ADAM_B2 ** ADAM_STEP

    def body(w_ref, g_ref, m_ref, v_ref, go_ref, d_ref, nm_ref, nv_ref):
        if parts:
            gv = g_ref[0].astype(F32)
            for s in range(1, N_DEV):
                gv = gv + g_ref[s].astype(F32)
        else:
            gv = g_ref[...]
        go_ref[0] = gv
        nm = ADAM_B1 * m_ref[0] + c1 * gv
        nv = ADAM_B2 * v_ref[0] + c2 * (gv * gv)
        nm_ref[0] = nm
        nv_ref[0] = nv
        d_ref[0] = -ADAM_LR * ((nm / bc1) / (jnp.sqrt(nv / bc2) + ADAM_EPS) + ADAM_WD * w_ref[0])

    spec = pl.BlockSpec((1, tm, C), lambda i: (0, i, 0))
    gspec = pl.BlockSpec((N_DEV, tm, C), lambda i: (0, i, 0)) if parts else pl.BlockSpec((tm, C), _row)
    return pl.pallas_call(
        body, name=name, grid=(R // tm,),
        in_specs=[spec, gspec, spec, spec], out_specs=[spec] * 4,
        out_shape=[jax.ShapeDtypeStruct((1, R, C), F32)] * 4,
        compiler_params=_cp(("parallel",)),
    )(w, g, m, v)


def _mods_part(c_all, w_ada, b_ada, *, name):
    C = w_ada.shape[1]

    def body(c_ref, w_ref, b_ref, o_ref):
        cv = c_ref[...]
        ca = cv * jax.nn.sigmoid(cv)
        o_ref[...] = jnp.dot(ca, w_ref[...], preferred_element_type=F32,
                             precision=lax.Precision.HIGHEST) + b_ref[...]

    return pl.pallas_call(
        body, name=name,
        out_shape=jax.ShapeDtypeStruct((N_DEV, C), F32),
        compiler_params=_cp(),
    )(c_all, w_ada, b_ada)


def _wada_grad(c_all_t, gm, *, name):
    C = gm.shape[1]

    def body(c_ref, g_ref, o_ref):
        cv = c_ref[...]
        ca = cv * jax.nn.sigmoid(cv)
        acc = ca[:, 0:1] * g_ref[0:1, :]
        for b in range(1, N_DEV):
            acc = acc + ca[:, b:b + 1] * g_ref[b:b + 1, :]
        o_ref[...] = acc

    return pl.pallas_call(
        body, name=name,
        out_shape=jax.ShapeDtypeStruct((D, C), F32),
        compiler_params=_cp(),
    )(c_all_t, gm)


def _peer(x, y, c, d):
    px = lax.rem(x + ((d >> 2) & 1), 2)
    py = lax.rem(y + ((d >> 1) & 1), 2)
    pc = lax.rem(c + (d & 1), 2)
    return (px, py, pc), 4 * px + 2 * py + pc


def _exchange(xs, *, scatter, name):
    n = len(xs)
    nsem = n * (N_DEV - 1)

    def body(*refs):
        ins, outs = refs[:n], refs[n:2 * n]
        token, send_sems, recv_sems, local_sems = refs[2 * n:]
        x, y, c = lax.axis_index("x"), lax.axis_index("y"), lax.axis_index("c")
        me = 4 * x + 2 * y + c
        token[...] = jnp.zeros_like(token)

        def src(t, idx):
            return ins[t].at[idx] if scatter else ins[t]

        local = [pltpu.make_async_copy(src(t, me), outs[t].at[me], local_sems.at[t]) for t in range(n)]
        for cp in local:
            cp.start()
        remote = []
        for t in range(n):
            for d in range(1, N_DEV):
                peer, pidx = _peer(x, y, c, d)
                k = t * (N_DEV - 1) + d - 1
                send = pltpu.make_async_remote_copy(src_ref=src(t, pidx), dst_ref=outs[t].at[me],
                                                    send_sem=send_sems.at[k], recv_sem=recv_sems.at[k],
                                                    device_id=peer, device_id_type=MESH)
                recv = pltpu.make_async_remote_copy(src_ref=src(t, pidx), dst_ref=outs[t].at[pidx],
                                                    send_sem=send_sems.at[k], recv_sem=recv_sems.at[k],
                                                    device_id=peer, device_id_type=MESH)
                send.start()
                remote.append((send, recv))
        for cp in local:
            cp.wait()
        for send, recv in remote:
            send.wait_send()
            recv.wait_recv()

    anyspec = pl.BlockSpec(memory_space=pl.ANY)
    out_shape = [jax.ShapeDtypeStruct(a.shape if scatter else (N_DEV,) + a.shape, a.dtype) for a in xs]
    out_shape.append(jax.ShapeDtypeStruct((8, 128), F32))
    return pl.pallas_call(
        body, name=name,
        in_specs=[anyspec] * n, out_specs=[anyspec] * n + [pl.BlockSpec(memory_space=pltpu.VMEM)],
        out_shape=out_shape,
        scratch_shapes=[pltpu.SemaphoreType.DMA((nsem,)), pltpu.SemaphoreType.DMA((nsem,)),
                        pltpu.SemaphoreType.DMA((n,))],
    )(*xs)


def _sum8(parts, *, name):
    _, R, C = parts.shape

    def body(p_ref, o_ref):
        acc = p_ref[0]
        for s in range(1, N_DEV):
            acc = acc + p_ref[s]
        o_ref[...] = acc

    return pl.pallas_call(body, name=name, out_shape=jax.ShapeDtypeStruct((R, C), F32),
                          compiler_params=_cp())(parts)


HBM_SPEC = pl.BlockSpec(memory_space=pltpu.HBM)
SEM_SPEC = pl.BlockSpec(memory_space=pltpu.SEMAPHORE)
N_PEER = N_DEV - 1


def _split_copies(src_refs, land_refs, send_sems, recv_sems, scatter):
    x, y, c = lax.axis_index("x"), lax.axis_index("y"), lax.axis_index("c")
    me = 4 * x + 2 * y + c
    pairs = []
    for j, (src, land) in enumerate(zip(src_refs, land_refs)):
        for d in range(1, N_DEV):
            peer, pidx = _peer(x, y, c, d)
            k = j * N_PEER + d - 1
            s = src.at[pidx] if scatter else src
            send = pltpu.make_async_remote_copy(src_ref=s, dst_ref=land.at[me], send_sem=send_sems.at[k],
                                                recv_sem=recv_sems.at[k], device_id=peer, device_id_type=MESH)
            recv = pltpu.make_async_remote_copy(src_ref=s, dst_ref=land.at[pidx], send_sem=send_sems.at[k],
                                                recv_sem=recv_sems.at[k], device_id=peer, device_id_type=MESH)
            pairs.append((send, recv))
    return pairs


def _own_slot(block, me):
    land = lax.empty((N_DEV,) + block.shape, block.dtype)
    return lax.dynamic_update_slice(land, block[None], (me, 0, 0))


def _split_start(srcs, lands, groups, *, scatter, name):
    n, ng = len(srcs), len(groups)

    def body(*refs):
        src_refs, land_refs = refs[:n], refs[n:2 * n]
        sems = refs[2 * n:2 * n + 2 * ng]
        token = refs[-1]
        for gi, g in enumerate(groups):
            pairs = _split_copies([src_refs[t] for t in g], [land_refs[t] for t in g], sems[2 * gi],
                                  sems[2 * gi + 1], scatter)
            for send, _ in pairs:
                send.start()
        token[...] = jnp.zeros_like(token)

    sem_shapes = []
    for g in groups:
        sem_shapes += [pltpu.SemaphoreType.DMA((len(g) * N_PEER,))] * 2
    thru = [pltpu.HBM(a.shape, a.dtype) for a in list(srcs) + list(lands)]
    outs = pl.pallas_call(
        body, name=name,
        out_shape=tuple(sem_shapes + thru + [jax.ShapeDtypeStruct((8, 128), F32)]),
        in_specs=[HBM_SPEC] * (2 * n),
        out_specs=tuple([SEM_SPEC] * (2 * ng) + [HBM_SPEC] * (2 * n) + [pl.BlockSpec(memory_space=pltpu.VMEM)]),
        input_output_aliases={i: 2 * ng + i for i in range(2 * n)},
        compiler_params=pltpu.CompilerParams(has_side_effects=pltpu.SideEffectType.DATAFLOW_SIDE_EFFECTING),
    )(*[pltpu.with_memory_space_constraint(a, pltpu.HBM) for a in list(srcs) + list(lands)])
    sems = [(outs[2 * gi], outs[2 * gi + 1]) for gi in range(ng)]
    return sems, outs[2 * ng:2 * ng + n], outs[2 * ng + n:2 * ng + 2 * n], outs[-1]


def _behind(v, token):
    if token is None:
        return v
    return v + token[0, 0].astype(v.dtype)


def _split_wait(srcs, lands, sems, after, *, scatter, name):
    m = len(srcs)

    def body(*refs):
        src_refs, land_refs = refs[:m], refs[m:2 * m]
        send_sems, recv_sems = refs[2 * m], refs[2 * m + 1]
        for send, recv in _split_copies(src_refs, land_refs, send_sems, recv_sems, scatter):
            send.wait_send()
            recv.wait_recv()

    outs = pl.pallas_call(
        body, name=name,
        out_shape=tuple(pltpu.HBM(a.shape, a.dtype) for a in list(srcs) + list(lands)),
        in_specs=[HBM_SPEC] * (2 * m) + [SEM_SPEC, SEM_SPEC, pl.BlockSpec(memory_space=pl.ANY)],
        out_specs=tuple([HBM_SPEC] * (2 * m)),
        input_output_aliases={i: i for i in range(2 * m)},
        compiler_params=pltpu.CompilerParams(has_side_effects=pltpu.SideEffectType.DATAFLOW_SIDE_EFFECTING),
    )(*srcs, *lands, sems[0], sems[1], after)
    return outs[m:]


TM_PROJ = 512
TN_PROJ = 512
TM_ROW = 512
TM_NN = 512
TK_TN = 512
TN_FFN = F // 2
TN_IN = NIN // 4


def _tn(a, b, name, tn):
    if a.ndim == 2:
        a = a[None]
    return _tn_matmul(a, b, tn=tn, tk=TK_TN, name=name)


def _local_step(x, tgt, mods, g1, gm, g2, gf, convw8, sinks, w_get, g_put):
    T = x.shape[0]
    sh1, sc1, gt1, sh2, sc2, gt2, sh3, sc3, gt3 = [mods[i:i + 1] for i in range(N_MOD)]
    cos, sin = _rope_tables(T)
    behind = _behind

    w = dict(w_get("gu1", mods))
    h1, ab1 = _norm_proj(x, g1, sc1, sh1, w["gu1"], tm=TM_PROJ, tn=TN_PROJ, name="ffn1_up")
    w.update(w_get("d1", ab1))
    x1, y1 = _ffn_down_fwd(ab1, w["d1"], x, gt1, tm=TM_ROW, name="ffn1_down")
    w.update(w_get("mix", x1))
    h2, proj = _norm_proj(x1, gm, sc2, sh2, w["win"], tm=TM_PROJ, tn=TN_PROJ, name="mix_in")
    qs, kr = _attn_prep(proj, cos, sin, name="attn_prep")
    bias = _attn_bias()
    attn, lse = _attn_fwd(qs, kr, proj, bias, sinks, name="attn_fwd")
    x2, gc, yc, ya, mg, o = _mixer_mid_fwd(proj, attn, w["cp"], w["ap"], w["out"], convw8, x1, gt2,
                                           tm=TM_ROW, name="mix_mid")
    w.update(w_get("ffn2", x2))
    h3, ab2 = _norm_proj(x2, g2, sc3, sh3, w["gu2"], tm=TM_PROJ, tn=TN_PROJ, name="ffn2_up")
    x3, y2 = _ffn_down_fwd(ab2, w["d2"], x2, gt3, tm=TM_ROW, name="ffn2_down")
    dx3, lsum, dgf = _final_fwd_bwd(x3, tgt, gf, tm=TM_ROW, name="final")

    dy2, dab2, dgt3 = _ffn_down_bwd(dx3, y2, gt3, ab2, w["d2"], tm=TM_ROW, tn=MXU_N, name="ffn2_down_bwd")
    g_d2 = _tn_matmul_swiglu(ab2, dy2, None, tn=TN_FFN, tk=TK_TN, name="ffn2_down_dw")
    dx2, dsh3, dsc3, dg2 = _nn_bwd_norm(dab2, w["gu2"], x2, g2, sc3, dx3, tm=TM_NN, name="ffn2_up_bwd")
    g_gu2 = _tn(dab2, h3, "ffn2_up_dw", TN_FFN)
    tok = g_put(dict(gu2=g_gu2, d2=g_d2))

    dout, dyc, dya, dgc, dat, dproj, dgt2 = _mixer_mid_bwd(dx2, behind(gt2, tok), o, proj, yc, ya, w["out"], w["cp"],
                                                           w["ap"], tm=TM_ROW, name="mix_mid_bwd")
    g_out = _tn(mg, dout, "mix_out_dw", D)
    g_cp = _tn(gc, dyc, "mix_cp_dw", D)
    g_ap = _tn(attn, dya, "mix_ap_dw", D)
    dproj, dkc, dkp, dvc, dvp, dsink = _attn_bwd(qs, kr, proj, bias, sinks, lse, attn, dat, cos, sin, dproj,
                                                 name="attn_bwd")
    dproj = _dkv_combine(dkc, dkp, dvc, dvp, dproj, name="attn_dkv")
    dproj, dcw = _conv_bwd(dgc, proj, convw8, dproj, tm=TM_ROW, name="conv_bwd")
    g_in = _tn(dproj, h2, "mix_in_dw", TN_IN)
    tok = g_put(dict(win=g_in, cp=g_cp, ap=g_ap, out=g_out))
    dx1, dsh2, dsc2, dgm = _nn_bwd_norm(dproj[None], w["win"], x1, gm, behind(sc2, tok), dx2, tm=TM_NN,
                                        name="mix_in_bwd")

    dy1, dab1, dgt1 = _ffn_down_bwd(dx1, y1, gt1, ab1, w["d1"], tm=TM_ROW, tn=MXU_N, name="ffn1_down_bwd")
    g_gu1 = _tn(dab1, h1, "ffn1_up_dw", TN_FFN)
    tok = g_put(dict(gu1=g_gu1))
    g_d1 = _tn_matmul_swiglu(ab1, dy1, tok, tn=TN_FFN, tk=TK_TN, name="ffn1_down_dw")
    tok = g_put(dict(d1=g_d1))
    dx0, dsh1, dsc1, dg1 = _nn_bwd_norm(dab1, w["gu1"], x, g1, behind(sc1, tok), dx1, tm=TM_NN,
                                        name="ffn1_up_bwd")

    small = dict(mods=jnp.concatenate([dsh1, dsc1, dgt1, dsh2, dsc2, dgt2, dsh3, dsc3, dgt3], axis=0),
                 g1=dg1, gm=dgm, g2=dg2, gf=dgf, convw=dcw[0:3], sinks=dsink[:, 0:N_HEADS])
    return lsum, dx0, small


BIG = ("gu1", "d1", "win", "cp", "ap", "out", "gu2", "d2")
TRANSPOSED = ("gu1", "win", "gu2")
SMALL_ROWS = 24
R_MODS, R_G1, R_GM, R_G2, R_GF, R_CONV, R_SINK = 0, 9, 10, 11, 12, 13, 16


def _pad_to(a, rows, cols):
    return jnp.pad(a, ((0, rows - a.shape[0]), (0, cols - a.shape[1])))


def _pack_small(b_ada, g1, gm, g2, gf, conv, sinks):
    rows = [b_ada.reshape(N_MOD, D), g1.reshape(1, D), gm.reshape(1, D), g2.reshape(1, D), gf.reshape(1, D),
            _pad_to(conv.reshape(3, -1), 3, D), _pad_to(sinks.reshape(1, N_HEADS), 1, D)]
    return _pad_to(jnp.concatenate(rows, axis=0), SMALL_ROWS, D)


def _unpack_small(p, conv_cols):
    return dict(b_ada=p[R_MODS:R_MODS + N_MOD].reshape(1, N_MOD * D), g_ffn1=p[R_G1:R_G1 + 1],
                g_mix=p[R_GM:R_GM + 1], g_ffn2=p[R_G2:R_G2 + 1], g_final=p[R_GF],
                conv_w=p[R_CONV:R_CONV + 3, 0:conv_cols][None], sinks=p[R_SINK:R_SINK + 1, 0:N_HEADS])


def kernel(x, c, w_ada, b_ada, g_ffn1, w1_gu, w1_down, g_mix, w_in, conv_w, w_conv_proj, w_attn_proj, sinks, w_out, g_ffn2, w2_gu, w2_down, g_final, loss_target, m_w_ada, m_b_ada, m_g_ffn1, m_w1_gu, m_w1_down, m_g_mix, m_w_in, m_conv_w, m_w_conv_proj, m_w_attn_proj, m_sinks, m_w_out, m_g_ffn2, m_w2_gu, m_w2_down, m_g_final, v_w_ada, v_b_ada, v_g_ffn1, v_w1_gu, v_w1_down, v_g_mix, v_w_in, v_conv_w, v_w_conv_proj, v_w_attn_proj, v_sinks, v_w_out, v_g_ffn2, v_w2_gu, v_w2_down, v_g_final):
    me = 4 * lax.axis_index("x") + 2 * lax.axis_index("y") + lax.axis_index("c")
    ada_cols = w_ada.shape[2]
    conv_cols = conv_w.shape[2]

    native = dict(gu1=w1_gu[0], d1=w1_down[0], win=w_in[0], cp=w_conv_proj[0], ap=w_attn_proj[0], out=w_out[0],
                  gu2=w2_gu[0], d2=w2_down[0])

    def shard(n, token):
        a = _behind(native[n], token)
        return (a.T if n in TRANSPOSED else a).astype(BF)

    c_all, conv_all, _ = _exchange([c, _pad_to(conv_w[0], 8, conv_cols)], scatter=False, name="gather_cond")
    c_all = c_all.reshape(N_DEV, D)
    conv_full = conv_all[:, 0:3, :].transpose(1, 0, 2).reshape(3, D)

    b_cols = lax.dynamic_slice(b_ada, (0, me * ada_cols), (1, ada_cols))
    mods_cols = _mods_part(c_all, w_ada[0], b_cols, name="ada_mods")
    mods_all, mods_token = _exchange([mods_cols], scatter=False, name="gather_mods")
    mods = lax.dynamic_index_in_dim(mods_all, me, axis=1, keepdims=False).reshape(N_MOD, D)

    groups = dict(gu1=("gu1",), d1=("d1",), mix=("win", "cp", "ap", "out"), ffn2=("gu2", "d2"))
    in_flight = {}
    first = [shard("gu1", mods_token)]
    sems, srcs, lands, token = _split_start(first, [_own_slot(s, me) for s in first], [[0]], scatter=False,
                                            name="gather_weights_start_gu1")
    in_flight["gu1"] = (sems[0], srcs, lands)
    rest = [n for n in BIG if n != "gu1"]
    shards = [shard(n, token) for n in rest]
    rest_groups = [[rest.index(n) for n in names] for g, names in groups.items() if g != "gu1"]
    sems, srcs, lands, rest_token = _split_start(shards, [_own_slot(s, me) for s in shards], rest_groups,
                                                 scatter=False, name="gather_weights_start_rest")
    for (g, names), gsems, idx in zip([kv for kv in groups.items() if kv[0] != "gu1"], sems, rest_groups):
        in_flight[g] = (gsems, [srcs[t] for t in idx], [lands[t] for t in idx])

    def w_get(group, after):
        if group == "gu1":
            after = rest_token
        gsems, gsrcs, glands = in_flight[group]
        landed = _split_wait(gsrcs, glands, gsems, after, scatter=False, name="gather_weights_wait_" + group)
        return {n: a.reshape(-1, D) for n, a in zip(groups[group], landed)}

    pending = []

    def g_put(gs):
        names = tuple(gs)
        srcs = [gs[n].reshape(N_DEV, -1, D) for n in names]
        lands = [_own_slot(lax.dynamic_index_in_dim(s, me, axis=0, keepdims=False), me) for s in srcs]
        sems, srcs, lands, tok = _split_start(srcs, lands, [list(range(len(names)))], scatter=True,
                                              name="scatter_grads_start_" + names[0])
        pending.append((names, sems[0], srcs, lands))
        return tok

    lsum, grad_x, small = _local_step(x[0], loss_target[0], mods, g_ffn1, g_mix, g_ffn2, g_final[None],
                                      _pad_to(conv_full, 8, D), sinks[0], w_get, g_put)
    loss = lax.psum((0.5 / D) * jnp.sum(lsum), ("x", "y", "c"))

    packed = _pack_small(small["mods"], small["g1"], small["gm"], small["g2"], small["gf"], small["convw"],
                         small["sinks"])
    packed_all, _ = _exchange([packed], scatter=False, name="gather_small")
    gsmall = _sum8(packed_all, name="sum_small")

    w_of = dict(ada=w_ada, gu1=w1_gu, d1=w1_down, win=w_in, cp=w_conv_proj, ap=w_attn_proj, out=w_out, gu2=w2_gu,
                d2=w2_down)
    m_of = dict(ada=m_w_ada, gu1=m_w1_gu, d1=m_w1_down, win=m_w_in, cp=m_w_conv_proj, ap=m_w_attn_proj, out=m_w_out,
                gu2=m_w2_gu, d2=m_w2_down)
    v_of = dict(ada=v_w_ada, gu1=v_w1_gu, d1=v_w1_down, win=v_w_in, cp=v_w_conv_proj, ap=v_w_attn_proj, out=v_w_out,
                gu2=v_w2_gu, d2=v_w2_down)
    upd = {}
    after = gsmall
    for names, sems, srcs, lands in pending:
        parts = _split_wait(srcs, lands, sems, after, scatter=True, name="scatter_grads_wait_" + names[0])
        for n, p in zip(names, parts):
            if n in TRANSPOSED:
                res = _adam(jnp.swapaxes(w_of[n], 1, 2), p, jnp.swapaxes(m_of[n], 1, 2), jnp.swapaxes(v_of[n], 1, 2),
                            tm=128, name="adam_" + n)
                upd[n] = [jnp.swapaxes(t, 1, 2) for t in res]
            else:
                upd[n] = _adam(w_of[n], p, m_of[n], v_of[n], tm=128, name="adam_" + n)
        after = upd[names[-1]][1]

    gm_cols = lax.dynamic_slice(packed_all[:, R_MODS:R_MODS + N_MOD, :].reshape(N_DEV, N_MOD * D),
                                (0, me * ada_cols), (N_DEV, ada_cols))
    upd["ada"] = _adam(w_ada, _wada_grad(c_all.T, gm_cols, name="ada_dw"), m_w_ada, v_w_ada, tm=128, name="adam_ada")
    conv_g = lax.dynamic_slice(gsmall[R_CONV:R_CONV + 3], (0, me * conv_cols), (3, conv_cols))
    gsmall_own = gsmall.at[R_CONV:R_CONV + 3].set(_pad_to(conv_g, 3, D))
    small_upd = _adam(_pack_small(b_ada, g_ffn1, g_mix, g_ffn2, g_final, conv_w, sinks)[None], gsmall_own,
                      _pack_small(m_b_ada, m_g_ffn1, m_g_mix, m_g_ffn2, m_g_final, m_conv_w, m_sinks)[None],
                      _pack_small(v_b_ada, v_g_ffn1, v_g_mix, v_g_ffn2, v_g_final, v_conv_w, v_sinks)[None],
                      tm=SMALL_ROWS, name="adam_small")
    small_out = [_unpack_small(p[0], conv_cols) for p in small_upd]

    big_name = dict(w_ada="ada", w1_gu="gu1", w1_down="d1", w_in="win", w_conv_proj="cp", w_attn_proj="ap",
                    w_out="out", w2_gu="gu2", w2_down="d2")
    order = ("w_ada", "b_ada", "g_ffn1", "w1_gu", "w1_down", "g_mix", "w_in", "conv_w", "w_conv_proj", "w_attn_proj",
             "sinks", "w_out", "g_ffn2", "w2_gu", "w2_down", "g_final")
    outs = [loss, grad_x[None]]
    for kind in range(4):
        for n in order:
            outs.append(upd[big_name[n]][kind] if n in big_name else small_out[kind][n])
    return tuple(outs)
```

```python
import functools

import jax
import jax.numpy as jnp
from jax import lax
from jax.experimental import pallas as pl
from jax.experimental.pallas import tpu as pltpu

D = 1024
F = 2816
NIN = 6656
N_HEADS = 16
N_KV = 4
HEAD_DIM = 64
BLK = 128
N_MOD = 9
N_DEV = 8
EPS = 1e-6
NEG_INF = -1e30
ROPE_THETA = 10000.0
O_BG, O_CG, O_U, O_Q, O_K, O_V, O_ZC, O_ZA = 0, 1024, 2048, 3072, 4096, 4352, 4608, 5632

ADAM_LR = 0.001
ADAM_B1 = 0.9
ADAM_B2 = 0.999
ADAM_EPS = 1e-08
ADAM_WD = 0.01
ADAM_STEP = 10

BF = jnp.bfloat16
F32 = jnp.float32
VMEM_LIMIT = 56 * 1024 * 1024
MXU_N = 256
MESH = pl.DeviceIdType.MESH

NT = (((1,), (1,)), ((), ()))
TN = (((0,), (0,)), ((), ()))


def _cp(sem=None):
    return pltpu.CompilerParams(dimension_semantics=sem, vmem_limit_bytes=VMEM_LIMIT)


def _tile(n, pref):
    if n <= pref:
        return n
    for t in range(pref - pref % 16, 15, -16):
        if n % t == 0:
            return t
    raise ValueError((n, pref))


def _sigmoid(v):
    return 0.5 * jnp.tanh(0.5 * v) + 0.5


def _row(i):
    return (i, 0)


def _const2(*_):
    return (0, 0)


def _resident(shape):
    return pl.BlockSpec(shape, lambda *_: (0,) * len(shape), pipeline_mode=pl.Buffered(1))


def _norm_proj(x, g, sc, sh, wt, *, tm, tn, name):
    T, N = x.shape[0], wt.shape[0]
    tm = _tile(T, tm)

    def body(x_ref, g_ref, sc_ref, sh_ref, w_ref, h_ref, o_ref):
        xv = x_ref[...]
        r = lax.rsqrt(jnp.mean(xv * xv, axis=-1, keepdims=True) + EPS)
        hb = ((xv * r) * g_ref[...] * (1.0 + sc_ref[...]) + sh_ref[...]).astype(BF)
        h_ref[...] = hb
        for c0 in range(0, N, tn):
            cols = pl.ds(c0, tn)
            o_ref[:, cols] = lax.dot_general(hb, w_ref[cols, :], NT, preferred_element_type=F32).astype(BF)

    vec = pl.BlockSpec((1, D), _const2)
    return pl.pallas_call(
        body, name=name, grid=(T // tm,),
        in_specs=[pl.BlockSpec((tm, D), _row), vec, vec, vec, _resident((N, D))],
        out_specs=[pl.BlockSpec((tm, D), _row), pl.BlockSpec((tm, N), _row)],
        out_shape=[jax.ShapeDtypeStruct((T, D), BF), jax.ShapeDtypeStruct((T, N), BF)],
        compiler_params=_cp(("parallel",)),
    )(x, g, sc, sh, wt)


def _ffn_down_fwd(ab, wd, x, gt, *, tm, name):
    T = x.shape[0]
    tm = _tile(T, tm)

    def body(a_ref, b_ref, wd_ref, x_ref, gt_ref, xo_ref, y_ref):
        y = None
        for c0 in range(0, F, MXU_N):
            cols = pl.ds(c0, MXU_N)
            a = a_ref[:, cols].astype(F32)
            act = (a * _sigmoid(a) * b_ref[:, cols].astype(F32)).astype(BF)
            part = jnp.dot(act, wd_ref[cols, :], preferred_element_type=F32)
            y = part if y is None else y + part
        y_ref[...] = y.astype(BF)
        xo_ref[...] = x_ref[...] + (0.5 * gt_ref[...]) * y

    return pl.pallas_call(
        body, name=name, grid=(T // tm,),
        in_specs=[pl.BlockSpec((tm, F), lambda i: (i, 0)), pl.BlockSpec((tm, F), lambda i: (i, 1)),
                  pl.BlockSpec((F, D), _const2), pl.BlockSpec((tm, D), _row), pl.BlockSpec((1, D), _const2)],
        out_specs=[pl.BlockSpec((tm, D), _row), pl.BlockSpec((tm, D), _row)],
        out_shape=[jax.ShapeDtypeStruct((T, D), F32), jax.ShapeDtypeStruct((T, D), BF)],
        compiler_params=_cp(("parallel",)),
    )(ab, ab, wd, x, gt)


def _final_fwd_bwd(x, tgt, g, *, tm, name):
    T = x.shape[0]
    tm = _tile(T, tm)

    def body(x_ref, t_ref, g_ref, dx_ref, ls_ref, dg_ref):
        @pl.when(pl.program_id(0) == 0)
        def _():
            ls_ref[...] = jnp.zeros_like(ls_ref)
            dg_ref[...] = jnp.zeros_like(dg_ref)
        xv = x_ref[...]
        gv = g_ref[...]
        r = lax.rsqrt(jnp.mean(xv * xv, axis=-1, keepdims=True) + EPS)
        xh = xv * r
        e = xh * gv - t_ref[...]
        ls_ref[...] += jnp.sum(e * e, axis=0, keepdims=True)
        dy = e * (1.0 / D)
        dg_ref[...] += jnp.sum(dy * xh, axis=0, keepdims=True)
        dxh = dy * gv
        dx_ref[...] = r * (dxh - xh * jnp.mean(dxh * xh, axis=-1, keepdims=True))

    vec = pl.BlockSpec((1, D), _const2)
    return pl.pallas_call(
        body, name=name, grid=(T // tm,),
        in_specs=[pl.BlockSpec((tm, D), _row), pl.BlockSpec((tm, D), _row), vec],
        out_specs=[pl.BlockSpec((tm, D), _row), vec, vec],
        out_shape=[jax.ShapeDtypeStruct((T, D), F32), jax.ShapeDtypeStruct((1, D), F32),
                   jax.ShapeDtypeStruct((1, D), F32)],
        compiler_params=_cp(("arbitrary",)),
    )(x, tgt, g)


def _ffn_down_bwd(dxo, y, gt, ab, wd, *, tm, tn, name):
    T = dxo.shape[0]
    tm = _tile(T, tm)

    def body(dxo_ref, y_ref, gt_ref, a_ref, b_ref, wd_ref, dy_ref, dab_ref, dgt_ref):
        @pl.when(pl.program_id(0) == 0)
        def _():
            dgt_ref[...] = jnp.zeros_like(dgt_ref)

        dxv = dxo_ref[...]
        dgt_ref[...] += 0.5 * jnp.sum(dxv * y_ref[...].astype(F32), axis=0, keepdims=True)
        dy = ((0.5 * gt_ref[...]) * dxv).astype(BF)
        dy_ref[...] = dy
        for c0 in range(0, F, tn):
            cols = pl.ds(c0, tn)
            dact = lax.dot_general(dy, wd_ref[cols, :], NT, preferred_element_type=F32)
            a = a_ref[:, cols].astype(F32)
            b = b_ref[:, cols].astype(F32)
            s = _sigmoid(a)
            dab_ref[0, :, cols] = (dact * b * (s * (1.0 + a * (1.0 - s)))).astype(BF)
            dab_ref[1, :, cols] = (dact * (a * s)).astype(BF)

    vec = pl.BlockSpec((1, D), _const2)
    rowspec = pl.BlockSpec((tm, D), _row)
    return pl.pallas_call(
        body, name=name, grid=(T // tm,),
        in_specs=[rowspec, rowspec, vec, pl.BlockSpec((tm, F), lambda i: (i, 0)),
                  pl.BlockSpec((tm, F), lambda i: (i, 1)), pl.BlockSpec((F, D), _const2)],
        out_specs=[rowspec, pl.BlockSpec((2, tm, F), lambda i: (0, i, 0)), vec],
        out_shape=[jax.ShapeDtypeStruct((T, D), BF), jax.ShapeDtypeStruct((2, T, F), BF),
                   jax.ShapeDtypeStruct((1, D), F32)],
        compiler_params=_cp(("arbitrary",)),
    )(dxo, y, gt, ab, ab, wd)


def _tn_matmul(a, b, *, tn, tk, name):
    S, T, Ns = a.shape
    tn, tk = _tile(Ns, tn), _tile(T, tk)
    nk, njs = T // tk, Ns // tn

    def body(a_ref, b_ref, o_ref, acc):
        k = pl.program_id(1)

        @pl.when(k == 0)
        def _():
            acc[...] = jnp.zeros_like(acc)
        acc[...] += lax.dot_general(a_ref[0], b_ref[...], TN, preferred_element_type=F32)

        @pl.when(k == nk - 1)
        def _():
            o_ref[...] = acc[...].astype(BF)

    return pl.pallas_call(
        body, name=name, grid=(S * njs, nk),
        in_specs=[pl.BlockSpec((1, tk, tn), lambda j, k: (j // njs, k, j % njs)),
                  pl.BlockSpec((tk, D), lambda j, k: (k, 0))],
        out_specs=pl.BlockSpec((tn, D), lambda j, k: (j, 0)),
        out_shape=jax.ShapeDtypeStruct((S * Ns, D), BF),
        scratch_shapes=[pltpu.VMEM((tn, D), F32)],
        compiler_params=_cp(("parallel", "arbitrary")),
    )(a, b)


def _tn_matmul_swiglu(ab, b, token, *, tn, tk, name):
    T = ab.shape[0]
    tn, tk = _tile(F, tn), _tile(T, tk)
    nk, nj = T // tk, F // tn
    deps = [] if token is None else [token]

    def body(a_ref, g_ref, b_ref, *rest):
        o_ref, acc = rest[len(deps):]
        k = pl.program_id(1)

        @pl.when(k == 0)
        def _():
            acc[...] = jnp.zeros_like(acc)
        bv = b_ref[...]
        for c0 in range(0, tn, MXU_N):
            cw = min(MXU_N, tn - c0)
            cols = pl.ds(c0, cw)
            a = a_ref[:, cols].astype(F32)
            act = (a * _sigmoid(a) * g_ref[:, cols].astype(F32)).astype(BF)
            acc[cols, :] += lax.dot_general(act, bv, TN, preferred_element_type=F32)

        @pl.when(k == nk - 1)
        def _():
            o_ref[...] = acc[...].astype(BF)

    return pl.pallas_call(
        body, name=name, grid=(nj, nk),
        in_specs=[pl.BlockSpec((tk, tn), lambda j, k: (k, j)), pl.BlockSpec((tk, tn), lambda j, k: (k, j + nj)),
                  pl.BlockSpec((tk, D), lambda j, k: (k, 0))] + [pl.BlockSpec(memory_space=pl.ANY)] * len(deps),
        out_specs=pl.BlockSpec((tn, D), lambda j, k: (j, 0)),
        out_shape=jax.ShapeDtypeStruct((F, D), BF),
        scratch_shapes=[pltpu.VMEM((tn, D), F32)],
        compiler_params=_cp(("parallel", "arbitrary")),
    )(ab, ab, b, *deps)


def _nn_bwd_norm(da, w, x, g, sc, dxo, *, tm, name):
    S, T, Ks = da.shape
    tm = _tile(T, tm)
    rc = _tile(tm, 256)

    def body(da_ref, w_ref, x_ref, g_ref, sc_ref, dxo_ref, dx_ref, dsh_ref, dsc_ref, dg_ref, acc):
        @pl.when(pl.program_id(0) == 0)
        def _():
            dsh_ref[...] = jnp.zeros_like(dsh_ref)
            dsc_ref[...] = jnp.zeros_like(dsc_ref)
            dg_ref[...] = jnp.zeros_like(dg_ref)

        d = jnp.dot(da_ref[0], w_ref[0:Ks, :], preferred_element_type=F32)
        for s in range(1, S):
            d = d + jnp.dot(da_ref[s], w_ref[s * Ks:(s + 1) * Ks, :], preferred_element_type=F32)
        acc[...] = d
        gv = g_ref[...]
        sc1 = 1.0 + sc_ref[...]
        dsh = jnp.zeros((1, D), F32)
        dsc = jnp.zeros((1, D), F32)
        dg = jnp.zeros((1, D), F32)
        for r0 in range(0, tm, rc):
            rows = pl.ds(r0, rc)
            u = acc[rows, :]
            xv = x_ref[rows, :]
            r = lax.rsqrt(jnp.mean(xv * xv, axis=-1, keepdims=True) + EPS)
            xh = xv * r
            dsh = dsh + jnp.sum(u, axis=0, keepdims=True)
            dsc = dsc + jnp.sum(u * (xh * gv), axis=0, keepdims=True)
            us = u * sc1
            dg = dg + jnp.sum(us * xh, axis=0, keepdims=True)
            dxh = us * gv
            dx_ref[rows, :] = dxo_ref[rows, :] + r * (dxh - xh * jnp.mean(dxh * xh, axis=-1, keepdims=True))
        dsh_ref[...] += dsh
        dsc_ref[...] += dsc
        dg_ref[...] += dg

    vec = pl.BlockSpec((1, D), _const2)
    rowspec = pl.BlockSpec((tm, D), _row)
    return pl.pallas_call(
        body, name=name, grid=(T // tm,),
        in_specs=[pl.BlockSpec((S, tm, Ks), lambda i: (0, i, 0)), _resident((S * Ks, D)), rowspec, vec, vec, rowspec],
        out_specs=[rowspec, vec, vec, vec],
        out_shape=[jax.ShapeDtypeStruct((T, D), F32)] + [jax.ShapeDtypeStruct((1, D), F32)] * 3,
        scratch_shapes=[pltpu.VMEM((tm, D), F32)],
        compiler_params=_cp(("arbitrary",)),
    )(da, w, x, g, sc, dxo)


def _rope(t, cos, sin_signed, lt32, inverse=False):
    sel = jnp.where(lt32, pltpu.roll(t, 96, 1), pltpu.roll(t, 32, 1))
    return t * cos - sel * sin_signed if inverse else t * cos + sel * sin_signed


def _rope_tables(T):
    inv = 1.0 / (ROPE_THETA ** (jnp.arange(0, HEAD_DIM, 2, dtype=F32) / HEAD_DIM))
    ang = jnp.arange(T, dtype=F32)[:, None] * inv[None, :]
    cos, sin = jnp.cos(ang), jnp.sin(ang)
    cos128 = jnp.tile(cos, (1, 4))
    sin128 = jnp.tile(jnp.concatenate([-sin, sin], axis=1), (1, 2))
    return cos128, sin128


QSCALE = HEAD_DIM ** -0.5


def _lane_masks(rows):
    lane = lax.broadcasted_iota(jnp.int32, (rows, 128), 1)
    return (lane % HEAD_DIM) < (HEAD_DIM // 2), [lane < HEAD_DIM, lane >= HEAD_DIM]


def _attn_bias():
    qi = lax.broadcasted_iota(jnp.int32, (4 * BLK, 2 * BLK), 0) % BLK
    kj = lax.broadcasted_iota(jnp.int32, (4 * BLK, 2 * BLK), 1)
    band = (kj > qi) & (kj <= qi + BLK)
    return jnp.stack([jnp.where(band & (kj >= BLK), 0.0, NEG_INF), jnp.where(band, 0.0, NEG_INF)]).astype(F32)


def _attn_prep(proj, cos, sin, *, name):
    T = proj.shape[0]
    tm = _tile(T, 4 * BLK)

    def body(q_ref, k_ref, c_ref, s_ref, qs_ref, kr_ref):
        lt32, halves = _lane_masks(BLK)
        for b in range(tm // BLK):
            rows = pl.ds(b * BLK, BLK)
            cc, sc = c_ref[rows, :], s_ref[rows, :]
            qr = [_rope(q_ref[rows, p * 128:(p + 1) * 128].astype(F32), cc, sc, lt32) * QSCALE for p in range(8)]
            for g in range(N_KV):
                qs_ref[g, pl.ds(4 * b * BLK, 4 * BLK), :] = _stack_heads(qr, g, halves).astype(BF)
            kr_ref[rows, :] = jnp.concatenate([_rope(k_ref[rows, r * 128:(r + 1) * 128].astype(F32), cc, sc, lt32)
                                               for r in range(2)], axis=1).astype(BF)

    tab = pl.BlockSpec((tm, 128), _row)
    return pl.pallas_call(
        body, name=name, grid=(T // tm,),
        in_specs=[pl.BlockSpec((tm, D), lambda n: (n, O_Q // D)), pl.BlockSpec((tm, 256), lambda n: (n, O_K // 256)),
                  tab, tab],
        out_specs=[pl.BlockSpec((N_KV, 4 * tm, 128), lambda n: (0, n, 0)), pl.BlockSpec((tm, 256), _row)],
        out_shape=[jax.ShapeDtypeStruct((N_KV, 4 * T, 128), BF), jax.ShapeDtypeStruct((T, 256), BF)],
        compiler_params=_cp(("parallel",)),
    )(proj, proj, cos, sin)


def _attn_specs():
    prev = lambda n: jnp.maximum(n - 1, 0)
    return [pl.BlockSpec((N_KV, 4 * BLK, 128), lambda n: (0, n, 0)),
            pl.BlockSpec((BLK, 256), _row), pl.BlockSpec((BLK, 256), lambda n: (prev(n), 0)),
            pl.BlockSpec((BLK, 256), lambda n: (n, O_V // 256)),
            pl.BlockSpec((BLK, 256), lambda n: (prev(n), O_V // 256)),
            pl.BlockSpec((1, 4 * BLK, 2 * BLK), lambda n: (jnp.minimum(n, 1), 0, 0)),
            pl.BlockSpec(memory_space=pltpu.SMEM)]


def _bands(kc_ref, kp_ref, vc_ref, vp_ref):
    kb, vb = [], []
    for r in range(2):
        cols = slice(r * 128, (r + 1) * 128)
        kb.append(jnp.concatenate([kp_ref[:, cols], kc_ref[:, cols]], axis=0))
        vb.append(jnp.concatenate([vp_ref[:, cols], vc_ref[:, cols]], axis=0))
    return kb, vb


def _sink_col(sink_ref, g):
    return jnp.concatenate([jnp.full((BLK, 1), sink_ref[4 * g + hh], F32) for hh in range(4)], axis=0)


def _unstack_heads(t, g, halves, acc):
    half = g % 2
    for hh in range(4):
        h = 4 * g + hh
        th = jnp.where(halves[half], t[hh * BLK:(hh + 1) * BLK], 0.0)
        if h % 2 != half:
            th = pltpu.roll(th, HEAD_DIM, 1)
        acc[h // 2] = acc[h // 2] + th


def _stack_heads(chunks, g, halves):
    half = g % 2
    parts = []
    for hh in range(4):
        h = 4 * g + hh
        t = chunks[h // 2]
        if h % 2 != half:
            t = pltpu.roll(t, HEAD_DIM, 1)
        parts.append(jnp.where(halves[half], t, 0.0))
    return jnp.concatenate(parts, axis=0)


def _attn_fwd(qs, kr, proj, bias, sinks, *, name):
    T = proj.shape[0]
    nb = T // BLK

    def body(qs_ref, kc_ref, kp_ref, vc_ref, vp_ref, bias_ref, sink_ref, o_ref, lse_ref):
        _, h128 = _lane_masks(BLK)
        _, h256 = _lane_masks(2 * BLK)
        _, h512 = _lane_masks(4 * BLK)
        kb, vb = _bands(kc_ref, kp_ref, vc_ref, vp_ref)
        outs = [jnp.zeros((BLK, 128), F32) for _ in range(8)]
        groups = range(N_KV)
        bias = bias_ref[0]
        sink = [_sink_col(sink_ref, g) for g in groups]
        s = [lax.dot_general(qs_ref[g], kb[g // 2], NT, preferred_element_type=F32) + bias for g in groups]
        m = [jnp.maximum(jnp.max(s[g], axis=-1, keepdims=True), sink[g]) for g in groups]
        p = [jnp.exp(s[g] - m[g]).astype(BF) for g in groups]
        vg = [jnp.where(h256[g % 2], vb[g // 2].astype(F32), 1.0).astype(BF) for g in groups]
        o = [jnp.dot(p[g], vg[g], preferred_element_type=F32) for g in groups]
        denom = [jnp.where(h512[g % 2], pltpu.roll(o[g], HEAD_DIM, 1), o[g]) + jnp.exp(sink[g] - m[g]) for g in groups]
        for g in groups:
            lse_ref[g] = m[g] + jnp.log(denom[g])
            _unstack_heads(o[g] * (1.0 / denom[g]), g, h128, outs)
        o_ref[...] = jnp.concatenate(outs, axis=1).astype(BF)

    return pl.pallas_call(
        body, name=name, grid=(nb,),
        in_specs=_attn_specs(),
        out_specs=[pl.BlockSpec((BLK, D), _row), pl.BlockSpec((N_KV, 4 * BLK, 128), lambda n: (0, n, 0))],
        out_shape=[jax.ShapeDtypeStruct((T, D), BF), jax.ShapeDtypeStruct((N_KV, 4 * T, 128), F32)],
        compiler_params=_cp(("parallel",)),
    )(qs, kr, kr, proj, proj, bias, sinks)


def _attn_bwd(qs, kr, proj, bias, sinks, lse, o, do, cos, sin, dproj, *, name):
    T = proj.shape[0]
    nb = T // BLK

    def body(qs_ref, kc_ref, kp_ref, vc_ref, vp_ref, bias_ref, sink_ref, lse_ref, o_ref, do_ref,
             cc_ref, sc_ref, cp_ref, sp_ref, dproj_ref, dq_ref, dkc_ref, dkp_ref, dvc_ref, dvp_ref, dsink_ref):
        @pl.when(pl.program_id(0) == 0)
        def _():
            dsink_ref[...] = jnp.zeros_like(dsink_ref)
        lt32, h128 = _lane_masks(BLK)
        kb, vb = _bands(kc_ref, kp_ref, vc_ref, vp_ref)
        oc = [o_ref[:, p * 128:(p + 1) * 128].astype(F32) for p in range(8)]
        doc = [do_ref[:, p * 128:(p + 1) * 128].astype(F32) for p in range(8)]
        dqs = [jnp.zeros((BLK, 128), F32) for _ in range(8)]
        lane1 = lax.broadcasted_iota(jnp.int32, (1, 128), 1)
        dsink = jnp.zeros((1, 128), F32)
        groups = range(N_KV)
        bias = bias_ref[0]
        q = [qs_ref[g] for g in groups]
        lse_g = [lse_ref[g] for g in groups]
        s = [lax.dot_general(q[g], kb[g // 2], NT, preferred_element_type=F32) + bias for g in groups]
        dos = [_stack_heads(doc, g, h128) for g in groups]
        dosb = [t.astype(BF) for t in dos]
        dp = [lax.dot_general(dosb[g], vb[g // 2], NT, preferred_element_type=F32) for g in groups]
        delta = [jnp.sum(dos[g] * _stack_heads(oc, g, h128), axis=-1, keepdims=True) for g in groups]
        p = [jnp.exp(s[g] - jnp.concatenate([lse_g[g], lse_g[g]], axis=1)) for g in groups]
        ds = [(p[g] * (dp[g] - delta[g])).astype(BF) for g in groups]
        pb = [t.astype(BF) for t in p]
        dvg = [lax.dot_general(pb[g], dosb[g], TN, preferred_element_type=F32) for g in groups]
        dkg = [lax.dot_general(ds[g], q[g], TN, preferred_element_type=F32) for g in groups]
        dqg = [jnp.dot(ds[g], kb[g // 2], preferred_element_type=F32) * QSCALE for g in groups]
        dvr = [dvg[0] + dvg[1], dvg[2] + dvg[3]]
        dkr = [dkg[0] + dkg[1], dkg[2] + dkg[3]]
        for g in groups:
            _unstack_heads(dqg[g], g, h128, dqs)
            dsk = -jnp.exp(_sink_col(sink_ref, g) - lse_g[g][:, 0:1]) * delta[g]
            for hh in range(4):
                val = jnp.sum(dsk[hh * BLK:(hh + 1) * BLK], axis=0, keepdims=True)
                dsink = dsink + jnp.where(lane1 == 4 * g + hh, val, 0.0)
        cc, sc, cp, sp = cc_ref[...], sc_ref[...], cp_ref[...], sp_ref[...]
        dsink_ref[...] += dsink
        dq_ref[...] = jnp.concatenate([_rope(t, cc, sc, lt32, inverse=True) for t in dqs], axis=1).astype(BF)
        dkp_ref[...] = jnp.concatenate([_rope(t[:BLK], cp, sp, lt32, inverse=True) for t in dkr], axis=1)
        dkc_ref[...] = jnp.concatenate([_rope(t[BLK:], cc, sc, lt32, inverse=True) for t in dkr], axis=1)
        dvp_ref[...] = jnp.concatenate([t[:BLK] for t in dvr], axis=1)
        dvc_ref[...] = jnp.concatenate([t[BLK:] for t in dvr], axis=1)

    kv = pl.BlockSpec((BLK, 256), _row)
    tc = pl.BlockSpec((BLK, 128), _row)
    tp = pl.BlockSpec((BLK, 128), lambda n: (jnp.maximum(n - 1, 0), 0))
    return pl.pallas_call(
        body, name=name, grid=(nb,),
        in_specs=_attn_specs() + [pl.BlockSpec((N_KV, 4 * BLK, 128), lambda n: (0, n, 0)),
                                  pl.BlockSpec((BLK, D), _row), pl.BlockSpec((BLK, D), _row), tc, tc, tp, tp,
                                  pl.BlockSpec(memory_space=pl.ANY)],
        out_specs=[pl.BlockSpec((BLK, D), lambda n: (n, O_Q // D)), kv, kv, kv, kv, pl.BlockSpec((1, 128), _const2)],
        out_shape=[jax.ShapeDtypeStruct(dproj.shape, BF)] + [jax.ShapeDtypeStruct((T, 256), F32)] * 4
        + [jax.ShapeDtypeStruct((1, 128), F32)],
        input_output_aliases={14: 0},
        compiler_params=_cp(("arbitrary",)),
    )(qs, kr, kr, proj, proj, bias, sinks, lse, o, do, cos, sin, cos, sin, dproj)


def _dkv_combine(dkc, dkp, dvc, dvp, dproj, *, name):
    T = dkc.shape[0]
    nb = T // BLK
    tm = _tile(T, 4 * BLK)
    bpt = tm // BLK
    nt = T // tm

    def body(dkc_ref, dkp_ref, dkn_ref, dvc_ref, dvp_ref, dvn_ref, dproj_ref, o_ref):
        keep = jnp.where(pl.program_id(0) == nt - 1, 0.0, 1.0)

        def shifted(prev_ref, next_ref):
            nxt = keep * next_ref[...]
            return nxt if bpt == 1 else jnp.concatenate([prev_ref[BLK:, :], nxt], axis=0)

        o_ref[:, 0:256] = (dkc_ref[...] + shifted(dkp_ref, dkn_ref)).astype(BF)
        o_ref[:, 256:512] = (dvc_ref[...] + shifted(dvp_ref, dvn_ref)).astype(BF)

    cur = pl.BlockSpec((tm, 256), _row)
    nxt = pl.BlockSpec((BLK, 256), lambda i: (jnp.minimum((i + 1) * bpt, nb - 1), 0))
    return pl.pallas_call(
        body, name=name, grid=(nt,),
        in_specs=[cur, cur, nxt, cur, cur, nxt, pl.BlockSpec(memory_space=pl.ANY)],
        out_specs=pl.BlockSpec((tm, 512), lambda i: (i, O_K // 512)),
        out_shape=jax.ShapeDtypeStruct(dproj.shape, BF),
        input_output_aliases={6: 0},
        compiler_params=_cp(("parallel",)),
    )(dkc, dkp, dkp, dvc, dvp, dvp, dproj)


HALO = 16


def _conv_shifts(cu, hprev, tm):
    row = lax.broadcasted_iota(jnp.int32, cu.shape, 0)
    h1 = hprev[HALO - 1:HALO, :]
    h2 = hprev[HALO - 2:HALO - 1, :]
    m1 = jnp.where(row == 0, h1, pltpu.roll(cu, 1, 0))
    m2 = jnp.where(row == 0, h2, jnp.where(row == 1, h1, pltpu.roll(cu, 2, 0)))
    return m1, m2


def _mixer_mid_fwd(proj, attn, wcp, wap, wout, convw, x, gt, *, tm, name):
    T = x.shape[0]
    tm = _tile(T, tm)
    hb = tm // HALO

    def body(bg_ref, cg_ref, u_ref, hcg_ref, hu_ref, zc0_ref, zc1_ref, za0_ref, za1_ref, at_ref,
             wcp_ref, wap_ref, wout_ref, cw_ref, x_ref, gt_ref,
             x2_ref, gc_ref, yc_ref, ya_ref, mg_ref, o_ref):
        first = jnp.where(pl.program_id(0) == 0, 0.0, 1.0)
        cu = cg_ref[...].astype(F32) * u_ref[...].astype(F32)
        hprev = first * (hcg_ref[...].astype(F32) * hu_ref[...].astype(F32))
        m1, m2 = _conv_shifts(cu, hprev, tm)
        cv = cw_ref[0:1, :] * m2 + cw_ref[1:2, :] * m1 + cw_ref[2:3, :] * cu
        gc = (bg_ref[...].astype(F32) * cv).astype(BF)
        gc_ref[...] = gc
        yc = jnp.dot(gc, wcp_ref[...], preferred_element_type=F32)
        ya = jnp.dot(at_ref[...], wap_ref[...], preferred_element_type=F32)
        yc_ref[...] = yc.astype(BF)
        ya_ref[...] = ya.astype(BF)
        zc = jnp.concatenate([zc0_ref[...], zc1_ref[...]], axis=1).astype(F32)
        za = jnp.concatenate([za0_ref[...], za1_ref[...]], axis=1).astype(F32)
        mg = (_sigmoid(zc) * yc + _sigmoid(za) * ya).astype(BF)
        mg_ref[...] = mg
        o = jnp.dot(mg, wout_ref[...], preferred_element_type=F32)
        o_ref[...] = o.astype(BF)
        x2_ref[...] = x_ref[...] + gt_ref[...] * o

    wspec = pl.BlockSpec((D, D), _const2)
    rowspec = pl.BlockSpec((tm, D), _row)
    return pl.pallas_call(
        body, name=name, grid=(T // tm,),
        in_specs=[_col(tm, O_BG), _col(tm, O_CG), _col(tm, O_U), _halo_prev(hb, O_CG), _halo_prev(hb, O_U),
                  _col(tm, O_ZC, 512), _col(tm, O_ZC + 512, 512), _col(tm, O_ZA, 512), _col(tm, O_ZA + 512, 512),
                  rowspec, wspec, wspec, wspec, pl.BlockSpec((8, D), _const2), rowspec, pl.BlockSpec((1, D), _const2)],
        out_specs=[rowspec] * 6,
        out_shape=[jax.ShapeDtypeStruct((T, D), F32)] + [jax.ShapeDtypeStruct((T, D), BF)] * 5,
        compiler_params=_cp(("parallel",)),
    )(proj, proj, proj, proj, proj, proj, proj, proj, proj, attn, wcp, wap, wout, convw, x, gt)


def _col(tm, c, w=D):
    assert c % w == 0
    return pl.BlockSpec((tm, w), lambda i: (i, c // w))


def _halo_prev(hb, c):
    return pl.BlockSpec((HALO, D), lambda i: (jnp.maximum(i * hb - 1, 0), c // D))


def _halo_next(hb, nblk, c=0):
    return pl.BlockSpec((HALO, D), lambda i: (jnp.minimum((i + 1) * hb, nblk - 1), c // D))


def _mixer_mid_bwd(dx2, gt, o, proj, yc, ya, wout, wcp, wap, *, tm, name):
    T = dx2.shape[0]
    tm = _tile(T, tm)
    zw = 512
    nz = 2 * D // zw

    def body(dx_ref, gt_ref, o_ref, zc0_ref, zc1_ref, za0_ref, za1_ref, yc_ref, ya_ref, wout_ref, wcp_ref, wap_ref,
             dout_ref, dyc_ref, dya_ref, dgc_ref, dat_ref, dz_ref, dgt_ref, dzs):
        i, j = pl.program_id(0), pl.program_id(1)

        @pl.when(jnp.logical_and(i == 0, j == 0))
        def _():
            dgt_ref[...] = jnp.zeros_like(dgt_ref)

        @pl.when(j == 0)
        def _():
            dxv = dx_ref[...]
            dgt_ref[...] += jnp.sum(dxv * o_ref[...].astype(F32), axis=0, keepdims=True)
            dout = (gt_ref[...] * dxv).astype(BF)
            dout_ref[...] = dout
            dmg = lax.dot_general(dout, wout_ref[...], NT, preferred_element_type=F32)
            sc = _sigmoid(jnp.concatenate([zc0_ref[...], zc1_ref[...]], axis=1).astype(F32))
            sa = _sigmoid(jnp.concatenate([za0_ref[...], za1_ref[...]], axis=1).astype(F32))
            dyc = (dmg * sc).astype(BF)
            dya = (dmg * sa).astype(BF)
            dyc_ref[...] = dyc
            dya_ref[...] = dya
            dzs[:, 0:D] = (dmg * yc_ref[...].astype(F32) * (sc * (1.0 - sc))).astype(BF)
            dzs[:, D:2 * D] = (dmg * ya_ref[...].astype(F32) * (sa * (1.0 - sa))).astype(BF)
            dgc_ref[...] = lax.dot_general(dyc, wcp_ref[...], NT, preferred_element_type=F32).astype(BF)
            dat_ref[...] = lax.dot_general(dya, wap_ref[...], NT, preferred_element_type=F32).astype(BF)

        for jj in range(nz):
            @pl.when(j == jj)
            def _(jj=jj):
                dz_ref[...] = dzs[:, jj * zw:(jj + 1) * zw]

    nt = T // tm

    def ahead(i, j):
        return jnp.minimum(i + jnp.minimum(j, 1), nt - 1)

    def zcol(c):
        return pl.BlockSpec((tm, zw), lambda i, j: (ahead(i, j), c // zw))

    wspec = pl.BlockSpec((D, D), _const2)
    rowin = pl.BlockSpec((tm, D), lambda i, j: (ahead(i, j), 0))
    rowspec = pl.BlockSpec((tm, D), lambda i, j: (i, 0))
    vec = pl.BlockSpec((1, D), _const2)
    return pl.pallas_call(
        body, name=name, grid=(nt, nz),
        in_specs=[rowin, vec, rowin, zcol(O_ZC), zcol(O_ZC + zw), zcol(O_ZA), zcol(O_ZA + zw),
                  rowin, rowin, wspec, wspec, wspec],
        out_specs=[rowspec] * 5 + [pl.BlockSpec((tm, zw), lambda i, j: (i, O_ZC // zw + j)), vec],
        out_shape=[jax.ShapeDtypeStruct((T, D), BF)] * 5 + [jax.ShapeDtypeStruct((T, NIN), BF),
                                                            jax.ShapeDtypeStruct((1, D), F32)],
        scratch_shapes=[pltpu.VMEM((tm, 2 * D), BF)],
        compiler_params=_cp(("arbitrary", "arbitrary")),
    )(dx2, gt, o, proj, proj, proj, proj, yc, ya, wout, wcp, wap)


def _conv_bwd(dgc, proj, convw, dproj, *, tm, name):
    T = dgc.shape[0]
    tm = _tile(T, tm)
    hb = tm // HALO
    nblk = T // HALO
    nt = T // tm

    def body(dgc_ref, ndgc_ref, bg_ref, nbg_ref, cg_ref, u_ref, hcg_ref, hu_ref, cw_ref, dproj_ref, dp_ref, dcw_ref):
        i = pl.program_id(0)

        @pl.when(i == 0)
        def _():
            dcw_ref[...] = jnp.zeros_like(dcw_ref)
        first = jnp.where(i == 0, 0.0, 1.0)
        last = jnp.where(i == nt - 1, 0.0, 1.0)
        cg = cg_ref[...].astype(F32)
        u = u_ref[...].astype(F32)
        bg = bg_ref[...].astype(F32)
        dg = dgc_ref[...].astype(F32)
        cu = cg * u
        hprev = first * (hcg_ref[...].astype(F32) * hu_ref[...].astype(F32))
        m1, m2 = _conv_shifts(cu, hprev, tm)
        w0, w1, w2 = cw_ref[0:1, :], cw_ref[1:2, :], cw_ref[2:3, :]
        cv = w0 * m2 + w1 * m1 + w2 * cu
        dcv = dg * bg
        nxt = last * (ndgc_ref[...].astype(F32) * nbg_ref[...].astype(F32))
        n0, n1 = nxt[0:1, :], nxt[1:2, :]
        row = lax.broadcasted_iota(jnp.int32, dcv.shape, 0)
        p1 = jnp.where(row == tm - 1, n0, pltpu.roll(dcv, tm - 1, 0))
        p2 = jnp.where(row == tm - 1, n1, jnp.where(row == tm - 2, n0, pltpu.roll(dcv, tm - 2, 0)))
        dcu = w2 * dcv + w1 * p1 + w0 * p2
        dp_ref[:, 0:D] = (dg * cv).astype(BF)
        dp_ref[:, D:2 * D] = (dcu * u).astype(BF)
        dp_ref[:, 2 * D:3 * D] = (dcu * cg).astype(BF)
        dcw_ref[0:1, :] += jnp.sum(dcv * m2, axis=0, keepdims=True)
        dcw_ref[1:2, :] += jnp.sum(dcv * m1, axis=0, keepdims=True)
        dcw_ref[2:3, :] += jnp.sum(dcv * cu, axis=0, keepdims=True)

    rowspec = pl.BlockSpec((tm, D), _row)
    cw = pl.BlockSpec((8, D), _const2)
    return pl.pallas_call(
        body, name=name, grid=(nt,),
        in_specs=[rowspec, _halo_next(hb, nblk), _col(tm, O_BG), _halo_next(hb, nblk, O_BG),
                  _col(tm, O_CG), _col(tm, O_U), _halo_prev(hb, O_CG), _halo_prev(hb, O_U), cw,
                  pl.BlockSpec(memory_space=pl.ANY)],
        out_specs=[pl.BlockSpec((tm, 3 * D), _row), cw],
        out_shape=[jax.ShapeDtypeStruct(dproj.shape, BF), jax.ShapeDtypeStruct((8, D), F32)],
        input_output_aliases={9: 0},
        compiler_params=_cp(("arbitrary",)),
    )(dgc, dgc, proj, proj, proj, proj, proj, proj, convw, dproj)


def _adam(w, g, m, v, *, tm, name):
    _, R, C = w.shape
    tm = _tile(R, tm)
    parts = g.ndim == 3
    c1 = 1.0 - ADAM_B1
    c2 = 1.0 - ADAM_B2
    bc1 = 1.0 - ADAM_B1 ** ADAM_STEP
    bc2 = 1.0 - ADAM_B2 ** ADAM_STEP

    def body(w_ref, g_ref, m_ref, v_ref, go_ref, d_ref, nm_ref, nv_ref):
        if parts:
            gv = g_ref[0].astype(F32)
            for s in range(1, N_DEV):
                gv = gv + g_ref[s].astype(F32)
        else:
            gv = g_ref[...]
        go_ref[0] = gv
        nm = ADAM_B1 * m_ref[0] + c1 * gv
        nv = ADAM_B2 * v_ref[0] + c2 * (gv * gv)
        nm_ref[0] = nm
        nv_ref[0] = nv
        d_ref[0] = -ADAM_LR * ((nm / bc1) / (jnp.sqrt(nv / bc2) + ADAM_EPS) + ADAM_WD * w_ref[0])

    spec = pl.BlockSpec((1, tm, C), lambda i: (0, i, 0))
    gspec = pl.BlockSpec((N_DEV, tm, C), lambda i: (0, i, 0)) if parts else pl.BlockSpec((tm, C), _row)
    return pl.pallas_call(
        body, name=name, grid=(R // tm,),
        in_specs=[spec, gspec, spec, spec], out_specs=[spec] * 4,
        out_shape=[jax.ShapeDtypeStruct((1, R, C), F32)] * 4,
        compiler_params=_cp(("parallel",)),
    )(w, g, m, v)


def _mods_part(c_all, w_ada, b_ada, *, name):
    C = w_ada.shape[1]

    def body(c_ref, w_ref, b_ref, o_ref):
        cv = c_ref[...]
        ca = cv * jax.nn.sigmoid(cv)
        o_ref[...] = jnp.dot(ca, w_ref[...], preferred_element_type=F32,
                             precision=lax.Precision.HIGHEST) + b_ref[...]

    return pl.pallas_call(
        body, name=name,
        out_shape=jax.ShapeDtypeStruct((N_DEV, C), F32),
        compiler_params=_cp(),
    )(c_all, w_ada, b_ada)


def _wada_grad(c_all_t, gm, *, name):
    C = gm.shape[1]

    def body(c_ref, g_ref, o_ref):
        cv = c_ref[...]
        ca = cv * jax.nn.sigmoid(cv)
        acc = ca[:, 0:1] * g_ref[0:1, :]
        for b in range(1, N_DEV):
            acc = acc + ca[:, b:b + 1] * g_ref[b:b + 1, :]
        o_ref[...] = acc

    return pl.pallas_call(
        body, name=name,
        out_shape=jax.ShapeDtypeStruct((D, C), F32),
        compiler_params=_cp(),
    )(c_all_t, gm)


def _peer(x, y, c, d):
    px = lax.rem(x + ((d >> 2) & 1), 2)
    py = lax.rem(y + ((d >> 1) & 1), 2)
    pc = lax.rem(c + (d & 1), 2)
    return (px, py, pc), 4 * px + 2 * py + pc


def _exchange(xs, *, scatter, name):
    n = len(xs)
    nsem = n * (N_DEV - 1)

    def body(*refs):
        ins, outs = refs[:n], refs[n:2 * n]
        token, send_sems, recv_sems, local_sems = refs[2 * n:]
        x, y, c = lax.axis_index("x"), lax.axis_index("y"), lax.axis_index("c")
        me = 4 * x + 2 * y + c
        token[...] = jnp.zeros_like(token)

        def src(t, idx):
            return ins[t].at[idx] if scatter else ins[t]

        local = [pltpu.make_async_copy(src(t, me), outs[t].at[me], local_sems.at[t]) for t in range(n)]
        for cp in local:
            cp.start()
        remote = []
        for t in range(n):
            for d in range(1, N_DEV):
                peer, pidx = _peer(x, y, c, d)
                k = t * (N_DEV - 1) + d - 1
                send = pltpu.make_async_remote_copy(src_ref=src(t, pidx), dst_ref=outs[t].at[me],
                                                    send_sem=send_sems.at[k], recv_sem=recv_sems.at[k],
                                                    device_id=peer, device_id_type=MESH)
                recv = pltpu.make_async_remote_copy(src_ref=src(t, pidx), dst_ref=outs[t].at[pidx],
                                                    send_sem=send_sems.at[k], recv_sem=recv_sems.at[k],
                                                    device_id=peer, device_id_type=MESH)
                send.start()
                remote.append((send, recv))
        for cp in local:
            cp.wait()
        for send, recv in remote:
            send.wait_send()
            recv.wait_recv()

    anyspec = pl.BlockSpec(memory_space=pl.ANY)
    out_shape = [jax.ShapeDtypeStruct(a.shape if scatter else (N_DEV,) + a.shape, a.dtype) for a in xs]
    out_shape.append(jax.ShapeDtypeStruct((8, 128), F32))
    return pl.pallas_call(
        body, name=name,
        in_specs=[anyspec] * n, out_specs=[anyspec] * n + [pl.BlockSpec(memory_space=pltpu.VMEM)],
        out_shape=out_shape,
        scratch_shapes=[pltpu.SemaphoreType.DMA((nsem,)), pltpu.SemaphoreType.DMA((nsem,)),
                        pltpu.SemaphoreType.DMA((n,))],
    )(*xs)


def _sum8(parts, *, name):
    _, R, C = parts.shape

    def body(p_ref, o_ref):
        acc = p_ref[0]
        for s in range(1, N_DEV):
            acc = acc + p_ref[s]
        o_ref[...] = acc

    return pl.pallas_call(body, name=name, out_shape=jax.ShapeDtypeStruct((R, C), F32),
                          compiler_params=_cp())(parts)


HBM_SPEC = pl.BlockSpec(memory_space=pltpu.HBM)
SEM_SPEC = pl.BlockSpec(memory_space=pltpu.SEMAPHORE)
N_PEER = N_DEV - 1


def _split_copies(src_refs, land_refs, send_sems, recv_sems, scatter):
    x, y, c = lax.axis_index("x"), lax.axis_index("y"), lax.axis_index("c")
    me = 4 * x + 2 * y + c
    pairs = []
    for j, (src, land) in enumerate(zip(src_refs, land_refs)):
        for d in range(1, N_DEV):
            peer, pidx = _peer(x, y, c, d)
            k = j * N_PEER + d - 1
            s = src.at[pidx] if scatter else src
            send = pltpu.make_async_remote_copy(src_ref=s, dst_ref=land.at[me], send_sem=send_sems.at[k],
                                                recv_sem=recv_sems.at[k], device_id=peer, device_id_type=MESH)
            recv = pltpu.make_async_remote_copy(src_ref=s, dst_ref=land.at[pidx], send_sem=send_sems.at[k],
                                                recv_sem=recv_sems.at[k], device_id=peer, device_id_type=MESH)
            pairs.append((send, recv))
    return pairs


def _own_slot(block, me):
    land = lax.empty((N_DEV,) + block.shape, block.dtype)
    return lax.dynamic_update_slice(land, block[None], (me, 0, 0))


def _split_start(srcs, lands, groups, *, scatter, name):
    n, ng = len(srcs), len(groups)

    def body(*refs):
        src_refs, land_refs = refs[:n], refs[n:2 * n]
        sems = refs[2 * n:2 * n + 2 * ng]
        token = refs[-1]
        for gi, g in enumerate(groups):
            pairs = _split_copies([src_refs[t] for t in g], [land_refs[t] for t in g], sems[2 * gi],
                                  sems[2 * gi + 1], scatter)
            for send, _ in pairs:
                send.start()
        token[...] = jnp.zeros_like(token)

    sem_shapes = []
    for g in groups:
        sem_shapes += [pltpu.SemaphoreType.DMA((len(g) * N_PEER,))] * 2
    thru = [pltpu.HBM(a.shape, a.dtype) for a in list(srcs) + list(lands)]
    outs = pl.pallas_call(
        body, name=name,
        out_shape=tuple(sem_shapes + thru + [jax.ShapeDtypeStruct((8, 128), F32)]),
        in_specs=[HBM_SPEC] * (2 * n),
        out_specs=tuple([SEM_SPEC] * (2 * ng) + [HBM_SPEC] * (2 * n) + [pl.BlockSpec(memory_space=pltpu.VMEM)]),
        input_output_aliases={i: 2 * ng + i for i in range(2 * n)},
        compiler_params=pltpu.CompilerParams(has_side_effects=pltpu.SideEffectType.DATAFLOW_SIDE_EFFECTING),
    )(*[pltpu.with_memory_space_constraint(a, pltpu.HBM) for a in list(srcs) + list(lands)])
    sems = [(outs[2 * gi], outs[2 * gi + 1]) for gi in range(ng)]
    return sems, outs[2 * ng:2 * ng + n], outs[2 * ng + n:2 * ng + 2 * n], outs[-1]


def _behind(v, token):
    if token is None:
        return v
    return v + token[0, 0].astype(v.dtype)


def _split_wait(srcs, lands, sems, after, *, scatter, name):
    m = len(srcs)

    def body(*refs):
        src_refs, land_refs = refs[:m], refs[m:2 * m]
        send_sems, recv_sems = refs[2 * m], refs[2 * m + 1]
        for send, recv in _split_copies(src_refs, land_refs, send_sems, recv_sems, scatter):
            send.wait_send()
            recv.wait_recv()

    outs = pl.pallas_call(
        body, name=name,
        out_shape=tuple(pltpu.HBM(a.shape, a.dtype) for a in list(srcs) + list(lands)),
        in_specs=[HBM_SPEC] * (2 * m) + [SEM_SPEC, SEM_SPEC, pl.BlockSpec(memory_space=pl.ANY)],
        out_specs=tuple([HBM_SPEC] * (2 * m)),
        input_output_aliases={i: i for i in range(2 * m)},
        compiler_params=pltpu.CompilerParams(has_side_effects=pltpu.SideEffectType.DATAFLOW_SIDE_EFFECTING),
    )(*srcs, *lands, sems[0], sems[1], after)
    return outs[m:]


TL_FIRST = (1, 2, 4, 6)
TL_ICI = (2, 4, 6)
EFFECT = pltpu.SideEffectType.DATAFLOW_SIDE_EFFECTING


def _tl_first(src_refs, land_refs, send_sems, recv_sems):
    x, y, c = lax.axis_index("x"), lax.axis_index("y"), lax.axis_index("c")
    me = 4 * x + 2 * y + c
    out = []
    for j, (src, land) in enumerate(zip(src_refs, land_refs)):
        for i, d in enumerate(TL_FIRST):
            peer, pidx = _peer(x, y, c, d)
            k = len(TL_FIRST) * j + i
            send = pltpu.make_async_remote_copy(src_ref=src, dst_ref=land.at[me], send_sem=send_sems.at[k],
                                                recv_sem=recv_sems.at[k], device_id=peer, device_id_type=MESH)
            recv = pltpu.make_async_remote_copy(src_ref=src, dst_ref=land.at[pidx], send_sem=send_sems.at[k],
                                                recv_sem=recv_sems.at[k], device_id=peer, device_id_type=MESH)
            out.append((d, send, recv))
    return out


def _tl_second(land_refs, send_sems, recv_sems):
    x, y, c = lax.axis_index("x"), lax.axis_index("y"), lax.axis_index("c")
    sibling, _ = _peer(x, y, c, 1)
    out = []
    for j, land in enumerate(land_refs):
        for i, d in enumerate(TL_ICI):
            _, mine = _peer(x, y, c, d)
            _, theirs = _peer(x, y, c, d + 1)
            k = len(TL_ICI) * j + i
            send = pltpu.make_async_remote_copy(src_ref=land.at[mine], dst_ref=land.at[mine], send_sem=send_sems.at[k],
                                                recv_sem=recv_sems.at[k], device_id=sibling, device_id_type=MESH)
            recv = pltpu.make_async_remote_copy(src_ref=land.at[mine], dst_ref=land.at[theirs],
                                                send_sem=send_sems.at[k], recv_sem=recv_sems.at[k],
                                                device_id=sibling, device_id_type=MESH)
            out.append((send, recv))
    return out


def _tl_start(srcs, lands, groups, *, name):
    n, ng = len(srcs), len(groups)

    def body(*refs):
        src_refs, land_refs = refs[:n], refs[n:2 * n]
        sems = refs[2 * n:2 * n + 2 * ng]
        for gi, g in enumerate(groups):
            for _, send, _ in _tl_first([src_refs[t] for t in g], [land_refs[t] for t in g], sems[2 * gi],
                                        sems[2 * gi + 1]):
                send.start()
        refs[-1][...] = jnp.zeros_like(refs[-1])

    sem_shapes = []
    for g in groups:
        sem_shapes += [pltpu.SemaphoreType.DMA((len(g) * len(TL_FIRST),))] * 2
    thru = [pltpu.HBM(a.shape, a.dtype) for a in list(srcs) + list(lands)]
    outs = pl.pallas_call(
        body, name=name,
        out_shape=tuple(sem_shapes + thru + [jax.ShapeDtypeStruct((8, 128), F32)]),
        in_specs=[HBM_SPEC] * (2 * n),
        out_specs=tuple([SEM_SPEC] * (2 * ng) + [HBM_SPEC] * (2 * n) + [pl.BlockSpec(memory_space=pltpu.VMEM)]),
        input_output_aliases={i: 2 * ng + i for i in range(2 * n)},
        compiler_params=pltpu.CompilerParams(has_side_effects=EFFECT),
    )(*[pltpu.with_memory_space_constraint(a, pltpu.HBM) for a in list(srcs) + list(lands)])
    sems = [(outs[2 * gi], outs[2 * gi + 1]) for gi in range(ng)]
    return sems, outs[2 * ng:2 * ng + n], outs[2 * ng + n:2 * ng + 2 * n], outs[-1]


def _tl_forward(srcs, lands, sems1, after, *, name):
    m = len(srcs)

    def body(*refs):
        src_refs, land_refs = refs[:m], refs[m:2 * m]
        send1, recv1 = refs[2 * m], refs[2 * m + 1]
        send2, recv2 = refs[2 * m + 3], refs[2 * m + 4]
        for d, _, recv in _tl_first(src_refs, land_refs, send1, recv1):
            if d in TL_ICI:
                recv.wait_recv()
        for send, _ in _tl_second(land_refs, send2, recv2):
            send.start()

    sem = pltpu.SemaphoreType.DMA((m * len(TL_ICI),))
    outs = pl.pallas_call(
        body, name=name,
        out_shape=tuple([sem, sem] + [pltpu.HBM(a.shape, a.dtype) for a in list(srcs) + list(lands)]),
        in_specs=[HBM_SPEC] * (2 * m) + [SEM_SPEC, SEM_SPEC, pl.BlockSpec(memory_space=pl.ANY)],
        out_specs=tuple([SEM_SPEC, SEM_SPEC] + [HBM_SPEC] * (2 * m)),
        input_output_aliases={i: 2 + i for i in range(2 * m)},
        compiler_params=pltpu.CompilerParams(has_side_effects=EFFECT),
    )(*srcs, *lands, sems1[0], sems1[1], after)
    return (outs[0], outs[1]), outs[2:2 + m], outs[2 + m:2 + 2 * m]


def _tl_wait(srcs, lands, sems1, sems2, after, *, name):
    m = len(srcs)

    def body(*refs):
        src_refs, land_refs = refs[:m], refs[m:2 * m]
        send1, recv1, send2, recv2 = refs[2 * m:2 * m + 4]
        for d, send, recv in _tl_first(src_refs, land_refs, send1, recv1):
            send.wait_send()
            if d not in TL_ICI:
                recv.wait_recv()
        for send, recv in _tl_second(land_refs, send2, recv2):
            send.wait_send()
            recv.wait_recv()

    outs = pl.pallas_call(
        body, name=name,
        out_shape=tuple(pltpu.HBM(a.shape, a.dtype) for a in list(srcs) + list(lands)),
        in_specs=[HBM_SPEC] * (2 * m) + [SEM_SPEC] * 4 + [pl.BlockSpec(memory_space=pl.ANY)],
        out_specs=tuple([HBM_SPEC] * (2 * m)),
        input_output_aliases={i: i for i in range(2 * m)},
        compiler_params=pltpu.CompilerParams(has_side_effects=EFFECT),
    )(*srcs, *lands, sems1[0], sems1[1], sems2[0], sems2[1], after)
    return outs[m:]


TM_PROJ = 512
TN_PROJ = 512
TM_ROW = 512
TM_NN = 512
TK_TN = 512
TN_FFN = F // 2
TN_IN = NIN // 4


def _tn(a, b, name, tn):
    if a.ndim == 2:
        a = a[None]
    return _tn_matmul(a, b, tn=tn, tk=TK_TN, name=name)


def _local_step(x, tgt, mods, g1, gm, g2, gf, convw8, sinks, w_get, g_put):
    T = x.shape[0]
    sh1, sc1, gt1, sh2, sc2, gt2, sh3, sc3, gt3 = [mods[i:i + 1] for i in range(N_MOD)]
    cos, sin = _rope_tables(T)
    behind = _behind

    w = dict(w_get("gu1", mods))
    h1, ab1 = _norm_proj(x, g1, sc1, sh1, w["gu1"], tm=TM_PROJ, tn=TN_PROJ, name="ffn1_up")
    w.update(w_get("d1", ab1))
    x1, y1 = _ffn_down_fwd(ab1, w["d1"], x, gt1, tm=TM_ROW, name="ffn1_down")
    w.update(w_get("mix", x1))
    h2, proj = _norm_proj(x1, gm, sc2, sh2, w["win"], tm=TM_PROJ, tn=TN_PROJ, name="mix_in")
    qs, kr = _attn_prep(proj, cos, sin, name="attn_prep")
    bias = _attn_bias()
    attn, lse = _attn_fwd(qs, kr, proj, bias, sinks, name="attn_fwd")
    x2, gc, yc, ya, mg, o = _mixer_mid_fwd(proj, attn, w["cp"], w["ap"], w["out"], convw8, x1, gt2,
                                           tm=TM_ROW, name="mix_mid")
    w.update(w_get("ffn2", x2))
    h3, ab2 = _norm_proj(x2, g2, sc3, sh3, w["gu2"], tm=TM_PROJ, tn=TN_PROJ, name="ffn2_up")
    x3, y2 = _ffn_down_fwd(ab2, w["d2"], x2, gt3, tm=TM_ROW, name="ffn2_down")
    dx3, lsum, dgf = _final_fwd_bwd(x3, tgt, gf, tm=TM_ROW, name="final")

    dy2, dab2, dgt3 = _ffn_down_bwd(dx3, y2, gt3, ab2, w["d2"], tm=TM_ROW, tn=MXU_N, name="ffn2_down_bwd")
    g_d2 = _tn_matmul_swiglu(ab2, dy2, None, tn=TN_FFN, tk=TK_TN, name="ffn2_down_dw")
    dx2, dsh3, dsc3, dg2 = _nn_bwd_norm(dab2, w["gu2"], x2, g2, sc3, dx3, tm=TM_NN, name="ffn2_up_bwd")
    g_gu2 = _tn(dab2, h3, "ffn2_up_dw", TN_FFN)
    tok = g_put(dict(gu2=g_gu2, d2=g_d2))

    dout, dyc, dya, dgc, dat, dproj, dgt2 = _mixer_mid_bwd(dx2, behind(gt2, tok), o, proj, yc, ya, w["out"], w["cp"],
                                                           w["ap"], tm=TM_ROW, name="mix_mid_bwd")
    g_out = _tn(mg, dout, "mix_out_dw", D)
    g_cp = _tn(gc, dyc, "mix_cp_dw", D)
    g_ap = _tn(attn, dya, "mix_ap_dw", D)
    dproj, dkc, dkp, dvc, dvp, dsink = _attn_bwd(qs, kr, proj, bias, sinks, lse, attn, dat, cos, sin, dproj,
                                                 name="attn_bwd")
    dproj = _dkv_combine(dkc, dkp, dvc, dvp, dproj, name="attn_dkv")
    dproj, dcw = _conv_bwd(dgc, proj, convw8, dproj, tm=TM_ROW, name="conv_bwd")
    g_in = _tn(dproj, h2, "mix_in_dw", TN_IN)
    tok = g_put(dict(win=g_in, cp=g_cp, ap=g_ap, out=g_out))
    dx1, dsh2, dsc2, dgm = _nn_bwd_norm(dproj[None], w["win"], x1, gm, behind(sc2, tok), dx2, tm=TM_NN,
                                        name="mix_in_bwd")

    dy1, dab1, dgt1 = _ffn_down_bwd(dx1, y1, gt1, ab1, w["d1"], tm=TM_ROW, tn=MXU_N, name="ffn1_down_bwd")
    g_gu1 = _tn(dab1, h1, "ffn1_up_dw", TN_FFN)
    tok = g_put(dict(gu1=g_gu1))
    g_d1 = _tn_matmul_swiglu(ab1, dy1, tok, tn=TN_FFN, tk=TK_TN, name="ffn1_down_dw")
    tok = g_put(dict(d1=g_d1))
    dx0, dsh1, dsc1, dg1 = _nn_bwd_norm(dab1, w["gu1"], x, g1, behind(sc1, tok), dx1, tm=TM_NN,
                                        name="ffn1_up_bwd")

    small = dict(mods=jnp.concatenate([dsh1, dsc1, dgt1, dsh2, dsc2, dgt2, dsh3, dsc3, dgt3], axis=0),
                 g1=dg1, gm=dgm, g2=dg2, gf=dgf, convw=dcw[0:3], sinks=dsink[:, 0:N_HEADS])
    return lsum, dx0, small


BIG = ("gu1", "d1", "win", "cp", "ap", "out", "gu2", "d2")
TRANSPOSED = ("gu1", "win", "gu2")
SMALL_ROWS = 24
R_MODS, R_G1, R_GM, R_G2, R_GF, R_CONV, R_SINK = 0, 9, 10, 11, 12, 13, 16


def _pad_to(a, rows, cols):
    return jnp.pad(a, ((0, rows - a.shape[0]), (0, cols - a.shape[1])))


def _pack_small(b_ada, g1, gm, g2, gf, conv, sinks):
    rows = [b_ada.reshape(N_MOD, D), g1.reshape(1, D), gm.reshape(1, D), g2.reshape(1, D), gf.reshape(1, D),
            _pad_to(conv.reshape(3, -1), 3, D), _pad_to(sinks.reshape(1, N_HEADS), 1, D)]
    return _pad_to(jnp.concatenate(rows, axis=0), SMALL_ROWS, D)


def _unpack_small(p, conv_cols):
    return dict(b_ada=p[R_MODS:R_MODS + N_MOD].reshape(1, N_MOD * D), g_ffn1=p[R_G1:R_G1 + 1],
                g_mix=p[R_GM:R_GM + 1], g_ffn2=p[R_G2:R_G2 + 1], g_final=p[R_GF],
                conv_w=p[R_CONV:R_CONV + 3, 0:conv_cols][None], sinks=p[R_SINK:R_SINK + 1, 0:N_HEADS])


def kernel(x, c, w_ada, b_ada, g_ffn1, w1_gu, w1_down, g_mix, w_in, conv_w, w_conv_proj, w_attn_proj, sinks, w_out, g_ffn2, w2_gu, w2_down, g_final, loss_target, m_w_ada, m_b_ada, m_g_ffn1, m_w1_gu, m_w1_down, m_g_mix, m_w_in, m_conv_w, m_w_conv_proj, m_w_attn_proj, m_sinks, m_w_out, m_g_ffn2, m_w2_gu, m_w2_down, m_g_final, v_w_ada, v_b_ada, v_g_ffn1, v_w1_gu, v_w1_down, v_g_mix, v_w_in, v_conv_w, v_w_conv_proj, v_w_attn_proj, v_sinks, v_w_out, v_g_ffn2, v_w2_gu, v_w2_down, v_g_final):
    me = 4 * lax.axis_index("x") + 2 * lax.axis_index("y") + lax.axis_index("c")
    ada_cols = w_ada.shape[2]
    conv_cols = conv_w.shape[2]

    native = dict(gu1=w1_gu[0], d1=w1_down[0], win=w_in[0], cp=w_conv_proj[0], ap=w_attn_proj[0], out=w_out[0],
                  gu2=w2_gu[0], d2=w2_down[0])

    def shard(n, token):
        a = _behind(native[n], token)
        return (a.T if n in TRANSPOSED else a).astype(BF)

    c_all, conv_all, _ = _exchange([c, _pad_to(conv_w[0], 8, conv_cols)], scatter=False, name="gather_cond")
    c_all = c_all.reshape(N_DEV, D)
    conv_full = conv_all[:, 0:3, :].transpose(1, 0, 2).reshape(3, D)

    b_cols = lax.dynamic_slice(b_ada, (0, me * ada_cols), (1, ada_cols))
    mods_cols = _mods_part(c_all, w_ada[0], b_cols, name="ada_mods")
    mods_all, mods_token = _exchange([mods_cols], scatter=False, name="gather_mods")
    mods = lax.dynamic_index_in_dim(mods_all, me, axis=1, keepdims=False).reshape(N_MOD, D)

    groups = dict(gu1=("gu1",), d1=("d1",), mix=("win", "cp", "ap", "out"), ffn2=("gu2", "d2"))
    in_flight = {}
    first = [shard("gu1", mods_token)]
    sems, srcs, lands, token = _tl_start(first, [_own_slot(s, me) for s in first], [[0]],
                                         name="gather_weights_start_gu1")
    in_flight["gu1"] = [sems[0], srcs, lands, None]
    rest = [n for n in BIG if n != "gu1"]
    shards = [shard(n, token) for n in rest]
    rest_groups = [[rest.index(n) for n in names] for g, names in groups.items() if g != "gu1"]
    sems, srcs, lands, rest_token = _tl_start(shards, [_own_slot(s, me) for s in shards], rest_groups,
                                              name="gather_weights_start_rest")
    for (g, names), gsems, idx in zip([kv for kv in groups.items() if kv[0] != "gu1"], sems, rest_groups):
        in_flight[g] = [gsems, [srcs[t] for t in idx], [lands[t] for t in idx], None]

    def forward(group, after):
        sems1, gsrcs, glands, _ = in_flight[group]
        sems2, gsrcs, glands = _tl_forward(gsrcs, glands, sems1, after, name="gather_weights_forward_" + group)
        in_flight[group] = [sems1, gsrcs, glands, sems2]

    forward_early = dict(d1="mix", mix="ffn2")

    def w_get(group, after):
        if group == "gu1":
            after = rest_token
        if in_flight[group][3] is None:
            forward(group, after)
        sems1, gsrcs, glands, sems2 = in_flight[group]
        landed = _tl_wait(gsrcs, glands, sems1, sems2, after, name="gather_weights_wait_" + group)
        if group in forward_early:
            forward(forward_early[group], landed[0])
        return {n: a.reshape(-1, D) for n, a in zip(groups[group], landed)}

    pending = []

    def g_put(gs):
        names = tuple(gs)
        srcs = [gs[n].reshape(N_DEV, -1, D) for n in names]
        lands = [_own_slot(lax.dynamic_index_in_dim(s, me, axis=0, keepdims=False), me) for s in srcs]
        sems, srcs, lands, tok = _split_start(srcs, lands, [list(range(len(names)))], scatter=True,
                                              name="scatter_grads_start_" + names[0])
        pending.append((names, sems[0], srcs, lands))
        return tok

    lsum, grad_x, small = _local_step(x[0], loss_target[0], mods, g_ffn1, g_mix, g_ffn2, g_final[None],
                                      _pad_to(conv_full, 8, D), sinks[0], w_get, g_put)
    loss = lax.psum((0.5 / D) * jnp.sum(lsum), ("x", "y", "c"))

    packed = _pack_small(small["mods"], small["g1"], small["gm"], small["g2"], small["gf"], small["convw"],
                         small["sinks"])
    packed_all, _ = _exchange([packed], scatter=False, name="gather_small")
    gsmall = _sum8(packed_all, name="sum_small")

    w_of = dict(ada=w_ada, gu1=w1_gu, d1=w1_down, win=w_in, cp=w_conv_proj, ap=w_attn_proj, out=w_out, gu2=w2_gu,
                d2=w2_down)
    m_of = dict(ada=m_w_ada, gu1=m_w1_gu, d1=m_w1_down, win=m_w_in, cp=m_w_conv_proj, ap=m_w_attn_proj, out=m_w_out,
                gu2=m_w2_gu, d2=m_w2_down)
    v_of = dict(ada=v_w_ada, gu1=v_w1_gu, d1=v_w1_down, win=v_w_in, cp=v_w_conv_proj, ap=v_w_attn_proj, out=v_w_out,
                gu2=v_w2_gu, d2=v_w2_down)
    upd = {}
    after = gsmall
    for names, sems, srcs, lands in pending:
        parts = _split_wait(srcs, lands, sems, after, scatter=True, name="scatter_grads_wait_" + names[0])
        for n, p in zip(names, parts):
            if n in TRANSPOSED:
                res = _adam(jnp.swapaxes(w_of[n], 1, 2), p, jnp.swapaxes(m_of[n], 1, 2), jnp.swapaxes(v_of[n], 1, 2),
                            tm=128, name="adam_" + n)
                upd[n] = [jnp.swapaxes(t, 1, 2) for t in res]
            else:
                upd[n] = _adam(w_of[n], p, m_of[n], v_of[n], tm=128, name="adam_" + n)
        after = upd[names[-1]][1]

    gm_cols = lax.dynamic_slice(packed_all[:, R_MODS:R_MODS + N_MOD, :].reshape(N_DEV, N_MOD * D),
                                (0, me * ada_cols), (N_DEV, ada_cols))
    upd["ada"] = _adam(w_ada, _wada_grad(c_all.T, gm_cols, name="ada_dw"), m_w_ada, v_w_ada, tm=128, name="adam_ada")
    conv_g = lax.dynamic_slice(gsmall[R_CONV:R_CONV + 3], (0, me * conv_cols), (3, conv_cols))
    gsmall_own = gsmall.at[R_CONV:R_CONV + 3].set(_pad_to(conv_g, 3, D))
    small_upd = _adam(_pack_small(b_ada, g_ffn1, g_mix, g_ffn2, g_final, conv_w, sinks)[None], gsmall_own,
                      _pack_small(m_b_ada, m_g_ffn1, m_g_mix, m_g_ffn2, m_g_final, m_conv_w, m_sinks)[None],
                      _pack_small(v_b_ada, v_g_ffn1, v_g_mix, v_g_ffn2, v_g_final, v_conv_w, v_sinks)[None],
                      tm=SMALL_ROWS, name="adam_small")
    small_out = [_unpack_small(p[0], conv_cols) for p in small_upd]

    big_name = dict(w_ada="ada", w1_gu="gu1", w1_down="d1", w_in="win", w_conv_proj="cp", w_attn_proj="ap",
                    w_out="out", w2_gu="gu2", w2_down="d2")
    order = ("w_ada", "b_ada", "g_ffn1", "w1_gu", "w1_down", "g_mix", "w_in", "conv_w", "w_conv_proj", "w_attn_proj",
             "sinks", "w_out", "g_ffn2", "w2_gu", "w2_down", "g_final")
    outs = [loss, grad_x[None]]
    for kind in range(4):
        for n in order:
            outs.append(upd[big_name[n]][kind] if n in big_name else small_out[kind][n])
    return tuple(outs)
```

```python
import functools

import jax
import jax.numpy as jnp
from jax import lax
from jax.experimental import pallas as pl
from jax.experimental.pallas import tpu as pltpu

D = 1024
F = 2816
NIN = 6656
N_HEADS = 16
N_KV = 4
HEAD_DIM = 64
BLK = 128
N_MOD = 9
N_DEV = 8
EPS = 1e-6
NEG_INF = -1e30
ROPE_THETA = 10000.0
O_BG, O_CG, O_U, O_Q, O_K, O_V, O_ZC, O_ZA = 0, 1024, 2048, 3072, 4096, 4352, 4608, 5632

ADAM_LR = 0.001
ADAM_B1 = 0.9
ADAM_B2 = 0.999
ADAM_EPS = 1e-08
ADAM_WD = 0.01
ADAM_STEP = 10

BF = jnp.bfloat16
F32 = jnp.float32
VMEM_LIMIT = 56 * 1024 * 1024
MXU_N = 256
MESH = pl.DeviceIdType.MESH

NT = (((1,), (1,)), ((), ()))
TN = (((0,), (0,)), ((), ()))


def _cp(sem=None):
    return pltpu.CompilerParams(dimension_semantics=sem, vmem_limit_bytes=VMEM_LIMIT)


def _tile(n, pref):
    if n <= pref:
        return n
    for t in range(pref - pref % 16, 15, -16):
        if n % t == 0:
            return t
    raise ValueError((n, pref))


def _sigmoid(v):
    return 0.5 * jnp.tanh(0.5 * v) + 0.5


def _row(i):
    return (i, 0)


def _const2(*_):
    return (0, 0)


def _resident(shape):
    return pl.BlockSpec(shape, lambda *_: (0,) * len(shape), pipeline_mode=pl.Buffered(1))


def _norm_proj(x, g, sc, sh, wt, *, tm, tn, name):
    T, N = x.shape[0], wt.shape[0]
    tm = _tile(T, tm)

    def body(x_ref, g_ref, sc_ref, sh_ref, w_ref, h_ref, o_ref):
        xv = x_ref[...]
        r = lax.rsqrt(jnp.mean(xv * xv, axis=-1, keepdims=True) + EPS)
        hb = ((xv * r) * g_ref[...] * (1.0 + sc_ref[...]) + sh_ref[...]).astype(BF)
        h_ref[...] = hb
        for c0 in range(0, N, tn):
            cols = pl.ds(c0, tn)
            o_ref[:, cols] = lax.dot_general(hb, w_ref[cols, :], NT, preferred_element_type=F32).astype(BF)

    vec = pl.BlockSpec((1, D), _const2)
    return pl.pallas_call(
        body, name=name, grid=(T // tm,),
        in_specs=[pl.BlockSpec((tm, D), _row), vec, vec, vec, _resident((N, D))],
        out_specs=[pl.BlockSpec((tm, D), _row), pl.BlockSpec((tm, N), _row)],
        out_shape=[jax.ShapeDtypeStruct((T, D), BF), jax.ShapeDtypeStruct((T, N), BF)],
        compiler_params=_cp(("parallel",)),
    )(x, g, sc, sh, wt)


def _ffn_down_fwd(ab, wd, x, gt, *, tm, name):
    T = x.shape[0]
    tm = _tile(T, tm)

    def body(a_ref, b_ref, wd_ref, x_ref, gt_ref, xo_ref, y_ref):
        y = None
        for c0 in range(0, F, MXU_N):
            cols = pl.ds(c0, MXU_N)
            a = a_ref[:, cols].astype(F32)
            act = (a * _sigmoid(a) * b_ref[:, cols].astype(F32)).astype(BF)
            part = jnp.dot(act, wd_ref[cols, :], preferred_element_type=F32)
            y = part if y is None else y + part
        y_ref[...] = y.astype(BF)
        xo_ref[...] = x_ref[...] + (0.5 * gt_ref[...]) * y

    return pl.pallas_call(
        body, name=name, grid=(T // tm,),
        in_specs=[pl.BlockSpec((tm, F), lambda i: (i, 0)), pl.BlockSpec((tm, F), lambda i: (i, 1)),
                  pl.BlockSpec((F, D), _const2), pl.BlockSpec((tm, D), _row), pl.BlockSpec((1, D), _const2)],
        out_specs=[pl.BlockSpec((tm, D), _row), pl.BlockSpec((tm, D), _row)],
        out_shape=[jax.ShapeDtypeStruct((T, D), F32), jax.ShapeDtypeStruct((T, D), BF)],
        compiler_params=_cp(("parallel",)),
    )(ab, ab, wd, x, gt)


def _final_fwd_bwd(x, tgt, g, *, tm, name):
    T = x.shape[0]
    tm = _tile(T, tm)

    def body(x_ref, t_ref, g_ref, dx_ref, ls_ref, dg_ref):
        @pl.when(pl.program_id(0) == 0)
        def _():
            ls_ref[...] = jnp.zeros_like(ls_ref)
            dg_ref[...] = jnp.zeros_like(dg_ref)
        xv = x_ref[...]
        gv = g_ref[...]
        r = lax.rsqrt(jnp.mean(xv * xv, axis=-1, keepdims=True) + EPS)
        xh = xv * r
        e = xh * gv - t_ref[...]
        ls_ref[...] += jnp.sum(e * e, axis=0, keepdims=True)
        dy = e * (1.0 / D)
        dg_ref[...] += jnp.sum(dy * xh, axis=0, keepdims=True)
        dxh = dy * gv
        dx_ref[...] = r * (dxh - xh * jnp.mean(dxh * xh, axis=-1, keepdims=True))

    vec = pl.BlockSpec((1, D), _const2)
    return pl.pallas_call(
        body, name=name, grid=(T // tm,),
        in_specs=[pl.BlockSpec((tm, D), _row), pl.BlockSpec((tm, D), _row), vec],
        out_specs=[pl.BlockSpec((tm, D), _row), vec, vec],
        out_shape=[jax.ShapeDtypeStruct((T, D), F32), jax.ShapeDtypeStruct((1, D), F32),
                   jax.ShapeDtypeStruct((1, D), F32)],
        compiler_params=_cp(("arbitrary",)),
    )(x, tgt, g)


def _ffn_down_bwd(dxo, y, gt, ab, wd, *, tm, tn, name):
    T = dxo.shape[0]
    tm = _tile(T, tm)

    def body(dxo_ref, y_ref, gt_ref, a_ref, b_ref, wd_ref, dy_ref, dab_ref, dgt_ref):
        @pl.when(pl.program_id(0) == 0)
        def _():
            dgt_ref[...] = jnp.zeros_like(dgt_ref)

        dxv = dxo_ref[...]
        dgt_ref[...] += 0.5 * jnp.sum(dxv * y_ref[...].astype(F32), axis=0, keepdims=True)
        dy = ((0.5 * gt_ref[...]) * dxv).astype(BF)
        dy_ref[...] = dy
        for c0 in range(0, F, tn):
            cols = pl.ds(c0, tn)
            dact = lax.dot_general(dy, wd_ref[cols, :], NT, preferred_element_type=F32)
            a = a_ref[:, cols].astype(F32)
            b = b_ref[:, cols].astype(F32)
            s = _sigmoid(a)
            dab_ref[0, :, cols] = (dact * b * (s * (1.0 + a * (1.0 - s)))).astype(BF)
            dab_ref[1, :, cols] = (dact * (a * s)).astype(BF)

    vec = pl.BlockSpec((1, D), _const2)
    rowspec = pl.BlockSpec((tm, D), _row)
    return pl.pallas_call(
        body, name=name, grid=(T // tm,),
        in_specs=[rowspec, rowspec, vec, pl.BlockSpec((tm, F), lambda i: (i, 0)),
                  pl.BlockSpec((tm, F), lambda i: (i, 1)), pl.BlockSpec((F, D), _const2)],
        out_specs=[rowspec, pl.BlockSpec((2, tm, F), lambda i: (0, i, 0)), vec],
        out_shape=[jax.ShapeDtypeStruct((T, D), BF), jax.ShapeDtypeStruct((2, T, F), BF),
                   jax.ShapeDtypeStruct((1, D), F32)],
        compiler_params=_cp(("arbitrary",)),
    )(dxo, y, gt, ab, ab, wd)


def _tn_matmul(a, b, *, tn, tk, name):
    S, T, Ns = a.shape
    tn, tk = _tile(Ns, tn), _tile(T, tk)
    nk, njs = T // tk, Ns // tn

    def body(a_ref, b_ref, o_ref, acc):
        k = pl.program_id(1)

        @pl.when(k == 0)
        def _():
            acc[...] = jnp.zeros_like(acc)
        acc[...] += lax.dot_general(a_ref[0], b_ref[...], TN, preferred_element_type=F32)

        @pl.when(k == nk - 1)
        def _():
            o_ref[...] = acc[...].astype(BF)

    return pl.pallas_call(
        body, name=name, grid=(S * njs, nk),
        in_specs=[pl.BlockSpec((1, tk, tn), lambda j, k: (j // njs, k, j % njs)),
                  pl.BlockSpec((tk, D), lambda j, k: (k, 0))],
        out_specs=pl.BlockSpec((tn, D), lambda j, k: (j, 0)),
        out_shape=jax.ShapeDtypeStruct((S * Ns, D), BF),
        scratch_shapes=[pltpu.VMEM((tn, D), F32)],
        compiler_params=_cp(("parallel", "arbitrary")),
    )(a, b)


def _tn_matmul_swiglu(ab, b, token, *, tn, tk, name):
    T = ab.shape[0]
    tn, tk = _tile(F, tn), _tile(T, tk)
    nk, nj = T // tk, F // tn
    deps = [] if token is None else [token]

    def body(a_ref, g_ref, b_ref, *rest):
        o_ref, acc = rest[len(deps):]
        k = pl.program_id(1)

        @pl.when(k == 0)
        def _():
            acc[...] = jnp.zeros_like(acc)
        bv = b_ref[...]
        for c0 in range(0, tn, MXU_N):
            cw = min(MXU_N, tn - c0)
            cols = pl.ds(c0, cw)
            a = a_ref[:, cols].astype(F32)
            act = (a * _sigmoid(a) * g_ref[:, cols].astype(F32)).astype(BF)
            acc[cols, :] += lax.dot_general(act, bv, TN, preferred_element_type=F32)

        @pl.when(k == nk - 1)
        def _():
            o_ref[...] = acc[...].astype(BF)

    return pl.pallas_call(
        body, name=name, grid=(nj, nk),
        in_specs=[pl.BlockSpec((tk, tn), lambda j, k: (k, j)), pl.BlockSpec((tk, tn), lambda j, k: (k, j + nj)),
                  pl.BlockSpec((tk, D), lambda j, k: (k, 0))] + [pl.BlockSpec(memory_space=pl.ANY)] * len(deps),
        out_specs=pl.BlockSpec((tn, D), lambda j, k: (j, 0)),
        out_shape=jax.ShapeDtypeStruct((F, D), BF),
        scratch_shapes=[pltpu.VMEM((tn, D), F32)],
        compiler_params=_cp(("parallel", "arbitrary")),
    )(ab, ab, b, *deps)


def _nn_bwd_norm(da, w, x, g, sc, dxo, *, tm, name):
    S, T, Ks = da.shape
    tm = _tile(T, tm)
    rc = _tile(tm, 256)

    def body(da_ref, w_ref, x_ref, g_ref, sc_ref, dxo_ref, dx_ref, dsh_ref, dsc_ref, dg_ref, acc):
        @pl.when(pl.program_id(0) == 0)
        def _():
            dsh_ref[...] = jnp.zeros_like(dsh_ref)
            dsc_ref[...] = jnp.zeros_like(dsc_ref)
            dg_ref[...] = jnp.zeros_like(dg_ref)

        d = jnp.dot(da_ref[0], w_ref[0:Ks, :], preferred_element_type=F32)
        for s in range(1, S):
            d = d + jnp.dot(da_ref[s], w_ref[s * Ks:(s + 1) * Ks, :], preferred_element_type=F32)
        acc[...] = d
        gv = g_ref[...]
        sc1 = 1.0 + sc_ref[...]
        dsh = jnp.zeros((1, D), F32)
        dsc = jnp.zeros((1, D), F32)
        dg = jnp.zeros((1, D), F32)
        for r0 in range(0, tm, rc):
            rows = pl.ds(r0, rc)
            u = acc[rows, :]
            xv = x_ref[rows, :]
            r = lax.rsqrt(jnp.mean(xv * xv, axis=-1, keepdims=True) + EPS)
            xh = xv * r
            dsh = dsh + jnp.sum(u, axis=0, keepdims=True)
            dsc = dsc + jnp.sum(u * (xh * gv), axis=0, keepdims=True)
            us = u * sc1
            dg = dg + jnp.sum(us * xh, axis=0, keepdims=True)
            dxh = us * gv
            dx_ref[rows, :] = dxo_ref[rows, :] + r * (dxh - xh * jnp.mean(dxh * xh, axis=-1, keepdims=True))
        dsh_ref[...] += dsh
        dsc_ref[...] += dsc
        dg_ref[...] += dg

    vec = pl.BlockSpec((1, D), _const2)
    rowspec = pl.BlockSpec((tm, D), _row)
    return pl.pallas_call(
        body, name=name, grid=(T // tm,),
        in_specs=[pl.BlockSpec((S, tm, Ks), lambda i: (0, i, 0)), _resident((S * Ks, D)), rowspec, vec, vec, rowspec],
        out_specs=[rowspec, vec, vec, vec],
        out_shape=[jax.ShapeDtypeStruct((T, D), F32)] + [jax.ShapeDtypeStruct((1, D), F32)] * 3,
        scratch_shapes=[pltpu.VMEM((tm, D), F32)],
        compiler_params=_cp(("arbitrary",)),
    )(da, w, x, g, sc, dxo)


def _rope(t, cos, sin_signed, lt32, inverse=False):
    sel = jnp.where(lt32, pltpu.roll(t, 96, 1), pltpu.roll(t, 32, 1))
    return t * cos - sel * sin_signed if inverse else t * cos + sel * sin_signed


def _rope_tables(T):
    inv = 1.0 / (ROPE_THETA ** (jnp.arange(0, HEAD_DIM, 2, dtype=F32) / HEAD_DIM))
    ang = jnp.arange(T, dtype=F32)[:, None] * inv[None, :]
    cos, sin = jnp.cos(ang), jnp.sin(ang)
    cos128 = jnp.tile(cos, (1, 4))
    sin128 = jnp.tile(jnp.concatenate([-sin, sin], axis=1), (1, 2))
    return cos128, sin128


QSCALE = HEAD_DIM ** -0.5


def _lane_masks(rows):
    lane = lax.broadcasted_iota(jnp.int32, (rows, 128), 1)
    return (lane % HEAD_DIM) < (HEAD_DIM // 2), [lane < HEAD_DIM, lane >= HEAD_DIM]


def _attn_bias():
    qi = lax.broadcasted_iota(jnp.int32, (4 * BLK, 2 * BLK), 0) % BLK
    kj = lax.broadcasted_iota(jnp.int32, (4 * BLK, 2 * BLK), 1)
    band = (kj > qi) & (kj <= qi + BLK)
    return jnp.stack([jnp.where(band & (kj >= BLK), 0.0, NEG_INF), jnp.where(band, 0.0, NEG_INF)]).astype(F32)


def _attn_prep(proj, cos, sin, *, name):
    T = proj.shape[0]
    tm = _tile(T, 4 * BLK)

    def body(q_ref, k_ref, c_ref, s_ref, qs_ref, kr_ref):
        lt32, halves = _lane_masks(BLK)
        for b in range(tm // BLK):
            rows = pl.ds(b * BLK, BLK)
            cc, sc = c_ref[rows, :], s_ref[rows, :]
            qr = [_rope(q_ref[rows, p * 128:(p + 1) * 128].astype(F32), cc, sc, lt32) * QSCALE for p in range(8)]
            for g in range(N_KV):
                qs_ref[g, pl.ds(4 * b * BLK, 4 * BLK), :] = _stack_heads(qr, g, halves).astype(BF)
            kr_ref[rows, :] = jnp.concatenate([_rope(k_ref[rows, r * 128:(r + 1) * 128].astype(F32), cc, sc, lt32)
                                               for r in range(2)], axis=1).astype(BF)

    tab = pl.BlockSpec((tm, 128), _row)
    return pl.pallas_call(
        body, name=name, grid=(T // tm,),
        in_specs=[pl.BlockSpec((tm, D), lambda n: (n, O_Q // D)), pl.BlockSpec((tm, 256), lambda n: (n, O_K // 256)),
                  tab, tab],
        out_specs=[pl.BlockSpec((N_KV, 4 * tm, 128), lambda n: (0, n, 0)), pl.BlockSpec((tm, 256), _row)],
        out_shape=[jax.ShapeDtypeStruct((N_KV, 4 * T, 128), BF), jax.ShapeDtypeStruct((T, 256), BF)],
        compiler_params=_cp(("parallel",)),
    )(proj, proj, cos, sin)


def _attn_specs():
    prev = lambda n: jnp.maximum(n - 1, 0)
    return [pl.BlockSpec((N_KV, 4 * BLK, 128), lambda n: (0, n, 0)),
            pl.BlockSpec((BLK, 256), _row), pl.BlockSpec((BLK, 256), lambda n: (prev(n), 0)),
            pl.BlockSpec((BLK, 256), lambda n: (n, O_V // 256)),
            pl.BlockSpec((BLK, 256), lambda n: (prev(n), O_V // 256)),
            pl.BlockSpec((1, 4 * BLK, 2 * BLK), lambda n: (jnp.minimum(n, 1), 0, 0)),
            pl.BlockSpec(memory_space=pltpu.SMEM)]


def _bands(kc_ref, kp_ref, vc_ref, vp_ref):
    kb, vb = [], []
    for r in range(2):
        cols = slice(r * 128, (r + 1) * 128)
        kb.append(jnp.concatenate([kp_ref[:, cols], kc_ref[:, cols]], axis=0))
        vb.append(jnp.concatenate([vp_ref[:, cols], vc_ref[:, cols]], axis=0))
    return kb, vb


def _sink_col(sink_ref, g):
    return jnp.concatenate([jnp.full((BLK, 1), sink_ref[4 * g + hh], F32) for hh in range(4)], axis=0)


def _unstack_heads(t, g, halves, acc):
    half = g % 2
    for hh in range(4):
        h = 4 * g + hh
        th = jnp.where(halves[half], t[hh * BLK:(hh + 1) * BLK], 0.0)
        if h % 2 != half:
            th = pltpu.roll(th, HEAD_DIM, 1)
        acc[h // 2] = acc[h // 2] + th


def _stack_heads(chunks, g, halves):
    half = g % 2
    parts = []
    for hh in range(4):
        h = 4 * g + hh
        t = chunks[h // 2]
        if h % 2 != half:
            t = pltpu.roll(t, HEAD_DIM, 1)
        parts.append(jnp.where(halves[half], t, 0.0))
    return jnp.concatenate(parts, axis=0)


def _attn_fwd(qs, kr, proj, bias, sinks, *, name):
    T = proj.shape[0]
    nb = T // BLK

    def body(qs_ref, kc_ref, kp_ref, vc_ref, vp_ref, bias_ref, sink_ref, o_ref, lse_ref):
        _, h128 = _lane_masks(BLK)
        _, h256 = _lane_masks(2 * BLK)
        _, h512 = _lane_masks(4 * BLK)
        kb, vb = _bands(kc_ref, kp_ref, vc_ref, vp_ref)
        outs = [jnp.zeros((BLK, 128), F32) for _ in range(8)]
        groups = range(N_KV)
        bias = bias_ref[0]
        sink = [_sink_col(sink_ref, g) for g in groups]
        s = [lax.dot_general(qs_ref[g], kb[g // 2], NT, preferred_element_type=F32) + bias for g in groups]
        m = [jnp.maximum(jnp.max(s[g], axis=-1, keepdims=True), sink[g]) for g in groups]
        p = [jnp.exp(s[g] - m[g]).astype(BF) for g in groups]
        vg = [jnp.where(h256[g % 2], vb[g // 2].astype(F32), 1.0).astype(BF) for g in groups]
        o = [jnp.dot(p[g], vg[g], preferred_element_type=F32) for g in groups]
        denom = [jnp.where(h512[g % 2], pltpu.roll(o[g], HEAD_DIM, 1), o[g]) + jnp.exp(sink[g] - m[g]) for g in groups]
        for g in groups:
            lse_ref[g] = m[g] + jnp.log(denom[g])
            _unstack_heads(o[g] * (1.0 / denom[g]), g, h128, outs)
        o_ref[...] = jnp.concatenate(outs, axis=1).astype(BF)

    return pl.pallas_call(
        body, name=name, grid=(nb,),
        in_specs=_attn_specs(),
        out_specs=[pl.BlockSpec((BLK, D), _row), pl.BlockSpec((N_KV, 4 * BLK, 128), lambda n: (0, n, 0))],
        out_shape=[jax.ShapeDtypeStruct((T, D), BF), jax.ShapeDtypeStruct((N_KV, 4 * T, 128), F32)],
        compiler_params=_cp(("parallel",)),
    )(qs, kr, kr, proj, proj, bias, sinks)


def _attn_bwd(qs, kr, proj, bias, sinks, lse, o, do, cos, sin, dproj, *, name):
    T = proj.shape[0]
    nb = T // BLK

    def body(qs_ref, kc_ref, kp_ref, vc_ref, vp_ref, bias_ref, sink_ref, lse_ref, o_ref, do_ref,
             cc_ref, sc_ref, cp_ref, sp_ref, dproj_ref, dq_ref, dkc_ref, dkp_ref, dvc_ref, dvp_ref, dsink_ref):
        @pl.when(pl.program_id(0) == 0)
        def _():
            dsink_ref[...] = jnp.zeros_like(dsink_ref)
        lt32, h128 = _lane_masks(BLK)
        kb, vb = _bands(kc_ref, kp_ref, vc_ref, vp_ref)
        oc = [o_ref[:, p * 128:(p + 1) * 128].astype(F32) for p in range(8)]
        doc = [do_ref[:, p * 128:(p + 1) * 128].astype(F32) for p in range(8)]
        dqs = [jnp.zeros((BLK, 128), F32) for _ in range(8)]
        lane1 = lax.broadcasted_iota(jnp.int32, (1, 128), 1)
        dsink = jnp.zeros((1, 128), F32)
        groups = range(N_KV)
        bias = bias_ref[0]
        q = [qs_ref[g] for g in groups]
        lse_g = [lse_ref[g] for g in groups]
        s = [lax.dot_general(q[g], kb[g // 2], NT, preferred_element_type=F32) + bias for g in groups]
        dos = [_stack_heads(doc, g, h128) for g in groups]
        dosb = [t.astype(BF) for t in dos]
        dp = [lax.dot_general(dosb[g], vb[g // 2], NT, preferred_element_type=F32) for g in groups]
        delta = [jnp.sum(dos[g] * _stack_heads(oc, g, h128), axis=-1, keepdims=True) for g in groups]
        p = [jnp.exp(s[g] - jnp.concatenate([lse_g[g], lse_g[g]], axis=1)) for g in groups]
        ds = [(p[g] * (dp[g] - delta[g])).astype(BF) for g in groups]
        pb = [t.astype(BF) for t in p]
        dvg = [lax.dot_general(pb[g], dosb[g], TN, preferred_element_type=F32) for g in groups]
        dkg = [lax.dot_general(ds[g], q[g], TN, preferred_element_type=F32) for g in groups]
        dqg = [jnp.dot(ds[g], kb[g // 2], preferred_element_type=F32) * QSCALE for g in groups]
        dvr = [dvg[0] + dvg[1], dvg[2] + dvg[3]]
        dkr = [dkg[0] + dkg[1], dkg[2] + dkg[3]]
        for g in groups:
            _unstack_heads(dqg[g], g, h128, dqs)
            dsk = -jnp.exp(_sink_col(sink_ref, g) - lse_g[g][:, 0:1]) * delta[g]
            for hh in range(4):
                val = jnp.sum(dsk[hh * BLK:(hh + 1) * BLK], axis=0, keepdims=True)
                dsink = dsink + jnp.where(lane1 == 4 * g + hh, val, 0.0)
        cc, sc, cp, sp = cc_ref[...], sc_ref[...], cp_ref[...], sp_ref[...]
        dsink_ref[...] += dsink
        dq_ref[...] = jnp.concatenate([_rope(t, cc, sc, lt32, inverse=True) for t in dqs], axis=1).astype(BF)
        dkp_ref[...] = jnp.concatenate([_rope(t[:BLK], cp, sp, lt32, inverse=True) for t in dkr], axis=1)
        dkc_ref[...] = jnp.concatenate([_rope(t[BLK:], cc, sc, lt32, inverse=True) for t in dkr], axis=1)
        dvp_ref[...] = jnp.concatenate([t[:BLK] for t in dvr], axis=1)
        dvc_ref[...] = jnp.concatenate([t[BLK:] for t in dvr], axis=1)

    kv = pl.BlockSpec((BLK, 256), _row)
    tc = pl.BlockSpec((BLK, 128), _row)
    tp = pl.BlockSpec((BLK, 128), lambda n: (jnp.maximum(n - 1, 0), 0))
    return pl.pallas_call(
        body, name=name, grid=(nb,),
        in_specs=_attn_specs() + [pl.BlockSpec((N_KV, 4 * BLK, 128), lambda n: (0, n, 0)),
                                  pl.BlockSpec((BLK, D), _row), pl.BlockSpec((BLK, D), _row), tc, tc, tp, tp,
                                  pl.BlockSpec(memory_space=pl.ANY)],
        out_specs=[pl.BlockSpec((BLK, D), lambda n: (n, O_Q // D)), kv, kv, kv, kv, pl.BlockSpec((1, 128), _const2)],
        out_shape=[jax.ShapeDtypeStruct(dproj.shape, BF)] + [jax.ShapeDtypeStruct((T, 256), F32)] * 4
        + [jax.ShapeDtypeStruct((1, 128), F32)],
        input_output_aliases={14: 0},
        compiler_params=_cp(("arbitrary",)),
    )(qs, kr, kr, proj, proj, bias, sinks, lse, o, do, cos, sin, cos, sin, dproj)


def _dkv_combine(dkc, dkp, dvc, dvp, dproj, *, name):
    T = dkc.shape[0]
    nb = T // BLK
    tm = _tile(T, 4 * BLK)
    bpt = tm // BLK
    nt = T // tm

    def body(dkc_ref, dkp_ref, dkn_ref, dvc_ref, dvp_ref, dvn_ref, dproj_ref, o_ref):
        keep = jnp.where(pl.program_id(0) == nt - 1, 0.0, 1.0)

        def shifted(prev_ref, next_ref):
            nxt = keep * next_ref[...]
            return nxt if bpt == 1 else jnp.concatenate([prev_ref[BLK:, :], nxt], axis=0)

        o_ref[:, 0:256] = (dkc_ref[...] + shifted(dkp_ref, dkn_ref)).astype(BF)
        o_ref[:, 256:512] = (dvc_ref[...] + shifted(dvp_ref, dvn_ref)).astype(BF)

    cur = pl.BlockSpec((tm, 256), _row)
    nxt = pl.BlockSpec((BLK, 256), lambda i: (jnp.minimum((i + 1) * bpt, nb - 1), 0))
    return pl.pallas_call(
        body, name=name, grid=(nt,),
        in_specs=[cur, cur, nxt, cur, cur, nxt, pl.BlockSpec(memory_space=pl.ANY)],
        out_specs=pl.BlockSpec((tm, 512), lambda i: (i, O_K // 512)),
        out_shape=jax.ShapeDtypeStruct(dproj.shape, BF),
        input_output_aliases={6: 0},
        compiler_params=_cp(("parallel",)),
    )(dkc, dkp, dkp, dvc, dvp, dvp, dproj)


HALO = 16


def _conv_shifts(cu, hprev, tm):
    row = lax.broadcasted_iota(jnp.int32, cu.shape, 0)
    h1 = hprev[HALO - 1:HALO, :]
    h2 = hprev[HALO - 2:HALO - 1, :]
    m1 = jnp.where(row == 0, h1, pltpu.roll(cu, 1, 0))
    m2 = jnp.where(row == 0, h2, jnp.where(row == 1, h1, pltpu.roll(cu, 2, 0)))
    return m1, m2


def _mixer_mid_fwd(proj, attn, wcp, wap, wout, convw, x, gt, *, tm, name):
    T = x.shape[0]
    tm = _tile(T, tm)
    hb = tm // HALO

    def body(bg_ref, cg_ref, u_ref, hcg_ref, hu_ref, zc0_ref, zc1_ref, za0_ref, za1_ref, at_ref,
             wcp_ref, wap_ref, wout_ref, cw_ref, x_ref, gt_ref,
             x2_ref, gc_ref, yc_ref, ya_ref, mg_ref, o_ref):
        first = jnp.where(pl.program_id(0) == 0, 0.0, 1.0)
        cu = cg_ref[...].astype(F32) * u_ref[...].astype(F32)
        hprev = first * (hcg_ref[...].astype(F32) * hu_ref[...].astype(F32))
        m1, m2 = _conv_shifts(cu, hprev, tm)
        cv = cw_ref[0:1, :] * m2 + cw_ref[1:2, :] * m1 + cw_ref[2:3, :] * cu
        gc = (bg_ref[...].astype(F32) * cv).astype(BF)
        gc_ref[...] = gc
        yc = jnp.dot(gc, wcp_ref[...], preferred_element_type=F32)
        ya = jnp.dot(at_ref[...], wap_ref[...], preferred_element_type=F32)
        yc_ref[...] = yc.astype(BF)
        ya_ref[...] = ya.astype(BF)
        zc = jnp.concatenate([zc0_ref[...], zc1_ref[...]], axis=1).astype(F32)
        za = jnp.concatenate([za0_ref[...], za1_ref[...]], axis=1).astype(F32)
        mg = (_sigmoid(zc) * yc + _sigmoid(za) * ya).astype(BF)
        mg_ref[...] = mg
        o = jnp.dot(mg, wout_ref[...], preferred_element_type=F32)
        o_ref[...] = o.astype(BF)
        x2_ref[...] = x_ref[...] + gt_ref[...] * o

    wspec = pl.BlockSpec((D, D), _const2)
    rowspec = pl.BlockSpec((tm, D), _row)
    return pl.pallas_call(
        body, name=name, grid=(T // tm,),
        in_specs=[_col(tm, O_BG), _col(tm, O_CG), _col(tm, O_U), _halo_prev(hb, O_CG), _halo_prev(hb, O_U),
                  _col(tm, O_ZC, 512), _col(tm, O_ZC + 512, 512), _col(tm, O_ZA, 512), _col(tm, O_ZA + 512, 512),
                  rowspec, wspec, wspec, wspec, pl.BlockSpec((8, D), _const2), rowspec, pl.BlockSpec((1, D), _const2)],
        out_specs=[rowspec] * 6,
        out_shape=[jax.ShapeDtypeStruct((T, D), F32)] + [jax.ShapeDtypeStruct((T, D), BF)] * 5,
        compiler_params=_cp(("parallel",)),
    )(proj, proj, proj, proj, proj, proj, proj, proj, proj, attn, wcp, wap, wout, convw, x, gt)


def _col(tm, c, w=D):
    assert c % w == 0
    return pl.BlockSpec((tm, w), lambda i: (i, c // w))


def _halo_prev(hb, c):
    return pl.BlockSpec((HALO, D), lambda i: (jnp.maximum(i * hb - 1, 0), c // D))


def _halo_next(hb, nblk, c=0):
    return pl.BlockSpec((HALO, D), lambda i: (jnp.minimum((i + 1) * hb, nblk - 1), c // D))


def _mixer_mid_bwd(dx2, gt, o, proj, yc, ya, wout, wcp, wap, *, tm, name):
    T = dx2.shape[0]
    tm = _tile(T, tm)
    zw = 512
    nz = 2 * D // zw

    def body(dx_ref, gt_ref, o_ref, zc0_ref, zc1_ref, za0_ref, za1_ref, yc_ref, ya_ref, wout_ref, wcp_ref, wap_ref,
             dout_ref, dyc_ref, dya_ref, dgc_ref, dat_ref, dz_ref, dgt_ref, dzs):
        i, j = pl.program_id(0), pl.program_id(1)

        @pl.when(jnp.logical_and(i == 0, j == 0))
        def _():
            dgt_ref[...] = jnp.zeros_like(dgt_ref)

        @pl.when(j == 0)
        def _():
            dxv = dx_ref[...]
            dgt_ref[...] += jnp.sum(dxv * o_ref[...].astype(F32), axis=0, keepdims=True)
            dout = (gt_ref[...] * dxv).astype(BF)
            dout_ref[...] = dout
            dmg = lax.dot_general(dout, wout_ref[...], NT, preferred_element_type=F32)
            sc = _sigmoid(jnp.concatenate([zc0_ref[...], zc1_ref[...]], axis=1).astype(F32))
            sa = _sigmoid(jnp.concatenate([za0_ref[...], za1_ref[...]], axis=1).astype(F32))
            dyc = (dmg * sc).astype(BF)
            dya = (dmg * sa).astype(BF)
            dyc_ref[...] = dyc
            dya_ref[...] = dya
            dzs[:, 0:D] = (dmg * yc_ref[...].astype(F32) * (sc * (1.0 - sc))).astype(BF)
            dzs[:, D:2 * D] = (dmg * ya_ref[...].astype(F32) * (sa * (1.0 - sa))).astype(BF)
            dgc_ref[...] = lax.dot_general(dyc, wcp_ref[...], NT, preferred_element_type=F32).astype(BF)
            dat_ref[...] = lax.dot_general(dya, wap_ref[...], NT, preferred_element_type=F32).astype(BF)

        for jj in range(nz):
            @pl.when(j == jj)
            def _(jj=jj):
                dz_ref[...] = dzs[:, jj * zw:(jj + 1) * zw]

    nt = T // tm

    def ahead(i, j):
        return jnp.minimum(i + jnp.minimum(j, 1), nt - 1)

    def zcol(c):
        return pl.BlockSpec((tm, zw), lambda i, j: (ahead(i, j), c // zw))

    wspec = pl.BlockSpec((D, D), _const2)
    rowin = pl.BlockSpec((tm, D), lambda i, j: (ahead(i, j), 0))
    rowspec = pl.BlockSpec((tm, D), lambda i, j: (i, 0))
    vec = pl.BlockSpec((1, D), _const2)
    return pl.pallas_call(
        body, name=name, grid=(nt, nz),
        in_specs=[rowin, vec, rowin, zcol(O_ZC), zcol(O_ZC + zw), zcol(O_ZA), zcol(O_ZA + zw),
                  rowin, rowin, wspec, wspec, wspec],
        out_specs=[rowspec] * 5 + [pl.BlockSpec((tm, zw), lambda i, j: (i, O_ZC // zw + j)), vec],
        out_shape=[jax.ShapeDtypeStruct((T, D), BF)] * 5 + [jax.ShapeDtypeStruct((T, NIN), BF),
                                                            jax.ShapeDtypeStruct((1, D), F32)],
        scratch_shapes=[pltpu.VMEM((tm, 2 * D), BF)],
        compiler_params=_cp(("arbitrary", "arbitrary")),
    )(dx2, gt, o, proj, proj, proj, proj, yc, ya, wout, wcp, wap)


def _conv_bwd(dgc, proj, convw, dproj, *, tm, name):
    T = dgc.shape[0]
    tm = _tile(T, tm)
    hb = tm // HALO
    nblk = T // HALO
    nt = T // tm

    def body(dgc_ref, ndgc_ref, bg_ref, nbg_ref, cg_ref, u_ref, hcg_ref, hu_ref, cw_ref, dproj_ref, dp_ref, dcw_ref):
        i = pl.program_id(0)

        @pl.when(i == 0)
        def _():
            dcw_ref[...] = jnp.zeros_like(dcw_ref)
        first = jnp.where(i == 0, 0.0, 1.0)
        last = jnp.where(i == nt - 1, 0.0, 1.0)
        cg = cg_ref[...].astype(F32)
        u = u_ref[...].astype(F32)
        bg = bg_ref[...].astype(F32)
        dg = dgc_ref[...].astype(F32)
        cu = cg * u
        hprev = first * (hcg_ref[...].astype(F32) * hu_ref[...].astype(F32))
        m1, m2 = _conv_shifts(cu, hprev, tm)
        w0, w1, w2 = cw_ref[0:1, :], cw_ref[1:2, :], cw_ref[2:3, :]
        cv = w0 * m2 + w1 * m1 + w2 * cu
        dcv = dg * bg
        nxt = last * (ndgc_ref[...].astype(F32) * nbg_ref[...].astype(F32))
        n0, n1 = nxt[0:1, :], nxt[1:2, :]
        row = lax.broadcasted_iota(jnp.int32, dcv.shape, 0)
        p1 = jnp.where(row == tm - 1, n0, pltpu.roll(dcv, tm - 1, 0))
        p2 = jnp.where(row == tm - 1, n1, jnp.where(row == tm - 2, n0, pltpu.roll(dcv, tm - 2, 0)))
        dcu = w2 * dcv + w1 * p1 + w0 * p2
        dp_ref[:, 0:D] = (dg * cv).astype(BF)
        dp_ref[:, D:2 * D] = (dcu * u).astype(BF)
        dp_ref[:, 2 * D:3 * D] = (dcu * cg).astype(BF)
        dcw_ref[0:1, :] += jnp.sum(dcv * m2, axis=0, keepdims=True)
        dcw_ref[1:2, :] += jnp.sum(dcv * m1, axis=0, keepdims=True)
        dcw_ref[2:3, :] += jnp.sum(dcv * cu, axis=0, keepdims=True)

    rowspec = pl.BlockSpec((tm, D), _row)
    cw = pl.BlockSpec((8, D), _const2)
    return pl.pallas_call(
        body, name=name, grid=(nt,),
        in_specs=[rowspec, _halo_next(hb, nblk), _col(tm, O_BG), _halo_next(hb, nblk, O_BG),
                  _col(tm, O_CG), _col(tm, O_U), _halo_prev(hb, O_CG), _halo_prev(hb, O_U), cw,
                  pl.BlockSpec(memory_space=pl.ANY)],
        out_specs=[pl.BlockSpec((tm, 3 * D), _row), cw],
        out_shape=[jax.ShapeDtypeStruct(dproj.shape, BF), jax.ShapeDtypeStruct((8, D), F32)],
        input_output_aliases={9: 0},
        compiler_params=_cp(("arbitrary",)),
    )(dgc, dgc, proj, proj, proj, proj, proj, proj, convw, dproj)


def _adam(w, g, m, v, *, tm, name):
    _, R, C = w.shape
    tm = _tile(R, tm)
    parts = g.ndim == 3
    c1 = 1.0 - ADAM_B1
    c2 = 1.0 - ADAM_B2
    bc1 = 1.0 - ADAM_B1 ** ADAM_STEP
    bc2 = 1.0 - ADAM_B2 ** ADAM_STEP

    def body(w_ref, g_ref, m_ref, v_ref, go_ref, d_ref, nm_ref, nv_ref):
        if parts:
            gv = g_ref[0].astype(F32)
            for s in range(1, N_DEV):
                gv = gv + g_ref[s].astype(F32)
        else:
            gv = g_ref[...]
        go_ref[0] = gv
        nm = ADAM_B1 * m_ref[0] + c1 * gv
        nv = ADAM_B2 * v_ref[0] + c2 * (gv * gv)
        nm_ref[0] = nm
        nv_ref[0] = nv
        d_ref[0] = -ADAM_LR * ((nm / bc1) / (jnp.sqrt(nv / bc2) + ADAM_EPS) + ADAM_WD * w_ref[0])

    spec = pl.BlockSpec((1, tm, C), lambda i: (0, i, 0))
    gspec = pl.BlockSpec((N_DEV, tm, C), lambda i: (0, i, 0)) if parts else pl.BlockSpec((tm, C), _row)
    return pl.pallas_call(
        body, name=name, grid=(R // tm,),
        in_specs=[spec, gspec, spec, spec], out_specs=[spec] * 4,
        out_shape=[jax.ShapeDtypeStruct((1, R, C), F32)] * 4,
        compiler_params=_cp(("parallel",)),
    )(w, g, m, v)


def _mods_part(c_all, w_ada, b_ada, *, name):
    C = w_ada.shape[1]

    def body(c_ref, w_ref, b_ref, o_ref):
        cv = c_ref[...]
        ca = cv * jax.nn.sigmoid(cv)
        o_ref[...] = jnp.dot(ca, w_ref[...], preferred_element_type=F32,
                             precision=lax.Precision.HIGHEST) + b_ref[...]

    return pl.pallas_call(
        body, name=name,
        out_shape=jax.ShapeDtypeStruct((N_DEV, C), F32),
        compiler_params=_cp(),
    )(c_all, w_ada, b_ada)


def _wada_grad(c_all_t, gm, *, name):
    C = gm.shape[1]

    def body(c_ref, g_ref, o_ref):
        cv = c_ref[...]
        ca = cv * jax.nn.sigmoid(cv)
        acc = ca[:, 0:1] * g_ref[0:1, :]
        for b in range(1, N_DEV):
            acc = acc + ca[:, b:b + 1] * g_ref[b:b + 1, :]
        o_ref[...] = acc

    return pl.pallas_call(
        body, name=name,
        out_shape=jax.ShapeDtypeStruct((D, C), F32),
        compiler_params=_cp(),
    )(c_all_t, gm)


def _peer(x, y, c, d):
    px = lax.rem(x + ((d >> 2) & 1), 2)
    py = lax.rem(y + ((d >> 1) & 1), 2)
    pc = lax.rem(c + (d & 1), 2)
    return (px, py, pc), 4 * px + 2 * py + pc


def _exchange(xs, *, scatter, name):
    n = len(xs)
    nsem = n * (N_DEV - 1)

    def body(*refs):
        ins, outs = refs[:n], refs[n:2 * n]
        token, send_sems, recv_sems, local_sems = refs[2 * n:]
        x, y, c = lax.axis_index("x"), lax.axis_index("y"), lax.axis_index("c")
        me = 4 * x + 2 * y + c
        token[...] = jnp.zeros_like(token)

        def src(t, idx):
            return ins[t].at[idx] if scatter else ins[t]

        local = [pltpu.make_async_copy(src(t, me), outs[t].at[me], local_sems.at[t]) for t in range(n)]
        for cp in local:
            cp.start()
        remote = []
        for t in range(n):
            for d in range(1, N_DEV):
                peer, pidx = _peer(x, y, c, d)
                k = t * (N_DEV - 1) + d - 1
                send = pltpu.make_async_remote_copy(src_ref=src(t, pidx), dst_ref=outs[t].at[me],
                                                    send_sem=send_sems.at[k], recv_sem=recv_sems.at[k],
                                                    device_id=peer, device_id_type=MESH)
                recv = pltpu.make_async_remote_copy(src_ref=src(t, pidx), dst_ref=outs[t].at[pidx],
                                                    send_sem=send_sems.at[k], recv_sem=recv_sems.at[k],
                                                    device_id=peer, device_id_type=MESH)
                send.start()
                remote.append((send, recv))
        for cp in local:
            cp.wait()
        for send, recv in remote:
            send.wait_send()
            recv.wait_recv()

    anyspec = pl.BlockSpec(memory_space=pl.ANY)
    out_shape = [jax.ShapeDtypeStruct(a.shape if scatter else (N_DEV,) + a.shape, a.dtype) for a in xs]
    out_shape.append(jax.ShapeDtypeStruct((8, 128), F32))
    return pl.pallas_call(
        body, name=name,
        in_specs=[anyspec] * n, out_specs=[anyspec] * n + [pl.BlockSpec(memory_space=pltpu.VMEM)],
        out_shape=out_shape,
        scratch_shapes=[pltpu.SemaphoreType.DMA((nsem,)), pltpu.SemaphoreType.DMA((nsem,)),
                        pltpu.SemaphoreType.DMA((n,))],
    )(*xs)


def _sum8(parts, *, name):
    _, R, C = parts.shape

    def body(p_ref, o_ref):
        acc = p_ref[0]
        for s in range(1, N_DEV):
            acc = acc + p_ref[s]
        o_ref[...] = acc

    return pl.pallas_call(body, name=name, out_shape=jax.ShapeDtypeStruct((R, C), F32),
                          compiler_params=_cp())(parts)


HBM_SPEC = pl.BlockSpec(memory_space=pltpu.HBM)
SEM_SPEC = pl.BlockSpec(memory_space=pltpu.SEMAPHORE)
N_PEER = N_DEV - 1


def _split_copies(src_refs, land_refs, send_sems, recv_sems, scatter):
    x, y, c = lax.axis_index("x"), lax.axis_index("y"), lax.axis_index("c")
    me = 4 * x + 2 * y + c
    pairs = []
    for j, (src, land) in enumerate(zip(src_refs, land_refs)):
        for d in range(1, N_DEV):
            peer, pidx = _peer(x, y, c, d)
            k = j * N_PEER + d - 1
            s = src.at[pidx] if scatter else src
            send = pltpu.make_async_remote_copy(src_ref=s, dst_ref=land.at[me], send_sem=send_sems.at[k],
                                                recv_sem=recv_sems.at[k], device_id=peer, device_id_type=MESH)
            recv = pltpu.make_async_remote_copy(src_ref=s, dst_ref=land.at[pidx], send_sem=send_sems.at[k],
                                                recv_sem=recv_sems.at[k], device_id=peer, device_id_type=MESH)
            pairs.append((send, recv))
    return pairs


def _own_slot(block, me):
    land = lax.empty((N_DEV,) + block.shape, block.dtype)
    return lax.dynamic_update_slice(land, block[None], (me, 0, 0))


def _split_start(srcs, lands, groups, *, scatter, name):
    n, ng = len(srcs), len(groups)

    def body(*refs):
        src_refs, land_refs = refs[:n], refs[n:2 * n]
        sems = refs[2 * n:2 * n + 2 * ng]
        token = refs[-1]
        for gi, g in enumerate(groups):
            pairs = _split_copies([src_refs[t] for t in g], [land_refs[t] for t in g], sems[2 * gi],
                                  sems[2 * gi + 1], scatter)
            for send, _ in pairs:
                send.start()
        token[...] = jnp.zeros_like(token)

    sem_shapes = []
    for g in groups:
        sem_shapes += [pltpu.SemaphoreType.DMA((len(g) * N_PEER,))] * 2
    thru = [pltpu.HBM(a.shape, a.dtype) for a in list(srcs) + list(lands)]
    outs = pl.pallas_call(
        body, name=name,
        out_shape=tuple(sem_shapes + thru + [jax.ShapeDtypeStruct((8, 128), F32)]),
        in_specs=[HBM_SPEC] * (2 * n),
        out_specs=tuple([SEM_SPEC] * (2 * ng) + [HBM_SPEC] * (2 * n) + [pl.BlockSpec(memory_space=pltpu.VMEM)]),
        input_output_aliases={i: 2 * ng + i for i in range(2 * n)},
        compiler_params=pltpu.CompilerParams(has_side_effects=pltpu.SideEffectType.DATAFLOW_SIDE_EFFECTING),
    )(*[pltpu.with_memory_space_constraint(a, pltpu.HBM) for a in list(srcs) + list(lands)])
    sems = [(outs[2 * gi], outs[2 * gi + 1]) for gi in range(ng)]
    return sems, outs[2 * ng:2 * ng + n], outs[2 * ng + n:2 * ng + 2 * n], outs[-1]


def _behind(v, token):
    if token is None:
        return v
    return v + token[0, 0].astype(v.dtype)


def _split_wait(srcs, lands, sems, after, *, scatter, name):
    m = len(srcs)

    def body(*refs):
        src_refs, land_refs = refs[:m], refs[m:2 * m]
        send_sems, recv_sems = refs[2 * m], refs[2 * m + 1]
        for send, recv in _split_copies(src_refs, land_refs, send_sems, recv_sems, scatter):
            send.wait_send()
            recv.wait_recv()

    outs = pl.pallas_call(
        body, name=name,
        out_shape=tuple(pltpu.HBM(a.shape, a.dtype) for a in list(srcs) + list(lands)),
        in_specs=[HBM_SPEC] * (2 * m) + [SEM_SPEC, SEM_SPEC, pl.BlockSpec(memory_space=pl.ANY)],
        out_specs=tuple([HBM_SPEC] * (2 * m)),
        input_output_aliases={i: i for i in range(2 * m)},
        compiler_params=pltpu.CompilerParams(has_side_effects=pltpu.SideEffectType.DATAFLOW_SIDE_EFFECTING),
    )(*srcs, *lands, sems[0], sems[1], after)
    return outs[m:]


TL_FIRST = (1, 2, 4, 6)
TL_ICI = (2, 4, 6)
EFFECT = pltpu.SideEffectType.DATAFLOW_SIDE_EFFECTING


def _tl_first(src_refs, land_refs, send_sems, recv_sems):
    x, y, c = lax.axis_index("x"), lax.axis_index("y"), lax.axis_index("c")
    me = 4 * x + 2 * y + c
    out = []
    for j, (src, land) in enumerate(zip(src_refs, land_refs)):
        for i, d in enumerate(TL_FIRST):
            peer, pidx = _peer(x, y, c, d)
            k = len(TL_FIRST) * j + i
            send = pltpu.make_async_remote_copy(src_ref=src, dst_ref=land.at[me], send_sem=send_sems.at[k],
                                                recv_sem=recv_sems.at[k], device_id=peer, device_id_type=MESH)
            recv = pltpu.make_async_remote_copy(src_ref=src, dst_ref=land.at[pidx], send_sem=send_sems.at[k],
                                                recv_sem=recv_sems.at[k], device_id=peer, device_id_type=MESH)
            out.append((d, send, recv))
    return out


def _tl_second(land_refs, send_sems, recv_sems):
    x, y, c = lax.axis_index("x"), lax.axis_index("y"), lax.axis_index("c")
    sibling, _ = _peer(x, y, c, 1)
    out = []
    for j, land in enumerate(land_refs):
        for i, d in enumerate(TL_ICI):
            _, mine = _peer(x, y, c, d)
            _, theirs = _peer(x, y, c, d + 1)
            k = len(TL_ICI) * j + i
            send = pltpu.make_async_remote_copy(src_ref=land.at[mine], dst_ref=land.at[mine], send_sem=send_sems.at[k],
                                                recv_sem=recv_sems.at[k], device_id=sibling, device_id_type=MESH)
            recv = pltpu.make_async_remote_copy(src_ref=land.at[mine], dst_ref=land.at[theirs],
                                                send_sem=send_sems.at[k], recv_sem=recv_sems.at[k],
                                                device_id=sibling, device_id_type=MESH)
            out.append((send, recv))
    return out


def _tl_start(srcs, lands, groups, *, name):
    n, ng = len(srcs), len(groups)

    def body(*refs):
        src_refs, land_refs = refs[:n], refs[n:2 * n]
        sems = refs[2 * n:2 * n + 2 * ng]
        for gi, g in enumerate(groups):
            for _, send, _ in _tl_first([src_refs[t] for t in g], [land_refs[t] for t in g], sems[2 * gi],
                                        sems[2 * gi + 1]):
                send.start()
        refs[-1][...] = jnp.zeros_like(refs[-1])

    sem_shapes = []
    for g in groups:
        sem_shapes += [pltpu.SemaphoreType.DMA((len(g) * len(TL_FIRST),))] * 2
    thru = [pltpu.HBM(a.shape, a.dtype) for a in list(srcs) + list(lands)]
    outs = pl.pallas_call(
        body, name=name,
        out_shape=tuple(sem_shapes + thru + [jax.ShapeDtypeStruct((8, 128), F32)]),
        in_specs=[HBM_SPEC] * (2 * n),
        out_specs=tuple([SEM_SPEC] * (2 * ng) + [HBM_SPEC] * (2 * n) + [pl.BlockSpec(memory_space=pltpu.VMEM)]),
        input_output_aliases={i: 2 * ng + i for i in range(2 * n)},
        compiler_params=pltpu.CompilerParams(has_side_effects=EFFECT),
    )(*[pltpu.with_memory_space_constraint(a, pltpu.HBM) for a in list(srcs) + list(lands)])
    sems = [(outs[2 * gi], outs[2 * gi + 1]) for gi in range(ng)]
    return sems, outs[2 * ng:2 * ng + n], outs[2 * ng + n:2 * ng + 2 * n], outs[-1]


def _tl_forward(srcs, lands, sems1, after, *, name):
    m = len(srcs)

    def body(*refs):
        src_refs, land_refs = refs[:m], refs[m:2 * m]
        send1, recv1 = refs[2 * m], refs[2 * m + 1]
        send2, recv2 = refs[2 * m + 3], refs[2 * m + 4]
        for d, _, recv in _tl_first(src_refs, land_refs, send1, recv1):
            if d in TL_ICI:
                recv.wait_recv()
        for send, _ in _tl_second(land_refs, send2, recv2):
            send.start()

    sem = pltpu.SemaphoreType.DMA((m * len(TL_ICI),))
    outs = pl.pallas_call(
        body, name=name,
        out_shape=tuple([sem, sem] + [pltpu.HBM(a.shape, a.dtype) for a in list(srcs) + list(lands)]),
        in_specs=[HBM_SPEC] * (2 * m) + [SEM_SPEC, SEM_SPEC, pl.BlockSpec(memory_space=pl.ANY)],
        out_specs=tuple([SEM_SPEC, SEM_SPEC] + [HBM_SPEC] * (2 * m)),
        input_output_aliases={i: 2 + i for i in range(2 * m)},
        compiler_params=pltpu.CompilerParams(has_side_effects=EFFECT),
    )(*srcs, *lands, sems1[0], sems1[1], after)
    return (outs[0], outs[1]), outs[2:2 + m], outs[2 + m:2 + 2 * m]


def _tl_wait(srcs, lands, sems1, sems2, after, *, name):
    m = len(srcs)

    def body(*refs):
        src_refs, land_refs = refs[:m], refs[m:2 * m]
        send1, recv1, send2, recv2 = refs[2 * m:2 * m + 4]
        for d, send, recv in _tl_first(src_refs, land_refs, send1, recv1):
            send.wait_send()
            if d not in TL_ICI:
                recv.wait_recv()
        for send, recv in _tl_second(land_refs, send2, recv2):
            send.wait_send()
            recv.wait_recv()

    outs = pl.pallas_call(
        body, name=name,
        out_shape=tuple(pltpu.HBM(a.shape, a.dtype) for a in list(srcs) + list(lands)),
        in_specs=[HBM_SPEC] * (2 * m) + [SEM_SPEC] * 4 + [pl.BlockSpec(memory_space=pl.ANY)],
        out_specs=tuple([HBM_SPEC] * (2 * m)),
        input_output_aliases={i: i for i in range(2 * m)},
        compiler_params=pltpu.CompilerParams(has_side_effects=EFFECT),
    )(*srcs, *lands, sems1[0], sems1[1], sems2[0], sems2[1], after)
    return outs[m:]


TM_PROJ = 512
TN_PROJ = 512
TM_ROW = 512
TM_NN = 512
TK_TN = 2048
TM_ADAM = 416
TN_FFN = F // 2
TN_IN = NIN // 4


def _tn(a, b, name, tn):
    if a.ndim == 2:
        a = a[None]
    return _tn_matmul(a, b, tn=tn, tk=TK_TN, name=name)


def _local_step(x, tgt, mods, g1, gm, g2, gf, convw8, sinks, w_get, g_put):
    T = x.shape[0]
    sh1, sc1, gt1, sh2, sc2, gt2, sh3, sc3, gt3 = [mods[i:i + 1] for i in range(N_MOD)]
    cos, sin = _rope_tables(T)
    behind = _behind

    w = dict(w_get("gu1", mods))
    h1, ab1 = _norm_proj(x, g1, sc1, sh1, w["gu1"], tm=TM_PROJ, tn=TN_PROJ, name="ffn1_up")
    w.update(w_get("d1", ab1))
    x1, y1 = _ffn_down_fwd(ab1, w["d1"], x, gt1, tm=TM_ROW, name="ffn1_down")
    w.update(w_get("mix", x1))
    h2, proj = _norm_proj(x1, gm, sc2, sh2, w["win"], tm=TM_PROJ, tn=TN_PROJ, name="mix_in")
    qs, kr = _attn_prep(proj, cos, sin, name="attn_prep")
    bias = _attn_bias()
    attn, lse = _attn_fwd(qs, kr, proj, bias, sinks, name="attn_fwd")
    x2, gc, yc, ya, mg, o = _mixer_mid_fwd(proj, attn, w["cp"], w["ap"], w["out"], convw8, x1, gt2,
                                           tm=TM_ROW, name="mix_mid")
    w.update(w_get("ffn2", x2))
    h3, ab2 = _norm_proj(x2, g2, sc3, sh3, w["gu2"], tm=TM_PROJ, tn=TN_PROJ, name="ffn2_up")
    x3, y2 = _ffn_down_fwd(ab2, w["d2"], x2, gt3, tm=TM_ROW, name="ffn2_down")
    dx3, lsum, dgf = _final_fwd_bwd(x3, tgt, gf, tm=TM_ROW, name="final")

    dy2, dab2, dgt3 = _ffn_down_bwd(dx3, y2, gt3, ab2, w["d2"], tm=TM_ROW, tn=MXU_N, name="ffn2_down_bwd")
    g_d2 = _tn_matmul_swiglu(ab2, dy2, None, tn=TN_FFN, tk=TK_TN, name="ffn2_down_dw")
    dx2, dsh3, dsc3, dg2 = _nn_bwd_norm(dab2, w["gu2"], x2, g2, sc3, dx3, tm=TM_NN, name="ffn2_up_bwd")
    g_gu2 = _tn(dab2, h3, "ffn2_up_dw", TN_FFN)
    tok = g_put(dict(gu2=g_gu2, d2=g_d2))

    dout, dyc, dya, dgc, dat, dproj, dgt2 = _mixer_mid_bwd(dx2, behind(gt2, tok), o, proj, yc, ya, w["out"], w["cp"],
                                                           w["ap"], tm=TM_ROW, name="mix_mid_bwd")
    g_out = _tn(mg, dout, "mix_out_dw", D)
    g_cp = _tn(gc, dyc, "mix_cp_dw", D)
    g_ap = _tn(attn, dya, "mix_ap_dw", D)
    dproj, dkc, dkp, dvc, dvp, dsink = _attn_bwd(qs, kr, proj, bias, sinks, lse, attn, dat, cos, sin, dproj,
                                                 name="attn_bwd")
    dproj = _dkv_combine(dkc, dkp, dvc, dvp, dproj, name="attn_dkv")
    dproj, dcw = _conv_bwd(dgc, proj, convw8, dproj, tm=TM_ROW, name="conv_bwd")
    g_in = _tn(dproj, h2, "mix_in_dw", TN_IN)
    tok = g_put(dict(win=g_in, cp=g_cp, ap=g_ap, out=g_out))
    dx1, dsh2, dsc2, dgm = _nn_bwd_norm(dproj[None], w["win"], x1, gm, behind(sc2, tok), dx2, tm=TM_NN,
                                        name="mix_in_bwd")

    dy1, dab1, dgt1 = _ffn_down_bwd(dx1, y1, gt1, ab1, w["d1"], tm=TM_ROW, tn=MXU_N, name="ffn1_down_bwd")
    g_gu1 = _tn(dab1, h1, "ffn1_up_dw", TN_FFN)
    tok = g_put(dict(gu1=g_gu1))
    g_d1 = _tn_matmul_swiglu(ab1, dy1, tok, tn=TN_FFN, tk=TK_TN, name="ffn1_down_dw")
    tok = g_put(dict(d1=g_d1))
    dx0, dsh1, dsc1, dg1 = _nn_bwd_norm(dab1, w["gu1"], x, g1, behind(sc1, tok), dx1, tm=TM_NN,
                                        name="ffn1_up_bwd")

    small = dict(mods=jnp.concatenate([dsh1, dsc1, dgt1, dsh2, dsc2, dgt2, dsh3, dsc3, dgt3], axis=0),
                 g1=dg1, gm=dgm, g2=dg2, gf=dgf, convw=dcw[0:3], sinks=dsink[:, 0:N_HEADS])
    return lsum, dx0, small


BIG = ("gu1", "d1", "win", "cp", "ap", "out", "gu2", "d2")
TRANSPOSED = ("gu1", "win", "gu2")
SMALL_ROWS = 24
R_MODS, R_G1, R_GM, R_G2, R_GF, R_CONV, R_SINK = 0, 9, 10, 11, 12, 13, 16


def _pad_to(a, rows, cols):
    return jnp.pad(a, ((0, rows - a.shape[0]), (0, cols - a.shape[1])))


def _pack_small(b_ada, g1, gm, g2, gf, conv, sinks):
    rows = [b_ada.reshape(N_MOD, D), g1.reshape(1, D), gm.reshape(1, D), g2.reshape(1, D), gf.reshape(1, D),
            _pad_to(conv.reshape(3, -1), 3, D), _pad_to(sinks.reshape(1, N_HEADS), 1, D)]
    return _pad_to(jnp.concatenate(rows, axis=0), SMALL_ROWS, D)


def _unpack_small(p, conv_cols):
    return dict(b_ada=p[R_MODS:R_MODS + N_MOD].reshape(1, N_MOD * D), g_ffn1=p[R_G1:R_G1 + 1],
                g_mix=p[R_GM:R_GM + 1], g_ffn2=p[R_G2:R_G2 + 1], g_final=p[R_GF],
                conv_w=p[R_CONV:R_CONV + 3, 0:conv_cols][None], sinks=p[R_SINK:R_SINK + 1, 0:N_HEADS])


def kernel(x, c, w_ada, b_ada, g_ffn1, w1_gu, w1_down, g_mix, w_in, conv_w, w_conv_proj, w_attn_proj, sinks, w_out, g_ffn2, w2_gu, w2_down, g_final, loss_target, m_w_ada, m_b_ada, m_g_ffn1, m_w1_gu, m_w1_down, m_g_mix, m_w_in, m_conv_w, m_w_conv_proj, m_w_attn_proj, m_sinks, m_w_out, m_g_ffn2, m_w2_gu, m_w2_down, m_g_final, v_w_ada, v_b_ada, v_g_ffn1, v_w1_gu, v_w1_down, v_g_mix, v_w_in, v_conv_w, v_w_conv_proj, v_w_attn_proj, v_sinks, v_w_out, v_g_ffn2, v_w2_gu, v_w2_down, v_g_final):
    me = 4 * lax.axis_index("x") + 2 * lax.axis_index("y") + lax.axis_index("c")
    ada_cols = w_ada.shape[2]
    conv_cols = conv_w.shape[2]

    native = dict(gu1=w1_gu[0], d1=w1_down[0], win=w_in[0], cp=w_conv_proj[0], ap=w_attn_proj[0], out=w_out[0],
                  gu2=w2_gu[0], d2=w2_down[0])

    def shard(n, token):
        a = _behind(native[n], token)
        return (a.T if n in TRANSPOSED else a).astype(BF)

    c_all, conv_all, _ = _exchange([c, _pad_to(conv_w[0], 8, conv_cols)], scatter=False, name="gather_cond")
    c_all = c_all.reshape(N_DEV, D)
    conv_full = conv_all[:, 0:3, :].transpose(1, 0, 2).reshape(3, D)

    b_cols = lax.dynamic_slice(b_ada, (0, me * ada_cols), (1, ada_cols))
    mods_cols = _mods_part(c_all, w_ada[0], b_cols, name="ada_mods")
    mods_all, mods_token = _exchange([mods_cols], scatter=False, name="gather_mods")
    mods = lax.dynamic_index_in_dim(mods_all, me, axis=1, keepdims=False).reshape(N_MOD, D)

    groups = dict(gu1=("gu1",), d1=("d1",), mix=("win", "cp", "ap", "out"), ffn2=("gu2", "d2"))
    in_flight = {}
    first = [shard("gu1", mods_token)]
    sems, srcs, lands, token = _tl_start(first, [_own_slot(s, me) for s in first], [[0]],
                                         name="gather_weights_start_gu1")
    in_flight["gu1"] = [sems[0], srcs, lands, None]
    rest = [n for n in BIG if n != "gu1"]
    shards = [shard(n, token) for n in rest]
    rest_groups = [[rest.index(n) for n in names] for g, names in groups.items() if g != "gu1"]
    sems, srcs, lands, rest_token = _tl_start(shards, [_own_slot(s, me) for s in shards], rest_groups,
                                              name="gather_weights_start_rest")
    for (g, names), gsems, idx in zip([kv for kv in groups.items() if kv[0] != "gu1"], sems, rest_groups):
        in_flight[g] = [gsems, [srcs[t] for t in idx], [lands[t] for t in idx], None]

    def forward(group, after):
        sems1, gsrcs, glands, _ = in_flight[group]
        sems2, gsrcs, glands = _tl_forward(gsrcs, glands, sems1, after, name="gather_weights_forward_" + group)
        in_flight[group] = [sems1, gsrcs, glands, sems2]

    forward_early = dict(d1="mix", mix="ffn2")

    def w_get(group, after):
        if group == "gu1":
            after = rest_token
        if in_flight[group][3] is None:
            forward(group, after)
        sems1, gsrcs, glands, sems2 = in_flight[group]
        landed = _tl_wait(gsrcs, glands, sems1, sems2, after, name="gather_weights_wait_" + group)
        if group in forward_early:
            forward(forward_early[group], landed[0])
        return {n: a.reshape(-1, D) for n, a in zip(groups[group], landed)}

    pending = []

    def g_put(gs):
        names = tuple(gs)
        srcs = [gs[n].reshape(N_DEV, -1, D) for n in names]
        lands = [_own_slot(lax.dynamic_index_in_dim(s, me, axis=0, keepdims=False), me) for s in srcs]
        sems, srcs, lands, tok = _split_start(srcs, lands, [list(range(len(names)))], scatter=True,
                                              name="scatter_grads_start_" + names[0])
        pending.append((names, sems[0], srcs, lands))
        return tok

    lsum, grad_x, small = _local_step(x[0], loss_target[0], mods, g_ffn1, g_mix, g_ffn2, g_final[None],
                                      _pad_to(conv_full, 8, D), sinks[0], w_get, g_put)
    loss = lax.psum((0.5 / D) * jnp.sum(lsum), ("x", "y", "c"))

    packed = _pack_small(small["mods"], small["g1"], small["gm"], small["g2"], small["gf"], small["convw"],
                         small["sinks"])
    packed_all, _ = _exchange([packed], scatter=False, name="gather_small")
    gsmall = _sum8(packed_all, name="sum_small")

    w_of = dict(ada=w_ada, gu1=w1_gu, d1=w1_down, win=w_in, cp=w_conv_proj, ap=w_attn_proj, out=w_out, gu2=w2_gu,
                d2=w2_down)
    m_of = dict(ada=m_w_ada, gu1=m_w1_gu, d1=m_w1_down, win=m_w_in, cp=m_w_conv_proj, ap=m_w_attn_proj, out=m_w_out,
                gu2=m_w2_gu, d2=m_w2_down)
    v_of = dict(ada=v_w_ada, gu1=v_w1_gu, d1=v_w1_down, win=v_w_in, cp=v_w_conv_proj, ap=v_w_attn_proj, out=v_w_out,
                gu2=v_w2_gu, d2=v_w2_down)
    upd = {}
    after = gsmall
    for names, sems, srcs, lands in pending:
        parts = _split_wait(srcs, lands, sems, after, scatter=True, name="scatter_grads_wait_" + names[0])
        for n, p in zip(names, parts):
            if n in TRANSPOSED:
                res = _adam(jnp.swapaxes(w_of[n], 1, 2), p, jnp.swapaxes(m_of[n], 1, 2), jnp.swapaxes(v_of[n], 1, 2),
                            tm=TM_ADAM, name="adam_" + n)
                upd[n] = [jnp.swapaxes(t, 1, 2) for t in res]
            else:
                upd[n] = _adam(w_of[n], p, m_of[n], v_of[n], tm=TM_ADAM, name="adam_" + n)
        after = upd[names[-1]][1]

    gm_cols = lax.dynamic_slice(packed_all[:, R_MODS:R_MODS + N_MOD, :].reshape(N_DEV, N_MOD * D),
                                (0, me * ada_cols), (N_DEV, ada_cols))
    upd["ada"] = _adam(w_ada, _wada_grad(c_all.T, gm_cols, name="ada_dw"), m_w_ada, v_w_ada, tm=256, name="adam_ada")
    conv_g = lax.dynamic_slice(gsmall[R_CONV:R_CONV + 3], (0, me * conv_cols), (3, conv_cols))
    gsmall_own = gsmall.at[R_CONV:R_CONV + 3].set(_pad_to(conv_g, 3, D))
    small_upd = _adam(_pack_small(b_ada, g_ffn1, g_mix, g_ffn2, g_final, conv_w, sinks)[None], gsmall_own,
                      _pack_small(m_b_ada, m_g_ffn1, m_g_mix, m_g_ffn2, m_g_final, m_conv_w, m_sinks)[None],
                      _pack_small(v_b_ada, v_g_ffn1, v_g_mix, v_g_ffn2, v_g_final, v_conv_w, v_sinks)[None],
                      tm=SMALL_ROWS, name="adam_small")
    small_out = [_unpack_small(p[0], conv_cols) for p in small_upd]

    big_name = dict(w_ada="ada", w1_gu="gu1", w1_down="d1", w_in="win", w_conv_proj="cp", w_attn_proj="ap",
                    w_out="out", w2_gu="gu2", w2_down="d2")
    order = ("w_ada", "b_ada", "g_ffn1", "w1_gu", "w1_down", "g_mix", "w_in", "conv_w", "w_conv_proj", "w_attn_proj",
             "sinks", "w_out", "g_ffn2", "w2_gu", "w2_down", "g_final")
    outs = [loss, grad_x[None]]
    for kind in range(4):
        for n in order:
            outs.append(upd[big_name[n]][kind] if n in big_name else small_out[kind][n])
    return tuple(outs)
```

```python
import functools

import jax
import jax.numpy as jnp
from jax import lax
from jax.experimental import pallas as pl
from jax.experimental.pallas import tpu as pltpu

D = 1024
F = 2816
NIN = 6656
N_HEADS = 16
N_KV = 4
HEAD_DIM = 64
BLK = 128
N_MOD = 9
N_DEV = 8
EPS = 1e-6
NEG_INF = -1e30
ROPE_THETA = 10000.0
O_BG, O_CG, O_U, O_Q, O_K, O_V, O_ZC, O_ZA = 0, 1024, 2048, 3072, 4096, 4352, 4608, 5632

ADAM_LR = 0.001
ADAM_B1 = 0.9
ADAM_B2 = 0.999
ADAM_EPS = 1e-08
ADAM_WD = 0.01
ADAM_STEP = 10

BF = jnp.bfloat16
F32 = jnp.float32
VMEM_LIMIT = 56 * 1024 * 1024
MXU_N = 256
MESH = pl.DeviceIdType.MESH

NT = (((1,), (1,)), ((), ()))
TN = (((0,), (0,)), ((), ()))


def _cp(sem=None):
    return pltpu.CompilerParams(dimension_semantics=sem, vmem_limit_bytes=VMEM_LIMIT)


def _tile(n, pref):
    if n <= pref:
        return n
    for t in range(pref - pref % 16, 15, -16):
        if n % t == 0:
            return t
    raise ValueError((n, pref))


def _sigmoid(v):
    return 0.5 * jnp.tanh(0.5 * v) + 0.5


def _row(i):
    return (i, 0)


def _const2(*_):
    return (0, 0)


def _resident(shape):
    return pl.BlockSpec(shape, lambda *_: (0,) * len(shape), pipeline_mode=pl.Buffered(1))


def _norm_proj(x, g, sc, sh, wt, *, tm, tn, name):
    T, N = x.shape[0], wt.shape[0]
    tm = _tile(T, tm)

    def body(x_ref, g_ref, sc_ref, sh_ref, w_ref, h_ref, o_ref):
        xv = x_ref[...]
        r = lax.rsqrt(jnp.mean(xv * xv, axis=-1, keepdims=True) + EPS)
        hb = ((xv * r) * g_ref[...] * (1.0 + sc_ref[...]) + sh_ref[...]).astype(BF)
        h_ref[...] = hb
        for c0 in range(0, N, tn):
            cols = pl.ds(c0, tn)
            o_ref[:, cols] = lax.dot_general(hb, w_ref[cols, :], NT, preferred_element_type=F32).astype(BF)

    vec = pl.BlockSpec((1, D), _const2)
    return pl.pallas_call(
        body, name=name, grid=(T // tm,),
        in_specs=[pl.BlockSpec((tm, D), _row), vec, vec, vec, _resident((N, D))],
        out_specs=[pl.BlockSpec((tm, D), _row), pl.BlockSpec((tm, N), _row)],
        out_shape=[jax.ShapeDtypeStruct((T, D), BF), jax.ShapeDtypeStruct((T, N), BF)],
        compiler_params=_cp(("parallel",)),
    )(x, g, sc, sh, wt)


def _ffn_down_fwd(ab, wd, x, gt, *, tm, name):
    T = x.shape[0]
    tm = _tile(T, tm)

    def body(a_ref, b_ref, wd_ref, x_ref, gt_ref, xo_ref, y_ref):
        y = None
        for c0 in range(0, F, MXU_N):
            cols = pl.ds(c0, MXU_N)
            a = a_ref[:, cols].astype(F32)
            act = (a * _sigmoid(a) * b_ref[:, cols].astype(F32)).astype(BF)
            part = jnp.dot(act, wd_ref[cols, :], preferred_element_type=F32)
            y = part if y is None else y + part
        y_ref[...] = y.astype(BF)
        xo_ref[...] = x_ref[...] + (0.5 * gt_ref[...]) * y

    return pl.pallas_call(
        body, name=name, grid=(T // tm,),
        in_specs=[pl.BlockSpec((tm, F), lambda i: (i, 0)), pl.BlockSpec((tm, F), lambda i: (i, 1)),
                  _resident((F, D)), pl.BlockSpec((tm, D), _row), pl.BlockSpec((1, D), _const2)],
        out_specs=[pl.BlockSpec((tm, D), _row), pl.BlockSpec((tm, D), _row)],
        out_shape=[jax.ShapeDtypeStruct((T, D), F32), jax.ShapeDtypeStruct((T, D), BF)],
        compiler_params=_cp(("parallel",)),
    )(ab, ab, wd, x, gt)


def _ffn_fwd(x, g, sc, sh, gt, wgu, wd, final, *, tm, name):
    T = x.shape[0]
    tm = _tile(T, tm)
    last = final is not None

    def body(x_ref, g_ref, sc_ref, sh_ref, gt_ref, wgu_ref, wd_ref, *rest):
        if last:
            t_ref, gf_ref, h_ref, ab_ref, y_ref, dx_ref, ls_ref, dgf_ref = rest
        else:
            h_ref, ab_ref, y_ref, xo_ref = rest
        xv = x_ref[...]
        r = lax.rsqrt(jnp.mean(xv * xv, axis=-1, keepdims=True) + EPS)
        hb = ((xv * r) * g_ref[...] * (1.0 + sc_ref[...]) + sh_ref[...]).astype(BF)
        h_ref[...] = hb
        y = None
        for c0 in range(0, F, MXU_N):
            a = lax.dot_general(hb, wgu_ref[pl.ds(c0, MXU_N), :], NT, preferred_element_type=F32)
            b = lax.dot_general(hb, wgu_ref[pl.ds(F + c0, MXU_N), :], NT, preferred_element_type=F32)
            ab = a.astype(BF)
            bb = b.astype(BF)
            ab_ref[:, pl.ds(c0, MXU_N)] = ab
            ab_ref[:, pl.ds(F + c0, MXU_N)] = bb
            a = ab.astype(F32)
            act = (a * _sigmoid(a) * bb.astype(F32)).astype(BF)
            part = jnp.dot(act, wd_ref[pl.ds(c0, MXU_N), :], preferred_element_type=F32)
            y = part if y is None else y + part
        y_ref[...] = y.astype(BF)
        xo = xv + (0.5 * gt_ref[...]) * y
        if not last:
            xo_ref[...] = xo
            return

        @pl.when(pl.program_id(0) == 0)
        def _():
            ls_ref[...] = jnp.zeros_like(ls_ref)
            dgf_ref[...] = jnp.zeros_like(dgf_ref)
        gv = gf_ref[...]
        r = lax.rsqrt(jnp.mean(xo * xo, axis=-1, keepdims=True) + EPS)
        xh = xo * r
        e = xh * gv - t_ref[...]
        ls_ref[...] += jnp.sum(e * e, axis=0, keepdims=True)
        dy = e * (1.0 / D)
        dgf_ref[...] += jnp.sum(dy * xh, axis=0, keepdims=True)
        dxh = dy * gv
        dx_ref[...] = r * (dxh - xh * jnp.mean(dxh * xh, axis=-1, keepdims=True))

    vec = pl.BlockSpec((1, D), _const2)
    rowspec = pl.BlockSpec((tm, D), _row)
    in_specs = [rowspec, vec, vec, vec, vec, _resident((2 * F, D)), _resident((F, D))]
    out_specs = [rowspec, pl.BlockSpec((tm, 2 * F), _row), rowspec, rowspec]
    out_shape = [jax.ShapeDtypeStruct((T, D), BF), jax.ShapeDtypeStruct((T, 2 * F), BF),
                 jax.ShapeDtypeStruct((T, D), BF), jax.ShapeDtypeStruct((T, D), F32)]
    args = [x, g, sc, sh, gt, wgu, wd]
    if last:
        in_specs += [rowspec, vec]
        out_specs += [vec, vec]
        out_shape += [jax.ShapeDtypeStruct((1, D), F32)] * 2
        args += list(final)
    return pl.pallas_call(
        body, name=name, grid=(T // tm,),
        in_specs=in_specs, out_specs=out_specs, out_shape=out_shape,
        compiler_params=_cp(("arbitrary",) if last else ("parallel",)),
    )(*args)


def _ffn_down_bwd(dxo, y, gt, ab, wd, *, tm, tn, name):
    T = dxo.shape[0]
    tm = _tile(T, tm)

    def body(dxo_ref, y_ref, gt_ref, a_ref, b_ref, wd_ref, dy_ref, dab_ref, dgt_ref):
        @pl.when(pl.program_id(0) == 0)
        def _():
            dgt_ref[...] = jnp.zeros_like(dgt_ref)

        dxv = dxo_ref[...]
        dgt_ref[...] += 0.5 * jnp.sum(dxv * y_ref[...].astype(F32), axis=0, keepdims=True)
        dy = ((0.5 * gt_ref[...]) * dxv).astype(BF)
        dy_ref[...] = dy
        for c0 in range(0, F, tn):
            cols = pl.ds(c0, tn)
            dact = lax.dot_general(dy, wd_ref[cols, :], NT, preferred_element_type=F32)
            a = a_ref[:, cols].astype(F32)
            b = b_ref[:, cols].astype(F32)
            s = _sigmoid(a)
            dab_ref[0, :, cols] = (dact * b * (s * (1.0 + a * (1.0 - s)))).astype(BF)
            dab_ref[1, :, cols] = (dact * (a * s)).astype(BF)

    vec = pl.BlockSpec((1, D), _const2)
    rowspec = pl.BlockSpec((tm, D), _row)
    return pl.pallas_call(
        body, name=name, grid=(T // tm,),
        in_specs=[rowspec, rowspec, vec, pl.BlockSpec((tm, F), lambda i: (i, 0)),
                  pl.BlockSpec((tm, F), lambda i: (i, 1)), pl.BlockSpec((F, D), _const2)],
        out_specs=[rowspec, pl.BlockSpec((2, tm, F), lambda i: (0, i, 0)), vec],
        out_shape=[jax.ShapeDtypeStruct((T, D), BF), jax.ShapeDtypeStruct((2, T, F), BF),
                   jax.ShapeDtypeStruct((1, D), F32)],
        compiler_params=_cp(("arbitrary",)),
    )(dxo, y, gt, ab, ab, wd)


def _tn_matmul(a, b, *, tn, tk, name):
    S, T, Ns = a.shape
    tn, tk = _tile(Ns, tn), _tile(T, tk)
    nk, njs = T // tk, Ns // tn

    def body(a_ref, b_ref, o_ref, acc):
        k = pl.program_id(1)

        @pl.when(k == 0)
        def _():
            acc[...] = jnp.zeros_like(acc)
        acc[...] += lax.dot_general(a_ref[0], b_ref[...], TN, preferred_element_type=F32)

        @pl.when(k == nk - 1)
        def _():
            o_ref[...] = acc[...].astype(BF)

    return pl.pallas_call(
        body, name=name, grid=(S * njs, nk),
        in_specs=[pl.BlockSpec((1, tk, tn), lambda j, k: (j // njs, k, j % njs)),
                  pl.BlockSpec((tk, D), lambda j, k: (k, 0))],
        out_specs=pl.BlockSpec((tn, D), lambda j, k: (j, 0)),
        out_shape=jax.ShapeDtypeStruct((S * Ns, D), BF),
        scratch_shapes=[pltpu.VMEM((tn, D), F32)],
        compiler_params=_cp(("parallel", "arbitrary")),
    )(a, b)


def _tn_matmul_swiglu(ab, b, token, *, tn, tk, name):
    T = ab.shape[0]
    tn, tk = _tile(F, tn), _tile(T, tk)
    nk, nj = T // tk, F // tn
    deps = [] if token is None else [token]

    def body(a_ref, g_ref, b_ref, *rest):
        o_ref, acc = rest[len(deps):]
        k = pl.program_id(1)

        @pl.when(k == 0)
        def _():
            acc[...] = jnp.zeros_like(acc)
        bv = b_ref[...]
        for c0 in range(0, tn, MXU_N):
            cw = min(MXU_N, tn - c0)
            cols = pl.ds(c0, cw)
            a = a_ref[:, cols].astype(F32)
            act = (a * _sigmoid(a) * g_ref[:, cols].astype(F32)).astype(BF)
            acc[cols, :] += lax.dot_general(act, bv, TN, preferred_element_type=F32)

        @pl.when(k == nk - 1)
        def _():
            o_ref[...] = acc[...].astype(BF)

    return pl.pallas_call(
        body, name=name, grid=(nj, nk),
        in_specs=[pl.BlockSpec((tk, tn), lambda j, k: (k, j)), pl.BlockSpec((tk, tn), lambda j, k: (k, j + nj)),
                  pl.BlockSpec((tk, D), lambda j, k: (k, 0))] + [pl.BlockSpec(memory_space=pl.ANY)] * len(deps),
        out_specs=pl.BlockSpec((tn, D), lambda j, k: (j, 0)),
        out_shape=jax.ShapeDtypeStruct((F, D), BF),
        scratch_shapes=[pltpu.VMEM((tn, D), F32)],
        compiler_params=_cp(("parallel", "arbitrary")),
    )(ab, ab, b, *deps)


def _nn_bwd_norm(da, w, x, g, sc, dxo, *, tm, name):
    S, T, Ks = da.shape
    tm = _tile(T, tm)
    rc = _tile(tm, 256)

    def body(da_ref, w_ref, x_ref, g_ref, sc_ref, dxo_ref, dx_ref, dsh_ref, dsc_ref, dg_ref, acc):
        @pl.when(pl.program_id(0) == 0)
        def _():
            dsh_ref[...] = jnp.zeros_like(dsh_ref)
            dsc_ref[...] = jnp.zeros_like(dsc_ref)
            dg_ref[...] = jnp.zeros_like(dg_ref)

        d = jnp.dot(da_ref[0], w_ref[0:Ks, :], preferred_element_type=F32)
        for s in range(1, S):
            d = d + jnp.dot(da_ref[s], w_ref[s * Ks:(s + 1) * Ks, :], preferred_element_type=F32)
        acc[...] = d
        gv = g_ref[...]
        sc1 = 1.0 + sc_ref[...]
        dsh = jnp.zeros((1, D), F32)
        dsc = jnp.zeros((1, D), F32)
        dg = jnp.zeros((1, D), F32)
        for r0 in range(0, tm, rc):
            rows = pl.ds(r0, rc)
            u = acc[rows, :]
            xv = x_ref[rows, :]
            r = lax.rsqrt(jnp.mean(xv * xv, axis=-1, keepdims=True) + EPS)
            xh = xv * r
            dsh = dsh + jnp.sum(u, axis=0, keepdims=True)
            dsc = dsc + jnp.sum(u * (xh * gv), axis=0, keepdims=True)
            us = u * sc1
            dg = dg + jnp.sum(us * xh, axis=0, keepdims=True)
            dxh = us * gv
            dx_ref[rows, :] = dxo_ref[rows, :] + r * (dxh - xh * jnp.mean(dxh * xh, axis=-1, keepdims=True))
        dsh_ref[...] += dsh
        dsc_ref[...] += dsc
        dg_ref[...] += dg

    vec = pl.BlockSpec((1, D), _const2)
    rowspec = pl.BlockSpec((tm, D), _row)
    return pl.pallas_call(
        body, name=name, grid=(T // tm,),
        in_specs=[pl.BlockSpec((S, tm, Ks), lambda i: (0, i, 0)), _resident((S * Ks, D)), rowspec, vec, vec, rowspec],
        out_specs=[rowspec, vec, vec, vec],
        out_shape=[jax.ShapeDtypeStruct((T, D), F32)] + [jax.ShapeDtypeStruct((1, D), F32)] * 3,
        scratch_shapes=[pltpu.VMEM((tm, D), F32)],
        compiler_params=_cp(("arbitrary",)),
    )(da, w, x, g, sc, dxo)


def _rope(t, cos, sin_signed, lt32, inverse=False):
    sel = jnp.where(lt32, pltpu.roll(t, 96, 1), pltpu.roll(t, 32, 1))
    return t * cos - sel * sin_signed if inverse else t * cos + sel * sin_signed


def _rope_tables(T):
    inv = 1.0 / (ROPE_THETA ** (jnp.arange(0, HEAD_DIM, 2, dtype=F32) / HEAD_DIM))
    ang = jnp.arange(T, dtype=F32)[:, None] * inv[None, :]
    cos, sin = jnp.cos(ang), jnp.sin(ang)
    cos128 = jnp.tile(cos, (1, 4))
    sin128 = jnp.tile(jnp.concatenate([-sin, sin], axis=1), (1, 2))
    return cos128, sin128


QSCALE = HEAD_DIM ** -0.5


def _lane_masks(rows):
    lane = lax.broadcasted_iota(jnp.int32, (rows, 128), 1)
    return (lane % HEAD_DIM) < (HEAD_DIM // 2), [lane < HEAD_DIM, lane >= HEAD_DIM]


def _attn_bias():
    qi = lax.broadcasted_iota(jnp.int32, (4 * BLK, 2 * BLK), 0) % BLK
    kj = lax.broadcasted_iota(jnp.int32, (4 * BLK, 2 * BLK), 1)
    band = (kj > qi) & (kj <= qi + BLK)
    return jnp.stack([jnp.where(band & (kj >= BLK), 0.0, NEG_INF), jnp.where(band, 0.0, NEG_INF)]).astype(F32)


def _attn_prep(proj, cos, sin, *, name):
    T = proj.shape[0]
    tm = _tile(T, 4 * BLK)

    def body(q_ref, k_ref, c_ref, s_ref, qs_ref, kr_ref):
        lt32, halves = _lane_masks(BLK)
        for b in range(tm // BLK):
            rows = pl.ds(b * BLK, BLK)
            cc, sc = c_ref[rows, :], s_ref[rows, :]
            qr = [_rope(q_ref[rows, p * 128:(p + 1) * 128].astype(F32), cc, sc, lt32) * QSCALE for p in range(8)]
            for g in range(N_KV):
                qs_ref[g, pl.ds(4 * b * BLK, 4 * BLK), :] = _stack_heads(qr, g, halves).astype(BF)
            kr_ref[rows, :] = jnp.concatenate([_rope(k_ref[rows, r * 128:(r + 1) * 128].astype(F32), cc, sc, lt32)
                                               for r in range(2)], axis=1).astype(BF)

    tab = pl.BlockSpec((tm, 128), _row)
    return pl.pallas_call(
        body, name=name, grid=(T // tm,),
        in_specs=[pl.BlockSpec((tm, D), lambda n: (n, O_Q // D)), pl.BlockSpec((tm, 256), lambda n: (n, O_K // 256)),
                  tab, tab],
        out_specs=[pl.BlockSpec((N_KV, 4 * tm, 128), lambda n: (0, n, 0)), pl.BlockSpec((tm, 256), _row)],
        out_shape=[jax.ShapeDtypeStruct((N_KV, 4 * T, 128), BF), jax.ShapeDtypeStruct((T, 256), BF)],
        compiler_params=_cp(("parallel",)),
    )(proj, proj, cos, sin)


def _attn_specs():
    prev = lambda n: jnp.maximum(n - 1, 0)
    return [pl.BlockSpec((N_KV, 4 * BLK, 128), lambda n: (0, n, 0)),
            pl.BlockSpec((BLK, 256), _row), pl.BlockSpec((BLK, 256), lambda n: (prev(n), 0)),
            pl.BlockSpec((BLK, 256), lambda n: (n, O_V // 256)),
            pl.BlockSpec((BLK, 256), lambda n: (prev(n), O_V // 256)),
            pl.BlockSpec((1, 4 * BLK, 2 * BLK), lambda n: (jnp.minimum(n, 1), 0, 0)),
            pl.BlockSpec(memory_space=pltpu.SMEM)]


def _bands(kc_ref, kp_ref, vc_ref, vp_ref):
    kb, vb = [], []
    for r in range(2):
        cols = slice(r * 128, (r + 1) * 128)
        kb.append(jnp.concatenate([kp_ref[:, cols], kc_ref[:, cols]], axis=0))
        vb.append(jnp.concatenate([vp_ref[:, cols], vc_ref[:, cols]], axis=0))
    return kb, vb


def _sink_col(sink_ref, g):
    return jnp.concatenate([jnp.full((BLK, 1), sink_ref[4 * g + hh], F32) for hh in range(4)], axis=0)


def _unstack_heads(t, g, halves, acc):
    half = g % 2
    for hh in range(4):
        h = 4 * g + hh
        th = jnp.where(halves[half], t[hh * BLK:(hh + 1) * BLK], 0.0)
        if h % 2 != half:
            th = pltpu.roll(th, HEAD_DIM, 1)
        acc[h // 2] = acc[h // 2] + th


def _stack_heads(chunks, g, halves):
    half = g % 2
    parts = []
    for hh in range(4):
        h = 4 * g + hh
        t = chunks[h // 2]
        if h % 2 != half:
            t = pltpu.roll(t, HEAD_DIM, 1)
        parts.append(jnp.where(halves[half], t, 0.0))
    return jnp.concatenate(parts, axis=0)


def _attn_fwd(qs, kr, proj, bias, sinks, *, name):
    T = proj.shape[0]
    nb = T // BLK

    def body(qs_ref, kc_ref, kp_ref, vc_ref, vp_ref, bias_ref, sink_ref, o_ref, lse_ref):
        _, h128 = _lane_masks(BLK)
        _, h256 = _lane_masks(2 * BLK)
        _, h512 = _lane_masks(4 * BLK)
        kb, vb = _bands(kc_ref, kp_ref, vc_ref, vp_ref)
        outs = [jnp.zeros((BLK, 128), F32) for _ in range(8)]
        groups = range(N_KV)
        bias = bias_ref[0]
        sink = [_sink_col(sink_ref, g) for g in groups]
        s = [lax.dot_general(qs_ref[g], kb[g // 2], NT, preferred_element_type=F32) + bias for g in groups]
        m = [jnp.maximum(jnp.max(s[g], axis=-1, keepdims=True), sink[g]) for g in groups]
        p = [jnp.exp(s[g] - m[g]).astype(BF) for g in groups]
        vg = [jnp.where(h256[g % 2], vb[g // 2].astype(F32), 1.0).astype(BF) for g in groups]
        o = [jnp.dot(p[g], vg[g], preferred_element_type=F32) for g in groups]
        denom = [jnp.where(h512[g % 2], pltpu.roll(o[g], HEAD_DIM, 1), o[g]) + jnp.exp(sink[g] - m[g]) for g in groups]
        for g in groups:
            lse_ref[g] = m[g] + jnp.log(denom[g])
            _unstack_heads(o[g] * (1.0 / denom[g]), g, h128, outs)
        o_ref[...] = jnp.concatenate(outs, axis=1).astype(BF)

    return pl.pallas_call(
        body, name=name, grid=(nb,),
        in_specs=_attn_specs(),
        out_specs=[pl.BlockSpec((BLK, D), _row), pl.BlockSpec((N_KV, 4 * BLK, 128), lambda n: (0, n, 0))],
        out_shape=[jax.ShapeDtypeStruct((T, D), BF), jax.ShapeDtypeStruct((N_KV, 4 * T, 128), F32)],
        compiler_params=_cp(("parallel",)),
    )(qs, kr, kr, proj, proj, bias, sinks)


def _attn_bwd(qs, kr, proj, bias, sinks, lse, o, do, cos, sin, dproj, *, name):
    T = proj.shape[0]
    nb = T // BLK

    def body(qs_ref, kc_ref, kp_ref, vc_ref, vp_ref, bias_ref, sink_ref, lse_ref, o_ref, do_ref,
             cc_ref, sc_ref, cp_ref, sp_ref, dproj_ref, dq_ref, dkc_ref, dkp_ref, dvc_ref, dvp_ref, dsink_ref):
        @pl.when(pl.program_id(0) == 0)
        def _():
            dsink_ref[...] = jnp.zeros_like(dsink_ref)
        lt32, h128 = _lane_masks(BLK)
        kb, vb = _bands(kc_ref, kp_ref, vc_ref, vp_ref)
        oc = [o_ref[:, p * 128:(p + 1) * 128].astype(F32) for p in range(8)]
        doc = [do_ref[:, p * 128:(p + 1) * 128].astype(F32) for p in range(8)]
        dqs = [jnp.zeros((BLK, 128), F32) for _ in range(8)]
        lane1 = lax.broadcasted_iota(jnp.int32, (1, 128), 1)
        dsink = jnp.zeros((1, 128), F32)
        groups = range(N_KV)
        bias = bias_ref[0]
        q = [qs_ref[g] for g in groups]
        lse_g = [lse_ref[g] for g in groups]
        s = [lax.dot_general(q[g], kb[g // 2], NT, preferred_element_type=F32) + bias for g in groups]
        dos = [_stack_heads(doc, g, h128) for g in groups]
        dosb = [t.astype(BF) for t in dos]
        dp = [lax.dot_general(dosb[g], vb[g // 2], NT, preferred_element_type=F32) for g in groups]
        delta = [jnp.sum(dos[g] * _stack_heads(oc, g, h128), axis=-1, keepdims=True) for g in groups]
        p = [jnp.exp(s[g] - jnp.concatenate([lse_g[g], lse_g[g]], axis=1)) for g in groups]
        ds = [(p[g] * (dp[g] - delta[g])).astype(BF) for g in groups]
        pb = [t.astype(BF) for t in p]
        dvg = [lax.dot_general(pb[g], dosb[g], TN, preferred_element_type=F32) for g in groups]
        dkg = [lax.dot_general(ds[g], q[g], TN, preferred_element_type=F32) for g in groups]
        dqg = [jnp.dot(ds[g], kb[g // 2], preferred_element_type=F32) * QSCALE for g in groups]
        dvr = [dvg[0] + dvg[1], dvg[2] + dvg[3]]
        dkr = [dkg[0] + dkg[1], dkg[2] + dkg[3]]
        for g in groups:
            _unstack_heads(dqg[g], g, h128, dqs)
            dsk = -jnp.exp(_sink_col(sink_ref, g) - lse_g[g][:, 0:1]) * delta[g]
            for hh in range(4):
                val = jnp.sum(dsk[hh * BLK:(hh + 1) * BLK], axis=0, keepdims=True)
                dsink = dsink + jnp.where(lane1 == 4 * g + hh, val, 0.0)
        cc, sc, cp, sp = cc_ref[...], sc_ref[...], cp_ref[...], sp_ref[...]
        dsink_ref[...] += dsink
        dq_ref[...] = jnp.concatenate([_rope(t, cc, sc, lt32, inverse=True) for t in dqs], axis=1).astype(BF)
        dkp_ref[...] = jnp.concatenate([_rope(t[:BLK], cp, sp, lt32, inverse=True) for t in dkr], axis=1)
        dkc_ref[...] = jnp.concatenate([_rope(t[BLK:], cc, sc, lt32, inverse=True) for t in dkr], axis=1)
        dvp_ref[...] = jnp.concatenate([t[:BLK] for t in dvr], axis=1)
        dvc_ref[...] = jnp.concatenate([t[BLK:] for t in dvr], axis=1)

    kv = pl.BlockSpec((BLK, 256), _row)
    tc = pl.BlockSpec((BLK, 128), _row)
    tp = pl.BlockSpec((BLK, 128), lambda n: (jnp.maximum(n - 1, 0), 0))
    return pl.pallas_call(
        body, name=name, grid=(nb,),
        in_specs=_attn_specs() + [pl.BlockSpec((N_KV, 4 * BLK, 128), lambda n: (0, n, 0)),
                                  pl.BlockSpec((BLK, D), _row), pl.BlockSpec((BLK, D), _row), tc, tc, tp, tp,
                                  pl.BlockSpec(memory_space=pl.ANY)],
        out_specs=[pl.BlockSpec((BLK, D), lambda n: (n, O_Q // D)), kv, kv, kv, kv, pl.BlockSpec((1, 128), _const2)],
        out_shape=[jax.ShapeDtypeStruct(dproj.shape, BF)] + [jax.ShapeDtypeStruct((T, 256), F32)] * 4
        + [jax.ShapeDtypeStruct((1, 128), F32)],
        input_output_aliases={14: 0},
        compiler_params=_cp(("arbitrary",)),
    )(qs, kr, kr, proj, proj, bias, sinks, lse, o, do, cos, sin, cos, sin, dproj)


def _dkv_combine(dkc, dkp, dvc, dvp, dproj, *, name):
    T = dkc.shape[0]
    nb = T // BLK
    tm = _tile(T, 4 * BLK)
    bpt = tm // BLK
    nt = T // tm

    def body(dkc_ref, dkp_ref, dkn_ref, dvc_ref, dvp_ref, dvn_ref, dproj_ref, o_ref):
        keep = jnp.where(pl.program_id(0) == nt - 1, 0.0, 1.0)

        def shifted(prev_ref, next_ref):
            nxt = keep * next_ref[...]
            return nxt if bpt == 1 else jnp.concatenate([prev_ref[BLK:, :], nxt], axis=0)

        o_ref[:, 0:256] = (dkc_ref[...] + shifted(dkp_ref, dkn_ref)).astype(BF)
        o_ref[:, 256:512] = (dvc_ref[...] + shifted(dvp_ref, dvn_ref)).astype(BF)

    cur = pl.BlockSpec((tm, 256), _row)
    nxt = pl.BlockSpec((BLK, 256), lambda i: (jnp.minimum((i + 1) * bpt, nb - 1), 0))
    return pl.pallas_call(
        body, name=name, grid=(nt,),
        in_specs=[cur, cur, nxt, cur, cur, nxt, pl.BlockSpec(memory_space=pl.ANY)],
        out_specs=pl.BlockSpec((tm, 512), lambda i: (i, O_K // 512)),
        out_shape=jax.ShapeDtypeStruct(dproj.shape, BF),
        input_output_aliases={6: 0},
        compiler_params=_cp(("parallel",)),
    )(dkc, dkp, dkp, dvc, dvp, dvp, dproj)


HALO = 16


def _conv_shifts(cu, hprev, tm):
    row = lax.broadcasted_iota(jnp.int32, cu.shape, 0)
    h1 = hprev[HALO - 1:HALO, :]
    h2 = hprev[HALO - 2:HALO - 1, :]
    m1 = jnp.where(row == 0, h1, pltpu.roll(cu, 1, 0))
    m2 = jnp.where(row == 0, h2, jnp.where(row == 1, h1, pltpu.roll(cu, 2, 0)))
    return m1, m2


def _mixer_mid_fwd(proj, attn, wcp, wap, wout, convw, x, gt, *, tm, name):
    T = x.shape[0]
    tm = _tile(T, tm)
    hb = tm // HALO

    def body(bg_ref, cg_ref, u_ref, hcg_ref, hu_ref, zc0_ref, zc1_ref, za0_ref, za1_ref, at_ref,
             wcp_ref, wap_ref, wout_ref, cw_ref, x_ref, gt_ref,
             x2_ref, gc_ref, yc_ref, ya_ref, mg_ref, o_ref):
        first = jnp.where(pl.program_id(0) == 0, 0.0, 1.0)
        cu = cg_ref[...].astype(F32) * u_ref[...].astype(F32)
        hprev = first * (hcg_ref[...].astype(F32) * hu_ref[...].astype(F32))
        m1, m2 = _conv_shifts(cu, hprev, tm)
        cv = cw_ref[0:1, :] * m2 + cw_ref[1:2, :] * m1 + cw_ref[2:3, :] * cu
        gc = (bg_ref[...].astype(F32) * cv).astype(BF)
        gc_ref[...] = gc
        yc = jnp.dot(gc, wcp_ref[...], preferred_element_type=F32)
        ya = jnp.dot(at_ref[...], wap_ref[...], preferred_element_type=F32)
        yc_ref[...] = yc.astype(BF)
        ya_ref[...] = ya.astype(BF)
        zc = jnp.concatenate([zc0_ref[...], zc1_ref[...]], axis=1).astype(F32)
        za = jnp.concatenate([za0_ref[...], za1_ref[...]], axis=1).astype(F32)
        mg = (_sigmoid(zc) * yc + _sigmoid(za) * ya).astype(BF)
        mg_ref[...] = mg
        o = jnp.dot(mg, wout_ref[...], preferred_element_type=F32)
        o_ref[...] = o.astype(BF)
        x2_ref[...] = x_ref[...] + gt_ref[...] * o

    wspec = pl.BlockSpec((D, D), _const2)
    rowspec = pl.BlockSpec((tm, D), _row)
    return pl.pallas_call(
        body, name=name, grid=(T // tm,),
        in_specs=[_col(tm, O_BG), _col(tm, O_CG), _col(tm, O_U), _halo_prev(hb, O_CG), _halo_prev(hb, O_U),
                  _col(tm, O_ZC, 512), _col(tm, O_ZC + 512, 512), _col(tm, O_ZA, 512), _col(tm, O_ZA + 512, 512),
                  rowspec, wspec, wspec, wspec, pl.BlockSpec((8, D), _const2), rowspec, pl.BlockSpec((1, D), _const2)],
        out_specs=[rowspec] * 6,
        out_shape=[jax.ShapeDtypeStruct((T, D), F32)] + [jax.ShapeDtypeStruct((T, D), BF)] * 5,
        compiler_params=_cp(("parallel",)),
    )(proj, proj, proj, proj, proj, proj, proj, proj, proj, attn, wcp, wap, wout, convw, x, gt)


def _col(tm, c, w=D):
    assert c % w == 0
    return pl.BlockSpec((tm, w), lambda i: (i, c // w))


def _halo_prev(hb, c):
    return pl.BlockSpec((HALO, D), lambda i: (jnp.maximum(i * hb - 1, 0), c // D))


def _halo_next(hb, nblk, c=0):
    return pl.BlockSpec((HALO, D), lambda i: (jnp.minimum((i + 1) * hb, nblk - 1), c // D))


def _mixer_mid_bwd(dx2, gt, o, proj, yc, ya, wout, wcp, wap, *, tm, name):
    T = dx2.shape[0]
    tm = _tile(T, tm)
    zw = 512
    nz = 2 * D // zw

    def body(dx_ref, gt_ref, o_ref, zc0_ref, zc1_ref, za0_ref, za1_ref, yc_ref, ya_ref, wout_ref, wcp_ref, wap_ref,
             dout_ref, dyc_ref, dya_ref, dgc_ref, dat_ref, dz_ref, dgt_ref, dzs):
        i, j = pl.program_id(0), pl.program_id(1)

        @pl.when(jnp.logical_and(i == 0, j == 0))
        def _():
            dgt_ref[...] = jnp.zeros_like(dgt_ref)

        @pl.when(j == 0)
        def _():
            dxv = dx_ref[...]
            dgt_ref[...] += jnp.sum(dxv * o_ref[...].astype(F32), axis=0, keepdims=True)
            dout = (gt_ref[...] * dxv).astype(BF)
            dout_ref[...] = dout
            dmg = lax.dot_general(dout, wout_ref[...], NT, preferred_element_type=F32)
            sc = _sigmoid(jnp.concatenate([zc0_ref[...], zc1_ref[...]], axis=1).astype(F32))
            sa = _sigmoid(jnp.concatenate([za0_ref[...], za1_ref[...]], axis=1).astype(F32))
            dyc = (dmg * sc).astype(BF)
            dya = (dmg * sa).astype(BF)
            dyc_ref[...] = dyc
            dya_ref[...] = dya
            dzs[:, 0:D] = (dmg * yc_ref[...].astype(F32) * (sc * (1.0 - sc))).astype(BF)
            dzs[:, D:2 * D] = (dmg * ya_ref[...].astype(F32) * (sa * (1.0 - sa))).astype(BF)
            dgc_ref[...] = lax.dot_general(dyc, wcp_ref[...], NT, preferred_element_type=F32).astype(BF)
            dat_ref[...] = lax.dot_general(dya, wap_ref[...], NT, preferred_element_type=F32).astype(BF)

        for jj in range(nz):
            @pl.when(j == jj)
            def _(jj=jj):
                dz_ref[...] = dzs[:, jj * zw:(jj + 1) * zw]

    nt = T // tm

    def ahead(i, j):
        return jnp.minimum(i + jnp.minimum(j, 1), nt - 1)

    def zcol(c):
        return pl.BlockSpec((tm, zw), lambda i, j: (ahead(i, j), c // zw))

    wspec = pl.BlockSpec((D, D), _const2)
    rowin = pl.BlockSpec((tm, D), lambda i, j: (ahead(i, j), 0))
    rowspec = pl.BlockSpec((tm, D), lambda i, j: (i, 0))
    vec = pl.BlockSpec((1, D), _const2)
    return pl.pallas_call(
        body, name=name, grid=(nt, nz),
        in_specs=[rowin, vec, rowin, zcol(O_ZC), zcol(O_ZC + zw), zcol(O_ZA), zcol(O_ZA + zw),
                  rowin, rowin, wspec, wspec, wspec],
        out_specs=[rowspec] * 5 + [pl.BlockSpec((tm, zw), lambda i, j: (i, O_ZC // zw + j)), vec],
        out_shape=[jax.ShapeDtypeStruct((T, D), BF)] * 5 + [jax.ShapeDtypeStruct((T, NIN), BF),
                                                            jax.ShapeDtypeStruct((1, D), F32)],
        scratch_shapes=[pltpu.VMEM((tm, 2 * D), BF)],
        compiler_params=_cp(("arbitrary", "arbitrary")),
    )(dx2, gt, o, proj, proj, proj, proj, yc, ya, wout, wcp, wap)


def _conv_bwd(dgc, proj, convw, dproj, *, tm, name):
    T = dgc.shape[0]
    tm = _tile(T, tm)
    hb = tm // HALO
    nblk = T // HALO
    nt = T // tm

    def body(dgc_ref, ndgc_ref, bg_ref, nbg_ref, cg_ref, u_ref, hcg_ref, hu_ref, cw_ref, dproj_ref, dp_ref, dcw_ref):
        i = pl.program_id(0)

        @pl.when(i == 0)
        def _():
            dcw_ref[...] = jnp.zeros_like(dcw_ref)
        first = jnp.where(i == 0, 0.0, 1.0)
        last = jnp.where(i == nt - 1, 0.0, 1.0)
        cg = cg_ref[...].astype(F32)
        u = u_ref[...].astype(F32)
        bg = bg_ref[...].astype(F32)
        dg = dgc_ref[...].astype(F32)
        cu = cg * u
        hprev = first * (hcg_ref[...].astype(F32) * hu_ref[...].astype(F32))
        m1, m2 = _conv_shifts(cu, hprev, tm)
        w0, w1, w2 = cw_ref[0:1, :], cw_ref[1:2, :], cw_ref[2:3, :]
        cv = w0 * m2 + w1 * m1 + w2 * cu
        dcv = dg * bg
        nxt = last * (ndgc_ref[...].astype(F32) * nbg_ref[...].astype(F32))
        n0, n1 = nxt[0:1, :], nxt[1:2, :]
        row = lax.broadcasted_iota(jnp.int32, dcv.shape, 0)
        p1 = jnp.where(row == tm - 1, n0, pltpu.roll(dcv, tm - 1, 0))
        p2 = jnp.where(row == tm - 1, n1, jnp.where(row == tm - 2, n0, pltpu.roll(dcv, tm - 2, 0)))
        dcu = w2 * dcv + w1 * p1 + w0 * p2
        dp_ref[:, 0:D] = (dg * cv).astype(BF)
        dp_ref[:, D:2 * D] = (dcu * u).astype(BF)
        dp_ref[:, 2 * D:3 * D] = (dcu * cg).astype(BF)
        dcw_ref[0:1, :] += jnp.sum(dcv * m2, axis=0, keepdims=True)
        dcw_ref[1:2, :] += jnp.sum(dcv * m1, axis=0, keepdims=True)
        dcw_ref[2:3, :] += jnp.sum(dcv * cu, axis=0, keepdims=True)

    rowspec = pl.BlockSpec((tm, D), _row)
    cw = pl.BlockSpec((8, D), _const2)
    return pl.pallas_call(
        body, name=name, grid=(nt,),
        in_specs=[rowspec, _halo_next(hb, nblk), _col(tm, O_BG), _halo_next(hb, nblk, O_BG),
                  _col(tm, O_CG), _col(tm, O_U), _halo_prev(hb, O_CG), _halo_prev(hb, O_U), cw,
                  pl.BlockSpec(memory_space=pl.ANY)],
        out_specs=[pl.BlockSpec((tm, 3 * D), _row), cw],
        out_shape=[jax.ShapeDtypeStruct(dproj.shape, BF), jax.ShapeDtypeStruct((8, D), F32)],
        input_output_aliases={9: 0},
        compiler_params=_cp(("arbitrary",)),
    )(dgc, dgc, proj, proj, proj, proj, proj, proj, convw, dproj)


def _adam(w, g, m, v, *, tm, name):
    _, R, C = w.shape
    tm = _tile(R, tm)
    parts = g.ndim == 3
    c1 = 1.0 - ADAM_B1
    c2 = 1.0 - ADAM_B2
    bc1 = 1.0 - ADAM_B1 ** ADAM_STEP
    bc2 = 1.0 - ADAM_B2 ** ADAM_STEP

    def body(w_ref, g_ref, m_ref, v_ref, go_ref, d_ref, nm_ref, nv_ref):
        if parts:
            gv = g_ref[0].astype(F32)
            for s in range(1, N_DEV):
                gv = gv + g_ref[s].astype(F32)
        else:
            gv = g_ref[...]
        go_ref[0] = gv
        nm = ADAM_B1 * m_ref[0] + c1 * gv
        nv = ADAM_B2 * v_ref[0] + c2 * (gv * gv)
        nm_ref[0] = nm
        nv_ref[0] = nv
        d_ref[0] = -ADAM_LR * ((nm / bc1) / (jnp.sqrt(nv / bc2) + ADAM_EPS) + ADAM_WD * w_ref[0])

    spec = pl.BlockSpec((1, tm, C), lambda i: (0, i, 0))
    gspec = pl.BlockSpec((N_DEV, tm, C), lambda i: (0, i, 0)) if parts else pl.BlockSpec((tm, C), _row)
    return pl.pallas_call(
        body, name=name, grid=(R // tm,),
        in_specs=[spec, gspec, spec, spec], out_specs=[spec] * 4,
        out_shape=[jax.ShapeDtypeStruct((1, R, C), F32)] * 4,
        compiler_params=_cp(("parallel",)),
    )(w, g, m, v)


def _mods_part(c_all, w_ada, b_ada, *, name):
    C = w_ada.shape[1]

    def body(c_ref, w_ref, b_ref, o_ref):
        cv = c_ref[...]
        ca = cv * jax.nn.sigmoid(cv)
        o_ref[...] = jnp.dot(ca, w_ref[...], preferred_element_type=F32,
                             precision=lax.Precision.HIGHEST) + b_ref[...]

    return pl.pallas_call(
        body, name=name,
        out_shape=jax.ShapeDtypeStruct((N_DEV, C), F32),
        compiler_params=_cp(),
    )(c_all, w_ada, b_ada)


def _wada_grad(c_all_t, gm, *, name):
    C = gm.shape[1]

    def body(c_ref, g_ref, o_ref):
        cv = c_ref[...]
        ca = cv * jax.nn.sigmoid(cv)
        acc = ca[:, 0:1] * g_ref[0:1, :]
        for b in range(1, N_DEV):
            acc = acc + ca[:, b:b + 1] * g_ref[b:b + 1, :]
        o_ref[...] = acc

    return pl.pallas_call(
        body, name=name,
        out_shape=jax.ShapeDtypeStruct((D, C), F32),
        compiler_params=_cp(),
    )(c_all_t, gm)


def _peer(x, y, c, d):
    px = lax.rem(x + ((d >> 2) & 1), 2)
    py = lax.rem(y + ((d >> 1) & 1), 2)
    pc = lax.rem(c + (d & 1), 2)
    return (px, py, pc), 4 * px + 2 * py + pc


def _exchange(xs, *, scatter, name):
    n = len(xs)
    nsem = n * (N_DEV - 1)

    def body(*refs):
        ins, outs = refs[:n], refs[n:2 * n]
        token, send_sems, recv_sems, local_sems = refs[2 * n:]
        x, y, c = lax.axis_index("x"), lax.axis_index("y"), lax.axis_index("c")
        me = 4 * x + 2 * y + c
        token[...] = jnp.zeros_like(token)

        def src(t, idx):
            return ins[t].at[idx] if scatter else ins[t]

        local = [pltpu.make_async_copy(src(t, me), outs[t].at[me], local_sems.at[t]) for t in range(n)]
        for cp in local:
            cp.start()
        remote = []
        for t in range(n):
            for d in range(1, N_DEV):
                peer, pidx = _peer(x, y, c, d)
                k = t * (N_DEV - 1) + d - 1
                send = pltpu.make_async_remote_copy(src_ref=src(t, pidx), dst_ref=outs[t].at[me],
                                                    send_sem=send_sems.at[k], recv_sem=recv_sems.at[k],
                                                    device_id=peer, device_id_type=MESH)
                recv = pltpu.make_async_remote_copy(src_ref=src(t, pidx), dst_ref=outs[t].at[pidx],
                                                    send_sem=send_sems.at[k], recv_sem=recv_sems.at[k],
                                                    device_id=peer, device_id_type=MESH)
                send.start()
                remote.append((send, recv))
        for cp in local:
            cp.wait()
        for send, recv in remote:
            send.wait_send()
            recv.wait_recv()

    anyspec = pl.BlockSpec(memory_space=pl.ANY)
    out_shape = [jax.ShapeDtypeStruct(a.shape if scatter else (N_DEV,) + a.shape, a.dtype) for a in xs]
    out_shape.append(jax.ShapeDtypeStruct((8, 128), F32))
    return pl.pallas_call(
        body, name=name,
        in_specs=[anyspec] * n, out_specs=[anyspec] * n + [pl.BlockSpec(memory_space=pltpu.VMEM)],
        out_shape=out_shape,
        scratch_shapes=[pltpu.SemaphoreType.DMA((nsem,)), pltpu.SemaphoreType.DMA((nsem,)),
                        pltpu.SemaphoreType.DMA((n,))],
    )(*xs)


def _sum8(parts, *, name):
    _, R, C = parts.shape

    def body(p_ref, o_ref):
        acc = p_ref[0]
        for s in range(1, N_DEV):
            acc = acc + p_ref[s]
        o_ref[...] = acc

    return pl.pallas_call(body, name=name, out_shape=jax.ShapeDtypeStruct((R, C), F32),
                          compiler_params=_cp())(parts)


HBM_SPEC = pl.BlockSpec(memory_space=pltpu.HBM)
SEM_SPEC = pl.BlockSpec(memory_space=pltpu.SEMAPHORE)
N_PEER = N_DEV - 1


def _split_copies(src_refs, land_refs, send_sems, recv_sems, scatter):
    x, y, c = lax.axis_index("x"), lax.axis_index("y"), lax.axis_index("c")
    me = 4 * x + 2 * y + c
    pairs = []
    for j, (src, land) in enumerate(zip(src_refs, land_refs)):
        for d in range(1, N_DEV):
            peer, pidx = _peer(x, y, c, d)
            k = j * N_PEER + d - 1
            s = src.at[pidx] if scatter else src
            send = pltpu.make_async_remote_copy(src_ref=s, dst_ref=land.at[me], send_sem=send_sems.at[k],
                                                recv_sem=recv_sems.at[k], device_id=peer, device_id_type=MESH)
            recv = pltpu.make_async_remote_copy(src_ref=s, dst_ref=land.at[pidx], send_sem=send_sems.at[k],
                                                recv_sem=recv_sems.at[k], device_id=peer, device_id_type=MESH)
            pairs.append((send, recv))
    return pairs


def _own_slot(block, me):
    land = lax.empty((N_DEV,) + block.shape, block.dtype)
    return lax.dynamic_update_slice(land, block[None], (me, 0, 0))


def _split_start(srcs, lands, groups, *, scatter, name):
    n, ng = len(srcs), len(groups)

    def body(*refs):
        src_refs, land_refs = refs[:n], refs[n:2 * n]
        sems = refs[2 * n:2 * n + 2 * ng]
        token = refs[-1]
        for gi, g in enumerate(groups):
            pairs = _split_copies([src_refs[t] for t in g], [land_refs[t] for t in g], sems[2 * gi],
                                  sems[2 * gi + 1], scatter)
            for send, _ in pairs:
                send.start()
        token[...] = jnp.zeros_like(token)

    sem_shapes = []
    for g in groups:
        sem_shapes += [pltpu.SemaphoreType.DMA((len(g) * N_PEER,))] * 2
    thru = [pltpu.HBM(a.shape, a.dtype) for a in list(srcs) + list(lands)]
    outs = pl.pallas_call(
        body, name=name,
        out_shape=tuple(sem_shapes + thru + [jax.ShapeDtypeStruct((8, 128), F32)]),
        in_specs=[HBM_SPEC] * (2 * n),
        out_specs=tuple([SEM_SPEC] * (2 * ng) + [HBM_SPEC] * (2 * n) + [pl.BlockSpec(memory_space=pltpu.VMEM)]),
        input_output_aliases={i: 2 * ng + i for i in range(2 * n)},
        compiler_params=pltpu.CompilerParams(has_side_effects=pltpu.SideEffectType.DATAFLOW_SIDE_EFFECTING),
    )(*[pltpu.with_memory_space_constraint(a, pltpu.HBM) for a in list(srcs) + list(lands)])
    sems = [(outs[2 * gi], outs[2 * gi + 1]) for gi in range(ng)]
    return sems, outs[2 * ng:2 * ng + n], outs[2 * ng + n:2 * ng + 2 * n], outs[-1]


def _behind(v, token):
    if token is None:
        return v
    return v + token[0, 0].astype(v.dtype)


def _split_wait(srcs, lands, sems, after, *, scatter, name):
    m = len(srcs)

    def body(*refs):
        src_refs, land_refs = refs[:m], refs[m:2 * m]
        send_sems, recv_sems = refs[2 * m], refs[2 * m + 1]
        for send, recv in _split_copies(src_refs, land_refs, send_sems, recv_sems, scatter):
            send.wait_send()
            recv.wait_recv()

    outs = pl.pallas_call(
        body, name=name,
        out_shape=tuple(pltpu.HBM(a.shape, a.dtype) for a in list(srcs) + list(lands)),
        in_specs=[HBM_SPEC] * (2 * m) + [SEM_SPEC, SEM_SPEC, pl.BlockSpec(memory_space=pl.ANY)],
        out_specs=tuple([HBM_SPEC] * (2 * m)),
        input_output_aliases={i: i for i in range(2 * m)},
        compiler_params=pltpu.CompilerParams(has_side_effects=pltpu.SideEffectType.DATAFLOW_SIDE_EFFECTING),
    )(*srcs, *lands, sems[0], sems[1], after)
    return outs[m:]


TL_FIRST = (1, 2, 4, 6)
TL_ICI = (2, 4, 6)
EFFECT = pltpu.SideEffectType.DATAFLOW_SIDE_EFFECTING


def _tl_first(src_refs, land_refs, send_sems, recv_sems):
    x, y, c = lax.axis_index("x"), lax.axis_index("y"), lax.axis_index("c")
    me = 4 * x + 2 * y + c
    out = []
    for j, (src, land) in enumerate(zip(src_refs, land_refs)):
        for i, d in enumerate(TL_FIRST):
            peer, pidx = _peer(x, y, c, d)
            k = len(TL_FIRST) * j + i
            send = pltpu.make_async_remote_copy(src_ref=src, dst_ref=land.at[me], send_sem=send_sems.at[k],
                                                recv_sem=recv_sems.at[k], device_id=peer, device_id_type=MESH)
            recv = pltpu.make_async_remote_copy(src_ref=src, dst_ref=land.at[pidx], send_sem=send_sems.at[k],
                                                recv_sem=recv_sems.at[k], device_id=peer, device_id_type=MESH)
            out.append((d, send, recv))
    return out


def _tl_second(land_refs, send_sems, recv_sems):
    x, y, c = lax.axis_index("x"), lax.axis_index("y"), lax.axis_index("c")
    sibling, _ = _peer(x, y, c, 1)
    out = []
    for j, land in enumerate(land_refs):
        for i, d in enumerate(TL_ICI):
            _, mine = _peer(x, y, c, d)
            _, theirs = _peer(x, y, c, d + 1)
            k = len(TL_ICI) * j + i
            send = pltpu.make_async_remote_copy(src_ref=land.at[mine], dst_ref=land.at[mine], send_sem=send_sems.at[k],
                                                recv_sem=recv_sems.at[k], device_id=sibling, device_id_type=MESH)
            recv = pltpu.make_async_remote_copy(src_ref=land.at[mine], dst_ref=land.at[theirs],
                                                send_sem=send_sems.at[k], recv_sem=recv_sems.at[k],
                                                device_id=sibling, device_id_type=MESH)
            out.append((send, recv))
    return out


def _tl_start(srcs, lands, groups, *, name):
    n, ng = len(srcs), len(groups)

    def body(*refs):
        src_refs, land_refs = refs[:n], refs[n:2 * n]
        sems = refs[2 * n:2 * n + 2 * ng]
        for gi, g in enumerate(groups):
            for _, send, _ in _tl_first([src_refs[t] for t in g], [land_refs[t] for t in g], sems[2 * gi],
                                        sems[2 * gi + 1]):
                send.start()
        refs[-1][...] = jnp.zeros_like(refs[-1])

    sem_shapes = []
    for g in groups:
        sem_shapes += [pltpu.SemaphoreType.DMA((len(g) * len(TL_FIRST),))] * 2
    thru = [pltpu.HBM(a.shape, a.dtype) for a in list(srcs) + list(lands)]
    outs = pl.pallas_call(
        body, name=name,
        out_shape=tuple(sem_shapes + thru + [jax.ShapeDtypeStruct((8, 128), F32)]),
        in_specs=[HBM_SPEC] * (2 * n),
        out_specs=tuple([SEM_SPEC] * (2 * ng) + [HBM_SPEC] * (2 * n) + [pl.BlockSpec(memory_space=pltpu.VMEM)]),
        input_output_aliases={i: 2 * ng + i for i in range(2 * n)},
        compiler_params=pltpu.CompilerParams(has_side_effects=EFFECT),
    )(*[pltpu.with_memory_space_constraint(a, pltpu.HBM) for a in list(srcs) + list(lands)])
    sems = [(outs[2 * gi], outs[2 * gi + 1]) for gi in range(ng)]
    return sems, outs[2 * ng:2 * ng + n], outs[2 * ng + n:2 * ng + 2 * n], outs[-1]


def _tl_forward(srcs, lands, sems1, after, *, name):
    m = len(srcs)

    def body(*refs):
        src_refs, land_refs = refs[:m], refs[m:2 * m]
        send1, recv1 = refs[2 * m], refs[2 * m + 1]
        send2, recv2 = refs[2 * m + 3], refs[2 * m + 4]
        for d, _, recv in _tl_first(src_refs, land_refs, send1, recv1):
            if d in TL_ICI:
                recv.wait_recv()
        for send, _ in _tl_second(land_refs, send2, recv2):
            send.start()

    sem = pltpu.SemaphoreType.DMA((m * len(TL_ICI),))
    outs = pl.pallas_call(
        body, name=name,
        out_shape=tuple([sem, sem] + [pltpu.HBM(a.shape, a.dtype) for a in list(srcs) + list(lands)]),
        in_specs=[HBM_SPEC] * (2 * m) + [SEM_SPEC, SEM_SPEC, pl.BlockSpec(memory_space=pl.ANY)],
        out_specs=tuple([SEM_SPEC, SEM_SPEC] + [HBM_SPEC] * (2 * m)),
        input_output_aliases={i: 2 + i for i in range(2 * m)},
        compiler_params=pltpu.CompilerParams(has_side_effects=EFFECT),
    )(*srcs, *lands, sems1[0], sems1[1], after)
    return (outs[0], outs[1]), outs[2:2 + m], outs[2 + m:2 + 2 * m]


def _tl_wait(srcs, lands, sems1, sems2, after, *, name):
    m = len(srcs)

    def body(*refs):
        src_refs, land_refs = refs[:m], refs[m:2 * m]
        send1, recv1, send2, recv2 = refs[2 * m:2 * m + 4]
        for d, send, recv in _tl_first(src_refs, land_refs, send1, recv1):
            send.wait_send()
            if d not in TL_ICI:
                recv.wait_recv()
        for send, recv in _tl_second(land_refs, send2, recv2):
            send.wait_send()
            recv.wait_recv()

    outs = pl.pallas_call(
        body, name=name,
        out_shape=tuple(pltpu.HBM(a.shape, a.dtype) for a in list(srcs) + list(lands)),
        in_specs=[HBM_SPEC] * (2 * m) + [SEM_SPEC] * 4 + [pl.BlockSpec(memory_space=pl.ANY)],
        out_specs=tuple([HBM_SPEC] * (2 * m)),
        input_output_aliases={i: i for i in range(2 * m)},
        compiler_params=pltpu.CompilerParams(has_side_effects=EFFECT),
    )(*srcs, *lands, sems1[0], sems1[1], sems2[0], sems2[1], after)
    return outs[m:]


TM_PROJ = 512
TN_PROJ = 512
TM_ROW = 512
TM_NN = 512
TK_TN = 2048
TM_ADAM = 416
TN_FFN = F // 2
TN_IN = NIN // 4


def _tn(a, b, name, tn):
    if a.ndim == 2:
        a = a[None]
    return _tn_matmul(a, b, tn=tn, tk=TK_TN, name=name)


def _local_step(x, tgt, mods, g1, gm, g2, gf, convw8, sinks, w_get, g_put):
    T = x.shape[0]
    sh1, sc1, gt1, sh2, sc2, gt2, sh3, sc3, gt3 = [mods[i:i + 1] for i in range(N_MOD)]
    cos, sin = _rope_tables(T)
    behind = _behind

    w = dict(w_get("gu1", mods))
    h1, ab1 = _norm_proj(x, g1, sc1, sh1, w["gu1"], tm=TM_PROJ, tn=TN_PROJ, name="ffn1_up")
    w.update(w_get("d1", ab1))
    x1, y1 = _ffn_down_fwd(ab1, w["d1"], x, gt1, tm=TM_ROW, name="ffn1_down")
    w.update(w_get("mix", x1))
    h2, proj = _norm_proj(x1, gm, sc2, sh2, w["win"], tm=TM_PROJ, tn=TN_PROJ, name="mix_in")
    qs, kr = _attn_prep(proj, cos, sin, name="attn_prep")
    bias = _attn_bias()
    attn, lse = _attn_fwd(qs, kr, proj, bias, sinks, name="attn_fwd")
    x2, gc, yc, ya, mg, o = _mixer_mid_fwd(proj, attn, w["cp"], w["ap"], w["out"], convw8, x1, gt2,
                                           tm=TM_ROW, name="mix_mid")
    w.update(w_get("ffn2", x2))
    h3, ab2, y2, dx3, lsum, dgf = _ffn_fwd(x2, g2, sc3, sh3, gt3, w["gu2"], w["d2"], (tgt, gf), tm=TM_ROW,
                                           name="ffn2_final")

    dy2, dab2, dgt3 = _ffn_down_bwd(dx3, y2, gt3, ab2, w["d2"], tm=TM_ROW, tn=MXU_N, name="ffn2_down_bwd")
    g_d2 = _tn_matmul_swiglu(ab2, dy2, None, tn=TN_FFN, tk=TK_TN, name="ffn2_down_dw")
    dx2, dsh3, dsc3, dg2 = _nn_bwd_norm(dab2, w["gu2"], x2, g2, sc3, dx3, tm=TM_NN, name="ffn2_up_bwd")
    g_gu2 = _tn(dab2, h3, "ffn2_up_dw", TN_FFN)
    tok = g_put(dict(gu2=g_gu2, d2=g_d2))

    dout, dyc, dya, dgc, dat, dproj, dgt2 = _mixer_mid_bwd(dx2, behind(gt2, tok), o, proj, yc, ya, w["out"], w["cp"],
                                                           w["ap"], tm=TM_ROW, name="mix_mid_bwd")
    g_out = _tn(mg, dout, "mix_out_dw", D)
    g_cp = _tn(gc, dyc, "mix_cp_dw", D)
    g_ap = _tn(attn, dya, "mix_ap_dw", D)
    dproj, dkc, dkp, dvc, dvp, dsink = _attn_bwd(qs, kr, proj, bias, sinks, lse, attn, dat, cos, sin, dproj,
                                                 name="attn_bwd")
    dproj = _dkv_combine(dkc, dkp, dvc, dvp, dproj, name="attn_dkv")
    dproj, dcw = _conv_bwd(dgc, proj, convw8, dproj, tm=TM_ROW, name="conv_bwd")
    g_in = _tn(dproj, h2, "mix_in_dw", TN_IN)
    tok = g_put(dict(win=g_in, cp=g_cp, ap=g_ap, out=g_out))
    dx1, dsh2, dsc2, dgm = _nn_bwd_norm(dproj[None], w["win"], x1, gm, behind(sc2, tok), dx2, tm=TM_NN,
                                        name="mix_in_bwd")

    dy1, dab1, dgt1 = _ffn_down_bwd(dx1, y1, gt1, ab1, w["d1"], tm=TM_ROW, tn=MXU_N, name="ffn1_down_bwd")
    g_gu1 = _tn(dab1, h1, "ffn1_up_dw", TN_FFN)
    tok = g_put(dict(gu1=g_gu1))
    g_d1 = _tn_matmul_swiglu(ab1, dy1, tok, tn=TN_FFN, tk=TK_TN, name="ffn1_down_dw")
    tok = g_put(dict(d1=g_d1))
    dx0, dsh1, dsc1, dg1 = _nn_bwd_norm(dab1, w["gu1"], x, g1, behind(sc1, tok), dx1, tm=TM_NN,
                                        name="ffn1_up_bwd")

    small = dict(mods=jnp.concatenate([dsh1, dsc1, dgt1, dsh2, dsc2, dgt2, dsh3, dsc3, dgt3], axis=0),
                 g1=dg1, gm=dgm, g2=dg2, gf=dgf, convw=dcw[0:3], sinks=dsink[:, 0:N_HEADS])
    return lsum, dx0, small


BIG = ("gu1", "d1", "win", "cp", "ap", "out", "gu2", "d2")
TRANSPOSED = ("gu1", "win", "gu2")
SMALL_ROWS = 24
R_MODS, R_G1, R_GM, R_G2, R_GF, R_CONV, R_SINK = 0, 9, 10, 11, 12, 13, 16


def _pad_to(a, rows, cols):
    return jnp.pad(a, ((0, rows - a.shape[0]), (0, cols - a.shape[1])))


def _pack_small(b_ada, g1, gm, g2, gf, conv, sinks):
    rows = [b_ada.reshape(N_MOD, D), g1.reshape(1, D), gm.reshape(1, D), g2.reshape(1, D), gf.reshape(1, D),
            _pad_to(conv.reshape(3, -1), 3, D), _pad_to(sinks.reshape(1, N_HEADS), 1, D)]
    return _pad_to(jnp.concatenate(rows, axis=0), SMALL_ROWS, D)


def _unpack_small(p, conv_cols):
    return dict(b_ada=p[R_MODS:R_MODS + N_MOD].reshape(1, N_MOD * D), g_ffn1=p[R_G1:R_G1 + 1],
                g_mix=p[R_GM:R_GM + 1], g_ffn2=p[R_G2:R_G2 + 1], g_final=p[R_GF],
                conv_w=p[R_CONV:R_CONV + 3, 0:conv_cols][None], sinks=p[R_SINK:R_SINK + 1, 0:N_HEADS])


def kernel(x, c, w_ada, b_ada, g_ffn1, w1_gu, w1_down, g_mix, w_in, conv_w, w_conv_proj, w_attn_proj, sinks, w_out, g_ffn2, w2_gu, w2_down, g_final, loss_target, m_w_ada, m_b_ada, m_g_ffn1, m_w1_gu, m_w1_down, m_g_mix, m_w_in, m_conv_w, m_w_conv_proj, m_w_attn_proj, m_sinks, m_w_out, m_g_ffn2, m_w2_gu, m_w2_down, m_g_final, v_w_ada, v_b_ada, v_g_ffn1, v_w1_gu, v_w1_down, v_g_mix, v_w_in, v_conv_w, v_w_conv_proj, v_w_attn_proj, v_sinks, v_w_out, v_g_ffn2, v_w2_gu, v_w2_down, v_g_final):
    me = 4 * lax.axis_index("x") + 2 * lax.axis_index("y") + lax.axis_index("c")
    ada_cols = w_ada.shape[2]
    conv_cols = conv_w.shape[2]

    native = dict(gu1=w1_gu[0], d1=w1_down[0], win=w_in[0], cp=w_conv_proj[0], ap=w_attn_proj[0], out=w_out[0],
                  gu2=w2_gu[0], d2=w2_down[0])

    def shard(n, token):
        a = _behind(native[n], token)
        return (a.T if n in TRANSPOSED else a).astype(BF)

    c_all, conv_all, _ = _exchange([c, _pad_to(conv_w[0], 8, conv_cols)], scatter=False, name="gather_cond")
    c_all = c_all.reshape(N_DEV, D)
    conv_full = conv_all[:, 0:3, :].transpose(1, 0, 2).reshape(3, D)

    b_cols = lax.dynamic_slice(b_ada, (0, me * ada_cols), (1, ada_cols))
    mods_cols = _mods_part(c_all, w_ada[0], b_cols, name="ada_mods")
    mods_all, mods_token = _exchange([mods_cols], scatter=False, name="gather_mods")
    mods = lax.dynamic_index_in_dim(mods_all, me, axis=1, keepdims=False).reshape(N_MOD, D)

    groups = dict(gu1=("gu1",), d1=("d1",), mix=("win", "cp", "ap", "out"), ffn2=("gu2", "d2"))
    in_flight = {}
    first = [shard("gu1", mods_token)]
    sems, srcs, lands, token = _tl_start(first, [_own_slot(s, me) for s in first], [[0]],
                                         name="gather_weights_start_gu1")
    in_flight["gu1"] = [sems[0], srcs, lands, None]
    rest = [n for n in BIG if n != "gu1"]
    shards = [shard(n, token) for n in rest]
    rest_groups = [[rest.index(n) for n in names] for g, names in groups.items() if g != "gu1"]
    sems, srcs, lands, rest_token = _tl_start(shards, [_own_slot(s, me) for s in shards], rest_groups,
                                              name="gather_weights_start_rest")
    for (g, names), gsems, idx in zip([kv for kv in groups.items() if kv[0] != "gu1"], sems, rest_groups):
        in_flight[g] = [gsems, [srcs[t] for t in idx], [lands[t] for t in idx], None]

    def forward(group, after):
        sems1, gsrcs, glands, _ = in_flight[group]
        sems2, gsrcs, glands = _tl_forward(gsrcs, glands, sems1, after, name="gather_weights_forward_" + group)
        in_flight[group] = [sems1, gsrcs, glands, sems2]

    forward_early = dict(d1="mix", mix="ffn2")

    def w_get(group, after):
        if group == "gu1":
            after = rest_token
        if in_flight[group][3] is None:
            forward(group, after)
        sems1, gsrcs, glands, sems2 = in_flight[group]
        landed = _tl_wait(gsrcs, glands, sems1, sems2, after, name="gather_weights_wait_" + group)
        if group in forward_early:
            forward(forward_early[group], landed[0])
        return {n: a.reshape(-1, D) for n, a in zip(groups[group], landed)}

    pending = []

    def g_put(gs):
        names = tuple(gs)
        srcs = [gs[n].reshape(N_DEV, -1, D) for n in names]
        lands = [_own_slot(lax.dynamic_index_in_dim(s, me, axis=0, keepdims=False), me) for s in srcs]
        sems, srcs, lands, tok = _split_start(srcs, lands, [list(range(len(names)))], scatter=True,
                                              name="scatter_grads_start_" + names[0])
        pending.append((names, sems[0], srcs, lands))
        return tok

    lsum, grad_x, small = _local_step(x[0], loss_target[0], mods, g_ffn1, g_mix, g_ffn2, g_final[None],
                                      _pad_to(conv_full, 8, D), sinks[0], w_get, g_put)
    loss = lax.psum((0.5 / D) * jnp.sum(lsum), ("x", "y", "c"))

    packed = _pack_small(small["mods"], small["g1"], small["gm"], small["g2"], small["gf"], small["convw"],
                         small["sinks"])
    packed_all, _ = _exchange([packed], scatter=False, name="gather_small")
    gsmall = _sum8(packed_all, name="sum_small")

    w_of = dict(ada=w_ada, gu1=w1_gu, d1=w1_down, win=w_in, cp=w_conv_proj, ap=w_attn_proj, out=w_out, gu2=w2_gu,
                d2=w2_down)
    m_of = dict(ada=m_w_ada, gu1=m_w1_gu, d1=m_w1_down, win=m_w_in, cp=m_w_conv_proj, ap=m_w_attn_proj, out=m_w_out,
                gu2=m_w2_gu, d2=m_w2_down)
    v_of = dict(ada=v_w_ada, gu1=v_w1_gu, d1=v_w1_down, win=v_w_in, cp=v_w_conv_proj, ap=v_w_attn_proj, out=v_w_out,
                gu2=v_w2_gu, d2=v_w2_down)
    upd = {}
    after = gsmall
    for names, sems, srcs, lands in pending:
        parts = _split_wait(srcs, lands, sems, after, scatter=True, name="scatter_grads_wait_" + names[0])
        for n, p in zip(names, parts):
            if n in TRANSPOSED:
                res = _adam(jnp.swapaxes(w_of[n], 1, 2), p, jnp.swapaxes(m_of[n], 1, 2), jnp.swapaxes(v_of[n], 1, 2),
                            tm=TM_ADAM, name="adam_" + n)
                upd[n] = [jnp.swapaxes(t, 1, 2) for t in res]
            else:
                upd[n] = _adam(w_of[n], p, m_of[n], v_of[n], tm=TM_ADAM, name="adam_" + n)
        after = upd[names[-1]][1]

    gm_cols = lax.dynamic_slice(packed_all[:, R_MODS:R_MODS + N_MOD, :].reshape(N_DEV, N_MOD * D),
                                (0, me * ada_cols), (N_DEV, ada_cols))
    upd["ada"] = _adam(w_ada, _wada_grad(c_all.T, gm_cols, name="ada_dw"), m_w_ada, v_w_ada, tm=256, name="adam_ada")
    conv_g = lax.dynamic_slice(gsmall[R_CONV:R_CONV + 3], (0, me * conv_cols), (3, conv_cols))
    gsmall_own = gsmall.at[R_CONV:R_CONV + 3].set(_pad_to(conv_g, 3, D))
    small_upd = _adam(_pack_small(b_ada, g_ffn1, g_mix, g_ffn2, g_final, conv_w, sinks)[None], gsmall_own,
                      _pack_small(m_b_ada, m_g_ffn1, m_g_mix, m_g_ffn2, m_g_final, m_conv_w, m_sinks)[None],
                      _pack_small(v_b_ada, v_g_ffn1, v_g_mix, v_g_ffn2, v_g_final, v_conv_w, v_sinks)[None],
                      tm=SMALL_ROWS, name="adam_small")
    small_out = [_unpack_small(p[0], conv_cols) for p in small_upd]

    big_name = dict(w_ada="ada", w1_gu="gu1", w1_down="d1", w_in="win", w_conv_proj="cp", w_attn_proj="ap",
                    w_out="out", w2_gu="gu2", w2_down="d2")
    order = ("w_ada", "b_ada", "g_ffn1", "w1_gu", "w1_down", "g_mix", "w_in", "conv_w", "w_conv_proj", "w_attn_proj",
             "sinks", "w_out", "g_ffn2", "w2_gu", "w2_down", "g_final")
    outs = [loss, grad_x[None]]
    for kind in range(4):
        for n in order:
            outs.append(upd[big_name[n]][kind] if n in big_name else small_out[kind][n])
    return tuple(outs)
```

```python
import functools

import jax
import jax.numpy as jnp
from jax import lax
from jax.experimental import pallas as pl
from jax.experimental.pallas import tpu as pltpu

D = 1024
F = 2816
NIN = 6656
N_HEADS = 16
N_KV = 4
HEAD_DIM = 64
BLK = 128
N_MOD = 9
N_DEV = 8
EPS = 1e-6
NEG_INF = -1e30
ROPE_THETA = 10000.0
O_BG, O_CG, O_U, O_Q, O_K, O_V, O_ZC, O_ZA = 0, 1024, 2048, 3072, 4096, 4352, 4608, 5632

ADAM_LR = 0.001
ADAM_B1 = 0.9
ADAM_B2 = 0.999
ADAM_EPS = 1e-08
ADAM_WD = 0.01
ADAM_STEP = 10

BF = jnp.bfloat16
F32 = jnp.float32
VMEM_LIMIT = 56 * 1024 * 1024
MXU_N = 256
MESH = pl.DeviceIdType.MESH

NT = (((1,), (1,)), ((), ()))
TN = (((0,), (0,)), ((), ()))


def _cp(sem=None):
    return pltpu.CompilerParams(dimension_semantics=sem, vmem_limit_bytes=VMEM_LIMIT)


def _tile(n, pref):
    if n <= pref:
        return n
    for t in range(pref - pref % 16, 15, -16):
        if n % t == 0:
            return t
    raise ValueError((n, pref))


def _sigmoid(v):
    return 0.5 * jnp.tanh(0.5 * v) + 0.5


def _row(i):
    return (i, 0)


def _const2(*_):
    return (0, 0)


def _resident(shape):
    return pl.BlockSpec(shape, lambda *_: (0,) * len(shape), pipeline_mode=pl.Buffered(1))


def _norm_proj(x, g, sc, sh, wt, *, tm, tn, name):
    T, N = x.shape[0], wt.shape[0]
    tm = _tile(T, tm)

    def body(x_ref, g_ref, sc_ref, sh_ref, w_ref, h_ref, o_ref):
        xv = x_ref[...]
        r = lax.rsqrt(jnp.mean(xv * xv, axis=-1, keepdims=True) + EPS)
        hb = ((xv * r) * g_ref[...] * (1.0 + sc_ref[...]) + sh_ref[...]).astype(BF)
        h_ref[...] = hb
        for c0 in range(0, N, tn):
            cols = pl.ds(c0, tn)
            o_ref[:, cols] = lax.dot_general(hb, w_ref[cols, :], NT, preferred_element_type=F32).astype(BF)

    vec = pl.BlockSpec((1, D), _const2)
    return pl.pallas_call(
        body, name=name, grid=(T // tm,),
        in_specs=[pl.BlockSpec((tm, D), _row), vec, vec, vec, _resident((N, D))],
        out_specs=[pl.BlockSpec((tm, D), _row), pl.BlockSpec((tm, N), _row)],
        out_shape=[jax.ShapeDtypeStruct((T, D), BF), jax.ShapeDtypeStruct((T, N), BF)],
        compiler_params=_cp(("parallel",)),
    )(x, g, sc, sh, wt)


def _ffn_down_fwd(ab, wd, x, gt, *, tm, name):
    T = x.shape[0]
    tm = _tile(T, tm)

    def body(a_ref, b_ref, wd_ref, x_ref, gt_ref, xo_ref, y_ref):
        y = None
        for c0 in range(0, F, MXU_N):
            cols = pl.ds(c0, MXU_N)
            a = a_ref[:, cols].astype(F32)
            act = (a * _sigmoid(a) * b_ref[:, cols].astype(F32)).astype(BF)
            part = jnp.dot(act, wd_ref[cols, :], preferred_element_type=F32)
            y = part if y is None else y + part
        y_ref[...] = y.astype(BF)
        xo_ref[...] = x_ref[...] + (0.5 * gt_ref[...]) * y

    return pl.pallas_call(
        body, name=name, grid=(T // tm,),
        in_specs=[pl.BlockSpec((tm, F), lambda i: (i, 0)), pl.BlockSpec((tm, F), lambda i: (i, 1)),
                  _resident((F, D)), pl.BlockSpec((tm, D), _row), pl.BlockSpec((1, D), _const2)],
        out_specs=[pl.BlockSpec((tm, D), _row), pl.BlockSpec((tm, D), _row)],
        out_shape=[jax.ShapeDtypeStruct((T, D), F32), jax.ShapeDtypeStruct((T, D), BF)],
        compiler_params=_cp(("parallel",)),
    )(ab, ab, wd, x, gt)


def _ffn_fwd(x, g, sc, sh, gt, wgu, wd, final, *, tm, name):
    T = x.shape[0]
    tm = _tile(T, tm)
    last = final is not None

    def body(x_ref, g_ref, sc_ref, sh_ref, gt_ref, wgu_ref, wd_ref, *rest):
        if last:
            t_ref, gf_ref, h_ref, ab_ref, y_ref, dx_ref, ls_ref, dgf_ref = rest
        else:
            h_ref, ab_ref, y_ref, xo_ref = rest
        xv = x_ref[...]
        r = lax.rsqrt(jnp.mean(xv * xv, axis=-1, keepdims=True) + EPS)
        hb = ((xv * r) * g_ref[...] * (1.0 + sc_ref[...]) + sh_ref[...]).astype(BF)
        h_ref[...] = hb
        y = None
        for c0 in range(0, F, MXU_N):
            a = lax.dot_general(hb, wgu_ref[pl.ds(c0, MXU_N), :], NT, preferred_element_type=F32)
            b = lax.dot_general(hb, wgu_ref[pl.ds(F + c0, MXU_N), :], NT, preferred_element_type=F32)
            ab = a.astype(BF)
            bb = b.astype(BF)
            ab_ref[:, pl.ds(c0, MXU_N)] = ab
            ab_ref[:, pl.ds(F + c0, MXU_N)] = bb
            a = ab.astype(F32)
            act = (a * _sigmoid(a) * bb.astype(F32)).astype(BF)
            part = jnp.dot(act, wd_ref[pl.ds(c0, MXU_N), :], preferred_element_type=F32)
            y = part if y is None else y + part
        y_ref[...] = y.astype(BF)
        xo = xv + (0.5 * gt_ref[...]) * y
        if not last:
            xo_ref[...] = xo
            return

        @pl.when(pl.program_id(0) == 0)
        def _():
            ls_ref[...] = jnp.zeros_like(ls_ref)
            dgf_ref[...] = jnp.zeros_like(dgf_ref)
        gv = gf_ref[...]
        r = lax.rsqrt(jnp.mean(xo * xo, axis=-1, keepdims=True) + EPS)
        xh = xo * r
        e = xh * gv - t_ref[...]
        ls_ref[...] += jnp.sum(e * e, axis=0, keepdims=True)
        dy = e * (1.0 / D)
        dgf_ref[...] += jnp.sum(dy * xh, axis=0, keepdims=True)
        dxh = dy * gv
        dx_ref[...] = r * (dxh - xh * jnp.mean(dxh * xh, axis=-1, keepdims=True))

    vec = pl.BlockSpec((1, D), _const2)
    rowspec = pl.BlockSpec((tm, D), _row)
    in_specs = [rowspec, vec, vec, vec, vec, _resident((2 * F, D)), _resident((F, D))]
    out_specs = [rowspec, pl.BlockSpec((tm, 2 * F), _row), rowspec, rowspec]
    out_shape = [jax.ShapeDtypeStruct((T, D), BF), jax.ShapeDtypeStruct((T, 2 * F), BF),
                 jax.ShapeDtypeStruct((T, D), BF), jax.ShapeDtypeStruct((T, D), F32)]
    args = [x, g, sc, sh, gt, wgu, wd]
    if last:
        in_specs += [rowspec, vec]
        out_specs += [vec, vec]
        out_shape += [jax.ShapeDtypeStruct((1, D), F32)] * 2
        args += list(final)
    return pl.pallas_call(
        body, name=name, grid=(T // tm,),
        in_specs=in_specs, out_specs=out_specs, out_shape=out_shape,
        compiler_params=_cp(("arbitrary",) if last else ("parallel",)),
    )(*args)


def _ffn_down_bwd(dxo, y, gt, ab, wd, *, tm, tn, name):
    T = dxo.shape[0]
    tm = _tile(T, tm)

    def body(dxo_ref, y_ref, gt_ref, a_ref, b_ref, wd_ref, dy_ref, dab_ref, dgt_ref):
        @pl.when(pl.program_id(0) == 0)
        def _():
            dgt_ref[...] = jnp.zeros_like(dgt_ref)

        dxv = dxo_ref[...]
        dgt_ref[...] += 0.5 * jnp.sum(dxv * y_ref[...].astype(F32), axis=0, keepdims=True)
        dy = ((0.5 * gt_ref[...]) * dxv).astype(BF)
        dy_ref[...] = dy
        for c0 in range(0, F, tn):
            cols = pl.ds(c0, tn)
            dact = lax.dot_general(dy, wd_ref[cols, :], NT, preferred_element_type=F32)
            a = a_ref[:, cols].astype(F32)
            b = b_ref[:, cols].astype(F32)
            s = _sigmoid(a)
            dab_ref[0, :, cols] = (dact * b * (s * (1.0 + a * (1.0 - s)))).astype(BF)
            dab_ref[1, :, cols] = (dact * (a * s)).astype(BF)

    vec = pl.BlockSpec((1, D), _const2)
    rowspec = pl.BlockSpec((tm, D), _row)
    return pl.pallas_call(
        body, name=name, grid=(T // tm,),
        in_specs=[rowspec, rowspec, vec, pl.BlockSpec((tm, F), lambda i: (i, 0)),
                  pl.BlockSpec((tm, F), lambda i: (i, 1)), pl.BlockSpec((F, D), _const2)],
        out_specs=[rowspec, pl.BlockSpec((2, tm, F), lambda i: (0, i, 0)), vec],
        out_shape=[jax.ShapeDtypeStruct((T, D), BF), jax.ShapeDtypeStruct((2, T, F), BF),
                   jax.ShapeDtypeStruct((1, D), F32)],
        compiler_params=_cp(("arbitrary",)),
    )(dxo, y, gt, ab, ab, wd)


def _ffn_down_bwd_dw(dxo, y, gt, ab, wd, *, tm, name):
    T = dxo.shape[0]
    tm = _tile(T, tm)
    nt = T // tm
    hw = F // 2
    chunks = [(c0, min(MXU_N, hw - c0)) for c0 in range(0, hw, MXU_N)]

    def body(dxo_ref, y_ref, gt_ref, a_ref, b_ref, wd_ref, dab_ref, dgt_ref, dwd_ref, dys, dyt, acc, stage, sem):
        i, j = pl.program_id(0), pl.program_id(1)

        @pl.when(jnp.logical_and(i == 0, j == 0))
        def _():
            dgt_ref[...] = jnp.zeros_like(dgt_ref)

        @pl.when(i == 0)
        def _():
            acc[j] = jnp.zeros((D, hw), F32)

        @pl.when(j == 0)
        def _():
            dxv = dxo_ref[...]
            dgt_ref[...] += 0.5 * jnp.sum(dxv * y_ref[...].astype(F32), axis=0, keepdims=True)
            dyf = (0.5 * gt_ref[...]) * dxv
            dys[...] = dyf.astype(BF)
            dyt[...] = dyf.T.astype(BF)

        dy = dys[...]
        dy_t = dyt[...]
        for c0, cw in chunks:
            cols = pl.ds(c0, cw)
            w_rows = pl.ds(pl.multiple_of(j * hw + c0, 128), cw)
            dact = lax.dot_general(dy, wd_ref[w_rows, :], NT, preferred_element_type=F32)
            a = a_ref[:, cols].astype(F32)
            b = b_ref[:, cols].astype(F32)
            s = _sigmoid(a)
            silu = a * s
            dab_ref[0, :, cols] = (dact * b * (s * (1.0 + a * (1.0 - s)))).astype(BF)
            dab_ref[1, :, cols] = (dact * silu).astype(BF)
            acc[j, :, cols] += jnp.dot(dy_t, (silu * b).astype(BF), preferred_element_type=F32)

        @pl.when(i == nt - 1)
        def _():
            for c0, cw in chunks:
                stage[0:cw, :] = acc[j, :, pl.ds(c0, cw)].T.astype(BF)
                out = pltpu.make_async_copy(stage.at[pl.ds(0, cw)],
                                            dwd_ref.at[pl.ds(pl.multiple_of(j * hw + c0, 128), cw)], sem)
                out.start()
                out.wait()

    vec = pl.BlockSpec((1, D), _const2)
    rowspec = pl.BlockSpec((tm, D), lambda i, j: (i, 0))
    return pl.pallas_call(
        body, name=name, grid=(nt, 2),
        in_specs=[rowspec, rowspec, vec, pl.BlockSpec((tm, hw), lambda i, j: (i, j)),
                  pl.BlockSpec((tm, hw), lambda i, j: (i, j + 2)), _resident((F, D))],
        out_specs=[pl.BlockSpec((2, tm, hw), lambda i, j: (0, i, j)), vec, pl.BlockSpec(memory_space=pl.ANY)],
        out_shape=[jax.ShapeDtypeStruct((2, T, F), BF), jax.ShapeDtypeStruct((1, D), F32),
                   jax.ShapeDtypeStruct((F, D), BF)],
        scratch_shapes=[pltpu.VMEM((tm, D), BF), pltpu.VMEM((D, tm), BF), pltpu.VMEM((2, D, hw), F32),
                        pltpu.VMEM((MXU_N, D), BF), pltpu.SemaphoreType.DMA(())],
        compiler_params=_cp(("arbitrary", "arbitrary")),
    )(dxo, y, gt, ab, ab, wd)


def _tn_matmul(a, b, *, tn, tk, name):
    S, T, Ns = a.shape
    tn, tk = _tile(Ns, tn), _tile(T, tk)
    nk, njs = T // tk, Ns // tn

    def body(a_ref, b_ref, o_ref, acc):
        k = pl.program_id(1)

        @pl.when(k == 0)
        def _():
            acc[...] = jnp.zeros_like(acc)
        acc[...] += lax.dot_general(a_ref[0], b_ref[...], TN, preferred_element_type=F32)

        @pl.when(k == nk - 1)
        def _():
            o_ref[...] = acc[...].astype(BF)

    return pl.pallas_call(
        body, name=name, grid=(S * njs, nk),
        in_specs=[pl.BlockSpec((1, tk, tn), lambda j, k: (j // njs, k, j % njs)),
                  pl.BlockSpec((tk, D), lambda j, k: (k, 0))],
        out_specs=pl.BlockSpec((tn, D), lambda j, k: (j, 0)),
        out_shape=jax.ShapeDtypeStruct((S * Ns, D), BF),
        scratch_shapes=[pltpu.VMEM((tn, D), F32)],
        compiler_params=_cp(("parallel", "arbitrary")),
    )(a, b)


def _tn_matmul_swiglu(ab, b, token, *, tn, tk, name):
    T = ab.shape[0]
    tn, tk = _tile(F, tn), _tile(T, tk)
    nk, nj = T // tk, F // tn
    deps = [] if token is None else [token]

    def body(a_ref, g_ref, b_ref, *rest):
        o_ref, acc = rest[len(deps):]
        k = pl.program_id(1)

        @pl.when(k == 0)
        def _():
            acc[...] = jnp.zeros_like(acc)
        bv = b_ref[...]
        for c0 in range(0, tn, MXU_N):
            cw = min(MXU_N, tn - c0)
            cols = pl.ds(c0, cw)
            a = a_ref[:, cols].astype(F32)
            act = (a * _sigmoid(a) * g_ref[:, cols].astype(F32)).astype(BF)
            acc[cols, :] += lax.dot_general(act, bv, TN, preferred_element_type=F32)

        @pl.when(k == nk - 1)
        def _():
            o_ref[...] = acc[...].astype(BF)

    return pl.pallas_call(
        body, name=name, grid=(nj, nk),
        in_specs=[pl.BlockSpec((tk, tn), lambda j, k: (k, j)), pl.BlockSpec((tk, tn), lambda j, k: (k, j + nj)),
                  pl.BlockSpec((tk, D), lambda j, k: (k, 0))] + [pl.BlockSpec(memory_space=pl.ANY)] * len(deps),
        out_specs=pl.BlockSpec((tn, D), lambda j, k: (j, 0)),
        out_shape=jax.ShapeDtypeStruct((F, D), BF),
        scratch_shapes=[pltpu.VMEM((tn, D), F32)],
        compiler_params=_cp(("parallel", "arbitrary")),
    )(ab, ab, b, *deps)


def _nn_bwd_norm(da, w, x, g, sc, dxo, *, tm, name):
    S, T, Ks = da.shape
    tm = _tile(T, tm)
    rc = _tile(tm, 256)

    def body(da_ref, w_ref, x_ref, g_ref, sc_ref, dxo_ref, dx_ref, dsh_ref, dsc_ref, dg_ref, acc):
        @pl.when(pl.program_id(0) == 0)
        def _():
            dsh_ref[...] = jnp.zeros_like(dsh_ref)
            dsc_ref[...] = jnp.zeros_like(dsc_ref)
            dg_ref[...] = jnp.zeros_like(dg_ref)

        d = jnp.dot(da_ref[0], w_ref[0:Ks, :], preferred_element_type=F32)
        for s in range(1, S):
            d = d + jnp.dot(da_ref[s], w_ref[s * Ks:(s + 1) * Ks, :], preferred_element_type=F32)
        acc[...] = d
        gv = g_ref[...]
        sc1 = 1.0 + sc_ref[...]
        dsh = jnp.zeros((1, D), F32)
        dsc = jnp.zeros((1, D), F32)
        dg = jnp.zeros((1, D), F32)
        for r0 in range(0, tm, rc):
            rows = pl.ds(r0, rc)
            u = acc[rows, :]
            xv = x_ref[rows, :]
            r = lax.rsqrt(jnp.mean(xv * xv, axis=-1, keepdims=True) + EPS)
            xh = xv * r
            dsh = dsh + jnp.sum(u, axis=0, keepdims=True)
            dsc = dsc + jnp.sum(u * (xh * gv), axis=0, keepdims=True)
            us = u * sc1
            dg = dg + jnp.sum(us * xh, axis=0, keepdims=True)
            dxh = us * gv
            dx_ref[rows, :] = dxo_ref[rows, :] + r * (dxh - xh * jnp.mean(dxh * xh, axis=-1, keepdims=True))
        dsh_ref[...] += dsh
        dsc_ref[...] += dsc
        dg_ref[...] += dg

    vec = pl.BlockSpec((1, D), _const2)
    rowspec = pl.BlockSpec((tm, D), _row)
    return pl.pallas_call(
        body, name=name, grid=(T // tm,),
        in_specs=[pl.BlockSpec((S, tm, Ks), lambda i: (0, i, 0)), _resident((S * Ks, D)), rowspec, vec, vec, rowspec],
        out_specs=[rowspec, vec, vec, vec],
        out_shape=[jax.ShapeDtypeStruct((T, D), F32)] + [jax.ShapeDtypeStruct((1, D), F32)] * 3,
        scratch_shapes=[pltpu.VMEM((tm, D), F32)],
        compiler_params=_cp(("arbitrary",)),
    )(da, w, x, g, sc, dxo)


def _rope(t, cos, sin_signed, lt32, inverse=False):
    sel = jnp.where(lt32, pltpu.roll(t, 96, 1), pltpu.roll(t, 32, 1))
    return t * cos - sel * sin_signed if inverse else t * cos + sel * sin_signed


def _rope_tables(T):
    inv = 1.0 / (ROPE_THETA ** (jnp.arange(0, HEAD_DIM, 2, dtype=F32) / HEAD_DIM))
    ang = jnp.arange(T, dtype=F32)[:, None] * inv[None, :]
    cos, sin = jnp.cos(ang), jnp.sin(ang)
    cos128 = jnp.tile(cos, (1, 4))
    sin128 = jnp.tile(jnp.concatenate([-sin, sin], axis=1), (1, 2))
    return cos128, sin128


QSCALE = HEAD_DIM ** -0.5


def _lane_masks(rows):
    lane = lax.broadcasted_iota(jnp.int32, (rows, 128), 1)
    return (lane % HEAD_DIM) < (HEAD_DIM // 2), [lane < HEAD_DIM, lane >= HEAD_DIM]


def _attn_bias():
    qi = lax.broadcasted_iota(jnp.int32, (4 * BLK, 2 * BLK), 0) % BLK
    kj = lax.broadcasted_iota(jnp.int32, (4 * BLK, 2 * BLK), 1)
    band = (kj > qi) & (kj <= qi + BLK)
    return jnp.stack([jnp.where(band & (kj >= BLK), 0.0, NEG_INF), jnp.where(band, 0.0, NEG_INF)]).astype(F32)


def _attn_prep(proj, cos, sin, *, name):
    T = proj.shape[0]
    tm = _tile(T, 4 * BLK)

    def body(q_ref, k_ref, c_ref, s_ref, qs_ref, kr_ref):
        lt32, halves = _lane_masks(BLK)
        for b in range(tm // BLK):
            rows = pl.ds(b * BLK, BLK)
            cc, sc = c_ref[rows, :], s_ref[rows, :]
            qr = [_rope(q_ref[rows, p * 128:(p + 1) * 128].astype(F32), cc, sc, lt32) * QSCALE for p in range(8)]
            for g in range(N_KV):
                qs_ref[g, pl.ds(4 * b * BLK, 4 * BLK), :] = _stack_heads(qr, g, halves).astype(BF)
            kr_ref[rows, :] = jnp.concatenate([_rope(k_ref[rows, r * 128:(r + 1) * 128].astype(F32), cc, sc, lt32)
                                               for r in range(2)], axis=1).astype(BF)

    tab = pl.BlockSpec((tm, 128), _row)
    return pl.pallas_call(
        body, name=name, grid=(T // tm,),
        in_specs=[pl.BlockSpec((tm, D), lambda n: (n, O_Q // D)), pl.BlockSpec((tm, 256), lambda n: (n, O_K // 256)),
                  tab, tab],
        out_specs=[pl.BlockSpec((N_KV, 4 * tm, 128), lambda n: (0, n, 0)), pl.BlockSpec((tm, 256), _row)],
        out_shape=[jax.ShapeDtypeStruct((N_KV, 4 * T, 128), BF), jax.ShapeDtypeStruct((T, 256), BF)],
        compiler_params=_cp(("parallel",)),
    )(proj, proj, cos, sin)


def _attn_specs():
    prev = lambda n: jnp.maximum(n - 1, 0)
    return [pl.BlockSpec((N_KV, 4 * BLK, 128), lambda n: (0, n, 0)),
            pl.BlockSpec((BLK, 256), _row), pl.BlockSpec((BLK, 256), lambda n: (prev(n), 0)),
            pl.BlockSpec((BLK, 256), lambda n: (n, O_V // 256)),
            pl.BlockSpec((BLK, 256), lambda n: (prev(n), O_V // 256)),
            pl.BlockSpec((1, 4 * BLK, 2 * BLK), lambda n: (jnp.minimum(n, 1), 0, 0)),
            pl.BlockSpec(memory_space=pltpu.SMEM)]


def _bands(kc_ref, kp_ref, vc_ref, vp_ref):
    kb, vb = [], []
    for r in range(2):
        cols = slice(r * 128, (r + 1) * 128)
        kb.append(jnp.concatenate([kp_ref[:, cols], kc_ref[:, cols]], axis=0))
        vb.append(jnp.concatenate([vp_ref[:, cols], vc_ref[:, cols]], axis=0))
    return kb, vb


def _sink_col(sink_ref, g):
    return jnp.concatenate([jnp.full((BLK, 1), sink_ref[4 * g + hh], F32) for hh in range(4)], axis=0)


def _unstack_heads(t, g, halves, acc):
    half = g % 2
    for hh in range(4):
        h = 4 * g + hh
        th = jnp.where(halves[half], t[hh * BLK:(hh + 1) * BLK], 0.0)
        if h % 2 != half:
            th = pltpu.roll(th, HEAD_DIM, 1)
        acc[h // 2] = acc[h // 2] + th


def _stack_heads(chunks, g, halves):
    half = g % 2
    parts = []
    for hh in range(4):
        h = 4 * g + hh
        t = chunks[h // 2]
        if h % 2 != half:
            t = pltpu.roll(t, HEAD_DIM, 1)
        parts.append(jnp.where(halves[half], t, 0.0))
    return jnp.concatenate(parts, axis=0)


def _attn_fwd(qs, kr, proj, bias, sinks, *, name):
    T = proj.shape[0]
    nb = T // BLK

    def body(qs_ref, kc_ref, kp_ref, vc_ref, vp_ref, bias_ref, sink_ref, o_ref, lse_ref):
        _, h128 = _lane_masks(BLK)
        _, h256 = _lane_masks(2 * BLK)
        _, h512 = _lane_masks(4 * BLK)
        kb, vb = _bands(kc_ref, kp_ref, vc_ref, vp_ref)
        outs = [jnp.zeros((BLK, 128), F32) for _ in range(8)]
        groups = range(N_KV)
        bias = bias_ref[0]
        sink = [_sink_col(sink_ref, g) for g in groups]
        s = [lax.dot_general(qs_ref[g], kb[g // 2], NT, preferred_element_type=F32) + bias for g in groups]
        m = [jnp.maximum(jnp.max(s[g], axis=-1, keepdims=True), sink[g]) for g in groups]
        p = [jnp.exp(s[g] - m[g]).astype(BF) for g in groups]
        vg = [jnp.where(h256[g % 2], vb[g // 2].astype(F32), 1.0).astype(BF) for g in groups]
        o = [jnp.dot(p[g], vg[g], preferred_element_type=F32) for g in groups]
        denom = [jnp.where(h512[g % 2], pltpu.roll(o[g], HEAD_DIM, 1), o[g]) + jnp.exp(sink[g] - m[g]) for g in groups]
        for g in groups:
            lse_ref[g] = m[g] + jnp.log(denom[g])
            _unstack_heads(o[g] * (1.0 / denom[g]), g, h128, outs)
        o_ref[...] = jnp.concatenate(outs, axis=1).astype(BF)

    return pl.pallas_call(
        body, name=name, grid=(nb,),
        in_specs=_attn_specs(),
        out_specs=[pl.BlockSpec((BLK, D), _row), pl.BlockSpec((N_KV, 4 * BLK, 128), lambda n: (0, n, 0))],
        out_shape=[jax.ShapeDtypeStruct((T, D), BF), jax.ShapeDtypeStruct((N_KV, 4 * T, 128), F32)],
        compiler_params=_cp(("parallel",)),
    )(qs, kr, kr, proj, proj, bias, sinks)


def _attn_bwd(qs, kr, proj, bias, sinks, lse, o, do, cos, sin, dproj, *, name):
    T = proj.shape[0]
    nb = T // BLK

    def body(qs_ref, kc_ref, kp_ref, vc_ref, vp_ref, bias_ref, sink_ref, lse_ref, o_ref, do_ref,
             cc_ref, sc_ref, cp_ref, sp_ref, dproj_ref, dq_ref, dkc_ref, dkp_ref, dvc_ref, dvp_ref, dsink_ref):
        @pl.when(pl.program_id(0) == 0)
        def _():
            dsink_ref[...] = jnp.zeros_like(dsink_ref)
        lt32, h128 = _lane_masks(BLK)
        kb, vb = _bands(kc_ref, kp_ref, vc_ref, vp_ref)
        oc = [o_ref[:, p * 128:(p + 1) * 128].astype(F32) for p in range(8)]
        doc = [do_ref[:, p * 128:(p + 1) * 128].astype(F32) for p in range(8)]
        dqs = [jnp.zeros((BLK, 128), F32) for _ in range(8)]
        lane1 = lax.broadcasted_iota(jnp.int32, (1, 128), 1)
        dsink = jnp.zeros((1, 128), F32)
        groups = range(N_KV)
        bias = bias_ref[0]
        q = [qs_ref[g] for g in groups]
        lse_g = [lse_ref[g] for g in groups]
        s = [lax.dot_general(q[g], kb[g // 2], NT, preferred_element_type=F32) + bias for g in groups]
        dos = [_stack_heads(doc, g, h128) for g in groups]
        dosb = [t.astype(BF) for t in dos]
        dp = [lax.dot_general(dosb[g], vb[g // 2], NT, preferred_element_type=F32) for g in groups]
        delta = [jnp.sum(dos[g] * _stack_heads(oc, g, h128), axis=-1, keepdims=True) for g in groups]
        p = [jnp.exp(s[g] - jnp.concatenate([lse_g[g], lse_g[g]], axis=1)) for g in groups]
        ds = [(p[g] * (dp[g] - delta[g])).astype(BF) for g in groups]
        pb = [t.astype(BF) for t in p]
        dvg = [lax.dot_general(pb[g], dosb[g], TN, preferred_element_type=F32) for g in groups]
        dkg = [lax.dot_general(ds[g], q[g], TN, preferred_element_type=F32) for g in groups]
        dqg = [jnp.dot(ds[g], kb[g // 2], preferred_element_type=F32) * QSCALE for g in groups]
        dvr = [dvg[0] + dvg[1], dvg[2] + dvg[3]]
        dkr = [dkg[0] + dkg[1], dkg[2] + dkg[3]]
        for g in groups:
            _unstack_heads(dqg[g], g, h128, dqs)
            dsk = -jnp.exp(_sink_col(sink_ref, g) - lse_g[g][:, 0:1]) * delta[g]
            for hh in range(4):
                val = jnp.sum(dsk[hh * BLK:(hh + 1) * BLK], axis=0, keepdims=True)
                dsink = dsink + jnp.where(lane1 == 4 * g + hh, val, 0.0)
        cc, sc, cp, sp = cc_ref[...], sc_ref[...], cp_ref[...], sp_ref[...]
        dsink_ref[...] += dsink
        dq_ref[...] = jnp.concatenate([_rope(t, cc, sc, lt32, inverse=True) for t in dqs], axis=1).astype(BF)
        dkp_ref[...] = jnp.concatenate([_rope(t[:BLK], cp, sp, lt32, inverse=True) for t in dkr], axis=1)
        dkc_ref[...] = jnp.concatenate([_rope(t[BLK:], cc, sc, lt32, inverse=True) for t in dkr], axis=1)
        dvp_ref[...] = jnp.concatenate([t[:BLK] for t in dvr], axis=1)
        dvc_ref[...] = jnp.concatenate([t[BLK:] for t in dvr], axis=1)

    kv = pl.BlockSpec((BLK, 256), _row)
    tc = pl.BlockSpec((BLK, 128), _row)
    tp = pl.BlockSpec((BLK, 128), lambda n: (jnp.maximum(n - 1, 0), 0))
    return pl.pallas_call(
        body, name=name, grid=(nb,),
        in_specs=_attn_specs() + [pl.BlockSpec((N_KV, 4 * BLK, 128), lambda n: (0, n, 0)),
                                  pl.BlockSpec((BLK, D), _row), pl.BlockSpec((BLK, D), _row), tc, tc, tp, tp,
                                  pl.BlockSpec(memory_space=pl.ANY)],
        out_specs=[pl.BlockSpec((BLK, D), lambda n: (n, O_Q // D)), kv, kv, kv, kv, pl.BlockSpec((1, 128), _const2)],
        out_shape=[jax.ShapeDtypeStruct(dproj.shape, BF)] + [jax.ShapeDtypeStruct((T, 256), F32)] * 4
        + [jax.ShapeDtypeStruct((1, 128), F32)],
        input_output_aliases={14: 0},
        compiler_params=_cp(("arbitrary",)),
    )(qs, kr, kr, proj, proj, bias, sinks, lse, o, do, cos, sin, cos, sin, dproj)


def _dkv_combine(dkc, dkp, dvc, dvp, dproj, *, name):
    T = dkc.shape[0]
    nb = T // BLK
    tm = _tile(T, 4 * BLK)
    bpt = tm // BLK
    nt = T // tm

    def body(dkc_ref, dkp_ref, dkn_ref, dvc_ref, dvp_ref, dvn_ref, dproj_ref, o_ref):
        keep = jnp.where(pl.program_id(0) == nt - 1, 0.0, 1.0)

        def shifted(prev_ref, next_ref):
            nxt = keep * next_ref[...]
            return nxt if bpt == 1 else jnp.concatenate([prev_ref[BLK:, :], nxt], axis=0)

        o_ref[:, 0:256] = (dkc_ref[...] + shifted(dkp_ref, dkn_ref)).astype(BF)
        o_ref[:, 256:512] = (dvc_ref[...] + shifted(dvp_ref, dvn_ref)).astype(BF)

    cur = pl.BlockSpec((tm, 256), _row)
    nxt = pl.BlockSpec((BLK, 256), lambda i: (jnp.minimum((i + 1) * bpt, nb - 1), 0))
    return pl.pallas_call(
        body, name=name, grid=(nt,),
        in_specs=[cur, cur, nxt, cur, cur, nxt, pl.BlockSpec(memory_space=pl.ANY)],
        out_specs=pl.BlockSpec((tm, 512), lambda i: (i, O_K // 512)),
        out_shape=jax.ShapeDtypeStruct(dproj.shape, BF),
        input_output_aliases={6: 0},
        compiler_params=_cp(("parallel",)),
    )(dkc, dkp, dkp, dvc, dvp, dvp, dproj)


HALO = 16


def _conv_shifts(cu, hprev, tm):
    row = lax.broadcasted_iota(jnp.int32, cu.shape, 0)
    h1 = hprev[HALO - 1:HALO, :]
    h2 = hprev[HALO - 2:HALO - 1, :]
    m1 = jnp.where(row == 0, h1, pltpu.roll(cu, 1, 0))
    m2 = jnp.where(row == 0, h2, jnp.where(row == 1, h1, pltpu.roll(cu, 2, 0)))
    return m1, m2


def _mixer_mid_fwd(proj, attn, wcp, wap, wout, convw, x, gt, *, tm, name):
    T = x.shape[0]
    tm = _tile(T, tm)
    hb = tm // HALO

    def body(bg_ref, cg_ref, u_ref, hcg_ref, hu_ref, zc0_ref, zc1_ref, za0_ref, za1_ref, at_ref,
             wcp_ref, wap_ref, wout_ref, cw_ref, x_ref, gt_ref,
             x2_ref, gc_ref, yc_ref, ya_ref, mg_ref, o_ref):
        first = jnp.where(pl.program_id(0) == 0, 0.0, 1.0)
        cu = cg_ref[...].astype(F32) * u_ref[...].astype(F32)
        hprev = first * (hcg_ref[...].astype(F32) * hu_ref[...].astype(F32))
        m1, m2 = _conv_shifts(cu, hprev, tm)
        cv = cw_ref[0:1, :] * m2 + cw_ref[1:2, :] * m1 + cw_ref[2:3, :] * cu
        gc = (bg_ref[...].astype(F32) * cv).astype(BF)
        gc_ref[...] = gc
        yc = jnp.dot(gc, wcp_ref[...], preferred_element_type=F32)
        ya = jnp.dot(at_ref[...], wap_ref[...], preferred_element_type=F32)
        yc_ref[...] = yc.astype(BF)
        ya_ref[...] = ya.astype(BF)
        zc = jnp.concatenate([zc0_ref[...], zc1_ref[...]], axis=1).astype(F32)
        za = jnp.concatenate([za0_ref[...], za1_ref[...]], axis=1).astype(F32)
        mg = (_sigmoid(zc) * yc + _sigmoid(za) * ya).astype(BF)
        mg_ref[...] = mg
        o = jnp.dot(mg, wout_ref[...], preferred_element_type=F32)
        o_ref[...] = o.astype(BF)
        x2_ref[...] = x_ref[...] + gt_ref[...] * o

    wspec = pl.BlockSpec((D, D), _const2)
    rowspec = pl.BlockSpec((tm, D), _row)
    return pl.pallas_call(
        body, name=name, grid=(T // tm,),
        in_specs=[_col(tm, O_BG), _col(tm, O_CG), _col(tm, O_U), _halo_prev(hb, O_CG), _halo_prev(hb, O_U),
                  _col(tm, O_ZC, 512), _col(tm, O_ZC + 512, 512), _col(tm, O_ZA, 512), _col(tm, O_ZA + 512, 512),
                  rowspec, wspec, wspec, wspec, pl.BlockSpec((8, D), _const2), rowspec, pl.BlockSpec((1, D), _const2)],
        out_specs=[rowspec] * 6,
        out_shape=[jax.ShapeDtypeStruct((T, D), F32)] + [jax.ShapeDtypeStruct((T, D), BF)] * 5,
        compiler_params=_cp(("parallel",)),
    )(proj, proj, proj, proj, proj, proj, proj, proj, proj, attn, wcp, wap, wout, convw, x, gt)


def _col(tm, c, w=D):
    assert c % w == 0
    return pl.BlockSpec((tm, w), lambda i: (i, c // w))


def _halo_prev(hb, c):
    return pl.BlockSpec((HALO, D), lambda i: (jnp.maximum(i * hb - 1, 0), c // D))


def _halo_next(hb, nblk, c=0):
    return pl.BlockSpec((HALO, D), lambda i: (jnp.minimum((i + 1) * hb, nblk - 1), c // D))


def _mixer_mid_bwd(dx2, gt, o, proj, yc, ya, wout, wcp, wap, *, tm, name):
    T = dx2.shape[0]
    tm = _tile(T, tm)
    nt = T // tm

    def body(dx_ref, gt_ref, o_ref, zc0_ref, zc1_ref, za0_ref, za1_ref, yc_ref, ya_ref, wout_ref, wcp_ref, wap_ref,
             dout_ref, dyc_ref, dya_ref, dgc_ref, dat_ref, dproj_ref, dgt_ref, dzs, sems):
        i = pl.program_id(0)
        slot = lax.rem(i, 2)

        def slab_copy(step, s):
            return pltpu.make_async_copy(
                dzs.at[s], dproj_ref.at[pl.ds(pl.multiple_of(step * tm, tm), tm), pl.ds(O_ZC, 2 * D)], sems.at[s])

        @pl.when(i == 0)
        def _():
            dgt_ref[...] = jnp.zeros_like(dgt_ref)

        dxv = dx_ref[...]
        dgt_ref[...] += jnp.sum(dxv * o_ref[...].astype(F32), axis=0, keepdims=True)
        dout = (gt_ref[...] * dxv).astype(BF)
        dout_ref[...] = dout
        dmg = lax.dot_general(dout, wout_ref[...], NT, preferred_element_type=F32)
        sc = _sigmoid(jnp.concatenate([zc0_ref[...], zc1_ref[...]], axis=1).astype(F32))
        sa = _sigmoid(jnp.concatenate([za0_ref[...], za1_ref[...]], axis=1).astype(F32))
        dyc = (dmg * sc).astype(BF)
        dya = (dmg * sa).astype(BF)
        dyc_ref[...] = dyc
        dya_ref[...] = dya
        dzs[slot, :, 0:D] = (dmg * yc_ref[...].astype(F32) * (sc * (1.0 - sc))).astype(BF)
        dzs[slot, :, D:2 * D] = (dmg * ya_ref[...].astype(F32) * (sa * (1.0 - sa))).astype(BF)
        slab_copy(i, slot).start()
        dgc_ref[...] = lax.dot_general(dyc, wcp_ref[...], NT, preferred_element_type=F32).astype(BF)
        dat_ref[...] = lax.dot_general(dya, wap_ref[...], NT, preferred_element_type=F32).astype(BF)

        @pl.when(i > 0)
        def _():
            slab_copy(i - 1, 1 - slot).wait()

        @pl.when(i == nt - 1)
        def _():
            slab_copy(i, slot).wait()

    def zcol(c):
        return pl.BlockSpec((tm, 512), lambda i: (i, c // 512))

    wspec = pl.BlockSpec((D, D), _const2)
    rowspec = pl.BlockSpec((tm, D), _row)
    vec = pl.BlockSpec((1, D), _const2)
    return pl.pallas_call(
        body, name=name, grid=(nt,),
        in_specs=[rowspec, vec, rowspec, zcol(O_ZC), zcol(O_ZC + 512), zcol(O_ZA), zcol(O_ZA + 512),
                  rowspec, rowspec, wspec, wspec, wspec],
        out_specs=[rowspec] * 5 + [pl.BlockSpec(memory_space=pl.ANY), vec],
        out_shape=[jax.ShapeDtypeStruct((T, D), BF)] * 5 + [jax.ShapeDtypeStruct((T, NIN), BF),
                                                            jax.ShapeDtypeStruct((1, D), F32)],
        scratch_shapes=[pltpu.VMEM((2, tm, 2 * D), BF), pltpu.SemaphoreType.DMA((2,))],
        compiler_params=_cp(("arbitrary",)),
    )(dx2, gt, o, proj, proj, proj, proj, yc, ya, wout, wcp, wap)


def _conv_bwd(dgc, proj, convw, dproj, *, tm, name):
    T = dgc.shape[0]
    tm = _tile(T, tm)
    hb = tm // HALO
    nblk = T // HALO
    nt = T // tm

    def body(dgc_ref, ndgc_ref, bg_ref, nbg_ref, cg_ref, u_ref, hcg_ref, hu_ref, cw_ref, dproj_ref, dp_ref, dcw_ref):
        i = pl.program_id(0)

        @pl.when(i == 0)
        def _():
            dcw_ref[...] = jnp.zeros_like(dcw_ref)
        first = jnp.where(i == 0, 0.0, 1.0)
        last = jnp.where(i == nt - 1, 0.0, 1.0)
        cg = cg_ref[...].astype(F32)
        u = u_ref[...].astype(F32)
        bg = bg_ref[...].astype(F32)
        dg = dgc_ref[...].astype(F32)
        cu = cg * u
        hprev = first * (hcg_ref[...].astype(F32) * hu_ref[...].astype(F32))
        m1, m2 = _conv_shifts(cu, hprev, tm)
        w0, w1, w2 = cw_ref[0:1, :], cw_ref[1:2, :], cw_ref[2:3, :]
        cv = w0 * m2 + w1 * m1 + w2 * cu
        dcv = dg * bg
        nxt = last * (ndgc_ref[...].astype(F32) * nbg_ref[...].astype(F32))
        n0, n1 = nxt[0:1, :], nxt[1:2, :]
        row = lax.broadcasted_iota(jnp.int32, dcv.shape, 0)
        p1 = jnp.where(row == tm - 1, n0, pltpu.roll(dcv, tm - 1, 0))
        p2 = jnp.where(row == tm - 1, n1, jnp.where(row == tm - 2, n0, pltpu.roll(dcv, tm - 2, 0)))
        dcu = w2 * dcv + w1 * p1 + w0 * p2
        dp_ref[:, 0:D] = (dg * cv).astype(BF)
        dp_ref[:, D:2 * D] = (dcu * u).astype(BF)
        dp_ref[:, 2 * D:3 * D] = (dcu * cg).astype(BF)
        dcw_ref[0:1, :] += jnp.sum(dcv * m2, axis=0, keepdims=True)
        dcw_ref[1:2, :] += jnp.sum(dcv * m1, axis=0, keepdims=True)
        dcw_ref[2:3, :] += jnp.sum(dcv * cu, axis=0, keepdims=True)

    rowspec = pl.BlockSpec((tm, D), _row)
    cw = pl.BlockSpec((8, D), _const2)
    return pl.pallas_call(
        body, name=name, grid=(nt,),
        in_specs=[rowspec, _halo_next(hb, nblk), _col(tm, O_BG), _halo_next(hb, nblk, O_BG),
                  _col(tm, O_CG), _col(tm, O_U), _halo_prev(hb, O_CG), _halo_prev(hb, O_U), cw,
                  pl.BlockSpec(memory_space=pl.ANY)],
        out_specs=[pl.BlockSpec((tm, 3 * D), _row), cw],
        out_shape=[jax.ShapeDtypeStruct(dproj.shape, BF), jax.ShapeDtypeStruct((8, D), F32)],
        input_output_aliases={9: 0},
        compiler_params=_cp(("arbitrary",)),
    )(dgc, dgc, proj, proj, proj, proj, proj, proj, convw, dproj)


def _adam(w, g, m, v, *, tm, name):
    _, R, C = w.shape
    tm = _tile(R, tm)
    parts = g.ndim == 3
    c1 = 1.0 - ADAM_B1
    c2 = 1.0 - ADAM_B2
    bc1 = 1.0 - ADAM_B1 ** ADAM_STEP
    bc2 = 1.0 - ADAM_B2 ** ADAM_STEP

    def body(w_ref, g_ref, m_ref, v_ref, go_ref, d_ref, nm_ref, nv_ref):
        if parts:
            gv = g_ref[0].astype(F32)
            for s in range(1, N_DEV):
                gv = gv + g_ref[s].astype(F32)
        else:
            gv = g_ref[...]
        go_ref[0] = gv
        nm = ADAM_B1 * m_ref[0] + c1 * gv
        nv = ADAM_B2 * v_ref[0] + c2 * (gv * gv)
        nm_ref[0] = nm
        nv_ref[0] = nv
        d_ref[0] = -ADAM_LR * ((nm / bc1) / (jnp.sqrt(nv / bc2) + ADAM_EPS) + ADAM_WD * w_ref[0])

    spec = pl.BlockSpec((1, tm, C), lambda i: (0, i, 0))
    gspec = pl.BlockSpec((N_DEV, tm, C), lambda i: (0, i, 0)) if parts else pl.BlockSpec((tm, C), _row)
    return pl.pallas_call(
        body, name=name, grid=(R // tm,),
        in_specs=[spec, gspec, spec, spec], out_specs=[spec] * 4,
        out_shape=[jax.ShapeDtypeStruct((1, R, C), F32)] * 4,
        compiler_params=_cp(("parallel",)),
    )(w, g, m, v)


def _mods_part(c_all, w_ada, b_ada, *, name):
    C = w_ada.shape[1]

    def body(c_ref, w_ref, b_ref, o_ref):
        cv = c_ref[...]
        ca = cv * jax.nn.sigmoid(cv)
        o_ref[...] = jnp.dot(ca, w_ref[...], preferred_element_type=F32,
                             precision=lax.Precision.HIGHEST) + b_ref[...]

    return pl.pallas_call(
        body, name=name,
        out_shape=jax.ShapeDtypeStruct((N_DEV, C), F32),
        compiler_params=_cp(),
    )(c_all, w_ada, b_ada)


def _wada_grad(c_all_t, gm, *, name):
    C = gm.shape[1]

    def body(c_ref, g_ref, o_ref):
        cv = c_ref[...]
        ca = cv * jax.nn.sigmoid(cv)
        acc = ca[:, 0:1] * g_ref[0:1, :]
        for b in range(1, N_DEV):
            acc = acc + ca[:, b:b + 1] * g_ref[b:b + 1, :]
        o_ref[...] = acc

    return pl.pallas_call(
        body, name=name,
        out_shape=jax.ShapeDtypeStruct((D, C), F32),
        compiler_params=_cp(),
    )(c_all_t, gm)


def _peer(x, y, c, d):
    px = lax.rem(x + ((d >> 2) & 1), 2)
    py = lax.rem(y + ((d >> 1) & 1), 2)
    pc = lax.rem(c + (d & 1), 2)
    return (px, py, pc), 4 * px + 2 * py + pc


def _exchange(xs, *, scatter, name):
    n = len(xs)
    nsem = n * (N_DEV - 1)

    def body(*refs):
        ins, outs = refs[:n], refs[n:2 * n]
        token, send_sems, recv_sems, local_sems = refs[2 * n:]
        x, y, c = lax.axis_index("x"), lax.axis_index("y"), lax.axis_index("c")
        me = 4 * x + 2 * y + c
        token[...] = jnp.zeros_like(token)

        def src(t, idx):
            return ins[t].at[idx] if scatter else ins[t]

        local = [pltpu.make_async_copy(src(t, me), outs[t].at[me], local_sems.at[t]) for t in range(n)]
        for cp in local:
            cp.start()
        remote = []
        for t in range(n):
            for d in range(1, N_DEV):
                peer, pidx = _peer(x, y, c, d)
                k = t * (N_DEV - 1) + d - 1
                send = pltpu.make_async_remote_copy(src_ref=src(t, pidx), dst_ref=outs[t].at[me],
                                                    send_sem=send_sems.at[k], recv_sem=recv_sems.at[k],
                                                    device_id=peer, device_id_type=MESH)
                recv = pltpu.make_async_remote_copy(src_ref=src(t, pidx), dst_ref=outs[t].at[pidx],
                                                    send_sem=send_sems.at[k], recv_sem=recv_sems.at[k],
                                                    device_id=peer, device_id_type=MESH)
                send.start()
                remote.append((send, recv))
        for cp in local:
            cp.wait()
        for send, recv in remote:
            send.wait_send()
            recv.wait_recv()

    anyspec = pl.BlockSpec(memory_space=pl.ANY)
    out_shape = [jax.ShapeDtypeStruct(a.shape if scatter else (N_DEV,) + a.shape, a.dtype) for a in xs]
    out_shape.append(jax.ShapeDtypeStruct((8, 128), F32))
    return pl.pallas_call(
        body, name=name,
        in_specs=[anyspec] * n, out_specs=[anyspec] * n + [pl.BlockSpec(memory_space=pltpu.VMEM)],
        out_shape=out_shape,
        scratch_shapes=[pltpu.SemaphoreType.DMA((nsem,)), pltpu.SemaphoreType.DMA((nsem,)),
                        pltpu.SemaphoreType.DMA((n,))],
    )(*xs)


def _sum8(parts, *, name):
    _, R, C = parts.shape

    def body(p_ref, o_ref):
        acc = p_ref[0]
        for s in range(1, N_DEV):
            acc = acc + p_ref[s]
        o_ref[...] = acc

    return pl.pallas_call(body, name=name, out_shape=jax.ShapeDtypeStruct((R, C), F32),
                          compiler_params=_cp())(parts)


HBM_SPEC = pl.BlockSpec(memory_space=pltpu.HBM)
SEM_SPEC = pl.BlockSpec(memory_space=pltpu.SEMAPHORE)
N_PEER = N_DEV - 1


def _split_copies(src_refs, land_refs, send_sems, recv_sems, scatter):
    x, y, c = lax.axis_index("x"), lax.axis_index("y"), lax.axis_index("c")
    me = 4 * x + 2 * y + c
    pairs = []
    for j, (src, land) in enumerate(zip(src_refs, land_refs)):
        for d in range(1, N_DEV):
            peer, pidx = _peer(x, y, c, d)
            k = j * N_PEER + d - 1
            s = src.at[pidx] if scatter else src
            send = pltpu.make_async_remote_copy(src_ref=s, dst_ref=land.at[me], send_sem=send_sems.at[k],
                                                recv_sem=recv_sems.at[k], device_id=peer, device_id_type=MESH)
            recv = pltpu.make_async_remote_copy(src_ref=s, dst_ref=land.at[pidx], send_sem=send_sems.at[k],
                                                recv_sem=recv_sems.at[k], device_id=peer, device_id_type=MESH)
            pairs.append((send, recv))
    return pairs


def _own_slot(block, me):
    land = lax.empty((N_DEV,) + block.shape, block.dtype)
    return lax.dynamic_update_slice(land, block[None], (me, 0, 0))


def _split_start(srcs, lands, groups, *, scatter, name):
    n, ng = len(srcs), len(groups)

    def body(*refs):
        src_refs, land_refs = refs[:n], refs[n:2 * n]
        sems = refs[2 * n:2 * n + 2 * ng]
        token = refs[-1]
        for gi, g in enumerate(groups):
            pairs = _split_copies([src_refs[t] for t in g], [land_refs[t] for t in g], sems[2 * gi],
                                  sems[2 * gi + 1], scatter)
            for send, _ in pairs:
                send.start()
        token[...] = jnp.zeros_like(token)

    sem_shapes = []
    for g in groups:
        sem_shapes += [pltpu.SemaphoreType.DMA((len(g) * N_PEER,))] * 2
    thru = [pltpu.HBM(a.shape, a.dtype) for a in list(srcs) + list(lands)]
    outs = pl.pallas_call(
        body, name=name,
        out_shape=tuple(sem_shapes + thru + [jax.ShapeDtypeStruct((8, 128), F32)]),
        in_specs=[HBM_SPEC] * (2 * n),
        out_specs=tuple([SEM_SPEC] * (2 * ng) + [HBM_SPEC] * (2 * n) + [pl.BlockSpec(memory_space=pltpu.VMEM)]),
        input_output_aliases={i: 2 * ng + i for i in range(2 * n)},
        compiler_params=pltpu.CompilerParams(has_side_effects=pltpu.SideEffectType.DATAFLOW_SIDE_EFFECTING),
    )(*[pltpu.with_memory_space_constraint(a, pltpu.HBM) for a in list(srcs) + list(lands)])
    sems = [(outs[2 * gi], outs[2 * gi + 1]) for gi in range(ng)]
    return sems, outs[2 * ng:2 * ng + n], outs[2 * ng + n:2 * ng + 2 * n], outs[-1]


def _behind(v, token):
    if token is None:
        return v
    return v + token[0, 0].astype(v.dtype)


def _split_wait(srcs, lands, sems, after, *, scatter, name):
    m = len(srcs)

    def body(*refs):
        src_refs, land_refs = refs[:m], refs[m:2 * m]
        send_sems, recv_sems = refs[2 * m], refs[2 * m + 1]
        for send, recv in _split_copies(src_refs, land_refs, send_sems, recv_sems, scatter):
            send.wait_send()
            recv.wait_recv()

    outs = pl.pallas_call(
        body, name=name,
        out_shape=tuple(pltpu.HBM(a.shape, a.dtype) for a in list(srcs) + list(lands)),
        in_specs=[HBM_SPEC] * (2 * m) + [SEM_SPEC, SEM_SPEC, pl.BlockSpec(memory_space=pl.ANY)],
        out_specs=tuple([HBM_SPEC] * (2 * m)),
        input_output_aliases={i: i for i in range(2 * m)},
        compiler_params=pltpu.CompilerParams(has_side_effects=pltpu.SideEffectType.DATAFLOW_SIDE_EFFECTING),
    )(*srcs, *lands, sems[0], sems[1], after)
    return outs[m:]


TL_FIRST = (1, 2, 4, 6)
TL_ICI = (2, 4, 6)
EFFECT = pltpu.SideEffectType.DATAFLOW_SIDE_EFFECTING


def _tl_first(src_refs, land_refs, send_sems, recv_sems):
    x, y, c = lax.axis_index("x"), lax.axis_index("y"), lax.axis_index("c")
    me = 4 * x + 2 * y + c
    out = []
    for j, (src, land) in enumerate(zip(src_refs, land_refs)):
        for i, d in enumerate(TL_FIRST):
            peer, pidx = _peer(x, y, c, d)
            k = len(TL_FIRST) * j + i
            send = pltpu.make_async_remote_copy(src_ref=src, dst_ref=land.at[me], send_sem=send_sems.at[k],
                                                recv_sem=recv_sems.at[k], device_id=peer, device_id_type=MESH)
            recv = pltpu.make_async_remote_copy(src_ref=src, dst_ref=land.at[pidx], send_sem=send_sems.at[k],
                                                recv_sem=recv_sems.at[k], device_id=peer, device_id_type=MESH)
            out.append((d, send, recv))
    return out


def _tl_second(land_refs, send_sems, recv_sems):
    x, y, c = lax.axis_index("x"), lax.axis_index("y"), lax.axis_index("c")
    sibling, _ = _peer(x, y, c, 1)
    out = []
    for j, land in enumerate(land_refs):
        for i, d in enumerate(TL_ICI):
            _, mine = _peer(x, y, c, d)
            _, theirs = _peer(x, y, c, d + 1)
            k = len(TL_ICI) * j + i
            send = pltpu.make_async_remote_copy(src_ref=land.at[mine], dst_ref=land.at[mine], send_sem=send_sems.at[k],
                                                recv_sem=recv_sems.at[k], device_id=sibling, device_id_type=MESH)
            recv = pltpu.make_async_remote_copy(src_ref=land.at[mine], dst_ref=land.at[theirs],
                                                send_sem=send_sems.at[k], recv_sem=recv_sems.at[k],
                                                device_id=sibling, device_id_type=MESH)
            out.append((send, recv))
    return out


def _tl_start(srcs, lands, groups, *, name):
    n, ng = len(srcs), len(groups)

    def body(*refs):
        src_refs, land_refs = refs[:n], refs[n:2 * n]
        sems = refs[2 * n:2 * n + 2 * ng]
        for gi, g in enumerate(groups):
            for _, send, _ in _tl_first([src_refs[t] for t in g], [land_refs[t] for t in g], sems[2 * gi],
                                        sems[2 * gi + 1]):
                send.start()
        refs[-1][...] = jnp.zeros_like(refs[-1])

    sem_shapes = []
    for g in groups:
        sem_shapes += [pltpu.SemaphoreType.DMA((len(g) * len(TL_FIRST),))] * 2
    thru = [pltpu.HBM(a.shape, a.dtype) for a in list(srcs) + list(lands)]
    outs = pl.pallas_call(
        body, name=name,
        out_shape=tuple(sem_shapes + thru + [jax.ShapeDtypeStruct((8, 128), F32)]),
        in_specs=[HBM_SPEC] * (2 * n),
        out_specs=tuple([SEM_SPEC] * (2 * ng) + [HBM_SPEC] * (2 * n) + [pl.BlockSpec(memory_space=pltpu.VMEM)]),
        input_output_aliases={i: 2 * ng + i for i in range(2 * n)},
        compiler_params=pltpu.CompilerParams(has_side_effects=EFFECT),
    )(*[pltpu.with_memory_space_constraint(a, pltpu.HBM) for a in list(srcs) + list(lands)])
    sems = [(outs[2 * gi], outs[2 * gi + 1]) for gi in range(ng)]
    return sems, outs[2 * ng:2 * ng + n], outs[2 * ng + n:2 * ng + 2 * n], outs[-1]


def _tl_forward(srcs, lands, sems1, after, *, name):
    m = len(srcs)

    def body(*refs):
        src_refs, land_refs = refs[:m], refs[m:2 * m]
        send1, recv1 = refs[2 * m], refs[2 * m + 1]
        send2, recv2 = refs[2 * m + 3], refs[2 * m + 4]
        for d, _, recv in _tl_first(src_refs, land_refs, send1, recv1):
            if d in TL_ICI:
                recv.wait_recv()
        for send, _ in _tl_second(land_refs, send2, recv2):
            send.start()

    sem = pltpu.SemaphoreType.DMA((m * len(TL_ICI),))
    outs = pl.pallas_call(
        body, name=name,
        out_shape=tuple([sem, sem] + [pltpu.HBM(a.shape, a.dtype) for a in list(srcs) + list(lands)]),
        in_specs=[HBM_SPEC] * (2 * m) + [SEM_SPEC, SEM_SPEC, pl.BlockSpec(memory_space=pl.ANY)],
        out_specs=tuple([SEM_SPEC, SEM_SPEC] + [HBM_SPEC] * (2 * m)),
        input_output_aliases={i: 2 + i for i in range(2 * m)},
        compiler_params=pltpu.CompilerParams(has_side_effects=EFFECT),
    )(*srcs, *lands, sems1[0], sems1[1], after)
    return (outs[0], outs[1]), outs[2:2 + m], outs[2 + m:2 + 2 * m]


def _tl_wait(srcs, lands, sems1, sems2, after, *, name):
    m = len(srcs)

    def body(*refs):
        src_refs, land_refs = refs[:m], refs[m:2 * m]
        send1, recv1, send2, recv2 = refs[2 * m:2 * m + 4]
        for d, send, recv in _tl_first(src_refs, land_refs, send1, recv1):
            send.wait_send()
            if d not in TL_ICI:
                recv.wait_recv()
        for send, recv in _tl_second(land_refs, send2, recv2):
            send.wait_send()
            recv.wait_recv()

    outs = pl.pallas_call(
        body, name=name,
        out_shape=tuple(pltpu.HBM(a.shape, a.dtype) for a in list(srcs) + list(lands)),
        in_specs=[HBM_SPEC] * (2 * m) + [SEM_SPEC] * 4 + [pl.BlockSpec(memory_space=pl.ANY)],
        out_specs=tuple([HBM_SPEC] * (2 * m)),
        input_output_aliases={i: i for i in range(2 * m)},
        compiler_params=pltpu.CompilerParams(has_side_effects=EFFECT),
    )(*srcs, *lands, sems1[0], sems1[1], sems2[0], sems2[1], after)
    return outs[m:]


TM_PROJ = 512
TN_PROJ = 512
TM_ROW = 512
TM_NN = 512
TK_TN = 2048
TM_ADAM = 416
TN_FFN = F // 2
TN_IN = NIN // 4


def _tn(a, b, name, tn):
    if a.ndim == 2:
        a = a[None]
    return _tn_matmul(a, b, tn=tn, tk=TK_TN, name=name)


def _local_step(x, tgt, mods, g1, gm, g2, gf, convw8, sinks, w_get, g_put):
    T = x.shape[0]
    sh1, sc1, gt1, sh2, sc2, gt2, sh3, sc3, gt3 = [mods[i:i + 1] for i in range(N_MOD)]
    cos, sin = _rope_tables(T)
    behind = _behind

    w = dict(w_get("gu1", mods))
    h1, ab1 = _norm_proj(x, g1, sc1, sh1, w["gu1"], tm=TM_PROJ, tn=TN_PROJ, name="ffn1_up")
    w.update(w_get("d1", ab1))
    x1, y1 = _ffn_down_fwd(ab1, w["d1"], x, gt1, tm=TM_ROW, name="ffn1_down")
    w.update(w_get("mix", x1))
    h2, proj = _norm_proj(x1, gm, sc2, sh2, w["win"], tm=TM_PROJ, tn=TN_PROJ, name="mix_in")
    qs, kr = _attn_prep(proj, cos, sin, name="attn_prep")
    bias = _attn_bias()
    attn, lse = _attn_fwd(qs, kr, proj, bias, sinks, name="attn_fwd")
    x2, gc, yc, ya, mg, o = _mixer_mid_fwd(proj, attn, w["cp"], w["ap"], w["out"], convw8, x1, gt2,
                                           tm=TM_ROW, name="mix_mid")
    w.update(w_get("ffn2", x2))
    h3, ab2, y2, dx3, lsum, dgf = _ffn_fwd(x2, g2, sc3, sh3, gt3, w["gu2"], w["d2"], (tgt, gf), tm=TM_ROW,
                                           name="ffn2_final")

    dab2, dgt3, g_d2 = _ffn_down_bwd_dw(dx3, y2, gt3, ab2, w["d2"], tm=TM_ROW, name="ffn2_down_bwd")
    dx2, dsh3, dsc3, dg2 = _nn_bwd_norm(dab2, w["gu2"], x2, g2, sc3, dx3, tm=TM_NN, name="ffn2_up_bwd")
    g_gu2 = _tn(dab2, h3, "ffn2_up_dw", TN_FFN)
    tok = g_put(dict(gu2=g_gu2, d2=g_d2))

    dout, dyc, dya, dgc, dat, dproj, dgt2 = _mixer_mid_bwd(dx2, behind(gt2, tok), o, proj, yc, ya, w["out"], w["cp"],
                                                           w["ap"], tm=TM_ROW, name="mix_mid_bwd")
    g_out = _tn(mg, dout, "mix_out_dw", D)
    g_cp = _tn(gc, dyc, "mix_cp_dw", D)
    g_ap = _tn(attn, dya, "mix_ap_dw", D)
    dproj, dkc, dkp, dvc, dvp, dsink = _attn_bwd(qs, kr, proj, bias, sinks, lse, attn, dat, cos, sin, dproj,
                                                 name="attn_bwd")
    dproj = _dkv_combine(dkc, dkp, dvc, dvp, dproj, name="attn_dkv")
    dproj, dcw = _conv_bwd(dgc, proj, convw8, dproj, tm=TM_ROW, name="conv_bwd")
    g_in = _tn(dproj, h2, "mix_in_dw", TN_IN)
    tok = g_put(dict(win=g_in, cp=g_cp, ap=g_ap, out=g_out))
    dx1, dsh2, dsc2, dgm = _nn_bwd_norm(dproj[None], w["win"], x1, gm, behind(sc2, tok), dx2, tm=TM_NN,
                                        name="mix_in_bwd")

    dab1, dgt1, g_d1 = _ffn_down_bwd_dw(dx1, y1, gt1, ab1, w["d1"], tm=TM_ROW, name="ffn1_down_bwd")
    g_put(dict(d1=g_d1))
    g_gu1 = _tn(dab1, h1, "ffn1_up_dw", TN_FFN)
    tok = g_put(dict(gu1=g_gu1))
    dx0, dsh1, dsc1, dg1 = _nn_bwd_norm(dab1, w["gu1"], x, g1, behind(sc1, tok), dx1, tm=TM_NN,
                                        name="ffn1_up_bwd")

    small = dict(mods=jnp.concatenate([dsh1, dsc1, dgt1, dsh2, dsc2, dgt2, dsh3, dsc3, dgt3], axis=0),
                 g1=dg1, gm=dgm, g2=dg2, gf=dgf, convw=dcw[0:3], sinks=dsink[:, 0:N_HEADS])
    return lsum, dx0, small


BIG = ("gu1", "d1", "win", "cp", "ap", "out", "gu2", "d2")
TRANSPOSED = ("gu1", "win", "gu2")
SMALL_ROWS = 24
R_MODS, R_G1, R_GM, R_G2, R_GF, R_CONV, R_SINK = 0, 9, 10, 11, 12, 13, 16


def _pad_to(a, rows, cols):
    return jnp.pad(a, ((0, rows - a.shape[0]), (0, cols - a.shape[1])))


def _pack_small(b_ada, g1, gm, g2, gf, conv, sinks):
    rows = [b_ada.reshape(N_MOD, D), g1.reshape(1, D), gm.reshape(1, D), g2.reshape(1, D), gf.reshape(1, D),
            _pad_to(conv.reshape(3, -1), 3, D), _pad_to(sinks.reshape(1, N_HEADS), 1, D)]
    return _pad_to(jnp.concatenate(rows, axis=0), SMALL_ROWS, D)


def _unpack_small(p, conv_cols):
    return dict(b_ada=p[R_MODS:R_MODS + N_MOD].reshape(1, N_MOD * D), g_ffn1=p[R_G1:R_G1 + 1],
                g_mix=p[R_GM:R_GM + 1], g_ffn2=p[R_G2:R_G2 + 1], g_final=p[R_GF],
                conv_w=p[R_CONV:R_CONV + 3, 0:conv_cols][None], sinks=p[R_SINK:R_SINK + 1, 0:N_HEADS])


def kernel(x, c, w_ada, b_ada, g_ffn1, w1_gu, w1_down, g_mix, w_in, conv_w, w_conv_proj, w_attn_proj, sinks, w_out, g_ffn2, w2_gu, w2_down, g_final, loss_target, m_w_ada, m_b_ada, m_g_ffn1, m_w1_gu, m_w1_down, m_g_mix, m_w_in, m_conv_w, m_w_conv_proj, m_w_attn_proj, m_sinks, m_w_out, m_g_ffn2, m_w2_gu, m_w2_down, m_g_final, v_w_ada, v_b_ada, v_g_ffn1, v_w1_gu, v_w1_down, v_g_mix, v_w_in, v_conv_w, v_w_conv_proj, v_w_attn_proj, v_sinks, v_w_out, v_g_ffn2, v_w2_gu, v_w2_down, v_g_final):
    me = 4 * lax.axis_index("x") + 2 * lax.axis_index("y") + lax.axis_index("c")
    ada_cols = w_ada.shape[2]
    conv_cols = conv_w.shape[2]

    native = dict(gu1=w1_gu[0], d1=w1_down[0], win=w_in[0], cp=w_conv_proj[0], ap=w_attn_proj[0], out=w_out[0],
                  gu2=w2_gu[0], d2=w2_down[0])

    def shard(n, token):
        a = _behind(native[n], token)
        return (a.T if n in TRANSPOSED else a).astype(BF)

    c_all, conv_all, _ = _exchange([c, _pad_to(conv_w[0], 8, conv_cols)], scatter=False, name="gather_cond")
    c_all = c_all.reshape(N_DEV, D)
    conv_full = conv_all[:, 0:3, :].transpose(1, 0, 2).reshape(3, D)

    b_cols = lax.dynamic_slice(b_ada, (0, me * ada_cols), (1, ada_cols))
    mods_cols = _mods_part(c_all, w_ada[0], b_cols, name="ada_mods")
    mods_all, mods_token = _exchange([mods_cols], scatter=False, name="gather_mods")
    mods = lax.dynamic_index_in_dim(mods_all, me, axis=1, keepdims=False).reshape(N_MOD, D)

    groups = dict(gu1=("gu1",), d1=("d1",), mix=("win", "cp", "ap", "out"), ffn2=("gu2", "d2"))
    in_flight = {}
    first = [shard("gu1", mods_token)]
    sems, srcs, lands, token = _tl_start(first, [_own_slot(s, me) for s in first], [[0]],
                                         name="gather_weights_start_gu1")
    in_flight["gu1"] = [sems[0], srcs, lands, None]
    rest = [n for n in BIG if n != "gu1"]
    shards = [shard(n, token) for n in rest]
    rest_groups = [[rest.index(n) for n in names] for g, names in groups.items() if g != "gu1"]
    sems, srcs, lands, rest_token = _tl_start(shards, [_own_slot(s, me) for s in shards], rest_groups,
                                              name="gather_weights_start_rest")
    for (g, names), gsems, idx in zip([kv for kv in groups.items() if kv[0] != "gu1"], sems, rest_groups):
        in_flight[g] = [gsems, [srcs[t] for t in idx], [lands[t] for t in idx], None]

    def forward(group, after):
        sems1, gsrcs, glands, _ = in_flight[group]
        sems2, gsrcs, glands = _tl_forward(gsrcs, glands, sems1, after, name="gather_weights_forward_" + group)
        in_flight[group] = [sems1, gsrcs, glands, sems2]

    forward_early = dict(d1="mix", mix="ffn2")

    def w_get(group, after):
        if group == "gu1":
            after = rest_token
        if in_flight[group][3] is None:
            forward(group, after)
        sems1, gsrcs, glands, sems2 = in_flight[group]
        landed = _tl_wait(gsrcs, glands, sems1, sems2, after, name="gather_weights_wait_" + group)
        if group in forward_early:
            forward(forward_early[group], landed[0])
        return {n: a.reshape(-1, D) for n, a in zip(groups[group], landed)}

    pending = []

    def g_put(gs):
        names = tuple(gs)
        srcs = [gs[n].reshape(N_DEV, -1, D) for n in names]
        lands = [_own_slot(lax.dynamic_index_in_dim(s, me, axis=0, keepdims=False), me) for s in srcs]
        sems, srcs, lands, tok = _split_start(srcs, lands, [list(range(len(names)))], scatter=True,
                                              name="scatter_grads_start_" + names[0])
        pending.append((names, sems[0], srcs, lands))
        return tok

    lsum, grad_x, small = _local_step(x[0], loss_target[0], mods, g_ffn1, g_mix, g_ffn2, g_final[None],
                                      _pad_to(conv_full, 8, D), sinks[0], w_get, g_put)
    loss = lax.psum((0.5 / D) * jnp.sum(lsum), ("x", "y", "c"))

    packed = _pack_small(small["mods"], small["g1"], small["gm"], small["g2"], small["gf"], small["convw"],
                         small["sinks"])
    packed_all, _ = _exchange([packed], scatter=False, name="gather_small")
    gsmall = _sum8(packed_all, name="sum_small")

    w_of = dict(ada=w_ada, gu1=w1_gu, d1=w1_down, win=w_in, cp=w_conv_proj, ap=w_attn_proj, out=w_out, gu2=w2_gu,
                d2=w2_down)
    m_of = dict(ada=m_w_ada, gu1=m_w1_gu, d1=m_w1_down, win=m_w_in, cp=m_w_conv_proj, ap=m_w_attn_proj, out=m_w_out,
                gu2=m_w2_gu, d2=m_w2_down)
    v_of = dict(ada=v_w_ada, gu1=v_w1_gu, d1=v_w1_down, win=v_w_in, cp=v_w_conv_proj, ap=v_w_attn_proj, out=v_w_out,
                gu2=v_w2_gu, d2=v_w2_down)
    upd = {}
    after = gsmall
    for names, sems, srcs, lands in pending:
        parts = _split_wait(srcs, lands, sems, after, scatter=True, name="scatter_grads_wait_" + names[0])
        for n, p in zip(names, parts):
            if n in TRANSPOSED:
                res = _adam(jnp.swapaxes(w_of[n], 1, 2), p, jnp.swapaxes(m_of[n], 1, 2), jnp.swapaxes(v_of[n], 1, 2),
                            tm=TM_ADAM, name="adam_" + n)
                upd[n] = [jnp.swapaxes(t, 1, 2) for t in res]
            else:
                upd[n] = _adam(w_of[n], p, m_of[n], v_of[n], tm=TM_ADAM, name="adam_" + n)
        after = upd[names[-1]][1]

    gm_cols = lax.dynamic_slice(packed_all[:, R_MODS:R_MODS + N_MOD, :].reshape(N_DEV, N_MOD * D),
                                (0, me * ada_cols), (N_DEV, ada_cols))
    upd["ada"] = _adam(w_ada, _wada_grad(c_all.T, gm_cols, name="ada_dw"), m_w_ada, v_w_ada, tm=256, name="adam_ada")
    conv_g = lax.dynamic_slice(gsmall[R_CONV:R_CONV + 3], (0, me * conv_cols), (3, conv_cols))
    gsmall_own = gsmall.at[R_CONV:R_CONV + 3].set(_pad_to(conv_g, 3, D))
    small_upd = _adam(_pack_small(b_ada, g_ffn1, g_mix, g_ffn2, g_final, conv_w, sinks)[None], gsmall_own,
                      _pack_small(m_b_ada, m_g_ffn1, m_g_mix, m_g_ffn2, m_g_final, m_conv_w, m_sinks)[None],
                      _pack_small(v_b_ada, v_g_ffn1, v_g_mix, v_g_ffn2, v_g_final, v_conv_w, v_sinks)[None],
                      tm=SMALL_ROWS, name="adam_small")
    small_out = [_unpack_small(p[0], conv_cols) for p in small_upd]

    big_name = dict(w_ada="ada", w1_gu="gu1", w1_down="d1", w_in="win", w_conv_proj="cp", w_attn_proj="ap",
                    w_out="out", w2_gu="gu2", w2_down="d2")
    order = ("w_ada", "b_ada", "g_ffn1", "w1_gu", "w1_down", "g_mix", "w_in", "conv_w", "w_conv_proj", "w_attn_proj",
             "sinks", "w_out", "g_ffn2", "w2_gu", "w2_down", "g_final")
    outs = [loss, grad_x[None]]
    for kind in range(4):
        for n in order:
            outs.append(upd[big_name[n]][kind] if n in big_name else small_out[kind][n])
    return tuple(outs)
```

```python
import functools

import jax
import jax.numpy as jnp
from jax import lax
from jax.experimental import pallas as pl
from jax.experimental.pallas import tpu as pltpu

D = 1024
F = 2816
NIN = 6656
N_HEADS = 16
N_KV = 4
HEAD_DIM = 64
BLK = 128
N_MOD = 9
N_DEV = 8
EPS = 1e-6
NEG_INF = -1e30
ROPE_THETA = 10000.0
O_BG, O_CG, O_U, O_Q, O_K, O_V, O_ZC, O_ZA = 0, 1024, 2048, 3072, 4096, 4352, 4608, 5632

ADAM_LR = 0.001
ADAM_B1 = 0.9
ADAM_B2 = 0.999
ADAM_EPS = 1e-08
ADAM_WD = 0.01
ADAM_STEP = 10

BF = jnp.bfloat16
F32 = jnp.float32
VMEM_LIMIT = 56 * 1024 * 1024
MXU_N = 256
MESH = pl.DeviceIdType.MESH

NT = (((1,), (1,)), ((), ()))
TN = (((0,), (0,)), ((), ()))


def _cp(sem=None):
    return pltpu.CompilerParams(dimension_semantics=sem, vmem_limit_bytes=VMEM_LIMIT)


def _tile(n, pref):
    if n <= pref:
        return n
    for t in range(pref - pref % 16, 15, -16):
        if n % t == 0:
            return t
    raise ValueError((n, pref))


def _sigmoid(v):
    return 0.5 * jnp.tanh(0.5 * v) + 0.5


def _row(i):
    return (i, 0)


def _const2(*_):
    return (0, 0)


def _resident(shape):
    return pl.BlockSpec(shape, lambda *_: (0,) * len(shape), pipeline_mode=pl.Buffered(1))


def _norm_proj(x, g, sc, sh, wt, *, tm, tn, name):
    T, N = x.shape[0], wt.shape[0]
    tm = _tile(T, tm)

    def body(x_ref, g_ref, sc_ref, sh_ref, w_ref, h_ref, o_ref):
        xv = x_ref[...]
        r = lax.rsqrt(jnp.mean(xv * xv, axis=-1, keepdims=True) + EPS)
        hb = ((xv * r) * g_ref[...] * (1.0 + sc_ref[...]) + sh_ref[...]).astype(BF)
        h_ref[...] = hb
        for c0 in range(0, N, tn):
            cols = pl.ds(c0, tn)
            o_ref[:, cols] = lax.dot_general(hb, w_ref[cols, :], NT, preferred_element_type=F32).astype(BF)

    vec = pl.BlockSpec((1, D), _const2)
    return pl.pallas_call(
        body, name=name, grid=(T // tm,),
        in_specs=[pl.BlockSpec((tm, D), _row), vec, vec, vec, _resident((N, D))],
        out_specs=[pl.BlockSpec((tm, D), _row), pl.BlockSpec((tm, N), _row)],
        out_shape=[jax.ShapeDtypeStruct((T, D), BF), jax.ShapeDtypeStruct((T, N), BF)],
        compiler_params=_cp(("parallel",)),
    )(x, g, sc, sh, wt)


def _ffn_down_fwd(ab, wd, x, gt, *, tm, name):
    T = x.shape[0]
    tm = _tile(T, tm)

    def body(a_ref, b_ref, wd_ref, x_ref, gt_ref, xo_ref, y_ref):
        y = None
        for c0 in range(0, F, MXU_N):
            cols = pl.ds(c0, MXU_N)
            a = a_ref[:, cols].astype(F32)
            act = (a * _sigmoid(a) * b_ref[:, cols].astype(F32)).astype(BF)
            part = jnp.dot(act, wd_ref[cols, :], preferred_element_type=F32)
            y = part if y is None else y + part
        y_ref[...] = y.astype(BF)
        xo_ref[...] = x_ref[...] + (0.5 * gt_ref[...]) * y

    return pl.pallas_call(
        body, name=name, grid=(T // tm,),
        in_specs=[pl.BlockSpec((tm, F), lambda i: (i, 0)), pl.BlockSpec((tm, F), lambda i: (i, 1)),
                  _resident((F, D)), pl.BlockSpec((tm, D), _row), pl.BlockSpec((1, D), _const2)],
        out_specs=[pl.BlockSpec((tm, D), _row), pl.BlockSpec((tm, D), _row)],
        out_shape=[jax.ShapeDtypeStruct((T, D), F32), jax.ShapeDtypeStruct((T, D), BF)],
        compiler_params=_cp(("parallel",)),
    )(ab, ab, wd, x, gt)


def _ffn_fwd(x, g, sc, sh, gt, wgu, wd, final, *, tm, name):
    T = x.shape[0]
    tm = _tile(T, tm)
    last = final is not None

    def body(x_ref, g_ref, sc_ref, sh_ref, gt_ref, wgu_ref, wd_ref, *rest):
        if last:
            t_ref, gf_ref, h_ref, ab_ref, y_ref, dx_ref, ls_ref, dgf_ref = rest
        else:
            h_ref, ab_ref, y_ref, xo_ref = rest
        xv = x_ref[...]
        r = lax.rsqrt(jnp.mean(xv * xv, axis=-1, keepdims=True) + EPS)
        hb = ((xv * r) * g_ref[...] * (1.0 + sc_ref[...]) + sh_ref[...]).astype(BF)
        h_ref[...] = hb
        y = None
        for c0 in range(0, F, MXU_N):
            a = lax.dot_general(hb, wgu_ref[pl.ds(c0, MXU_N), :], NT, preferred_element_type=F32)
            b = lax.dot_general(hb, wgu_ref[pl.ds(F + c0, MXU_N), :], NT, preferred_element_type=F32)
            ab = a.astype(BF)
            bb = b.astype(BF)
            ab_ref[:, pl.ds(c0, MXU_N)] = ab
            ab_ref[:, pl.ds(F + c0, MXU_N)] = bb
            a = ab.astype(F32)
            act = (a * _sigmoid(a) * bb.astype(F32)).astype(BF)
            part = jnp.dot(act, wd_ref[pl.ds(c0, MXU_N), :], preferred_element_type=F32)
            y = part if y is None else y + part
        y_ref[...] = y.astype(BF)
        xo = xv + (0.5 * gt_ref[...]) * y
        if not last:
            xo_ref[...] = xo
            return

        @pl.when(pl.program_id(0) == 0)
        def _():
            ls_ref[...] = jnp.zeros_like(ls_ref)
            dgf_ref[...] = jnp.zeros_like(dgf_ref)
        gv = gf_ref[...]
        r = lax.rsqrt(jnp.mean(xo * xo, axis=-1, keepdims=True) + EPS)
        xh = xo * r
        e = xh * gv - t_ref[...]
        ls_ref[...] += jnp.sum(e * e, axis=0, keepdims=True)
        dy = e * (1.0 / D)
        dgf_ref[...] += jnp.sum(dy * xh, axis=0, keepdims=True)
        dxh = dy * gv
        dx_ref[...] = r * (dxh - xh * jnp.mean(dxh * xh, axis=-1, keepdims=True))

    vec = pl.BlockSpec((1, D), _const2)
    rowspec = pl.BlockSpec((tm, D), _row)
    in_specs = [rowspec, vec, vec, vec, vec, _resident((2 * F, D)), _resident((F, D))]
    out_specs = [rowspec, pl.BlockSpec((tm, 2 * F), _row), rowspec, rowspec]
    out_shape = [jax.ShapeDtypeStruct((T, D), BF), jax.ShapeDtypeStruct((T, 2 * F), BF),
                 jax.ShapeDtypeStruct((T, D), BF), jax.ShapeDtypeStruct((T, D), F32)]
    args = [x, g, sc, sh, gt, wgu, wd]
    if last:
        in_specs += [rowspec, vec]
        out_specs += [vec, vec]
        out_shape += [jax.ShapeDtypeStruct((1, D), F32)] * 2
        args += list(final)
    return pl.pallas_call(
        body, name=name, grid=(T // tm,),
        in_specs=in_specs, out_specs=out_specs, out_shape=out_shape,
        compiler_params=_cp(("arbitrary",) if last else ("parallel",)),
    )(*args)


def _ffn_down_bwd(dxo, y, gt, ab, wd, *, tm, tn, name):
    T = dxo.shape[0]
    tm = _tile(T, tm)

    def body(dxo_ref, y_ref, gt_ref, a_ref, b_ref, wd_ref, dy_ref, dab_ref, dgt_ref):
        @pl.when(pl.program_id(0) == 0)
        def _():
            dgt_ref[...] = jnp.zeros_like(dgt_ref)

        dxv = dxo_ref[...]
        dgt_ref[...] += 0.5 * jnp.sum(dxv * y_ref[...].astype(F32), axis=0, keepdims=True)
        dy = ((0.5 * gt_ref[...]) * dxv).astype(BF)
        dy_ref[...] = dy
        for c0 in range(0, F, tn):
            cols = pl.ds(c0, tn)
            dact = lax.dot_general(dy, wd_ref[cols, :], NT, preferred_element_type=F32)
            a = a_ref[:, cols].astype(F32)
            b = b_ref[:, cols].astype(F32)
            s = _sigmoid(a)
            dab_ref[0, :, cols] = (dact * b * (s * (1.0 + a * (1.0 - s)))).astype(BF)
            dab_ref[1, :, cols] = (dact * (a * s)).astype(BF)

    vec = pl.BlockSpec((1, D), _const2)
    rowspec = pl.BlockSpec((tm, D), _row)
    return pl.pallas_call(
        body, name=name, grid=(T // tm,),
        in_specs=[rowspec, rowspec, vec, pl.BlockSpec((tm, F), lambda i: (i, 0)),
                  pl.BlockSpec((tm, F), lambda i: (i, 1)), pl.BlockSpec((F, D), _const2)],
        out_specs=[rowspec, pl.BlockSpec((2, tm, F), lambda i: (0, i, 0)), vec],
        out_shape=[jax.ShapeDtypeStruct((T, D), BF), jax.ShapeDtypeStruct((2, T, F), BF),
                   jax.ShapeDtypeStruct((1, D), F32)],
        compiler_params=_cp(("arbitrary",)),
    )(dxo, y, gt, ab, ab, wd)


def _ffn_down_bwd_dw(dxo, y, gt, ab, wd, *, tm, name):
    T = dxo.shape[0]
    tm = _tile(T, tm)
    nt = T // tm
    hw = F // 2
    chunks = [(c0, min(MXU_N, hw - c0)) for c0 in range(0, hw, MXU_N)]

    def body(dxo_ref, y_ref, gt_ref, a_ref, b_ref, wd_ref, dab_ref, dgt_ref, dwd_ref, dys, dyt, acc, stage, sem):
        i, j = pl.program_id(0), pl.program_id(1)

        @pl.when(jnp.logical_and(i == 0, j == 0))
        def _():
            dgt_ref[...] = jnp.zeros_like(dgt_ref)

        @pl.when(i == 0)
        def _():
            acc[j] = jnp.zeros((D, hw), F32)

        @pl.when(j == 0)
        def _():
            dxv = dxo_ref[...]
            dgt_ref[...] += 0.5 * jnp.sum(dxv * y_ref[...].astype(F32), axis=0, keepdims=True)
            dyf = (0.5 * gt_ref[...]) * dxv
            dys[...] = dyf.astype(BF)
            dyt[...] = dyf.T.astype(BF)

        dy = dys[...]
        dy_t = dyt[...]

        def dact_of(c0, cw):
            w_rows = pl.ds(pl.multiple_of(j * hw + c0, 128), cw)
            return lax.dot_general(dy, wd_ref[w_rows, :], NT, preferred_element_type=F32)

        ahead = dact_of(*chunks[0])
        for n, (c0, cw) in enumerate(chunks):
            cols = pl.ds(c0, cw)
            dact = ahead
            if n + 1 < len(chunks):
                ahead = dact_of(*chunks[n + 1])
            a = a_ref[:, cols].astype(F32)
            b = b_ref[:, cols].astype(F32)
            s = _sigmoid(a)
            silu = a * s
            dab_ref[0, :, cols] = (dact * b * (s * (1.0 + a * (1.0 - s)))).astype(BF)
            dab_ref[1, :, cols] = (dact * silu).astype(BF)
            acc[j, :, cols] += jnp.dot(dy_t, (silu * b).astype(BF), preferred_element_type=F32)

        @pl.when(i == nt - 1)
        def _():
            for c0, cw in chunks:
                stage[0:cw, :] = acc[j, :, pl.ds(c0, cw)].T.astype(BF)
                out = pltpu.make_async_copy(stage.at[pl.ds(0, cw)],
                                            dwd_ref.at[pl.ds(pl.multiple_of(j * hw + c0, 128), cw)], sem)
                out.start()
                out.wait()

    vec = pl.BlockSpec((1, D), _const2)
    rowspec = pl.BlockSpec((tm, D), lambda i, j: (i, 0))
    return pl.pallas_call(
        body, name=name, grid=(nt, 2),
        in_specs=[rowspec, rowspec, vec, pl.BlockSpec((tm, hw), lambda i, j: (i, j)),
                  pl.BlockSpec((tm, hw), lambda i, j: (i, j + 2)), _resident((F, D))],
        out_specs=[pl.BlockSpec((2, tm, hw), lambda i, j: (0, i, j)), vec, pl.BlockSpec(memory_space=pl.ANY)],
        out_shape=[jax.ShapeDtypeStruct((2, T, F), BF), jax.ShapeDtypeStruct((1, D), F32),
                   jax.ShapeDtypeStruct((F, D), BF)],
        scratch_shapes=[pltpu.VMEM((tm, D), BF), pltpu.VMEM((D, tm), BF), pltpu.VMEM((2, D, hw), F32),
                        pltpu.VMEM((MXU_N, D), BF), pltpu.SemaphoreType.DMA(())],
        compiler_params=_cp(("arbitrary", "arbitrary")),
    )(dxo, y, gt, ab, ab, wd)


def _tn_matmul(a, b, token=None, *, tn, tk, name):
    S, T, Ns = a.shape
    tn, tk = _tile(Ns, tn), _tile(T, tk)
    nk, njs = T // tk, Ns // tn
    deps = [] if token is None else [token]

    def body(a_ref, b_ref, *rest):
        o_ref, acc = rest[len(deps):]
        k = pl.program_id(1)

        @pl.when(k == 0)
        def _():
            acc[...] = jnp.zeros_like(acc)
        acc[...] += lax.dot_general(a_ref[0], b_ref[...], TN, preferred_element_type=F32)

        @pl.when(k == nk - 1)
        def _():
            o_ref[...] = acc[...].astype(BF)

    return pl.pallas_call(
        body, name=name, grid=(S * njs, nk),
        in_specs=[pl.BlockSpec((1, tk, tn), lambda j, k: (j // njs, k, j % njs)),
                  pl.BlockSpec((tk, D), lambda j, k: (k, 0))] + [pl.BlockSpec(memory_space=pl.ANY)] * len(deps),
        out_specs=pl.BlockSpec((tn, D), lambda j, k: (j, 0)),
        out_shape=jax.ShapeDtypeStruct((S * Ns, D), BF),
        scratch_shapes=[pltpu.VMEM((tn, D), F32)],
        compiler_params=_cp(("parallel", "arbitrary")),
    )(a, b, *deps)


def _tn_matmul_swiglu(ab, b, token, *, tn, tk, name):
    T = ab.shape[0]
    tn, tk = _tile(F, tn), _tile(T, tk)
    nk, nj = T // tk, F // tn
    deps = [] if token is None else [token]

    def body(a_ref, g_ref, b_ref, *rest):
        o_ref, acc = rest[len(deps):]
        k = pl.program_id(1)

        @pl.when(k == 0)
        def _():
            acc[...] = jnp.zeros_like(acc)
        bv = b_ref[...]
        for c0 in range(0, tn, MXU_N):
            cw = min(MXU_N, tn - c0)
            cols = pl.ds(c0, cw)
            a = a_ref[:, cols].astype(F32)
            act = (a * _sigmoid(a) * g_ref[:, cols].astype(F32)).astype(BF)
            acc[cols, :] += lax.dot_general(act, bv, TN, preferred_element_type=F32)

        @pl.when(k == nk - 1)
        def _():
            o_ref[...] = acc[...].astype(BF)

    return pl.pallas_call(
        body, name=name, grid=(nj, nk),
        in_specs=[pl.BlockSpec((tk, tn), lambda j, k: (k, j)), pl.BlockSpec((tk, tn), lambda j, k: (k, j + nj)),
                  pl.BlockSpec((tk, D), lambda j, k: (k, 0))] + [pl.BlockSpec(memory_space=pl.ANY)] * len(deps),
        out_specs=pl.BlockSpec((tn, D), lambda j, k: (j, 0)),
        out_shape=jax.ShapeDtypeStruct((F, D), BF),
        scratch_shapes=[pltpu.VMEM((tn, D), F32)],
        compiler_params=_cp(("parallel", "arbitrary")),
    )(ab, ab, b, *deps)


def _nn_bwd_norm(da, w, x, g, sc, dxo, *, tm, name):
    S, T, Ks = da.shape
    tm = _tile(T, tm)
    rc = _tile(tm, 256)

    def body(da_ref, w_ref, x_ref, g_ref, sc_ref, dxo_ref, dx_ref, dsh_ref, dsc_ref, dg_ref, acc):
        @pl.when(pl.program_id(0) == 0)
        def _():
            dsh_ref[...] = jnp.zeros_like(dsh_ref)
            dsc_ref[...] = jnp.zeros_like(dsc_ref)
            dg_ref[...] = jnp.zeros_like(dg_ref)

        d = jnp.dot(da_ref[0], w_ref[0:Ks, :], preferred_element_type=F32)
        for s in range(1, S):
            d = d + jnp.dot(da_ref[s], w_ref[s * Ks:(s + 1) * Ks, :], preferred_element_type=F32)
        acc[...] = d
        gv = g_ref[...]
        sc1 = 1.0 + sc_ref[...]
        dsh = jnp.zeros((1, D), F32)
        dsc = jnp.zeros((1, D), F32)
        dg = jnp.zeros((1, D), F32)
        for r0 in range(0, tm, rc):
            rows = pl.ds(r0, rc)
            u = acc[rows, :]
            xv = x_ref[rows, :]
            r = lax.rsqrt(jnp.mean(xv * xv, axis=-1, keepdims=True) + EPS)
            xh = xv * r
            dsh = dsh + jnp.sum(u, axis=0, keepdims=True)
            dsc = dsc + jnp.sum(u * (xh * gv), axis=0, keepdims=True)
            us = u * sc1
            dg = dg + jnp.sum(us * xh, axis=0, keepdims=True)
            dxh = us * gv
            dx_ref[rows, :] = dxo_ref[rows, :] + r * (dxh - xh * jnp.mean(dxh * xh, axis=-1, keepdims=True))
        dsh_ref[...] += dsh
        dsc_ref[...] += dsc
        dg_ref[...] += dg

    vec = pl.BlockSpec((1, D), _const2)
    rowspec = pl.BlockSpec((tm, D), _row)
    return pl.pallas_call(
        body, name=name, grid=(T // tm,),
        in_specs=[pl.BlockSpec((S, tm, Ks), lambda i: (0, i, 0)), _resident((S * Ks, D)), rowspec, vec, vec, rowspec],
        out_specs=[rowspec, vec, vec, vec],
        out_shape=[jax.ShapeDtypeStruct((T, D), F32)] + [jax.ShapeDtypeStruct((1, D), F32)] * 3,
        scratch_shapes=[pltpu.VMEM((tm, D), F32)],
        compiler_params=_cp(("arbitrary",)),
    )(da, w, x, g, sc, dxo)


def _rope(t, cos, sin_signed, lt32, inverse=False):
    sel = jnp.where(lt32, pltpu.roll(t, 96, 1), pltpu.roll(t, 32, 1))
    return t * cos - sel * sin_signed if inverse else t * cos + sel * sin_signed


def _rope_tables(T):
    inv = 1.0 / (ROPE_THETA ** (jnp.arange(0, HEAD_DIM, 2, dtype=F32) / HEAD_DIM))
    ang = jnp.arange(T, dtype=F32)[:, None] * inv[None, :]
    cos, sin = jnp.cos(ang), jnp.sin(ang)
    cos128 = jnp.tile(cos, (1, 4))
    sin128 = jnp.tile(jnp.concatenate([-sin, sin], axis=1), (1, 2))
    return cos128, sin128


QSCALE = HEAD_DIM ** -0.5


def _lane_masks(rows):
    lane = lax.broadcasted_iota(jnp.int32, (rows, 128), 1)
    return (lane % HEAD_DIM) < (HEAD_DIM // 2), [lane < HEAD_DIM, lane >= HEAD_DIM]


def _attn_bias():
    qi = lax.broadcasted_iota(jnp.int32, (4 * BLK, 2 * BLK), 0) % BLK
    kj = lax.broadcasted_iota(jnp.int32, (4 * BLK, 2 * BLK), 1)
    band = (kj > qi) & (kj <= qi + BLK)
    return jnp.stack([jnp.where(band & (kj >= BLK), 0.0, NEG_INF), jnp.where(band, 0.0, NEG_INF)]).astype(F32)


def _attn_prep(proj, cos, sin, *, name):
    T = proj.shape[0]
    tm = _tile(T, 4 * BLK)

    def body(q_ref, k_ref, c_ref, s_ref, qs_ref, kr_ref):
        lt32, halves = _lane_masks(BLK)
        for b in range(tm // BLK):
            rows = pl.ds(b * BLK, BLK)
            cc, sc = c_ref[rows, :], s_ref[rows, :]
            qr = [_rope(q_ref[rows, p * 128:(p + 1) * 128].astype(F32), cc, sc, lt32) * QSCALE for p in range(8)]
            for g in range(N_KV):
                qs_ref[g, pl.ds(4 * b * BLK, 4 * BLK), :] = _stack_heads(qr, g, halves).astype(BF)
            kr_ref[rows, :] = jnp.concatenate([_rope(k_ref[rows, r * 128:(r + 1) * 128].astype(F32), cc, sc, lt32)
                                               for r in range(2)], axis=1).astype(BF)

    tab = pl.BlockSpec((tm, 128), _row)
    return pl.pallas_call(
        body, name=name, grid=(T // tm,),
        in_specs=[pl.BlockSpec((tm, D), lambda n: (n, O_Q // D)), pl.BlockSpec((tm, 256), lambda n: (n, O_K // 256)),
                  tab, tab],
        out_specs=[pl.BlockSpec((N_KV, 4 * tm, 128), lambda n: (0, n, 0)), pl.BlockSpec((tm, 256), _row)],
        out_shape=[jax.ShapeDtypeStruct((N_KV, 4 * T, 128), BF), jax.ShapeDtypeStruct((T, 256), BF)],
        compiler_params=_cp(("parallel",)),
    )(proj, proj, cos, sin)


def _attn_specs():
    prev = lambda n: jnp.maximum(n - 1, 0)
    return [pl.BlockSpec((N_KV, 4 * BLK, 128), lambda n: (0, n, 0)),
            pl.BlockSpec((BLK, 256), _row), pl.BlockSpec((BLK, 256), lambda n: (prev(n), 0)),
            pl.BlockSpec((BLK, 256), lambda n: (n, O_V // 256)),
            pl.BlockSpec((BLK, 256), lambda n: (prev(n), O_V // 256)),
            pl.BlockSpec((1, 4 * BLK, 2 * BLK), lambda n: (jnp.minimum(n, 1), 0, 0)),
            pl.BlockSpec(memory_space=pltpu.SMEM)]


def _bands(kc_ref, kp_ref, vc_ref, vp_ref):
    kb, vb = [], []
    for r in range(2):
        cols = slice(r * 128, (r + 1) * 128)
        kb.append(jnp.concatenate([kp_ref[:, cols], kc_ref[:, cols]], axis=0))
        vb.append(jnp.concatenate([vp_ref[:, cols], vc_ref[:, cols]], axis=0))
    return kb, vb


def _sink_rows(sink_ref, g):
    return jnp.concatenate([jnp.full((BLK, 128), sink_ref[4 * g + hh], F32) for hh in range(4)], axis=0)


def _both(t):
    return jnp.concatenate([t, t], axis=1)


def _unstack_heads(t, g, halves, acc):
    half = g % 2
    for hh in range(4):
        h = 4 * g + hh
        th = jnp.where(halves[half], t[hh * BLK:(hh + 1) * BLK], 0.0)
        if h % 2 != half:
            th = pltpu.roll(th, HEAD_DIM, 1)
        acc[h // 2] = acc[h // 2] + th


def _stack_heads(chunks, g, halves):
    half = g % 2
    parts = []
    for hh in range(4):
        h = 4 * g + hh
        t = chunks[h // 2]
        if h % 2 != half:
            t = pltpu.roll(t, HEAD_DIM, 1)
        parts.append(jnp.where(halves[half], t, 0.0))
    return jnp.concatenate(parts, axis=0)


def _attn_fwd(qs, kr, proj, bias, sinks, *, name):
    T = proj.shape[0]
    nb = T // BLK

    def body(qs_ref, kc_ref, kp_ref, vc_ref, vp_ref, bias_ref, sink_ref, o_ref, lse_ref):
        _, h128 = _lane_masks(BLK)
        _, h256 = _lane_masks(2 * BLK)
        _, h512 = _lane_masks(4 * BLK)
        kb, vb = _bands(kc_ref, kp_ref, vc_ref, vp_ref)
        outs = [jnp.zeros((BLK, 128), F32) for _ in range(8)]
        groups = range(N_KV)
        bias = bias_ref[0]
        sink = [_sink_rows(sink_ref, g) for g in groups]
        s = [lax.dot_general(qs_ref[g], kb[g // 2], NT, preferred_element_type=F32) + bias for g in groups]
        m = [jnp.maximum(jnp.broadcast_to(jnp.max(s[g], axis=-1, keepdims=True), (4 * BLK, 128)), sink[g])
             for g in groups]
        p = [jnp.exp(s[g] - _both(m[g])).astype(BF) for g in groups]
        vg = [jnp.where(h256[g % 2], vb[g // 2].astype(F32), 1.0).astype(BF) for g in groups]
        o = [jnp.dot(p[g], vg[g], preferred_element_type=F32) for g in groups]
        denom = [jnp.where(h512[g % 2], pltpu.roll(o[g], HEAD_DIM, 1), o[g]) + jnp.exp(sink[g] - m[g]) for g in groups]
        for g in groups:
            lse_ref[g] = m[g] + jnp.log(denom[g])
            _unstack_heads(o[g] * (1.0 / denom[g]), g, h128, outs)
        o_ref[...] = jnp.concatenate(outs, axis=1).astype(BF)

    return pl.pallas_call(
        body, name=name, grid=(nb,),
        in_specs=_attn_specs(),
        out_specs=[pl.BlockSpec((BLK, D), _row), pl.BlockSpec((N_KV, 4 * BLK, 128), lambda n: (0, n, 0))],
        out_shape=[jax.ShapeDtypeStruct((T, D), BF), jax.ShapeDtypeStruct((N_KV, 4 * T, 128), F32)],
        compiler_params=_cp(("parallel",)),
    )(qs, kr, kr, proj, proj, bias, sinks)


def _attn_bwd(qs, kr, proj, bias, sinks, lse, o, do, cos, sin, dproj, *, name):
    T = proj.shape[0]
    nb = T // BLK

    def body(qs_ref, kc_ref, kp_ref, vc_ref, vp_ref, bias_ref, sink_ref, lse_ref, o_ref, do_ref,
             cc_ref, sc_ref, cp_ref, sp_ref, dproj_ref, dq_ref, dkc_ref, dkp_ref, dvc_ref, dvp_ref, dsink_ref):
        @pl.when(pl.program_id(0) == 0)
        def _():
            dsink_ref[...] = jnp.zeros_like(dsink_ref)
        lt32, h128 = _lane_masks(BLK)
        kb, vb = _bands(kc_ref, kp_ref, vc_ref, vp_ref)
        oc = [o_ref[:, p * 128:(p + 1) * 128].astype(F32) for p in range(8)]
        doc = [do_ref[:, p * 128:(p + 1) * 128].astype(F32) for p in range(8)]
        dqs = [jnp.zeros((BLK, 128), F32) for _ in range(8)]
        lane1 = lax.broadcasted_iota(jnp.int32, (1, 128), 1)
        dsink = jnp.zeros((1, 128), F32)
        groups = range(N_KV)
        bias = bias_ref[0]
        q = [qs_ref[g] for g in groups]
        lse_g = [lse_ref[g] for g in groups]
        s = [lax.dot_general(q[g], kb[g // 2], NT, preferred_element_type=F32) + bias for g in groups]
        dos = [_stack_heads(doc, g, h128) for g in groups]
        dosb = [t.astype(BF) for t in dos]
        dp = [lax.dot_general(dosb[g], vb[g // 2], NT, preferred_element_type=F32) for g in groups]
        delta = [jnp.broadcast_to(jnp.sum(dos[g] * _stack_heads(oc, g, h128), axis=-1, keepdims=True), (4 * BLK, 128))
                 for g in groups]
        p = [jnp.exp(s[g] - _both(lse_g[g])) for g in groups]
        ds = [(p[g] * (dp[g] - _both(delta[g]))).astype(BF) for g in groups]
        pb = [t.astype(BF) for t in p]
        dvg = [lax.dot_general(pb[g], dosb[g], TN, preferred_element_type=F32) for g in groups]
        dkg = [lax.dot_general(ds[g], q[g], TN, preferred_element_type=F32) for g in groups]
        dqg = [jnp.dot(ds[g], kb[g // 2], preferred_element_type=F32) * QSCALE for g in groups]
        dvr = [dvg[0] + dvg[1], dvg[2] + dvg[3]]
        dkr = [dkg[0] + dkg[1], dkg[2] + dkg[3]]
        for g in groups:
            _unstack_heads(dqg[g], g, h128, dqs)
            dsk = -jnp.exp(_sink_rows(sink_ref, g) - lse_g[g]) * delta[g]
            for hh in range(4):
                val = jnp.sum(dsk[hh * BLK:(hh + 1) * BLK], axis=0, keepdims=True)
                dsink = dsink + jnp.where(lane1 == 4 * g + hh, val, 0.0)
        cc, sc, cp, sp = cc_ref[...], sc_ref[...], cp_ref[...], sp_ref[...]
        dsink_ref[...] += dsink
        dq_ref[...] = jnp.concatenate([_rope(t, cc, sc, lt32, inverse=True) for t in dqs], axis=1).astype(BF)
        dkp_ref[...] = jnp.concatenate([_rope(t[:BLK], cp, sp, lt32, inverse=True) for t in dkr], axis=1)
        dkc_ref[...] = jnp.concatenate([_rope(t[BLK:], cc, sc, lt32, inverse=True) for t in dkr], axis=1)
        dvp_ref[...] = jnp.concatenate([t[:BLK] for t in dvr], axis=1)
        dvc_ref[...] = jnp.concatenate([t[BLK:] for t in dvr], axis=1)

    kv = pl.BlockSpec((BLK, 256), _row)
    tc = pl.BlockSpec((BLK, 128), _row)
    tp = pl.BlockSpec((BLK, 128), lambda n: (jnp.maximum(n - 1, 0), 0))
    return pl.pallas_call(
        body, name=name, grid=(nb,),
        in_specs=_attn_specs() + [pl.BlockSpec((N_KV, 4 * BLK, 128), lambda n: (0, n, 0)),
                                  pl.BlockSpec((BLK, D), _row), pl.BlockSpec((BLK, D), _row), tc, tc, tp, tp,
                                  pl.BlockSpec(memory_space=pl.ANY)],
        out_specs=[pl.BlockSpec((BLK, D), lambda n: (n, O_Q // D)), kv, kv, kv, kv, pl.BlockSpec((1, 128), _const2)],
        out_shape=[jax.ShapeDtypeStruct(dproj.shape, BF)] + [jax.ShapeDtypeStruct((T, 256), F32)] * 4
        + [jax.ShapeDtypeStruct((1, 128), F32)],
        input_output_aliases={14: 0},
        compiler_params=_cp(("arbitrary",)),
    )(qs, kr, kr, proj, proj, bias, sinks, lse, o, do, cos, sin, cos, sin, dproj)


def _dkv_combine(dkc, dkp, dvc, dvp, dproj, *, name):
    T = dkc.shape[0]
    nb = T // BLK
    tm = _tile(T, 4 * BLK)
    bpt = tm // BLK
    nt = T // tm

    def body(dkc_ref, dkp_ref, dkn_ref, dvc_ref, dvp_ref, dvn_ref, dproj_ref, o_ref):
        keep = jnp.where(pl.program_id(0) == nt - 1, 0.0, 1.0)

        def shifted(prev_ref, next_ref):
            nxt = keep * next_ref[...]
            return nxt if bpt == 1 else jnp.concatenate([prev_ref[BLK:, :], nxt], axis=0)

        o_ref[:, 0:256] = (dkc_ref[...] + shifted(dkp_ref, dkn_ref)).astype(BF)
        o_ref[:, 256:512] = (dvc_ref[...] + shifted(dvp_ref, dvn_ref)).astype(BF)

    cur = pl.BlockSpec((tm, 256), _row)
    nxt = pl.BlockSpec((BLK, 256), lambda i: (jnp.minimum((i + 1) * bpt, nb - 1), 0))
    return pl.pallas_call(
        body, name=name, grid=(nt,),
        in_specs=[cur, cur, nxt, cur, cur, nxt, pl.BlockSpec(memory_space=pl.ANY)],
        out_specs=pl.BlockSpec((tm, 512), lambda i: (i, O_K // 512)),
        out_shape=jax.ShapeDtypeStruct(dproj.shape, BF),
        input_output_aliases={6: 0},
        compiler_params=_cp(("parallel",)),
    )(dkc, dkp, dkp, dvc, dvp, dvp, dproj)


HALO = 16


def _conv_shifts(cu, hprev, tm):
    row = lax.broadcasted_iota(jnp.int32, cu.shape, 0)
    h1 = hprev[HALO - 1:HALO, :]
    h2 = hprev[HALO - 2:HALO - 1, :]
    m1 = jnp.where(row == 0, h1, pltpu.roll(cu, 1, 0))
    m2 = jnp.where(row == 0, h2, jnp.where(row == 1, h1, pltpu.roll(cu, 2, 0)))
    return m1, m2


def _mixer_mid_fwd(proj, attn, wcp, wap, wout, convw, x, gt, *, tm, name):
    T = x.shape[0]
    tm = _tile(T, tm)
    hb = tm // HALO

    def body(bg_ref, cg_ref, u_ref, hcg_ref, hu_ref, zc0_ref, zc1_ref, za0_ref, za1_ref, at_ref,
             wcp_ref, wap_ref, wout_ref, cw_ref, x_ref, gt_ref,
             x2_ref, gc_ref, yc_ref, ya_ref, mg_ref, o_ref):
        first = jnp.where(pl.program_id(0) == 0, 0.0, 1.0)
        cu = cg_ref[...].astype(F32) * u_ref[...].astype(F32)
        hprev = first * (hcg_ref[...].astype(F32) * hu_ref[...].astype(F32))
        m1, m2 = _conv_shifts(cu, hprev, tm)
        cv = cw_ref[0:1, :] * m2 + cw_ref[1:2, :] * m1 + cw_ref[2:3, :] * cu
        gc = (bg_ref[...].astype(F32) * cv).astype(BF)
        gc_ref[...] = gc
        yc = jnp.dot(gc, wcp_ref[...], preferred_element_type=F32)
        ya = jnp.dot(at_ref[...], wap_ref[...], preferred_element_type=F32)
        yc_ref[...] = yc.astype(BF)
        ya_ref[...] = ya.astype(BF)
        zc = jnp.concatenate([zc0_ref[...], zc1_ref[...]], axis=1).astype(F32)
        za = jnp.concatenate([za0_ref[...], za1_ref[...]], axis=1).astype(F32)
        mg = (_sigmoid(zc) * yc + _sigmoid(za) * ya).astype(BF)
        mg_ref[...] = mg
        o = jnp.dot(mg, wout_ref[...], preferred_element_type=F32)
        o_ref[...] = o.astype(BF)
        x2_ref[...] = x_ref[...] + gt_ref[...] * o

    wspec = pl.BlockSpec((D, D), _const2)
    rowspec = pl.BlockSpec((tm, D), _row)
    return pl.pallas_call(
        body, name=name, grid=(T // tm,),
        in_specs=[_col(tm, O_BG), _col(tm, O_CG), _col(tm, O_U), _halo_prev(hb, O_CG), _halo_prev(hb, O_U),
                  _col(tm, O_ZC, 512), _col(tm, O_ZC + 512, 512), _col(tm, O_ZA, 512), _col(tm, O_ZA + 512, 512),
                  rowspec, wspec, wspec, wspec, pl.BlockSpec((8, D), _const2), rowspec, pl.BlockSpec((1, D), _const2)],
        out_specs=[rowspec] * 6,
        out_shape=[jax.ShapeDtypeStruct((T, D), F32)] + [jax.ShapeDtypeStruct((T, D), BF)] * 5,
        compiler_params=_cp(("parallel",)),
    )(proj, proj, proj, proj, proj, proj, proj, proj, proj, attn, wcp, wap, wout, convw, x, gt)


def _col(tm, c, w=D):
    assert c % w == 0
    return pl.BlockSpec((tm, w), lambda i: (i, c // w))


def _halo_prev(hb, c):
    return pl.BlockSpec((HALO, D), lambda i: (jnp.maximum(i * hb - 1, 0), c // D))


def _halo_next(hb, nblk, c=0):
    return pl.BlockSpec((HALO, D), lambda i: (jnp.minimum((i + 1) * hb, nblk - 1), c // D))


def _mixer_mid_bwd(dx2, gt, o, proj, yc, ya, wout, wcp, wap, *, tm, name):
    T = dx2.shape[0]
    tm = _tile(T, tm)
    nt = T // tm

    def body(dx_ref, gt_ref, o_ref, zc0_ref, zc1_ref, za0_ref, za1_ref, yc_ref, ya_ref, wout_ref, wcp_ref, wap_ref,
             dout_ref, dyc_ref, dya_ref, dgc_ref, dat_ref, dproj_ref, dgt_ref, dzs, sems):
        i = pl.program_id(0)
        slot = lax.rem(i, 2)

        def slab_copy(step, s):
            return pltpu.make_async_copy(
                dzs.at[s], dproj_ref.at[pl.ds(pl.multiple_of(step * tm, tm), tm), pl.ds(O_ZC, 2 * D)], sems.at[s])

        @pl.when(i == 0)
        def _():
            dgt_ref[...] = jnp.zeros_like(dgt_ref)

        dxv = dx_ref[...]
        dgt_ref[...] += jnp.sum(dxv * o_ref[...].astype(F32), axis=0, keepdims=True)
        dout = (gt_ref[...] * dxv).astype(BF)
        dout_ref[...] = dout
        dmg = lax.dot_general(dout, wout_ref[...], NT, preferred_element_type=F32)
        sc = _sigmoid(jnp.concatenate([zc0_ref[...], zc1_ref[...]], axis=1).astype(F32))
        sa = _sigmoid(jnp.concatenate([za0_ref[...], za1_ref[...]], axis=1).astype(F32))
        dyc = (dmg * sc).astype(BF)
        dya = (dmg * sa).astype(BF)
        dyc_ref[...] = dyc
        dya_ref[...] = dya
        dzs[slot, :, 0:D] = (dmg * yc_ref[...].astype(F32) * (sc * (1.0 - sc))).astype(BF)
        dzs[slot, :, D:2 * D] = (dmg * ya_ref[...].astype(F32) * (sa * (1.0 - sa))).astype(BF)
        slab_copy(i, slot).start()
        dgc_ref[...] = lax.dot_general(dyc, wcp_ref[...], NT, preferred_element_type=F32).astype(BF)
        dat_ref[...] = lax.dot_general(dya, wap_ref[...], NT, preferred_element_type=F32).astype(BF)

        @pl.when(i > 0)
        def _():
            slab_copy(i - 1, 1 - slot).wait()

        @pl.when(i == nt - 1)
        def _():
            slab_copy(i, slot).wait()

    def zcol(c):
        return pl.BlockSpec((tm, 512), lambda i: (i, c // 512))

    wspec = pl.BlockSpec((D, D), _const2)
    rowspec = pl.BlockSpec((tm, D), _row)
    vec = pl.BlockSpec((1, D), _const2)
    return pl.pallas_call(
        body, name=name, grid=(nt,),
        in_specs=[rowspec, vec, rowspec, zcol(O_ZC), zcol(O_ZC + 512), zcol(O_ZA), zcol(O_ZA + 512),
                  rowspec, rowspec, wspec, wspec, wspec],
        out_specs=[rowspec] * 5 + [pl.BlockSpec(memory_space=pl.ANY), vec],
        out_shape=[jax.ShapeDtypeStruct((T, D), BF)] * 5 + [jax.ShapeDtypeStruct((T, NIN), BF),
                                                            jax.ShapeDtypeStruct((1, D), F32)],
        scratch_shapes=[pltpu.VMEM((2, tm, 2 * D), BF), pltpu.SemaphoreType.DMA((2,))],
        compiler_params=_cp(("arbitrary",)),
    )(dx2, gt, o, proj, proj, proj, proj, yc, ya, wout, wcp, wap)


def _conv_bwd(dgc, proj, convw, dproj, *, tm, name):
    T = dgc.shape[0]
    tm = _tile(T, tm)
    hb = tm // HALO
    nblk = T // HALO
    nt = T // tm

    def body(dgc_ref, ndgc_ref, bg_ref, nbg_ref, cg_ref, u_ref, hcg_ref, hu_ref, cw_ref, dproj_ref, dp_ref, dcw_ref):
        i = pl.program_id(0)

        @pl.when(i == 0)
        def _():
            dcw_ref[...] = jnp.zeros_like(dcw_ref)
        first = jnp.where(i == 0, 0.0, 1.0)
        last = jnp.where(i == nt - 1, 0.0, 1.0)
        cg = cg_ref[...].astype(F32)
        u = u_ref[...].astype(F32)
        bg = bg_ref[...].astype(F32)
        dg = dgc_ref[...].astype(F32)
        cu = cg * u
        hprev = first * (hcg_ref[...].astype(F32) * hu_ref[...].astype(F32))
        m1, m2 = _conv_shifts(cu, hprev, tm)
        w0, w1, w2 = cw_ref[0:1, :], cw_ref[1:2, :], cw_ref[2:3, :]
        cv = w0 * m2 + w1 * m1 + w2 * cu
        dcv = dg * bg
        nxt = last * (ndgc_ref[...].astype(F32) * nbg_ref[...].astype(F32))
        n0, n1 = nxt[0:1, :], nxt[1:2, :]
        row = lax.broadcasted_iota(jnp.int32, dcv.shape, 0)
        p1 = jnp.where(row == tm - 1, n0, pltpu.roll(dcv, tm - 1, 0))
        p2 = jnp.where(row == tm - 1, n1, jnp.where(row == tm - 2, n0, pltpu.roll(dcv, tm - 2, 0)))
        dcu = w2 * dcv + w1 * p1 + w0 * p2
        dp_ref[:, 0:D] = (dg * cv).astype(BF)
        dp_ref[:, D:2 * D] = (dcu * u).astype(BF)
        dp_ref[:, 2 * D:3 * D] = (dcu * cg).astype(BF)
        dcw_ref[0:1, :] += jnp.sum(dcv * m2, axis=0, keepdims=True)
        dcw_ref[1:2, :] += jnp.sum(dcv * m1, axis=0, keepdims=True)
        dcw_ref[2:3, :] += jnp.sum(dcv * cu, axis=0, keepdims=True)

    rowspec = pl.BlockSpec((tm, D), _row)
    cw = pl.BlockSpec((8, D), _const2)
    return pl.pallas_call(
        body, name=name, grid=(nt,),
        in_specs=[rowspec, _halo_next(hb, nblk), _col(tm, O_BG), _halo_next(hb, nblk, O_BG),
                  _col(tm, O_CG), _col(tm, O_U), _halo_prev(hb, O_CG), _halo_prev(hb, O_U), cw,
                  pl.BlockSpec(memory_space=pl.ANY)],
        out_specs=[pl.BlockSpec((tm, 3 * D), _row), cw],
        out_shape=[jax.ShapeDtypeStruct(dproj.shape, BF), jax.ShapeDtypeStruct((8, D), F32)],
        input_output_aliases={9: 0},
        compiler_params=_cp(("arbitrary",)),
    )(dgc, dgc, proj, proj, proj, proj, proj, proj, convw, dproj)


def _adam(w, g, m, v, *, tm, name):
    _, R, C = w.shape
    tm = _tile(R, tm)
    parts = g.ndim == 3
    c1 = 1.0 - ADAM_B1
    c2 = 1.0 - ADAM_B2
    bc1 = 1.0 - ADAM_B1 ** ADAM_STEP
    bc2 = 1.0 - ADAM_B2 ** ADAM_STEP

    def body(w_ref, g_ref, m_ref, v_ref, go_ref, d_ref, nm_ref, nv_ref):
        if parts:
            gv = g_ref[0].astype(F32)
            for s in range(1, N_DEV):
                gv = gv + g_ref[s].astype(F32)
        else:
            gv = g_ref[...]
        go_ref[0] = gv
        nm = ADAM_B1 * m_ref[0] + c1 * gv
        nv = ADAM_B2 * v_ref[0] + c2 * (gv * gv)
        nm_ref[0] = nm
        nv_ref[0] = nv
        d_ref[0] = -ADAM_LR * ((nm / bc1) / (jnp.sqrt(nv / bc2) + ADAM_EPS) + ADAM_WD * w_ref[0])

    spec = pl.BlockSpec((1, tm, C), lambda i: (0, i, 0))
    gspec = pl.BlockSpec((N_DEV, tm, C), lambda i: (0, i, 0)) if parts else pl.BlockSpec((tm, C), _row)
    return pl.pallas_call(
        body, name=name, grid=(R // tm,),
        in_specs=[spec, gspec, spec, spec], out_specs=[spec] * 4,
        out_shape=[jax.ShapeDtypeStruct((1, R, C), F32)] * 4,
        compiler_params=_cp(("parallel",)),
    )(w, g, m, v)


def _mods_part(c_all, w_ada, b_ada, *, name):
    C = w_ada.shape[1]

    def body(c_ref, w_ref, b_ref, o_ref):
        cv = c_ref[...]
        ca = cv * jax.nn.sigmoid(cv)
        o_ref[...] = jnp.dot(ca, w_ref[...], preferred_element_type=F32,
                             precision=lax.Precision.HIGHEST) + b_ref[...]

    return pl.pallas_call(
        body, name=name,
        out_shape=jax.ShapeDtypeStruct((N_DEV, C), F32),
        compiler_params=_cp(),
    )(c_all, w_ada, b_ada)


def _wada_grad(c_all_t, gm, *, name):
    C = gm.shape[1]

    def body(c_ref, g_ref, o_ref):
        cv = c_ref[...]
        ca = cv * jax.nn.sigmoid(cv)
        acc = ca[:, 0:1] * g_ref[0:1, :]
        for b in range(1, N_DEV):
            acc = acc + ca[:, b:b + 1] * g_ref[b:b + 1, :]
        o_ref[...] = acc

    return pl.pallas_call(
        body, name=name,
        out_shape=jax.ShapeDtypeStruct((D, C), F32),
        compiler_params=_cp(),
    )(c_all_t, gm)


def _peer(x, y, c, d):
    px = lax.rem(x + ((d >> 2) & 1), 2)
    py = lax.rem(y + ((d >> 1) & 1), 2)
    pc = lax.rem(c + (d & 1), 2)
    return (px, py, pc), 4 * px + 2 * py + pc


def _exchange(xs, *, scatter, name):
    n = len(xs)
    nsem = n * (N_DEV - 1)

    def body(*refs):
        ins, outs = refs[:n], refs[n:2 * n]
        token, send_sems, recv_sems, local_sems = refs[2 * n:]
        x, y, c = lax.axis_index("x"), lax.axis_index("y"), lax.axis_index("c")
        me = 4 * x + 2 * y + c
        token[...] = jnp.zeros_like(token)

        def src(t, idx):
            return ins[t].at[idx] if scatter else ins[t]

        local = [pltpu.make_async_copy(src(t, me), outs[t].at[me], local_sems.at[t]) for t in range(n)]
        for cp in local:
            cp.start()
        remote = []
        for t in range(n):
            for d in range(1, N_DEV):
                peer, pidx = _peer(x, y, c, d)
                k = t * (N_DEV - 1) + d - 1
                send = pltpu.make_async_remote_copy(src_ref=src(t, pidx), dst_ref=outs[t].at[me],
                                                    send_sem=send_sems.at[k], recv_sem=recv_sems.at[k],
                                                    device_id=peer, device_id_type=MESH)
                recv = pltpu.make_async_remote_copy(src_ref=src(t, pidx), dst_ref=outs[t].at[pidx],
                                                    send_sem=send_sems.at[k], recv_sem=recv_sems.at[k],
                                                    device_id=peer, device_id_type=MESH)
                send.start()
                remote.append((send, recv))
        for cp in local:
            cp.wait()
        for send, recv in remote:
            send.wait_send()
            recv.wait_recv()

    anyspec = pl.BlockSpec(memory_space=pl.ANY)
    out_shape = [jax.ShapeDtypeStruct(a.shape if scatter else (N_DEV,) + a.shape, a.dtype) for a in xs]
    out_shape.append(jax.ShapeDtypeStruct((8, 128), F32))
    return pl.pallas_call(
        body, name=name,
        in_specs=[anyspec] * n, out_specs=[anyspec] * n + [pl.BlockSpec(memory_space=pltpu.VMEM)],
        out_shape=out_shape,
        scratch_shapes=[pltpu.SemaphoreType.DMA((nsem,)), pltpu.SemaphoreType.DMA((nsem,)),
                        pltpu.SemaphoreType.DMA((n,))],
    )(*xs)


def _sum8(parts, *, name):
    _, R, C = parts.shape

    def body(p_ref, o_ref):
        acc = p_ref[0]
        for s in range(1, N_DEV):
            acc = acc + p_ref[s]
        o_ref[...] = acc

    return pl.pallas_call(body, name=name, out_shape=jax.ShapeDtypeStruct((R, C), F32),
                          compiler_params=_cp())(parts)


HBM_SPEC = pl.BlockSpec(memory_space=pltpu.HBM)
SEM_SPEC = pl.BlockSpec(memory_space=pltpu.SEMAPHORE)
N_PEER = N_DEV - 1


def _split_copies(src_refs, land_refs, send_sems, recv_sems, scatter):
    x, y, c = lax.axis_index("x"), lax.axis_index("y"), lax.axis_index("c")
    me = 4 * x + 2 * y + c
    pairs = []
    for j, (src, land) in enumerate(zip(src_refs, land_refs)):
        for d in range(1, N_DEV):
            peer, pidx = _peer(x, y, c, d)
            k = j * N_PEER + d - 1
            s = src.at[pidx] if scatter else src
            send = pltpu.make_async_remote_copy(src_ref=s, dst_ref=land.at[me], send_sem=send_sems.at[k],
                                                recv_sem=recv_sems.at[k], device_id=peer, device_id_type=MESH)
            recv = pltpu.make_async_remote_copy(src_ref=s, dst_ref=land.at[pidx], send_sem=send_sems.at[k],
                                                recv_sem=recv_sems.at[k], device_id=peer, device_id_type=MESH)
            pairs.append((send, recv))
    return pairs


def _own_slot(block, me):
    land = lax.empty((N_DEV,) + block.shape, block.dtype)
    return lax.dynamic_update_slice(land, block[None], (me, 0, 0))


def _split_start(srcs, lands, groups, *, scatter, name):
    n, ng = len(srcs), len(groups)

    def body(*refs):
        src_refs, land_refs = refs[:n], refs[n:2 * n]
        sems = refs[2 * n:2 * n + 2 * ng]
        token = refs[-1]
        for gi, g in enumerate(groups):
            pairs = _split_copies([src_refs[t] for t in g], [land_refs[t] for t in g], sems[2 * gi],
                                  sems[2 * gi + 1], scatter)
            for send, _ in pairs:
                send.start()
        token[...] = jnp.zeros_like(token)

    sem_shapes = []
    for g in groups:
        sem_shapes += [pltpu.SemaphoreType.DMA((len(g) * N_PEER,))] * 2
    thru = [pltpu.HBM(a.shape, a.dtype) for a in list(srcs) + list(lands)]
    outs = pl.pallas_call(
        body, name=name,
        out_shape=tuple(sem_shapes + thru + [jax.ShapeDtypeStruct((8, 128), F32)]),
        in_specs=[HBM_SPEC] * (2 * n),
        out_specs=tuple([SEM_SPEC] * (2 * ng) + [HBM_SPEC] * (2 * n) + [pl.BlockSpec(memory_space=pltpu.VMEM)]),
        input_output_aliases={i: 2 * ng + i for i in range(2 * n)},
        compiler_params=pltpu.CompilerParams(has_side_effects=pltpu.SideEffectType.DATAFLOW_SIDE_EFFECTING),
    )(*[pltpu.with_memory_space_constraint(a, pltpu.HBM) for a in list(srcs) + list(lands)])
    sems = [(outs[2 * gi], outs[2 * gi + 1]) for gi in range(ng)]
    return sems, outs[2 * ng:2 * ng + n], outs[2 * ng + n:2 * ng + 2 * n], outs[-1]


def _behind(v, token):
    if token is None:
        return v
    return v + token[0, 0].astype(v.dtype)


def _split_wait(srcs, lands, sems, after, *, scatter, name):
    m = len(srcs)

    def body(*refs):
        src_refs, land_refs = refs[:m], refs[m:2 * m]
        send_sems, recv_sems = refs[2 * m], refs[2 * m + 1]
        for send, recv in _split_copies(src_refs, land_refs, send_sems, recv_sems, scatter):
            send.wait_send()
            recv.wait_recv()

    outs = pl.pallas_call(
        body, name=name,
        out_shape=tuple(pltpu.HBM(a.shape, a.dtype) for a in list(srcs) + list(lands)),
        in_specs=[HBM_SPEC] * (2 * m) + [SEM_SPEC, SEM_SPEC, pl.BlockSpec(memory_space=pl.ANY)],
        out_specs=tuple([HBM_SPEC] * (2 * m)),
        input_output_aliases={i: i for i in range(2 * m)},
        compiler_params=pltpu.CompilerParams(has_side_effects=pltpu.SideEffectType.DATAFLOW_SIDE_EFFECTING),
    )(*srcs, *lands, sems[0], sems[1], after)
    return outs[m:]


TL_FIRST = (1, 2, 4, 6)
TL_ICI = (2, 4, 6)
EFFECT = pltpu.SideEffectType.DATAFLOW_SIDE_EFFECTING


def _tl_first(src_refs, land_refs, send_sems, recv_sems):
    x, y, c = lax.axis_index("x"), lax.axis_index("y"), lax.axis_index("c")
    me = 4 * x + 2 * y + c
    out = []
    for j, (src, land) in enumerate(zip(src_refs, land_refs)):
        for i, d in enumerate(TL_FIRST):
            peer, pidx = _peer(x, y, c, d)
            k = len(TL_FIRST) * j + i
            send = pltpu.make_async_remote_copy(src_ref=src, dst_ref=land.at[me], send_sem=send_sems.at[k],
                                                recv_sem=recv_sems.at[k], device_id=peer, device_id_type=MESH)
            recv = pltpu.make_async_remote_copy(src_ref=src, dst_ref=land.at[pidx], send_sem=send_sems.at[k],
                                                recv_sem=recv_sems.at[k], device_id=peer, device_id_type=MESH)
            out.append((d, send, recv))
    return out


def _tl_second(land_refs, send_sems, recv_sems):
    x, y, c = lax.axis_index("x"), lax.axis_index("y"), lax.axis_index("c")
    sibling, _ = _peer(x, y, c, 1)
    out = []
    for j, land in enumerate(land_refs):
        for i, d in enumerate(TL_ICI):
            _, mine = _peer(x, y, c, d)
            _, theirs = _peer(x, y, c, d + 1)
            k = len(TL_ICI) * j + i
            send = pltpu.make_async_remote_copy(src_ref=land.at[mine], dst_ref=land.at[mine], send_sem=send_sems.at[k],
                                                recv_sem=recv_sems.at[k], device_id=sibling, device_id_type=MESH)
            recv = pltpu.make_async_remote_copy(src_ref=land.at[mine], dst_ref=land.at[theirs],
                                                send_sem=send_sems.at[k], recv_sem=recv_sems.at[k],
                                                device_id=sibling, device_id_type=MESH)
            out.append((send, recv))
    return out


def _tl_start(srcs, lands, groups, *, name):
    n, ng = len(srcs), len(groups)

    def body(*refs):
        src_refs, land_refs = refs[:n], refs[n:2 * n]
        sems = refs[2 * n:2 * n + 2 * ng]
        for gi, g in enumerate(groups):
            for _, send, _ in _tl_first([src_refs[t] for t in g], [land_refs[t] for t in g], sems[2 * gi],
                                        sems[2 * gi + 1]):
                send.start()
        refs[-1][...] = jnp.zeros_like(refs[-1])

    sem_shapes = []
    for g in groups:
        sem_shapes += [pltpu.SemaphoreType.DMA((len(g) * len(TL_FIRST),))] * 2
    thru = [pltpu.HBM(a.shape, a.dtype) for a in list(srcs) + list(lands)]
    outs = pl.pallas_call(
        body, name=name,
        out_shape=tuple(sem_shapes + thru + [jax.ShapeDtypeStruct((8, 128), F32)]),
        in_specs=[HBM_SPEC] * (2 * n),
        out_specs=tuple([SEM_SPEC] * (2 * ng) + [HBM_SPEC] * (2 * n) + [pl.BlockSpec(memory_space=pltpu.VMEM)]),
        input_output_aliases={i: 2 * ng + i for i in range(2 * n)},
        compiler_params=pltpu.CompilerParams(has_side_effects=EFFECT),
    )(*[pltpu.with_memory_space_constraint(a, pltpu.HBM) for a in list(srcs) + list(lands)])
    sems = [(outs[2 * gi], outs[2 * gi + 1]) for gi in range(ng)]
    return sems, outs[2 * ng:2 * ng + n], outs[2 * ng + n:2 * ng + 2 * n], outs[-1]


def _tl_forward(srcs, lands, sems1, after, *, name):
    m = len(srcs)

    def body(*refs):
        src_refs, land_refs = refs[:m], refs[m:2 * m]
        send1, recv1 = refs[2 * m], refs[2 * m + 1]
        send2, recv2 = refs[2 * m + 3], refs[2 * m + 4]
        for d, _, recv in _tl_first(src_refs, land_refs, send1, recv1):
            if d in TL_ICI:
                recv.wait_recv()
        for send, _ in _tl_second(land_refs, send2, recv2):
            send.start()

    sem = pltpu.SemaphoreType.DMA((m * len(TL_ICI),))
    outs = pl.pallas_call(
        body, name=name,
        out_shape=tuple([sem, sem] + [pltpu.HBM(a.shape, a.dtype) for a in list(srcs) + list(lands)]),
        in_specs=[HBM_SPEC] * (2 * m) + [SEM_SPEC, SEM_SPEC, pl.BlockSpec(memory_space=pl.ANY)],
        out_specs=tuple([SEM_SPEC, SEM_SPEC] + [HBM_SPEC] * (2 * m)),
        input_output_aliases={i: 2 + i for i in range(2 * m)},
        compiler_params=pltpu.CompilerParams(has_side_effects=EFFECT),
    )(*srcs, *lands, sems1[0], sems1[1], after)
    return (outs[0], outs[1]), outs[2:2 + m], outs[2 + m:2 + 2 * m]


def _tl_wait(srcs, lands, sems1, sems2, after, *, name):
    m = len(srcs)

    def body(*refs):
        src_refs, land_refs = refs[:m], refs[m:2 * m]
        send1, recv1, send2, recv2 = refs[2 * m:2 * m + 4]
        for d, send, recv in _tl_first(src_refs, land_refs, send1, recv1):
            send.wait_send()
            if d not in TL_ICI:
                recv.wait_recv()
        for send, recv in _tl_second(land_refs, send2, recv2):
            send.wait_send()
            recv.wait_recv()

    outs = pl.pallas_call(
        body, name=name,
        out_shape=tuple(pltpu.HBM(a.shape, a.dtype) for a in list(srcs) + list(lands)),
        in_specs=[HBM_SPEC] * (2 * m) + [SEM_SPEC] * 4 + [pl.BlockSpec(memory_space=pl.ANY)],
        out_specs=tuple([HBM_SPEC] * (2 * m)),
        input_output_aliases={i: i for i in range(2 * m)},
        compiler_params=pltpu.CompilerParams(has_side_effects=EFFECT),
    )(*srcs, *lands, sems1[0], sems1[1], sems2[0], sems2[1], after)
    return outs[m:]


TM_PROJ = 512
TN_PROJ = 512
TM_ROW = 512
TM_NN = 512
TK_TN = 2048
TM_ADAM = 416
TN_FFN = F // 2
TN_IN = NIN // 4


def _tn(a, b, name, tn, token=None):
    if a.ndim == 2:
        a = a[None]
    return _tn_matmul(a, b, token, tn=tn, tk=TK_TN, name=name)


def _local_step(x, tgt, mods, g1, gm, g2, gf, convw8, sinks, w_get, g_put):
    T = x.shape[0]
    sh1, sc1, gt1, sh2, sc2, gt2, sh3, sc3, gt3 = [mods[i:i + 1] for i in range(N_MOD)]
    cos, sin = _rope_tables(T)
    behind = _behind

    w = dict(w_get("gu1", mods))
    h1, ab1 = _norm_proj(x, g1, sc1, sh1, w["gu1"], tm=TM_PROJ, tn=TN_PROJ, name="ffn1_up")
    w.update(w_get("d1", ab1))
    x1, y1 = _ffn_down_fwd(ab1, w["d1"], x, gt1, tm=TM_ROW, name="ffn1_down")
    w.update(w_get("mix", x1))
    h2, proj = _norm_proj(x1, gm, sc2, sh2, w["win"], tm=TM_PROJ, tn=TN_PROJ, name="mix_in")
    qs, kr = _attn_prep(proj, cos, sin, name="attn_prep")
    bias = _attn_bias()
    attn, lse = _attn_fwd(qs, kr, proj, bias, sinks, name="attn_fwd")
    x2, gc, yc, ya, mg, o = _mixer_mid_fwd(proj, attn, w["cp"], w["ap"], w["out"], convw8, x1, gt2,
                                           tm=TM_ROW, name="mix_mid")
    w.update(w_get("ffn2", x2))
    h3, ab2, y2, dx3, lsum, dgf = _ffn_fwd(x2, g2, sc3, sh3, gt3, w["gu2"], w["d2"], (tgt, gf), tm=TM_ROW,
                                           name="ffn2_final")

    dab2, dgt3, g_d2 = _ffn_down_bwd_dw(dx3, y2, gt3, ab2, w["d2"], tm=TM_ROW, name="ffn2_down_bwd")
    dx2, dsh3, dsc3, dg2 = _nn_bwd_norm(dab2, w["gu2"], x2, g2, sc3, dx3, tm=TM_NN, name="ffn2_up_bwd")
    g_gu2 = _tn(dab2, h3, "ffn2_up_dw", TN_FFN)
    tok = g_put(dict(gu2=g_gu2, d2=g_d2))

    dout, dyc, dya, dgc, dat, dproj, dgt2 = _mixer_mid_bwd(dx2, behind(gt2, tok), o, proj, yc, ya, w["out"], w["cp"],
                                                           w["ap"], tm=TM_ROW, name="mix_mid_bwd")
    g_out = _tn(mg, dout, "mix_out_dw", D)
    g_cp = _tn(gc, dyc, "mix_cp_dw", D)
    g_ap = _tn(attn, dya, "mix_ap_dw", D)
    dproj, dkc, dkp, dvc, dvp, dsink = _attn_bwd(qs, kr, proj, bias, sinks, lse, attn, dat, cos, sin, dproj,
                                                 name="attn_bwd")
    dproj = _dkv_combine(dkc, dkp, dvc, dvp, dproj, name="attn_dkv")
    dproj, dcw = _conv_bwd(dgc, proj, convw8, dproj, tm=TM_ROW, name="conv_bwd")
    g_in = _tn(dproj, h2, "mix_in_dw", TN_IN)
    tok = g_put(dict(win=g_in, cp=g_cp, ap=g_ap, out=g_out))
    dx1, dsh2, dsc2, dgm = _nn_bwd_norm(dproj[None], w["win"], x1, gm, behind(sc2, tok), dx2, tm=TM_NN,
                                        name="mix_in_bwd")

    dab1, dgt1, g_d1 = _ffn_down_bwd_dw(dx1, y1, gt1, ab1, w["d1"], tm=TM_ROW, name="ffn1_down_bwd")
    tok = g_put(dict(d1=g_d1))
    g_gu1 = _tn(dab1, h1, "ffn1_up_dw", TN_FFN, tok)
    tok = g_put(dict(gu1=g_gu1))
    dx0, dsh1, dsc1, dg1 = _nn_bwd_norm(dab1, w["gu1"], x, g1, behind(sc1, tok), dx1, tm=TM_NN,
                                        name="ffn1_up_bwd")

    small = dict(mods=jnp.concatenate([dsh1, dsc1, dgt1, dsh2, dsc2, dgt2, dsh3, dsc3, dgt3], axis=0),
                 g1=dg1, gm=dgm, g2=dg2, gf=dgf, convw=dcw[0:3], sinks=dsink[:, 0:N_HEADS])
    return lsum, dx0, small


BIG = ("gu1", "d1", "win", "cp", "ap", "out", "gu2", "d2")
TRANSPOSED = ("gu1", "win", "gu2")
SMALL_ROWS = 24
R_MODS, R_G1, R_GM, R_G2, R_GF, R_CONV, R_SINK = 0, 9, 10, 11, 12, 13, 16


def _pad_to(a, rows, cols):
    return jnp.pad(a, ((0, rows - a.shape[0]), (0, cols - a.shape[1])))


def _pack_small(b_ada, g1, gm, g2, gf, conv, sinks):
    rows = [b_ada.reshape(N_MOD, D), g1.reshape(1, D), gm.reshape(1, D), g2.reshape(1, D), gf.reshape(1, D),
            _pad_to(conv.reshape(3, -1), 3, D), _pad_to(sinks.reshape(1, N_HEADS), 1, D)]
    return _pad_to(jnp.concatenate(rows, axis=0), SMALL_ROWS, D)


def _unpack_small(p, conv_cols):
    return dict(b_ada=p[R_MODS:R_MODS + N_MOD].reshape(1, N_MOD * D), g_ffn1=p[R_G1:R_G1 + 1],
                g_mix=p[R_GM:R_GM + 1], g_ffn2=p[R_G2:R_G2 + 1], g_final=p[R_GF],
                conv_w=p[R_CONV:R_CONV + 3, 0:conv_cols][None], sinks=p[R_SINK:R_SINK + 1, 0:N_HEADS])


def kernel(x, c, w_ada, b_ada, g_ffn1, w1_gu, w1_down, g_mix, w_in, conv_w, w_conv_proj, w_attn_proj, sinks, w_out, g_ffn2, w2_gu, w2_down, g_final, loss_target, m_w_ada, m_b_ada, m_g_ffn1, m_w1_gu, m_w1_down, m_g_mix, m_w_in, m_conv_w, m_w_conv_proj, m_w_attn_proj, m_sinks, m_w_out, m_g_ffn2, m_w2_gu, m_w2_down, m_g_final, v_w_ada, v_b_ada, v_g_ffn1, v_w1_gu, v_w1_down, v_g_mix, v_w_in, v_conv_w, v_w_conv_proj, v_w_attn_proj, v_sinks, v_w_out, v_g_ffn2, v_w2_gu, v_w2_down, v_g_final):
    me = 4 * lax.axis_index("x") + 2 * lax.axis_index("y") + lax.axis_index("c")
    ada_cols = w_ada.shape[2]
    conv_cols = conv_w.shape[2]

    native = dict(gu1=w1_gu[0], d1=w1_down[0], win=w_in[0], cp=w_conv_proj[0], ap=w_attn_proj[0], out=w_out[0],
                  gu2=w2_gu[0], d2=w2_down[0])

    def shard(n, token):
        a = _behind(native[n], token)
        return (a.T if n in TRANSPOSED else a).astype(BF)

    c_all, conv_all, _ = _exchange([c, _pad_to(conv_w[0], 8, conv_cols)], scatter=False, name="gather_cond")
    c_all = c_all.reshape(N_DEV, D)
    conv_full = conv_all[:, 0:3, :].transpose(1, 0, 2).reshape(3, D)

    b_cols = lax.dynamic_slice(b_ada, (0, me * ada_cols), (1, ada_cols))
    mods_cols = _mods_part(c_all, w_ada[0], b_cols, name="ada_mods")
    mods_all, mods_token = _exchange([mods_cols], scatter=False, name="gather_mods")
    mods = lax.dynamic_index_in_dim(mods_all, me, axis=1, keepdims=False).reshape(N_MOD, D)

    groups = dict(gu1=("gu1",), d1=("d1",), mix=("win", "cp", "ap", "out"), ffn2=("gu2", "d2"))
    in_flight = {}
    first = [shard("gu1", mods_token)]
    sems, srcs, lands, token = _tl_start(first, [_own_slot(s, me) for s in first], [[0]],
                                         name="gather_weights_start_gu1")
    in_flight["gu1"] = [sems[0], srcs, lands, None]
    rest = [n for n in BIG if n != "gu1"]
    shards = [shard(n, token) for n in rest]
    rest_groups = [[rest.index(n) for n in names] for g, names in groups.items() if g != "gu1"]
    sems, srcs, lands, rest_token = _tl_start(shards, [_own_slot(s, me) for s in shards], rest_groups,
                                              name="gather_weights_start_rest")
    for (g, names), gsems, idx in zip([kv for kv in groups.items() if kv[0] != "gu1"], sems, rest_groups):
        in_flight[g] = [gsems, [srcs[t] for t in idx], [lands[t] for t in idx], None]

    def forward(group, after):
        sems1, gsrcs, glands, _ = in_flight[group]
        sems2, gsrcs, glands = _tl_forward(gsrcs, glands, sems1, after, name="gather_weights_forward_" + group)
        in_flight[group] = [sems1, gsrcs, glands, sems2]

    forward_early = dict(d1="mix", mix="ffn2")

    def w_get(group, after):
        if group == "gu1":
            after = rest_token
        if in_flight[group][3] is None:
            forward(group, after)
        sems1, gsrcs, glands, sems2 = in_flight[group]
        landed = _tl_wait(gsrcs, glands, sems1, sems2, after, name="gather_weights_wait_" + group)
        if group in forward_early:
            forward(forward_early[group], landed[0])
        return {n: a.reshape(-1, D) for n, a in zip(groups[group], landed)}

    pending = []

    def g_put(gs):
        names = tuple(gs)
        srcs = [gs[n].reshape(N_DEV, -1, D) for n in names]
        lands = [_own_slot(lax.dynamic_index_in_dim(s, me, axis=0, keepdims=False), me) for s in srcs]
        sems, srcs, lands, tok = _split_start(srcs, lands, [list(range(len(names)))], scatter=True,
                                              name="scatter_grads_start_" + names[0])
        pending.append((names, sems[0], srcs, lands))
        return tok

    lsum, grad_x, small = _local_step(x[0], loss_target[0], mods, g_ffn1, g_mix, g_ffn2, g_final[None],
                                      _pad_to(conv_full, 8, D), sinks[0], w_get, g_put)
    loss = lax.psum((0.5 / D) * jnp.sum(lsum), ("x", "y", "c"))

    packed = _pack_small(small["mods"], small["g1"], small["gm"], small["g2"], small["gf"], small["convw"],
                         small["sinks"])
    packed_all, _ = _exchange([packed], scatter=False, name="gather_small")
    gsmall = _sum8(packed_all, name="sum_small")

    w_of = dict(ada=w_ada, gu1=w1_gu, d1=w1_down, win=w_in, cp=w_conv_proj, ap=w_attn_proj, out=w_out, gu2=w2_gu,
                d2=w2_down)
    m_of = dict(ada=m_w_ada, gu1=m_w1_gu, d1=m_w1_down, win=m_w_in, cp=m_w_conv_proj, ap=m_w_attn_proj, out=m_w_out,
                gu2=m_w2_gu, d2=m_w2_down)
    v_of = dict(ada=v_w_ada, gu1=v_w1_gu, d1=v_w1_down, win=v_w_in, cp=v_w_conv_proj, ap=v_w_attn_proj, out=v_w_out,
                gu2=v_w2_gu, d2=v_w2_down)
    upd = {}
    after = gsmall
    for names, sems, srcs, lands in pending:
        parts = _split_wait(srcs, lands, sems, after, scatter=True, name="scatter_grads_wait_" + names[0])
        for n, p in zip(names, parts):
            if n in TRANSPOSED:
                res = _adam(jnp.swapaxes(w_of[n], 1, 2), p, jnp.swapaxes(m_of[n], 1, 2), jnp.swapaxes(v_of[n], 1, 2),
                            tm=TM_ADAM, name="adam_" + n)
                upd[n] = [jnp.swapaxes(t, 1, 2) for t in res]
            else:
                upd[n] = _adam(w_of[n], p, m_of[n], v_of[n], tm=TM_ADAM, name="adam_" + n)
        after = upd[names[-1]][1]

    gm_cols = lax.dynamic_slice(packed_all[:, R_MODS:R_MODS + N_MOD, :].reshape(N_DEV, N_MOD * D),
                                (0, me * ada_cols), (N_DEV, ada_cols))
    upd["ada"] = _adam(w_ada, _wada_grad(c_all.T, gm_cols, name="ada_dw"), m_w_ada, v_w_ada, tm=256, name="adam_ada")
    conv_g = lax.dynamic_slice(gsmall[R_CONV:R_CONV + 3], (0, me * conv_cols), (3, conv_cols))
    gsmall_own = gsmall.at[R_CONV:R_CONV + 3].set(_pad_to(conv_g, 3, D))
    small_upd = _adam(_pack_small(b_ada, g_ffn1, g_mix, g_ffn2, g_final, conv_w, sinks)[None], gsmall_own,
                      _pack_small(m_b_ada, m_g_ffn1, m_g_mix, m_g_ffn2, m_g_final, m_conv_w, m_sinks)[None],
                      _pack_small(v_b_ada, v_g_ffn1, v_g_mix, v_g_ffn2, v_g_final, v_conv_w, v_sinks)[None],
                      tm=SMALL_ROWS, name="adam_small")
    small_out = [_unpack_small(p[0], conv_cols) for p in small_upd]

    big_name = dict(w_ada="ada", w1_gu="gu1", w1_down="d1", w_in="win", w_conv_proj="cp", w_attn_proj="ap",
                    w_out="out", w2_gu="gu2", w2_down="d2")
    order = ("w_ada", "b_ada", "g_ffn1", "w1_gu", "w1_down", "g_mix", "w_in", "conv_w", "w_conv_proj", "w_attn_proj",
             "sinks", "w_out", "g_ffn2", "w2_gu", "w2_down", "g_final")
    outs = [loss, grad_x[None]]
    for kind in range(4):
        for n in order:
            outs.append(upd[big_name[n]][kind] if n in big_name else small_out[kind][n])
    return tuple(outs)
```

```python
import jax
import jax.numpy as jnp
from jax import lax
from jax.experimental import pallas as pl
from jax.experimental.pallas import tpu as pltpu

D = 1024
F = 2816
NIN = 6656
N_HEADS = 16
N_KV = 4
HEAD_DIM = 64
BLK = 128
N_MOD = 9
N_DEV = 8
EPS = 1e-6
NEG_INF = -1e30
ROPE_THETA = 10000.0
O_BG, O_CG, O_U, O_Q, O_K, O_V, O_ZC, O_ZA = 0, 1024, 2048, 3072, 4096, 4352, 4608, 5632

ADAM_LR = 0.001
ADAM_B1 = 0.9
ADAM_B2 = 0.999
ADAM_EPS = 1e-08
ADAM_WD = 0.01
ADAM_STEP = 10

BF = jnp.bfloat16
F32 = jnp.float32
VMEM_LIMIT = 56 * 1024 * 1024
MXU_N = 256
MESH = pl.DeviceIdType.MESH

NT = (((1,), (1,)), ((), ()))
TN = (((0,), (0,)), ((), ()))


def _cp(sem=None):
    return pltpu.CompilerParams(dimension_semantics=sem, vmem_limit_bytes=VMEM_LIMIT)


def _tile(n, pref):
    if n <= pref:
        return n
    for t in range(pref - pref % 16, 15, -16):
        if n % t == 0:
            return t
    raise ValueError((n, pref))


def _sigmoid(v):
    return 0.5 * jnp.tanh(0.5 * v) + 0.5


def _row(i):
    return (i, 0)


def _const2(*_):
    return (0, 0)


def _resident(shape):
    return pl.BlockSpec(shape, lambda *_: (0,) * len(shape), pipeline_mode=pl.Buffered(1))


def _norm_proj(x, g, sc, sh, wt, *, tm, tn, name):
    T, N = x.shape[0], wt.shape[0]
    tm = _tile(T, tm)

    def body(x_ref, g_ref, sc_ref, sh_ref, w_ref, h_ref, o_ref):
        xv = x_ref[...]
        r = lax.rsqrt(jnp.mean(xv * xv, axis=-1, keepdims=True) + EPS)
        hb = ((xv * r) * g_ref[...] * (1.0 + sc_ref[...]) + sh_ref[...]).astype(BF)
        h_ref[...] = hb
        for c0 in range(0, N, tn):
            cols = pl.ds(c0, tn)
            o_ref[:, cols] = lax.dot_general(hb, w_ref[cols, :], NT, preferred_element_type=F32).astype(BF)

    vec = pl.BlockSpec((1, D), _const2)
    return pl.pallas_call(
        body, name=name, grid=(T // tm,),
        in_specs=[pl.BlockSpec((tm, D), _row), vec, vec, vec, _resident((N, D))],
        out_specs=[pl.BlockSpec((tm, D), _row), pl.BlockSpec((tm, N), _row)],
        out_shape=[jax.ShapeDtypeStruct((T, D), BF), jax.ShapeDtypeStruct((T, N), BF)],
        compiler_params=_cp(("parallel",)),
    )(x, g, sc, sh, wt)


def _ffn_down_fwd(ab, wd, x, gt, *, tm, name):
    T = x.shape[0]
    tm = _tile(T, tm)

    def body(a_ref, b_ref, wd_ref, x_ref, gt_ref, xo_ref, y_ref):
        y = None
        for c0 in range(0, F, MXU_N):
            cols = pl.ds(c0, MXU_N)
            a = a_ref[:, cols].astype(F32)
            act = (a * _sigmoid(a) * b_ref[:, cols].astype(F32)).astype(BF)
            part = jnp.dot(act, wd_ref[cols, :], preferred_element_type=F32)
            y = part if y is None else y + part
        y_ref[...] = y.astype(BF)
        xo_ref[...] = x_ref[...] + (0.5 * gt_ref[...]) * y

    return pl.pallas_call(
        body, name=name, grid=(T // tm,),
        in_specs=[pl.BlockSpec((tm, F), lambda i: (i, 0)), pl.BlockSpec((tm, F), lambda i: (i, 1)),
                  _resident((F, D)), pl.BlockSpec((tm, D), _row), pl.BlockSpec((1, D), _const2)],
        out_specs=[pl.BlockSpec((tm, D), _row), pl.BlockSpec((tm, D), _row)],
        out_shape=[jax.ShapeDtypeStruct((T, D), F32), jax.ShapeDtypeStruct((T, D), BF)],
        compiler_params=_cp(("parallel",)),
    )(ab, ab, wd, x, gt)


def _ffn_fwd(x, g, sc, sh, gt, wgu, wd, final, *, tm, name):
    T = x.shape[0]
    tm = _tile(T, tm)
    last = final is not None

    def body(x_ref, g_ref, sc_ref, sh_ref, gt_ref, wgu_ref, wd_ref, *rest):
        if last:
            t_ref, gf_ref, h_ref, ab_ref, y_ref, dx_ref, ls_ref, dgf_ref = rest
        else:
            h_ref, ab_ref, y_ref, xo_ref = rest
        xv = x_ref[...]
        r = lax.rsqrt(jnp.mean(xv * xv, axis=-1, keepdims=True) + EPS)
        hb = ((xv * r) * g_ref[...] * (1.0 + sc_ref[...]) + sh_ref[...]).astype(BF)
        h_ref[...] = hb
        y = None
        for c0 in range(0, F, MXU_N):
            a = lax.dot_general(hb, wgu_ref[pl.ds(c0, MXU_N), :], NT, preferred_element_type=F32)
            b = lax.dot_general(hb, wgu_ref[pl.ds(F + c0, MXU_N), :], NT, preferred_element_type=F32)
            ab = a.astype(BF)
            bb = b.astype(BF)
            ab_ref[:, pl.ds(c0, MXU_N)] = ab
            ab_ref[:, pl.ds(F + c0, MXU_N)] = bb
            a = ab.astype(F32)
            act = (a * _sigmoid(a) * bb.astype(F32)).astype(BF)
            part = jnp.dot(act, wd_ref[pl.ds(c0, MXU_N), :], preferred_element_type=F32)
            y = part if y is None else y + part
        y_ref[...] = y.astype(BF)
        xo = xv + (0.5 * gt_ref[...]) * y
        if not last:
            xo_ref[...] = xo
            return

        @pl.when(pl.program_id(0) == 0)
        def _():
            ls_ref[...] = jnp.zeros_like(ls_ref)
            dgf_ref[...] = jnp.zeros_like(dgf_ref)
        gv = gf_ref[...]
        r = lax.rsqrt(jnp.mean(xo * xo, axis=-1, keepdims=True) + EPS)
        xh = xo * r
        e = xh * gv - t_ref[...]
        ls_ref[...] += jnp.sum(e * e, axis=0, keepdims=True)
        dy = e * (1.0 / D)
        dgf_ref[...] += jnp.sum(dy * xh, axis=0, keepdims=True)
        dxh = dy * gv
        dx_ref[...] = r * (dxh - xh * jnp.mean(dxh * xh, axis=-1, keepdims=True))

    vec = pl.BlockSpec((1, D), _const2)
    rowspec = pl.BlockSpec((tm, D), _row)
    in_specs = [rowspec, vec, vec, vec, vec, _resident((2 * F, D)), _resident((F, D))]
    out_specs = [rowspec, pl.BlockSpec((tm, 2 * F), _row), rowspec, rowspec]
    out_shape = [jax.ShapeDtypeStruct((T, D), BF), jax.ShapeDtypeStruct((T, 2 * F), BF),
                 jax.ShapeDtypeStruct((T, D), BF), jax.ShapeDtypeStruct((T, D), F32)]
    args = [x, g, sc, sh, gt, wgu, wd]
    if last:
        in_specs += [rowspec, vec]
        out_specs += [vec, vec]
        out_shape += [jax.ShapeDtypeStruct((1, D), F32)] * 2
        args += list(final)
    return pl.pallas_call(
        body, name=name, grid=(T // tm,),
        in_specs=in_specs, out_specs=out_specs, out_shape=out_shape,
        compiler_params=_cp(("arbitrary",) if last else ("parallel",)),
    )(*args)


def _ffn_down_bwd_dw(dxo, y, gt, ab, wd, *, tm, name):
    T = dxo.shape[0]
    tm = _tile(T, tm)
    nt = T // tm
    hw = F // 2
    chunks = [(c0, min(MXU_N, hw - c0)) for c0 in range(0, hw, MXU_N)]

    def body(dxo_ref, y_ref, gt_ref, a_ref, b_ref, wd_ref, dab_ref, dgt_ref, dwd_ref, dys, dyt, acc, stage, sem):
        i, j = pl.program_id(0), pl.program_id(1)

        @pl.when(jnp.logical_and(i == 0, j == 0))
        def _():
            dgt_ref[...] = jnp.zeros_like(dgt_ref)

        @pl.when(i == 0)
        def _():
            acc[j] = jnp.zeros((D, hw), F32)

        @pl.when(j == 0)
        def _():
            dxv = dxo_ref[...]
            dgt_ref[...] += 0.5 * jnp.sum(dxv * y_ref[...].astype(F32), axis=0, keepdims=True)
            dyf = (0.5 * gt_ref[...]) * dxv
            dys[...] = dyf.astype(BF)
            dyt[...] = dyf.T.astype(BF)

        dy = dys[...]
        dy_t = dyt[...]

        def dact_of(c0, cw):
            w_rows = pl.ds(pl.multiple_of(j * hw + c0, 128), cw)
            return lax.dot_general(dy, wd_ref[w_rows, :], NT, preferred_element_type=F32)

        ahead = dact_of(*chunks[0])
        for n, (c0, cw) in enumerate(chunks):
            cols = pl.ds(c0, cw)
            dact = ahead
            if n + 1 < len(chunks):
                ahead = dact_of(*chunks[n + 1])
            a = a_ref[:, cols].astype(F32)
            b = b_ref[:, cols].astype(F32)
            s = _sigmoid(a)
            silu = a * s
            dab_ref[0, :, cols] = (dact * b * (s * (1.0 + a * (1.0 - s)))).astype(BF)
            dab_ref[1, :, cols] = (dact * silu).astype(BF)
            acc[j, :, cols] += jnp.dot(dy_t, (silu * b).astype(BF), preferred_element_type=F32)

        @pl.when(i == nt - 1)
        def _():
            for c0, cw in chunks:
                stage[0:cw, :] = acc[j, :, pl.ds(c0, cw)].T.astype(BF)
                out = pltpu.make_async_copy(stage.at[pl.ds(0, cw)],
                                            dwd_ref.at[pl.ds(pl.multiple_of(j * hw + c0, 128), cw)], sem)
                out.start()
                out.wait()

    vec = pl.BlockSpec((1, D), _const2)
    rowspec = pl.BlockSpec((tm, D), lambda i, j: (i, 0))
    return pl.pallas_call(
        body, name=name, grid=(nt, 2),
        in_specs=[rowspec, rowspec, vec, pl.BlockSpec((tm, hw), lambda i, j: (i, j)),
                  pl.BlockSpec((tm, hw), lambda i, j: (i, j + 2)), _resident((F, D))],
        out_specs=[pl.BlockSpec((2, tm, hw), lambda i, j: (0, i, j)), vec, pl.BlockSpec(memory_space=pl.ANY)],
        out_shape=[jax.ShapeDtypeStruct((2, T, F), BF), jax.ShapeDtypeStruct((1, D), F32),
                   jax.ShapeDtypeStruct((F, D), BF)],
        scratch_shapes=[pltpu.VMEM((tm, D), BF), pltpu.VMEM((D, tm), BF), pltpu.VMEM((2, D, hw), F32),
                        pltpu.VMEM((MXU_N, D), BF), pltpu.SemaphoreType.DMA(())],
        compiler_params=_cp(("arbitrary", "arbitrary")),
    )(dxo, y, gt, ab, ab, wd)


def _tn_matmul(a, b, token=None, *, tn, tk, name):
    S, T, Ns = a.shape
    tn, tk = _tile(Ns, tn), _tile(T, tk)
    nk, njs = T // tk, Ns // tn
    deps = [] if token is None else [token]

    def body(a_ref, b_ref, *rest):
        o_ref, acc = rest[len(deps):]
        k = pl.program_id(1)

        @pl.when(k == 0)
        def _():
            acc[...] = jnp.zeros_like(acc)
        acc[...] += lax.dot_general(a_ref[0], b_ref[...], TN, preferred_element_type=F32)

        @pl.when(k == nk - 1)
        def _():
            o_ref[...] = acc[...].astype(BF)

    return pl.pallas_call(
        body, name=name, grid=(S * njs, nk),
        in_specs=[pl.BlockSpec((1, tk, tn), lambda j, k: (j // njs, k, j % njs)),
                  pl.BlockSpec((tk, D), lambda j, k: (k, 0))] + [pl.BlockSpec(memory_space=pl.ANY)] * len(deps),
        out_specs=pl.BlockSpec((tn, D), lambda j, k: (j, 0)),
        out_shape=jax.ShapeDtypeStruct((S * Ns, D), BF),
        scratch_shapes=[pltpu.VMEM((tn, D), F32)],
        compiler_params=_cp(("parallel", "arbitrary")),
    )(a, b, *deps)


def _nn_bwd_norm(da, w, x, g, sc, dxo, *, tm, name):
    S, T, Ks = da.shape
    tm = _tile(T, tm)
    rc = _tile(tm, 256)

    def body(da_ref, w_ref, x_ref, g_ref, sc_ref, dxo_ref, dx_ref, dsh_ref, dsc_ref, dg_ref, acc):
        @pl.when(pl.program_id(0) == 0)
        def _():
            dsh_ref[...] = jnp.zeros_like(dsh_ref)
            dsc_ref[...] = jnp.zeros_like(dsc_ref)
            dg_ref[...] = jnp.zeros_like(dg_ref)

        d = jnp.dot(da_ref[0], w_ref[0:Ks, :], preferred_element_type=F32)
        for s in range(1, S):
            d = d + jnp.dot(da_ref[s], w_ref[s * Ks:(s + 1) * Ks, :], preferred_element_type=F32)
        acc[...] = d
        gv = g_ref[...]
        sc1 = 1.0 + sc_ref[...]
        dsh = jnp.zeros((1, D), F32)
        dsc = jnp.zeros((1, D), F32)
        dg = jnp.zeros((1, D), F32)
        for r0 in range(0, tm, rc):
            rows = pl.ds(r0, rc)
            u = acc[rows, :]
            xv = x_ref[rows, :]
            r = lax.rsqrt(jnp.mean(xv * xv, axis=-1, keepdims=True) + EPS)
            xh = xv * r
            dsh = dsh + jnp.sum(u, axis=0, keepdims=True)
            dsc = dsc + jnp.sum(u * (xh * gv), axis=0, keepdims=True)
            us = u * sc1
            dg = dg + jnp.sum(us * xh, axis=0, keepdims=True)
            dxh = us * gv
            dx_ref[rows, :] = dxo_ref[rows, :] + r * (dxh - xh * jnp.mean(dxh * xh, axis=-1, keepdims=True))
        dsh_ref[...] += dsh
        dsc_ref[...] += dsc
        dg_ref[...] += dg

    vec = pl.BlockSpec((1, D), _const2)
    rowspec = pl.BlockSpec((tm, D), _row)
    return pl.pallas_call(
        body, name=name, grid=(T // tm,),
        in_specs=[pl.BlockSpec((S, tm, Ks), lambda i: (0, i, 0)), _resident((S * Ks, D)), rowspec, vec, vec, rowspec],
        out_specs=[rowspec, vec, vec, vec],
        out_shape=[jax.ShapeDtypeStruct((T, D), F32)] + [jax.ShapeDtypeStruct((1, D), F32)] * 3,
        scratch_shapes=[pltpu.VMEM((tm, D), F32)],
        compiler_params=_cp(("arbitrary",)),
    )(da, w, x, g, sc, dxo)


def _rope(t, cos, sin_signed, lt32, inverse=False):
    sel = jnp.where(lt32, pltpu.roll(t, 96, 1), pltpu.roll(t, 32, 1))
    return t * cos - sel * sin_signed if inverse else t * cos + sel * sin_signed


def _rope_tables(T):
    inv = 1.0 / (ROPE_THETA ** (jnp.arange(0, HEAD_DIM, 2, dtype=F32) / HEAD_DIM))
    ang = jnp.arange(T, dtype=F32)[:, None] * inv[None, :]
    cos, sin = jnp.cos(ang), jnp.sin(ang)
    cos128 = jnp.tile(cos, (1, 4))
    sin128 = jnp.tile(jnp.concatenate([-sin, sin], axis=1), (1, 2))
    return cos128, sin128


QSCALE = HEAD_DIM ** -0.5


def _lane_masks(rows):
    lane = lax.broadcasted_iota(jnp.int32, (rows, 128), 1)
    return (lane % HEAD_DIM) < (HEAD_DIM // 2), [lane < HEAD_DIM, lane >= HEAD_DIM]


def _attn_bias():
    qi = lax.broadcasted_iota(jnp.int32, (4 * BLK, 2 * BLK), 0) % BLK
    kj = lax.broadcasted_iota(jnp.int32, (4 * BLK, 2 * BLK), 1)
    band = (kj > qi) & (kj <= qi + BLK)
    return jnp.stack([jnp.where(band & (kj >= BLK), 0.0, NEG_INF), jnp.where(band, 0.0, NEG_INF)]).astype(F32)


def _attn_prep(proj, cos, sin, *, name):
    T = proj.shape[0]
    tm = _tile(T, 4 * BLK)

    def body(q_ref, k_ref, c_ref, s_ref, qs_ref, kr_ref):
        lt32, halves = _lane_masks(BLK)
        for b in range(tm // BLK):
            rows = pl.ds(b * BLK, BLK)
            cc, sc = c_ref[rows, :], s_ref[rows, :]
            qr = [_rope(q_ref[rows, p * 128:(p + 1) * 128].astype(F32), cc, sc, lt32) * QSCALE for p in range(8)]
            for g in range(N_KV):
                qs_ref[g, pl.ds(4 * b * BLK, 4 * BLK), :] = _stack_heads(qr, g, halves).astype(BF)
            kr_ref[rows, :] = jnp.concatenate([_rope(k_ref[rows, r * 128:(r + 1) * 128].astype(F32), cc, sc, lt32)
                                               for r in range(2)], axis=1).astype(BF)

    tab = pl.BlockSpec((tm, 128), _row)
    return pl.pallas_call(
        body, name=name, grid=(T // tm,),
        in_specs=[pl.BlockSpec((tm, D), lambda n: (n, O_Q // D)), pl.BlockSpec((tm, 256), lambda n: (n, O_K // 256)),
                  tab, tab],
        out_specs=[pl.BlockSpec((N_KV, 4 * tm, 128), lambda n: (0, n, 0)), pl.BlockSpec((tm, 256), _row)],
        out_shape=[jax.ShapeDtypeStruct((N_KV, 4 * T, 128), BF), jax.ShapeDtypeStruct((T, 256), BF)],
        compiler_params=_cp(("parallel",)),
    )(proj, proj, cos, sin)


def _attn_specs():
    prev = lambda n: jnp.maximum(n - 1, 0)
    return [pl.BlockSpec((N_KV, 4 * BLK, 128), lambda n: (0, n, 0)),
            pl.BlockSpec((BLK, 256), _row), pl.BlockSpec((BLK, 256), lambda n: (prev(n), 0)),
            pl.BlockSpec((BLK, 256), lambda n: (n, O_V // 256)),
            pl.BlockSpec((BLK, 256), lambda n: (prev(n), O_V // 256)),
            pl.BlockSpec((1, 4 * BLK, 2 * BLK), lambda n: (jnp.minimum(n, 1), 0, 0)),
            pl.BlockSpec(memory_space=pltpu.SMEM)]


def _bands(kc_ref, kp_ref, vc_ref, vp_ref):
    kb, vb = [], []
    for r in range(2):
        cols = slice(r * 128, (r + 1) * 128)
        kb.append(jnp.concatenate([kp_ref[:, cols], kc_ref[:, cols]], axis=0))
        vb.append(jnp.concatenate([vp_ref[:, cols], vc_ref[:, cols]], axis=0))
    return kb, vb


def _sink_rows(sink_ref, g):
    return jnp.concatenate([jnp.full((BLK, 128), sink_ref[4 * g + hh], F32) for hh in range(4)], axis=0)


def _both(t):
    return jnp.concatenate([t, t], axis=1)


def _unstack_heads(t, g, halves, acc):
    half = g % 2
    for hh in range(4):
        h = 4 * g + hh
        th = jnp.where(halves[half], t[hh * BLK:(hh + 1) * BLK], 0.0)
        if h % 2 != half:
            th = pltpu.roll(th, HEAD_DIM, 1)
        acc[h // 2] = acc[h // 2] + th


def _stack_heads(chunks, g, halves):
    half = g % 2
    parts = []
    for hh in range(4):
        h = 4 * g + hh
        t = chunks[h // 2]
        if h % 2 != half:
            t = pltpu.roll(t, HEAD_DIM, 1)
        parts.append(jnp.where(halves[half], t, 0.0))
    return jnp.concatenate(parts, axis=0)


def _attn_fwd(qs, kr, proj, bias, sinks, *, name):
    T = proj.shape[0]
    nb = T // BLK

    def body(qs_ref, kc_ref, kp_ref, vc_ref, vp_ref, bias_ref, sink_ref, o_ref, lse_ref):
        _, h128 = _lane_masks(BLK)
        _, h256 = _lane_masks(2 * BLK)
        _, h512 = _lane_masks(4 * BLK)
        kb, vb = _bands(kc_ref, kp_ref, vc_ref, vp_ref)
        outs = [jnp.zeros((BLK, 128), F32) for _ in range(8)]
        groups = range(N_KV)
        bias = bias_ref[0]
        sink = [_sink_rows(sink_ref, g) for g in groups]
        s = [lax.dot_general(qs_ref[g], kb[g // 2], NT, preferred_element_type=F32) + bias for g in groups]
        m = [jnp.maximum(jnp.broadcast_to(jnp.max(s[g], axis=-1, keepdims=True), (4 * BLK, 128)), sink[g])
             for g in groups]
        p = [jnp.exp(s[g] - _both(m[g])).astype(BF) for g in groups]
        vg = [jnp.where(h256[g % 2], vb[g // 2].astype(F32), 1.0).astype(BF) for g in groups]
        o = [jnp.dot(p[g], vg[g], preferred_element_type=F32) for g in groups]
        denom = [jnp.where(h512[g % 2], pltpu.roll(o[g], HEAD_DIM, 1), o[g]) + jnp.exp(sink[g] - m[g]) for g in groups]
        for g in groups:
            lse_ref[g] = m[g] + jnp.log(denom[g])
            _unstack_heads(o[g] * (1.0 / denom[g]), g, h128, outs)
        o_ref[...] = jnp.concatenate(outs, axis=1).astype(BF)

    return pl.pallas_call(
        body, name=name, grid=(nb,),
        in_specs=_attn_specs(),
        out_specs=[pl.BlockSpec((BLK, D), _row), pl.BlockSpec((N_KV, 4 * BLK, 128), lambda n: (0, n, 0))],
        out_shape=[jax.ShapeDtypeStruct((T, D), BF), jax.ShapeDtypeStruct((N_KV, 4 * T, 128), F32)],
        compiler_params=_cp(("parallel",)),
    )(qs, kr, kr, proj, proj, bias, sinks)


def _attn_bwd(qs, kr, proj, bias, sinks, lse, o, do, cos, sin, dproj, *, name):
    T = proj.shape[0]
    nb = T // BLK

    def body(qs_ref, kc_ref, kp_ref, vc_ref, vp_ref, bias_ref, sink_ref, lse_ref, o_ref, do_ref,
             cc_ref, sc_ref, cp_ref, sp_ref, dproj_ref, dq_ref, dkc_ref, dkp_ref, dvc_ref, dvp_ref, dsink_ref):
        @pl.when(pl.program_id(0) == 0)
        def _():
            dsink_ref[...] = jnp.zeros_like(dsink_ref)
        lt32, h128 = _lane_masks(BLK)
        kb, vb = _bands(kc_ref, kp_ref, vc_ref, vp_ref)
        oc = [o_ref[:, p * 128:(p + 1) * 128].astype(F32) for p in range(8)]
        doc = [do_ref[:, p * 128:(p + 1) * 128].astype(F32) for p in range(8)]
        dqs = [jnp.zeros((BLK, 128), F32) for _ in range(8)]
        lane1 = lax.broadcasted_iota(jnp.int32, (1, 128), 1)
        dsink = jnp.zeros((1, 128), F32)
        groups = range(N_KV)
        bias = bias_ref[0]
        q = [qs_ref[g] for g in groups]
        lse_g = [lse_ref[g] for g in groups]
        s = [lax.dot_general(q[g], kb[g // 2], NT, preferred_element_type=F32) + bias for g in groups]
        dos = [_stack_heads(doc, g, h128) for g in groups]
        dosb = [t.astype(BF) for t in dos]
        dp = [lax.dot_general(dosb[g], vb[g // 2], NT, preferred_element_type=F32) for g in groups]
        delta = [jnp.broadcast_to(jnp.sum(dos[g] * _stack_heads(oc, g, h128), axis=-1, keepdims=True), (4 * BLK, 128))
                 for g in groups]
        p = [jnp.exp(s[g] - _both(lse_g[g])) for g in groups]
        ds = [(p[g] * (dp[g] - _both(delta[g]))).astype(BF) for g in groups]
        pb = [t.astype(BF) for t in p]
        dvg = [lax.dot_general(pb[g], dosb[g], TN, preferred_element_type=F32) for g in groups]
        dkg = [lax.dot_general(ds[g], q[g], TN, preferred_element_type=F32) for g in groups]
        dqg = [jnp.dot(ds[g], kb[g // 2], preferred_element_type=F32) * QSCALE for g in groups]
        dvr = [dvg[0] + dvg[1], dvg[2] + dvg[3]]
        dkr = [dkg[0] + dkg[1], dkg[2] + dkg[3]]
        for g in groups:
            _unstack_heads(dqg[g], g, h128, dqs)
            dsk = -jnp.exp(_sink_rows(sink_ref, g) - lse_g[g]) * delta[g]
            for hh in range(4):
                val = jnp.sum(dsk[hh * BLK:(hh + 1) * BLK], axis=0, keepdims=True)
                dsink = dsink + jnp.where(lane1 == 4 * g + hh, val, 0.0)
        cc, sc, cp, sp = cc_ref[...], sc_ref[...], cp_ref[...], sp_ref[...]
        dsink_ref[...] += dsink
        dq_ref[...] = jnp.concatenate([_rope(t, cc, sc, lt32, inverse=True) for t in dqs], axis=1).astype(BF)
        dkp_ref[...] = jnp.concatenate([_rope(t[:BLK], cp, sp, lt32, inverse=True) for t in dkr], axis=1)
        dkc_ref[...] = jnp.concatenate([_rope(t[BLK:], cc, sc, lt32, inverse=True) for t in dkr], axis=1)
        dvp_ref[...] = jnp.concatenate([t[:BLK] for t in dvr], axis=1)
        dvc_ref[...] = jnp.concatenate([t[BLK:] for t in dvr], axis=1)

    kv = pl.BlockSpec((BLK, 256), _row)
    tc = pl.BlockSpec((BLK, 128), _row)
    tp = pl.BlockSpec((BLK, 128), lambda n: (jnp.maximum(n - 1, 0), 0))
    return pl.pallas_call(
        body, name=name, grid=(nb,),
        in_specs=_attn_specs() + [pl.BlockSpec((N_KV, 4 * BLK, 128), lambda n: (0, n, 0)),
                                  pl.BlockSpec((BLK, D), _row), pl.BlockSpec((BLK, D), _row), tc, tc, tp, tp,
                                  pl.BlockSpec(memory_space=pl.ANY)],
        out_specs=[pl.BlockSpec((BLK, D), lambda n: (n, O_Q // D)), kv, kv, kv, kv, pl.BlockSpec((1, 128), _const2)],
        out_shape=[jax.ShapeDtypeStruct(dproj.shape, BF)] + [jax.ShapeDtypeStruct((T, 256), F32)] * 4
        + [jax.ShapeDtypeStruct((1, 128), F32)],
        input_output_aliases={14: 0},
        compiler_params=_cp(("arbitrary",)),
    )(qs, kr, kr, proj, proj, bias, sinks, lse, o, do, cos, sin, cos, sin, dproj)


def _dkv_combine(dkc, dkp, dvc, dvp, dproj, *, name):
    T = dkc.shape[0]
    nb = T // BLK
    tm = _tile(T, 4 * BLK)
    bpt = tm // BLK
    nt = T // tm

    def body(dkc_ref, dkp_ref, dkn_ref, dvc_ref, dvp_ref, dvn_ref, dproj_ref, o_ref):
        keep = jnp.where(pl.program_id(0) == nt - 1, 0.0, 1.0)

        def shifted(prev_ref, next_ref):
            nxt = keep * next_ref[...]
            return nxt if bpt == 1 else jnp.concatenate([prev_ref[BLK:, :], nxt], axis=0)

        o_ref[:, 0:256] = (dkc_ref[...] + shifted(dkp_ref, dkn_ref)).astype(BF)
        o_ref[:, 256:512] = (dvc_ref[...] + shifted(dvp_ref, dvn_ref)).astype(BF)

    cur = pl.BlockSpec((tm, 256), _row)
    nxt = pl.BlockSpec((BLK, 256), lambda i: (jnp.minimum((i + 1) * bpt, nb - 1), 0))
    return pl.pallas_call(
        body, name=name, grid=(nt,),
        in_specs=[cur, cur, nxt, cur, cur, nxt, pl.BlockSpec(memory_space=pl.ANY)],
        out_specs=pl.BlockSpec((tm, 512), lambda i: (i, O_K // 512)),
        out_shape=jax.ShapeDtypeStruct(dproj.shape, BF),
        input_output_aliases={6: 0},
        compiler_params=_cp(("parallel",)),
    )(dkc, dkp, dkp, dvc, dvp, dvp, dproj)


HALO = 16


def _conv_shifts(cu, hprev, tm):
    row = lax.broadcasted_iota(jnp.int32, cu.shape, 0)
    h1 = hprev[HALO - 1:HALO, :]
    h2 = hprev[HALO - 2:HALO - 1, :]
    m1 = jnp.where(row == 0, h1, pltpu.roll(cu, 1, 0))
    m2 = jnp.where(row == 0, h2, jnp.where(row == 1, h1, pltpu.roll(cu, 2, 0)))
    return m1, m2


def _mixer_mid_fwd(proj, attn, wcp, wap, wout, convw, x, gt, *, tm, name):
    T = x.shape[0]
    tm = _tile(T, tm)
    hb = tm // HALO

    def body(bg_ref, cg_ref, u_ref, hcg_ref, hu_ref, zc0_ref, zc1_ref, za0_ref, za1_ref, at_ref,
             wcp_ref, wap_ref, wout_ref, cw_ref, x_ref, gt_ref,
             x2_ref, gc_ref, yc_ref, ya_ref, mg_ref, o_ref):
        first = jnp.where(pl.program_id(0) == 0, 0.0, 1.0)
        cu = cg_ref[...].astype(F32) * u_ref[...].astype(F32)
        hprev = first * (hcg_ref[...].astype(F32) * hu_ref[...].astype(F32))
        m1, m2 = _conv_shifts(cu, hprev, tm)
        cv = cw_ref[0:1, :] * m2 + cw_ref[1:2, :] * m1 + cw_ref[2:3, :] * cu
        gc = (bg_ref[...].astype(F32) * cv).astype(BF)
        gc_ref[...] = gc
        yc = jnp.dot(gc, wcp_ref[...], preferred_element_type=F32)
        ya = jnp.dot(at_ref[...], wap_ref[...], preferred_element_type=F32)
        yc_ref[...] = yc.astype(BF)
        ya_ref[...] = ya.astype(BF)
        zc = jnp.concatenate([zc0_ref[...], zc1_ref[...]], axis=1).astype(F32)
        za = jnp.concatenate([za0_ref[...], za1_ref[...]], axis=1).astype(F32)
        mg = (_sigmoid(zc) * yc + _sigmoid(za) * ya).astype(BF)
        mg_ref[...] = mg
        o = jnp.dot(mg, wout_ref[...], preferred_element_type=F32)
        o_ref[...] = o.astype(BF)
        x2_ref[...] = x_ref[...] + gt_ref[...] * o

    wspec = pl.BlockSpec((D, D), _const2)
    rowspec = pl.BlockSpec((tm, D), _row)
    return pl.pallas_call(
        body, name=name, grid=(T // tm,),
        in_specs=[_col(tm, O_BG), _col(tm, O_CG), _col(tm, O_U), _halo_prev(hb, O_CG), _halo_prev(hb, O_U),
                  _col(tm, O_ZC, 512), _col(tm, O_ZC + 512, 512), _col(tm, O_ZA, 512), _col(tm, O_ZA + 512, 512),
                  rowspec, wspec, wspec, wspec, pl.BlockSpec((8, D), _const2), rowspec, pl.BlockSpec((1, D), _const2)],
        out_specs=[rowspec] * 6,
        out_shape=[jax.ShapeDtypeStruct((T, D), F32)] + [jax.ShapeDtypeStruct((T, D), BF)] * 5,
        compiler_params=_cp(("parallel",)),
    )(proj, proj, proj, proj, proj, proj, proj, proj, proj, attn, wcp, wap, wout, convw, x, gt)


def _col(tm, c, w=D):
    assert c % w == 0
    return pl.BlockSpec((tm, w), lambda i: (i, c // w))


def _halo_prev(hb, c):
    return pl.BlockSpec((HALO, D), lambda i: (jnp.maximum(i * hb - 1, 0), c // D))


def _halo_next(hb, nblk, c=0):
    return pl.BlockSpec((HALO, D), lambda i: (jnp.minimum((i + 1) * hb, nblk - 1), c // D))


def _mixer_mid_bwd(dx2, gt, o, proj, yc, ya, wout, wcp, wap, *, tm, name):
    T = dx2.shape[0]
    tm = _tile(T, tm)
    nt = T // tm

    def body(dx_ref, gt_ref, o_ref, zc0_ref, zc1_ref, za0_ref, za1_ref, yc_ref, ya_ref, wout_ref, wcp_ref, wap_ref,
             dout_ref, dyc_ref, dya_ref, dgc_ref, dat_ref, dproj_ref, dgt_ref, dzs, sems):
        i = pl.program_id(0)
        slot = lax.rem(i, 2)

        def slab_copy(step, s):
            return pltpu.make_async_copy(
                dzs.at[s], dproj_ref.at[pl.ds(pl.multiple_of(step * tm, tm), tm), pl.ds(O_ZC, 2 * D)], sems.at[s])

        @pl.when(i == 0)
        def _():
            dgt_ref[...] = jnp.zeros_like(dgt_ref)

        dxv = dx_ref[...]
        dgt_ref[...] += jnp.sum(dxv * o_ref[...].astype(F32), axis=0, keepdims=True)
        dout = (gt_ref[...] * dxv).astype(BF)
        dout_ref[...] = dout
        dmg = lax.dot_general(dout, wout_ref[...], NT, preferred_element_type=F32)
        sc = _sigmoid(jnp.concatenate([zc0_ref[...], zc1_ref[...]], axis=1).astype(F32))
        sa = _sigmoid(jnp.concatenate([za0_ref[...], za1_ref[...]], axis=1).astype(F32))
        dyc = (dmg * sc).astype(BF)
        dya = (dmg * sa).astype(BF)
        dyc_ref[...] = dyc
        dya_ref[...] = dya
        dzs[slot, :, 0:D] = (dmg * yc_ref[...].astype(F32) * (sc * (1.0 - sc))).astype(BF)
        dzs[slot, :, D:2 * D] = (dmg * ya_ref[...].astype(F32) * (sa * (1.0 - sa))).astype(BF)
        slab_copy(i, slot).start()
        dgc_ref[...] = lax.dot_general(dyc, wcp_ref[...], NT, preferred_element_type=F32).astype(BF)
        dat_ref[...] = lax.dot_general(dya, wap_ref[...], NT, preferred_element_type=F32).astype(BF)

        @pl.when(i > 0)
        def _():
            slab_copy(i - 1, 1 - slot).wait()

        @pl.when(i == nt - 1)
        def _():
            slab_copy(i, slot).wait()

    def zcol(c):
        return pl.BlockSpec((tm, 512), lambda i: (i, c // 512))

    wspec = pl.BlockSpec((D, D), _const2)
    rowspec = pl.BlockSpec((tm, D), _row)
    vec = pl.BlockSpec((1, D), _const2)
    return pl.pallas_call(
        body, name=name, grid=(nt,),
        in_specs=[rowspec, vec, rowspec, zcol(O_ZC), zcol(O_ZC + 512), zcol(O_ZA), zcol(O_ZA + 512),
                  rowspec, rowspec, wspec, wspec, wspec],
        out_specs=[rowspec] * 5 + [pl.BlockSpec(memory_space=pl.ANY), vec],
        out_shape=[jax.ShapeDtypeStruct((T, D), BF)] * 5 + [jax.ShapeDtypeStruct((T, NIN), BF),
                                                            jax.ShapeDtypeStruct((1, D), F32)],
        scratch_shapes=[pltpu.VMEM((2, tm, 2 * D), BF), pltpu.SemaphoreType.DMA((2,))],
        compiler_params=_cp(("arbitrary",)),
    )(dx2, gt, o, proj, proj, proj, proj, yc, ya, wout, wcp, wap)


def _conv_bwd(dgc, proj, convw, dproj, *, tm, name):
    T = dgc.shape[0]
    tm = _tile(T, tm)
    hb = tm // HALO
    nblk = T // HALO
    nt = T // tm

    def body(dgc_ref, ndgc_ref, bg_ref, nbg_ref, cg_ref, u_ref, hcg_ref, hu_ref, cw_ref, dproj_ref, dp_ref, dcw_ref):
        i = pl.program_id(0)

        @pl.when(i == 0)
        def _():
            dcw_ref[...] = jnp.zeros_like(dcw_ref)
        first = jnp.where(i == 0, 0.0, 1.0)
        last = jnp.where(i == nt - 1, 0.0, 1.0)
        cg = cg_ref[...].astype(F32)
        u = u_ref[...].astype(F32)
        bg = bg_ref[...].astype(F32)
        dg = dgc_ref[...].astype(F32)
        cu = cg * u
        hprev = first * (hcg_ref[...].astype(F32) * hu_ref[...].astype(F32))
        m1, m2 = _conv_shifts(cu, hprev, tm)
        w0, w1, w2 = cw_ref[0:1, :], cw_ref[1:2, :], cw_ref[2:3, :]
        cv = w0 * m2 + w1 * m1 + w2 * cu
        dcv = dg * bg
        nxt = last * (ndgc_ref[...].astype(F32) * nbg_ref[...].astype(F32))
        n0, n1 = nxt[0:1, :], nxt[1:2, :]
        row = lax.broadcasted_iota(jnp.int32, dcv.shape, 0)
        p1 = jnp.where(row == tm - 1, n0, pltpu.roll(dcv, tm - 1, 0))
        p2 = jnp.where(row == tm - 1, n1, jnp.where(row == tm - 2, n0, pltpu.roll(dcv, tm - 2, 0)))
        dcu = w2 * dcv + w1 * p1 + w0 * p2
        dp_ref[:, 0:D] = (dg * cv).astype(BF)
        dp_ref[:, D:2 * D] = (dcu * u).astype(BF)
        dp_ref[:, 2 * D:3 * D] = (dcu * cg).astype(BF)
        dcw_ref[0:1, :] += jnp.sum(dcv * m2, axis=0, keepdims=True)
        dcw_ref[1:2, :] += jnp.sum(dcv * m1, axis=0, keepdims=True)
        dcw_ref[2:3, :] += jnp.sum(dcv * cu, axis=0, keepdims=True)

    rowspec = pl.BlockSpec((tm, D), _row)
    cw = pl.BlockSpec((8, D), _const2)
    return pl.pallas_call(
        body, name=name, grid=(nt,),
        in_specs=[rowspec, _halo_next(hb, nblk), _col(tm, O_BG), _halo_next(hb, nblk, O_BG),
                  _col(tm, O_CG), _col(tm, O_U), _halo_prev(hb, O_CG), _halo_prev(hb, O_U), cw,
                  pl.BlockSpec(memory_space=pl.ANY)],
        out_specs=[pl.BlockSpec((tm, 3 * D), _row), cw],
        out_shape=[jax.ShapeDtypeStruct(dproj.shape, BF), jax.ShapeDtypeStruct((8, D), F32)],
        input_output_aliases={9: 0},
        compiler_params=_cp(("arbitrary",)),
    )(dgc, dgc, proj, proj, proj, proj, proj, proj, convw, dproj)


def _adam(w, g, m, v, *, tm, name):
    _, R, C = w.shape
    tm = _tile(R, tm)
    parts = g.ndim == 3
    c1 = 1.0 - ADAM_B1
    c2 = 1.0 - ADAM_B2
    bc1 = 1.0 - ADAM_B1 ** ADAM_STEP
    bc2 = 1.0 - ADAM_B2 ** ADAM_STEP

    def body(w_ref, g_ref, m_ref, v_ref, go_ref, d_ref, nm_ref, nv_ref):
        if parts:
            gv = g_ref[0].astype(F32)
            for s in range(1, N_DEV):
                gv = gv + g_ref[s].astype(F32)
        else:
            gv = g_ref[...]
        go_ref[0] = gv
        nm = ADAM_B1 * m_ref[0] + c1 * gv
        nv = ADAM_B2 * v_ref[0] + c2 * (gv * gv)
        nm_ref[0] = nm
        nv_ref[0] = nv
        d_ref[0] = -ADAM_LR * ((nm / bc1) / (jnp.sqrt(nv / bc2) + ADAM_EPS) + ADAM_WD * w_ref[0])

    spec = pl.BlockSpec((1, tm, C), lambda i: (0, i, 0))
    gspec = pl.BlockSpec((N_DEV, tm, C), lambda i: (0, i, 0)) if parts else pl.BlockSpec((tm, C), _row)
    return pl.pallas_call(
        body, name=name, grid=(R // tm,),
        in_specs=[spec, gspec, spec, spec], out_specs=[spec] * 4,
        out_shape=[jax.ShapeDtypeStruct((1, R, C), F32)] * 4,
        compiler_params=_cp(("parallel",)),
    )(w, g, m, v)


def _mods_part(c_all, w_ada, b_ada, *, name):
    C = w_ada.shape[1]

    def body(c_ref, w_ref, b_ref, o_ref):
        cv = c_ref[...]
        ca = cv * jax.nn.sigmoid(cv)
        o_ref[...] = jnp.dot(ca, w_ref[...], preferred_element_type=F32,
                             precision=lax.Precision.HIGHEST) + b_ref[...]

    return pl.pallas_call(
        body, name=name,
        out_shape=jax.ShapeDtypeStruct((N_DEV, C), F32),
        compiler_params=_cp(),
    )(c_all, w_ada, b_ada)


def _wada_grad(c_all_t, gm, *, name):
    C = gm.shape[1]

    def body(c_ref, g_ref, o_ref):
        cv = c_ref[...]
        ca = cv * jax.nn.sigmoid(cv)
        acc = ca[:, 0:1] * g_ref[0:1, :]
        for b in range(1, N_DEV):
            acc = acc + ca[:, b:b + 1] * g_ref[b:b + 1, :]
        o_ref[...] = acc

    return pl.pallas_call(
        body, name=name,
        out_shape=jax.ShapeDtypeStruct((D, C), F32),
        compiler_params=_cp(),
    )(c_all_t, gm)


def _peer(x, y, c, d):
    px = lax.rem(x + ((d >> 2) & 1), 2)
    py = lax.rem(y + ((d >> 1) & 1), 2)
    pc = lax.rem(c + (d & 1), 2)
    return (px, py, pc), 4 * px + 2 * py + pc


def _exchange(xs, *, scatter, name):
    n = len(xs)
    nsem = n * (N_DEV - 1)

    def body(*refs):
        ins, outs = refs[:n], refs[n:2 * n]
        token, send_sems, recv_sems, local_sems = refs[2 * n:]
        x, y, c = lax.axis_index("x"), lax.axis_index("y"), lax.axis_index("c")
        me = 4 * x + 2 * y + c
        token[...] = jnp.zeros_like(token)

        def src(t, idx):
            return ins[t].at[idx] if scatter else ins[t]

        local = [pltpu.make_async_copy(src(t, me), outs[t].at[me], local_sems.at[t]) for t in range(n)]
        for cp in local:
            cp.start()
        remote = []
        for t in range(n):
            for d in range(1, N_DEV):
                peer, pidx = _peer(x, y, c, d)
                k = t * (N_DEV - 1) + d - 1
                send = pltpu.make_async_remote_copy(src_ref=src(t, pidx), dst_ref=outs[t].at[me],
                                                    send_sem=send_sems.at[k], recv_sem=recv_sems.at[k],
                                                    device_id=peer, device_id_type=MESH)
                recv = pltpu.make_async_remote_copy(src_ref=src(t, pidx), dst_ref=outs[t].at[pidx],
                                                    send_sem=send_sems.at[k], recv_sem=recv_sems.at[k],
                                                    device_id=peer, device_id_type=MESH)
                send.start()
                remote.append((send, recv))
        for cp in local:
            cp.wait()
        for send, recv in remote:
            send.wait_send()
            recv.wait_recv()

    anyspec = pl.BlockSpec(memory_space=pl.ANY)
    out_shape = [jax.ShapeDtypeStruct(a.shape if scatter else (N_DEV,) + a.shape, a.dtype) for a in xs]
    out_shape.append(jax.ShapeDtypeStruct((8, 128), F32))
    return pl.pallas_call(
        body, name=name,
        in_specs=[anyspec] * n, out_specs=[anyspec] * n + [pl.BlockSpec(memory_space=pltpu.VMEM)],
        out_shape=out_shape,
        scratch_shapes=[pltpu.SemaphoreType.DMA((nsem,)), pltpu.SemaphoreType.DMA((nsem,)),
                        pltpu.SemaphoreType.DMA((n,))],
    )(*xs)


def _sum8(parts, *, name):
    _, R, C = parts.shape

    def body(p_ref, o_ref):
        acc = p_ref[0]
        for s in range(1, N_DEV):
            acc = acc + p_ref[s]
        o_ref[...] = acc

    return pl.pallas_call(body, name=name, out_shape=jax.ShapeDtypeStruct((R, C), F32),
                          compiler_params=_cp())(parts)


HBM_SPEC = pl.BlockSpec(memory_space=pltpu.HBM)
SEM_SPEC = pl.BlockSpec(memory_space=pltpu.SEMAPHORE)
N_PEER = N_DEV - 1


def _split_copies(src_refs, land_refs, send_sems, recv_sems, scatter):
    x, y, c = lax.axis_index("x"), lax.axis_index("y"), lax.axis_index("c")
    me = 4 * x + 2 * y + c
    pairs = []
    for j, (src, land) in enumerate(zip(src_refs, land_refs)):
        for d in range(1, N_DEV):
            peer, pidx = _peer(x, y, c, d)
            k = j * N_PEER + d - 1
            s = src.at[pidx] if scatter else src
            send = pltpu.make_async_remote_copy(src_ref=s, dst_ref=land.at[me], send_sem=send_sems.at[k],
                                                recv_sem=recv_sems.at[k], device_id=peer, device_id_type=MESH)
            recv = pltpu.make_async_remote_copy(src_ref=s, dst_ref=land.at[pidx], send_sem=send_sems.at[k],
                                                recv_sem=recv_sems.at[k], device_id=peer, device_id_type=MESH)
            pairs.append((send, recv))
    return pairs


def _own_slot(block, me):
    land = lax.empty((N_DEV,) + block.shape, block.dtype)
    return lax.dynamic_update_slice(land, block[None], (me, 0, 0))


def _split_start(srcs, lands, groups, *, scatter, name):
    n, ng = len(srcs), len(groups)

    def body(*refs):
        src_refs, land_refs = refs[:n], refs[n:2 * n]
        sems = refs[2 * n:2 * n + 2 * ng]
        token = refs[-1]
        for gi, g in enumerate(groups):
            pairs = _split_copies([src_refs[t] for t in g], [land_refs[t] for t in g], sems[2 * gi],
                                  sems[2 * gi + 1], scatter)
            for send, _ in pairs:
                send.start()
        token[...] = jnp.zeros_like(token)

    sem_shapes = []
    for g in groups:
        sem_shapes += [pltpu.SemaphoreType.DMA((len(g) * N_PEER,))] * 2
    thru = [pltpu.HBM(a.shape, a.dtype) for a in list(srcs) + list(lands)]
    outs = pl.pallas_call(
        body, name=name,
        out_shape=tuple(sem_shapes + thru + [jax.ShapeDtypeStruct((8, 128), F32)]),
        in_specs=[HBM_SPEC] * (2 * n),
        out_specs=tuple([SEM_SPEC] * (2 * ng) + [HBM_SPEC] * (2 * n) + [pl.BlockSpec(memory_space=pltpu.VMEM)]),
        input_output_aliases={i: 2 * ng + i for i in range(2 * n)},
        compiler_params=pltpu.CompilerParams(has_side_effects=pltpu.SideEffectType.DATAFLOW_SIDE_EFFECTING),
    )(*[pltpu.with_memory_space_constraint(a, pltpu.HBM) for a in list(srcs) + list(lands)])
    sems = [(outs[2 * gi], outs[2 * gi + 1]) for gi in range(ng)]
    return sems, outs[2 * ng:2 * ng + n], outs[2 * ng + n:2 * ng + 2 * n], outs[-1]


def _behind(v, token):
    if token is None:
        return v
    return v + token[0, 0].astype(v.dtype)


def _split_wait(srcs, lands, sems, after, *, scatter, name):
    m = len(srcs)

    def body(*refs):
        src_refs, land_refs = refs[:m], refs[m:2 * m]
        send_sems, recv_sems = refs[2 * m], refs[2 * m + 1]
        for send, recv in _split_copies(src_refs, land_refs, send_sems, recv_sems, scatter):
            send.wait_send()
            recv.wait_recv()

    outs = pl.pallas_call(
        body, name=name,
        out_shape=tuple(pltpu.HBM(a.shape, a.dtype) for a in list(srcs) + list(lands)),
        in_specs=[HBM_SPEC] * (2 * m) + [SEM_SPEC, SEM_SPEC, pl.BlockSpec(memory_space=pl.ANY)],
        out_specs=tuple([HBM_SPEC] * (2 * m)),
        input_output_aliases={i: i for i in range(2 * m)},
        compiler_params=pltpu.CompilerParams(has_side_effects=pltpu.SideEffectType.DATAFLOW_SIDE_EFFECTING),
    )(*srcs, *lands, sems[0], sems[1], after)
    return outs[m:]


TL_FIRST = (1, 2, 4, 6)
TL_ICI = (2, 4, 6)
EFFECT = pltpu.SideEffectType.DATAFLOW_SIDE_EFFECTING


def _tl_first(src_refs, land_refs, send_sems, recv_sems):
    x, y, c = lax.axis_index("x"), lax.axis_index("y"), lax.axis_index("c")
    me = 4 * x + 2 * y + c
    out = []
    for j, (src, land) in enumerate(zip(src_refs, land_refs)):
        for i, d in enumerate(TL_FIRST):
            peer, pidx = _peer(x, y, c, d)
            k = len(TL_FIRST) * j + i
            send = pltpu.make_async_remote_copy(src_ref=src, dst_ref=land.at[me], send_sem=send_sems.at[k],
                                                recv_sem=recv_sems.at[k], device_id=peer, device_id_type=MESH)
            recv = pltpu.make_async_remote_copy(src_ref=src, dst_ref=land.at[pidx], send_sem=send_sems.at[k],
                                                recv_sem=recv_sems.at[k], device_id=peer, device_id_type=MESH)
            out.append((d, send, recv))
    return out


def _tl_second(land_refs, send_sems, recv_sems):
    x, y, c = lax.axis_index("x"), lax.axis_index("y"), lax.axis_index("c")
    sibling, _ = _peer(x, y, c, 1)
    out = []
    for j, land in enumerate(land_refs):
        for i, d in enumerate(TL_ICI):
            _, mine = _peer(x, y, c, d)
            _, theirs = _peer(x, y, c, d + 1)
            k = len(TL_ICI) * j + i
            send = pltpu.make_async_remote_copy(src_ref=land.at[mine], dst_ref=land.at[mine], send_sem=send_sems.at[k],
                                                recv_sem=recv_sems.at[k], device_id=sibling, device_id_type=MESH)
            recv = pltpu.make_async_remote_copy(src_ref=land.at[mine], dst_ref=land.at[theirs],
                                                send_sem=send_sems.at[k], recv_sem=recv_sems.at[k],
                                                device_id=sibling, device_id_type=MESH)
            out.append((send, recv))
    return out


def _tl_start(srcs, lands, groups, *, name):
    n, ng = len(srcs), len(groups)

    def body(*refs):
        src_refs, land_refs = refs[:n], refs[n:2 * n]
        sems = refs[2 * n:2 * n + 2 * ng]
        for gi, g in enumerate(groups):
            for _, send, _ in _tl_first([src_refs[t] for t in g], [land_refs[t] for t in g], sems[2 * gi],
                                        sems[2 * gi + 1]):
                send.start()
        refs[-1][...] = jnp.zeros_like(refs[-1])

    sem_shapes = []
    for g in groups:
        sem_shapes += [pltpu.SemaphoreType.DMA((len(g) * len(TL_FIRST),))] * 2
    thru = [pltpu.HBM(a.shape, a.dtype) for a in list(srcs) + list(lands)]
    outs = pl.pallas_call(
        body, name=name,
        out_shape=tuple(sem_shapes + thru + [jax.ShapeDtypeStruct((8, 128), F32)]),
        in_specs=[HBM_SPEC] * (2 * n),
        out_specs=tuple([SEM_SPEC] * (2 * ng) + [HBM_SPEC] * (2 * n) + [pl.BlockSpec(memory_space=pltpu.VMEM)]),
        input_output_aliases={i: 2 * ng + i for i in range(2 * n)},
        compiler_params=pltpu.CompilerParams(has_side_effects=EFFECT),
    )(*[pltpu.with_memory_space_constraint(a, pltpu.HBM) for a in list(srcs) + list(lands)])
    sems = [(outs[2 * gi], outs[2 * gi + 1]) for gi in range(ng)]
    return sems, outs[2 * ng:2 * ng + n], outs[2 * ng + n:2 * ng + 2 * n], outs[-1]


def _tl_forward(srcs, lands, sems1, after, *, name):
    m = len(srcs)

    def body(*refs):
        src_refs, land_refs = refs[:m], refs[m:2 * m]
        send1, recv1 = refs[2 * m], refs[2 * m + 1]
        send2, recv2 = refs[2 * m + 3], refs[2 * m + 4]
        for d, _, recv in _tl_first(src_refs, land_refs, send1, recv1):
            if d in TL_ICI:
                recv.wait_recv()
        for send, _ in _tl_second(land_refs, send2, recv2):
            send.start()

    sem = pltpu.SemaphoreType.DMA((m * len(TL_ICI),))
    outs = pl.pallas_call(
        body, name=name,
        out_shape=tuple([sem, sem] + [pltpu.HBM(a.shape, a.dtype) for a in list(srcs) + list(lands)]),
        in_specs=[HBM_SPEC] * (2 * m) + [SEM_SPEC, SEM_SPEC, pl.BlockSpec(memory_space=pl.ANY)],
        out_specs=tuple([SEM_SPEC, SEM_SPEC] + [HBM_SPEC] * (2 * m)),
        input_output_aliases={i: 2 + i for i in range(2 * m)},
        compiler_params=pltpu.CompilerParams(has_side_effects=EFFECT),
    )(*srcs, *lands, sems1[0], sems1[1], after)
    return (outs[0], outs[1]), outs[2:2 + m], outs[2 + m:2 + 2 * m]


def _tl_wait(srcs, lands, sems1, sems2, after, *, name):
    m = len(srcs)

    def body(*refs):
        src_refs, land_refs = refs[:m], refs[m:2 * m]
        send1, recv1, send2, recv2 = refs[2 * m:2 * m + 4]
        for d, send, recv in _tl_first(src_refs, land_refs, send1, recv1):
            send.wait_send()
            if d not in TL_ICI:
                recv.wait_recv()
        for send, recv in _tl_second(land_refs, send2, recv2):
            send.wait_send()
            recv.wait_recv()

    outs = pl.pallas_call(
        body, name=name,
        out_shape=tuple(pltpu.HBM(a.shape, a.dtype) for a in list(srcs) + list(lands)),
        in_specs=[HBM_SPEC] * (2 * m) + [SEM_SPEC] * 4 + [pl.BlockSpec(memory_space=pl.ANY)],
        out_specs=tuple([HBM_SPEC] * (2 * m)),
        input_output_aliases={i: i for i in range(2 * m)},
        compiler_params=pltpu.CompilerParams(has_side_effects=EFFECT),
    )(*srcs, *lands, sems1[0], sems1[1], sems2[0], sems2[1], after)
    return outs[m:]


TM_PROJ = 512
TN_PROJ = 512
TM_ROW = 512
TM_NN = 512
TK_TN = 2048
TM_ADAM = 416
TN_FFN = F // 2
TN_IN = NIN // 4


def _tn(a, b, name, tn, token=None):
    if a.ndim == 2:
        a = a[None]
    return _tn_matmul(a, b, token, tn=tn, tk=TK_TN, name=name)


def _local_step(x, tgt, mods, g1, gm, g2, gf, convw8, sinks, w_get, g_put):
    T = x.shape[0]
    sh1, sc1, gt1, sh2, sc2, gt2, sh3, sc3, gt3 = [mods[i:i + 1] for i in range(N_MOD)]
    cos, sin = _rope_tables(T)
    behind = _behind

    w = dict(w_get("gu1", mods))
    h1, ab1 = _norm_proj(x, g1, sc1, sh1, w["gu1"], tm=TM_PROJ, tn=TN_PROJ, name="ffn1_up")
    w.update(w_get("d1", ab1))
    x1, y1 = _ffn_down_fwd(ab1, w["d1"], x, gt1, tm=TM_ROW, name="ffn1_down")
    w.update(w_get("mix", x1))
    h2, proj = _norm_proj(x1, gm, sc2, sh2, w["win"], tm=TM_PROJ, tn=TN_PROJ, name="mix_in")
    qs, kr = _attn_prep(proj, cos, sin, name="attn_prep")
    bias = _attn_bias()
    attn, lse = _attn_fwd(qs, kr, proj, bias, sinks, name="attn_fwd")
    x2, gc, yc, ya, mg, o = _mixer_mid_fwd(proj, attn, w["cp"], w["ap"], w["out"], convw8, x1, gt2,
                                           tm=TM_ROW, name="mix_mid")
    w.update(w_get("ffn2", x2))
    h3, ab2, y2, dx3, lsum, dgf = _ffn_fwd(x2, g2, sc3, sh3, gt3, w["gu2"], w["d2"], (tgt, gf), tm=TM_ROW,
                                           name="ffn2_final")

    dab2, dgt3, g_d2 = _ffn_down_bwd_dw(dx3, y2, gt3, ab2, w["d2"], tm=TM_ROW, name="ffn2_down_bwd")
    dx2, dsh3, dsc3, dg2 = _nn_bwd_norm(dab2, w["gu2"], x2, g2, sc3, dx3, tm=TM_NN, name="ffn2_up_bwd")
    g_gu2 = _tn(dab2, h3, "ffn2_up_dw", TN_FFN)
    tok = g_put(dict(gu2=g_gu2, d2=g_d2))

    dout, dyc, dya, dgc, dat, dproj, dgt2 = _mixer_mid_bwd(dx2, behind(gt2, tok), o, proj, yc, ya, w["out"], w["cp"],
                                                           w["ap"], tm=TM_ROW, name="mix_mid_bwd")
    g_out = _tn(mg, dout, "mix_out_dw", D)
    g_cp = _tn(gc, dyc, "mix_cp_dw", D)
    g_ap = _tn(attn, dya, "mix_ap_dw", D)
    dproj, dkc, dkp, dvc, dvp, dsink = _attn_bwd(qs, kr, proj, bias, sinks, lse, attn, dat, cos, sin, dproj,
                                                 name="attn_bwd")
    dproj = _dkv_combine(dkc, dkp, dvc, dvp, dproj, name="attn_dkv")
    dproj, dcw = _conv_bwd(dgc, proj, convw8, dproj, tm=TM_ROW, name="conv_bwd")
    g_in = _tn(dproj, h2, "mix_in_dw", TN_IN)
    tok = g_put(dict(win=g_in, cp=g_cp, ap=g_ap, out=g_out))
    dx1, dsh2, dsc2, dgm = _nn_bwd_norm(dproj[None], w["win"], x1, gm, behind(sc2, tok), dx2, tm=TM_NN,
                                        name="mix_in_bwd")

    dab1, dgt1, g_d1 = _ffn_down_bwd_dw(dx1, y1, gt1, ab1, w["d1"], tm=TM_ROW, name="ffn1_down_bwd")
    tok = g_put(dict(d1=g_d1))
    g_gu1 = _tn(dab1, h1, "ffn1_up_dw", TN_FFN, tok)
    tok = g_put(dict(gu1=g_gu1))
    dx0, dsh1, dsc1, dg1 = _nn_bwd_norm(dab1, w["gu1"], x, g1, behind(sc1, tok), dx1, tm=TM_NN,
                                        name="ffn1_up_bwd")

    small = dict(mods=jnp.concatenate([dsh1, dsc1, dgt1, dsh2, dsc2, dgt2, dsh3, dsc3, dgt3], axis=0),
                 g1=dg1, gm=dgm, g2=dg2, gf=dgf, convw=dcw[0:3], sinks=dsink[:, 0:N_HEADS])
    return lsum, dx0, small


BIG = ("gu1", "d1", "win", "cp", "ap", "out", "gu2", "d2")
TRANSPOSED = ("gu1", "win", "gu2")
SMALL_ROWS = 24
R_MODS, R_G1, R_GM, R_G2, R_GF, R_CONV, R_SINK = 0, 9, 10, 11, 12, 13, 16


def _pad_to(a, rows, cols):
    return jnp.pad(a, ((0, rows - a.shape[0]), (0, cols - a.shape[1])))


def _pack_small(b_ada, g1, gm, g2, gf, conv, sinks):
    rows = [b_ada.reshape(N_MOD, D), g1.reshape(1, D), gm.reshape(1, D), g2.reshape(1, D), gf.reshape(1, D),
            _pad_to(conv.reshape(3, -1), 3, D), _pad_to(sinks.reshape(1, N_HEADS), 1, D)]
    return _pad_to(jnp.concatenate(rows, axis=0), SMALL_ROWS, D)


def _unpack_small(p, conv_cols):
    return dict(b_ada=p[R_MODS:R_MODS + N_MOD].reshape(1, N_MOD * D), g_ffn1=p[R_G1:R_G1 + 1],
                g_mix=p[R_GM:R_GM + 1], g_ffn2=p[R_G2:R_G2 + 1], g_final=p[R_GF],
                conv_w=p[R_CONV:R_CONV + 3, 0:conv_cols][None], sinks=p[R_SINK:R_SINK + 1, 0:N_HEADS])


def kernel(x, c, w_ada, b_ada, g_ffn1, w1_gu, w1_down, g_mix, w_in, conv_w, w_conv_proj, w_attn_proj, sinks, w_out, g_ffn2, w2_gu, w2_down, g_final, loss_target, m_w_ada, m_b_ada, m_g_ffn1, m_w1_gu, m_w1_down, m_g_mix, m_w_in, m_conv_w, m_w_conv_proj, m_w_attn_proj, m_sinks, m_w_out, m_g_ffn2, m_w2_gu, m_w2_down, m_g_final, v_w_ada, v_b_ada, v_g_ffn1, v_w1_gu, v_w1_down, v_g_mix, v_w_in, v_conv_w, v_w_conv_proj, v_w_attn_proj, v_sinks, v_w_out, v_g_ffn2, v_w2_gu, v_w2_down, v_g_final):
    me = 4 * lax.axis_index("x") + 2 * lax.axis_index("y") + lax.axis_index("c")
    ada_cols = w_ada.shape[2]
    conv_cols = conv_w.shape[2]

    native = dict(gu1=w1_gu[0], d1=w1_down[0], win=w_in[0], cp=w_conv_proj[0], ap=w_attn_proj[0], out=w_out[0],
                  gu2=w2_gu[0], d2=w2_down[0])

    def shard(n, token):
        a = _behind(native[n], token)
        return (a.T if n in TRANSPOSED else a).astype(BF)

    groups = dict(gu1=("gu1",), d1=("d1",), mix=("win", "cp", "ap", "out"), ffn2=("gu2", "d2"))
    in_flight = {}
    first = [shard("gu1", None)]
    sems, srcs, lands, token = _tl_start(first, [_own_slot(s, me) for s in first], [[0]],
                                         name="gather_weights_start_gu1")
    in_flight["gu1"] = [sems[0], srcs, lands, None]

    c_all, conv_all, _ = _exchange([_behind(c, token), _pad_to(conv_w[0], 8, conv_cols)], scatter=False,
                                   name="gather_cond")
    c_all = c_all.reshape(N_DEV, D)
    conv_full = conv_all[:, 0:3, :].transpose(1, 0, 2).reshape(3, D)

    b_cols = lax.dynamic_slice(b_ada, (0, me * ada_cols), (1, ada_cols))
    mods_cols = _mods_part(c_all, w_ada[0], b_cols, name="ada_mods")
    mods_all, mods_token = _exchange([mods_cols], scatter=False, name="gather_mods")
    mods = lax.dynamic_index_in_dim(mods_all, me, axis=1, keepdims=False).reshape(N_MOD, D)

    rest = [n for n in BIG if n != "gu1"]
    shards = [shard(n, mods_token) for n in rest]
    rest_groups = [[rest.index(n) for n in names] for g, names in groups.items() if g != "gu1"]
    sems, srcs, lands, rest_token = _tl_start(shards, [_own_slot(s, me) for s in shards], rest_groups,
                                              name="gather_weights_start_rest")
    for (g, names), gsems, idx in zip([kv for kv in groups.items() if kv[0] != "gu1"], sems, rest_groups):
        in_flight[g] = [gsems, [srcs[t] for t in idx], [lands[t] for t in idx], None]

    def forward(group, after):
        sems1, gsrcs, glands, _ = in_flight[group]
        sems2, gsrcs, glands = _tl_forward(gsrcs, glands, sems1, after, name="gather_weights_forward_" + group)
        in_flight[group] = [sems1, gsrcs, glands, sems2]

    forward_early = dict(d1="mix", mix="ffn2")

    def w_get(group, after):
        if group == "gu1":
            after = rest_token
        if in_flight[group][3] is None:
            forward(group, after)
        sems1, gsrcs, glands, sems2 = in_flight[group]
        landed = _tl_wait(gsrcs, glands, sems1, sems2, after, name="gather_weights_wait_" + group)
        if group in forward_early:
            forward(forward_early[group], landed[0])
        return {n: a.reshape(-1, D) for n, a in zip(groups[group], landed)}

    pending = []

    def g_put(gs):
        names = tuple(gs)
        srcs = [gs[n].reshape(N_DEV, -1, D) for n in names]
        lands = [_own_slot(lax.dynamic_index_in_dim(s, me, axis=0, keepdims=False), me) for s in srcs]
        sems, srcs, lands, tok = _split_start(srcs, lands, [list(range(len(names)))], scatter=True,
                                              name="scatter_grads_start_" + names[0])
        pending.append((names, sems[0], srcs, lands))
        return tok

    lsum, grad_x, small = _local_step(x[0], loss_target[0], mods, g_ffn1, g_mix, g_ffn2, g_final[None],
                                      _pad_to(conv_full, 8, D), sinks[0], w_get, g_put)
    loss = lax.psum((0.5 / D) * jnp.sum(lsum), ("x", "y", "c"))

    packed = _pack_small(small["mods"], small["g1"], small["gm"], small["g2"], small["gf"], small["convw"],
                         small["sinks"])
    packed_all, _ = _exchange([packed], scatter=False, name="gather_small")
    gsmall = _sum8(packed_all, name="sum_small")

    w_of = dict(ada=w_ada, gu1=w1_gu, d1=w1_down, win=w_in, cp=w_conv_proj, ap=w_attn_proj, out=w_out, gu2=w2_gu,
                d2=w2_down)
    m_of = dict(ada=m_w_ada, gu1=m_w1_gu, d1=m_w1_down, win=m_w_in, cp=m_w_conv_proj, ap=m_w_attn_proj, out=m_w_out,
                gu2=m_w2_gu, d2=m_w2_down)
    v_of = dict(ada=v_w_ada, gu1=v_w1_gu, d1=v_w1_down, win=v_w_in, cp=v_w_conv_proj, ap=v_w_attn_proj, out=v_w_out,
                gu2=v_w2_gu, d2=v_w2_down)
    upd = {}
    after = gsmall
    for names, sems, srcs, lands in pending:
        parts = _split_wait(srcs, lands, sems, after, scatter=True, name="scatter_grads_wait_" + names[0])
        for n, p in zip(names, parts):
            if n in TRANSPOSED:
                res = _adam(jnp.swapaxes(w_of[n], 1, 2), p, jnp.swapaxes(m_of[n], 1, 2), jnp.swapaxes(v_of[n], 1, 2),
                            tm=TM_ADAM, name="adam_" + n)
                upd[n] = [jnp.swapaxes(t, 1, 2) for t in res]
            else:
                upd[n] = _adam(w_of[n], p, m_of[n], v_of[n], tm=TM_ADAM, name="adam_" + n)
        after = upd[names[-1]][1]

    gm_cols = lax.dynamic_slice(packed_all[:, R_MODS:R_MODS + N_MOD, :].reshape(N_DEV, N_MOD * D),
                                (0, me * ada_cols), (N_DEV, ada_cols))
    upd["ada"] = _adam(w_ada, _wada_grad(c_all.T, gm_cols, name="ada_dw"), m_w_ada, v_w_ada, tm=256, name="adam_ada")
    conv_g = lax.dynamic_slice(gsmall[R_CONV:R_CONV + 3], (0, me * conv_cols), (3, conv_cols))
    gsmall_own = gsmall.at[R_CONV:R_CONV + 3].set(_pad_to(conv_g, 3, D))
    small_upd = _adam(_pack_small(b_ada, g_ffn1, g_mix, g_ffn2, g_final, conv_w, sinks)[None], gsmall_own,
                      _pack_small(m_b_ada, m_g_ffn1, m_g_mix, m_g_ffn2, m_g_final, m_conv_w, m_sinks)[None],
                      _pack_small(v_b_ada, v_g_ffn1, v_g_mix, v_g_ffn2, v_g_final, v_conv_w, v_sinks)[None],
                      tm=SMALL_ROWS, name="adam_small")
    small_out = [_unpack_small(p[0], conv_cols) for p in small_upd]

    big_name = dict(w_ada="ada", w1_gu="gu1", w1_down="d1", w_in="win", w_conv_proj="cp", w_attn_proj="ap",
                    w_out="out", w2_gu="gu2", w2_down="d2")
    order = ("w_ada", "b_ada", "g_ffn1", "w1_gu", "w1_down", "g_mix", "w_in", "conv_w", "w_conv_proj", "w_attn_proj",
             "sinks", "w_out", "g_ffn2", "w2_gu", "w2_down", "g_final")
    outs = [loss, grad_x[None]]
    for kind in range(4):
        for n in order:
            outs.append(upd[big_name[n]][kind] if n in big_name else small_out[kind][n])
    return tuple(outs)
```

```python
import jax
import jax.numpy as jnp
from jax import lax
from jax.experimental import pallas as pl
from jax.experimental.pallas import tpu as pltpu

D = 1024
F = 2816
NIN = 6656
N_HEADS = 16
N_KV = 4
HEAD_DIM = 64
BLK = 128
N_MOD = 9
N_DEV = 8
EPS = 1e-6
NEG_INF = -1e30
ROPE_THETA = 10000.0
O_BG, O_CG, O_U, O_Q, O_K, O_V, O_ZC, O_ZA = 0, 1024, 2048, 3072, 4096, 4352, 4608, 5632

ADAM_LR = 0.001
ADAM_B1 = 0.9
ADAM_B2 = 0.999
ADAM_EPS = 1e-08
ADAM_WD = 0.01
ADAM_STEP = 10

BF = jnp.bfloat16
F32 = jnp.float32
VMEM_LIMIT = 56 * 1024 * 1024
MXU_N = 256
MESH = pl.DeviceIdType.MESH

NT = (((1,), (1,)), ((), ()))
TN = (((0,), (0,)), ((), ()))


def _cp(sem=None):
    return pltpu.CompilerParams(dimension_semantics=sem, vmem_limit_bytes=VMEM_LIMIT)


def _tile(n, pref):
    if n <= pref:
        return n
    for t in range(pref - pref % 16, 15, -16):
        if n % t == 0:
            return t
    raise ValueError((n, pref))


def _sigmoid(v):
    return 0.5 * jnp.tanh(0.5 * v) + 0.5


def _row(i):
    return (i, 0)


def _const2(*_):
    return (0, 0)


def _resident(shape):
    return pl.BlockSpec(shape, lambda *_: (0,) * len(shape), pipeline_mode=pl.Buffered(1))


def _norm_proj(x, g, sc, sh, wt, *, tm, tn, name):
    T, N = x.shape[0], wt.shape[0]
    tm = _tile(T, tm)

    def body(x_ref, g_ref, sc_ref, sh_ref, w_ref, h_ref, o_ref):
        xv = x_ref[...]
        r = lax.rsqrt(jnp.mean(xv * xv, axis=-1, keepdims=True) + EPS)
        hb = ((xv * r) * g_ref[...] * (1.0 + sc_ref[...]) + sh_ref[...]).astype(BF)
        h_ref[...] = hb
        for c0 in range(0, N, tn):
            cols = pl.ds(c0, tn)
            o_ref[:, cols] = lax.dot_general(hb, w_ref[cols, :], NT, preferred_element_type=F32).astype(BF)

    vec = pl.BlockSpec((1, D), _const2)
    return pl.pallas_call(
        body, name=name, grid=(T // tm,),
        in_specs=[pl.BlockSpec((tm, D), _row), vec, vec, vec, _resident((N, D))],
        out_specs=[pl.BlockSpec((tm, D), _row), pl.BlockSpec((tm, N), _row)],
        out_shape=[jax.ShapeDtypeStruct((T, D), BF), jax.ShapeDtypeStruct((T, N), BF)],
        compiler_params=_cp(("parallel",)),
    )(x, g, sc, sh, wt)


def _ffn_down_fwd(ab, wd, x, gt, *, tm, name):
    T = x.shape[0]
    tm = _tile(T, tm)

    def body(a_ref, b_ref, wd_ref, x_ref, gt_ref, xo_ref, y_ref):
        y = None
        for c0 in range(0, F, MXU_N):
            cols = pl.ds(c0, MXU_N)
            a = a_ref[:, cols].astype(F32)
            act = (a * _sigmoid(a) * b_ref[:, cols].astype(F32)).astype(BF)
            part = jnp.dot(act, wd_ref[cols, :], preferred_element_type=F32)
            y = part if y is None else y + part
        y_ref[...] = y.astype(BF)
        xo_ref[...] = x_ref[...] + (0.5 * gt_ref[...]) * y

    return pl.pallas_call(
        body, name=name, grid=(T // tm,),
        in_specs=[pl.BlockSpec((tm, F), lambda i: (i, 0)), pl.BlockSpec((tm, F), lambda i: (i, 1)),
                  _resident((F, D)), pl.BlockSpec((tm, D), _row), pl.BlockSpec((1, D), _const2)],
        out_specs=[pl.BlockSpec((tm, D), _row), pl.BlockSpec((tm, D), _row)],
        out_shape=[jax.ShapeDtypeStruct((T, D), F32), jax.ShapeDtypeStruct((T, D), BF)],
        compiler_params=_cp(("parallel",)),
    )(ab, ab, wd, x, gt)


def _ffn_fwd(x, g, sc, sh, gt, wgu, wd, final, *, tm, name):
    T = x.shape[0]
    tm = _tile(T, tm)
    last = final is not None

    def body(x_ref, g_ref, sc_ref, sh_ref, gt_ref, wgu_ref, wd_ref, *rest):
        if last:
            t_ref, gf_ref, h_ref, ab_ref, y_ref, dx_ref, ls_ref, dgf_ref = rest
        else:
            h_ref, ab_ref, y_ref, xo_ref = rest
        xv = x_ref[...]
        r = lax.rsqrt(jnp.mean(xv * xv, axis=-1, keepdims=True) + EPS)
        hb = ((xv * r) * g_ref[...] * (1.0 + sc_ref[...]) + sh_ref[...]).astype(BF)
        h_ref[...] = hb
        y = None
        for c0 in range(0, F, MXU_N):
            a = lax.dot_general(hb, wgu_ref[pl.ds(c0, MXU_N), :], NT, preferred_element_type=F32)
            b = lax.dot_general(hb, wgu_ref[pl.ds(F + c0, MXU_N), :], NT, preferred_element_type=F32)
            ab = a.astype(BF)
            bb = b.astype(BF)
            ab_ref[:, pl.ds(c0, MXU_N)] = ab
            ab_ref[:, pl.ds(F + c0, MXU_N)] = bb
            a = ab.astype(F32)
            act = (a * _sigmoid(a) * bb.astype(F32)).astype(BF)
            part = jnp.dot(act, wd_ref[pl.ds(c0, MXU_N), :], preferred_element_type=F32)
            y = part if y is None else y + part
        y_ref[...] = y.astype(BF)
        xo = xv + (0.5 * gt_ref[...]) * y
        if not last:
            xo_ref[...] = xo
            return

        @pl.when(pl.program_id(0) == 0)
        def _():
            ls_ref[...] = jnp.zeros_like(ls_ref)
            dgf_ref[...] = jnp.zeros_like(dgf_ref)
        gv = gf_ref[...]
        r = lax.rsqrt(jnp.mean(xo * xo, axis=-1, keepdims=True) + EPS)
        xh = xo * r
        e = xh * gv - t_ref[...]
        ls_ref[...] += jnp.sum(e * e, axis=0, keepdims=True)
        dy = e * (1.0 / D)
        dgf_ref[...] += jnp.sum(dy * xh, axis=0, keepdims=True)
        dxh = dy * gv
        dx_ref[...] = r * (dxh - xh * jnp.mean(dxh * xh, axis=-1, keepdims=True))

    vec = pl.BlockSpec((1, D), _const2)
    rowspec = pl.BlockSpec((tm, D), _row)
    in_specs = [rowspec, vec, vec, vec, vec, _resident((2 * F, D)), _resident((F, D))]
    out_specs = [rowspec, pl.BlockSpec((tm, 2 * F), _row), rowspec, rowspec]
    out_shape = [jax.ShapeDtypeStruct((T, D), BF), jax.ShapeDtypeStruct((T, 2 * F), BF),
                 jax.ShapeDtypeStruct((T, D), BF), jax.ShapeDtypeStruct((T, D), F32)]
    args = [x, g, sc, sh, gt, wgu, wd]
    if last:
        in_specs += [rowspec, vec]
        out_specs += [vec, vec]
        out_shape += [jax.ShapeDtypeStruct((1, D), F32)] * 2
        args += list(final)
    return pl.pallas_call(
        body, name=name, grid=(T // tm,),
        in_specs=in_specs, out_specs=out_specs, out_shape=out_shape,
        compiler_params=_cp(("arbitrary",) if last else ("parallel",)),
    )(*args)


def _ffn_down_bwd_dw(dxo, y, gt, ab, wd, *, tm, name):
    T = dxo.shape[0]
    tm = _tile(T, tm)
    nt = T // tm
    hw = F // 2
    chunks = [(c0, min(MXU_N, hw - c0)) for c0 in range(0, hw, MXU_N)]

    def body(dxo_ref, y_ref, gt_ref, a_ref, b_ref, wd_ref, dab_ref, dgt_ref, dwd_ref, dys, dyt, acc, stage, sem):
        i, j = pl.program_id(0), pl.program_id(1)

        @pl.when(jnp.logical_and(i == 0, j == 0))
        def _():
            dgt_ref[...] = jnp.zeros_like(dgt_ref)

        @pl.when(i == 0)
        def _():
            acc[j] = jnp.zeros((D, hw), F32)

        @pl.when(j == 0)
        def _():
            dxv = dxo_ref[...]
            dgt_ref[...] += 0.5 * jnp.sum(dxv * y_ref[...].astype(F32), axis=0, keepdims=True)
            dyf = (0.5 * gt_ref[...]) * dxv
            dys[...] = dyf.astype(BF)
            dyt[...] = dyf.T.astype(BF)

        dy = dys[...]
        dy_t = dyt[...]

        def dact_of(c0, cw):
            w_rows = pl.ds(pl.multiple_of(j * hw + c0, 128), cw)
            return lax.dot_general(dy, wd_ref[w_rows, :], NT, preferred_element_type=F32)

        ahead = dact_of(*chunks[0])
        for n, (c0, cw) in enumerate(chunks):
            cols = pl.ds(c0, cw)
            dact = ahead
            if n + 1 < len(chunks):
                ahead = dact_of(*chunks[n + 1])
            a = a_ref[:, cols].astype(F32)
            b = b_ref[:, cols].astype(F32)
            s = _sigmoid(a)
            silu = a * s
            dab_ref[0, :, cols] = (dact * b * (s * (1.0 + a * (1.0 - s)))).astype(BF)
            dab_ref[1, :, cols] = (dact * silu).astype(BF)
            acc[j, :, cols] += jnp.dot(dy_t, (silu * b).astype(BF), preferred_element_type=F32)

        @pl.when(i == nt - 1)
        def _():
            for c0, cw in chunks:
                stage[0:cw, :] = acc[j, :, pl.ds(c0, cw)].T.astype(BF)
                out = pltpu.make_async_copy(stage.at[pl.ds(0, cw)],
                                            dwd_ref.at[pl.ds(pl.multiple_of(j * hw + c0, 128), cw)], sem)
                out.start()
                out.wait()

    vec = pl.BlockSpec((1, D), _const2)
    rowspec = pl.BlockSpec((tm, D), lambda i, j: (i, 0))
    return pl.pallas_call(
        body, name=name, grid=(nt, 2),
        in_specs=[rowspec, rowspec, vec, pl.BlockSpec((tm, hw), lambda i, j: (i, j)),
                  pl.BlockSpec((tm, hw), lambda i, j: (i, j + 2)), _resident((F, D))],
        out_specs=[pl.BlockSpec((2, tm, hw), lambda i, j: (0, i, j)), vec, pl.BlockSpec(memory_space=pl.ANY)],
        out_shape=[jax.ShapeDtypeStruct((2, T, F), BF), jax.ShapeDtypeStruct((1, D), F32),
                   jax.ShapeDtypeStruct((F, D), BF)],
        scratch_shapes=[pltpu.VMEM((tm, D), BF), pltpu.VMEM((D, tm), BF), pltpu.VMEM((2, D, hw), F32),
                        pltpu.VMEM((MXU_N, D), BF), pltpu.SemaphoreType.DMA(())],
        compiler_params=_cp(("arbitrary", "arbitrary")),
    )(dxo, y, gt, ab, ab, wd)


def _tn_matmul(a, b, token=None, *, tn, tk, name):
    S, T, Ns = a.shape
    tn, tk = _tile(Ns, tn), _tile(T, tk)
    nk, njs = T // tk, Ns // tn
    deps = [] if token is None else [token]

    def body(a_ref, b_ref, *rest):
        o_ref, acc = rest[len(deps):]
        k = pl.program_id(1)

        @pl.when(k == 0)
        def _():
            acc[...] = jnp.zeros_like(acc)
        acc[...] += lax.dot_general(a_ref[0], b_ref[...], TN, preferred_element_type=F32)

        @pl.when(k == nk - 1)
        def _():
            o_ref[...] = acc[...].astype(BF)

    return pl.pallas_call(
        body, name=name, grid=(S * njs, nk),
        in_specs=[pl.BlockSpec((1, tk, tn), lambda j, k: (j // njs, k, j % njs)),
                  pl.BlockSpec((tk, D), lambda j, k: (k, 0))] + [pl.BlockSpec(memory_space=pl.ANY)] * len(deps),
        out_specs=pl.BlockSpec((tn, D), lambda j, k: (j, 0)),
        out_shape=jax.ShapeDtypeStruct((S * Ns, D), BF),
        scratch_shapes=[pltpu.VMEM((tn, D), F32)],
        compiler_params=_cp(("parallel", "arbitrary")),
    )(a, b, *deps)


def _nn_bwd_norm(da, w, x, g, sc, dxo, *, tm, name):
    S, T, Ks = da.shape
    tm = _tile(T, tm)
    rc = _tile(tm, 256)

    def body(da_ref, w_ref, x_ref, g_ref, sc_ref, dxo_ref, dx_ref, dsh_ref, dsc_ref, dg_ref, acc):
        @pl.when(pl.program_id(0) == 0)
        def _():
            dsh_ref[...] = jnp.zeros_like(dsh_ref)
            dsc_ref[...] = jnp.zeros_like(dsc_ref)
            dg_ref[...] = jnp.zeros_like(dg_ref)

        d = jnp.dot(da_ref[0], w_ref[0:Ks, :], preferred_element_type=F32)
        for s in range(1, S):
            d = d + jnp.dot(da_ref[s], w_ref[s * Ks:(s + 1) * Ks, :], preferred_element_type=F32)
        acc[...] = d
        gv = g_ref[...]
        sc1 = 1.0 + sc_ref[...]
        dsh = jnp.zeros((1, D), F32)
        dsc = jnp.zeros((1, D), F32)
        dg = jnp.zeros((1, D), F32)
        for r0 in range(0, tm, rc):
            rows = pl.ds(r0, rc)
            u = acc[rows, :]
            xv = x_ref[rows, :]
            r = lax.rsqrt(jnp.mean(xv * xv, axis=-1, keepdims=True) + EPS)
            xh = xv * r
            dsh = dsh + jnp.sum(u, axis=0, keepdims=True)
            dsc = dsc + jnp.sum(u * (xh * gv), axis=0, keepdims=True)
            us = u * sc1
            dg = dg + jnp.sum(us * xh, axis=0, keepdims=True)
            dxh = us * gv
            dx_ref[rows, :] = dxo_ref[rows, :] + r * (dxh - xh * jnp.mean(dxh * xh, axis=-1, keepdims=True))
        dsh_ref[...] += dsh
        dsc_ref[...] += dsc
        dg_ref[...] += dg

    vec = pl.BlockSpec((1, D), _const2)
    rowspec = pl.BlockSpec((tm, D), _row)
    return pl.pallas_call(
        body, name=name, grid=(T // tm,),
        in_specs=[pl.BlockSpec((S, tm, Ks), lambda i: (0, i, 0)), _resident((S * Ks, D)), rowspec, vec, vec, rowspec],
        out_specs=[rowspec, vec, vec, vec],
        out_shape=[jax.ShapeDtypeStruct((T, D), F32)] + [jax.ShapeDtypeStruct((1, D), F32)] * 3,
        scratch_shapes=[pltpu.VMEM((tm, D), F32)],
        compiler_params=_cp(("arbitrary",)),
    )(da, w, x, g, sc, dxo)


def _rope(t, cos, sin_signed, lt32, inverse=False):
    sel = jnp.where(lt32, pltpu.roll(t, 96, 1), pltpu.roll(t, 32, 1))
    return t * cos - sel * sin_signed if inverse else t * cos + sel * sin_signed


def _rope_tables(T, token=None):
    inv = 1.0 / (ROPE_THETA ** (jnp.arange(0, HEAD_DIM, 2, dtype=F32) / HEAD_DIM))
    ang = _behind(jnp.arange(T, dtype=F32)[:, None] * inv[None, :], token)
    cos, sin = jnp.cos(ang), jnp.sin(ang)
    cos128 = jnp.tile(cos, (1, 4))
    sin128 = jnp.tile(jnp.concatenate([-sin, sin], axis=1), (1, 2))
    return cos128, sin128


QSCALE = HEAD_DIM ** -0.5


def _lane_masks(rows):
    lane = lax.broadcasted_iota(jnp.int32, (rows, 128), 1)
    return (lane % HEAD_DIM) < (HEAD_DIM // 2), [lane < HEAD_DIM, lane >= HEAD_DIM]


def _attn_bias():
    qi = lax.broadcasted_iota(jnp.int32, (4 * BLK, 2 * BLK), 0) % BLK
    kj = lax.broadcasted_iota(jnp.int32, (4 * BLK, 2 * BLK), 1)
    band = (kj > qi) & (kj <= qi + BLK)
    return jnp.stack([jnp.where(band & (kj >= BLK), 0.0, NEG_INF), jnp.where(band, 0.0, NEG_INF)]).astype(F32)


def _attn_prep(proj, cos, sin, *, name):
    T = proj.shape[0]
    tm = _tile(T, 4 * BLK)

    def body(q_ref, k_ref, c_ref, s_ref, qs_ref, kr_ref):
        lt32, halves = _lane_masks(BLK)
        for b in range(tm // BLK):
            rows = pl.ds(b * BLK, BLK)
            cc, sc = c_ref[rows, :], s_ref[rows, :]
            qr = [_rope(q_ref[rows, p * 128:(p + 1) * 128].astype(F32), cc, sc, lt32) * QSCALE for p in range(8)]
            for g in range(N_KV):
                qs_ref[g, pl.ds(4 * b * BLK, 4 * BLK), :] = _stack_heads(qr, g, halves).astype(BF)
            kr_ref[rows, :] = jnp.concatenate([_rope(k_ref[rows, r * 128:(r + 1) * 128].astype(F32), cc, sc, lt32)
                                               for r in range(2)], axis=1).astype(BF)

    tab = pl.BlockSpec((tm, 128), _row)
    return pl.pallas_call(
        body, name=name, grid=(T // tm,),
        in_specs=[pl.BlockSpec((tm, D), lambda n: (n, O_Q // D)), pl.BlockSpec((tm, 256), lambda n: (n, O_K // 256)),
                  tab, tab],
        out_specs=[pl.BlockSpec((N_KV, 4 * tm, 128), lambda n: (0, n, 0)), pl.BlockSpec((tm, 256), _row)],
        out_shape=[jax.ShapeDtypeStruct((N_KV, 4 * T, 128), BF), jax.ShapeDtypeStruct((T, 256), BF)],
        compiler_params=_cp(("parallel",)),
    )(proj, proj, cos, sin)


def _attn_specs():
    prev = lambda n: jnp.maximum(n - 1, 0)
    return [pl.BlockSpec((N_KV, 4 * BLK, 128), lambda n: (0, n, 0)),
            pl.BlockSpec((BLK, 256), _row), pl.BlockSpec((BLK, 256), lambda n: (prev(n), 0)),
            pl.BlockSpec((BLK, 256), lambda n: (n, O_V // 256)),
            pl.BlockSpec((BLK, 256), lambda n: (prev(n), O_V // 256)),
            pl.BlockSpec((1, 4 * BLK, 2 * BLK), lambda n: (jnp.minimum(n, 1), 0, 0)),
            pl.BlockSpec(memory_space=pltpu.SMEM)]


def _bands(kc_ref, kp_ref, vc_ref, vp_ref):
    kb, vb = [], []
    for r in range(2):
        cols = slice(r * 128, (r + 1) * 128)
        kb.append(jnp.concatenate([kp_ref[:, cols], kc_ref[:, cols]], axis=0))
        vb.append(jnp.concatenate([vp_ref[:, cols], vc_ref[:, cols]], axis=0))
    return kb, vb


def _sink_rows(sink_ref, g):
    return jnp.concatenate([jnp.full((BLK, 128), sink_ref[4 * g + hh], F32) for hh in range(4)], axis=0)


def _both(t):
    return jnp.concatenate([t, t], axis=1)


def _unstack_heads(t, g, halves, acc):
    half = g % 2
    for hh in range(4):
        h = 4 * g + hh
        th = jnp.where(halves[half], t[hh * BLK:(hh + 1) * BLK], 0.0)
        if h % 2 != half:
            th = pltpu.roll(th, HEAD_DIM, 1)
        acc[h // 2] = acc[h // 2] + th


def _stack_heads(chunks, g, halves):
    half = g % 2
    parts = []
    for hh in range(4):
        h = 4 * g + hh
        t = chunks[h // 2]
        if h % 2 != half:
            t = pltpu.roll(t, HEAD_DIM, 1)
        parts.append(jnp.where(halves[half], t, 0.0))
    return jnp.concatenate(parts, axis=0)


def _attn_fwd(qs, kr, proj, bias, sinks, *, name):
    T = proj.shape[0]
    nb = T // BLK

    def body(qs_ref, kc_ref, kp_ref, vc_ref, vp_ref, bias_ref, sink_ref, o_ref, lse_ref):
        _, h128 = _lane_masks(BLK)
        _, h256 = _lane_masks(2 * BLK)
        _, h512 = _lane_masks(4 * BLK)
        kb, vb = _bands(kc_ref, kp_ref, vc_ref, vp_ref)
        outs = [jnp.zeros((BLK, 128), F32) for _ in range(8)]
        groups = range(N_KV)
        bias = bias_ref[0]
        sink = [_sink_rows(sink_ref, g) for g in groups]
        s = [lax.dot_general(qs_ref[g], kb[g // 2], NT, preferred_element_type=F32) + bias for g in groups]
        m = [jnp.maximum(jnp.broadcast_to(jnp.max(s[g], axis=-1, keepdims=True), (4 * BLK, 128)), sink[g])
             for g in groups]
        p = [jnp.exp(s[g] - _both(m[g])).astype(BF) for g in groups]
        vg = [jnp.where(h256[g % 2], vb[g // 2].astype(F32), 1.0).astype(BF) for g in groups]
        o = [jnp.dot(p[g], vg[g], preferred_element_type=F32) for g in groups]
        denom = [jnp.where(h512[g % 2], pltpu.roll(o[g], HEAD_DIM, 1), o[g]) + jnp.exp(sink[g] - m[g]) for g in groups]
        for g in groups:
            lse_ref[g] = m[g] + jnp.log(denom[g])
            _unstack_heads(o[g] * (1.0 / denom[g]), g, h128, outs)
        o_ref[...] = jnp.concatenate(outs, axis=1).astype(BF)

    return pl.pallas_call(
        body, name=name, grid=(nb,),
        in_specs=_attn_specs(),
        out_specs=[pl.BlockSpec((BLK, D), _row), pl.BlockSpec((N_KV, 4 * BLK, 128), lambda n: (0, n, 0))],
        out_shape=[jax.ShapeDtypeStruct((T, D), BF), jax.ShapeDtypeStruct((N_KV, 4 * T, 128), F32)],
        compiler_params=_cp(("parallel",)),
    )(qs, kr, kr, proj, proj, bias, sinks)


def _attn_bwd(qs, kr, proj, bias, sinks, lse, o, do, cos, sin, dproj, *, name):
    T = proj.shape[0]
    nb = T // BLK

    def body(qs_ref, kc_ref, kp_ref, vc_ref, vp_ref, bias_ref, sink_ref, lse_ref, o_ref, do_ref,
             cc_ref, sc_ref, cp_ref, sp_ref, dproj_ref, dq_ref, dkc_ref, dkp_ref, dvc_ref, dvp_ref, dsink_ref):
        @pl.when(pl.program_id(0) == 0)
        def _():
            dsink_ref[...] = jnp.zeros_like(dsink_ref)
        lt32, h128 = _lane_masks(BLK)
        kb, vb = _bands(kc_ref, kp_ref, vc_ref, vp_ref)
        oc = [o_ref[:, p * 128:(p + 1) * 128].astype(F32) for p in range(8)]
        doc = [do_ref[:, p * 128:(p + 1) * 128].astype(F32) for p in range(8)]
        dqs = [jnp.zeros((BLK, 128), F32) for _ in range(8)]
        lane1 = lax.broadcasted_iota(jnp.int32, (1, 128), 1)
        dsink = jnp.zeros((1, 128), F32)
        groups = range(N_KV)
        bias = bias_ref[0]
        q = [qs_ref[g] for g in groups]
        lse_g = [lse_ref[g] for g in groups]
        s = [lax.dot_general(q[g], kb[g // 2], NT, preferred_element_type=F32) + bias for g in groups]
        dos = [_stack_heads(doc, g, h128) for g in groups]
        dosb = [t.astype(BF) for t in dos]
        dp = [lax.dot_general(dosb[g], vb[g // 2], NT, preferred_element_type=F32) for g in groups]
        delta = [jnp.broadcast_to(jnp.sum(dos[g] * _stack_heads(oc, g, h128), axis=-1, keepdims=True), (4 * BLK, 128))
                 for g in groups]
        p = [jnp.exp(s[g] - _both(lse_g[g])) for g in groups]
        ds = [(p[g] * (dp[g] - _both(delta[g]))).astype(BF) for g in groups]
        pb = [t.astype(BF) for t in p]
        dvg = [lax.dot_general(pb[g], dosb[g], TN, preferred_element_type=F32) for g in groups]
        dkg = [lax.dot_general(ds[g], q[g], TN, preferred_element_type=F32) for g in groups]
        dqg = [jnp.dot(ds[g], kb[g // 2], preferred_element_type=F32) * QSCALE for g in groups]
        dvr = [dvg[0] + dvg[1], dvg[2] + dvg[3]]
        dkr = [dkg[0] + dkg[1], dkg[2] + dkg[3]]
        for g in groups:
            _unstack_heads(dqg[g], g, h128, dqs)
            dsk = -jnp.exp(_sink_rows(sink_ref, g) - lse_g[g]) * delta[g]
            for hh in range(4):
                val = jnp.sum(dsk[hh * BLK:(hh + 1) * BLK], axis=0, keepdims=True)
                dsink = dsink + jnp.where(lane1 == 4 * g + hh, val, 0.0)
        cc, sc, cp, sp = cc_ref[...], sc_ref[...], cp_ref[...], sp_ref[...]
        dsink_ref[...] += dsink
        dq_ref[...] = jnp.concatenate([_rope(t, cc, sc, lt32, inverse=True) for t in dqs], axis=1).astype(BF)
        dkp_ref[...] = jnp.concatenate([_rope(t[:BLK], cp, sp, lt32, inverse=True) for t in dkr], axis=1)
        dkc_ref[...] = jnp.concatenate([_rope(t[BLK:], cc, sc, lt32, inverse=True) for t in dkr], axis=1)
        dvp_ref[...] = jnp.concatenate([t[:BLK] for t in dvr], axis=1)
        dvc_ref[...] = jnp.concatenate([t[BLK:] for t in dvr], axis=1)

    kv = pl.BlockSpec((BLK, 256), _row)
    tc = pl.BlockSpec((BLK, 128), _row)
    tp = pl.BlockSpec((BLK, 128), lambda n: (jnp.maximum(n - 1, 0), 0))
    return pl.pallas_call(
        body, name=name, grid=(nb,),
        in_specs=_attn_specs() + [pl.BlockSpec((N_KV, 4 * BLK, 128), lambda n: (0, n, 0)),
                                  pl.BlockSpec((BLK, D), _row), pl.BlockSpec((BLK, D), _row), tc, tc, tp, tp,
                                  pl.BlockSpec(memory_space=pl.ANY)],
        out_specs=[pl.BlockSpec((BLK, D), lambda n: (n, O_Q // D)), kv, kv, kv, kv, pl.BlockSpec((1, 128), _const2)],
        out_shape=[jax.ShapeDtypeStruct(dproj.shape, BF)] + [jax.ShapeDtypeStruct((T, 256), F32)] * 4
        + [jax.ShapeDtypeStruct((1, 128), F32)],
        input_output_aliases={14: 0},
        compiler_params=_cp(("arbitrary",)),
    )(qs, kr, kr, proj, proj, bias, sinks, lse, o, do, cos, sin, cos, sin, dproj)


def _dkv_combine(dkc, dkp, dvc, dvp, dproj, *, name):
    T = dkc.shape[0]
    nb = T // BLK
    tm = _tile(T, 4 * BLK)
    bpt = tm // BLK
    nt = T // tm

    def body(dkc_ref, dkp_ref, dkn_ref, dvc_ref, dvp_ref, dvn_ref, dproj_ref, o_ref):
        keep = jnp.where(pl.program_id(0) == nt - 1, 0.0, 1.0)

        def shifted(prev_ref, next_ref):
            nxt = keep * next_ref[...]
            return nxt if bpt == 1 else jnp.concatenate([prev_ref[BLK:, :], nxt], axis=0)

        o_ref[:, 0:256] = (dkc_ref[...] + shifted(dkp_ref, dkn_ref)).astype(BF)
        o_ref[:, 256:512] = (dvc_ref[...] + shifted(dvp_ref, dvn_ref)).astype(BF)

    cur = pl.BlockSpec((tm, 256), _row)
    nxt = pl.BlockSpec((BLK, 256), lambda i: (jnp.minimum((i + 1) * bpt, nb - 1), 0))
    return pl.pallas_call(
        body, name=name, grid=(nt,),
        in_specs=[cur, cur, nxt, cur, cur, nxt, pl.BlockSpec(memory_space=pl.ANY)],
        out_specs=pl.BlockSpec((tm, 512), lambda i: (i, O_K // 512)),
        out_shape=jax.ShapeDtypeStruct(dproj.shape, BF),
        input_output_aliases={6: 0},
        compiler_params=_cp(("parallel",)),
    )(dkc, dkp, dkp, dvc, dvp, dvp, dproj)


HALO = 16


def _conv_shifts(cu, hprev, tm):
    row = lax.broadcasted_iota(jnp.int32, (8, cu.shape[1]), 0)
    h1 = hprev[HALO - 1:HALO, :]
    h2 = hprev[HALO - 2:HALO - 1, :]
    m1 = pltpu.roll(cu, 1, 0)
    m2 = pltpu.roll(cu, 2, 0)
    m1 = jnp.concatenate([jnp.where(row == 0, h1, m1[0:8]), m1[8:]], axis=0)
    m2 = jnp.concatenate([jnp.where(row == 0, h2, jnp.where(row == 1, h1, m2[0:8])), m2[8:]], axis=0)
    return m1, m2


def _mixer_mid_fwd(proj, attn, wcp, wap, wout, convw, x, gt, *, tm, name):
    T = x.shape[0]
    tm = _tile(T, tm)
    hb = tm // HALO

    def body(bg_ref, cg_ref, u_ref, hcg_ref, hu_ref, zc0_ref, zc1_ref, za0_ref, za1_ref, at_ref,
             wcp_ref, wap_ref, wout_ref, cw_ref, x_ref, gt_ref,
             x2_ref, gc_ref, yc_ref, ya_ref, mg_ref, o_ref):
        first = jnp.where(pl.program_id(0) == 0, 0.0, 1.0)
        cu = cg_ref[...].astype(F32) * u_ref[...].astype(F32)
        hprev = first * (hcg_ref[...].astype(F32) * hu_ref[...].astype(F32))
        m1, m2 = _conv_shifts(cu, hprev, tm)
        cv = cw_ref[0:1, :] * m2 + cw_ref[1:2, :] * m1 + cw_ref[2:3, :] * cu
        gc = (bg_ref[...].astype(F32) * cv).astype(BF)
        gc_ref[...] = gc
        yc = jnp.dot(gc, wcp_ref[...], preferred_element_type=F32)
        ya = jnp.dot(at_ref[...], wap_ref[...], preferred_element_type=F32)
        yc_ref[...] = yc.astype(BF)
        ya_ref[...] = ya.astype(BF)
        zc = jnp.concatenate([zc0_ref[...], zc1_ref[...]], axis=1).astype(F32)
        za = jnp.concatenate([za0_ref[...], za1_ref[...]], axis=1).astype(F32)
        mg = (_sigmoid(zc) * yc + _sigmoid(za) * ya).astype(BF)
        mg_ref[...] = mg
        o = jnp.dot(mg, wout_ref[...], preferred_element_type=F32)
        o_ref[...] = o.astype(BF)
        x2_ref[...] = x_ref[...] + gt_ref[...] * o

    wspec = pl.BlockSpec((D, D), _const2)
    rowspec = pl.BlockSpec((tm, D), _row)
    return pl.pallas_call(
        body, name=name, grid=(T // tm,),
        in_specs=[_col(tm, O_BG), _col(tm, O_CG), _col(tm, O_U), _halo_prev(hb, O_CG), _halo_prev(hb, O_U),
                  _col(tm, O_ZC, 512), _col(tm, O_ZC + 512, 512), _col(tm, O_ZA, 512), _col(tm, O_ZA + 512, 512),
                  rowspec, wspec, wspec, wspec, pl.BlockSpec((8, D), _const2), rowspec, pl.BlockSpec((1, D), _const2)],
        out_specs=[rowspec] * 6,
        out_shape=[jax.ShapeDtypeStruct((T, D), F32)] + [jax.ShapeDtypeStruct((T, D), BF)] * 5,
        compiler_params=_cp(("parallel",)),
    )(proj, proj, proj, proj, proj, proj, proj, proj, proj, attn, wcp, wap, wout, convw, x, gt)


def _col(tm, c, w=D):
    assert c % w == 0
    return pl.BlockSpec((tm, w), lambda i: (i, c // w))


def _halo_prev(hb, c):
    return pl.BlockSpec((HALO, D), lambda i: (jnp.maximum(i * hb - 1, 0), c // D))


def _halo_next(hb, nblk, c=0):
    return pl.BlockSpec((HALO, D), lambda i: (jnp.minimum((i + 1) * hb, nblk - 1), c // D))


def _mixer_mid_bwd(dx2, gt, o, proj, yc, ya, wout, wcp, wap, *, tm, name):
    T = dx2.shape[0]
    tm = _tile(T, tm)
    nt = T // tm

    def body(dx_ref, gt_ref, o_ref, zc0_ref, zc1_ref, za0_ref, za1_ref, yc_ref, ya_ref, wout_ref, wcp_ref, wap_ref,
             dout_ref, dyc_ref, dya_ref, dgc_ref, dat_ref, dproj_ref, dgt_ref, dzs, sems):
        i = pl.program_id(0)
        slot = lax.rem(i, 2)

        def slab_copy(step, s):
            return pltpu.make_async_copy(
                dzs.at[s], dproj_ref.at[pl.ds(pl.multiple_of(step * tm, tm), tm), pl.ds(O_ZC, 2 * D)], sems.at[s])

        @pl.when(i == 0)
        def _():
            dgt_ref[...] = jnp.zeros_like(dgt_ref)

        dxv = dx_ref[...]
        dgt_ref[...] += jnp.sum(dxv * o_ref[...].astype(F32), axis=0, keepdims=True)
        dout = (gt_ref[...] * dxv).astype(BF)
        dout_ref[...] = dout
        dmg = lax.dot_general(dout, wout_ref[...], NT, preferred_element_type=F32)
        sc = _sigmoid(jnp.concatenate([zc0_ref[...], zc1_ref[...]], axis=1).astype(F32))
        sa = _sigmoid(jnp.concatenate([za0_ref[...], za1_ref[...]], axis=1).astype(F32))
        dyc = (dmg * sc).astype(BF)
        dya = (dmg * sa).astype(BF)
        dyc_ref[...] = dyc
        dya_ref[...] = dya
        dzs[slot, :, 0:D] = (dmg * yc_ref[...].astype(F32) * (sc * (1.0 - sc))).astype(BF)
        dzs[slot, :, D:2 * D] = (dmg * ya_ref[...].astype(F32) * (sa * (1.0 - sa))).astype(BF)
        slab_copy(i, slot).start()
        dgc_ref[...] = lax.dot_general(dyc, wcp_ref[...], NT, preferred_element_type=F32).astype(BF)
        dat_ref[...] = lax.dot_general(dya, wap_ref[...], NT, preferred_element_type=F32).astype(BF)

        @pl.when(i > 0)
        def _():
            slab_copy(i - 1, 1 - slot).wait()

        @pl.when(i == nt - 1)
        def _():
            slab_copy(i, slot).wait()

    def zcol(c):
        return pl.BlockSpec((tm, 512), lambda i: (i, c // 512))

    wspec = pl.BlockSpec((D, D), _const2)
    rowspec = pl.BlockSpec((tm, D), _row)
    vec = pl.BlockSpec((1, D), _const2)
    return pl.pallas_call(
        body, name=name, grid=(nt,),
        in_specs=[rowspec, vec, rowspec, zcol(O_ZC), zcol(O_ZC + 512), zcol(O_ZA), zcol(O_ZA + 512),
                  rowspec, rowspec, wspec, wspec, wspec],
        out_specs=[rowspec] * 5 + [pl.BlockSpec(memory_space=pl.ANY), vec],
        out_shape=[jax.ShapeDtypeStruct((T, D), BF)] * 5 + [jax.ShapeDtypeStruct((T, NIN), BF),
                                                            jax.ShapeDtypeStruct((1, D), F32)],
        scratch_shapes=[pltpu.VMEM((2, tm, 2 * D), BF), pltpu.SemaphoreType.DMA((2,))],
        compiler_params=_cp(("arbitrary",)),
    )(dx2, gt, o, proj, proj, proj, proj, yc, ya, wout, wcp, wap)


def _conv_bwd(dgc, proj, convw, dproj, *, tm, name):
    T = dgc.shape[0]
    tm = _tile(T, tm)
    hb = tm // HALO
    nblk = T // HALO
    nt = T // tm

    def body(dgc_ref, ndgc_ref, bg_ref, nbg_ref, cg_ref, u_ref, hcg_ref, hu_ref, cw_ref, dproj_ref, dp_ref, dcw_ref):
        i = pl.program_id(0)

        @pl.when(i == 0)
        def _():
            dcw_ref[...] = jnp.zeros_like(dcw_ref)
        first = jnp.where(i == 0, 0.0, 1.0)
        last = jnp.where(i == nt - 1, 0.0, 1.0)
        cg = cg_ref[...].astype(F32)
        u = u_ref[...].astype(F32)
        bg = bg_ref[...].astype(F32)
        dg = dgc_ref[...].astype(F32)
        cu = cg * u
        hprev = first * (hcg_ref[...].astype(F32) * hu_ref[...].astype(F32))
        m1, m2 = _conv_shifts(cu, hprev, tm)
        w0, w1, w2 = cw_ref[0:1, :], cw_ref[1:2, :], cw_ref[2:3, :]
        cv = w0 * m2 + w1 * m1 + w2 * cu
        dcv = dg * bg
        nxt = last * (ndgc_ref[...].astype(F32) * nbg_ref[...].astype(F32))
        n0, n1 = nxt[0:1, :], nxt[1:2, :]
        row = lax.broadcasted_iota(jnp.int32, (8, D), 0)
        p1 = pltpu.roll(dcv, tm - 1, 0)
        p2 = pltpu.roll(dcv, tm - 2, 0)
        p1 = jnp.concatenate([p1[:tm - 8], jnp.where(row == 7, n0, p1[tm - 8:])], axis=0)
        p2 = jnp.concatenate([p2[:tm - 8], jnp.where(row == 7, n1, jnp.where(row == 6, n0, p2[tm - 8:]))], axis=0)
        dcu = w2 * dcv + w1 * p1 + w0 * p2
        dp_ref[:, 0:D] = (dg * cv).astype(BF)
        dp_ref[:, D:2 * D] = (dcu * u).astype(BF)
        dp_ref[:, 2 * D:3 * D] = (dcu * cg).astype(BF)
        dcw_ref[0:1, :] += jnp.sum(dcv * m2, axis=0, keepdims=True)
        dcw_ref[1:2, :] += jnp.sum(dcv * m1, axis=0, keepdims=True)
        dcw_ref[2:3, :] += jnp.sum(dcv * cu, axis=0, keepdims=True)

    rowspec = pl.BlockSpec((tm, D), _row)
    cw = pl.BlockSpec((8, D), _const2)
    return pl.pallas_call(
        body, name=name, grid=(nt,),
        in_specs=[rowspec, _halo_next(hb, nblk), _col(tm, O_BG), _halo_next(hb, nblk, O_BG),
                  _col(tm, O_CG), _col(tm, O_U), _halo_prev(hb, O_CG), _halo_prev(hb, O_U), cw,
                  pl.BlockSpec(memory_space=pl.ANY)],
        out_specs=[pl.BlockSpec((tm, 3 * D), _row), cw],
        out_shape=[jax.ShapeDtypeStruct(dproj.shape, BF), jax.ShapeDtypeStruct((8, D), F32)],
        input_output_aliases={9: 0},
        compiler_params=_cp(("arbitrary",)),
    )(dgc, dgc, proj, proj, proj, proj, proj, proj, convw, dproj)


def _adam(w, g, m, v, *, tm, name):
    _, R, C = w.shape
    tm = _tile(R, tm)
    parts = g.ndim == 3
    c1 = 1.0 - ADAM_B1
    c2 = 1.0 - ADAM_B2
    bc1 = 1.0 - ADAM_B1 ** ADAM_STEP
    bc2 = 1.0 - ADAM_B2 ** ADAM_STEP

    def body(w_ref, g_ref, m_ref, v_ref, go_ref, d_ref, nm_ref, nv_ref):
        if parts:
            gv = g_ref[0].astype(F32)
            for s in range(1, N_DEV):
                gv = gv + g_ref[s].astype(F32)
        else:
            gv = g_ref[...]
        go_ref[0] = gv
        nm = ADAM_B1 * m_ref[0] + c1 * gv
        nv = ADAM_B2 * v_ref[0] + c2 * (gv * gv)
        nm_ref[0] = nm
        nv_ref[0] = nv
        d_ref[0] = -ADAM_LR * ((nm / bc1) / (jnp.sqrt(nv / bc2) + ADAM_EPS) + ADAM_WD * w_ref[0])

    spec = pl.BlockSpec((1, tm, C), lambda i: (0, i, 0))
    gspec = pl.BlockSpec((N_DEV, tm, C), lambda i: (0, i, 0)) if parts else pl.BlockSpec((tm, C), _row)
    return pl.pallas_call(
        body, name=name, grid=(R // tm,),
        in_specs=[spec, gspec, spec, spec], out_specs=[spec] * 4,
        out_shape=[jax.ShapeDtypeStruct((1, R, C), F32)] * 4,
        compiler_params=_cp(("parallel",)),
    )(w, g, m, v)


def _mods_part(c_all, w_ada, b_ada, *, name):
    C = w_ada.shape[1]

    def body(c_ref, w_ref, b_ref, o_ref):
        cv = c_ref[...]
        ca = cv * jax.nn.sigmoid(cv)
        o_ref[...] = jnp.dot(ca, w_ref[...], preferred_element_type=F32,
                             precision=lax.Precision.HIGHEST) + b_ref[...]

    return pl.pallas_call(
        body, name=name,
        out_shape=jax.ShapeDtypeStruct((N_DEV, C), F32),
        compiler_params=_cp(),
    )(c_all, w_ada, b_ada)


def _wada_grad(c_all_t, gm, *, name):
    C = gm.shape[1]

    def body(c_ref, g_ref, o_ref):
        cv = c_ref[...]
        ca = cv * jax.nn.sigmoid(cv)
        acc = ca[:, 0:1] * g_ref[0:1, :]
        for b in range(1, N_DEV):
            acc = acc + ca[:, b:b + 1] * g_ref[b:b + 1, :]
        o_ref[...] = acc

    return pl.pallas_call(
        body, name=name,
        out_shape=jax.ShapeDtypeStruct((D, C), F32),
        compiler_params=_cp(),
    )(c_all_t, gm)


def _peer(x, y, c, d):
    px = lax.rem(x + ((d >> 2) & 1), 2)
    py = lax.rem(y + ((d >> 1) & 1), 2)
    pc = lax.rem(c + (d & 1), 2)
    return (px, py, pc), 4 * px + 2 * py + pc


def _exchange(xs, *, scatter, name):
    n = len(xs)
    nsem = n * (N_DEV - 1)

    def body(*refs):
        ins, outs = refs[:n], refs[n:2 * n]
        token, send_sems, recv_sems, local_sems = refs[2 * n:]
        x, y, c = lax.axis_index("x"), lax.axis_index("y"), lax.axis_index("c")
        me = 4 * x + 2 * y + c
        token[...] = jnp.zeros_like(token)

        def src(t, idx):
            return ins[t].at[idx] if scatter else ins[t]

        local = [pltpu.make_async_copy(src(t, me), outs[t].at[me], local_sems.at[t]) for t in range(n)]
        for cp in local:
            cp.start()
        remote = []
        for t in range(n):
            for d in range(1, N_DEV):
                peer, pidx = _peer(x, y, c, d)
                k = t * (N_DEV - 1) + d - 1
                send = pltpu.make_async_remote_copy(src_ref=src(t, pidx), dst_ref=outs[t].at[me],
                                                    send_sem=send_sems.at[k], recv_sem=recv_sems.at[k],
                                                    device_id=peer, device_id_type=MESH)
                recv = pltpu.make_async_remote_copy(src_ref=src(t, pidx), dst_ref=outs[t].at[pidx],
                                                    send_sem=send_sems.at[k], recv_sem=recv_sems.at[k],
                                                    device_id=peer, device_id_type=MESH)
                send.start()
                remote.append((send, recv))
        for cp in local:
            cp.wait()
        for send, recv in remote:
            send.wait_send()
            recv.wait_recv()

    anyspec = pl.BlockSpec(memory_space=pl.ANY)
    out_shape = [jax.ShapeDtypeStruct(a.shape if scatter else (N_DEV,) + a.shape, a.dtype) for a in xs]
    out_shape.append(jax.ShapeDtypeStruct((8, 128), F32))
    return pl.pallas_call(
        body, name=name,
        in_specs=[anyspec] * n, out_specs=[anyspec] * n + [pl.BlockSpec(memory_space=pltpu.VMEM)],
        out_shape=out_shape,
        scratch_shapes=[pltpu.SemaphoreType.DMA((nsem,)), pltpu.SemaphoreType.DMA((nsem,)),
                        pltpu.SemaphoreType.DMA((n,))],
    )(*xs)


def _sum8(parts, *, name):
    _, R, C = parts.shape

    def body(p_ref, o_ref):
        acc = p_ref[0]
        for s in range(1, N_DEV):
            acc = acc + p_ref[s]
        o_ref[...] = acc

    return pl.pallas_call(body, name=name, out_shape=jax.ShapeDtypeStruct((R, C), F32),
                          compiler_params=_cp())(parts)


HBM_SPEC = pl.BlockSpec(memory_space=pltpu.HBM)
SEM_SPEC = pl.BlockSpec(memory_space=pltpu.SEMAPHORE)
N_PEER = N_DEV - 1


def _split_copies(src_refs, land_refs, send_sems, recv_sems, scatter):
    x, y, c = lax.axis_index("x"), lax.axis_index("y"), lax.axis_index("c")
    me = 4 * x + 2 * y + c
    pairs = []
    for j, (src, land) in enumerate(zip(src_refs, land_refs)):
        for d in range(1, N_DEV):
            peer, pidx = _peer(x, y, c, d)
            k = j * N_PEER + d - 1
            s = src.at[pidx] if scatter else src
            send = pltpu.make_async_remote_copy(src_ref=s, dst_ref=land.at[me], send_sem=send_sems.at[k],
                                                recv_sem=recv_sems.at[k], device_id=peer, device_id_type=MESH)
            recv = pltpu.make_async_remote_copy(src_ref=s, dst_ref=land.at[pidx], send_sem=send_sems.at[k],
                                                recv_sem=recv_sems.at[k], device_id=peer, device_id_type=MESH)
            pairs.append((send, recv))
    return pairs


def _own_slot(block, me):
    land = lax.empty((N_DEV,) + block.shape, block.dtype)
    return lax.dynamic_update_slice(land, block[None], (me, 0, 0))


def _split_start(srcs, lands, groups, *, scatter, name):
    n, ng = len(srcs), len(groups)

    def body(*refs):
        src_refs, land_refs = refs[:n], refs[n:2 * n]
        sems = refs[2 * n:2 * n + 2 * ng]
        token = refs[-1]
        for gi, g in enumerate(groups):
            pairs = _split_copies([src_refs[t] for t in g], [land_refs[t] for t in g], sems[2 * gi],
                                  sems[2 * gi + 1], scatter)
            for send, _ in pairs:
                send.start()
        token[...] = jnp.zeros_like(token)

    sem_shapes = []
    for g in groups:
        sem_shapes += [pltpu.SemaphoreType.DMA((len(g) * N_PEER,))] * 2
    thru = [pltpu.HBM(a.shape, a.dtype) for a in list(srcs) + list(lands)]
    outs = pl.pallas_call(
        body, name=name,
        out_shape=tuple(sem_shapes + thru + [jax.ShapeDtypeStruct((8, 128), F32)]),
        in_specs=[HBM_SPEC] * (2 * n),
        out_specs=tuple([SEM_SPEC] * (2 * ng) + [HBM_SPEC] * (2 * n) + [pl.BlockSpec(memory_space=pltpu.VMEM)]),
        input_output_aliases={i: 2 * ng + i for i in range(2 * n)},
        compiler_params=pltpu.CompilerParams(has_side_effects=pltpu.SideEffectType.DATAFLOW_SIDE_EFFECTING),
    )(*[pltpu.with_memory_space_constraint(a, pltpu.HBM) for a in list(srcs) + list(lands)])
    sems = [(outs[2 * gi], outs[2 * gi + 1]) for gi in range(ng)]
    return sems, outs[2 * ng:2 * ng + n], outs[2 * ng + n:2 * ng + 2 * n], outs[-1]


def _behind(v, token):
    if token is None:
        return v
    return v + token[0, 0].astype(v.dtype)


def _split_wait(srcs, lands, sems, after, *, scatter, name):
    m = len(srcs)

    def body(*refs):
        src_refs, land_refs = refs[:m], refs[m:2 * m]
        send_sems, recv_sems = refs[2 * m], refs[2 * m + 1]
        for send, recv in _split_copies(src_refs, land_refs, send_sems, recv_sems, scatter):
            send.wait_send()
            recv.wait_recv()

    outs = pl.pallas_call(
        body, name=name,
        out_shape=tuple(pltpu.HBM(a.shape, a.dtype) for a in list(srcs) + list(lands)),
        in_specs=[HBM_SPEC] * (2 * m) + [SEM_SPEC, SEM_SPEC, pl.BlockSpec(memory_space=pl.ANY)],
        out_specs=tuple([HBM_SPEC] * (2 * m)),
        input_output_aliases={i: i for i in range(2 * m)},
        compiler_params=pltpu.CompilerParams(has_side_effects=pltpu.SideEffectType.DATAFLOW_SIDE_EFFECTING),
    )(*srcs, *lands, sems[0], sems[1], after)
    return outs[m:]


TL_FIRST = (1, 2, 4, 6)
TL_ICI = (2, 4, 6)
EFFECT = pltpu.SideEffectType.DATAFLOW_SIDE_EFFECTING


def _tl_first(src_refs, land_refs, send_sems, recv_sems):
    x, y, c = lax.axis_index("x"), lax.axis_index("y"), lax.axis_index("c")
    me = 4 * x + 2 * y + c
    out = []
    for j, (src, land) in enumerate(zip(src_refs, land_refs)):
        for i, d in enumerate(TL_FIRST):
            peer, pidx = _peer(x, y, c, d)
            k = len(TL_FIRST) * j + i
            send = pltpu.make_async_remote_copy(src_ref=src, dst_ref=land.at[me], send_sem=send_sems.at[k],
                                                recv_sem=recv_sems.at[k], device_id=peer, device_id_type=MESH)
            recv = pltpu.make_async_remote_copy(src_ref=src, dst_ref=land.at[pidx], send_sem=send_sems.at[k],
                                                recv_sem=recv_sems.at[k], device_id=peer, device_id_type=MESH)
            out.append((d, send, recv))
    return out


def _tl_second(land_refs, send_sems, recv_sems):
    x, y, c = lax.axis_index("x"), lax.axis_index("y"), lax.axis_index("c")
    sibling, _ = _peer(x, y, c, 1)
    out = []
    for j, land in enumerate(land_refs):
        for i, d in enumerate(TL_ICI):
            _, mine = _peer(x, y, c, d)
            _, theirs = _peer(x, y, c, d + 1)
            k = len(TL_ICI) * j + i
            send = pltpu.make_async_remote_copy(src_ref=land.at[mine], dst_ref=land.at[mine], send_sem=send_sems.at[k],
                                                recv_sem=recv_sems.at[k], device_id=sibling, device_id_type=MESH)
            recv = pltpu.make_async_remote_copy(src_ref=land.at[mine], dst_ref=land.at[theirs],
                                                send_sem=send_sems.at[k], recv_sem=recv_sems.at[k],
                                                device_id=sibling, device_id_type=MESH)
            out.append((send, recv))
    return out


def _tl_start(srcs, lands, groups, *, name):
    n, ng = len(srcs), len(groups)

    def body(*refs):
        src_refs, land_refs = refs[:n], refs[n:2 * n]
        sems = refs[2 * n:2 * n + 2 * ng]
        for gi, g in enumerate(groups):
            for _, send, _ in _tl_first([src_refs[t] for t in g], [land_refs[t] for t in g], sems[2 * gi],
                                        sems[2 * gi + 1]):
                send.start()
        refs[-1][...] = jnp.zeros_like(refs[-1])

    sem_shapes = []
    for g in groups:
        sem_shapes += [pltpu.SemaphoreType.DMA((len(g) * len(TL_FIRST),))] * 2
    thru = [pltpu.HBM(a.shape, a.dtype) for a in list(srcs) + list(lands)]
    outs = pl.pallas_call(
        body, name=name,
        out_shape=tuple(sem_shapes + thru + [jax.ShapeDtypeStruct((8, 128), F32)]),
        in_specs=[HBM_SPEC] * (2 * n),
        out_specs=tuple([SEM_SPEC] * (2 * ng) + [HBM_SPEC] * (2 * n) + [pl.BlockSpec(memory_space=pltpu.VMEM)]),
        input_output_aliases={i: 2 * ng + i for i in range(2 * n)},
        compiler_params=pltpu.CompilerParams(has_side_effects=EFFECT),
    )(*[pltpu.with_memory_space_constraint(a, pltpu.HBM) for a in list(srcs) + list(lands)])
    sems = [(outs[2 * gi], outs[2 * gi + 1]) for gi in range(ng)]
    return sems, outs[2 * ng:2 * ng + n], outs[2 * ng + n:2 * ng + 2 * n], outs[-1]


def _tl_forward(srcs, lands, sems1, after, *, name):
    m = len(srcs)

    def body(*refs):
        src_refs, land_refs = refs[:m], refs[m:2 * m]
        send1, recv1 = refs[2 * m], refs[2 * m + 1]
        send2, recv2 = refs[2 * m + 3], refs[2 * m + 4]
        for d, _, recv in _tl_first(src_refs, land_refs, send1, recv1):
            if d in TL_ICI:
                recv.wait_recv()
        for send, _ in _tl_second(land_refs, send2, recv2):
            send.start()

    sem = pltpu.SemaphoreType.DMA((m * len(TL_ICI),))
    outs = pl.pallas_call(
        body, name=name,
        out_shape=tuple([sem, sem] + [pltpu.HBM(a.shape, a.dtype) for a in list(srcs) + list(lands)]),
        in_specs=[HBM_SPEC] * (2 * m) + [SEM_SPEC, SEM_SPEC, pl.BlockSpec(memory_space=pl.ANY)],
        out_specs=tuple([SEM_SPEC, SEM_SPEC] + [HBM_SPEC] * (2 * m)),
        input_output_aliases={i: 2 + i for i in range(2 * m)},
        compiler_params=pltpu.CompilerParams(has_side_effects=EFFECT),
    )(*srcs, *lands, sems1[0], sems1[1], after)
    return (outs[0], outs[1]), outs[2:2 + m], outs[2 + m:2 + 2 * m]


def _tl_wait(srcs, lands, sems1, sems2, after, *, name):
    m = len(srcs)

    def body(*refs):
        src_refs, land_refs = refs[:m], refs[m:2 * m]
        send1, recv1, send2, recv2 = refs[2 * m:2 * m + 4]
        for d, send, recv in _tl_first(src_refs, land_refs, send1, recv1):
            send.wait_send()
            if d not in TL_ICI:
                recv.wait_recv()
        for send, recv in _tl_second(land_refs, send2, recv2):
            send.wait_send()
            recv.wait_recv()

    outs = pl.pallas_call(
        body, name=name,
        out_shape=tuple(pltpu.HBM(a.shape, a.dtype) for a in list(srcs) + list(lands)),
        in_specs=[HBM_SPEC] * (2 * m) + [SEM_SPEC] * 4 + [pl.BlockSpec(memory_space=pl.ANY)],
        out_specs=tuple([HBM_SPEC] * (2 * m)),
        input_output_aliases={i: i for i in range(2 * m)},
        compiler_params=pltpu.CompilerParams(has_side_effects=EFFECT),
    )(*srcs, *lands, sems1[0], sems1[1], sems2[0], sems2[1], after)
    return outs[m:]


TM_PROJ = 512
TN_PROJ = 512
TM_ROW = 512
TM_NN = 512
TK_TN = 2048
TM_ADAM = 416
TN_FFN = F // 2
TN_IN = NIN // 4


def _tn(a, b, name, tn, token=None):
    if a.ndim == 2:
        a = a[None]
    return _tn_matmul(a, b, token, tn=tn, tk=TK_TN, name=name)


def _local_step(x, tgt, mods, g1, gm, g2, gf, convw8, sinks, w_get, g_put, token=None):
    T = x.shape[0]
    sh1, sc1, gt1, sh2, sc2, gt2, sh3, sc3, gt3 = [mods[i:i + 1] for i in range(N_MOD)]
    cos, sin = _rope_tables(T, token)
    behind = _behind

    w = dict(w_get("gu1", mods))
    h1, ab1 = _norm_proj(x, g1, sc1, sh1, w["gu1"], tm=TM_PROJ, tn=TN_PROJ, name="ffn1_up")
    w.update(w_get("d1", ab1))
    x1, y1 = _ffn_down_fwd(ab1, w["d1"], x, gt1, tm=TM_ROW, name="ffn1_down")
    w.update(w_get("mix", x1))
    h2, proj = _norm_proj(x1, gm, sc2, sh2, w["win"], tm=TM_PROJ, tn=TN_PROJ, name="mix_in")
    qs, kr = _attn_prep(proj, cos, sin, name="attn_prep")
    bias = _attn_bias()
    attn, lse = _attn_fwd(qs, kr, proj, bias, sinks, name="attn_fwd")
    x2, gc, yc, ya, mg, o = _mixer_mid_fwd(proj, attn, w["cp"], w["ap"], w["out"], convw8, x1, gt2,
                                           tm=TM_ROW, name="mix_mid")
    w.update(w_get("ffn2", x2))
    h3, ab2, y2, dx3, lsum, dgf = _ffn_fwd(x2, g2, sc3, sh3, gt3, w["gu2"], w["d2"], (tgt, gf), tm=TM_ROW,
                                           name="ffn2_final")

    dab2, dgt3, g_d2 = _ffn_down_bwd_dw(dx3, y2, gt3, ab2, w["d2"], tm=TM_ROW, name="ffn2_down_bwd")
    dx2, dsh3, dsc3, dg2 = _nn_bwd_norm(dab2, w["gu2"], x2, g2, sc3, dx3, tm=TM_NN, name="ffn2_up_bwd")
    g_gu2 = _tn(dab2, h3, "ffn2_up_dw", TN_FFN)
    tok = g_put(dict(gu2=g_gu2, d2=g_d2))

    dout, dyc, dya, dgc, dat, dproj, dgt2 = _mixer_mid_bwd(dx2, behind(gt2, tok), o, proj, yc, ya, w["out"], w["cp"],
                                                           w["ap"], tm=TM_ROW, name="mix_mid_bwd")
    g_out = _tn(mg, dout, "mix_out_dw", D)
    g_cp = _tn(gc, dyc, "mix_cp_dw", D)
    g_ap = _tn(attn, dya, "mix_ap_dw", D)
    dproj, dkc, dkp, dvc, dvp, dsink = _attn_bwd(qs, kr, proj, bias, sinks, lse, attn, dat, cos, sin, dproj,
                                                 name="attn_bwd")
    dproj = _dkv_combine(dkc, dkp, dvc, dvp, dproj, name="attn_dkv")
    dproj, dcw = _conv_bwd(dgc, proj, convw8, dproj, tm=TM_ROW, name="conv_bwd")
    g_in = _tn(dproj, h2, "mix_in_dw", TN_IN)
    tok = g_put(dict(win=g_in, cp=g_cp, ap=g_ap, out=g_out))
    dx1, dsh2, dsc2, dgm = _nn_bwd_norm(dproj[None], w["win"], x1, gm, behind(sc2, tok), dx2, tm=TM_NN,
                                        name="mix_in_bwd")

    dab1, dgt1, g_d1 = _ffn_down_bwd_dw(dx1, y1, gt1, ab1, w["d1"], tm=TM_ROW, name="ffn1_down_bwd")
    tok = g_put(dict(d1=g_d1))
    g_gu1 = _tn(dab1, h1, "ffn1_up_dw", TN_FFN, tok)
    tok = g_put(dict(gu1=g_gu1))
    dx0, dsh1, dsc1, dg1 = _nn_bwd_norm(dab1, w["gu1"], x, g1, behind(sc1, tok), dx1, tm=TM_NN,
                                        name="ffn1_up_bwd")

    small = dict(mods=jnp.concatenate([dsh1, dsc1, dgt1, dsh2, dsc2, dgt2, dsh3, dsc3, dgt3], axis=0),
                 g1=dg1, gm=dgm, g2=dg2, gf=dgf, convw=dcw[0:3], sinks=dsink[:, 0:N_HEADS])
    return lsum, dx0, small


BIG = ("gu1", "d1", "win", "cp", "ap", "out", "gu2", "d2")
TRANSPOSED = ("gu1", "win", "gu2")
SMALL_ROWS = 24
R_MODS, R_G1, R_GM, R_G2, R_GF, R_CONV, R_SINK = 0, 9, 10, 11, 12, 13, 16


def _pad_to(a, rows, cols):
    return jnp.pad(a, ((0, rows - a.shape[0]), (0, cols - a.shape[1])))


def _pack_small(b_ada, g1, gm, g2, gf, conv, sinks):
    rows = [b_ada.reshape(N_MOD, D), g1.reshape(1, D), gm.reshape(1, D), g2.reshape(1, D), gf.reshape(1, D),
            _pad_to(conv.reshape(3, -1), 3, D), _pad_to(sinks.reshape(1, N_HEADS), 1, D)]
    return _pad_to(jnp.concatenate(rows, axis=0), SMALL_ROWS, D)


def _unpack_small(p, conv_cols):
    return dict(b_ada=p[R_MODS:R_MODS + N_MOD].reshape(1, N_MOD * D), g_ffn1=p[R_G1:R_G1 + 1],
                g_mix=p[R_GM:R_GM + 1], g_ffn2=p[R_G2:R_G2 + 1], g_final=p[R_GF],
                conv_w=p[R_CONV:R_CONV + 3, 0:conv_cols][None], sinks=p[R_SINK:R_SINK + 1, 0:N_HEADS])


def kernel(x, c, w_ada, b_ada, g_ffn1, w1_gu, w1_down, g_mix, w_in, conv_w, w_conv_proj, w_attn_proj, sinks, w_out, g_ffn2, w2_gu, w2_down, g_final, loss_target, m_w_ada, m_b_ada, m_g_ffn1, m_w1_gu, m_w1_down, m_g_mix, m_w_in, m_conv_w, m_w_conv_proj, m_w_attn_proj, m_sinks, m_w_out, m_g_ffn2, m_w2_gu, m_w2_down, m_g_final, v_w_ada, v_b_ada, v_g_ffn1, v_w1_gu, v_w1_down, v_g_mix, v_w_in, v_conv_w, v_w_conv_proj, v_w_attn_proj, v_sinks, v_w_out, v_g_ffn2, v_w2_gu, v_w2_down, v_g_final):
    me = 4 * lax.axis_index("x") + 2 * lax.axis_index("y") + lax.axis_index("c")
    ada_cols = w_ada.shape[2]
    conv_cols = conv_w.shape[2]

    native = dict(gu1=w1_gu[0], d1=w1_down[0], win=w_in[0], cp=w_conv_proj[0], ap=w_attn_proj[0], out=w_out[0],
                  gu2=w2_gu[0], d2=w2_down[0])

    def shard(n, token):
        a = _behind(native[n], token)
        return (a.T if n in TRANSPOSED else a).astype(BF)

    c_all, conv_all, _ = _exchange([c, _pad_to(conv_w[0], 8, conv_cols)], scatter=False, name="gather_cond")
    c_all = c_all.reshape(N_DEV, D)
    conv_full = conv_all[:, 0:3, :].transpose(1, 0, 2).reshape(3, D)

    b_cols = lax.dynamic_slice(b_ada, (0, me * ada_cols), (1, ada_cols))
    mods_cols = _mods_part(c_all, w_ada[0], b_cols, name="ada_mods")
    mods_all, mods_token = _exchange([mods_cols], scatter=False, name="gather_mods")
    mods = lax.dynamic_index_in_dim(mods_all, me, axis=1, keepdims=False).reshape(N_MOD, D)

    groups = dict(gu1=("gu1",), d1=("d1",), mix=("win", "cp", "ap", "out"), ffn2=("gu2", "d2"))
    in_flight = {}
    first = [shard("gu1", mods_token)]
    sems, srcs, lands, token = _tl_start(first, [_own_slot(s, me) for s in first], [[0]],
                                         name="gather_weights_start_gu1")
    in_flight["gu1"] = [sems[0], srcs, lands, None]
    rest = [n for n in BIG if n != "gu1"]
    shards = [shard(n, token) for n in rest]
    rest_groups = [[rest.index(n) for n in names] for g, names in groups.items() if g != "gu1"]
    sems, srcs, lands, rest_token = _tl_start(shards, [_own_slot(s, me) for s in shards], rest_groups,
                                              name="gather_weights_start_rest")
    for (g, names), gsems, idx in zip([kv for kv in groups.items() if kv[0] != "gu1"], sems, rest_groups):
        in_flight[g] = [gsems, [srcs[t] for t in idx], [lands[t] for t in idx], None]

    def forward(group, after):
        sems1, gsrcs, glands, _ = in_flight[group]
        sems2, gsrcs, glands = _tl_forward(gsrcs, glands, sems1, after, name="gather_weights_forward_" + group)
        in_flight[group] = [sems1, gsrcs, glands, sems2]

    forward_early = dict(d1="mix", mix="ffn2")

    def w_get(group, after):
        if group == "gu1":
            after = rest_token
        if in_flight[group][3] is None:
            forward(group, after)
        sems1, gsrcs, glands, sems2 = in_flight[group]
        landed = _tl_wait(gsrcs, glands, sems1, sems2, after, name="gather_weights_wait_" + group)
        if group in forward_early:
            forward(forward_early[group], landed[0])
        return {n: a.reshape(-1, D) for n, a in zip(groups[group], landed)}

    pending = []

    def g_put(gs):
        names = tuple(gs)
        srcs = [gs[n].reshape(N_DEV, -1, D) for n in names]
        lands = [_own_slot(lax.dynamic_index_in_dim(s, me, axis=0, keepdims=False), me) for s in srcs]
        sems, srcs, lands, tok = _split_start(srcs, lands, [list(range(len(names)))], scatter=True,
                                              name="scatter_grads_start_" + names[0])
        pending.append((names, sems[0], srcs, lands))
        return tok

    lsum, grad_x, small = _local_step(x[0], loss_target[0], mods, g_ffn1, g_mix, g_ffn2, g_final[None],
                                      _pad_to(conv_full, 8, D), sinks[0], w_get, g_put, rest_token)
    loss = lax.psum((0.5 / D) * jnp.sum(lsum), ("x", "y", "c"))

    packed = _pack_small(small["mods"], small["g1"], small["gm"], small["g2"], small["gf"], small["convw"],
                         small["sinks"])
    packed_all, _ = _exchange([packed], scatter=False, name="gather_small")
    gsmall = _sum8(packed_all, name="sum_small")

    w_of = dict(ada=w_ada, gu1=w1_gu, d1=w1_down, win=w_in, cp=w_conv_proj, ap=w_attn_proj, out=w_out, gu2=w2_gu,
                d2=w2_down)
    m_of = dict(ada=m_w_ada, gu1=m_w1_gu, d1=m_w1_down, win=m_w_in, cp=m_w_conv_proj, ap=m_w_attn_proj, out=m_w_out,
                gu2=m_w2_gu, d2=m_w2_down)
    v_of = dict(ada=v_w_ada, gu1=v_w1_gu, d1=v_w1_down, win=v_w_in, cp=v_w_conv_proj, ap=v_w_attn_proj, out=v_w_out,
                gu2=v_w2_gu, d2=v_w2_down)
    upd = {}
    after = gsmall
    for names, sems, srcs, lands in pending:
        parts = _split_wait(srcs, lands, sems, after, scatter=True, name="scatter_grads_wait_" + names[0])
        for n, p in zip(names, parts):
            if n in TRANSPOSED:
                res = _adam(jnp.swapaxes(w_of[n], 1, 2), p, jnp.swapaxes(m_of[n], 1, 2), jnp.swapaxes(v_of[n], 1, 2),
                            tm=TM_ADAM, name="adam_" + n)
                upd[n] = [jnp.swapaxes(t, 1, 2) for t in res]
            else:
                upd[n] = _adam(w_of[n], p, m_of[n], v_of[n], tm=TM_ADAM, name="adam_" + n)
        after = upd[names[-1]][1]

    gm_cols = lax.dynamic_slice(packed_all[:, R_MODS:R_MODS + N_MOD, :].reshape(N_DEV, N_MOD * D),
                                (0, me * ada_cols), (N_DEV, ada_cols))
    upd["ada"] = _adam(w_ada, _wada_grad(c_all.T, gm_cols, name="ada_dw"), m_w_ada, v_w_ada, tm=256, name="adam_ada")
    conv_g = lax.dynamic_slice(gsmall[R_CONV:R_CONV + 3], (0, me * conv_cols), (3, conv_cols))
    gsmall_own = gsmall.at[R_CONV:R_CONV + 3].set(_pad_to(conv_g, 3, D))
    small_upd = _adam(_pack_small(b_ada, g_ffn1, g_mix, g_ffn2, g_final, conv_w, sinks)[None], gsmall_own,
                      _pack_small(m_b_ada, m_g_ffn1, m_g_mix, m_g_ffn2, m_g_final, m_conv_w, m_sinks)[None],
                      _pack_small(v_b_ada, v_g_ffn1, v_g_mix, v_g_ffn2, v_g_final, v_conv_w, v_sinks)[None],
                      tm=SMALL_ROWS, name="adam_small")
    small_out = [_unpack_small(p[0], conv_cols) for p in small_upd]

    big_name = dict(w_ada="ada", w1_gu="gu1", w1_down="d1", w_in="win", w_conv_proj="cp", w_attn_proj="ap",
                    w_out="out", w2_gu="gu2", w2_down="d2")
    order = ("w_ada", "b_ada", "g_ffn1", "w1_gu", "w1_down", "g_mix", "w_in", "conv_w", "w_conv_proj", "w_attn_proj",
             "sinks", "w_out", "g_ffn2", "w2_gu", "w2_down", "g_final")
    outs = [loss, grad_x[None]]
    for kind in range(4):
        for n in order:
            outs.append(upd[big_name[n]][kind] if n in big_name else small_out[kind][n])
    return tuple(outs)
```

```python
import jax
import jax.numpy as jnp
from jax import lax
from jax.experimental import pallas as pl
from jax.experimental.pallas import tpu as pltpu

D = 1024
F = 2816
NIN = 6656
N_HEADS = 16
N_KV = 4
HEAD_DIM = 64
BLK = 128
N_MOD = 9
N_DEV = 8
EPS = 1e-6
NEG_INF = -1e30
ROPE_THETA = 10000.0
O_BG, O_CG, O_U, O_Q, O_K, O_V, O_ZC, O_ZA = 0, 1024, 2048, 3072, 4096, 4352, 4608, 5632

ADAM_LR = 0.001
ADAM_B1 = 0.9
ADAM_B2 = 0.999
ADAM_EPS = 1e-08
ADAM_WD = 0.01
ADAM_STEP = 10

BF = jnp.bfloat16
F32 = jnp.float32
VMEM_LIMIT = 56 * 1024 * 1024
MXU_N = 256
MESH = pl.DeviceIdType.MESH

NT = (((1,), (1,)), ((), ()))
TN = (((0,), (0,)), ((), ()))


def _cp(sem=None):
    return pltpu.CompilerParams(dimension_semantics=sem, vmem_limit_bytes=VMEM_LIMIT)


def _tile(n, pref):
    if n <= pref:
        return n
    for t in range(pref - pref % 16, 15, -16):
        if n % t == 0:
            return t
    raise ValueError((n, pref))


def _sigmoid(v):
    return 0.5 * jnp.tanh(0.5 * v) + 0.5


def _row(i):
    return (i, 0)


def _const2(*_):
    return (0, 0)


def _resident(shape):
    return pl.BlockSpec(shape, lambda *_: (0,) * len(shape), pipeline_mode=pl.Buffered(1))


def _norm_proj(x, g, sc, sh, wt, *, tm, tn, name):
    T, N = x.shape[0], wt.shape[0]
    tm = _tile(T, tm)

    def body(x_ref, g_ref, sc_ref, sh_ref, w_ref, h_ref, o_ref):
        xv = x_ref[...]
        r = lax.rsqrt(jnp.mean(xv * xv, axis=-1, keepdims=True) + EPS)
        hb = ((xv * r) * g_ref[...] * (1.0 + sc_ref[...]) + sh_ref[...]).astype(BF)
        h_ref[...] = hb
        for c0 in range(0, N, tn):
            cols = pl.ds(c0, tn)
            o_ref[:, cols] = lax.dot_general(hb, w_ref[cols, :], NT, preferred_element_type=F32).astype(BF)

    vec = pl.BlockSpec((1, D), _const2)
    return pl.pallas_call(
        body, name=name, grid=(T // tm,),
        in_specs=[pl.BlockSpec((tm, D), _row), vec, vec, vec, _resident((N, D))],
        out_specs=[pl.BlockSpec((tm, D), _row), pl.BlockSpec((tm, N), _row)],
        out_shape=[jax.ShapeDtypeStruct((T, D), BF), jax.ShapeDtypeStruct((T, N), BF)],
        compiler_params=_cp(("parallel",)),
    )(x, g, sc, sh, wt)


def _ffn_down_fwd(ab, wd, x, gt, *, tm, name):
    T = x.shape[0]
    tm = _tile(T, tm)

    def body(a_ref, b_ref, wd_ref, x_ref, gt_ref, xo_ref, y_ref):
        y = None
        for c0 in range(0, F, MXU_N):
            cols = pl.ds(c0, MXU_N)
            a = a_ref[:, cols].astype(F32)
            act = (a * _sigmoid(a) * b_ref[:, cols].astype(F32)).astype(BF)
            part = jnp.dot(act, wd_ref[cols, :], preferred_element_type=F32)
            y = part if y is None else y + part
        y_ref[...] = y.astype(BF)
        xo_ref[...] = x_ref[...] + (0.5 * gt_ref[...]) * y

    return pl.pallas_call(
        body, name=name, grid=(T // tm,),
        in_specs=[pl.BlockSpec((tm, F), lambda i: (i, 0)), pl.BlockSpec((tm, F), lambda i: (i, 1)),
                  _resident((F, D)), pl.BlockSpec((tm, D), _row), pl.BlockSpec((1, D), _const2)],
        out_specs=[pl.BlockSpec((tm, D), _row), pl.BlockSpec((tm, D), _row)],
        out_shape=[jax.ShapeDtypeStruct((T, D), F32), jax.ShapeDtypeStruct((T, D), BF)],
        compiler_params=_cp(("parallel",)),
    )(ab, ab, wd, x, gt)


def _ffn_fwd(x, g, sc, sh, gt, wgu, wd, final, *, tm, name):
    T = x.shape[0]
    tm = _tile(T, tm)
    last = final is not None

    def body(x_ref, g_ref, sc_ref, sh_ref, gt_ref, wgu_ref, wd_ref, *rest):
        if last:
            t_ref, gf_ref, h_ref, ab_ref, y_ref, dx_ref, ls_ref, dgf_ref = rest
        else:
            h_ref, ab_ref, y_ref, xo_ref = rest
        xv = x_ref[...]
        r = lax.rsqrt(jnp.mean(xv * xv, axis=-1, keepdims=True) + EPS)
        hb = ((xv * r) * g_ref[...] * (1.0 + sc_ref[...]) + sh_ref[...]).astype(BF)
        h_ref[...] = hb
        y = None
        for c0 in range(0, F, MXU_N):
            a = lax.dot_general(hb, wgu_ref[pl.ds(c0, MXU_N), :], NT, preferred_element_type=F32)
            b = lax.dot_general(hb, wgu_ref[pl.ds(F + c0, MXU_N), :], NT, preferred_element_type=F32)
            ab = a.astype(BF)
            bb = b.astype(BF)
            ab_ref[:, pl.ds(c0, MXU_N)] = ab
            ab_ref[:, pl.ds(F + c0, MXU_N)] = bb
            a = ab.astype(F32)
            act = (a * _sigmoid(a) * bb.astype(F32)).astype(BF)
            part = jnp.dot(act, wd_ref[pl.ds(c0, MXU_N), :], preferred_element_type=F32)
            y = part if y is None else y + part
        y_ref[...] = y.astype(BF)
        xo = xv + (0.5 * gt_ref[...]) * y
        if not last:
            xo_ref[...] = xo
            return

        @pl.when(pl.program_id(0) == 0)
        def _():
            ls_ref[...] = jnp.zeros_like(ls_ref)
            dgf_ref[...] = jnp.zeros_like(dgf_ref)
        gv = gf_ref[...]
        r = lax.rsqrt(jnp.mean(xo * xo, axis=-1, keepdims=True) + EPS)
        xh = xo * r
        e = xh * gv - t_ref[...]
        ls_ref[...] += jnp.sum(e * e, axis=0, keepdims=True)
        dy = e * (1.0 / D)
        dgf_ref[...] += jnp.sum(dy * xh, axis=0, keepdims=True)
        dxh = dy * gv
        dx_ref[...] = r * (dxh - xh * jnp.mean(dxh * xh, axis=-1, keepdims=True))

    vec = pl.BlockSpec((1, D), _const2)
    rowspec = pl.BlockSpec((tm, D), _row)
    in_specs = [rowspec, vec, vec, vec, vec, _resident((2 * F, D)), _resident((F, D))]
    out_specs = [rowspec, pl.BlockSpec((tm, 2 * F), _row), rowspec, rowspec]
    out_shape = [jax.ShapeDtypeStruct((T, D), BF), jax.ShapeDtypeStruct((T, 2 * F), BF),
                 jax.ShapeDtypeStruct((T, D), BF), jax.ShapeDtypeStruct((T, D), F32)]
    args = [x, g, sc, sh, gt, wgu, wd]
    if last:
        in_specs += [rowspec, vec]
        out_specs += [vec, vec]
        out_shape += [jax.ShapeDtypeStruct((1, D), F32)] * 2
        args += list(final)
    return pl.pallas_call(
        body, name=name, grid=(T // tm,),
        in_specs=in_specs, out_specs=out_specs, out_shape=out_shape,
        compiler_params=_cp(("arbitrary",) if last else ("parallel",)),
    )(*args)


def _ffn_down_bwd_dw(dxo, y, gt, ab, wd, *, tm, name):
    T = dxo.shape[0]
    tm = _tile(T, tm)
    nt = T // tm
    hw = F // 2
    chunks = [(c0, min(MXU_N, hw - c0)) for c0 in range(0, hw, MXU_N)]

    def body(dxo_ref, y_ref, gt_ref, a_ref, b_ref, wd_ref, dab_ref, dgt_ref, dwd_ref, dys, dyt, acc, stage, sem):
        i, j = pl.program_id(0), pl.program_id(1)

        @pl.when(jnp.logical_and(i == 0, j == 0))
        def _():
            dgt_ref[...] = jnp.zeros_like(dgt_ref)

        @pl.when(i == 0)
        def _():
            acc[j] = jnp.zeros((D, hw), F32)

        @pl.when(j == 0)
        def _():
            dxv = dxo_ref[...]
            dgt_ref[...] += 0.5 * jnp.sum(dxv * y_ref[...].astype(F32), axis=0, keepdims=True)
            dyf = (0.5 * gt_ref[...]) * dxv
            dys[...] = dyf.astype(BF)
            dyt[...] = dyf.T.astype(BF)

        dy = dys[...]
        dy_t = dyt[...]

        def dact_of(c0, cw):
            w_rows = pl.ds(pl.multiple_of(j * hw + c0, 128), cw)
            return lax.dot_general(dy, wd_ref[w_rows, :], NT, preferred_element_type=F32)

        ahead = dact_of(*chunks[0])
        for n, (c0, cw) in enumerate(chunks):
            cols = pl.ds(c0, cw)
            dact = ahead
            if n + 1 < len(chunks):
                ahead = dact_of(*chunks[n + 1])
            a = a_ref[:, cols].astype(F32)
            b = b_ref[:, cols].astype(F32)
            s = _sigmoid(a)
            silu = a * s
            dab_ref[0, :, cols] = (dact * b * (s * (1.0 + a * (1.0 - s)))).astype(BF)
            dab_ref[1, :, cols] = (dact * silu).astype(BF)
            acc[j, :, cols] += jnp.dot(dy_t, (silu * b).astype(BF), preferred_element_type=F32)

        @pl.when(i == nt - 1)
        def _():
            for c0, cw in chunks:
                stage[0:cw, :] = acc[j, :, pl.ds(c0, cw)].T.astype(BF)
                out = pltpu.make_async_copy(stage.at[pl.ds(0, cw)],
                                            dwd_ref.at[pl.ds(pl.multiple_of(j * hw + c0, 128), cw)], sem)
                out.start()
                out.wait()

    vec = pl.BlockSpec((1, D), _const2)
    rowspec = pl.BlockSpec((tm, D), lambda i, j: (i, 0))
    return pl.pallas_call(
        body, name=name, grid=(nt, 2),
        in_specs=[rowspec, rowspec, vec, pl.BlockSpec((tm, hw), lambda i, j: (i, j)),
                  pl.BlockSpec((tm, hw), lambda i, j: (i, j + 2)), _resident((F, D))],
        out_specs=[pl.BlockSpec((2, tm, hw), lambda i, j: (0, i, j)), vec, pl.BlockSpec(memory_space=pl.ANY)],
        out_shape=[jax.ShapeDtypeStruct((2, T, F), BF), jax.ShapeDtypeStruct((1, D), F32),
                   jax.ShapeDtypeStruct((F, D), BF)],
        scratch_shapes=[pltpu.VMEM((tm, D), BF), pltpu.VMEM((D, tm), BF), pltpu.VMEM((2, D, hw), F32),
                        pltpu.VMEM((MXU_N, D), BF), pltpu.SemaphoreType.DMA(())],
        compiler_params=_cp(("arbitrary", "arbitrary")),
    )(dxo, y, gt, ab, ab, wd)


def _tn_matmul(a, b, token=None, *, tn, tk, name):
    S, T, Ns = a.shape
    tn, tk = _tile(Ns, tn), _tile(T, tk)
    nk, njs = T // tk, Ns // tn
    deps = [] if token is None else [token]

    def body(a_ref, b_ref, *rest):
        o_ref, acc = rest[len(deps):]
        k = pl.program_id(1)

        @pl.when(k == 0)
        def _():
            acc[...] = jnp.zeros_like(acc)
        acc[...] += lax.dot_general(a_ref[0], b_ref[...], TN, preferred_element_type=F32)

        @pl.when(k == nk - 1)
        def _():
            o_ref[...] = acc[...].astype(BF)

    return pl.pallas_call(
        body, name=name, grid=(S * njs, nk),
        in_specs=[pl.BlockSpec((1, tk, tn), lambda j, k: (j // njs, k, j % njs)),
                  pl.BlockSpec((tk, D), lambda j, k: (k, 0))] + [pl.BlockSpec(memory_space=pl.ANY)] * len(deps),
        out_specs=pl.BlockSpec((tn, D), lambda j, k: (j, 0)),
        out_shape=jax.ShapeDtypeStruct((S * Ns, D), BF),
        scratch_shapes=[pltpu.VMEM((tn, D), F32)],
        compiler_params=_cp(("parallel", "arbitrary")),
    )(a, b, *deps)


def _nn_bwd_norm(da, w, x, g, sc, dxo, *, tm, name):
    S, T, Ks = da.shape
    tm = _tile(T, tm)
    rc = _tile(tm, 256)

    def body(da_ref, w_ref, x_ref, g_ref, sc_ref, dxo_ref, dx_ref, dsh_ref, dsc_ref, dg_ref, acc):
        @pl.when(pl.program_id(0) == 0)
        def _():
            dsh_ref[...] = jnp.zeros_like(dsh_ref)
            dsc_ref[...] = jnp.zeros_like(dsc_ref)
            dg_ref[...] = jnp.zeros_like(dg_ref)

        d = jnp.dot(da_ref[0], w_ref[0:Ks, :], preferred_element_type=F32)
        for s in range(1, S):
            d = d + jnp.dot(da_ref[s], w_ref[s * Ks:(s + 1) * Ks, :], preferred_element_type=F32)
        acc[...] = d
        gv = g_ref[...]
        sc1 = 1.0 + sc_ref[...]
        dsh = jnp.zeros((1, D), F32)
        dsc = jnp.zeros((1, D), F32)
        dg = jnp.zeros((1, D), F32)
        for r0 in range(0, tm, rc):
            rows = pl.ds(r0, rc)
            u = acc[rows, :]
            xv = x_ref[rows, :]
            r = lax.rsqrt(jnp.mean(xv * xv, axis=-1, keepdims=True) + EPS)
            xh = xv * r
            dsh = dsh + jnp.sum(u, axis=0, keepdims=True)
            dsc = dsc + jnp.sum(u * (xh * gv), axis=0, keepdims=True)
            us = u * sc1
            dg = dg + jnp.sum(us * xh, axis=0, keepdims=True)
            dxh = us * gv
            dx_ref[rows, :] = dxo_ref[rows, :] + r * (dxh - xh * jnp.mean(dxh * xh, axis=-1, keepdims=True))
        dsh_ref[...] += dsh
        dsc_ref[...] += dsc
        dg_ref[...] += dg

    vec = pl.BlockSpec((1, D), _const2)
    rowspec = pl.BlockSpec((tm, D), _row)
    return pl.pallas_call(
        body, name=name, grid=(T // tm,),
        in_specs=[pl.BlockSpec((S, tm, Ks), lambda i: (0, i, 0)), _resident((S * Ks, D)), rowspec, vec, vec, rowspec],
        out_specs=[rowspec, vec, vec, vec],
        out_shape=[jax.ShapeDtypeStruct((T, D), F32)] + [jax.ShapeDtypeStruct((1, D), F32)] * 3,
        scratch_shapes=[pltpu.VMEM((tm, D), F32)],
        compiler_params=_cp(("arbitrary",)),
    )(da, w, x, g, sc, dxo)


def _rope(t, cos, sin_signed, lt32, inverse=False):
    sel = jnp.where(lt32, pltpu.roll(t, 96, 1), pltpu.roll(t, 32, 1))
    return t * cos - sel * sin_signed if inverse else t * cos + sel * sin_signed


def _rope_tables(T, token=None):
    inv = 1.0 / (ROPE_THETA ** (jnp.arange(0, HEAD_DIM, 2, dtype=F32) / HEAD_DIM))
    ang = _behind(jnp.arange(T, dtype=F32)[:, None] * inv[None, :], token)
    cos, sin = jnp.cos(ang), jnp.sin(ang)
    cos128 = jnp.tile(cos, (1, 4))
    sin128 = jnp.tile(jnp.concatenate([-sin, sin], axis=1), (1, 2))
    return cos128, sin128


QSCALE = HEAD_DIM ** -0.5


def _lane_masks(rows):
    lane = lax.broadcasted_iota(jnp.int32, (rows, 128), 1)
    return (lane % HEAD_DIM) < (HEAD_DIM // 2), [lane < HEAD_DIM, lane >= HEAD_DIM]


def _attn_bias():
    qi = lax.broadcasted_iota(jnp.int32, (4 * BLK, 2 * BLK), 0) % BLK
    kj = lax.broadcasted_iota(jnp.int32, (4 * BLK, 2 * BLK), 1)
    band = (kj > qi) & (kj <= qi + BLK)
    return jnp.stack([jnp.where(band & (kj >= BLK), 0.0, NEG_INF), jnp.where(band, 0.0, NEG_INF)]).astype(F32)


def _attn_prep(proj, cos, sin, *, name):
    T = proj.shape[0]
    tm = _tile(T, 4 * BLK)

    def body(q_ref, k_ref, c_ref, s_ref, qs_ref, kr_ref):
        lt32, halves = _lane_masks(BLK)
        for b in range(tm // BLK):
            rows = pl.ds(b * BLK, BLK)
            cc, sc = c_ref[rows, :], s_ref[rows, :]
            qr = [_rope(q_ref[rows, p * 128:(p + 1) * 128].astype(F32), cc, sc, lt32) * QSCALE for p in range(8)]
            for g in range(N_KV):
                qs_ref[g, pl.ds(4 * b * BLK, 4 * BLK), :] = _stack_heads(qr, g, halves).astype(BF)
            kr_ref[rows, :] = jnp.concatenate([_rope(k_ref[rows, r * 128:(r + 1) * 128].astype(F32), cc, sc, lt32)
                                               for r in range(2)], axis=1).astype(BF)

    tab = pl.BlockSpec((tm, 128), _row)
    return pl.pallas_call(
        body, name=name, grid=(T // tm,),
        in_specs=[pl.BlockSpec((tm, D), lambda n: (n, O_Q // D)), pl.BlockSpec((tm, 256), lambda n: (n, O_K // 256)),
                  tab, tab],
        out_specs=[pl.BlockSpec((N_KV, 4 * tm, 128), lambda n: (0, n, 0)), pl.BlockSpec((tm, 256), _row)],
        out_shape=[jax.ShapeDtypeStruct((N_KV, 4 * T, 128), BF), jax.ShapeDtypeStruct((T, 256), BF)],
        compiler_params=_cp(("parallel",)),
    )(proj, proj, cos, sin)


def _attn_specs():
    prev = lambda n: jnp.maximum(n - 1, 0)
    return [pl.BlockSpec((N_KV, 4 * BLK, 128), lambda n: (0, n, 0)),
            pl.BlockSpec((BLK, 256), _row), pl.BlockSpec((BLK, 256), lambda n: (prev(n), 0)),
            pl.BlockSpec((BLK, 256), lambda n: (n, O_V // 256)),
            pl.BlockSpec((BLK, 256), lambda n: (prev(n), O_V // 256)),
            pl.BlockSpec((1, 4 * BLK, 2 * BLK), lambda n: (jnp.minimum(n, 1), 0, 0)),
            pl.BlockSpec(memory_space=pltpu.SMEM)]


def _bands(kc_ref, kp_ref, vc_ref, vp_ref):
    kb, vb = [], []
    for r in range(2):
        cols = slice(r * 128, (r + 1) * 128)
        kb.append(jnp.concatenate([kp_ref[:, cols], kc_ref[:, cols]], axis=0))
        vb.append(jnp.concatenate([vp_ref[:, cols], vc_ref[:, cols]], axis=0))
    return kb, vb


def _sink_rows(sink_ref, g):
    return jnp.concatenate([jnp.full((BLK, 128), sink_ref[4 * g + hh], F32) for hh in range(4)], axis=0)


def _both(t):
    return jnp.concatenate([t, t], axis=1)


def _unstack_heads(t, g, halves, acc):
    half = g % 2
    for hh in range(4):
        h = 4 * g + hh
        th = jnp.where(halves[half], t[hh * BLK:(hh + 1) * BLK], 0.0)
        if h % 2 != half:
            th = pltpu.roll(th, HEAD_DIM, 1)
        acc[h // 2] = acc[h // 2] + th


def _stack_heads(chunks, g, halves):
    half = g % 2
    parts = []
    for hh in range(4):
        h = 4 * g + hh
        t = chunks[h // 2]
        if h % 2 != half:
            t = pltpu.roll(t, HEAD_DIM, 1)
        parts.append(jnp.where(halves[half], t, 0.0))
    return jnp.concatenate(parts, axis=0)


def _attn_fwd(qs, kr, proj, bias, sinks, *, name):
    T = proj.shape[0]
    nb = T // BLK

    def body(qs_ref, kc_ref, kp_ref, vc_ref, vp_ref, bias_ref, sink_ref, o_ref, lse_ref):
        _, h128 = _lane_masks(BLK)
        _, h256 = _lane_masks(2 * BLK)
        _, h512 = _lane_masks(4 * BLK)
        kb, vb = _bands(kc_ref, kp_ref, vc_ref, vp_ref)
        outs = [jnp.zeros((BLK, 128), F32) for _ in range(8)]
        groups = range(N_KV)
        bias = bias_ref[0]
        sink = [_sink_rows(sink_ref, g) for g in groups]
        s = [lax.dot_general(qs_ref[g], kb[g // 2], NT, preferred_element_type=F32) + bias for g in groups]
        m = [jnp.maximum(jnp.broadcast_to(jnp.max(s[g], axis=-1, keepdims=True), (4 * BLK, 128)), sink[g])
             for g in groups]
        p = [jnp.exp(s[g] - _both(m[g])).astype(BF) for g in groups]
        vg = [jnp.where(h256[g % 2], vb[g // 2].astype(F32), 1.0).astype(BF) for g in groups]
        o = [jnp.dot(p[g], vg[g], preferred_element_type=F32) for g in groups]
        denom = [jnp.where(h512[g % 2], pltpu.roll(o[g], HEAD_DIM, 1), o[g]) + jnp.exp(sink[g] - m[g]) for g in groups]
        for g in groups:
            lse_ref[g] = m[g] + jnp.log(denom[g])
            _unstack_heads(o[g] * (1.0 / denom[g]), g, h128, outs)
        o_ref[...] = jnp.concatenate(outs, axis=1).astype(BF)

    return pl.pallas_call(
        body, name=name, grid=(nb,),
        in_specs=_attn_specs(),
        out_specs=[pl.BlockSpec((BLK, D), _row), pl.BlockSpec((N_KV, 4 * BLK, 128), lambda n: (0, n, 0))],
        out_shape=[jax.ShapeDtypeStruct((T, D), BF), jax.ShapeDtypeStruct((N_KV, 4 * T, 128), F32)],
        compiler_params=_cp(("parallel",)),
    )(qs, kr, kr, proj, proj, bias, sinks)


def _attn_bwd(qs, kr, proj, bias, sinks, lse, o, do, cos, sin, dproj, *, name):
    T = proj.shape[0]
    nb = T // BLK

    def body(qs_ref, kc_ref, kp_ref, vc_ref, vp_ref, bias_ref, sink_ref, lse_ref, o_ref, do_ref,
             cc_ref, sc_ref, cp_ref, sp_ref, dproj_ref, dq_ref, dkc_ref, dkp_ref, dvc_ref, dvp_ref, dsink_ref):
        @pl.when(pl.program_id(0) == 0)
        def _():
            dsink_ref[...] = jnp.zeros_like(dsink_ref)
        lt32, h128 = _lane_masks(BLK)
        kb, vb = _bands(kc_ref, kp_ref, vc_ref, vp_ref)
        oc = [o_ref[:, p * 128:(p + 1) * 128].astype(F32) for p in range(8)]
        doc = [do_ref[:, p * 128:(p + 1) * 128].astype(F32) for p in range(8)]
        dqs = [jnp.zeros((BLK, 128), F32) for _ in range(8)]
        lane1 = lax.broadcasted_iota(jnp.int32, (1, 128), 1)
        dsink = jnp.zeros((1, 128), F32)
        groups = range(N_KV)
        bias = bias_ref[0]
        q = [qs_ref[g] for g in groups]
        lse_g = [lse_ref[g] for g in groups]
        s = [lax.dot_general(q[g], kb[g // 2], NT, preferred_element_type=F32) + bias for g in groups]
        dos = [_stack_heads(doc, g, h128) for g in groups]
        dosb = [t.astype(BF) for t in dos]
        dp = [lax.dot_general(dosb[g], vb[g // 2], NT, preferred_element_type=F32) for g in groups]
        delta = [jnp.broadcast_to(jnp.sum(dos[g] * _stack_heads(oc, g, h128), axis=-1, keepdims=True), (4 * BLK, 128))
                 for g in groups]
        p = [jnp.exp(s[g] - _both(lse_g[g])) for g in groups]
        ds = [(p[g] * (dp[g] - _both(delta[g]))).astype(BF) for g in groups]
        pb = [t.astype(BF) for t in p]
        dvg = [lax.dot_general(pb[g], dosb[g], TN, preferred_element_type=F32) for g in groups]
        dkg = [lax.dot_general(ds[g], q[g], TN, preferred_element_type=F32) for g in groups]
        dqg = [jnp.dot(ds[g], kb[g // 2], preferred_element_type=F32) * QSCALE for g in groups]
        dvr = [dvg[0] + dvg[1], dvg[2] + dvg[3]]
        dkr = [dkg[0] + dkg[1], dkg[2] + dkg[3]]
        for g in groups:
            _unstack_heads(dqg[g], g, h128, dqs)
            dsk = -jnp.exp(_sink_rows(sink_ref, g) - lse_g[g]) * delta[g]
            for hh in range(4):
                val = jnp.sum(dsk[hh * BLK:(hh + 1) * BLK], axis=0, keepdims=True)
                dsink = dsink + jnp.where(lane1 == 4 * g + hh, val, 0.0)
        cc, sc, cp, sp = cc_ref[...], sc_ref[...], cp_ref[...], sp_ref[...]
        dsink_ref[...] += dsink
        dq_ref[...] = jnp.concatenate([_rope(t, cc, sc, lt32, inverse=True) for t in dqs], axis=1).astype(BF)
        dkp_ref[...] = jnp.concatenate([_rope(t[:BLK], cp, sp, lt32, inverse=True) for t in dkr], axis=1)
        dkc_ref[...] = jnp.concatenate([_rope(t[BLK:], cc, sc, lt32, inverse=True) for t in dkr], axis=1)
        dvp_ref[...] = jnp.concatenate([t[:BLK] for t in dvr], axis=1)
        dvc_ref[...] = jnp.concatenate([t[BLK:] for t in dvr], axis=1)

    kv = pl.BlockSpec((BLK, 256), _row)
    tc = pl.BlockSpec((BLK, 128), _row)
    tp = pl.BlockSpec((BLK, 128), lambda n: (jnp.maximum(n - 1, 0), 0))
    return pl.pallas_call(
        body, name=name, grid=(nb,),
        in_specs=_attn_specs() + [pl.BlockSpec((N_KV, 4 * BLK, 128), lambda n: (0, n, 0)),
                                  pl.BlockSpec((BLK, D), _row), pl.BlockSpec((BLK, D), _row), tc, tc, tp, tp,
                                  pl.BlockSpec(memory_space=pl.ANY)],
        out_specs=[pl.BlockSpec((BLK, D), lambda n: (n, O_Q // D)), kv, kv, kv, kv, pl.BlockSpec((1, 128), _const2)],
        out_shape=[jax.ShapeDtypeStruct(dproj.shape, BF)] + [jax.ShapeDtypeStruct((T, 256), F32)] * 4
        + [jax.ShapeDtypeStruct((1, 128), F32)],
        input_output_aliases={14: 0},
        compiler_params=_cp(("arbitrary",)),
    )(qs, kr, kr, proj, proj, bias, sinks, lse, o, do, cos, sin, cos, sin, dproj)


def _dkv_combine(dkc, dkp, dvc, dvp, dproj, *, name):
    T = dkc.shape[0]
    nb = T // BLK
    tm = _tile(T, 4 * BLK)
    bpt = tm // BLK
    nt = T // tm

    def body(dkc_ref, dkp_ref, dkn_ref, dvc_ref, dvp_ref, dvn_ref, dproj_ref, o_ref):
        keep = jnp.where(pl.program_id(0) == nt - 1, 0.0, 1.0)

        def shifted(prev_ref, next_ref):
            nxt = keep * next_ref[...]
            return nxt if bpt == 1 else jnp.concatenate([prev_ref[BLK:, :], nxt], axis=0)

        o_ref[:, 0:256] = (dkc_ref[...] + shifted(dkp_ref, dkn_ref)).astype(BF)
        o_ref[:, 256:512] = (dvc_ref[...] + shifted(dvp_ref, dvn_ref)).astype(BF)

    cur = pl.BlockSpec((tm, 256), _row)
    nxt = pl.BlockSpec((BLK, 256), lambda i: (jnp.minimum((i + 1) * bpt, nb - 1), 0))
    return pl.pallas_call(
        body, name=name, grid=(nt,),
        in_specs=[cur, cur, nxt, cur, cur, nxt, pl.BlockSpec(memory_space=pl.ANY)],
        out_specs=pl.BlockSpec((tm, 512), lambda i: (i, O_K // 512)),
        out_shape=jax.ShapeDtypeStruct(dproj.shape, BF),
        input_output_aliases={6: 0},
        compiler_params=_cp(("parallel",)),
    )(dkc, dkp, dkp, dvc, dvp, dvp, dproj)


HALO = 16


def _conv_shifts(cu, hprev, tm):
    row = lax.broadcasted_iota(jnp.int32, (8, cu.shape[1]), 0)
    h1 = hprev[HALO - 1:HALO, :]
    h2 = hprev[HALO - 2:HALO - 1, :]
    m1 = pltpu.roll(cu, 1, 0)
    m2 = pltpu.roll(cu, 2, 0)
    m1 = jnp.concatenate([jnp.where(row == 0, h1, m1[0:8]), m1[8:]], axis=0)
    m2 = jnp.concatenate([jnp.where(row == 0, h2, jnp.where(row == 1, h1, m2[0:8])), m2[8:]], axis=0)
    return m1, m2


def _mixer_mid_fwd(proj, attn, wcp, wap, wout, convw, x, gt, *, tm, name):
    T = x.shape[0]
    tm = _tile(T, tm)
    hb = tm // HALO

    def body(bg_ref, cg_ref, u_ref, hcg_ref, hu_ref, zc0_ref, zc1_ref, za0_ref, za1_ref, at_ref,
             wcp_ref, wap_ref, wout_ref, cw_ref, x_ref, gt_ref,
             x2_ref, gc_ref, yc_ref, ya_ref, mg_ref, o_ref):
        first = jnp.where(pl.program_id(0) == 0, 0.0, 1.0)
        cu = cg_ref[...].astype(F32) * u_ref[...].astype(F32)
        hprev = first * (hcg_ref[...].astype(F32) * hu_ref[...].astype(F32))
        m1, m2 = _conv_shifts(cu, hprev, tm)
        cv = cw_ref[0:1, :] * m2 + cw_ref[1:2, :] * m1 + cw_ref[2:3, :] * cu
        gc = (bg_ref[...].astype(F32) * cv).astype(BF)
        gc_ref[...] = gc
        yc = jnp.dot(gc, wcp_ref[...], preferred_element_type=F32)
        ya = jnp.dot(at_ref[...], wap_ref[...], preferred_element_type=F32)
        yc_ref[...] = yc.astype(BF)
        ya_ref[...] = ya.astype(BF)
        zc = jnp.concatenate([zc0_ref[...], zc1_ref[...]], axis=1).astype(F32)
        za = jnp.concatenate([za0_ref[...], za1_ref[...]], axis=1).astype(F32)
        mg = (_sigmoid(zc) * yc + _sigmoid(za) * ya).astype(BF)
        mg_ref[...] = mg
        o = jnp.dot(mg, wout_ref[...], preferred_element_type=F32)
        o_ref[...] = o.astype(BF)
        x2_ref[...] = x_ref[...] + gt_ref[...] * o

    wspec = pl.BlockSpec((D, D), _const2)
    rowspec = pl.BlockSpec((tm, D), _row)
    return pl.pallas_call(
        body, name=name, grid=(T // tm,),
        in_specs=[_col(tm, O_BG), _col(tm, O_CG), _col(tm, O_U), _halo_prev(hb, O_CG), _halo_prev(hb, O_U),
                  _col(tm, O_ZC, 512), _col(tm, O_ZC + 512, 512), _col(tm, O_ZA, 512), _col(tm, O_ZA + 512, 512),
                  rowspec, wspec, wspec, wspec, pl.BlockSpec((8, D), _const2), rowspec, pl.BlockSpec((1, D), _const2)],
        out_specs=[rowspec] * 6,
        out_shape=[jax.ShapeDtypeStruct((T, D), F32)] + [jax.ShapeDtypeStruct((T, D), BF)] * 5,
        compiler_params=_cp(("parallel",)),
    )(proj, proj, proj, proj, proj, proj, proj, proj, proj, attn, wcp, wap, wout, convw, x, gt)


def _col(tm, c, w=D):
    assert c % w == 0
    return pl.BlockSpec((tm, w), lambda i: (i, c // w))


def _halo_prev(hb, c):
    return pl.BlockSpec((HALO, D), lambda i: (jnp.maximum(i * hb - 1, 0), c // D))


def _halo_next(hb, nblk, c=0):
    return pl.BlockSpec((HALO, D), lambda i: (jnp.minimum((i + 1) * hb, nblk - 1), c // D))


def _mixer_mid_bwd(dx2, gt, o, proj, yc, ya, wout, wcp, wap, *, tm, name):
    T = dx2.shape[0]
    tm = _tile(T, tm)
    nt = T // tm

    def body(dx_ref, gt_ref, o_ref, zc0_ref, zc1_ref, za0_ref, za1_ref, yc_ref, ya_ref, wout_ref, wcp_ref, wap_ref,
             dout_ref, dyc_ref, dya_ref, dgc_ref, dat_ref, dproj_ref, dgt_ref, dzs, sems):
        i = pl.program_id(0)
        slot = lax.rem(i, 2)

        def slab_copy(step, s):
            return pltpu.make_async_copy(
                dzs.at[s], dproj_ref.at[pl.ds(pl.multiple_of(step * tm, tm), tm), pl.ds(O_ZC, 2 * D)], sems.at[s])

        @pl.when(i == 0)
        def _():
            dgt_ref[...] = jnp.zeros_like(dgt_ref)

        dxv = dx_ref[...]
        dgt_ref[...] += jnp.sum(dxv * o_ref[...].astype(F32), axis=0, keepdims=True)
        dout = (gt_ref[...] * dxv).astype(BF)
        dout_ref[...] = dout
        dmg = lax.dot_general(dout, wout_ref[...], NT, preferred_element_type=F32)
        sc = _sigmoid(jnp.concatenate([zc0_ref[...], zc1_ref[...]], axis=1).astype(F32))
        sa = _sigmoid(jnp.concatenate([za0_ref[...], za1_ref[...]], axis=1).astype(F32))
        dyc = (dmg * sc).astype(BF)
        dya = (dmg * sa).astype(BF)
        dyc_ref[...] = dyc
        dya_ref[...] = dya
        dzs[slot, :, 0:D] = (dmg * yc_ref[...].astype(F32) * (sc * (1.0 - sc))).astype(BF)
        dzs[slot, :, D:2 * D] = (dmg * ya_ref[...].astype(F32) * (sa * (1.0 - sa))).astype(BF)
        slab_copy(i, slot).start()
        dgc_ref[...] = lax.dot_general(dyc, wcp_ref[...], NT, preferred_element_type=F32).astype(BF)
        dat_ref[...] = lax.dot_general(dya, wap_ref[...], NT, preferred_element_type=F32).astype(BF)

        @pl.when(i > 0)
        def _():
            slab_copy(i - 1, 1 - slot).wait()

        @pl.when(i == nt - 1)
        def _():
            slab_copy(i, slot).wait()

    def zcol(c):
        return pl.BlockSpec((tm, 512), lambda i: (i, c // 512))

    wspec = pl.BlockSpec((D, D), _const2)
    rowspec = pl.BlockSpec((tm, D), _row)
    vec = pl.BlockSpec((1, D), _const2)
    return pl.pallas_call(
        body, name=name, grid=(nt,),
        in_specs=[rowspec, vec, rowspec, zcol(O_ZC), zcol(O_ZC + 512), zcol(O_ZA), zcol(O_ZA + 512),
                  rowspec, rowspec, wspec, wspec, wspec],
        out_specs=[rowspec] * 5 + [pl.BlockSpec(memory_space=pl.ANY), vec],
        out_shape=[jax.ShapeDtypeStruct((T, D), BF)] * 5 + [jax.ShapeDtypeStruct((T, NIN), BF),
                                                            jax.ShapeDtypeStruct((1, D), F32)],
        scratch_shapes=[pltpu.VMEM((2, tm, 2 * D), BF), pltpu.SemaphoreType.DMA((2,))],
        compiler_params=_cp(("arbitrary",)),
    )(dx2, gt, o, proj, proj, proj, proj, yc, ya, wout, wcp, wap)


def _conv_bwd(dgc, proj, convw, dproj, *, tm, name):
    T = dgc.shape[0]
    tm = _tile(T, tm)
    hb = tm // HALO
    nblk = T // HALO
    nt = T // tm

    def body(dgc_ref, ndgc_ref, bg_ref, nbg_ref, cg_ref, u_ref, hcg_ref, hu_ref, cw_ref, dproj_ref, dp_ref, dcw_ref):
        i = pl.program_id(0)

        @pl.when(i == 0)
        def _():
            dcw_ref[...] = jnp.zeros_like(dcw_ref)
        first = jnp.where(i == 0, 0.0, 1.0)
        last = jnp.where(i == nt - 1, 0.0, 1.0)
        cg = cg_ref[...].astype(F32)
        u = u_ref[...].astype(F32)
        bg = bg_ref[...].astype(F32)
        dg = dgc_ref[...].astype(F32)
        cu = cg * u
        hprev = first * (hcg_ref[...].astype(F32) * hu_ref[...].astype(F32))
        m1, m2 = _conv_shifts(cu, hprev, tm)
        w0, w1, w2 = cw_ref[0:1, :], cw_ref[1:2, :], cw_ref[2:3, :]
        cv = w0 * m2 + w1 * m1 + w2 * cu
        dcv = dg * bg
        nxt = last * (ndgc_ref[...].astype(F32) * nbg_ref[...].astype(F32))
        n0, n1 = nxt[0:1, :], nxt[1:2, :]
        row = lax.broadcasted_iota(jnp.int32, (8, D), 0)
        p1 = pltpu.roll(dcv, tm - 1, 0)
        p2 = pltpu.roll(dcv, tm - 2, 0)
        p1 = jnp.concatenate([p1[:tm - 8], jnp.where(row == 7, n0, p1[tm - 8:])], axis=0)
        p2 = jnp.concatenate([p2[:tm - 8], jnp.where(row == 7, n1, jnp.where(row == 6, n0, p2[tm - 8:]))], axis=0)
        dcu = w2 * dcv + w1 * p1 + w0 * p2
        dp_ref[:, 0:D] = (dg * cv).astype(BF)
        dp_ref[:, D:2 * D] = (dcu * u).astype(BF)
        dp_ref[:, 2 * D:3 * D] = (dcu * cg).astype(BF)
        dcw_ref[0:1, :] += jnp.sum(dcv * m2, axis=0, keepdims=True)
        dcw_ref[1:2, :] += jnp.sum(dcv * m1, axis=0, keepdims=True)
        dcw_ref[2:3, :] += jnp.sum(dcv * cu, axis=0, keepdims=True)

    rowspec = pl.BlockSpec((tm, D), _row)
    cw = pl.BlockSpec((8, D), _const2)
    return pl.pallas_call(
        body, name=name, grid=(nt,),
        in_specs=[rowspec, _halo_next(hb, nblk), _col(tm, O_BG), _halo_next(hb, nblk, O_BG),
                  _col(tm, O_CG), _col(tm, O_U), _halo_prev(hb, O_CG), _halo_prev(hb, O_U), cw,
                  pl.BlockSpec(memory_space=pl.ANY)],
        out_specs=[pl.BlockSpec((tm, 3 * D), _row), cw],
        out_shape=[jax.ShapeDtypeStruct(dproj.shape, BF), jax.ShapeDtypeStruct((8, D), F32)],
        input_output_aliases={9: 0},
        compiler_params=_cp(("arbitrary",)),
    )(dgc, dgc, proj, proj, proj, proj, proj, proj, convw, dproj)


def _adam(w, g, m, v, *, tm, name):
    _, R, C = w.shape
    tm = _tile(R, tm)
    parts = g.ndim == 3
    c1 = 1.0 - ADAM_B1
    c2 = 1.0 - ADAM_B2
    bc1 = 1.0 - ADAM_B1 ** ADAM_STEP
    bc2 = 1.0 - ADAM_B2 ** ADAM_STEP

    def body(w_ref, g_ref, m_ref, v_ref, go_ref, d_ref, nm_ref, nv_ref):
        if parts:
            gv = g_ref[0].astype(F32)
            for s in range(1, N_DEV):
                gv = gv + g_ref[s].astype(F32)
        else:
            gv = g_ref[...]
        go_ref[0] = gv
        nm = ADAM_B1 * m_ref[0] + c1 * gv
        nv = ADAM_B2 * v_ref[0] + c2 * (gv * gv)
        nm_ref[0] = nm
        nv_ref[0] = nv
        d_ref[0] = -ADAM_LR * ((nm / bc1) / (jnp.sqrt(nv / bc2) + ADAM_EPS) + ADAM_WD * w_ref[0])

    spec = pl.BlockSpec((1, tm, C), lambda i: (0, i, 0))
    gspec = pl.BlockSpec((N_DEV, tm, C), lambda i: (0, i, 0)) if parts else pl.BlockSpec((tm, C), _row)
    return pl.pallas_call(
        body, name=name, grid=(R // tm,),
        in_specs=[spec, gspec, spec, spec], out_specs=[spec] * 4,
        out_shape=[jax.ShapeDtypeStruct((1, R, C), F32)] * 4,
        compiler_params=_cp(("parallel",)),
    )(w, g, m, v)


def _mods_part(c_all, w_ada, b_ada, *, name):
    C = w_ada.shape[1]

    def body(c_ref, w_ref, b_ref, o_ref):
        cv = c_ref[...]
        ca = cv * jax.nn.sigmoid(cv)
        o_ref[...] = jnp.dot(ca, w_ref[...], preferred_element_type=F32,
                             precision=lax.Precision.HIGHEST) + b_ref[...]

    return pl.pallas_call(
        body, name=name,
        out_shape=jax.ShapeDtypeStruct((N_DEV, C), F32),
        compiler_params=_cp(),
    )(c_all, w_ada, b_ada)


def _wada_grad(c_all_t, gm, *, name):
    C = gm.shape[1]

    def body(c_ref, g_ref, o_ref):
        cv = c_ref[...]
        ca = cv * jax.nn.sigmoid(cv)
        acc = ca[:, 0:1] * g_ref[0:1, :]
        for b in range(1, N_DEV):
            acc = acc + ca[:, b:b + 1] * g_ref[b:b + 1, :]
        o_ref[...] = acc

    return pl.pallas_call(
        body, name=name,
        out_shape=jax.ShapeDtypeStruct((D, C), F32),
        compiler_params=_cp(),
    )(c_all_t, gm)


def _peer(x, y, c, d):
    px = lax.rem(x + ((d >> 2) & 1), 2)
    py = lax.rem(y + ((d >> 1) & 1), 2)
    pc = lax.rem(c + (d & 1), 2)
    return (px, py, pc), 4 * px + 2 * py + pc


def _exchange(xs, *, scatter, name):
    n = len(xs)
    nsem = n * (N_DEV - 1)

    def body(*refs):
        ins, outs = refs[:n], refs[n:2 * n]
        token, send_sems, recv_sems, local_sems = refs[2 * n:]
        x, y, c = lax.axis_index("x"), lax.axis_index("y"), lax.axis_index("c")
        me = 4 * x + 2 * y + c
        token[...] = jnp.zeros_like(token)

        def src(t, idx):
            return ins[t].at[idx] if scatter else ins[t]

        local = [pltpu.make_async_copy(src(t, me), outs[t].at[me], local_sems.at[t]) for t in range(n)]
        for cp in local:
            cp.start()
        remote = []
        for t in range(n):
            for d in range(1, N_DEV):
                peer, pidx = _peer(x, y, c, d)
                k = t * (N_DEV - 1) + d - 1
                send = pltpu.make_async_remote_copy(src_ref=src(t, pidx), dst_ref=outs[t].at[me],
                                                    send_sem=send_sems.at[k], recv_sem=recv_sems.at[k],
                                                    device_id=peer, device_id_type=MESH)
                recv = pltpu.make_async_remote_copy(src_ref=src(t, pidx), dst_ref=outs[t].at[pidx],
                                                    send_sem=send_sems.at[k], recv_sem=recv_sems.at[k],
                                                    device_id=peer, device_id_type=MESH)
                send.start()
                remote.append((send, recv))
        for cp in local:
            cp.wait()
        for send, recv in remote:
            send.wait_send()
            recv.wait_recv()

    anyspec = pl.BlockSpec(memory_space=pl.ANY)
    out_shape = [jax.ShapeDtypeStruct(a.shape if scatter else (N_DEV,) + a.shape, a.dtype) for a in xs]
    out_shape.append(jax.ShapeDtypeStruct((8, 128), F32))
    return pl.pallas_call(
        body, name=name,
        in_specs=[anyspec] * n, out_specs=[anyspec] * n + [pl.BlockSpec(memory_space=pltpu.VMEM)],
        out_shape=out_shape,
        scratch_shapes=[pltpu.SemaphoreType.DMA((nsem,)), pltpu.SemaphoreType.DMA((nsem,)),
                        pltpu.SemaphoreType.DMA((n,))],
    )(*xs)


def _sum8(parts, *, name):
    _, R, C = parts.shape

    def body(p_ref, o_ref):
        acc = p_ref[0]
        for s in range(1, N_DEV):
            acc = acc + p_ref[s]
        o_ref[...] = acc

    return pl.pallas_call(body, name=name, out_shape=jax.ShapeDtypeStruct((R, C), F32),
                          compiler_params=_cp())(parts)


HBM_SPEC = pl.BlockSpec(memory_space=pltpu.HBM)
SEM_SPEC = pl.BlockSpec(memory_space=pltpu.SEMAPHORE)
N_PEER = N_DEV - 1


def _split_copies(src_refs, land_refs, send_sems, recv_sems, scatter):
    x, y, c = lax.axis_index("x"), lax.axis_index("y"), lax.axis_index("c")
    me = 4 * x + 2 * y + c
    pairs = []
    for j, (src, land) in enumerate(zip(src_refs, land_refs)):
        for d in range(1, N_DEV):
            peer, pidx = _peer(x, y, c, d)
            k = j * N_PEER + d - 1
            s = src.at[pidx] if scatter else src
            send = pltpu.make_async_remote_copy(src_ref=s, dst_ref=land.at[me], send_sem=send_sems.at[k],
                                                recv_sem=recv_sems.at[k], device_id=peer, device_id_type=MESH)
            recv = pltpu.make_async_remote_copy(src_ref=s, dst_ref=land.at[pidx], send_sem=send_sems.at[k],
                                                recv_sem=recv_sems.at[k], device_id=peer, device_id_type=MESH)
            pairs.append((send, recv))
    return pairs


def _own_slot(block, me):
    land = lax.empty((N_DEV,) + block.shape, block.dtype)
    return lax.dynamic_update_slice(land, block[None], (me, 0, 0))


def _split_start(srcs, lands, groups, *, scatter, name):
    n, ng = len(srcs), len(groups)

    def body(*refs):
        src_refs, land_refs = refs[:n], refs[n:2 * n]
        sems = refs[2 * n:2 * n + 2 * ng]
        token = refs[-1]
        for gi, g in enumerate(groups):
            pairs = _split_copies([src_refs[t] for t in g], [land_refs[t] for t in g], sems[2 * gi],
                                  sems[2 * gi + 1], scatter)
            for send, _ in pairs:
                send.start()
        token[...] = jnp.zeros_like(token)

    sem_shapes = []
    for g in groups:
        sem_shapes += [pltpu.SemaphoreType.DMA((len(g) * N_PEER,))] * 2
    thru = [pltpu.HBM(a.shape, a.dtype) for a in list(srcs) + list(lands)]
    outs = pl.pallas_call(
        body, name=name,
        out_shape=tuple(sem_shapes + thru + [jax.ShapeDtypeStruct((8, 128), F32)]),
        in_specs=[HBM_SPEC] * (2 * n),
        out_specs=tuple([SEM_SPEC] * (2 * ng) + [HBM_SPEC] * (2 * n) + [pl.BlockSpec(memory_space=pltpu.VMEM)]),
        input_output_aliases={i: 2 * ng + i for i in range(2 * n)},
        compiler_params=pltpu.CompilerParams(has_side_effects=pltpu.SideEffectType.DATAFLOW_SIDE_EFFECTING),
    )(*[pltpu.with_memory_space_constraint(a, pltpu.HBM) for a in list(srcs) + list(lands)])
    sems = [(outs[2 * gi], outs[2 * gi + 1]) for gi in range(ng)]
    return sems, outs[2 * ng:2 * ng + n], outs[2 * ng + n:2 * ng + 2 * n], outs[-1]


def _behind(v, token):
    if token is None:
        return v
    return v + token[0, 0].astype(v.dtype)


def _split_wait(srcs, lands, sems, after, *, scatter, name):
    m = len(srcs)

    def body(*refs):
        src_refs, land_refs = refs[:m], refs[m:2 * m]
        send_sems, recv_sems = refs[2 * m], refs[2 * m + 1]
        for send, recv in _split_copies(src_refs, land_refs, send_sems, recv_sems, scatter):
            send.wait_send()
            recv.wait_recv()

    outs = pl.pallas_call(
        body, name=name,
        out_shape=tuple(pltpu.HBM(a.shape, a.dtype) for a in list(srcs) + list(lands)),
        in_specs=[HBM_SPEC] * (2 * m) + [SEM_SPEC, SEM_SPEC, pl.BlockSpec(memory_space=pl.ANY)],
        out_specs=tuple([HBM_SPEC] * (2 * m)),
        input_output_aliases={i: i for i in range(2 * m)},
        compiler_params=pltpu.CompilerParams(has_side_effects=pltpu.SideEffectType.DATAFLOW_SIDE_EFFECTING),
    )(*srcs, *lands, sems[0], sems[1], after)
    return outs[m:]


TL_FIRST = (1, 2, 4, 6)
TL_ICI = (2, 4, 6)
EFFECT = pltpu.SideEffectType.DATAFLOW_SIDE_EFFECTING


def _tl_first(src_refs, land_refs, send_sems, recv_sems):
    x, y, c = lax.axis_index("x"), lax.axis_index("y"), lax.axis_index("c")
    me = 4 * x + 2 * y + c
    out = []
    for j, (src, land) in enumerate(zip(src_refs, land_refs)):
        for i, d in enumerate(TL_FIRST):
            peer, pidx = _peer(x, y, c, d)
            k = len(TL_FIRST) * j + i
            send = pltpu.make_async_remote_copy(src_ref=src, dst_ref=land.at[me], send_sem=send_sems.at[k],
                                                recv_sem=recv_sems.at[k], device_id=peer, device_id_type=MESH)
            recv = pltpu.make_async_remote_copy(src_ref=src, dst_ref=land.at[pidx], send_sem=send_sems.at[k],
                                                recv_sem=recv_sems.at[k], device_id=peer, device_id_type=MESH)
            out.append((d, send, recv))
    return out


def _tl_second(land_refs, send_sems, recv_sems):
    x, y, c = lax.axis_index("x"), lax.axis_index("y"), lax.axis_index("c")
    sibling, _ = _peer(x, y, c, 1)
    out = []
    for j, land in enumerate(land_refs):
        for i, d in enumerate(TL_ICI):
            _, mine = _peer(x, y, c, d)
            _, theirs = _peer(x, y, c, d + 1)
            k = len(TL_ICI) * j + i
            send = pltpu.make_async_remote_copy(src_ref=land.at[mine], dst_ref=land.at[mine], send_sem=send_sems.at[k],
                                                recv_sem=recv_sems.at[k], device_id=sibling, device_id_type=MESH)
            recv = pltpu.make_async_remote_copy(src_ref=land.at[mine], dst_ref=land.at[theirs],
                                                send_sem=send_sems.at[k], recv_sem=recv_sems.at[k],
                                                device_id=sibling, device_id_type=MESH)
            out.append((send, recv))
    return out


def _tl_start(srcs, lands, groups, *, name):
    n, ng = len(srcs), len(groups)

    def body(*refs):
        src_refs, land_refs = refs[:n], refs[n:2 * n]
        sems = refs[2 * n:2 * n + 2 * ng]
        for gi, g in enumerate(groups):
            for _, send, _ in _tl_first([src_refs[t] for t in g], [land_refs[t] for t in g], sems[2 * gi],
                                        sems[2 * gi + 1]):
                send.start()
        refs[-1][...] = jnp.zeros_like(refs[-1])

    sem_shapes = []
    for g in groups:
        sem_shapes += [pltpu.SemaphoreType.DMA((len(g) * len(TL_FIRST),))] * 2
    thru = [pltpu.HBM(a.shape, a.dtype) for a in list(srcs) + list(lands)]
    outs = pl.pallas_call(
        body, name=name,
        out_shape=tuple(sem_shapes + thru + [jax.ShapeDtypeStruct((8, 128), F32)]),
        in_specs=[HBM_SPEC] * (2 * n),
        out_specs=tuple([SEM_SPEC] * (2 * ng) + [HBM_SPEC] * (2 * n) + [pl.BlockSpec(memory_space=pltpu.VMEM)]),
        input_output_aliases={i: 2 * ng + i for i in range(2 * n)},
        compiler_params=pltpu.CompilerParams(has_side_effects=EFFECT),
    )(*[pltpu.with_memory_space_constraint(a, pltpu.HBM) for a in list(srcs) + list(lands)])
    sems = [(outs[2 * gi], outs[2 * gi + 1]) for gi in range(ng)]
    return sems, outs[2 * ng:2 * ng + n], outs[2 * ng + n:2 * ng + 2 * n], outs[-1]


def _tl_forward(srcs, lands, sems1, after, *, name):
    m = len(srcs)

    def body(*refs):
        src_refs, land_refs = refs[:m], refs[m:2 * m]
        send1, recv1 = refs[2 * m], refs[2 * m + 1]
        send2, recv2 = refs[2 * m + 3], refs[2 * m + 4]
        for d, _, recv in _tl_first(src_refs, land_refs, send1, recv1):
            if d in TL_ICI:
                recv.wait_recv()
        for send, _ in _tl_second(land_refs, send2, recv2):
            send.start()

    sem = pltpu.SemaphoreType.DMA((m * len(TL_ICI),))
    outs = pl.pallas_call(
        body, name=name,
        out_shape=tuple([sem, sem] + [pltpu.HBM(a.shape, a.dtype) for a in list(srcs) + list(lands)]),
        in_specs=[HBM_SPEC] * (2 * m) + [SEM_SPEC, SEM_SPEC, pl.BlockSpec(memory_space=pl.ANY)],
        out_specs=tuple([SEM_SPEC, SEM_SPEC] + [HBM_SPEC] * (2 * m)),
        input_output_aliases={i: 2 + i for i in range(2 * m)},
        compiler_params=pltpu.CompilerParams(has_side_effects=EFFECT),
    )(*srcs, *lands, sems1[0], sems1[1], after)
    return (outs[0], outs[1]), outs[2:2 + m], outs[2 + m:2 + 2 * m]


def _tl_wait(srcs, lands, sems1, sems2, after, *, name):
    m = len(srcs)

    def body(*refs):
        src_refs, land_refs = refs[:m], refs[m:2 * m]
        send1, recv1, send2, recv2 = refs[2 * m:2 * m + 4]
        for d, send, recv in _tl_first(src_refs, land_refs, send1, recv1):
            send.wait_send()
            if d not in TL_ICI:
                recv.wait_recv()
        for send, recv in _tl_second(land_refs, send2, recv2):
            send.wait_send()
            recv.wait_recv()

    outs = pl.pallas_call(
        body, name=name,
        out_shape=tuple(pltpu.HBM(a.shape, a.dtype) for a in list(srcs) + list(lands)),
        in_specs=[HBM_SPEC] * (2 * m) + [SEM_SPEC] * 4 + [pl.BlockSpec(memory_space=pl.ANY)],
        out_specs=tuple([HBM_SPEC] * (2 * m)),
        input_output_aliases={i: i for i in range(2 * m)},
        compiler_params=pltpu.CompilerParams(has_side_effects=EFFECT),
    )(*srcs, *lands, sems1[0], sems1[1], sems2[0], sems2[1], after)
    return outs[m:]


TM_PROJ = 512
TN_PROJ = 512
TM_ROW = 512
TM_NN = 512
TK_TN = 2048
TM_ADAM = 416
TN_FFN = F // 2
TN_IN = NIN // 4


def _tn(a, b, name, tn, token=None):
    if a.ndim == 2:
        a = a[None]
    return _tn_matmul(a, b, token, tn=tn, tk=TK_TN, name=name)


def _local_step(x, tgt, mods, g1, gm, g2, gf, convw8, sinks, w_get, g_put, tables=None):
    T = x.shape[0]
    sh1, sc1, gt1, sh2, sc2, gt2, sh3, sc3, gt3 = [mods[i:i + 1] for i in range(N_MOD)]
    cos, sin = _rope_tables(T) if tables is None else tables
    behind = _behind

    w = dict(w_get("gu1", mods))
    h1, ab1 = _norm_proj(x, g1, sc1, sh1, w["gu1"], tm=TM_PROJ, tn=TN_PROJ, name="ffn1_up")
    w.update(w_get("d1", ab1))
    x1, y1 = _ffn_down_fwd(ab1, w["d1"], x, gt1, tm=TM_ROW, name="ffn1_down")
    w.update(w_get("mix", x1))
    h2, proj = _norm_proj(x1, gm, sc2, sh2, w["win"], tm=TM_PROJ, tn=TN_PROJ, name="mix_in")
    qs, kr = _attn_prep(proj, cos, sin, name="attn_prep")
    bias = _attn_bias()
    attn, lse = _attn_fwd(qs, kr, proj, bias, sinks, name="attn_fwd")
    x2, gc, yc, ya, mg, o = _mixer_mid_fwd(proj, attn, w["cp"], w["ap"], w["out"], convw8, x1, gt2,
                                           tm=TM_ROW, name="mix_mid")
    w.update(w_get("ffn2", x2))
    h3, ab2, y2, dx3, lsum, dgf = _ffn_fwd(x2, g2, sc3, sh3, gt3, w["gu2"], w["d2"], (tgt, gf), tm=TM_ROW,
                                           name="ffn2_final")

    dab2, dgt3, g_d2 = _ffn_down_bwd_dw(dx3, y2, gt3, ab2, w["d2"], tm=TM_ROW, name="ffn2_down_bwd")
    dx2, dsh3, dsc3, dg2 = _nn_bwd_norm(dab2, w["gu2"], x2, g2, sc3, dx3, tm=TM_NN, name="ffn2_up_bwd")
    g_gu2 = _tn(dab2, h3, "ffn2_up_dw", TN_FFN)
    tok = g_put(dict(gu2=g_gu2, d2=g_d2))

    dout, dyc, dya, dgc, dat, dproj, dgt2 = _mixer_mid_bwd(dx2, behind(gt2, tok), o, proj, yc, ya, w["out"], w["cp"],
                                                           w["ap"], tm=TM_ROW, name="mix_mid_bwd")
    g_out = _tn(mg, dout, "mix_out_dw", D)
    g_cp = _tn(gc, dyc, "mix_cp_dw", D)
    g_ap = _tn(attn, dya, "mix_ap_dw", D)
    dproj, dkc, dkp, dvc, dvp, dsink = _attn_bwd(qs, kr, proj, bias, sinks, lse, attn, dat, cos, sin, dproj,
                                                 name="attn_bwd")
    dproj = _dkv_combine(dkc, dkp, dvc, dvp, dproj, name="attn_dkv")
    dproj, dcw = _conv_bwd(dgc, proj, convw8, dproj, tm=TM_ROW, name="conv_bwd")
    g_in = _tn(dproj, h2, "mix_in_dw", TN_IN)
    tok = g_put(dict(win=g_in, cp=g_cp, ap=g_ap, out=g_out))
    dx1, dsh2, dsc2, dgm = _nn_bwd_norm(dproj[None], w["win"], x1, gm, behind(sc2, tok), dx2, tm=TM_NN,
                                        name="mix_in_bwd")

    dab1, dgt1, g_d1 = _ffn_down_bwd_dw(dx1, y1, gt1, ab1, w["d1"], tm=TM_ROW, name="ffn1_down_bwd")
    tok = g_put(dict(d1=g_d1))
    g_gu1 = _tn(dab1, h1, "ffn1_up_dw", TN_FFN, tok)
    tok = g_put(dict(gu1=g_gu1))
    dx0, dsh1, dsc1, dg1 = _nn_bwd_norm(dab1, w["gu1"], x, g1, behind(sc1, tok), dx1, tm=TM_NN,
                                        name="ffn1_up_bwd")

    small = dict(mods=jnp.concatenate([dsh1, dsc1, dgt1, dsh2, dsc2, dgt2, dsh3, dsc3, dgt3], axis=0),
                 g1=dg1, gm=dgm, g2=dg2, gf=dgf, convw=dcw[0:3], sinks=dsink[:, 0:N_HEADS])
    return lsum, dx0, small


BIG = ("gu1", "d1", "win", "cp", "ap", "out", "gu2", "d2")
TRANSPOSED = ("gu1", "win", "gu2")
SMALL_ROWS = 24
R_MODS, R_G1, R_GM, R_G2, R_GF, R_CONV, R_SINK = 0, 9, 10, 11, 12, 13, 16


def _pad_to(a, rows, cols):
    return jnp.pad(a, ((0, rows - a.shape[0]), (0, cols - a.shape[1])))


def _pack_small(b_ada, g1, gm, g2, gf, conv, sinks):
    rows = [b_ada.reshape(N_MOD, D), g1.reshape(1, D), gm.reshape(1, D), g2.reshape(1, D), gf.reshape(1, D),
            _pad_to(conv.reshape(3, -1), 3, D), _pad_to(sinks.reshape(1, N_HEADS), 1, D)]
    return _pad_to(jnp.concatenate(rows, axis=0), SMALL_ROWS, D)


def _unpack_small(p, conv_cols):
    return dict(b_ada=p[R_MODS:R_MODS + N_MOD].reshape(1, N_MOD * D), g_ffn1=p[R_G1:R_G1 + 1],
                g_mix=p[R_GM:R_GM + 1], g_ffn2=p[R_G2:R_G2 + 1], g_final=p[R_GF],
                conv_w=p[R_CONV:R_CONV + 3, 0:conv_cols][None], sinks=p[R_SINK:R_SINK + 1, 0:N_HEADS])


def kernel(x, c, w_ada, b_ada, g_ffn1, w1_gu, w1_down, g_mix, w_in, conv_w, w_conv_proj, w_attn_proj, sinks, w_out, g_ffn2, w2_gu, w2_down, g_final, loss_target, m_w_ada, m_b_ada, m_g_ffn1, m_w1_gu, m_w1_down, m_g_mix, m_w_in, m_conv_w, m_w_conv_proj, m_w_attn_proj, m_sinks, m_w_out, m_g_ffn2, m_w2_gu, m_w2_down, m_g_final, v_w_ada, v_b_ada, v_g_ffn1, v_w1_gu, v_w1_down, v_g_mix, v_w_in, v_conv_w, v_w_conv_proj, v_w_attn_proj, v_sinks, v_w_out, v_g_ffn2, v_w2_gu, v_w2_down, v_g_final):
    me = 4 * lax.axis_index("x") + 2 * lax.axis_index("y") + lax.axis_index("c")
    ada_cols = w_ada.shape[2]
    conv_cols = conv_w.shape[2]

    native = dict(gu1=w1_gu[0], d1=w1_down[0], win=w_in[0], cp=w_conv_proj[0], ap=w_attn_proj[0], out=w_out[0],
                  gu2=w2_gu[0], d2=w2_down[0])

    def shard(n, token):
        a = _behind(native[n], token)
        return (a.T if n in TRANSPOSED else a).astype(BF)

    c_all, conv_all, _ = _exchange([c, _pad_to(conv_w[0], 8, conv_cols)], scatter=False, name="gather_cond")
    c_all = c_all.reshape(N_DEV, D)
    conv_full = conv_all[:, 0:3, :].transpose(1, 0, 2).reshape(3, D)

    b_cols = lax.dynamic_slice(b_ada, (0, me * ada_cols), (1, ada_cols))
    mods_cols = _mods_part(c_all, w_ada[0], b_cols, name="ada_mods")
    mods_all, mods_token = _exchange([mods_cols], scatter=False, name="gather_mods")
    mods = lax.dynamic_index_in_dim(mods_all, me, axis=1, keepdims=False).reshape(N_MOD, D)

    groups = dict(gu1=("gu1",), d1=("d1",), mix=("win", "cp", "ap", "out"), ffn2=("gu2", "d2"))
    in_flight = {}
    first = [shard("gu1", mods_token)]
    sems, srcs, lands, token = _tl_start(first, [_own_slot(s, me) for s in first], [[0]],
                                         name="gather_weights_start_gu1")
    in_flight["gu1"] = [sems[0], srcs, lands, None]
    rest = [n for n in BIG if n != "gu1"]
    shards = [shard(n, token) for n in rest]
    rest_groups = [[rest.index(n) for n in names] for g, names in groups.items() if g != "gu1"]
    sems, srcs, lands, rest_token = _tl_start(shards, [_own_slot(s, me) for s in shards], rest_groups,
                                              name="gather_weights_start_rest")
    for (g, names), gsems, idx in zip([kv for kv in groups.items() if kv[0] != "gu1"], sems, rest_groups):
        in_flight[g] = [gsems, [srcs[t] for t in idx], [lands[t] for t in idx], None]

    def forward(group, after):
        sems1, gsrcs, glands, _ = in_flight[group]
        sems2, gsrcs, glands = _tl_forward(gsrcs, glands, sems1, after, name="gather_weights_forward_" + group)
        in_flight[group] = [sems1, gsrcs, glands, sems2]

    forward_early = dict(d1="mix", mix="ffn2")

    tables = _rope_tables(x.shape[1], rest_token)

    def w_get(group, after):
        if group == "gu1":
            after = tables[0]
        if in_flight[group][3] is None:
            forward(group, after)
        sems1, gsrcs, glands, sems2 = in_flight[group]
        landed = _tl_wait(gsrcs, glands, sems1, sems2, after, name="gather_weights_wait_" + group)
        if group in forward_early:
            forward(forward_early[group], landed[0])
        return {n: a.reshape(-1, D) for n, a in zip(groups[group], landed)}

    pending = []

    def g_put(gs):
        names = tuple(gs)
        srcs = [gs[n].reshape(N_DEV, -1, D) for n in names]
        lands = [_own_slot(lax.dynamic_index_in_dim(s, me, axis=0, keepdims=False), me) for s in srcs]
        sems, srcs, lands, tok = _split_start(srcs, lands, [list(range(len(names)))], scatter=True,
                                              name="scatter_grads_start_" + names[0])
        pending.append((names, sems[0], srcs, lands))
        return tok

    lsum, grad_x, small = _local_step(x[0], loss_target[0], mods, g_ffn1, g_mix, g_ffn2, g_final[None],
                                      _pad_to(conv_full, 8, D), sinks[0], w_get, g_put, tables)
    loss = lax.psum((0.5 / D) * jnp.sum(lsum), ("x", "y", "c"))

    packed = _pack_small(small["mods"], small["g1"], small["gm"], small["g2"], small["gf"], small["convw"],
                         small["sinks"])
    packed_all, _ = _exchange([packed], scatter=False, name="gather_small")
    gsmall = _sum8(packed_all, name="sum_small")

    w_of = dict(ada=w_ada, gu1=w1_gu, d1=w1_down, win=w_in, cp=w_conv_proj, ap=w_attn_proj, out=w_out, gu2=w2_gu,
                d2=w2_down)
    m_of = dict(ada=m_w_ada, gu1=m_w1_gu, d1=m_w1_down, win=m_w_in, cp=m_w_conv_proj, ap=m_w_attn_proj, out=m_w_out,
                gu2=m_w2_gu, d2=m_w2_down)
    v_of = dict(ada=v_w_ada, gu1=v_w1_gu, d1=v_w1_down, win=v_w_in, cp=v_w_conv_proj, ap=v_w_attn_proj, out=v_w_out,
                gu2=v_w2_gu, d2=v_w2_down)
    upd = {}
    after = gsmall
    for names, sems, srcs, lands in pending:
        parts = _split_wait(srcs, lands, sems, after, scatter=True, name="scatter_grads_wait_" + names[0])
        for n, p in zip(names, parts):
            if n in TRANSPOSED:
                res = _adam(jnp.swapaxes(w_of[n], 1, 2), p, jnp.swapaxes(m_of[n], 1, 2), jnp.swapaxes(v_of[n], 1, 2),
                            tm=TM_ADAM, name="adam_" + n)
                upd[n] = [jnp.swapaxes(t, 1, 2) for t in res]
            else:
                upd[n] = _adam(w_of[n], p, m_of[n], v_of[n], tm=TM_ADAM, name="adam_" + n)
        after = upd[names[-1]][1]

    gm_cols = lax.dynamic_slice(packed_all[:, R_MODS:R_MODS + N_MOD, :].reshape(N_DEV, N_MOD * D),
                                (0, me * ada_cols), (N_DEV, ada_cols))
    upd["ada"] = _adam(w_ada, _wada_grad(c_all.T, gm_cols, name="ada_dw"), m_w_ada, v_w_ada, tm=256, name="adam_ada")
    conv_g = lax.dynamic_slice(gsmall[R_CONV:R_CONV + 3], (0, me * conv_cols), (3, conv_cols))
    gsmall_own = gsmall.at[R_CONV:R_CONV + 3].set(_pad_to(conv_g, 3, D))
    small_upd = _adam(_pack_small(b_ada, g_ffn1, g_mix, g_ffn2, g_final, conv_w, sinks)[None], gsmall_own,
                      _pack_small(m_b_ada, m_g_ffn1, m_g_mix, m_g_ffn2, m_g_final, m_conv_w, m_sinks)[None],
                      _pack_small(v_b_ada, v_g_ffn1, v_g_mix, v_g_ffn2, v_g_final, v_conv_w, v_sinks)[None],
                      tm=SMALL_ROWS, name="adam_small")
    small_out = [_unpack_small(p[0], conv_cols) for p in small_upd]

    big_name = dict(w_ada="ada", w1_gu="gu1", w1_down="d1", w_in="win", w_conv_proj="cp", w_attn_proj="ap",
                    w_out="out", w2_gu="gu2", w2_down="d2")
    order = ("w_ada", "b_ada", "g_ffn1", "w1_gu", "w1_down", "g_mix", "w_in", "conv_w", "w_conv_proj", "w_attn_proj",
             "sinks", "w_out", "g_ffn2", "w2_gu", "w2_down", "g_final")
    outs = [loss, grad_x[None]]
    for kind in range(4):
        for n in order:
            outs.append(upd[big_name[n]][kind] if n in big_name else small_out[kind][n])
    return tuple(outs)
```

```python
import jax
import jax.numpy as jnp
from jax import lax
from jax.experimental import pallas as pl
from jax.experimental.pallas import tpu as pltpu

D = 1024
F = 2816
NIN = 6656
N_HEADS = 16
N_KV = 4
HEAD_DIM = 64
BLK = 128
N_MOD = 9
N_DEV = 8
EPS = 1e-6
NEG_INF = -1e30
ROPE_THETA = 10000.0
O_BG, O_CG, O_U, O_Q, O_K, O_V, O_ZC, O_ZA = 0, 1024, 2048, 3072, 4096, 4352, 4608, 5632

ADAM_LR = 0.001
ADAM_B1 = 0.9
ADAM_B2 = 0.999
ADAM_EPS = 1e-08
ADAM_WD = 0.01
ADAM_STEP = 10

BF = jnp.bfloat16
F32 = jnp.float32
VMEM_LIMIT = 56 * 1024 * 1024
MXU_N = 256
MESH = pl.DeviceIdType.MESH

NT = (((1,), (1,)), ((), ()))
TN = (((0,), (0,)), ((), ()))


def _cp(sem=None):
    return pltpu.CompilerParams(dimension_semantics=sem, vmem_limit_bytes=VMEM_LIMIT)


def _tile(n, pref):
    if n <= pref:
        return n
    for t in range(pref - pref % 16, 15, -16):
        if n % t == 0:
            return t
    raise ValueError((n, pref))


def _sigmoid(v):
    return 0.5 * jnp.tanh(0.5 * v) + 0.5


def _row(i):
    return (i, 0)


def _const2(*_):
    return (0, 0)


def _resident(shape):
    return pl.BlockSpec(shape, lambda *_: (0,) * len(shape), pipeline_mode=pl.Buffered(1))


def _norm_proj(x, g, sc, sh, wt, *, tm, tn, name):
    T, N = x.shape[0], wt.shape[0]
    tm = _tile(T, tm)

    def body(x_ref, g_ref, sc_ref, sh_ref, w_ref, h_ref, o_ref):
        xv = x_ref[...]
        r = lax.rsqrt(jnp.mean(xv * xv, axis=-1, keepdims=True) + EPS)
        hb = ((xv * r) * g_ref[...] * (1.0 + sc_ref[...]) + sh_ref[...]).astype(BF)
        h_ref[...] = hb
        for c0 in range(0, N, tn):
            cols = pl.ds(c0, tn)
            o_ref[:, cols] = lax.dot_general(hb, w_ref[cols, :], NT, preferred_element_type=F32).astype(BF)

    vec = pl.BlockSpec((1, D), _const2)
    return pl.pallas_call(
        body, name=name, grid=(T // tm,),
        in_specs=[pl.BlockSpec((tm, D), _row), vec, vec, vec, _resident((N, D))],
        out_specs=[pl.BlockSpec((tm, D), _row), pl.BlockSpec((tm, N), _row)],
        out_shape=[jax.ShapeDtypeStruct((T, D), BF), jax.ShapeDtypeStruct((T, N), BF)],
        compiler_params=_cp(("parallel",)),
    )(x, g, sc, sh, wt)


def _ffn_down_fwd(ab, wd, x, gt, *, tm, name):
    T = x.shape[0]
    tm = _tile(T, tm)

    def body(a_ref, b_ref, wd_ref, x_ref, gt_ref, xo_ref, y_ref):
        y = None
        for c0 in range(0, F, MXU_N):
            cols = pl.ds(c0, MXU_N)
            a = a_ref[:, cols].astype(F32)
            act = (a * _sigmoid(a) * b_ref[:, cols].astype(F32)).astype(BF)
            part = jnp.dot(act, wd_ref[cols, :], preferred_element_type=F32)
            y = part if y is None else y + part
        y_ref[...] = y.astype(BF)
        xo_ref[...] = x_ref[...] + (0.5 * gt_ref[...]) * y

    return pl.pallas_call(
        body, name=name, grid=(T // tm,),
        in_specs=[pl.BlockSpec((tm, F), lambda i: (i, 0)), pl.BlockSpec((tm, F), lambda i: (i, 1)),
                  _resident((F, D)), pl.BlockSpec((tm, D), _row), pl.BlockSpec((1, D), _const2)],
        out_specs=[pl.BlockSpec((tm, D), _row), pl.BlockSpec((tm, D), _row)],
        out_shape=[jax.ShapeDtypeStruct((T, D), F32), jax.ShapeDtypeStruct((T, D), BF)],
        compiler_params=_cp(("parallel",)),
    )(ab, ab, wd, x, gt)


def _ffn_fwd(x, g, sc, sh, gt, wgu, wd, final, *, tm, name):
    T = x.shape[0]
    tm = _tile(T, tm)
    last = final is not None

    def body(x_ref, g_ref, sc_ref, sh_ref, gt_ref, wgu_ref, wd_ref, *rest):
        if last:
            t_ref, gf_ref, h_ref, ab_ref, y_ref, dx_ref, ls_ref, dgf_ref = rest
        else:
            h_ref, ab_ref, y_ref, xo_ref = rest
        xv = x_ref[...]
        r = lax.rsqrt(jnp.mean(xv * xv, axis=-1, keepdims=True) + EPS)
        hb = ((xv * r) * g_ref[...] * (1.0 + sc_ref[...]) + sh_ref[...]).astype(BF)
        h_ref[...] = hb
        y = None
        for c0 in range(0, F, MXU_N):
            a = lax.dot_general(hb, wgu_ref[pl.ds(c0, MXU_N), :], NT, preferred_element_type=F32)
            b = lax.dot_general(hb, wgu_ref[pl.ds(F + c0, MXU_N), :], NT, preferred_element_type=F32)
            ab = a.astype(BF)
            bb = b.astype(BF)
            ab_ref[:, pl.ds(c0, MXU_N)] = ab
            ab_ref[:, pl.ds(F + c0, MXU_N)] = bb
            a = ab.astype(F32)
            act = (a * _sigmoid(a) * bb.astype(F32)).astype(BF)
            part = jnp.dot(act, wd_ref[pl.ds(c0, MXU_N), :], preferred_element_type=F32)
            y = part if y is None else y + part
        y_ref[...] = y.astype(BF)
        xo = xv + (0.5 * gt_ref[...]) * y
        if not last:
            xo_ref[...] = xo
            return

        @pl.when(pl.program_id(0) == 0)
        def _():
            ls_ref[...] = jnp.zeros_like(ls_ref)
            dgf_ref[...] = jnp.zeros_like(dgf_ref)
        gv = gf_ref[...]
        r = lax.rsqrt(jnp.mean(xo * xo, axis=-1, keepdims=True) + EPS)
        xh = xo * r
        e = xh * gv - t_ref[...]
        ls_ref[...] += jnp.sum(e * e, axis=0, keepdims=True)
        dy = e * (1.0 / D)
        dgf_ref[...] += jnp.sum(dy * xh, axis=0, keepdims=True)
        dxh = dy * gv
        dx_ref[...] = r * (dxh - xh * jnp.mean(dxh * xh, axis=-1, keepdims=True))

    vec = pl.BlockSpec((1, D), _const2)
    rowspec = pl.BlockSpec((tm, D), _row)
    in_specs = [rowspec, vec, vec, vec, vec, _resident((2 * F, D)), _resident((F, D))]
    out_specs = [rowspec, pl.BlockSpec((tm, 2 * F), _row), rowspec, rowspec]
    out_shape = [jax.ShapeDtypeStruct((T, D), BF), jax.ShapeDtypeStruct((T, 2 * F), BF),
                 jax.ShapeDtypeStruct((T, D), BF), jax.ShapeDtypeStruct((T, D), F32)]
    args = [x, g, sc, sh, gt, wgu, wd]
    if last:
        in_specs += [rowspec, vec]
        out_specs += [vec, vec]
        out_shape += [jax.ShapeDtypeStruct((1, D), F32)] * 2
        args += list(final)
    return pl.pallas_call(
        body, name=name, grid=(T // tm,),
        in_specs=in_specs, out_specs=out_specs, out_shape=out_shape,
        compiler_params=_cp(("arbitrary",) if last else ("parallel",)),
    )(*args)


def _ffn_down_bwd_dw(dxo, y, gt, ab, wd, *, tm, name):
    T = dxo.shape[0]
    tm = _tile(T, tm)
    nt = T // tm
    hw = F // 2
    chunks = [(c0, min(MXU_N, hw - c0)) for c0 in range(0, hw, MXU_N)]

    def body(dxo_ref, y_ref, gt_ref, a_ref, b_ref, wd_ref, dab_ref, dgt_ref, dwd_ref, dys, dyt, acc, stage, sem):
        i, j = pl.program_id(0), pl.program_id(1)

        @pl.when(jnp.logical_and(i == 0, j == 0))
        def _():
            dgt_ref[...] = jnp.zeros_like(dgt_ref)

        @pl.when(i == 0)
        def _():
            acc[j] = jnp.zeros((D, hw), F32)

        @pl.when(j == 0)
        def _():
            dxv = dxo_ref[...]
            dgt_ref[...] += 0.5 * jnp.sum(dxv * y_ref[...].astype(F32), axis=0, keepdims=True)
            dyf = (0.5 * gt_ref[...]) * dxv
            dys[...] = dyf.astype(BF)
            dyt[...] = dyf.T.astype(BF)

        dy = dys[...]
        dy_t = dyt[...]

        def dact_of(c0, cw):
            w_rows = pl.ds(pl.multiple_of(j * hw + c0, 128), cw)
            return lax.dot_general(dy, wd_ref[w_rows, :], NT, preferred_element_type=F32)

        ahead = dact_of(*chunks[0])
        for n, (c0, cw) in enumerate(chunks):
            cols = pl.ds(c0, cw)
            dact = ahead
            if n + 1 < len(chunks):
                ahead = dact_of(*chunks[n + 1])
            a = a_ref[:, cols].astype(F32)
            b = b_ref[:, cols].astype(F32)
            s = _sigmoid(a)
            silu = a * s
            dab_ref[0, :, cols] = (dact * b * (s * (1.0 + a * (1.0 - s)))).astype(BF)
            dab_ref[1, :, cols] = (dact * silu).astype(BF)
            acc[j, :, cols] += jnp.dot(dy_t, (silu * b).astype(BF), preferred_element_type=F32)

        @pl.when(i == nt - 1)
        def _():
            for c0, cw in chunks:
                stage[0:cw, :] = acc[j, :, pl.ds(c0, cw)].T.astype(BF)
                out = pltpu.make_async_copy(stage.at[pl.ds(0, cw)],
                                            dwd_ref.at[pl.ds(pl.multiple_of(j * hw + c0, 128), cw)], sem)
                out.start()
                out.wait()

    vec = pl.BlockSpec((1, D), _const2)
    rowspec = pl.BlockSpec((tm, D), lambda i, j: (i, 0))
    return pl.pallas_call(
        body, name=name, grid=(nt, 2),
        in_specs=[rowspec, rowspec, vec, pl.BlockSpec((tm, hw), lambda i, j: (i, j)),
                  pl.BlockSpec((tm, hw), lambda i, j: (i, j + 2)), _resident((F, D))],
        out_specs=[pl.BlockSpec((2, tm, hw), lambda i, j: (0, i, j)), vec, pl.BlockSpec(memory_space=pl.ANY)],
        out_shape=[jax.ShapeDtypeStruct((2, T, F), BF), jax.ShapeDtypeStruct((1, D), F32),
                   jax.ShapeDtypeStruct((F, D), BF)],
        scratch_shapes=[pltpu.VMEM((tm, D), BF), pltpu.VMEM((D, tm), BF), pltpu.VMEM((2, D, hw), F32),
                        pltpu.VMEM((MXU_N, D), BF), pltpu.SemaphoreType.DMA(())],
        compiler_params=_cp(("arbitrary", "arbitrary")),
    )(dxo, y, gt, ab, ab, wd)


def _tn_matmul(a, b, token=None, *, tn, tk, name):
    S, T, Ns = a.shape
    tn, tk = _tile(Ns, tn), _tile(T, tk)
    nk, njs = T // tk, Ns // tn
    deps = [] if token is None else [token]

    def body(a_ref, b_ref, *rest):
        o_ref, acc = rest[len(deps):]
        k = pl.program_id(1)

        @pl.when(k == 0)
        def _():
            acc[...] = jnp.zeros_like(acc)
        acc[...] += lax.dot_general(a_ref[0], b_ref[...], TN, preferred_element_type=F32)

        @pl.when(k == nk - 1)
        def _():
            o_ref[...] = acc[...].astype(BF)

    return pl.pallas_call(
        body, name=name, grid=(S * njs, nk),
        in_specs=[pl.BlockSpec((1, tk, tn), lambda j, k: (j // njs, k, j % njs)),
                  pl.BlockSpec((tk, D), lambda j, k: (k, 0))] + [pl.BlockSpec(memory_space=pl.ANY)] * len(deps),
        out_specs=pl.BlockSpec((tn, D), lambda j, k: (j, 0)),
        out_shape=jax.ShapeDtypeStruct((S * Ns, D), BF),
        scratch_shapes=[pltpu.VMEM((tn, D), F32)],
        compiler_params=_cp(("parallel", "arbitrary")),
    )(a, b, *deps)


def _nn_bwd_norm(da, w, x, g, sc, dxo, *, tm, name):
    S, T, Ks = da.shape
    tm = _tile(T, tm)
    rc = _tile(tm, 256)

    def body(da_ref, w_ref, x_ref, g_ref, sc_ref, dxo_ref, dx_ref, dsh_ref, dsc_ref, dg_ref, acc):
        @pl.when(pl.program_id(0) == 0)
        def _():
            dsh_ref[...] = jnp.zeros_like(dsh_ref)
            dsc_ref[...] = jnp.zeros_like(dsc_ref)
            dg_ref[...] = jnp.zeros_like(dg_ref)

        d = jnp.dot(da_ref[0], w_ref[0:Ks, :], preferred_element_type=F32)
        for s in range(1, S):
            d = d + jnp.dot(da_ref[s], w_ref[s * Ks:(s + 1) * Ks, :], preferred_element_type=F32)
        acc[...] = d
        gv = g_ref[...]
        sc1 = 1.0 + sc_ref[...]
        dsh = jnp.zeros((1, D), F32)
        dsc = jnp.zeros((1, D), F32)
        dg = jnp.zeros((1, D), F32)
        for r0 in range(0, tm, rc):
            rows = pl.ds(r0, rc)
            u = acc[rows, :]
            xv = x_ref[rows, :]
            r = lax.rsqrt(jnp.mean(xv * xv, axis=-1, keepdims=True) + EPS)
            xh = xv * r
            dsh = dsh + jnp.sum(u, axis=0, keepdims=True)
            dsc = dsc + jnp.sum(u * (xh * gv), axis=0, keepdims=True)
            us = u * sc1
            dg = dg + jnp.sum(us * xh, axis=0, keepdims=True)
            dxh = us * gv
            dx_ref[rows, :] = dxo_ref[rows, :] + r * (dxh - xh * jnp.mean(dxh * xh, axis=-1, keepdims=True))
        dsh_ref[...] += dsh
        dsc_ref[...] += dsc
        dg_ref[...] += dg

    vec = pl.BlockSpec((1, D), _const2)
    rowspec = pl.BlockSpec((tm, D), _row)
    return pl.pallas_call(
        body, name=name, grid=(T // tm,),
        in_specs=[pl.BlockSpec((S, tm, Ks), lambda i: (0, i, 0)), _resident((S * Ks, D)), rowspec, vec, vec, rowspec],
        out_specs=[rowspec, vec, vec, vec],
        out_shape=[jax.ShapeDtypeStruct((T, D), F32)] + [jax.ShapeDtypeStruct((1, D), F32)] * 3,
        scratch_shapes=[pltpu.VMEM((tm, D), F32)],
        compiler_params=_cp(("arbitrary",)),
    )(da, w, x, g, sc, dxo)


def _rope(t, cos, sin_signed, lt32, inverse=False):
    sel = jnp.where(lt32, pltpu.roll(t, 96, 1), pltpu.roll(t, 32, 1))
    return t * cos - sel * sin_signed if inverse else t * cos + sel * sin_signed


def _rope_tables(T, token=None):
    inv = 1.0 / (ROPE_THETA ** (jnp.arange(0, HEAD_DIM, 2, dtype=F32) / HEAD_DIM))
    ang = _behind(jnp.arange(T, dtype=F32)[:, None] * inv[None, :], token)
    cos, sin = jnp.cos(ang), jnp.sin(ang)
    cos128 = jnp.tile(cos, (1, 4))
    sin128 = jnp.tile(jnp.concatenate([-sin, sin], axis=1), (1, 2))
    return cos128, sin128


QSCALE = HEAD_DIM ** -0.5


def _lane_masks(rows):
    lane = lax.broadcasted_iota(jnp.int32, (rows, 128), 1)
    return (lane % HEAD_DIM) < (HEAD_DIM // 2), [lane < HEAD_DIM, lane >= HEAD_DIM]


def _attn_bias():
    qi = lax.broadcasted_iota(jnp.int32, (4 * BLK, 2 * BLK), 0) % BLK
    kj = lax.broadcasted_iota(jnp.int32, (4 * BLK, 2 * BLK), 1)
    band = (kj > qi) & (kj <= qi + BLK)
    return jnp.stack([jnp.where(band & (kj >= BLK), 0.0, NEG_INF), jnp.where(band, 0.0, NEG_INF)]).astype(F32)


def _attn_prep(proj, cos, sin, *, name):
    T = proj.shape[0]
    tm = _tile(T, 4 * BLK)

    def body(q_ref, k_ref, c_ref, s_ref, qs_ref, kr_ref):
        lt32, halves = _lane_masks(BLK)
        for b in range(tm // BLK):
            rows = pl.ds(b * BLK, BLK)
            cc, sc = c_ref[rows, :], s_ref[rows, :]
            qr = [_rope(q_ref[rows, p * 128:(p + 1) * 128].astype(F32), cc, sc, lt32) * QSCALE for p in range(8)]
            for g in range(N_KV):
                qs_ref[g, pl.ds(4 * b * BLK, 4 * BLK), :] = _stack_heads(qr, g, halves).astype(BF)
            kr_ref[rows, :] = jnp.concatenate([_rope(k_ref[rows, r * 128:(r + 1) * 128].astype(F32), cc, sc, lt32)
                                               for r in range(2)], axis=1).astype(BF)

    tab = pl.BlockSpec((tm, 128), _row)
    return pl.pallas_call(
        body, name=name, grid=(T // tm,),
        in_specs=[pl.BlockSpec((tm, D), lambda n: (n, O_Q // D)), pl.BlockSpec((tm, 256), lambda n: (n, O_K // 256)),
                  tab, tab],
        out_specs=[pl.BlockSpec((N_KV, 4 * tm, 128), lambda n: (0, n, 0)), pl.BlockSpec((tm, 256), _row)],
        out_shape=[jax.ShapeDtypeStruct((N_KV, 4 * T, 128), BF), jax.ShapeDtypeStruct((T, 256), BF)],
        compiler_params=_cp(("parallel",)),
    )(proj, proj, cos, sin)


def _attn_specs():
    prev = lambda n: jnp.maximum(n - 1, 0)
    return [pl.BlockSpec((N_KV, 4 * BLK, 128), lambda n: (0, n, 0)),
            pl.BlockSpec((BLK, 256), _row), pl.BlockSpec((BLK, 256), lambda n: (prev(n), 0)),
            pl.BlockSpec((BLK, 256), lambda n: (n, O_V // 256)),
            pl.BlockSpec((BLK, 256), lambda n: (prev(n), O_V // 256)),
            pl.BlockSpec((1, 4 * BLK, 2 * BLK), lambda n: (jnp.minimum(n, 1), 0, 0)),
            pl.BlockSpec(memory_space=pltpu.SMEM)]


def _bands(kc_ref, kp_ref, vc_ref, vp_ref):
    kb, vb = [], []
    for r in range(2):
        cols = slice(r * 128, (r + 1) * 128)
        kb.append(jnp.concatenate([kp_ref[:, cols], kc_ref[:, cols]], axis=0))
        vb.append(jnp.concatenate([vp_ref[:, cols], vc_ref[:, cols]], axis=0))
    return kb, vb


def _sink_rows(sink_ref, g):
    return jnp.concatenate([jnp.full((BLK, 128), sink_ref[4 * g + hh], F32) for hh in range(4)], axis=0)


def _both(t):
    return jnp.concatenate([t, t], axis=1)


def _unstack_heads(t, g, halves, acc):
    half = g % 2
    for hh in range(4):
        h = 4 * g + hh
        th = jnp.where(halves[half], t[hh * BLK:(hh + 1) * BLK], 0.0)
        if h % 2 != half:
            th = pltpu.roll(th, HEAD_DIM, 1)
        acc[h // 2] = acc[h // 2] + th


def _stack_heads(chunks, g, halves):
    half = g % 2
    parts = []
    for hh in range(4):
        h = 4 * g + hh
        t = chunks[h // 2]
        if h % 2 != half:
            t = pltpu.roll(t, HEAD_DIM, 1)
        parts.append(jnp.where(halves[half], t, 0.0))
    return jnp.concatenate(parts, axis=0)


def _attn_fwd(qs, kr, proj, bias, sinks, *, name):
    T = proj.shape[0]
    nb = T // BLK

    def body(qs_ref, kc_ref, kp_ref, vc_ref, vp_ref, bias_ref, sink_ref, o_ref, lse_ref):
        _, h128 = _lane_masks(BLK)
        _, h256 = _lane_masks(2 * BLK)
        _, h512 = _lane_masks(4 * BLK)
        kb, vb = _bands(kc_ref, kp_ref, vc_ref, vp_ref)
        outs = [jnp.zeros((BLK, 128), F32) for _ in range(8)]
        groups = range(N_KV)
        bias = bias_ref[0]
        sink = [_sink_rows(sink_ref, g) for g in groups]
        s = [lax.dot_general(qs_ref[g], kb[g // 2], NT, preferred_element_type=F32) + bias for g in groups]
        m = [jnp.maximum(jnp.broadcast_to(jnp.max(s[g], axis=-1, keepdims=True), (4 * BLK, 128)), sink[g])
             for g in groups]
        p = [jnp.exp(s[g] - _both(m[g])).astype(BF) for g in groups]
        vg = [jnp.where(h256[g % 2], vb[g // 2].astype(F32), 1.0).astype(BF) for g in groups]
        o = [jnp.dot(p[g], vg[g], preferred_element_type=F32) for g in groups]
        denom = [jnp.where(h512[g % 2], pltpu.roll(o[g], HEAD_DIM, 1), o[g]) + jnp.exp(sink[g] - m[g]) for g in groups]
        for g in groups:
            lse_ref[g] = m[g] + jnp.log(denom[g])
            _unstack_heads(o[g] * (1.0 / denom[g]), g, h128, outs)
        o_ref[...] = jnp.concatenate(outs, axis=1).astype(BF)

    return pl.pallas_call(
        body, name=name, grid=(nb,),
        in_specs=_attn_specs(),
        out_specs=[pl.BlockSpec((BLK, D), _row), pl.BlockSpec((N_KV, 4 * BLK, 128), lambda n: (0, n, 0))],
        out_shape=[jax.ShapeDtypeStruct((T, D), BF), jax.ShapeDtypeStruct((N_KV, 4 * T, 128), F32)],
        compiler_params=_cp(("parallel",)),
    )(qs, kr, kr, proj, proj, bias, sinks)


def _attn_bwd(qs, kr, proj, bias, sinks, lse, o, do, cos, sin, dproj, *, name):
    T = proj.shape[0]
    nb = T // BLK

    def body(qs_ref, kc_ref, kp_ref, vc_ref, vp_ref, bias_ref, sink_ref, lse_ref, o_ref, do_ref,
             cc_ref, sc_ref, cp_ref, sp_ref, dproj_ref, dq_ref, dkc_ref, dkp_ref, dvc_ref, dvp_ref, dsink_ref):
        @pl.when(pl.program_id(0) == 0)
        def _():
            dsink_ref[...] = jnp.zeros_like(dsink_ref)
        lt32, h128 = _lane_masks(BLK)
        kb, vb = _bands(kc_ref, kp_ref, vc_ref, vp_ref)
        oc = [o_ref[:, p * 128:(p + 1) * 128].astype(F32) for p in range(8)]
        doc = [do_ref[:, p * 128:(p + 1) * 128].astype(F32) for p in range(8)]
        dqs = [jnp.zeros((BLK, 128), F32) for _ in range(8)]
        lane1 = lax.broadcasted_iota(jnp.int32, (1, 128), 1)
        dsink = jnp.zeros((1, 128), F32)
        groups = range(N_KV)
        bias = bias_ref[0]
        q = [qs_ref[g] for g in groups]
        lse_g = [lse_ref[g] for g in groups]
        s = [lax.dot_general(q[g], kb[g // 2], NT, preferred_element_type=F32) + bias for g in groups]
        dos = [_stack_heads(doc, g, h128) for g in groups]
        dosb = [t.astype(BF) for t in dos]
        dp = [lax.dot_general(dosb[g], vb[g // 2], NT, preferred_element_type=F32) for g in groups]
        delta = [jnp.broadcast_to(jnp.sum(dos[g] * _stack_heads(oc, g, h128), axis=-1, keepdims=True), (4 * BLK, 128))
                 for g in groups]
        p = [jnp.exp(s[g] - _both(lse_g[g])) for g in groups]
        ds = [(p[g] * (dp[g] - _both(delta[g]))).astype(BF) for g in groups]
        pb = [t.astype(BF) for t in p]
        dvg = [lax.dot_general(pb[g], dosb[g], TN, preferred_element_type=F32) for g in groups]
        dkg = [lax.dot_general(ds[g], q[g], TN, preferred_element_type=F32) for g in groups]
        dqg = [jnp.dot(ds[g], kb[g // 2], preferred_element_type=F32) * QSCALE for g in groups]
        dvr = [dvg[0] + dvg[1], dvg[2] + dvg[3]]
        dkr = [dkg[0] + dkg[1], dkg[2] + dkg[3]]
        for g in groups:
            _unstack_heads(dqg[g], g, h128, dqs)
            dsk = -jnp.exp(_sink_rows(sink_ref, g) - lse_g[g]) * delta[g]
            for hh in range(4):
                val = jnp.sum(dsk[hh * BLK:(hh + 1) * BLK], axis=0, keepdims=True)
                dsink = dsink + jnp.where(lane1 == 4 * g + hh, val, 0.0)
        cc, sc, cp, sp = cc_ref[...], sc_ref[...], cp_ref[...], sp_ref[...]
        dsink_ref[...] += dsink
        dq_ref[...] = jnp.concatenate([_rope(t, cc, sc, lt32, inverse=True) for t in dqs], axis=1).astype(BF)
        dkp_ref[...] = jnp.concatenate([_rope(t[:BLK], cp, sp, lt32, inverse=True) for t in dkr], axis=1)
        dkc_ref[...] = jnp.concatenate([_rope(t[BLK:], cc, sc, lt32, inverse=True) for t in dkr], axis=1)
        dvp_ref[...] = jnp.concatenate([t[:BLK] for t in dvr], axis=1)
        dvc_ref[...] = jnp.concatenate([t[BLK:] for t in dvr], axis=1)

    kv = pl.BlockSpec((BLK, 256), _row)
    tc = pl.BlockSpec((BLK, 128), _row)
    tp = pl.BlockSpec((BLK, 128), lambda n: (jnp.maximum(n - 1, 0), 0))
    return pl.pallas_call(
        body, name=name, grid=(nb,),
        in_specs=_attn_specs() + [pl.BlockSpec((N_KV, 4 * BLK, 128), lambda n: (0, n, 0)),
                                  pl.BlockSpec((BLK, D), _row), pl.BlockSpec((BLK, D), _row), tc, tc, tp, tp,
                                  pl.BlockSpec(memory_space=pl.ANY)],
        out_specs=[pl.BlockSpec((BLK, D), lambda n: (n, O_Q // D)), kv, kv, kv, kv, pl.BlockSpec((1, 128), _const2)],
        out_shape=[jax.ShapeDtypeStruct(dproj.shape, BF)] + [jax.ShapeDtypeStruct((T, 256), F32)] * 4
        + [jax.ShapeDtypeStruct((1, 128), F32)],
        input_output_aliases={14: 0},
        compiler_params=_cp(("arbitrary",)),
    )(qs, kr, kr, proj, proj, bias, sinks, lse, o, do, cos, sin, cos, sin, dproj)


def _dkv_combine(dkc, dkp, dvc, dvp, dproj, *, name):
    T = dkc.shape[0]
    nb = T // BLK
    tm = _tile(T, 4 * BLK)
    bpt = tm // BLK
    nt = T // tm

    def body(dkc_ref, dkp_ref, dkn_ref, dvc_ref, dvp_ref, dvn_ref, dproj_ref, o_ref):
        keep = jnp.where(pl.program_id(0) == nt - 1, 0.0, 1.0)

        def shifted(prev_ref, next_ref):
            nxt = keep * next_ref[...]
            return nxt if bpt == 1 else jnp.concatenate([prev_ref[BLK:, :], nxt], axis=0)

        o_ref[:, 0:256] = (dkc_ref[...] + shifted(dkp_ref, dkn_ref)).astype(BF)
        o_ref[:, 256:512] = (dvc_ref[...] + shifted(dvp_ref, dvn_ref)).astype(BF)

    cur = pl.BlockSpec((tm, 256), _row)
    nxt = pl.BlockSpec((BLK, 256), lambda i: (jnp.minimum((i + 1) * bpt, nb - 1), 0))
    return pl.pallas_call(
        body, name=name, grid=(nt,),
        in_specs=[cur, cur, nxt, cur, cur, nxt, pl.BlockSpec(memory_space=pl.ANY)],
        out_specs=pl.BlockSpec((tm, 512), lambda i: (i, O_K // 512)),
        out_shape=jax.ShapeDtypeStruct(dproj.shape, BF),
        input_output_aliases={6: 0},
        compiler_params=_cp(("parallel",)),
    )(dkc, dkp, dkp, dvc, dvp, dvp, dproj)


HALO = 16


def _conv_shifts(cu, hprev, tm):
    row = lax.broadcasted_iota(jnp.int32, (8, cu.shape[1]), 0)
    h1 = hprev[HALO - 1:HALO, :]
    h2 = hprev[HALO - 2:HALO - 1, :]
    m1 = pltpu.roll(cu, 1, 0)
    m2 = pltpu.roll(cu, 2, 0)
    m1 = jnp.concatenate([jnp.where(row == 0, h1, m1[0:8]), m1[8:]], axis=0)
    m2 = jnp.concatenate([jnp.where(row == 0, h2, jnp.where(row == 1, h1, m2[0:8])), m2[8:]], axis=0)
    return m1, m2


def _mixer_mid_fwd(proj, attn, wcp, wap, wout, convw, x, gt, *, tm, name):
    T = x.shape[0]
    tm = _tile(T, tm)
    hb = tm // HALO

    def body(bg_ref, cg_ref, u_ref, hcg_ref, hu_ref, zc0_ref, zc1_ref, za0_ref, za1_ref, at_ref,
             wcp_ref, wap_ref, wout_ref, cw_ref, x_ref, gt_ref,
             x2_ref, gc_ref, yc_ref, ya_ref, mg_ref, o_ref):
        first = jnp.where(pl.program_id(0) == 0, 0.0, 1.0)
        cu = cg_ref[...].astype(F32) * u_ref[...].astype(F32)
        hprev = first * (hcg_ref[...].astype(F32) * hu_ref[...].astype(F32))
        m1, m2 = _conv_shifts(cu, hprev, tm)
        cv = cw_ref[0:1, :] * m2 + cw_ref[1:2, :] * m1 + cw_ref[2:3, :] * cu
        gc = (bg_ref[...].astype(F32) * cv).astype(BF)
        gc_ref[...] = gc
        yc = jnp.dot(gc, wcp_ref[...], preferred_element_type=F32)
        ya = jnp.dot(at_ref[...], wap_ref[...], preferred_element_type=F32)
        yc_ref[...] = yc.astype(BF)
        ya_ref[...] = ya.astype(BF)
        zc = jnp.concatenate([zc0_ref[...], zc1_ref[...]], axis=1).astype(F32)
        za = jnp.concatenate([za0_ref[...], za1_ref[...]], axis=1).astype(F32)
        mg = (_sigmoid(zc) * yc + _sigmoid(za) * ya).astype(BF)
        mg_ref[...] = mg
        o = jnp.dot(mg, wout_ref[...], preferred_element_type=F32)
        o_ref[...] = o.astype(BF)
        x2_ref[...] = x_ref[...] + gt_ref[...] * o

    wspec = pl.BlockSpec((D, D), _const2)
    rowspec = pl.BlockSpec((tm, D), _row)
    return pl.pallas_call(
        body, name=name, grid=(T // tm,),
        in_specs=[_col(tm, O_BG), _col(tm, O_CG), _col(tm, O_U), _halo_prev(hb, O_CG), _halo_prev(hb, O_U),
                  _col(tm, O_ZC, 512), _col(tm, O_ZC + 512, 512), _col(tm, O_ZA, 512), _col(tm, O_ZA + 512, 512),
                  rowspec, wspec, wspec, wspec, pl.BlockSpec((8, D), _const2), rowspec, pl.BlockSpec((1, D), _const2)],
        out_specs=[rowspec] * 6,
        out_shape=[jax.ShapeDtypeStruct((T, D), F32)] + [jax.ShapeDtypeStruct((T, D), BF)] * 5,
        compiler_params=_cp(("parallel",)),
    )(proj, proj, proj, proj, proj, proj, proj, proj, proj, attn, wcp, wap, wout, convw, x, gt)


def _col(tm, c, w=D):
    assert c % w == 0
    return pl.BlockSpec((tm, w), lambda i: (i, c // w))


def _halo_prev(hb, c):
    return pl.BlockSpec((HALO, D), lambda i: (jnp.maximum(i * hb - 1, 0), c // D))


def _halo_next(hb, nblk, c=0):
    return pl.BlockSpec((HALO, D), lambda i: (jnp.minimum((i + 1) * hb, nblk - 1), c // D))


def _mixer_mid_bwd(dx2, gt, o, proj, yc, ya, wout, wcp, wap, *, tm, name):
    T = dx2.shape[0]
    tm = _tile(T, tm)
    nt = T // tm

    def body(dx_ref, gt_ref, o_ref, zc0_ref, zc1_ref, za0_ref, za1_ref, yc_ref, ya_ref, wout_ref, wcp_ref, wap_ref,
             dout_ref, dyc_ref, dya_ref, dgc_ref, dat_ref, dproj_ref, dgt_ref, dzs, sems):
        i = pl.program_id(0)
        slot = lax.rem(i, 2)

        def slab_copy(step, s):
            return pltpu.make_async_copy(
                dzs.at[s], dproj_ref.at[pl.ds(pl.multiple_of(step * tm, tm), tm), pl.ds(O_ZC, 2 * D)], sems.at[s])

        @pl.when(i == 0)
        def _():
            dgt_ref[...] = jnp.zeros_like(dgt_ref)

        dxv = dx_ref[...]
        dgt_ref[...] += jnp.sum(dxv * o_ref[...].astype(F32), axis=0, keepdims=True)
        dout = (gt_ref[...] * dxv).astype(BF)
        dout_ref[...] = dout
        dmg = lax.dot_general(dout, wout_ref[...], NT, preferred_element_type=F32)
        sc = _sigmoid(jnp.concatenate([zc0_ref[...], zc1_ref[...]], axis=1).astype(F32))
        sa = _sigmoid(jnp.concatenate([za0_ref[...], za1_ref[...]], axis=1).astype(F32))
        dyc = (dmg * sc).astype(BF)
        dya = (dmg * sa).astype(BF)
        dyc_ref[...] = dyc
        dya_ref[...] = dya
        dzs[slot, :, 0:D] = (dmg * yc_ref[...].astype(F32) * (sc * (1.0 - sc))).astype(BF)
        dzs[slot, :, D:2 * D] = (dmg * ya_ref[...].astype(F32) * (sa * (1.0 - sa))).astype(BF)
        slab_copy(i, slot).start()
        dgc_ref[...] = lax.dot_general(dyc, wcp_ref[...], NT, preferred_element_type=F32).astype(BF)
        dat_ref[...] = lax.dot_general(dya, wap_ref[...], NT, preferred_element_type=F32).astype(BF)

        @pl.when(i > 0)
        def _():
            slab_copy(i - 1, 1 - slot).wait()

        @pl.when(i == nt - 1)
        def _():
            slab_copy(i, slot).wait()

    def zcol(c):
        return pl.BlockSpec((tm, 512), lambda i: (i, c // 512))

    wspec = pl.BlockSpec((D, D), _const2)
    rowspec = pl.BlockSpec((tm, D), _row)
    vec = pl.BlockSpec((1, D), _const2)
    return pl.pallas_call(
        body, name=name, grid=(nt,),
        in_specs=[rowspec, vec, rowspec, zcol(O_ZC), zcol(O_ZC + 512), zcol(O_ZA), zcol(O_ZA + 512),
                  rowspec, rowspec, wspec, wspec, wspec],
        out_specs=[rowspec] * 5 + [pl.BlockSpec(memory_space=pl.ANY), vec],
        out_shape=[jax.ShapeDtypeStruct((T, D), BF)] * 5 + [jax.ShapeDtypeStruct((T, NIN), BF),
                                                            jax.ShapeDtypeStruct((1, D), F32)],
        scratch_shapes=[pltpu.VMEM((2, tm, 2 * D), BF), pltpu.SemaphoreType.DMA((2,))],
        compiler_params=_cp(("arbitrary",)),
    )(dx2, gt, o, proj, proj, proj, proj, yc, ya, wout, wcp, wap)


def _conv_bwd(dgc, proj, convw, dproj, *, tm, name):
    T = dgc.shape[0]
    tm = _tile(T, tm)
    hb = tm // HALO
    nblk = T // HALO
    nt = T // tm

    def body(dgc_ref, ndgc_ref, bg_ref, nbg_ref, cg_ref, u_ref, hcg_ref, hu_ref, cw_ref, dproj_ref, dp_ref, dcw_ref):
        i = pl.program_id(0)

        @pl.when(i == 0)
        def _():
            dcw_ref[...] = jnp.zeros_like(dcw_ref)
        first = jnp.where(i == 0, 0.0, 1.0)
        last = jnp.where(i == nt - 1, 0.0, 1.0)
        cg = cg_ref[...].astype(F32)
        u = u_ref[...].astype(F32)
        bg = bg_ref[...].astype(F32)
        dg = dgc_ref[...].astype(F32)
        cu = cg * u
        hprev = first * (hcg_ref[...].astype(F32) * hu_ref[...].astype(F32))
        m1, m2 = _conv_shifts(cu, hprev, tm)
        w0, w1, w2 = cw_ref[0:1, :], cw_ref[1:2, :], cw_ref[2:3, :]
        cv = w0 * m2 + w1 * m1 + w2 * cu
        dcv = dg * bg
        nxt = last * (ndgc_ref[...].astype(F32) * nbg_ref[...].astype(F32))
        n0, n1 = nxt[0:1, :], nxt[1:2, :]
        row = lax.broadcasted_iota(jnp.int32, (8, D), 0)
        p1 = pltpu.roll(dcv, tm - 1, 0)
        p2 = pltpu.roll(dcv, tm - 2, 0)
        p1 = jnp.concatenate([p1[:tm - 8], jnp.where(row == 7, n0, p1[tm - 8:])], axis=0)
        p2 = jnp.concatenate([p2[:tm - 8], jnp.where(row == 7, n1, jnp.where(row == 6, n0, p2[tm - 8:]))], axis=0)
        dcu = w2 * dcv + w1 * p1 + w0 * p2
        dp_ref[:, 0:D] = (dg * cv).astype(BF)
        dp_ref[:, D:2 * D] = (dcu * u).astype(BF)
        dp_ref[:, 2 * D:3 * D] = (dcu * cg).astype(BF)
        dcw_ref[0:1, :] += jnp.sum(dcv * m2, axis=0, keepdims=True)
        dcw_ref[1:2, :] += jnp.sum(dcv * m1, axis=0, keepdims=True)
        dcw_ref[2:3, :] += jnp.sum(dcv * cu, axis=0, keepdims=True)

    rowspec = pl.BlockSpec((tm, D), _row)
    cw = pl.BlockSpec((8, D), _const2)
    return pl.pallas_call(
        body, name=name, grid=(nt,),
        in_specs=[rowspec, _halo_next(hb, nblk), _col(tm, O_BG), _halo_next(hb, nblk, O_BG),
                  _col(tm, O_CG), _col(tm, O_U), _halo_prev(hb, O_CG), _halo_prev(hb, O_U), cw,
                  pl.BlockSpec(memory_space=pl.ANY)],
        out_specs=[pl.BlockSpec((tm, 3 * D), _row), cw],
        out_shape=[jax.ShapeDtypeStruct(dproj.shape, BF), jax.ShapeDtypeStruct((8, D), F32)],
        input_output_aliases={9: 0},
        compiler_params=_cp(("arbitrary",)),
    )(dgc, dgc, proj, proj, proj, proj, proj, proj, convw, dproj)


def _adam_math(w, g, m, v):
    nm = ADAM_B1 * m + (1.0 - ADAM_B1) * g
    nv = ADAM_B2 * v + (1.0 - ADAM_B2) * (g * g)
    m_hat = nm / (1.0 - ADAM_B1 ** ADAM_STEP)
    v_hat = nv / (1.0 - ADAM_B2 ** ADAM_STEP)
    return -ADAM_LR * (m_hat / (jnp.sqrt(v_hat) + ADAM_EPS) + ADAM_WD * w), nm, nv


SMALL = ("b_ada", "g_ffn1", "g_mix", "g_ffn2", "g_final", "conv_w", "sinks")


def _adam_small(gsum, conv_g, w, m, v, *, name):
    nsm = len(SMALL)

    def body(*refs):
        gs_ref, cg_ref = refs[0], refs[1]
        w_refs, m_refs, v_refs = (refs[2 + k * nsm:2 + (k + 1) * nsm] for k in range(3))
        outs = refs[2 + 3 * nsm:]
        for p, n in enumerate(SMALL):
            if n == "b_ada":
                pieces = [(slice(None), slice(r * D, (r + 1) * D), gs_ref[R_MODS + r:R_MODS + r + 1, :])
                          for r in range(N_MOD)]
            elif n == "conv_w":
                pieces = [(slice(None), slice(None), cg_ref[...])]
            elif n == "sinks":
                pieces = [(slice(None), slice(None), gs_ref[R_SINK:R_SINK + 1, 0:N_HEADS])]
            else:
                row = dict(g_ffn1=R_G1, g_mix=R_GM, g_ffn2=R_G2, g_final=R_GF)[n]
                pieces = [(slice(None), slice(None), gs_ref[row:row + 1, :])]
            for rs, cs, g in pieces:
                d, nm, nv = _adam_math(w_refs[p][rs, cs], g, m_refs[p][rs, cs], v_refs[p][rs, cs])
                for k, val in enumerate((g, d, nm, nv)):
                    outs[k * nsm + p][rs, cs] = val

    args = [gsum, conv_g] + [d[n] for d in (w, m, v) for n in SMALL]
    shapes = [jax.ShapeDtypeStruct(w[n].shape, F32) for _ in range(4) for n in SMALL]
    res = pl.pallas_call(body, name=name, out_shape=shapes, compiler_params=_cp())(*args)
    return [dict(zip(SMALL, res[k * nsm:(k + 1) * nsm])) for k in range(4)]


def _adam(w, g, m, v, *, tm, name):
    _, R, C = w.shape
    tm = _tile(R, tm)
    parts = g.ndim == 3

    def body(w_ref, g_ref, m_ref, v_ref, go_ref, d_ref, nm_ref, nv_ref):
        if parts:
            gv = g_ref[0].astype(F32)
            for s in range(1, N_DEV):
                gv = gv + g_ref[s].astype(F32)
        else:
            gv = g_ref[...]
        go_ref[0] = gv
        d_ref[0], nm_ref[0], nv_ref[0] = _adam_math(w_ref[0], gv, m_ref[0], v_ref[0])

    spec = pl.BlockSpec((1, tm, C), lambda i: (0, i, 0))
    gspec = pl.BlockSpec((N_DEV, tm, C), lambda i: (0, i, 0)) if parts else pl.BlockSpec((tm, C), _row)
    return pl.pallas_call(
        body, name=name, grid=(R // tm,),
        in_specs=[spec, gspec, spec, spec], out_specs=[spec] * 4,
        out_shape=[jax.ShapeDtypeStruct((1, R, C), F32)] * 4,
        compiler_params=_cp(("parallel",)),
    )(w, g, m, v)


def _mods_part(c_all, w_ada, b_ada, *, name):
    C = w_ada.shape[1]

    def body(c_ref, w_ref, b_ref, o_ref):
        cv = c_ref[...]
        ca = cv * jax.nn.sigmoid(cv)
        o_ref[...] = jnp.dot(ca, w_ref[...], preferred_element_type=F32,
                             precision=lax.Precision.HIGHEST) + b_ref[...]

    return pl.pallas_call(
        body, name=name,
        out_shape=jax.ShapeDtypeStruct((N_DEV, C), F32),
        compiler_params=_cp(),
    )(c_all, w_ada, b_ada)


def _wada_grad(c_all_t, gm, *, name):
    C = gm.shape[1]

    def body(c_ref, g_ref, o_ref):
        cv = c_ref[...]
        ca = cv * jax.nn.sigmoid(cv)
        acc = ca[:, 0:1] * g_ref[0:1, :]
        for b in range(1, N_DEV):
            acc = acc + ca[:, b:b + 1] * g_ref[b:b + 1, :]
        o_ref[...] = acc

    return pl.pallas_call(
        body, name=name,
        out_shape=jax.ShapeDtypeStruct((D, C), F32),
        compiler_params=_cp(),
    )(c_all_t, gm)


def _peer(x, y, c, d):
    px = lax.rem(x + ((d >> 2) & 1), 2)
    py = lax.rem(y + ((d >> 1) & 1), 2)
    pc = lax.rem(c + (d & 1), 2)
    return (px, py, pc), 4 * px + 2 * py + pc


def _exchange(xs, *, scatter, name):
    n = len(xs)
    nsem = n * (N_DEV - 1)

    def body(*refs):
        ins, outs = refs[:n], refs[n:2 * n]
        token, send_sems, recv_sems, local_sems = refs[2 * n:]
        x, y, c = lax.axis_index("x"), lax.axis_index("y"), lax.axis_index("c")
        me = 4 * x + 2 * y + c
        token[...] = jnp.zeros_like(token)

        def src(t, idx):
            return ins[t].at[idx] if scatter else ins[t]

        local = [pltpu.make_async_copy(src(t, me), outs[t].at[me], local_sems.at[t]) for t in range(n)]
        for cp in local:
            cp.start()
        remote = []
        for t in range(n):
            for d in range(1, N_DEV):
                peer, pidx = _peer(x, y, c, d)
                k = t * (N_DEV - 1) + d - 1
                send = pltpu.make_async_remote_copy(src_ref=src(t, pidx), dst_ref=outs[t].at[me],
                                                    send_sem=send_sems.at[k], recv_sem=recv_sems.at[k],
                                                    device_id=peer, device_id_type=MESH)
                recv = pltpu.make_async_remote_copy(src_ref=src(t, pidx), dst_ref=outs[t].at[pidx],
                                                    send_sem=send_sems.at[k], recv_sem=recv_sems.at[k],
                                                    device_id=peer, device_id_type=MESH)
                send.start()
                remote.append((send, recv))
        for cp in local:
            cp.wait()
        for send, recv in remote:
            send.wait_send()
            recv.wait_recv()

    anyspec = pl.BlockSpec(memory_space=pl.ANY)
    out_shape = [jax.ShapeDtypeStruct(a.shape if scatter else (N_DEV,) + a.shape, a.dtype) for a in xs]
    out_shape.append(jax.ShapeDtypeStruct((8, 128), F32))
    return pl.pallas_call(
        body, name=name,
        in_specs=[anyspec] * n, out_specs=[anyspec] * n + [pl.BlockSpec(memory_space=pltpu.VMEM)],
        out_shape=out_shape,
        scratch_shapes=[pltpu.SemaphoreType.DMA((nsem,)), pltpu.SemaphoreType.DMA((nsem,)),
                        pltpu.SemaphoreType.DMA((n,))],
    )(*xs)


def _sum8(parts, *, name):
    _, R, C = parts.shape

    def body(p_ref, o_ref):
        acc = p_ref[0]
        for s in range(1, N_DEV):
            acc = acc + p_ref[s]
        o_ref[...] = acc

    return pl.pallas_call(body, name=name, out_shape=jax.ShapeDtypeStruct((R, C), F32),
                          compiler_params=_cp())(parts)


HBM_SPEC = pl.BlockSpec(memory_space=pltpu.HBM)
SEM_SPEC = pl.BlockSpec(memory_space=pltpu.SEMAPHORE)
N_PEER = N_DEV - 1


def _split_copies(src_refs, land_refs, send_sems, recv_sems, scatter):
    x, y, c = lax.axis_index("x"), lax.axis_index("y"), lax.axis_index("c")
    me = 4 * x + 2 * y + c
    pairs = []
    for j, (src, land) in enumerate(zip(src_refs, land_refs)):
        for d in range(1, N_DEV):
            peer, pidx = _peer(x, y, c, d)
            k = j * N_PEER + d - 1
            s = src.at[pidx] if scatter else src
            send = pltpu.make_async_remote_copy(src_ref=s, dst_ref=land.at[me], send_sem=send_sems.at[k],
                                                recv_sem=recv_sems.at[k], device_id=peer, device_id_type=MESH)
            recv = pltpu.make_async_remote_copy(src_ref=s, dst_ref=land.at[pidx], send_sem=send_sems.at[k],
                                                recv_sem=recv_sems.at[k], device_id=peer, device_id_type=MESH)
            pairs.append((send, recv))
    return pairs


def _own_slot(block, me):
    land = lax.empty((N_DEV,) + block.shape, block.dtype)
    return lax.dynamic_update_slice(land, block[None], (me, 0, 0))


def _split_start(srcs, lands, groups, *, scatter, name):
    n, ng = len(srcs), len(groups)

    def body(*refs):
        src_refs, land_refs = refs[:n], refs[n:2 * n]
        sems = refs[2 * n:2 * n + 2 * ng]
        token = refs[-1]
        for gi, g in enumerate(groups):
            pairs = _split_copies([src_refs[t] for t in g], [land_refs[t] for t in g], sems[2 * gi],
                                  sems[2 * gi + 1], scatter)
            for send, _ in pairs:
                send.start()
        token[...] = jnp.zeros_like(token)

    sem_shapes = []
    for g in groups:
        sem_shapes += [pltpu.SemaphoreType.DMA((len(g) * N_PEER,))] * 2
    thru = [pltpu.HBM(a.shape, a.dtype) for a in list(srcs) + list(lands)]
    outs = pl.pallas_call(
        body, name=name,
        out_shape=tuple(sem_shapes + thru + [jax.ShapeDtypeStruct((8, 128), F32)]),
        in_specs=[HBM_SPEC] * (2 * n),
        out_specs=tuple([SEM_SPEC] * (2 * ng) + [HBM_SPEC] * (2 * n) + [pl.BlockSpec(memory_space=pltpu.VMEM)]),
        input_output_aliases={i: 2 * ng + i for i in range(2 * n)},
        compiler_params=pltpu.CompilerParams(has_side_effects=pltpu.SideEffectType.DATAFLOW_SIDE_EFFECTING),
    )(*[pltpu.with_memory_space_constraint(a, pltpu.HBM) for a in list(srcs) + list(lands)])
    sems = [(outs[2 * gi], outs[2 * gi + 1]) for gi in range(ng)]
    return sems, outs[2 * ng:2 * ng + n], outs[2 * ng + n:2 * ng + 2 * n], outs[-1]


def _behind(v, token):
    if token is None:
        return v
    return v + token[0, 0].astype(v.dtype)


def _split_wait(srcs, lands, sems, after, *, scatter, name):
    m = len(srcs)

    def body(*refs):
        src_refs, land_refs = refs[:m], refs[m:2 * m]
        send_sems, recv_sems = refs[2 * m], refs[2 * m + 1]
        for send, recv in _split_copies(src_refs, land_refs, send_sems, recv_sems, scatter):
            send.wait_send()
            recv.wait_recv()

    outs = pl.pallas_call(
        body, name=name,
        out_shape=tuple(pltpu.HBM(a.shape, a.dtype) for a in list(srcs) + list(lands)),
        in_specs=[HBM_SPEC] * (2 * m) + [SEM_SPEC, SEM_SPEC, pl.BlockSpec(memory_space=pl.ANY)],
        out_specs=tuple([HBM_SPEC] * (2 * m)),
        input_output_aliases={i: i for i in range(2 * m)},
        compiler_params=pltpu.CompilerParams(has_side_effects=pltpu.SideEffectType.DATAFLOW_SIDE_EFFECTING),
    )(*srcs, *lands, sems[0], sems[1], after)
    return outs[m:]


TL_FIRST = (1, 2, 4, 6)
TL_ICI = (2, 4, 6)
EFFECT = pltpu.SideEffectType.DATAFLOW_SIDE_EFFECTING


def _tl_first(src_refs, land_refs, send_sems, recv_sems):
    x, y, c = lax.axis_index("x"), lax.axis_index("y"), lax.axis_index("c")
    me = 4 * x + 2 * y + c
    out = []
    for j, (src, land) in enumerate(zip(src_refs, land_refs)):
        for i, d in enumerate(TL_FIRST):
            peer, pidx = _peer(x, y, c, d)
            k = len(TL_FIRST) * j + i
            send = pltpu.make_async_remote_copy(src_ref=src, dst_ref=land.at[me], send_sem=send_sems.at[k],
                                                recv_sem=recv_sems.at[k], device_id=peer, device_id_type=MESH)
            recv = pltpu.make_async_remote_copy(src_ref=src, dst_ref=land.at[pidx], send_sem=send_sems.at[k],
                                                recv_sem=recv_sems.at[k], device_id=peer, device_id_type=MESH)
            out.append((d, send, recv))
    return out


def _tl_second(land_refs, send_sems, recv_sems):
    x, y, c = lax.axis_index("x"), lax.axis_index("y"), lax.axis_index("c")
    sibling, _ = _peer(x, y, c, 1)
    out = []
    for j, land in enumerate(land_refs):
        for i, d in enumerate(TL_ICI):
            _, mine = _peer(x, y, c, d)
            _, theirs = _peer(x, y, c, d + 1)
            k = len(TL_ICI) * j + i
            send = pltpu.make_async_remote_copy(src_ref=land.at[mine], dst_ref=land.at[mine], send_sem=send_sems.at[k],
                                                recv_sem=recv_sems.at[k], device_id=sibling, device_id_type=MESH)
            recv = pltpu.make_async_remote_copy(src_ref=land.at[mine], dst_ref=land.at[theirs],
                                                send_sem=send_sems.at[k], recv_sem=recv_sems.at[k],
                                                device_id=sibling, device_id_type=MESH)
            out.append((send, recv))
    return out


def _tl_start(srcs, lands, groups, *, name):
    n, ng = len(srcs), len(groups)

    def body(*refs):
        src_refs, land_refs = refs[:n], refs[n:2 * n]
        sems = refs[2 * n:2 * n + 2 * ng]
        for gi, g in enumerate(groups):
            for _, send, _ in _tl_first([src_refs[t] for t in g], [land_refs[t] for t in g], sems[2 * gi],
                                        sems[2 * gi + 1]):
                send.start()
        refs[-1][...] = jnp.zeros_like(refs[-1])

    sem_shapes = []
    for g in groups:
        sem_shapes += [pltpu.SemaphoreType.DMA((len(g) * len(TL_FIRST),))] * 2
    thru = [pltpu.HBM(a.shape, a.dtype) for a in list(srcs) + list(lands)]
    outs = pl.pallas_call(
        body, name=name,
        out_shape=tuple(sem_shapes + thru + [jax.ShapeDtypeStruct((8, 128), F32)]),
        in_specs=[HBM_SPEC] * (2 * n),
        out_specs=tuple([SEM_SPEC] * (2 * ng) + [HBM_SPEC] * (2 * n) + [pl.BlockSpec(memory_space=pltpu.VMEM)]),
        input_output_aliases={i: 2 * ng + i for i in range(2 * n)},
        compiler_params=pltpu.CompilerParams(has_side_effects=EFFECT),
    )(*[pltpu.with_memory_space_constraint(a, pltpu.HBM) for a in list(srcs) + list(lands)])
    sems = [(outs[2 * gi], outs[2 * gi + 1]) for gi in range(ng)]
    return sems, outs[2 * ng:2 * ng + n], outs[2 * ng + n:2 * ng + 2 * n], outs[-1]


def _tl_forward(srcs, lands, sems1, after, *, name):
    m = len(srcs)

    def body(*refs):
        src_refs, land_refs = refs[:m], refs[m:2 * m]
        send1, recv1 = refs[2 * m], refs[2 * m + 1]
        send2, recv2 = refs[2 * m + 3], refs[2 * m + 4]
        for d, _, recv in _tl_first(src_refs, land_refs, send1, recv1):
            if d in TL_ICI:
                recv.wait_recv()
        for send, _ in _tl_second(land_refs, send2, recv2):
            send.start()

    sem = pltpu.SemaphoreType.DMA((m * len(TL_ICI),))
    outs = pl.pallas_call(
        body, name=name,
        out_shape=tuple([sem, sem] + [pltpu.HBM(a.shape, a.dtype) for a in list(srcs) + list(lands)]),
        in_specs=[HBM_SPEC] * (2 * m) + [SEM_SPEC, SEM_SPEC, pl.BlockSpec(memory_space=pl.ANY)],
        out_specs=tuple([SEM_SPEC, SEM_SPEC] + [HBM_SPEC] * (2 * m)),
        input_output_aliases={i: 2 + i for i in range(2 * m)},
        compiler_params=pltpu.CompilerParams(has_side_effects=EFFECT),
    )(*srcs, *lands, sems1[0], sems1[1], after)
    return (outs[0], outs[1]), outs[2:2 + m], outs[2 + m:2 + 2 * m]


def _tl_wait(srcs, lands, sems1, sems2, after, *, name):
    m = len(srcs)

    def body(*refs):
        src_refs, land_refs = refs[:m], refs[m:2 * m]
        send1, recv1, send2, recv2 = refs[2 * m:2 * m + 4]
        for d, send, recv in _tl_first(src_refs, land_refs, send1, recv1):
            send.wait_send()
            if d not in TL_ICI:
                recv.wait_recv()
        for send, recv in _tl_second(land_refs, send2, recv2):
            send.wait_send()
            recv.wait_recv()

    outs = pl.pallas_call(
        body, name=name,
        out_shape=tuple(pltpu.HBM(a.shape, a.dtype) for a in list(srcs) + list(lands)),
        in_specs=[HBM_SPEC] * (2 * m) + [SEM_SPEC] * 4 + [pl.BlockSpec(memory_space=pl.ANY)],
        out_specs=tuple([HBM_SPEC] * (2 * m)),
        input_output_aliases={i: i for i in range(2 * m)},
        compiler_params=pltpu.CompilerParams(has_side_effects=EFFECT),
    )(*srcs, *lands, sems1[0], sems1[1], sems2[0], sems2[1], after)
    return outs[m:]


TM_PROJ = 512
TN_PROJ = 512
TM_ROW = 512
TM_NN = 512
TK_TN = 2048
TM_ADAM = 416
TN_FFN = F // 2
TN_IN = NIN // 4


def _tn(a, b, name, tn, token=None):
    if a.ndim == 2:
        a = a[None]
    return _tn_matmul(a, b, token, tn=tn, tk=TK_TN, name=name)


def _local_step(x, tgt, mods, g1, gm, g2, gf, convw8, sinks, w_get, g_put, tables=None):
    T = x.shape[0]
    sh1, sc1, gt1, sh2, sc2, gt2, sh3, sc3, gt3 = [mods[i:i + 1] for i in range(N_MOD)]
    cos, sin = _rope_tables(T) if tables is None else tables
    behind = _behind

    w = dict(w_get("gu1", mods))
    h1, ab1 = _norm_proj(x, g1, sc1, sh1, w["gu1"], tm=TM_PROJ, tn=TN_PROJ, name="ffn1_up")
    w.update(w_get("d1", ab1))
    x1, y1 = _ffn_down_fwd(ab1, w["d1"], x, gt1, tm=TM_ROW, name="ffn1_down")
    w.update(w_get("mix", x1))
    h2, proj = _norm_proj(x1, gm, sc2, sh2, w["win"], tm=TM_PROJ, tn=TN_PROJ, name="mix_in")
    qs, kr = _attn_prep(proj, cos, sin, name="attn_prep")
    bias = _attn_bias()
    attn, lse = _attn_fwd(qs, kr, proj, bias, sinks, name="attn_fwd")
    x2, gc, yc, ya, mg, o = _mixer_mid_fwd(proj, attn, w["cp"], w["ap"], w["out"], convw8, x1, gt2,
                                           tm=TM_ROW, name="mix_mid")
    w.update(w_get("ffn2", x2))
    h3, ab2, y2, dx3, lsum, dgf = _ffn_fwd(x2, g2, sc3, sh3, gt3, w["gu2"], w["d2"], (tgt, gf), tm=TM_ROW,
                                           name="ffn2_final")

    dab2, dgt3, g_d2 = _ffn_down_bwd_dw(dx3, y2, gt3, ab2, w["d2"], tm=TM_ROW, name="ffn2_down_bwd")
    dx2, dsh3, dsc3, dg2 = _nn_bwd_norm(dab2, w["gu2"], x2, g2, sc3, dx3, tm=TM_NN, name="ffn2_up_bwd")
    g_gu2 = _tn(dab2, h3, "ffn2_up_dw", TN_FFN)
    tok = g_put(dict(gu2=g_gu2, d2=g_d2))

    dout, dyc, dya, dgc, dat, dproj, dgt2 = _mixer_mid_bwd(dx2, behind(gt2, tok), o, proj, yc, ya, w["out"], w["cp"],
                                                           w["ap"], tm=TM_ROW, name="mix_mid_bwd")
    g_out = _tn(mg, dout, "mix_out_dw", D)
    g_cp = _tn(gc, dyc, "mix_cp_dw", D)
    g_ap = _tn(attn, dya, "mix_ap_dw", D)
    dproj, dkc, dkp, dvc, dvp, dsink = _attn_bwd(qs, kr, proj, bias, sinks, lse, attn, dat, cos, sin, dproj,
                                                 name="attn_bwd")
    dproj = _dkv_combine(dkc, dkp, dvc, dvp, dproj, name="attn_dkv")
    dproj, dcw = _conv_bwd(dgc, proj, convw8, dproj, tm=TM_ROW, name="conv_bwd")
    g_in = _tn(dproj, h2, "mix_in_dw", TN_IN)
    tok = g_put(dict(win=g_in, cp=g_cp, ap=g_ap, out=g_out))
    dx1, dsh2, dsc2, dgm = _nn_bwd_norm(dproj[None], w["win"], x1, gm, behind(sc2, tok), dx2, tm=TM_NN,
                                        name="mix_in_bwd")

    dab1, dgt1, g_d1 = _ffn_down_bwd_dw(dx1, y1, gt1, ab1, w["d1"], tm=TM_ROW, name="ffn1_down_bwd")
    tok = g_put(dict(d1=g_d1))
    g_gu1 = _tn(dab1, h1, "ffn1_up_dw", TN_FFN, tok)
    tok = g_put(dict(gu1=g_gu1))
    dx0, dsh1, dsc1, dg1 = _nn_bwd_norm(dab1, w["gu1"], x, g1, behind(sc1, tok), dx1, tm=TM_NN,
                                        name="ffn1_up_bwd")

    small = dict(mods=jnp.concatenate([dsh1, dsc1, dgt1, dsh2, dsc2, dgt2, dsh3, dsc3, dgt3], axis=0),
                 g1=dg1, gm=dgm, g2=dg2, gf=dgf, convw=dcw[0:3], sinks=dsink[:, 0:N_HEADS])
    return lsum, dx0, small


BIG = ("gu1", "d1", "win", "cp", "ap", "out", "gu2", "d2")
TRANSPOSED = ("gu1", "win", "gu2")
SMALL_ROWS = 24
R_MODS, R_G1, R_GM, R_G2, R_GF, R_CONV, R_SINK = 0, 9, 10, 11, 12, 13, 16


def _pad_to(a, rows, cols):
    return jnp.pad(a, ((0, rows - a.shape[0]), (0, cols - a.shape[1])))


def _pack_small(b_ada, g1, gm, g2, gf, conv, sinks):
    rows = [b_ada.reshape(N_MOD, D), g1.reshape(1, D), gm.reshape(1, D), g2.reshape(1, D), gf.reshape(1, D),
            _pad_to(conv.reshape(3, -1), 3, D), _pad_to(sinks.reshape(1, N_HEADS), 1, D)]
    return _pad_to(jnp.concatenate(rows, axis=0), SMALL_ROWS, D)


def kernel(x, c, w_ada, b_ada, g_ffn1, w1_gu, w1_down, g_mix, w_in, conv_w, w_conv_proj, w_attn_proj, sinks, w_out, g_ffn2, w2_gu, w2_down, g_final, loss_target, m_w_ada, m_b_ada, m_g_ffn1, m_w1_gu, m_w1_down, m_g_mix, m_w_in, m_conv_w, m_w_conv_proj, m_w_attn_proj, m_sinks, m_w_out, m_g_ffn2, m_w2_gu, m_w2_down, m_g_final, v_w_ada, v_b_ada, v_g_ffn1, v_w1_gu, v_w1_down, v_g_mix, v_w_in, v_conv_w, v_w_conv_proj, v_w_attn_proj, v_sinks, v_w_out, v_g_ffn2, v_w2_gu, v_w2_down, v_g_final):
    me = 4 * lax.axis_index("x") + 2 * lax.axis_index("y") + lax.axis_index("c")
    ada_cols = w_ada.shape[2]
    conv_cols = conv_w.shape[2]

    native = dict(gu1=w1_gu[0], d1=w1_down[0], win=w_in[0], cp=w_conv_proj[0], ap=w_attn_proj[0], out=w_out[0],
                  gu2=w2_gu[0], d2=w2_down[0])

    def shard(n, token):
        a = _behind(native[n], token)
        return (a.T if n in TRANSPOSED else a).astype(BF)

    c_all, conv_all, _ = _exchange([c, _pad_to(conv_w[0], 8, conv_cols)], scatter=False, name="gather_cond")
    c_all = c_all.reshape(N_DEV, D)
    conv_full = conv_all[:, 0:3, :].transpose(1, 0, 2).reshape(3, D)

    b_cols = lax.dynamic_slice(b_ada, (0, me * ada_cols), (1, ada_cols))
    mods_cols = _mods_part(c_all, w_ada[0], b_cols, name="ada_mods")
    mods_all, mods_token = _exchange([mods_cols], scatter=False, name="gather_mods")
    mods = lax.dynamic_index_in_dim(mods_all, me, axis=1, keepdims=False).reshape(N_MOD, D)

    groups = dict(gu1=("gu1",), d1=("d1",), mix=("win", "cp", "ap", "out"), ffn2=("gu2", "d2"))
    in_flight = {}
    first = [shard("gu1", mods_token)]
    sems, srcs, lands, token = _tl_start(first, [_own_slot(s, me) for s in first], [[0]],
                                         name="gather_weights_start_gu1")
    in_flight["gu1"] = [sems[0], srcs, lands, None]
    rest = [n for n in BIG if n != "gu1"]
    shards = [shard(n, token) for n in rest]
    rest_groups = [[rest.index(n) for n in names] for g, names in groups.items() if g != "gu1"]
    sems, srcs, lands, rest_token = _tl_start(shards, [_own_slot(s, me) for s in shards], rest_groups,
                                              name="gather_weights_start_rest")
    for (g, names), gsems, idx in zip([kv for kv in groups.items() if kv[0] != "gu1"], sems, rest_groups):
        in_flight[g] = [gsems, [srcs[t] for t in idx], [lands[t] for t in idx], None]

    def forward(group, after):
        sems1, gsrcs, glands, _ = in_flight[group]
        sems2, gsrcs, glands = _tl_forward(gsrcs, glands, sems1, after, name="gather_weights_forward_" + group)
        in_flight[group] = [sems1, gsrcs, glands, sems2]

    forward_early = dict(d1="mix", mix="ffn2")

    tables = _rope_tables(x.shape[1], rest_token)

    def w_get(group, after):
        if group == "gu1":
            after = tables[0]
        if in_flight[group][3] is None:
            forward(group, after)
        sems1, gsrcs, glands, sems2 = in_flight[group]
        landed = _tl_wait(gsrcs, glands, sems1, sems2, after, name="gather_weights_wait_" + group)
        if group in forward_early:
            forward(forward_early[group], landed[0])
        return {n: a.reshape(-1, D) for n, a in zip(groups[group], landed)}

    pending = []

    def g_put(gs):
        names = tuple(gs)
        srcs = [gs[n].reshape(N_DEV, -1, D) for n in names]
        lands = [_own_slot(lax.dynamic_index_in_dim(s, me, axis=0, keepdims=False), me) for s in srcs]
        sems, srcs, lands, tok = _split_start(srcs, lands, [list(range(len(names)))], scatter=True,
                                              name="scatter_grads_start_" + names[0])
        pending.append((names, sems[0], srcs, lands))
        return tok

    lsum, grad_x, small = _local_step(x[0], loss_target[0], mods, g_ffn1, g_mix, g_ffn2, g_final[None],
                                      _pad_to(conv_full, 8, D), sinks[0], w_get, g_put, tables)
    loss = lax.psum((0.5 / D) * jnp.sum(lsum), ("x", "y", "c"))

    packed = _pack_small(small["mods"], small["g1"], small["gm"], small["g2"], small["gf"], small["convw"],
                         small["sinks"])
    packed_all, _ = _exchange([packed], scatter=False, name="gather_small")
    gsmall = _sum8(packed_all, name="sum_small")

    w_of = dict(ada=w_ada, gu1=w1_gu, d1=w1_down, win=w_in, cp=w_conv_proj, ap=w_attn_proj, out=w_out, gu2=w2_gu,
                d2=w2_down)
    m_of = dict(ada=m_w_ada, gu1=m_w1_gu, d1=m_w1_down, win=m_w_in, cp=m_w_conv_proj, ap=m_w_attn_proj, out=m_w_out,
                gu2=m_w2_gu, d2=m_w2_down)
    v_of = dict(ada=v_w_ada, gu1=v_w1_gu, d1=v_w1_down, win=v_w_in, cp=v_w_conv_proj, ap=v_w_attn_proj, out=v_w_out,
                gu2=v_w2_gu, d2=v_w2_down)
    upd = {}
    after = gsmall
    for names, sems, srcs, lands in pending:
        parts = _split_wait(srcs, lands, sems, after, scatter=True, name="scatter_grads_wait_" + names[0])
        for n, p in zip(names, parts):
            if n in TRANSPOSED:
                res = _adam(jnp.swapaxes(w_of[n], 1, 2), p, jnp.swapaxes(m_of[n], 1, 2), jnp.swapaxes(v_of[n], 1, 2),
                            tm=TM_ADAM, name="adam_" + n)
                upd[n] = [jnp.swapaxes(t, 1, 2) for t in res]
            else:
                upd[n] = _adam(w_of[n], p, m_of[n], v_of[n], tm=TM_ADAM, name="adam_" + n)
        after = upd[names[-1]][1]

    gm_cols = lax.dynamic_slice(packed_all[:, R_MODS:R_MODS + N_MOD, :].reshape(N_DEV, N_MOD * D),
                                (0, me * ada_cols), (N_DEV, ada_cols))
    upd["ada"] = _adam(w_ada, _wada_grad(c_all.T, gm_cols, name="ada_dw"), m_w_ada, v_w_ada, tm=256, name="adam_ada")
    conv_g = lax.dynamic_slice(gsmall, (R_CONV, me * conv_cols), (3, conv_cols))

    def natural(b, g1, gm, g2, gf, cw, sk):
        return dict(b_ada=b, g_ffn1=g1, g_mix=gm, g_ffn2=g2, g_final=gf[None], conv_w=cw[0], sinks=sk)

    small_out = _adam_small(gsmall, conv_g, natural(b_ada, g_ffn1, g_mix, g_ffn2, g_final, conv_w, sinks),
                            natural(m_b_ada, m_g_ffn1, m_g_mix, m_g_ffn2, m_g_final, m_conv_w, m_sinks),
                            natural(v_b_ada, v_g_ffn1, v_g_mix, v_g_ffn2, v_g_final, v_conv_w, v_sinks),
                            name="adam_small")
    for res in small_out:
        res["g_final"] = res["g_final"][0]
        res["conv_w"] = res["conv_w"][None]

    big_name = dict(w_ada="ada", w1_gu="gu1", w1_down="d1", w_in="win", w_conv_proj="cp", w_attn_proj="ap",
                    w_out="out", w2_gu="gu2", w2_down="d2")
    order = ("w_ada", "b_ada", "g_ffn1", "w1_gu", "w1_down", "g_mix", "w_in", "conv_w", "w_conv_proj", "w_attn_proj",
             "sinks", "w_out", "g_ffn2", "w2_gu", "w2_down", "g_final")
    outs = [loss, grad_x[None]]
    for kind in range(4):
        for n in order:
            outs.append(upd[big_name[n]][kind] if n in big_name else small_out[kind][n])
    return tuple(outs)
```

```python
import jax
import jax.numpy as jnp
from jax import lax
from jax.experimental import pallas as pl
from jax.experimental.pallas import tpu as pltpu

D = 1024
F = 2816
NIN = 6656
N_HEADS = 16
N_KV = 4
HEAD_DIM = 64
BLK = 128
N_MOD = 9
N_DEV = 8
EPS = 1e-6
NEG_INF = -1e30
ROPE_THETA = 10000.0
O_BG, O_CG, O_U, O_Q, O_K, O_V, O_ZC, O_ZA = 0, 1024, 2048, 3072, 4096, 4352, 4608, 5632

ADAM_LR = 0.001
ADAM_B1 = 0.9
ADAM_B2 = 0.999
ADAM_EPS = 1e-08
ADAM_WD = 0.01
ADAM_STEP = 10

BF = jnp.bfloat16
F32 = jnp.float32
VMEM_LIMIT = 56 * 1024 * 1024
MXU_N = 256
MESH = pl.DeviceIdType.MESH

NT = (((1,), (1,)), ((), ()))
TN = (((0,), (0,)), ((), ()))


def _cp(sem=None):
    return pltpu.CompilerParams(dimension_semantics=sem, vmem_limit_bytes=VMEM_LIMIT)


def _tile(n, pref):
    if n <= pref:
        return n
    for t in range(pref - pref % 16, 15, -16):
        if n % t == 0:
            return t
    raise ValueError((n, pref))


def _sigmoid(v):
    return 0.5 * jnp.tanh(0.5 * v) + 0.5


def _row(i):
    return (i, 0)


def _const2(*_):
    return (0, 0)


def _resident(shape):
    return pl.BlockSpec(shape, lambda *_: (0,) * len(shape), pipeline_mode=pl.Buffered(1))


def _norm_proj(x, g, sc, sh, wt, *, tm, tn, name):
    T, N = x.shape[0], wt.shape[0]
    tm = _tile(T, tm)

    def body(x_ref, g_ref, sc_ref, sh_ref, w_ref, h_ref, o_ref):
        xv = x_ref[...]
        r = lax.rsqrt(jnp.mean(xv * xv, axis=-1, keepdims=True) + EPS)
        hb = ((xv * r) * g_ref[...] * (1.0 + sc_ref[...]) + sh_ref[...]).astype(BF)
        h_ref[...] = hb
        for c0 in range(0, N, tn):
            cols = pl.ds(c0, tn)
            o_ref[:, cols] = lax.dot_general(hb, w_ref[cols, :], NT, preferred_element_type=F32).astype(BF)

    vec = pl.BlockSpec((1, D), _const2)
    return pl.pallas_call(
        body, name=name, grid=(T // tm,),
        in_specs=[pl.BlockSpec((tm, D), _row), vec, vec, vec, _resident((N, D))],
        out_specs=[pl.BlockSpec((tm, D), _row), pl.BlockSpec((tm, N), _row)],
        out_shape=[jax.ShapeDtypeStruct((T, D), BF), jax.ShapeDtypeStruct((T, N), BF)],
        compiler_params=_cp(("parallel",)),
    )(x, g, sc, sh, wt)


def _ffn_down_fwd(ab, wd, x, gt, *, tm, name):
    T = x.shape[0]
    tm = _tile(T, tm)

    def body(a_ref, b_ref, wd_ref, x_ref, gt_ref, xo_ref, y_ref):
        y = None
        for c0 in range(0, F, MXU_N):
            cols = pl.ds(c0, MXU_N)
            a = a_ref[:, cols].astype(F32)
            act = (a * _sigmoid(a) * b_ref[:, cols].astype(F32)).astype(BF)
            part = jnp.dot(act, wd_ref[cols, :], preferred_element_type=F32)
            y = part if y is None else y + part
        y_ref[...] = y.astype(BF)
        xo_ref[...] = x_ref[...] + (0.5 * gt_ref[...]) * y

    return pl.pallas_call(
        body, name=name, grid=(T // tm,),
        in_specs=[pl.BlockSpec((tm, F), lambda i: (i, 0)), pl.BlockSpec((tm, F), lambda i: (i, 1)),
                  _resident((F, D)), pl.BlockSpec((tm, D), _row), pl.BlockSpec((1, D), _const2)],
        out_specs=[pl.BlockSpec((tm, D), _row), pl.BlockSpec((tm, D), _row)],
        out_shape=[jax.ShapeDtypeStruct((T, D), F32), jax.ShapeDtypeStruct((T, D), BF)],
        compiler_params=_cp(("parallel",)),
    )(ab, ab, wd, x, gt)


def _ffn_fwd(x, g, sc, sh, gt, wgu, wd, final, *, tm, name):
    T = x.shape[0]
    tm = _tile(T, tm)
    last = final is not None

    def body(x_ref, g_ref, sc_ref, sh_ref, gt_ref, wgu_ref, wd_ref, *rest):
        if last:
            t_ref, gf_ref, h_ref, ab_ref, y_ref, dx_ref, ls_ref, dgf_ref = rest
        else:
            h_ref, ab_ref, y_ref, xo_ref = rest
        xv = x_ref[...]
        r = lax.rsqrt(jnp.mean(xv * xv, axis=-1, keepdims=True) + EPS)
        hb = ((xv * r) * g_ref[...] * (1.0 + sc_ref[...]) + sh_ref[...]).astype(BF)
        h_ref[...] = hb
        y = None
        for c0 in range(0, F, MXU_N):
            a = lax.dot_general(hb, wgu_ref[pl.ds(c0, MXU_N), :], NT, preferred_element_type=F32)
            b = lax.dot_general(hb, wgu_ref[pl.ds(F + c0, MXU_N), :], NT, preferred_element_type=F32)
            ab = a.astype(BF)
            bb = b.astype(BF)
            ab_ref[:, pl.ds(c0, MXU_N)] = ab
            ab_ref[:, pl.ds(F + c0, MXU_N)] = bb
            a = ab.astype(F32)
            act = (a * _sigmoid(a) * bb.astype(F32)).astype(BF)
            part = jnp.dot(act, wd_ref[pl.ds(c0, MXU_N), :], preferred_element_type=F32)
            y = part if y is None else y + part
        y_ref[...] = y.astype(BF)
        xo = xv + (0.5 * gt_ref[...]) * y
        if not last:
            xo_ref[...] = xo
            return

        @pl.when(pl.program_id(0) == 0)
        def _():
            ls_ref[...] = jnp.zeros_like(ls_ref)
            dgf_ref[...] = jnp.zeros_like(dgf_ref)
        gv = gf_ref[...]
        r = lax.rsqrt(jnp.mean(xo * xo, axis=-1, keepdims=True) + EPS)
        xh = xo * r
        e = xh * gv - t_ref[...]
        ls_ref[...] += jnp.sum(e * e, axis=0, keepdims=True)
        dy = e * (1.0 / D)
        dgf_ref[...] += jnp.sum(dy * xh, axis=0, keepdims=True)
        dxh = dy * gv
        dx_ref[...] = r * (dxh - xh * jnp.mean(dxh * xh, axis=-1, keepdims=True))

    vec = pl.BlockSpec((1, D), _const2)
    rowspec = pl.BlockSpec((tm, D), _row)
    in_specs = [rowspec, vec, vec, vec, vec, _resident((2 * F, D)), _resident((F, D))]
    out_specs = [rowspec, pl.BlockSpec((tm, 2 * F), _row), rowspec, rowspec]
    out_shape = [jax.ShapeDtypeStruct((T, D), BF), jax.ShapeDtypeStruct((T, 2 * F), BF),
                 jax.ShapeDtypeStruct((T, D), BF), jax.ShapeDtypeStruct((T, D), F32)]
    args = [x, g, sc, sh, gt, wgu, wd]
    if last:
        in_specs += [rowspec, vec]
        out_specs += [vec, vec]
        out_shape += [jax.ShapeDtypeStruct((1, D), F32)] * 2
        args += list(final)
    return pl.pallas_call(
        body, name=name, grid=(T // tm,),
        in_specs=in_specs, out_specs=out_specs, out_shape=out_shape,
        compiler_params=_cp(("arbitrary",) if last else ("parallel",)),
    )(*args)


def _ffn_down_bwd_dw(dxo, y, gt, ab, wd, *, tm, name):
    T = dxo.shape[0]
    tm = _tile(T, tm)
    nt = T // tm
    hw = F // 2
    chunks = [(c0, min(MXU_N, hw - c0)) for c0 in range(0, hw, MXU_N)]

    def body(dxo_ref, y_ref, gt_ref, a_ref, b_ref, wd_ref, dab_ref, dgt_ref, dwd_ref, dys, dyt, acc, stage, sem):
        i, j = pl.program_id(0), pl.program_id(1)

        @pl.when(jnp.logical_and(i == 0, j == 0))
        def _():
            dgt_ref[...] = jnp.zeros_like(dgt_ref)

        @pl.when(j == 0)
        def _():
            dxv = dxo_ref[...]
            dgt_ref[...] += 0.5 * jnp.sum(dxv * y_ref[...].astype(F32), axis=0, keepdims=True)
            dyf = (0.5 * gt_ref[...]) * dxv
            dys[...] = dyf.astype(BF)
            dyt[...] = dyf.T.astype(BF)

        def half(jj):
            @pl.when(i == 0)
            def _():
                acc[jj] = jnp.zeros((D, hw), F32)

            dy = dys[...]
            dy_t = dyt[...]
            for c0, cw in chunks:
                cols = pl.ds(c0, cw)
                dact = lax.dot_general(dy, wd_ref[pl.ds(jj * hw + c0, cw), :], NT, preferred_element_type=F32)
                a = a_ref[:, cols].astype(F32)
                b = b_ref[:, cols].astype(F32)
                s = _sigmoid(a)
                silu = a * s
                dab_ref[0, :, cols] = (dact * b * (s * (1.0 + a * (1.0 - s)))).astype(BF)
                dab_ref[1, :, cols] = (dact * silu).astype(BF)
                acc[jj, :, cols] += jnp.dot(dy_t, (silu * b).astype(BF), preferred_element_type=F32)

            @pl.when(i == nt - 1)
            def _():
                for c0, cw in chunks:
                    stage[0:cw, :] = acc[jj, :, pl.ds(c0, cw)].T.astype(BF)
                    out = pltpu.make_async_copy(stage.at[pl.ds(0, cw)], dwd_ref.at[pl.ds(jj * hw + c0, cw)], sem)
                    out.start()
                    out.wait()

        for jj in range(2):
            pl.when(j == jj)(lambda jj=jj: half(jj))

    vec = pl.BlockSpec((1, D), _const2)
    rowspec = pl.BlockSpec((tm, D), lambda i, j: (i, 0))
    return pl.pallas_call(
        body, name=name, grid=(nt, 2),
        in_specs=[rowspec, rowspec, vec, pl.BlockSpec((tm, hw), lambda i, j: (i, j)),
                  pl.BlockSpec((tm, hw), lambda i, j: (i, j + 2)), _resident((F, D))],
        out_specs=[pl.BlockSpec((2, tm, hw), lambda i, j: (0, i, j)), vec, pl.BlockSpec(memory_space=pl.ANY)],
        out_shape=[jax.ShapeDtypeStruct((2, T, F), BF), jax.ShapeDtypeStruct((1, D), F32),
                   jax.ShapeDtypeStruct((F, D), BF)],
        scratch_shapes=[pltpu.VMEM((tm, D), BF), pltpu.VMEM((D, tm), BF), pltpu.VMEM((2, D, hw), F32),
                        pltpu.VMEM((MXU_N, D), BF), pltpu.SemaphoreType.DMA(())],
        compiler_params=_cp(("arbitrary", "arbitrary")),
    )(dxo, y, gt, ab, ab, wd)


def _tn_matmul(a, b, token=None, *, tn, tk, name):
    S, T, Ns = a.shape
    tn, tk = _tile(Ns, tn), _tile(T, tk)
    nk, njs = T // tk, Ns // tn
    deps = [] if token is None else [token]

    def body(a_ref, b_ref, *rest):
        o_ref, acc = rest[len(deps):]
        k = pl.program_id(1)

        @pl.when(k == 0)
        def _():
            acc[...] = jnp.zeros_like(acc)
        acc[...] += lax.dot_general(a_ref[0], b_ref[...], TN, preferred_element_type=F32)

        @pl.when(k == nk - 1)
        def _():
            o_ref[...] = acc[...].astype(BF)

    return pl.pallas_call(
        body, name=name, grid=(S * njs, nk),
        in_specs=[pl.BlockSpec((1, tk, tn), lambda j, k: (j // njs, k, j % njs)),
                  pl.BlockSpec((tk, D), lambda j, k: (k, 0))] + [pl.BlockSpec(memory_space=pl.ANY)] * len(deps),
        out_specs=pl.BlockSpec((tn, D), lambda j, k: (j, 0)),
        out_shape=jax.ShapeDtypeStruct((S * Ns, D), BF),
        scratch_shapes=[pltpu.VMEM((tn, D), F32)],
        compiler_params=_cp(("parallel", "arbitrary")),
    )(a, b, *deps)


def _nn_bwd_norm(da, w, x, g, sc, dxo, *, tm, name):
    S, T, Ks = da.shape
    tm = _tile(T, tm)
    rc = _tile(tm, 256)

    def body(da_ref, w_ref, x_ref, g_ref, sc_ref, dxo_ref, dx_ref, dsh_ref, dsc_ref, dg_ref, acc):
        @pl.when(pl.program_id(0) == 0)
        def _():
            dsh_ref[...] = jnp.zeros_like(dsh_ref)
            dsc_ref[...] = jnp.zeros_like(dsc_ref)
            dg_ref[...] = jnp.zeros_like(dg_ref)

        d = jnp.dot(da_ref[0], w_ref[0:Ks, :], preferred_element_type=F32)
        for s in range(1, S):
            d = d + jnp.dot(da_ref[s], w_ref[s * Ks:(s + 1) * Ks, :], preferred_element_type=F32)
        acc[...] = d
        gv = g_ref[...]
        sc1 = 1.0 + sc_ref[...]
        dsh = jnp.zeros((1, D), F32)
        dsc = jnp.zeros((1, D), F32)
        dg = jnp.zeros((1, D), F32)
        for r0 in range(0, tm, rc):
            rows = pl.ds(r0, rc)
            u = acc[rows, :]
            xv = x_ref[rows, :]
            r = lax.rsqrt(jnp.mean(xv * xv, axis=-1, keepdims=True) + EPS)
            xh = xv * r
            dsh = dsh + jnp.sum(u, axis=0, keepdims=True)
            dsc = dsc + jnp.sum(u * (xh * gv), axis=0, keepdims=True)
            us = u * sc1
            dg = dg + jnp.sum(us * xh, axis=0, keepdims=True)
            dxh = us * gv
            dx_ref[rows, :] = dxo_ref[rows, :] + r * (dxh - xh * jnp.mean(dxh * xh, axis=-1, keepdims=True))
        dsh_ref[...] += dsh
        dsc_ref[...] += dsc
        dg_ref[...] += dg

    vec = pl.BlockSpec((1, D), _const2)
    rowspec = pl.BlockSpec((tm, D), _row)
    return pl.pallas_call(
        body, name=name, grid=(T // tm,),
        in_specs=[pl.BlockSpec((S, tm, Ks), lambda i: (0, i, 0)), _resident((S * Ks, D)), rowspec, vec, vec, rowspec],
        out_specs=[rowspec, vec, vec, vec],
        out_shape=[jax.ShapeDtypeStruct((T, D), F32)] + [jax.ShapeDtypeStruct((1, D), F32)] * 3,
        scratch_shapes=[pltpu.VMEM((tm, D), F32)],
        compiler_params=_cp(("arbitrary",)),
    )(da, w, x, g, sc, dxo)


def _rope(t, cos, sin_signed, lt32, inverse=False):
    sel = jnp.where(lt32, pltpu.roll(t, 96, 1), pltpu.roll(t, 32, 1))
    return t * cos - sel * sin_signed if inverse else t * cos + sel * sin_signed


def _rope_tables(T, token=None):
    inv = 1.0 / (ROPE_THETA ** (jnp.arange(0, HEAD_DIM, 2, dtype=F32) / HEAD_DIM))
    ang = _behind(jnp.arange(T, dtype=F32)[:, None] * inv[None, :], token)
    cos, sin = jnp.cos(ang), jnp.sin(ang)
    cos128 = jnp.tile(cos, (1, 4))
    sin128 = jnp.tile(jnp.concatenate([-sin, sin], axis=1), (1, 2))
    return cos128, sin128


QSCALE = HEAD_DIM ** -0.5


def _lane_masks(rows):
    lane = lax.broadcasted_iota(jnp.int32, (rows, 128), 1)
    return (lane % HEAD_DIM) < (HEAD_DIM // 2), [lane < HEAD_DIM, lane >= HEAD_DIM]


def _attn_bias():
    qi = lax.broadcasted_iota(jnp.int32, (4 * BLK, 2 * BLK), 0) % BLK
    kj = lax.broadcasted_iota(jnp.int32, (4 * BLK, 2 * BLK), 1)
    band = (kj > qi) & (kj <= qi + BLK)
    return jnp.stack([jnp.where(band & (kj >= BLK), 0.0, NEG_INF), jnp.where(band, 0.0, NEG_INF)]).astype(F32)


def _attn_prep(proj, cos, sin, *, name):
    T = proj.shape[0]
    tm = _tile(T, 4 * BLK)

    def body(q_ref, k_ref, c_ref, s_ref, qs_ref, kr_ref):
        lt32, halves = _lane_masks(BLK)
        for b in range(tm // BLK):
            rows = pl.ds(b * BLK, BLK)
            cc, sc = c_ref[rows, :], s_ref[rows, :]
            qr = [_rope(q_ref[rows, p * 128:(p + 1) * 128].astype(F32), cc, sc, lt32) * QSCALE for p in range(8)]
            for g in range(N_KV):
                qs_ref[g, pl.ds(4 * b * BLK, 4 * BLK), :] = _stack_heads(qr, g, halves).astype(BF)
            kr_ref[rows, :] = jnp.concatenate([_rope(k_ref[rows, r * 128:(r + 1) * 128].astype(F32), cc, sc, lt32)
                                               for r in range(2)], axis=1).astype(BF)

    tab = pl.BlockSpec((tm, 128), _row)
    return pl.pallas_call(
        body, name=name, grid=(T // tm,),
        in_specs=[pl.BlockSpec((tm, D), lambda n: (n, O_Q // D)), pl.BlockSpec((tm, 256), lambda n: (n, O_K // 256)),
                  tab, tab],
        out_specs=[pl.BlockSpec((N_KV, 4 * tm, 128), lambda n: (0, n, 0)), pl.BlockSpec((tm, 256), _row)],
        out_shape=[jax.ShapeDtypeStruct((N_KV, 4 * T, 128), BF), jax.ShapeDtypeStruct((T, 256), BF)],
        compiler_params=_cp(("parallel",)),
    )(proj, proj, cos, sin)


def _attn_specs():
    prev = lambda n: jnp.maximum(n - 1, 0)
    return [pl.BlockSpec((N_KV, 4 * BLK, 128), lambda n: (0, n, 0)),
            pl.BlockSpec((BLK, 256), _row), pl.BlockSpec((BLK, 256), lambda n: (prev(n), 0)),
            pl.BlockSpec((BLK, 256), lambda n: (n, O_V // 256)),
            pl.BlockSpec((BLK, 256), lambda n: (prev(n), O_V // 256)),
            pl.BlockSpec((1, 4 * BLK, 2 * BLK), lambda n: (jnp.minimum(n, 1), 0, 0)),
            pl.BlockSpec(memory_space=pltpu.SMEM)]


def _bands(kc_ref, kp_ref, vc_ref, vp_ref):
    kb, vb = [], []
    for r in range(2):
        cols = slice(r * 128, (r + 1) * 128)
        kb.append(jnp.concatenate([kp_ref[:, cols], kc_ref[:, cols]], axis=0))
        vb.append(jnp.concatenate([vp_ref[:, cols], vc_ref[:, cols]], axis=0))
    return kb, vb


def _sink_rows(sink_ref, g):
    return jnp.concatenate([jnp.full((BLK, 128), sink_ref[4 * g + hh], F32) for hh in range(4)], axis=0)


def _both(t):
    return jnp.concatenate([t, t], axis=1)


def _unstack_heads(t, g, halves, acc):
    half = g % 2
    for hh in range(4):
        h = 4 * g + hh
        th = jnp.where(halves[half], t[hh * BLK:(hh + 1) * BLK], 0.0)
        if h % 2 != half:
            th = pltpu.roll(th, HEAD_DIM, 1)
        acc[h // 2] = acc[h // 2] + th


def _stack_heads(chunks, g, halves):
    half = g % 2
    parts = []
    for hh in range(4):
        h = 4 * g + hh
        t = chunks[h // 2]
        if h % 2 != half:
            t = pltpu.roll(t, HEAD_DIM, 1)
        parts.append(jnp.where(halves[half], t, 0.0))
    return jnp.concatenate(parts, axis=0)


def _attn_fwd(qs, kr, proj, bias, sinks, *, name):
    T = proj.shape[0]
    nb = T // BLK

    def body(qs_ref, kc_ref, kp_ref, vc_ref, vp_ref, bias_ref, sink_ref, o_ref, lse_ref):
        _, h128 = _lane_masks(BLK)
        _, h256 = _lane_masks(2 * BLK)
        _, h512 = _lane_masks(4 * BLK)
        kb, vb = _bands(kc_ref, kp_ref, vc_ref, vp_ref)
        outs = [jnp.zeros((BLK, 128), F32) for _ in range(8)]
        groups = range(N_KV)
        bias = bias_ref[0]
        sink = [_sink_rows(sink_ref, g) for g in groups]
        s = [lax.dot_general(qs_ref[g], kb[g // 2], NT, preferred_element_type=F32) + bias for g in groups]
        m = [jnp.maximum(jnp.broadcast_to(jnp.max(s[g], axis=-1, keepdims=True), (4 * BLK, 128)), sink[g])
             for g in groups]
        p = [jnp.exp(s[g] - _both(m[g])).astype(BF) for g in groups]
        vg = [jnp.where(h256[g % 2], vb[g // 2].astype(F32), 1.0).astype(BF) for g in groups]
        o = [jnp.dot(p[g], vg[g], preferred_element_type=F32) for g in groups]
        denom = [jnp.where(h512[g % 2], pltpu.roll(o[g], HEAD_DIM, 1), o[g]) + jnp.exp(sink[g] - m[g]) for g in groups]
        for g in groups:
            lse_ref[g] = m[g] + jnp.log(denom[g])
            _unstack_heads(o[g] * (1.0 / denom[g]), g, h128, outs)
        o_ref[...] = jnp.concatenate(outs, axis=1).astype(BF)

    return pl.pallas_call(
        body, name=name, grid=(nb,),
        in_specs=_attn_specs(),
        out_specs=[pl.BlockSpec((BLK, D), _row), pl.BlockSpec((N_KV, 4 * BLK, 128), lambda n: (0, n, 0))],
        out_shape=[jax.ShapeDtypeStruct((T, D), BF), jax.ShapeDtypeStruct((N_KV, 4 * T, 128), F32)],
        compiler_params=_cp(("parallel",)),
    )(qs, kr, kr, proj, proj, bias, sinks)


def _attn_bwd(qs, kr, proj, bias, sinks, lse, o, do, cos, sin, dproj, *, name):
    T = proj.shape[0]
    nb = T // BLK

    def body(qs_ref, kc_ref, kp_ref, vc_ref, vp_ref, bias_ref, sink_ref, lse_ref, o_ref, do_ref,
             cc_ref, sc_ref, cp_ref, sp_ref, dproj_ref, dq_ref, dkc_ref, dkp_ref, dvc_ref, dvp_ref, dsink_ref):
        @pl.when(pl.program_id(0) == 0)
        def _():
            dsink_ref[...] = jnp.zeros_like(dsink_ref)
        lt32, h128 = _lane_masks(BLK)
        kb, vb = _bands(kc_ref, kp_ref, vc_ref, vp_ref)
        oc = [o_ref[:, p * 128:(p + 1) * 128].astype(F32) for p in range(8)]
        doc = [do_ref[:, p * 128:(p + 1) * 128].astype(F32) for p in range(8)]
        dqs = [jnp.zeros((BLK, 128), F32) for _ in range(8)]
        lane1 = lax.broadcasted_iota(jnp.int32, (1, 128), 1)
        dsink = jnp.zeros((1, 128), F32)
        groups = range(N_KV)
        bias = bias_ref[0]
        q = [qs_ref[g] for g in groups]
        lse_g = [lse_ref[g] for g in groups]
        s = [lax.dot_general(q[g], kb[g // 2], NT, preferred_element_type=F32) + bias for g in groups]
        dos = [_stack_heads(doc, g, h128) for g in groups]
        dosb = [t.astype(BF) for t in dos]
        dp = [lax.dot_general(dosb[g], vb[g // 2], NT, preferred_element_type=F32) for g in groups]
        delta = [jnp.broadcast_to(jnp.sum(dos[g] * _stack_heads(oc, g, h128), axis=-1, keepdims=True), (4 * BLK, 128))
                 for g in groups]
        p = [jnp.exp(s[g] - _both(lse_g[g])) for g in groups]
        ds = [(p[g] * (dp[g] - _both(delta[g]))).astype(BF) for g in groups]
        pb = [t.astype(BF) for t in p]
        dvg = [lax.dot_general(pb[g], dosb[g], TN, preferred_element_type=F32) for g in groups]
        dkg = [lax.dot_general(ds[g], q[g], TN, preferred_element_type=F32) for g in groups]
        dqg = [jnp.dot(ds[g], kb[g // 2], preferred_element_type=F32) * QSCALE for g in groups]
        dvr = [dvg[0] + dvg[1], dvg[2] + dvg[3]]
        dkr = [dkg[0] + dkg[1], dkg[2] + dkg[3]]
        for g in groups:
            _unstack_heads(dqg[g], g, h128, dqs)
            dsk = -jnp.exp(_sink_rows(sink_ref, g) - lse_g[g]) * delta[g]
            for hh in range(4):
                val = jnp.sum(dsk[hh * BLK:(hh + 1) * BLK], axis=0, keepdims=True)
                dsink = dsink + jnp.where(lane1 == 4 * g + hh, val, 0.0)
        cc, sc, cp, sp = cc_ref[...], sc_ref[...], cp_ref[...], sp_ref[...]
        dsink_ref[...] += dsink
        dq_ref[...] = jnp.concatenate([_rope(t, cc, sc, lt32, inverse=True) for t in dqs], axis=1).astype(BF)
        dkp_ref[...] = jnp.concatenate([_rope(t[:BLK], cp, sp, lt32, inverse=True) for t in dkr], axis=1)
        dkc_ref[...] = jnp.concatenate([_rope(t[BLK:], cc, sc, lt32, inverse=True) for t in dkr], axis=1)
        dvp_ref[...] = jnp.concatenate([t[:BLK] for t in dvr], axis=1)
        dvc_ref[...] = jnp.concatenate([t[BLK:] for t in dvr], axis=1)

    kv = pl.BlockSpec((BLK, 256), _row)
    tc = pl.BlockSpec((BLK, 128), _row)
    tp = pl.BlockSpec((BLK, 128), lambda n: (jnp.maximum(n - 1, 0), 0))
    return pl.pallas_call(
        body, name=name, grid=(nb,),
        in_specs=_attn_specs() + [pl.BlockSpec((N_KV, 4 * BLK, 128), lambda n: (0, n, 0)),
                                  pl.BlockSpec((BLK, D), _row), pl.BlockSpec((BLK, D), _row), tc, tc, tp, tp,
                                  pl.BlockSpec(memory_space=pl.ANY)],
        out_specs=[pl.BlockSpec((BLK, D), lambda n: (n, O_Q // D)), kv, kv, kv, kv, pl.BlockSpec((1, 128), _const2)],
        out_shape=[jax.ShapeDtypeStruct(dproj.shape, BF)] + [jax.ShapeDtypeStruct((T, 256), F32)] * 4
        + [jax.ShapeDtypeStruct((1, 128), F32)],
        input_output_aliases={14: 0},
        compiler_params=_cp(("arbitrary",)),
    )(qs, kr, kr, proj, proj, bias, sinks, lse, o, do, cos, sin, cos, sin, dproj)


def _dkv_combine(dkc, dkp, dvc, dvp, dproj, *, name):
    T = dkc.shape[0]
    nb = T // BLK
    tm = _tile(T, 4 * BLK)
    bpt = tm // BLK
    nt = T // tm

    def body(dkc_ref, dkp_ref, dkn_ref, dvc_ref, dvp_ref, dvn_ref, dproj_ref, o_ref):
        keep = jnp.where(pl.program_id(0) == nt - 1, 0.0, 1.0)

        def shifted(prev_ref, next_ref):
            nxt = keep * next_ref[...]
            return nxt if bpt == 1 else jnp.concatenate([prev_ref[BLK:, :], nxt], axis=0)

        o_ref[:, 0:256] = (dkc_ref[...] + shifted(dkp_ref, dkn_ref)).astype(BF)
        o_ref[:, 256:512] = (dvc_ref[...] + shifted(dvp_ref, dvn_ref)).astype(BF)

    cur = pl.BlockSpec((tm, 256), _row)
    nxt = pl.BlockSpec((BLK, 256), lambda i: (jnp.minimum((i + 1) * bpt, nb - 1), 0))
    return pl.pallas_call(
        body, name=name, grid=(nt,),
        in_specs=[cur, cur, nxt, cur, cur, nxt, pl.BlockSpec(memory_space=pl.ANY)],
        out_specs=pl.BlockSpec((tm, 512), lambda i: (i, O_K // 512)),
        out_shape=jax.ShapeDtypeStruct(dproj.shape, BF),
        input_output_aliases={6: 0},
        compiler_params=_cp(("parallel",)),
    )(dkc, dkp, dkp, dvc, dvp, dvp, dproj)


HALO = 16


def _conv_shifts(cu, hprev, tm):
    row = lax.broadcasted_iota(jnp.int32, (8, cu.shape[1]), 0)
    h1 = hprev[HALO - 1:HALO, :]
    h2 = hprev[HALO - 2:HALO - 1, :]
    m1 = pltpu.roll(cu, 1, 0)
    m2 = pltpu.roll(cu, 2, 0)
    m1 = jnp.concatenate([jnp.where(row == 0, h1, m1[0:8]), m1[8:]], axis=0)
    m2 = jnp.concatenate([jnp.where(row == 0, h2, jnp.where(row == 1, h1, m2[0:8])), m2[8:]], axis=0)
    return m1, m2


def _mixer_mid_fwd(proj, attn, wcp, wap, wout, convw, x, gt, *, tm, name):
    T = x.shape[0]
    tm = _tile(T, tm)
    hb = tm // HALO

    def body(bg_ref, cg_ref, u_ref, hcg_ref, hu_ref, zc0_ref, zc1_ref, za0_ref, za1_ref, at_ref,
             wcp_ref, wap_ref, wout_ref, cw_ref, x_ref, gt_ref,
             x2_ref, gc_ref, yc_ref, ya_ref, mg_ref, o_ref):
        first = jnp.where(pl.program_id(0) == 0, 0.0, 1.0)
        cu = cg_ref[...].astype(F32) * u_ref[...].astype(F32)
        hprev = first * (hcg_ref[...].astype(F32) * hu_ref[...].astype(F32))
        m1, m2 = _conv_shifts(cu, hprev, tm)
        cv = cw_ref[0:1, :] * m2 + cw_ref[1:2, :] * m1 + cw_ref[2:3, :] * cu
        gc = (bg_ref[...].astype(F32) * cv).astype(BF)
        gc_ref[...] = gc
        yc = jnp.dot(gc, wcp_ref[...], preferred_element_type=F32)
        ya = jnp.dot(at_ref[...], wap_ref[...], preferred_element_type=F32)
        yc_ref[...] = yc.astype(BF)
        ya_ref[...] = ya.astype(BF)
        zc = jnp.concatenate([zc0_ref[...], zc1_ref[...]], axis=1).astype(F32)
        za = jnp.concatenate([za0_ref[...], za1_ref[...]], axis=1).astype(F32)
        mg = (_sigmoid(zc) * yc + _sigmoid(za) * ya).astype(BF)
        mg_ref[...] = mg
        o = jnp.dot(mg, wout_ref[...], preferred_element_type=F32)
        o_ref[...] = o.astype(BF)
        x2_ref[...] = x_ref[...] + gt_ref[...] * o

    wspec = pl.BlockSpec((D, D), _const2)
    rowspec = pl.BlockSpec((tm, D), _row)
    return pl.pallas_call(
        body, name=name, grid=(T // tm,),
        in_specs=[_col(tm, O_BG), _col(tm, O_CG), _col(tm, O_U), _halo_prev(hb, O_CG), _halo_prev(hb, O_U),
                  _col(tm, O_ZC, 512), _col(tm, O_ZC + 512, 512), _col(tm, O_ZA, 512), _col(tm, O_ZA + 512, 512),
                  rowspec, wspec, wspec, wspec, pl.BlockSpec((8, D), _const2), rowspec, pl.BlockSpec((1, D), _const2)],
        out_specs=[rowspec] * 6,
        out_shape=[jax.ShapeDtypeStruct((T, D), F32)] + [jax.ShapeDtypeStruct((T, D), BF)] * 5,
        compiler_params=_cp(("parallel",)),
    )(proj, proj, proj, proj, proj, proj, proj, proj, proj, attn, wcp, wap, wout, convw, x, gt)


def _col(tm, c, w=D):
    assert c % w == 0
    return pl.BlockSpec((tm, w), lambda i: (i, c // w))


def _halo_prev(hb, c):
    return pl.BlockSpec((HALO, D), lambda i: (jnp.maximum(i * hb - 1, 0), c // D))


def _halo_next(hb, nblk, c=0):
    return pl.BlockSpec((HALO, D), lambda i: (jnp.minimum((i + 1) * hb, nblk - 1), c // D))


def _mixer_mid_bwd(dx2, gt, o, proj, yc, ya, wout, wcp, wap, *, tm, name):
    T = dx2.shape[0]
    tm = _tile(T, tm)
    nt = T // tm

    def body(dx_ref, gt_ref, o_ref, zc0_ref, zc1_ref, za0_ref, za1_ref, yc_ref, ya_ref, wout_ref, wcp_ref, wap_ref,
             dout_ref, dyc_ref, dya_ref, dgc_ref, dat_ref, dproj_ref, dgt_ref, dzs, sems):
        i = pl.program_id(0)
        slot = lax.rem(i, 2)

        def slab_copy(step, s):
            return pltpu.make_async_copy(
                dzs.at[s], dproj_ref.at[pl.ds(pl.multiple_of(step * tm, tm), tm), pl.ds(O_ZC, 2 * D)], sems.at[s])

        @pl.when(i == 0)
        def _():
            dgt_ref[...] = jnp.zeros_like(dgt_ref)

        dxv = dx_ref[...]
        dgt_ref[...] += jnp.sum(dxv * o_ref[...].astype(F32), axis=0, keepdims=True)
        dout = (gt_ref[...] * dxv).astype(BF)
        dout_ref[...] = dout
        dmg = lax.dot_general(dout, wout_ref[...], NT, preferred_element_type=F32)
        sc = _sigmoid(jnp.concatenate([zc0_ref[...], zc1_ref[...]], axis=1).astype(F32))
        sa = _sigmoid(jnp.concatenate([za0_ref[...], za1_ref[...]], axis=1).astype(F32))
        dyc = (dmg * sc).astype(BF)
        dya = (dmg * sa).astype(BF)
        dyc_ref[...] = dyc
        dya_ref[...] = dya
        dzs[slot, :, 0:D] = (dmg * yc_ref[...].astype(F32) * (sc * (1.0 - sc))).astype(BF)
        dzs[slot, :, D:2 * D] = (dmg * ya_ref[...].astype(F32) * (sa * (1.0 - sa))).astype(BF)
        slab_copy(i, slot).start()
        dgc_ref[...] = lax.dot_general(dyc, wcp_ref[...], NT, preferred_element_type=F32).astype(BF)
        dat_ref[...] = lax.dot_general(dya, wap_ref[...], NT, preferred_element_type=F32).astype(BF)

        @pl.when(i > 0)
        def _():
            slab_copy(i - 1, 1 - slot).wait()

        @pl.when(i == nt - 1)
        def _():
            slab_copy(i, slot).wait()

    def zcol(c):
        return pl.BlockSpec((tm, 512), lambda i: (i, c // 512))

    wspec = pl.BlockSpec((D, D), _const2)
    rowspec = pl.BlockSpec((tm, D), _row)
    vec = pl.BlockSpec((1, D), _const2)
    return pl.pallas_call(
        body, name=name, grid=(nt,),
        in_specs=[rowspec, vec, rowspec, zcol(O_ZC), zcol(O_ZC + 512), zcol(O_ZA), zcol(O_ZA + 512),
                  rowspec, rowspec, wspec, wspec, wspec],
        out_specs=[rowspec] * 5 + [pl.BlockSpec(memory_space=pl.ANY), vec],
        out_shape=[jax.ShapeDtypeStruct((T, D), BF)] * 5 + [jax.ShapeDtypeStruct((T, NIN), BF),
                                                            jax.ShapeDtypeStruct((1, D), F32)],
        scratch_shapes=[pltpu.VMEM((2, tm, 2 * D), BF), pltpu.SemaphoreType.DMA((2,))],
        compiler_params=_cp(("arbitrary",)),
    )(dx2, gt, o, proj, proj, proj, proj, yc, ya, wout, wcp, wap)


def _conv_bwd(dgc, proj, convw, dproj, *, tm, name):
    T = dgc.shape[0]
    tm = _tile(T, tm)
    hb = tm // HALO
    nblk = T // HALO
    nt = T // tm

    def body(dgc_ref, ndgc_ref, bg_ref, nbg_ref, cg_ref, u_ref, hcg_ref, hu_ref, cw_ref, dproj_ref, dp_ref, dcw_ref):
        i = pl.program_id(0)

        @pl.when(i == 0)
        def _():
            dcw_ref[...] = jnp.zeros_like(dcw_ref)
        first = jnp.where(i == 0, 0.0, 1.0)
        last = jnp.where(i == nt - 1, 0.0, 1.0)
        cg = cg_ref[...].astype(F32)
        u = u_ref[...].astype(F32)
        bg = bg_ref[...].astype(F32)
        dg = dgc_ref[...].astype(F32)
        cu = cg * u
        hprev = first * (hcg_ref[...].astype(F32) * hu_ref[...].astype(F32))
        m1, m2 = _conv_shifts(cu, hprev, tm)
        w0, w1, w2 = cw_ref[0:1, :], cw_ref[1:2, :], cw_ref[2:3, :]
        cv = w0 * m2 + w1 * m1 + w2 * cu
        dcv = dg * bg
        nxt = last * (ndgc_ref[...].astype(F32) * nbg_ref[...].astype(F32))
        n0, n1 = nxt[0:1, :], nxt[1:2, :]
        row = lax.broadcasted_iota(jnp.int32, (8, D), 0)
        p1 = pltpu.roll(dcv, tm - 1, 0)
        p2 = pltpu.roll(dcv, tm - 2, 0)
        p1 = jnp.concatenate([p1[:tm - 8], jnp.where(row == 7, n0, p1[tm - 8:])], axis=0)
        p2 = jnp.concatenate([p2[:tm - 8], jnp.where(row == 7, n1, jnp.where(row == 6, n0, p2[tm - 8:]))], axis=0)
        dcu = w2 * dcv + w1 * p1 + w0 * p2
        dp_ref[:, 0:D] = (dg * cv).astype(BF)
        dp_ref[:, D:2 * D] = (dcu * u).astype(BF)
        dp_ref[:, 2 * D:3 * D] = (dcu * cg).astype(BF)
        dcw_ref[0:1, :] += jnp.sum(dcv * m2, axis=0, keepdims=True)
        dcw_ref[1:2, :] += jnp.sum(dcv * m1, axis=0, keepdims=True)
        dcw_ref[2:3, :] += jnp.sum(dcv * cu, axis=0, keepdims=True)

    rowspec = pl.BlockSpec((tm, D), _row)
    cw = pl.BlockSpec((8, D), _const2)
    return pl.pallas_call(
        body, name=name, grid=(nt,),
        in_specs=[rowspec, _halo_next(hb, nblk), _col(tm, O_BG), _halo_next(hb, nblk, O_BG),
                  _col(tm, O_CG), _col(tm, O_U), _halo_prev(hb, O_CG), _halo_prev(hb, O_U), cw,
                  pl.BlockSpec(memory_space=pl.ANY)],
        out_specs=[pl.BlockSpec((tm, 3 * D), _row), cw],
        out_shape=[jax.ShapeDtypeStruct(dproj.shape, BF), jax.ShapeDtypeStruct((8, D), F32)],
        input_output_aliases={9: 0},
        compiler_params=_cp(("arbitrary",)),
    )(dgc, dgc, proj, proj, proj, proj, proj, proj, convw, dproj)


def _adam_math(w, g, m, v):
    nm = ADAM_B1 * m + (1.0 - ADAM_B1) * g
    nv = ADAM_B2 * v + (1.0 - ADAM_B2) * (g * g)
    m_hat = nm / (1.0 - ADAM_B1 ** ADAM_STEP)
    v_hat = nv / (1.0 - ADAM_B2 ** ADAM_STEP)
    return -ADAM_LR * (m_hat / (jnp.sqrt(v_hat) + ADAM_EPS) + ADAM_WD * w), nm, nv


SMALL = ("b_ada", "g_ffn1", "g_mix", "g_ffn2", "g_final", "conv_w", "sinks")


def _adam_small(gsum, conv_g, w, m, v, *, name):
    nsm = len(SMALL)

    def body(*refs):
        gs_ref, cg_ref = refs[0], refs[1]
        w_refs, m_refs, v_refs = (refs[2 + k * nsm:2 + (k + 1) * nsm] for k in range(3))
        outs = refs[2 + 3 * nsm:]
        for p, n in enumerate(SMALL):
            if n == "b_ada":
                pieces = [(slice(None), slice(r * D, (r + 1) * D), gs_ref[R_MODS + r:R_MODS + r + 1, :])
                          for r in range(N_MOD)]
            elif n == "conv_w":
                pieces = [(slice(None), slice(None), cg_ref[...])]
            elif n == "sinks":
                pieces = [(slice(None), slice(None), gs_ref[R_SINK:R_SINK + 1, 0:N_HEADS])]
            else:
                row = dict(g_ffn1=R_G1, g_mix=R_GM, g_ffn2=R_G2, g_final=R_GF)[n]
                pieces = [(slice(None), slice(None), gs_ref[row:row + 1, :])]
            for rs, cs, g in pieces:
                d, nm, nv = _adam_math(w_refs[p][rs, cs], g, m_refs[p][rs, cs], v_refs[p][rs, cs])
                for k, val in enumerate((g, d, nm, nv)):
                    outs[k * nsm + p][rs, cs] = val

    args = [gsum, conv_g] + [d[n] for d in (w, m, v) for n in SMALL]
    shapes = [jax.ShapeDtypeStruct(w[n].shape, F32) for _ in range(4) for n in SMALL]
    res = pl.pallas_call(body, name=name, out_shape=shapes, compiler_params=_cp())(*args)
    return [dict(zip(SMALL, res[k * nsm:(k + 1) * nsm])) for k in range(4)]


def _adam(w, g, m, v, *, tm, name):
    _, R, C = w.shape
    tm = _tile(R, tm)
    parts = g.ndim == 3

    def body(w_ref, g_ref, m_ref, v_ref, go_ref, d_ref, nm_ref, nv_ref):
        if parts:
            gv = g_ref[0].astype(F32)
            for s in range(1, N_DEV):
                gv = gv + g_ref[s].astype(F32)
        else:
            gv = g_ref[...]
        go_ref[0] = gv
        d_ref[0], nm_ref[0], nv_ref[0] = _adam_math(w_ref[0], gv, m_ref[0], v_ref[0])

    spec = pl.BlockSpec((1, tm, C), lambda i: (0, i, 0))
    gspec = pl.BlockSpec((N_DEV, tm, C), lambda i: (0, i, 0)) if parts else pl.BlockSpec((tm, C), _row)
    return pl.pallas_call(
        body, name=name, grid=(R // tm,),
        in_specs=[spec, gspec, spec, spec], out_specs=[spec] * 4,
        out_shape=[jax.ShapeDtypeStruct((1, R, C), F32)] * 4,
        compiler_params=_cp(("parallel",)),
    )(w, g, m, v)


def _mods_part(c_all, w_ada, b_ada, *, name):
    C = w_ada.shape[1]

    def body(c_ref, w_ref, b_ref, o_ref):
        cv = c_ref[...]
        ca = cv * jax.nn.sigmoid(cv)
        o_ref[...] = jnp.dot(ca, w_ref[...], preferred_element_type=F32,
                             precision=lax.Precision.HIGHEST) + b_ref[...]

    return pl.pallas_call(
        body, name=name,
        out_shape=jax.ShapeDtypeStruct((N_DEV, C), F32),
        compiler_params=_cp(),
    )(c_all, w_ada, b_ada)


def _wada_grad(c_all_t, gm, *, name):
    C = gm.shape[1]

    def body(c_ref, g_ref, o_ref):
        cv = c_ref[...]
        ca = cv * jax.nn.sigmoid(cv)
        acc = ca[:, 0:1] * g_ref[0:1, :]
        for b in range(1, N_DEV):
            acc = acc + ca[:, b:b + 1] * g_ref[b:b + 1, :]
        o_ref[...] = acc

    return pl.pallas_call(
        body, name=name,
        out_shape=jax.ShapeDtypeStruct((D, C), F32),
        compiler_params=_cp(),
    )(c_all_t, gm)


def _peer(x, y, c, d):
    px = lax.rem(x + ((d >> 2) & 1), 2)
    py = lax.rem(y + ((d >> 1) & 1), 2)
    pc = lax.rem(c + (d & 1), 2)
    return (px, py, pc), 4 * px + 2 * py + pc


def _exchange(xs, *, scatter, name):
    n = len(xs)
    nsem = n * (N_DEV - 1)

    def body(*refs):
        ins, outs = refs[:n], refs[n:2 * n]
        token, send_sems, recv_sems, local_sems = refs[2 * n:]
        x, y, c = lax.axis_index("x"), lax.axis_index("y"), lax.axis_index("c")
        me = 4 * x + 2 * y + c
        token[...] = jnp.zeros_like(token)

        def src(t, idx):
            return ins[t].at[idx] if scatter else ins[t]

        local = [pltpu.make_async_copy(src(t, me), outs[t].at[me], local_sems.at[t]) for t in range(n)]
        for cp in local:
            cp.start()
        remote = []
        for t in range(n):
            for d in range(1, N_DEV):
                peer, pidx = _peer(x, y, c, d)
                k = t * (N_DEV - 1) + d - 1
                send = pltpu.make_async_remote_copy(src_ref=src(t, pidx), dst_ref=outs[t].at[me],
                                                    send_sem=send_sems.at[k], recv_sem=recv_sems.at[k],
                                                    device_id=peer, device_id_type=MESH)
                recv = pltpu.make_async_remote_copy(src_ref=src(t, pidx), dst_ref=outs[t].at[pidx],
                                                    send_sem=send_sems.at[k], recv_sem=recv_sems.at[k],
                                                    device_id=peer, device_id_type=MESH)
                send.start()
                remote.append((send, recv))
        for cp in local:
            cp.wait()
        for send, recv in remote:
            send.wait_send()
            recv.wait_recv()

    anyspec = pl.BlockSpec(memory_space=pl.ANY)
    out_shape = [jax.ShapeDtypeStruct(a.shape if scatter else (N_DEV,) + a.shape, a.dtype) for a in xs]
    out_shape.append(jax.ShapeDtypeStruct((8, 128), F32))
    return pl.pallas_call(
        body, name=name,
        in_specs=[anyspec] * n, out_specs=[anyspec] * n + [pl.BlockSpec(memory_space=pltpu.VMEM)],
        out_shape=out_shape,
        scratch_shapes=[pltpu.SemaphoreType.DMA((nsem,)), pltpu.SemaphoreType.DMA((nsem,)),
                        pltpu.SemaphoreType.DMA((n,))],
    )(*xs)


def _sum8(parts, *, name):
    _, R, C = parts.shape

    def body(p_ref, o_ref):
        acc = p_ref[0]
        for s in range(1, N_DEV):
            acc = acc + p_ref[s]
        o_ref[...] = acc

    return pl.pallas_call(body, name=name, out_shape=jax.ShapeDtypeStruct((R, C), F32),
                          compiler_params=_cp())(parts)


HBM_SPEC = pl.BlockSpec(memory_space=pltpu.HBM)
SEM_SPEC = pl.BlockSpec(memory_space=pltpu.SEMAPHORE)
N_PEER = N_DEV - 1


def _split_copies(src_refs, land_refs, send_sems, recv_sems, scatter):
    x, y, c = lax.axis_index("x"), lax.axis_index("y"), lax.axis_index("c")
    me = 4 * x + 2 * y + c
    pairs = []
    for j, (src, land) in enumerate(zip(src_refs, land_refs)):
        for d in range(1, N_DEV):
            peer, pidx = _peer(x, y, c, d)
            k = j * N_PEER + d - 1
            s = src.at[pidx] if scatter else src
            send = pltpu.make_async_remote_copy(src_ref=s, dst_ref=land.at[me], send_sem=send_sems.at[k],
                                                recv_sem=recv_sems.at[k], device_id=peer, device_id_type=MESH)
            recv = pltpu.make_async_remote_copy(src_ref=s, dst_ref=land.at[pidx], send_sem=send_sems.at[k],
                                                recv_sem=recv_sems.at[k], device_id=peer, device_id_type=MESH)
            pairs.append((send, recv))
    return pairs


def _own_slot(block, me):
    land = lax.empty((N_DEV,) + block.shape, block.dtype)
    return lax.dynamic_update_slice(land, block[None], (me, 0, 0))


def _split_start(srcs, lands, groups, *, scatter, name):
    n, ng = len(srcs), len(groups)

    def body(*refs):
        src_refs, land_refs = refs[:n], refs[n:2 * n]
        sems = refs[2 * n:2 * n + 2 * ng]
        token = refs[-1]
        for gi, g in enumerate(groups):
            pairs = _split_copies([src_refs[t] for t in g], [land_refs[t] for t in g], sems[2 * gi],
                                  sems[2 * gi + 1], scatter)
            for send, _ in pairs:
                send.start()
        token[...] = jnp.zeros_like(token)

    sem_shapes = []
    for g in groups:
        sem_shapes += [pltpu.SemaphoreType.DMA((len(g) * N_PEER,))] * 2
    thru = [pltpu.HBM(a.shape, a.dtype) for a in list(srcs) + list(lands)]
    outs = pl.pallas_call(
        body, name=name,
        out_shape=tuple(sem_shapes + thru + [jax.ShapeDtypeStruct((8, 128), F32)]),
        in_specs=[HBM_SPEC] * (2 * n),
        out_specs=tuple([SEM_SPEC] * (2 * ng) + [HBM_SPEC] * (2 * n) + [pl.BlockSpec(memory_space=pltpu.VMEM)]),
        input_output_aliases={i: 2 * ng + i for i in range(2 * n)},
        compiler_params=pltpu.CompilerParams(has_side_effects=pltpu.SideEffectType.DATAFLOW_SIDE_EFFECTING),
    )(*[pltpu.with_memory_space_constraint(a, pltpu.HBM) for a in list(srcs) + list(lands)])
    sems = [(outs[2 * gi], outs[2 * gi + 1]) for gi in range(ng)]
    return sems, outs[2 * ng:2 * ng + n], outs[2 * ng + n:2 * ng + 2 * n], outs[-1]


def _behind(v, token):
    if token is None:
        return v
    return v + token[0, 0].astype(v.dtype)


def _split_wait(srcs, lands, sems, after, *, scatter, name):
    m = len(srcs)

    def body(*refs):
        src_refs, land_refs = refs[:m], refs[m:2 * m]
        send_sems, recv_sems = refs[2 * m], refs[2 * m + 1]
        for send, recv in _split_copies(src_refs, land_refs, send_sems, recv_sems, scatter):
            send.wait_send()
            recv.wait_recv()

    outs = pl.pallas_call(
        body, name=name,
        out_shape=tuple(pltpu.HBM(a.shape, a.dtype) for a in list(srcs) + list(lands)),
        in_specs=[HBM_SPEC] * (2 * m) + [SEM_SPEC, SEM_SPEC, pl.BlockSpec(memory_space=pl.ANY)],
        out_specs=tuple([HBM_SPEC] * (2 * m)),
        input_output_aliases={i: i for i in range(2 * m)},
        compiler_params=pltpu.CompilerParams(has_side_effects=pltpu.SideEffectType.DATAFLOW_SIDE_EFFECTING),
    )(*srcs, *lands, sems[0], sems[1], after)
    return outs[m:]


TL_FIRST = (1, 2, 4, 6)
TL_ICI = (2, 4, 6)
EFFECT = pltpu.SideEffectType.DATAFLOW_SIDE_EFFECTING


def _tl_first(src_refs, land_refs, send_sems, recv_sems):
    x, y, c = lax.axis_index("x"), lax.axis_index("y"), lax.axis_index("c")
    me = 4 * x + 2 * y + c
    out = []
    for j, (src, land) in enumerate(zip(src_refs, land_refs)):
        for i, d in enumerate(TL_FIRST):
            peer, pidx = _peer(x, y, c, d)
            k = len(TL_FIRST) * j + i
            send = pltpu.make_async_remote_copy(src_ref=src, dst_ref=land.at[me], send_sem=send_sems.at[k],
                                                recv_sem=recv_sems.at[k], device_id=peer, device_id_type=MESH)
            recv = pltpu.make_async_remote_copy(src_ref=src, dst_ref=land.at[pidx], send_sem=send_sems.at[k],
                                                recv_sem=recv_sems.at[k], device_id=peer, device_id_type=MESH)
            out.append((d, send, recv))
    return out


def _tl_second(land_refs, send_sems, recv_sems):
    x, y, c = lax.axis_index("x"), lax.axis_index("y"), lax.axis_index("c")
    sibling, _ = _peer(x, y, c, 1)
    out = []
    for j, land in enumerate(land_refs):
        for i, d in enumerate(TL_ICI):
            _, mine = _peer(x, y, c, d)
            _, theirs = _peer(x, y, c, d + 1)
            k = len(TL_ICI) * j + i
            send = pltpu.make_async_remote_copy(src_ref=land.at[mine], dst_ref=land.at[mine], send_sem=send_sems.at[k],
                                                recv_sem=recv_sems.at[k], device_id=sibling, device_id_type=MESH)
            recv = pltpu.make_async_remote_copy(src_ref=land.at[mine], dst_ref=land.at[theirs],
                                                send_sem=send_sems.at[k], recv_sem=recv_sems.at[k],
                                                device_id=sibling, device_id_type=MESH)
            out.append((send, recv))
    return out


def _tl_start(srcs, lands, groups, *, name):
    n, ng = len(srcs), len(groups)

    def body(*refs):
        src_refs, land_refs = refs[:n], refs[n:2 * n]
        sems = refs[2 * n:2 * n + 2 * ng]
        for gi, g in enumerate(groups):
            for _, send, _ in _tl_first([src_refs[t] for t in g], [land_refs[t] for t in g], sems[2 * gi],
                                        sems[2 * gi + 1]):
                send.start()
        refs[-1][...] = jnp.zeros_like(refs[-1])

    sem_shapes = []
    for g in groups:
        sem_shapes += [pltpu.SemaphoreType.DMA((len(g) * len(TL_FIRST),))] * 2
    thru = [pltpu.HBM(a.shape, a.dtype) for a in list(srcs) + list(lands)]
    outs = pl.pallas_call(
        body, name=name,
        out_shape=tuple(sem_shapes + thru + [jax.ShapeDtypeStruct((8, 128), F32)]),
        in_specs=[HBM_SPEC] * (2 * n),
        out_specs=tuple([SEM_SPEC] * (2 * ng) + [HBM_SPEC] * (2 * n) + [pl.BlockSpec(memory_space=pltpu.VMEM)]),
        input_output_aliases={i: 2 * ng + i for i in range(2 * n)},
        compiler_params=pltpu.CompilerParams(has_side_effects=EFFECT),
    )(*[pltpu.with_memory_space_constraint(a, pltpu.HBM) for a in list(srcs) + list(lands)])
    sems = [(outs[2 * gi], outs[2 * gi + 1]) for gi in range(ng)]
    return sems, outs[2 * ng:2 * ng + n], outs[2 * ng + n:2 * ng + 2 * n], outs[-1]


def _tl_forward(srcs, lands, sems1, after, *, name):
    m = len(srcs)

    def body(*refs):
        src_refs, land_refs = refs[:m], refs[m:2 * m]
        send1, recv1 = refs[2 * m], refs[2 * m + 1]
        send2, recv2 = refs[2 * m + 3], refs[2 * m + 4]
        for d, _, recv in _tl_first(src_refs, land_refs, send1, recv1):
            if d in TL_ICI:
                recv.wait_recv()
        for send, _ in _tl_second(land_refs, send2, recv2):
            send.start()

    sem = pltpu.SemaphoreType.DMA((m * len(TL_ICI),))
    outs = pl.pallas_call(
        body, name=name,
        out_shape=tuple([sem, sem] + [pltpu.HBM(a.shape, a.dtype) for a in list(srcs) + list(lands)]),
        in_specs=[HBM_SPEC] * (2 * m) + [SEM_SPEC, SEM_SPEC, pl.BlockSpec(memory_space=pl.ANY)],
        out_specs=tuple([SEM_SPEC, SEM_SPEC] + [HBM_SPEC] * (2 * m)),
        input_output_aliases={i: 2 + i for i in range(2 * m)},
        compiler_params=pltpu.CompilerParams(has_side_effects=EFFECT),
    )(*srcs, *lands, sems1[0], sems1[1], after)
    return (outs[0], outs[1]), outs[2:2 + m], outs[2 + m:2 + 2 * m]


def _tl_wait(srcs, lands, sems1, sems2, after, *, name):
    m = len(srcs)

    def body(*refs):
        src_refs, land_refs = refs[:m], refs[m:2 * m]
        send1, recv1, send2, recv2 = refs[2 * m:2 * m + 4]
        for d, send, recv in _tl_first(src_refs, land_refs, send1, recv1):
            send.wait_send()
            if d not in TL_ICI:
                recv.wait_recv()
        for send, recv in _tl_second(land_refs, send2, recv2):
            send.wait_send()
            recv.wait_recv()

    outs = pl.pallas_call(
        body, name=name,
        out_shape=tuple(pltpu.HBM(a.shape, a.dtype) for a in list(srcs) + list(lands)),
        in_specs=[HBM_SPEC] * (2 * m) + [SEM_SPEC] * 4 + [pl.BlockSpec(memory_space=pl.ANY)],
        out_specs=tuple([HBM_SPEC] * (2 * m)),
        input_output_aliases={i: i for i in range(2 * m)},
        compiler_params=pltpu.CompilerParams(has_side_effects=EFFECT),
    )(*srcs, *lands, sems1[0], sems1[1], sems2[0], sems2[1], after)
    return outs[m:]


TM_PROJ = 512
TN_PROJ = 512
TM_ROW = 512
TM_NN = 512
TK_TN = 2048
TM_ADAM = 416
TN_FFN = F // 2
TN_IN = NIN // 4


def _tn(a, b, name, tn, token=None):
    if a.ndim == 2:
        a = a[None]
    return _tn_matmul(a, b, token, tn=tn, tk=TK_TN, name=name)


def _local_step(x, tgt, mods, g1, gm, g2, gf, convw8, sinks, w_get, g_put, tables=None):
    T = x.shape[0]
    sh1, sc1, gt1, sh2, sc2, gt2, sh3, sc3, gt3 = [mods[i:i + 1] for i in range(N_MOD)]
    cos, sin = _rope_tables(T) if tables is None else tables
    behind = _behind

    w = dict(w_get("gu1", mods))
    h1, ab1 = _norm_proj(x, g1, sc1, sh1, w["gu1"], tm=TM_PROJ, tn=TN_PROJ, name="ffn1_up")
    w.update(w_get("d1", ab1))
    x1, y1 = _ffn_down_fwd(ab1, w["d1"], x, gt1, tm=TM_ROW, name="ffn1_down")
    w.update(w_get("mix", x1))
    h2, proj = _norm_proj(x1, gm, sc2, sh2, w["win"], tm=TM_PROJ, tn=TN_PROJ, name="mix_in")
    qs, kr = _attn_prep(proj, cos, sin, name="attn_prep")
    bias = _attn_bias()
    attn, lse = _attn_fwd(qs, kr, proj, bias, sinks, name="attn_fwd")
    x2, gc, yc, ya, mg, o = _mixer_mid_fwd(proj, attn, w["cp"], w["ap"], w["out"], convw8, x1, gt2,
                                           tm=TM_ROW, name="mix_mid")
    w.update(w_get("ffn2", x2))
    h3, ab2, y2, dx3, lsum, dgf = _ffn_fwd(x2, g2, sc3, sh3, gt3, w["gu2"], w["d2"], (tgt, gf), tm=TM_ROW,
                                           name="ffn2_final")

    dab2, dgt3, g_d2 = _ffn_down_bwd_dw(dx3, y2, gt3, ab2, w["d2"], tm=TM_ROW, name="ffn2_down_bwd")
    dx2, dsh3, dsc3, dg2 = _nn_bwd_norm(dab2, w["gu2"], x2, g2, sc3, dx3, tm=TM_NN, name="ffn2_up_bwd")
    g_gu2 = _tn(dab2, h3, "ffn2_up_dw", TN_FFN)
    tok = g_put(dict(gu2=g_gu2, d2=g_d2))

    dout, dyc, dya, dgc, dat, dproj, dgt2 = _mixer_mid_bwd(dx2, behind(gt2, tok), o, proj, yc, ya, w["out"], w["cp"],
                                                           w["ap"], tm=TM_ROW, name="mix_mid_bwd")
    g_out = _tn(mg, dout, "mix_out_dw", D)
    g_cp = _tn(gc, dyc, "mix_cp_dw", D)
    g_ap = _tn(attn, dya, "mix_ap_dw", D)
    dproj, dkc, dkp, dvc, dvp, dsink = _attn_bwd(qs, kr, proj, bias, sinks, lse, attn, dat, cos, sin, dproj,
                                                 name="attn_bwd")
    dproj = _dkv_combine(dkc, dkp, dvc, dvp, dproj, name="attn_dkv")
    dproj, dcw = _conv_bwd(dgc, proj, convw8, dproj, tm=TM_ROW, name="conv_bwd")
    g_in = _tn(dproj, h2, "mix_in_dw", TN_IN)
    tok = g_put(dict(win=g_in, cp=g_cp, ap=g_ap, out=g_out))
    dx1, dsh2, dsc2, dgm = _nn_bwd_norm(dproj[None], w["win"], x1, gm, behind(sc2, tok), dx2, tm=TM_NN,
                                        name="mix_in_bwd")

    dab1, dgt1, g_d1 = _ffn_down_bwd_dw(dx1, y1, gt1, ab1, w["d1"], tm=TM_ROW, name="ffn1_down_bwd")
    tok = g_put(dict(d1=g_d1))
    g_gu1 = _tn(dab1, h1, "ffn1_up_dw", TN_FFN, tok)
    tok = g_put(dict(gu1=g_gu1))
    dx0, dsh1, dsc1, dg1 = _nn_bwd_norm(dab1, w["gu1"], x, g1, behind(sc1, tok), dx1, tm=TM_NN,
                                        name="ffn1_up_bwd")

    small = dict(mods=jnp.concatenate([dsh1, dsc1, dgt1, dsh2, dsc2, dgt2, dsh3, dsc3, dgt3], axis=0),
                 g1=dg1, gm=dgm, g2=dg2, gf=dgf, convw=dcw[0:3], sinks=dsink[:, 0:N_HEADS])
    return lsum, dx0, small


BIG = ("gu1", "d1", "win", "cp", "ap", "out", "gu2", "d2")
TRANSPOSED = ("gu1", "win", "gu2")
SMALL_ROWS = 24
R_MODS, R_G1, R_GM, R_G2, R_GF, R_CONV, R_SINK = 0, 9, 10, 11, 12, 13, 16


def _pad_to(a, rows, cols):
    return jnp.pad(a, ((0, rows - a.shape[0]), (0, cols - a.shape[1])))


def _pack_small(b_ada, g1, gm, g2, gf, conv, sinks):
    rows = [b_ada.reshape(N_MOD, D), g1.reshape(1, D), gm.reshape(1, D), g2.reshape(1, D), gf.reshape(1, D),
            _pad_to(conv.reshape(3, -1), 3, D), _pad_to(sinks.reshape(1, N_HEADS), 1, D)]
    return _pad_to(jnp.concatenate(rows, axis=0), SMALL_ROWS, D)


def kernel(x, c, w_ada, b_ada, g_ffn1, w1_gu, w1_down, g_mix, w_in, conv_w, w_conv_proj, w_attn_proj, sinks, w_out, g_ffn2, w2_gu, w2_down, g_final, loss_target, m_w_ada, m_b_ada, m_g_ffn1, m_w1_gu, m_w1_down, m_g_mix, m_w_in, m_conv_w, m_w_conv_proj, m_w_attn_proj, m_sinks, m_w_out, m_g_ffn2, m_w2_gu, m_w2_down, m_g_final, v_w_ada, v_b_ada, v_g_ffn1, v_w1_gu, v_w1_down, v_g_mix, v_w_in, v_conv_w, v_w_conv_proj, v_w_attn_proj, v_sinks, v_w_out, v_g_ffn2, v_w2_gu, v_w2_down, v_g_final):
    me = 4 * lax.axis_index("x") + 2 * lax.axis_index("y") + lax.axis_index("c")
    ada_cols = w_ada.shape[2]
    conv_cols = conv_w.shape[2]

    native = dict(gu1=w1_gu[0], d1=w1_down[0], win=w_in[0], cp=w_conv_proj[0], ap=w_attn_proj[0], out=w_out[0],
                  gu2=w2_gu[0], d2=w2_down[0])

    def shard(n, token):
        a = _behind(native[n], token)
        return (a.T if n in TRANSPOSED else a).astype(BF)

    c_all, conv_all, _ = _exchange([c, _pad_to(conv_w[0], 8, conv_cols)], scatter=False, name="gather_cond")
    c_all = c_all.reshape(N_DEV, D)
    conv_full = conv_all[:, 0:3, :].transpose(1, 0, 2).reshape(3, D)

    b_cols = lax.dynamic_slice(b_ada, (0, me * ada_cols), (1, ada_cols))
    mods_cols = _mods_part(c_all, w_ada[0], b_cols, name="ada_mods")
    mods_all, mods_token = _exchange([mods_cols], scatter=False, name="gather_mods")
    mods = lax.dynamic_index_in_dim(mods_all, me, axis=1, keepdims=False).reshape(N_MOD, D)

    groups = dict(gu1=("gu1",), d1=("d1",), mix=("win", "cp", "ap", "out"), ffn2=("gu2", "d2"))
    in_flight = {}
    first = [shard("gu1", mods_token)]
    sems, srcs, lands, token = _tl_start(first, [_own_slot(s, me) for s in first], [[0]],
                                         name="gather_weights_start_gu1")
    in_flight["gu1"] = [sems[0], srcs, lands, None]
    rest = [n for n in BIG if n != "gu1"]
    shards = [shard(n, token) for n in rest]
    rest_groups = [[rest.index(n) for n in names] for g, names in groups.items() if g != "gu1"]
    sems, srcs, lands, rest_token = _tl_start(shards, [_own_slot(s, me) for s in shards], rest_groups,
                                              name="gather_weights_start_rest")
    for (g, names), gsems, idx in zip([kv for kv in groups.items() if kv[0] != "gu1"], sems, rest_groups):
        in_flight[g] = [gsems, [srcs[t] for t in idx], [lands[t] for t in idx], None]

    def forward(group, after):
        sems1, gsrcs, glands, _ = in_flight[group]
        sems2, gsrcs, glands = _tl_forward(gsrcs, glands, sems1, after, name="gather_weights_forward_" + group)
        in_flight[group] = [sems1, gsrcs, glands, sems2]

    forward_early = dict(d1="mix", mix="ffn2")

    tables = _rope_tables(x.shape[1], rest_token)

    def w_get(group, after):
        if group == "gu1":
            after = tables[0]
        if in_flight[group][3] is None:
            forward(group, after)
        sems1, gsrcs, glands, sems2 = in_flight[group]
        landed = _tl_wait(gsrcs, glands, sems1, sems2, after, name="gather_weights_wait_" + group)
        if group in forward_early:
            forward(forward_early[group], landed[0])
        return {n: a.reshape(-1, D) for n, a in zip(groups[group], landed)}

    pending = []

    def g_put(gs):
        names = tuple(gs)
        srcs = [gs[n].reshape(N_DEV, -1, D) for n in names]
        lands = [_own_slot(lax.dynamic_index_in_dim(s, me, axis=0, keepdims=False), me) for s in srcs]
        sems, srcs, lands, tok = _split_start(srcs, lands, [list(range(len(names)))], scatter=True,
                                              name="scatter_grads_start_" + names[0])
        pending.append((names, sems[0], srcs, lands))
        return tok

    lsum, grad_x, small = _local_step(x[0], loss_target[0], mods, g_ffn1, g_mix, g_ffn2, g_final[None],
                                      _pad_to(conv_full, 8, D), sinks[0], w_get, g_put, tables)
    loss = lax.psum((0.5 / D) * jnp.sum(lsum), ("x", "y", "c"))

    packed = _pack_small(small["mods"], small["g1"], small["gm"], small["g2"], small["gf"], small["convw"],
                         small["sinks"])
    packed_all, _ = _exchange([packed], scatter=False, name="gather_small")
    gsmall = _sum8(packed_all, name="sum_small")

    w_of = dict(ada=w_ada, gu1=w1_gu, d1=w1_down, win=w_in, cp=w_conv_proj, ap=w_attn_proj, out=w_out, gu2=w2_gu,
                d2=w2_down)
    m_of = dict(ada=m_w_ada, gu1=m_w1_gu, d1=m_w1_down, win=m_w_in, cp=m_w_conv_proj, ap=m_w_attn_proj, out=m_w_out,
                gu2=m_w2_gu, d2=m_w2_down)
    v_of = dict(ada=v_w_ada, gu1=v_w1_gu, d1=v_w1_down, win=v_w_in, cp=v_w_conv_proj, ap=v_w_attn_proj, out=v_w_out,
                gu2=v_w2_gu, d2=v_w2_down)
    upd = {}
    after = gsmall
    for names, sems, srcs, lands in pending:
        parts = _split_wait(srcs, lands, sems, after, scatter=True, name="scatter_grads_wait_" + names[0])
        for n, p in zip(names, parts):
            if n in TRANSPOSED:
                res = _adam(jnp.swapaxes(w_of[n], 1, 2), p, jnp.swapaxes(m_of[n], 1, 2), jnp.swapaxes(v_of[n], 1, 2),
                            tm=TM_ADAM, name="adam_" + n)
                upd[n] = [jnp.swapaxes(t, 1, 2) for t in res]
            else:
                upd[n] = _adam(w_of[n], p, m_of[n], v_of[n], tm=TM_ADAM, name="adam_" + n)
        after = upd[names[-1]][1]

    gm_cols = lax.dynamic_slice(packed_all[:, R_MODS:R_MODS + N_MOD, :].reshape(N_DEV, N_MOD * D),
                                (0, me * ada_cols), (N_DEV, ada_cols))
    upd["ada"] = _adam(w_ada, _wada_grad(c_all.T, gm_cols, name="ada_dw"), m_w_ada, v_w_ada, tm=256, name="adam_ada")
    conv_g = lax.dynamic_slice(gsmall, (R_CONV, me * conv_cols), (3, conv_cols))

    def natural(b, g1, gm, g2, gf, cw, sk):
        return dict(b_ada=b, g_ffn1=g1, g_mix=gm, g_ffn2=g2, g_final=gf[None], conv_w=cw[0], sinks=sk)

    small_out = _adam_small(gsmall, conv_g, natural(b_ada, g_ffn1, g_mix, g_ffn2, g_final, conv_w, sinks),
                            natural(m_b_ada, m_g_ffn1, m_g_mix, m_g_ffn2, m_g_final, m_conv_w, m_sinks),
                            natural(v_b_ada, v_g_ffn1, v_g_mix, v_g_ffn2, v_g_final, v_conv_w, v_sinks),
                            name="adam_small")
    for res in small_out:
        res["g_final"] = res["g_final"][0]
        res["conv_w"] = res["conv_w"][None]

    big_name = dict(w_ada="ada", w1_gu="gu1", w1_down="d1", w_in="win", w_conv_proj="cp", w_attn_proj="ap",
                    w_out="out", w2_gu="gu2", w2_down="d2")
    order = ("w_ada", "b_ada", "g_ffn1", "w1_gu", "w1_down", "g_mix", "w_in", "conv_w", "w_conv_proj", "w_attn_proj",
             "sinks", "w_out", "g_ffn2", "w2_gu", "w2_down", "g_final")
    outs = [loss, grad_x[None]]
    for kind in range(4):
        for n in order:
            outs.append(upd[big_name[n]][kind] if n in big_name else small_out[kind][n])
    return tuple(outs)
```

```python
import jax
import jax.numpy as jnp
from jax import lax
from jax.experimental import pallas as pl
from jax.experimental.pallas import tpu as pltpu

D = 1024
F = 2816
NIN = 6656
N_HEADS = 16
N_KV = 4
HEAD_DIM = 64
BLK = 128
N_MOD = 9
N_DEV = 8
EPS = 1e-6
NEG_INF = -1e30
ROPE_THETA = 10000.0
O_BG, O_CG, O_U, O_Q, O_K, O_V, O_ZC, O_ZA = 0, 1024, 2048, 3072, 4096, 4352, 4608, 5632

ADAM_LR = 0.001
ADAM_B1 = 0.9
ADAM_B2 = 0.999
ADAM_EPS = 1e-08
ADAM_WD = 0.01
ADAM_STEP = 10

BF = jnp.bfloat16
F32 = jnp.float32
VMEM_LIMIT = 56 * 1024 * 1024
MXU_N = 256
MESH = pl.DeviceIdType.MESH

NT = (((1,), (1,)), ((), ()))
TN = (((0,), (0,)), ((), ()))


def _cp(sem=None):
    return pltpu.CompilerParams(dimension_semantics=sem, vmem_limit_bytes=VMEM_LIMIT)


def _tile(n, pref):
    if n <= pref:
        return n
    for t in range(pref - pref % 16, 15, -16):
        if n % t == 0:
            return t
    raise ValueError((n, pref))


def _sigmoid(v):
    return 0.5 * jnp.tanh(0.5 * v) + 0.5


def _row(i):
    return (i, 0)


def _const2(*_):
    return (0, 0)


def _resident(shape):
    return pl.BlockSpec(shape, lambda *_: (0,) * len(shape), pipeline_mode=pl.Buffered(1))


def _norm_proj(x, g, sc, sh, wt, rope=None, *, tm, tn, name):
    T, N = x.shape[0], wt.shape[0]
    tm = _tile(T, tm)

    def body(x_ref, g_ref, sc_ref, sh_ref, w_ref, *rest):
        if rope is None:
            h_ref, o_ref = rest
        else:
            c_ref, s_ref, h_ref, o_ref, qs_ref, kr_ref = rest
        xv = x_ref[...]
        r = lax.rsqrt(jnp.mean(xv * xv, axis=-1, keepdims=True) + EPS)
        hb = ((xv * r) * g_ref[...] * (1.0 + sc_ref[...]) + sh_ref[...]).astype(BF)
        h_ref[...] = hb
        for c0 in range(0, N, tn):
            cols = pl.ds(c0, tn)
            o_ref[:, cols] = lax.dot_general(hb, w_ref[cols, :], NT, preferred_element_type=F32).astype(BF)
            if rope is not None and c0 < O_V <= c0 + tn:
                _attn_prep_tile(o_ref, c_ref, s_ref, qs_ref, kr_ref, tm)

    vec = pl.BlockSpec((1, D), _const2)
    rowspec = pl.BlockSpec((tm, D), _row)
    in_specs = [rowspec, vec, vec, vec, _resident((N, D))]
    out_specs = [rowspec, pl.BlockSpec((tm, N), _row)]
    out_shape = [jax.ShapeDtypeStruct((T, D), BF), jax.ShapeDtypeStruct((T, N), BF)]
    args = [x, g, sc, sh, wt]
    if rope is not None:
        in_specs += [pl.BlockSpec((tm, 128), _row)] * 2
        out_specs += [pl.BlockSpec((N_KV, 4 * tm, 128), lambda i: (0, i, 0)), pl.BlockSpec((tm, 256), _row)]
        out_shape += [jax.ShapeDtypeStruct((N_KV, 4 * T, 128), BF), jax.ShapeDtypeStruct((T, 256), BF)]
        args += list(rope)
    return pl.pallas_call(
        body, name=name, grid=(T // tm,),
        in_specs=in_specs, out_specs=out_specs, out_shape=out_shape,
        compiler_params=_cp(("parallel",)),
    )(*args)


def _ffn_down_fwd(ab, wd, x, gt, *, tm, name):
    T = x.shape[0]
    tm = _tile(T, tm)

    def body(a_ref, b_ref, wd_ref, x_ref, gt_ref, xo_ref, y_ref):
        y = None
        for c0 in range(0, F, MXU_N):
            cols = pl.ds(c0, MXU_N)
            a = a_ref[:, cols].astype(F32)
            act = (a * _sigmoid(a) * b_ref[:, cols].astype(F32)).astype(BF)
            part = jnp.dot(act, wd_ref[cols, :], preferred_element_type=F32)
            y = part if y is None else y + part
        y_ref[...] = y.astype(BF)
        xo_ref[...] = x_ref[...] + (0.5 * gt_ref[...]) * y

    return pl.pallas_call(
        body, name=name, grid=(T // tm,),
        in_specs=[pl.BlockSpec((tm, F), lambda i: (i, 0)), pl.BlockSpec((tm, F), lambda i: (i, 1)),
                  _resident((F, D)), pl.BlockSpec((tm, D), _row), pl.BlockSpec((1, D), _const2)],
        out_specs=[pl.BlockSpec((tm, D), _row), pl.BlockSpec((tm, D), _row)],
        out_shape=[jax.ShapeDtypeStruct((T, D), F32), jax.ShapeDtypeStruct((T, D), BF)],
        compiler_params=_cp(("parallel",)),
    )(ab, ab, wd, x, gt)


def _ffn_fwd(x, g, sc, sh, gt, wgu, wd, final, *, tm, name):
    T = x.shape[0]
    tm = _tile(T, tm)
    last = final is not None

    def body(x_ref, g_ref, sc_ref, sh_ref, gt_ref, wgu_ref, wd_ref, *rest):
        if last:
            t_ref, gf_ref, h_ref, ab_ref, y_ref, dx_ref, ls_ref, dgf_ref = rest
        else:
            h_ref, ab_ref, y_ref, xo_ref = rest
        xv = x_ref[...]
        r = lax.rsqrt(jnp.mean(xv * xv, axis=-1, keepdims=True) + EPS)
        hb = ((xv * r) * g_ref[...] * (1.0 + sc_ref[...]) + sh_ref[...]).astype(BF)
        h_ref[...] = hb
        y = None
        for c0 in range(0, F, MXU_N):
            a = lax.dot_general(hb, wgu_ref[pl.ds(c0, MXU_N), :], NT, preferred_element_type=F32)
            b = lax.dot_general(hb, wgu_ref[pl.ds(F + c0, MXU_N), :], NT, preferred_element_type=F32)
            ab = a.astype(BF)
            bb = b.astype(BF)
            ab_ref[:, pl.ds(c0, MXU_N)] = ab
            ab_ref[:, pl.ds(F + c0, MXU_N)] = bb
            a = ab.astype(F32)
            act = (a * _sigmoid(a) * bb.astype(F32)).astype(BF)
            part = jnp.dot(act, wd_ref[pl.ds(c0, MXU_N), :], preferred_element_type=F32)
            y = part if y is None else y + part
        y_ref[...] = y.astype(BF)
        xo = xv + (0.5 * gt_ref[...]) * y
        if not last:
            xo_ref[...] = xo
            return

        @pl.when(pl.program_id(0) == 0)
        def _():
            ls_ref[...] = jnp.zeros_like(ls_ref)
            dgf_ref[...] = jnp.zeros_like(dgf_ref)
        gv = gf_ref[...]
        r = lax.rsqrt(jnp.mean(xo * xo, axis=-1, keepdims=True) + EPS)
        xh = xo * r
        e = xh * gv - t_ref[...]
        ls_ref[...] += jnp.sum(e * e, axis=0, keepdims=True)
        dy = e * (1.0 / D)
        dgf_ref[...] += jnp.sum(dy * xh, axis=0, keepdims=True)
        dxh = dy * gv
        dx_ref[...] = r * (dxh - xh * jnp.mean(dxh * xh, axis=-1, keepdims=True))

    vec = pl.BlockSpec((1, D), _const2)
    rowspec = pl.BlockSpec((tm, D), _row)
    in_specs = [rowspec, vec, vec, vec, vec, _resident((2 * F, D)), _resident((F, D))]
    out_specs = [rowspec, pl.BlockSpec((tm, 2 * F), _row), rowspec, rowspec]
    out_shape = [jax.ShapeDtypeStruct((T, D), BF), jax.ShapeDtypeStruct((T, 2 * F), BF),
                 jax.ShapeDtypeStruct((T, D), BF), jax.ShapeDtypeStruct((T, D), F32)]
    args = [x, g, sc, sh, gt, wgu, wd]
    if last:
        in_specs += [rowspec, vec]
        out_specs += [vec, vec]
        out_shape += [jax.ShapeDtypeStruct((1, D), F32)] * 2
        args += list(final)
    return pl.pallas_call(
        body, name=name, grid=(T // tm,),
        in_specs=in_specs, out_specs=out_specs, out_shape=out_shape,
        compiler_params=_cp(("arbitrary",) if last else ("parallel",)),
    )(*args)


def _ffn_down_bwd_dw(dxo, y, gt, ab, wd, *, tm, name):
    T = dxo.shape[0]
    tm = _tile(T, tm)
    nt = T // tm
    hw = F // 2
    chunks = [(c0, min(MXU_N, hw - c0)) for c0 in range(0, hw, MXU_N)]

    def body(dxo_ref, y_ref, gt_ref, a_ref, b_ref, wd_ref, dab_ref, dgt_ref, dwd_ref, dys, dyt, acc, stage, sem):
        i, j = pl.program_id(0), pl.program_id(1)

        @pl.when(jnp.logical_and(i == 0, j == 0))
        def _():
            dgt_ref[...] = jnp.zeros_like(dgt_ref)

        @pl.when(j == 0)
        def _():
            dxv = dxo_ref[...]
            dgt_ref[...] += 0.5 * jnp.sum(dxv * y_ref[...].astype(F32), axis=0, keepdims=True)
            dyf = (0.5 * gt_ref[...]) * dxv
            dys[...] = dyf.astype(BF)
            dyt[...] = dyf.T.astype(BF)

        def half(jj):
            @pl.when(i == 0)
            def _():
                acc[jj] = jnp.zeros((D, hw), F32)

            dy = dys[...]
            dy_t = dyt[...]
            for c0, cw in chunks:
                cols = pl.ds(c0, cw)
                dact = lax.dot_general(dy, wd_ref[pl.ds(jj * hw + c0, cw), :], NT, preferred_element_type=F32)
                a = a_ref[:, cols].astype(F32)
                b = b_ref[:, cols].astype(F32)
                s = _sigmoid(a)
                silu = a * s
                dab_ref[0, :, cols] = (dact * b * (s * (1.0 + a * (1.0 - s)))).astype(BF)
                dab_ref[1, :, cols] = (dact * silu).astype(BF)
                acc[jj, :, cols] += jnp.dot(dy_t, (silu * b).astype(BF), preferred_element_type=F32)

            @pl.when(i == nt - 1)
            def _():
                for c0, cw in chunks:
                    stage[0:cw, :] = acc[jj, :, pl.ds(c0, cw)].T.astype(BF)
                    out = pltpu.make_async_copy(stage.at[pl.ds(0, cw)], dwd_ref.at[pl.ds(jj * hw + c0, cw)], sem)
                    out.start()
                    out.wait()

        for jj in range(2):
            pl.when(j == jj)(lambda jj=jj: half(jj))

    vec = pl.BlockSpec((1, D), _const2)
    rowspec = pl.BlockSpec((tm, D), lambda i, j: (i, 0))
    return pl.pallas_call(
        body, name=name, grid=(nt, 2),
        in_specs=[rowspec, rowspec, vec, pl.BlockSpec((tm, hw), lambda i, j: (i, j)),
                  pl.BlockSpec((tm, hw), lambda i, j: (i, j + 2)), _resident((F, D))],
        out_specs=[pl.BlockSpec((2, tm, hw), lambda i, j: (0, i, j)), vec, pl.BlockSpec(memory_space=pl.ANY)],
        out_shape=[jax.ShapeDtypeStruct((2, T, F), BF), jax.ShapeDtypeStruct((1, D), F32),
                   jax.ShapeDtypeStruct((F, D), BF)],
        scratch_shapes=[pltpu.VMEM((tm, D), BF), pltpu.VMEM((D, tm), BF), pltpu.VMEM((2, D, hw), F32),
                        pltpu.VMEM((MXU_N, D), BF), pltpu.SemaphoreType.DMA(())],
        compiler_params=_cp(("arbitrary", "arbitrary")),
    )(dxo, y, gt, ab, ab, wd)


def _tn_matmul(a, b, token=None, *, tn, tk, name):
    S, T, Ns = a.shape
    tn, tk = _tile(Ns, tn), _tile(T, tk)
    nk, njs = T // tk, Ns // tn
    deps = [] if token is None else [token]

    def body(a_ref, b_ref, *rest):
        o_ref, acc = rest[len(deps):]
        k = pl.program_id(1)

        @pl.when(k == 0)
        def _():
            acc[...] = jnp.zeros_like(acc)
        acc[...] += lax.dot_general(a_ref[0], b_ref[...], TN, preferred_element_type=F32)

        @pl.when(k == nk - 1)
        def _():
            o_ref[...] = acc[...].astype(BF)

    return pl.pallas_call(
        body, name=name, grid=(S * njs, nk),
        in_specs=[pl.BlockSpec((1, tk, tn), lambda j, k: (j // njs, k, j % njs)),
                  pl.BlockSpec((tk, D), lambda j, k: (k, 0))] + [pl.BlockSpec(memory_space=pl.ANY)] * len(deps),
        out_specs=pl.BlockSpec((tn, D), lambda j, k: (j, 0)),
        out_shape=jax.ShapeDtypeStruct((S * Ns, D), BF),
        scratch_shapes=[pltpu.VMEM((tn, D), F32)],
        compiler_params=_cp(("parallel", "arbitrary")),
    )(a, b, *deps)


def _nn_bwd_norm(da, w, x, g, sc, dxo, *, tm, name):
    S, T, Ks = da.shape
    tm = _tile(T, tm)
    rc = _tile(tm, 256)

    def body(da_ref, w_ref, x_ref, g_ref, sc_ref, dxo_ref, dx_ref, dsh_ref, dsc_ref, dg_ref, acc):
        @pl.when(pl.program_id(0) == 0)
        def _():
            dsh_ref[...] = jnp.zeros_like(dsh_ref)
            dsc_ref[...] = jnp.zeros_like(dsc_ref)
            dg_ref[...] = jnp.zeros_like(dg_ref)

        d = jnp.dot(da_ref[0], w_ref[0:Ks, :], preferred_element_type=F32)
        for s in range(1, S):
            d = d + jnp.dot(da_ref[s], w_ref[s * Ks:(s + 1) * Ks, :], preferred_element_type=F32)
        acc[...] = d
        gv = g_ref[...]
        sc1 = 1.0 + sc_ref[...]
        dsh = jnp.zeros((1, D), F32)
        dsc = jnp.zeros((1, D), F32)
        dg = jnp.zeros((1, D), F32)
        for r0 in range(0, tm, rc):
            rows = pl.ds(r0, rc)
            u = acc[rows, :]
            xv = x_ref[rows, :]
            r = lax.rsqrt(jnp.mean(xv * xv, axis=-1, keepdims=True) + EPS)
            xh = xv * r
            dsh = dsh + jnp.sum(u, axis=0, keepdims=True)
            dsc = dsc + jnp.sum(u * (xh * gv), axis=0, keepdims=True)
            us = u * sc1
            dg = dg + jnp.sum(us * xh, axis=0, keepdims=True)
            dxh = us * gv
            dx_ref[rows, :] = dxo_ref[rows, :] + r * (dxh - xh * jnp.mean(dxh * xh, axis=-1, keepdims=True))
        dsh_ref[...] += dsh
        dsc_ref[...] += dsc
        dg_ref[...] += dg

    vec = pl.BlockSpec((1, D), _const2)
    rowspec = pl.BlockSpec((tm, D), _row)
    return pl.pallas_call(
        body, name=name, grid=(T // tm,),
        in_specs=[pl.BlockSpec((S, tm, Ks), lambda i: (0, i, 0)), _resident((S * Ks, D)), rowspec, vec, vec, rowspec],
        out_specs=[rowspec, vec, vec, vec],
        out_shape=[jax.ShapeDtypeStruct((T, D), F32)] + [jax.ShapeDtypeStruct((1, D), F32)] * 3,
        scratch_shapes=[pltpu.VMEM((tm, D), F32)],
        compiler_params=_cp(("arbitrary",)),
    )(da, w, x, g, sc, dxo)


def _rope(t, cos, sin_signed, lt32, inverse=False):
    sel = jnp.where(lt32, pltpu.roll(t, 96, 1), pltpu.roll(t, 32, 1))
    return t * cos - sel * sin_signed if inverse else t * cos + sel * sin_signed


def _rope_tables(T, token=None):
    inv = 1.0 / (ROPE_THETA ** (jnp.arange(0, HEAD_DIM, 2, dtype=F32) / HEAD_DIM))
    ang = _behind(jnp.arange(T, dtype=F32)[:, None] * inv[None, :], token)
    cos, sin = jnp.cos(ang), jnp.sin(ang)
    cos128 = jnp.tile(cos, (1, 4))
    sin128 = jnp.tile(jnp.concatenate([-sin, sin], axis=1), (1, 2))
    return cos128, sin128


QSCALE = HEAD_DIM ** -0.5


def _lane_masks(rows):
    lane = lax.broadcasted_iota(jnp.int32, (rows, 128), 1)
    return (lane % HEAD_DIM) < (HEAD_DIM // 2), [lane < HEAD_DIM, lane >= HEAD_DIM]


def _attn_bias():
    qi = lax.broadcasted_iota(jnp.int32, (4 * BLK, 2 * BLK), 0) % BLK
    kj = lax.broadcasted_iota(jnp.int32, (4 * BLK, 2 * BLK), 1)
    band = (kj > qi) & (kj <= qi + BLK)
    return jnp.stack([jnp.where(band & (kj >= BLK), 0.0, NEG_INF), jnp.where(band, 0.0, NEG_INF)]).astype(F32)


def _attn_prep_tile(proj_ref, c_ref, s_ref, qs_ref, kr_ref, tm):
    lt32, halves = _lane_masks(BLK)
    for b in range(tm // BLK):
        rows = pl.ds(b * BLK, BLK)
        cc, sc = c_ref[rows, :], s_ref[rows, :]
        qr = [_rope(proj_ref[rows, pl.ds(O_Q + p * 128, 128)].astype(F32), cc, sc, lt32) * QSCALE for p in range(8)]
        for g in range(N_KV):
            qs_ref[g, pl.ds(4 * b * BLK, 4 * BLK), :] = _stack_heads(qr, g, halves).astype(BF)
        kr_ref[rows, :] = jnp.concatenate([_rope(proj_ref[rows, pl.ds(O_K + r * 128, 128)].astype(F32), cc, sc, lt32)
                                           for r in range(2)], axis=1).astype(BF)


def _attn_specs():
    prev = lambda n: jnp.maximum(n - 1, 0)
    return [pl.BlockSpec((N_KV, 4 * BLK, 128), lambda n: (0, n, 0)),
            pl.BlockSpec((BLK, 256), _row), pl.BlockSpec((BLK, 256), lambda n: (prev(n), 0)),
            pl.BlockSpec((BLK, 256), lambda n: (n, O_V // 256)),
            pl.BlockSpec((BLK, 256), lambda n: (prev(n), O_V // 256)),
            pl.BlockSpec((1, 4 * BLK, 2 * BLK), lambda n: (jnp.minimum(n, 1), 0, 0)),
            pl.BlockSpec(memory_space=pltpu.SMEM)]


def _bands(kc_ref, kp_ref, vc_ref, vp_ref):
    kb, vb = [], []
    for r in range(2):
        cols = slice(r * 128, (r + 1) * 128)
        kb.append(jnp.concatenate([kp_ref[:, cols], kc_ref[:, cols]], axis=0))
        vb.append(jnp.concatenate([vp_ref[:, cols], vc_ref[:, cols]], axis=0))
    return kb, vb


def _sink_rows(sink_ref, g):
    return jnp.concatenate([jnp.full((BLK, 128), sink_ref[4 * g + hh], F32) for hh in range(4)], axis=0)


def _both(t):
    return jnp.concatenate([t, t], axis=1)


def _unstack_heads(t, g, halves, acc):
    half = g % 2
    for hh in range(4):
        h = 4 * g + hh
        th = jnp.where(halves[half], t[hh * BLK:(hh + 1) * BLK], 0.0)
        if h % 2 != half:
            th = pltpu.roll(th, HEAD_DIM, 1)
        acc[h // 2] = acc[h // 2] + th


def _stack_heads(chunks, g, halves):
    half = g % 2
    parts = []
    for hh in range(4):
        h = 4 * g + hh
        t = chunks[h // 2]
        if h % 2 != half:
            t = pltpu.roll(t, HEAD_DIM, 1)
        parts.append(jnp.where(halves[half], t, 0.0))
    return jnp.concatenate(parts, axis=0)


def _attn_fwd(qs, kr, proj, bias, sinks, *, name):
    T = proj.shape[0]
    nb = T // BLK

    def body(qs_ref, kc_ref, kp_ref, vc_ref, vp_ref, bias_ref, sink_ref, o_ref, lse_ref):
        _, h128 = _lane_masks(BLK)
        _, h256 = _lane_masks(2 * BLK)
        _, h512 = _lane_masks(4 * BLK)
        kb, vb = _bands(kc_ref, kp_ref, vc_ref, vp_ref)
        outs = [jnp.zeros((BLK, 128), F32) for _ in range(8)]
        groups = range(N_KV)
        bias = bias_ref[0]
        sink = [_sink_rows(sink_ref, g) for g in groups]
        s = [lax.dot_general(qs_ref[g], kb[g // 2], NT, preferred_element_type=F32) + bias for g in groups]
        m = [jnp.maximum(jnp.broadcast_to(jnp.max(s[g], axis=-1, keepdims=True), (4 * BLK, 128)), sink[g])
             for g in groups]
        p = [jnp.exp(s[g] - _both(m[g])).astype(BF) for g in groups]
        vg = [jnp.where(h256[g % 2], vb[g // 2].astype(F32), 1.0).astype(BF) for g in groups]
        o = [jnp.dot(p[g], vg[g], preferred_element_type=F32) for g in groups]
        denom = [jnp.where(h512[g % 2], pltpu.roll(o[g], HEAD_DIM, 1), o[g]) + jnp.exp(sink[g] - m[g]) for g in groups]
        for g in groups:
            lse_ref[g] = m[g] + jnp.log(denom[g])
            _unstack_heads(o[g] * (1.0 / denom[g]), g, h128, outs)
        o_ref[...] = jnp.concatenate(outs, axis=1).astype(BF)

    return pl.pallas_call(
        body, name=name, grid=(nb,),
        in_specs=_attn_specs(),
        out_specs=[pl.BlockSpec((BLK, D), _row), pl.BlockSpec((N_KV, 4 * BLK, 128), lambda n: (0, n, 0))],
        out_shape=[jax.ShapeDtypeStruct((T, D), BF), jax.ShapeDtypeStruct((N_KV, 4 * T, 128), F32)],
        compiler_params=_cp(("parallel",)),
    )(qs, kr, kr, proj, proj, bias, sinks)


def _attn_bwd(qs, kr, proj, bias, sinks, lse, o, do, cos, sin, dproj, *, name):
    T = proj.shape[0]
    nb = T // BLK

    def body(qs_ref, kc_ref, kp_ref, vc_ref, vp_ref, bias_ref, sink_ref, lse_ref, o_ref, do_ref,
             cc_ref, sc_ref, cp_ref, sp_ref, dproj_ref, dq_ref, dkc_ref, dkp_ref, dvc_ref, dvp_ref, dsink_ref):
        @pl.when(pl.program_id(0) == 0)
        def _():
            dsink_ref[...] = jnp.zeros_like(dsink_ref)
        lt32, h128 = _lane_masks(BLK)
        kb, vb = _bands(kc_ref, kp_ref, vc_ref, vp_ref)
        oc = [o_ref[:, p * 128:(p + 1) * 128].astype(F32) for p in range(8)]
        doc = [do_ref[:, p * 128:(p + 1) * 128].astype(F32) for p in range(8)]
        dqs = [jnp.zeros((BLK, 128), F32) for _ in range(8)]
        lane1 = lax.broadcasted_iota(jnp.int32, (1, 128), 1)
        dsink = jnp.zeros((1, 128), F32)
        groups = range(N_KV)
        bias = bias_ref[0]
        q = [qs_ref[g] for g in groups]
        lse_g = [lse_ref[g] for g in groups]
        s = [lax.dot_general(q[g], kb[g // 2], NT, preferred_element_type=F32) + bias for g in groups]
        dos = [_stack_heads(doc, g, h128) for g in groups]
        dosb = [t.astype(BF) for t in dos]
        dp = [lax.dot_general(dosb[g], vb[g // 2], NT, preferred_element_type=F32) for g in groups]
        delta = [jnp.broadcast_to(jnp.sum(dos[g] * _stack_heads(oc, g, h128), axis=-1, keepdims=True), (4 * BLK, 128))
                 for g in groups]
        p = [jnp.exp(s[g] - _both(lse_g[g])) for g in groups]
        ds = [(p[g] * (dp[g] - _both(delta[g]))).astype(BF) for g in groups]
        pb = [t.astype(BF) for t in p]
        dvg = [lax.dot_general(pb[g], dosb[g], TN, preferred_element_type=F32) for g in groups]
        dkg = [lax.dot_general(ds[g], q[g], TN, preferred_element_type=F32) for g in groups]
        dqg = [jnp.dot(ds[g], kb[g // 2], preferred_element_type=F32) * QSCALE for g in groups]
        dvr = [dvg[0] + dvg[1], dvg[2] + dvg[3]]
        dkr = [dkg[0] + dkg[1], dkg[2] + dkg[3]]
        for g in groups:
            _unstack_heads(dqg[g], g, h128, dqs)
            dsk = -jnp.exp(_sink_rows(sink_ref, g) - lse_g[g]) * delta[g]
            for hh in range(4):
                val = jnp.sum(dsk[hh * BLK:(hh + 1) * BLK], axis=0, keepdims=True)
                dsink = dsink + jnp.where(lane1 == 4 * g + hh, val, 0.0)
        cc, sc, cp, sp = cc_ref[...], sc_ref[...], cp_ref[...], sp_ref[...]
        dsink_ref[...] += dsink
        dq_ref[...] = jnp.concatenate([_rope(t, cc, sc, lt32, inverse=True) for t in dqs], axis=1).astype(BF)
        dkp_ref[...] = jnp.concatenate([_rope(t[:BLK], cp, sp, lt32, inverse=True) for t in dkr], axis=1)
        dkc_ref[...] = jnp.concatenate([_rope(t[BLK:], cc, sc, lt32, inverse=True) for t in dkr], axis=1)
        dvp_ref[...] = jnp.concatenate([t[:BLK] for t in dvr], axis=1)
        dvc_ref[...] = jnp.concatenate([t[BLK:] for t in dvr], axis=1)

    kv = pl.BlockSpec((BLK, 256), _row)
    tc = pl.BlockSpec((BLK, 128), _row)
    tp = pl.BlockSpec((BLK, 128), lambda n: (jnp.maximum(n - 1, 0), 0))
    return pl.pallas_call(
        body, name=name, grid=(nb,),
        in_specs=_attn_specs() + [pl.BlockSpec((N_KV, 4 * BLK, 128), lambda n: (0, n, 0)),
                                  pl.BlockSpec((BLK, D), _row), pl.BlockSpec((BLK, D), _row), tc, tc, tp, tp,
                                  pl.BlockSpec(memory_space=pl.ANY)],
        out_specs=[pl.BlockSpec((BLK, D), lambda n: (n, O_Q // D)), kv, kv, kv, kv, pl.BlockSpec((1, 128), _const2)],
        out_shape=[jax.ShapeDtypeStruct(dproj.shape, BF)] + [jax.ShapeDtypeStruct((T, 256), F32)] * 4
        + [jax.ShapeDtypeStruct((1, 128), F32)],
        input_output_aliases={14: 0},
        compiler_params=_cp(("arbitrary",)),
    )(qs, kr, kr, proj, proj, bias, sinks, lse, o, do, cos, sin, cos, sin, dproj)


def _dkv_combine(dkc, dkp, dvc, dvp, dproj, *, name):
    T = dkc.shape[0]
    nb = T // BLK
    tm = _tile(T, 4 * BLK)
    bpt = tm // BLK
    nt = T // tm

    def body(dkc_ref, dkp_ref, dkn_ref, dvc_ref, dvp_ref, dvn_ref, dproj_ref, o_ref):
        keep = jnp.where(pl.program_id(0) == nt - 1, 0.0, 1.0)

        def shifted(prev_ref, next_ref):
            nxt = keep * next_ref[...]
            return nxt if bpt == 1 else jnp.concatenate([prev_ref[BLK:, :], nxt], axis=0)

        o_ref[:, 0:256] = (dkc_ref[...] + shifted(dkp_ref, dkn_ref)).astype(BF)
        o_ref[:, 256:512] = (dvc_ref[...] + shifted(dvp_ref, dvn_ref)).astype(BF)

    cur = pl.BlockSpec((tm, 256), _row)
    nxt = pl.BlockSpec((BLK, 256), lambda i: (jnp.minimum((i + 1) * bpt, nb - 1), 0))
    return pl.pallas_call(
        body, name=name, grid=(nt,),
        in_specs=[cur, cur, nxt, cur, cur, nxt, pl.BlockSpec(memory_space=pl.ANY)],
        out_specs=pl.BlockSpec((tm, 512), lambda i: (i, O_K // 512)),
        out_shape=jax.ShapeDtypeStruct(dproj.shape, BF),
        input_output_aliases={6: 0},
        compiler_params=_cp(("parallel",)),
    )(dkc, dkp, dkp, dvc, dvp, dvp, dproj)


HALO = 16


def _conv_shifts(cu, hprev, tm):
    row = lax.broadcasted_iota(jnp.int32, (8, cu.shape[1]), 0)
    h1 = hprev[HALO - 1:HALO, :]
    h2 = hprev[HALO - 2:HALO - 1, :]
    m1 = pltpu.roll(cu, 1, 0)
    m2 = pltpu.roll(cu, 2, 0)
    m1 = jnp.concatenate([jnp.where(row == 0, h1, m1[0:8]), m1[8:]], axis=0)
    m2 = jnp.concatenate([jnp.where(row == 0, h2, jnp.where(row == 1, h1, m2[0:8])), m2[8:]], axis=0)
    return m1, m2


def _mixer_mid_fwd(proj, attn, wcp, wap, wout, convw, x, gt, *, tm, name):
    T = x.shape[0]
    tm = _tile(T, tm)
    hb = tm // HALO

    def body(bg_ref, cg_ref, u_ref, hcg_ref, hu_ref, zc0_ref, zc1_ref, za0_ref, za1_ref, at_ref,
             wcp_ref, wap_ref, wout_ref, cw_ref, x_ref, gt_ref,
             x2_ref, gc_ref, yc_ref, ya_ref, mg_ref, o_ref):
        first = jnp.where(pl.program_id(0) == 0, 0.0, 1.0)
        cu = cg_ref[...].astype(F32) * u_ref[...].astype(F32)
        hprev = first * (hcg_ref[...].astype(F32) * hu_ref[...].astype(F32))
        m1, m2 = _conv_shifts(cu, hprev, tm)
        cv = cw_ref[0:1, :] * m2 + cw_ref[1:2, :] * m1 + cw_ref[2:3, :] * cu
        gc = (bg_ref[...].astype(F32) * cv).astype(BF)
        gc_ref[...] = gc
        yc = jnp.dot(gc, wcp_ref[...], preferred_element_type=F32)
        ya = jnp.dot(at_ref[...], wap_ref[...], preferred_element_type=F32)
        yc_ref[...] = yc.astype(BF)
        ya_ref[...] = ya.astype(BF)
        zc = jnp.concatenate([zc0_ref[...], zc1_ref[...]], axis=1).astype(F32)
        za = jnp.concatenate([za0_ref[...], za1_ref[...]], axis=1).astype(F32)
        mg = (_sigmoid(zc) * yc + _sigmoid(za) * ya).astype(BF)
        mg_ref[...] = mg
        o = jnp.dot(mg, wout_ref[...], preferred_element_type=F32)
        o_ref[...] = o.astype(BF)
        x2_ref[...] = x_ref[...] + gt_ref[...] * o

    wspec = pl.BlockSpec((D, D), _const2)
    rowspec = pl.BlockSpec((tm, D), _row)
    return pl.pallas_call(
        body, name=name, grid=(T // tm,),
        in_specs=[_col(tm, O_BG), _col(tm, O_CG), _col(tm, O_U), _halo_prev(hb, O_CG), _halo_prev(hb, O_U),
                  _col(tm, O_ZC, 512), _col(tm, O_ZC + 512, 512), _col(tm, O_ZA, 512), _col(tm, O_ZA + 512, 512),
                  rowspec, wspec, wspec, wspec, pl.BlockSpec((8, D), _const2), rowspec, pl.BlockSpec((1, D), _const2)],
        out_specs=[rowspec] * 6,
        out_shape=[jax.ShapeDtypeStruct((T, D), F32)] + [jax.ShapeDtypeStruct((T, D), BF)] * 5,
        compiler_params=_cp(("parallel",)),
    )(proj, proj, proj, proj, proj, proj, proj, proj, proj, attn, wcp, wap, wout, convw, x, gt)


def _col(tm, c, w=D):
    assert c % w == 0
    return pl.BlockSpec((tm, w), lambda i: (i, c // w))


def _halo_prev(hb, c):
    return pl.BlockSpec((HALO, D), lambda i: (jnp.maximum(i * hb - 1, 0), c // D))


def _halo_next(hb, nblk, c=0):
    return pl.BlockSpec((HALO, D), lambda i: (jnp.minimum((i + 1) * hb, nblk - 1), c // D))


def _mixer_mid_bwd(dx2, gt, o, proj, yc, ya, wout, wcp, wap, *, tm, name):
    T = dx2.shape[0]
    tm = _tile(T, tm)
    nt = T // tm

    def body(dx_ref, gt_ref, o_ref, zc0_ref, zc1_ref, za0_ref, za1_ref, yc_ref, ya_ref, wout_ref, wcp_ref, wap_ref,
             dout_ref, dyc_ref, dya_ref, dgc_ref, dat_ref, dproj_ref, dgt_ref, dzs, sems):
        i = pl.program_id(0)
        slot = lax.rem(i, 2)

        def slab_copy(step, s):
            return pltpu.make_async_copy(
                dzs.at[s], dproj_ref.at[pl.ds(pl.multiple_of(step * tm, tm), tm), pl.ds(O_ZC, 2 * D)], sems.at[s])

        @pl.when(i == 0)
        def _():
            dgt_ref[...] = jnp.zeros_like(dgt_ref)

        dxv = dx_ref[...]
        dgt_ref[...] += jnp.sum(dxv * o_ref[...].astype(F32), axis=0, keepdims=True)
        dout = (gt_ref[...] * dxv).astype(BF)
        dout_ref[...] = dout
        dmg = lax.dot_general(dout, wout_ref[...], NT, preferred_element_type=F32)
        sc = _sigmoid(jnp.concatenate([zc0_ref[...], zc1_ref[...]], axis=1).astype(F32))
        sa = _sigmoid(jnp.concatenate([za0_ref[...], za1_ref[...]], axis=1).astype(F32))
        dyc = (dmg * sc).astype(BF)
        dya = (dmg * sa).astype(BF)
        dyc_ref[...] = dyc
        dya_ref[...] = dya
        dzs[slot, :, 0:D] = (dmg * yc_ref[...].astype(F32) * (sc * (1.0 - sc))).astype(BF)
        dzs[slot, :, D:2 * D] = (dmg * ya_ref[...].astype(F32) * (sa * (1.0 - sa))).astype(BF)
        slab_copy(i, slot).start()
        dgc_ref[...] = lax.dot_general(dyc, wcp_ref[...], NT, preferred_element_type=F32).astype(BF)
        dat_ref[...] = lax.dot_general(dya, wap_ref[...], NT, preferred_element_type=F32).astype(BF)

        @pl.when(i > 0)
        def _():
            slab_copy(i - 1, 1 - slot).wait()

        @pl.when(i == nt - 1)
        def _():
            slab_copy(i, slot).wait()

    def zcol(c):
        return pl.BlockSpec((tm, 512), lambda i: (i, c // 512))

    wspec = pl.BlockSpec((D, D), _const2)
    rowspec = pl.BlockSpec((tm, D), _row)
    vec = pl.BlockSpec((1, D), _const2)
    return pl.pallas_call(
        body, name=name, grid=(nt,),
        in_specs=[rowspec, vec, rowspec, zcol(O_ZC), zcol(O_ZC + 512), zcol(O_ZA), zcol(O_ZA + 512),
                  rowspec, rowspec, wspec, wspec, wspec],
        out_specs=[rowspec] * 5 + [pl.BlockSpec(memory_space=pl.ANY), vec],
        out_shape=[jax.ShapeDtypeStruct((T, D), BF)] * 5 + [jax.ShapeDtypeStruct((T, NIN), BF),
                                                            jax.ShapeDtypeStruct((1, D), F32)],
        scratch_shapes=[pltpu.VMEM((2, tm, 2 * D), BF), pltpu.SemaphoreType.DMA((2,))],
        compiler_params=_cp(("arbitrary",)),
    )(dx2, gt, o, proj, proj, proj, proj, yc, ya, wout, wcp, wap)


def _conv_bwd(dgc, proj, convw, dproj, *, tm, name):
    T = dgc.shape[0]
    tm = _tile(T, tm)
    hb = tm // HALO
    nblk = T // HALO
    nt = T // tm

    def body(dgc_ref, ndgc_ref, bg_ref, nbg_ref, cg_ref, u_ref, hcg_ref, hu_ref, cw_ref, dproj_ref, dp_ref, dcw_ref):
        i = pl.program_id(0)

        @pl.when(i == 0)
        def _():
            dcw_ref[...] = jnp.zeros_like(dcw_ref)
        first = jnp.where(i == 0, 0.0, 1.0)
        last = jnp.where(i == nt - 1, 0.0, 1.0)
        cg = cg_ref[...].astype(F32)
        u = u_ref[...].astype(F32)
        bg = bg_ref[...].astype(F32)
        dg = dgc_ref[...].astype(F32)
        cu = cg * u
        hprev = first * (hcg_ref[...].astype(F32) * hu_ref[...].astype(F32))
        m1, m2 = _conv_shifts(cu, hprev, tm)
        w0, w1, w2 = cw_ref[0:1, :], cw_ref[1:2, :], cw_ref[2:3, :]
        cv = w0 * m2 + w1 * m1 + w2 * cu
        dcv = dg * bg
        nxt = last * (ndgc_ref[...].astype(F32) * nbg_ref[...].astype(F32))
        n0, n1 = nxt[0:1, :], nxt[1:2, :]
        row = lax.broadcasted_iota(jnp.int32, (8, D), 0)
        p1 = pltpu.roll(dcv, tm - 1, 0)
        p2 = pltpu.roll(dcv, tm - 2, 0)
        p1 = jnp.concatenate([p1[:tm - 8], jnp.where(row == 7, n0, p1[tm - 8:])], axis=0)
        p2 = jnp.concatenate([p2[:tm - 8], jnp.where(row == 7, n1, jnp.where(row == 6, n0, p2[tm - 8:]))], axis=0)
        dcu = w2 * dcv + w1 * p1 + w0 * p2
        dp_ref[:, 0:D] = (dg * cv).astype(BF)
        dp_ref[:, D:2 * D] = (dcu * u).astype(BF)
        dp_ref[:, 2 * D:3 * D] = (dcu * cg).astype(BF)
        dcw_ref[0:1, :] += jnp.sum(dcv * m2, axis=0, keepdims=True)
        dcw_ref[1:2, :] += jnp.sum(dcv * m1, axis=0, keepdims=True)
        dcw_ref[2:3, :] += jnp.sum(dcv * cu, axis=0, keepdims=True)

    rowspec = pl.BlockSpec((tm, D), _row)
    cw = pl.BlockSpec((8, D), _const2)
    return pl.pallas_call(
        body, name=name, grid=(nt,),
        in_specs=[rowspec, _halo_next(hb, nblk), _col(tm, O_BG), _halo_next(hb, nblk, O_BG),
                  _col(tm, O_CG), _col(tm, O_U), _halo_prev(hb, O_CG), _halo_prev(hb, O_U), cw,
                  pl.BlockSpec(memory_space=pl.ANY)],
        out_specs=[pl.BlockSpec((tm, 3 * D), _row), cw],
        out_shape=[jax.ShapeDtypeStruct(dproj.shape, BF), jax.ShapeDtypeStruct((8, D), F32)],
        input_output_aliases={9: 0},
        compiler_params=_cp(("arbitrary",)),
    )(dgc, dgc, proj, proj, proj, proj, proj, proj, convw, dproj)


def _adam_math(w, g, m, v):
    nm = ADAM_B1 * m + (1.0 - ADAM_B1) * g
    nv = ADAM_B2 * v + (1.0 - ADAM_B2) * (g * g)
    m_hat = nm / (1.0 - ADAM_B1 ** ADAM_STEP)
    v_hat = nv / (1.0 - ADAM_B2 ** ADAM_STEP)
    return -ADAM_LR * (m_hat / (jnp.sqrt(v_hat) + ADAM_EPS) + ADAM_WD * w), nm, nv


SMALL = ("b_ada", "g_ffn1", "g_mix", "g_ffn2", "g_final", "conv_w", "sinks")


def _adam_small(gsum, conv_g, w, m, v, *, name):
    nsm = len(SMALL)

    def body(*refs):
        gs_ref, cg_ref = refs[0], refs[1]
        w_refs, m_refs, v_refs = (refs[2 + k * nsm:2 + (k + 1) * nsm] for k in range(3))
        outs = refs[2 + 3 * nsm:]
        for p, n in enumerate(SMALL):
            if n == "b_ada":
                pieces = [(slice(None), slice(r * D, (r + 1) * D), gs_ref[R_MODS + r:R_MODS + r + 1, :])
                          for r in range(N_MOD)]
            elif n == "conv_w":
                pieces = [(slice(None), slice(None), cg_ref[...])]
            elif n == "sinks":
                pieces = [(slice(None), slice(None), gs_ref[R_SINK:R_SINK + 1, 0:N_HEADS])]
            else:
                row = dict(g_ffn1=R_G1, g_mix=R_GM, g_ffn2=R_G2, g_final=R_GF)[n]
                pieces = [(slice(None), slice(None), gs_ref[row:row + 1, :])]
            for rs, cs, g in pieces:
                d, nm, nv = _adam_math(w_refs[p][rs, cs], g, m_refs[p][rs, cs], v_refs[p][rs, cs])
                for k, val in enumerate((g, d, nm, nv)):
                    outs[k * nsm + p][rs, cs] = val

    args = [gsum, conv_g] + [d[n] for d in (w, m, v) for n in SMALL]
    shapes = [jax.ShapeDtypeStruct(w[n].shape, F32) for _ in range(4) for n in SMALL]
    res = pl.pallas_call(body, name=name, out_shape=shapes, compiler_params=_cp())(*args)
    return [dict(zip(SMALL, res[k * nsm:(k + 1) * nsm])) for k in range(4)]


def _adam(w, g, m, v, *, tm, name):
    _, R, C = w.shape
    tm = _tile(R, tm)
    parts = g.ndim == 3

    def body(w_ref, g_ref, m_ref, v_ref, go_ref, d_ref, nm_ref, nv_ref):
        if parts:
            gv = g_ref[0].astype(F32)
            for s in range(1, N_DEV):
                gv = gv + g_ref[s].astype(F32)
        else:
            gv = g_ref[...]
        go_ref[0] = gv
        d_ref[0], nm_ref[0], nv_ref[0] = _adam_math(w_ref[0], gv, m_ref[0], v_ref[0])

    spec = pl.BlockSpec((1, tm, C), lambda i: (0, i, 0))
    gspec = pl.BlockSpec((N_DEV, tm, C), lambda i: (0, i, 0)) if parts else pl.BlockSpec((tm, C), _row)
    return pl.pallas_call(
        body, name=name, grid=(R // tm,),
        in_specs=[spec, gspec, spec, spec], out_specs=[spec] * 4,
        out_shape=[jax.ShapeDtypeStruct((1, R, C), F32)] * 4,
        compiler_params=_cp(("parallel",)),
    )(w, g, m, v)


def _mods_part(c_all, w_ada, b_ada, *, name):
    C = w_ada.shape[1]

    def body(c_ref, w_ref, b_ref, o_ref):
        cv = c_ref[...]
        ca = cv * jax.nn.sigmoid(cv)
        o_ref[...] = jnp.dot(ca, w_ref[...], preferred_element_type=F32,
                             precision=lax.Precision.HIGHEST) + b_ref[...]

    return pl.pallas_call(
        body, name=name,
        out_shape=jax.ShapeDtypeStruct((N_DEV, C), F32),
        compiler_params=_cp(),
    )(c_all, w_ada, b_ada)


def _wada_grad(c_all_t, gm, *, name):
    C = gm.shape[1]

    def body(c_ref, g_ref, o_ref):
        cv = c_ref[...]
        ca = cv * jax.nn.sigmoid(cv)
        acc = ca[:, 0:1] * g_ref[0:1, :]
        for b in range(1, N_DEV):
            acc = acc + ca[:, b:b + 1] * g_ref[b:b + 1, :]
        o_ref[...] = acc

    return pl.pallas_call(
        body, name=name,
        out_shape=jax.ShapeDtypeStruct((D, C), F32),
        compiler_params=_cp(),
    )(c_all_t, gm)


def _peer(x, y, c, d):
    px = lax.rem(x + ((d >> 2) & 1), 2)
    py = lax.rem(y + ((d >> 1) & 1), 2)
    pc = lax.rem(c + (d & 1), 2)
    return (px, py, pc), 4 * px + 2 * py + pc


def _exchange(xs, *, scatter, name):
    n = len(xs)
    nsem = n * (N_DEV - 1)

    def body(*refs):
        ins, outs = refs[:n], refs[n:2 * n]
        token, send_sems, recv_sems, local_sems = refs[2 * n:]
        x, y, c = lax.axis_index("x"), lax.axis_index("y"), lax.axis_index("c")
        me = 4 * x + 2 * y + c
        token[...] = jnp.zeros_like(token)

        def src(t, idx):
            return ins[t].at[idx] if scatter else ins[t]

        local = [pltpu.make_async_copy(src(t, me), outs[t].at[me], local_sems.at[t]) for t in range(n)]
        for cp in local:
            cp.start()
        remote = []
        for t in range(n):
            for d in range(1, N_DEV):
                peer, pidx = _peer(x, y, c, d)
                k = t * (N_DEV - 1) + d - 1
                send = pltpu.make_async_remote_copy(src_ref=src(t, pidx), dst_ref=outs[t].at[me],
                                                    send_sem=send_sems.at[k], recv_sem=recv_sems.at[k],
                                                    device_id=peer, device_id_type=MESH)
                recv = pltpu.make_async_remote_copy(src_ref=src(t, pidx), dst_ref=outs[t].at[pidx],
                                                    send_sem=send_sems.at[k], recv_sem=recv_sems.at[k],
                                                    device_id=peer, device_id_type=MESH)
                send.start()
                remote.append((send, recv))
        for cp in local:
            cp.wait()
        for send, recv in remote:
            send.wait_send()
            recv.wait_recv()

    anyspec = pl.BlockSpec(memory_space=pl.ANY)
    out_shape = [jax.ShapeDtypeStruct(a.shape if scatter else (N_DEV,) + a.shape, a.dtype) for a in xs]
    out_shape.append(jax.ShapeDtypeStruct((8, 128), F32))
    return pl.pallas_call(
        body, name=name,
        in_specs=[anyspec] * n, out_specs=[anyspec] * n + [pl.BlockSpec(memory_space=pltpu.VMEM)],
        out_shape=out_shape,
        scratch_shapes=[pltpu.SemaphoreType.DMA((nsem,)), pltpu.SemaphoreType.DMA((nsem,)),
                        pltpu.SemaphoreType.DMA((n,))],
    )(*xs)


def _sum8(parts, *, name):
    _, R, C = parts.shape

    def body(p_ref, o_ref):
        acc = p_ref[0]
        for s in range(1, N_DEV):
            acc = acc + p_ref[s]
        o_ref[...] = acc

    return pl.pallas_call(body, name=name, out_shape=jax.ShapeDtypeStruct((R, C), F32),
                          compiler_params=_cp())(parts)


HBM_SPEC = pl.BlockSpec(memory_space=pltpu.HBM)
SEM_SPEC = pl.BlockSpec(memory_space=pltpu.SEMAPHORE)
N_PEER = N_DEV - 1


def _split_copies(src_refs, land_refs, send_sems, recv_sems, scatter):
    x, y, c = lax.axis_index("x"), lax.axis_index("y"), lax.axis_index("c")
    me = 4 * x + 2 * y + c
    pairs = []
    for j, (src, land) in enumerate(zip(src_refs, land_refs)):
        for d in range(1, N_DEV):
            peer, pidx = _peer(x, y, c, d)
            k = j * N_PEER + d - 1
            s = src.at[pidx] if scatter else src
            send = pltpu.make_async_remote_copy(src_ref=s, dst_ref=land.at[me], send_sem=send_sems.at[k],
                                                recv_sem=recv_sems.at[k], device_id=peer, device_id_type=MESH)
            recv = pltpu.make_async_remote_copy(src_ref=s, dst_ref=land.at[pidx], send_sem=send_sems.at[k],
                                                recv_sem=recv_sems.at[k], device_id=peer, device_id_type=MESH)
            pairs.append((send, recv))
    return pairs


def _own_slot(block, me):
    land = lax.empty((N_DEV,) + block.shape, block.dtype)
    return lax.dynamic_update_slice(land, block[None], (me, 0, 0))


def _split_start(srcs, lands, groups, *, scatter, name):
    n, ng = len(srcs), len(groups)

    def body(*refs):
        src_refs, land_refs = refs[:n], refs[n:2 * n]
        sems = refs[2 * n:2 * n + 2 * ng]
        token = refs[-1]
        for gi, g in enumerate(groups):
            pairs = _split_copies([src_refs[t] for t in g], [land_refs[t] for t in g], sems[2 * gi],
                                  sems[2 * gi + 1], scatter)
            for send, _ in pairs:
                send.start()
        token[...] = jnp.zeros_like(token)

    sem_shapes = []
    for g in groups:
        sem_shapes += [pltpu.SemaphoreType.DMA((len(g) * N_PEER,))] * 2
    thru = [pltpu.HBM(a.shape, a.dtype) for a in list(srcs) + list(lands)]
    outs = pl.pallas_call(
        body, name=name,
        out_shape=tuple(sem_shapes + thru + [jax.ShapeDtypeStruct((8, 128), F32)]),
        in_specs=[HBM_SPEC] * (2 * n),
        out_specs=tuple([SEM_SPEC] * (2 * ng) + [HBM_SPEC] * (2 * n) + [pl.BlockSpec(memory_space=pltpu.VMEM)]),
        input_output_aliases={i: 2 * ng + i for i in range(2 * n)},
        compiler_params=pltpu.CompilerParams(has_side_effects=pltpu.SideEffectType.DATAFLOW_SIDE_EFFECTING),
    )(*[pltpu.with_memory_space_constraint(a, pltpu.HBM) for a in list(srcs) + list(lands)])
    sems = [(outs[2 * gi], outs[2 * gi + 1]) for gi in range(ng)]
    return sems, outs[2 * ng:2 * ng + n], outs[2 * ng + n:2 * ng + 2 * n], outs[-1]


def _behind(v, token):
    if token is None:
        return v
    return v + token[0, 0].astype(v.dtype)


def _split_wait(srcs, lands, sems, after, *, scatter, name):
    m = len(srcs)

    def body(*refs):
        src_refs, land_refs = refs[:m], refs[m:2 * m]
        send_sems, recv_sems = refs[2 * m], refs[2 * m + 1]
        for send, recv in _split_copies(src_refs, land_refs, send_sems, recv_sems, scatter):
            send.wait_send()
            recv.wait_recv()

    outs = pl.pallas_call(
        body, name=name,
        out_shape=tuple(pltpu.HBM(a.shape, a.dtype) for a in list(srcs) + list(lands)),
        in_specs=[HBM_SPEC] * (2 * m) + [SEM_SPEC, SEM_SPEC, pl.BlockSpec(memory_space=pl.ANY)],
        out_specs=tuple([HBM_SPEC] * (2 * m)),
        input_output_aliases={i: i for i in range(2 * m)},
        compiler_params=pltpu.CompilerParams(has_side_effects=pltpu.SideEffectType.DATAFLOW_SIDE_EFFECTING),
    )(*srcs, *lands, sems[0], sems[1], after)
    return outs[m:]


TL_FIRST = (1, 2, 4, 6)
TL_ICI = (2, 4, 6)
EFFECT = pltpu.SideEffectType.DATAFLOW_SIDE_EFFECTING


def _tl_first(src_refs, land_refs, send_sems, recv_sems):
    x, y, c = lax.axis_index("x"), lax.axis_index("y"), lax.axis_index("c")
    me = 4 * x + 2 * y + c
    out = []
    for j, (src, land) in enumerate(zip(src_refs, land_refs)):
        for i, d in enumerate(TL_FIRST):
            peer, pidx = _peer(x, y, c, d)
            k = len(TL_FIRST) * j + i
            send = pltpu.make_async_remote_copy(src_ref=src, dst_ref=land.at[me], send_sem=send_sems.at[k],
                                                recv_sem=recv_sems.at[k], device_id=peer, device_id_type=MESH)
            recv = pltpu.make_async_remote_copy(src_ref=src, dst_ref=land.at[pidx], send_sem=send_sems.at[k],
                                                recv_sem=recv_sems.at[k], device_id=peer, device_id_type=MESH)
            out.append((d, send, recv))
    return out


def _tl_second(land_refs, send_sems, recv_sems):
    x, y, c = lax.axis_index("x"), lax.axis_index("y"), lax.axis_index("c")
    sibling, _ = _peer(x, y, c, 1)
    out = []
    for j, land in enumerate(land_refs):
        for i, d in enumerate(TL_ICI):
            _, mine = _peer(x, y, c, d)
            _, theirs = _peer(x, y, c, d + 1)
            k = len(TL_ICI) * j + i
            send = pltpu.make_async_remote_copy(src_ref=land.at[mine], dst_ref=land.at[mine], send_sem=send_sems.at[k],
                                                recv_sem=recv_sems.at[k], device_id=sibling, device_id_type=MESH)
            recv = pltpu.make_async_remote_copy(src_ref=land.at[mine], dst_ref=land.at[theirs],
                                                send_sem=send_sems.at[k], recv_sem=recv_sems.at[k],
                                                device_id=sibling, device_id_type=MESH)
            out.append((send, recv))
    return out


def _tl_start(srcs, lands, groups, *, name):
    n, ng = len(srcs), len(groups)

    def body(*refs):
        src_refs, land_refs = refs[:n], refs[n:2 * n]
        sems = refs[2 * n:2 * n + 2 * ng]
        for gi, g in enumerate(groups):
            for _, send, _ in _tl_first([src_refs[t] for t in g], [land_refs[t] for t in g], sems[2 * gi],
                                        sems[2 * gi + 1]):
                send.start()
        refs[-1][...] = jnp.zeros_like(refs[-1])

    sem_shapes = []
    for g in groups:
        sem_shapes += [pltpu.SemaphoreType.DMA((len(g) * len(TL_FIRST),))] * 2
    thru = [pltpu.HBM(a.shape, a.dtype) for a in list(srcs) + list(lands)]
    outs = pl.pallas_call(
        body, name=name,
        out_shape=tuple(sem_shapes + thru + [jax.ShapeDtypeStruct((8, 128), F32)]),
        in_specs=[HBM_SPEC] * (2 * n),
        out_specs=tuple([SEM_SPEC] * (2 * ng) + [HBM_SPEC] * (2 * n) + [pl.BlockSpec(memory_space=pltpu.VMEM)]),
        input_output_aliases={i: 2 * ng + i for i in range(2 * n)},
        compiler_params=pltpu.CompilerParams(has_side_effects=EFFECT),
    )(*[pltpu.with_memory_space_constraint(a, pltpu.HBM) for a in list(srcs) + list(lands)])
    sems = [(outs[2 * gi], outs[2 * gi + 1]) for gi in range(ng)]
    return sems, outs[2 * ng:2 * ng + n], outs[2 * ng + n:2 * ng + 2 * n], outs[-1]


def _tl_forward(srcs, lands, sems1, after, *, name):
    m = len(srcs)

    def body(*refs):
        src_refs, land_refs = refs[:m], refs[m:2 * m]
        send1, recv1 = refs[2 * m], refs[2 * m + 1]
        send2, recv2 = refs[2 * m + 3], refs[2 * m + 4]
        for d, _, recv in _tl_first(src_refs, land_refs, send1, recv1):
            if d in TL_ICI:
                recv.wait_recv()
        for send, _ in _tl_second(land_refs, send2, recv2):
            send.start()

    sem = pltpu.SemaphoreType.DMA((m * len(TL_ICI),))
    outs = pl.pallas_call(
        body, name=name,
        out_shape=tuple([sem, sem] + [pltpu.HBM(a.shape, a.dtype) for a in list(srcs) + list(lands)]),
        in_specs=[HBM_SPEC] * (2 * m) + [SEM_SPEC, SEM_SPEC, pl.BlockSpec(memory_space=pl.ANY)],
        out_specs=tuple([SEM_SPEC, SEM_SPEC] + [HBM_SPEC] * (2 * m)),
        input_output_aliases={i: 2 + i for i in range(2 * m)},
        compiler_params=pltpu.CompilerParams(has_side_effects=EFFECT),
    )(*srcs, *lands, sems1[0], sems1[1], after)
    return (outs[0], outs[1]), outs[2:2 + m], outs[2 + m:2 + 2 * m]


def _tl_wait(srcs, lands, sems1, sems2, after, *, name):
    m = len(srcs)

    def body(*refs):
        src_refs, land_refs = refs[:m], refs[m:2 * m]
        send1, recv1, send2, recv2 = refs[2 * m:2 * m + 4]
        for d, send, recv in _tl_first(src_refs, land_refs, send1, recv1):
            send.wait_send()
            if d not in TL_ICI:
                recv.wait_recv()
        for send, recv in _tl_second(land_refs, send2, recv2):
            send.wait_send()
            recv.wait_recv()

    outs = pl.pallas_call(
        body, name=name,
        out_shape=tuple(pltpu.HBM(a.shape, a.dtype) for a in list(srcs) + list(lands)),
        in_specs=[HBM_SPEC] * (2 * m) + [SEM_SPEC] * 4 + [pl.BlockSpec(memory_space=pl.ANY)],
        out_specs=tuple([HBM_SPEC] * (2 * m)),
        input_output_aliases={i: i for i in range(2 * m)},
        compiler_params=pltpu.CompilerParams(has_side_effects=EFFECT),
    )(*srcs, *lands, sems1[0], sems1[1], sems2[0], sems2[1], after)
    return outs[m:]


TM_PROJ = 512
TN_PROJ = 512
TM_ROW = 512
TM_NN = 512
TK_TN = 2048
TM_ADAM = 416
TN_FFN = F // 2
TN_IN = NIN // 4


def _tn(a, b, name, tn, token=None):
    if a.ndim == 2:
        a = a[None]
    return _tn_matmul(a, b, token, tn=tn, tk=TK_TN, name=name)


def _local_step(x, tgt, mods, g1, gm, g2, gf, convw8, sinks, w_get, g_put, tables=None):
    T = x.shape[0]
    sh1, sc1, gt1, sh2, sc2, gt2, sh3, sc3, gt3 = [mods[i:i + 1] for i in range(N_MOD)]
    cos, sin = _rope_tables(T) if tables is None else tables
    behind = _behind

    w = dict(w_get("gu1", mods))
    h1, ab1 = _norm_proj(x, g1, sc1, sh1, w["gu1"], tm=TM_PROJ, tn=TN_PROJ, name="ffn1_up")
    w.update(w_get("d1", ab1))
    x1, y1 = _ffn_down_fwd(ab1, w["d1"], x, gt1, tm=TM_ROW, name="ffn1_down")
    w.update(w_get("mix", x1))
    h2, proj, qs, kr = _norm_proj(x1, gm, sc2, sh2, w["win"], (cos, sin), tm=TM_PROJ, tn=TN_PROJ, name="mix_in")
    bias = _attn_bias()
    attn, lse = _attn_fwd(qs, kr, proj, bias, sinks, name="attn_fwd")
    x2, gc, yc, ya, mg, o = _mixer_mid_fwd(proj, attn, w["cp"], w["ap"], w["out"], convw8, x1, gt2,
                                           tm=TM_ROW, name="mix_mid")
    w.update(w_get("ffn2", x2))
    h3, ab2, y2, dx3, lsum, dgf = _ffn_fwd(x2, g2, sc3, sh3, gt3, w["gu2"], w["d2"], (tgt, gf), tm=TM_ROW,
                                           name="ffn2_final")

    dab2, dgt3, g_d2 = _ffn_down_bwd_dw(dx3, y2, gt3, ab2, w["d2"], tm=TM_ROW, name="ffn2_down_bwd")
    dx2, dsh3, dsc3, dg2 = _nn_bwd_norm(dab2, w["gu2"], x2, g2, sc3, dx3, tm=TM_NN, name="ffn2_up_bwd")
    g_gu2 = _tn(dab2, h3, "ffn2_up_dw", TN_FFN)
    tok = g_put(dict(gu2=g_gu2, d2=g_d2))

    dout, dyc, dya, dgc, dat, dproj, dgt2 = _mixer_mid_bwd(dx2, behind(gt2, tok), o, proj, yc, ya, w["out"], w["cp"],
                                                           w["ap"], tm=TM_ROW, name="mix_mid_bwd")
    g_out = _tn(mg, dout, "mix_out_dw", D)
    g_cp = _tn(gc, dyc, "mix_cp_dw", D)
    g_ap = _tn(attn, dya, "mix_ap_dw", D)
    dproj, dkc, dkp, dvc, dvp, dsink = _attn_bwd(qs, kr, proj, bias, sinks, lse, attn, dat, cos, sin, dproj,
                                                 name="attn_bwd")
    dproj = _dkv_combine(dkc, dkp, dvc, dvp, dproj, name="attn_dkv")
    dproj, dcw = _conv_bwd(dgc, proj, convw8, dproj, tm=TM_ROW, name="conv_bwd")
    g_in = _tn(dproj, h2, "mix_in_dw", TN_IN)
    tok = g_put(dict(win=g_in, cp=g_cp, ap=g_ap, out=g_out))
    dx1, dsh2, dsc2, dgm = _nn_bwd_norm(dproj[None], w["win"], x1, gm, behind(sc2, tok), dx2, tm=TM_NN,
                                        name="mix_in_bwd")

    dab1, dgt1, g_d1 = _ffn_down_bwd_dw(dx1, y1, gt1, ab1, w["d1"], tm=TM_ROW, name="ffn1_down_bwd")
    tok = g_put(dict(d1=g_d1))
    g_gu1 = _tn(dab1, h1, "ffn1_up_dw", TN_FFN, tok)
    tok = g_put(dict(gu1=g_gu1))
    dx0, dsh1, dsc1, dg1 = _nn_bwd_norm(dab1, w["gu1"], x, g1, behind(sc1, tok), dx1, tm=TM_NN,
                                        name="ffn1_up_bwd")

    small = dict(mods=jnp.concatenate([dsh1, dsc1, dgt1, dsh2, dsc2, dgt2, dsh3, dsc3, dgt3], axis=0),
                 g1=dg1, gm=dgm, g2=dg2, gf=dgf, convw=dcw[0:3], sinks=dsink[:, 0:N_HEADS])
    return lsum, dx0, small


BIG = ("gu1", "d1", "win", "cp", "ap", "out", "gu2", "d2")
TRANSPOSED = ("gu1", "win", "gu2")
SMALL_ROWS = 24
R_MODS, R_G1, R_GM, R_G2, R_GF, R_CONV, R_SINK = 0, 9, 10, 11, 12, 13, 16


def _pad_to(a, rows, cols):
    return jnp.pad(a, ((0, rows - a.shape[0]), (0, cols - a.shape[1])))


def _pack_small(b_ada, g1, gm, g2, gf, conv, sinks):
    rows = [b_ada.reshape(N_MOD, D), g1.reshape(1, D), gm.reshape(1, D), g2.reshape(1, D), gf.reshape(1, D),
            _pad_to(conv.reshape(3, -1), 3, D), _pad_to(sinks.reshape(1, N_HEADS), 1, D)]
    return _pad_to(jnp.concatenate(rows, axis=0), SMALL_ROWS, D)


def kernel(x, c, w_ada, b_ada, g_ffn1, w1_gu, w1_down, g_mix, w_in, conv_w, w_conv_proj, w_attn_proj, sinks, w_out, g_ffn2, w2_gu, w2_down, g_final, loss_target, m_w_ada, m_b_ada, m_g_ffn1, m_w1_gu, m_w1_down, m_g_mix, m_w_in, m_conv_w, m_w_conv_proj, m_w_attn_proj, m_sinks, m_w_out, m_g_ffn2, m_w2_gu, m_w2_down, m_g_final, v_w_ada, v_b_ada, v_g_ffn1, v_w1_gu, v_w1_down, v_g_mix, v_w_in, v_conv_w, v_w_conv_proj, v_w_attn_proj, v_sinks, v_w_out, v_g_ffn2, v_w2_gu, v_w2_down, v_g_final):
    me = 4 * lax.axis_index("x") + 2 * lax.axis_index("y") + lax.axis_index("c")
    ada_cols = w_ada.shape[2]
    conv_cols = conv_w.shape[2]

    native = dict(gu1=w1_gu[0], d1=w1_down[0], win=w_in[0], cp=w_conv_proj[0], ap=w_attn_proj[0], out=w_out[0],
                  gu2=w2_gu[0], d2=w2_down[0])

    def shard(n, token):
        a = _behind(native[n], token)
        return (a.T if n in TRANSPOSED else a).astype(BF)

    c_all, conv_all, _ = _exchange([c, _pad_to(conv_w[0], 8, conv_cols)], scatter=False, name="gather_cond")
    c_all = c_all.reshape(N_DEV, D)
    conv_full = conv_all[:, 0:3, :].transpose(1, 0, 2).reshape(3, D)

    b_cols = lax.dynamic_slice(b_ada, (0, me * ada_cols), (1, ada_cols))
    mods_cols = _mods_part(c_all, w_ada[0], b_cols, name="ada_mods")
    mods_all, mods_token = _exchange([mods_cols], scatter=False, name="gather_mods")
    mods = lax.dynamic_index_in_dim(mods_all, me, axis=1, keepdims=False).reshape(N_MOD, D)

    groups = dict(gu1=("gu1",), d1=("d1",), mix=("win", "cp", "ap", "out"), ffn2=("gu2", "d2"))
    in_flight = {}
    first = [shard("gu1", mods_token)]
    sems, srcs, lands, token = _tl_start(first, [_own_slot(s, me) for s in first], [[0]],
                                         name="gather_weights_start_gu1")
    in_flight["gu1"] = [sems[0], srcs, lands, None]
    rest = [n for n in BIG if n != "gu1"]
    shards = [shard(n, token) for n in rest]
    rest_groups = [[rest.index(n) for n in names] for g, names in groups.items() if g != "gu1"]
    sems, srcs, lands, rest_token = _tl_start(shards, [_own_slot(s, me) for s in shards], rest_groups,
                                              name="gather_weights_start_rest")
    for (g, names), gsems, idx in zip([kv for kv in groups.items() if kv[0] != "gu1"], sems, rest_groups):
        in_flight[g] = [gsems, [srcs[t] for t in idx], [lands[t] for t in idx], None]

    def forward(group, after):
        sems1, gsrcs, glands, _ = in_flight[group]
        sems2, gsrcs, glands = _tl_forward(gsrcs, glands, sems1, after, name="gather_weights_forward_" + group)
        in_flight[group] = [sems1, gsrcs, glands, sems2]

    forward_early = dict(d1="mix", mix="ffn2")

    tables = _rope_tables(x.shape[1], rest_token)

    def w_get(group, after):
        if group == "gu1":
            after = tables[0]
        if in_flight[group][3] is None:
            forward(group, after)
        sems1, gsrcs, glands, sems2 = in_flight[group]
        landed = _tl_wait(gsrcs, glands, sems1, sems2, after, name="gather_weights_wait_" + group)
        if group in forward_early:
            forward(forward_early[group], landed[0])
        return {n: a.reshape(-1, D) for n, a in zip(groups[group], landed)}

    pending = []

    def g_put(gs):
        names = tuple(gs)
        srcs = [gs[n].reshape(N_DEV, -1, D) for n in names]
        lands = [_own_slot(lax.dynamic_index_in_dim(s, me, axis=0, keepdims=False), me) for s in srcs]
        sems, srcs, lands, tok = _split_start(srcs, lands, [list(range(len(names)))], scatter=True,
                                              name="scatter_grads_start_" + names[0])
        pending.append((names, sems[0], srcs, lands))
        return tok

    lsum, grad_x, small = _local_step(x[0], loss_target[0], mods, g_ffn1, g_mix, g_ffn2, g_final[None],
                                      _pad_to(conv_full, 8, D), sinks[0], w_get, g_put, tables)
    loss = lax.psum((0.5 / D) * jnp.sum(lsum), ("x", "y", "c"))

    packed = _pack_small(small["mods"], small["g1"], small["gm"], small["g2"], small["gf"], small["convw"],
                         small["sinks"])
    packed_all, _ = _exchange([packed], scatter=False, name="gather_small")
    gsmall = _sum8(packed_all, name="sum_small")

    w_of = dict(ada=w_ada, gu1=w1_gu, d1=w1_down, win=w_in, cp=w_conv_proj, ap=w_attn_proj, out=w_out, gu2=w2_gu,
                d2=w2_down)
    m_of = dict(ada=m_w_ada, gu1=m_w1_gu, d1=m_w1_down, win=m_w_in, cp=m_w_conv_proj, ap=m_w_attn_proj, out=m_w_out,
                gu2=m_w2_gu, d2=m_w2_down)
    v_of = dict(ada=v_w_ada, gu1=v_w1_gu, d1=v_w1_down, win=v_w_in, cp=v_w_conv_proj, ap=v_w_attn_proj, out=v_w_out,
                gu2=v_w2_gu, d2=v_w2_down)
    upd = {}
    after = gsmall
    for names, sems, srcs, lands in pending:
        parts = _split_wait(srcs, lands, sems, after, scatter=True, name="scatter_grads_wait_" + names[0])
        for n, p in zip(names, parts):
            if n in TRANSPOSED:
                res = _adam(jnp.swapaxes(w_of[n], 1, 2), p, jnp.swapaxes(m_of[n], 1, 2), jnp.swapaxes(v_of[n], 1, 2),
                            tm=TM_ADAM, name="adam_" + n)
                upd[n] = [jnp.swapaxes(t, 1, 2) for t in res]
            else:
                upd[n] = _adam(w_of[n], p, m_of[n], v_of[n], tm=TM_ADAM, name="adam_" + n)
        after = upd[names[-1]][1]

    gm_cols = lax.dynamic_slice(packed_all[:, R_MODS:R_MODS + N_MOD, :].reshape(N_DEV, N_MOD * D),
                                (0, me * ada_cols), (N_DEV, ada_cols))
    upd["ada"] = _adam(w_ada, _wada_grad(c_all.T, gm_cols, name="ada_dw"), m_w_ada, v_w_ada, tm=256, name="adam_ada")
    conv_g = lax.dynamic_slice(gsmall, (R_CONV, me * conv_cols), (3, conv_cols))

    def natural(b, g1, gm, g2, gf, cw, sk):
        return dict(b_ada=b, g_ffn1=g1, g_mix=gm, g_ffn2=g2, g_final=gf[None], conv_w=cw[0], sinks=sk)

    small_out = _adam_small(gsmall, conv_g, natural(b_ada, g_ffn1, g_mix, g_ffn2, g_final, conv_w, sinks),
                            natural(m_b_ada, m_g_ffn1, m_g_mix, m_g_ffn2, m_g_final, m_conv_w, m_sinks),
                            natural(v_b_ada, v_g_ffn1, v_g_mix, v_g_ffn2, v_g_final, v_conv_w, v_sinks),
                            name="adam_small")
    for res in small_out:
        res["g_final"] = res["g_final"][0]
        res["conv_w"] = res["conv_w"][None]

    big_name = dict(w_ada="ada", w1_gu="gu1", w1_down="d1", w_in="win", w_conv_proj="cp", w_attn_proj="ap",
                    w_out="out", w2_gu="gu2", w2_down="d2")
    order = ("w_ada", "b_ada", "g_ffn1", "w1_gu", "w1_down", "g_mix", "w_in", "conv_w", "w_conv_proj", "w_attn_proj",
             "sinks", "w_out", "g_ffn2", "w2_gu", "w2_down", "g_final")
    outs = [loss, grad_x[None]]
    for kind in range(4):
        for n in order:
            outs.append(upd[big_name[n]][kind] if n in big_name else small_out[kind][n])
    return tuple(outs)
```

```python
import jax
import jax.numpy as jnp
from jax import lax
from jax.experimental import pallas as pl
from jax.experimental.pallas import tpu as pltpu

D = 1024
F = 2816
NIN = 6656
N_HEADS = 16
N_KV = 4
HEAD_DIM = 64
BLK = 128
N_MOD = 9
N_DEV = 8
EPS = 1e-6
NEG_INF = -1e30
ROPE_THETA = 10000.0
O_BG, O_CG, O_U, O_Q, O_K, O_V, O_ZC, O_ZA = 0, 1024, 2048, 3072, 4096, 4352, 4608, 5632

ADAM_LR = 0.001
ADAM_B1 = 0.9
ADAM_B2 = 0.999
ADAM_EPS = 1e-08
ADAM_WD = 0.01
ADAM_STEP = 10

BF = jnp.bfloat16
F32 = jnp.float32
VMEM_LIMIT = 56 * 1024 * 1024
MXU_N = 256
MESH = pl.DeviceIdType.MESH

NT = (((1,), (1,)), ((), ()))
TN = (((0,), (0,)), ((), ()))


def _cp(sem=None):
    return pltpu.CompilerParams(dimension_semantics=sem, vmem_limit_bytes=VMEM_LIMIT)


def _tile(n, pref):
    if n <= pref:
        return n
    for t in range(pref - pref % 16, 15, -16):
        if n % t == 0:
            return t
    raise ValueError((n, pref))


def _sigmoid(v):
    return 0.5 * jnp.tanh(0.5 * v) + 0.5


def _row(i):
    return (i, 0)


def _const2(*_):
    return (0, 0)


def _resident(shape):
    return pl.BlockSpec(shape, lambda *_: (0,) * len(shape), pipeline_mode=pl.Buffered(1))


def _norm_proj(x, g, sc, sh, wt, rope=None, *, tm, tn, name):
    T, N = x.shape[0], wt.shape[0]
    tm = _tile(T, tm)

    def body(x_ref, g_ref, sc_ref, sh_ref, w_ref, *rest):
        if rope is None:
            h_ref, o_ref = rest
        else:
            c_ref, s_ref, h_ref, o_ref, qs_ref, kr_ref = rest
        xv = x_ref[...]
        r = lax.rsqrt(jnp.mean(xv * xv, axis=-1, keepdims=True) + EPS)
        hb = ((xv * r) * g_ref[...] * (1.0 + sc_ref[...]) + sh_ref[...]).astype(BF)
        h_ref[...] = hb
        for c0 in range(0, N, tn):
            cols = pl.ds(c0, tn)
            o_ref[:, cols] = lax.dot_general(hb, w_ref[cols, :], NT, preferred_element_type=F32).astype(BF)
            if rope is not None and c0 < O_V <= c0 + tn:
                _attn_prep_tile(o_ref, c_ref, s_ref, qs_ref, kr_ref, tm)

    vec = pl.BlockSpec((1, D), _const2)
    rowspec = pl.BlockSpec((tm, D), _row)
    in_specs = [rowspec, vec, vec, vec, _resident((N, D))]
    out_specs = [rowspec, pl.BlockSpec((tm, N), _row)]
    out_shape = [jax.ShapeDtypeStruct((T, D), BF), jax.ShapeDtypeStruct((T, N), BF)]
    args = [x, g, sc, sh, wt]
    if rope is not None:
        in_specs += [pl.BlockSpec((tm, 128), _row)] * 2
        out_specs += [pl.BlockSpec((N_KV, 4 * tm, 128), lambda i: (0, i, 0)), pl.BlockSpec((tm, 256), _row)]
        out_shape += [jax.ShapeDtypeStruct((N_KV, 4 * T, 128), BF), jax.ShapeDtypeStruct((T, 256), BF)]
        args += list(rope)
    return pl.pallas_call(
        body, name=name, grid=(T // tm,),
        in_specs=in_specs, out_specs=out_specs, out_shape=out_shape,
        compiler_params=_cp(("parallel",)),
    )(*args)


def _ffn_down_fwd(ab, wd, x, gt, *, tm, name):
    T = x.shape[0]
    tm = _tile(T, tm)

    def body(a_ref, b_ref, wd_ref, x_ref, gt_ref, xo_ref, y_ref):
        y = None
        for c0 in range(0, F, MXU_N):
            cols = pl.ds(c0, MXU_N)
            a = a_ref[:, cols].astype(F32)
            act = (a * _sigmoid(a) * b_ref[:, cols].astype(F32)).astype(BF)
            part = jnp.dot(act, wd_ref[cols, :], preferred_element_type=F32)
            y = part if y is None else y + part
        y_ref[...] = y.astype(BF)
        xo_ref[...] = x_ref[...] + (0.5 * gt_ref[...]) * y

    return pl.pallas_call(
        body, name=name, grid=(T // tm,),
        in_specs=[pl.BlockSpec((tm, F), lambda i: (i, 0)), pl.BlockSpec((tm, F), lambda i: (i, 1)),
                  _resident((F, D)), pl.BlockSpec((tm, D), _row), pl.BlockSpec((1, D), _const2)],
        out_specs=[pl.BlockSpec((tm, D), _row), pl.BlockSpec((tm, D), _row)],
        out_shape=[jax.ShapeDtypeStruct((T, D), F32), jax.ShapeDtypeStruct((T, D), BF)],
        compiler_params=_cp(("parallel",)),
    )(ab, ab, wd, x, gt)


def _ffn_fwd(x, g, sc, sh, gt, wgu, wd, final, *, tm, name):
    T = x.shape[0]
    tm = _tile(T, tm)
    last = final is not None

    def body(x_ref, g_ref, sc_ref, sh_ref, gt_ref, wgu_ref, wd_ref, *rest):
        if last:
            t_ref, gf_ref, h_ref, ab_ref, y_ref, dx_ref, ls_ref, dgf_ref = rest
        else:
            h_ref, ab_ref, y_ref, xo_ref = rest
        xv = x_ref[...]
        r = lax.rsqrt(jnp.mean(xv * xv, axis=-1, keepdims=True) + EPS)
        hb = ((xv * r) * g_ref[...] * (1.0 + sc_ref[...]) + sh_ref[...]).astype(BF)
        h_ref[...] = hb
        y = None
        for c0 in range(0, F, MXU_N):
            a = lax.dot_general(hb, wgu_ref[pl.ds(c0, MXU_N), :], NT, preferred_element_type=F32)
            b = lax.dot_general(hb, wgu_ref[pl.ds(F + c0, MXU_N), :], NT, preferred_element_type=F32)
            ab = a.astype(BF)
            bb = b.astype(BF)
            ab_ref[:, pl.ds(c0, MXU_N)] = ab
            ab_ref[:, pl.ds(F + c0, MXU_N)] = bb
            a = ab.astype(F32)
            act = (a * _sigmoid(a) * bb.astype(F32)).astype(BF)
            part = jnp.dot(act, wd_ref[pl.ds(c0, MXU_N), :], preferred_element_type=F32)
            y = part if y is None else y + part
        y_ref[...] = y.astype(BF)
        xo = xv + (0.5 * gt_ref[...]) * y
        if not last:
            xo_ref[...] = xo
            return

        @pl.when(pl.program_id(0) == 0)
        def _():
            ls_ref[...] = jnp.zeros_like(ls_ref)
            dgf_ref[...] = jnp.zeros_like(dgf_ref)
        gv = gf_ref[...]
        r = lax.rsqrt(jnp.mean(xo * xo, axis=-1, keepdims=True) + EPS)
        xh = xo * r
        e = xh * gv - t_ref[...]
        ls_ref[...] += jnp.sum(e * e, axis=0, keepdims=True)
        dy = e * (1.0 / D)
        dgf_ref[...] += jnp.sum(dy * xh, axis=0, keepdims=True)
        dxh = dy * gv
        dx_ref[...] = r * (dxh - xh * jnp.mean(dxh * xh, axis=-1, keepdims=True))

    vec = pl.BlockSpec((1, D), _const2)
    rowspec = pl.BlockSpec((tm, D), _row)
    in_specs = [rowspec, vec, vec, vec, vec, _resident((2 * F, D)), _resident((F, D))]
    out_specs = [rowspec, pl.BlockSpec((tm, 2 * F), _row), rowspec, rowspec]
    out_shape = [jax.ShapeDtypeStruct((T, D), BF), jax.ShapeDtypeStruct((T, 2 * F), BF),
                 jax.ShapeDtypeStruct((T, D), BF), jax.ShapeDtypeStruct((T, D), F32)]
    args = [x, g, sc, sh, gt, wgu, wd]
    if last:
        in_specs += [rowspec, vec]
        out_specs += [vec, vec]
        out_shape += [jax.ShapeDtypeStruct((1, D), F32)] * 2
        args += list(final)
    return pl.pallas_call(
        body, name=name, grid=(T // tm,),
        in_specs=in_specs, out_specs=out_specs, out_shape=out_shape,
        compiler_params=_cp(("arbitrary",) if last else ("parallel",)),
    )(*args)


def _ffn_down_bwd_dw(dxo, y, gt, ab, wd, *, tm, name):
    T = dxo.shape[0]
    tm = _tile(T, tm)
    nt = T // tm
    hw = F // 2
    chunks = [(c0, min(MXU_N, hw - c0)) for c0 in range(0, hw, MXU_N)]

    def body(dxo_ref, y_ref, gt_ref, a_ref, b_ref, wd_ref, dab_ref, dgt_ref, dwd_ref, dys, dyt, acc, stage, sem):
        i, j = pl.program_id(0), pl.program_id(1)

        @pl.when(jnp.logical_and(i == 0, j == 0))
        def _():
            dgt_ref[...] = jnp.zeros_like(dgt_ref)

        @pl.when(j == 0)
        def _():
            dxv = dxo_ref[...]
            dgt_ref[...] += 0.5 * jnp.sum(dxv * y_ref[...].astype(F32), axis=0, keepdims=True)
            dyf = (0.5 * gt_ref[...]) * dxv
            dys[...] = dyf.astype(BF)
            dyt[...] = dyf.T.astype(BF)

        def half(jj):
            @pl.when(i == 0)
            def _():
                acc[jj] = jnp.zeros((D, hw), F32)

            dy = dys[...]
            dy_t = dyt[...]
            for c0, cw in chunks:
                cols = pl.ds(c0, cw)
                dact = lax.dot_general(dy, wd_ref[pl.ds(jj * hw + c0, cw), :], NT, preferred_element_type=F32)
                a = a_ref[:, cols].astype(F32)
                b = b_ref[:, cols].astype(F32)
                s = _sigmoid(a)
                silu = a * s
                dab_ref[0, :, cols] = (dact * b * (s * (1.0 + a * (1.0 - s)))).astype(BF)
                dab_ref[1, :, cols] = (dact * silu).astype(BF)
                acc[jj, :, cols] += jnp.dot(dy_t, (silu * b).astype(BF), preferred_element_type=F32)

            @pl.when(i == nt - 1)
            def _():
                for c0, cw in chunks:
                    stage[0:cw, :] = acc[jj, :, pl.ds(c0, cw)].T.astype(BF)
                    out = pltpu.make_async_copy(stage.at[pl.ds(0, cw)], dwd_ref.at[pl.ds(jj * hw + c0, cw)], sem)
                    out.start()
                    out.wait()

        for jj in range(2):
            pl.when(j == jj)(lambda jj=jj: half(jj))

    vec = pl.BlockSpec((1, D), _const2)
    rowspec = pl.BlockSpec((tm, D), lambda i, j: (i, 0))
    return pl.pallas_call(
        body, name=name, grid=(nt, 2),
        in_specs=[rowspec, rowspec, vec, pl.BlockSpec((tm, hw), lambda i, j: (i, j)),
                  pl.BlockSpec((tm, hw), lambda i, j: (i, j + 2)), _resident((F, D))],
        out_specs=[pl.BlockSpec((2, tm, hw), lambda i, j: (0, i, j)), vec, pl.BlockSpec(memory_space=pl.ANY)],
        out_shape=[jax.ShapeDtypeStruct((2, T, F), BF), jax.ShapeDtypeStruct((1, D), F32),
                   jax.ShapeDtypeStruct((F, D), BF)],
        scratch_shapes=[pltpu.VMEM((tm, D), BF), pltpu.VMEM((D, tm), BF), pltpu.VMEM((2, D, hw), F32),
                        pltpu.VMEM((MXU_N, D), BF), pltpu.SemaphoreType.DMA(())],
        compiler_params=_cp(("arbitrary", "arbitrary")),
    )(dxo, y, gt, ab, ab, wd)


def _tn_matmul(a, b, token=None, *, tn, tk, name):
    S, T, Ns = a.shape
    tn, tk = _tile(Ns, tn), _tile(T, tk)
    nk, njs = T // tk, Ns // tn
    deps = [] if token is None else [token]

    def body(a_ref, b_ref, *rest):
        o_ref, acc = rest[len(deps):]
        k = pl.program_id(1)

        @pl.when(k == 0)
        def _():
            acc[...] = jnp.zeros_like(acc)
        acc[...] += lax.dot_general(a_ref[0], b_ref[...], TN, preferred_element_type=F32)

        @pl.when(k == nk - 1)
        def _():
            o_ref[...] = acc[...].astype(BF)

    return pl.pallas_call(
        body, name=name, grid=(S * njs, nk),
        in_specs=[pl.BlockSpec((1, tk, tn), lambda j, k: (j // njs, k, j % njs)),
                  pl.BlockSpec((tk, D), lambda j, k: (k, 0))] + [pl.BlockSpec(memory_space=pl.ANY)] * len(deps),
        out_specs=pl.BlockSpec((tn, D), lambda j, k: (j, 0)),
        out_shape=jax.ShapeDtypeStruct((S * Ns, D), BF),
        scratch_shapes=[pltpu.VMEM((tn, D), F32)],
        compiler_params=_cp(("parallel", "arbitrary")),
    )(a, b, *deps)


def _nn_bwd_norm(da, w, x, g, sc, dxo, *, tm, name):
    S, T, Ks = da.shape
    tm = _tile(T, tm)
    rc = _tile(tm, 256)

    def body(da_ref, w_ref, x_ref, g_ref, sc_ref, dxo_ref, dx_ref, dsh_ref, dsc_ref, dg_ref, acc):
        @pl.when(pl.program_id(0) == 0)
        def _():
            dsh_ref[...] = jnp.zeros_like(dsh_ref)
            dsc_ref[...] = jnp.zeros_like(dsc_ref)
            dg_ref[...] = jnp.zeros_like(dg_ref)

        d = jnp.dot(da_ref[0], w_ref[0:Ks, :], preferred_element_type=F32)
        for s in range(1, S):
            d = d + jnp.dot(da_ref[s], w_ref[s * Ks:(s + 1) * Ks, :], preferred_element_type=F32)
        acc[...] = d
        gv = g_ref[...]
        sc1 = 1.0 + sc_ref[...]
        dsh = jnp.zeros((1, D), F32)
        dsc = jnp.zeros((1, D), F32)
        dg = jnp.zeros((1, D), F32)
        for r0 in range(0, tm, rc):
            rows = pl.ds(r0, rc)
            u = acc[rows, :]
            xv = x_ref[rows, :]
            r = lax.rsqrt(jnp.mean(xv * xv, axis=-1, keepdims=True) + EPS)
            xh = xv * r
            dsh = dsh + jnp.sum(u, axis=0, keepdims=True)
            dsc = dsc + jnp.sum(u * (xh * gv), axis=0, keepdims=True)
            us = u * sc1
            dg = dg + jnp.sum(us * xh, axis=0, keepdims=True)
            dxh = us * gv
            dx_ref[rows, :] = dxo_ref[rows, :] + r * (dxh - xh * jnp.mean(dxh * xh, axis=-1, keepdims=True))
        dsh_ref[...] += dsh
        dsc_ref[...] += dsc
        dg_ref[...] += dg

    vec = pl.BlockSpec((1, D), _const2)
    rowspec = pl.BlockSpec((tm, D), _row)
    return pl.pallas_call(
        body, name=name, grid=(T // tm,),
        in_specs=[pl.BlockSpec((S, tm, Ks), lambda i: (0, i, 0)), _resident((S * Ks, D)), rowspec, vec, vec, rowspec],
        out_specs=[rowspec, vec, vec, vec],
        out_shape=[jax.ShapeDtypeStruct((T, D), F32)] + [jax.ShapeDtypeStruct((1, D), F32)] * 3,
        scratch_shapes=[pltpu.VMEM((tm, D), F32)],
        compiler_params=_cp(("arbitrary",)),
    )(da, w, x, g, sc, dxo)


def _rope(t, cos, sin_signed, lt32, inverse=False):
    sel = jnp.where(lt32, pltpu.roll(t, 96, 1), pltpu.roll(t, 32, 1))
    return t * cos - sel * sin_signed if inverse else t * cos + sel * sin_signed


def _rope_tables(T, token=None):
    inv = 1.0 / (ROPE_THETA ** (jnp.arange(0, HEAD_DIM, 2, dtype=F32) / HEAD_DIM))
    ang = _behind(jnp.arange(T, dtype=F32)[:, None] * inv[None, :], token)
    cos, sin = jnp.cos(ang), jnp.sin(ang)
    cos128 = jnp.tile(cos, (1, 4))
    sin128 = jnp.tile(jnp.concatenate([-sin, sin], axis=1), (1, 2))
    return cos128, sin128


QSCALE = HEAD_DIM ** -0.5


def _lane_masks(rows):
    lane = lax.broadcasted_iota(jnp.int32, (rows, 128), 1)
    return (lane % HEAD_DIM) < (HEAD_DIM // 2), [lane < HEAD_DIM, lane >= HEAD_DIM]


def _attn_bias():
    qi = lax.broadcasted_iota(jnp.int32, (4 * BLK, 2 * BLK), 0) % BLK
    kj = lax.broadcasted_iota(jnp.int32, (4 * BLK, 2 * BLK), 1)
    band = (kj > qi) & (kj <= qi + BLK)
    return jnp.stack([jnp.where(band & (kj >= BLK), 0.0, NEG_INF), jnp.where(band, 0.0, NEG_INF)]).astype(F32)


def _attn_prep_tile(proj_ref, c_ref, s_ref, qs_ref, kr_ref, tm):
    lt32, halves = _lane_masks(BLK)
    for b in range(tm // BLK):
        rows = pl.ds(b * BLK, BLK)
        cc, sc = c_ref[rows, :], s_ref[rows, :]
        qr = [_rope(proj_ref[rows, pl.ds(O_Q + p * 128, 128)].astype(F32), cc, sc, lt32) * QSCALE for p in range(8)]
        for g in range(N_KV):
            qs_ref[g, pl.ds(4 * b * BLK, 4 * BLK), :] = _stack_heads(qr, g, halves).astype(BF)
        kr_ref[rows, :] = jnp.concatenate([_rope(proj_ref[rows, pl.ds(O_K + r * 128, 128)].astype(F32), cc, sc, lt32)
                                           for r in range(2)], axis=1).astype(BF)


def _attn_specs():
    prev = lambda n: jnp.maximum(n - 1, 0)
    return [pl.BlockSpec((N_KV, 4 * BLK, 128), lambda n: (0, n, 0)),
            pl.BlockSpec((BLK, 256), _row), pl.BlockSpec((BLK, 256), lambda n: (prev(n), 0)),
            pl.BlockSpec((BLK, 256), lambda n: (n, O_V // 256)),
            pl.BlockSpec((BLK, 256), lambda n: (prev(n), O_V // 256)),
            pl.BlockSpec((1, 4 * BLK, 2 * BLK), lambda n: (jnp.minimum(n, 1), 0, 0)),
            pl.BlockSpec(memory_space=pltpu.SMEM)]


def _bands(kc_ref, kp_ref, vc_ref, vp_ref):
    kb, vb = [], []
    for r in range(2):
        cols = slice(r * 128, (r + 1) * 128)
        kb.append(jnp.concatenate([kp_ref[:, cols], kc_ref[:, cols]], axis=0))
        vb.append(jnp.concatenate([vp_ref[:, cols], vc_ref[:, cols]], axis=0))
    return kb, vb


def _sink_rows(sink_ref, g):
    return jnp.concatenate([jnp.full((BLK, 128), sink_ref[4 * g + hh], F32) for hh in range(4)], axis=0)


def _both(t):
    return jnp.concatenate([t, t], axis=1)


def _unstack_heads(t, g, halves, acc):
    half = g % 2
    for hh in range(4):
        h = 4 * g + hh
        th = jnp.where(halves[half], t[hh * BLK:(hh + 1) * BLK], 0.0)
        if h % 2 != half:
            th = pltpu.roll(th, HEAD_DIM, 1)
        acc[h // 2] = acc[h // 2] + th


def _stack_heads(chunks, g, halves):
    half = g % 2
    parts = []
    for hh in range(4):
        h = 4 * g + hh
        t = chunks[h // 2]
        if h % 2 != half:
            t = pltpu.roll(t, HEAD_DIM, 1)
        parts.append(jnp.where(halves[half], t, 0.0))
    return jnp.concatenate(parts, axis=0)


def _attn_fwd(qs, kr, proj, bias, sinks, *, name):
    T = proj.shape[0]
    nb = T // BLK

    def body(qs_ref, kc_ref, kp_ref, vc_ref, vp_ref, bias_ref, sink_ref, o_ref, lse_ref):
        _, h128 = _lane_masks(BLK)
        _, h256 = _lane_masks(2 * BLK)
        _, h512 = _lane_masks(4 * BLK)
        kb, vb = _bands(kc_ref, kp_ref, vc_ref, vp_ref)
        outs = [jnp.zeros((BLK, 128), F32) for _ in range(8)]
        groups = range(N_KV)
        bias = bias_ref[0]
        sink = [_sink_rows(sink_ref, g) for g in groups]
        s = [lax.dot_general(qs_ref[g], kb[g // 2], NT, preferred_element_type=F32) + bias for g in groups]
        m = [jnp.maximum(jnp.broadcast_to(jnp.max(s[g], axis=-1, keepdims=True), (4 * BLK, 128)), sink[g])
             for g in groups]
        p = [jnp.exp(s[g] - _both(m[g])).astype(BF) for g in groups]
        vg = [jnp.where(h256[g % 2], vb[g // 2].astype(F32), 1.0).astype(BF) for g in groups]
        o = [jnp.dot(p[g], vg[g], preferred_element_type=F32) for g in groups]
        denom = [jnp.where(h512[g % 2], pltpu.roll(o[g], HEAD_DIM, 1), o[g]) + jnp.exp(sink[g] - m[g]) for g in groups]
        for g in groups:
            lse_ref[g] = m[g] + jnp.log(denom[g])
            _unstack_heads(o[g] * (1.0 / denom[g]), g, h128, outs)
        o_ref[...] = jnp.concatenate(outs, axis=1).astype(BF)

    return pl.pallas_call(
        body, name=name, grid=(nb,),
        in_specs=_attn_specs(),
        out_specs=[pl.BlockSpec((BLK, D), _row), pl.BlockSpec((N_KV, 4 * BLK, 128), lambda n: (0, n, 0))],
        out_shape=[jax.ShapeDtypeStruct((T, D), BF), jax.ShapeDtypeStruct((N_KV, 4 * T, 128), F32)],
        compiler_params=_cp(("parallel",)),
    )(qs, kr, kr, proj, proj, bias, sinks)


def _attn_bwd(qs, kr, proj, bias, sinks, lse, o, do, cos, sin, dproj, *, name):
    T = proj.shape[0]
    nb = T // BLK

    def body(qs_ref, kc_ref, kp_ref, vc_ref, vp_ref, bias_ref, sink_ref, lse_ref, o_ref, do_ref,
             cc_ref, sc_ref, cp_ref, sp_ref, dproj_ref, dq_ref, dkc_ref, dkp_ref, dvc_ref, dvp_ref, dsink_ref):
        @pl.when(pl.program_id(0) == 0)
        def _():
            dsink_ref[...] = jnp.zeros_like(dsink_ref)
        lt32, h128 = _lane_masks(BLK)
        kb, vb = _bands(kc_ref, kp_ref, vc_ref, vp_ref)
        oc = [o_ref[:, p * 128:(p + 1) * 128].astype(F32) for p in range(8)]
        doc = [do_ref[:, p * 128:(p + 1) * 128].astype(F32) for p in range(8)]
        dqs = [jnp.zeros((BLK, 128), F32) for _ in range(8)]
        lane1 = lax.broadcasted_iota(jnp.int32, (1, 128), 1)
        dsink = jnp.zeros((1, 128), F32)
        groups = range(N_KV)
        bias = bias_ref[0]
        q = [qs_ref[g] for g in groups]
        lse_g = [lse_ref[g] for g in groups]
        s = [lax.dot_general(q[g], kb[g // 2], NT, preferred_element_type=F32) + bias for g in groups]
        dos = [_stack_heads(doc, g, h128) for g in groups]
        dosb = [t.astype(BF) for t in dos]
        dp = [lax.dot_general(dosb[g], vb[g // 2], NT, preferred_element_type=F32) for g in groups]
        delta = [jnp.broadcast_to(jnp.sum(dos[g] * _stack_heads(oc, g, h128), axis=-1, keepdims=True), (4 * BLK, 128))
                 for g in groups]
        p = [jnp.exp(s[g] - _both(lse_g[g])) for g in groups]
        ds = [(p[g] * (dp[g] - _both(delta[g]))).astype(BF) for g in groups]
        pb = [t.astype(BF) for t in p]
        dvg = [lax.dot_general(pb[g], dosb[g], TN, preferred_element_type=F32) for g in groups]
        dkg = [lax.dot_general(ds[g], q[g], TN, preferred_element_type=F32) for g in groups]
        dqg = [jnp.dot(ds[g], kb[g // 2], preferred_element_type=F32) * QSCALE for g in groups]
        dvr = [dvg[0] + dvg[1], dvg[2] + dvg[3]]
        dkr = [dkg[0] + dkg[1], dkg[2] + dkg[3]]
        for g in groups:
            _unstack_heads(dqg[g], g, h128, dqs)
            dsk = -jnp.exp(_sink_rows(sink_ref, g) - lse_g[g]) * delta[g]
            for hh in range(4):
                val = jnp.sum(dsk[hh * BLK:(hh + 1) * BLK], axis=0, keepdims=True)
                dsink = dsink + jnp.where(lane1 == 4 * g + hh, val, 0.0)
        cc, sc, cp, sp = cc_ref[...], sc_ref[...], cp_ref[...], sp_ref[...]
        dsink_ref[...] += dsink
        dq_ref[...] = jnp.concatenate([_rope(t, cc, sc, lt32, inverse=True) for t in dqs], axis=1).astype(BF)
        dkp_ref[...] = jnp.concatenate([_rope(t[:BLK], cp, sp, lt32, inverse=True) for t in dkr], axis=1)
        dkc_ref[...] = jnp.concatenate([_rope(t[BLK:], cc, sc, lt32, inverse=True) for t in dkr], axis=1)
        dvp_ref[...] = jnp.concatenate([t[:BLK] for t in dvr], axis=1)
        dvc_ref[...] = jnp.concatenate([t[BLK:] for t in dvr], axis=1)

    kv = pl.BlockSpec((BLK, 256), _row)
    tc = pl.BlockSpec((BLK, 128), _row)
    tp = pl.BlockSpec((BLK, 128), lambda n: (jnp.maximum(n - 1, 0), 0))
    return pl.pallas_call(
        body, name=name, grid=(nb,),
        in_specs=_attn_specs() + [pl.BlockSpec((N_KV, 4 * BLK, 128), lambda n: (0, n, 0)),
                                  pl.BlockSpec((BLK, D), _row), pl.BlockSpec((BLK, D), _row), tc, tc, tp, tp,
                                  pl.BlockSpec(memory_space=pl.ANY)],
        out_specs=[pl.BlockSpec((BLK, D), lambda n: (n, O_Q // D)), kv, kv, kv, kv, pl.BlockSpec((1, 128), _const2)],
        out_shape=[jax.ShapeDtypeStruct(dproj.shape, BF)] + [jax.ShapeDtypeStruct((T, 256), F32)] * 4
        + [jax.ShapeDtypeStruct((1, 128), F32)],
        input_output_aliases={14: 0},
        compiler_params=_cp(("arbitrary",)),
    )(qs, kr, kr, proj, proj, bias, sinks, lse, o, do, cos, sin, cos, sin, dproj)


def _dkv_combine(dkc, dkp, dvc, dvp, dproj, *, name):
    T = dkc.shape[0]
    nb = T // BLK
    tm = _tile(T, 4 * BLK)
    bpt = tm // BLK
    nt = T // tm

    def body(dkc_ref, dkp_ref, dkn_ref, dvc_ref, dvp_ref, dvn_ref, dproj_ref, o_ref):
        keep = jnp.where(pl.program_id(0) == nt - 1, 0.0, 1.0)

        def shifted(prev_ref, next_ref):
            nxt = keep * next_ref[...]
            return nxt if bpt == 1 else jnp.concatenate([prev_ref[BLK:, :], nxt], axis=0)

        o_ref[:, 0:256] = (dkc_ref[...] + shifted(dkp_ref, dkn_ref)).astype(BF)
        o_ref[:, 256:512] = (dvc_ref[...] + shifted(dvp_ref, dvn_ref)).astype(BF)

    cur = pl.BlockSpec((tm, 256), _row)
    nxt = pl.BlockSpec((BLK, 256), lambda i: (jnp.minimum((i + 1) * bpt, nb - 1), 0))
    return pl.pallas_call(
        body, name=name, grid=(nt,),
        in_specs=[cur, cur, nxt, cur, cur, nxt, pl.BlockSpec(memory_space=pl.ANY)],
        out_specs=pl.BlockSpec((tm, 512), lambda i: (i, O_K // 512)),
        out_shape=jax.ShapeDtypeStruct(dproj.shape, BF),
        input_output_aliases={6: 0},
        compiler_params=_cp(("parallel",)),
    )(dkc, dkp, dkp, dvc, dvp, dvp, dproj)


HALO = 16


def _conv_shifts(cu, hprev, tm):
    row = lax.broadcasted_iota(jnp.int32, (8, cu.shape[1]), 0)
    h1 = hprev[HALO - 1:HALO, :]
    h2 = hprev[HALO - 2:HALO - 1, :]
    m1 = pltpu.roll(cu, 1, 0)
    m2 = pltpu.roll(cu, 2, 0)
    m1 = jnp.concatenate([jnp.where(row == 0, h1, m1[0:8]), m1[8:]], axis=0)
    m2 = jnp.concatenate([jnp.where(row == 0, h2, jnp.where(row == 1, h1, m2[0:8])), m2[8:]], axis=0)
    return m1, m2


def _mixer_mid_fwd(proj, attn, wcp, wap, wout, convw, x, gt, *, tm, name):
    T = x.shape[0]
    tm = _tile(T, tm)
    hb = tm // HALO

    def body(bg_ref, cg_ref, u_ref, hcg_ref, hu_ref, zc0_ref, zc1_ref, za0_ref, za1_ref, at_ref,
             wcp_ref, wap_ref, wout_ref, cw_ref, x_ref, gt_ref,
             x2_ref, gc_ref, yc_ref, ya_ref, mg_ref, o_ref):
        first = jnp.where(pl.program_id(0) == 0, 0.0, 1.0)
        cu = cg_ref[...].astype(F32) * u_ref[...].astype(F32)
        hprev = first * (hcg_ref[...].astype(F32) * hu_ref[...].astype(F32))
        m1, m2 = _conv_shifts(cu, hprev, tm)
        cv = cw_ref[0:1, :] * m2 + cw_ref[1:2, :] * m1 + cw_ref[2:3, :] * cu
        gc = (bg_ref[...].astype(F32) * cv).astype(BF)
        gc_ref[...] = gc
        yc = jnp.dot(gc, wcp_ref[...], preferred_element_type=F32)
        ya = jnp.dot(at_ref[...], wap_ref[...], preferred_element_type=F32)
        yc_ref[...] = yc.astype(BF)
        ya_ref[...] = ya.astype(BF)
        zc = jnp.concatenate([zc0_ref[...], zc1_ref[...]], axis=1).astype(F32)
        za = jnp.concatenate([za0_ref[...], za1_ref[...]], axis=1).astype(F32)
        mg = (_sigmoid(zc) * yc + _sigmoid(za) * ya).astype(BF)
        mg_ref[...] = mg
        o = jnp.dot(mg, wout_ref[...], preferred_element_type=F32)
        o_ref[...] = o.astype(BF)
        x2_ref[...] = x_ref[...] + gt_ref[...] * o

    wspec = pl.BlockSpec((D, D), _const2)
    rowspec = pl.BlockSpec((tm, D), _row)
    return pl.pallas_call(
        body, name=name, grid=(T // tm,),
        in_specs=[_col(tm, O_BG), _col(tm, O_CG), _col(tm, O_U), _halo_prev(hb, O_CG), _halo_prev(hb, O_U),
                  _col(tm, O_ZC, 512), _col(tm, O_ZC + 512, 512), _col(tm, O_ZA, 512), _col(tm, O_ZA + 512, 512),
                  rowspec, wspec, wspec, wspec, pl.BlockSpec((8, D), _const2), rowspec, pl.BlockSpec((1, D), _const2)],
        out_specs=[rowspec] * 6,
        out_shape=[jax.ShapeDtypeStruct((T, D), F32)] + [jax.ShapeDtypeStruct((T, D), BF)] * 5,
        compiler_params=_cp(("parallel",)),
    )(proj, proj, proj, proj, proj, proj, proj, proj, proj, attn, wcp, wap, wout, convw, x, gt)


def _col(tm, c, w=D):
    assert c % w == 0
    return pl.BlockSpec((tm, w), lambda i: (i, c // w))


def _halo_prev(hb, c):
    return pl.BlockSpec((HALO, D), lambda i: (jnp.maximum(i * hb - 1, 0), c // D))


def _halo_next(hb, nblk, c=0):
    return pl.BlockSpec((HALO, D), lambda i: (jnp.minimum((i + 1) * hb, nblk - 1), c // D))


def _mixer_mid_bwd(dx2, gt, o, proj, yc, ya, wout, wcp, wap, *, tm, name):
    T = dx2.shape[0]
    tm = _tile(T, tm)
    nt = T // tm

    def body(dx_ref, gt_ref, o_ref, zc0_ref, zc1_ref, za0_ref, za1_ref, yc_ref, ya_ref, wout_ref, wcp_ref, wap_ref,
             dout_ref, dyc_ref, dya_ref, dgc_ref, dat_ref, dproj_ref, dgt_ref, dzs, sems):
        i = pl.program_id(0)
        slot = lax.rem(i, 2)

        def slab_copy(step, s):
            return pltpu.make_async_copy(
                dzs.at[s], dproj_ref.at[pl.ds(pl.multiple_of(step * tm, tm), tm), pl.ds(O_ZC, 2 * D)], sems.at[s])

        @pl.when(i == 0)
        def _():
            dgt_ref[...] = jnp.zeros_like(dgt_ref)

        dxv = dx_ref[...]
        dgt_ref[...] += jnp.sum(dxv * o_ref[...].astype(F32), axis=0, keepdims=True)
        dout = (gt_ref[...] * dxv).astype(BF)
        dout_ref[...] = dout
        dmg = lax.dot_general(dout, wout_ref[...], NT, preferred_element_type=F32)
        sc = _sigmoid(jnp.concatenate([zc0_ref[...], zc1_ref[...]], axis=1).astype(F32))
        sa = _sigmoid(jnp.concatenate([za0_ref[...], za1_ref[...]], axis=1).astype(F32))
        dyc = (dmg * sc).astype(BF)
        dya = (dmg * sa).astype(BF)
        dyc_ref[...] = dyc
        dya_ref[...] = dya
        dzs[slot, :, 0:D] = (dmg * yc_ref[...].astype(F32) * (sc * (1.0 - sc))).astype(BF)
        dzs[slot, :, D:2 * D] = (dmg * ya_ref[...].astype(F32) * (sa * (1.0 - sa))).astype(BF)
        slab_copy(i, slot).start()
        dgc_ref[...] = lax.dot_general(dyc, wcp_ref[...], NT, preferred_element_type=F32).astype(BF)
        dat_ref[...] = lax.dot_general(dya, wap_ref[...], NT, preferred_element_type=F32).astype(BF)

        @pl.when(i > 0)
        def _():
            slab_copy(i - 1, 1 - slot).wait()

        @pl.when(i == nt - 1)
        def _():
            slab_copy(i, slot).wait()

    def zcol(c):
        return pl.BlockSpec((tm, 512), lambda i: (i, c // 512))

    wspec = pl.BlockSpec((D, D), _const2)
    rowspec = pl.BlockSpec((tm, D), _row)
    vec = pl.BlockSpec((1, D), _const2)
    return pl.pallas_call(
        body, name=name, grid=(nt,),
        in_specs=[rowspec, vec, rowspec, zcol(O_ZC), zcol(O_ZC + 512), zcol(O_ZA), zcol(O_ZA + 512),
                  rowspec, rowspec, wspec, wspec, wspec],
        out_specs=[rowspec] * 5 + [pl.BlockSpec(memory_space=pl.ANY), vec],
        out_shape=[jax.ShapeDtypeStruct((T, D), BF)] * 5 + [jax.ShapeDtypeStruct((T, NIN), BF),
                                                            jax.ShapeDtypeStruct((1, D), F32)],
        scratch_shapes=[pltpu.VMEM((2, tm, 2 * D), BF), pltpu.SemaphoreType.DMA((2,))],
        compiler_params=_cp(("arbitrary",)),
    )(dx2, gt, o, proj, proj, proj, proj, yc, ya, wout, wcp, wap)


def _conv_bwd(dgc, proj, convw, dproj, *, tm, name):
    T = dgc.shape[0]
    tm = _tile(T, tm)
    hb = tm // HALO
    nblk = T // HALO
    nt = T // tm

    def body(dgc_ref, ndgc_ref, bg_ref, nbg_ref, cg_ref, u_ref, hcg_ref, hu_ref, cw_ref, dproj_ref, dp_ref, dcw_ref):
        i = pl.program_id(0)

        @pl.when(i == 0)
        def _():
            dcw_ref[...] = jnp.zeros_like(dcw_ref)
        first = jnp.where(i == 0, 0.0, 1.0)
        last = jnp.where(i == nt - 1, 0.0, 1.0)
        cg = cg_ref[...].astype(F32)
        u = u_ref[...].astype(F32)
        bg = bg_ref[...].astype(F32)
        dg = dgc_ref[...].astype(F32)
        cu = cg * u
        hprev = first * (hcg_ref[...].astype(F32) * hu_ref[...].astype(F32))
        m1, m2 = _conv_shifts(cu, hprev, tm)
        w0, w1, w2 = cw_ref[0:1, :], cw_ref[1:2, :], cw_ref[2:3, :]
        cv = w0 * m2 + w1 * m1 + w2 * cu
        dcv = dg * bg
        nxt = last * (ndgc_ref[...].astype(F32) * nbg_ref[...].astype(F32))
        n0, n1 = nxt[0:1, :], nxt[1:2, :]
        row = lax.broadcasted_iota(jnp.int32, (8, D), 0)
        p1 = pltpu.roll(dcv, tm - 1, 0)
        p2 = pltpu.roll(dcv, tm - 2, 0)
        p1 = jnp.concatenate([p1[:tm - 8], jnp.where(row == 7, n0, p1[tm - 8:])], axis=0)
        p2 = jnp.concatenate([p2[:tm - 8], jnp.where(row == 7, n1, jnp.where(row == 6, n0, p2[tm - 8:]))], axis=0)
        dcu = w2 * dcv + w1 * p1 + w0 * p2
        dp_ref[:, 0:D] = (dg * cv).astype(BF)
        dp_ref[:, D:2 * D] = (dcu * u).astype(BF)
        dp_ref[:, 2 * D:3 * D] = (dcu * cg).astype(BF)
        dcw_ref[0:1, :] += jnp.sum(dcv * m2, axis=0, keepdims=True)
        dcw_ref[1:2, :] += jnp.sum(dcv * m1, axis=0, keepdims=True)
        dcw_ref[2:3, :] += jnp.sum(dcv * cu, axis=0, keepdims=True)

    rowspec = pl.BlockSpec((tm, D), _row)
    cw = pl.BlockSpec((8, D), _const2)
    return pl.pallas_call(
        body, name=name, grid=(nt,),
        in_specs=[rowspec, _halo_next(hb, nblk), _col(tm, O_BG), _halo_next(hb, nblk, O_BG),
                  _col(tm, O_CG), _col(tm, O_U), _halo_prev(hb, O_CG), _halo_prev(hb, O_U), cw,
                  pl.BlockSpec(memory_space=pl.ANY)],
        out_specs=[pl.BlockSpec((tm, 3 * D), _row), cw],
        out_shape=[jax.ShapeDtypeStruct(dproj.shape, BF), jax.ShapeDtypeStruct((8, D), F32)],
        input_output_aliases={9: 0},
        compiler_params=_cp(("arbitrary",)),
    )(dgc, dgc, proj, proj, proj, proj, proj, proj, convw, dproj)


def _adam_math(w, g, m, v):
    nm = ADAM_B1 * m + (1.0 - ADAM_B1) * g
    nv = ADAM_B2 * v + (1.0 - ADAM_B2) * (g * g)
    m_hat = nm / (1.0 - ADAM_B1 ** ADAM_STEP)
    v_hat = nv / (1.0 - ADAM_B2 ** ADAM_STEP)
    return -ADAM_LR * (m_hat / (jnp.sqrt(v_hat) + ADAM_EPS) + ADAM_WD * w), nm, nv


SMALL = ("b_ada", "g_ffn1", "g_mix", "g_ffn2", "g_final", "conv_w", "sinks")


def _adam_small(gsum, conv_g, w, m, v, *, name):
    nsm = len(SMALL)

    def body(*refs):
        gs_ref, cg_ref = refs[0], refs[1]
        w_refs, m_refs, v_refs = (refs[2 + k * nsm:2 + (k + 1) * nsm] for k in range(3))
        outs = refs[2 + 3 * nsm:]
        for p, n in enumerate(SMALL):
            if n == "b_ada":
                pieces = [(slice(None), slice(r * D, (r + 1) * D), gs_ref[R_MODS + r:R_MODS + r + 1, :])
                          for r in range(N_MOD)]
            elif n == "conv_w":
                pieces = [(slice(None), slice(None), cg_ref[...])]
            elif n == "sinks":
                pieces = [(slice(None), slice(None), gs_ref[R_SINK:R_SINK + 1, 0:N_HEADS])]
            else:
                row = dict(g_ffn1=R_G1, g_mix=R_GM, g_ffn2=R_G2, g_final=R_GF)[n]
                pieces = [(slice(None), slice(None), gs_ref[row:row + 1, :])]
            for rs, cs, g in pieces:
                d, nm, nv = _adam_math(w_refs[p][rs, cs], g, m_refs[p][rs, cs], v_refs[p][rs, cs])
                for k, val in enumerate((g, d, nm, nv)):
                    outs[k * nsm + p][rs, cs] = val

    args = [gsum, conv_g] + [d[n] for d in (w, m, v) for n in SMALL]
    shapes = [jax.ShapeDtypeStruct(w[n].shape, F32) for _ in range(4) for n in SMALL]
    res = pl.pallas_call(body, name=name, out_shape=shapes, compiler_params=_cp())(*args)
    return [dict(zip(SMALL, res[k * nsm:(k + 1) * nsm])) for k in range(4)]


def _adam(w, g, m, v, *, tm, name):
    _, R, C = w.shape
    tm = _tile(R, tm)
    parts = g.ndim == 3

    def body(w_ref, g_ref, m_ref, v_ref, go_ref, d_ref, nm_ref, nv_ref):
        if parts:
            gv = g_ref[0].astype(F32)
            for s in range(1, N_DEV):
                gv = gv + g_ref[s].astype(F32)
        else:
            gv = g_ref[...]
        go_ref[0] = gv
        d_ref[0], nm_ref[0], nv_ref[0] = _adam_math(w_ref[0], gv, m_ref[0], v_ref[0])

    spec = pl.BlockSpec((1, tm, C), lambda i: (0, i, 0))
    gspec = pl.BlockSpec((N_DEV, tm, C), lambda i: (0, i, 0)) if parts else pl.BlockSpec((tm, C), _row)
    return pl.pallas_call(
        body, name=name, grid=(R // tm,),
        in_specs=[spec, gspec, spec, spec], out_specs=[spec] * 4,
        out_shape=[jax.ShapeDtypeStruct((1, R, C), F32)] * 4,
        compiler_params=_cp(("parallel",)),
    )(w, g, m, v)


def _mods_part(c_all, w_ada, b_ada, *, name):
    C = w_ada.shape[1]

    def body(c_ref, w_ref, b_ref, o_ref):
        cv = c_ref[...]
        ca = cv * jax.nn.sigmoid(cv)
        o_ref[...] = jnp.dot(ca, w_ref[...], preferred_element_type=F32,
                             precision=lax.Precision.HIGHEST) + b_ref[...]

    return pl.pallas_call(
        body, name=name,
        out_shape=jax.ShapeDtypeStruct((N_DEV, C), F32),
        compiler_params=_cp(),
    )(c_all, w_ada, b_ada)


def _wada_grad(c_all_t, gm, *, name):
    C = gm.shape[1]

    def body(c_ref, g_ref, o_ref):
        cv = c_ref[...]
        ca = cv * jax.nn.sigmoid(cv)
        acc = ca[:, 0:1] * g_ref[0:1, :]
        for b in range(1, N_DEV):
            acc = acc + ca[:, b:b + 1] * g_ref[b:b + 1, :]
        o_ref[...] = acc

    return pl.pallas_call(
        body, name=name,
        out_shape=jax.ShapeDtypeStruct((D, C), F32),
        compiler_params=_cp(),
    )(c_all_t, gm)


def _peer(x, y, c, d):
    px = lax.rem(x + ((d >> 2) & 1), 2)
    py = lax.rem(y + ((d >> 1) & 1), 2)
    pc = lax.rem(c + (d & 1), 2)
    return (px, py, pc), 4 * px + 2 * py + pc


def _exchange(xs, *, scatter, name):
    n = len(xs)
    nsem = n * (N_DEV - 1)

    def body(*refs):
        ins, outs = refs[:n], refs[n:2 * n]
        token, send_sems, recv_sems, local_sems = refs[2 * n:]
        x, y, c = lax.axis_index("x"), lax.axis_index("y"), lax.axis_index("c")
        me = 4 * x + 2 * y + c
        token[...] = jnp.zeros_like(token)

        def src(t, idx):
            return ins[t].at[idx] if scatter else ins[t]

        local = [pltpu.make_async_copy(src(t, me), outs[t].at[me], local_sems.at[t]) for t in range(n)]
        for cp in local:
            cp.start()
        remote = []
        for t in range(n):
            for d in range(1, N_DEV):
                peer, pidx = _peer(x, y, c, d)
                k = t * (N_DEV - 1) + d - 1
                send = pltpu.make_async_remote_copy(src_ref=src(t, pidx), dst_ref=outs[t].at[me],
                                                    send_sem=send_sems.at[k], recv_sem=recv_sems.at[k],
                                                    device_id=peer, device_id_type=MESH)
                recv = pltpu.make_async_remote_copy(src_ref=src(t, pidx), dst_ref=outs[t].at[pidx],
                                                    send_sem=send_sems.at[k], recv_sem=recv_sems.at[k],
                                                    device_id=peer, device_id_type=MESH)
                send.start()
                remote.append((send, recv))
        for cp in local:
            cp.wait()
        for send, recv in remote:
            send.wait_send()
            recv.wait_recv()

    anyspec = pl.BlockSpec(memory_space=pl.ANY)
    out_shape = [jax.ShapeDtypeStruct(a.shape if scatter else (N_DEV,) + a.shape, a.dtype) for a in xs]
    out_shape.append(jax.ShapeDtypeStruct((8, 128), F32))
    return pl.pallas_call(
        body, name=name,
        in_specs=[anyspec] * n, out_specs=[anyspec] * n + [pl.BlockSpec(memory_space=pltpu.VMEM)],
        out_shape=out_shape,
        scratch_shapes=[pltpu.SemaphoreType.DMA((nsem,)), pltpu.SemaphoreType.DMA((nsem,)),
                        pltpu.SemaphoreType.DMA((n,))],
    )(*xs)


def _sum8(parts, *, name):
    _, R, C = parts.shape

    def body(p_ref, o_ref):
        acc = p_ref[0]
        for s in range(1, N_DEV):
            acc = acc + p_ref[s]
        o_ref[...] = acc

    return pl.pallas_call(body, name=name, out_shape=jax.ShapeDtypeStruct((R, C), F32),
                          compiler_params=_cp())(parts)


HBM_SPEC = pl.BlockSpec(memory_space=pltpu.HBM)
SEM_SPEC = pl.BlockSpec(memory_space=pltpu.SEMAPHORE)
N_PEER = N_DEV - 1


def _split_copies(src_refs, land_refs, send_sems, recv_sems, scatter):
    x, y, c = lax.axis_index("x"), lax.axis_index("y"), lax.axis_index("c")
    me = 4 * x + 2 * y + c
    pairs = []
    for j, (src, land) in enumerate(zip(src_refs, land_refs)):
        for d in range(1, N_DEV):
            peer, pidx = _peer(x, y, c, d)
            k = j * N_PEER + d - 1
            s = src.at[pidx] if scatter else src
            send = pltpu.make_async_remote_copy(src_ref=s, dst_ref=land.at[me], send_sem=send_sems.at[k],
                                                recv_sem=recv_sems.at[k], device_id=peer, device_id_type=MESH)
            recv = pltpu.make_async_remote_copy(src_ref=s, dst_ref=land.at[pidx], send_sem=send_sems.at[k],
                                                recv_sem=recv_sems.at[k], device_id=peer, device_id_type=MESH)
            pairs.append((send, recv))
    return pairs


def _own_slot(block, me):
    land = lax.empty((N_DEV,) + block.shape, block.dtype)
    return lax.dynamic_update_slice(land, block[None], (me, 0, 0))


def _split_start(srcs, lands, groups, *, scatter, name):
    n, ng = len(srcs), len(groups)

    def body(*refs):
        src_refs, land_refs = refs[:n], refs[n:2 * n]
        sems = refs[2 * n:2 * n + 2 * ng]
        token = refs[-1]
        for gi, g in enumerate(groups):
            pairs = _split_copies([src_refs[t] for t in g], [land_refs[t] for t in g], sems[2 * gi],
                                  sems[2 * gi + 1], scatter)
            for send, _ in pairs:
                send.start()
        token[...] = jnp.zeros_like(token)

    sem_shapes = []
    for g in groups:
        sem_shapes += [pltpu.SemaphoreType.DMA((len(g) * N_PEER,))] * 2
    thru = [pltpu.HBM(a.shape, a.dtype) for a in list(srcs) + list(lands)]
    outs = pl.pallas_call(
        body, name=name,
        out_shape=tuple(sem_shapes + thru + [jax.ShapeDtypeStruct((8, 128), F32)]),
        in_specs=[HBM_SPEC] * (2 * n),
        out_specs=tuple([SEM_SPEC] * (2 * ng) + [HBM_SPEC] * (2 * n) + [pl.BlockSpec(memory_space=pltpu.VMEM)]),
        input_output_aliases={i: 2 * ng + i for i in range(2 * n)},
        compiler_params=pltpu.CompilerParams(has_side_effects=pltpu.SideEffectType.DATAFLOW_SIDE_EFFECTING),
    )(*[pltpu.with_memory_space_constraint(a, pltpu.HBM) for a in list(srcs) + list(lands)])
    sems = [(outs[2 * gi], outs[2 * gi + 1]) for gi in range(ng)]
    return sems, outs[2 * ng:2 * ng + n], outs[2 * ng + n:2 * ng + 2 * n], outs[-1]


def _behind(v, token):
    if token is None:
        return v
    return v + token[0, 0].astype(v.dtype)


def _split_wait(srcs, lands, sems, after, *, scatter, name):
    m = len(srcs)

    def body(*refs):
        src_refs, land_refs = refs[:m], refs[m:2 * m]
        send_sems, recv_sems = refs[2 * m], refs[2 * m + 1]
        for send, recv in _split_copies(src_refs, land_refs, send_sems, recv_sems, scatter):
            send.wait_send()
            recv.wait_recv()

    outs = pl.pallas_call(
        body, name=name,
        out_shape=tuple(pltpu.HBM(a.shape, a.dtype) for a in list(srcs) + list(lands)),
        in_specs=[HBM_SPEC] * (2 * m) + [SEM_SPEC, SEM_SPEC, pl.BlockSpec(memory_space=pl.ANY)],
        out_specs=tuple([HBM_SPEC] * (2 * m)),
        input_output_aliases={i: i for i in range(2 * m)},
        compiler_params=pltpu.CompilerParams(has_side_effects=pltpu.SideEffectType.DATAFLOW_SIDE_EFFECTING),
    )(*srcs, *lands, sems[0], sems[1], after)
    return outs[m:]


TL_FIRST = (1, 2, 4, 6)
TL_ICI = (2, 4, 6)
EFFECT = pltpu.SideEffectType.DATAFLOW_SIDE_EFFECTING


def _tl_first(src_refs, land_refs, send_sems, recv_sems):
    x, y, c = lax.axis_index("x"), lax.axis_index("y"), lax.axis_index("c")
    me = 4 * x + 2 * y + c
    out = []
    for j, (src, land) in enumerate(zip(src_refs, land_refs)):
        for i, d in enumerate(TL_FIRST):
            peer, pidx = _peer(x, y, c, d)
            k = len(TL_FIRST) * j + i
            send = pltpu.make_async_remote_copy(src_ref=src, dst_ref=land.at[me], send_sem=send_sems.at[k],
                                                recv_sem=recv_sems.at[k], device_id=peer, device_id_type=MESH)
            recv = pltpu.make_async_remote_copy(src_ref=src, dst_ref=land.at[pidx], send_sem=send_sems.at[k],
                                                recv_sem=recv_sems.at[k], device_id=peer, device_id_type=MESH)
            out.append((d, send, recv))
    return out


def _tl_second(land_refs, send_sems, recv_sems):
    x, y, c = lax.axis_index("x"), lax.axis_index("y"), lax.axis_index("c")
    sibling, _ = _peer(x, y, c, 1)
    out = []
    for j, land in enumerate(land_refs):
        for i, d in enumerate(TL_ICI):
            _, mine = _peer(x, y, c, d)
            _, theirs = _peer(x, y, c, d + 1)
            k = len(TL_ICI) * j + i
            send = pltpu.make_async_remote_copy(src_ref=land.at[mine], dst_ref=land.at[mine], send_sem=send_sems.at[k],
                                                recv_sem=recv_sems.at[k], device_id=sibling, device_id_type=MESH)
            recv = pltpu.make_async_remote_copy(src_ref=land.at[mine], dst_ref=land.at[theirs],
                                                send_sem=send_sems.at[k], recv_sem=recv_sems.at[k],
                                                device_id=sibling, device_id_type=MESH)
            out.append((send, recv))
    return out


def _tl_start(srcs, lands, groups, *, name):
    n, ng = len(srcs), len(groups)

    def body(*refs):
        src_refs, land_refs = refs[:n], refs[n:2 * n]
        sems = refs[2 * n:2 * n + 2 * ng]
        for gi, g in enumerate(groups):
            for _, send, _ in _tl_first([src_refs[t] for t in g], [land_refs[t] for t in g], sems[2 * gi],
                                        sems[2 * gi + 1]):
                send.start()
        refs[-1][...] = jnp.zeros_like(refs[-1])

    sem_shapes = []
    for g in groups:
        sem_shapes += [pltpu.SemaphoreType.DMA((len(g) * len(TL_FIRST),))] * 2
    thru = [pltpu.HBM(a.shape, a.dtype) for a in list(srcs) + list(lands)]
    outs = pl.pallas_call(
        body, name=name,
        out_shape=tuple(sem_shapes + thru + [jax.ShapeDtypeStruct((8, 128), F32)]),
        in_specs=[HBM_SPEC] * (2 * n),
        out_specs=tuple([SEM_SPEC] * (2 * ng) + [HBM_SPEC] * (2 * n) + [pl.BlockSpec(memory_space=pltpu.VMEM)]),
        input_output_aliases={i: 2 * ng + i for i in range(2 * n)},
        compiler_params=pltpu.CompilerParams(has_side_effects=EFFECT),
    )(*[pltpu.with_memory_space_constraint(a, pltpu.HBM) for a in list(srcs) + list(lands)])
    sems = [(outs[2 * gi], outs[2 * gi + 1]) for gi in range(ng)]
    return sems, outs[2 * ng:2 * ng + n], outs[2 * ng + n:2 * ng + 2 * n], outs[-1]


def _tl_forward(srcs, lands, sems1, after, *, name):
    m = len(srcs)

    def body(*refs):
        src_refs, land_refs = refs[:m], refs[m:2 * m]
        send1, recv1 = refs[2 * m], refs[2 * m + 1]
        send2, recv2 = refs[2 * m + 3], refs[2 * m + 4]
        for d, _, recv in _tl_first(src_refs, land_refs, send1, recv1):
            if d in TL_ICI:
                recv.wait_recv()
        for send, _ in _tl_second(land_refs, send2, recv2):
            send.start()

    sem = pltpu.SemaphoreType.DMA((m * len(TL_ICI),))
    outs = pl.pallas_call(
        body, name=name,
        out_shape=tuple([sem, sem] + [pltpu.HBM(a.shape, a.dtype) for a in list(srcs) + list(lands)]),
        in_specs=[HBM_SPEC] * (2 * m) + [SEM_SPEC, SEM_SPEC, pl.BlockSpec(memory_space=pl.ANY)],
        out_specs=tuple([SEM_SPEC, SEM_SPEC] + [HBM_SPEC] * (2 * m)),
        input_output_aliases={i: 2 + i for i in range(2 * m)},
        compiler_params=pltpu.CompilerParams(has_side_effects=EFFECT),
    )(*srcs, *lands, sems1[0], sems1[1], after)
    return (outs[0], outs[1]), outs[2:2 + m], outs[2 + m:2 + 2 * m]


def _tl_wait(srcs, lands, sems1, sems2, after, *, name):
    m = len(srcs)

    def body(*refs):
        src_refs, land_refs = refs[:m], refs[m:2 * m]
        send1, recv1, send2, recv2 = refs[2 * m:2 * m + 4]
        for d, send, recv in _tl_first(src_refs, land_refs, send1, recv1):
            send.wait_send()
            if d not in TL_ICI:
                recv.wait_recv()
        for send, recv in _tl_second(land_refs, send2, recv2):
            send.wait_send()
            recv.wait_recv()

    outs = pl.pallas_call(
        body, name=name,
        out_shape=tuple(pltpu.HBM(a.shape, a.dtype) for a in list(srcs) + list(lands)),
        in_specs=[HBM_SPEC] * (2 * m) + [SEM_SPEC] * 4 + [pl.BlockSpec(memory_space=pl.ANY)],
        out_specs=tuple([HBM_SPEC] * (2 * m)),
        input_output_aliases={i: i for i in range(2 * m)},
        compiler_params=pltpu.CompilerParams(has_side_effects=EFFECT),
    )(*srcs, *lands, sems1[0], sems1[1], sems2[0], sems2[1], after)
    return outs[m:]


TM_PROJ = 512
TN_PROJ = 512
TM_ROW = 512
TM_NN = 512
TK_TN = 2048
TM_ADAM = 416
TN_FFN = F // 2
TN_IN = NIN // 4


def _tn(a, b, name, tn, token=None):
    if a.ndim == 2:
        a = a[None]
    return _tn_matmul(a, b, token, tn=tn, tk=TK_TN, name=name)


def _local_step(x, tgt, mods, g1, gm, g2, gf, convw8, sinks, w_get, g_put, tables=None):
    T = x.shape[0]
    sh1, sc1, gt1, sh2, sc2, gt2, sh3, sc3, gt3 = [mods[i:i + 1] for i in range(N_MOD)]
    cos, sin = _rope_tables(T) if tables is None else tables
    behind = _behind

    w = dict(w_get("gu1", mods))
    h1, ab1 = _norm_proj(x, g1, sc1, sh1, w["gu1"], tm=TM_PROJ, tn=TN_PROJ, name="ffn1_up")
    w.update(w_get("d1", ab1))
    x1, y1 = _ffn_down_fwd(ab1, w["d1"], x, gt1, tm=TM_ROW, name="ffn1_down")
    w.update(w_get("mix", x1))
    h2, proj, qs, kr = _norm_proj(x1, gm, sc2, sh2, w["win"], (cos, sin), tm=TM_PROJ, tn=TN_PROJ, name="mix_in")
    bias = _attn_bias()
    attn, lse = _attn_fwd(qs, kr, proj, bias, sinks, name="attn_fwd")
    x2, gc, yc, ya, mg, o = _mixer_mid_fwd(proj, attn, w["cp"], w["ap"], w["out"], convw8, x1, gt2,
                                           tm=TM_ROW, name="mix_mid")
    w.update(w_get("ffn2", x2))
    h3, ab2, y2, dx3, lsum, dgf = _ffn_fwd(x2, g2, sc3, sh3, gt3, w["gu2"], w["d2"], (tgt, gf), tm=TM_ROW,
                                           name="ffn2_final")

    dab2, dgt3, g_d2 = _ffn_down_bwd_dw(dx3, y2, gt3, ab2, w["d2"], tm=TM_ROW, name="ffn2_down_bwd")
    dx2, dsh3, dsc3, dg2 = _nn_bwd_norm(dab2, w["gu2"], x2, g2, sc3, dx3, tm=TM_NN, name="ffn2_up_bwd")
    g_gu2 = _tn(dab2, h3, "ffn2_up_dw", TN_FFN)
    tok = g_put(dict(gu2=g_gu2, d2=g_d2))

    dout, dyc, dya, dgc, dat, dproj, dgt2 = _mixer_mid_bwd(dx2, behind(gt2, tok), o, proj, yc, ya, w["out"], w["cp"],
                                                           w["ap"], tm=TM_ROW, name="mix_mid_bwd")
    g_out = _tn(mg, dout, "mix_out_dw", D)
    g_cp = _tn(gc, dyc, "mix_cp_dw", D)
    g_ap = _tn(attn, dya, "mix_ap_dw", D)
    dproj, dkc, dkp, dvc, dvp, dsink = _attn_bwd(qs, kr, proj, bias, sinks, lse, attn, dat, cos, sin, dproj,
                                                 name="attn_bwd")
    dproj = _dkv_combine(dkc, dkp, dvc, dvp, dproj, name="attn_dkv")
    dproj, dcw = _conv_bwd(dgc, proj, convw8, dproj, tm=TM_ROW, name="conv_bwd")
    g_in = _tn(dproj, h2, "mix_in_dw", TN_IN)
    tok = g_put(dict(win=g_in, cp=g_cp, ap=g_ap, out=g_out))
    dx1, dsh2, dsc2, dgm = _nn_bwd_norm(dproj[None], w["win"], x1, gm, behind(sc2, tok), dx2, tm=TM_NN,
                                        name="mix_in_bwd")

    dab1, dgt1, g_d1 = _ffn_down_bwd_dw(dx1, y1, gt1, ab1, w["d1"], tm=TM_ROW, name="ffn1_down_bwd")
    tok = g_put(dict(d1=g_d1))
    g_gu1 = _tn(dab1, h1, "ffn1_up_dw", TN_FFN, tok)
    tok = g_put(dict(gu1=g_gu1))
    dx0, dsh1, dsc1, dg1 = _nn_bwd_norm(dab1, w["gu1"], x, g1, behind(sc1, tok), dx1, tm=TM_NN,
                                        name="ffn1_up_bwd")

    small = dict(mods=jnp.concatenate([dsh1, dsc1, dgt1, dsh2, dsc2, dgt2, dsh3, dsc3, dgt3], axis=0),
                 g1=dg1, gm=dgm, g2=dg2, gf=dgf, convw=dcw[0:3], sinks=dsink[:, 0:N_HEADS])
    return lsum, dx0, small


BIG = ("gu1", "d1", "win", "cp", "ap", "out", "gu2", "d2")
TRANSPOSED = ("gu1", "win", "gu2")
SMALL_ROWS = 24
R_MODS, R_G1, R_GM, R_G2, R_GF, R_CONV, R_SINK, R_LOSS = 0, 9, 10, 11, 12, 13, 16, 17


def _pad_to(a, rows, cols):
    return jnp.pad(a, ((0, rows - a.shape[0]), (0, cols - a.shape[1])))


def _pack_small(b_ada, g1, gm, g2, gf, conv, sinks, lsum):
    rows = [b_ada.reshape(N_MOD, D), g1.reshape(1, D), gm.reshape(1, D), g2.reshape(1, D), gf.reshape(1, D),
            _pad_to(conv.reshape(3, -1), 3, D), _pad_to(sinks.reshape(1, N_HEADS), 1, D), lsum.reshape(1, D)]
    return _pad_to(jnp.concatenate(rows, axis=0), SMALL_ROWS, D)


def kernel(x, c, w_ada, b_ada, g_ffn1, w1_gu, w1_down, g_mix, w_in, conv_w, w_conv_proj, w_attn_proj, sinks, w_out, g_ffn2, w2_gu, w2_down, g_final, loss_target, m_w_ada, m_b_ada, m_g_ffn1, m_w1_gu, m_w1_down, m_g_mix, m_w_in, m_conv_w, m_w_conv_proj, m_w_attn_proj, m_sinks, m_w_out, m_g_ffn2, m_w2_gu, m_w2_down, m_g_final, v_w_ada, v_b_ada, v_g_ffn1, v_w1_gu, v_w1_down, v_g_mix, v_w_in, v_conv_w, v_w_conv_proj, v_w_attn_proj, v_sinks, v_w_out, v_g_ffn2, v_w2_gu, v_w2_down, v_g_final):
    me = 4 * lax.axis_index("x") + 2 * lax.axis_index("y") + lax.axis_index("c")
    ada_cols = w_ada.shape[2]
    conv_cols = conv_w.shape[2]

    native = dict(gu1=w1_gu[0], d1=w1_down[0], win=w_in[0], cp=w_conv_proj[0], ap=w_attn_proj[0], out=w_out[0],
                  gu2=w2_gu[0], d2=w2_down[0])

    def shard(n, token):
        a = _behind(native[n], token)
        return (a.T if n in TRANSPOSED else a).astype(BF)

    c_all, conv_all, _ = _exchange([c, _pad_to(conv_w[0], 8, conv_cols)], scatter=False, name="gather_cond")
    c_all = c_all.reshape(N_DEV, D)
    conv_full = conv_all[:, 0:3, :].transpose(1, 0, 2).reshape(3, D)

    b_cols = lax.dynamic_slice(b_ada, (0, me * ada_cols), (1, ada_cols))
    mods_cols = _mods_part(c_all, w_ada[0], b_cols, name="ada_mods")
    mods_all, mods_token = _exchange([mods_cols], scatter=False, name="gather_mods")
    mods = lax.dynamic_index_in_dim(mods_all, me, axis=1, keepdims=False).reshape(N_MOD, D)

    groups = dict(gu1=("gu1",), d1=("d1",), mix=("win", "cp", "ap", "out"), ffn2=("gu2", "d2"))
    in_flight = {}
    first = [shard("gu1", mods_token)]
    sems, srcs, lands, token = _tl_start(first, [_own_slot(s, me) for s in first], [[0]],
                                         name="gather_weights_start_gu1")
    in_flight["gu1"] = [sems[0], srcs, lands, None]
    rest = [n for n in BIG if n != "gu1"]
    shards = [shard(n, token) for n in rest]
    rest_groups = [[rest.index(n) for n in names] for g, names in groups.items() if g != "gu1"]
    sems, srcs, lands, rest_token = _tl_start(shards, [_own_slot(s, me) for s in shards], rest_groups,
                                              name="gather_weights_start_rest")
    for (g, names), gsems, idx in zip([kv for kv in groups.items() if kv[0] != "gu1"], sems, rest_groups):
        in_flight[g] = [gsems, [srcs[t] for t in idx], [lands[t] for t in idx], None]

    def forward(group, after):
        sems1, gsrcs, glands, _ = in_flight[group]
        sems2, gsrcs, glands = _tl_forward(gsrcs, glands, sems1, after, name="gather_weights_forward_" + group)
        in_flight[group] = [sems1, gsrcs, glands, sems2]

    forward_early = dict(d1="mix", mix="ffn2")

    tables = _rope_tables(x.shape[1], rest_token)

    def w_get(group, after):
        if group == "gu1":
            after = tables[0]
        if in_flight[group][3] is None:
            forward(group, after)
        sems1, gsrcs, glands, sems2 = in_flight[group]
        landed = _tl_wait(gsrcs, glands, sems1, sems2, after, name="gather_weights_wait_" + group)
        if group in forward_early:
            forward(forward_early[group], landed[0])
        return {n: a.reshape(-1, D) for n, a in zip(groups[group], landed)}

    pending = []

    def g_put(gs):
        names = tuple(gs)
        srcs = [gs[n].reshape(N_DEV, -1, D) for n in names]
        lands = [_own_slot(lax.dynamic_index_in_dim(s, me, axis=0, keepdims=False), me) for s in srcs]
        sems, srcs, lands, tok = _split_start(srcs, lands, [list(range(len(names)))], scatter=True,
                                              name="scatter_grads_start_" + names[0])
        pending.append((names, sems[0], srcs, lands))
        return tok

    lsum, grad_x, small = _local_step(x[0], loss_target[0], mods, g_ffn1, g_mix, g_ffn2, g_final[None],
                                      _pad_to(conv_full, 8, D), sinks[0], w_get, g_put, tables)

    packed = _pack_small(small["mods"], small["g1"], small["gm"], small["g2"], small["gf"], small["convw"],
                         small["sinks"], lsum)
    packed_all, _ = _exchange([packed], scatter=False, name="gather_small")
    gsmall = _sum8(packed_all, name="sum_small")
    loss = (0.5 / D) * jnp.sum(gsmall[R_LOSS])

    w_of = dict(ada=w_ada, gu1=w1_gu, d1=w1_down, win=w_in, cp=w_conv_proj, ap=w_attn_proj, out=w_out, gu2=w2_gu,
                d2=w2_down)
    m_of = dict(ada=m_w_ada, gu1=m_w1_gu, d1=m_w1_down, win=m_w_in, cp=m_w_conv_proj, ap=m_w_attn_proj, out=m_w_out,
                gu2=m_w2_gu, d2=m_w2_down)
    v_of = dict(ada=v_w_ada, gu1=v_w1_gu, d1=v_w1_down, win=v_w_in, cp=v_w_conv_proj, ap=v_w_attn_proj, out=v_w_out,
                gu2=v_w2_gu, d2=v_w2_down)
    upd = {}
    after = gsmall
    for names, sems, srcs, lands in pending:
        parts = _split_wait(srcs, lands, sems, after, scatter=True, name="scatter_grads_wait_" + names[0])
        for n, p in zip(names, parts):
            if n in TRANSPOSED:
                res = _adam(jnp.swapaxes(w_of[n], 1, 2), p, jnp.swapaxes(m_of[n], 1, 2), jnp.swapaxes(v_of[n], 1, 2),
                            tm=TM_ADAM, name="adam_" + n)
                upd[n] = [jnp.swapaxes(t, 1, 2) for t in res]
            else:
                upd[n] = _adam(w_of[n], p, m_of[n], v_of[n], tm=TM_ADAM, name="adam_" + n)
        after = upd[names[-1]][1]

    gm_cols = lax.dynamic_slice(packed_all[:, R_MODS:R_MODS + N_MOD, :].reshape(N_DEV, N_MOD * D),
                                (0, me * ada_cols), (N_DEV, ada_cols))
    upd["ada"] = _adam(w_ada, _wada_grad(c_all.T, gm_cols, name="ada_dw"), m_w_ada, v_w_ada, tm=256, name="adam_ada")
    conv_g = lax.dynamic_slice(gsmall, (R_CONV, me * conv_cols), (3, conv_cols))

    def natural(b, g1, gm, g2, gf, cw, sk):
        return dict(b_ada=b, g_ffn1=g1, g_mix=gm, g_ffn2=g2, g_final=gf[None], conv_w=cw[0], sinks=sk)

    small_out = _adam_small(gsmall, conv_g, natural(b_ada, g_ffn1, g_mix, g_ffn2, g_final, conv_w, sinks),
                            natural(m_b_ada, m_g_ffn1, m_g_mix, m_g_ffn2, m_g_final, m_conv_w, m_sinks),
                            natural(v_b_ada, v_g_ffn1, v_g_mix, v_g_ffn2, v_g_final, v_conv_w, v_sinks),
                            name="adam_small")
    for res in small_out:
        res["g_final"] = res["g_final"][0]
        res["conv_w"] = res["conv_w"][None]

    big_name = dict(w_ada="ada", w1_gu="gu1", w1_down="d1", w_in="win", w_conv_proj="cp", w_attn_proj="ap",
                    w_out="out", w2_gu="gu2", w2_down="d2")
    order = ("w_ada", "b_ada", "g_ffn1", "w1_gu", "w1_down", "g_mix", "w_in", "conv_w", "w_conv_proj", "w_attn_proj",
             "sinks", "w_out", "g_ffn2", "w2_gu", "w2_down", "g_final")
    outs = [loss, grad_x[None]]
    for kind in range(4):
        for n in order:
            outs.append(upd[big_name[n]][kind] if n in big_name else small_out[kind][n])
    return tuple(outs)
```

```python
import jax
import jax.numpy as jnp
from jax import lax
from jax.experimental import pallas as pl
from jax.experimental.pallas import tpu as pltpu

D = 1024
F = 2816
NIN = 6656
N_HEADS = 16
N_KV = 4
HEAD_DIM = 64
BLK = 128
N_MOD = 9
N_DEV = 8
EPS = 1e-6
NEG_INF = -1e30
ROPE_THETA = 10000.0
O_BG, O_CG, O_U, O_Q, O_K, O_V, O_ZC, O_ZA = 0, 1024, 2048, 3072, 4096, 4352, 4608, 5632

ADAM_LR = 0.001
ADAM_B1 = 0.9
ADAM_B2 = 0.999
ADAM_EPS = 1e-08
ADAM_WD = 0.01
ADAM_STEP = 10

BF = jnp.bfloat16
F32 = jnp.float32
VMEM_LIMIT = 56 * 1024 * 1024
MXU_N = 256
MESH = pl.DeviceIdType.MESH

NT = (((1,), (1,)), ((), ()))
TN = (((0,), (0,)), ((), ()))


def _cp(sem=None):
    return pltpu.CompilerParams(dimension_semantics=sem, vmem_limit_bytes=VMEM_LIMIT)


def _tile(n, pref):
    if n <= pref:
        return n
    for t in range(pref - pref % 16, 15, -16):
        if n % t == 0:
            return t
    raise ValueError((n, pref))


def _sigmoid(v):
    return 0.5 * jnp.tanh(0.5 * v) + 0.5


def _row(i):
    return (i, 0)


def _const2(*_):
    return (0, 0)


def _resident(shape):
    return pl.BlockSpec(shape, lambda *_: (0,) * len(shape), pipeline_mode=pl.Buffered(1))


def _norm_proj(x, g, sc, sh, wt, rope=None, *, tm, tn, name):
    T, N = x.shape[0], wt.shape[0]
    tm = _tile(T, tm)

    def body(x_ref, g_ref, sc_ref, sh_ref, w_ref, *rest):
        if rope is None:
            h_ref, o_ref = rest
        else:
            c_ref, s_ref, h_ref, o_ref, qs_ref, kr_ref = rest
        xv = x_ref[...]
        r = lax.rsqrt(jnp.mean(xv * xv, axis=-1, keepdims=True) + EPS)
        hb = ((xv * r) * g_ref[...] * (1.0 + sc_ref[...]) + sh_ref[...]).astype(BF)
        h_ref[...] = hb
        for c0 in range(0, N, tn):
            cols = pl.ds(c0, tn)
            o_ref[:, cols] = lax.dot_general(hb, w_ref[cols, :], NT, preferred_element_type=F32).astype(BF)
            if rope is not None and c0 < O_V <= c0 + tn:
                _attn_prep_tile(o_ref, c_ref, s_ref, qs_ref, kr_ref, tm)

    vec = pl.BlockSpec((1, D), _const2)
    rowspec = pl.BlockSpec((tm, D), _row)
    in_specs = [rowspec, vec, vec, vec, _resident((N, D))]
    out_specs = [rowspec, pl.BlockSpec((tm, N), _row)]
    out_shape = [jax.ShapeDtypeStruct((T, D), BF), jax.ShapeDtypeStruct((T, N), BF)]
    args = [x, g, sc, sh, wt]
    if rope is not None:
        in_specs += [pl.BlockSpec((tm, 128), _row)] * 2
        out_specs += [pl.BlockSpec((N_KV, 4 * tm, 128), lambda i: (0, i, 0)), pl.BlockSpec((tm, 256), _row)]
        out_shape += [jax.ShapeDtypeStruct((N_KV, 4 * T, 128), BF), jax.ShapeDtypeStruct((T, 256), BF)]
        args += list(rope)
    return pl.pallas_call(
        body, name=name, grid=(T // tm,),
        in_specs=in_specs, out_specs=out_specs, out_shape=out_shape,
        compiler_params=_cp(("parallel",)),
    )(*args)


def _ffn_down_fwd(ab, wd, x, gt, *, tm, name):
    T = x.shape[0]
    tm = _tile(T, tm)

    def body(a_ref, b_ref, wd_ref, x_ref, gt_ref, xo_ref, y_ref):
        y = None
        for c0 in range(0, F, MXU_N):
            cols = pl.ds(c0, MXU_N)
            a = a_ref[:, cols].astype(F32)
            act = (a * _sigmoid(a) * b_ref[:, cols].astype(F32)).astype(BF)
            part = jnp.dot(act, wd_ref[cols, :], preferred_element_type=F32)
            y = part if y is None else y + part
        y_ref[...] = y.astype(BF)
        xo_ref[...] = x_ref[...] + (0.5 * gt_ref[...]) * y

    return pl.pallas_call(
        body, name=name, grid=(T // tm,),
        in_specs=[pl.BlockSpec((tm, F), lambda i: (i, 0)), pl.BlockSpec((tm, F), lambda i: (i, 1)),
                  _resident((F, D)), pl.BlockSpec((tm, D), _row), pl.BlockSpec((1, D), _const2)],
        out_specs=[pl.BlockSpec((tm, D), _row), pl.BlockSpec((tm, D), _row)],
        out_shape=[jax.ShapeDtypeStruct((T, D), F32), jax.ShapeDtypeStruct((T, D), BF)],
        compiler_params=_cp(("parallel",)),
    )(ab, ab, wd, x, gt)


def _ffn_fwd(x, g, sc, sh, gt, wgu, wd, final, *, tm, name):
    T = x.shape[0]
    tm = _tile(T, tm)
    last = final is not None

    def body(x_ref, g_ref, sc_ref, sh_ref, gt_ref, wgu_ref, wd_ref, *rest):
        if last:
            t_ref, gf_ref, h_ref, ab_ref, y_ref, dx_ref, ls_ref, dgf_ref = rest
        else:
            h_ref, ab_ref, y_ref, xo_ref = rest
        xv = x_ref[...]
        r = lax.rsqrt(jnp.mean(xv * xv, axis=-1, keepdims=True) + EPS)
        hb = ((xv * r) * g_ref[...] * (1.0 + sc_ref[...]) + sh_ref[...]).astype(BF)
        h_ref[...] = hb
        y = None
        for c0 in range(0, F, MXU_N):
            a = lax.dot_general(hb, wgu_ref[pl.ds(c0, MXU_N), :], NT, preferred_element_type=F32)
            b = lax.dot_general(hb, wgu_ref[pl.ds(F + c0, MXU_N), :], NT, preferred_element_type=F32)
            ab = a.astype(BF)
            bb = b.astype(BF)
            ab_ref[:, pl.ds(c0, MXU_N)] = ab
            ab_ref[:, pl.ds(F + c0, MXU_N)] = bb
            a = ab.astype(F32)
            act = (a * _sigmoid(a) * bb.astype(F32)).astype(BF)
            part = jnp.dot(act, wd_ref[pl.ds(c0, MXU_N), :], preferred_element_type=F32)
            y = part if y is None else y + part
        y_ref[...] = y.astype(BF)
        xo = xv + (0.5 * gt_ref[...]) * y
        if not last:
            xo_ref[...] = xo
            return

        @pl.when(pl.program_id(0) == 0)
        def _():
            ls_ref[...] = jnp.zeros_like(ls_ref)
            dgf_ref[...] = jnp.zeros_like(dgf_ref)
        gv = gf_ref[...]
        r = lax.rsqrt(jnp.mean(xo * xo, axis=-1, keepdims=True) + EPS)
        xh = xo * r
        e = xh * gv - t_ref[...]
        ls_ref[...] += jnp.sum(e * e, axis=0, keepdims=True)
        dy = e * (1.0 / D)
        dgf_ref[...] += jnp.sum(dy * xh, axis=0, keepdims=True)
        dxh = dy * gv
        dx_ref[...] = r * (dxh - xh * jnp.mean(dxh * xh, axis=-1, keepdims=True))

    vec = pl.BlockSpec((1, D), _const2)
    rowspec = pl.BlockSpec((tm, D), _row)
    in_specs = [rowspec, vec, vec, vec, vec, _resident((2 * F, D)), _resident((F, D))]
    out_specs = [rowspec, pl.BlockSpec((tm, 2 * F), _row), rowspec, rowspec]
    out_shape = [jax.ShapeDtypeStruct((T, D), BF), jax.ShapeDtypeStruct((T, 2 * F), BF),
                 jax.ShapeDtypeStruct((T, D), BF), jax.ShapeDtypeStruct((T, D), F32)]
    args = [x, g, sc, sh, gt, wgu, wd]
    if last:
        in_specs += [rowspec, vec]
        out_specs += [vec, vec]
        out_shape += [jax.ShapeDtypeStruct((1, D), F32)] * 2
        args += list(final)
    return pl.pallas_call(
        body, name=name, grid=(T // tm,),
        in_specs=in_specs, out_specs=out_specs, out_shape=out_shape,
        compiler_params=_cp(("arbitrary",) if last else ("parallel",)),
    )(*args)


def _ffn_down_bwd_dw(dxo, y, gt, ab, wd, *, tm, name):
    T = dxo.shape[0]
    tm = _tile(T, tm)
    nt = T // tm
    hw = F // 2
    chunks = [(c0, min(MXU_N, hw - c0)) for c0 in range(0, hw, MXU_N)]

    def body(dxo_ref, y_ref, gt_ref, a_ref, b_ref, wd_ref, dab_ref, dgt_ref, dwd_ref, dys, dyt, acc, stage, sem):
        i, j = pl.program_id(0), pl.program_id(1)

        @pl.when(jnp.logical_and(i == 0, j == 0))
        def _():
            dgt_ref[...] = jnp.zeros_like(dgt_ref)

        @pl.when(j == 0)
        def _():
            dxv = dxo_ref[...]
            dgt_ref[...] += 0.5 * jnp.sum(dxv * y_ref[...].astype(F32), axis=0, keepdims=True)
            dyf = (0.5 * gt_ref[...]) * dxv
            dys[...] = dyf.astype(BF)
            dyt[...] = dyf.T.astype(BF)

        def half(jj):
            @pl.when(i == 0)
            def _():
                acc[jj] = jnp.zeros((D, hw), F32)

            dy = dys[...]
            dy_t = dyt[...]
            for c0, cw in chunks:
                cols = pl.ds(c0, cw)
                dact = lax.dot_general(dy, wd_ref[pl.ds(jj * hw + c0, cw), :], NT, preferred_element_type=F32)
                a = a_ref[:, cols].astype(F32)
                b = b_ref[:, cols].astype(F32)
                s = _sigmoid(a)
                silu = a * s
                dab_ref[0, :, cols] = (dact * b * (s * (1.0 + a * (1.0 - s)))).astype(BF)
                dab_ref[1, :, cols] = (dact * silu).astype(BF)
                acc[jj, :, cols] += jnp.dot(dy_t, (silu * b).astype(BF), preferred_element_type=F32)

            @pl.when(i == nt - 1)
            def _():
                for c0, cw in chunks:
                    stage[0:cw, :] = acc[jj, :, pl.ds(c0, cw)].T.astype(BF)
                    out = pltpu.make_async_copy(stage.at[pl.ds(0, cw)], dwd_ref.at[pl.ds(jj * hw + c0, cw)], sem)
                    out.start()
                    out.wait()

        for jj in range(2):
            pl.when(j == jj)(lambda jj=jj: half(jj))

    vec = pl.BlockSpec((1, D), _const2)
    rowspec = pl.BlockSpec((tm, D), lambda i, j: (i, 0))
    return pl.pallas_call(
        body, name=name, grid=(nt, 2),
        in_specs=[rowspec, rowspec, vec, pl.BlockSpec((tm, hw), lambda i, j: (i, j)),
                  pl.BlockSpec((tm, hw), lambda i, j: (i, j + 2)), _resident((F, D))],
        out_specs=[pl.BlockSpec((2, tm, hw), lambda i, j: (0, i, j)), vec, pl.BlockSpec(memory_space=pl.ANY)],
        out_shape=[jax.ShapeDtypeStruct((2, T, F), BF), jax.ShapeDtypeStruct((1, D), F32),
                   jax.ShapeDtypeStruct((F, D), BF)],
        scratch_shapes=[pltpu.VMEM((tm, D), BF), pltpu.VMEM((D, tm), BF), pltpu.VMEM((2, D, hw), F32),
                        pltpu.VMEM((MXU_N, D), BF), pltpu.SemaphoreType.DMA(())],
        compiler_params=_cp(("arbitrary", "arbitrary")),
    )(dxo, y, gt, ab, ab, wd)


def _tn_matmul(a, b, token=None, *, tn, tk, name):
    S, T, Ns = a.shape
    tn, tk = _tile(Ns, tn), _tile(T, tk)
    nk, njs = T // tk, Ns // tn
    deps = [] if token is None else [token]

    def body(a_ref, b_ref, *rest):
        o_ref, acc = rest[len(deps):]
        k = pl.program_id(1)

        @pl.when(k == 0)
        def _():
            acc[...] = jnp.zeros_like(acc)
        acc[...] += lax.dot_general(a_ref[0], b_ref[...], TN, preferred_element_type=F32)

        @pl.when(k == nk - 1)
        def _():
            o_ref[...] = acc[...].astype(BF)

    return pl.pallas_call(
        body, name=name, grid=(S * njs, nk),
        in_specs=[pl.BlockSpec((1, tk, tn), lambda j, k: (j // njs, k, j % njs)),
                  pl.BlockSpec((tk, D), lambda j, k: (k, 0))] + [pl.BlockSpec(memory_space=pl.ANY)] * len(deps),
        out_specs=pl.BlockSpec((tn, D), lambda j, k: (j, 0)),
        out_shape=jax.ShapeDtypeStruct((S * Ns, D), BF),
        scratch_shapes=[pltpu.VMEM((tn, D), F32)],
        compiler_params=_cp(("parallel", "arbitrary")),
    )(a, b, *deps)


def _nn_bwd_norm(da, w, x, g, sc, dxo, *, tm, name):
    S, T, Ks = da.shape
    tm = _tile(T, tm)
    rc = _tile(tm, 256)

    def body(da_ref, w_ref, x_ref, g_ref, sc_ref, dxo_ref, dx_ref, dsh_ref, dsc_ref, dg_ref, acc):
        @pl.when(pl.program_id(0) == 0)
        def _():
            dsh_ref[...] = jnp.zeros_like(dsh_ref)
            dsc_ref[...] = jnp.zeros_like(dsc_ref)
            dg_ref[...] = jnp.zeros_like(dg_ref)

        d = jnp.dot(da_ref[0], w_ref[0:Ks, :], preferred_element_type=F32)
        for s in range(1, S):
            d = d + jnp.dot(da_ref[s], w_ref[s * Ks:(s + 1) * Ks, :], preferred_element_type=F32)
        acc[...] = d
        gv = g_ref[...]
        sc1 = 1.0 + sc_ref[...]
        dsh = jnp.zeros((1, D), F32)
        dsc = jnp.zeros((1, D), F32)
        dg = jnp.zeros((1, D), F32)
        for r0 in range(0, tm, rc):
            rows = pl.ds(r0, rc)
            u = acc[rows, :]
            xv = x_ref[rows, :]
            r = lax.rsqrt(jnp.mean(xv * xv, axis=-1, keepdims=True) + EPS)
            xh = xv * r
            dsh = dsh + jnp.sum(u, axis=0, keepdims=True)
            dsc = dsc + jnp.sum(u * (xh * gv), axis=0, keepdims=True)
            us = u * sc1
            dg = dg + jnp.sum(us * xh, axis=0, keepdims=True)
            dxh = us * gv
            dx_ref[rows, :] = dxo_ref[rows, :] + r * (dxh - xh * jnp.mean(dxh * xh, axis=-1, keepdims=True))
        dsh_ref[...] += dsh
        dsc_ref[...] += dsc
        dg_ref[...] += dg

    vec = pl.BlockSpec((1, D), _const2)
    rowspec = pl.BlockSpec((tm, D), _row)
    return pl.pallas_call(
        body, name=name, grid=(T // tm,),
        in_specs=[pl.BlockSpec((S, tm, Ks), lambda i: (0, i, 0)), _resident((S * Ks, D)), rowspec, vec, vec, rowspec],
        out_specs=[rowspec, vec, vec, vec],
        out_shape=[jax.ShapeDtypeStruct((T, D), F32)] + [jax.ShapeDtypeStruct((1, D), F32)] * 3,
        scratch_shapes=[pltpu.VMEM((tm, D), F32)],
        compiler_params=_cp(("arbitrary",)),
    )(da, w, x, g, sc, dxo)


def _rope(t, cos, sin_signed, lt32, inverse=False):
    sel = jnp.where(lt32, pltpu.roll(t, 96, 1), pltpu.roll(t, 32, 1))
    return t * cos - sel * sin_signed if inverse else t * cos + sel * sin_signed


def _rope_tables(T, token=None):
    inv = 1.0 / (ROPE_THETA ** (jnp.arange(0, HEAD_DIM, 2, dtype=F32) / HEAD_DIM))
    ang = _behind(jnp.arange(T, dtype=F32)[:, None] * inv[None, :], token)
    cos, sin = jnp.cos(ang), jnp.sin(ang)
    cos128 = jnp.tile(cos, (1, 4))
    sin128 = jnp.tile(jnp.concatenate([-sin, sin], axis=1), (1, 2))
    return cos128, sin128


QSCALE = HEAD_DIM ** -0.5


def _lane_masks(rows):
    lane = lax.broadcasted_iota(jnp.int32, (rows, 128), 1)
    return (lane % HEAD_DIM) < (HEAD_DIM // 2), [lane < HEAD_DIM, lane >= HEAD_DIM]


def _attn_bias():
    qi = lax.broadcasted_iota(jnp.int32, (4 * BLK, 2 * BLK), 0) % BLK
    kj = lax.broadcasted_iota(jnp.int32, (4 * BLK, 2 * BLK), 1)
    band = (kj > qi) & (kj <= qi + BLK)
    return jnp.stack([jnp.where(band & (kj >= BLK), 0.0, NEG_INF), jnp.where(band, 0.0, NEG_INF)]).astype(F32)


def _attn_prep_tile(proj_ref, c_ref, s_ref, qs_ref, kr_ref, tm):
    lt32, halves = _lane_masks(BLK)
    for b in range(tm // BLK):
        rows = pl.ds(b * BLK, BLK)
        cc, sc = c_ref[rows, :], s_ref[rows, :]
        qr = [_rope(proj_ref[rows, pl.ds(O_Q + p * 128, 128)].astype(F32), cc, sc, lt32) * QSCALE for p in range(8)]
        for g in range(N_KV):
            qs_ref[g, pl.ds(4 * b * BLK, 4 * BLK), :] = _stack_heads(qr, g, halves).astype(BF)
        kr_ref[rows, :] = jnp.concatenate([_rope(proj_ref[rows, pl.ds(O_K + r * 128, 128)].astype(F32), cc, sc, lt32)
                                           for r in range(2)], axis=1).astype(BF)


ATT_BPS = 2
ATT_ROWS = ATT_BPS * BLK


def _before(n):
    return jnp.maximum(ATT_BPS * n - 1, 0)


def _attn_specs():
    return [pl.BlockSpec((N_KV, 4 * ATT_ROWS, 128), lambda n: (0, n, 0)),
            pl.BlockSpec((ATT_ROWS, 256), _row), pl.BlockSpec((BLK, 256), lambda n: (_before(n), 0)),
            pl.BlockSpec((ATT_ROWS, 256), lambda n: (n, O_V // 256)),
            pl.BlockSpec((BLK, 256), lambda n: (_before(n), O_V // 256)),
            pl.BlockSpec((2, 4 * BLK, 2 * BLK), lambda n: (0, 0, 0)),
            pl.BlockSpec(memory_space=pltpu.SMEM)]


def _bands(sb, kc_ref, kp_ref, vc_ref, vp_ref):
    own = pl.ds(sb * BLK, BLK)
    above = pl.ds((sb - 1) * BLK, BLK)
    kb, vb = [], []
    for r in range(2):
        cols = pl.ds(r * 128, 128)
        kprev = kp_ref[:, cols] if sb == 0 else kc_ref[above, cols]
        vprev = vp_ref[:, cols] if sb == 0 else vc_ref[above, cols]
        kb.append(jnp.concatenate([kprev, kc_ref[own, cols]], axis=0))
        vb.append(jnp.concatenate([vprev, vc_ref[own, cols]], axis=0))
    return kb, vb


def _block_bias(sb, bias_ref):
    return bias_ref[jnp.minimum(pl.program_id(0), 1)] if sb == 0 else bias_ref[1]


def _sink_rows(sink_ref, g):
    return jnp.concatenate([jnp.full((BLK, 128), sink_ref[4 * g + hh], F32) for hh in range(4)], axis=0)


def _both(t):
    return jnp.concatenate([t, t], axis=1)


def _unstack_heads(t, g, halves, acc):
    half = g % 2
    for hh in range(4):
        h = 4 * g + hh
        th = jnp.where(halves[half], t[hh * BLK:(hh + 1) * BLK], 0.0)
        if h % 2 != half:
            th = pltpu.roll(th, HEAD_DIM, 1)
        acc[h // 2] = acc[h // 2] + th


def _stack_heads(chunks, g, halves):
    half = g % 2
    parts = []
    for hh in range(4):
        h = 4 * g + hh
        t = chunks[h // 2]
        if h % 2 != half:
            t = pltpu.roll(t, HEAD_DIM, 1)
        parts.append(jnp.where(halves[half], t, 0.0))
    return jnp.concatenate(parts, axis=0)


def _attn_fwd(qs, kr, proj, bias, sinks, *, name):
    T = proj.shape[0]
    assert T % ATT_ROWS == 0

    def body(qs_ref, kc_ref, kp_ref, vc_ref, vp_ref, bias_ref, sink_ref, o_ref, lse_ref):
        _, h128 = _lane_masks(BLK)
        _, h256 = _lane_masks(2 * BLK)
        _, h512 = _lane_masks(4 * BLK)
        groups = range(N_KV)
        sink = [_sink_rows(sink_ref, g) for g in groups]
        for sb in range(ATT_BPS):
            rows = pl.ds(4 * sb * BLK, 4 * BLK)
            kb, vb = _bands(sb, kc_ref, kp_ref, vc_ref, vp_ref)
            outs = [jnp.zeros((BLK, 128), F32) for _ in range(8)]
            bias = _block_bias(sb, bias_ref)
            s = [lax.dot_general(qs_ref[g, rows, :], kb[g // 2], NT, preferred_element_type=F32) + bias for g in groups]
            m = [jnp.maximum(jnp.broadcast_to(jnp.max(s[g], axis=-1, keepdims=True), (4 * BLK, 128)), sink[g])
                 for g in groups]
            p = [jnp.exp(s[g] - _both(m[g])).astype(BF) for g in groups]
            vg = [jnp.where(h256[g % 2], vb[g // 2].astype(F32), 1.0).astype(BF) for g in groups]
            o = [jnp.dot(p[g], vg[g], preferred_element_type=F32) for g in groups]
            denom = [jnp.where(h512[g % 2], pltpu.roll(o[g], HEAD_DIM, 1), o[g]) + jnp.exp(sink[g] - m[g])
                     for g in groups]
            for g in groups:
                lse_ref[g, rows, :] = m[g] + jnp.log(denom[g])
                _unstack_heads(o[g] * (1.0 / denom[g]), g, h128, outs)
            o_ref[pl.ds(sb * BLK, BLK), :] = jnp.concatenate(outs, axis=1).astype(BF)

    return pl.pallas_call(
        body, name=name, grid=(T // ATT_ROWS,),
        in_specs=_attn_specs(),
        out_specs=[pl.BlockSpec((ATT_ROWS, D), _row), pl.BlockSpec((N_KV, 4 * ATT_ROWS, 128), lambda n: (0, n, 0))],
        out_shape=[jax.ShapeDtypeStruct((T, D), BF), jax.ShapeDtypeStruct((N_KV, 4 * T, 128), F32)],
        compiler_params=_cp(("parallel",)),
    )(qs, kr, kr, proj, proj, bias, sinks)


def _attn_bwd(qs, kr, proj, bias, sinks, lse, o, do, cos, sin, dproj, *, name):
    T = proj.shape[0]
    assert T % ATT_ROWS == 0

    def body(qs_ref, kc_ref, kp_ref, vc_ref, vp_ref, bias_ref, sink_ref, lse_ref, o_ref, do_ref,
             cc_ref, sc_ref, cp_ref, sp_ref, dproj_ref, dq_ref, dkc_ref, dkp_ref, dvc_ref, dvp_ref, dsink_ref):
        @pl.when(pl.program_id(0) == 0)
        def _():
            dsink_ref[...] = jnp.zeros_like(dsink_ref)
        lt32, h128 = _lane_masks(BLK)
        lane1 = lax.broadcasted_iota(jnp.int32, (1, 128), 1)
        dsink = jnp.zeros((1, 128), F32)
        groups = range(N_KV)
        for sb in range(ATT_BPS):
            own = pl.ds(sb * BLK, BLK)
            rows = pl.ds(4 * sb * BLK, 4 * BLK)
            kb, vb = _bands(sb, kc_ref, kp_ref, vc_ref, vp_ref)
            oc = [o_ref[own, pl.ds(p * 128, 128)].astype(F32) for p in range(8)]
            doc = [do_ref[own, pl.ds(p * 128, 128)].astype(F32) for p in range(8)]
            dqs = [jnp.zeros((BLK, 128), F32) for _ in range(8)]
            bias = _block_bias(sb, bias_ref)
            q = [qs_ref[g, rows, :] for g in groups]
            lse_g = [lse_ref[g, rows, :] for g in groups]
            s = [lax.dot_general(q[g], kb[g // 2], NT, preferred_element_type=F32) + bias for g in groups]
            dos = [_stack_heads(doc, g, h128) for g in groups]
            dosb = [t.astype(BF) for t in dos]
            dp = [lax.dot_general(dosb[g], vb[g // 2], NT, preferred_element_type=F32) for g in groups]
            delta = [jnp.broadcast_to(jnp.sum(dos[g] * _stack_heads(oc, g, h128), axis=-1, keepdims=True),
                                      (4 * BLK, 128)) for g in groups]
            p = [jnp.exp(s[g] - _both(lse_g[g])) for g in groups]
            ds = [(p[g] * (dp[g] - _both(delta[g]))).astype(BF) for g in groups]
            pb = [t.astype(BF) for t in p]
            dvg = [lax.dot_general(pb[g], dosb[g], TN, preferred_element_type=F32) for g in groups]
            dkg = [lax.dot_general(ds[g], q[g], TN, preferred_element_type=F32) for g in groups]
            dqg = [jnp.dot(ds[g], kb[g // 2], preferred_element_type=F32) * QSCALE for g in groups]
            dvr = [dvg[0] + dvg[1], dvg[2] + dvg[3]]
            dkr = [dkg[0] + dkg[1], dkg[2] + dkg[3]]
            for g in groups:
                _unstack_heads(dqg[g], g, h128, dqs)
                dsk = -jnp.exp(_sink_rows(sink_ref, g) - lse_g[g]) * delta[g]
                for hh in range(4):
                    val = jnp.sum(dsk[hh * BLK:(hh + 1) * BLK], axis=0, keepdims=True)
                    dsink = dsink + jnp.where(lane1 == 4 * g + hh, val, 0.0)
            cc, sc = cc_ref[own, :], sc_ref[own, :]
            cp, sp = (cp_ref[...], sp_ref[...]) if sb == 0 else (cc_ref[pl.ds((sb - 1) * BLK, BLK), :],
                                                                  sc_ref[pl.ds((sb - 1) * BLK, BLK), :])
            dq_ref[own, :] = jnp.concatenate([_rope(t, cc, sc, lt32, inverse=True) for t in dqs], axis=1).astype(BF)
            dkp_ref[own, :] = jnp.concatenate([_rope(t[:BLK], cp, sp, lt32, inverse=True) for t in dkr], axis=1)
            dkc_ref[own, :] = jnp.concatenate([_rope(t[BLK:], cc, sc, lt32, inverse=True) for t in dkr], axis=1)
            dvp_ref[own, :] = jnp.concatenate([t[:BLK] for t in dvr], axis=1)
            dvc_ref[own, :] = jnp.concatenate([t[BLK:] for t in dvr], axis=1)
        dsink_ref[...] += dsink

    kv = pl.BlockSpec((ATT_ROWS, 256), _row)
    tc = pl.BlockSpec((ATT_ROWS, 128), _row)
    tp = pl.BlockSpec((BLK, 128), lambda n: (_before(n), 0))
    return pl.pallas_call(
        body, name=name, grid=(T // ATT_ROWS,),
        in_specs=_attn_specs() + [pl.BlockSpec((N_KV, 4 * ATT_ROWS, 128), lambda n: (0, n, 0)),
                                  pl.BlockSpec((ATT_ROWS, D), _row), pl.BlockSpec((ATT_ROWS, D), _row), tc, tc, tp, tp,
                                  pl.BlockSpec(memory_space=pl.ANY)],
        out_specs=[pl.BlockSpec((ATT_ROWS, D), lambda n: (n, O_Q // D)), kv, kv, kv, kv,
                   pl.BlockSpec((1, 128), _const2)],
        out_shape=[jax.ShapeDtypeStruct(dproj.shape, BF)] + [jax.ShapeDtypeStruct((T, 256), F32)] * 4
        + [jax.ShapeDtypeStruct((1, 128), F32)],
        input_output_aliases={14: 0},
        compiler_params=_cp(("arbitrary",)),
    )(qs, kr, kr, proj, proj, bias, sinks, lse, o, do, cos, sin, cos, sin, dproj)


def _dkv_combine(dkc, dkp, dvc, dvp, dproj, *, name):
    T = dkc.shape[0]
    nb = T // BLK
    tm = _tile(T, 4 * BLK)
    bpt = tm // BLK
    nt = T // tm

    def body(dkc_ref, dkp_ref, dkn_ref, dvc_ref, dvp_ref, dvn_ref, dproj_ref, o_ref):
        keep = jnp.where(pl.program_id(0) == nt - 1, 0.0, 1.0)

        def shifted(prev_ref, next_ref):
            nxt = keep * next_ref[...]
            return nxt if bpt == 1 else jnp.concatenate([prev_ref[BLK:, :], nxt], axis=0)

        o_ref[:, 0:256] = (dkc_ref[...] + shifted(dkp_ref, dkn_ref)).astype(BF)
        o_ref[:, 256:512] = (dvc_ref[...] + shifted(dvp_ref, dvn_ref)).astype(BF)

    cur = pl.BlockSpec((tm, 256), _row)
    nxt = pl.BlockSpec((BLK, 256), lambda i: (jnp.minimum((i + 1) * bpt, nb - 1), 0))
    return pl.pallas_call(
        body, name=name, grid=(nt,),
        in_specs=[cur, cur, nxt, cur, cur, nxt, pl.BlockSpec(memory_space=pl.ANY)],
        out_specs=pl.BlockSpec((tm, 512), lambda i: (i, O_K // 512)),
        out_shape=jax.ShapeDtypeStruct(dproj.shape, BF),
        input_output_aliases={6: 0},
        compiler_params=_cp(("parallel",)),
    )(dkc, dkp, dkp, dvc, dvp, dvp, dproj)


HALO = 16


def _conv_shifts(cu, hprev, tm):
    row = lax.broadcasted_iota(jnp.int32, (8, cu.shape[1]), 0)
    h1 = hprev[HALO - 1:HALO, :]
    h2 = hprev[HALO - 2:HALO - 1, :]
    m1 = pltpu.roll(cu, 1, 0)
    m2 = pltpu.roll(cu, 2, 0)
    m1 = jnp.concatenate([jnp.where(row == 0, h1, m1[0:8]), m1[8:]], axis=0)
    m2 = jnp.concatenate([jnp.where(row == 0, h2, jnp.where(row == 1, h1, m2[0:8])), m2[8:]], axis=0)
    return m1, m2


def _mixer_mid_fwd(proj, attn, wcp, wap, wout, convw, x, gt, *, tm, name):
    T = x.shape[0]
    tm = _tile(T, tm)
    hb = tm // HALO

    def body(bg_ref, cg_ref, u_ref, hcg_ref, hu_ref, zc0_ref, zc1_ref, za0_ref, za1_ref, at_ref,
             wcp_ref, wap_ref, wout_ref, cw_ref, x_ref, gt_ref,
             x2_ref, gc_ref, yc_ref, ya_ref, mg_ref, o_ref):
        first = jnp.where(pl.program_id(0) == 0, 0.0, 1.0)
        cu = cg_ref[...].astype(F32) * u_ref[...].astype(F32)
        hprev = first * (hcg_ref[...].astype(F32) * hu_ref[...].astype(F32))
        m1, m2 = _conv_shifts(cu, hprev, tm)
        cv = cw_ref[0:1, :] * m2 + cw_ref[1:2, :] * m1 + cw_ref[2:3, :] * cu
        gc = (bg_ref[...].astype(F32) * cv).astype(BF)
        gc_ref[...] = gc
        yc = jnp.dot(gc, wcp_ref[...], preferred_element_type=F32)
        ya = jnp.dot(at_ref[...], wap_ref[...], preferred_element_type=F32)
        yc_ref[...] = yc.astype(BF)
        ya_ref[...] = ya.astype(BF)
        zc = jnp.concatenate([zc0_ref[...], zc1_ref[...]], axis=1).astype(F32)
        za = jnp.concatenate([za0_ref[...], za1_ref[...]], axis=1).astype(F32)
        mg = (_sigmoid(zc) * yc + _sigmoid(za) * ya).astype(BF)
        mg_ref[...] = mg
        o = jnp.dot(mg, wout_ref[...], preferred_element_type=F32)
        o_ref[...] = o.astype(BF)
        x2_ref[...] = x_ref[...] + gt_ref[...] * o

    wspec = pl.BlockSpec((D, D), _const2)
    rowspec = pl.BlockSpec((tm, D), _row)
    return pl.pallas_call(
        body, name=name, grid=(T // tm,),
        in_specs=[_col(tm, O_BG), _col(tm, O_CG), _col(tm, O_U), _halo_prev(hb, O_CG), _halo_prev(hb, O_U),
                  _col(tm, O_ZC, 512), _col(tm, O_ZC + 512, 512), _col(tm, O_ZA, 512), _col(tm, O_ZA + 512, 512),
                  rowspec, wspec, wspec, wspec, pl.BlockSpec((8, D), _const2), rowspec, pl.BlockSpec((1, D), _const2)],
        out_specs=[rowspec] * 6,
        out_shape=[jax.ShapeDtypeStruct((T, D), F32)] + [jax.ShapeDtypeStruct((T, D), BF)] * 5,
        compiler_params=_cp(("parallel",)),
    )(proj, proj, proj, proj, proj, proj, proj, proj, proj, attn, wcp, wap, wout, convw, x, gt)


def _col(tm, c, w=D):
    assert c % w == 0
    return pl.BlockSpec((tm, w), lambda i: (i, c // w))


def _halo_prev(hb, c):
    return pl.BlockSpec((HALO, D), lambda i: (jnp.maximum(i * hb - 1, 0), c // D))


def _halo_next(hb, nblk, c=0):
    return pl.BlockSpec((HALO, D), lambda i: (jnp.minimum((i + 1) * hb, nblk - 1), c // D))


def _mixer_mid_bwd(dx2, gt, o, proj, yc, ya, wout, wcp, wap, *, tm, name):
    T = dx2.shape[0]
    tm = _tile(T, tm)
    nt = T // tm

    def body(dx_ref, gt_ref, o_ref, zc0_ref, zc1_ref, za0_ref, za1_ref, yc_ref, ya_ref, wout_ref, wcp_ref, wap_ref,
             dout_ref, dyc_ref, dya_ref, dgc_ref, dat_ref, dproj_ref, dgt_ref, dzs, sems):
        i = pl.program_id(0)
        slot = lax.rem(i, 2)

        def slab_copy(step, s):
            return pltpu.make_async_copy(
                dzs.at[s], dproj_ref.at[pl.ds(pl.multiple_of(step * tm, tm), tm), pl.ds(O_ZC, 2 * D)], sems.at[s])

        @pl.when(i == 0)
        def _():
            dgt_ref[...] = jnp.zeros_like(dgt_ref)

        dxv = dx_ref[...]
        dgt_ref[...] += jnp.sum(dxv * o_ref[...].astype(F32), axis=0, keepdims=True)
        dout = (gt_ref[...] * dxv).astype(BF)
        dout_ref[...] = dout
        dmg = lax.dot_general(dout, wout_ref[...], NT, preferred_element_type=F32)
        sc = _sigmoid(jnp.concatenate([zc0_ref[...], zc1_ref[...]], axis=1).astype(F32))
        sa = _sigmoid(jnp.concatenate([za0_ref[...], za1_ref[...]], axis=1).astype(F32))
        dyc = (dmg * sc).astype(BF)
        dya = (dmg * sa).astype(BF)
        dyc_ref[...] = dyc
        dya_ref[...] = dya
        dzs[slot, :, 0:D] = (dmg * yc_ref[...].astype(F32) * (sc * (1.0 - sc))).astype(BF)
        dzs[slot, :, D:2 * D] = (dmg * ya_ref[...].astype(F32) * (sa * (1.0 - sa))).astype(BF)
        slab_copy(i, slot).start()
        dgc_ref[...] = lax.dot_general(dyc, wcp_ref[...], NT, preferred_element_type=F32).astype(BF)
        dat_ref[...] = lax.dot_general(dya, wap_ref[...], NT, preferred_element_type=F32).astype(BF)

        @pl.when(i > 0)
        def _():
            slab_copy(i - 1, 1 - slot).wait()

        @pl.when(i == nt - 1)
        def _():
            slab_copy(i, slot).wait()

    def zcol(c):
        return pl.BlockSpec((tm, 512), lambda i: (i, c // 512))

    wspec = pl.BlockSpec((D, D), _const2)
    rowspec = pl.BlockSpec((tm, D), _row)
    vec = pl.BlockSpec((1, D), _const2)
    return pl.pallas_call(
        body, name=name, grid=(nt,),
        in_specs=[rowspec, vec, rowspec, zcol(O_ZC), zcol(O_ZC + 512), zcol(O_ZA), zcol(O_ZA + 512),
                  rowspec, rowspec, wspec, wspec, wspec],
        out_specs=[rowspec] * 5 + [pl.BlockSpec(memory_space=pl.ANY), vec],
        out_shape=[jax.ShapeDtypeStruct((T, D), BF)] * 5 + [jax.ShapeDtypeStruct((T, NIN), BF),
                                                            jax.ShapeDtypeStruct((1, D), F32)],
        scratch_shapes=[pltpu.VMEM((2, tm, 2 * D), BF), pltpu.SemaphoreType.DMA((2,))],
        compiler_params=_cp(("arbitrary",)),
    )(dx2, gt, o, proj, proj, proj, proj, yc, ya, wout, wcp, wap)


def _conv_bwd(dgc, proj, convw, dproj, *, tm, name):
    T = dgc.shape[0]
    tm = _tile(T, tm)
    hb = tm // HALO
    nblk = T // HALO
    nt = T // tm

    def body(dgc_ref, ndgc_ref, bg_ref, nbg_ref, cg_ref, u_ref, hcg_ref, hu_ref, cw_ref, dproj_ref, dp_ref, dcw_ref):
        i = pl.program_id(0)

        @pl.when(i == 0)
        def _():
            dcw_ref[...] = jnp.zeros_like(dcw_ref)
        first = jnp.where(i == 0, 0.0, 1.0)
        last = jnp.where(i == nt - 1, 0.0, 1.0)
        cg = cg_ref[...].astype(F32)
        u = u_ref[...].astype(F32)
        bg = bg_ref[...].astype(F32)
        dg = dgc_ref[...].astype(F32)
        cu = cg * u
        hprev = first * (hcg_ref[...].astype(F32) * hu_ref[...].astype(F32))
        m1, m2 = _conv_shifts(cu, hprev, tm)
        w0, w1, w2 = cw_ref[0:1, :], cw_ref[1:2, :], cw_ref[2:3, :]
        cv = w0 * m2 + w1 * m1 + w2 * cu
        dcv = dg * bg
        nxt = last * (ndgc_ref[...].astype(F32) * nbg_ref[...].astype(F32))
        n0, n1 = nxt[0:1, :], nxt[1:2, :]
        row = lax.broadcasted_iota(jnp.int32, (8, D), 0)
        p1 = pltpu.roll(dcv, tm - 1, 0)
        p2 = pltpu.roll(dcv, tm - 2, 0)
        p1 = jnp.concatenate([p1[:tm - 8], jnp.where(row == 7, n0, p1[tm - 8:])], axis=0)
        p2 = jnp.concatenate([p2[:tm - 8], jnp.where(row == 7, n1, jnp.where(row == 6, n0, p2[tm - 8:]))], axis=0)
        dcu = w2 * dcv + w1 * p1 + w0 * p2
        dp_ref[:, 0:D] = (dg * cv).astype(BF)
        dp_ref[:, D:2 * D] = (dcu * u).astype(BF)
        dp_ref[:, 2 * D:3 * D] = (dcu * cg).astype(BF)
        dcw_ref[0:1, :] += jnp.sum(dcv * m2, axis=0, keepdims=True)
        dcw_ref[1:2, :] += jnp.sum(dcv * m1, axis=0, keepdims=True)
        dcw_ref[2:3, :] += jnp.sum(dcv * cu, axis=0, keepdims=True)

    rowspec = pl.BlockSpec((tm, D), _row)
    cw = pl.BlockSpec((8, D), _const2)
    return pl.pallas_call(
        body, name=name, grid=(nt,),
        in_specs=[rowspec, _halo_next(hb, nblk), _col(tm, O_BG), _halo_next(hb, nblk, O_BG),
                  _col(tm, O_CG), _col(tm, O_U), _halo_prev(hb, O_CG), _halo_prev(hb, O_U), cw,
                  pl.BlockSpec(memory_space=pl.ANY)],
        out_specs=[pl.BlockSpec((tm, 3 * D), _row), cw],
        out_shape=[jax.ShapeDtypeStruct(dproj.shape, BF), jax.ShapeDtypeStruct((8, D), F32)],
        input_output_aliases={9: 0},
        compiler_params=_cp(("arbitrary",)),
    )(dgc, dgc, proj, proj, proj, proj, proj, proj, convw, dproj)


def _adam_math(w, g, m, v):
    nm = ADAM_B1 * m + (1.0 - ADAM_B1) * g
    nv = ADAM_B2 * v + (1.0 - ADAM_B2) * (g * g)
    m_hat = nm / (1.0 - ADAM_B1 ** ADAM_STEP)
    v_hat = nv / (1.0 - ADAM_B2 ** ADAM_STEP)
    return -ADAM_LR * (m_hat / (jnp.sqrt(v_hat) + ADAM_EPS) + ADAM_WD * w), nm, nv


SMALL = ("b_ada", "g_ffn1", "g_mix", "g_ffn2", "g_final", "conv_w", "sinks")


def _adam_small(gsum, conv_g, w, m, v, *, name):
    nsm = len(SMALL)

    def body(*refs):
        gs_ref, cg_ref = refs[0], refs[1]
        w_refs, m_refs, v_refs = (refs[2 + k * nsm:2 + (k + 1) * nsm] for k in range(3))
        outs = refs[2 + 3 * nsm:]
        for p, n in enumerate(SMALL):
            if n == "b_ada":
                pieces = [(slice(None), slice(r * D, (r + 1) * D), gs_ref[R_MODS + r:R_MODS + r + 1, :])
                          for r in range(N_MOD)]
            elif n == "conv_w":
                pieces = [(slice(None), slice(None), cg_ref[...])]
            elif n == "sinks":
                pieces = [(slice(None), slice(None), gs_ref[R_SINK:R_SINK + 1, 0:N_HEADS])]
            else:
                row = dict(g_ffn1=R_G1, g_mix=R_GM, g_ffn2=R_G2, g_final=R_GF)[n]
                pieces = [(slice(None), slice(None), gs_ref[row:row + 1, :])]
            for rs, cs, g in pieces:
                d, nm, nv = _adam_math(w_refs[p][rs, cs], g, m_refs[p][rs, cs], v_refs[p][rs, cs])
                for k, val in enumerate((g, d, nm, nv)):
                    outs[k * nsm + p][rs, cs] = val

    args = [gsum, conv_g] + [d[n] for d in (w, m, v) for n in SMALL]
    shapes = [jax.ShapeDtypeStruct(w[n].shape, F32) for _ in range(4) for n in SMALL]
    res = pl.pallas_call(body, name=name, out_shape=shapes, compiler_params=_cp())(*args)
    return [dict(zip(SMALL, res[k * nsm:(k + 1) * nsm])) for k in range(4)]


def _adam(w, g, m, v, *, tm, name):
    _, R, C = w.shape
    tm = _tile(R, tm)
    parts = g.ndim == 3

    def body(w_ref, g_ref, m_ref, v_ref, go_ref, d_ref, nm_ref, nv_ref):
        if parts:
            gv = g_ref[0].astype(F32)
            for s in range(1, N_DEV):
                gv = gv + g_ref[s].astype(F32)
        else:
            gv = g_ref[...]
        go_ref[0] = gv
        d_ref[0], nm_ref[0], nv_ref[0] = _adam_math(w_ref[0], gv, m_ref[0], v_ref[0])

    spec = pl.BlockSpec((1, tm, C), lambda i: (0, i, 0))
    gspec = pl.BlockSpec((N_DEV, tm, C), lambda i: (0, i, 0)) if parts else pl.BlockSpec((tm, C), _row)
    return pl.pallas_call(
        body, name=name, grid=(R // tm,),
        in_specs=[spec, gspec, spec, spec], out_specs=[spec] * 4,
        out_shape=[jax.ShapeDtypeStruct((1, R, C), F32)] * 4,
        compiler_params=_cp(("parallel",)),
    )(w, g, m, v)


def _mods_part(c_all, w_ada, b_ada, *, name):
    C = w_ada.shape[1]

    def body(c_ref, w_ref, b_ref, o_ref):
        cv = c_ref[...]
        ca = cv * jax.nn.sigmoid(cv)
        o_ref[...] = jnp.dot(ca, w_ref[...], preferred_element_type=F32,
                             precision=lax.Precision.HIGHEST) + b_ref[...]

    return pl.pallas_call(
        body, name=name,
        out_shape=jax.ShapeDtypeStruct((N_DEV, C), F32),
        compiler_params=_cp(),
    )(c_all, w_ada, b_ada)


def _wada_grad(c_all_t, gm, *, name):
    C = gm.shape[1]

    def body(c_ref, g_ref, o_ref):
        cv = c_ref[...]
        ca = cv * jax.nn.sigmoid(cv)
        acc = ca[:, 0:1] * g_ref[0:1, :]
        for b in range(1, N_DEV):
            acc = acc + ca[:, b:b + 1] * g_ref[b:b + 1, :]
        o_ref[...] = acc

    return pl.pallas_call(
        body, name=name,
        out_shape=jax.ShapeDtypeStruct((D, C), F32),
        compiler_params=_cp(),
    )(c_all_t, gm)


def _peer(x, y, c, d):
    px = lax.rem(x + ((d >> 2) & 1), 2)
    py = lax.rem(y + ((d >> 1) & 1), 2)
    pc = lax.rem(c + (d & 1), 2)
    return (px, py, pc), 4 * px + 2 * py + pc


def _exchange(xs, *, scatter, name):
    n = len(xs)
    nsem = n * (N_DEV - 1)

    def body(*refs):
        ins, outs = refs[:n], refs[n:2 * n]
        token, send_sems, recv_sems, local_sems = refs[2 * n:]
        x, y, c = lax.axis_index("x"), lax.axis_index("y"), lax.axis_index("c")
        me = 4 * x + 2 * y + c
        token[...] = jnp.zeros_like(token)

        def src(t, idx):
            return ins[t].at[idx] if scatter else ins[t]

        local = [pltpu.make_async_copy(src(t, me), outs[t].at[me], local_sems.at[t]) for t in range(n)]
        for cp in local:
            cp.start()
        remote = []
        for t in range(n):
            for d in range(1, N_DEV):
                peer, pidx = _peer(x, y, c, d)
                k = t * (N_DEV - 1) + d - 1
                send = pltpu.make_async_remote_copy(src_ref=src(t, pidx), dst_ref=outs[t].at[me],
                                                    send_sem=send_sems.at[k], recv_sem=recv_sems.at[k],
                                                    device_id=peer, device_id_type=MESH)
                recv = pltpu.make_async_remote_copy(src_ref=src(t, pidx), dst_ref=outs[t].at[pidx],
                                                    send_sem=send_sems.at[k], recv_sem=recv_sems.at[k],
                                                    device_id=peer, device_id_type=MESH)
                send.start()
                remote.append((send, recv))
        for cp in local:
            cp.wait()
        for send, recv in remote:
            send.wait_send()
            recv.wait_recv()

    anyspec = pl.BlockSpec(memory_space=pl.ANY)
    out_shape = [jax.ShapeDtypeStruct(a.shape if scatter else (N_DEV,) + a.shape, a.dtype) for a in xs]
    out_shape.append(jax.ShapeDtypeStruct((8, 128), F32))
    return pl.pallas_call(
        body, name=name,
        in_specs=[anyspec] * n, out_specs=[anyspec] * n + [pl.BlockSpec(memory_space=pltpu.VMEM)],
        out_shape=out_shape,
        scratch_shapes=[pltpu.SemaphoreType.DMA((nsem,)), pltpu.SemaphoreType.DMA((nsem,)),
                        pltpu.SemaphoreType.DMA((n,))],
    )(*xs)


def _sum8(parts, *, name):
    _, R, C = parts.shape

    def body(p_ref, o_ref):
        acc = p_ref[0]
        for s in range(1, N_DEV):
            acc = acc + p_ref[s]
        o_ref[...] = acc

    return pl.pallas_call(body, name=name, out_shape=jax.ShapeDtypeStruct((R, C), F32),
                          compiler_params=_cp())(parts)


HBM_SPEC = pl.BlockSpec(memory_space=pltpu.HBM)
SEM_SPEC = pl.BlockSpec(memory_space=pltpu.SEMAPHORE)
N_PEER = N_DEV - 1


def _split_copies(src_refs, land_refs, send_sems, recv_sems, scatter):
    x, y, c = lax.axis_index("x"), lax.axis_index("y"), lax.axis_index("c")
    me = 4 * x + 2 * y + c
    pairs = []
    for j, (src, land) in enumerate(zip(src_refs, land_refs)):
        for d in range(1, N_DEV):
            peer, pidx = _peer(x, y, c, d)
            k = j * N_PEER + d - 1
            s = src.at[pidx] if scatter else src
            send = pltpu.make_async_remote_copy(src_ref=s, dst_ref=land.at[me], send_sem=send_sems.at[k],
                                                recv_sem=recv_sems.at[k], device_id=peer, device_id_type=MESH)
            recv = pltpu.make_async_remote_copy(src_ref=s, dst_ref=land.at[pidx], send_sem=send_sems.at[k],
                                                recv_sem=recv_sems.at[k], device_id=peer, device_id_type=MESH)
            pairs.append((send, recv))
    return pairs


def _own_slot(block, me):
    land = lax.empty((N_DEV,) + block.shape, block.dtype)
    return lax.dynamic_update_slice(land, block[None], (me, 0, 0))


def _split_start(srcs, lands, groups, *, scatter, name):
    n, ng = len(srcs), len(groups)

    def body(*refs):
        src_refs, land_refs = refs[:n], refs[n:2 * n]
        sems = refs[2 * n:2 * n + 2 * ng]
        token = refs[-1]
        for gi, g in enumerate(groups):
            pairs = _split_copies([src_refs[t] for t in g], [land_refs[t] for t in g], sems[2 * gi],
                                  sems[2 * gi + 1], scatter)
            for send, _ in pairs:
                send.start()
        token[...] = jnp.zeros_like(token)

    sem_shapes = []
    for g in groups:
        sem_shapes += [pltpu.SemaphoreType.DMA((len(g) * N_PEER,))] * 2
    thru = [pltpu.HBM(a.shape, a.dtype) for a in list(srcs) + list(lands)]
    outs = pl.pallas_call(
        body, name=name,
        out_shape=tuple(sem_shapes + thru + [jax.ShapeDtypeStruct((8, 128), F32)]),
        in_specs=[HBM_SPEC] * (2 * n),
        out_specs=tuple([SEM_SPEC] * (2 * ng) + [HBM_SPEC] * (2 * n) + [pl.BlockSpec(memory_space=pltpu.VMEM)]),
        input_output_aliases={i: 2 * ng + i for i in range(2 * n)},
        compiler_params=pltpu.CompilerParams(has_side_effects=pltpu.SideEffectType.DATAFLOW_SIDE_EFFECTING),
    )(*[pltpu.with_memory_space_constraint(a, pltpu.HBM) for a in list(srcs) + list(lands)])
    sems = [(outs[2 * gi], outs[2 * gi + 1]) for gi in range(ng)]
    return sems, outs[2 * ng:2 * ng + n], outs[2 * ng + n:2 * ng + 2 * n], outs[-1]


def _behind(v, token):
    if token is None:
        return v
    return v + token[0, 0].astype(v.dtype)


def _split_wait(srcs, lands, sems, after, *, scatter, name):
    m = len(srcs)

    def body(*refs):
        src_refs, land_refs = refs[:m], refs[m:2 * m]
        send_sems, recv_sems = refs[2 * m], refs[2 * m + 1]
        for send, recv in _split_copies(src_refs, land_refs, send_sems, recv_sems, scatter):
            send.wait_send()
            recv.wait_recv()

    outs = pl.pallas_call(
        body, name=name,
        out_shape=tuple(pltpu.HBM(a.shape, a.dtype) for a in list(srcs) + list(lands)),
        in_specs=[HBM_SPEC] * (2 * m) + [SEM_SPEC, SEM_SPEC, pl.BlockSpec(memory_space=pl.ANY)],
        out_specs=tuple([HBM_SPEC] * (2 * m)),
        input_output_aliases={i: i for i in range(2 * m)},
        compiler_params=pltpu.CompilerParams(has_side_effects=pltpu.SideEffectType.DATAFLOW_SIDE_EFFECTING),
    )(*srcs, *lands, sems[0], sems[1], after)
    return outs[m:]


TL_FIRST = (1, 2, 4, 6)
TL_ICI = (2, 4, 6)
EFFECT = pltpu.SideEffectType.DATAFLOW_SIDE_EFFECTING


def _tl_first(src_refs, land_refs, send_sems, recv_sems):
    x, y, c = lax.axis_index("x"), lax.axis_index("y"), lax.axis_index("c")
    me = 4 * x + 2 * y + c
    out = []
    for j, (src, land) in enumerate(zip(src_refs, land_refs)):
        for i, d in enumerate(TL_FIRST):
            peer, pidx = _peer(x, y, c, d)
            k = len(TL_FIRST) * j + i
            send = pltpu.make_async_remote_copy(src_ref=src, dst_ref=land.at[me], send_sem=send_sems.at[k],
                                                recv_sem=recv_sems.at[k], device_id=peer, device_id_type=MESH)
            recv = pltpu.make_async_remote_copy(src_ref=src, dst_ref=land.at[pidx], send_sem=send_sems.at[k],
                                                recv_sem=recv_sems.at[k], device_id=peer, device_id_type=MESH)
            out.append((d, send, recv))
    return out


def _tl_second(land_refs, send_sems, recv_sems):
    x, y, c = lax.axis_index("x"), lax.axis_index("y"), lax.axis_index("c")
    sibling, _ = _peer(x, y, c, 1)
    out = []
    for j, land in enumerate(land_refs):
        for i, d in enumerate(TL_ICI):
            _, mine = _peer(x, y, c, d)
            _, theirs = _peer(x, y, c, d + 1)
            k = len(TL_ICI) * j + i
            send = pltpu.make_async_remote_copy(src_ref=land.at[mine], dst_ref=land.at[mine], send_sem=send_sems.at[k],
                                                recv_sem=recv_sems.at[k], device_id=sibling, device_id_type=MESH)
            recv = pltpu.make_async_remote_copy(src_ref=land.at[mine], dst_ref=land.at[theirs],
                                                send_sem=send_sems.at[k], recv_sem=recv_sems.at[k],
                                                device_id=sibling, device_id_type=MESH)
            out.append((send, recv))
    return out


def _tl_start(srcs, lands, groups, *, name):
    n, ng = len(srcs), len(groups)

    def body(*refs):
        src_refs, land_refs = refs[:n], refs[n:2 * n]
        sems = refs[2 * n:2 * n + 2 * ng]
        for gi, g in enumerate(groups):
            for _, send, _ in _tl_first([src_refs[t] for t in g], [land_refs[t] for t in g], sems[2 * gi],
                                        sems[2 * gi + 1]):
                send.start()
        refs[-1][...] = jnp.zeros_like(refs[-1])

    sem_shapes = []
    for g in groups:
        sem_shapes += [pltpu.SemaphoreType.DMA((len(g) * len(TL_FIRST),))] * 2
    thru = [pltpu.HBM(a.shape, a.dtype) for a in list(srcs) + list(lands)]
    outs = pl.pallas_call(
        body, name=name,
        out_shape=tuple(sem_shapes + thru + [jax.ShapeDtypeStruct((8, 128), F32)]),
        in_specs=[HBM_SPEC] * (2 * n),
        out_specs=tuple([SEM_SPEC] * (2 * ng) + [HBM_SPEC] * (2 * n) + [pl.BlockSpec(memory_space=pltpu.VMEM)]),
        input_output_aliases={i: 2 * ng + i for i in range(2 * n)},
        compiler_params=pltpu.CompilerParams(has_side_effects=EFFECT),
    )(*[pltpu.with_memory_space_constraint(a, pltpu.HBM) for a in list(srcs) + list(lands)])
    sems = [(outs[2 * gi], outs[2 * gi + 1]) for gi in range(ng)]
    return sems, outs[2 * ng:2 * ng + n], outs[2 * ng + n:2 * ng + 2 * n], outs[-1]


def _tl_forward(srcs, lands, sems1, after, *, name):
    m = len(srcs)

    def body(*refs):
        src_refs, land_refs = refs[:m], refs[m:2 * m]
        send1, recv1 = refs[2 * m], refs[2 * m + 1]
        send2, recv2 = refs[2 * m + 3], refs[2 * m + 4]
        for d, _, recv in _tl_first(src_refs, land_refs, send1, recv1):
            if d in TL_ICI:
                recv.wait_recv()
        for send, _ in _tl_second(land_refs, send2, recv2):
            send.start()

    sem = pltpu.SemaphoreType.DMA((m * len(TL_ICI),))
    outs = pl.pallas_call(
        body, name=name,
        out_shape=tuple([sem, sem] + [pltpu.HBM(a.shape, a.dtype) for a in list(srcs) + list(lands)]),
        in_specs=[HBM_SPEC] * (2 * m) + [SEM_SPEC, SEM_SPEC, pl.BlockSpec(memory_space=pl.ANY)],
        out_specs=tuple([SEM_SPEC, SEM_SPEC] + [HBM_SPEC] * (2 * m)),
        input_output_aliases={i: 2 + i for i in range(2 * m)},
        compiler_params=pltpu.CompilerParams(has_side_effects=EFFECT),
    )(*srcs, *lands, sems1[0], sems1[1], after)
    return (outs[0], outs[1]), outs[2:2 + m], outs[2 + m:2 + 2 * m]


def _tl_wait(srcs, lands, sems1, sems2, after, *, name):
    m = len(srcs)

    def body(*refs):
        src_refs, land_refs = refs[:m], refs[m:2 * m]
        send1, recv1, send2, recv2 = refs[2 * m:2 * m + 4]
        for d, send, recv in _tl_first(src_refs, land_refs, send1, recv1):
            send.wait_send()
            if d not in TL_ICI:
                recv.wait_recv()
        for send, recv in _tl_second(land_refs, send2, recv2):
            send.wait_send()
            recv.wait_recv()

    outs = pl.pallas_call(
        body, name=name,
        out_shape=tuple(pltpu.HBM(a.shape, a.dtype) for a in list(srcs) + list(lands)),
        in_specs=[HBM_SPEC] * (2 * m) + [SEM_SPEC] * 4 + [pl.BlockSpec(memory_space=pl.ANY)],
        out_specs=tuple([HBM_SPEC] * (2 * m)),
        input_output_aliases={i: i for i in range(2 * m)},
        compiler_params=pltpu.CompilerParams(has_side_effects=EFFECT),
    )(*srcs, *lands, sems1[0], sems1[1], sems2[0], sems2[1], after)
    return outs[m:]


TM_PROJ = 512
TN_PROJ = 512
TM_ROW = 512
TM_NN = 512
TK_TN = 2048
TM_ADAM = 416
TN_FFN = F // 2
TN_IN = NIN // 4


def _tn(a, b, name, tn, token=None):
    if a.ndim == 2:
        a = a[None]
    return _tn_matmul(a, b, token, tn=tn, tk=TK_TN, name=name)


def _local_step(x, tgt, mods, g1, gm, g2, gf, convw8, sinks, w_get, g_put, tables=None):
    T = x.shape[0]
    sh1, sc1, gt1, sh2, sc2, gt2, sh3, sc3, gt3 = [mods[i:i + 1] for i in range(N_MOD)]
    cos, sin = _rope_tables(T) if tables is None else tables
    behind = _behind

    w = dict(w_get("gu1", mods))
    h1, ab1 = _norm_proj(x, g1, sc1, sh1, w["gu1"], tm=TM_PROJ, tn=TN_PROJ, name="ffn1_up")
    w.update(w_get("d1", ab1))
    x1, y1 = _ffn_down_fwd(ab1, w["d1"], x, gt1, tm=TM_ROW, name="ffn1_down")
    w.update(w_get("mix", x1))
    h2, proj, qs, kr = _norm_proj(x1, gm, sc2, sh2, w["win"], (cos, sin), tm=TM_PROJ, tn=TN_PROJ, name="mix_in")
    bias = _attn_bias()
    attn, lse = _attn_fwd(qs, kr, proj, bias, sinks, name="attn_fwd")
    x2, gc, yc, ya, mg, o = _mixer_mid_fwd(proj, attn, w["cp"], w["ap"], w["out"], convw8, x1, gt2,
                                           tm=TM_ROW, name="mix_mid")
    w.update(w_get("ffn2", x2))
    h3, ab2, y2, dx3, lsum, dgf = _ffn_fwd(x2, g2, sc3, sh3, gt3, w["gu2"], w["d2"], (tgt, gf), tm=TM_ROW,
                                           name="ffn2_final")

    dab2, dgt3, g_d2 = _ffn_down_bwd_dw(dx3, y2, gt3, ab2, w["d2"], tm=TM_ROW, name="ffn2_down_bwd")
    dx2, dsh3, dsc3, dg2 = _nn_bwd_norm(dab2, w["gu2"], x2, g2, sc3, dx3, tm=TM_NN, name="ffn2_up_bwd")
    g_gu2 = _tn(dab2, h3, "ffn2_up_dw", TN_FFN)
    tok = g_put(dict(gu2=g_gu2, d2=g_d2))

    dout, dyc, dya, dgc, dat, dproj, dgt2 = _mixer_mid_bwd(dx2, behind(gt2, tok), o, proj, yc, ya, w["out"], w["cp"],
                                                           w["ap"], tm=TM_ROW, name="mix_mid_bwd")
    g_out = _tn(mg, dout, "mix_out_dw", D)
    g_cp = _tn(gc, dyc, "mix_cp_dw", D)
    g_ap = _tn(attn, dya, "mix_ap_dw", D)
    dproj, dkc, dkp, dvc, dvp, dsink = _attn_bwd(qs, kr, proj, bias, sinks, lse, attn, dat, cos, sin, dproj,
                                                 name="attn_bwd")
    dproj = _dkv_combine(dkc, dkp, dvc, dvp, dproj, name="attn_dkv")
    dproj, dcw = _conv_bwd(dgc, proj, convw8, dproj, tm=TM_ROW, name="conv_bwd")
    g_in = _tn(dproj, h2, "mix_in_dw", TN_IN)
    tok = g_put(dict(win=g_in, cp=g_cp, ap=g_ap, out=g_out))
    dx1, dsh2, dsc2, dgm = _nn_bwd_norm(dproj[None], w["win"], x1, gm, behind(sc2, tok), dx2, tm=TM_NN,
                                        name="mix_in_bwd")

    dab1, dgt1, g_d1 = _ffn_down_bwd_dw(dx1, y1, gt1, ab1, w["d1"], tm=TM_ROW, name="ffn1_down_bwd")
    tok = g_put(dict(d1=g_d1))
    g_gu1 = _tn(dab1, h1, "ffn1_up_dw", TN_FFN, tok)
    tok = g_put(dict(gu1=g_gu1))
    dx0, dsh1, dsc1, dg1 = _nn_bwd_norm(dab1, w["gu1"], x, g1, behind(sc1, tok), dx1, tm=TM_NN,
                                        name="ffn1_up_bwd")

    small = dict(mods=jnp.concatenate([dsh1, dsc1, dgt1, dsh2, dsc2, dgt2, dsh3, dsc3, dgt3], axis=0),
                 g1=dg1, gm=dgm, g2=dg2, gf=dgf, convw=dcw[0:3], sinks=dsink[:, 0:N_HEADS])
    return lsum, dx0, small


BIG = ("gu1", "d1", "win", "cp", "ap", "out", "gu2", "d2")
TRANSPOSED = ("gu1", "win", "gu2")
SMALL_ROWS = 24
R_MODS, R_G1, R_GM, R_G2, R_GF, R_CONV, R_SINK, R_LOSS = 0, 9, 10, 11, 12, 13, 16, 17


def _pad_to(a, rows, cols):
    return jnp.pad(a, ((0, rows - a.shape[0]), (0, cols - a.shape[1])))


def _pack_small(b_ada, g1, gm, g2, gf, conv, sinks, lsum):
    rows = [b_ada.reshape(N_MOD, D), g1.reshape(1, D), gm.reshape(1, D), g2.reshape(1, D), gf.reshape(1, D),
            _pad_to(conv.reshape(3, -1), 3, D), _pad_to(sinks.reshape(1, N_HEADS), 1, D), lsum.reshape(1, D)]
    return _pad_to(jnp.concatenate(rows, axis=0), SMALL_ROWS, D)


def kernel(x, c, w_ada, b_ada, g_ffn1, w1_gu, w1_down, g_mix, w_in, conv_w, w_conv_proj, w_attn_proj, sinks, w_out, g_ffn2, w2_gu, w2_down, g_final, loss_target, m_w_ada, m_b_ada, m_g_ffn1, m_w1_gu, m_w1_down, m_g_mix, m_w_in, m_conv_w, m_w_conv_proj, m_w_attn_proj, m_sinks, m_w_out, m_g_ffn2, m_w2_gu, m_w2_down, m_g_final, v_w_ada, v_b_ada, v_g_ffn1, v_w1_gu, v_w1_down, v_g_mix, v_w_in, v_conv_w, v_w_conv_proj, v_w_attn_proj, v_sinks, v_w_out, v_g_ffn2, v_w2_gu, v_w2_down, v_g_final):
    me = 4 * lax.axis_index("x") + 2 * lax.axis_index("y") + lax.axis_index("c")
    ada_cols = w_ada.shape[2]
    conv_cols = conv_w.shape[2]

    native = dict(gu1=w1_gu[0], d1=w1_down[0], win=w_in[0], cp=w_conv_proj[0], ap=w_attn_proj[0], out=w_out[0],
                  gu2=w2_gu[0], d2=w2_down[0])

    def shard(n, token):
        a = _behind(native[n], token)
        return (a.T if n in TRANSPOSED else a).astype(BF)

    c_all, conv_all, _ = _exchange([c, _pad_to(conv_w[0], 8, conv_cols)], scatter=False, name="gather_cond")
    c_all = c_all.reshape(N_DEV, D)
    conv_full = conv_all[:, 0:3, :].transpose(1, 0, 2).reshape(3, D)

    b_cols = lax.dynamic_slice(b_ada, (0, me * ada_cols), (1, ada_cols))
    mods_cols = _mods_part(c_all, w_ada[0], b_cols, name="ada_mods")
    mods_all, mods_token = _exchange([mods_cols], scatter=False, name="gather_mods")
    mods = lax.dynamic_index_in_dim(mods_all, me, axis=1, keepdims=False).reshape(N_MOD, D)

    groups = dict(gu1=("gu1",), d1=("d1",), mix=("win", "cp", "ap", "out"), ffn2=("gu2", "d2"))
    in_flight = {}
    first = [shard("gu1", mods_token)]
    sems, srcs, lands, token = _tl_start(first, [_own_slot(s, me) for s in first], [[0]],
                                         name="gather_weights_start_gu1")
    in_flight["gu1"] = [sems[0], srcs, lands, None]
    rest = [n for n in BIG if n != "gu1"]
    shards = [shard(n, token) for n in rest]
    rest_groups = [[rest.index(n) for n in names] for g, names in groups.items() if g != "gu1"]
    sems, srcs, lands, rest_token = _tl_start(shards, [_own_slot(s, me) for s in shards], rest_groups,
                                              name="gather_weights_start_rest")
    for (g, names), gsems, idx in zip([kv for kv in groups.items() if kv[0] != "gu1"], sems, rest_groups):
        in_flight[g] = [gsems, [srcs[t] for t in idx], [lands[t] for t in idx], None]

    def forward(group, after):
        sems1, gsrcs, glands, _ = in_flight[group]
        sems2, gsrcs, glands = _tl_forward(gsrcs, glands, sems1, after, name="gather_weights_forward_" + group)
        in_flight[group] = [sems1, gsrcs, glands, sems2]

    forward_early = dict(d1="mix", mix="ffn2")

    tables = _rope_tables(x.shape[1], rest_token)

    def w_get(group, after):
        if group == "gu1":
            after = tables[0]
        if in_flight[group][3] is None:
            forward(group, after)
        sems1, gsrcs, glands, sems2 = in_flight[group]
        landed = _tl_wait(gsrcs, glands, sems1, sems2, after, name="gather_weights_wait_" + group)
        if group in forward_early:
            forward(forward_early[group], landed[0])
        return {n: a.reshape(-1, D) for n, a in zip(groups[group], landed)}

    pending = []

    def g_put(gs):
        names = tuple(gs)
        srcs = [gs[n].reshape(N_DEV, -1, D) for n in names]
        lands = [_own_slot(lax.dynamic_index_in_dim(s, me, axis=0, keepdims=False), me) for s in srcs]
        sems, srcs, lands, tok = _split_start(srcs, lands, [list(range(len(names)))], scatter=True,
                                              name="scatter_grads_start_" + names[0])
        pending.append((names, sems[0], srcs, lands))
        return tok

    lsum, grad_x, small = _local_step(x[0], loss_target[0], mods, g_ffn1, g_mix, g_ffn2, g_final[None],
                                      _pad_to(conv_full, 8, D), sinks[0], w_get, g_put, tables)

    packed = _pack_small(small["mods"], small["g1"], small["gm"], small["g2"], small["gf"], small["convw"],
                         small["sinks"], lsum)
    packed_all, _ = _exchange([packed], scatter=False, name="gather_small")
    gsmall = _sum8(packed_all, name="sum_small")
    loss = (0.5 / D) * jnp.sum(gsmall[R_LOSS])

    w_of = dict(ada=w_ada, gu1=w1_gu, d1=w1_down, win=w_in, cp=w_conv_proj, ap=w_attn_proj, out=w_out, gu2=w2_gu,
                d2=w2_down)
    m_of = dict(ada=m_w_ada, gu1=m_w1_gu, d1=m_w1_down, win=m_w_in, cp=m_w_conv_proj, ap=m_w_attn_proj, out=m_w_out,
                gu2=m_w2_gu, d2=m_w2_down)
    v_of = dict(ada=v_w_ada, gu1=v_w1_gu, d1=v_w1_down, win=v_w_in, cp=v_w_conv_proj, ap=v_w_attn_proj, out=v_w_out,
                gu2=v_w2_gu, d2=v_w2_down)
    upd = {}
    after = gsmall
    for names, sems, srcs, lands in pending:
        parts = _split_wait(srcs, lands, sems, after, scatter=True, name="scatter_grads_wait_" + names[0])
        for n, p in zip(names, parts):
            if n in TRANSPOSED:
                res = _adam(jnp.swapaxes(w_of[n], 1, 2), p, jnp.swapaxes(m_of[n], 1, 2), jnp.swapaxes(v_of[n], 1, 2),
                            tm=TM_ADAM, name="adam_" + n)
                upd[n] = [jnp.swapaxes(t, 1, 2) for t in res]
            else:
                upd[n] = _adam(w_of[n], p, m_of[n], v_of[n], tm=TM_ADAM, name="adam_" + n)
        after = upd[names[-1]][1]

    gm_cols = lax.dynamic_slice(packed_all[:, R_MODS:R_MODS + N_MOD, :].reshape(N_DEV, N_MOD * D),
                                (0, me * ada_cols), (N_DEV, ada_cols))
    upd["ada"] = _adam(w_ada, _wada_grad(c_all.T, gm_cols, name="ada_dw"), m_w_ada, v_w_ada, tm=256, name="adam_ada")
    conv_g = lax.dynamic_slice(gsmall, (R_CONV, me * conv_cols), (3, conv_cols))

    def natural(b, g1, gm, g2, gf, cw, sk):
        return dict(b_ada=b, g_ffn1=g1, g_mix=gm, g_ffn2=g2, g_final=gf[None], conv_w=cw[0], sinks=sk)

    small_out = _adam_small(gsmall, conv_g, natural(b_ada, g_ffn1, g_mix, g_ffn2, g_final, conv_w, sinks),
                            natural(m_b_ada, m_g_ffn1, m_g_mix, m_g_ffn2, m_g_final, m_conv_w, m_sinks),
                            natural(v_b_ada, v_g_ffn1, v_g_mix, v_g_ffn2, v_g_final, v_conv_w, v_sinks),
                            name="adam_small")
    for res in small_out:
        res["g_final"] = res["g_final"][0]
        res["conv_w"] = res["conv_w"][None]

    big_name = dict(w_ada="ada", w1_gu="gu1", w1_down="d1", w_in="win", w_conv_proj="cp", w_attn_proj="ap",
                    w_out="out", w2_gu="gu2", w2_down="d2")
    order = ("w_ada", "b_ada", "g_ffn1", "w1_gu", "w1_down", "g_mix", "w_in", "conv_w", "w_conv_proj", "w_attn_proj",
             "sinks", "w_out", "g_ffn2", "w2_gu", "w2_down", "g_final")
    outs = [loss, grad_x[None]]
    for kind in range(4):
        for n in order:
            outs.append(upd[big_name[n]][kind] if n in big_name else small_out[kind][n])
    return tuple(outs)
```

```python
import jax
import jax.numpy as jnp
from jax import lax
from jax.experimental import pallas as pl
from jax.experimental.pallas import tpu as pltpu

D = 1024
F = 2816
NIN = 6656
N_HEADS = 16
N_KV = 4
HEAD_DIM = 64
BLK = 128
N_MOD = 9
N_DEV = 8
EPS = 1e-6
NEG_INF = -1e30
ROPE_THETA = 10000.0
O_BG, O_CG, O_U, O_Q, O_K, O_V, O_ZC, O_ZA = 0, 1024, 2048, 3072, 4096, 4352, 4608, 5632

ADAM_LR = 0.001
ADAM_B1 = 0.9
ADAM_B2 = 0.999
ADAM_EPS = 1e-08
ADAM_WD = 0.01
ADAM_STEP = 10

BF = jnp.bfloat16
F32 = jnp.float32
VMEM_LIMIT = 56 * 1024 * 1024
MXU_N = 256
MESH = pl.DeviceIdType.MESH

NT = (((1,), (1,)), ((), ()))
TN = (((0,), (0,)), ((), ()))


def _cp(sem=None):
    return pltpu.CompilerParams(dimension_semantics=sem, vmem_limit_bytes=VMEM_LIMIT)


def _tile(n, pref):
    if n <= pref:
        return n
    for t in range(pref - pref % 16, 15, -16):
        if n % t == 0:
            return t
    raise ValueError((n, pref))


def _sigmoid(v):
    return 0.5 * jnp.tanh(0.5 * v) + 0.5


def _row(i):
    return (i, 0)


def _const2(*_):
    return (0, 0)


def _resident(shape):
    return pl.BlockSpec(shape, lambda *_: (0,) * len(shape), pipeline_mode=pl.Buffered(1))


def _norm_proj(x, g, sc, sh, wt, rope=None, *, tm, tn, name):
    T, N = x.shape[0], wt.shape[0]
    tm = _tile(T, tm)

    def body(x_ref, g_ref, sc_ref, sh_ref, w_ref, *rest):
        if rope is None:
            h_ref, o_ref = rest
        else:
            c_ref, s_ref, h_ref, o_ref, qs_ref, kr_ref = rest
        xv = x_ref[...]
        r = lax.rsqrt(jnp.mean(xv * xv, axis=-1, keepdims=True) + EPS)
        hb = ((xv * r) * g_ref[...] * (1.0 + sc_ref[...]) + sh_ref[...]).astype(BF)
        h_ref[...] = hb
        for c0 in range(0, N, tn):
            cols = pl.ds(c0, tn)
            o_ref[:, cols] = lax.dot_general(hb, w_ref[cols, :], NT, preferred_element_type=F32).astype(BF)
            if rope is not None and c0 < O_V <= c0 + tn:
                _attn_prep_tile(o_ref, c_ref, s_ref, qs_ref, kr_ref, tm)

    vec = pl.BlockSpec((1, D), _const2)
    rowspec = pl.BlockSpec((tm, D), _row)
    in_specs = [rowspec, vec, vec, vec, _resident((N, D))]
    out_specs = [rowspec, pl.BlockSpec((tm, N), _row)]
    out_shape = [jax.ShapeDtypeStruct((T, D), BF), jax.ShapeDtypeStruct((T, N), BF)]
    args = [x, g, sc, sh, wt]
    if rope is not None:
        in_specs += [pl.BlockSpec((tm, 128), _row)] * 2
        out_specs += [pl.BlockSpec((N_KV, 4 * tm, 128), lambda i: (0, i, 0)), pl.BlockSpec((tm, 256), _row)]
        out_shape += [jax.ShapeDtypeStruct((N_KV, 4 * T, 128), BF), jax.ShapeDtypeStruct((T, 256), BF)]
        args += list(rope)
    return pl.pallas_call(
        body, name=name, grid=(T // tm,),
        in_specs=in_specs, out_specs=out_specs, out_shape=out_shape,
        compiler_params=_cp(("parallel",)),
    )(*args)


def _ffn_down_fwd(ab, wd, x, gt, *, tm, name):
    T = x.shape[0]
    tm = _tile(T, tm)

    def body(a_ref, b_ref, wd_ref, x_ref, gt_ref, xo_ref, y_ref):
        y = None
        for c0 in range(0, F, MXU_N):
            cols = pl.ds(c0, MXU_N)
            a = a_ref[:, cols].astype(F32)
            act = (a * _sigmoid(a) * b_ref[:, cols].astype(F32)).astype(BF)
            part = jnp.dot(act, wd_ref[cols, :], preferred_element_type=F32)
            y = part if y is None else y + part
        y_ref[...] = y.astype(BF)
        xo_ref[...] = x_ref[...] + (0.5 * gt_ref[...]) * y

    return pl.pallas_call(
        body, name=name, grid=(T // tm,),
        in_specs=[pl.BlockSpec((tm, F), lambda i: (i, 0)), pl.BlockSpec((tm, F), lambda i: (i, 1)),
                  _resident((F, D)), pl.BlockSpec((tm, D), _row), pl.BlockSpec((1, D), _const2)],
        out_specs=[pl.BlockSpec((tm, D), _row), pl.BlockSpec((tm, D), _row)],
        out_shape=[jax.ShapeDtypeStruct((T, D), F32), jax.ShapeDtypeStruct((T, D), BF)],
        compiler_params=_cp(("parallel",)),
    )(ab, ab, wd, x, gt)


def _ffn_fwd(x, g, sc, sh, gt, wgu, wd, final, *, tm, name):
    T = x.shape[0]
    tm = _tile(T, tm)
    last = final is not None

    def body(x_ref, g_ref, sc_ref, sh_ref, gt_ref, wgu_ref, wd_ref, *rest):
        if last:
            t_ref, gf_ref, h_ref, ab_ref, y_ref, dx_ref, ls_ref, dgf_ref = rest
        else:
            h_ref, ab_ref, y_ref, xo_ref = rest
        xv = x_ref[...]
        r = lax.rsqrt(jnp.mean(xv * xv, axis=-1, keepdims=True) + EPS)
        hb = ((xv * r) * g_ref[...] * (1.0 + sc_ref[...]) + sh_ref[...]).astype(BF)
        h_ref[...] = hb
        y = None
        for c0 in range(0, F, MXU_N):
            a = lax.dot_general(hb, wgu_ref[pl.ds(c0, MXU_N), :], NT, preferred_element_type=F32)
            b = lax.dot_general(hb, wgu_ref[pl.ds(F + c0, MXU_N), :], NT, preferred_element_type=F32)
            ab = a.astype(BF)
            bb = b.astype(BF)
            ab_ref[:, pl.ds(c0, MXU_N)] = ab
            ab_ref[:, pl.ds(F + c0, MXU_N)] = bb
            a = ab.astype(F32)
            act = (a * _sigmoid(a) * bb.astype(F32)).astype(BF)
            part = jnp.dot(act, wd_ref[pl.ds(c0, MXU_N), :], preferred_element_type=F32)
            y = part if y is None else y + part
        y_ref[...] = y.astype(BF)
        xo = xv + (0.5 * gt_ref[...]) * y
        if not last:
            xo_ref[...] = xo
            return

        @pl.when(pl.program_id(0) == 0)
        def _():
            ls_ref[...] = jnp.zeros_like(ls_ref)
            dgf_ref[...] = jnp.zeros_like(dgf_ref)
        gv = gf_ref[...]
        r = lax.rsqrt(jnp.mean(xo * xo, axis=-1, keepdims=True) + EPS)
        xh = xo * r
        e = xh * gv - t_ref[...]
        ls_ref[...] += jnp.sum(e * e, axis=0, keepdims=True)
        dy = e * (1.0 / D)
        dgf_ref[...] += jnp.sum(dy * xh, axis=0, keepdims=True)
        dxh = dy * gv
        dx_ref[...] = r * (dxh - xh * jnp.mean(dxh * xh, axis=-1, keepdims=True))

    vec = pl.BlockSpec((1, D), _const2)
    rowspec = pl.BlockSpec((tm, D), _row)
    in_specs = [rowspec, vec, vec, vec, vec, _resident((2 * F, D)), _resident((F, D))]
    out_specs = [rowspec, pl.BlockSpec((tm, 2 * F), _row), rowspec, rowspec]
    out_shape = [jax.ShapeDtypeStruct((T, D), BF), jax.ShapeDtypeStruct((T, 2 * F), BF),
                 jax.ShapeDtypeStruct((T, D), BF), jax.ShapeDtypeStruct((T, D), F32)]
    args = [x, g, sc, sh, gt, wgu, wd]
    if last:
        in_specs += [rowspec, vec]
        out_specs += [vec, vec]
        out_shape += [jax.ShapeDtypeStruct((1, D), F32)] * 2
        args += list(final)
    return pl.pallas_call(
        body, name=name, grid=(T // tm,),
        in_specs=in_specs, out_specs=out_specs, out_shape=out_shape,
        compiler_params=_cp(("arbitrary",) if last else ("parallel",)),
    )(*args)


def _ffn_down_bwd_dw(dxo, y, gt, ab, wd, *, tm, name):
    T = dxo.shape[0]
    tm = _tile(T, tm)
    nt = T // tm
    hw = F // 2
    chunks = [(c0, min(MXU_N, hw - c0)) for c0 in range(0, hw, MXU_N)]

    def body(dxo_ref, y_ref, gt_ref, a_ref, b_ref, wd_ref, dab_ref, dgt_ref, dwd_ref, dys, dyt, acc, stage, sem):
        i, j = pl.program_id(0), pl.program_id(1)

        @pl.when(jnp.logical_and(i == 0, j == 0))
        def _():
            dgt_ref[...] = jnp.zeros_like(dgt_ref)

        @pl.when(j == 0)
        def _():
            dxv = dxo_ref[...]
            dgt_ref[...] += 0.5 * jnp.sum(dxv * y_ref[...].astype(F32), axis=0, keepdims=True)
            dyf = (0.5 * gt_ref[...]) * dxv
            dys[...] = dyf.astype(BF)
            dyt[...] = dyf.T.astype(BF)

        def half(jj):
            @pl.when(i == 0)
            def _():
                acc[jj] = jnp.zeros((D, hw), F32)

            dy = dys[...]
            dy_t = dyt[...]
            for c0, cw in chunks:
                cols = pl.ds(c0, cw)
                dact = lax.dot_general(dy, wd_ref[pl.ds(jj * hw + c0, cw), :], NT, preferred_element_type=F32)
                a = a_ref[:, cols].astype(F32)
                b = b_ref[:, cols].astype(F32)
                s = _sigmoid(a)
                silu = a * s
                dab_ref[0, :, cols] = (dact * b * (s * (1.0 + a * (1.0 - s)))).astype(BF)
                dab_ref[1, :, cols] = (dact * silu).astype(BF)
                acc[jj, :, cols] += jnp.dot(dy_t, (silu * b).astype(BF), preferred_element_type=F32)

            @pl.when(i == nt - 1)
            def _():
                for c0, cw in chunks:
                    stage[0:cw, :] = acc[jj, :, pl.ds(c0, cw)].T.astype(BF)
                    out = pltpu.make_async_copy(stage.at[pl.ds(0, cw)], dwd_ref.at[pl.ds(jj * hw + c0, cw)], sem)
                    out.start()
                    out.wait()

        for jj in range(2):
            pl.when(j == jj)(lambda jj=jj: half(jj))

    vec = pl.BlockSpec((1, D), _const2)
    rowspec = pl.BlockSpec((tm, D), lambda i, j: (i, 0))
    return pl.pallas_call(
        body, name=name, grid=(nt, 2),
        in_specs=[rowspec, rowspec, vec, pl.BlockSpec((tm, hw), lambda i, j: (i, j)),
                  pl.BlockSpec((tm, hw), lambda i, j: (i, j + 2)), _resident((F, D))],
        out_specs=[pl.BlockSpec((2, tm, hw), lambda i, j: (0, i, j)), vec, pl.BlockSpec(memory_space=pl.ANY)],
        out_shape=[jax.ShapeDtypeStruct((2, T, F), BF), jax.ShapeDtypeStruct((1, D), F32),
                   jax.ShapeDtypeStruct((F, D), BF)],
        scratch_shapes=[pltpu.VMEM((tm, D), BF), pltpu.VMEM((D, tm), BF), pltpu.VMEM((2, D, hw), F32),
                        pltpu.VMEM((MXU_N, D), BF), pltpu.SemaphoreType.DMA(())],
        compiler_params=_cp(("arbitrary", "arbitrary")),
    )(dxo, y, gt, ab, ab, wd)


def _tn_matmul(a, b, token=None, *, tn, tk, name):
    S, T, Ns = a.shape
    tn, tk = _tile(Ns, tn), _tile(T, tk)
    nk, njs = T // tk, Ns // tn
    deps = [] if token is None else [token]

    def body(a_ref, b_ref, *rest):
        o_ref, acc = rest[len(deps):]
        k = pl.program_id(1)

        @pl.when(k == 0)
        def _():
            acc[...] = jnp.zeros_like(acc)
        acc[...] += lax.dot_general(a_ref[0], b_ref[...], TN, preferred_element_type=F32)

        @pl.when(k == nk - 1)
        def _():
            o_ref[...] = acc[...].astype(BF)

    return pl.pallas_call(
        body, name=name, grid=(S * njs, nk),
        in_specs=[pl.BlockSpec((1, tk, tn), lambda j, k: (j // njs, k, j % njs)),
                  pl.BlockSpec((tk, D), lambda j, k: (k, 0))] + [pl.BlockSpec(memory_space=pl.ANY)] * len(deps),
        out_specs=pl.BlockSpec((tn, D), lambda j, k: (j, 0)),
        out_shape=jax.ShapeDtypeStruct((S * Ns, D), BF),
        scratch_shapes=[pltpu.VMEM((tn, D), F32)],
        compiler_params=_cp(("parallel", "arbitrary")),
    )(a, b, *deps)


def _nn_bwd_norm(da, w, x, g, sc, dxo, *, tm, name):
    S, T, Ks = da.shape
    tm = _tile(T, tm)
    rc = _tile(tm, 256)

    def body(da_ref, w_ref, x_ref, g_ref, sc_ref, dxo_ref, dx_ref, dsh_ref, dsc_ref, dg_ref, acc):
        @pl.when(pl.program_id(0) == 0)
        def _():
            dsh_ref[...] = jnp.zeros_like(dsh_ref)
            dsc_ref[...] = jnp.zeros_like(dsc_ref)
            dg_ref[...] = jnp.zeros_like(dg_ref)

        d = jnp.dot(da_ref[0], w_ref[0:Ks, :], preferred_element_type=F32)
        for s in range(1, S):
            d = d + jnp.dot(da_ref[s], w_ref[s * Ks:(s + 1) * Ks, :], preferred_element_type=F32)
        acc[...] = d
        gv = g_ref[...]
        sc1 = 1.0 + sc_ref[...]
        dsh = jnp.zeros((1, D), F32)
        dsc = jnp.zeros((1, D), F32)
        dg = jnp.zeros((1, D), F32)
        for r0 in range(0, tm, rc):
            rows = pl.ds(r0, rc)
            u = acc[rows, :]
            xv = x_ref[rows, :]
            r = lax.rsqrt(jnp.mean(xv * xv, axis=-1, keepdims=True) + EPS)
            xh = xv * r
            dsh = dsh + jnp.sum(u, axis=0, keepdims=True)
            dsc = dsc + jnp.sum(u * (xh * gv), axis=0, keepdims=True)
            us = u * sc1
            dg = dg + jnp.sum(us * xh, axis=0, keepdims=True)
            dxh = us * gv
            dx_ref[rows, :] = dxo_ref[rows, :] + r * (dxh - xh * jnp.mean(dxh * xh, axis=-1, keepdims=True))
        dsh_ref[...] += dsh
        dsc_ref[...] += dsc
        dg_ref[...] += dg

    vec = pl.BlockSpec((1, D), _const2)
    rowspec = pl.BlockSpec((tm, D), _row)
    return pl.pallas_call(
        body, name=name, grid=(T // tm,),
        in_specs=[pl.BlockSpec((S, tm, Ks), lambda i: (0, i, 0)), _resident((S * Ks, D)), rowspec, vec, vec, rowspec],
        out_specs=[rowspec, vec, vec, vec],
        out_shape=[jax.ShapeDtypeStruct((T, D), F32)] + [jax.ShapeDtypeStruct((1, D), F32)] * 3,
        scratch_shapes=[pltpu.VMEM((tm, D), F32)],
        compiler_params=_cp(("arbitrary",)),
    )(da, w, x, g, sc, dxo)


def _rope(t, cos, sin_signed, lt32, inverse=False):
    sel = jnp.where(lt32, pltpu.roll(t, 96, 1), pltpu.roll(t, 32, 1))
    return t * cos - sel * sin_signed if inverse else t * cos + sel * sin_signed


def _rope_tables(T, token=None):
    inv = 1.0 / (ROPE_THETA ** (jnp.arange(0, HEAD_DIM, 2, dtype=F32) / HEAD_DIM))
    ang = _behind(jnp.arange(T, dtype=F32)[:, None] * inv[None, :], token)
    cos, sin = jnp.cos(ang), jnp.sin(ang)
    cos128 = jnp.tile(cos, (1, 4))
    sin128 = jnp.tile(jnp.concatenate([-sin, sin], axis=1), (1, 2))
    return cos128, sin128


QSCALE = HEAD_DIM ** -0.5


def _lane_masks(rows):
    lane = lax.broadcasted_iota(jnp.int32, (rows, 128), 1)
    return (lane % HEAD_DIM) < (HEAD_DIM // 2), [lane < HEAD_DIM, lane >= HEAD_DIM]


def _attn_bias():
    qi = lax.broadcasted_iota(jnp.int32, (4 * BLK, 2 * BLK), 0) % BLK
    kj = lax.broadcasted_iota(jnp.int32, (4 * BLK, 2 * BLK), 1)
    band = (kj > qi) & (kj <= qi + BLK)
    return jnp.stack([jnp.where(band & (kj >= BLK), 0.0, NEG_INF), jnp.where(band, 0.0, NEG_INF)]).astype(F32)


def _attn_prep_tile(proj_ref, c_ref, s_ref, qs_ref, kr_ref, tm):
    lt32, halves = _lane_masks(BLK)
    for b in range(tm // BLK):
        rows = pl.ds(b * BLK, BLK)
        cc, sc = c_ref[rows, :], s_ref[rows, :]
        qr = [_rope(proj_ref[rows, pl.ds(O_Q + p * 128, 128)].astype(F32), cc, sc, lt32) * QSCALE for p in range(8)]
        for g in range(N_KV):
            qs_ref[g, pl.ds(4 * b * BLK, 4 * BLK), :] = _stack_heads(qr, g, halves).astype(BF)
        kr_ref[rows, :] = jnp.concatenate([_rope(proj_ref[rows, pl.ds(O_K + r * 128, 128)].astype(F32), cc, sc, lt32)
                                           for r in range(2)], axis=1).astype(BF)


ATT_BPS = 2
ATT_ROWS = ATT_BPS * BLK


def _before(n):
    return jnp.maximum(ATT_BPS * n - 1, 0)


def _attn_specs():
    return [pl.BlockSpec((N_KV, 4 * ATT_ROWS, 128), lambda n: (0, n, 0)),
            pl.BlockSpec((ATT_ROWS, 256), _row), pl.BlockSpec((BLK, 256), lambda n: (_before(n), 0)),
            pl.BlockSpec((ATT_ROWS, 256), lambda n: (n, O_V // 256)),
            pl.BlockSpec((BLK, 256), lambda n: (_before(n), O_V // 256)),
            pl.BlockSpec((2, 4 * BLK, 2 * BLK), lambda n: (0, 0, 0)),
            pl.BlockSpec(memory_space=pltpu.SMEM)]


def _bands(sb, kc_ref, kp_ref, vc_ref, vp_ref):
    own = pl.ds(sb * BLK, BLK)
    above = pl.ds((sb - 1) * BLK, BLK)
    kb, vb = [], []
    for r in range(2):
        cols = pl.ds(r * 128, 128)
        kprev = kp_ref[:, cols] if sb == 0 else kc_ref[above, cols]
        vprev = vp_ref[:, cols] if sb == 0 else vc_ref[above, cols]
        kb.append(jnp.concatenate([kprev, kc_ref[own, cols]], axis=0))
        vb.append(jnp.concatenate([vprev, vc_ref[own, cols]], axis=0))
    return kb, vb


def _block_bias(sb, bias_ref):
    return bias_ref[jnp.minimum(pl.program_id(0), 1)] if sb == 0 else bias_ref[1]


def _sink_rows(sink_ref, g):
    return jnp.concatenate([jnp.full((BLK, 128), sink_ref[4 * g + hh], F32) for hh in range(4)], axis=0)


def _both(t):
    return jnp.concatenate([t, t], axis=1)


def _unstack_heads(t, g, halves, acc):
    half = g % 2
    for hh in range(4):
        h = 4 * g + hh
        th = jnp.where(halves[half], t[hh * BLK:(hh + 1) * BLK], 0.0)
        if h % 2 != half:
            th = pltpu.roll(th, HEAD_DIM, 1)
        acc[h // 2] = acc[h // 2] + th


def _stack_heads(chunks, g, halves):
    half = g % 2
    parts = []
    for hh in range(4):
        h = 4 * g + hh
        t = chunks[h // 2]
        if h % 2 != half:
            t = pltpu.roll(t, HEAD_DIM, 1)
        parts.append(jnp.where(halves[half], t, 0.0))
    return jnp.concatenate(parts, axis=0)


def _attn_fwd(qs, kr, proj, bias, sinks, *, name):
    T = proj.shape[0]
    assert T % ATT_ROWS == 0

    def body(qs_ref, kc_ref, kp_ref, vc_ref, vp_ref, bias_ref, sink_ref, o_ref, lse_ref):
        _, h128 = _lane_masks(BLK)
        _, h256 = _lane_masks(2 * BLK)
        _, h512 = _lane_masks(4 * BLK)
        groups = range(N_KV)
        sink = [_sink_rows(sink_ref, g) for g in groups]
        for sb in range(ATT_BPS):
            rows = pl.ds(4 * sb * BLK, 4 * BLK)
            kb, vb = _bands(sb, kc_ref, kp_ref, vc_ref, vp_ref)
            outs = [jnp.zeros((BLK, 128), F32) for _ in range(8)]
            bias = _block_bias(sb, bias_ref)
            s = [lax.dot_general(qs_ref[g, rows, :], kb[g // 2], NT, preferred_element_type=F32) + bias for g in groups]
            m = [jnp.maximum(jnp.broadcast_to(jnp.max(s[g], axis=-1, keepdims=True), (4 * BLK, 128)), sink[g])
                 for g in groups]
            p = [jnp.exp(s[g] - _both(m[g])).astype(BF) for g in groups]
            vg = [jnp.where(h256[g % 2], vb[g // 2].astype(F32), 1.0).astype(BF) for g in groups]
            o = [jnp.dot(p[g], vg[g], preferred_element_type=F32) for g in groups]
            denom = [jnp.where(h512[g % 2], pltpu.roll(o[g], HEAD_DIM, 1), o[g]) + jnp.exp(sink[g] - m[g])
                     for g in groups]
            for g in groups:
                lse_ref[g, rows, :] = m[g] + jnp.log(denom[g])
                _unstack_heads(o[g] * (1.0 / denom[g]), g, h128, outs)
            o_ref[pl.ds(sb * BLK, BLK), :] = jnp.concatenate(outs, axis=1).astype(BF)

    return pl.pallas_call(
        body, name=name, grid=(T // ATT_ROWS,),
        in_specs=_attn_specs(),
        out_specs=[pl.BlockSpec((ATT_ROWS, D), _row), pl.BlockSpec((N_KV, 4 * ATT_ROWS, 128), lambda n: (0, n, 0))],
        out_shape=[jax.ShapeDtypeStruct((T, D), BF), jax.ShapeDtypeStruct((N_KV, 4 * T, 128), F32)],
        compiler_params=_cp(("parallel",)),
    )(qs, kr, kr, proj, proj, bias, sinks)


def _attn_bwd(qs, kr, proj, bias, sinks, lse, o, do, cos, sin, dproj, *, name):
    T = proj.shape[0]
    assert T % ATT_ROWS == 0

    def body(qs_ref, kc_ref, kp_ref, vc_ref, vp_ref, bias_ref, sink_ref, lse_ref, o_ref, do_ref,
             cc_ref, sc_ref, cp_ref, sp_ref, dproj_ref, dq_ref, dkc_ref, dkp_ref, dvc_ref, dvp_ref, dsink_ref):
        @pl.when(pl.program_id(0) == 0)
        def _():
            dsink_ref[...] = jnp.zeros_like(dsink_ref)
        lt32, h128 = _lane_masks(BLK)
        lane1 = lax.broadcasted_iota(jnp.int32, (1, 128), 1)
        dsink = jnp.zeros((1, 128), F32)
        groups = range(N_KV)
        for sb in range(ATT_BPS):
            own = pl.ds(sb * BLK, BLK)
            rows = pl.ds(4 * sb * BLK, 4 * BLK)
            kb, vb = _bands(sb, kc_ref, kp_ref, vc_ref, vp_ref)
            oc = [o_ref[own, pl.ds(p * 128, 128)].astype(F32) for p in range(8)]
            doc = [do_ref[own, pl.ds(p * 128, 128)].astype(F32) for p in range(8)]
            dqs = [jnp.zeros((BLK, 128), F32) for _ in range(8)]
            bias = _block_bias(sb, bias_ref)
            q = [qs_ref[g, rows, :] for g in groups]
            lse_g = [lse_ref[g, rows, :] for g in groups]
            s = [lax.dot_general(q[g], kb[g // 2], NT, preferred_element_type=F32) + bias for g in groups]
            dos = [_stack_heads(doc, g, h128) for g in groups]
            dosb = [t.astype(BF) for t in dos]
            dp = [lax.dot_general(dosb[g], vb[g // 2], NT, preferred_element_type=F32) for g in groups]
            delta = [jnp.broadcast_to(jnp.sum(dos[g] * _stack_heads(oc, g, h128), axis=-1, keepdims=True),
                                      (4 * BLK, 128)) for g in groups]
            p = [jnp.exp(s[g] - _both(lse_g[g])) for g in groups]
            ds = [(p[g] * (dp[g] - _both(delta[g]))).astype(BF) for g in groups]
            pb = [t.astype(BF) for t in p]
            dvg = [lax.dot_general(pb[g], dosb[g], TN, preferred_element_type=F32) for g in groups]
            dkg = [lax.dot_general(ds[g], q[g], TN, preferred_element_type=F32) for g in groups]
            dqg = [jnp.dot(ds[g], kb[g // 2], preferred_element_type=F32) * QSCALE for g in groups]
            dvr = [dvg[0] + dvg[1], dvg[2] + dvg[3]]
            dkr = [dkg[0] + dkg[1], dkg[2] + dkg[3]]
            for g in groups:
                _unstack_heads(dqg[g], g, h128, dqs)
                dsk = -jnp.exp(_sink_rows(sink_ref, g) - lse_g[g]) * delta[g]
                for hh in range(4):
                    val = jnp.sum(dsk[hh * BLK:(hh + 1) * BLK], axis=0, keepdims=True)
                    dsink = dsink + jnp.where(lane1 == 4 * g + hh, val, 0.0)
            cc, sc = cc_ref[own, :], sc_ref[own, :]
            cp, sp = (cp_ref[...], sp_ref[...]) if sb == 0 else (cc_ref[pl.ds((sb - 1) * BLK, BLK), :],
                                                                  sc_ref[pl.ds((sb - 1) * BLK, BLK), :])
            dq_ref[own, :] = jnp.concatenate([_rope(t, cc, sc, lt32, inverse=True) for t in dqs], axis=1).astype(BF)
            dkp_ref[own, :] = jnp.concatenate([_rope(t[:BLK], cp, sp, lt32, inverse=True) for t in dkr], axis=1)
            dkc_ref[own, :] = jnp.concatenate([_rope(t[BLK:], cc, sc, lt32, inverse=True) for t in dkr], axis=1)
            dvp_ref[own, :] = jnp.concatenate([t[:BLK] for t in dvr], axis=1)
            dvc_ref[own, :] = jnp.concatenate([t[BLK:] for t in dvr], axis=1)
        dsink_ref[...] += dsink

    kv = pl.BlockSpec((ATT_ROWS, 256), _row)
    tc = pl.BlockSpec((ATT_ROWS, 128), _row)
    tp = pl.BlockSpec((BLK, 128), lambda n: (_before(n), 0))
    return pl.pallas_call(
        body, name=name, grid=(T // ATT_ROWS,),
        in_specs=_attn_specs() + [pl.BlockSpec((N_KV, 4 * ATT_ROWS, 128), lambda n: (0, n, 0)),
                                  pl.BlockSpec((ATT_ROWS, D), _row), pl.BlockSpec((ATT_ROWS, D), _row), tc, tc, tp, tp,
                                  pl.BlockSpec(memory_space=pl.ANY)],
        out_specs=[pl.BlockSpec((ATT_ROWS, D), lambda n: (n, O_Q // D)), kv, kv, kv, kv,
                   pl.BlockSpec((1, 128), _const2)],
        out_shape=[jax.ShapeDtypeStruct(dproj.shape, BF)] + [jax.ShapeDtypeStruct((T, 256), F32)] * 4
        + [jax.ShapeDtypeStruct((1, 128), F32)],
        input_output_aliases={14: 0},
        compiler_params=_cp(("arbitrary",)),
    )(qs, kr, kr, proj, proj, bias, sinks, lse, o, do, cos, sin, cos, sin, dproj)


def _dkv_combine(dkc, dkp, dvc, dvp, dproj, *, name):
    T = dkc.shape[0]
    nb = T // BLK
    tm = _tile(T, 4 * BLK)
    bpt = tm // BLK
    nt = T // tm

    def body(dkc_ref, dkp_ref, dkn_ref, dvc_ref, dvp_ref, dvn_ref, dproj_ref, o_ref):
        keep = jnp.where(pl.program_id(0) == nt - 1, 0.0, 1.0)

        def shifted(prev_ref, next_ref):
            nxt = keep * next_ref[...]
            return nxt if bpt == 1 else jnp.concatenate([prev_ref[BLK:, :], nxt], axis=0)

        o_ref[:, 0:256] = (dkc_ref[...] + shifted(dkp_ref, dkn_ref)).astype(BF)
        o_ref[:, 256:512] = (dvc_ref[...] + shifted(dvp_ref, dvn_ref)).astype(BF)

    cur = pl.BlockSpec((tm, 256), _row)
    nxt = pl.BlockSpec((BLK, 256), lambda i: (jnp.minimum((i + 1) * bpt, nb - 1), 0))
    return pl.pallas_call(
        body, name=name, grid=(nt,),
        in_specs=[cur, cur, nxt, cur, cur, nxt, pl.BlockSpec(memory_space=pl.ANY)],
        out_specs=pl.BlockSpec((tm, 512), lambda i: (i, O_K // 512)),
        out_shape=jax.ShapeDtypeStruct(dproj.shape, BF),
        input_output_aliases={6: 0},
        compiler_params=_cp(("parallel",)),
    )(dkc, dkp, dkp, dvc, dvp, dvp, dproj)


HALO = 16


def _conv_shifts(cu, hprev, tm):
    row = lax.broadcasted_iota(jnp.int32, (8, cu.shape[1]), 0)
    h1 = hprev[HALO - 1:HALO, :]
    h2 = hprev[HALO - 2:HALO - 1, :]
    m1 = pltpu.roll(cu, 1, 0)
    m2 = pltpu.roll(cu, 2, 0)
    m1 = jnp.concatenate([jnp.where(row == 0, h1, m1[0:8]), m1[8:]], axis=0)
    m2 = jnp.concatenate([jnp.where(row == 0, h2, jnp.where(row == 1, h1, m2[0:8])), m2[8:]], axis=0)
    return m1, m2


def _mixer_mid_fwd(proj, attn, wcp, wap, wout, convw, x, gt, *, tm, name):
    T = x.shape[0]
    tm = _tile(T, tm)
    hb = tm // HALO

    def body(bg_ref, cg_ref, u_ref, hcg_ref, hu_ref, zc0_ref, zc1_ref, za0_ref, za1_ref, at_ref,
             wcp_ref, wap_ref, wout_ref, cw_ref, x_ref, gt_ref,
             x2_ref, gc_ref, yc_ref, ya_ref, mg_ref, o_ref):
        first = jnp.where(pl.program_id(0) == 0, 0.0, 1.0)
        cu = cg_ref[...].astype(F32) * u_ref[...].astype(F32)
        hprev = first * (hcg_ref[...].astype(F32) * hu_ref[...].astype(F32))
        m1, m2 = _conv_shifts(cu, hprev, tm)
        cv = cw_ref[0:1, :] * m2 + cw_ref[1:2, :] * m1 + cw_ref[2:3, :] * cu
        gc = (bg_ref[...].astype(F32) * cv).astype(BF)
        gc_ref[...] = gc
        yc = jnp.dot(gc, wcp_ref[...], preferred_element_type=F32)
        ya = jnp.dot(at_ref[...], wap_ref[...], preferred_element_type=F32)
        yc_ref[...] = yc.astype(BF)
        ya_ref[...] = ya.astype(BF)
        zc = jnp.concatenate([zc0_ref[...], zc1_ref[...]], axis=1).astype(F32)
        za = jnp.concatenate([za0_ref[...], za1_ref[...]], axis=1).astype(F32)
        mg = (_sigmoid(zc) * yc + _sigmoid(za) * ya).astype(BF)
        mg_ref[...] = mg
        o = jnp.dot(mg, wout_ref[...], preferred_element_type=F32)
        o_ref[...] = o.astype(BF)
        x2_ref[...] = x_ref[...] + gt_ref[...] * o

    wspec = pl.BlockSpec((D, D), _const2)
    rowspec = pl.BlockSpec((tm, D), _row)
    return pl.pallas_call(
        body, name=name, grid=(T // tm,),
        in_specs=[_col(tm, O_BG), _col(tm, O_CG), _col(tm, O_U), _halo_prev(hb, O_CG), _halo_prev(hb, O_U),
                  _col(tm, O_ZC, 512), _col(tm, O_ZC + 512, 512), _col(tm, O_ZA, 512), _col(tm, O_ZA + 512, 512),
                  rowspec, wspec, wspec, wspec, pl.BlockSpec((8, D), _const2), rowspec, pl.BlockSpec((1, D), _const2)],
        out_specs=[rowspec] * 6,
        out_shape=[jax.ShapeDtypeStruct((T, D), F32)] + [jax.ShapeDtypeStruct((T, D), BF)] * 5,
        compiler_params=_cp(("parallel",)),
    )(proj, proj, proj, proj, proj, proj, proj, proj, proj, attn, wcp, wap, wout, convw, x, gt)


def _col(tm, c, w=D):
    assert c % w == 0
    return pl.BlockSpec((tm, w), lambda i: (i, c // w))


def _halo_prev(hb, c):
    return pl.BlockSpec((HALO, D), lambda i: (jnp.maximum(i * hb - 1, 0), c // D))


def _halo_next(hb, nblk, c=0):
    return pl.BlockSpec((HALO, D), lambda i: (jnp.minimum((i + 1) * hb, nblk - 1), c // D))


def _mixer_mid_bwd(dx2, gt, o, proj, yc, ya, wout, wcp, wap, *, tm, name):
    T = dx2.shape[0]
    tm = _tile(T, tm)
    nt = T // tm

    def body(dx_ref, gt_ref, o_ref, zc0_ref, zc1_ref, za0_ref, za1_ref, yc_ref, ya_ref, wout_ref, wcp_ref, wap_ref,
             dout_ref, dyc_ref, dya_ref, dgc_ref, dat_ref, dproj_ref, dgt_ref, dzs, sems):
        i = pl.program_id(0)
        slot = lax.rem(i, 2)

        def slab_copy(step, s):
            return pltpu.make_async_copy(
                dzs.at[s], dproj_ref.at[pl.ds(pl.multiple_of(step * tm, tm), tm), pl.ds(O_ZC, 2 * D)], sems.at[s])

        @pl.when(i == 0)
        def _():
            dgt_ref[...] = jnp.zeros_like(dgt_ref)

        dxv = dx_ref[...]
        dgt_ref[...] += jnp.sum(dxv * o_ref[...].astype(F32), axis=0, keepdims=True)
        dout = (gt_ref[...] * dxv).astype(BF)
        dout_ref[...] = dout
        dmg = lax.dot_general(dout, wout_ref[...], NT, preferred_element_type=F32)
        sc = _sigmoid(jnp.concatenate([zc0_ref[...], zc1_ref[...]], axis=1).astype(F32))
        sa = _sigmoid(jnp.concatenate([za0_ref[...], za1_ref[...]], axis=1).astype(F32))
        dyc = (dmg * sc).astype(BF)
        dya = (dmg * sa).astype(BF)
        dyc_ref[...] = dyc
        dya_ref[...] = dya
        dzs[slot, :, 0:D] = (dmg * yc_ref[...].astype(F32) * (sc * (1.0 - sc))).astype(BF)
        dzs[slot, :, D:2 * D] = (dmg * ya_ref[...].astype(F32) * (sa * (1.0 - sa))).astype(BF)
        slab_copy(i, slot).start()
        dgc_ref[...] = lax.dot_general(dyc, wcp_ref[...], NT, preferred_element_type=F32).astype(BF)
        dat_ref[...] = lax.dot_general(dya, wap_ref[...], NT, preferred_element_type=F32).astype(BF)

        @pl.when(i > 0)
        def _():
            slab_copy(i - 1, 1 - slot).wait()

        @pl.when(i == nt - 1)
        def _():
            slab_copy(i, slot).wait()

    def zcol(c):
        return pl.BlockSpec((tm, 512), lambda i: (i, c // 512))

    wspec = pl.BlockSpec((D, D), _const2)
    rowspec = pl.BlockSpec((tm, D), _row)
    vec = pl.BlockSpec((1, D), _const2)
    return pl.pallas_call(
        body, name=name, grid=(nt,),
        in_specs=[rowspec, vec, rowspec, zcol(O_ZC), zcol(O_ZC + 512), zcol(O_ZA), zcol(O_ZA + 512),
                  rowspec, rowspec, wspec, wspec, wspec],
        out_specs=[rowspec] * 5 + [pl.BlockSpec(memory_space=pl.ANY), vec],
        out_shape=[jax.ShapeDtypeStruct((T, D), BF)] * 5 + [jax.ShapeDtypeStruct((T, NIN), BF),
                                                            jax.ShapeDtypeStruct((1, D), F32)],
        scratch_shapes=[pltpu.VMEM((2, tm, 2 * D), BF), pltpu.SemaphoreType.DMA((2,))],
        compiler_params=_cp(("arbitrary",)),
    )(dx2, gt, o, proj, proj, proj, proj, yc, ya, wout, wcp, wap)


def _conv_bwd(dgc, proj, convw, dproj, *, tm, name):
    T = dgc.shape[0]
    tm = _tile(T, tm)
    hb = tm // HALO
    nblk = T // HALO
    nt = T // tm

    def body(dgc_ref, ndgc_ref, bg_ref, nbg_ref, cg_ref, u_ref, hcg_ref, hu_ref, cw_ref, dproj_ref, dp_ref, dcw_ref):
        i = pl.program_id(0)

        @pl.when(i == 0)
        def _():
            dcw_ref[...] = jnp.zeros_like(dcw_ref)
        first = jnp.where(i == 0, 0.0, 1.0)
        last = jnp.where(i == nt - 1, 0.0, 1.0)
        cg = cg_ref[...].astype(F32)
        u = u_ref[...].astype(F32)
        bg = bg_ref[...].astype(F32)
        dg = dgc_ref[...].astype(F32)
        cu = cg * u
        hprev = first * (hcg_ref[...].astype(F32) * hu_ref[...].astype(F32))
        m1, m2 = _conv_shifts(cu, hprev, tm)
        w0, w1, w2 = cw_ref[0:1, :], cw_ref[1:2, :], cw_ref[2:3, :]
        cv = w0 * m2 + w1 * m1 + w2 * cu
        dcv = dg * bg
        nxt = last * (ndgc_ref[...].astype(F32) * nbg_ref[...].astype(F32))
        n0, n1 = nxt[0:1, :], nxt[1:2, :]
        row = lax.broadcasted_iota(jnp.int32, (8, D), 0)
        p1 = pltpu.roll(dcv, tm - 1, 0)
        p2 = pltpu.roll(dcv, tm - 2, 0)
        p1 = jnp.concatenate([p1[:tm - 8], jnp.where(row == 7, n0, p1[tm - 8:])], axis=0)
        p2 = jnp.concatenate([p2[:tm - 8], jnp.where(row == 7, n1, jnp.where(row == 6, n0, p2[tm - 8:]))], axis=0)
        dcu = w2 * dcv + w1 * p1 + w0 * p2
        dp_ref[:, 0:D] = (dg * cv).astype(BF)
        dp_ref[:, D:2 * D] = (dcu * u).astype(BF)
        dp_ref[:, 2 * D:3 * D] = (dcu * cg).astype(BF)
        dcw_ref[0:1, :] += jnp.sum(dcv * m2, axis=0, keepdims=True)
        dcw_ref[1:2, :] += jnp.sum(dcv * m1, axis=0, keepdims=True)
        dcw_ref[2:3, :] += jnp.sum(dcv * cu, axis=0, keepdims=True)

    rowspec = pl.BlockSpec((tm, D), _row)
    cw = pl.BlockSpec((8, D), _const2)
    return pl.pallas_call(
        body, name=name, grid=(nt,),
        in_specs=[rowspec, _halo_next(hb, nblk), _col(tm, O_BG), _halo_next(hb, nblk, O_BG),
                  _col(tm, O_CG), _col(tm, O_U), _halo_prev(hb, O_CG), _halo_prev(hb, O_U), cw,
                  pl.BlockSpec(memory_space=pl.ANY)],
        out_specs=[pl.BlockSpec((tm, 3 * D), _row), cw],
        out_shape=[jax.ShapeDtypeStruct(dproj.shape, BF), jax.ShapeDtypeStruct((8, D), F32)],
        input_output_aliases={9: 0},
        compiler_params=_cp(("arbitrary",)),
    )(dgc, dgc, proj, proj, proj, proj, proj, proj, convw, dproj)


def _adam_math(w, g, m, v):
    nm = ADAM_B1 * m + (1.0 - ADAM_B1) * g
    nv = ADAM_B2 * v + (1.0 - ADAM_B2) * (g * g)
    m_hat = nm / (1.0 - ADAM_B1 ** ADAM_STEP)
    v_hat = nv / (1.0 - ADAM_B2 ** ADAM_STEP)
    return -ADAM_LR * (m_hat / (jnp.sqrt(v_hat) + ADAM_EPS) + ADAM_WD * w), nm, nv


SMALL = ("b_ada", "g_ffn1", "g_mix", "g_ffn2", "g_final", "conv_w", "sinks")


def _adam_small(gsum, conv_g, w, m, v, *, name):
    nsm = len(SMALL)

    def body(*refs):
        gs_ref, cg_ref = refs[0], refs[1]
        w_refs, m_refs, v_refs = (refs[2 + k * nsm:2 + (k + 1) * nsm] for k in range(3))
        outs = refs[2 + 3 * nsm:]
        for p, n in enumerate(SMALL):
            if n == "b_ada":
                pieces = [(slice(None), slice(r * D, (r + 1) * D), gs_ref[R_MODS + r:R_MODS + r + 1, :])
                          for r in range(N_MOD)]
            elif n == "conv_w":
                pieces = [(slice(None), slice(None), cg_ref[...])]
            elif n == "sinks":
                pieces = [(slice(None), slice(None), gs_ref[R_SINK:R_SINK + 1, 0:N_HEADS])]
            else:
                row = dict(g_ffn1=R_G1, g_mix=R_GM, g_ffn2=R_G2, g_final=R_GF)[n]
                pieces = [(slice(None), slice(None), gs_ref[row:row + 1, :])]
            for rs, cs, g in pieces:
                d, nm, nv = _adam_math(w_refs[p][rs, cs], g, m_refs[p][rs, cs], v_refs[p][rs, cs])
                for k, val in enumerate((g, d, nm, nv)):
                    outs[k * nsm + p][rs, cs] = val

    args = [gsum, conv_g] + [d[n] for d in (w, m, v) for n in SMALL]
    shapes = [jax.ShapeDtypeStruct(w[n].shape, F32) for _ in range(4) for n in SMALL]
    res = pl.pallas_call(body, name=name, out_shape=shapes, compiler_params=_cp())(*args)
    return [dict(zip(SMALL, res[k * nsm:(k + 1) * nsm])) for k in range(4)]


def _adam(w, g, m, v, *, tm, name):
    _, R, C = w.shape
    tm = _tile(R, tm)
    parts = g.ndim == 3

    def body(w_ref, g_ref, m_ref, v_ref, go_ref, d_ref, nm_ref, nv_ref):
        if parts:
            gv = g_ref[0].astype(F32)
            for s in range(1, N_DEV):
                gv = gv + g_ref[s].astype(F32)
        else:
            gv = g_ref[...]
        go_ref[0] = gv
        d_ref[0], nm_ref[0], nv_ref[0] = _adam_math(w_ref[0], gv, m_ref[0], v_ref[0])

    spec = pl.BlockSpec((1, tm, C), lambda i: (0, i, 0))
    gspec = pl.BlockSpec((N_DEV, tm, C), lambda i: (0, i, 0)) if parts else pl.BlockSpec((tm, C), _row)
    return pl.pallas_call(
        body, name=name, grid=(R // tm,),
        in_specs=[spec, gspec, spec, spec], out_specs=[spec] * 4,
        out_shape=[jax.ShapeDtypeStruct((1, R, C), F32)] * 4,
        compiler_params=_cp(("parallel",)),
    )(w, g, m, v)


def _mods_part(c_all, w_ada, b_ada, *, name):
    C = w_ada.shape[1]

    def body(c_ref, w_ref, b_ref, o_ref):
        cv = c_ref[...]
        ca = cv * jax.nn.sigmoid(cv)
        o_ref[...] = jnp.dot(ca, w_ref[...], preferred_element_type=F32,
                             precision=lax.Precision.HIGHEST) + b_ref[...]

    return pl.pallas_call(
        body, name=name,
        out_shape=jax.ShapeDtypeStruct((N_DEV, C), F32),
        compiler_params=_cp(),
    )(c_all, w_ada, b_ada)


def _wada_grad(c_all_t, gm, *, name):
    C = gm.shape[1]

    def body(c_ref, g_ref, o_ref):
        cv = c_ref[...]
        ca = cv * jax.nn.sigmoid(cv)
        acc = ca[:, 0:1] * g_ref[0:1, :]
        for b in range(1, N_DEV):
            acc = acc + ca[:, b:b + 1] * g_ref[b:b + 1, :]
        o_ref[...] = acc

    return pl.pallas_call(
        body, name=name,
        out_shape=jax.ShapeDtypeStruct((D, C), F32),
        compiler_params=_cp(),
    )(c_all_t, gm)


def _peer(x, y, c, d):
    px = lax.rem(x + ((d >> 2) & 1), 2)
    py = lax.rem(y + ((d >> 1) & 1), 2)
    pc = lax.rem(c + (d & 1), 2)
    return (px, py, pc), 4 * px + 2 * py + pc


def _exchange(xs, *, scatter, name):
    n = len(xs)
    nsem = n * (N_DEV - 1)

    def body(*refs):
        ins, outs = refs[:n], refs[n:2 * n]
        token, send_sems, recv_sems, local_sems = refs[2 * n:]
        x, y, c = lax.axis_index("x"), lax.axis_index("y"), lax.axis_index("c")
        me = 4 * x + 2 * y + c
        token[...] = jnp.zeros_like(token)

        def src(t, idx):
            return ins[t].at[idx] if scatter else ins[t]

        local = [pltpu.make_async_copy(src(t, me), outs[t].at[me], local_sems.at[t]) for t in range(n)]
        for cp in local:
            cp.start()
        remote = []
        for t in range(n):
            for d in range(1, N_DEV):
                peer, pidx = _peer(x, y, c, d)
                k = t * (N_DEV - 1) + d - 1
                send = pltpu.make_async_remote_copy(src_ref=src(t, pidx), dst_ref=outs[t].at[me],
                                                    send_sem=send_sems.at[k], recv_sem=recv_sems.at[k],
                                                    device_id=peer, device_id_type=MESH)
                recv = pltpu.make_async_remote_copy(src_ref=src(t, pidx), dst_ref=outs[t].at[pidx],
                                                    send_sem=send_sems.at[k], recv_sem=recv_sems.at[k],
                                                    device_id=peer, device_id_type=MESH)
                send.start()
                remote.append((send, recv))
        for cp in local:
            cp.wait()
        for send, recv in remote:
            send.wait_send()
            recv.wait_recv()

    anyspec = pl.BlockSpec(memory_space=pl.ANY)
    out_shape = [jax.ShapeDtypeStruct(a.shape if scatter else (N_DEV,) + a.shape, a.dtype) for a in xs]
    out_shape.append(jax.ShapeDtypeStruct((8, 128), F32))
    return pl.pallas_call(
        body, name=name,
        in_specs=[anyspec] * n, out_specs=[anyspec] * n + [pl.BlockSpec(memory_space=pltpu.VMEM)],
        out_shape=out_shape,
        scratch_shapes=[pltpu.SemaphoreType.DMA((nsem,)), pltpu.SemaphoreType.DMA((nsem,)),
                        pltpu.SemaphoreType.DMA((n,))],
    )(*xs)


def _sum8(parts, *, name):
    _, R, C = parts.shape

    def body(p_ref, o_ref):
        acc = p_ref[0]
        for s in range(1, N_DEV):
            acc = acc + p_ref[s]
        o_ref[...] = acc

    return pl.pallas_call(body, name=name, out_shape=jax.ShapeDtypeStruct((R, C), F32),
                          compiler_params=_cp())(parts)


HBM_SPEC = pl.BlockSpec(memory_space=pltpu.HBM)
SEM_SPEC = pl.BlockSpec(memory_space=pltpu.SEMAPHORE)
N_PEER = N_DEV - 1


def _split_copies(src_refs, land_refs, send_sems, recv_sems, scatter):
    x, y, c = lax.axis_index("x"), lax.axis_index("y"), lax.axis_index("c")
    me = 4 * x + 2 * y + c
    pairs = []
    for j, (src, land) in enumerate(zip(src_refs, land_refs)):
        for d in range(1, N_DEV):
            peer, pidx = _peer(x, y, c, d)
            k = j * N_PEER + d - 1
            s = src.at[pidx] if scatter else src
            send = pltpu.make_async_remote_copy(src_ref=s, dst_ref=land.at[me], send_sem=send_sems.at[k],
                                                recv_sem=recv_sems.at[k], device_id=peer, device_id_type=MESH)
            recv = pltpu.make_async_remote_copy(src_ref=s, dst_ref=land.at[pidx], send_sem=send_sems.at[k],
                                                recv_sem=recv_sems.at[k], device_id=peer, device_id_type=MESH)
            pairs.append((send, recv))
    return pairs


def _own_slot(block, me):
    land = lax.empty((N_DEV,) + block.shape, block.dtype)
    return lax.dynamic_update_slice(land, block[None], (me, 0, 0))


def _split_start(srcs, lands, groups, *, scatter, name):
    n, ng = len(srcs), len(groups)

    def body(*refs):
        src_refs, land_refs = refs[:n], refs[n:2 * n]
        sems = refs[2 * n:2 * n + 2 * ng]
        token = refs[-1]
        for gi, g in enumerate(groups):
            pairs = _split_copies([src_refs[t] for t in g], [land_refs[t] for t in g], sems[2 * gi],
                                  sems[2 * gi + 1], scatter)
            for send, _ in pairs:
                send.start()
        token[...] = jnp.zeros_like(token)

    sem_shapes = []
    for g in groups:
        sem_shapes += [pltpu.SemaphoreType.DMA((len(g) * N_PEER,))] * 2
    thru = [pltpu.HBM(a.shape, a.dtype) for a in list(srcs) + list(lands)]
    outs = pl.pallas_call(
        body, name=name,
        out_shape=tuple(sem_shapes + thru + [jax.ShapeDtypeStruct((8, 128), F32)]),
        in_specs=[HBM_SPEC] * (2 * n),
        out_specs=tuple([SEM_SPEC] * (2 * ng) + [HBM_SPEC] * (2 * n) + [pl.BlockSpec(memory_space=pltpu.VMEM)]),
        input_output_aliases={i: 2 * ng + i for i in range(2 * n)},
        compiler_params=pltpu.CompilerParams(has_side_effects=pltpu.SideEffectType.DATAFLOW_SIDE_EFFECTING),
    )(*[pltpu.with_memory_space_constraint(a, pltpu.HBM) for a in list(srcs) + list(lands)])
    sems = [(outs[2 * gi], outs[2 * gi + 1]) for gi in range(ng)]
    return sems, outs[2 * ng:2 * ng + n], outs[2 * ng + n:2 * ng + 2 * n], outs[-1]


def _behind(v, token):
    if token is None:
        return v
    return v + token[0, 0].astype(v.dtype)


def _split_wait(srcs, lands, sems, after, *, scatter, name):
    m = len(srcs)

    def body(*refs):
        src_refs, land_refs = refs[:m], refs[m:2 * m]
        send_sems, recv_sems = refs[2 * m], refs[2 * m + 1]
        for send, recv in _split_copies(src_refs, land_refs, send_sems, recv_sems, scatter):
            send.wait_send()
            recv.wait_recv()

    outs = pl.pallas_call(
        body, name=name,
        out_shape=tuple(pltpu.HBM(a.shape, a.dtype) for a in list(srcs) + list(lands)),
        in_specs=[HBM_SPEC] * (2 * m) + [SEM_SPEC, SEM_SPEC, pl.BlockSpec(memory_space=pl.ANY)],
        out_specs=tuple([HBM_SPEC] * (2 * m)),
        input_output_aliases={i: i for i in range(2 * m)},
        compiler_params=pltpu.CompilerParams(has_side_effects=pltpu.SideEffectType.DATAFLOW_SIDE_EFFECTING),
    )(*srcs, *lands, sems[0], sems[1], after)
    return outs[m:]


TL_FIRST = (1, 2, 4, 6)
TL_ICI = (2, 4, 6)
EFFECT = pltpu.SideEffectType.DATAFLOW_SIDE_EFFECTING


def _tl_first(src_refs, land_refs, send_sems, recv_sems):
    x, y, c = lax.axis_index("x"), lax.axis_index("y"), lax.axis_index("c")
    me = 4 * x + 2 * y + c
    out = []
    for j, (src, land) in enumerate(zip(src_refs, land_refs)):
        for i, d in enumerate(TL_FIRST):
            peer, pidx = _peer(x, y, c, d)
            k = len(TL_FIRST) * j + i
            send = pltpu.make_async_remote_copy(src_ref=src, dst_ref=land.at[me], send_sem=send_sems.at[k],
                                                recv_sem=recv_sems.at[k], device_id=peer, device_id_type=MESH)
            recv = pltpu.make_async_remote_copy(src_ref=src, dst_ref=land.at[pidx], send_sem=send_sems.at[k],
                                                recv_sem=recv_sems.at[k], device_id=peer, device_id_type=MESH)
            out.append((d, send, recv))
    return out


def _tl_second(land_refs, send_sems, recv_sems):
    x, y, c = lax.axis_index("x"), lax.axis_index("y"), lax.axis_index("c")
    sibling, _ = _peer(x, y, c, 1)
    out = []
    for j, land in enumerate(land_refs):
        for i, d in enumerate(TL_ICI):
            _, mine = _peer(x, y, c, d)
            _, theirs = _peer(x, y, c, d + 1)
            k = len(TL_ICI) * j + i
            send = pltpu.make_async_remote_copy(src_ref=land.at[mine], dst_ref=land.at[mine], send_sem=send_sems.at[k],
                                                recv_sem=recv_sems.at[k], device_id=sibling, device_id_type=MESH)
            recv = pltpu.make_async_remote_copy(src_ref=land.at[mine], dst_ref=land.at[theirs],
                                                send_sem=send_sems.at[k], recv_sem=recv_sems.at[k],
                                                device_id=sibling, device_id_type=MESH)
            out.append((send, recv))
    return out


def _tl_start(srcs, lands, groups, *, name):
    n, ng = len(srcs), len(groups)

    def body(*refs):
        src_refs, land_refs = refs[:n], refs[n:2 * n]
        sems = refs[2 * n:2 * n + 2 * ng]
        for gi, g in enumerate(groups):
            for _, send, _ in _tl_first([src_refs[t] for t in g], [land_refs[t] for t in g], sems[2 * gi],
                                        sems[2 * gi + 1]):
                send.start()
        refs[-1][...] = jnp.zeros_like(refs[-1])

    sem_shapes = []
    for g in groups:
        sem_shapes += [pltpu.SemaphoreType.DMA((len(g) * len(TL_FIRST),))] * 2
    thru = [pltpu.HBM(a.shape, a.dtype) for a in list(srcs) + list(lands)]
    outs = pl.pallas_call(
        body, name=name,
        out_shape=tuple(sem_shapes + thru + [jax.ShapeDtypeStruct((8, 128), F32)]),
        in_specs=[HBM_SPEC] * (2 * n),
        out_specs=tuple([SEM_SPEC] * (2 * ng) + [HBM_SPEC] * (2 * n) + [pl.BlockSpec(memory_space=pltpu.VMEM)]),
        input_output_aliases={i: 2 * ng + i for i in range(2 * n)},
        compiler_params=pltpu.CompilerParams(has_side_effects=EFFECT),
    )(*[pltpu.with_memory_space_constraint(a, pltpu.HBM) for a in list(srcs) + list(lands)])
    sems = [(outs[2 * gi], outs[2 * gi + 1]) for gi in range(ng)]
    return sems, outs[2 * ng:2 * ng + n], outs[2 * ng + n:2 * ng + 2 * n], outs[-1]


def _tl_forward(srcs, lands, sems1, after, *, name):
    m = len(srcs)

    def body(*refs):
        src_refs, land_refs = refs[:m], refs[m:2 * m]
        send1, recv1 = refs[2 * m], refs[2 * m + 1]
        send2, recv2 = refs[2 * m + 3], refs[2 * m + 4]
        for d, _, recv in _tl_first(src_refs, land_refs, send1, recv1):
            if d in TL_ICI:
                recv.wait_recv()
        for send, _ in _tl_second(land_refs, send2, recv2):
            send.start()

    sem = pltpu.SemaphoreType.DMA((m * len(TL_ICI),))
    outs = pl.pallas_call(
        body, name=name,
        out_shape=tuple([sem, sem] + [pltpu.HBM(a.shape, a.dtype) for a in list(srcs) + list(lands)]),
        in_specs=[HBM_SPEC] * (2 * m) + [SEM_SPEC, SEM_SPEC, pl.BlockSpec(memory_space=pl.ANY)],
        out_specs=tuple([SEM_SPEC, SEM_SPEC] + [HBM_SPEC] * (2 * m)),
        input_output_aliases={i: 2 + i for i in range(2 * m)},
        compiler_params=pltpu.CompilerParams(has_side_effects=EFFECT),
    )(*srcs, *lands, sems1[0], sems1[1], after)
    return (outs[0], outs[1]), outs[2:2 + m], outs[2 + m:2 + 2 * m]


def _tl_wait(srcs, lands, sems1, sems2, after, *, name):
    m = len(srcs)

    def body(*refs):
        src_refs, land_refs = refs[:m], refs[m:2 * m]
        send1, recv1, send2, recv2 = refs[2 * m:2 * m + 4]
        for d, send, recv in _tl_first(src_refs, land_refs, send1, recv1):
            send.wait_send()
            if d not in TL_ICI:
                recv.wait_recv()
        for send, recv in _tl_second(land_refs, send2, recv2):
            send.wait_send()
            recv.wait_recv()

    outs = pl.pallas_call(
        body, name=name,
        out_shape=tuple(pltpu.HBM(a.shape, a.dtype) for a in list(srcs) + list(lands)),
        in_specs=[HBM_SPEC] * (2 * m) + [SEM_SPEC] * 4 + [pl.BlockSpec(memory_space=pl.ANY)],
        out_specs=tuple([HBM_SPEC] * (2 * m)),
        input_output_aliases={i: i for i in range(2 * m)},
        compiler_params=pltpu.CompilerParams(has_side_effects=EFFECT),
    )(*srcs, *lands, sems1[0], sems1[1], sems2[0], sems2[1], after)
    return outs[m:]


TM_PROJ = 512
TN_PROJ = 512
TM_ROW = 512
TM_NN = 512
TK_TN = 2048
TM_ADAM = 416
TN_FFN = F // 2
TN_IN = NIN // 4


def _tn(a, b, name, tn, token=None):
    if a.ndim == 2:
        a = a[None]
    return _tn_matmul(a, b, token, tn=tn, tk=TK_TN, name=name)


def _local_step(x, tgt, mods, g1, gm, g2, gf, convw8, sinks, w_get, g_put, tables=None):
    T = x.shape[0]
    sh1, sc1, gt1, sh2, sc2, gt2, sh3, sc3, gt3 = [mods[i:i + 1] for i in range(N_MOD)]
    cos, sin = _rope_tables(T) if tables is None else tables
    behind = _behind

    w = dict(w_get("gu1", mods))
    h1, ab1 = _norm_proj(x, g1, sc1, sh1, w["gu1"], tm=TM_PROJ, tn=TN_PROJ, name="ffn1_up")
    w.update(w_get("d1", ab1))
    x1, y1 = _ffn_down_fwd(ab1, w["d1"], x, gt1, tm=TM_ROW, name="ffn1_down")
    w.update(w_get("mix", x1))
    h2, proj, qs, kr = _norm_proj(x1, gm, sc2, sh2, w["win"], (cos, sin), tm=TM_PROJ, tn=TN_PROJ, name="mix_in")
    bias = _attn_bias()
    attn, lse = _attn_fwd(qs, kr, proj, bias, sinks, name="attn_fwd")
    x2, gc, yc, ya, mg, o = _mixer_mid_fwd(proj, attn, w["cp"], w["ap"], w["out"], convw8, x1, gt2,
                                           tm=TM_ROW, name="mix_mid")
    w.update(w_get("ffn2", x2))
    h3, ab2, y2, dx3, lsum, dgf = _ffn_fwd(x2, g2, sc3, sh3, gt3, w["gu2"], w["d2"], (tgt, gf), tm=TM_ROW,
                                           name="ffn2_final")

    dab2, dgt3, g_d2 = _ffn_down_bwd_dw(dx3, y2, gt3, ab2, w["d2"], tm=TM_ROW, name="ffn2_down_bwd")
    dx2, dsh3, dsc3, dg2 = _nn_bwd_norm(dab2, w["gu2"], x2, g2, sc3, dx3, tm=TM_NN, name="ffn2_up_bwd")
    g_gu2 = _tn(dab2, h3, "ffn2_up_dw", TN_FFN)
    tok = g_put(dict(gu2=g_gu2, d2=g_d2))

    dout, dyc, dya, dgc, dat, dproj, dgt2 = _mixer_mid_bwd(dx2, behind(gt2, tok), o, proj, yc, ya, w["out"], w["cp"],
                                                           w["ap"], tm=TM_ROW, name="mix_mid_bwd")
    g_out = _tn(mg, dout, "mix_out_dw", D)
    g_cp = _tn(gc, dyc, "mix_cp_dw", D)
    g_ap = _tn(attn, dya, "mix_ap_dw", D)
    dproj, dkc, dkp, dvc, dvp, dsink = _attn_bwd(qs, kr, proj, bias, sinks, lse, attn, dat, cos, sin, dproj,
                                                 name="attn_bwd")
    dproj = _dkv_combine(dkc, dkp, dvc, dvp, dproj, name="attn_dkv")
    dproj, dcw = _conv_bwd(dgc, proj, convw8, dproj, tm=TM_ROW, name="conv_bwd")
    g_in = _tn(dproj, h2, "mix_in_dw", TN_IN)
    tok = g_put(dict(win=g_in, cp=g_cp, ap=g_ap, out=g_out))
    dx1, dsh2, dsc2, dgm = _nn_bwd_norm(dproj[None], w["win"], x1, gm, behind(sc2, tok), dx2, tm=TM_NN,
                                        name="mix_in_bwd")

    dab1, dgt1, g_d1 = _ffn_down_bwd_dw(dx1, y1, gt1, ab1, w["d1"], tm=TM_ROW, name="ffn1_down_bwd")
    tok = g_put(dict(d1=g_d1))
    g_gu1 = _tn(dab1, h1, "ffn1_up_dw", TN_FFN, tok)
    tok = g_put(dict(gu1=g_gu1))
    dx0, dsh1, dsc1, dg1 = _nn_bwd_norm(dab1, w["gu1"], x, g1, behind(sc1, tok), dx1, tm=TM_NN,
                                        name="ffn1_up_bwd")

    small = dict(mods=jnp.concatenate([dsh1, dsc1, dgt1, dsh2, dsc2, dgt2, dsh3, dsc3, dgt3], axis=0),
                 g1=dg1, gm=dgm, g2=dg2, gf=dgf, convw=dcw[0:3], sinks=dsink[:, 0:N_HEADS])
    return lsum, dx0, small


BIG = ("gu1", "d1", "win", "cp", "ap", "out", "gu2", "d2")
TRANSPOSED = ("gu1", "win", "gu2")
SMALL_ROWS = 24
R_MODS, R_G1, R_GM, R_G2, R_GF, R_CONV, R_SINK, R_LOSS = 0, 9, 10, 11, 12, 13, 16, 17


def _pad_to(a, rows, cols):
    return jnp.pad(a, ((0, rows - a.shape[0]), (0, cols - a.shape[1])))


def _pack_small(b_ada, g1, gm, g2, gf, conv, sinks, lsum):
    rows = [b_ada.reshape(N_MOD, D), g1.reshape(1, D), gm.reshape(1, D), g2.reshape(1, D), gf.reshape(1, D),
            _pad_to(conv.reshape(3, -1), 3, D), _pad_to(sinks.reshape(1, N_HEADS), 1, D), lsum.reshape(1, D)]
    return _pad_to(jnp.concatenate(rows, axis=0), SMALL_ROWS, D)


def kernel(x, c, w_ada, b_ada, g_ffn1, w1_gu, w1_down, g_mix, w_in, conv_w, w_conv_proj, w_attn_proj, sinks, w_out, g_ffn2, w2_gu, w2_down, g_final, loss_target, m_w_ada, m_b_ada, m_g_ffn1, m_w1_gu, m_w1_down, m_g_mix, m_w_in, m_conv_w, m_w_conv_proj, m_w_attn_proj, m_sinks, m_w_out, m_g_ffn2, m_w2_gu, m_w2_down, m_g_final, v_w_ada, v_b_ada, v_g_ffn1, v_w1_gu, v_w1_down, v_g_mix, v_w_in, v_conv_w, v_w_conv_proj, v_w_attn_proj, v_sinks, v_w_out, v_g_ffn2, v_w2_gu, v_w2_down, v_g_final):
    me = 4 * lax.axis_index("x") + 2 * lax.axis_index("y") + lax.axis_index("c")
    ada_cols = w_ada.shape[2]
    conv_cols = conv_w.shape[2]

    native = dict(gu1=w1_gu[0], d1=w1_down[0], win=w_in[0], cp=w_conv_proj[0], ap=w_attn_proj[0], out=w_out[0],
                  gu2=w2_gu[0], d2=w2_down[0])

    def shard(n, token):
        a = _behind(native[n], token)
        return (a.T if n in TRANSPOSED else a).astype(BF)

    c_all, conv_all, _ = _exchange([c, _pad_to(conv_w[0], 8, conv_cols)], scatter=False, name="gather_cond")
    c_all = c_all.reshape(N_DEV, D)
    conv_full = conv_all[:, 0:3, :].transpose(1, 0, 2).reshape(3, D)

    b_cols = lax.dynamic_slice(b_ada, (0, me * ada_cols), (1, ada_cols))
    mods_cols = _mods_part(c_all, w_ada[0], b_cols, name="ada_mods")
    mods_all, mods_token = _exchange([mods_cols], scatter=False, name="gather_mods")
    mods = lax.dynamic_index_in_dim(mods_all, me, axis=1, keepdims=False).reshape(N_MOD, D)

    groups = dict(gu1=("gu1",), d1=("d1",), mix=("win", "cp", "ap", "out"), ffn2=("gu2", "d2"))
    in_flight = {}
    first = [shard("gu1", mods_token)]
    sems, srcs, lands, token = _tl_start(first, [_own_slot(s, me) for s in first], [[0]],
                                         name="gather_weights_start_gu1")
    in_flight["gu1"] = [sems[0], srcs, lands, None]
    rest = [n for n in BIG if n != "gu1"]
    shards = [shard(n, token) for n in rest]
    rest_groups = [[rest.index(n) for n in names] for g, names in groups.items() if g != "gu1"]
    sems, srcs, lands, rest_token = _tl_start(shards, [_own_slot(s, me) for s in shards], rest_groups,
                                              name="gather_weights_start_rest")
    for (g, names), gsems, idx in zip([kv for kv in groups.items() if kv[0] != "gu1"], sems, rest_groups):
        in_flight[g] = [gsems, [srcs[t] for t in idx], [lands[t] for t in idx], None]

    def forward(group, after):
        sems1, gsrcs, glands, _ = in_flight[group]
        sems2, gsrcs, glands = _tl_forward(gsrcs, glands, sems1, after, name="gather_weights_forward_" + group)
        in_flight[group] = [sems1, gsrcs, glands, sems2]

    forward_early = dict(d1="mix", mix="ffn2")

    tables = _rope_tables(x.shape[1], rest_token)

    def w_get(group, after):
        if group == "gu1":
            after = tables[0]
        if in_flight[group][3] is None:
            forward(group, after)
        sems1, gsrcs, glands, sems2 = in_flight[group]
        landed = _tl_wait(gsrcs, glands, sems1, sems2, after, name="gather_weights_wait_" + group)
        if group in forward_early:
            forward(forward_early[group], landed[0])
        return {n: a.reshape(-1, D) for n, a in zip(groups[group], landed)}

    pending = []

    def g_put(gs):
        names = tuple(gs)
        srcs = [gs[n].reshape(N_DEV, -1, D) for n in names]
        lands = [_own_slot(lax.dynamic_index_in_dim(s, me, axis=0, keepdims=False), me) for s in srcs]
        sems, srcs, lands, tok = _split_start(srcs, lands, [list(range(len(names)))], scatter=True,
                                              name="scatter_grads_start_" + names[0])
        pending.append((names, sems[0], srcs, lands))
        return tok

    lsum, grad_x, small = _local_step(x[0], loss_target[0], mods, g_ffn1, g_mix, g_ffn2, g_final[None],
                                      _pad_to(conv_full, 8, D), sinks[0], w_get, g_put, tables)

    packed = _pack_small(small["mods"], small["g1"], small["gm"], small["g2"], small["gf"], small["convw"],
                         small["sinks"], lsum)
    sm_sems, sm_srcs, sm_lands, sm_token = _split_start([packed], [_own_slot(packed, me)], [[0]], scatter=False,
                                                        name="gather_small_start")

    w_of = dict(ada=w_ada, gu1=w1_gu, d1=w1_down, win=w_in, cp=w_conv_proj, ap=w_attn_proj, out=w_out, gu2=w2_gu,
                d2=w2_down)
    m_of = dict(ada=m_w_ada, gu1=m_w1_gu, d1=m_w1_down, win=m_w_in, cp=m_w_conv_proj, ap=m_w_attn_proj, out=m_w_out,
                gu2=m_w2_gu, d2=m_w2_down)
    v_of = dict(ada=v_w_ada, gu1=v_w1_gu, d1=v_w1_down, win=v_w_in, cp=v_w_conv_proj, ap=v_w_attn_proj, out=v_w_out,
                gu2=v_w2_gu, d2=v_w2_down)
    upd = {}
    after = sm_token
    for k, (names, sems, srcs, lands) in enumerate(pending):
        if k == 2:
            (packed_all,) = _split_wait(sm_srcs, sm_lands, sm_sems[0], after, scatter=False, name="gather_small_wait")
            gsmall = _sum8(packed_all, name="sum_small")
            loss = (0.5 / D) * jnp.sum(gsmall[R_LOSS])
            after = gsmall
        parts = _split_wait(srcs, lands, sems, after, scatter=True, name="scatter_grads_wait_" + names[0])
        for n, p in zip(names, parts):
            if n in TRANSPOSED:
                res = _adam(jnp.swapaxes(w_of[n], 1, 2), p, jnp.swapaxes(m_of[n], 1, 2), jnp.swapaxes(v_of[n], 1, 2),
                            tm=TM_ADAM, name="adam_" + n)
                upd[n] = [jnp.swapaxes(t, 1, 2) for t in res]
            else:
                upd[n] = _adam(w_of[n], p, m_of[n], v_of[n], tm=TM_ADAM, name="adam_" + n)
        after = upd[names[-1]][1]

    gm_cols = lax.dynamic_slice(packed_all[:, R_MODS:R_MODS + N_MOD, :].reshape(N_DEV, N_MOD * D),
                                (0, me * ada_cols), (N_DEV, ada_cols))
    upd["ada"] = _adam(w_ada, _wada_grad(c_all.T, gm_cols, name="ada_dw"), m_w_ada, v_w_ada, tm=256, name="adam_ada")
    conv_g = lax.dynamic_slice(gsmall, (R_CONV, me * conv_cols), (3, conv_cols))

    def natural(b, g1, gm, g2, gf, cw, sk):
        return dict(b_ada=b, g_ffn1=g1, g_mix=gm, g_ffn2=g2, g_final=gf[None], conv_w=cw[0], sinks=sk)

    small_out = _adam_small(gsmall, conv_g, natural(b_ada, g_ffn1, g_mix, g_ffn2, g_final, conv_w, sinks),
                            natural(m_b_ada, m_g_ffn1, m_g_mix, m_g_ffn2, m_g_final, m_conv_w, m_sinks),
                            natural(v_b_ada, v_g_ffn1, v_g_mix, v_g_ffn2, v_g_final, v_conv_w, v_sinks),
                            name="adam_small")
    for res in small_out:
        res["g_final"] = res["g_final"][0]
        res["conv_w"] = res["conv_w"][None]

    big_name = dict(w_ada="ada", w1_gu="gu1", w1_down="d1", w_in="win", w_conv_proj="cp", w_attn_proj="ap",
                    w_out="out", w2_gu="gu2", w2_down="d2")
    order = ("w_ada", "b_ada", "g_ffn1", "w1_gu", "w1_down", "g_mix", "w_in", "conv_w", "w_conv_proj", "w_attn_proj",
             "sinks", "w_out", "g_ffn2", "w2_gu", "w2_down", "g_final")
    outs = [loss, grad_x[None]]
    for kind in range(4):
        for n in order:
            outs.append(upd[big_name[n]][kind] if n in big_name else small_out[kind][n])
    return tuple(outs)
```

```python
import jax
import jax.numpy as jnp
from jax import lax
from jax.experimental import pallas as pl
from jax.experimental.pallas import tpu as pltpu

D = 1024
F = 2816
NIN = 6656
N_HEADS = 16
N_KV = 4
HEAD_DIM = 64
BLK = 128
N_MOD = 9
N_DEV = 8
EPS = 1e-6
NEG_INF = -1e30
ROPE_THETA = 10000.0
O_BG, O_CG, O_U, O_Q, O_K, O_V, O_ZC, O_ZA = 0, 1024, 2048, 3072, 4096, 4352, 4608, 5632

ADAM_LR = 0.001
ADAM_B1 = 0.9
ADAM_B2 = 0.999
ADAM_EPS = 1e-08
ADAM_WD = 0.01
ADAM_STEP = 10

BF = jnp.bfloat16
F32 = jnp.float32
VMEM_LIMIT = 56 * 1024 * 1024
MXU_N = 256
MESH = pl.DeviceIdType.MESH

NT = (((1,), (1,)), ((), ()))
TN = (((0,), (0,)), ((), ()))


def _cp(sem=None):
    return pltpu.CompilerParams(dimension_semantics=sem, vmem_limit_bytes=VMEM_LIMIT)


def _tile(n, pref):
    if n <= pref:
        return n
    for t in range(pref - pref % 16, 15, -16):
        if n % t == 0:
            return t
    raise ValueError((n, pref))


def _sigmoid(v):
    return 0.5 * jnp.tanh(0.5 * v) + 0.5


def _row(i):
    return (i, 0)


def _const2(*_):
    return (0, 0)


def _resident(shape):
    return pl.BlockSpec(shape, lambda *_: (0,) * len(shape), pipeline_mode=pl.Buffered(1))


def _norm_proj(x, g, sc, sh, wt, rope=None, *, tm, tn, name):
    T, N = x.shape[0], wt.shape[0]
    tm = _tile(T, tm)

    def body(x_ref, g_ref, sc_ref, sh_ref, w_ref, *rest):
        if rope is None:
            h_ref, o_ref = rest
        else:
            c_ref, s_ref, h_ref, o_ref, qs_ref, kr_ref = rest
        xv = x_ref[...]
        r = lax.rsqrt(jnp.mean(xv * xv, axis=-1, keepdims=True) + EPS)
        hb = ((xv * r) * g_ref[...] * (1.0 + sc_ref[...]) + sh_ref[...]).astype(BF)
        h_ref[...] = hb
        for c0 in range(0, N, tn):
            cols = pl.ds(c0, tn)
            o_ref[:, cols] = lax.dot_general(hb, w_ref[cols, :], NT, preferred_element_type=F32).astype(BF)
            if rope is not None and c0 < O_V <= c0 + tn:
                _attn_prep_tile(o_ref, c_ref, s_ref, qs_ref, kr_ref, tm)

    vec = pl.BlockSpec((1, D), _const2)
    rowspec = pl.BlockSpec((tm, D), _row)
    in_specs = [rowspec, vec, vec, vec, _resident((N, D))]
    out_specs = [rowspec, pl.BlockSpec((tm, N), _row)]
    out_shape = [jax.ShapeDtypeStruct((T, D), BF), jax.ShapeDtypeStruct((T, N), BF)]
    args = [x, g, sc, sh, wt]
    if rope is not None:
        in_specs += [pl.BlockSpec((tm, 128), _row)] * 2
        out_specs += [pl.BlockSpec((N_KV, 4 * tm, 128), lambda i: (0, i, 0)), pl.BlockSpec((tm, 256), _row)]
        out_shape += [jax.ShapeDtypeStruct((N_KV, 4 * T, 128), BF), jax.ShapeDtypeStruct((T, 256), BF)]
        args += list(rope)
    return pl.pallas_call(
        body, name=name, grid=(T // tm,),
        in_specs=in_specs, out_specs=out_specs, out_shape=out_shape,
        compiler_params=_cp(("parallel",)),
    )(*args)


def _ffn_down_fwd(ab, wd, x, gt, *, tm, name):
    T = x.shape[0]
    tm = _tile(T, tm)

    def body(a_ref, b_ref, wd_ref, x_ref, gt_ref, xo_ref, y_ref):
        y = None
        for c0 in range(0, F, MXU_N):
            cols = pl.ds(c0, MXU_N)
            a = a_ref[:, cols].astype(F32)
            act = (a * _sigmoid(a) * b_ref[:, cols].astype(F32)).astype(BF)
            part = jnp.dot(act, wd_ref[cols, :], preferred_element_type=F32)
            y = part if y is None else y + part
        y_ref[...] = y.astype(BF)
        xo_ref[...] = x_ref[...] + (0.5 * gt_ref[...]) * y

    return pl.pallas_call(
        body, name=name, grid=(T // tm,),
        in_specs=[pl.BlockSpec((tm, F), lambda i: (i, 0)), pl.BlockSpec((tm, F), lambda i: (i, 1)),
                  _resident((F, D)), pl.BlockSpec((tm, D), _row), pl.BlockSpec((1, D), _const2)],
        out_specs=[pl.BlockSpec((tm, D), _row), pl.BlockSpec((tm, D), _row)],
        out_shape=[jax.ShapeDtypeStruct((T, D), F32), jax.ShapeDtypeStruct((T, D), BF)],
        compiler_params=_cp(("parallel",)),
    )(ab, ab, wd, x, gt)


def _ffn_fwd(x, g, sc, sh, gt, wgu, wd, final, *, tm, name):
    T = x.shape[0]
    tm = _tile(T, tm)
    last = final is not None

    def body(x_ref, g_ref, sc_ref, sh_ref, gt_ref, wgu_ref, wd_ref, *rest):
        if last:
            t_ref, gf_ref, h_ref, ab_ref, y_ref, dx_ref, ls_ref, dgf_ref = rest
        else:
            h_ref, ab_ref, y_ref, xo_ref = rest
        xv = x_ref[...]
        r = lax.rsqrt(jnp.mean(xv * xv, axis=-1, keepdims=True) + EPS)
        hb = ((xv * r) * g_ref[...] * (1.0 + sc_ref[...]) + sh_ref[...]).astype(BF)
        h_ref[...] = hb
        y = None
        for c0 in range(0, F, MXU_N):
            a = lax.dot_general(hb, wgu_ref[pl.ds(c0, MXU_N), :], NT, preferred_element_type=F32)
            b = lax.dot_general(hb, wgu_ref[pl.ds(F + c0, MXU_N), :], NT, preferred_element_type=F32)
            ab = a.astype(BF)
            bb = b.astype(BF)
            ab_ref[:, pl.ds(c0, MXU_N)] = ab
            ab_ref[:, pl.ds(F + c0, MXU_N)] = bb
            a = ab.astype(F32)
            act = (a * _sigmoid(a) * bb.astype(F32)).astype(BF)
            part = jnp.dot(act, wd_ref[pl.ds(c0, MXU_N), :], preferred_element_type=F32)
            y = part if y is None else y + part
        y_ref[...] = y.astype(BF)
        xo = xv + (0.5 * gt_ref[...]) * y
        if not last:
            xo_ref[...] = xo
            return

        @pl.when(pl.program_id(0) == 0)
        def _():
            ls_ref[...] = jnp.zeros_like(ls_ref)
            dgf_ref[...] = jnp.zeros_like(dgf_ref)
        gv = gf_ref[...]
        r = lax.rsqrt(jnp.mean(xo * xo, axis=-1, keepdims=True) + EPS)
        xh = xo * r
        e = xh * gv - t_ref[...]
        ls_ref[...] += jnp.sum(e * e, axis=0, keepdims=True)
        dy = e * (1.0 / D)
        dgf_ref[...] += jnp.sum(dy * xh, axis=0, keepdims=True)
        dxh = dy * gv
        dx_ref[...] = r * (dxh - xh * jnp.mean(dxh * xh, axis=-1, keepdims=True))

    vec = pl.BlockSpec((1, D), _const2)
    rowspec = pl.BlockSpec((tm, D), _row)
    in_specs = [rowspec, vec, vec, vec, vec, _resident((2 * F, D)), _resident((F, D))]
    out_specs = [rowspec, pl.BlockSpec((tm, 2 * F), _row), rowspec, rowspec]
    out_shape = [jax.ShapeDtypeStruct((T, D), BF), jax.ShapeDtypeStruct((T, 2 * F), BF),
                 jax.ShapeDtypeStruct((T, D), BF), jax.ShapeDtypeStruct((T, D), F32)]
    args = [x, g, sc, sh, gt, wgu, wd]
    if last:
        in_specs += [rowspec, vec]
        out_specs += [vec, vec]
        out_shape += [jax.ShapeDtypeStruct((1, D), F32)] * 2
        args += list(final)
    return pl.pallas_call(
        body, name=name, grid=(T // tm,),
        in_specs=in_specs, out_specs=out_specs, out_shape=out_shape,
        compiler_params=_cp(("arbitrary",) if last else ("parallel",)),
    )(*args)


def _ffn_down_bwd_dw(dxo, y, gt, ab, wd, *, tm, name):
    T = dxo.shape[0]
    tm = _tile(T, tm)
    nt = T // tm
    hw = F // 2
    chunks = [(c0, min(MXU_N, hw - c0)) for c0 in range(0, hw, MXU_N)]

    def body(dxo_ref, y_ref, gt_ref, a_ref, b_ref, wd_ref, dab_ref, dgt_ref, dwd_ref, dys, dyt, acc, stage, sem):
        i, j = pl.program_id(0), pl.program_id(1)

        @pl.when(jnp.logical_and(i == 0, j == 0))
        def _():
            dgt_ref[...] = jnp.zeros_like(dgt_ref)

        @pl.when(j == 0)
        def _():
            dxv = dxo_ref[...]
            dgt_ref[...] += 0.5 * jnp.sum(dxv * y_ref[...].astype(F32), axis=0, keepdims=True)
            dyf = (0.5 * gt_ref[...]) * dxv
            dys[...] = dyf.astype(BF)
            dyt[...] = dyf.T.astype(BF)

        def half(jj):
            @pl.when(i == 0)
            def _():
                acc[jj] = jnp.zeros((D, hw), F32)

            dy = dys[...]
            dy_t = dyt[...]
            for c0, cw in chunks:
                cols = pl.ds(c0, cw)
                dact = lax.dot_general(dy, wd_ref[pl.ds(jj * hw + c0, cw), :], NT, preferred_element_type=F32)
                a = a_ref[:, cols].astype(F32)
                b = b_ref[:, cols].astype(F32)
                s = _sigmoid(a)
                silu = a * s
                dab_ref[0, :, cols] = (dact * b * (s * (1.0 + a * (1.0 - s)))).astype(BF)
                dab_ref[1, :, cols] = (dact * silu).astype(BF)
                acc[jj, :, cols] += jnp.dot(dy_t, (silu * b).astype(BF), preferred_element_type=F32)

            @pl.when(i == nt - 1)
            def _():
                for c0, cw in chunks:
                    stage[0:cw, :] = acc[jj, :, pl.ds(c0, cw)].T.astype(BF)
                    out = pltpu.make_async_copy(stage.at[pl.ds(0, cw)], dwd_ref.at[pl.ds(jj * hw + c0, cw)], sem)
                    out.start()
                    out.wait()

        for jj in range(2):
            pl.when(j == jj)(lambda jj=jj: half(jj))

    vec = pl.BlockSpec((1, D), _const2)
    rowspec = pl.BlockSpec((tm, D), lambda i, j: (i, 0))
    return pl.pallas_call(
        body, name=name, grid=(nt, 2),
        in_specs=[rowspec, rowspec, vec, pl.BlockSpec((tm, hw), lambda i, j: (i, j)),
                  pl.BlockSpec((tm, hw), lambda i, j: (i, j + 2)), _resident((F, D))],
        out_specs=[pl.BlockSpec((2, tm, hw), lambda i, j: (0, i, j)), vec, pl.BlockSpec(memory_space=pl.ANY)],
        out_shape=[jax.ShapeDtypeStruct((2, T, F), BF), jax.ShapeDtypeStruct((1, D), F32),
                   jax.ShapeDtypeStruct((F, D), BF)],
        scratch_shapes=[pltpu.VMEM((tm, D), BF), pltpu.VMEM((D, tm), BF), pltpu.VMEM((2, D, hw), F32),
                        pltpu.VMEM((MXU_N, D), BF), pltpu.SemaphoreType.DMA(())],
        compiler_params=_cp(("arbitrary", "arbitrary")),
    )(dxo, y, gt, ab, ab, wd)


def _tn_matmul(a, b, token=None, *, tn, tk, name):
    S, T, Ns = a.shape
    tn, tk = _tile(Ns, tn), _tile(T, tk)
    nk, njs = T // tk, Ns // tn
    deps = [] if token is None else [token]

    def body(a_ref, b_ref, *rest):
        o_ref, acc = rest[len(deps):]
        k = pl.program_id(1)

        @pl.when(k == 0)
        def _():
            acc[...] = jnp.zeros_like(acc)
        acc[...] += lax.dot_general(a_ref[0], b_ref[...], TN, preferred_element_type=F32)

        @pl.when(k == nk - 1)
        def _():
            o_ref[...] = acc[...].astype(BF)

    return pl.pallas_call(
        body, name=name, grid=(S * njs, nk),
        in_specs=[pl.BlockSpec((1, tk, tn), lambda j, k: (j // njs, k, j % njs)),
                  pl.BlockSpec((tk, D), lambda j, k: (k, 0))] + [pl.BlockSpec(memory_space=pl.ANY)] * len(deps),
        out_specs=pl.BlockSpec((tn, D), lambda j, k: (j, 0)),
        out_shape=jax.ShapeDtypeStruct((S * Ns, D), BF),
        scratch_shapes=[pltpu.VMEM((tn, D), F32)],
        compiler_params=_cp(("parallel", "arbitrary")),
    )(a, b, *deps)


def _nn_bwd_norm(da, w, x, g, sc, dxo, *, tm, name):
    S, T, Ks = da.shape
    tm = _tile(T, tm)
    rc = _tile(tm, 256)

    def body(da_ref, w_ref, x_ref, g_ref, sc_ref, dxo_ref, dx_ref, dsh_ref, dsc_ref, dg_ref, acc):
        @pl.when(pl.program_id(0) == 0)
        def _():
            dsh_ref[...] = jnp.zeros_like(dsh_ref)
            dsc_ref[...] = jnp.zeros_like(dsc_ref)
            dg_ref[...] = jnp.zeros_like(dg_ref)

        d = jnp.dot(da_ref[0], w_ref[0:Ks, :], preferred_element_type=F32)
        for s in range(1, S):
            d = d + jnp.dot(da_ref[s], w_ref[s * Ks:(s + 1) * Ks, :], preferred_element_type=F32)
        acc[...] = d
        gv = g_ref[...]
        sc1 = 1.0 + sc_ref[...]
        dsh = jnp.zeros((1, D), F32)
        dsc = jnp.zeros((1, D), F32)
        dg = jnp.zeros((1, D), F32)
        for r0 in range(0, tm, rc):
            rows = pl.ds(r0, rc)
            u = acc[rows, :]
            xv = x_ref[rows, :]
            r = lax.rsqrt(jnp.mean(xv * xv, axis=-1, keepdims=True) + EPS)
            xh = xv * r
            dsh = dsh + jnp.sum(u, axis=0, keepdims=True)
            dsc = dsc + jnp.sum(u * (xh * gv), axis=0, keepdims=True)
            us = u * sc1
            dg = dg + jnp.sum(us * xh, axis=0, keepdims=True)
            dxh = us * gv
            dx_ref[rows, :] = dxo_ref[rows, :] + r * (dxh - xh * jnp.mean(dxh * xh, axis=-1, keepdims=True))
        dsh_ref[...] += dsh
        dsc_ref[...] += dsc
        dg_ref[...] += dg

    vec = pl.BlockSpec((1, D), _const2)
    rowspec = pl.BlockSpec((tm, D), _row)
    return pl.pallas_call(
        body, name=name, grid=(T // tm,),
        in_specs=[pl.BlockSpec((S, tm, Ks), lambda i: (0, i, 0)), _resident((S * Ks, D)), rowspec, vec, vec, rowspec],
        out_specs=[rowspec, vec, vec, vec],
        out_shape=[jax.ShapeDtypeStruct((T, D), F32)] + [jax.ShapeDtypeStruct((1, D), F32)] * 3,
        scratch_shapes=[pltpu.VMEM((tm, D), F32)],
        compiler_params=_cp(("arbitrary",)),
    )(da, w, x, g, sc, dxo)


def _rope(t, cos, sin_signed, lt32, inverse=False):
    sel = jnp.where(lt32, pltpu.roll(t, 96, 1), pltpu.roll(t, 32, 1))
    return t * cos - sel * sin_signed if inverse else t * cos + sel * sin_signed


def _rope_tables(T, token=None):
    inv = 1.0 / (ROPE_THETA ** (jnp.arange(0, HEAD_DIM, 2, dtype=F32) / HEAD_DIM))
    ang = _behind(jnp.arange(T, dtype=F32)[:, None] * inv[None, :], token)
    cos, sin = jnp.cos(ang), jnp.sin(ang)
    cos128 = jnp.tile(cos, (1, 4))
    sin128 = jnp.tile(jnp.concatenate([-sin, sin], axis=1), (1, 2))
    return cos128, sin128


QSCALE = HEAD_DIM ** -0.5


def _lane_masks(rows):
    lane = lax.broadcasted_iota(jnp.int32, (rows, 128), 1)
    return (lane % HEAD_DIM) < (HEAD_DIM // 2), [lane < HEAD_DIM, lane >= HEAD_DIM]


def _attn_bias():
    qi = lax.broadcasted_iota(jnp.int32, (4 * BLK, 2 * BLK), 0) % BLK
    kj = lax.broadcasted_iota(jnp.int32, (4 * BLK, 2 * BLK), 1)
    band = (kj > qi) & (kj <= qi + BLK)
    return jnp.stack([jnp.where(band & (kj >= BLK), 0.0, NEG_INF), jnp.where(band, 0.0, NEG_INF)]).astype(F32)


def _attn_prep_tile(proj_ref, c_ref, s_ref, qs_ref, kr_ref, tm):
    lt32, halves = _lane_masks(BLK)
    for b in range(tm // BLK):
        rows = pl.ds(b * BLK, BLK)
        cc, sc = c_ref[rows, :], s_ref[rows, :]
        qr = [_rope(proj_ref[rows, pl.ds(O_Q + p * 128, 128)].astype(F32), cc, sc, lt32) * QSCALE for p in range(8)]
        for g in range(N_KV):
            qs_ref[g, pl.ds(4 * b * BLK, 4 * BLK), :] = _stack_heads(qr, g, halves).astype(BF)
        kr_ref[rows, :] = jnp.concatenate([_rope(proj_ref[rows, pl.ds(O_K + r * 128, 128)].astype(F32), cc, sc, lt32)
                                           for r in range(2)], axis=1).astype(BF)


ATT_BPS = 4
ATT_ROWS = ATT_BPS * BLK


def _before(n):
    return jnp.maximum(ATT_BPS * n - 1, 0)


def _attn_specs():
    return [pl.BlockSpec((N_KV, 4 * ATT_ROWS, 128), lambda n: (0, n, 0)),
            pl.BlockSpec((ATT_ROWS, 256), _row), pl.BlockSpec((BLK, 256), lambda n: (_before(n), 0)),
            pl.BlockSpec((ATT_ROWS, 256), lambda n: (n, O_V // 256)),
            pl.BlockSpec((BLK, 256), lambda n: (_before(n), O_V // 256)),
            pl.BlockSpec((2, 4 * BLK, 2 * BLK), lambda n: (0, 0, 0)),
            pl.BlockSpec(memory_space=pltpu.SMEM)]


def _bands(sb, kc_ref, kp_ref, vc_ref, vp_ref):
    own = pl.ds(sb * BLK, BLK)
    above = pl.ds((sb - 1) * BLK, BLK)
    kb, vb = [], []
    for r in range(2):
        cols = pl.ds(r * 128, 128)
        kprev = kp_ref[:, cols] if sb == 0 else kc_ref[above, cols]
        vprev = vp_ref[:, cols] if sb == 0 else vc_ref[above, cols]
        kb.append(jnp.concatenate([kprev, kc_ref[own, cols]], axis=0))
        vb.append(jnp.concatenate([vprev, vc_ref[own, cols]], axis=0))
    return kb, vb


def _block_bias(sb, bias_ref):
    return bias_ref[jnp.minimum(pl.program_id(0), 1)] if sb == 0 else bias_ref[1]


def _sink_rows(sink_ref, g):
    return jnp.concatenate([jnp.full((BLK, 128), sink_ref[4 * g + hh], F32) for hh in range(4)], axis=0)


def _both(t):
    return jnp.concatenate([t, t], axis=1)


def _unstack_heads(t, g, halves, acc):
    half = g % 2
    for hh in range(4):
        h = 4 * g + hh
        th = jnp.where(halves[half], t[hh * BLK:(hh + 1) * BLK], 0.0)
        if h % 2 != half:
            th = pltpu.roll(th, HEAD_DIM, 1)
        acc[h // 2] = acc[h // 2] + th


def _stack_heads(chunks, g, halves):
    half = g % 2
    parts = []
    for hh in range(4):
        h = 4 * g + hh
        t = chunks[h // 2]
        if h % 2 != half:
            t = pltpu.roll(t, HEAD_DIM, 1)
        parts.append(jnp.where(halves[half], t, 0.0))
    return jnp.concatenate(parts, axis=0)


def _attn_fwd(qs, kr, proj, bias, sinks, *, name):
    T = proj.shape[0]
    assert T % ATT_ROWS == 0

    def body(qs_ref, kc_ref, kp_ref, vc_ref, vp_ref, bias_ref, sink_ref, o_ref, lse_ref):
        _, h128 = _lane_masks(BLK)
        _, h256 = _lane_masks(2 * BLK)
        _, h512 = _lane_masks(4 * BLK)
        groups = range(N_KV)
        sink = [_sink_rows(sink_ref, g) for g in groups]
        for sb in range(ATT_BPS):
            rows = pl.ds(4 * sb * BLK, 4 * BLK)
            kb, vb = _bands(sb, kc_ref, kp_ref, vc_ref, vp_ref)
            outs = [jnp.zeros((BLK, 128), F32) for _ in range(8)]
            bias = _block_bias(sb, bias_ref)
            s = [lax.dot_general(qs_ref[g, rows, :], kb[g // 2], NT, preferred_element_type=F32) + bias for g in groups]
            m = [jnp.maximum(jnp.broadcast_to(jnp.max(s[g], axis=-1, keepdims=True), (4 * BLK, 128)), sink[g])
                 for g in groups]
            p = [jnp.exp(s[g] - _both(m[g])).astype(BF) for g in groups]
            vg = [jnp.where(h256[g % 2], vb[g // 2].astype(F32), 1.0).astype(BF) for g in groups]
            o = [jnp.dot(p[g], vg[g], preferred_element_type=F32) for g in groups]
            denom = [jnp.where(h512[g % 2], pltpu.roll(o[g], HEAD_DIM, 1), o[g]) + jnp.exp(sink[g] - m[g])
                     for g in groups]
            for g in groups:
                lse_ref[g, rows, :] = m[g] + jnp.log(denom[g])
                _unstack_heads(o[g] * (1.0 / denom[g]), g, h128, outs)
            o_ref[pl.ds(sb * BLK, BLK), :] = jnp.concatenate(outs, axis=1).astype(BF)

    return pl.pallas_call(
        body, name=name, grid=(T // ATT_ROWS,),
        in_specs=_attn_specs(),
        out_specs=[pl.BlockSpec((ATT_ROWS, D), _row), pl.BlockSpec((N_KV, 4 * ATT_ROWS, 128), lambda n: (0, n, 0))],
        out_shape=[jax.ShapeDtypeStruct((T, D), BF), jax.ShapeDtypeStruct((N_KV, 4 * T, 128), F32)],
        compiler_params=_cp(("parallel",)),
    )(qs, kr, kr, proj, proj, bias, sinks)


def _attn_bwd(qs, kr, proj, bias, sinks, lse, o, do, cos, sin, dproj, *, name):
    T = proj.shape[0]
    assert T % ATT_ROWS == 0

    def body(qs_ref, kc_ref, kp_ref, vc_ref, vp_ref, bias_ref, sink_ref, lse_ref, o_ref, do_ref,
             cc_ref, sc_ref, cp_ref, sp_ref, dproj_ref, dq_ref, dkc_ref, dkp_ref, dvc_ref, dvp_ref, dsink_ref):
        @pl.when(pl.program_id(0) == 0)
        def _():
            dsink_ref[...] = jnp.zeros_like(dsink_ref)
        lt32, h128 = _lane_masks(BLK)
        lane1 = lax.broadcasted_iota(jnp.int32, (1, 128), 1)
        dsink = jnp.zeros((1, 128), F32)
        groups = range(N_KV)
        for sb in range(ATT_BPS):
            own = pl.ds(sb * BLK, BLK)
            rows = pl.ds(4 * sb * BLK, 4 * BLK)
            kb, vb = _bands(sb, kc_ref, kp_ref, vc_ref, vp_ref)
            oc = [o_ref[own, pl.ds(p * 128, 128)].astype(F32) for p in range(8)]
            doc = [do_ref[own, pl.ds(p * 128, 128)].astype(F32) for p in range(8)]
            dqs = [jnp.zeros((BLK, 128), F32) for _ in range(8)]
            bias = _block_bias(sb, bias_ref)
            q = [qs_ref[g, rows, :] for g in groups]
            lse_g = [lse_ref[g, rows, :] for g in groups]
            s = [lax.dot_general(q[g], kb[g // 2], NT, preferred_element_type=F32) + bias for g in groups]
            dos = [_stack_heads(doc, g, h128) for g in groups]
            dosb = [t.astype(BF) for t in dos]
            dp = [lax.dot_general(dosb[g], vb[g // 2], NT, preferred_element_type=F32) for g in groups]
            delta = [jnp.broadcast_to(jnp.sum(dos[g] * _stack_heads(oc, g, h128), axis=-1, keepdims=True),
                                      (4 * BLK, 128)) for g in groups]
            p = [jnp.exp(s[g] - _both(lse_g[g])) for g in groups]
            ds = [(p[g] * (dp[g] - _both(delta[g]))).astype(BF) for g in groups]
            pb = [t.astype(BF) for t in p]
            dvg = [lax.dot_general(pb[g], dosb[g], TN, preferred_element_type=F32) for g in groups]
            dkg = [lax.dot_general(ds[g], q[g], TN, preferred_element_type=F32) for g in groups]
            dqg = [jnp.dot(ds[g], kb[g // 2], preferred_element_type=F32) * QSCALE for g in groups]
            dvr = [dvg[0] + dvg[1], dvg[2] + dvg[3]]
            dkr = [dkg[0] + dkg[1], dkg[2] + dkg[3]]
            for g in groups:
                _unstack_heads(dqg[g], g, h128, dqs)
                dsk = -jnp.exp(_sink_rows(sink_ref, g) - lse_g[g]) * delta[g]
                for hh in range(4):
                    val = jnp.sum(dsk[hh * BLK:(hh + 1) * BLK], axis=0, keepdims=True)
                    dsink = dsink + jnp.where(lane1 == 4 * g + hh, val, 0.0)
            cc, sc = cc_ref[own, :], sc_ref[own, :]
            cp, sp = (cp_ref[...], sp_ref[...]) if sb == 0 else (cc_ref[pl.ds((sb - 1) * BLK, BLK), :],
                                                                  sc_ref[pl.ds((sb - 1) * BLK, BLK), :])
            dq_ref[own, :] = jnp.concatenate([_rope(t, cc, sc, lt32, inverse=True) for t in dqs], axis=1).astype(BF)
            dkp_ref[own, :] = jnp.concatenate([_rope(t[:BLK], cp, sp, lt32, inverse=True) for t in dkr], axis=1)
            dkc_ref[own, :] = jnp.concatenate([_rope(t[BLK:], cc, sc, lt32, inverse=True) for t in dkr], axis=1)
            dvp_ref[own, :] = jnp.concatenate([t[:BLK] for t in dvr], axis=1)
            dvc_ref[own, :] = jnp.concatenate([t[BLK:] for t in dvr], axis=1)
        dsink_ref[...] += dsink

    kv = pl.BlockSpec((ATT_ROWS, 256), _row)
    tc = pl.BlockSpec((ATT_ROWS, 128), _row)
    tp = pl.BlockSpec((BLK, 128), lambda n: (_before(n), 0))
    return pl.pallas_call(
        body, name=name, grid=(T // ATT_ROWS,),
        in_specs=_attn_specs() + [pl.BlockSpec((N_KV, 4 * ATT_ROWS, 128), lambda n: (0, n, 0)),
                                  pl.BlockSpec((ATT_ROWS, D), _row), pl.BlockSpec((ATT_ROWS, D), _row), tc, tc, tp, tp,
                                  pl.BlockSpec(memory_space=pl.ANY)],
        out_specs=[pl.BlockSpec((ATT_ROWS, D), lambda n: (n, O_Q // D)), kv, kv, kv, kv,
                   pl.BlockSpec((1, 128), _const2)],
        out_shape=[jax.ShapeDtypeStruct(dproj.shape, BF)] + [jax.ShapeDtypeStruct((T, 256), F32)] * 4
        + [jax.ShapeDtypeStruct((1, 128), F32)],
        input_output_aliases={14: 0},
        compiler_params=_cp(("arbitrary",)),
    )(qs, kr, kr, proj, proj, bias, sinks, lse, o, do, cos, sin, cos, sin, dproj)


def _dkv_combine(dkc, dkp, dvc, dvp, dproj, *, name):
    T = dkc.shape[0]
    nb = T // BLK
    tm = _tile(T, 4 * BLK)
    bpt = tm // BLK
    nt = T // tm

    def body(dkc_ref, dkp_ref, dkn_ref, dvc_ref, dvp_ref, dvn_ref, dproj_ref, o_ref):
        keep = jnp.where(pl.program_id(0) == nt - 1, 0.0, 1.0)

        def shifted(prev_ref, next_ref):
            nxt = keep * next_ref[...]
            return nxt if bpt == 1 else jnp.concatenate([prev_ref[BLK:, :], nxt], axis=0)

        o_ref[:, 0:256] = (dkc_ref[...] + shifted(dkp_ref, dkn_ref)).astype(BF)
        o_ref[:, 256:512] = (dvc_ref[...] + shifted(dvp_ref, dvn_ref)).astype(BF)

    cur = pl.BlockSpec((tm, 256), _row)
    nxt = pl.BlockSpec((BLK, 256), lambda i: (jnp.minimum((i + 1) * bpt, nb - 1), 0))
    return pl.pallas_call(
        body, name=name, grid=(nt,),
        in_specs=[cur, cur, nxt, cur, cur, nxt, pl.BlockSpec(memory_space=pl.ANY)],
        out_specs=pl.BlockSpec((tm, 512), lambda i: (i, O_K // 512)),
        out_shape=jax.ShapeDtypeStruct(dproj.shape, BF),
        input_output_aliases={6: 0},
        compiler_params=_cp(("parallel",)),
    )(dkc, dkp, dkp, dvc, dvp, dvp, dproj)


HALO = 16


def _conv_shifts(cu, hprev, tm):
    row = lax.broadcasted_iota(jnp.int32, (8, cu.shape[1]), 0)
    h1 = hprev[HALO - 1:HALO, :]
    h2 = hprev[HALO - 2:HALO - 1, :]
    m1 = pltpu.roll(cu, 1, 0)
    m2 = pltpu.roll(cu, 2, 0)
    m1 = jnp.concatenate([jnp.where(row == 0, h1, m1[0:8]), m1[8:]], axis=0)
    m2 = jnp.concatenate([jnp.where(row == 0, h2, jnp.where(row == 1, h1, m2[0:8])), m2[8:]], axis=0)
    return m1, m2


def _mixer_mid_fwd(proj, attn, wcp, wap, wout, convw, x, gt, *, tm, name):
    T = x.shape[0]
    tm = _tile(T, tm)
    hb = tm // HALO

    def body(bg_ref, cg_ref, u_ref, hcg_ref, hu_ref, zc0_ref, zc1_ref, za0_ref, za1_ref, at_ref,
             wcp_ref, wap_ref, wout_ref, cw_ref, x_ref, gt_ref,
             x2_ref, gc_ref, yc_ref, ya_ref, mg_ref, o_ref):
        first = jnp.where(pl.program_id(0) == 0, 0.0, 1.0)
        cu = cg_ref[...].astype(F32) * u_ref[...].astype(F32)
        hprev = first * (hcg_ref[...].astype(F32) * hu_ref[...].astype(F32))
        m1, m2 = _conv_shifts(cu, hprev, tm)
        cv = cw_ref[0:1, :] * m2 + cw_ref[1:2, :] * m1 + cw_ref[2:3, :] * cu
        gc = (bg_ref[...].astype(F32) * cv).astype(BF)
        gc_ref[...] = gc
        yc = jnp.dot(gc, wcp_ref[...], preferred_element_type=F32)
        ya = jnp.dot(at_ref[...], wap_ref[...], preferred_element_type=F32)
        yc_ref[...] = yc.astype(BF)
        ya_ref[...] = ya.astype(BF)
        zc = jnp.concatenate([zc0_ref[...], zc1_ref[...]], axis=1).astype(F32)
        za = jnp.concatenate([za0_ref[...], za1_ref[...]], axis=1).astype(F32)
        mg = (_sigmoid(zc) * yc + _sigmoid(za) * ya).astype(BF)
        mg_ref[...] = mg
        o = jnp.dot(mg, wout_ref[...], preferred_element_type=F32)
        o_ref[...] = o.astype(BF)
        x2_ref[...] = x_ref[...] + gt_ref[...] * o

    wspec = pl.BlockSpec((D, D), _const2)
    rowspec = pl.BlockSpec((tm, D), _row)
    return pl.pallas_call(
        body, name=name, grid=(T // tm,),
        in_specs=[_col(tm, O_BG), _col(tm, O_CG), _col(tm, O_U), _halo_prev(hb, O_CG), _halo_prev(hb, O_U),
                  _col(tm, O_ZC, 512), _col(tm, O_ZC + 512, 512), _col(tm, O_ZA, 512), _col(tm, O_ZA + 512, 512),
                  rowspec, wspec, wspec, wspec, pl.BlockSpec((8, D), _const2), rowspec, pl.BlockSpec((1, D), _const2)],
        out_specs=[rowspec] * 6,
        out_shape=[jax.ShapeDtypeStruct((T, D), F32)] + [jax.ShapeDtypeStruct((T, D), BF)] * 5,
        compiler_params=_cp(("parallel",)),
    )(proj, proj, proj, proj, proj, proj, proj, proj, proj, attn, wcp, wap, wout, convw, x, gt)


def _col(tm, c, w=D):
    assert c % w == 0
    return pl.BlockSpec((tm, w), lambda i: (i, c // w))


def _halo_prev(hb, c):
    return pl.BlockSpec((HALO, D), lambda i: (jnp.maximum(i * hb - 1, 0), c // D))


def _halo_next(hb, nblk, c=0):
    return pl.BlockSpec((HALO, D), lambda i: (jnp.minimum((i + 1) * hb, nblk - 1), c // D))


def _mixer_mid_bwd(dx2, gt, o, proj, yc, ya, wout, wcp, wap, *, tm, name):
    T = dx2.shape[0]
    tm = _tile(T, tm)
    nt = T // tm

    def body(dx_ref, gt_ref, o_ref, zc0_ref, zc1_ref, za0_ref, za1_ref, yc_ref, ya_ref, wout_ref, wcp_ref, wap_ref,
             dout_ref, dyc_ref, dya_ref, dgc_ref, dat_ref, dproj_ref, dgt_ref, dzs, sems):
        i = pl.program_id(0)
        slot = lax.rem(i, 2)

        def slab_copy(step, s):
            return pltpu.make_async_copy(
                dzs.at[s], dproj_ref.at[pl.ds(pl.multiple_of(step * tm, tm), tm), pl.ds(O_ZC, 2 * D)], sems.at[s])

        @pl.when(i == 0)
        def _():
            dgt_ref[...] = jnp.zeros_like(dgt_ref)

        dxv = dx_ref[...]
        dgt_ref[...] += jnp.sum(dxv * o_ref[...].astype(F32), axis=0, keepdims=True)
        dout = (gt_ref[...] * dxv).astype(BF)
        dout_ref[...] = dout
        dmg = lax.dot_general(dout, wout_ref[...], NT, preferred_element_type=F32)
        sc = _sigmoid(jnp.concatenate([zc0_ref[...], zc1_ref[...]], axis=1).astype(F32))
        sa = _sigmoid(jnp.concatenate([za0_ref[...], za1_ref[...]], axis=1).astype(F32))
        dyc = (dmg * sc).astype(BF)
        dya = (dmg * sa).astype(BF)
        dyc_ref[...] = dyc
        dya_ref[...] = dya
        dzs[slot, :, 0:D] = (dmg * yc_ref[...].astype(F32) * (sc * (1.0 - sc))).astype(BF)
        dzs[slot, :, D:2 * D] = (dmg * ya_ref[...].astype(F32) * (sa * (1.0 - sa))).astype(BF)
        slab_copy(i, slot).start()
        dgc_ref[...] = lax.dot_general(dyc, wcp_ref[...], NT, preferred_element_type=F32).astype(BF)
        dat_ref[...] = lax.dot_general(dya, wap_ref[...], NT, preferred_element_type=F32).astype(BF)

        @pl.when(i > 0)
        def _():
            slab_copy(i - 1, 1 - slot).wait()

        @pl.when(i == nt - 1)
        def _():
            slab_copy(i, slot).wait()

    def zcol(c):
        return pl.BlockSpec((tm, 512), lambda i: (i, c // 512))

    wspec = pl.BlockSpec((D, D), _const2)
    rowspec = pl.BlockSpec((tm, D), _row)
    vec = pl.BlockSpec((1, D), _const2)
    return pl.pallas_call(
        body, name=name, grid=(nt,),
        in_specs=[rowspec, vec, rowspec, zcol(O_ZC), zcol(O_ZC + 512), zcol(O_ZA), zcol(O_ZA + 512),
                  rowspec, rowspec, wspec, wspec, wspec],
        out_specs=[rowspec] * 5 + [pl.BlockSpec(memory_space=pl.ANY), vec],
        out_shape=[jax.ShapeDtypeStruct((T, D), BF)] * 5 + [jax.ShapeDtypeStruct((T, NIN), BF),
                                                            jax.ShapeDtypeStruct((1, D), F32)],
        scratch_shapes=[pltpu.VMEM((2, tm, 2 * D), BF), pltpu.SemaphoreType.DMA((2,))],
        compiler_params=_cp(("arbitrary",)),
    )(dx2, gt, o, proj, proj, proj, proj, yc, ya, wout, wcp, wap)


def _conv_bwd(dgc, proj, convw, dproj, *, tm, name):
    T = dgc.shape[0]
    tm = _tile(T, tm)
    hb = tm // HALO
    nblk = T // HALO
    nt = T // tm

    def body(dgc_ref, ndgc_ref, bg_ref, nbg_ref, cg_ref, u_ref, hcg_ref, hu_ref, cw_ref, dproj_ref, dp_ref, dcw_ref):
        i = pl.program_id(0)

        @pl.when(i == 0)
        def _():
            dcw_ref[...] = jnp.zeros_like(dcw_ref)
        first = jnp.where(i == 0, 0.0, 1.0)
        last = jnp.where(i == nt - 1, 0.0, 1.0)
        cg = cg_ref[...].astype(F32)
        u = u_ref[...].astype(F32)
        bg = bg_ref[...].astype(F32)
        dg = dgc_ref[...].astype(F32)
        cu = cg * u
        hprev = first * (hcg_ref[...].astype(F32) * hu_ref[...].astype(F32))
        m1, m2 = _conv_shifts(cu, hprev, tm)
        w0, w1, w2 = cw_ref[0:1, :], cw_ref[1:2, :], cw_ref[2:3, :]
        cv = w0 * m2 + w1 * m1 + w2 * cu
        dcv = dg * bg
        nxt = last * (ndgc_ref[...].astype(F32) * nbg_ref[...].astype(F32))
        n0, n1 = nxt[0:1, :], nxt[1:2, :]
        row = lax.broadcasted_iota(jnp.int32, (8, D), 0)
        p1 = pltpu.roll(dcv, tm - 1, 0)
        p2 = pltpu.roll(dcv, tm - 2, 0)
        p1 = jnp.concatenate([p1[:tm - 8], jnp.where(row == 7, n0, p1[tm - 8:])], axis=0)
        p2 = jnp.concatenate([p2[:tm - 8], jnp.where(row == 7, n1, jnp.where(row == 6, n0, p2[tm - 8:]))], axis=0)
        dcu = w2 * dcv + w1 * p1 + w0 * p2
        dp_ref[:, 0:D] = (dg * cv).astype(BF)
        dp_ref[:, D:2 * D] = (dcu * u).astype(BF)
        dp_ref[:, 2 * D:3 * D] = (dcu * cg).astype(BF)
        dcw_ref[0:1, :] += jnp.sum(dcv * m2, axis=0, keepdims=True)
        dcw_ref[1:2, :] += jnp.sum(dcv * m1, axis=0, keepdims=True)
        dcw_ref[2:3, :] += jnp.sum(dcv * cu, axis=0, keepdims=True)

    rowspec = pl.BlockSpec((tm, D), _row)
    cw = pl.BlockSpec((8, D), _const2)
    return pl.pallas_call(
        body, name=name, grid=(nt,),
        in_specs=[rowspec, _halo_next(hb, nblk), _col(tm, O_BG), _halo_next(hb, nblk, O_BG),
                  _col(tm, O_CG), _col(tm, O_U), _halo_prev(hb, O_CG), _halo_prev(hb, O_U), cw,
                  pl.BlockSpec(memory_space=pl.ANY)],
        out_specs=[pl.BlockSpec((tm, 3 * D), _row), cw],
        out_shape=[jax.ShapeDtypeStruct(dproj.shape, BF), jax.ShapeDtypeStruct((8, D), F32)],
        input_output_aliases={9: 0},
        compiler_params=_cp(("arbitrary",)),
    )(dgc, dgc, proj, proj, proj, proj, proj, proj, convw, dproj)


def _adam_math(w, g, m, v):
    nm = ADAM_B1 * m + (1.0 - ADAM_B1) * g
    nv = ADAM_B2 * v + (1.0 - ADAM_B2) * (g * g)
    m_hat = nm / (1.0 - ADAM_B1 ** ADAM_STEP)
    v_hat = nv / (1.0 - ADAM_B2 ** ADAM_STEP)
    return -ADAM_LR * (m_hat / (jnp.sqrt(v_hat) + ADAM_EPS) + ADAM_WD * w), nm, nv


SMALL = ("b_ada", "g_ffn1", "g_mix", "g_ffn2", "g_final", "conv_w", "sinks")


def _adam_small(gsum, conv_g, w, m, v, *, name):
    nsm = len(SMALL)

    def body(*refs):
        gs_ref, cg_ref = refs[0], refs[1]
        w_refs, m_refs, v_refs = (refs[2 + k * nsm:2 + (k + 1) * nsm] for k in range(3))
        outs = refs[2 + 3 * nsm:]
        for p, n in enumerate(SMALL):
            if n == "b_ada":
                pieces = [(slice(None), slice(r * D, (r + 1) * D), gs_ref[R_MODS + r:R_MODS + r + 1, :])
                          for r in range(N_MOD)]
            elif n == "conv_w":
                pieces = [(slice(None), slice(None), cg_ref[...])]
            elif n == "sinks":
                pieces = [(slice(None), slice(None), gs_ref[R_SINK:R_SINK + 1, 0:N_HEADS])]
            else:
                row = dict(g_ffn1=R_G1, g_mix=R_GM, g_ffn2=R_G2, g_final=R_GF)[n]
                pieces = [(slice(None), slice(None), gs_ref[row:row + 1, :])]
            for rs, cs, g in pieces:
                d, nm, nv = _adam_math(w_refs[p][rs, cs], g, m_refs[p][rs, cs], v_refs[p][rs, cs])
                for k, val in enumerate((g, d, nm, nv)):
                    outs[k * nsm + p][rs, cs] = val

    args = [gsum, conv_g] + [d[n] for d in (w, m, v) for n in SMALL]
    shapes = [jax.ShapeDtypeStruct(w[n].shape, F32) for _ in range(4) for n in SMALL]
    res = pl.pallas_call(body, name=name, out_shape=shapes, compiler_params=_cp())(*args)
    return [dict(zip(SMALL, res[k * nsm:(k + 1) * nsm])) for k in range(4)]


def _adam(w, g, m, v, *, tm, name):
    _, R, C = w.shape
    tm = _tile(R, tm)
    parts = g.ndim == 3

    def body(w_ref, g_ref, m_ref, v_ref, go_ref, d_ref, nm_ref, nv_ref):
        if parts:
            gv = g_ref[0].astype(F32)
            for s in range(1, N_DEV):
                gv = gv + g_ref[s].astype(F32)
        else:
            gv = g_ref[...]
        go_ref[0] = gv
        d_ref[0], nm_ref[0], nv_ref[0] = _adam_math(w_ref[0], gv, m_ref[0], v_ref[0])

    spec = pl.BlockSpec((1, tm, C), lambda i: (0, i, 0))
    gspec = pl.BlockSpec((N_DEV, tm, C), lambda i: (0, i, 0)) if parts else pl.BlockSpec((tm, C), _row)
    return pl.pallas_call(
        body, name=name, grid=(R // tm,),
        in_specs=[spec, gspec, spec, spec], out_specs=[spec] * 4,
        out_shape=[jax.ShapeDtypeStruct((1, R, C), F32)] * 4,
        compiler_params=_cp(("parallel",)),
    )(w, g, m, v)


def _mods_part(c_all, w_ada, b_ada, *, name):
    C = w_ada.shape[1]

    def body(c_ref, w_ref, b_ref, o_ref):
        cv = c_ref[...]
        ca = cv * jax.nn.sigmoid(cv)
        o_ref[...] = jnp.dot(ca, w_ref[...], preferred_element_type=F32,
                             precision=lax.Precision.HIGHEST) + b_ref[...]

    return pl.pallas_call(
        body, name=name,
        out_shape=jax.ShapeDtypeStruct((N_DEV, C), F32),
        compiler_params=_cp(),
    )(c_all, w_ada, b_ada)


def _wada_grad(c_all_t, gm, *, name):
    C = gm.shape[1]

    def body(c_ref, g_ref, o_ref):
        cv = c_ref[...]
        ca = cv * jax.nn.sigmoid(cv)
        acc = ca[:, 0:1] * g_ref[0:1, :]
        for b in range(1, N_DEV):
            acc = acc + ca[:, b:b + 1] * g_ref[b:b + 1, :]
        o_ref[...] = acc

    return pl.pallas_call(
        body, name=name,
        out_shape=jax.ShapeDtypeStruct((D, C), F32),
        compiler_params=_cp(),
    )(c_all_t, gm)


def _peer(x, y, c, d):
    px = lax.rem(x + ((d >> 2) & 1), 2)
    py = lax.rem(y + ((d >> 1) & 1), 2)
    pc = lax.rem(c + (d & 1), 2)
    return (px, py, pc), 4 * px + 2 * py + pc


def _exchange(xs, *, scatter, name):
    n = len(xs)
    nsem = n * (N_DEV - 1)

    def body(*refs):
        ins, outs = refs[:n], refs[n:2 * n]
        token, send_sems, recv_sems, local_sems = refs[2 * n:]
        x, y, c = lax.axis_index("x"), lax.axis_index("y"), lax.axis_index("c")
        me = 4 * x + 2 * y + c
        token[...] = jnp.zeros_like(token)

        def src(t, idx):
            return ins[t].at[idx] if scatter else ins[t]

        local = [pltpu.make_async_copy(src(t, me), outs[t].at[me], local_sems.at[t]) for t in range(n)]
        for cp in local:
            cp.start()
        remote = []
        for t in range(n):
            for d in range(1, N_DEV):
                peer, pidx = _peer(x, y, c, d)
                k = t * (N_DEV - 1) + d - 1
                send = pltpu.make_async_remote_copy(src_ref=src(t, pidx), dst_ref=outs[t].at[me],
                                                    send_sem=send_sems.at[k], recv_sem=recv_sems.at[k],
                                                    device_id=peer, device_id_type=MESH)
                recv = pltpu.make_async_remote_copy(src_ref=src(t, pidx), dst_ref=outs[t].at[pidx],
                                                    send_sem=send_sems.at[k], recv_sem=recv_sems.at[k],
                                                    device_id=peer, device_id_type=MESH)
                send.start()
                remote.append((send, recv))
        for cp in local:
            cp.wait()
        for send, recv in remote:
            send.wait_send()
            recv.wait_recv()

    anyspec = pl.BlockSpec(memory_space=pl.ANY)
    out_shape = [jax.ShapeDtypeStruct(a.shape if scatter else (N_DEV,) + a.shape, a.dtype) for a in xs]
    out_shape.append(jax.ShapeDtypeStruct((8, 128), F32))
    return pl.pallas_call(
        body, name=name,
        in_specs=[anyspec] * n, out_specs=[anyspec] * n + [pl.BlockSpec(memory_space=pltpu.VMEM)],
        out_shape=out_shape,
        scratch_shapes=[pltpu.SemaphoreType.DMA((nsem,)), pltpu.SemaphoreType.DMA((nsem,)),
                        pltpu.SemaphoreType.DMA((n,))],
    )(*xs)


def _sum8(parts, *, name):
    _, R, C = parts.shape

    def body(p_ref, o_ref):
        acc = p_ref[0]
        for s in range(1, N_DEV):
            acc = acc + p_ref[s]
        o_ref[...] = acc

    return pl.pallas_call(body, name=name, out_shape=jax.ShapeDtypeStruct((R, C), F32),
                          compiler_params=_cp())(parts)


HBM_SPEC = pl.BlockSpec(memory_space=pltpu.HBM)
SEM_SPEC = pl.BlockSpec(memory_space=pltpu.SEMAPHORE)
N_PEER = N_DEV - 1


def _split_copies(src_refs, land_refs, send_sems, recv_sems, scatter):
    x, y, c = lax.axis_index("x"), lax.axis_index("y"), lax.axis_index("c")
    me = 4 * x + 2 * y + c
    pairs = []
    for j, (src, land) in enumerate(zip(src_refs, land_refs)):
        for d in range(1, N_DEV):
            peer, pidx = _peer(x, y, c, d)
            k = j * N_PEER + d - 1
            s = src.at[pidx] if scatter else src
            send = pltpu.make_async_remote_copy(src_ref=s, dst_ref=land.at[me], send_sem=send_sems.at[k],
                                                recv_sem=recv_sems.at[k], device_id=peer, device_id_type=MESH)
            recv = pltpu.make_async_remote_copy(src_ref=s, dst_ref=land.at[pidx], send_sem=send_sems.at[k],
                                                recv_sem=recv_sems.at[k], device_id=peer, device_id_type=MESH)
            pairs.append((send, recv))
    return pairs


def _own_slot(block, me):
    land = lax.empty((N_DEV,) + block.shape, block.dtype)
    return lax.dynamic_update_slice(land, block[None], (me, 0, 0))


def _split_start(srcs, lands, groups, *, scatter, name):
    n, ng = len(srcs), len(groups)

    def body(*refs):
        src_refs, land_refs = refs[:n], refs[n:2 * n]
        sems = refs[2 * n:2 * n + 2 * ng]
        token = refs[-1]
        for gi, g in enumerate(groups):
            pairs = _split_copies([src_refs[t] for t in g], [land_refs[t] for t in g], sems[2 * gi],
                                  sems[2 * gi + 1], scatter)
            for send, _ in pairs:
                send.start()
        token[...] = jnp.zeros_like(token)

    sem_shapes = []
    for g in groups:
        sem_shapes += [pltpu.SemaphoreType.DMA((len(g) * N_PEER,))] * 2
    thru = [pltpu.HBM(a.shape, a.dtype) for a in list(srcs) + list(lands)]
    outs = pl.pallas_call(
        body, name=name,
        out_shape=tuple(sem_shapes + thru + [jax.ShapeDtypeStruct((8, 128), F32)]),
        in_specs=[HBM_SPEC] * (2 * n),
        out_specs=tuple([SEM_SPEC] * (2 * ng) + [HBM_SPEC] * (2 * n) + [pl.BlockSpec(memory_space=pltpu.VMEM)]),
        input_output_aliases={i: 2 * ng + i for i in range(2 * n)},
        compiler_params=pltpu.CompilerParams(has_side_effects=pltpu.SideEffectType.DATAFLOW_SIDE_EFFECTING),
    )(*[pltpu.with_memory_space_constraint(a, pltpu.HBM) for a in list(srcs) + list(lands)])
    sems = [(outs[2 * gi], outs[2 * gi + 1]) for gi in range(ng)]
    return sems, outs[2 * ng:2 * ng + n], outs[2 * ng + n:2 * ng + 2 * n], outs[-1]


def _behind(v, token):
    if token is None:
        return v
    return v + token[0, 0].astype(v.dtype)


def _split_wait(srcs, lands, sems, after, *, scatter, name):
    m = len(srcs)

    def body(*refs):
        src_refs, land_refs = refs[:m], refs[m:2 * m]
        send_sems, recv_sems = refs[2 * m], refs[2 * m + 1]
        for send, recv in _split_copies(src_refs, land_refs, send_sems, recv_sems, scatter):
            send.wait_send()
            recv.wait_recv()

    outs = pl.pallas_call(
        body, name=name,
        out_shape=tuple(pltpu.HBM(a.shape, a.dtype) for a in list(srcs) + list(lands)),
        in_specs=[HBM_SPEC] * (2 * m) + [SEM_SPEC, SEM_SPEC, pl.BlockSpec(memory_space=pl.ANY)],
        out_specs=tuple([HBM_SPEC] * (2 * m)),
        input_output_aliases={i: i for i in range(2 * m)},
        compiler_params=pltpu.CompilerParams(has_side_effects=pltpu.SideEffectType.DATAFLOW_SIDE_EFFECTING),
    )(*srcs, *lands, sems[0], sems[1], after)
    return outs[m:]


TL_FIRST = (1, 2, 4, 6)
TL_ICI = (2, 4, 6)
EFFECT = pltpu.SideEffectType.DATAFLOW_SIDE_EFFECTING


def _tl_first(src_refs, land_refs, send_sems, recv_sems):
    x, y, c = lax.axis_index("x"), lax.axis_index("y"), lax.axis_index("c")
    me = 4 * x + 2 * y + c
    out = []
    for j, (src, land) in enumerate(zip(src_refs, land_refs)):
        for i, d in enumerate(TL_FIRST):
            peer, pidx = _peer(x, y, c, d)
            k = len(TL_FIRST) * j + i
            send = pltpu.make_async_remote_copy(src_ref=src, dst_ref=land.at[me], send_sem=send_sems.at[k],
                                                recv_sem=recv_sems.at[k], device_id=peer, device_id_type=MESH)
            recv = pltpu.make_async_remote_copy(src_ref=src, dst_ref=land.at[pidx], send_sem=send_sems.at[k],
                                                recv_sem=recv_sems.at[k], device_id=peer, device_id_type=MESH)
            out.append((d, send, recv))
    return out


def _tl_second(land_refs, send_sems, recv_sems):
    x, y, c = lax.axis_index("x"), lax.axis_index("y"), lax.axis_index("c")
    sibling, _ = _peer(x, y, c, 1)
    out = []
    for j, land in enumerate(land_refs):
        for i, d in enumerate(TL_ICI):
            _, mine = _peer(x, y, c, d)
            _, theirs = _peer(x, y, c, d + 1)
            k = len(TL_ICI) * j + i
            send = pltpu.make_async_remote_copy(src_ref=land.at[mine], dst_ref=land.at[mine], send_sem=send_sems.at[k],
                                                recv_sem=recv_sems.at[k], device_id=sibling, device_id_type=MESH)
            recv = pltpu.make_async_remote_copy(src_ref=land.at[mine], dst_ref=land.at[theirs],
                                                send_sem=send_sems.at[k], recv_sem=recv_sems.at[k],
                                                device_id=sibling, device_id_type=MESH)
            out.append((send, recv))
    return out


def _tl_start(srcs, lands, groups, *, name):
    n, ng = len(srcs), len(groups)

    def body(*refs):
        src_refs, land_refs = refs[:n], refs[n:2 * n]
        sems = refs[2 * n:2 * n + 2 * ng]
        for gi, g in enumerate(groups):
            for _, send, _ in _tl_first([src_refs[t] for t in g], [land_refs[t] for t in g], sems[2 * gi],
                                        sems[2 * gi + 1]):
                send.start()
        refs[-1][...] = jnp.zeros_like(refs[-1])

    sem_shapes = []
    for g in groups:
        sem_shapes += [pltpu.SemaphoreType.DMA((len(g) * len(TL_FIRST),))] * 2
    thru = [pltpu.HBM(a.shape, a.dtype) for a in list(srcs) + list(lands)]
    outs = pl.pallas_call(
        body, name=name,
        out_shape=tuple(sem_shapes + thru + [jax.ShapeDtypeStruct((8, 128), F32)]),
        in_specs=[HBM_SPEC] * (2 * n),
        out_specs=tuple([SEM_SPEC] * (2 * ng) + [HBM_SPEC] * (2 * n) + [pl.BlockSpec(memory_space=pltpu.VMEM)]),
        input_output_aliases={i: 2 * ng + i for i in range(2 * n)},
        compiler_params=pltpu.CompilerParams(has_side_effects=EFFECT),
    )(*[pltpu.with_memory_space_constraint(a, pltpu.HBM) for a in list(srcs) + list(lands)])
    sems = [(outs[2 * gi], outs[2 * gi + 1]) for gi in range(ng)]
    return sems, outs[2 * ng:2 * ng + n], outs[2 * ng + n:2 * ng + 2 * n], outs[-1]


def _tl_forward(srcs, lands, sems1, after, *, name):
    m = len(srcs)

    def body(*refs):
        src_refs, land_refs = refs[:m], refs[m:2 * m]
        send1, recv1 = refs[2 * m], refs[2 * m + 1]
        send2, recv2 = refs[2 * m + 3], refs[2 * m + 4]
        for d, _, recv in _tl_first(src_refs, land_refs, send1, recv1):
            if d in TL_ICI:
                recv.wait_recv()
        for send, _ in _tl_second(land_refs, send2, recv2):
            send.start()

    sem = pltpu.SemaphoreType.DMA((m * len(TL_ICI),))
    outs = pl.pallas_call(
        body, name=name,
        out_shape=tuple([sem, sem] + [pltpu.HBM(a.shape, a.dtype) for a in list(srcs) + list(lands)]),
        in_specs=[HBM_SPEC] * (2 * m) + [SEM_SPEC, SEM_SPEC, pl.BlockSpec(memory_space=pl.ANY)],
        out_specs=tuple([SEM_SPEC, SEM_SPEC] + [HBM_SPEC] * (2 * m)),
        input_output_aliases={i: 2 + i for i in range(2 * m)},
        compiler_params=pltpu.CompilerParams(has_side_effects=EFFECT),
    )(*srcs, *lands, sems1[0], sems1[1], after)
    return (outs[0], outs[1]), outs[2:2 + m], outs[2 + m:2 + 2 * m]


def _tl_wait(srcs, lands, sems1, sems2, after, *, name):
    m = len(srcs)

    def body(*refs):
        src_refs, land_refs = refs[:m], refs[m:2 * m]
        send1, recv1, send2, recv2 = refs[2 * m:2 * m + 4]
        for d, send, recv in _tl_first(src_refs, land_refs, send1, recv1):
            send.wait_send()
            if d not in TL_ICI:
                recv.wait_recv()
        for send, recv in _tl_second(land_refs, send2, recv2):
            send.wait_send()
            recv.wait_recv()

    outs = pl.pallas_call(
        body, name=name,
        out_shape=tuple(pltpu.HBM(a.shape, a.dtype) for a in list(srcs) + list(lands)),
        in_specs=[HBM_SPEC] * (2 * m) + [SEM_SPEC] * 4 + [pl.BlockSpec(memory_space=pl.ANY)],
        out_specs=tuple([HBM_SPEC] * (2 * m)),
        input_output_aliases={i: i for i in range(2 * m)},
        compiler_params=pltpu.CompilerParams(has_side_effects=EFFECT),
    )(*srcs, *lands, sems1[0], sems1[1], sems2[0], sems2[1], after)
    return outs[m:]


TM_PROJ = 512
TN_PROJ = 512
TM_ROW = 512
TM_NN = 512
TK_TN = 2048
TM_ADAM = 416
TN_FFN = F // 2
TN_IN = NIN // 4


def _tn(a, b, name, tn, token=None):
    if a.ndim == 2:
        a = a[None]
    return _tn_matmul(a, b, token, tn=tn, tk=TK_TN, name=name)


def _local_step(x, tgt, mods, g1, gm, g2, gf, convw8, sinks, w_get, g_put, tables=None):
    T = x.shape[0]
    sh1, sc1, gt1, sh2, sc2, gt2, sh3, sc3, gt3 = [mods[i:i + 1] for i in range(N_MOD)]
    cos, sin = _rope_tables(T) if tables is None else tables
    behind = _behind

    w = dict(w_get("gu1", mods))
    h1, ab1 = _norm_proj(x, g1, sc1, sh1, w["gu1"], tm=TM_PROJ, tn=TN_PROJ, name="ffn1_up")
    w.update(w_get("d1", ab1))
    x1, y1 = _ffn_down_fwd(ab1, w["d1"], x, gt1, tm=TM_ROW, name="ffn1_down")
    w.update(w_get("mix", x1))
    h2, proj, qs, kr = _norm_proj(x1, gm, sc2, sh2, w["win"], (cos, sin), tm=TM_PROJ, tn=TN_PROJ, name="mix_in")
    bias = _attn_bias()
    attn, lse = _attn_fwd(qs, kr, proj, bias, sinks, name="attn_fwd")
    x2, gc, yc, ya, mg, o = _mixer_mid_fwd(proj, attn, w["cp"], w["ap"], w["out"], convw8, x1, gt2,
                                           tm=TM_ROW, name="mix_mid")
    w.update(w_get("ffn2", x2))
    h3, ab2, y2, dx3, lsum, dgf = _ffn_fwd(x2, g2, sc3, sh3, gt3, w["gu2"], w["d2"], (tgt, gf), tm=TM_ROW,
                                           name="ffn2_final")

    dab2, dgt3, g_d2 = _ffn_down_bwd_dw(dx3, y2, gt3, ab2, w["d2"], tm=TM_ROW, name="ffn2_down_bwd")
    dx2, dsh3, dsc3, dg2 = _nn_bwd_norm(dab2, w["gu2"], x2, g2, sc3, dx3, tm=TM_NN, name="ffn2_up_bwd")
    g_gu2 = _tn(dab2, h3, "ffn2_up_dw", TN_FFN)
    tok = g_put(dict(gu2=g_gu2, d2=g_d2))

    dout, dyc, dya, dgc, dat, dproj, dgt2 = _mixer_mid_bwd(dx2, behind(gt2, tok), o, proj, yc, ya, w["out"], w["cp"],
                                                           w["ap"], tm=TM_ROW, name="mix_mid_bwd")
    g_out = _tn(mg, dout, "mix_out_dw", D)
    g_cp = _tn(gc, dyc, "mix_cp_dw", D)
    g_ap = _tn(attn, dya, "mix_ap_dw", D)
    dproj, dkc, dkp, dvc, dvp, dsink = _attn_bwd(qs, kr, proj, bias, sinks, lse, attn, dat, cos, sin, dproj,
                                                 name="attn_bwd")
    dproj = _dkv_combine(dkc, dkp, dvc, dvp, dproj, name="attn_dkv")
    dproj, dcw = _conv_bwd(dgc, proj, convw8, dproj, tm=TM_ROW, name="conv_bwd")
    g_in = _tn(dproj, h2, "mix_in_dw", TN_IN)
    tok = g_put(dict(win=g_in, cp=g_cp, ap=g_ap, out=g_out))
    dx1, dsh2, dsc2, dgm = _nn_bwd_norm(dproj[None], w["win"], x1, gm, behind(sc2, tok), dx2, tm=TM_NN,
                                        name="mix_in_bwd")

    dab1, dgt1, g_d1 = _ffn_down_bwd_dw(dx1, y1, gt1, ab1, w["d1"], tm=TM_ROW, name="ffn1_down_bwd")
    tok = g_put(dict(d1=g_d1))
    g_gu1 = _tn(dab1, h1, "ffn1_up_dw", TN_FFN, tok)
    tok = g_put(dict(gu1=g_gu1))
    dx0, dsh1, dsc1, dg1 = _nn_bwd_norm(dab1, w["gu1"], x, g1, behind(sc1, tok), dx1, tm=TM_NN,
                                        name="ffn1_up_bwd")

    small = dict(mods=jnp.concatenate([dsh1, dsc1, dgt1, dsh2, dsc2, dgt2, dsh3, dsc3, dgt3], axis=0),
                 g1=dg1, gm=dgm, g2=dg2, gf=dgf, convw=dcw[0:3], sinks=dsink[:, 0:N_HEADS])
    return lsum, dx0, small


BIG = ("gu1", "d1", "win", "cp", "ap", "out", "gu2", "d2")
TRANSPOSED = ("gu1", "win", "gu2")
SMALL_ROWS = 24
R_MODS, R_G1, R_GM, R_G2, R_GF, R_CONV, R_SINK, R_LOSS = 0, 9, 10, 11, 12, 13, 16, 17


def _pad_to(a, rows, cols):
    return jnp.pad(a, ((0, rows - a.shape[0]), (0, cols - a.shape[1])))


def _pack_small(b_ada, g1, gm, g2, gf, conv, sinks, lsum):
    rows = [b_ada.reshape(N_MOD, D), g1.reshape(1, D), gm.reshape(1, D), g2.reshape(1, D), gf.reshape(1, D),
            _pad_to(conv.reshape(3, -1), 3, D), _pad_to(sinks.reshape(1, N_HEADS), 1, D), lsum.reshape(1, D)]
    return _pad_to(jnp.concatenate(rows, axis=0), SMALL_ROWS, D)


def kernel(x, c, w_ada, b_ada, g_ffn1, w1_gu, w1_down, g_mix, w_in, conv_w, w_conv_proj, w_attn_proj, sinks, w_out, g_ffn2, w2_gu, w2_down, g_final, loss_target, m_w_ada, m_b_ada, m_g_ffn1, m_w1_gu, m_w1_down, m_g_mix, m_w_in, m_conv_w, m_w_conv_proj, m_w_attn_proj, m_sinks, m_w_out, m_g_ffn2, m_w2_gu, m_w2_down, m_g_final, v_w_ada, v_b_ada, v_g_ffn1, v_w1_gu, v_w1_down, v_g_mix, v_w_in, v_conv_w, v_w_conv_proj, v_w_attn_proj, v_sinks, v_w_out, v_g_ffn2, v_w2_gu, v_w2_down, v_g_final):
    me = 4 * lax.axis_index("x") + 2 * lax.axis_index("y") + lax.axis_index("c")
    ada_cols = w_ada.shape[2]
    conv_cols = conv_w.shape[2]

    native = dict(gu1=w1_gu[0], d1=w1_down[0], win=w_in[0], cp=w_conv_proj[0], ap=w_attn_proj[0], out=w_out[0],
                  gu2=w2_gu[0], d2=w2_down[0])

    def shard(n, token):
        a = _behind(native[n], token)
        return (a.T if n in TRANSPOSED else a).astype(BF)

    c_all, conv_all, _ = _exchange([c, _pad_to(conv_w[0], 8, conv_cols)], scatter=False, name="gather_cond")
    c_all = c_all.reshape(N_DEV, D)
    conv_full = conv_all[:, 0:3, :].transpose(1, 0, 2).reshape(3, D)

    b_cols = lax.dynamic_slice(b_ada, (0, me * ada_cols), (1, ada_cols))
    mods_cols = _mods_part(c_all, w_ada[0], b_cols, name="ada_mods")
    mods_all, mods_token = _exchange([mods_cols], scatter=False, name="gather_mods")
    mods = lax.dynamic_index_in_dim(mods_all, me, axis=1, keepdims=False).reshape(N_MOD, D)

    groups = dict(gu1=("gu1",), d1=("d1",), mix=("win", "cp", "ap", "out"), ffn2=("gu2", "d2"))
    in_flight = {}
    first = [shard("gu1", mods_token)]
    sems, srcs, lands, token = _tl_start(first, [_own_slot(s, me) for s in first], [[0]],
                                         name="gather_weights_start_gu1")
    in_flight["gu1"] = [sems[0], srcs, lands, None]
    rest = [n for n in BIG if n != "gu1"]
    shards = [shard(n, token) for n in rest]
    rest_groups = [[rest.index(n) for n in names] for g, names in groups.items() if g != "gu1"]
    sems, srcs, lands, rest_token = _tl_start(shards, [_own_slot(s, me) for s in shards], rest_groups,
                                              name="gather_weights_start_rest")
    for (g, names), gsems, idx in zip([kv for kv in groups.items() if kv[0] != "gu1"], sems, rest_groups):
        in_flight[g] = [gsems, [srcs[t] for t in idx], [lands[t] for t in idx], None]

    def forward(group, after):
        sems1, gsrcs, glands, _ = in_flight[group]
        sems2, gsrcs, glands = _tl_forward(gsrcs, glands, sems1, after, name="gather_weights_forward_" + group)
        in_flight[group] = [sems1, gsrcs, glands, sems2]

    forward_early = dict(d1="mix", mix="ffn2")

    tables = _rope_tables(x.shape[1], rest_token)

    def w_get(group, after):
        if group == "gu1":
            after = tables[0]
        if in_flight[group][3] is None:
            forward(group, after)
        sems1, gsrcs, glands, sems2 = in_flight[group]
        landed = _tl_wait(gsrcs, glands, sems1, sems2, after, name="gather_weights_wait_" + group)
        if group in forward_early:
            forward(forward_early[group], landed[0])
        return {n: a.reshape(-1, D) for n, a in zip(groups[group], landed)}

    pending = []

    def g_put(gs):
        names = tuple(gs)
        srcs = [gs[n].reshape(N_DEV, -1, D) for n in names]
        lands = [_own_slot(lax.dynamic_index_in_dim(s, me, axis=0, keepdims=False), me) for s in srcs]
        sems, srcs, lands, tok = _split_start(srcs, lands, [list(range(len(names)))], scatter=True,
                                              name="scatter_grads_start_" + names[0])
        pending.append((names, sems[0], srcs, lands))
        return tok

    lsum, grad_x, small = _local_step(x[0], loss_target[0], mods, g_ffn1, g_mix, g_ffn2, g_final[None],
                                      _pad_to(conv_full, 8, D), sinks[0], w_get, g_put, tables)

    packed = _pack_small(small["mods"], small["g1"], small["gm"], small["g2"], small["gf"], small["convw"],
                         small["sinks"], lsum)
    sm_sems, sm_srcs, sm_lands, sm_token = _split_start([packed], [_own_slot(packed, me)], [[0]], scatter=False,
                                                        name="gather_small_start")

    w_of = dict(ada=w_ada, gu1=w1_gu, d1=w1_down, win=w_in, cp=w_conv_proj, ap=w_attn_proj, out=w_out, gu2=w2_gu,
                d2=w2_down)
    m_of = dict(ada=m_w_ada, gu1=m_w1_gu, d1=m_w1_down, win=m_w_in, cp=m_w_conv_proj, ap=m_w_attn_proj, out=m_w_out,
                gu2=m_w2_gu, d2=m_w2_down)
    v_of = dict(ada=v_w_ada, gu1=v_w1_gu, d1=v_w1_down, win=v_w_in, cp=v_w_conv_proj, ap=v_w_attn_proj, out=v_w_out,
                gu2=v_w2_gu, d2=v_w2_down)
    upd = {}
    after = sm_token
    for k, (names, sems, srcs, lands) in enumerate(pending):
        if k == 2:
            (packed_all,) = _split_wait(sm_srcs, sm_lands, sm_sems[0], after, scatter=False, name="gather_small_wait")
            gsmall = _sum8(packed_all, name="sum_small")
            loss = (0.5 / D) * jnp.sum(gsmall[R_LOSS])
            after = gsmall
        parts = _split_wait(srcs, lands, sems, after, scatter=True, name="scatter_grads_wait_" + names[0])
        for n, p in zip(names, parts):
            if n in TRANSPOSED:
                res = _adam(jnp.swapaxes(w_of[n], 1, 2), p, jnp.swapaxes(m_of[n], 1, 2), jnp.swapaxes(v_of[n], 1, 2),
                            tm=TM_ADAM, name="adam_" + n)
                upd[n] = [jnp.swapaxes(t, 1, 2) for t in res]
            else:
                upd[n] = _adam(w_of[n], p, m_of[n], v_of[n], tm=TM_ADAM, name="adam_" + n)
        after = upd[names[-1]][1]

    gm_cols = lax.dynamic_slice(packed_all[:, R_MODS:R_MODS + N_MOD, :].reshape(N_DEV, N_MOD * D),
                                (0, me * ada_cols), (N_DEV, ada_cols))
    upd["ada"] = _adam(w_ada, _wada_grad(c_all.T, gm_cols, name="ada_dw"), m_w_ada, v_w_ada, tm=256, name="adam_ada")
    conv_g = lax.dynamic_slice(gsmall, (R_CONV, me * conv_cols), (3, conv_cols))

    def natural(b, g1, gm, g2, gf, cw, sk):
        return dict(b_ada=b, g_ffn1=g1, g_mix=gm, g_ffn2=g2, g_final=gf[None], conv_w=cw[0], sinks=sk)

    small_out = _adam_small(gsmall, conv_g, natural(b_ada, g_ffn1, g_mix, g_ffn2, g_final, conv_w, sinks),
                            natural(m_b_ada, m_g_ffn1, m_g_mix, m_g_ffn2, m_g_final, m_conv_w, m_sinks),
                            natural(v_b_ada, v_g_ffn1, v_g_mix, v_g_ffn2, v_g_final, v_conv_w, v_sinks),
                            name="adam_small")
    for res in small_out:
        res["g_final"] = res["g_final"][0]
        res["conv_w"] = res["conv_w"][None]

    big_name = dict(w_ada="ada", w1_gu="gu1", w1_down="d1", w_in="win", w_conv_proj="cp", w_attn_proj="ap",
                    w_out="out", w2_gu="gu2", w2_down="d2")
    order = ("w_ada", "b_ada", "g_ffn1", "w1_gu", "w1_down", "g_mix", "w_in", "conv_w", "w_conv_proj", "w_attn_proj",
             "sinks", "w_out", "g_ffn2", "w2_gu", "w2_down", "g_final")
    outs = [loss, grad_x[None]]
    for kind in range(4):
        for n in order:
            outs.append(upd[big_name[n]][kind] if n in big_name else small_out[kind][n])
    return tuple(outs)
```

```python
import jax
import jax.numpy as jnp
from jax import lax
from jax.experimental import pallas as pl
from jax.experimental.pallas import tpu as pltpu

D = 1024
F = 2816
NIN = 6656
N_HEADS = 16
N_KV = 4
HEAD_DIM = 64
BLK = 128
N_MOD = 9
N_DEV = 8
EPS = 1e-6
NEG_INF = -1e30
ROPE_THETA = 10000.0
O_BG, O_CG, O_U, O_Q, O_K, O_V, O_ZC, O_ZA = 0, 1024, 2048, 3072, 4096, 4352, 4608, 5632

ADAM_LR = 0.001
ADAM_B1 = 0.9
ADAM_B2 = 0.999
ADAM_EPS = 1e-08
ADAM_WD = 0.01
ADAM_STEP = 10

BF = jnp.bfloat16
F32 = jnp.float32
VMEM_LIMIT = 56 * 1024 * 1024
MXU_N = 256
MESH = pl.DeviceIdType.MESH

NT = (((1,), (1,)), ((), ()))
TN = (((0,), (0,)), ((), ()))


def _cp(sem=None):
    return pltpu.CompilerParams(dimension_semantics=sem, vmem_limit_bytes=VMEM_LIMIT)


def _tile(n, pref):
    if n <= pref:
        return n
    for t in range(pref - pref % 16, 15, -16):
        if n % t == 0:
            return t
    raise ValueError((n, pref))


def _sigmoid(v):
    return 0.5 * jnp.tanh(0.5 * v) + 0.5


def _row(i):
    return (i, 0)


def _const2(*_):
    return (0, 0)


def _resident(shape):
    return pl.BlockSpec(shape, lambda *_: (0,) * len(shape), pipeline_mode=pl.Buffered(1))


def _norm_proj(x, g, sc, sh, wt, rope=None, *, tm, tn, name):
    T, N = x.shape[0], wt.shape[0]
    tm = _tile(T, tm)

    def body(x_ref, g_ref, sc_ref, sh_ref, w_ref, *rest):
        if rope is None:
            h_ref, o_ref = rest
        else:
            c_ref, s_ref, h_ref, o_ref, qs_ref, kr_ref = rest
        xv = x_ref[...]
        r = lax.rsqrt(jnp.mean(xv * xv, axis=-1, keepdims=True) + EPS)
        hb = ((xv * r) * g_ref[...] * (1.0 + sc_ref[...]) + sh_ref[...]).astype(BF)
        h_ref[...] = hb
        for c0 in range(0, N, tn):
            cols = pl.ds(c0, tn)
            o_ref[:, cols] = lax.dot_general(hb, w_ref[cols, :], NT, preferred_element_type=F32).astype(BF)
            if rope is not None and c0 < O_V <= c0 + tn:
                _attn_prep_tile(o_ref, c_ref, s_ref, qs_ref, kr_ref, tm)

    vec = pl.BlockSpec((1, D), _const2)
    rowspec = pl.BlockSpec((tm, D), _row)
    in_specs = [rowspec, vec, vec, vec, _resident((N, D))]
    out_specs = [rowspec, pl.BlockSpec((tm, N), _row)]
    out_shape = [jax.ShapeDtypeStruct((T, D), BF), jax.ShapeDtypeStruct((T, N), BF)]
    args = [x, g, sc, sh, wt]
    if rope is not None:
        in_specs += [pl.BlockSpec((tm, 128), _row)] * 2
        out_specs += [pl.BlockSpec((N_KV, 4 * tm, 128), lambda i: (0, i, 0)), pl.BlockSpec((tm, 256), _row)]
        out_shape += [jax.ShapeDtypeStruct((N_KV, 4 * T, 128), BF), jax.ShapeDtypeStruct((T, 256), BF)]
        args += list(rope)
    return pl.pallas_call(
        body, name=name, grid=(T // tm,),
        in_specs=in_specs, out_specs=out_specs, out_shape=out_shape,
        compiler_params=_cp(("parallel",)),
    )(*args)


def _ffn_down_fwd(ab, wd, x, gt, *, tm, name):
    T = x.shape[0]
    tm = _tile(T, tm)

    def body(a_ref, b_ref, wd_ref, x_ref, gt_ref, xo_ref, y_ref):
        y = None
        for c0 in range(0, F, MXU_N):
            cols = pl.ds(c0, MXU_N)
            a = a_ref[:, cols].astype(F32)
            act = (a * _sigmoid(a) * b_ref[:, cols].astype(F32)).astype(BF)
            part = jnp.dot(act, wd_ref[cols, :], preferred_element_type=F32)
            y = part if y is None else y + part
        y_ref[...] = y.astype(BF)
        xo_ref[...] = x_ref[...] + (0.5 * gt_ref[...]) * y

    return pl.pallas_call(
        body, name=name, grid=(T // tm,),
        in_specs=[pl.BlockSpec((tm, F), lambda i: (i, 0)), pl.BlockSpec((tm, F), lambda i: (i, 1)),
                  _resident((F, D)), pl.BlockSpec((tm, D), _row), pl.BlockSpec((1, D), _const2)],
        out_specs=[pl.BlockSpec((tm, D), _row), pl.BlockSpec((tm, D), _row)],
        out_shape=[jax.ShapeDtypeStruct((T, D), F32), jax.ShapeDtypeStruct((T, D), BF)],
        compiler_params=_cp(("parallel",)),
    )(ab, ab, wd, x, gt)


def _ffn_fwd(x, g, sc, sh, gt, wgu, wd, final, *, tm, name):
    T = x.shape[0]
    tm = _tile(T, tm)
    last = final is not None

    def body(x_ref, g_ref, sc_ref, sh_ref, gt_ref, wgu_ref, wd_ref, *rest):
        if last:
            t_ref, gf_ref, h_ref, ab_ref, y_ref, dx_ref, ls_ref, dgf_ref = rest
        else:
            h_ref, ab_ref, y_ref, xo_ref = rest
        xv = x_ref[...]
        r = lax.rsqrt(jnp.mean(xv * xv, axis=-1, keepdims=True) + EPS)
        hb = ((xv * r) * g_ref[...] * (1.0 + sc_ref[...]) + sh_ref[...]).astype(BF)
        h_ref[...] = hb
        y = None
        for c0 in range(0, F, MXU_N):
            a = lax.dot_general(hb, wgu_ref[pl.ds(c0, MXU_N), :], NT, preferred_element_type=F32)
            b = lax.dot_general(hb, wgu_ref[pl.ds(F + c0, MXU_N), :], NT, preferred_element_type=F32)
            ab = a.astype(BF)
            bb = b.astype(BF)
            ab_ref[:, pl.ds(c0, MXU_N)] = ab
            ab_ref[:, pl.ds(F + c0, MXU_N)] = bb
            a = ab.astype(F32)
            act = (a * _sigmoid(a) * bb.astype(F32)).astype(BF)
            part = jnp.dot(act, wd_ref[pl.ds(c0, MXU_N), :], preferred_element_type=F32)
            y = part if y is None else y + part
        y_ref[...] = y.astype(BF)
        xo = xv + (0.5 * gt_ref[...]) * y
        if not last:
            xo_ref[...] = xo
            return

        @pl.when(pl.program_id(0) == 0)
        def _():
            ls_ref[...] = jnp.zeros_like(ls_ref)
            dgf_ref[...] = jnp.zeros_like(dgf_ref)
        gv = gf_ref[...]
        r = lax.rsqrt(jnp.mean(xo * xo, axis=-1, keepdims=True) + EPS)
        xh = xo * r
        e = xh * gv - t_ref[...]
        ls_ref[...] += jnp.sum(e * e, axis=0, keepdims=True)
        dy = e * (1.0 / D)
        dgf_ref[...] += jnp.sum(dy * xh, axis=0, keepdims=True)
        dxh = dy * gv
        dx_ref[...] = r * (dxh - xh * jnp.mean(dxh * xh, axis=-1, keepdims=True))

    vec = pl.BlockSpec((1, D), _const2)
    rowspec = pl.BlockSpec((tm, D), _row)
    in_specs = [rowspec, vec, vec, vec, vec, _resident((2 * F, D)), _resident((F, D))]
    out_specs = [rowspec, pl.BlockSpec((tm, 2 * F), _row), rowspec, rowspec]
    out_shape = [jax.ShapeDtypeStruct((T, D), BF), jax.ShapeDtypeStruct((T, 2 * F), BF),
                 jax.ShapeDtypeStruct((T, D), BF), jax.ShapeDtypeStruct((T, D), F32)]
    args = [x, g, sc, sh, gt, wgu, wd]
    if last:
        in_specs += [rowspec, vec]
        out_specs += [vec, vec]
        out_shape += [jax.ShapeDtypeStruct((1, D), F32)] * 2
        args += list(final)
    return pl.pallas_call(
        body, name=name, grid=(T // tm,),
        in_specs=in_specs, out_specs=out_specs, out_shape=out_shape,
        compiler_params=_cp(("arbitrary",) if last else ("parallel",)),
    )(*args)


def _ffn_down_bwd_dw(dxo, y, gt, ab, wd, *, tm, name):
    T = dxo.shape[0]
    tm = _tile(T, tm)
    nt = T // tm
    hw = F // 2
    chunks = [(c0, min(MXU_N, hw - c0)) for c0 in range(0, hw, MXU_N)]

    def body(dxo_ref, y_ref, gt_ref, a_ref, b_ref, wd_ref, dab_ref, dgt_ref, dwd_ref, dys, dyt, acc, stage, sem):
        i, j = pl.program_id(0), pl.program_id(1)

        @pl.when(jnp.logical_and(i == 0, j == 0))
        def _():
            dgt_ref[...] = jnp.zeros_like(dgt_ref)

        @pl.when(j == 0)
        def _():
            dxv = dxo_ref[...]
            dgt_ref[...] += 0.5 * jnp.sum(dxv * y_ref[...].astype(F32), axis=0, keepdims=True)
            dyf = (0.5 * gt_ref[...]) * dxv
            dys[...] = dyf.astype(BF)
            dyt[...] = dyf.T.astype(BF)

        def half(jj):
            @pl.when(i == 0)
            def _():
                acc[jj] = jnp.zeros((D, hw), F32)

            dy = dys[...]
            dy_t = dyt[...]
            for c0, cw in chunks:
                cols = pl.ds(c0, cw)
                dact = lax.dot_general(dy, wd_ref[pl.ds(jj * hw + c0, cw), :], NT, preferred_element_type=F32)
                a = a_ref[:, cols].astype(F32)
                b = b_ref[:, cols].astype(F32)
                s = _sigmoid(a)
                silu = a * s
                dab_ref[0, :, cols] = (dact * b * (s * (1.0 + a * (1.0 - s)))).astype(BF)
                dab_ref[1, :, cols] = (dact * silu).astype(BF)
                acc[jj, :, cols] += jnp.dot(dy_t, (silu * b).astype(BF), preferred_element_type=F32)

            @pl.when(i == nt - 1)
            def _():
                for c0, cw in chunks:
                    stage[0:cw, :] = acc[jj, :, pl.ds(c0, cw)].T.astype(BF)
                    out = pltpu.make_async_copy(stage.at[pl.ds(0, cw)], dwd_ref.at[pl.ds(jj * hw + c0, cw)], sem)
                    out.start()
                    out.wait()

        for jj in range(2):
            pl.when(j == jj)(lambda jj=jj: half(jj))

    vec = pl.BlockSpec((1, D), _const2)
    rowspec = pl.BlockSpec((tm, D), lambda i, j: (i, 0))
    return pl.pallas_call(
        body, name=name, grid=(nt, 2),
        in_specs=[rowspec, rowspec, vec, pl.BlockSpec((tm, hw), lambda i, j: (i, j)),
                  pl.BlockSpec((tm, hw), lambda i, j: (i, j + 2)), _resident((F, D))],
        out_specs=[pl.BlockSpec((2, tm, hw), lambda i, j: (0, i, j)), vec, pl.BlockSpec(memory_space=pl.ANY)],
        out_shape=[jax.ShapeDtypeStruct((2, T, F), BF), jax.ShapeDtypeStruct((1, D), F32),
                   jax.ShapeDtypeStruct((F, D), BF)],
        scratch_shapes=[pltpu.VMEM((tm, D), BF), pltpu.VMEM((D, tm), BF), pltpu.VMEM((2, D, hw), F32),
                        pltpu.VMEM((MXU_N, D), BF), pltpu.SemaphoreType.DMA(())],
        compiler_params=_cp(("arbitrary", "arbitrary")),
    )(dxo, y, gt, ab, ab, wd)


def _tn_matmul(a, b, token=None, *, tn, tk, name):
    S, T, Ns = a.shape
    tn, tk = _tile(Ns, tn), _tile(T, tk)
    nk, njs = T // tk, Ns // tn
    deps = [] if token is None else [token]

    def body(a_ref, b_ref, *rest):
        o_ref, acc = rest[len(deps):]
        k = pl.program_id(1)

        @pl.when(k == 0)
        def _():
            acc[...] = jnp.zeros_like(acc)
        acc[...] += lax.dot_general(a_ref[0], b_ref[...], TN, preferred_element_type=F32)

        @pl.when(k == nk - 1)
        def _():
            o_ref[...] = acc[...].astype(BF)

    return pl.pallas_call(
        body, name=name, grid=(S * njs, nk),
        in_specs=[pl.BlockSpec((1, tk, tn), lambda j, k: (j // njs, k, j % njs)),
                  pl.BlockSpec((tk, D), lambda j, k: (k, 0))] + [pl.BlockSpec(memory_space=pl.ANY)] * len(deps),
        out_specs=pl.BlockSpec((tn, D), lambda j, k: (j, 0)),
        out_shape=jax.ShapeDtypeStruct((S * Ns, D), BF),
        scratch_shapes=[pltpu.VMEM((tn, D), F32)],
        compiler_params=_cp(("parallel", "arbitrary")),
    )(a, b, *deps)


def _nn_bwd_norm(da, w, x, g, sc, dxo, *, tm, name):
    S, T, Ks = da.shape
    tm = _tile(T, tm)
    rc = _tile(tm, 256)

    def body(da_ref, w_ref, x_ref, g_ref, sc_ref, dxo_ref, dx_ref, dsh_ref, dsc_ref, dg_ref, acc):
        @pl.when(pl.program_id(0) == 0)
        def _():
            dsh_ref[...] = jnp.zeros_like(dsh_ref)
            dsc_ref[...] = jnp.zeros_like(dsc_ref)
            dg_ref[...] = jnp.zeros_like(dg_ref)

        d = jnp.dot(da_ref[0], w_ref[0:Ks, :], preferred_element_type=F32)
        for s in range(1, S):
            d = d + jnp.dot(da_ref[s], w_ref[s * Ks:(s + 1) * Ks, :], preferred_element_type=F32)
        acc[...] = d
        gv = g_ref[...]
        sc1 = 1.0 + sc_ref[...]
        dsh = jnp.zeros((1, D), F32)
        dsc = jnp.zeros((1, D), F32)
        dg = jnp.zeros((1, D), F32)
        for r0 in range(0, tm, rc):
            rows = pl.ds(r0, rc)
            u = acc[rows, :]
            xv = x_ref[rows, :]
            r = lax.rsqrt(jnp.mean(xv * xv, axis=-1, keepdims=True) + EPS)
            xh = xv * r
            dsh = dsh + jnp.sum(u, axis=0, keepdims=True)
            dsc = dsc + jnp.sum(u * (xh * gv), axis=0, keepdims=True)
            us = u * sc1
            dg = dg + jnp.sum(us * xh, axis=0, keepdims=True)
            dxh = us * gv
            dx_ref[rows, :] = dxo_ref[rows, :] + r * (dxh - xh * jnp.mean(dxh * xh, axis=-1, keepdims=True))
        dsh_ref[...] += dsh
        dsc_ref[...] += dsc
        dg_ref[...] += dg

    vec = pl.BlockSpec((1, D), _const2)
    rowspec = pl.BlockSpec((tm, D), _row)
    return pl.pallas_call(
        body, name=name, grid=(T // tm,),
        in_specs=[pl.BlockSpec((S, tm, Ks), lambda i: (0, i, 0)), _resident((S * Ks, D)), rowspec, vec, vec, rowspec],
        out_specs=[rowspec, vec, vec, vec],
        out_shape=[jax.ShapeDtypeStruct((T, D), F32)] + [jax.ShapeDtypeStruct((1, D), F32)] * 3,
        scratch_shapes=[pltpu.VMEM((tm, D), F32)],
        compiler_params=_cp(("arbitrary",)),
    )(da, w, x, g, sc, dxo)


def _rope(t, cos, sin_signed, lt32, inverse=False):
    sel = jnp.where(lt32, pltpu.roll(t, 96, 1), pltpu.roll(t, 32, 1))
    return t * cos - sel * sin_signed if inverse else t * cos + sel * sin_signed


def _rope_tables(T, token=None):
    inv = 1.0 / (ROPE_THETA ** (jnp.arange(0, HEAD_DIM, 2, dtype=F32) / HEAD_DIM))
    ang = _behind(jnp.arange(T, dtype=F32)[:, None] * inv[None, :], token)
    cos, sin = jnp.cos(ang), jnp.sin(ang)
    cos128 = jnp.tile(cos, (1, 4))
    sin128 = jnp.tile(jnp.concatenate([-sin, sin], axis=1), (1, 2))
    return cos128, sin128


QSCALE = HEAD_DIM ** -0.5


def _lane_masks(rows):
    lane = lax.broadcasted_iota(jnp.int32, (rows, 128), 1)
    return (lane % HEAD_DIM) < (HEAD_DIM // 2), [lane < HEAD_DIM, lane >= HEAD_DIM]


def _attn_bias():
    qi = lax.broadcasted_iota(jnp.int32, (4 * BLK, 2 * BLK), 0) % BLK
    kj = lax.broadcasted_iota(jnp.int32, (4 * BLK, 2 * BLK), 1)
    band = (kj > qi) & (kj <= qi + BLK)
    return jnp.stack([jnp.where(band & (kj >= BLK), 0.0, NEG_INF), jnp.where(band, 0.0, NEG_INF)]).astype(F32)


def _attn_prep_tile(proj_ref, c_ref, s_ref, qs_ref, kr_ref, tm):
    lt32, halves = _lane_masks(BLK)
    for b in range(tm // BLK):
        rows = pl.ds(b * BLK, BLK)
        cc, sc = c_ref[rows, :], s_ref[rows, :]
        qr = [_rope(proj_ref[rows, pl.ds(O_Q + p * 128, 128)].astype(F32), cc, sc, lt32) * QSCALE for p in range(8)]
        for g in range(N_KV):
            qs_ref[g, pl.ds(4 * b * BLK, 4 * BLK), :] = _stack_heads(qr, g, halves).astype(BF)
        kr_ref[rows, :] = jnp.concatenate([_rope(proj_ref[rows, pl.ds(O_K + r * 128, 128)].astype(F32), cc, sc, lt32)
                                           for r in range(2)], axis=1).astype(BF)


ATT_BPS = 4
ATT_ROWS = ATT_BPS * BLK


def _before(n):
    return jnp.maximum(ATT_BPS * n - 1, 0)


def _attn_specs():
    return [pl.BlockSpec((N_KV, 4 * ATT_ROWS, 128), lambda n: (0, n, 0)),
            pl.BlockSpec((ATT_ROWS, 256), _row), pl.BlockSpec((BLK, 256), lambda n: (_before(n), 0)),
            pl.BlockSpec((ATT_ROWS, 256), lambda n: (n, O_V // 256)),
            pl.BlockSpec((BLK, 256), lambda n: (_before(n), O_V // 256)),
            pl.BlockSpec((2, 4 * BLK, 2 * BLK), lambda n: (0, 0, 0)),
            pl.BlockSpec(memory_space=pltpu.SMEM)]


def _bands(sb, kc_ref, kp_ref, vc_ref, vp_ref):
    own = pl.ds(sb * BLK, BLK)
    above = pl.ds((sb - 1) * BLK, BLK)
    kb, vb = [], []
    for r in range(2):
        cols = pl.ds(r * 128, 128)
        kprev = kp_ref[:, cols] if sb == 0 else kc_ref[above, cols]
        vprev = vp_ref[:, cols] if sb == 0 else vc_ref[above, cols]
        kb.append(jnp.concatenate([kprev, kc_ref[own, cols]], axis=0))
        vb.append(jnp.concatenate([vprev, vc_ref[own, cols]], axis=0))
    return kb, vb


def _block_bias(sb, bias_ref):
    return bias_ref[jnp.minimum(pl.program_id(0), 1)] if sb == 0 else bias_ref[1]


def _sink_rows(sink_ref, g):
    return jnp.concatenate([jnp.full((BLK, 128), sink_ref[4 * g + hh], F32) for hh in range(4)], axis=0)


def _both(t):
    return jnp.concatenate([t, t], axis=1)


def _unstack_heads(t, g, halves, acc):
    half = g % 2
    for hh in range(4):
        h = 4 * g + hh
        th = jnp.where(halves[half], t[hh * BLK:(hh + 1) * BLK], 0.0)
        if h % 2 != half:
            th = pltpu.roll(th, HEAD_DIM, 1)
        acc[h // 2] = acc[h // 2] + th


def _stack_heads(chunks, g, halves):
    half = g % 2
    parts = []
    for hh in range(4):
        h = 4 * g + hh
        t = chunks[h // 2]
        if h % 2 != half:
            t = pltpu.roll(t, HEAD_DIM, 1)
        parts.append(jnp.where(halves[half], t, 0.0))
    return jnp.concatenate(parts, axis=0)


def _attn_fwd(qs, kr, proj, bias, sinks, *, name):
    T = proj.shape[0]
    assert T % ATT_ROWS == 0

    def body(qs_ref, kc_ref, kp_ref, vc_ref, vp_ref, bias_ref, sink_ref, o_ref, lse_ref):
        _, h128 = _lane_masks(BLK)
        _, h256 = _lane_masks(2 * BLK)
        _, h512 = _lane_masks(4 * BLK)
        groups = range(N_KV)
        sink = [_sink_rows(sink_ref, g) for g in groups]
        for sb in range(ATT_BPS):
            rows = pl.ds(4 * sb * BLK, 4 * BLK)
            kb, vb = _bands(sb, kc_ref, kp_ref, vc_ref, vp_ref)
            outs = [jnp.zeros((BLK, 128), F32) for _ in range(8)]
            bias = _block_bias(sb, bias_ref)
            s = [lax.dot_general(qs_ref[g, rows, :], kb[g // 2], NT, preferred_element_type=F32) + bias for g in groups]
            m = [jnp.maximum(jnp.broadcast_to(jnp.max(s[g], axis=-1, keepdims=True), (4 * BLK, 128)), sink[g])
                 for g in groups]
            p = [jnp.exp(s[g] - _both(m[g])).astype(BF) for g in groups]
            vg = [jnp.where(h256[g % 2], vb[g // 2].astype(F32), 1.0).astype(BF) for g in groups]
            o = [jnp.dot(p[g], vg[g], preferred_element_type=F32) for g in groups]
            denom = [jnp.where(h512[g % 2], pltpu.roll(o[g], HEAD_DIM, 1), o[g]) + jnp.exp(sink[g] - m[g])
                     for g in groups]
            for g in groups:
                lse_ref[g, rows, :] = m[g] + jnp.log(denom[g])
                _unstack_heads(o[g] * (1.0 / denom[g]), g, h128, outs)
            o_ref[pl.ds(sb * BLK, BLK), :] = jnp.concatenate(outs, axis=1).astype(BF)

    return pl.pallas_call(
        body, name=name, grid=(T // ATT_ROWS,),
        in_specs=_attn_specs(),
        out_specs=[pl.BlockSpec((ATT_ROWS, D), _row), pl.BlockSpec((N_KV, 4 * ATT_ROWS, 128), lambda n: (0, n, 0))],
        out_shape=[jax.ShapeDtypeStruct((T, D), BF), jax.ShapeDtypeStruct((N_KV, 4 * T, 128), F32)],
        compiler_params=_cp(("parallel",)),
    )(qs, kr, kr, proj, proj, bias, sinks)


def _attn_bwd(qs, kr, proj, bias, sinks, lse, o, do, cos, sin, dproj, *, name):
    T = proj.shape[0]
    assert T % ATT_ROWS == 0

    def body(qs_ref, kc_ref, kp_ref, vc_ref, vp_ref, bias_ref, sink_ref, lse_ref, o_ref, do_ref,
             cc_ref, sc_ref, cp_ref, sp_ref, dproj_ref, dq_ref, dkc_ref, dkp_ref, dvc_ref, dvp_ref, dsink_ref):
        @pl.when(pl.program_id(0) == 0)
        def _():
            dsink_ref[...] = jnp.zeros_like(dsink_ref)
        lt32, h128 = _lane_masks(BLK)
        lane1 = lax.broadcasted_iota(jnp.int32, (1, 128), 1)
        dsink = jnp.zeros((1, 128), F32)
        groups = range(N_KV)
        for sb in range(ATT_BPS):
            own = pl.ds(sb * BLK, BLK)
            rows = pl.ds(4 * sb * BLK, 4 * BLK)
            kb, vb = _bands(sb, kc_ref, kp_ref, vc_ref, vp_ref)
            oc = [o_ref[own, pl.ds(p * 128, 128)].astype(F32) for p in range(8)]
            doc = [do_ref[own, pl.ds(p * 128, 128)].astype(F32) for p in range(8)]
            dqs = [jnp.zeros((BLK, 128), F32) for _ in range(8)]
            bias = _block_bias(sb, bias_ref)
            q = [qs_ref[g, rows, :] for g in groups]
            lse_g = [lse_ref[g, rows, :] for g in groups]
            s = [lax.dot_general(q[g], kb[g // 2], NT, preferred_element_type=F32) + bias for g in groups]
            dos = [_stack_heads(doc, g, h128) for g in groups]
            dosb = [t.astype(BF) for t in dos]
            dp = [lax.dot_general(dosb[g], vb[g // 2], NT, preferred_element_type=F32) for g in groups]
            delta = [jnp.broadcast_to(jnp.sum(dos[g] * _stack_heads(oc, g, h128), axis=-1, keepdims=True),
                                      (4 * BLK, 128)) for g in groups]
            p = [jnp.exp(s[g] - _both(lse_g[g])) for g in groups]
            ds = [(p[g] * (dp[g] - _both(delta[g]))).astype(BF) for g in groups]
            pb = [t.astype(BF) for t in p]
            dvg = [lax.dot_general(pb[g], dosb[g], TN, preferred_element_type=F32) for g in groups]
            dkg = [lax.dot_general(ds[g], q[g], TN, preferred_element_type=F32) for g in groups]
            dqg = [jnp.dot(ds[g], kb[g // 2], preferred_element_type=F32) * QSCALE for g in groups]
            dvr = [dvg[0] + dvg[1], dvg[2] + dvg[3]]
            dkr = [dkg[0] + dkg[1], dkg[2] + dkg[3]]
            for g in groups:
                _unstack_heads(dqg[g], g, h128, dqs)
                dsk = -jnp.exp(_sink_rows(sink_ref, g) - lse_g[g]) * delta[g]
                for hh in range(4):
                    val = jnp.sum(dsk[hh * BLK:(hh + 1) * BLK], axis=0, keepdims=True)
                    dsink = dsink + jnp.where(lane1 == 4 * g + hh, val, 0.0)
            cc, sc = cc_ref[own, :], sc_ref[own, :]
            cp, sp = (cp_ref[...], sp_ref[...]) if sb == 0 else (cc_ref[pl.ds((sb - 1) * BLK, BLK), :],
                                                                  sc_ref[pl.ds((sb - 1) * BLK, BLK), :])
            dq_ref[own, :] = jnp.concatenate([_rope(t, cc, sc, lt32, inverse=True) for t in dqs], axis=1).astype(BF)
            dkp_ref[own, :] = jnp.concatenate([_rope(t[:BLK], cp, sp, lt32, inverse=True) for t in dkr], axis=1)
            dkc_ref[own, :] = jnp.concatenate([_rope(t[BLK:], cc, sc, lt32, inverse=True) for t in dkr], axis=1)
            dvp_ref[own, :] = jnp.concatenate([t[:BLK] for t in dvr], axis=1)
            dvc_ref[own, :] = jnp.concatenate([t[BLK:] for t in dvr], axis=1)
        dsink_ref[...] += dsink

    kv = pl.BlockSpec((ATT_ROWS, 256), _row)
    tc = pl.BlockSpec((ATT_ROWS, 128), _row)
    tp = pl.BlockSpec((BLK, 128), lambda n: (_before(n), 0))
    return pl.pallas_call(
        body, name=name, grid=(T // ATT_ROWS,),
        in_specs=_attn_specs() + [pl.BlockSpec((N_KV, 4 * ATT_ROWS, 128), lambda n: (0, n, 0)),
                                  pl.BlockSpec((ATT_ROWS, D), _row), pl.BlockSpec((ATT_ROWS, D), _row), tc, tc, tp, tp,
                                  pl.BlockSpec(memory_space=pl.ANY)],
        out_specs=[pl.BlockSpec((ATT_ROWS, D), lambda n: (n, O_Q // D)), kv, kv, kv, kv,
                   pl.BlockSpec((1, 128), _const2)],
        out_shape=[jax.ShapeDtypeStruct(dproj.shape, BF)] + [jax.ShapeDtypeStruct((T, 256), F32)] * 4
        + [jax.ShapeDtypeStruct((1, 128), F32)],
        input_output_aliases={14: 0},
        compiler_params=_cp(("arbitrary",)),
    )(qs, kr, kr, proj, proj, bias, sinks, lse, o, do, cos, sin, cos, sin, dproj)


def _dkv_combine(dkc, dkp, dvc, dvp, dproj, *, name):
    T = dkc.shape[0]
    nb = T // BLK
    tm = _tile(T, 4 * BLK)
    bpt = tm // BLK
    nt = T // tm

    def body(dkc_ref, dkp_ref, dkn_ref, dvc_ref, dvp_ref, dvn_ref, dproj_ref, o_ref):
        keep = jnp.where(pl.program_id(0) == nt - 1, 0.0, 1.0)

        def shifted(prev_ref, next_ref):
            nxt = keep * next_ref[...]
            return nxt if bpt == 1 else jnp.concatenate([prev_ref[BLK:, :], nxt], axis=0)

        o_ref[:, 0:256] = (dkc_ref[...] + shifted(dkp_ref, dkn_ref)).astype(BF)
        o_ref[:, 256:512] = (dvc_ref[...] + shifted(dvp_ref, dvn_ref)).astype(BF)

    cur = pl.BlockSpec((tm, 256), _row)
    nxt = pl.BlockSpec((BLK, 256), lambda i: (jnp.minimum((i + 1) * bpt, nb - 1), 0))
    return pl.pallas_call(
        body, name=name, grid=(nt,),
        in_specs=[cur, cur, nxt, cur, cur, nxt, pl.BlockSpec(memory_space=pl.ANY)],
        out_specs=pl.BlockSpec((tm, 512), lambda i: (i, O_K // 512)),
        out_shape=jax.ShapeDtypeStruct(dproj.shape, BF),
        input_output_aliases={6: 0},
        compiler_params=_cp(("parallel",)),
    )(dkc, dkp, dkp, dvc, dvp, dvp, dproj)


HALO = 16


def _conv_shifts(cu, hprev, tm):
    row = lax.broadcasted_iota(jnp.int32, (8, cu.shape[1]), 0)
    h1 = hprev[HALO - 1:HALO, :]
    h2 = hprev[HALO - 2:HALO - 1, :]
    m1 = pltpu.roll(cu, 1, 0)
    m2 = pltpu.roll(cu, 2, 0)
    m1 = jnp.concatenate([jnp.where(row == 0, h1, m1[0:8]), m1[8:]], axis=0)
    m2 = jnp.concatenate([jnp.where(row == 0, h2, jnp.where(row == 1, h1, m2[0:8])), m2[8:]], axis=0)
    return m1, m2


def _mixer_mid_fwd(proj, attn, wcp, wap, wout, convw, x, gt, *, tm, name):
    T = x.shape[0]
    tm = _tile(T, tm)
    hb = tm // HALO

    def body(bg_ref, cg_ref, u_ref, hcg_ref, hu_ref, zc0_ref, zc1_ref, za0_ref, za1_ref, at_ref,
             wcp_ref, wap_ref, wout_ref, cw_ref, x_ref, gt_ref,
             x2_ref, gc_ref, yc_ref, ya_ref, mg_ref, o_ref):
        first = jnp.where(pl.program_id(0) == 0, 0.0, 1.0)
        cu = cg_ref[...].astype(F32) * u_ref[...].astype(F32)
        hprev = first * (hcg_ref[...].astype(F32) * hu_ref[...].astype(F32))
        m1, m2 = _conv_shifts(cu, hprev, tm)
        cv = cw_ref[0:1, :] * m2 + cw_ref[1:2, :] * m1 + cw_ref[2:3, :] * cu
        gc = (bg_ref[...].astype(F32) * cv).astype(BF)
        gc_ref[...] = gc
        yc = jnp.dot(gc, wcp_ref[...], preferred_element_type=F32)
        ya = jnp.dot(at_ref[...], wap_ref[...], preferred_element_type=F32)
        yc_ref[...] = yc.astype(BF)
        ya_ref[...] = ya.astype(BF)
        zc = jnp.concatenate([zc0_ref[...], zc1_ref[...]], axis=1).astype(F32)
        za = jnp.concatenate([za0_ref[...], za1_ref[...]], axis=1).astype(F32)
        mg = (_sigmoid(zc) * yc + _sigmoid(za) * ya).astype(BF)
        mg_ref[...] = mg
        o = jnp.dot(mg, wout_ref[...], preferred_element_type=F32)
        o_ref[...] = o.astype(BF)
        x2_ref[...] = x_ref[...] + gt_ref[...] * o

    wspec = pl.BlockSpec((D, D), _const2)
    rowspec = pl.BlockSpec((tm, D), _row)
    return pl.pallas_call(
        body, name=name, grid=(T // tm,),
        in_specs=[_col(tm, O_BG), _col(tm, O_CG), _col(tm, O_U), _halo_prev(hb, O_CG), _halo_prev(hb, O_U),
                  _col(tm, O_ZC, 512), _col(tm, O_ZC + 512, 512), _col(tm, O_ZA, 512), _col(tm, O_ZA + 512, 512),
                  rowspec, wspec, wspec, wspec, pl.BlockSpec((8, D), _const2), rowspec, pl.BlockSpec((1, D), _const2)],
        out_specs=[rowspec] * 6,
        out_shape=[jax.ShapeDtypeStruct((T, D), F32)] + [jax.ShapeDtypeStruct((T, D), BF)] * 5,
        compiler_params=_cp(("parallel",)),
    )(proj, proj, proj, proj, proj, proj, proj, proj, proj, attn, wcp, wap, wout, convw, x, gt)


def _col(tm, c, w=D):
    assert c % w == 0
    return pl.BlockSpec((tm, w), lambda i: (i, c // w))


def _halo_prev(hb, c):
    return pl.BlockSpec((HALO, D), lambda i: (jnp.maximum(i * hb - 1, 0), c // D))


def _halo_next(hb, nblk, c=0):
    return pl.BlockSpec((HALO, D), lambda i: (jnp.minimum((i + 1) * hb, nblk - 1), c // D))


def _mixer_mid_bwd(dx2, gt, o, proj, yc, ya, wout, wcp, wap, *, tm, name):
    T = dx2.shape[0]
    tm = _tile(T, tm)
    nt = T // tm

    def body(dx_ref, gt_ref, o_ref, zc0_ref, zc1_ref, za0_ref, za1_ref, yc_ref, ya_ref, wout_ref, wcp_ref, wap_ref,
             dout_ref, dyc_ref, dya_ref, dgc_ref, dat_ref, dproj_ref, dgt_ref, dzs, sems):
        i = pl.program_id(0)
        slot = lax.rem(i, 2)

        def slab_copy(step, s):
            return pltpu.make_async_copy(
                dzs.at[s], dproj_ref.at[pl.ds(pl.multiple_of(step * tm, tm), tm), pl.ds(O_ZC, 2 * D)], sems.at[s])

        @pl.when(i == 0)
        def _():
            dgt_ref[...] = jnp.zeros_like(dgt_ref)

        dxv = dx_ref[...]
        dgt_ref[...] += jnp.sum(dxv * o_ref[...].astype(F32), axis=0, keepdims=True)
        dout = (gt_ref[...] * dxv).astype(BF)
        dout_ref[...] = dout
        dmg = lax.dot_general(dout, wout_ref[...], NT, preferred_element_type=F32)
        sc = _sigmoid(jnp.concatenate([zc0_ref[...], zc1_ref[...]], axis=1).astype(F32))
        sa = _sigmoid(jnp.concatenate([za0_ref[...], za1_ref[...]], axis=1).astype(F32))
        dyc = (dmg * sc).astype(BF)
        dya = (dmg * sa).astype(BF)
        dyc_ref[...] = dyc
        dya_ref[...] = dya
        dzs[slot, :, 0:D] = (dmg * yc_ref[...].astype(F32) * (sc * (1.0 - sc))).astype(BF)
        dzs[slot, :, D:2 * D] = (dmg * ya_ref[...].astype(F32) * (sa * (1.0 - sa))).astype(BF)
        slab_copy(i, slot).start()
        dgc_ref[...] = lax.dot_general(dyc, wcp_ref[...], NT, preferred_element_type=F32).astype(BF)
        dat_ref[...] = lax.dot_general(dya, wap_ref[...], NT, preferred_element_type=F32).astype(BF)

        @pl.when(i > 0)
        def _():
            slab_copy(i - 1, 1 - slot).wait()

        @pl.when(i == nt - 1)
        def _():
            slab_copy(i, slot).wait()

    def zcol(c):
        return pl.BlockSpec((tm, 512), lambda i: (i, c // 512))

    wspec = pl.BlockSpec((D, D), _const2)
    rowspec = pl.BlockSpec((tm, D), _row)
    vec = pl.BlockSpec((1, D), _const2)
    return pl.pallas_call(
        body, name=name, grid=(nt,),
        in_specs=[rowspec, vec, rowspec, zcol(O_ZC), zcol(O_ZC + 512), zcol(O_ZA), zcol(O_ZA + 512),
                  rowspec, rowspec, wspec, wspec, wspec],
        out_specs=[rowspec] * 5 + [pl.BlockSpec(memory_space=pl.ANY), vec],
        out_shape=[jax.ShapeDtypeStruct((T, D), BF)] * 5 + [jax.ShapeDtypeStruct((T, NIN), BF),
                                                            jax.ShapeDtypeStruct((1, D), F32)],
        scratch_shapes=[pltpu.VMEM((2, tm, 2 * D), BF), pltpu.SemaphoreType.DMA((2,))],
        compiler_params=_cp(("arbitrary",)),
    )(dx2, gt, o, proj, proj, proj, proj, yc, ya, wout, wcp, wap)


def _conv_bwd(dgc, proj, convw, dproj, *, tm, name):
    T = dgc.shape[0]
    tm = _tile(T, tm)
    hb = tm // HALO
    nblk = T // HALO
    nt = T // tm

    def body(dgc_ref, ndgc_ref, bg_ref, nbg_ref, cg_ref, u_ref, hcg_ref, hu_ref, cw_ref, dproj_ref, dp_ref, dcw_ref):
        i = pl.program_id(0)

        @pl.when(i == 0)
        def _():
            dcw_ref[...] = jnp.zeros_like(dcw_ref)
        first = jnp.where(i == 0, 0.0, 1.0)
        last = jnp.where(i == nt - 1, 0.0, 1.0)
        cg = cg_ref[...].astype(F32)
        u = u_ref[...].astype(F32)
        bg = bg_ref[...].astype(F32)
        dg = dgc_ref[...].astype(F32)
        cu = cg * u
        hprev = first * (hcg_ref[...].astype(F32) * hu_ref[...].astype(F32))
        m1, m2 = _conv_shifts(cu, hprev, tm)
        w0, w1, w2 = cw_ref[0:1, :], cw_ref[1:2, :], cw_ref[2:3, :]
        cv = w0 * m2 + w1 * m1 + w2 * cu
        dcv = dg * bg
        nxt = last * (ndgc_ref[...].astype(F32) * nbg_ref[...].astype(F32))
        n0, n1 = nxt[0:1, :], nxt[1:2, :]
        row = lax.broadcasted_iota(jnp.int32, (8, D), 0)
        p1 = pltpu.roll(dcv, tm - 1, 0)
        p2 = pltpu.roll(dcv, tm - 2, 0)
        p1 = jnp.concatenate([p1[:tm - 8], jnp.where(row == 7, n0, p1[tm - 8:])], axis=0)
        p2 = jnp.concatenate([p2[:tm - 8], jnp.where(row == 7, n1, jnp.where(row == 6, n0, p2[tm - 8:]))], axis=0)
        dcu = w2 * dcv + w1 * p1 + w0 * p2
        dp_ref[:, 0:D] = (dg * cv).astype(BF)
        dp_ref[:, D:2 * D] = (dcu * u).astype(BF)
        dp_ref[:, 2 * D:3 * D] = (dcu * cg).astype(BF)
        dcw_ref[0:1, :] += jnp.sum(dcv * m2, axis=0, keepdims=True)
        dcw_ref[1:2, :] += jnp.sum(dcv * m1, axis=0, keepdims=True)
        dcw_ref[2:3, :] += jnp.sum(dcv * cu, axis=0, keepdims=True)

    rowspec = pl.BlockSpec((tm, D), _row)
    cw = pl.BlockSpec((8, D), _const2)
    return pl.pallas_call(
        body, name=name, grid=(nt,),
        in_specs=[rowspec, _halo_next(hb, nblk), _col(tm, O_BG), _halo_next(hb, nblk, O_BG),
                  _col(tm, O_CG), _col(tm, O_U), _halo_prev(hb, O_CG), _halo_prev(hb, O_U), cw,
                  pl.BlockSpec(memory_space=pl.ANY)],
        out_specs=[pl.BlockSpec((tm, 3 * D), _row), cw],
        out_shape=[jax.ShapeDtypeStruct(dproj.shape, BF), jax.ShapeDtypeStruct((8, D), F32)],
        input_output_aliases={9: 0},
        compiler_params=_cp(("arbitrary",)),
    )(dgc, dgc, proj, proj, proj, proj, proj, proj, convw, dproj)


def _adam_math(w, g, m, v):
    nm = ADAM_B1 * m + (1.0 - ADAM_B1) * g
    nv = ADAM_B2 * v + (1.0 - ADAM_B2) * (g * g)
    m_hat = nm / (1.0 - ADAM_B1 ** ADAM_STEP)
    v_hat = nv / (1.0 - ADAM_B2 ** ADAM_STEP)
    return -ADAM_LR * (m_hat / (jnp.sqrt(v_hat) + ADAM_EPS) + ADAM_WD * w), nm, nv


SMALL = ("b_ada", "g_ffn1", "g_mix", "g_ffn2", "g_final", "conv_w", "sinks")


def _adam_small(gsum, conv_g, w, m, v, *, name):
    nsm = len(SMALL)

    def body(*refs):
        gs_ref, cg_ref = refs[0], refs[1]
        w_refs, m_refs, v_refs = (refs[2 + k * nsm:2 + (k + 1) * nsm] for k in range(3))
        outs = refs[2 + 3 * nsm:]
        for p, n in enumerate(SMALL):
            if n == "b_ada":
                pieces = [(slice(None), slice(r * D, (r + 1) * D), gs_ref[R_MODS + r:R_MODS + r + 1, :])
                          for r in range(N_MOD)]
            elif n == "conv_w":
                pieces = [(slice(None), slice(None), cg_ref[...])]
            elif n == "sinks":
                pieces = [(slice(None), slice(None), gs_ref[R_SINK:R_SINK + 1, 0:N_HEADS])]
            else:
                row = dict(g_ffn1=R_G1, g_mix=R_GM, g_ffn2=R_G2, g_final=R_GF)[n]
                pieces = [(slice(None), slice(None), gs_ref[row:row + 1, :])]
            for rs, cs, g in pieces:
                d, nm, nv = _adam_math(w_refs[p][rs, cs], g, m_refs[p][rs, cs], v_refs[p][rs, cs])
                for k, val in enumerate((g, d, nm, nv)):
                    outs[k * nsm + p][rs, cs] = val

    args = [gsum, conv_g] + [d[n] for d in (w, m, v) for n in SMALL]
    shapes = [jax.ShapeDtypeStruct(w[n].shape, F32) for _ in range(4) for n in SMALL]
    res = pl.pallas_call(body, name=name, out_shape=shapes, compiler_params=_cp())(*args)
    return [dict(zip(SMALL, res[k * nsm:(k + 1) * nsm])) for k in range(4)]


def _adam(w, g, m, v, *, tm, name):
    _, R, C = w.shape
    tm = _tile(R, tm)
    parts = g.ndim == 3

    def body(w_ref, g_ref, m_ref, v_ref, go_ref, d_ref, nm_ref, nv_ref):
        if parts:
            gv = g_ref[0].astype(F32)
            for s in range(1, N_DEV):
                gv = gv + g_ref[s].astype(F32)
        else:
            gv = g_ref[...]
        go_ref[0] = gv
        d_ref[0], nm_ref[0], nv_ref[0] = _adam_math(w_ref[0], gv, m_ref[0], v_ref[0])

    spec = pl.BlockSpec((1, tm, C), lambda i: (0, i, 0))
    gspec = pl.BlockSpec((N_DEV, tm, C), lambda i: (0, i, 0)) if parts else pl.BlockSpec((tm, C), _row)
    return pl.pallas_call(
        body, name=name, grid=(R // tm,),
        in_specs=[spec, gspec, spec, spec], out_specs=[spec] * 4,
        out_shape=[jax.ShapeDtypeStruct((1, R, C), F32)] * 4,
        compiler_params=_cp(("parallel",)),
    )(w, g, m, v)


def _mods_part(c_all, w_ada, b_ada, *, name):
    C = w_ada.shape[1]

    def body(c_ref, w_ref, b_ref, o_ref):
        cv = c_ref[...]
        ca = cv * jax.nn.sigmoid(cv)
        o_ref[...] = jnp.dot(ca, w_ref[...], preferred_element_type=F32,
                             precision=lax.Precision.HIGHEST) + b_ref[...]

    return pl.pallas_call(
        body, name=name,
        out_shape=jax.ShapeDtypeStruct((N_DEV, C), F32),
        compiler_params=_cp(),
    )(c_all, w_ada, b_ada)


def _wada_grad(c_all_t, gm, *, name):
    C = gm.shape[1]

    def body(c_ref, g_ref, o_ref):
        cv = c_ref[...]
        ca = cv * jax.nn.sigmoid(cv)
        acc = ca[:, 0:1] * g_ref[0:1, :]
        for b in range(1, N_DEV):
            acc = acc + ca[:, b:b + 1] * g_ref[b:b + 1, :]
        o_ref[...] = acc

    return pl.pallas_call(
        body, name=name,
        out_shape=jax.ShapeDtypeStruct((D, C), F32),
        compiler_params=_cp(),
    )(c_all_t, gm)


def _peer(x, y, c, d):
    px = lax.rem(x + ((d >> 2) & 1), 2)
    py = lax.rem(y + ((d >> 1) & 1), 2)
    pc = lax.rem(c + (d & 1), 2)
    return (px, py, pc), 4 * px + 2 * py + pc


def _exchange(xs, *, scatter, name):
    n = len(xs)
    nsem = n * (N_DEV - 1)

    def body(*refs):
        ins, outs = refs[:n], refs[n:2 * n]
        token, send_sems, recv_sems, local_sems = refs[2 * n:]
        x, y, c = lax.axis_index("x"), lax.axis_index("y"), lax.axis_index("c")
        me = 4 * x + 2 * y + c
        token[...] = jnp.zeros_like(token)

        def src(t, idx):
            return ins[t].at[idx] if scatter else ins[t]

        local = [pltpu.make_async_copy(src(t, me), outs[t].at[me], local_sems.at[t]) for t in range(n)]
        for cp in local:
            cp.start()
        remote = []
        for t in range(n):
            for d in range(1, N_DEV):
                peer, pidx = _peer(x, y, c, d)
                k = t * (N_DEV - 1) + d - 1
                send = pltpu.make_async_remote_copy(src_ref=src(t, pidx), dst_ref=outs[t].at[me],
                                                    send_sem=send_sems.at[k], recv_sem=recv_sems.at[k],
                                                    device_id=peer, device_id_type=MESH)
                recv = pltpu.make_async_remote_copy(src_ref=src(t, pidx), dst_ref=outs[t].at[pidx],
                                                    send_sem=send_sems.at[k], recv_sem=recv_sems.at[k],
                                                    device_id=peer, device_id_type=MESH)
                send.start()
                remote.append((send, recv))
        for cp in local:
            cp.wait()
        for send, recv in remote:
            send.wait_send()
            recv.wait_recv()

    anyspec = pl.BlockSpec(memory_space=pl.ANY)
    out_shape = [jax.ShapeDtypeStruct(a.shape if scatter else (N_DEV,) + a.shape, a.dtype) for a in xs]
    out_shape.append(jax.ShapeDtypeStruct((8, 128), F32))
    return pl.pallas_call(
        body, name=name,
        in_specs=[anyspec] * n, out_specs=[anyspec] * n + [pl.BlockSpec(memory_space=pltpu.VMEM)],
        out_shape=out_shape,
        scratch_shapes=[pltpu.SemaphoreType.DMA((nsem,)), pltpu.SemaphoreType.DMA((nsem,)),
                        pltpu.SemaphoreType.DMA((n,))],
    )(*xs)


def _sum8(parts, *, name):
    _, R, C = parts.shape

    def body(p_ref, o_ref):
        acc = p_ref[0]
        for s in range(1, N_DEV):
            acc = acc + p_ref[s]
        o_ref[...] = acc

    return pl.pallas_call(body, name=name, out_shape=jax.ShapeDtypeStruct((R, C), F32),
                          compiler_params=_cp())(parts)


HBM_SPEC = pl.BlockSpec(memory_space=pltpu.HBM)
SEM_SPEC = pl.BlockSpec(memory_space=pltpu.SEMAPHORE)
N_PEER = N_DEV - 1


def _split_copies(src_refs, land_refs, send_sems, recv_sems, scatter):
    x, y, c = lax.axis_index("x"), lax.axis_index("y"), lax.axis_index("c")
    me = 4 * x + 2 * y + c
    pairs = []
    for j, (src, land) in enumerate(zip(src_refs, land_refs)):
        for d in range(1, N_DEV):
            peer, pidx = _peer(x, y, c, d)
            k = j * N_PEER + d - 1
            s = src.at[pidx] if scatter else src
            send = pltpu.make_async_remote_copy(src_ref=s, dst_ref=land.at[me], send_sem=send_sems.at[k],
                                                recv_sem=recv_sems.at[k], device_id=peer, device_id_type=MESH)
            recv = pltpu.make_async_remote_copy(src_ref=s, dst_ref=land.at[pidx], send_sem=send_sems.at[k],
                                                recv_sem=recv_sems.at[k], device_id=peer, device_id_type=MESH)
            pairs.append((send, recv))
    return pairs


def _own_slot(block, me):
    land = lax.empty((N_DEV,) + block.shape, block.dtype)
    return lax.dynamic_update_slice(land, block[None], (me, 0, 0))


def _split_start(srcs, lands, groups, *, scatter, name):
    n, ng = len(srcs), len(groups)

    def body(*refs):
        src_refs, land_refs = refs[:n], refs[n:2 * n]
        sems = refs[2 * n:2 * n + 2 * ng]
        token = refs[-1]
        for gi, g in enumerate(groups):
            pairs = _split_copies([src_refs[t] for t in g], [land_refs[t] for t in g], sems[2 * gi],
                                  sems[2 * gi + 1], scatter)
            for send, _ in pairs:
                send.start()
        token[...] = jnp.zeros_like(token)

    sem_shapes = []
    for g in groups:
        sem_shapes += [pltpu.SemaphoreType.DMA((len(g) * N_PEER,))] * 2
    thru = [pltpu.HBM(a.shape, a.dtype) for a in list(srcs) + list(lands)]
    outs = pl.pallas_call(
        body, name=name,
        out_shape=tuple(sem_shapes + thru + [jax.ShapeDtypeStruct((8, 128), F32)]),
        in_specs=[HBM_SPEC] * (2 * n),
        out_specs=tuple([SEM_SPEC] * (2 * ng) + [HBM_SPEC] * (2 * n) + [pl.BlockSpec(memory_space=pltpu.VMEM)]),
        input_output_aliases={i: 2 * ng + i for i in range(2 * n)},
        compiler_params=pltpu.CompilerParams(has_side_effects=pltpu.SideEffectType.DATAFLOW_SIDE_EFFECTING),
    )(*[pltpu.with_memory_space_constraint(a, pltpu.HBM) for a in list(srcs) + list(lands)])
    sems = [(outs[2 * gi], outs[2 * gi + 1]) for gi in range(ng)]
    return sems, outs[2 * ng:2 * ng + n], outs[2 * ng + n:2 * ng + 2 * n], outs[-1]


def _behind(v, token):
    if token is None:
        return v
    return v + token[0, 0].astype(v.dtype)


def _split_wait(srcs, lands, sems, after, *, scatter, name):
    m = len(srcs)

    def body(*refs):
        src_refs, land_refs = refs[:m], refs[m:2 * m]
        send_sems, recv_sems = refs[2 * m], refs[2 * m + 1]
        for send, recv in _split_copies(src_refs, land_refs, send_sems, recv_sems, scatter):
            send.wait_send()
            recv.wait_recv()

    outs = pl.pallas_call(
        body, name=name,
        out_shape=tuple(pltpu.HBM(a.shape, a.dtype) for a in list(srcs) + list(lands)),
        in_specs=[HBM_SPEC] * (2 * m) + [SEM_SPEC, SEM_SPEC, pl.BlockSpec(memory_space=pl.ANY)],
        out_specs=tuple([HBM_SPEC] * (2 * m)),
        input_output_aliases={i: i for i in range(2 * m)},
        compiler_params=pltpu.CompilerParams(has_side_effects=pltpu.SideEffectType.DATAFLOW_SIDE_EFFECTING),
    )(*srcs, *lands, sems[0], sems[1], after)
    return outs[m:]


TL_FIRST = (1, 2, 4, 6)
TL_ICI = (2, 4, 6)
EFFECT = pltpu.SideEffectType.DATAFLOW_SIDE_EFFECTING


def _tl_first(src_refs, land_refs, send_sems, recv_sems):
    x, y, c = lax.axis_index("x"), lax.axis_index("y"), lax.axis_index("c")
    me = 4 * x + 2 * y + c
    out = []
    for j, (src, land) in enumerate(zip(src_refs, land_refs)):
        for i, d in enumerate(TL_FIRST):
            peer, pidx = _peer(x, y, c, d)
            k = len(TL_FIRST) * j + i
            send = pltpu.make_async_remote_copy(src_ref=src, dst_ref=land.at[me], send_sem=send_sems.at[k],
                                                recv_sem=recv_sems.at[k], device_id=peer, device_id_type=MESH)
            recv = pltpu.make_async_remote_copy(src_ref=src, dst_ref=land.at[pidx], send_sem=send_sems.at[k],
                                                recv_sem=recv_sems.at[k], device_id=peer, device_id_type=MESH)
            out.append((d, send, recv))
    return out


def _tl_second(land_refs, send_sems, recv_sems):
    x, y, c = lax.axis_index("x"), lax.axis_index("y"), lax.axis_index("c")
    sibling, _ = _peer(x, y, c, 1)
    out = []
    for j, land in enumerate(land_refs):
        for i, d in enumerate(TL_ICI):
            _, mine = _peer(x, y, c, d)
            _, theirs = _peer(x, y, c, d + 1)
            k = len(TL_ICI) * j + i
            send = pltpu.make_async_remote_copy(src_ref=land.at[mine], dst_ref=land.at[mine], send_sem=send_sems.at[k],
                                                recv_sem=recv_sems.at[k], device_id=sibling, device_id_type=MESH)
            recv = pltpu.make_async_remote_copy(src_ref=land.at[mine], dst_ref=land.at[theirs],
                                                send_sem=send_sems.at[k], recv_sem=recv_sems.at[k],
                                                device_id=sibling, device_id_type=MESH)
            out.append((send, recv))
    return out


def _tl_start(srcs, lands, groups, *, name):
    n, ng = len(srcs), len(groups)

    def body(*refs):
        src_refs, land_refs = refs[:n], refs[n:2 * n]
        sems = refs[2 * n:2 * n + 2 * ng]
        for gi, g in enumerate(groups):
            for _, send, _ in _tl_first([src_refs[t] for t in g], [land_refs[t] for t in g], sems[2 * gi],
                                        sems[2 * gi + 1]):
                send.start()
        refs[-1][...] = jnp.zeros_like(refs[-1])

    sem_shapes = []
    for g in groups:
        sem_shapes += [pltpu.SemaphoreType.DMA((len(g) * len(TL_FIRST),))] * 2
    thru = [pltpu.HBM(a.shape, a.dtype) for a in list(srcs) + list(lands)]
    outs = pl.pallas_call(
        body, name=name,
        out_shape=tuple(sem_shapes + thru + [jax.ShapeDtypeStruct((8, 128), F32)]),
        in_specs=[HBM_SPEC] * (2 * n),
        out_specs=tuple([SEM_SPEC] * (2 * ng) + [HBM_SPEC] * (2 * n) + [pl.BlockSpec(memory_space=pltpu.VMEM)]),
        input_output_aliases={i: 2 * ng + i for i in range(2 * n)},
        compiler_params=pltpu.CompilerParams(has_side_effects=EFFECT),
    )(*[pltpu.with_memory_space_constraint(a, pltpu.HBM) for a in list(srcs) + list(lands)])
    sems = [(outs[2 * gi], outs[2 * gi + 1]) for gi in range(ng)]
    return sems, outs[2 * ng:2 * ng + n], outs[2 * ng + n:2 * ng + 2 * n], outs[-1]


def _tl_forward(srcs, lands, sems1, after, *, name):
    m = len(srcs)

    def body(*refs):
        src_refs, land_refs = refs[:m], refs[m:2 * m]
        send1, recv1 = refs[2 * m], refs[2 * m + 1]
        send2, recv2 = refs[2 * m + 3], refs[2 * m + 4]
        for d, _, recv in _tl_first(src_refs, land_refs, send1, recv1):
            if d in TL_ICI:
                recv.wait_recv()
        for send, _ in _tl_second(land_refs, send2, recv2):
            send.start()

    sem = pltpu.SemaphoreType.DMA((m * len(TL_ICI),))
    outs = pl.pallas_call(
        body, name=name,
        out_shape=tuple([sem, sem] + [pltpu.HBM(a.shape, a.dtype) for a in list(srcs) + list(lands)]),
        in_specs=[HBM_SPEC] * (2 * m) + [SEM_SPEC, SEM_SPEC, pl.BlockSpec(memory_space=pl.ANY)],
        out_specs=tuple([SEM_SPEC, SEM_SPEC] + [HBM_SPEC] * (2 * m)),
        input_output_aliases={i: 2 + i for i in range(2 * m)},
        compiler_params=pltpu.CompilerParams(has_side_effects=EFFECT),
    )(*srcs, *lands, sems1[0], sems1[1], after)
    return (outs[0], outs[1]), outs[2:2 + m], outs[2 + m:2 + 2 * m]


def _tl_wait(srcs, lands, sems1, sems2, after, *, name):
    m = len(srcs)

    def body(*refs):
        src_refs, land_refs = refs[:m], refs[m:2 * m]
        send1, recv1, send2, recv2 = refs[2 * m:2 * m + 4]
        for d, send, recv in _tl_first(src_refs, land_refs, send1, recv1):
            send.wait_send()
            if d not in TL_ICI:
                recv.wait_recv()
        for send, recv in _tl_second(land_refs, send2, recv2):
            send.wait_send()
            recv.wait_recv()

    outs = pl.pallas_call(
        body, name=name,
        out_shape=tuple(pltpu.HBM(a.shape, a.dtype) for a in list(srcs) + list(lands)),
        in_specs=[HBM_SPEC] * (2 * m) + [SEM_SPEC] * 4 + [pl.BlockSpec(memory_space=pl.ANY)],
        out_specs=tuple([HBM_SPEC] * (2 * m)),
        input_output_aliases={i: i for i in range(2 * m)},
        compiler_params=pltpu.CompilerParams(has_side_effects=EFFECT),
    )(*srcs, *lands, sems1[0], sems1[1], sems2[0], sems2[1], after)
    return outs[m:]


TM_PROJ = 512
TN_PROJ = 512
TM_ROW = 512
TM_NN = 512
TK_TN = 2048
TM_ADAM = 208
TN_FFN = F // 2
TN_IN = NIN // 4


def _tn(a, b, name, tn, token=None):
    if a.ndim == 2:
        a = a[None]
    return _tn_matmul(a, b, token, tn=tn, tk=TK_TN, name=name)


def _local_step(x, tgt, mods, g1, gm, g2, gf, convw8, sinks, w_get, g_put, tables=None):
    T = x.shape[0]
    sh1, sc1, gt1, sh2, sc2, gt2, sh3, sc3, gt3 = [mods[i:i + 1] for i in range(N_MOD)]
    cos, sin = _rope_tables(T) if tables is None else tables
    behind = _behind

    w = dict(w_get("gu1", mods))
    h1, ab1 = _norm_proj(x, g1, sc1, sh1, w["gu1"], tm=TM_PROJ, tn=TN_PROJ, name="ffn1_up")
    w.update(w_get("d1", ab1))
    x1, y1 = _ffn_down_fwd(ab1, w["d1"], x, gt1, tm=TM_ROW, name="ffn1_down")
    w.update(w_get("mix", x1))
    h2, proj, qs, kr = _norm_proj(x1, gm, sc2, sh2, w["win"], (cos, sin), tm=TM_PROJ, tn=TN_PROJ, name="mix_in")
    bias = _attn_bias()
    attn, lse = _attn_fwd(qs, kr, proj, bias, sinks, name="attn_fwd")
    x2, gc, yc, ya, mg, o = _mixer_mid_fwd(proj, attn, w["cp"], w["ap"], w["out"], convw8, x1, gt2,
                                           tm=TM_ROW, name="mix_mid")
    w.update(w_get("ffn2", x2))
    h3, ab2, y2, dx3, lsum, dgf = _ffn_fwd(x2, g2, sc3, sh3, gt3, w["gu2"], w["d2"], (tgt, gf), tm=TM_ROW,
                                           name="ffn2_final")

    dab2, dgt3, g_d2 = _ffn_down_bwd_dw(dx3, y2, gt3, ab2, w["d2"], tm=TM_ROW, name="ffn2_down_bwd")
    dx2, dsh3, dsc3, dg2 = _nn_bwd_norm(dab2, w["gu2"], x2, g2, sc3, dx3, tm=TM_NN, name="ffn2_up_bwd")
    g_gu2 = _tn(dab2, h3, "ffn2_up_dw", TN_FFN)
    tok = g_put(dict(gu2=g_gu2, d2=g_d2))

    dout, dyc, dya, dgc, dat, dproj, dgt2 = _mixer_mid_bwd(dx2, behind(gt2, tok), o, proj, yc, ya, w["out"], w["cp"],
                                                           w["ap"], tm=TM_ROW, name="mix_mid_bwd")
    g_out = _tn(mg, dout, "mix_out_dw", D)
    g_cp = _tn(gc, dyc, "mix_cp_dw", D)
    g_ap = _tn(attn, dya, "mix_ap_dw", D)
    dproj, dkc, dkp, dvc, dvp, dsink = _attn_bwd(qs, kr, proj, bias, sinks, lse, attn, dat, cos, sin, dproj,
                                                 name="attn_bwd")
    dproj = _dkv_combine(dkc, dkp, dvc, dvp, dproj, name="attn_dkv")
    dproj, dcw = _conv_bwd(dgc, proj, convw8, dproj, tm=TM_ROW, name="conv_bwd")
    g_in = _tn(dproj, h2, "mix_in_dw", TN_IN)
    tok = g_put(dict(win=g_in, cp=g_cp, ap=g_ap, out=g_out))
    dx1, dsh2, dsc2, dgm = _nn_bwd_norm(dproj[None], w["win"], x1, gm, behind(sc2, tok), dx2, tm=TM_NN,
                                        name="mix_in_bwd")

    dab1, dgt1, g_d1 = _ffn_down_bwd_dw(dx1, y1, gt1, ab1, w["d1"], tm=TM_ROW, name="ffn1_down_bwd")
    tok = g_put(dict(d1=g_d1))
    g_gu1 = _tn(dab1, h1, "ffn1_up_dw", TN_FFN, tok)
    tok = g_put(dict(gu1=g_gu1))
    dx0, dsh1, dsc1, dg1 = _nn_bwd_norm(dab1, w["gu1"], x, g1, behind(sc1, tok), dx1, tm=TM_NN,
                                        name="ffn1_up_bwd")

    small = dict(mods=jnp.concatenate([dsh1, dsc1, dgt1, dsh2, dsc2, dgt2, dsh3, dsc3, dgt3], axis=0),
                 g1=dg1, gm=dgm, g2=dg2, gf=dgf, convw=dcw[0:3], sinks=dsink[:, 0:N_HEADS])
    return lsum, dx0, small


BIG = ("gu1", "d1", "win", "cp", "ap", "out", "gu2", "d2")
TRANSPOSED = ("gu1", "win", "gu2")
SMALL_ROWS = 24
R_MODS, R_G1, R_GM, R_G2, R_GF, R_CONV, R_SINK, R_LOSS = 0, 9, 10, 11, 12, 13, 16, 17


def _pad_to(a, rows, cols):
    return jnp.pad(a, ((0, rows - a.shape[0]), (0, cols - a.shape[1])))


def _pack_small(b_ada, g1, gm, g2, gf, conv, sinks, lsum):
    rows = [b_ada.reshape(N_MOD, D), g1.reshape(1, D), gm.reshape(1, D), g2.reshape(1, D), gf.reshape(1, D),
            _pad_to(conv.reshape(3, -1), 3, D), _pad_to(sinks.reshape(1, N_HEADS), 1, D), lsum.reshape(1, D)]
    return _pad_to(jnp.concatenate(rows, axis=0), SMALL_ROWS, D)


def kernel(x, c, w_ada, b_ada, g_ffn1, w1_gu, w1_down, g_mix, w_in, conv_w, w_conv_proj, w_attn_proj, sinks, w_out, g_ffn2, w2_gu, w2_down, g_final, loss_target, m_w_ada, m_b_ada, m_g_ffn1, m_w1_gu, m_w1_down, m_g_mix, m_w_in, m_conv_w, m_w_conv_proj, m_w_attn_proj, m_sinks, m_w_out, m_g_ffn2, m_w2_gu, m_w2_down, m_g_final, v_w_ada, v_b_ada, v_g_ffn1, v_w1_gu, v_w1_down, v_g_mix, v_w_in, v_conv_w, v_w_conv_proj, v_w_attn_proj, v_sinks, v_w_out, v_g_ffn2, v_w2_gu, v_w2_down, v_g_final):
    me = 4 * lax.axis_index("x") + 2 * lax.axis_index("y") + lax.axis_index("c")
    ada_cols = w_ada.shape[2]
    conv_cols = conv_w.shape[2]

    native = dict(gu1=w1_gu[0], d1=w1_down[0], win=w_in[0], cp=w_conv_proj[0], ap=w_attn_proj[0], out=w_out[0],
                  gu2=w2_gu[0], d2=w2_down[0])

    def shard(n, token):
        a = _behind(native[n], token)
        return (a.T if n in TRANSPOSED else a).astype(BF)

    c_all, conv_all, _ = _exchange([c, _pad_to(conv_w[0], 8, conv_cols)], scatter=False, name="gather_cond")
    c_all = c_all.reshape(N_DEV, D)
    conv_full = conv_all[:, 0:3, :].transpose(1, 0, 2).reshape(3, D)

    b_cols = lax.dynamic_slice(b_ada, (0, me * ada_cols), (1, ada_cols))
    mods_cols = _mods_part(c_all, w_ada[0], b_cols, name="ada_mods")
    mods_all, mods_token = _exchange([mods_cols], scatter=False, name="gather_mods")
    mods = lax.dynamic_index_in_dim(mods_all, me, axis=1, keepdims=False).reshape(N_MOD, D)

    groups = dict(gu1=("gu1",), d1=("d1",), mix=("win", "cp", "ap", "out"), ffn2=("gu2", "d2"))
    in_flight = {}
    first = [shard("gu1", mods_token)]
    sems, srcs, lands, token = _tl_start(first, [_own_slot(s, me) for s in first], [[0]],
                                         name="gather_weights_start_gu1")
    in_flight["gu1"] = [sems[0], srcs, lands, None]
    rest = [n for n in BIG if n != "gu1"]
    shards = [shard(n, token) for n in rest]
    rest_groups = [[rest.index(n) for n in names] for g, names in groups.items() if g != "gu1"]
    sems, srcs, lands, rest_token = _tl_start(shards, [_own_slot(s, me) for s in shards], rest_groups,
                                              name="gather_weights_start_rest")
    for (g, names), gsems, idx in zip([kv for kv in groups.items() if kv[0] != "gu1"], sems, rest_groups):
        in_flight[g] = [gsems, [srcs[t] for t in idx], [lands[t] for t in idx], None]

    def forward(group, after):
        sems1, gsrcs, glands, _ = in_flight[group]
        sems2, gsrcs, glands = _tl_forward(gsrcs, glands, sems1, after, name="gather_weights_forward_" + group)
        in_flight[group] = [sems1, gsrcs, glands, sems2]

    forward_early = dict(d1="mix", mix="ffn2")

    tables = _rope_tables(x.shape[1], rest_token)

    def w_get(group, after):
        if group == "gu1":
            after = tables[0]
        if in_flight[group][3] is None:
            forward(group, after)
        sems1, gsrcs, glands, sems2 = in_flight[group]
        landed = _tl_wait(gsrcs, glands, sems1, sems2, after, name="gather_weights_wait_" + group)
        if group in forward_early:
            forward(forward_early[group], landed[0])
        return {n: a.reshape(-1, D) for n, a in zip(groups[group], landed)}

    pending = []

    def g_put(gs):
        names = tuple(gs)
        srcs = [gs[n].reshape(N_DEV, -1, D) for n in names]
        lands = [_own_slot(lax.dynamic_index_in_dim(s, me, axis=0, keepdims=False), me) for s in srcs]
        sems, srcs, lands, tok = _split_start(srcs, lands, [list(range(len(names)))], scatter=True,
                                              name="scatter_grads_start_" + names[0])
        pending.append((names, sems[0], srcs, lands))
        return tok

    lsum, grad_x, small = _local_step(x[0], loss_target[0], mods, g_ffn1, g_mix, g_ffn2, g_final[None],
                                      _pad_to(conv_full, 8, D), sinks[0], w_get, g_put, tables)

    packed = _pack_small(small["mods"], small["g1"], small["gm"], small["g2"], small["gf"], small["convw"],
                         small["sinks"], lsum)
    sm_sems, sm_srcs, sm_lands, sm_token = _split_start([packed], [_own_slot(packed, me)], [[0]], scatter=False,
                                                        name="gather_small_start")

    w_of = dict(ada=w_ada, gu1=w1_gu, d1=w1_down, win=w_in, cp=w_conv_proj, ap=w_attn_proj, out=w_out, gu2=w2_gu,
                d2=w2_down)
    m_of = dict(ada=m_w_ada, gu1=m_w1_gu, d1=m_w1_down, win=m_w_in, cp=m_w_conv_proj, ap=m_w_attn_proj, out=m_w_out,
                gu2=m_w2_gu, d2=m_w2_down)
    v_of = dict(ada=v_w_ada, gu1=v_w1_gu, d1=v_w1_down, win=v_w_in, cp=v_w_conv_proj, ap=v_w_attn_proj, out=v_w_out,
                gu2=v_w2_gu, d2=v_w2_down)
    upd = {}
    after = sm_token
    for k, (names, sems, srcs, lands) in enumerate(pending):
        if k == 2:
            (packed_all,) = _split_wait(sm_srcs, sm_lands, sm_sems[0], after, scatter=False, name="gather_small_wait")
            gsmall = _sum8(packed_all, name="sum_small")
            loss = (0.5 / D) * jnp.sum(gsmall[R_LOSS])
            after = gsmall
        parts = _split_wait(srcs, lands, sems, after, scatter=True, name="scatter_grads_wait_" + names[0])
        for n, p in zip(names, parts):
            if n in TRANSPOSED:
                res = _adam(jnp.swapaxes(w_of[n], 1, 2), p, jnp.swapaxes(m_of[n], 1, 2), jnp.swapaxes(v_of[n], 1, 2),
                            tm=TM_ADAM, name="adam_" + n)
                upd[n] = [jnp.swapaxes(t, 1, 2) for t in res]
            else:
                upd[n] = _adam(w_of[n], p, m_of[n], v_of[n], tm=TM_ADAM, name="adam_" + n)
        after = upd[names[-1]][1]

    gm_cols = lax.dynamic_slice(packed_all[:, R_MODS:R_MODS + N_MOD, :].reshape(N_DEV, N_MOD * D),
                                (0, me * ada_cols), (N_DEV, ada_cols))
    upd["ada"] = _adam(w_ada, _wada_grad(c_all.T, gm_cols, name="ada_dw"), m_w_ada, v_w_ada, tm=256, name="adam_ada")
    conv_g = lax.dynamic_slice(gsmall, (R_CONV, me * conv_cols), (3, conv_cols))

    def natural(b, g1, gm, g2, gf, cw, sk):
        return dict(b_ada=b, g_ffn1=g1, g_mix=gm, g_ffn2=g2, g_final=gf[None], conv_w=cw[0], sinks=sk)

    small_out = _adam_small(gsmall, conv_g, natural(b_ada, g_ffn1, g_mix, g_ffn2, g_final, conv_w, sinks),
                            natural(m_b_ada, m_g_ffn1, m_g_mix, m_g_ffn2, m_g_final, m_conv_w, m_sinks),
                            natural(v_b_ada, v_g_ffn1, v_g_mix, v_g_ffn2, v_g_final, v_conv_w, v_sinks),
                            name="adam_small")
    for res in small_out:
        res["g_final"] = res["g_final"][0]
        res["conv_w"] = res["conv_w"][None]

    big_name = dict(w_ada="ada", w1_gu="gu1", w1_down="d1", w_in="win", w_conv_proj="cp", w_attn_proj="ap",
                    w_out="out", w2_gu="gu2", w2_down="d2")
    order = ("w_ada", "b_ada", "g_ffn1", "w1_gu", "w1_down", "g_mix", "w_in", "conv_w", "w_conv_proj", "w_attn_proj",
             "sinks", "w_out", "g_ffn2", "w2_gu", "w2_down", "g_final")
    outs = [loss, grad_x[None]]
    for kind in range(4):
        for n in order:
            outs.append(upd[big_name[n]][kind] if n in big_name else small_out[kind][n])
    return tuple(outs)
```

```python
import jax
import jax.numpy as jnp
from jax import lax
from jax.experimental import pallas as pl
from jax.experimental.pallas import tpu as pltpu

D = 1024
F = 2816
NIN = 6656
N_HEADS = 16
N_KV = 4
HEAD_DIM = 64
BLK = 128
N_MOD = 9
N_DEV = 8
EPS = 1e-6
NEG_INF = -1e30
ROPE_THETA = 10000.0
O_BG, O_CG, O_U, O_Q, O_K, O_V, O_ZC, O_ZA = 0, 1024, 2048, 3072, 4096, 4352, 4608, 5632

ADAM_LR = 0.001
ADAM_B1 = 0.9
ADAM_B2 = 0.999
ADAM_EPS = 1e-08
ADAM_WD = 0.01
ADAM_STEP = 10

BF = jnp.bfloat16
F32 = jnp.float32
VMEM_LIMIT = 56 * 1024 * 1024
MXU_N = 256
MESH = pl.DeviceIdType.MESH

NT = (((1,), (1,)), ((), ()))
TN = (((0,), (0,)), ((), ()))


def _cp(sem=None):
    return pltpu.CompilerParams(dimension_semantics=sem, vmem_limit_bytes=VMEM_LIMIT)


def _tile(n, pref):
    if n <= pref:
        return n
    for t in range(pref - pref % 16, 15, -16):
        if n % t == 0:
            return t
    raise ValueError((n, pref))


def _sigmoid(v):
    return 0.5 * jnp.tanh(0.5 * v) + 0.5


def _row(i):
    return (i, 0)


def _const2(*_):
    return (0, 0)


def _resident(shape):
    return pl.BlockSpec(shape, lambda *_: (0,) * len(shape), pipeline_mode=pl.Buffered(1))


def _norm_proj(x, g, sc, sh, wt, rope=None, *, tm, tn, name):
    T, N = x.shape[0], wt.shape[0]
    tm = _tile(T, tm)

    def body(x_ref, g_ref, sc_ref, sh_ref, w_ref, *rest):
        if rope is None:
            h_ref, o_ref = rest
        else:
            c_ref, s_ref, h_ref, o_ref, qs_ref, kr_ref = rest
        xv = x_ref[...]
        r = lax.rsqrt(jnp.mean(xv * xv, axis=-1, keepdims=True) + EPS)
        hb = ((xv * r) * g_ref[...] * (1.0 + sc_ref[...]) + sh_ref[...]).astype(BF)
        h_ref[...] = hb
        for c0 in range(0, N, tn):
            cols = pl.ds(c0, tn)
            o_ref[:, cols] = lax.dot_general(hb, w_ref[cols, :], NT, preferred_element_type=F32).astype(BF)
            if rope is not None and c0 < O_V <= c0 + tn:
                _attn_prep_tile(o_ref, c_ref, s_ref, qs_ref, kr_ref, tm)

    vec = pl.BlockSpec((1, D), _const2)
    rowspec = pl.BlockSpec((tm, D), _row)
    in_specs = [rowspec, vec, vec, vec, _resident((N, D))]
    out_specs = [rowspec, pl.BlockSpec((tm, N), _row)]
    out_shape = [jax.ShapeDtypeStruct((T, D), BF), jax.ShapeDtypeStruct((T, N), BF)]
    args = [x, g, sc, sh, wt]
    if rope is not None:
        in_specs += [pl.BlockSpec((tm, 128), _row)] * 2
        out_specs += [pl.BlockSpec((N_KV, 4 * tm, 128), lambda i: (0, i, 0)), pl.BlockSpec((tm, 256), _row)]
        out_shape += [jax.ShapeDtypeStruct((N_KV, 4 * T, 128), BF), jax.ShapeDtypeStruct((T, 256), BF)]
        args += list(rope)
    return pl.pallas_call(
        body, name=name, grid=(T // tm,),
        in_specs=in_specs, out_specs=out_specs, out_shape=out_shape,
        compiler_params=_cp(("parallel",)),
    )(*args)


def _ffn_down_fwd(ab, wd, x, gt, *, tm, name):
    T = x.shape[0]
    tm = _tile(T, tm)

    def body(a_ref, b_ref, wd_ref, x_ref, gt_ref, xo_ref):
        y = None
        for c0 in range(0, F, MXU_N):
            cols = pl.ds(c0, MXU_N)
            a = a_ref[:, cols].astype(F32)
            act = (a * _sigmoid(a) * b_ref[:, cols].astype(F32)).astype(BF)
            part = jnp.dot(act, wd_ref[cols, :], preferred_element_type=F32)
            y = part if y is None else y + part
        xo_ref[...] = x_ref[...] + (0.5 * gt_ref[...]) * y

    return pl.pallas_call(
        body, name=name, grid=(T // tm,),
        in_specs=[pl.BlockSpec((tm, F), lambda i: (i, 0)), pl.BlockSpec((tm, F), lambda i: (i, 1)),
                  _resident((F, D)), pl.BlockSpec((tm, D), _row), pl.BlockSpec((1, D), _const2)],
        out_specs=pl.BlockSpec((tm, D), _row),
        out_shape=jax.ShapeDtypeStruct((T, D), F32),
        compiler_params=_cp(("parallel",)),
    )(ab, ab, wd, x, gt)


def _ffn_fwd(x, g, sc, sh, gt, wgu, wd, final, *, tm, name):
    T = x.shape[0]
    tm = _tile(T, tm)
    last = final is not None

    def body(x_ref, g_ref, sc_ref, sh_ref, gt_ref, wgu_ref, wd_ref, *rest):
        if last:
            t_ref, gf_ref, h_ref, ab_ref, dx_ref, ls_ref, dgf_ref = rest
        else:
            h_ref, ab_ref, xo_ref = rest
        xv = x_ref[...]
        r = lax.rsqrt(jnp.mean(xv * xv, axis=-1, keepdims=True) + EPS)
        hb = ((xv * r) * g_ref[...] * (1.0 + sc_ref[...]) + sh_ref[...]).astype(BF)
        h_ref[...] = hb
        y = None
        for c0 in range(0, F, MXU_N):
            a = lax.dot_general(hb, wgu_ref[pl.ds(c0, MXU_N), :], NT, preferred_element_type=F32)
            b = lax.dot_general(hb, wgu_ref[pl.ds(F + c0, MXU_N), :], NT, preferred_element_type=F32)
            ab = a.astype(BF)
            bb = b.astype(BF)
            ab_ref[:, pl.ds(c0, MXU_N)] = ab
            ab_ref[:, pl.ds(F + c0, MXU_N)] = bb
            a = ab.astype(F32)
            act = (a * _sigmoid(a) * bb.astype(F32)).astype(BF)
            part = jnp.dot(act, wd_ref[pl.ds(c0, MXU_N), :], preferred_element_type=F32)
            y = part if y is None else y + part
        xo = xv + (0.5 * gt_ref[...]) * y
        if not last:
            xo_ref[...] = xo
            return

        @pl.when(pl.program_id(0) == 0)
        def _():
            ls_ref[...] = jnp.zeros_like(ls_ref)
            dgf_ref[...] = jnp.zeros_like(dgf_ref)
        gv = gf_ref[...]
        r = lax.rsqrt(jnp.mean(xo * xo, axis=-1, keepdims=True) + EPS)
        xh = xo * r
        e = xh * gv - t_ref[...]
        ls_ref[...] += jnp.sum(e * e, axis=0, keepdims=True)
        dy = e * (1.0 / D)
        dgf_ref[...] += jnp.sum(dy * xh, axis=0, keepdims=True)
        dxh = dy * gv
        dx_ref[...] = r * (dxh - xh * jnp.mean(dxh * xh, axis=-1, keepdims=True))

    vec = pl.BlockSpec((1, D), _const2)
    rowspec = pl.BlockSpec((tm, D), _row)
    in_specs = [rowspec, vec, vec, vec, vec, _resident((2 * F, D)), _resident((F, D))]
    out_specs = [rowspec, pl.BlockSpec((tm, 2 * F), _row), rowspec]
    out_shape = [jax.ShapeDtypeStruct((T, D), BF), jax.ShapeDtypeStruct((T, 2 * F), BF),
                 jax.ShapeDtypeStruct((T, D), F32)]
    args = [x, g, sc, sh, gt, wgu, wd]
    if last:
        in_specs += [rowspec, vec]
        out_specs += [vec, vec]
        out_shape += [jax.ShapeDtypeStruct((1, D), F32)] * 2
        args += list(final)
    return pl.pallas_call(
        body, name=name, grid=(T // tm,),
        in_specs=in_specs, out_specs=out_specs, out_shape=out_shape,
        compiler_params=_cp(("arbitrary",) if last else ("parallel",)),
    )(*args)


def _ffn_down_bwd_dw(dxo, gt, ab, wd, *, tm, name):
    T = dxo.shape[0]
    tm = _tile(T, tm)
    nt = T // tm
    hw = F // 2
    chunks = [(c0, min(MXU_N, hw - c0)) for c0 in range(0, hw, MXU_N)]

    def body(dxo_ref, gt_ref, a_ref, b_ref, wd_ref, dab_ref, dgt_ref, dwd_ref, dys, dyt, acc, stage, sem):
        i, j = pl.program_id(0), pl.program_id(1)

        @pl.when(jnp.logical_and(i == 0, j == 0))
        def _():
            dgt_ref[...] = jnp.zeros_like(dgt_ref)

        @pl.when(j == 0)
        def _():
            dxv = dxo_ref[...]
            dys[...] = ((0.5 * gt_ref[...]) * dxv).astype(BF)
            dyt[...] = dxv.T.astype(BF)

        def half(jj):
            @pl.when(i == 0)
            def _():
                acc[jj] = jnp.zeros((D, hw), F32)

            dy = dys[...]
            dy_t = dyt[...]
            for c0, cw in chunks:
                cols = pl.ds(c0, cw)
                dact = lax.dot_general(dy, wd_ref[pl.ds(jj * hw + c0, cw), :], NT, preferred_element_type=F32)
                a = a_ref[:, cols].astype(F32)
                b = b_ref[:, cols].astype(F32)
                s = _sigmoid(a)
                silu = a * s
                dab_ref[0, :, cols] = (dact * b * (s * (1.0 + a * (1.0 - s)))).astype(BF)
                dab_ref[1, :, cols] = (dact * silu).astype(BF)
                acc[jj, :, cols] += jnp.dot(dy_t, (silu * b).astype(BF), preferred_element_type=F32)

            @pl.when(i == nt - 1)
            def _():
                half_gt = 0.5 * gt_ref[...]
                for c0, cw in chunks:
                    g_rows = acc[jj, :, pl.ds(c0, cw)].T
                    w_rows = wd_ref[pl.ds(jj * hw + c0, cw), :].astype(F32)
                    dgt_ref[...] += 0.5 * jnp.sum(g_rows * w_rows, axis=0, keepdims=True)
                    stage[0:cw, :] = (g_rows * half_gt).astype(BF)
                    out = pltpu.make_async_copy(stage.at[pl.ds(0, cw)], dwd_ref.at[pl.ds(jj * hw + c0, cw)], sem)
                    out.start()
                    out.wait()

        for jj in range(2):
            pl.when(j == jj)(lambda jj=jj: half(jj))

    vec = pl.BlockSpec((1, D), _const2)
    rowspec = pl.BlockSpec((tm, D), lambda i, j: (i, 0))
    return pl.pallas_call(
        body, name=name, grid=(nt, 2),
        in_specs=[rowspec, vec, pl.BlockSpec((tm, hw), lambda i, j: (i, j)),
                  pl.BlockSpec((tm, hw), lambda i, j: (i, j + 2)), _resident((F, D))],
        out_specs=[pl.BlockSpec((2, tm, hw), lambda i, j: (0, i, j)), vec, pl.BlockSpec(memory_space=pl.ANY)],
        out_shape=[jax.ShapeDtypeStruct((2, T, F), BF), jax.ShapeDtypeStruct((1, D), F32),
                   jax.ShapeDtypeStruct((F, D), BF)],
        scratch_shapes=[pltpu.VMEM((tm, D), BF), pltpu.VMEM((D, tm), BF), pltpu.VMEM((2, D, hw), F32),
                        pltpu.VMEM((MXU_N, D), BF), pltpu.SemaphoreType.DMA(())],
        compiler_params=_cp(("arbitrary", "arbitrary")),
    )(dxo, gt, ab, ab, wd)


def _tn_matmul(a, b, token=None, *, tn, tk, name):
    S, T, Ns = a.shape
    tn, tk = _tile(Ns, tn), _tile(T, tk)
    nk, njs = T // tk, Ns // tn
    deps = [] if token is None else [token]

    def body(a_ref, b_ref, *rest):
        o_ref, acc = rest[len(deps):]
        k = pl.program_id(1)

        @pl.when(k == 0)
        def _():
            acc[...] = jnp.zeros_like(acc)
        acc[...] += lax.dot_general(a_ref[0], b_ref[...], TN, preferred_element_type=F32)

        @pl.when(k == nk - 1)
        def _():
            o_ref[...] = acc[...].astype(BF)

    return pl.pallas_call(
        body, name=name, grid=(S * njs, nk),
        in_specs=[pl.BlockSpec((1, tk, tn), lambda j, k: (j // njs, k, j % njs)),
                  pl.BlockSpec((tk, D), lambda j, k: (k, 0))] + [pl.BlockSpec(memory_space=pl.ANY)] * len(deps),
        out_specs=pl.BlockSpec((tn, D), lambda j, k: (j, 0)),
        out_shape=jax.ShapeDtypeStruct((S * Ns, D), BF),
        scratch_shapes=[pltpu.VMEM((tn, D), F32)],
        compiler_params=_cp(("parallel", "arbitrary")),
    )(a, b, *deps)


def _nn_bwd_norm(da, w, x, g, sc, dxo, *, tm, name):
    S, T, Ks = da.shape
    tm = _tile(T, tm)
    rc = _tile(tm, 256)

    def body(da_ref, w_ref, x_ref, g_ref, sc_ref, dxo_ref, dx_ref, dsh_ref, dsc_ref, dg_ref, acc):
        @pl.when(pl.program_id(0) == 0)
        def _():
            dsh_ref[...] = jnp.zeros_like(dsh_ref)
            dsc_ref[...] = jnp.zeros_like(dsc_ref)
            dg_ref[...] = jnp.zeros_like(dg_ref)

        d = jnp.dot(da_ref[0], w_ref[0:Ks, :], preferred_element_type=F32)
        for s in range(1, S):
            d = d + jnp.dot(da_ref[s], w_ref[s * Ks:(s + 1) * Ks, :], preferred_element_type=F32)
        acc[...] = d
        gv = g_ref[...]
        sc1 = 1.0 + sc_ref[...]
        dsh = jnp.zeros((1, D), F32)
        dsc = jnp.zeros((1, D), F32)
        dg = jnp.zeros((1, D), F32)
        for r0 in range(0, tm, rc):
            rows = pl.ds(r0, rc)
            u = acc[rows, :]
            xv = x_ref[rows, :]
            r = lax.rsqrt(jnp.mean(xv * xv, axis=-1, keepdims=True) + EPS)
            xh = xv * r
            dsh = dsh + jnp.sum(u, axis=0, keepdims=True)
            dsc = dsc + jnp.sum(u * (xh * gv), axis=0, keepdims=True)
            us = u * sc1
            dg = dg + jnp.sum(us * xh, axis=0, keepdims=True)
            dxh = us * gv
            dx_ref[rows, :] = dxo_ref[rows, :] + r * (dxh - xh * jnp.mean(dxh * xh, axis=-1, keepdims=True))
        dsh_ref[...] += dsh
        dsc_ref[...] += dsc
        dg_ref[...] += dg

    vec = pl.BlockSpec((1, D), _const2)
    rowspec = pl.BlockSpec((tm, D), _row)
    return pl.pallas_call(
        body, name=name, grid=(T // tm,),
        in_specs=[pl.BlockSpec((S, tm, Ks), lambda i: (0, i, 0)), _resident((S * Ks, D)), rowspec, vec, vec, rowspec],
        out_specs=[rowspec, vec, vec, vec],
        out_shape=[jax.ShapeDtypeStruct((T, D), F32)] + [jax.ShapeDtypeStruct((1, D), F32)] * 3,
        scratch_shapes=[pltpu.VMEM((tm, D), F32)],
        compiler_params=_cp(("arbitrary",)),
    )(da, w, x, g, sc, dxo)


def _rope(t, cos, sin_signed, lt32, inverse=False):
    sel = jnp.where(lt32, pltpu.roll(t, 96, 1), pltpu.roll(t, 32, 1))
    return t * cos - sel * sin_signed if inverse else t * cos + sel * sin_signed


def _rope_tables(T, token=None):
    inv = 1.0 / (ROPE_THETA ** (jnp.arange(0, HEAD_DIM, 2, dtype=F32) / HEAD_DIM))
    ang = _behind(jnp.arange(T, dtype=F32)[:, None] * inv[None, :], token)
    cos, sin = jnp.cos(ang), jnp.sin(ang)
    cos128 = jnp.tile(cos, (1, 4))
    sin128 = jnp.tile(jnp.concatenate([-sin, sin], axis=1), (1, 2))
    return cos128, sin128


QSCALE = HEAD_DIM ** -0.5


def _lane_masks(rows):
    lane = lax.broadcasted_iota(jnp.int32, (rows, 128), 1)
    return (lane % HEAD_DIM) < (HEAD_DIM // 2), [lane < HEAD_DIM, lane >= HEAD_DIM]


def _attn_bias():
    qi = lax.broadcasted_iota(jnp.int32, (4 * BLK, 2 * BLK), 0) % BLK
    kj = lax.broadcasted_iota(jnp.int32, (4 * BLK, 2 * BLK), 1)
    band = (kj > qi) & (kj <= qi + BLK)
    return jnp.stack([jnp.where(band & (kj >= BLK), 0.0, NEG_INF), jnp.where(band, 0.0, NEG_INF)]).astype(F32)


def _attn_prep_tile(proj_ref, c_ref, s_ref, qs_ref, kr_ref, tm):
    lt32, halves = _lane_masks(BLK)
    for b in range(tm // BLK):
        rows = pl.ds(b * BLK, BLK)
        cc, sc = c_ref[rows, :], s_ref[rows, :]
        qr = [_rope(proj_ref[rows, pl.ds(O_Q + p * 128, 128)].astype(F32), cc, sc, lt32) * QSCALE for p in range(8)]
        for g in range(N_KV):
            qs_ref[g, pl.ds(4 * b * BLK, 4 * BLK), :] = _stack_heads(qr, g, halves).astype(BF)
        kr_ref[rows, :] = jnp.concatenate([_rope(proj_ref[rows, pl.ds(O_K + r * 128, 128)].astype(F32), cc, sc, lt32)
                                           for r in range(2)], axis=1).astype(BF)


ATT_BPS = 4
ATT_ROWS = ATT_BPS * BLK


def _before(n):
    return jnp.maximum(ATT_BPS * n - 1, 0)


def _attn_specs():
    return [pl.BlockSpec((N_KV, 4 * ATT_ROWS, 128), lambda n: (0, n, 0)),
            pl.BlockSpec((ATT_ROWS, 256), _row), pl.BlockSpec((BLK, 256), lambda n: (_before(n), 0)),
            pl.BlockSpec((ATT_ROWS, 256), lambda n: (n, O_V // 256)),
            pl.BlockSpec((BLK, 256), lambda n: (_before(n), O_V // 256)),
            pl.BlockSpec((2, 4 * BLK, 2 * BLK), lambda n: (0, 0, 0)),
            pl.BlockSpec(memory_space=pltpu.SMEM)]


def _bands(sb, kc_ref, kp_ref, vc_ref, vp_ref):
    own = pl.ds(sb * BLK, BLK)
    above = pl.ds((sb - 1) * BLK, BLK)
    kb, vb = [], []
    for r in range(2):
        cols = pl.ds(r * 128, 128)
        kprev = kp_ref[:, cols] if sb == 0 else kc_ref[above, cols]
        vprev = vp_ref[:, cols] if sb == 0 else vc_ref[above, cols]
        kb.append(jnp.concatenate([kprev, kc_ref[own, cols]], axis=0))
        vb.append(jnp.concatenate([vprev, vc_ref[own, cols]], axis=0))
    return kb, vb


def _block_bias(sb, bias_ref):
    return bias_ref[jnp.minimum(pl.program_id(0), 1)] if sb == 0 else bias_ref[1]


def _sink_rows(sink_ref, g):
    return jnp.concatenate([jnp.full((BLK, 128), sink_ref[4 * g + hh], F32) for hh in range(4)], axis=0)


def _both(t):
    return jnp.concatenate([t, t], axis=1)


def _unstack_heads(t, g, halves, acc):
    half = g % 2
    for hh in range(4):
        h = 4 * g + hh
        th = jnp.where(halves[half], t[hh * BLK:(hh + 1) * BLK], 0.0)
        if h % 2 != half:
            th = pltpu.roll(th, HEAD_DIM, 1)
        acc[h // 2] = acc[h // 2] + th


def _stack_heads(chunks, g, halves):
    half = g % 2
    parts = []
    for hh in range(4):
        h = 4 * g + hh
        t = chunks[h // 2]
        if h % 2 != half:
            t = pltpu.roll(t, HEAD_DIM, 1)
        parts.append(jnp.where(halves[half], t, 0.0))
    return jnp.concatenate(parts, axis=0)


def _attn_fwd(qs, kr, proj, bias, sinks, *, name):
    T = proj.shape[0]
    assert T % ATT_ROWS == 0

    def body(qs_ref, kc_ref, kp_ref, vc_ref, vp_ref, bias_ref, sink_ref, o_ref, lse_ref):
        _, h128 = _lane_masks(BLK)
        _, h256 = _lane_masks(2 * BLK)
        _, h512 = _lane_masks(4 * BLK)
        groups = range(N_KV)
        sink = [_sink_rows(sink_ref, g) for g in groups]
        for sb in range(ATT_BPS):
            rows = pl.ds(4 * sb * BLK, 4 * BLK)
            kb, vb = _bands(sb, kc_ref, kp_ref, vc_ref, vp_ref)
            outs = [jnp.zeros((BLK, 128), F32) for _ in range(8)]
            bias = _block_bias(sb, bias_ref)
            s = [lax.dot_general(qs_ref[g, rows, :], kb[g // 2], NT, preferred_element_type=F32) + bias for g in groups]
            m = [jnp.maximum(jnp.broadcast_to(jnp.max(s[g], axis=-1, keepdims=True), (4 * BLK, 128)), sink[g])
                 for g in groups]
            p = [jnp.exp(s[g] - _both(m[g])).astype(BF) for g in groups]
            vg = [jnp.where(h256[g % 2], vb[g // 2].astype(F32), 1.0).astype(BF) for g in groups]
            o = [jnp.dot(p[g], vg[g], preferred_element_type=F32) for g in groups]
            denom = [jnp.where(h512[g % 2], pltpu.roll(o[g], HEAD_DIM, 1), o[g]) + jnp.exp(sink[g] - m[g])
                     for g in groups]
            for g in groups:
                lse_ref[g, rows, :] = m[g] + jnp.log(denom[g])
                _unstack_heads(o[g] * (1.0 / denom[g]), g, h128, outs)
            o_ref[pl.ds(sb * BLK, BLK), :] = jnp.concatenate(outs, axis=1).astype(BF)

    return pl.pallas_call(
        body, name=name, grid=(T // ATT_ROWS,),
        in_specs=_attn_specs(),
        out_specs=[pl.BlockSpec((ATT_ROWS, D), _row), pl.BlockSpec((N_KV, 4 * ATT_ROWS, 128), lambda n: (0, n, 0))],
        out_shape=[jax.ShapeDtypeStruct((T, D), BF), jax.ShapeDtypeStruct((N_KV, 4 * T, 128), F32)],
        compiler_params=_cp(("parallel",)),
    )(qs, kr, kr, proj, proj, bias, sinks)


def _attn_bwd(qs, kr, proj, bias, sinks, lse, o, do, cos, sin, dproj, *, name):
    T = proj.shape[0]
    assert T % ATT_ROWS == 0

    def body(qs_ref, kc_ref, kp_ref, vc_ref, vp_ref, bias_ref, sink_ref, lse_ref, o_ref, do_ref,
             cc_ref, sc_ref, cp_ref, sp_ref, dproj_ref, dq_ref, dkc_ref, dkp_ref, dvc_ref, dvp_ref, dsink_ref):
        @pl.when(pl.program_id(0) == 0)
        def _():
            dsink_ref[...] = jnp.zeros_like(dsink_ref)
        lt32, h128 = _lane_masks(BLK)
        lane1 = lax.broadcasted_iota(jnp.int32, (1, 128), 1)
        dsink = jnp.zeros((1, 128), F32)
        groups = range(N_KV)
        for sb in range(ATT_BPS):
            own = pl.ds(sb * BLK, BLK)
            rows = pl.ds(4 * sb * BLK, 4 * BLK)
            kb, vb = _bands(sb, kc_ref, kp_ref, vc_ref, vp_ref)
            oc = [o_ref[own, pl.ds(p * 128, 128)].astype(F32) for p in range(8)]
            doc = [do_ref[own, pl.ds(p * 128, 128)].astype(F32) for p in range(8)]
            dqs = [jnp.zeros((BLK, 128), F32) for _ in range(8)]
            bias = _block_bias(sb, bias_ref)
            q = [qs_ref[g, rows, :] for g in groups]
            lse_g = [lse_ref[g, rows, :] for g in groups]
            s = [lax.dot_general(q[g], kb[g // 2], NT, preferred_element_type=F32) + bias for g in groups]
            dos = [_stack_heads(doc, g, h128) for g in groups]
            dosb = [t.astype(BF) for t in dos]
            dp = [lax.dot_general(dosb[g], vb[g // 2], NT, preferred_element_type=F32) for g in groups]
            delta = [jnp.broadcast_to(jnp.sum(dos[g] * _stack_heads(oc, g, h128), axis=-1, keepdims=True),
                                      (4 * BLK, 128)) for g in groups]
            p = [jnp.exp(s[g] - _both(lse_g[g])) for g in groups]
            ds = [(p[g] * (dp[g] - _both(delta[g]))).astype(BF) for g in groups]
            pb = [t.astype(BF) for t in p]
            dvg = [lax.dot_general(pb[g], dosb[g], TN, preferred_element_type=F32) for g in groups]
            dkg = [lax.dot_general(ds[g], q[g], TN, preferred_element_type=F32) for g in groups]
            dqg = [jnp.dot(ds[g], kb[g // 2], preferred_element_type=F32) * QSCALE for g in groups]
            dvr = [dvg[0] + dvg[1], dvg[2] + dvg[3]]
            dkr = [dkg[0] + dkg[1], dkg[2] + dkg[3]]
            for g in groups:
                _unstack_heads(dqg[g], g, h128, dqs)
                dsk = -jnp.exp(_sink_rows(sink_ref, g) - lse_g[g]) * delta[g]
                for hh in range(4):
                    val = jnp.sum(dsk[hh * BLK:(hh + 1) * BLK], axis=0, keepdims=True)
                    dsink = dsink + jnp.where(lane1 == 4 * g + hh, val, 0.0)
            cc, sc = cc_ref[own, :], sc_ref[own, :]
            cp, sp = (cp_ref[...], sp_ref[...]) if sb == 0 else (cc_ref[pl.ds((sb - 1) * BLK, BLK), :],
                                                                  sc_ref[pl.ds((sb - 1) * BLK, BLK), :])
            dq_ref[own, :] = jnp.concatenate([_rope(t, cc, sc, lt32, inverse=True) for t in dqs], axis=1).astype(BF)
            dkp_ref[own, :] = jnp.concatenate([_rope(t[:BLK], cp, sp, lt32, inverse=True) for t in dkr], axis=1)
            dkc_ref[own, :] = jnp.concatenate([_rope(t[BLK:], cc, sc, lt32, inverse=True) for t in dkr], axis=1)
            dvp_ref[own, :] = jnp.concatenate([t[:BLK] for t in dvr], axis=1)
            dvc_ref[own, :] = jnp.concatenate([t[BLK:] for t in dvr], axis=1)
        dsink_ref[...] += dsink

    kv = pl.BlockSpec((ATT_ROWS, 256), _row)
    tc = pl.BlockSpec((ATT_ROWS, 128), _row)
    tp = pl.BlockSpec((BLK, 128), lambda n: (_before(n), 0))
    return pl.pallas_call(
        body, name=name, grid=(T // ATT_ROWS,),
        in_specs=_attn_specs() + [pl.BlockSpec((N_KV, 4 * ATT_ROWS, 128), lambda n: (0, n, 0)),
                                  pl.BlockSpec((ATT_ROWS, D), _row), pl.BlockSpec((ATT_ROWS, D), _row), tc, tc, tp, tp,
                                  pl.BlockSpec(memory_space=pl.ANY)],
        out_specs=[pl.BlockSpec((ATT_ROWS, D), lambda n: (n, O_Q // D)), kv, kv, kv, kv,
                   pl.BlockSpec((1, 128), _const2)],
        out_shape=[jax.ShapeDtypeStruct(dproj.shape, BF)] + [jax.ShapeDtypeStruct((T, 256), F32)] * 4
        + [jax.ShapeDtypeStruct((1, 128), F32)],
        input_output_aliases={14: 0},
        compiler_params=_cp(("arbitrary",)),
    )(qs, kr, kr, proj, proj, bias, sinks, lse, o, do, cos, sin, cos, sin, dproj)


def _dkv_combine(dkc, dkp, dvc, dvp, dproj, *, name):
    T = dkc.shape[0]
    nb = T // BLK
    tm = _tile(T, 4 * BLK)
    bpt = tm // BLK
    nt = T // tm

    def body(dkc_ref, dkp_ref, dkn_ref, dvc_ref, dvp_ref, dvn_ref, dproj_ref, o_ref):
        keep = jnp.where(pl.program_id(0) == nt - 1, 0.0, 1.0)

        def shifted(prev_ref, next_ref):
            nxt = keep * next_ref[...]
            return nxt if bpt == 1 else jnp.concatenate([prev_ref[BLK:, :], nxt], axis=0)

        o_ref[:, 0:256] = (dkc_ref[...] + shifted(dkp_ref, dkn_ref)).astype(BF)
        o_ref[:, 256:512] = (dvc_ref[...] + shifted(dvp_ref, dvn_ref)).astype(BF)

    cur = pl.BlockSpec((tm, 256), _row)
    nxt = pl.BlockSpec((BLK, 256), lambda i: (jnp.minimum((i + 1) * bpt, nb - 1), 0))
    return pl.pallas_call(
        body, name=name, grid=(nt,),
        in_specs=[cur, cur, nxt, cur, cur, nxt, pl.BlockSpec(memory_space=pl.ANY)],
        out_specs=pl.BlockSpec((tm, 512), lambda i: (i, O_K // 512)),
        out_shape=jax.ShapeDtypeStruct(dproj.shape, BF),
        input_output_aliases={6: 0},
        compiler_params=_cp(("parallel",)),
    )(dkc, dkp, dkp, dvc, dvp, dvp, dproj)


HALO = 16


def _conv_shifts(cu, hprev, tm):
    row = lax.broadcasted_iota(jnp.int32, (8, cu.shape[1]), 0)
    h1 = hprev[HALO - 1:HALO, :]
    h2 = hprev[HALO - 2:HALO - 1, :]
    m1 = pltpu.roll(cu, 1, 0)
    m2 = pltpu.roll(cu, 2, 0)
    m1 = jnp.concatenate([jnp.where(row == 0, h1, m1[0:8]), m1[8:]], axis=0)
    m2 = jnp.concatenate([jnp.where(row == 0, h2, jnp.where(row == 1, h1, m2[0:8])), m2[8:]], axis=0)
    return m1, m2


def _mixer_mid_fwd(proj, attn, wcp, wap, wout, convw, x, gt, *, tm, name):
    T = x.shape[0]
    tm = _tile(T, tm)
    hb = tm // HALO

    def body(bg_ref, cg_ref, u_ref, hcg_ref, hu_ref, zc0_ref, zc1_ref, za0_ref, za1_ref, at_ref,
             wcp_ref, wap_ref, wout_ref, cw_ref, x_ref, gt_ref,
             x2_ref, gc_ref, yc_ref, ya_ref, mg_ref, o_ref):
        first = jnp.where(pl.program_id(0) == 0, 0.0, 1.0)
        cu = cg_ref[...].astype(F32) * u_ref[...].astype(F32)
        hprev = first * (hcg_ref[...].astype(F32) * hu_ref[...].astype(F32))
        m1, m2 = _conv_shifts(cu, hprev, tm)
        cv = cw_ref[0:1, :] * m2 + cw_ref[1:2, :] * m1 + cw_ref[2:3, :] * cu
        gc = (bg_ref[...].astype(F32) * cv).astype(BF)
        gc_ref[...] = gc
        yc = jnp.dot(gc, wcp_ref[...], preferred_element_type=F32)
        ya = jnp.dot(at_ref[...], wap_ref[...], preferred_element_type=F32)
        yc_ref[...] = yc.astype(BF)
        ya_ref[...] = ya.astype(BF)
        zc = jnp.concatenate([zc0_ref[...], zc1_ref[...]], axis=1).astype(F32)
        za = jnp.concatenate([za0_ref[...], za1_ref[...]], axis=1).astype(F32)
        mg = (_sigmoid(zc) * yc + _sigmoid(za) * ya).astype(BF)
        mg_ref[...] = mg
        o = jnp.dot(mg, wout_ref[...], preferred_element_type=F32)
        o_ref[...] = o.astype(BF)
        x2_ref[...] = x_ref[...] + gt_ref[...] * o

    wspec = pl.BlockSpec((D, D), _const2)
    rowspec = pl.BlockSpec((tm, D), _row)
    return pl.pallas_call(
        body, name=name, grid=(T // tm,),
        in_specs=[_col(tm, O_BG), _col(tm, O_CG), _col(tm, O_U), _halo_prev(hb, O_CG), _halo_prev(hb, O_U),
                  _col(tm, O_ZC, 512), _col(tm, O_ZC + 512, 512), _col(tm, O_ZA, 512), _col(tm, O_ZA + 512, 512),
                  rowspec, wspec, wspec, wspec, pl.BlockSpec((8, D), _const2), rowspec, pl.BlockSpec((1, D), _const2)],
        out_specs=[rowspec] * 6,
        out_shape=[jax.ShapeDtypeStruct((T, D), F32)] + [jax.ShapeDtypeStruct((T, D), BF)] * 5,
        compiler_params=_cp(("parallel",)),
    )(proj, proj, proj, proj, proj, proj, proj, proj, proj, attn, wcp, wap, wout, convw, x, gt)


def _col(tm, c, w=D):
    assert c % w == 0
    return pl.BlockSpec((tm, w), lambda i: (i, c // w))


def _halo_prev(hb, c):
    return pl.BlockSpec((HALO, D), lambda i: (jnp.maximum(i * hb - 1, 0), c // D))


def _halo_next(hb, nblk, c=0):
    return pl.BlockSpec((HALO, D), lambda i: (jnp.minimum((i + 1) * hb, nblk - 1), c // D))


def _mixer_mid_bwd(dx2, gt, o, proj, yc, ya, wout, wcp, wap, *, tm, name):
    T = dx2.shape[0]
    tm = _tile(T, tm)
    nt = T // tm

    def body(dx_ref, gt_ref, o_ref, zc0_ref, zc1_ref, za0_ref, za1_ref, yc_ref, ya_ref, wout_ref, wcp_ref, wap_ref,
             dout_ref, dyc_ref, dya_ref, dgc_ref, dat_ref, dproj_ref, dgt_ref, dzs, sems):
        i = pl.program_id(0)
        slot = lax.rem(i, 2)

        def slab_copy(step, s):
            return pltpu.make_async_copy(
                dzs.at[s], dproj_ref.at[pl.ds(pl.multiple_of(step * tm, tm), tm), pl.ds(O_ZC, 2 * D)], sems.at[s])

        @pl.when(i == 0)
        def _():
            dgt_ref[...] = jnp.zeros_like(dgt_ref)

        dxv = dx_ref[...]
        dgt_ref[...] += jnp.sum(dxv * o_ref[...].astype(F32), axis=0, keepdims=True)
        dout = (gt_ref[...] * dxv).astype(BF)
        dout_ref[...] = dout
        dmg = lax.dot_general(dout, wout_ref[...], NT, preferred_element_type=F32)
        sc = _sigmoid(jnp.concatenate([zc0_ref[...], zc1_ref[...]], axis=1).astype(F32))
        sa = _sigmoid(jnp.concatenate([za0_ref[...], za1_ref[...]], axis=1).astype(F32))
        dyc = (dmg * sc).astype(BF)
        dya = (dmg * sa).astype(BF)
        dyc_ref[...] = dyc
        dya_ref[...] = dya
        dzs[slot, :, 0:D] = (dmg * yc_ref[...].astype(F32) * (sc * (1.0 - sc))).astype(BF)
        dzs[slot, :, D:2 * D] = (dmg * ya_ref[...].astype(F32) * (sa * (1.0 - sa))).astype(BF)
        slab_copy(i, slot).start()
        dgc_ref[...] = lax.dot_general(dyc, wcp_ref[...], NT, preferred_element_type=F32).astype(BF)
        dat_ref[...] = lax.dot_general(dya, wap_ref[...], NT, preferred_element_type=F32).astype(BF)

        @pl.when(i > 0)
        def _():
            slab_copy(i - 1, 1 - slot).wait()

        @pl.when(i == nt - 1)
        def _():
            slab_copy(i, slot).wait()

    def zcol(c):
        return pl.BlockSpec((tm, 512), lambda i: (i, c // 512))

    wspec = pl.BlockSpec((D, D), _const2)
    rowspec = pl.BlockSpec((tm, D), _row)
    vec = pl.BlockSpec((1, D), _const2)
    return pl.pallas_call(
        body, name=name, grid=(nt,),
        in_specs=[rowspec, vec, rowspec, zcol(O_ZC), zcol(O_ZC + 512), zcol(O_ZA), zcol(O_ZA + 512),
                  rowspec, rowspec, wspec, wspec, wspec],
        out_specs=[rowspec] * 5 + [pl.BlockSpec(memory_space=pl.ANY), vec],
        out_shape=[jax.ShapeDtypeStruct((T, D), BF)] * 5 + [jax.ShapeDtypeStruct((T, NIN), BF),
                                                            jax.ShapeDtypeStruct((1, D), F32)],
        scratch_shapes=[pltpu.VMEM((2, tm, 2 * D), BF), pltpu.SemaphoreType.DMA((2,))],
        compiler_params=_cp(("arbitrary",)),
    )(dx2, gt, o, proj, proj, proj, proj, yc, ya, wout, wcp, wap)


def _conv_bwd(dgc, proj, convw, dproj, *, tm, name):
    T = dgc.shape[0]
    tm = _tile(T, tm)
    hb = tm // HALO
    nblk = T // HALO
    nt = T // tm

    def body(dgc_ref, ndgc_ref, bg_ref, nbg_ref, cg_ref, u_ref, hcg_ref, hu_ref, cw_ref, dproj_ref, dp_ref, dcw_ref):
        i = pl.program_id(0)

        @pl.when(i == 0)
        def _():
            dcw_ref[...] = jnp.zeros_like(dcw_ref)
        first = jnp.where(i == 0, 0.0, 1.0)
        last = jnp.where(i == nt - 1, 0.0, 1.0)
        cg = cg_ref[...].astype(F32)
        u = u_ref[...].astype(F32)
        bg = bg_ref[...].astype(F32)
        dg = dgc_ref[...].astype(F32)
        cu = cg * u
        hprev = first * (hcg_ref[...].astype(F32) * hu_ref[...].astype(F32))
        m1, m2 = _conv_shifts(cu, hprev, tm)
        w0, w1, w2 = cw_ref[0:1, :], cw_ref[1:2, :], cw_ref[2:3, :]
        cv = w0 * m2 + w1 * m1 + w2 * cu
        dcv = dg * bg
        nxt = last * (ndgc_ref[...].astype(F32) * nbg_ref[...].astype(F32))
        n0, n1 = nxt[0:1, :], nxt[1:2, :]
        row = lax.broadcasted_iota(jnp.int32, (8, D), 0)
        p1 = pltpu.roll(dcv, tm - 1, 0)
        p2 = pltpu.roll(dcv, tm - 2, 0)
        p1 = jnp.concatenate([p1[:tm - 8], jnp.where(row == 7, n0, p1[tm - 8:])], axis=0)
        p2 = jnp.concatenate([p2[:tm - 8], jnp.where(row == 7, n1, jnp.where(row == 6, n0, p2[tm - 8:]))], axis=0)
        dcu = w2 * dcv + w1 * p1 + w0 * p2
        dp_ref[:, 0:D] = (dg * cv).astype(BF)
        dp_ref[:, D:2 * D] = (dcu * u).astype(BF)
        dp_ref[:, 2 * D:3 * D] = (dcu * cg).astype(BF)
        dcw_ref[0:1, :] += jnp.sum(dcv * m2, axis=0, keepdims=True)
        dcw_ref[1:2, :] += jnp.sum(dcv * m1, axis=0, keepdims=True)
        dcw_ref[2:3, :] += jnp.sum(dcv * cu, axis=0, keepdims=True)

    rowspec = pl.BlockSpec((tm, D), _row)
    cw = pl.BlockSpec((8, D), _const2)
    return pl.pallas_call(
        body, name=name, grid=(nt,),
        in_specs=[rowspec, _halo_next(hb, nblk), _col(tm, O_BG), _halo_next(hb, nblk, O_BG),
                  _col(tm, O_CG), _col(tm, O_U), _halo_prev(hb, O_CG), _halo_prev(hb, O_U), cw,
                  pl.BlockSpec(memory_space=pl.ANY)],
        out_specs=[pl.BlockSpec((tm, 3 * D), _row), cw],
        out_shape=[jax.ShapeDtypeStruct(dproj.shape, BF), jax.ShapeDtypeStruct((8, D), F32)],
        input_output_aliases={9: 0},
        compiler_params=_cp(("arbitrary",)),
    )(dgc, dgc, proj, proj, proj, proj, proj, proj, convw, dproj)


def _adam_math(w, g, m, v):
    nm = ADAM_B1 * m + (1.0 - ADAM_B1) * g
    nv = ADAM_B2 * v + (1.0 - ADAM_B2) * (g * g)
    m_hat = nm / (1.0 - ADAM_B1 ** ADAM_STEP)
    v_hat = nv / (1.0 - ADAM_B2 ** ADAM_STEP)
    return -ADAM_LR * (m_hat / (jnp.sqrt(v_hat) + ADAM_EPS) + ADAM_WD * w), nm, nv


SMALL = ("b_ada", "g_ffn1", "g_mix", "g_ffn2", "g_final", "conv_w", "sinks")


def _adam_small(gsum, conv_g, w, m, v, *, name):
    nsm = len(SMALL)

    def body(*refs):
        gs_ref, cg_ref = refs[0], refs[1]
        w_refs, m_refs, v_refs = (refs[2 + k * nsm:2 + (k + 1) * nsm] for k in range(3))
        outs = refs[2 + 3 * nsm:]
        for p, n in enumerate(SMALL):
            if n == "b_ada":
                pieces = [(slice(None), slice(r * D, (r + 1) * D), gs_ref[R_MODS + r:R_MODS + r + 1, :])
                          for r in range(N_MOD)]
            elif n == "conv_w":
                pieces = [(slice(None), slice(None), cg_ref[...])]
            elif n == "sinks":
                pieces = [(slice(None), slice(None), gs_ref[R_SINK:R_SINK + 1, 0:N_HEADS])]
            else:
                row = dict(g_ffn1=R_G1, g_mix=R_GM, g_ffn2=R_G2, g_final=R_GF)[n]
                pieces = [(slice(None), slice(None), gs_ref[row:row + 1, :])]
            for rs, cs, g in pieces:
                d, nm, nv = _adam_math(w_refs[p][rs, cs], g, m_refs[p][rs, cs], v_refs[p][rs, cs])
                for k, val in enumerate((g, d, nm, nv)):
                    outs[k * nsm + p][rs, cs] = val

    args = [gsum, conv_g] + [d[n] for d in (w, m, v) for n in SMALL]
    shapes = [jax.ShapeDtypeStruct(w[n].shape, F32) for _ in range(4) for n in SMALL]
    res = pl.pallas_call(body, name=name, out_shape=shapes, compiler_params=_cp())(*args)
    return [dict(zip(SMALL, res[k * nsm:(k + 1) * nsm])) for k in range(4)]


def _adam(w, g, m, v, *, tm, name):
    _, R, C = w.shape
    tm = _tile(R, tm)
    parts = g.ndim == 3

    def body(w_ref, g_ref, m_ref, v_ref, go_ref, d_ref, nm_ref, nv_ref):
        if parts:
            gv = g_ref[0].astype(F32)
            for s in range(1, N_DEV):
                gv = gv + g_ref[s].astype(F32)
        else:
            gv = g_ref[...]
        go_ref[0] = gv
        d_ref[0], nm_ref[0], nv_ref[0] = _adam_math(w_ref[0], gv, m_ref[0], v_ref[0])

    spec = pl.BlockSpec((1, tm, C), lambda i: (0, i, 0))
    gspec = pl.BlockSpec((N_DEV, tm, C), lambda i: (0, i, 0)) if parts else pl.BlockSpec((tm, C), _row)
    return pl.pallas_call(
        body, name=name, grid=(R // tm,),
        in_specs=[spec, gspec, spec, spec], out_specs=[spec] * 4,
        out_shape=[jax.ShapeDtypeStruct((1, R, C), F32)] * 4,
        compiler_params=_cp(("parallel",)),
    )(w, g, m, v)


def _mods_part(c_all, w_ada, b_ada, *, name):
    C = w_ada.shape[1]

    def body(c_ref, w_ref, b_ref, o_ref):
        cv = c_ref[...]
        ca = cv * jax.nn.sigmoid(cv)
        o_ref[...] = jnp.dot(ca, w_ref[...], preferred_element_type=F32,
                             precision=lax.Precision.HIGHEST) + b_ref[...]

    return pl.pallas_call(
        body, name=name,
        out_shape=jax.ShapeDtypeStruct((N_DEV, C), F32),
        compiler_params=_cp(),
    )(c_all, w_ada, b_ada)


def _wada_grad(c_all_t, gm, *, name):
    C = gm.shape[1]

    def body(c_ref, g_ref, o_ref):
        cv = c_ref[...]
        ca = cv * jax.nn.sigmoid(cv)
        acc = ca[:, 0:1] * g_ref[0:1, :]
        for b in range(1, N_DEV):
            acc = acc + ca[:, b:b + 1] * g_ref[b:b + 1, :]
        o_ref[...] = acc

    return pl.pallas_call(
        body, name=name,
        out_shape=jax.ShapeDtypeStruct((D, C), F32),
        compiler_params=_cp(),
    )(c_all_t, gm)


def _peer(x, y, c, d):
    px = lax.rem(x + ((d >> 2) & 1), 2)
    py = lax.rem(y + ((d >> 1) & 1), 2)
    pc = lax.rem(c + (d & 1), 2)
    return (px, py, pc), 4 * px + 2 * py + pc


def _exchange(xs, *, scatter, name):
    n = len(xs)
    nsem = n * (N_DEV - 1)

    def body(*refs):
        ins, outs = refs[:n], refs[n:2 * n]
        token, send_sems, recv_sems, local_sems = refs[2 * n:]
        x, y, c = lax.axis_index("x"), lax.axis_index("y"), lax.axis_index("c")
        me = 4 * x + 2 * y + c
        token[...] = jnp.zeros_like(token)

        def src(t, idx):
            return ins[t].at[idx] if scatter else ins[t]

        local = [pltpu.make_async_copy(src(t, me), outs[t].at[me], local_sems.at[t]) for t in range(n)]
        for cp in local:
            cp.start()
        remote = []
        for t in range(n):
            for d in range(1, N_DEV):
                peer, pidx = _peer(x, y, c, d)
                k = t * (N_DEV - 1) + d - 1
                send = pltpu.make_async_remote_copy(src_ref=src(t, pidx), dst_ref=outs[t].at[me],
                                                    send_sem=send_sems.at[k], recv_sem=recv_sems.at[k],
                                                    device_id=peer, device_id_type=MESH)
                recv = pltpu.make_async_remote_copy(src_ref=src(t, pidx), dst_ref=outs[t].at[pidx],
                                                    send_sem=send_sems.at[k], recv_sem=recv_sems.at[k],
                                                    device_id=peer, device_id_type=MESH)
                send.start()
                remote.append((send, recv))
        for cp in local:
            cp.wait()
        for send, recv in remote:
            send.wait_send()
            recv.wait_recv()

    anyspec = pl.BlockSpec(memory_space=pl.ANY)
    out_shape = [jax.ShapeDtypeStruct(a.shape if scatter else (N_DEV,) + a.shape, a.dtype) for a in xs]
    out_shape.append(jax.ShapeDtypeStruct((8, 128), F32))
    return pl.pallas_call(
        body, name=name,
        in_specs=[anyspec] * n, out_specs=[anyspec] * n + [pl.BlockSpec(memory_space=pltpu.VMEM)],
        out_shape=out_shape,
        scratch_shapes=[pltpu.SemaphoreType.DMA((nsem,)), pltpu.SemaphoreType.DMA((nsem,)),
                        pltpu.SemaphoreType.DMA((n,))],
    )(*xs)


def _sum8(parts, *, name):
    _, R, C = parts.shape

    def body(p_ref, o_ref):
        acc = p_ref[0]
        for s in range(1, N_DEV):
            acc = acc + p_ref[s]
        o_ref[...] = acc

    return pl.pallas_call(body, name=name, out_shape=jax.ShapeDtypeStruct((R, C), F32),
                          compiler_params=_cp())(parts)


HBM_SPEC = pl.BlockSpec(memory_space=pltpu.HBM)
SEM_SPEC = pl.BlockSpec(memory_space=pltpu.SEMAPHORE)
N_PEER = N_DEV - 1


def _split_copies(src_refs, land_refs, send_sems, recv_sems, scatter):
    x, y, c = lax.axis_index("x"), lax.axis_index("y"), lax.axis_index("c")
    me = 4 * x + 2 * y + c
    pairs = []
    for j, (src, land) in enumerate(zip(src_refs, land_refs)):
        for d in range(1, N_DEV):
            peer, pidx = _peer(x, y, c, d)
            k = j * N_PEER + d - 1
            s = src.at[pidx] if scatter else src
            send = pltpu.make_async_remote_copy(src_ref=s, dst_ref=land.at[me], send_sem=send_sems.at[k],
                                                recv_sem=recv_sems.at[k], device_id=peer, device_id_type=MESH)
            recv = pltpu.make_async_remote_copy(src_ref=s, dst_ref=land.at[pidx], send_sem=send_sems.at[k],
                                                recv_sem=recv_sems.at[k], device_id=peer, device_id_type=MESH)
            pairs.append((send, recv))
    return pairs


def _own_slot(block, me):
    land = lax.empty((N_DEV,) + block.shape, block.dtype)
    return lax.dynamic_update_slice(land, block[None], (me, 0, 0))


def _split_start(srcs, lands, groups, *, scatter, name):
    n, ng = len(srcs), len(groups)

    def body(*refs):
        src_refs, land_refs = refs[:n], refs[n:2 * n]
        sems = refs[2 * n:2 * n + 2 * ng]
        token = refs[-1]
        for gi, g in enumerate(groups):
            pairs = _split_copies([src_refs[t] for t in g], [land_refs[t] for t in g], sems[2 * gi],
                                  sems[2 * gi + 1], scatter)
            for send, _ in pairs:
                send.start()
        token[...] = jnp.zeros_like(token)

    sem_shapes = []
    for g in groups:
        sem_shapes += [pltpu.SemaphoreType.DMA((len(g) * N_PEER,))] * 2
    thru = [pltpu.HBM(a.shape, a.dtype) for a in list(srcs) + list(lands)]
    outs = pl.pallas_call(
        body, name=name,
        out_shape=tuple(sem_shapes + thru + [jax.ShapeDtypeStruct((8, 128), F32)]),
        in_specs=[HBM_SPEC] * (2 * n),
        out_specs=tuple([SEM_SPEC] * (2 * ng) + [HBM_SPEC] * (2 * n) + [pl.BlockSpec(memory_space=pltpu.VMEM)]),
        input_output_aliases={i: 2 * ng + i for i in range(2 * n)},
        compiler_params=pltpu.CompilerParams(has_side_effects=pltpu.SideEffectType.DATAFLOW_SIDE_EFFECTING),
    )(*[pltpu.with_memory_space_constraint(a, pltpu.HBM) for a in list(srcs) + list(lands)])
    sems = [(outs[2 * gi], outs[2 * gi + 1]) for gi in range(ng)]
    return sems, outs[2 * ng:2 * ng + n], outs[2 * ng + n:2 * ng + 2 * n], outs[-1]


def _behind(v, token):
    if token is None:
        return v
    return v + token[0, 0].astype(v.dtype)


def _split_wait(srcs, lands, sems, after, *, scatter, name):
    m = len(srcs)

    def body(*refs):
        src_refs, land_refs = refs[:m], refs[m:2 * m]
        send_sems, recv_sems = refs[2 * m], refs[2 * m + 1]
        for send, recv in _split_copies(src_refs, land_refs, send_sems, recv_sems, scatter):
            send.wait_send()
            recv.wait_recv()

    outs = pl.pallas_call(
        body, name=name,
        out_shape=tuple(pltpu.HBM(a.shape, a.dtype) for a in list(srcs) + list(lands)),
        in_specs=[HBM_SPEC] * (2 * m) + [SEM_SPEC, SEM_SPEC, pl.BlockSpec(memory_space=pl.ANY)],
        out_specs=tuple([HBM_SPEC] * (2 * m)),
        input_output_aliases={i: i for i in range(2 * m)},
        compiler_params=pltpu.CompilerParams(has_side_effects=pltpu.SideEffectType.DATAFLOW_SIDE_EFFECTING),
    )(*srcs, *lands, sems[0], sems[1], after)
    return outs[m:]


TL_FIRST = (1, 2, 4, 6)
TL_ICI = (2, 4, 6)
EFFECT = pltpu.SideEffectType.DATAFLOW_SIDE_EFFECTING


def _tl_first(src_refs, land_refs, send_sems, recv_sems):
    x, y, c = lax.axis_index("x"), lax.axis_index("y"), lax.axis_index("c")
    me = 4 * x + 2 * y + c
    out = []
    for j, (src, land) in enumerate(zip(src_refs, land_refs)):
        for i, d in enumerate(TL_FIRST):
            peer, pidx = _peer(x, y, c, d)
            k = len(TL_FIRST) * j + i
            send = pltpu.make_async_remote_copy(src_ref=src, dst_ref=land.at[me], send_sem=send_sems.at[k],
                                                recv_sem=recv_sems.at[k], device_id=peer, device_id_type=MESH)
            recv = pltpu.make_async_remote_copy(src_ref=src, dst_ref=land.at[pidx], send_sem=send_sems.at[k],
                                                recv_sem=recv_sems.at[k], device_id=peer, device_id_type=MESH)
            out.append((d, send, recv))
    return out


def _tl_second(land_refs, send_sems, recv_sems):
    x, y, c = lax.axis_index("x"), lax.axis_index("y"), lax.axis_index("c")
    sibling, _ = _peer(x, y, c, 1)
    out = []
    for j, land in enumerate(land_refs):
        for i, d in enumerate(TL_ICI):
            _, mine = _peer(x, y, c, d)
            _, theirs = _peer(x, y, c, d + 1)
            k = len(TL_ICI) * j + i
            send = pltpu.make_async_remote_copy(src_ref=land.at[mine], dst_ref=land.at[mine], send_sem=send_sems.at[k],
                                                recv_sem=recv_sems.at[k], device_id=sibling, device_id_type=MESH)
            recv = pltpu.make_async_remote_copy(src_ref=land.at[mine], dst_ref=land.at[theirs],
                                                send_sem=send_sems.at[k], recv_sem=recv_sems.at[k],
                                                device_id=sibling, device_id_type=MESH)
            out.append((send, recv))
    return out


def _tl_start(srcs, lands, groups, *, name):
    n, ng = len(srcs), len(groups)

    def body(*refs):
        src_refs, land_refs = refs[:n], refs[n:2 * n]
        sems = refs[2 * n:2 * n + 2 * ng]
        for gi, g in enumerate(groups):
            for _, send, _ in _tl_first([src_refs[t] for t in g], [land_refs[t] for t in g], sems[2 * gi],
                                        sems[2 * gi + 1]):
                send.start()
        refs[-1][...] = jnp.zeros_like(refs[-1])

    sem_shapes = []
    for g in groups:
        sem_shapes += [pltpu.SemaphoreType.DMA((len(g) * len(TL_FIRST),))] * 2
    thru = [pltpu.HBM(a.shape, a.dtype) for a in list(srcs) + list(lands)]
    outs = pl.pallas_call(
        body, name=name,
        out_shape=tuple(sem_shapes + thru + [jax.ShapeDtypeStruct((8, 128), F32)]),
        in_specs=[HBM_SPEC] * (2 * n),
        out_specs=tuple([SEM_SPEC] * (2 * ng) + [HBM_SPEC] * (2 * n) + [pl.BlockSpec(memory_space=pltpu.VMEM)]),
        input_output_aliases={i: 2 * ng + i for i in range(2 * n)},
        compiler_params=pltpu.CompilerParams(has_side_effects=EFFECT),
    )(*[pltpu.with_memory_space_constraint(a, pltpu.HBM) for a in list(srcs) + list(lands)])
    sems = [(outs[2 * gi], outs[2 * gi + 1]) for gi in range(ng)]
    return sems, outs[2 * ng:2 * ng + n], outs[2 * ng + n:2 * ng + 2 * n], outs[-1]


def _tl_forward(srcs, lands, sems1, after, *, name):
    m = len(srcs)

    def body(*refs):
        src_refs, land_refs = refs[:m], refs[m:2 * m]
        send1, recv1 = refs[2 * m], refs[2 * m + 1]
        send2, recv2 = refs[2 * m + 3], refs[2 * m + 4]
        for d, _, recv in _tl_first(src_refs, land_refs, send1, recv1):
            if d in TL_ICI:
                recv.wait_recv()
        for send, _ in _tl_second(land_refs, send2, recv2):
            send.start()

    sem = pltpu.SemaphoreType.DMA((m * len(TL_ICI),))
    outs = pl.pallas_call(
        body, name=name,
        out_shape=tuple([sem, sem] + [pltpu.HBM(a.shape, a.dtype) for a in list(srcs) + list(lands)]),
        in_specs=[HBM_SPEC] * (2 * m) + [SEM_SPEC, SEM_SPEC, pl.BlockSpec(memory_space=pl.ANY)],
        out_specs=tuple([SEM_SPEC, SEM_SPEC] + [HBM_SPEC] * (2 * m)),
        input_output_aliases={i: 2 + i for i in range(2 * m)},
        compiler_params=pltpu.CompilerParams(has_side_effects=EFFECT),
    )(*srcs, *lands, sems1[0], sems1[1], after)
    return (outs[0], outs[1]), outs[2:2 + m], outs[2 + m:2 + 2 * m]


def _tl_wait(srcs, lands, sems1, sems2, after, *, name):
    m = len(srcs)

    def body(*refs):
        src_refs, land_refs = refs[:m], refs[m:2 * m]
        send1, recv1, send2, recv2 = refs[2 * m:2 * m + 4]
        for d, send, recv in _tl_first(src_refs, land_refs, send1, recv1):
            send.wait_send()
            if d not in TL_ICI:
                recv.wait_recv()
        for send, recv in _tl_second(land_refs, send2, recv2):
            send.wait_send()
            recv.wait_recv()

    outs = pl.pallas_call(
        body, name=name,
        out_shape=tuple(pltpu.HBM(a.shape, a.dtype) for a in list(srcs) + list(lands)),
        in_specs=[HBM_SPEC] * (2 * m) + [SEM_SPEC] * 4 + [pl.BlockSpec(memory_space=pl.ANY)],
        out_specs=tuple([HBM_SPEC] * (2 * m)),
        input_output_aliases={i: i for i in range(2 * m)},
        compiler_params=pltpu.CompilerParams(has_side_effects=EFFECT),
    )(*srcs, *lands, sems1[0], sems1[1], sems2[0], sems2[1], after)
    return outs[m:]


TM_PROJ = 512
TN_PROJ = 512
TM_ROW = 512
TM_NN = 512
TK_TN = 2048
TM_ADAM = 208
TN_FFN = F // 2
TN_IN = NIN // 4


def _tn(a, b, name, tn, token=None):
    if a.ndim == 2:
        a = a[None]
    return _tn_matmul(a, b, token, tn=tn, tk=TK_TN, name=name)


def _local_step(x, tgt, mods, g1, gm, g2, gf, convw8, sinks, w_get, g_put, tables=None):
    T = x.shape[0]
    sh1, sc1, gt1, sh2, sc2, gt2, sh3, sc3, gt3 = [mods[i:i + 1] for i in range(N_MOD)]
    cos, sin = _rope_tables(T) if tables is None else tables
    behind = _behind

    w = dict(w_get("gu1", mods))
    h1, ab1 = _norm_proj(x, g1, sc1, sh1, w["gu1"], tm=TM_PROJ, tn=TN_PROJ, name="ffn1_up")
    w.update(w_get("d1", ab1))
    x1 = _ffn_down_fwd(ab1, w["d1"], x, gt1, tm=TM_ROW, name="ffn1_down")
    w.update(w_get("mix", x1))
    h2, proj, qs, kr = _norm_proj(x1, gm, sc2, sh2, w["win"], (cos, sin), tm=TM_PROJ, tn=TN_PROJ, name="mix_in")
    bias = _attn_bias()
    attn, lse = _attn_fwd(qs, kr, proj, bias, sinks, name="attn_fwd")
    x2, gc, yc, ya, mg, o = _mixer_mid_fwd(proj, attn, w["cp"], w["ap"], w["out"], convw8, x1, gt2,
                                           tm=TM_ROW, name="mix_mid")
    w.update(w_get("ffn2", x2))
    h3, ab2, dx3, lsum, dgf = _ffn_fwd(x2, g2, sc3, sh3, gt3, w["gu2"], w["d2"], (tgt, gf), tm=TM_ROW,
                                           name="ffn2_final")

    dab2, dgt3, g_d2 = _ffn_down_bwd_dw(dx3, gt3, ab2, w["d2"], tm=TM_ROW, name="ffn2_down_bwd")
    dx2, dsh3, dsc3, dg2 = _nn_bwd_norm(dab2, w["gu2"], x2, g2, sc3, dx3, tm=TM_NN, name="ffn2_up_bwd")
    g_gu2 = _tn(dab2, h3, "ffn2_up_dw", TN_FFN)
    tok = g_put(dict(gu2=g_gu2, d2=g_d2))

    dout, dyc, dya, dgc, dat, dproj, dgt2 = _mixer_mid_bwd(dx2, behind(gt2, tok), o, proj, yc, ya, w["out"], w["cp"],
                                                           w["ap"], tm=TM_ROW, name="mix_mid_bwd")
    g_out = _tn(mg, dout, "mix_out_dw", D)
    g_cp = _tn(gc, dyc, "mix_cp_dw", D)
    g_ap = _tn(attn, dya, "mix_ap_dw", D)
    dproj, dkc, dkp, dvc, dvp, dsink = _attn_bwd(qs, kr, proj, bias, sinks, lse, attn, dat, cos, sin, dproj,
                                                 name="attn_bwd")
    dproj = _dkv_combine(dkc, dkp, dvc, dvp, dproj, name="attn_dkv")
    dproj, dcw = _conv_bwd(dgc, proj, convw8, dproj, tm=TM_ROW, name="conv_bwd")
    g_in = _tn(dproj, h2, "mix_in_dw", TN_IN)
    tok = g_put(dict(win=g_in, cp=g_cp, ap=g_ap, out=g_out))
    dx1, dsh2, dsc2, dgm = _nn_bwd_norm(dproj[None], w["win"], x1, gm, behind(sc2, tok), dx2, tm=TM_NN,
                                        name="mix_in_bwd")

    dab1, dgt1, g_d1 = _ffn_down_bwd_dw(dx1, gt1, ab1, w["d1"], tm=TM_ROW, name="ffn1_down_bwd")
    tok = g_put(dict(d1=g_d1))
    g_gu1 = _tn(dab1, h1, "ffn1_up_dw", TN_FFN, tok)
    tok = g_put(dict(gu1=g_gu1))
    dx0, dsh1, dsc1, dg1 = _nn_bwd_norm(dab1, w["gu1"], x, g1, behind(sc1, tok), dx1, tm=TM_NN,
                                        name="ffn1_up_bwd")

    small = dict(mods=jnp.concatenate([dsh1, dsc1, dgt1, dsh2, dsc2, dgt2, dsh3, dsc3, dgt3], axis=0),
                 g1=dg1, gm=dgm, g2=dg2, gf=dgf, convw=dcw[0:3], sinks=dsink[:, 0:N_HEADS])
    return lsum, dx0, small


BIG = ("gu1", "d1", "win", "cp", "ap", "out", "gu2", "d2")
TRANSPOSED = ("gu1", "win", "gu2")
SMALL_ROWS = 24
R_MODS, R_G1, R_GM, R_G2, R_GF, R_CONV, R_SINK, R_LOSS = 0, 9, 10, 11, 12, 13, 16, 17


def _pad_to(a, rows, cols):
    return jnp.pad(a, ((0, rows - a.shape[0]), (0, cols - a.shape[1])))


def _pack_small(b_ada, g1, gm, g2, gf, conv, sinks, lsum):
    rows = [b_ada.reshape(N_MOD, D), g1.reshape(1, D), gm.reshape(1, D), g2.reshape(1, D), gf.reshape(1, D),
            _pad_to(conv.reshape(3, -1), 3, D), _pad_to(sinks.reshape(1, N_HEADS), 1, D), lsum.reshape(1, D)]
    return _pad_to(jnp.concatenate(rows, axis=0), SMALL_ROWS, D)


def kernel(x, c, w_ada, b_ada, g_ffn1, w1_gu, w1_down, g_mix, w_in, conv_w, w_conv_proj, w_attn_proj, sinks, w_out, g_ffn2, w2_gu, w2_down, g_final, loss_target, m_w_ada, m_b_ada, m_g_ffn1, m_w1_gu, m_w1_down, m_g_mix, m_w_in, m_conv_w, m_w_conv_proj, m_w_attn_proj, m_sinks, m_w_out, m_g_ffn2, m_w2_gu, m_w2_down, m_g_final, v_w_ada, v_b_ada, v_g_ffn1, v_w1_gu, v_w1_down, v_g_mix, v_w_in, v_conv_w, v_w_conv_proj, v_w_attn_proj, v_sinks, v_w_out, v_g_ffn2, v_w2_gu, v_w2_down, v_g_final):
    me = 4 * lax.axis_index("x") + 2 * lax.axis_index("y") + lax.axis_index("c")
    ada_cols = w_ada.shape[2]
    conv_cols = conv_w.shape[2]

    native = dict(gu1=w1_gu[0], d1=w1_down[0], win=w_in[0], cp=w_conv_proj[0], ap=w_attn_proj[0], out=w_out[0],
                  gu2=w2_gu[0], d2=w2_down[0])

    def shard(n, token):
        a = _behind(native[n], token)
        return (a.T if n in TRANSPOSED else a).astype(BF)

    c_all, conv_all, _ = _exchange([c, _pad_to(conv_w[0], 8, conv_cols)], scatter=False, name="gather_cond")
    c_all = c_all.reshape(N_DEV, D)
    conv_full = conv_all[:, 0:3, :].transpose(1, 0, 2).reshape(3, D)

    b_cols = lax.dynamic_slice(b_ada, (0, me * ada_cols), (1, ada_cols))
    mods_cols = _mods_part(c_all, w_ada[0], b_cols, name="ada_mods")
    mods_all, mods_token = _exchange([mods_cols], scatter=False, name="gather_mods")
    mods = lax.dynamic_index_in_dim(mods_all, me, axis=1, keepdims=False).reshape(N_MOD, D)

    groups = dict(gu1=("gu1",), d1=("d1",), mix=("win", "cp", "ap", "out"), ffn2=("gu2", "d2"))
    in_flight = {}
    first = [shard("gu1", mods_token)]
    sems, srcs, lands, token = _tl_start(first, [_own_slot(s, me) for s in first], [[0]],
                                         name="gather_weights_start_gu1")
    in_flight["gu1"] = [sems[0], srcs, lands, None]
    rest = [n for n in BIG if n != "gu1"]
    shards = [shard(n, token) for n in rest]
    rest_groups = [[rest.index(n) for n in names] for g, names in groups.items() if g != "gu1"]
    sems, srcs, lands, rest_token = _tl_start(shards, [_own_slot(s, me) for s in shards], rest_groups,
                                              name="gather_weights_start_rest")
    for (g, names), gsems, idx in zip([kv for kv in groups.items() if kv[0] != "gu1"], sems, rest_groups):
        in_flight[g] = [gsems, [srcs[t] for t in idx], [lands[t] for t in idx], None]

    def forward(group, after):
        sems1, gsrcs, glands, _ = in_flight[group]
        sems2, gsrcs, glands = _tl_forward(gsrcs, glands, sems1, after, name="gather_weights_forward_" + group)
        in_flight[group] = [sems1, gsrcs, glands, sems2]

    forward_early = dict(d1="mix", mix="ffn2")

    tables = _rope_tables(x.shape[1], rest_token)

    def w_get(group, after):
        if group == "gu1":
            after = tables[0]
        if in_flight[group][3] is None:
            forward(group, after)
        sems1, gsrcs, glands, sems2 = in_flight[group]
        landed = _tl_wait(gsrcs, glands, sems1, sems2, after, name="gather_weights_wait_" + group)
        if group in forward_early:
            forward(forward_early[group], landed[0])
        return {n: a.reshape(-1, D) for n, a in zip(groups[group], landed)}

    pending = []

    def g_put(gs):
        names = tuple(gs)
        srcs = [gs[n].reshape(N_DEV, -1, D) for n in names]
        lands = [_own_slot(lax.dynamic_index_in_dim(s, me, axis=0, keepdims=False), me) for s in srcs]
        sems, srcs, lands, tok = _split_start(srcs, lands, [list(range(len(names)))], scatter=True,
                                              name="scatter_grads_start_" + names[0])
        pending.append((names, sems[0], srcs, lands))
        return tok

    lsum, grad_x, small = _local_step(x[0], loss_target[0], mods, g_ffn1, g_mix, g_ffn2, g_final[None],
                                      _pad_to(conv_full, 8, D), sinks[0], w_get, g_put, tables)

    packed = _pack_small(small["mods"], small["g1"], small["gm"], small["g2"], small["gf"], small["convw"],
                         small["sinks"], lsum)
    sm_sems, sm_srcs, sm_lands, sm_token = _split_start([packed], [_own_slot(packed, me)], [[0]], scatter=False,
                                                        name="gather_small_start")

    w_of = dict(ada=w_ada, gu1=w1_gu, d1=w1_down, win=w_in, cp=w_conv_proj, ap=w_attn_proj, out=w_out, gu2=w2_gu,
                d2=w2_down)
    m_of = dict(ada=m_w_ada, gu1=m_w1_gu, d1=m_w1_down, win=m_w_in, cp=m_w_conv_proj, ap=m_w_attn_proj, out=m_w_out,
                gu2=m_w2_gu, d2=m_w2_down)
    v_of = dict(ada=v_w_ada, gu1=v_w1_gu, d1=v_w1_down, win=v_w_in, cp=v_w_conv_proj, ap=v_w_attn_proj, out=v_w_out,
                gu2=v_w2_gu, d2=v_w2_down)
    upd = {}
    after = sm_token
    for k, (names, sems, srcs, lands) in enumerate(pending):
        if k == 2:
            (packed_all,) = _split_wait(sm_srcs, sm_lands, sm_sems[0], after, scatter=False, name="gather_small_wait")
            gsmall = _sum8(packed_all, name="sum_small")
            loss = (0.5 / D) * jnp.sum(gsmall[R_LOSS])
            after = gsmall
        parts = _split_wait(srcs, lands, sems, after, scatter=True, name="scatter_grads_wait_" + names[0])
        for n, p in zip(names, parts):
            if n in TRANSPOSED:
                res = _adam(jnp.swapaxes(w_of[n], 1, 2), p, jnp.swapaxes(m_of[n], 1, 2), jnp.swapaxes(v_of[n], 1, 2),
                            tm=TM_ADAM, name="adam_" + n)
                upd[n] = [jnp.swapaxes(t, 1, 2) for t in res]
            else:
                upd[n] = _adam(w_of[n], p, m_of[n], v_of[n], tm=TM_ADAM, name="adam_" + n)
        after = upd[names[-1]][1]

    gm_cols = lax.dynamic_slice(packed_all[:, R_MODS:R_MODS + N_MOD, :].reshape(N_DEV, N_MOD * D),
                                (0, me * ada_cols), (N_DEV, ada_cols))
    upd["ada"] = _adam(w_ada, _wada_grad(c_all.T, gm_cols, name="ada_dw"), m_w_ada, v_w_ada, tm=256, name="adam_ada")
    conv_g = lax.dynamic_slice(gsmall, (R_CONV, me * conv_cols), (3, conv_cols))

    def natural(b, g1, gm, g2, gf, cw, sk):
        return dict(b_ada=b, g_ffn1=g1, g_mix=gm, g_ffn2=g2, g_final=gf[None], conv_w=cw[0], sinks=sk)

    small_out = _adam_small(gsmall, conv_g, natural(b_ada, g_ffn1, g_mix, g_ffn2, g_final, conv_w, sinks),
                            natural(m_b_ada, m_g_ffn1, m_g_mix, m_g_ffn2, m_g_final, m_conv_w, m_sinks),
                            natural(v_b_ada, v_g_ffn1, v_g_mix, v_g_ffn2, v_g_final, v_conv_w, v_sinks),
                            name="adam_small")
    for res in small_out:
        res["g_final"] = res["g_final"][0]
        res["conv_w"] = res["conv_w"][None]

    big_name = dict(w_ada="ada", w1_gu="gu1", w1_down="d1", w_in="win", w_conv_proj="cp", w_attn_proj="ap",
                    w_out="out", w2_gu="gu2", w2_down="d2")
    order = ("w_ada", "b_ada", "g_ffn1", "w1_gu", "w1_down", "g_mix", "w_in", "conv_w", "w_conv_proj", "w_attn_proj",
             "sinks", "w_out", "g_ffn2", "w2_gu", "w2_down", "g_final")
    outs = [loss, grad_x[None]]
    for kind in range(4):
        for n in order:
            outs.append(upd[big_name[n]][kind] if n in big_name else small_out[kind][n])
    return tuple(outs)
```

```python
import jax
import jax.numpy as jnp
from jax import lax
from jax.experimental import pallas as pl
from jax.experimental.pallas import tpu as pltpu

D = 1024
F = 2816
NIN = 6656
N_HEADS = 16
N_KV = 4
HEAD_DIM = 64
BLK = 128
N_MOD = 9
N_DEV = 8
EPS = 1e-6
NEG_INF = -1e30
ROPE_THETA = 10000.0
O_BG, O_CG, O_U, O_Q, O_K, O_V, O_ZC, O_ZA = 0, 1024, 2048, 3072, 4096, 4352, 4608, 5632

ADAM_LR = 0.001
ADAM_B1 = 0.9
ADAM_B2 = 0.999
ADAM_EPS = 1e-08
ADAM_WD = 0.01
ADAM_STEP = 10

BF = jnp.bfloat16
F32 = jnp.float32
VMEM_LIMIT = 56 * 1024 * 1024
MXU_N = 256
DOWN_BWD_PARTS = 1
MESH = pl.DeviceIdType.MESH

NT = (((1,), (1,)), ((), ()))
TN = (((0,), (0,)), ((), ()))


def _cp(sem=None):
    return pltpu.CompilerParams(dimension_semantics=sem, vmem_limit_bytes=VMEM_LIMIT)


def _tile(n, pref):
    if n <= pref:
        return n
    for t in range(pref - pref % 16, 15, -16):
        if n % t == 0:
            return t
    raise ValueError((n, pref))


def _sigmoid(v):
    return 0.5 * jnp.tanh(0.5 * v) + 0.5


def _row(i):
    return (i, 0)


def _const2(*_):
    return (0, 0)


def _resident(shape):
    return pl.BlockSpec(shape, lambda *_: (0,) * len(shape), pipeline_mode=pl.Buffered(1))


def _norm_proj(x, g, sc, sh, wt, rope=None, *, tm, tn, name):
    T, N = x.shape[0], wt.shape[0]
    tm = _tile(T, tm)

    def body(x_ref, g_ref, sc_ref, sh_ref, w_ref, *rest):
        if rope is None:
            h_ref, o_ref = rest
        else:
            c_ref, s_ref, h_ref, o_ref, qs_ref, kr_ref = rest
        xv = x_ref[...]
        r = lax.rsqrt(jnp.mean(xv * xv, axis=-1, keepdims=True) + EPS)
        hb = ((xv * r) * g_ref[...] * (1.0 + sc_ref[...]) + sh_ref[...]).astype(BF)
        h_ref[...] = hb
        for c0 in range(0, N, tn):
            cols = pl.ds(c0, tn)
            o_ref[:, cols] = lax.dot_general(hb, w_ref[cols, :], NT, preferred_element_type=F32).astype(BF)
            if rope is not None and c0 < O_V <= c0 + tn:
                _attn_prep_tile(o_ref, c_ref, s_ref, qs_ref, kr_ref, tm)

    vec = pl.BlockSpec((1, D), _const2)
    rowspec = pl.BlockSpec((tm, D), _row)
    in_specs = [rowspec, vec, vec, vec, _resident((N, D))]
    out_specs = [rowspec, pl.BlockSpec((tm, N), _row)]
    out_shape = [jax.ShapeDtypeStruct((T, D), BF), jax.ShapeDtypeStruct((T, N), BF)]
    args = [x, g, sc, sh, wt]
    if rope is not None:
        in_specs += [pl.BlockSpec((tm, 128), _row)] * 2
        out_specs += [pl.BlockSpec((N_KV, 4 * tm, 128), lambda i: (0, i, 0)), pl.BlockSpec((tm, 256), _row)]
        out_shape += [jax.ShapeDtypeStruct((N_KV, 4 * T, 128), BF), jax.ShapeDtypeStruct((T, 256), BF)]
        args += list(rope)
    return pl.pallas_call(
        body, name=name, grid=(T // tm,),
        in_specs=in_specs, out_specs=out_specs, out_shape=out_shape,
        compiler_params=_cp(("parallel",)),
    )(*args)


def _ffn_down_fwd(ab, wd, x, gt, *, tm, name):
    T = x.shape[0]
    tm = _tile(T, tm)

    def body(a_ref, b_ref, wd_ref, x_ref, gt_ref, xo_ref):
        y = None
        for c0 in range(0, F, MXU_N):
            cols = pl.ds(c0, MXU_N)
            a = a_ref[:, cols].astype(F32)
            act = (a * _sigmoid(a) * b_ref[:, cols].astype(F32)).astype(BF)
            part = jnp.dot(act, wd_ref[cols, :], preferred_element_type=F32)
            y = part if y is None else y + part
        xo_ref[...] = x_ref[...] + (0.5 * gt_ref[...]) * y

    return pl.pallas_call(
        body, name=name, grid=(T // tm,),
        in_specs=[pl.BlockSpec((tm, F), lambda i: (i, 0)), pl.BlockSpec((tm, F), lambda i: (i, 1)),
                  _resident((F, D)), pl.BlockSpec((tm, D), _row), pl.BlockSpec((1, D), _const2)],
        out_specs=pl.BlockSpec((tm, D), _row),
        out_shape=jax.ShapeDtypeStruct((T, D), F32),
        compiler_params=_cp(("parallel",)),
    )(ab, ab, wd, x, gt)


def _ffn_fwd(x, g, sc, sh, gt, wgu, wd, final, *, tm, name):
    T = x.shape[0]
    tm = _tile(T, tm)
    last = final is not None

    def body(x_ref, g_ref, sc_ref, sh_ref, gt_ref, wgu_ref, wd_ref, *rest):
        if last:
            t_ref, gf_ref, h_ref, ab_ref, dx_ref, ls_ref, dgf_ref = rest
        else:
            h_ref, ab_ref, xo_ref = rest
        xv = x_ref[...]
        r = lax.rsqrt(jnp.mean(xv * xv, axis=-1, keepdims=True) + EPS)
        hb = ((xv * r) * g_ref[...] * (1.0 + sc_ref[...]) + sh_ref[...]).astype(BF)
        h_ref[...] = hb
        y = None
        for c0 in range(0, F, MXU_N):
            a = lax.dot_general(hb, wgu_ref[pl.ds(c0, MXU_N), :], NT, preferred_element_type=F32)
            b = lax.dot_general(hb, wgu_ref[pl.ds(F + c0, MXU_N), :], NT, preferred_element_type=F32)
            ab = a.astype(BF)
            bb = b.astype(BF)
            ab_ref[:, pl.ds(c0, MXU_N)] = ab
            ab_ref[:, pl.ds(F + c0, MXU_N)] = bb
            a = ab.astype(F32)
            act = (a * _sigmoid(a) * bb.astype(F32)).astype(BF)
            part = jnp.dot(act, wd_ref[pl.ds(c0, MXU_N), :], preferred_element_type=F32)
            y = part if y is None else y + part
        xo = xv + (0.5 * gt_ref[...]) * y
        if not last:
            xo_ref[...] = xo
            return

        @pl.when(pl.program_id(0) == 0)
        def _():
            ls_ref[...] = jnp.zeros_like(ls_ref)
            dgf_ref[...] = jnp.zeros_like(dgf_ref)
        gv = gf_ref[...]
        r = lax.rsqrt(jnp.mean(xo * xo, axis=-1, keepdims=True) + EPS)
        xh = xo * r
        e = xh * gv - t_ref[...]
        ls_ref[...] += jnp.sum(e * e, axis=0, keepdims=True)
        dy = e * (1.0 / D)
        dgf_ref[...] += jnp.sum(dy * xh, axis=0, keepdims=True)
        dxh = dy * gv
        dx_ref[...] = r * (dxh - xh * jnp.mean(dxh * xh, axis=-1, keepdims=True))

    vec = pl.BlockSpec((1, D), _const2)
    rowspec = pl.BlockSpec((tm, D), _row)
    in_specs = [rowspec, vec, vec, vec, vec, _resident((2 * F, D)), _resident((F, D))]
    out_specs = [rowspec, pl.BlockSpec((tm, 2 * F), _row), rowspec]
    out_shape = [jax.ShapeDtypeStruct((T, D), BF), jax.ShapeDtypeStruct((T, 2 * F), BF),
                 jax.ShapeDtypeStruct((T, D), F32)]
    args = [x, g, sc, sh, gt, wgu, wd]
    if last:
        in_specs += [rowspec, vec]
        out_specs += [vec, vec]
        out_shape += [jax.ShapeDtypeStruct((1, D), F32)] * 2
        args += list(final)
    return pl.pallas_call(
        body, name=name, grid=(T // tm,),
        in_specs=in_specs, out_specs=out_specs, out_shape=out_shape,
        compiler_params=_cp(("arbitrary",) if last else ("parallel",)),
    )(*args)


def _ffn_down_bwd_dw(dxo, gt, ab, wd, *, tm, name):
    T = dxo.shape[0]
    tm = _tile(T, tm)
    nt = T // tm
    nh = DOWN_BWD_PARTS
    hw = F // nh
    chunks = [(c0, min(MXU_N, hw - c0)) for c0 in range(0, hw, MXU_N)]

    def body(dxo_ref, gt_ref, a_ref, b_ref, wd_ref, dab_ref, dgt_ref, dwd_ref, dys, dyt, acc, stage, sem):
        i, j = pl.program_id(0), pl.program_id(1)

        @pl.when(jnp.logical_and(i == 0, j == 0))
        def _():
            dgt_ref[...] = jnp.zeros_like(dgt_ref)

        @pl.when(j == 0)
        def _():
            dxv = dxo_ref[...]
            dys[...] = ((0.5 * gt_ref[...]) * dxv).astype(BF)
            dyt[...] = dxv.T.astype(BF)

        def half(jj):
            @pl.when(i == 0)
            def _():
                acc[jj] = jnp.zeros((D, hw), F32)

            dy = dys[...]
            dy_t = dyt[...]
            for c0, cw in chunks:
                cols = pl.ds(c0, cw)
                dact = lax.dot_general(dy, wd_ref[pl.ds(jj * hw + c0, cw), :], NT, preferred_element_type=F32)
                a = a_ref[:, cols].astype(F32)
                b = b_ref[:, cols].astype(F32)
                s = _sigmoid(a)
                silu = a * s
                dab_ref[0, :, cols] = (dact * b * (s * (1.0 + a * (1.0 - s)))).astype(BF)
                dab_ref[1, :, cols] = (dact * silu).astype(BF)
                acc[jj, :, cols] += jnp.dot(dy_t, (silu * b).astype(BF), preferred_element_type=F32)

            @pl.when(i == nt - 1)
            def _():
                half_gt = 0.5 * gt_ref[...]
                for c0, cw in chunks:
                    g_rows = acc[jj, :, pl.ds(c0, cw)].T
                    w_rows = wd_ref[pl.ds(jj * hw + c0, cw), :].astype(F32)
                    dgt_ref[...] += 0.5 * jnp.sum(g_rows * w_rows, axis=0, keepdims=True)
                    stage[0:cw, :] = (g_rows * half_gt).astype(BF)
                    out = pltpu.make_async_copy(stage.at[pl.ds(0, cw)], dwd_ref.at[pl.ds(jj * hw + c0, cw)], sem)
                    out.start()
                    out.wait()

        for jj in range(nh):
            pl.when(j == jj)(lambda jj=jj: half(jj))

    vec = pl.BlockSpec((1, D), _const2)
    rowspec = pl.BlockSpec((tm, D), lambda i, j: (i, 0))
    return pl.pallas_call(
        body, name=name, grid=(nt, nh),
        in_specs=[rowspec, vec, pl.BlockSpec((tm, hw), lambda i, j: (i, j)),
                  pl.BlockSpec((tm, hw), lambda i, j: (i, j + nh)), _resident((F, D))],
        out_specs=[pl.BlockSpec((2, tm, hw), lambda i, j: (0, i, j)), vec, pl.BlockSpec(memory_space=pl.ANY)],
        out_shape=[jax.ShapeDtypeStruct((2, T, F), BF), jax.ShapeDtypeStruct((1, D), F32),
                   jax.ShapeDtypeStruct((F, D), BF)],
        scratch_shapes=[pltpu.VMEM((tm, D), BF), pltpu.VMEM((D, tm), BF), pltpu.VMEM((nh, D, hw), F32),
                        pltpu.VMEM((MXU_N, D), BF), pltpu.SemaphoreType.DMA(())],
        compiler_params=_cp(("arbitrary", "arbitrary")),
    )(dxo, gt, ab, ab, wd)


def _tn_matmul(a, b, token=None, *, tn, tk, name):
    S, T, Ns = a.shape
    tn, tk = _tile(Ns, tn), _tile(T, tk)
    nk, njs = T // tk, Ns // tn
    deps = [] if token is None else [token]

    def body(a_ref, b_ref, *rest):
        o_ref, acc = rest[len(deps):]
        k = pl.program_id(1)

        @pl.when(k == 0)
        def _():
            acc[...] = jnp.zeros_like(acc)
        acc[...] += lax.dot_general(a_ref[0], b_ref[...], TN, preferred_element_type=F32)

        @pl.when(k == nk - 1)
        def _():
            o_ref[...] = acc[...].astype(BF)

    return pl.pallas_call(
        body, name=name, grid=(S * njs, nk),
        in_specs=[pl.BlockSpec((1, tk, tn), lambda j, k: (j // njs, k, j % njs)),
                  pl.BlockSpec((tk, D), lambda j, k: (k, 0))] + [pl.BlockSpec(memory_space=pl.ANY)] * len(deps),
        out_specs=pl.BlockSpec((tn, D), lambda j, k: (j, 0)),
        out_shape=jax.ShapeDtypeStruct((S * Ns, D), BF),
        scratch_shapes=[pltpu.VMEM((tn, D), F32)],
        compiler_params=_cp(("parallel", "arbitrary")),
    )(a, b, *deps)


def _nn_bwd_norm(da, w, x, g, sc, dxo, *, tm, name):
    S, T, Ks = da.shape
    tm = _tile(T, tm)
    rc = _tile(tm, 256)

    def body(da_ref, w_ref, x_ref, g_ref, sc_ref, dxo_ref, dx_ref, dsh_ref, dsc_ref, dg_ref, acc):
        @pl.when(pl.program_id(0) == 0)
        def _():
            dsh_ref[...] = jnp.zeros_like(dsh_ref)
            dsc_ref[...] = jnp.zeros_like(dsc_ref)
            dg_ref[...] = jnp.zeros_like(dg_ref)

        d = jnp.dot(da_ref[0], w_ref[0:Ks, :], preferred_element_type=F32)
        for s in range(1, S):
            d = d + jnp.dot(da_ref[s], w_ref[s * Ks:(s + 1) * Ks, :], preferred_element_type=F32)
        acc[...] = d
        gv = g_ref[...]
        sc1 = 1.0 + sc_ref[...]
        dsh = jnp.zeros((1, D), F32)
        dsc = jnp.zeros((1, D), F32)
        dg = jnp.zeros((1, D), F32)
        for r0 in range(0, tm, rc):
            rows = pl.ds(r0, rc)
            u = acc[rows, :]
            xv = x_ref[rows, :]
            r = lax.rsqrt(jnp.mean(xv * xv, axis=-1, keepdims=True) + EPS)
            xh = xv * r
            dsh = dsh + jnp.sum(u, axis=0, keepdims=True)
            dsc = dsc + jnp.sum(u * (xh * gv), axis=0, keepdims=True)
            us = u * sc1
            dg = dg + jnp.sum(us * xh, axis=0, keepdims=True)
            dxh = us * gv
            dx_ref[rows, :] = dxo_ref[rows, :] + r * (dxh - xh * jnp.mean(dxh * xh, axis=-1, keepdims=True))
        dsh_ref[...] += dsh
        dsc_ref[...] += dsc
        dg_ref[...] += dg

    vec = pl.BlockSpec((1, D), _const2)
    rowspec = pl.BlockSpec((tm, D), _row)
    return pl.pallas_call(
        body, name=name, grid=(T // tm,),
        in_specs=[pl.BlockSpec((S, tm, Ks), lambda i: (0, i, 0)), _resident((S * Ks, D)), rowspec, vec, vec, rowspec],
        out_specs=[rowspec, vec, vec, vec],
        out_shape=[jax.ShapeDtypeStruct((T, D), F32)] + [jax.ShapeDtypeStruct((1, D), F32)] * 3,
        scratch_shapes=[pltpu.VMEM((tm, D), F32)],
        compiler_params=_cp(("arbitrary",)),
    )(da, w, x, g, sc, dxo)


def _rope(t, cos, sin_signed, lt32, inverse=False):
    sel = jnp.where(lt32, pltpu.roll(t, 96, 1), pltpu.roll(t, 32, 1))
    return t * cos - sel * sin_signed if inverse else t * cos + sel * sin_signed


def _rope_tables(T, token=None):
    inv = 1.0 / (ROPE_THETA ** (jnp.arange(0, HEAD_DIM, 2, dtype=F32) / HEAD_DIM))
    ang = _behind(jnp.arange(T, dtype=F32)[:, None] * inv[None, :], token)
    cos, sin = jnp.cos(ang), jnp.sin(ang)
    cos128 = jnp.tile(cos, (1, 4))
    sin128 = jnp.tile(jnp.concatenate([-sin, sin], axis=1), (1, 2))
    return cos128, sin128


QSCALE = HEAD_DIM ** -0.5


def _lane_masks(rows):
    lane = lax.broadcasted_iota(jnp.int32, (rows, 128), 1)
    return (lane % HEAD_DIM) < (HEAD_DIM // 2), [lane < HEAD_DIM, lane >= HEAD_DIM]


def _attn_bias():
    qi = lax.broadcasted_iota(jnp.int32, (4 * BLK, 2 * BLK), 0) % BLK
    kj = lax.broadcasted_iota(jnp.int32, (4 * BLK, 2 * BLK), 1)
    band = (kj > qi) & (kj <= qi + BLK)
    return jnp.stack([jnp.where(band & (kj >= BLK), 0.0, NEG_INF), jnp.where(band, 0.0, NEG_INF)]).astype(F32)


def _attn_prep_tile(proj_ref, c_ref, s_ref, qs_ref, kr_ref, tm):
    lt32, halves = _lane_masks(BLK)
    for b in range(tm // BLK):
        rows = pl.ds(b * BLK, BLK)
        cc, sc = c_ref[rows, :], s_ref[rows, :]
        qr = [_rope(proj_ref[rows, pl.ds(O_Q + p * 128, 128)].astype(F32), cc, sc, lt32) * QSCALE for p in range(8)]
        for g in range(N_KV):
            qs_ref[g, pl.ds(4 * b * BLK, 4 * BLK), :] = _stack_heads(qr, g, halves).astype(BF)
        kr_ref[rows, :] = jnp.concatenate([_rope(proj_ref[rows, pl.ds(O_K + r * 128, 128)].astype(F32), cc, sc, lt32)
                                           for r in range(2)], axis=1).astype(BF)


ATT_BPS = 4
ATT_ROWS = ATT_BPS * BLK


def _before(n):
    return jnp.maximum(ATT_BPS * n - 1, 0)


def _attn_specs():
    return [pl.BlockSpec((N_KV, 4 * ATT_ROWS, 128), lambda n: (0, n, 0)),
            pl.BlockSpec((ATT_ROWS, 256), _row), pl.BlockSpec((BLK, 256), lambda n: (_before(n), 0)),
            pl.BlockSpec((ATT_ROWS, 256), lambda n: (n, O_V // 256)),
            pl.BlockSpec((BLK, 256), lambda n: (_before(n), O_V // 256)),
            pl.BlockSpec((2, 4 * BLK, 2 * BLK), lambda n: (0, 0, 0)),
            pl.BlockSpec(memory_space=pltpu.SMEM)]


def _bands(sb, kc_ref, kp_ref, vc_ref, vp_ref):
    own = pl.ds(sb * BLK, BLK)
    above = pl.ds((sb - 1) * BLK, BLK)
    kb, vb = [], []
    for r in range(2):
        cols = pl.ds(r * 128, 128)
        kprev = kp_ref[:, cols] if sb == 0 else kc_ref[above, cols]
        vprev = vp_ref[:, cols] if sb == 0 else vc_ref[above, cols]
        kb.append(jnp.concatenate([kprev, kc_ref[own, cols]], axis=0))
        vb.append(jnp.concatenate([vprev, vc_ref[own, cols]], axis=0))
    return kb, vb


def _block_bias(sb, bias_ref):
    return bias_ref[jnp.minimum(pl.program_id(0), 1)] if sb == 0 else bias_ref[1]


def _sink_rows(sink_ref, g):
    return jnp.concatenate([jnp.full((BLK, 128), sink_ref[4 * g + hh], F32) for hh in range(4)], axis=0)


def _both(t):
    return jnp.concatenate([t, t], axis=1)


def _unstack_heads(t, g, halves, acc):
    half = g % 2
    for hh in range(4):
        h = 4 * g + hh
        th = jnp.where(halves[half], t[hh * BLK:(hh + 1) * BLK], 0.0)
        if h % 2 != half:
            th = pltpu.roll(th, HEAD_DIM, 1)
        acc[h // 2] = acc[h // 2] + th


def _stack_heads(chunks, g, halves):
    half = g % 2
    parts = []
    for hh in range(4):
        h = 4 * g + hh
        t = chunks[h // 2]
        if h % 2 != half:
            t = pltpu.roll(t, HEAD_DIM, 1)
        parts.append(jnp.where(halves[half], t, 0.0))
    return jnp.concatenate(parts, axis=0)


def _attn_fwd(qs, kr, proj, bias, sinks, *, name):
    T = proj.shape[0]
    assert T % ATT_ROWS == 0

    def body(qs_ref, kc_ref, kp_ref, vc_ref, vp_ref, bias_ref, sink_ref, o_ref, lse_ref):
        _, h128 = _lane_masks(BLK)
        _, h256 = _lane_masks(2 * BLK)
        _, h512 = _lane_masks(4 * BLK)
        groups = range(N_KV)
        sink = [_sink_rows(sink_ref, g) for g in groups]
        for sb in range(ATT_BPS):
            rows = pl.ds(4 * sb * BLK, 4 * BLK)
            kb, vb = _bands(sb, kc_ref, kp_ref, vc_ref, vp_ref)
            outs = [jnp.zeros((BLK, 128), F32) for _ in range(8)]
            bias = _block_bias(sb, bias_ref)
            s = [lax.dot_general(qs_ref[g, rows, :], kb[g // 2], NT, preferred_element_type=F32) + bias for g in groups]
            m = [jnp.maximum(jnp.broadcast_to(jnp.max(s[g], axis=-1, keepdims=True), (4 * BLK, 128)), sink[g])
                 for g in groups]
            p = [jnp.exp(s[g] - _both(m[g])).astype(BF) for g in groups]
            vg = [jnp.where(h256[g % 2], vb[g // 2].astype(F32), 1.0).astype(BF) for g in groups]
            o = [jnp.dot(p[g], vg[g], preferred_element_type=F32) for g in groups]
            denom = [jnp.where(h512[g % 2], pltpu.roll(o[g], HEAD_DIM, 1), o[g]) + jnp.exp(sink[g] - m[g])
                     for g in groups]
            for g in groups:
                lse_ref[g, rows, :] = m[g] + jnp.log(denom[g])
                _unstack_heads(o[g] * (1.0 / denom[g]), g, h128, outs)
            o_ref[pl.ds(sb * BLK, BLK), :] = jnp.concatenate(outs, axis=1).astype(BF)

    return pl.pallas_call(
        body, name=name, grid=(T // ATT_ROWS,),
        in_specs=_attn_specs(),
        out_specs=[pl.BlockSpec((ATT_ROWS, D), _row), pl.BlockSpec((N_KV, 4 * ATT_ROWS, 128), lambda n: (0, n, 0))],
        out_shape=[jax.ShapeDtypeStruct((T, D), BF), jax.ShapeDtypeStruct((N_KV, 4 * T, 128), F32)],
        compiler_params=_cp(("parallel",)),
    )(qs, kr, kr, proj, proj, bias, sinks)


def _attn_bwd(qs, kr, proj, bias, sinks, lse, o, do, cos, sin, dproj, *, name):
    T = proj.shape[0]
    assert T % ATT_ROWS == 0

    def body(qs_ref, kc_ref, kp_ref, vc_ref, vp_ref, bias_ref, sink_ref, lse_ref, o_ref, do_ref,
             cc_ref, sc_ref, cp_ref, sp_ref, dproj_ref, dq_ref, dkc_ref, dkp_ref, dvc_ref, dvp_ref, dsink_ref):
        @pl.when(pl.program_id(0) == 0)
        def _():
            dsink_ref[...] = jnp.zeros_like(dsink_ref)
        lt32, h128 = _lane_masks(BLK)
        lane1 = lax.broadcasted_iota(jnp.int32, (1, 128), 1)
        dsink = jnp.zeros((1, 128), F32)
        groups = range(N_KV)
        for sb in range(ATT_BPS):
            own = pl.ds(sb * BLK, BLK)
            rows = pl.ds(4 * sb * BLK, 4 * BLK)
            kb, vb = _bands(sb, kc_ref, kp_ref, vc_ref, vp_ref)
            oc = [o_ref[own, pl.ds(p * 128, 128)].astype(F32) for p in range(8)]
            doc = [do_ref[own, pl.ds(p * 128, 128)].astype(F32) for p in range(8)]
            dqs = [jnp.zeros((BLK, 128), F32) for _ in range(8)]
            bias = _block_bias(sb, bias_ref)
            q = [qs_ref[g, rows, :] for g in groups]
            lse_g = [lse_ref[g, rows, :] for g in groups]
            s = [lax.dot_general(q[g], kb[g // 2], NT, preferred_element_type=F32) + bias for g in groups]
            dos = [_stack_heads(doc, g, h128) for g in groups]
            dosb = [t.astype(BF) for t in dos]
            dp = [lax.dot_general(dosb[g], vb[g // 2], NT, preferred_element_type=F32) for g in groups]
            delta = [jnp.broadcast_to(jnp.sum(dos[g] * _stack_heads(oc, g, h128), axis=-1, keepdims=True),
                                      (4 * BLK, 128)) for g in groups]
            p = [jnp.exp(s[g] - _both(lse_g[g])) for g in groups]
            ds = [(p[g] * (dp[g] - _both(delta[g]))).astype(BF) for g in groups]
            pb = [t.astype(BF) for t in p]
            dvg = [lax.dot_general(pb[g], dosb[g], TN, preferred_element_type=F32) for g in groups]
            dkg = [lax.dot_general(ds[g], q[g], TN, preferred_element_type=F32) for g in groups]
            dqg = [jnp.dot(ds[g], kb[g // 2], preferred_element_type=F32) * QSCALE for g in groups]
            dvr = [dvg[0] + dvg[1], dvg[2] + dvg[3]]
            dkr = [dkg[0] + dkg[1], dkg[2] + dkg[3]]
            for g in groups:
                _unstack_heads(dqg[g], g, h128, dqs)
                dsk = -jnp.exp(_sink_rows(sink_ref, g) - lse_g[g]) * delta[g]
                for hh in range(4):
                    val = jnp.sum(dsk[hh * BLK:(hh + 1) * BLK], axis=0, keepdims=True)
                    dsink = dsink + jnp.where(lane1 == 4 * g + hh, val, 0.0)
            cc, sc = cc_ref[own, :], sc_ref[own, :]
            cp, sp = (cp_ref[...], sp_ref[...]) if sb == 0 else (cc_ref[pl.ds((sb - 1) * BLK, BLK), :],
                                                                  sc_ref[pl.ds((sb - 1) * BLK, BLK), :])
            dq_ref[own, :] = jnp.concatenate([_rope(t, cc, sc, lt32, inverse=True) for t in dqs], axis=1).astype(BF)
            dkp_ref[own, :] = jnp.concatenate([_rope(t[:BLK], cp, sp, lt32, inverse=True) for t in dkr], axis=1)
            dkc_ref[own, :] = jnp.concatenate([_rope(t[BLK:], cc, sc, lt32, inverse=True) for t in dkr], axis=1)
            dvp_ref[own, :] = jnp.concatenate([t[:BLK] for t in dvr], axis=1)
            dvc_ref[own, :] = jnp.concatenate([t[BLK:] for t in dvr], axis=1)
        dsink_ref[...] += dsink

    kv = pl.BlockSpec((ATT_ROWS, 256), _row)
    tc = pl.BlockSpec((ATT_ROWS, 128), _row)
    tp = pl.BlockSpec((BLK, 128), lambda n: (_before(n), 0))
    return pl.pallas_call(
        body, name=name, grid=(T // ATT_ROWS,),
        in_specs=_attn_specs() + [pl.BlockSpec((N_KV, 4 * ATT_ROWS, 128), lambda n: (0, n, 0)),
                                  pl.BlockSpec((ATT_ROWS, D), _row), pl.BlockSpec((ATT_ROWS, D), _row), tc, tc, tp, tp,
                                  pl.BlockSpec(memory_space=pl.ANY)],
        out_specs=[pl.BlockSpec((ATT_ROWS, D), lambda n: (n, O_Q // D)), kv, kv, kv, kv,
                   pl.BlockSpec((1, 128), _const2)],
        out_shape=[jax.ShapeDtypeStruct(dproj.shape, BF)] + [jax.ShapeDtypeStruct((T, 256), F32)] * 4
        + [jax.ShapeDtypeStruct((1, 128), F32)],
        input_output_aliases={14: 0},
        compiler_params=_cp(("arbitrary",)),
    )(qs, kr, kr, proj, proj, bias, sinks, lse, o, do, cos, sin, cos, sin, dproj)


def _dkv_combine(dkc, dkp, dvc, dvp, dproj, *, name):
    T = dkc.shape[0]
    nb = T // BLK
    tm = _tile(T, 4 * BLK)
    bpt = tm // BLK
    nt = T // tm

    def body(dkc_ref, dkp_ref, dkn_ref, dvc_ref, dvp_ref, dvn_ref, dproj_ref, o_ref):
        keep = jnp.where(pl.program_id(0) == nt - 1, 0.0, 1.0)

        def shifted(prev_ref, next_ref):
            nxt = keep * next_ref[...]
            return nxt if bpt == 1 else jnp.concatenate([prev_ref[BLK:, :], nxt], axis=0)

        o_ref[:, 0:256] = (dkc_ref[...] + shifted(dkp_ref, dkn_ref)).astype(BF)
        o_ref[:, 256:512] = (dvc_ref[...] + shifted(dvp_ref, dvn_ref)).astype(BF)

    cur = pl.BlockSpec((tm, 256), _row)
    nxt = pl.BlockSpec((BLK, 256), lambda i: (jnp.minimum((i + 1) * bpt, nb - 1), 0))
    return pl.pallas_call(
        body, name=name, grid=(nt,),
        in_specs=[cur, cur, nxt, cur, cur, nxt, pl.BlockSpec(memory_space=pl.ANY)],
        out_specs=pl.BlockSpec((tm, 512), lambda i: (i, O_K // 512)),
        out_shape=jax.ShapeDtypeStruct(dproj.shape, BF),
        input_output_aliases={6: 0},
        compiler_params=_cp(("parallel",)),
    )(dkc, dkp, dkp, dvc, dvp, dvp, dproj)


HALO = 16


def _conv_shifts(cu, hprev, tm):
    row = lax.broadcasted_iota(jnp.int32, (8, cu.shape[1]), 0)
    h1 = hprev[HALO - 1:HALO, :]
    h2 = hprev[HALO - 2:HALO - 1, :]
    m1 = pltpu.roll(cu, 1, 0)
    m2 = pltpu.roll(cu, 2, 0)
    m1 = jnp.concatenate([jnp.where(row == 0, h1, m1[0:8]), m1[8:]], axis=0)
    m2 = jnp.concatenate([jnp.where(row == 0, h2, jnp.where(row == 1, h1, m2[0:8])), m2[8:]], axis=0)
    return m1, m2


def _mixer_mid_fwd(proj, attn, wcp, wap, wout, convw, x, gt, *, tm, name):
    T = x.shape[0]
    tm = _tile(T, tm)
    hb = tm // HALO

    def body(bg_ref, cg_ref, u_ref, hcg_ref, hu_ref, zc0_ref, zc1_ref, za0_ref, za1_ref, at_ref,
             wcp_ref, wap_ref, wout_ref, cw_ref, x_ref, gt_ref,
             x2_ref, gc_ref, yc_ref, ya_ref, mg_ref, o_ref):
        first = jnp.where(pl.program_id(0) == 0, 0.0, 1.0)
        cu = cg_ref[...].astype(F32) * u_ref[...].astype(F32)
        hprev = first * (hcg_ref[...].astype(F32) * hu_ref[...].astype(F32))
        m1, m2 = _conv_shifts(cu, hprev, tm)
        cv = cw_ref[0:1, :] * m2 + cw_ref[1:2, :] * m1 + cw_ref[2:3, :] * cu
        gc = (bg_ref[...].astype(F32) * cv).astype(BF)
        gc_ref[...] = gc
        yc = jnp.dot(gc, wcp_ref[...], preferred_element_type=F32)
        ya = jnp.dot(at_ref[...], wap_ref[...], preferred_element_type=F32)
        yc_ref[...] = yc.astype(BF)
        ya_ref[...] = ya.astype(BF)
        zc = jnp.concatenate([zc0_ref[...], zc1_ref[...]], axis=1).astype(F32)
        za = jnp.concatenate([za0_ref[...], za1_ref[...]], axis=1).astype(F32)
        mg = (_sigmoid(zc) * yc + _sigmoid(za) * ya).astype(BF)
        mg_ref[...] = mg
        o = jnp.dot(mg, wout_ref[...], preferred_element_type=F32)
        o_ref[...] = o.astype(BF)
        x2_ref[...] = x_ref[...] + gt_ref[...] * o

    wspec = pl.BlockSpec((D, D), _const2)
    rowspec = pl.BlockSpec((tm, D), _row)
    return pl.pallas_call(
        body, name=name, grid=(T // tm,),
        in_specs=[_col(tm, O_BG), _col(tm, O_CG), _col(tm, O_U), _halo_prev(hb, O_CG), _halo_prev(hb, O_U),
                  _col(tm, O_ZC, 512), _col(tm, O_ZC + 512, 512), _col(tm, O_ZA, 512), _col(tm, O_ZA + 512, 512),
                  rowspec, wspec, wspec, wspec, pl.BlockSpec((8, D), _const2), rowspec, pl.BlockSpec((1, D), _const2)],
        out_specs=[rowspec] * 6,
        out_shape=[jax.ShapeDtypeStruct((T, D), F32)] + [jax.ShapeDtypeStruct((T, D), BF)] * 5,
        compiler_params=_cp(("parallel",)),
    )(proj, proj, proj, proj, proj, proj, proj, proj, proj, attn, wcp, wap, wout, convw, x, gt)


def _col(tm, c, w=D):
    assert c % w == 0
    return pl.BlockSpec((tm, w), lambda i: (i, c // w))


def _halo_prev(hb, c):
    return pl.BlockSpec((HALO, D), lambda i: (jnp.maximum(i * hb - 1, 0), c // D))


def _halo_next(hb, nblk, c=0):
    return pl.BlockSpec((HALO, D), lambda i: (jnp.minimum((i + 1) * hb, nblk - 1), c // D))


def _mixer_mid_bwd(dx2, gt, o, proj, yc, ya, wout, wcp, wap, *, tm, name):
    T = dx2.shape[0]
    tm = _tile(T, tm)
    nt = T // tm

    def body(dx_ref, gt_ref, o_ref, zc0_ref, zc1_ref, za0_ref, za1_ref, yc_ref, ya_ref, wout_ref, wcp_ref, wap_ref,
             dout_ref, dyc_ref, dya_ref, dgc_ref, dat_ref, dproj_ref, dgt_ref, dzs, sems):
        i = pl.program_id(0)
        slot = lax.rem(i, 2)

        def slab_copy(step, s):
            return pltpu.make_async_copy(
                dzs.at[s], dproj_ref.at[pl.ds(pl.multiple_of(step * tm, tm), tm), pl.ds(O_ZC, 2 * D)], sems.at[s])

        @pl.when(i == 0)
        def _():
            dgt_ref[...] = jnp.zeros_like(dgt_ref)

        dxv = dx_ref[...]
        dgt_ref[...] += jnp.sum(dxv * o_ref[...].astype(F32), axis=0, keepdims=True)
        dout = (gt_ref[...] * dxv).astype(BF)
        dout_ref[...] = dout
        dmg = lax.dot_general(dout, wout_ref[...], NT, preferred_element_type=F32)
        sc = _sigmoid(jnp.concatenate([zc0_ref[...], zc1_ref[...]], axis=1).astype(F32))
        sa = _sigmoid(jnp.concatenate([za0_ref[...], za1_ref[...]], axis=1).astype(F32))
        dyc = (dmg * sc).astype(BF)
        dya = (dmg * sa).astype(BF)
        dyc_ref[...] = dyc
        dya_ref[...] = dya
        dzs[slot, :, 0:D] = (dmg * yc_ref[...].astype(F32) * (sc * (1.0 - sc))).astype(BF)
        dzs[slot, :, D:2 * D] = (dmg * ya_ref[...].astype(F32) * (sa * (1.0 - sa))).astype(BF)
        slab_copy(i, slot).start()
        dgc_ref[...] = lax.dot_general(dyc, wcp_ref[...], NT, preferred_element_type=F32).astype(BF)
        dat_ref[...] = lax.dot_general(dya, wap_ref[...], NT, preferred_element_type=F32).astype(BF)

        @pl.when(i > 0)
        def _():
            slab_copy(i - 1, 1 - slot).wait()

        @pl.when(i == nt - 1)
        def _():
            slab_copy(i, slot).wait()

    def zcol(c):
        return pl.BlockSpec((tm, 512), lambda i: (i, c // 512))

    wspec = pl.BlockSpec((D, D), _const2)
    rowspec = pl.BlockSpec((tm, D), _row)
    vec = pl.BlockSpec((1, D), _const2)
    return pl.pallas_call(
        body, name=name, grid=(nt,),
        in_specs=[rowspec, vec, rowspec, zcol(O_ZC), zcol(O_ZC + 512), zcol(O_ZA), zcol(O_ZA + 512),
                  rowspec, rowspec, wspec, wspec, wspec],
        out_specs=[rowspec] * 5 + [pl.BlockSpec(memory_space=pl.ANY), vec],
        out_shape=[jax.ShapeDtypeStruct((T, D), BF)] * 5 + [jax.ShapeDtypeStruct((T, NIN), BF),
                                                            jax.ShapeDtypeStruct((1, D), F32)],
        scratch_shapes=[pltpu.VMEM((2, tm, 2 * D), BF), pltpu.SemaphoreType.DMA((2,))],
        compiler_params=_cp(("arbitrary",)),
    )(dx2, gt, o, proj, proj, proj, proj, yc, ya, wout, wcp, wap)


def _conv_bwd(dgc, proj, convw, dproj, *, tm, name):
    T = dgc.shape[0]
    tm = _tile(T, tm)
    hb = tm // HALO
    nblk = T // HALO
    nt = T // tm

    def body(dgc_ref, ndgc_ref, bg_ref, nbg_ref, cg_ref, u_ref, hcg_ref, hu_ref, cw_ref, dproj_ref, dp_ref, dcw_ref):
        i = pl.program_id(0)

        @pl.when(i == 0)
        def _():
            dcw_ref[...] = jnp.zeros_like(dcw_ref)
        first = jnp.where(i == 0, 0.0, 1.0)
        last = jnp.where(i == nt - 1, 0.0, 1.0)
        cg = cg_ref[...].astype(F32)
        u = u_ref[...].astype(F32)
        bg = bg_ref[...].astype(F32)
        dg = dgc_ref[...].astype(F32)
        cu = cg * u
        hprev = first * (hcg_ref[...].astype(F32) * hu_ref[...].astype(F32))
        m1, m2 = _conv_shifts(cu, hprev, tm)
        w0, w1, w2 = cw_ref[0:1, :], cw_ref[1:2, :], cw_ref[2:3, :]
        cv = w0 * m2 + w1 * m1 + w2 * cu
        dcv = dg * bg
        nxt = last * (ndgc_ref[...].astype(F32) * nbg_ref[...].astype(F32))
        n0, n1 = nxt[0:1, :], nxt[1:2, :]
        row = lax.broadcasted_iota(jnp.int32, (8, D), 0)
        p1 = pltpu.roll(dcv, tm - 1, 0)
        p2 = pltpu.roll(dcv, tm - 2, 0)
        p1 = jnp.concatenate([p1[:tm - 8], jnp.where(row == 7, n0, p1[tm - 8:])], axis=0)
        p2 = jnp.concatenate([p2[:tm - 8], jnp.where(row == 7, n1, jnp.where(row == 6, n0, p2[tm - 8:]))], axis=0)
        dcu = w2 * dcv + w1 * p1 + w0 * p2
        dp_ref[:, 0:D] = (dg * cv).astype(BF)
        dp_ref[:, D:2 * D] = (dcu * u).astype(BF)
        dp_ref[:, 2 * D:3 * D] = (dcu * cg).astype(BF)
        dcw_ref[0:1, :] += jnp.sum(dcv * m2, axis=0, keepdims=True)
        dcw_ref[1:2, :] += jnp.sum(dcv * m1, axis=0, keepdims=True)
        dcw_ref[2:3, :] += jnp.sum(dcv * cu, axis=0, keepdims=True)

    rowspec = pl.BlockSpec((tm, D), _row)
    cw = pl.BlockSpec((8, D), _const2)
    return pl.pallas_call(
        body, name=name, grid=(nt,),
        in_specs=[rowspec, _halo_next(hb, nblk), _col(tm, O_BG), _halo_next(hb, nblk, O_BG),
                  _col(tm, O_CG), _col(tm, O_U), _halo_prev(hb, O_CG), _halo_prev(hb, O_U), cw,
                  pl.BlockSpec(memory_space=pl.ANY)],
        out_specs=[pl.BlockSpec((tm, 3 * D), _row), cw],
        out_shape=[jax.ShapeDtypeStruct(dproj.shape, BF), jax.ShapeDtypeStruct((8, D), F32)],
        input_output_aliases={9: 0},
        compiler_params=_cp(("arbitrary",)),
    )(dgc, dgc, proj, proj, proj, proj, proj, proj, convw, dproj)


def _adam_math(w, g, m, v):
    nm = ADAM_B1 * m + (1.0 - ADAM_B1) * g
    nv = ADAM_B2 * v + (1.0 - ADAM_B2) * (g * g)
    m_hat = nm / (1.0 - ADAM_B1 ** ADAM_STEP)
    v_hat = nv / (1.0 - ADAM_B2 ** ADAM_STEP)
    return -ADAM_LR * (m_hat / (jnp.sqrt(v_hat) + ADAM_EPS) + ADAM_WD * w), nm, nv


SMALL = ("b_ada", "g_ffn1", "g_mix", "g_ffn2", "g_final", "conv_w", "sinks")


def _adam_small(gsum, conv_g, w, m, v, *, name):
    nsm = len(SMALL)

    def body(*refs):
        gs_ref, cg_ref = refs[0], refs[1]
        w_refs, m_refs, v_refs = (refs[2 + k * nsm:2 + (k + 1) * nsm] for k in range(3))
        outs = refs[2 + 3 * nsm:]
        for p, n in enumerate(SMALL):
            if n == "b_ada":
                pieces = [(slice(None), slice(r * D, (r + 1) * D), gs_ref[R_MODS + r:R_MODS + r + 1, :])
                          for r in range(N_MOD)]
            elif n == "conv_w":
                pieces = [(slice(None), slice(None), cg_ref[...])]
            elif n == "sinks":
                pieces = [(slice(None), slice(None), gs_ref[R_SINK:R_SINK + 1, 0:N_HEADS])]
            else:
                row = dict(g_ffn1=R_G1, g_mix=R_GM, g_ffn2=R_G2, g_final=R_GF)[n]
                pieces = [(slice(None), slice(None), gs_ref[row:row + 1, :])]
            for rs, cs, g in pieces:
                d, nm, nv = _adam_math(w_refs[p][rs, cs], g, m_refs[p][rs, cs], v_refs[p][rs, cs])
                for k, val in enumerate((g, d, nm, nv)):
                    outs[k * nsm + p][rs, cs] = val

    args = [gsum, conv_g] + [d[n] for d in (w, m, v) for n in SMALL]
    shapes = [jax.ShapeDtypeStruct(w[n].shape, F32) for _ in range(4) for n in SMALL]
    res = pl.pallas_call(body, name=name, out_shape=shapes, compiler_params=_cp())(*args)
    return [dict(zip(SMALL, res[k * nsm:(k + 1) * nsm])) for k in range(4)]


def _adam(w, g, m, v, *, tm, name):
    _, R, C = w.shape
    tm = _tile(R, tm)
    parts = g.ndim == 3

    def body(w_ref, g_ref, m_ref, v_ref, go_ref, d_ref, nm_ref, nv_ref):
        if parts:
            gv = g_ref[0].astype(F32)
            for s in range(1, N_DEV):
                gv = gv + g_ref[s].astype(F32)
        else:
            gv = g_ref[...]
        go_ref[0] = gv
        d_ref[0], nm_ref[0], nv_ref[0] = _adam_math(w_ref[0], gv, m_ref[0], v_ref[0])

    spec = pl.BlockSpec((1, tm, C), lambda i: (0, i, 0))
    gspec = pl.BlockSpec((N_DEV, tm, C), lambda i: (0, i, 0)) if parts else pl.BlockSpec((tm, C), _row)
    return pl.pallas_call(
        body, name=name, grid=(R // tm,),
        in_specs=[spec, gspec, spec, spec], out_specs=[spec] * 4,
        out_shape=[jax.ShapeDtypeStruct((1, R, C), F32)] * 4,
        compiler_params=_cp(("parallel",)),
    )(w, g, m, v)


def _mods_part(c_all, w_ada, b_ada, *, name):
    C = w_ada.shape[1]

    def body(c_ref, w_ref, b_ref, o_ref):
        cv = c_ref[...]
        ca = cv * jax.nn.sigmoid(cv)
        o_ref[...] = jnp.dot(ca, w_ref[...], preferred_element_type=F32,
                             precision=lax.Precision.HIGHEST) + b_ref[...]

    return pl.pallas_call(
        body, name=name,
        out_shape=jax.ShapeDtypeStruct((N_DEV, C), F32),
        compiler_params=_cp(),
    )(c_all, w_ada, b_ada)


def _wada_grad(c_all_t, gm, *, name):
    C = gm.shape[1]

    def body(c_ref, g_ref, o_ref):
        cv = c_ref[...]
        ca = cv * jax.nn.sigmoid(cv)
        acc = ca[:, 0:1] * g_ref[0:1, :]
        for b in range(1, N_DEV):
            acc = acc + ca[:, b:b + 1] * g_ref[b:b + 1, :]
        o_ref[...] = acc

    return pl.pallas_call(
        body, name=name,
        out_shape=jax.ShapeDtypeStruct((D, C), F32),
        compiler_params=_cp(),
    )(c_all_t, gm)


def _peer(x, y, c, d):
    px = lax.rem(x + ((d >> 2) & 1), 2)
    py = lax.rem(y + ((d >> 1) & 1), 2)
    pc = lax.rem(c + (d & 1), 2)
    return (px, py, pc), 4 * px + 2 * py + pc


def _exchange(xs, *, scatter, name):
    n = len(xs)
    nsem = n * (N_DEV - 1)

    def body(*refs):
        ins, outs = refs[:n], refs[n:2 * n]
        token, send_sems, recv_sems, local_sems = refs[2 * n:]
        x, y, c = lax.axis_index("x"), lax.axis_index("y"), lax.axis_index("c")
        me = 4 * x + 2 * y + c
        token[...] = jnp.zeros_like(token)

        def src(t, idx):
            return ins[t].at[idx] if scatter else ins[t]

        local = [pltpu.make_async_copy(src(t, me), outs[t].at[me], local_sems.at[t]) for t in range(n)]
        for cp in local:
            cp.start()
        remote = []
        for t in range(n):
            for d in range(1, N_DEV):
                peer, pidx = _peer(x, y, c, d)
                k = t * (N_DEV - 1) + d - 1
                send = pltpu.make_async_remote_copy(src_ref=src(t, pidx), dst_ref=outs[t].at[me],
                                                    send_sem=send_sems.at[k], recv_sem=recv_sems.at[k],
                                                    device_id=peer, device_id_type=MESH)
                recv = pltpu.make_async_remote_copy(src_ref=src(t, pidx), dst_ref=outs[t].at[pidx],
                                                    send_sem=send_sems.at[k], recv_sem=recv_sems.at[k],
                                                    device_id=peer, device_id_type=MESH)
                send.start()
                remote.append((send, recv))
        for cp in local:
            cp.wait()
        for send, recv in remote:
            send.wait_send()
            recv.wait_recv()

    anyspec = pl.BlockSpec(memory_space=pl.ANY)
    out_shape = [jax.ShapeDtypeStruct(a.shape if scatter else (N_DEV,) + a.shape, a.dtype) for a in xs]
    out_shape.append(jax.ShapeDtypeStruct((8, 128), F32))
    return pl.pallas_call(
        body, name=name,
        in_specs=[anyspec] * n, out_specs=[anyspec] * n + [pl.BlockSpec(memory_space=pltpu.VMEM)],
        out_shape=out_shape,
        scratch_shapes=[pltpu.SemaphoreType.DMA((nsem,)), pltpu.SemaphoreType.DMA((nsem,)),
                        pltpu.SemaphoreType.DMA((n,))],
    )(*xs)


def _sum8(parts, *, name):
    _, R, C = parts.shape

    def body(p_ref, o_ref):
        acc = p_ref[0]
        for s in range(1, N_DEV):
            acc = acc + p_ref[s]
        o_ref[...] = acc

    return pl.pallas_call(body, name=name, out_shape=jax.ShapeDtypeStruct((R, C), F32),
                          compiler_params=_cp())(parts)


HBM_SPEC = pl.BlockSpec(memory_space=pltpu.HBM)
SEM_SPEC = pl.BlockSpec(memory_space=pltpu.SEMAPHORE)
N_PEER = N_DEV - 1


def _split_copies(src_refs, land_refs, send_sems, recv_sems, scatter):
    x, y, c = lax.axis_index("x"), lax.axis_index("y"), lax.axis_index("c")
    me = 4 * x + 2 * y + c
    pairs = []
    for j, (src, land) in enumerate(zip(src_refs, land_refs)):
        for d in range(1, N_DEV):
            peer, pidx = _peer(x, y, c, d)
            k = j * N_PEER + d - 1
            s = src.at[pidx] if scatter else src
            send = pltpu.make_async_remote_copy(src_ref=s, dst_ref=land.at[me], send_sem=send_sems.at[k],
                                                recv_sem=recv_sems.at[k], device_id=peer, device_id_type=MESH)
            recv = pltpu.make_async_remote_copy(src_ref=s, dst_ref=land.at[pidx], send_sem=send_sems.at[k],
                                                recv_sem=recv_sems.at[k], device_id=peer, device_id_type=MESH)
            pairs.append((send, recv))
    return pairs


def _own_slot(block, me):
    land = lax.empty((N_DEV,) + block.shape, block.dtype)
    return lax.dynamic_update_slice(land, block[None], (me, 0, 0))


def _split_start(srcs, lands, groups, *, scatter, name):
    n, ng = len(srcs), len(groups)

    def body(*refs):
        src_refs, land_refs = refs[:n], refs[n:2 * n]
        sems = refs[2 * n:2 * n + 2 * ng]
        token = refs[-1]
        for gi, g in enumerate(groups):
            pairs = _split_copies([src_refs[t] for t in g], [land_refs[t] for t in g], sems[2 * gi],
                                  sems[2 * gi + 1], scatter)
            for send, _ in pairs:
                send.start()
        token[...] = jnp.zeros_like(token)

    sem_shapes = []
    for g in groups:
        sem_shapes += [pltpu.SemaphoreType.DMA((len(g) * N_PEER,))] * 2
    thru = [pltpu.HBM(a.shape, a.dtype) for a in list(srcs) + list(lands)]
    outs = pl.pallas_call(
        body, name=name,
        out_shape=tuple(sem_shapes + thru + [jax.ShapeDtypeStruct((8, 128), F32)]),
        in_specs=[HBM_SPEC] * (2 * n),
        out_specs=tuple([SEM_SPEC] * (2 * ng) + [HBM_SPEC] * (2 * n) + [pl.BlockSpec(memory_space=pltpu.VMEM)]),
        input_output_aliases={i: 2 * ng + i for i in range(2 * n)},
        compiler_params=pltpu.CompilerParams(has_side_effects=pltpu.SideEffectType.DATAFLOW_SIDE_EFFECTING),
    )(*[pltpu.with_memory_space_constraint(a, pltpu.HBM) for a in list(srcs) + list(lands)])
    sems = [(outs[2 * gi], outs[2 * gi + 1]) for gi in range(ng)]
    return sems, outs[2 * ng:2 * ng + n], outs[2 * ng + n:2 * ng + 2 * n], outs[-1]


def _behind(v, token):
    if token is None:
        return v
    return v + token[0, 0].astype(v.dtype)


def _split_wait(srcs, lands, sems, after, *, scatter, name):
    m = len(srcs)

    def body(*refs):
        src_refs, land_refs = refs[:m], refs[m:2 * m]
        send_sems, recv_sems = refs[2 * m], refs[2 * m + 1]
        for send, recv in _split_copies(src_refs, land_refs, send_sems, recv_sems, scatter):
            send.wait_send()
            recv.wait_recv()

    outs = pl.pallas_call(
        body, name=name,
        out_shape=tuple(pltpu.HBM(a.shape, a.dtype) for a in list(srcs) + list(lands)),
        in_specs=[HBM_SPEC] * (2 * m) + [SEM_SPEC, SEM_SPEC, pl.BlockSpec(memory_space=pl.ANY)],
        out_specs=tuple([HBM_SPEC] * (2 * m)),
        input_output_aliases={i: i for i in range(2 * m)},
        compiler_params=pltpu.CompilerParams(has_side_effects=pltpu.SideEffectType.DATAFLOW_SIDE_EFFECTING),
    )(*srcs, *lands, sems[0], sems[1], after)
    return outs[m:]


TL_FIRST = (1, 2, 4, 6)
TL_ICI = (2, 4, 6)
EFFECT = pltpu.SideEffectType.DATAFLOW_SIDE_EFFECTING


def _tl_first(src_refs, land_refs, send_sems, recv_sems):
    x, y, c = lax.axis_index("x"), lax.axis_index("y"), lax.axis_index("c")
    me = 4 * x + 2 * y + c
    out = []
    for j, (src, land) in enumerate(zip(src_refs, land_refs)):
        for i, d in enumerate(TL_FIRST):
            peer, pidx = _peer(x, y, c, d)
            k = len(TL_FIRST) * j + i
            send = pltpu.make_async_remote_copy(src_ref=src, dst_ref=land.at[me], send_sem=send_sems.at[k],
                                                recv_sem=recv_sems.at[k], device_id=peer, device_id_type=MESH)
            recv = pltpu.make_async_remote_copy(src_ref=src, dst_ref=land.at[pidx], send_sem=send_sems.at[k],
                                                recv_sem=recv_sems.at[k], device_id=peer, device_id_type=MESH)
            out.append((d, send, recv))
    return out


def _tl_second(land_refs, send_sems, recv_sems):
    x, y, c = lax.axis_index("x"), lax.axis_index("y"), lax.axis_index("c")
    sibling, _ = _peer(x, y, c, 1)
    out = []
    for j, land in enumerate(land_refs):
        for i, d in enumerate(TL_ICI):
            _, mine = _peer(x, y, c, d)
            _, theirs = _peer(x, y, c, d + 1)
            k = len(TL_ICI) * j + i
            send = pltpu.make_async_remote_copy(src_ref=land.at[mine], dst_ref=land.at[mine], send_sem=send_sems.at[k],
                                                recv_sem=recv_sems.at[k], device_id=sibling, device_id_type=MESH)
            recv = pltpu.make_async_remote_copy(src_ref=land.at[mine], dst_ref=land.at[theirs],
                                                send_sem=send_sems.at[k], recv_sem=recv_sems.at[k],
                                                device_id=sibling, device_id_type=MESH)
            out.append((send, recv))
    return out


def _tl_start(srcs, lands, groups, *, name):
    n, ng = len(srcs), len(groups)

    def body(*refs):
        src_refs, land_refs = refs[:n], refs[n:2 * n]
        sems = refs[2 * n:2 * n + 2 * ng]
        for gi, g in enumerate(groups):
            for _, send, _ in _tl_first([src_refs[t] for t in g], [land_refs[t] for t in g], sems[2 * gi],
                                        sems[2 * gi + 1]):
                send.start()
        refs[-1][...] = jnp.zeros_like(refs[-1])

    sem_shapes = []
    for g in groups:
        sem_shapes += [pltpu.SemaphoreType.DMA((len(g) * len(TL_FIRST),))] * 2
    thru = [pltpu.HBM(a.shape, a.dtype) for a in list(srcs) + list(lands)]
    outs = pl.pallas_call(
        body, name=name,
        out_shape=tuple(sem_shapes + thru + [jax.ShapeDtypeStruct((8, 128), F32)]),
        in_specs=[HBM_SPEC] * (2 * n),
        out_specs=tuple([SEM_SPEC] * (2 * ng) + [HBM_SPEC] * (2 * n) + [pl.BlockSpec(memory_space=pltpu.VMEM)]),
        input_output_aliases={i: 2 * ng + i for i in range(2 * n)},
        compiler_params=pltpu.CompilerParams(has_side_effects=EFFECT),
    )(*[pltpu.with_memory_space_constraint(a, pltpu.HBM) for a in list(srcs) + list(lands)])
    sems = [(outs[2 * gi], outs[2 * gi + 1]) for gi in range(ng)]
    return sems, outs[2 * ng:2 * ng + n], outs[2 * ng + n:2 * ng + 2 * n], outs[-1]


def _tl_forward(srcs, lands, sems1, after, *, name):
    m = len(srcs)

    def body(*refs):
        src_refs, land_refs = refs[:m], refs[m:2 * m]
        send1, recv1 = refs[2 * m], refs[2 * m + 1]
        send2, recv2 = refs[2 * m + 3], refs[2 * m + 4]
        for d, _, recv in _tl_first(src_refs, land_refs, send1, recv1):
            if d in TL_ICI:
                recv.wait_recv()
        for send, _ in _tl_second(land_refs, send2, recv2):
            send.start()

    sem = pltpu.SemaphoreType.DMA((m * len(TL_ICI),))
    outs = pl.pallas_call(
        body, name=name,
        out_shape=tuple([sem, sem] + [pltpu.HBM(a.shape, a.dtype) for a in list(srcs) + list(lands)]),
        in_specs=[HBM_SPEC] * (2 * m) + [SEM_SPEC, SEM_SPEC, pl.BlockSpec(memory_space=pl.ANY)],
        out_specs=tuple([SEM_SPEC, SEM_SPEC] + [HBM_SPEC] * (2 * m)),
        input_output_aliases={i: 2 + i for i in range(2 * m)},
        compiler_params=pltpu.CompilerParams(has_side_effects=EFFECT),
    )(*srcs, *lands, sems1[0], sems1[1], after)
    return (outs[0], outs[1]), outs[2:2 + m], outs[2 + m:2 + 2 * m]


def _tl_wait(srcs, lands, sems1, sems2, after, *, name):
    m = len(srcs)

    def body(*refs):
        src_refs, land_refs = refs[:m], refs[m:2 * m]
        send1, recv1, send2, recv2 = refs[2 * m:2 * m + 4]
        for d, send, recv in _tl_first(src_refs, land_refs, send1, recv1):
            send.wait_send()
            if d not in TL_ICI:
                recv.wait_recv()
        for send, recv in _tl_second(land_refs, send2, recv2):
            send.wait_send()
            recv.wait_recv()

    outs = pl.pallas_call(
        body, name=name,
        out_shape=tuple(pltpu.HBM(a.shape, a.dtype) for a in list(srcs) + list(lands)),
        in_specs=[HBM_SPEC] * (2 * m) + [SEM_SPEC] * 4 + [pl.BlockSpec(memory_space=pl.ANY)],
        out_specs=tuple([HBM_SPEC] * (2 * m)),
        input_output_aliases={i: i for i in range(2 * m)},
        compiler_params=pltpu.CompilerParams(has_side_effects=EFFECT),
    )(*srcs, *lands, sems1[0], sems1[1], sems2[0], sems2[1], after)
    return outs[m:]


TM_PROJ = 512
TN_PROJ = 512
TM_ROW = 512
TM_NN = 512
TK_TN = 2048
TM_ADAM = 208
TN_FFN = F // 2
TN_IN = NIN // 4


def _tn(a, b, name, tn, token=None):
    if a.ndim == 2:
        a = a[None]
    return _tn_matmul(a, b, token, tn=tn, tk=TK_TN, name=name)


def _local_step(x, tgt, mods, g1, gm, g2, gf, convw8, sinks, w_get, g_put, tables=None):
    T = x.shape[0]
    sh1, sc1, gt1, sh2, sc2, gt2, sh3, sc3, gt3 = [mods[i:i + 1] for i in range(N_MOD)]
    cos, sin = _rope_tables(T) if tables is None else tables
    behind = _behind

    w = dict(w_get("gu1", mods))
    h1, ab1 = _norm_proj(x, g1, sc1, sh1, w["gu1"], tm=TM_PROJ, tn=TN_PROJ, name="ffn1_up")
    w.update(w_get("d1", ab1))
    x1 = _ffn_down_fwd(ab1, w["d1"], x, gt1, tm=TM_ROW, name="ffn1_down")
    w.update(w_get("mix", x1))
    h2, proj, qs, kr = _norm_proj(x1, gm, sc2, sh2, w["win"], (cos, sin), tm=TM_PROJ, tn=TN_PROJ, name="mix_in")
    bias = _attn_bias()
    attn, lse = _attn_fwd(qs, kr, proj, bias, sinks, name="attn_fwd")
    x2, gc, yc, ya, mg, o = _mixer_mid_fwd(proj, attn, w["cp"], w["ap"], w["out"], convw8, x1, gt2,
                                           tm=TM_ROW, name="mix_mid")
    w.update(w_get("ffn2", x2))
    h3, ab2, dx3, lsum, dgf = _ffn_fwd(x2, g2, sc3, sh3, gt3, w["gu2"], w["d2"], (tgt, gf), tm=TM_ROW,
                                           name="ffn2_final")

    dab2, dgt3, g_d2 = _ffn_down_bwd_dw(dx3, gt3, ab2, w["d2"], tm=TM_ROW, name="ffn2_down_bwd")
    dx2, dsh3, dsc3, dg2 = _nn_bwd_norm(dab2, w["gu2"], x2, g2, sc3, dx3, tm=TM_NN, name="ffn2_up_bwd")
    g_gu2 = _tn(dab2, h3, "ffn2_up_dw", TN_FFN)
    tok = g_put(dict(gu2=g_gu2, d2=g_d2))

    dout, dyc, dya, dgc, dat, dproj, dgt2 = _mixer_mid_bwd(dx2, behind(gt2, tok), o, proj, yc, ya, w["out"], w["cp"],
                                                           w["ap"], tm=TM_ROW, name="mix_mid_bwd")
    g_out = _tn(mg, dout, "mix_out_dw", D)
    g_cp = _tn(gc, dyc, "mix_cp_dw", D)
    g_ap = _tn(attn, dya, "mix_ap_dw", D)
    dproj, dkc, dkp, dvc, dvp, dsink = _attn_bwd(qs, kr, proj, bias, sinks, lse, attn, dat, cos, sin, dproj,
                                                 name="attn_bwd")
    dproj = _dkv_combine(dkc, dkp, dvc, dvp, dproj, name="attn_dkv")
    dproj, dcw = _conv_bwd(dgc, proj, convw8, dproj, tm=TM_ROW, name="conv_bwd")
    g_in = _tn(dproj, h2, "mix_in_dw", TN_IN)
    tok = g_put(dict(win=g_in, cp=g_cp, ap=g_ap, out=g_out))
    dx1, dsh2, dsc2, dgm = _nn_bwd_norm(dproj[None], w["win"], x1, gm, behind(sc2, tok), dx2, tm=TM_NN,
                                        name="mix_in_bwd")

    dab1, dgt1, g_d1 = _ffn_down_bwd_dw(dx1, gt1, ab1, w["d1"], tm=TM_ROW, name="ffn1_down_bwd")
    tok = g_put(dict(d1=g_d1))
    g_gu1 = _tn(dab1, h1, "ffn1_up_dw", TN_FFN, tok)
    tok = g_put(dict(gu1=g_gu1))
    dx0, dsh1, dsc1, dg1 = _nn_bwd_norm(dab1, w["gu1"], x, g1, behind(sc1, tok), dx1, tm=TM_NN,
                                        name="ffn1_up_bwd")

    small = dict(mods=jnp.concatenate([dsh1, dsc1, dgt1, dsh2, dsc2, dgt2, dsh3, dsc3, dgt3], axis=0),
                 g1=dg1, gm=dgm, g2=dg2, gf=dgf, convw=dcw[0:3], sinks=dsink[:, 0:N_HEADS])
    return lsum, dx0, small


BIG = ("gu1", "d1", "win", "cp", "ap", "out", "gu2", "d2")
TRANSPOSED = ("gu1", "win", "gu2")
SMALL_ROWS = 24
R_MODS, R_G1, R_GM, R_G2, R_GF, R_CONV, R_SINK, R_LOSS = 0, 9, 10, 11, 12, 13, 16, 17


def _pad_to(a, rows, cols):
    return jnp.pad(a, ((0, rows - a.shape[0]), (0, cols - a.shape[1])))


def _pack_small(b_ada, g1, gm, g2, gf, conv, sinks, lsum):
    rows = [b_ada.reshape(N_MOD, D), g1.reshape(1, D), gm.reshape(1, D), g2.reshape(1, D), gf.reshape(1, D),
            _pad_to(conv.reshape(3, -1), 3, D), _pad_to(sinks.reshape(1, N_HEADS), 1, D), lsum.reshape(1, D)]
    return _pad_to(jnp.concatenate(rows, axis=0), SMALL_ROWS, D)


def kernel(x, c, w_ada, b_ada, g_ffn1, w1_gu, w1_down, g_mix, w_in, conv_w, w_conv_proj, w_attn_proj, sinks, w_out, g_ffn2, w2_gu, w2_down, g_final, loss_target, m_w_ada, m_b_ada, m_g_ffn1, m_w1_gu, m_w1_down, m_g_mix, m_w_in, m_conv_w, m_w_conv_proj, m_w_attn_proj, m_sinks, m_w_out, m_g_ffn2, m_w2_gu, m_w2_down, m_g_final, v_w_ada, v_b_ada, v_g_ffn1, v_w1_gu, v_w1_down, v_g_mix, v_w_in, v_conv_w, v_w_conv_proj, v_w_attn_proj, v_sinks, v_w_out, v_g_ffn2, v_w2_gu, v_w2_down, v_g_final):
    me = 4 * lax.axis_index("x") + 2 * lax.axis_index("y") + lax.axis_index("c")
    ada_cols = w_ada.shape[2]
    conv_cols = conv_w.shape[2]

    native = dict(gu1=w1_gu[0], d1=w1_down[0], win=w_in[0], cp=w_conv_proj[0], ap=w_attn_proj[0], out=w_out[0],
                  gu2=w2_gu[0], d2=w2_down[0])

    def shard(n, token):
        a = _behind(native[n], token)
        return (a.T if n in TRANSPOSED else a).astype(BF)

    c_all, conv_all, _ = _exchange([c, _pad_to(conv_w[0], 8, conv_cols)], scatter=False, name="gather_cond")
    c_all = c_all.reshape(N_DEV, D)
    conv_full = conv_all[:, 0:3, :].transpose(1, 0, 2).reshape(3, D)

    b_cols = lax.dynamic_slice(b_ada, (0, me * ada_cols), (1, ada_cols))
    mods_cols = _mods_part(c_all, w_ada[0], b_cols, name="ada_mods")
    mods_all, mods_token = _exchange([mods_cols], scatter=False, name="gather_mods")
    mods = lax.dynamic_index_in_dim(mods_all, me, axis=1, keepdims=False).reshape(N_MOD, D)

    groups = dict(gu1=("gu1",), d1=("d1",), mix=("win", "cp", "ap", "out"), ffn2=("gu2", "d2"))
    in_flight = {}
    first = [shard("gu1", mods_token)]
    sems, srcs, lands, token = _tl_start(first, [_own_slot(s, me) for s in first], [[0]],
                                         name="gather_weights_start_gu1")
    in_flight["gu1"] = [sems[0], srcs, lands, None]
    rest = [n for n in BIG if n != "gu1"]
    shards = [shard(n, token) for n in rest]
    rest_groups = [[rest.index(n) for n in names] for g, names in groups.items() if g != "gu1"]
    sems, srcs, lands, rest_token = _tl_start(shards, [_own_slot(s, me) for s in shards], rest_groups,
                                              name="gather_weights_start_rest")
    for (g, names), gsems, idx in zip([kv for kv in groups.items() if kv[0] != "gu1"], sems, rest_groups):
        in_flight[g] = [gsems, [srcs[t] for t in idx], [lands[t] for t in idx], None]

    def forward(group, after):
        sems1, gsrcs, glands, _ = in_flight[group]
        sems2, gsrcs, glands = _tl_forward(gsrcs, glands, sems1, after, name="gather_weights_forward_" + group)
        in_flight[group] = [sems1, gsrcs, glands, sems2]

    forward_early = dict(d1="mix", mix="ffn2")

    tables = _rope_tables(x.shape[1], rest_token)

    def w_get(group, after):
        if group == "gu1":
            after = tables[0]
        if in_flight[group][3] is None:
            forward(group, after)
        sems1, gsrcs, glands, sems2 = in_flight[group]
        landed = _tl_wait(gsrcs, glands, sems1, sems2, after, name="gather_weights_wait_" + group)
        if group in forward_early:
            forward(forward_early[group], landed[0])
        return {n: a.reshape(-1, D) for n, a in zip(groups[group], landed)}

    pending = []

    def g_put(gs):
        names = tuple(gs)
        srcs = [gs[n].reshape(N_DEV, -1, D) for n in names]
        lands = [_own_slot(lax.dynamic_index_in_dim(s, me, axis=0, keepdims=False), me) for s in srcs]
        sems, srcs, lands, tok = _split_start(srcs, lands, [list(range(len(names)))], scatter=True,
                                              name="scatter_grads_start_" + names[0])
        pending.append((names, sems[0], srcs, lands))
        return tok

    lsum, grad_x, small = _local_step(x[0], loss_target[0], mods, g_ffn1, g_mix, g_ffn2, g_final[None],
                                      _pad_to(conv_full, 8, D), sinks[0], w_get, g_put, tables)

    packed = _pack_small(small["mods"], small["g1"], small["gm"], small["g2"], small["gf"], small["convw"],
                         small["sinks"], lsum)
    sm_sems, sm_srcs, sm_lands, sm_token = _split_start([packed], [_own_slot(packed, me)], [[0]], scatter=False,
                                                        name="gather_small_start")

    w_of = dict(ada=w_ada, gu1=w1_gu, d1=w1_down, win=w_in, cp=w_conv_proj, ap=w_attn_proj, out=w_out, gu2=w2_gu,
                d2=w2_down)
    m_of = dict(ada=m_w_ada, gu1=m_w1_gu, d1=m_w1_down, win=m_w_in, cp=m_w_conv_proj, ap=m_w_attn_proj, out=m_w_out,
                gu2=m_w2_gu, d2=m_w2_down)
    v_of = dict(ada=v_w_ada, gu1=v_w1_gu, d1=v_w1_down, win=v_w_in, cp=v_w_conv_proj, ap=v_w_attn_proj, out=v_w_out,
                gu2=v_w2_gu, d2=v_w2_down)
    upd = {}
    after = sm_token
    for k, (names, sems, srcs, lands) in enumerate(pending):
        if k == 2:
            (packed_all,) = _split_wait(sm_srcs, sm_lands, sm_sems[0], after, scatter=False, name="gather_small_wait")
            gsmall = _sum8(packed_all, name="sum_small")
            loss = (0.5 / D) * jnp.sum(gsmall[R_LOSS])
            after = gsmall
        parts = _split_wait(srcs, lands, sems, after, scatter=True, name="scatter_grads_wait_" + names[0])
        for n, p in zip(names, parts):
            if n in TRANSPOSED:
                res = _adam(jnp.swapaxes(w_of[n], 1, 2), p, jnp.swapaxes(m_of[n], 1, 2), jnp.swapaxes(v_of[n], 1, 2),
                            tm=TM_ADAM, name="adam_" + n)
                upd[n] = [jnp.swapaxes(t, 1, 2) for t in res]
            else:
                upd[n] = _adam(w_of[n], p, m_of[n], v_of[n], tm=TM_ADAM, name="adam_" + n)
        after = upd[names[-1]][1]

    gm_cols = lax.dynamic_slice(packed_all[:, R_MODS:R_MODS + N_MOD, :].reshape(N_DEV, N_MOD * D),
                                (0, me * ada_cols), (N_DEV, ada_cols))
    upd["ada"] = _adam(w_ada, _wada_grad(c_all.T, gm_cols, name="ada_dw"), m_w_ada, v_w_ada, tm=256, name="adam_ada")
    conv_g = lax.dynamic_slice(gsmall, (R_CONV, me * conv_cols), (3, conv_cols))

    def natural(b, g1, gm, g2, gf, cw, sk):
        return dict(b_ada=b, g_ffn1=g1, g_mix=gm, g_ffn2=g2, g_final=gf[None], conv_w=cw[0], sinks=sk)

    small_out = _adam_small(gsmall, conv_g, natural(b_ada, g_ffn1, g_mix, g_ffn2, g_final, conv_w, sinks),
                            natural(m_b_ada, m_g_ffn1, m_g_mix, m_g_ffn2, m_g_final, m_conv_w, m_sinks),
                            natural(v_b_ada, v_g_ffn1, v_g_mix, v_g_ffn2, v_g_final, v_conv_w, v_sinks),
                            name="adam_small")
    for res in small_out:
        res["g_final"] = res["g_final"][0]
        res["conv_w"] = res["conv_w"][None]

    big_name = dict(w_ada="ada", w1_gu="gu1", w1_down="d1", w_in="win", w_conv_proj="cp", w_attn_proj="ap",
                    w_out="out", w2_gu="gu2", w2_down="d2")
    order = ("w_ada", "b_ada", "g_ffn1", "w1_gu", "w1_down", "g_mix", "w_in", "conv_w", "w_conv_proj", "w_attn_proj",
             "sinks", "w_out", "g_ffn2", "w2_gu", "w2_down", "g_final")
    outs = [loss, grad_x[None]]
    for kind in range(4):
        for n in order:
            outs.append(upd[big_name[n]][kind] if n in big_name else small_out[kind][n])
    return tuple(outs)
```

```python
import jax
import jax.numpy as jnp
from jax import lax
from jax.experimental import pallas as pl
from jax.experimental.pallas import tpu as pltpu

D = 1024
F = 2816
NIN = 6656
N_HEADS = 16
N_KV = 4
HEAD_DIM = 64
BLK = 128
N_MOD = 9
N_DEV = 8
EPS = 1e-6
NEG_INF = -1e30
ROPE_THETA = 10000.0
O_BG, O_CG, O_U, O_Q, O_K, O_V, O_ZC, O_ZA = 0, 1024, 2048, 3072, 4096, 4352, 4608, 5632

ADAM_LR = 0.001
ADAM_B1 = 0.9
ADAM_B2 = 0.999
ADAM_EPS = 1e-08
ADAM_WD = 0.01
ADAM_STEP = 10

BF = jnp.bfloat16
F32 = jnp.float32
VMEM_LIMIT = 56 * 1024 * 1024
MXU_N = 256
DOWN_BWD_PARTS = 1
MESH = pl.DeviceIdType.MESH

NT = (((1,), (1,)), ((), ()))
TN = (((0,), (0,)), ((), ()))


def _cp(sem=None):
    return pltpu.CompilerParams(dimension_semantics=sem, vmem_limit_bytes=VMEM_LIMIT)


def _tile(n, pref):
    if n <= pref:
        return n
    for t in range(pref - pref % 16, 15, -16):
        if n % t == 0:
            return t
    raise ValueError((n, pref))


def _sigmoid(v):
    return 0.5 * jnp.tanh(0.5 * v) + 0.5


def _row(i):
    return (i, 0)


def _const2(*_):
    return (0, 0)


def _resident(shape):
    return pl.BlockSpec(shape, lambda *_: (0,) * len(shape), pipeline_mode=pl.Buffered(1))


def _norm_proj(x, g, sc, sh, wt, rope=None, *, tm, tn, name):
    T, N = x.shape[0], wt.shape[0]
    tm = _tile(T, tm)

    def body(x_ref, g_ref, sc_ref, sh_ref, w_ref, *rest):
        if rope is None:
            h_ref, o_ref = rest
        else:
            c_ref, s_ref, h_ref, o_ref, qs_ref, kr_ref = rest
        xv = x_ref[...]
        r = lax.rsqrt(jnp.mean(xv * xv, axis=-1, keepdims=True) + EPS)
        hb = ((xv * r) * g_ref[...] * (1.0 + sc_ref[...]) + sh_ref[...]).astype(BF)
        h_ref[...] = hb
        for c0 in range(0, N, tn):
            cols = pl.ds(c0, tn)
            o_ref[:, cols] = lax.dot_general(hb, w_ref[cols, :], NT, preferred_element_type=F32).astype(BF)
            if rope is not None and c0 < O_V <= c0 + tn:
                _attn_prep_tile(o_ref, c_ref, s_ref, qs_ref, kr_ref, tm)

    vec = pl.BlockSpec((1, D), _const2)
    rowspec = pl.BlockSpec((tm, D), _row)
    in_specs = [rowspec, vec, vec, vec, _resident((N, D))]
    out_specs = [rowspec, pl.BlockSpec((tm, N), _row)]
    out_shape = [jax.ShapeDtypeStruct((T, D), BF), jax.ShapeDtypeStruct((T, N), BF)]
    args = [x, g, sc, sh, wt]
    if rope is not None:
        in_specs += [pl.BlockSpec((tm, 128), _row)] * 2
        out_specs += [pl.BlockSpec((N_KV, 4 * tm, 128), lambda i: (0, i, 0)), pl.BlockSpec((tm, 256), _row)]
        out_shape += [jax.ShapeDtypeStruct((N_KV, 4 * T, 128), BF), jax.ShapeDtypeStruct((T, 256), BF)]
        args += list(rope)
    return pl.pallas_call(
        body, name=name, grid=(T // tm,),
        in_specs=in_specs, out_specs=out_specs, out_shape=out_shape,
        compiler_params=_cp(("parallel",)),
    )(*args)


def _ffn_down_fwd(ab, wd, x, gt, *, tm, name):
    T = x.shape[0]
    tm = _tile(T, tm)

    def body(a_ref, b_ref, wd_ref, x_ref, gt_ref, xo_ref):
        y = None
        for c0 in range(0, F, MXU_N):
            cols = pl.ds(c0, MXU_N)
            a = a_ref[:, cols].astype(F32)
            act = (a * _sigmoid(a) * b_ref[:, cols].astype(F32)).astype(BF)
            part = jnp.dot(act, wd_ref[cols, :], preferred_element_type=F32)
            y = part if y is None else y + part
        xo_ref[...] = x_ref[...] + (0.5 * gt_ref[...]) * y

    return pl.pallas_call(
        body, name=name, grid=(T // tm,),
        in_specs=[pl.BlockSpec((tm, F), lambda i: (i, 0)), pl.BlockSpec((tm, F), lambda i: (i, 1)),
                  _resident((F, D)), pl.BlockSpec((tm, D), _row), pl.BlockSpec((1, D), _const2)],
        out_specs=pl.BlockSpec((tm, D), _row),
        out_shape=jax.ShapeDtypeStruct((T, D), F32),
        compiler_params=_cp(("parallel",)),
    )(ab, ab, wd, x, gt)


def _ffn_fwd(x, g, sc, sh, gt, wgu, wd, final, *, tm, name):
    T = x.shape[0]
    tm = _tile(T, tm)
    last = final is not None

    def body(x_ref, g_ref, sc_ref, sh_ref, gt_ref, wgu_ref, wd_ref, *rest):
        if last:
            t_ref, gf_ref, h_ref, ab_ref, dx_ref, ls_ref, dgf_ref = rest
        else:
            h_ref, ab_ref, xo_ref = rest
        xv = x_ref[...]
        r = lax.rsqrt(jnp.mean(xv * xv, axis=-1, keepdims=True) + EPS)
        hb = ((xv * r) * g_ref[...] * (1.0 + sc_ref[...]) + sh_ref[...]).astype(BF)
        h_ref[...] = hb
        y = None
        for c0 in range(0, F, MXU_N):
            a = lax.dot_general(hb, wgu_ref[pl.ds(c0, MXU_N), :], NT, preferred_element_type=F32)
            b = lax.dot_general(hb, wgu_ref[pl.ds(F + c0, MXU_N), :], NT, preferred_element_type=F32)
            ab = a.astype(BF)
            bb = b.astype(BF)
            ab_ref[:, pl.ds(c0, MXU_N)] = ab
            ab_ref[:, pl.ds(F + c0, MXU_N)] = bb
            a = ab.astype(F32)
            act = (a * _sigmoid(a) * bb.astype(F32)).astype(BF)
            part = jnp.dot(act, wd_ref[pl.ds(c0, MXU_N), :], preferred_element_type=F32)
            y = part if y is None else y + part
        xo = xv + (0.5 * gt_ref[...]) * y
        if not last:
            xo_ref[...] = xo
            return

        @pl.when(pl.program_id(0) == 0)
        def _():
            ls_ref[...] = jnp.zeros_like(ls_ref)
            dgf_ref[...] = jnp.zeros_like(dgf_ref)
        gv = gf_ref[...]
        r = lax.rsqrt(jnp.mean(xo * xo, axis=-1, keepdims=True) + EPS)
        xh = xo * r
        e = xh * gv - t_ref[...]
        ls_ref[...] += jnp.sum(e * e, axis=0, keepdims=True)
        dy = e * (1.0 / D)
        dgf_ref[...] += jnp.sum(dy * xh, axis=0, keepdims=True)
        dxh = dy * gv
        dx_ref[...] = r * (dxh - xh * jnp.mean(dxh * xh, axis=-1, keepdims=True))

    vec = pl.BlockSpec((1, D), _const2)
    rowspec = pl.BlockSpec((tm, D), _row)
    in_specs = [rowspec, vec, vec, vec, vec, _resident((2 * F, D)), _resident((F, D))]
    out_specs = [rowspec, pl.BlockSpec((tm, 2 * F), _row), rowspec]
    out_shape = [jax.ShapeDtypeStruct((T, D), BF), jax.ShapeDtypeStruct((T, 2 * F), BF),
                 jax.ShapeDtypeStruct((T, D), F32)]
    args = [x, g, sc, sh, gt, wgu, wd]
    if last:
        in_specs += [rowspec, vec]
        out_specs += [vec, vec]
        out_shape += [jax.ShapeDtypeStruct((1, D), F32)] * 2
        args += list(final)
    return pl.pallas_call(
        body, name=name, grid=(T // tm,),
        in_specs=in_specs, out_specs=out_specs, out_shape=out_shape,
        compiler_params=_cp(("arbitrary",) if last else ("parallel",)),
    )(*args)


def _ffn_down_bwd_dw(dxo, gt, ab, wd, *, tm, name):
    T = dxo.shape[0]
    tm = _tile(T, tm)
    nt = T // tm
    nh = DOWN_BWD_PARTS
    hw = F // nh
    chunks = [(c0, min(MXU_N, hw - c0)) for c0 in range(0, hw, MXU_N)]

    def body(dxo_ref, gt_ref, a_ref, b_ref, wd_ref, dab_ref, dgt_ref, dwd_ref, dys, dyt, acc, stage, sem):
        i, j = pl.program_id(0), pl.program_id(1)

        @pl.when(jnp.logical_and(i == 0, j == 0))
        def _():
            dgt_ref[...] = jnp.zeros_like(dgt_ref)

        @pl.when(j == 0)
        def _():
            dxv = dxo_ref[...]
            dys[...] = ((0.5 * gt_ref[...]) * dxv).astype(BF)
            dyt[...] = dxv.T.astype(BF)

        def half(jj):
            @pl.when(i == 0)
            def _():
                acc[jj] = jnp.zeros((D, hw), F32)

            dy = dys[...]
            dy_t = dyt[...]
            for c0, cw in chunks:
                cols = pl.ds(c0, cw)
                dact = lax.dot_general(dy, wd_ref[pl.ds(jj * hw + c0, cw), :], NT, preferred_element_type=F32)
                a = a_ref[:, cols].astype(F32)
                b = b_ref[:, cols].astype(F32)
                s = _sigmoid(a)
                silu = a * s
                dab_ref[0, :, cols] = (dact * b * (s * (1.0 + a * (1.0 - s)))).astype(BF)
                dab_ref[1, :, cols] = (dact * silu).astype(BF)
                acc[jj, :, cols] += jnp.dot(dy_t, (silu * b).astype(BF), preferred_element_type=F32)

            @pl.when(i == nt - 1)
            def _():
                half_gt = 0.5 * gt_ref[...]
                for c0, cw in chunks:
                    g_rows = acc[jj, :, pl.ds(c0, cw)].T
                    w_rows = wd_ref[pl.ds(jj * hw + c0, cw), :].astype(F32)
                    dgt_ref[...] += 0.5 * jnp.sum(g_rows * w_rows, axis=0, keepdims=True)
                    stage[0:cw, :] = (g_rows * half_gt).astype(BF)
                    out = pltpu.make_async_copy(stage.at[pl.ds(0, cw)], dwd_ref.at[pl.ds(jj * hw + c0, cw)], sem)
                    out.start()
                    out.wait()

        for jj in range(nh):
            pl.when(j == jj)(lambda jj=jj: half(jj))

    vec = pl.BlockSpec((1, D), _const2)
    rowspec = pl.BlockSpec((tm, D), lambda i, j: (i, 0))
    return pl.pallas_call(
        body, name=name, grid=(nt, nh),
        in_specs=[rowspec, vec, pl.BlockSpec((tm, hw), lambda i, j: (i, j)),
                  pl.BlockSpec((tm, hw), lambda i, j: (i, j + nh)), _resident((F, D))],
        out_specs=[pl.BlockSpec((2, tm, hw), lambda i, j: (0, i, j)), vec, pl.BlockSpec(memory_space=pl.ANY)],
        out_shape=[jax.ShapeDtypeStruct((2, T, F), BF), jax.ShapeDtypeStruct((1, D), F32),
                   jax.ShapeDtypeStruct((F, D), BF)],
        scratch_shapes=[pltpu.VMEM((tm, D), BF), pltpu.VMEM((D, tm), BF), pltpu.VMEM((nh, D, hw), F32),
                        pltpu.VMEM((MXU_N, D), BF), pltpu.SemaphoreType.DMA(())],
        compiler_params=_cp(("arbitrary", "arbitrary")),
    )(dxo, gt, ab, ab, wd)


def _tn_matmul(a, b, token=None, *, tn, tk, name):
    S, T, Ns = a.shape
    tn, tk = _tile(Ns, tn), _tile(T, tk)
    nk, njs = T // tk, Ns // tn
    deps = [] if token is None else [token]

    def body(a_ref, b_ref, *rest):
        o_ref, acc = rest[len(deps):]
        k = pl.program_id(1)

        @pl.when(k == 0)
        def _():
            acc[...] = jnp.zeros_like(acc)
        acc[...] += lax.dot_general(a_ref[0], b_ref[...], TN, preferred_element_type=F32)

        @pl.when(k == nk - 1)
        def _():
            o_ref[...] = acc[...].astype(BF)

    return pl.pallas_call(
        body, name=name, grid=(S * njs, nk),
        in_specs=[pl.BlockSpec((1, tk, tn), lambda j, k: (j // njs, k, j % njs)),
                  pl.BlockSpec((tk, D), lambda j, k: (k, 0))] + [pl.BlockSpec(memory_space=pl.ANY)] * len(deps),
        out_specs=pl.BlockSpec((tn, D), lambda j, k: (j, 0)),
        out_shape=jax.ShapeDtypeStruct((S * Ns, D), BF),
        scratch_shapes=[pltpu.VMEM((tn, D), F32)],
        compiler_params=_cp(("parallel", "arbitrary")),
    )(a, b, *deps)


def _nn_bwd_norm(da, w, x, g, sc, dxo, *, tm, name):
    S, T, Ks = da.shape
    tm = _tile(T, tm)
    rc = _tile(tm, 256)

    def body(da_ref, w_ref, x_ref, g_ref, sc_ref, dxo_ref, dx_ref, dsh_ref, dsc_ref, dg_ref, acc):
        @pl.when(pl.program_id(0) == 0)
        def _():
            dsh_ref[...] = jnp.zeros_like(dsh_ref)
            dsc_ref[...] = jnp.zeros_like(dsc_ref)
            dg_ref[...] = jnp.zeros_like(dg_ref)

        d = jnp.dot(da_ref[0], w_ref[0:Ks, :], preferred_element_type=F32)
        for s in range(1, S):
            d = d + jnp.dot(da_ref[s], w_ref[s * Ks:(s + 1) * Ks, :], preferred_element_type=F32)
        acc[...] = d
        gv = g_ref[...]
        sc1 = 1.0 + sc_ref[...]
        dsh = jnp.zeros((1, D), F32)
        dsc = jnp.zeros((1, D), F32)
        dg = jnp.zeros((1, D), F32)
        for r0 in range(0, tm, rc):
            rows = pl.ds(r0, rc)
            u = acc[rows, :]
            xv = x_ref[rows, :]
            r = lax.rsqrt(jnp.mean(xv * xv, axis=-1, keepdims=True) + EPS)
            xh = xv * r
            dsh = dsh + jnp.sum(u, axis=0, keepdims=True)
            dsc = dsc + jnp.sum(u * (xh * gv), axis=0, keepdims=True)
            us = u * sc1
            dg = dg + jnp.sum(us * xh, axis=0, keepdims=True)
            dxh = us * gv
            dx_ref[rows, :] = dxo_ref[rows, :] + r * (dxh - xh * jnp.mean(dxh * xh, axis=-1, keepdims=True))
        dsh_ref[...] += dsh
        dsc_ref[...] += dsc
        dg_ref[...] += dg

    vec = pl.BlockSpec((1, D), _const2)
    rowspec = pl.BlockSpec((tm, D), _row)
    return pl.pallas_call(
        body, name=name, grid=(T // tm,),
        in_specs=[pl.BlockSpec((S, tm, Ks), lambda i: (0, i, 0)), _resident((S * Ks, D)), rowspec, vec, vec, rowspec],
        out_specs=[rowspec, vec, vec, vec],
        out_shape=[jax.ShapeDtypeStruct((T, D), F32)] + [jax.ShapeDtypeStruct((1, D), F32)] * 3,
        scratch_shapes=[pltpu.VMEM((tm, D), F32)],
        compiler_params=_cp(("arbitrary",)),
    )(da, w, x, g, sc, dxo)


def _rope(t, cos, sin_signed, lt32, inverse=False):
    sel = jnp.where(lt32, pltpu.roll(t, 96, 1), pltpu.roll(t, 32, 1))
    return t * cos - sel * sin_signed if inverse else t * cos + sel * sin_signed


def _rope_tables(T, token=None):
    inv = 1.0 / (ROPE_THETA ** (jnp.arange(0, HEAD_DIM, 2, dtype=F32) / HEAD_DIM))
    ang = _behind(jnp.arange(T, dtype=F32)[:, None] * inv[None, :], token)
    cos, sin = jnp.cos(ang), jnp.sin(ang)
    cos128 = jnp.tile(cos, (1, 4))
    sin128 = jnp.tile(jnp.concatenate([-sin, sin], axis=1), (1, 2))
    return cos128, sin128


QSCALE = HEAD_DIM ** -0.5


def _lane_masks(rows):
    lane = lax.broadcasted_iota(jnp.int32, (rows, 128), 1)
    return (lane % HEAD_DIM) < (HEAD_DIM // 2), [lane < HEAD_DIM, lane >= HEAD_DIM]


def _attn_bias():
    qi = lax.broadcasted_iota(jnp.int32, (4 * BLK, 2 * BLK), 0) % BLK
    kj = lax.broadcasted_iota(jnp.int32, (4 * BLK, 2 * BLK), 1)
    band = (kj > qi) & (kj <= qi + BLK)
    return jnp.stack([jnp.where(band & (kj >= BLK), 0.0, NEG_INF), jnp.where(band, 0.0, NEG_INF)]).astype(F32)


def _attn_prep_tile(proj_ref, c_ref, s_ref, qs_ref, kr_ref, tm):
    lt32, halves = _lane_masks(BLK)
    for b in range(tm // BLK):
        rows = pl.ds(b * BLK, BLK)
        cc, sc = c_ref[rows, :], s_ref[rows, :]
        qr = [_rope(proj_ref[rows, pl.ds(O_Q + p * 128, 128)].astype(F32), cc, sc, lt32) * QSCALE for p in range(8)]
        for g in range(N_KV):
            qs_ref[g, pl.ds(4 * b * BLK, 4 * BLK), :] = _stack_heads(qr, g, halves).astype(BF)
        kr_ref[rows, :] = jnp.concatenate([_rope(proj_ref[rows, pl.ds(O_K + r * 128, 128)].astype(F32), cc, sc, lt32)
                                           for r in range(2)], axis=1).astype(BF)


ATT_BPS = 4
ATT_ROWS = ATT_BPS * BLK


def _before(n):
    return jnp.maximum(ATT_BPS * n - 1, 0)


def _attn_specs():
    return [pl.BlockSpec((N_KV, 4 * ATT_ROWS, 128), lambda n: (0, n, 0)),
            pl.BlockSpec((ATT_ROWS, 256), _row), pl.BlockSpec((BLK, 256), lambda n: (_before(n), 0)),
            pl.BlockSpec((ATT_ROWS, 256), lambda n: (n, O_V // 256)),
            pl.BlockSpec((BLK, 256), lambda n: (_before(n), O_V // 256)),
            pl.BlockSpec((2, 4 * BLK, 2 * BLK), lambda n: (0, 0, 0)),
            pl.BlockSpec(memory_space=pltpu.SMEM)]


def _bands(sb, kc_ref, kp_ref, vc_ref, vp_ref):
    own = pl.ds(sb * BLK, BLK)
    above = pl.ds((sb - 1) * BLK, BLK)
    kb, vb = [], []
    for r in range(2):
        cols = pl.ds(r * 128, 128)
        kprev = kp_ref[:, cols] if sb == 0 else kc_ref[above, cols]
        vprev = vp_ref[:, cols] if sb == 0 else vc_ref[above, cols]
        kb.append(jnp.concatenate([kprev, kc_ref[own, cols]], axis=0))
        vb.append(jnp.concatenate([vprev, vc_ref[own, cols]], axis=0))
    return kb, vb


def _block_bias(sb, bias_ref):
    return bias_ref[jnp.minimum(pl.program_id(0), 1)] if sb == 0 else bias_ref[1]


def _sink_rows(sink_ref, g):
    return jnp.concatenate([jnp.full((BLK, 128), sink_ref[4 * g + hh], F32) for hh in range(4)], axis=0)


def _both(t):
    return jnp.concatenate([t, t], axis=1)


def _unstack_heads(t, g, halves, acc):
    half = g % 2
    for hh in range(4):
        h = 4 * g + hh
        th = jnp.where(halves[half], t[hh * BLK:(hh + 1) * BLK], 0.0)
        if h % 2 != half:
            th = pltpu.roll(th, HEAD_DIM, 1)
        acc[h // 2] = acc[h // 2] + th


def _stack_heads(chunks, g, halves):
    half = g % 2
    parts = []
    for hh in range(4):
        h = 4 * g + hh
        t = chunks[h // 2]
        if h % 2 != half:
            t = pltpu.roll(t, HEAD_DIM, 1)
        parts.append(jnp.where(halves[half], t, 0.0))
    return jnp.concatenate(parts, axis=0)


def _attn_fwd(qs, kr, proj, bias, sinks, *, name):
    T = proj.shape[0]
    assert T % ATT_ROWS == 0

    def body(qs_ref, kc_ref, kp_ref, vc_ref, vp_ref, bias_ref, sink_ref, o_ref, lse_ref):
        _, h128 = _lane_masks(BLK)
        _, h256 = _lane_masks(2 * BLK)
        _, h512 = _lane_masks(4 * BLK)
        groups = range(N_KV)
        sink = [_sink_rows(sink_ref, g) for g in groups]
        for sb in range(ATT_BPS):
            rows = pl.ds(4 * sb * BLK, 4 * BLK)
            kb, vb = _bands(sb, kc_ref, kp_ref, vc_ref, vp_ref)
            outs = [jnp.zeros((BLK, 128), F32) for _ in range(8)]
            bias = _block_bias(sb, bias_ref)
            s = [lax.dot_general(qs_ref[g, rows, :], kb[g // 2], NT, preferred_element_type=F32) + bias for g in groups]
            m = [jnp.maximum(jnp.broadcast_to(jnp.max(s[g], axis=-1, keepdims=True), (4 * BLK, 128)), sink[g])
                 for g in groups]
            p = [jnp.exp(s[g] - _both(m[g])).astype(BF) for g in groups]
            vg = [jnp.where(h256[g % 2], vb[g // 2].astype(F32), 1.0).astype(BF) for g in groups]
            o = [jnp.dot(p[g], vg[g], preferred_element_type=F32) for g in groups]
            denom = [jnp.where(h512[g % 2], pltpu.roll(o[g], HEAD_DIM, 1), o[g]) + jnp.exp(sink[g] - m[g])
                     for g in groups]
            for g in groups:
                lse_ref[g, rows, :] = m[g] + jnp.log(denom[g])
                _unstack_heads(o[g] * (1.0 / denom[g]), g, h128, outs)
            o_ref[pl.ds(sb * BLK, BLK), :] = jnp.concatenate(outs, axis=1).astype(BF)

    return pl.pallas_call(
        body, name=name, grid=(T // ATT_ROWS,),
        in_specs=_attn_specs(),
        out_specs=[pl.BlockSpec((ATT_ROWS, D), _row), pl.BlockSpec((N_KV, 4 * ATT_ROWS, 128), lambda n: (0, n, 0))],
        out_shape=[jax.ShapeDtypeStruct((T, D), BF), jax.ShapeDtypeStruct((N_KV, 4 * T, 128), F32)],
        compiler_params=_cp(("parallel",)),
    )(qs, kr, kr, proj, proj, bias, sinks)


def _attn_bwd(qs, kr, proj, bias, sinks, lse, o, do, cos, sin, dproj, *, name):
    T = proj.shape[0]
    assert T % ATT_ROWS == 0

    def body(qs_ref, kc_ref, kp_ref, vc_ref, vp_ref, bias_ref, sink_ref, lse_ref, o_ref, do_ref,
             cc_ref, sc_ref, cp_ref, sp_ref, dproj_ref, dq_ref, dkc_ref, dkp_ref, dvc_ref, dvp_ref, dsink_ref):
        @pl.when(pl.program_id(0) == 0)
        def _():
            dsink_ref[...] = jnp.zeros_like(dsink_ref)
        lt32, h128 = _lane_masks(BLK)
        lane1 = lax.broadcasted_iota(jnp.int32, (1, 128), 1)
        dsink = jnp.zeros((1, 128), F32)
        groups = range(N_KV)
        for sb in range(ATT_BPS):
            own = pl.ds(sb * BLK, BLK)
            rows = pl.ds(4 * sb * BLK, 4 * BLK)
            kb, vb = _bands(sb, kc_ref, kp_ref, vc_ref, vp_ref)
            oc = [o_ref[own, pl.ds(p * 128, 128)].astype(F32) for p in range(8)]
            doc = [do_ref[own, pl.ds(p * 128, 128)].astype(F32) for p in range(8)]
            dqs = [jnp.zeros((BLK, 128), F32) for _ in range(8)]
            bias = _block_bias(sb, bias_ref)
            q = [qs_ref[g, rows, :] for g in groups]
            lse_g = [lse_ref[g, rows, :] for g in groups]
            s = [lax.dot_general(q[g], kb[g // 2], NT, preferred_element_type=F32) + bias for g in groups]
            dos = [_stack_heads(doc, g, h128) for g in groups]
            dosb = [t.astype(BF) for t in dos]
            dp = [lax.dot_general(dosb[g], vb[g // 2], NT, preferred_element_type=F32) for g in groups]
            delta = [jnp.broadcast_to(jnp.sum(dos[g] * _stack_heads(oc, g, h128), axis=-1, keepdims=True),
                                      (4 * BLK, 128)) for g in groups]
            p = [jnp.exp(s[g] - _both(lse_g[g])) for g in groups]
            ds = [(p[g] * (dp[g] - _both(delta[g]))).astype(BF) for g in groups]
            pb = [t.astype(BF) for t in p]
            dvg = [lax.dot_general(pb[g], dosb[g], TN, preferred_element_type=F32) for g in groups]
            dkg = [lax.dot_general(ds[g], q[g], TN, preferred_element_type=F32) for g in groups]
            dqg = [jnp.dot(ds[g], kb[g // 2], preferred_element_type=F32) * QSCALE for g in groups]
            dvr = [dvg[0] + dvg[1], dvg[2] + dvg[3]]
            dkr = [dkg[0] + dkg[1], dkg[2] + dkg[3]]
            for g in groups:
                _unstack_heads(dqg[g], g, h128, dqs)
                dsk = -jnp.exp(_sink_rows(sink_ref, g) - lse_g[g]) * delta[g]
                for hh in range(4):
                    val = jnp.sum(dsk[hh * BLK:(hh + 1) * BLK], axis=0, keepdims=True)
                    dsink = dsink + jnp.where(lane1 == 4 * g + hh, val, 0.0)
            cc, sc = cc_ref[own, :], sc_ref[own, :]
            cp, sp = (cp_ref[...], sp_ref[...]) if sb == 0 else (cc_ref[pl.ds((sb - 1) * BLK, BLK), :],
                                                                  sc_ref[pl.ds((sb - 1) * BLK, BLK), :])
            dq_ref[own, :] = jnp.concatenate([_rope(t, cc, sc, lt32, inverse=True) for t in dqs], axis=1).astype(BF)
            dkp_ref[own, :] = jnp.concatenate([_rope(t[:BLK], cp, sp, lt32, inverse=True) for t in dkr],
                                              axis=1).astype(BF)
            dkc_ref[own, :] = jnp.concatenate([_rope(t[BLK:], cc, sc, lt32, inverse=True) for t in dkr],
                                              axis=1).astype(BF)
            dvp_ref[own, :] = jnp.concatenate([t[:BLK] for t in dvr], axis=1).astype(BF)
            dvc_ref[own, :] = jnp.concatenate([t[BLK:] for t in dvr], axis=1).astype(BF)
        dsink_ref[...] += dsink

    kv = pl.BlockSpec((ATT_ROWS, 256), _row)
    tc = pl.BlockSpec((ATT_ROWS, 128), _row)
    tp = pl.BlockSpec((BLK, 128), lambda n: (_before(n), 0))
    return pl.pallas_call(
        body, name=name, grid=(T // ATT_ROWS,),
        in_specs=_attn_specs() + [pl.BlockSpec((N_KV, 4 * ATT_ROWS, 128), lambda n: (0, n, 0)),
                                  pl.BlockSpec((ATT_ROWS, D), _row), pl.BlockSpec((ATT_ROWS, D), _row), tc, tc, tp, tp,
                                  pl.BlockSpec(memory_space=pl.ANY)],
        out_specs=[pl.BlockSpec((ATT_ROWS, D), lambda n: (n, O_Q // D)), kv, kv, kv, kv,
                   pl.BlockSpec((1, 128), _const2)],
        out_shape=[jax.ShapeDtypeStruct(dproj.shape, BF)] + [jax.ShapeDtypeStruct((T, 256), BF)] * 4
        + [jax.ShapeDtypeStruct((1, 128), F32)],
        input_output_aliases={14: 0},
        compiler_params=_cp(("arbitrary",)),
    )(qs, kr, kr, proj, proj, bias, sinks, lse, o, do, cos, sin, cos, sin, dproj)


def _dkv_combine(dkc, dkp, dvc, dvp, dproj, *, name):
    T = dkc.shape[0]
    nb = T // BLK
    tm = _tile(T, 4 * BLK)
    bpt = tm // BLK
    nt = T // tm

    def body(dkc_ref, dkp_ref, dkn_ref, dvc_ref, dvp_ref, dvn_ref, dproj_ref, o_ref):
        keep = jnp.where(pl.program_id(0) == nt - 1, 0.0, 1.0)

        def shifted(prev_ref, next_ref):
            nxt = keep * next_ref[...].astype(F32)
            return nxt if bpt == 1 else jnp.concatenate([prev_ref[BLK:, :].astype(F32), nxt], axis=0)

        o_ref[:, 0:256] = (dkc_ref[...].astype(F32) + shifted(dkp_ref, dkn_ref)).astype(BF)
        o_ref[:, 256:512] = (dvc_ref[...].astype(F32) + shifted(dvp_ref, dvn_ref)).astype(BF)

    cur = pl.BlockSpec((tm, 256), _row)
    nxt = pl.BlockSpec((BLK, 256), lambda i: (jnp.minimum((i + 1) * bpt, nb - 1), 0))
    return pl.pallas_call(
        body, name=name, grid=(nt,),
        in_specs=[cur, cur, nxt, cur, cur, nxt, pl.BlockSpec(memory_space=pl.ANY)],
        out_specs=pl.BlockSpec((tm, 512), lambda i: (i, O_K // 512)),
        out_shape=jax.ShapeDtypeStruct(dproj.shape, BF),
        input_output_aliases={6: 0},
        compiler_params=_cp(("parallel",)),
    )(dkc, dkp, dkp, dvc, dvp, dvp, dproj)


HALO = 16


def _conv_shifts(cu, hprev, tm):
    row = lax.broadcasted_iota(jnp.int32, (8, cu.shape[1]), 0)
    h1 = hprev[HALO - 1:HALO, :]
    h2 = hprev[HALO - 2:HALO - 1, :]
    m1 = pltpu.roll(cu, 1, 0)
    m2 = pltpu.roll(cu, 2, 0)
    m1 = jnp.concatenate([jnp.where(row == 0, h1, m1[0:8]), m1[8:]], axis=0)
    m2 = jnp.concatenate([jnp.where(row == 0, h2, jnp.where(row == 1, h1, m2[0:8])), m2[8:]], axis=0)
    return m1, m2


def _mixer_mid_fwd(proj, attn, wcp, wap, wout, convw, x, gt, *, tm, name):
    T = x.shape[0]
    tm = _tile(T, tm)
    hb = tm // HALO

    def body(bg_ref, cg_ref, u_ref, hcg_ref, hu_ref, zc0_ref, zc1_ref, za0_ref, za1_ref, at_ref,
             wcp_ref, wap_ref, wout_ref, cw_ref, x_ref, gt_ref,
             x2_ref, gc_ref, yc_ref, ya_ref, mg_ref, o_ref):
        first = jnp.where(pl.program_id(0) == 0, 0.0, 1.0)
        cu = cg_ref[...].astype(F32) * u_ref[...].astype(F32)
        hprev = first * (hcg_ref[...].astype(F32) * hu_ref[...].astype(F32))
        m1, m2 = _conv_shifts(cu, hprev, tm)
        cv = cw_ref[0:1, :] * m2 + cw_ref[1:2, :] * m1 + cw_ref[2:3, :] * cu
        gc = (bg_ref[...].astype(F32) * cv).astype(BF)
        gc_ref[...] = gc
        yc = jnp.dot(gc, wcp_ref[...], preferred_element_type=F32)
        ya = jnp.dot(at_ref[...], wap_ref[...], preferred_element_type=F32)
        yc_ref[...] = yc.astype(BF)
        ya_ref[...] = ya.astype(BF)
        zc = jnp.concatenate([zc0_ref[...], zc1_ref[...]], axis=1).astype(F32)
        za = jnp.concatenate([za0_ref[...], za1_ref[...]], axis=1).astype(F32)
        mg = (_sigmoid(zc) * yc + _sigmoid(za) * ya).astype(BF)
        mg_ref[...] = mg
        o = jnp.dot(mg, wout_ref[...], preferred_element_type=F32)
        o_ref[...] = o.astype(BF)
        x2_ref[...] = x_ref[...] + gt_ref[...] * o

    wspec = pl.BlockSpec((D, D), _const2)
    rowspec = pl.BlockSpec((tm, D), _row)
    return pl.pallas_call(
        body, name=name, grid=(T // tm,),
        in_specs=[_col(tm, O_BG), _col(tm, O_CG), _col(tm, O_U), _halo_prev(hb, O_CG), _halo_prev(hb, O_U),
                  _col(tm, O_ZC, 512), _col(tm, O_ZC + 512, 512), _col(tm, O_ZA, 512), _col(tm, O_ZA + 512, 512),
                  rowspec, wspec, wspec, wspec, pl.BlockSpec((8, D), _const2), rowspec, pl.BlockSpec((1, D), _const2)],
        out_specs=[rowspec] * 6,
        out_shape=[jax.ShapeDtypeStruct((T, D), F32)] + [jax.ShapeDtypeStruct((T, D), BF)] * 5,
        compiler_params=_cp(("parallel",)),
    )(proj, proj, proj, proj, proj, proj, proj, proj, proj, attn, wcp, wap, wout, convw, x, gt)


def _col(tm, c, w=D):
    assert c % w == 0
    return pl.BlockSpec((tm, w), lambda i: (i, c // w))


def _halo_prev(hb, c):
    return pl.BlockSpec((HALO, D), lambda i: (jnp.maximum(i * hb - 1, 0), c // D))


def _halo_next(hb, nblk, c=0):
    return pl.BlockSpec((HALO, D), lambda i: (jnp.minimum((i + 1) * hb, nblk - 1), c // D))


def _mixer_mid_bwd(dx2, gt, o, proj, yc, ya, wout, wcp, wap, *, tm, name):
    T = dx2.shape[0]
    tm = _tile(T, tm)
    nt = T // tm

    def body(dx_ref, gt_ref, o_ref, zc0_ref, zc1_ref, za0_ref, za1_ref, yc_ref, ya_ref, wout_ref, wcp_ref, wap_ref,
             dout_ref, dyc_ref, dya_ref, dgc_ref, dat_ref, dproj_ref, dgt_ref, dzs, sems):
        i = pl.program_id(0)
        slot = lax.rem(i, 2)

        def slab_copy(step, s):
            return pltpu.make_async_copy(
                dzs.at[s], dproj_ref.at[pl.ds(pl.multiple_of(step * tm, tm), tm), pl.ds(O_ZC, 2 * D)], sems.at[s])

        @pl.when(i == 0)
        def _():
            dgt_ref[...] = jnp.zeros_like(dgt_ref)

        dxv = dx_ref[...]
        dgt_ref[...] += jnp.sum(dxv * o_ref[...].astype(F32), axis=0, keepdims=True)
        dout = (gt_ref[...] * dxv).astype(BF)
        dout_ref[...] = dout
        dmg = lax.dot_general(dout, wout_ref[...], NT, preferred_element_type=F32)
        sc = _sigmoid(jnp.concatenate([zc0_ref[...], zc1_ref[...]], axis=1).astype(F32))
        sa = _sigmoid(jnp.concatenate([za0_ref[...], za1_ref[...]], axis=1).astype(F32))
        dyc = (dmg * sc).astype(BF)
        dya = (dmg * sa).astype(BF)
        dyc_ref[...] = dyc
        dya_ref[...] = dya
        dzs[slot, :, 0:D] = (dmg * yc_ref[...].astype(F32) * (sc * (1.0 - sc))).astype(BF)
        dzs[slot, :, D:2 * D] = (dmg * ya_ref[...].astype(F32) * (sa * (1.0 - sa))).astype(BF)
        slab_copy(i, slot).start()
        dgc_ref[...] = lax.dot_general(dyc, wcp_ref[...], NT, preferred_element_type=F32).astype(BF)
        dat_ref[...] = lax.dot_general(dya, wap_ref[...], NT, preferred_element_type=F32).astype(BF)

        @pl.when(i > 0)
        def _():
            slab_copy(i - 1, 1 - slot).wait()

        @pl.when(i == nt - 1)
        def _():
            slab_copy(i, slot).wait()

    def zcol(c):
        return pl.BlockSpec((tm, 512), lambda i: (i, c // 512))

    wspec = pl.BlockSpec((D, D), _const2)
    rowspec = pl.BlockSpec((tm, D), _row)
    vec = pl.BlockSpec((1, D), _const2)
    return pl.pallas_call(
        body, name=name, grid=(nt,),
        in_specs=[rowspec, vec, rowspec, zcol(O_ZC), zcol(O_ZC + 512), zcol(O_ZA), zcol(O_ZA + 512),
                  rowspec, rowspec, wspec, wspec, wspec],
        out_specs=[rowspec] * 5 + [pl.BlockSpec(memory_space=pl.ANY), vec],
        out_shape=[jax.ShapeDtypeStruct((T, D), BF)] * 5 + [jax.ShapeDtypeStruct((T, NIN), BF),
                                                            jax.ShapeDtypeStruct((1, D), F32)],
        scratch_shapes=[pltpu.VMEM((2, tm, 2 * D), BF), pltpu.SemaphoreType.DMA((2,))],
        compiler_params=_cp(("arbitrary",)),
    )(dx2, gt, o, proj, proj, proj, proj, yc, ya, wout, wcp, wap)


def _conv_bwd(dgc, proj, convw, dproj, *, tm, name):
    T = dgc.shape[0]
    tm = _tile(T, tm)
    hb = tm // HALO
    nblk = T // HALO
    nt = T // tm

    def body(dgc_ref, ndgc_ref, bg_ref, nbg_ref, cg_ref, u_ref, hcg_ref, hu_ref, cw_ref, dproj_ref, dp_ref, dcw_ref):
        i = pl.program_id(0)

        @pl.when(i == 0)
        def _():
            dcw_ref[...] = jnp.zeros_like(dcw_ref)
        first = jnp.where(i == 0, 0.0, 1.0)
        last = jnp.where(i == nt - 1, 0.0, 1.0)
        cg = cg_ref[...].astype(F32)
        u = u_ref[...].astype(F32)
        bg = bg_ref[...].astype(F32)
        dg = dgc_ref[...].astype(F32)
        cu = cg * u
        hprev = first * (hcg_ref[...].astype(F32) * hu_ref[...].astype(F32))
        m1, m2 = _conv_shifts(cu, hprev, tm)
        w0, w1, w2 = cw_ref[0:1, :], cw_ref[1:2, :], cw_ref[2:3, :]
        cv = w0 * m2 + w1 * m1 + w2 * cu
        dcv = dg * bg
        nxt = last * (ndgc_ref[...].astype(F32) * nbg_ref[...].astype(F32))
        n0, n1 = nxt[0:1, :], nxt[1:2, :]
        row = lax.broadcasted_iota(jnp.int32, (8, D), 0)
        p1 = pltpu.roll(dcv, tm - 1, 0)
        p2 = pltpu.roll(dcv, tm - 2, 0)
        p1 = jnp.concatenate([p1[:tm - 8], jnp.where(row == 7, n0, p1[tm - 8:])], axis=0)
        p2 = jnp.concatenate([p2[:tm - 8], jnp.where(row == 7, n1, jnp.where(row == 6, n0, p2[tm - 8:]))], axis=0)
        dcu = w2 * dcv + w1 * p1 + w0 * p2
        dp_ref[:, 0:D] = (dg * cv).astype(BF)
        dp_ref[:, D:2 * D] = (dcu * u).astype(BF)
        dp_ref[:, 2 * D:3 * D] = (dcu * cg).astype(BF)
        dcw_ref[0:1, :] += jnp.sum(dcv * m2, axis=0, keepdims=True)
        dcw_ref[1:2, :] += jnp.sum(dcv * m1, axis=0, keepdims=True)
        dcw_ref[2:3, :] += jnp.sum(dcv * cu, axis=0, keepdims=True)

    rowspec = pl.BlockSpec((tm, D), _row)
    cw = pl.BlockSpec((8, D), _const2)
    return pl.pallas_call(
        body, name=name, grid=(nt,),
        in_specs=[rowspec, _halo_next(hb, nblk), _col(tm, O_BG), _halo_next(hb, nblk, O_BG),
                  _col(tm, O_CG), _col(tm, O_U), _halo_prev(hb, O_CG), _halo_prev(hb, O_U), cw,
                  pl.BlockSpec(memory_space=pl.ANY)],
        out_specs=[pl.BlockSpec((tm, 3 * D), _row), cw],
        out_shape=[jax.ShapeDtypeStruct(dproj.shape, BF), jax.ShapeDtypeStruct((8, D), F32)],
        input_output_aliases={9: 0},
        compiler_params=_cp(("arbitrary",)),
    )(dgc, dgc, proj, proj, proj, proj, proj, proj, convw, dproj)


def _adam_math(w, g, m, v):
    nm = ADAM_B1 * m + (1.0 - ADAM_B1) * g
    nv = ADAM_B2 * v + (1.0 - ADAM_B2) * (g * g)
    m_hat = nm / (1.0 - ADAM_B1 ** ADAM_STEP)
    v_hat = nv / (1.0 - ADAM_B2 ** ADAM_STEP)
    return -ADAM_LR * (m_hat / (jnp.sqrt(v_hat) + ADAM_EPS) + ADAM_WD * w), nm, nv


SMALL = ("b_ada", "g_ffn1", "g_mix", "g_ffn2", "g_final", "conv_w", "sinks")


def _adam_small(gsum, conv_g, w, m, v, *, name):
    nsm = len(SMALL)

    def body(*refs):
        gs_ref, cg_ref = refs[0], refs[1]
        w_refs, m_refs, v_refs = (refs[2 + k * nsm:2 + (k + 1) * nsm] for k in range(3))
        outs = refs[2 + 3 * nsm:]
        for p, n in enumerate(SMALL):
            if n == "b_ada":
                pieces = [(slice(None), slice(r * D, (r + 1) * D), gs_ref[R_MODS + r:R_MODS + r + 1, :])
                          for r in range(N_MOD)]
            elif n == "conv_w":
                pieces = [(slice(None), slice(None), cg_ref[...])]
            elif n == "sinks":
                pieces = [(slice(None), slice(None), gs_ref[R_SINK:R_SINK + 1, 0:N_HEADS])]
            else:
                row = dict(g_ffn1=R_G1, g_mix=R_GM, g_ffn2=R_G2, g_final=R_GF)[n]
                pieces = [(slice(None), slice(None), gs_ref[row:row + 1, :])]
            for rs, cs, g in pieces:
                d, nm, nv = _adam_math(w_refs[p][rs, cs], g, m_refs[p][rs, cs], v_refs[p][rs, cs])
                for k, val in enumerate((g, d, nm, nv)):
                    outs[k * nsm + p][rs, cs] = val

    args = [gsum, conv_g] + [d[n] for d in (w, m, v) for n in SMALL]
    shapes = [jax.ShapeDtypeStruct(w[n].shape, F32) for _ in range(4) for n in SMALL]
    res = pl.pallas_call(body, name=name, out_shape=shapes, compiler_params=_cp())(*args)
    return [dict(zip(SMALL, res[k * nsm:(k + 1) * nsm])) for k in range(4)]


def _adam(w, g, m, v, *, tm, name):
    _, R, C = w.shape
    tm = _tile(R, tm)
    parts = g.ndim == 3

    def body(w_ref, g_ref, m_ref, v_ref, go_ref, d_ref, nm_ref, nv_ref):
        if parts:
            gv = g_ref[0].astype(F32)
            for s in range(1, N_DEV):
                gv = gv + g_ref[s].astype(F32)
        else:
            gv = g_ref[...]
        go_ref[0] = gv
        d_ref[0], nm_ref[0], nv_ref[0] = _adam_math(w_ref[0], gv, m_ref[0], v_ref[0])

    spec = pl.BlockSpec((1, tm, C), lambda i: (0, i, 0))
    gspec = pl.BlockSpec((N_DEV, tm, C), lambda i: (0, i, 0)) if parts else pl.BlockSpec((tm, C), _row)
    return pl.pallas_call(
        body, name=name, grid=(R // tm,),
        in_specs=[spec, gspec, spec, spec], out_specs=[spec] * 4,
        out_shape=[jax.ShapeDtypeStruct((1, R, C), F32)] * 4,
        compiler_params=_cp(("parallel",)),
    )(w, g, m, v)


def _mods_part(c_all, w_ada, b_ada, *, name):
    C = w_ada.shape[1]

    def body(c_ref, w_ref, b_ref, o_ref):
        cv = c_ref[...]
        ca = cv * jax.nn.sigmoid(cv)
        o_ref[...] = jnp.dot(ca, w_ref[...], preferred_element_type=F32,
                             precision=lax.Precision.HIGHEST) + b_ref[...]

    return pl.pallas_call(
        body, name=name,
        out_shape=jax.ShapeDtypeStruct((N_DEV, C), F32),
        compiler_params=_cp(),
    )(c_all, w_ada, b_ada)


def _wada_grad(c_all_t, gm, *, name):
    C = gm.shape[1]

    def body(c_ref, g_ref, o_ref):
        cv = c_ref[...]
        ca = cv * jax.nn.sigmoid(cv)
        acc = ca[:, 0:1] * g_ref[0:1, :]
        for b in range(1, N_DEV):
            acc = acc + ca[:, b:b + 1] * g_ref[b:b + 1, :]
        o_ref[...] = acc

    return pl.pallas_call(
        body, name=name,
        out_shape=jax.ShapeDtypeStruct((D, C), F32),
        compiler_params=_cp(),
    )(c_all_t, gm)


def _peer(x, y, c, d):
    px = lax.rem(x + ((d >> 2) & 1), 2)
    py = lax.rem(y + ((d >> 1) & 1), 2)
    pc = lax.rem(c + (d & 1), 2)
    return (px, py, pc), 4 * px + 2 * py + pc


def _exchange(xs, *, scatter, name):
    n = len(xs)
    nsem = n * (N_DEV - 1)

    def body(*refs):
        ins, outs = refs[:n], refs[n:2 * n]
        token, send_sems, recv_sems, local_sems = refs[2 * n:]
        x, y, c = lax.axis_index("x"), lax.axis_index("y"), lax.axis_index("c")
        me = 4 * x + 2 * y + c
        token[...] = jnp.zeros_like(token)

        def src(t, idx):
            return ins[t].at[idx] if scatter else ins[t]

        local = [pltpu.make_async_copy(src(t, me), outs[t].at[me], local_sems.at[t]) for t in range(n)]
        for cp in local:
            cp.start()
        remote = []
        for t in range(n):
            for d in range(1, N_DEV):
                peer, pidx = _peer(x, y, c, d)
                k = t * (N_DEV - 1) + d - 1
                send = pltpu.make_async_remote_copy(src_ref=src(t, pidx), dst_ref=outs[t].at[me],
                                                    send_sem=send_sems.at[k], recv_sem=recv_sems.at[k],
                                                    device_id=peer, device_id_type=MESH)
                recv = pltpu.make_async_remote_copy(src_ref=src(t, pidx), dst_ref=outs[t].at[pidx],
                                                    send_sem=send_sems.at[k], recv_sem=recv_sems.at[k],
                                                    device_id=peer, device_id_type=MESH)
                send.start()
                remote.append((send, recv))
        for cp in local:
            cp.wait()
        for send, recv in remote:
            send.wait_send()
            recv.wait_recv()

    anyspec = pl.BlockSpec(memory_space=pl.ANY)
    out_shape = [jax.ShapeDtypeStruct(a.shape if scatter else (N_DEV,) + a.shape, a.dtype) for a in xs]
    out_shape.append(jax.ShapeDtypeStruct((8, 128), F32))
    return pl.pallas_call(
        body, name=name,
        in_specs=[anyspec] * n, out_specs=[anyspec] * n + [pl.BlockSpec(memory_space=pltpu.VMEM)],
        out_shape=out_shape,
        scratch_shapes=[pltpu.SemaphoreType.DMA((nsem,)), pltpu.SemaphoreType.DMA((nsem,)),
                        pltpu.SemaphoreType.DMA((n,))],
    )(*xs)


def _sum8(parts, *, name):
    _, R, C = parts.shape

    def body(p_ref, o_ref):
        acc = p_ref[0]
        for s in range(1, N_DEV):
            acc = acc + p_ref[s]
        o_ref[...] = acc

    return pl.pallas_call(body, name=name, out_shape=jax.ShapeDtypeStruct((R, C), F32),
                          compiler_params=_cp())(parts)


HBM_SPEC = pl.BlockSpec(memory_space=pltpu.HBM)
SEM_SPEC = pl.BlockSpec(memory_space=pltpu.SEMAPHORE)
N_PEER = N_DEV - 1


def _split_copies(src_refs, land_refs, send_sems, recv_sems, scatter):
    x, y, c = lax.axis_index("x"), lax.axis_index("y"), lax.axis_index("c")
    me = 4 * x + 2 * y + c
    pairs = []
    for j, (src, land) in enumerate(zip(src_refs, land_refs)):
        for d in range(1, N_DEV):
            peer, pidx = _peer(x, y, c, d)
            k = j * N_PEER + d - 1
            s = src.at[pidx] if scatter else src
            send = pltpu.make_async_remote_copy(src_ref=s, dst_ref=land.at[me], send_sem=send_sems.at[k],
                                                recv_sem=recv_sems.at[k], device_id=peer, device_id_type=MESH)
            recv = pltpu.make_async_remote_copy(src_ref=s, dst_ref=land.at[pidx], send_sem=send_sems.at[k],
                                                recv_sem=recv_sems.at[k], device_id=peer, device_id_type=MESH)
            pairs.append((send, recv))
    return pairs


def _own_slot(block, me):
    land = lax.empty((N_DEV,) + block.shape, block.dtype)
    return lax.dynamic_update_slice(land, block[None], (me, 0, 0))


def _split_start(srcs, lands, groups, *, scatter, name):
    n, ng = len(srcs), len(groups)

    def body(*refs):
        src_refs, land_refs = refs[:n], refs[n:2 * n]
        sems = refs[2 * n:2 * n + 2 * ng]
        token = refs[-1]
        for gi, g in enumerate(groups):
            pairs = _split_copies([src_refs[t] for t in g], [land_refs[t] for t in g], sems[2 * gi],
                                  sems[2 * gi + 1], scatter)
            for send, _ in pairs:
                send.start()
        token[...] = jnp.zeros_like(token)

    sem_shapes = []
    for g in groups:
        sem_shapes += [pltpu.SemaphoreType.DMA((len(g) * N_PEER,))] * 2
    thru = [pltpu.HBM(a.shape, a.dtype) for a in list(srcs) + list(lands)]
    outs = pl.pallas_call(
        body, name=name,
        out_shape=tuple(sem_shapes + thru + [jax.ShapeDtypeStruct((8, 128), F32)]),
        in_specs=[HBM_SPEC] * (2 * n),
        out_specs=tuple([SEM_SPEC] * (2 * ng) + [HBM_SPEC] * (2 * n) + [pl.BlockSpec(memory_space=pltpu.VMEM)]),
        input_output_aliases={i: 2 * ng + i for i in range(2 * n)},
        compiler_params=pltpu.CompilerParams(has_side_effects=pltpu.SideEffectType.DATAFLOW_SIDE_EFFECTING),
    )(*[pltpu.with_memory_space_constraint(a, pltpu.HBM) for a in list(srcs) + list(lands)])
    sems = [(outs[2 * gi], outs[2 * gi + 1]) for gi in range(ng)]
    return sems, outs[2 * ng:2 * ng + n], outs[2 * ng + n:2 * ng + 2 * n], outs[-1]


def _behind(v, token):
    if token is None:
        return v
    return v + token[0, 0].astype(v.dtype)


def _split_wait(srcs, lands, sems, after, *, scatter, name):
    m = len(srcs)

    def body(*refs):
        src_refs, land_refs = refs[:m], refs[m:2 * m]
        send_sems, recv_sems = refs[2 * m], refs[2 * m + 1]
        for send, recv in _split_copies(src_refs, land_refs, send_sems, recv_sems, scatter):
            send.wait_send()
            recv.wait_recv()

    outs = pl.pallas_call(
        body, name=name,
        out_shape=tuple(pltpu.HBM(a.shape, a.dtype) for a in list(srcs) + list(lands)),
        in_specs=[HBM_SPEC] * (2 * m) + [SEM_SPEC, SEM_SPEC, pl.BlockSpec(memory_space=pl.ANY)],
        out_specs=tuple([HBM_SPEC] * (2 * m)),
        input_output_aliases={i: i for i in range(2 * m)},
        compiler_params=pltpu.CompilerParams(has_side_effects=pltpu.SideEffectType.DATAFLOW_SIDE_EFFECTING),
    )(*srcs, *lands, sems[0], sems[1], after)
    return outs[m:]


TL_FIRST = (1, 2, 4, 6)
TL_ICI = (2, 4, 6)
EFFECT = pltpu.SideEffectType.DATAFLOW_SIDE_EFFECTING


def _tl_first(src_refs, land_refs, send_sems, recv_sems):
    x, y, c = lax.axis_index("x"), lax.axis_index("y"), lax.axis_index("c")
    me = 4 * x + 2 * y + c
    out = []
    for j, (src, land) in enumerate(zip(src_refs, land_refs)):
        for i, d in enumerate(TL_FIRST):
            peer, pidx = _peer(x, y, c, d)
            k = len(TL_FIRST) * j + i
            send = pltpu.make_async_remote_copy(src_ref=src, dst_ref=land.at[me], send_sem=send_sems.at[k],
                                                recv_sem=recv_sems.at[k], device_id=peer, device_id_type=MESH)
            recv = pltpu.make_async_remote_copy(src_ref=src, dst_ref=land.at[pidx], send_sem=send_sems.at[k],
                                                recv_sem=recv_sems.at[k], device_id=peer, device_id_type=MESH)
            out.append((d, send, recv))
    return out


def _tl_second(land_refs, send_sems, recv_sems):
    x, y, c = lax.axis_index("x"), lax.axis_index("y"), lax.axis_index("c")
    sibling, _ = _peer(x, y, c, 1)
    out = []
    for j, land in enumerate(land_refs):
        for i, d in enumerate(TL_ICI):
            _, mine = _peer(x, y, c, d)
            _, theirs = _peer(x, y, c, d + 1)
            k = len(TL_ICI) * j + i
            send = pltpu.make_async_remote_copy(src_ref=land.at[mine], dst_ref=land.at[mine], send_sem=send_sems.at[k],
                                                recv_sem=recv_sems.at[k], device_id=sibling, device_id_type=MESH)
            recv = pltpu.make_async_remote_copy(src_ref=land.at[mine], dst_ref=land.at[theirs],
                                                send_sem=send_sems.at[k], recv_sem=recv_sems.at[k],
                                                device_id=sibling, device_id_type=MESH)
            out.append((send, recv))
    return out


def _tl_start(srcs, lands, groups, *, name):
    n, ng = len(srcs), len(groups)

    def body(*refs):
        src_refs, land_refs = refs[:n], refs[n:2 * n]
        sems = refs[2 * n:2 * n + 2 * ng]
        for gi, g in enumerate(groups):
            for _, send, _ in _tl_first([src_refs[t] for t in g], [land_refs[t] for t in g], sems[2 * gi],
                                        sems[2 * gi + 1]):
                send.start()
        refs[-1][...] = jnp.zeros_like(refs[-1])

    sem_shapes = []
    for g in groups:
        sem_shapes += [pltpu.SemaphoreType.DMA((len(g) * len(TL_FIRST),))] * 2
    thru = [pltpu.HBM(a.shape, a.dtype) for a in list(srcs) + list(lands)]
    outs = pl.pallas_call(
        body, name=name,
        out_shape=tuple(sem_shapes + thru + [jax.ShapeDtypeStruct((8, 128), F32)]),
        in_specs=[HBM_SPEC] * (2 * n),
        out_specs=tuple([SEM_SPEC] * (2 * ng) + [HBM_SPEC] * (2 * n) + [pl.BlockSpec(memory_space=pltpu.VMEM)]),
        input_output_aliases={i: 2 * ng + i for i in range(2 * n)},
        compiler_params=pltpu.CompilerParams(has_side_effects=EFFECT),
    )(*[pltpu.with_memory_space_constraint(a, pltpu.HBM) for a in list(srcs) + list(lands)])
    sems = [(outs[2 * gi], outs[2 * gi + 1]) for gi in range(ng)]
    return sems, outs[2 * ng:2 * ng + n], outs[2 * ng + n:2 * ng + 2 * n], outs[-1]


def _tl_forward(srcs, lands, sems1, after, *, name):
    m = len(srcs)

    def body(*refs):
        src_refs, land_refs = refs[:m], refs[m:2 * m]
        send1, recv1 = refs[2 * m], refs[2 * m + 1]
        send2, recv2 = refs[2 * m + 3], refs[2 * m + 4]
        for d, _, recv in _tl_first(src_refs, land_refs, send1, recv1):
            if d in TL_ICI:
                recv.wait_recv()
        for send, _ in _tl_second(land_refs, send2, recv2):
            send.start()

    sem = pltpu.SemaphoreType.DMA((m * len(TL_ICI),))
    outs = pl.pallas_call(
        body, name=name,
        out_shape=tuple([sem, sem] + [pltpu.HBM(a.shape, a.dtype) for a in list(srcs) + list(lands)]),
        in_specs=[HBM_SPEC] * (2 * m) + [SEM_SPEC, SEM_SPEC, pl.BlockSpec(memory_space=pl.ANY)],
        out_specs=tuple([SEM_SPEC, SEM_SPEC] + [HBM_SPEC] * (2 * m)),
        input_output_aliases={i: 2 + i for i in range(2 * m)},
        compiler_params=pltpu.CompilerParams(has_side_effects=EFFECT),
    )(*srcs, *lands, sems1[0], sems1[1], after)
    return (outs[0], outs[1]), outs[2:2 + m], outs[2 + m:2 + 2 * m]


def _tl_wait(srcs, lands, sems1, sems2, after, *, name):
    m = len(srcs)

    def body(*refs):
        src_refs, land_refs = refs[:m], refs[m:2 * m]
        send1, recv1, send2, recv2 = refs[2 * m:2 * m + 4]
        for d, send, recv in _tl_first(src_refs, land_refs, send1, recv1):
            send.wait_send()
            if d not in TL_ICI:
                recv.wait_recv()
        for send, recv in _tl_second(land_refs, send2, recv2):
            send.wait_send()
            recv.wait_recv()

    outs = pl.pallas_call(
        body, name=name,
        out_shape=tuple(pltpu.HBM(a.shape, a.dtype) for a in list(srcs) + list(lands)),
        in_specs=[HBM_SPEC] * (2 * m) + [SEM_SPEC] * 4 + [pl.BlockSpec(memory_space=pl.ANY)],
        out_specs=tuple([HBM_SPEC] * (2 * m)),
        input_output_aliases={i: i for i in range(2 * m)},
        compiler_params=pltpu.CompilerParams(has_side_effects=EFFECT),
    )(*srcs, *lands, sems1[0], sems1[1], sems2[0], sems2[1], after)
    return outs[m:]


TM_PROJ = 512
TN_PROJ = 512
TM_ROW = 512
TM_NN = 512
TK_TN = 2048
TM_ADAM = 208
TN_FFN = F // 2
TN_IN = NIN // 4


def _tn(a, b, name, tn, token=None):
    if a.ndim == 2:
        a = a[None]
    return _tn_matmul(a, b, token, tn=tn, tk=TK_TN, name=name)


def _local_step(x, tgt, mods, g1, gm, g2, gf, convw8, sinks, w_get, g_put, tables=None):
    T = x.shape[0]
    sh1, sc1, gt1, sh2, sc2, gt2, sh3, sc3, gt3 = [mods[i:i + 1] for i in range(N_MOD)]
    cos, sin = _rope_tables(T) if tables is None else tables
    behind = _behind

    w = dict(w_get("gu1", mods))
    h1, ab1 = _norm_proj(x, g1, sc1, sh1, w["gu1"], tm=TM_PROJ, tn=TN_PROJ, name="ffn1_up")
    w.update(w_get("d1", ab1))
    x1 = _ffn_down_fwd(ab1, w["d1"], x, gt1, tm=TM_ROW, name="ffn1_down")
    w.update(w_get("mix", x1))
    h2, proj, qs, kr = _norm_proj(x1, gm, sc2, sh2, w["win"], (cos, sin), tm=TM_PROJ, tn=TN_PROJ, name="mix_in")
    bias = _attn_bias()
    attn, lse = _attn_fwd(qs, kr, proj, bias, sinks, name="attn_fwd")
    x2, gc, yc, ya, mg, o = _mixer_mid_fwd(proj, attn, w["cp"], w["ap"], w["out"], convw8, x1, gt2,
                                           tm=TM_ROW, name="mix_mid")
    w.update(w_get("ffn2", x2))
    h3, ab2, dx3, lsum, dgf = _ffn_fwd(x2, g2, sc3, sh3, gt3, w["gu2"], w["d2"], (tgt, gf), tm=TM_ROW,
                                           name="ffn2_final")

    dab2, dgt3, g_d2 = _ffn_down_bwd_dw(dx3, gt3, ab2, w["d2"], tm=TM_ROW, name="ffn2_down_bwd")
    dx2, dsh3, dsc3, dg2 = _nn_bwd_norm(dab2, w["gu2"], x2, g2, sc3, dx3, tm=TM_NN, name="ffn2_up_bwd")
    g_gu2 = _tn(dab2, h3, "ffn2_up_dw", TN_FFN)
    tok = g_put(dict(gu2=g_gu2, d2=g_d2))

    dout, dyc, dya, dgc, dat, dproj, dgt2 = _mixer_mid_bwd(dx2, behind(gt2, tok), o, proj, yc, ya, w["out"], w["cp"],
                                                           w["ap"], tm=TM_ROW, name="mix_mid_bwd")
    g_out = _tn(mg, dout, "mix_out_dw", D)
    g_cp = _tn(gc, dyc, "mix_cp_dw", D)
    g_ap = _tn(attn, dya, "mix_ap_dw", D)
    dproj, dkc, dkp, dvc, dvp, dsink = _attn_bwd(qs, kr, proj, bias, sinks, lse, attn, dat, cos, sin, dproj,
                                                 name="attn_bwd")
    dproj = _dkv_combine(dkc, dkp, dvc, dvp, dproj, name="attn_dkv")
    dproj, dcw = _conv_bwd(dgc, proj, convw8, dproj, tm=TM_ROW, name="conv_bwd")
    g_in = _tn(dproj, h2, "mix_in_dw", TN_IN)
    tok = g_put(dict(win=g_in, cp=g_cp, ap=g_ap, out=g_out))
    dx1, dsh2, dsc2, dgm = _nn_bwd_norm(dproj[None], w["win"], x1, gm, behind(sc2, tok), dx2, tm=TM_NN,
                                        name="mix_in_bwd")

    dab1, dgt1, g_d1 = _ffn_down_bwd_dw(dx1, gt1, ab1, w["d1"], tm=TM_ROW, name="ffn1_down_bwd")
    tok = g_put(dict(d1=g_d1))
    g_gu1 = _tn(dab1, h1, "ffn1_up_dw", TN_FFN, tok)
    tok = g_put(dict(gu1=g_gu1))
    dx0, dsh1, dsc1, dg1 = _nn_bwd_norm(dab1, w["gu1"], x, g1, behind(sc1, tok), dx1, tm=TM_NN,
                                        name="ffn1_up_bwd")

    small = dict(mods=jnp.concatenate([dsh1, dsc1, dgt1, dsh2, dsc2, dgt2, dsh3, dsc3, dgt3], axis=0),
                 g1=dg1, gm=dgm, g2=dg2, gf=dgf, convw=dcw[0:3], sinks=dsink[:, 0:N_HEADS])
    return lsum, dx0, small


BIG = ("gu1", "d1", "win", "cp", "ap", "out", "gu2", "d2")
TRANSPOSED = ("gu1", "win", "gu2")
SMALL_ROWS = 24
R_MODS, R_G1, R_GM, R_G2, R_GF, R_CONV, R_SINK, R_LOSS = 0, 9, 10, 11, 12, 13, 16, 17


def _pad_to(a, rows, cols):
    return jnp.pad(a, ((0, rows - a.shape[0]), (0, cols - a.shape[1])))


def _pack_small(b_ada, g1, gm, g2, gf, conv, sinks, lsum):
    rows = [b_ada.reshape(N_MOD, D), g1.reshape(1, D), gm.reshape(1, D), g2.reshape(1, D), gf.reshape(1, D),
            _pad_to(conv.reshape(3, -1), 3, D), _pad_to(sinks.reshape(1, N_HEADS), 1, D), lsum.reshape(1, D)]
    return _pad_to(jnp.concatenate(rows, axis=0), SMALL_ROWS, D)


def kernel(x, c, w_ada, b_ada, g_ffn1, w1_gu, w1_down, g_mix, w_in, conv_w, w_conv_proj, w_attn_proj, sinks, w_out, g_ffn2, w2_gu, w2_down, g_final, loss_target, m_w_ada, m_b_ada, m_g_ffn1, m_w1_gu, m_w1_down, m_g_mix, m_w_in, m_conv_w, m_w_conv_proj, m_w_attn_proj, m_sinks, m_w_out, m_g_ffn2, m_w2_gu, m_w2_down, m_g_final, v_w_ada, v_b_ada, v_g_ffn1, v_w1_gu, v_w1_down, v_g_mix, v_w_in, v_conv_w, v_w_conv_proj, v_w_attn_proj, v_sinks, v_w_out, v_g_ffn2, v_w2_gu, v_w2_down, v_g_final):
    me = 4 * lax.axis_index("x") + 2 * lax.axis_index("y") + lax.axis_index("c")
    ada_cols = w_ada.shape[2]
    conv_cols = conv_w.shape[2]

    native = dict(gu1=w1_gu[0], d1=w1_down[0], win=w_in[0], cp=w_conv_proj[0], ap=w_attn_proj[0], out=w_out[0],
                  gu2=w2_gu[0], d2=w2_down[0])

    def shard(n, token):
        a = _behind(native[n], token)
        return (a.T if n in TRANSPOSED else a).astype(BF)

    c_all, conv_all, _ = _exchange([c, _pad_to(conv_w[0], 8, conv_cols)], scatter=False, name="gather_cond")
    c_all = c_all.reshape(N_DEV, D)
    conv_full = conv_all[:, 0:3, :].transpose(1, 0, 2).reshape(3, D)

    b_cols = lax.dynamic_slice(b_ada, (0, me * ada_cols), (1, ada_cols))
    mods_cols = _mods_part(c_all, w_ada[0], b_cols, name="ada_mods")
    mods_all, mods_token = _exchange([mods_cols], scatter=False, name="gather_mods")
    mods = lax.dynamic_index_in_dim(mods_all, me, axis=1, keepdims=False).reshape(N_MOD, D)

    groups = dict(gu1=("gu1",), d1=("d1",), mix=("win", "cp", "ap", "out"), ffn2=("gu2", "d2"))
    in_flight = {}
    first = [shard("gu1", mods_token)]
    sems, srcs, lands, token = _tl_start(first, [_own_slot(s, me) for s in first], [[0]],
                                         name="gather_weights_start_gu1")
    in_flight["gu1"] = [sems[0], srcs, lands, None]
    rest = [n for n in BIG if n != "gu1"]
    shards = [shard(n, token) for n in rest]
    rest_groups = [[rest.index(n) for n in names] for g, names in groups.items() if g != "gu1"]
    sems, srcs, lands, rest_token = _tl_start(shards, [_own_slot(s, me) for s in shards], rest_groups,
                                              name="gather_weights_start_rest")
    for (g, names), gsems, idx in zip([kv for kv in groups.items() if kv[0] != "gu1"], sems, rest_groups):
        in_flight[g] = [gsems, [srcs[t] for t in idx], [lands[t] for t in idx], None]

    def forward(group, after):
        sems1, gsrcs, glands, _ = in_flight[group]
        sems2, gsrcs, glands = _tl_forward(gsrcs, glands, sems1, after, name="gather_weights_forward_" + group)
        in_flight[group] = [sems1, gsrcs, glands, sems2]

    forward_early = dict(d1="mix", mix="ffn2")

    tables = _rope_tables(x.shape[1], rest_token)

    def w_get(group, after):
        if group == "gu1":
            after = tables[0]
        if in_flight[group][3] is None:
            forward(group, after)
        sems1, gsrcs, glands, sems2 = in_flight[group]
        landed = _tl_wait(gsrcs, glands, sems1, sems2, after, name="gather_weights_wait_" + group)
        if group in forward_early:
            forward(forward_early[group], landed[0])
        return {n: a.reshape(-1, D) for n, a in zip(groups[group], landed)}

    pending = []

    def g_put(gs):
        names = tuple(gs)
        srcs = [gs[n].reshape(N_DEV, -1, D) for n in names]
        lands = [_own_slot(lax.dynamic_index_in_dim(s, me, axis=0, keepdims=False), me) for s in srcs]
        sems, srcs, lands, tok = _split_start(srcs, lands, [list(range(len(names)))], scatter=True,
                                              name="scatter_grads_start_" + names[0])
        pending.append((names, sems[0], srcs, lands))
        return tok

    lsum, grad_x, small = _local_step(x[0], loss_target[0], mods, g_ffn1, g_mix, g_ffn2, g_final[None],
                                      _pad_to(conv_full, 8, D), sinks[0], w_get, g_put, tables)

    packed = _pack_small(small["mods"], small["g1"], small["gm"], small["g2"], small["gf"], small["convw"],
                         small["sinks"], lsum)
    sm_sems, sm_srcs, sm_lands, sm_token = _split_start([packed], [_own_slot(packed, me)], [[0]], scatter=False,
                                                        name="gather_small_start")

    w_of = dict(ada=w_ada, gu1=w1_gu, d1=w1_down, win=w_in, cp=w_conv_proj, ap=w_attn_proj, out=w_out, gu2=w2_gu,
                d2=w2_down)
    m_of = dict(ada=m_w_ada, gu1=m_w1_gu, d1=m_w1_down, win=m_w_in, cp=m_w_conv_proj, ap=m_w_attn_proj, out=m_w_out,
                gu2=m_w2_gu, d2=m_w2_down)
    v_of = dict(ada=v_w_ada, gu1=v_w1_gu, d1=v_w1_down, win=v_w_in, cp=v_w_conv_proj, ap=v_w_attn_proj, out=v_w_out,
                gu2=v_w2_gu, d2=v_w2_down)
    upd = {}
    after = sm_token
    for k, (names, sems, srcs, lands) in enumerate(pending):
        if k == 2:
            (packed_all,) = _split_wait(sm_srcs, sm_lands, sm_sems[0], after, scatter=False, name="gather_small_wait")
            gsmall = _sum8(packed_all, name="sum_small")
            loss = (0.5 / D) * jnp.sum(gsmall[R_LOSS])
            after = gsmall
        parts = _split_wait(srcs, lands, sems, after, scatter=True, name="scatter_grads_wait_" + names[0])
        for n, p in zip(names, parts):
            if n in TRANSPOSED:
                res = _adam(jnp.swapaxes(w_of[n], 1, 2), p, jnp.swapaxes(m_of[n], 1, 2), jnp.swapaxes(v_of[n], 1, 2),
                            tm=TM_ADAM, name="adam_" + n)
                upd[n] = [jnp.swapaxes(t, 1, 2) for t in res]
            else:
                upd[n] = _adam(w_of[n], p, m_of[n], v_of[n], tm=TM_ADAM, name="adam_" + n)
        after = upd[names[-1]][1]

    gm_cols = lax.dynamic_slice(packed_all[:, R_MODS:R_MODS + N_MOD, :].reshape(N_DEV, N_MOD * D),
                                (0, me * ada_cols), (N_DEV, ada_cols))
    upd["ada"] = _adam(w_ada, _wada_grad(c_all.T, gm_cols, name="ada_dw"), m_w_ada, v_w_ada, tm=256, name="adam_ada")
    conv_g = lax.dynamic_slice(gsmall, (R_CONV, me * conv_cols), (3, conv_cols))

    def natural(b, g1, gm, g2, gf, cw, sk):
        return dict(b_ada=b, g_ffn1=g1, g_mix=gm, g_ffn2=g2, g_final=gf[None], conv_w=cw[0], sinks=sk)

    small_out = _adam_small(gsmall, conv_g, natural(b_ada, g_ffn1, g_mix, g_ffn2, g_final, conv_w, sinks),
                            natural(m_b_ada, m_g_ffn1, m_g_mix, m_g_ffn2, m_g_final, m_conv_w, m_sinks),
                            natural(v_b_ada, v_g_ffn1, v_g_mix, v_g_ffn2, v_g_final, v_conv_w, v_sinks),
                            name="adam_small")
    for res in small_out:
        res["g_final"] = res["g_final"][0]
        res["conv_w"] = res["conv_w"][None]

    big_name = dict(w_ada="ada", w1_gu="gu1", w1_down="d1", w_in="win", w_conv_proj="cp", w_attn_proj="ap",
                    w_out="out", w2_gu="gu2", w2_down="d2")
    order = ("w_ada", "b_ada", "g_ffn1", "w1_gu", "w1_down", "g_mix", "w_in", "conv_w", "w_conv_proj", "w_attn_proj",
             "sinks", "w_out", "g_ffn2", "w2_gu", "w2_down", "g_final")
    outs = [loss, grad_x[None]]
    for kind in range(4):
        for n in order:
            outs.append(upd[big_name[n]][kind] if n in big_name else small_out[kind][n])
    return tuple(outs)
```

```python
import jax
import jax.numpy as jnp
from jax import lax
from jax.experimental import pallas as pl
from jax.experimental.pallas import tpu as pltpu

D = 1024
F = 2816
NIN = 6656
N_HEADS = 16
N_KV = 4
HEAD_DIM = 64
BLK = 128
N_MOD = 9
N_DEV = 8
EPS = 1e-6
NEG_INF = -1e30
ROPE_THETA = 10000.0
O_BG, O_CG, O_U, O_Q, O_K, O_V, O_ZC, O_ZA = 0, 1024, 2048, 3072, 4096, 4352, 4608, 5632

ADAM_LR = 0.001
ADAM_B1 = 0.9
ADAM_B2 = 0.999
ADAM_EPS = 1e-08
ADAM_WD = 0.01
ADAM_STEP = 10

BF = jnp.bfloat16
F32 = jnp.float32
VMEM_LIMIT = 56 * 1024 * 1024
MXU_N = 256
DOWN_BWD_PARTS = 1
MESH = pl.DeviceIdType.MESH

NT = (((1,), (1,)), ((), ()))
TN = (((0,), (0,)), ((), ()))


def _cp(sem=None):
    return pltpu.CompilerParams(dimension_semantics=sem, vmem_limit_bytes=VMEM_LIMIT)


def _tile(n, pref):
    if n <= pref:
        return n
    for t in range(pref - pref % 16, 15, -16):
        if n % t == 0:
            return t
    raise ValueError((n, pref))


def _sigmoid(v):
    return 0.5 * jnp.tanh(0.5 * v) + 0.5


def _row(i):
    return (i, 0)


def _const2(*_):
    return (0, 0)


def _resident(shape):
    return pl.BlockSpec(shape, lambda *_: (0,) * len(shape), pipeline_mode=pl.Buffered(1))


def _norm_proj(x, g, sc, sh, wt, rope=None, *, tm, tn, name):
    T, N = x.shape[0], wt.shape[0]
    tm = _tile(T, tm)

    def body(x_ref, g_ref, sc_ref, sh_ref, w_ref, *rest):
        if rope is None:
            h_ref, o_ref = rest
        else:
            c_ref, s_ref, h_ref, o_ref, qs_ref, kr_ref = rest
        xv = x_ref[...]
        r = lax.rsqrt(jnp.mean(xv * xv, axis=-1, keepdims=True) + EPS)
        hb = ((xv * r) * g_ref[...] * (1.0 + sc_ref[...]) + sh_ref[...]).astype(BF)
        h_ref[...] = hb
        for c0 in range(0, N, tn):
            cols = pl.ds(c0, tn)
            o_ref[:, cols] = lax.dot_general(hb, w_ref[cols, :], NT, preferred_element_type=F32).astype(BF)
            if rope is not None and c0 < O_V <= c0 + tn:
                _attn_prep_tile(o_ref, c_ref, s_ref, qs_ref, kr_ref, tm)

    vec = pl.BlockSpec((1, D), _const2)
    rowspec = pl.BlockSpec((tm, D), _row)
    in_specs = [rowspec, vec, vec, vec, _resident((N, D))]
    out_specs = [rowspec, pl.BlockSpec((tm, N), _row)]
    out_shape = [jax.ShapeDtypeStruct((T, D), BF), jax.ShapeDtypeStruct((T, N), BF)]
    args = [x, g, sc, sh, wt]
    if rope is not None:
        in_specs += [pl.BlockSpec((tm, 128), _row)] * 2
        out_specs += [pl.BlockSpec((N_KV, 4 * tm, 128), lambda i: (0, i, 0)), pl.BlockSpec((tm, 256), _row)]
        out_shape += [jax.ShapeDtypeStruct((N_KV, 4 * T, 128), BF), jax.ShapeDtypeStruct((T, 256), BF)]
        args += list(rope)
    return pl.pallas_call(
        body, name=name, grid=(T // tm,),
        in_specs=in_specs, out_specs=out_specs, out_shape=out_shape,
        compiler_params=_cp(("parallel",)),
    )(*args)


def _ffn_down_fwd(ab, wd, x, gt, *, tm, name):
    T = x.shape[0]
    tm = _tile(T, tm)

    def body(a_ref, b_ref, wd_ref, x_ref, gt_ref, xo_ref):
        y = None
        for c0 in range(0, F, MXU_N):
            cols = pl.ds(c0, MXU_N)
            a = a_ref[:, cols].astype(F32)
            act = (a * _sigmoid(a) * b_ref[:, cols].astype(F32)).astype(BF)
            part = jnp.dot(act, wd_ref[cols, :], preferred_element_type=F32)
            y = part if y is None else y + part
        xo_ref[...] = x_ref[...] + (0.5 * gt_ref[...]) * y

    return pl.pallas_call(
        body, name=name, grid=(T // tm,),
        in_specs=[pl.BlockSpec((tm, F), lambda i: (i, 0)), pl.BlockSpec((tm, F), lambda i: (i, 1)),
                  _resident((F, D)), pl.BlockSpec((tm, D), _row), pl.BlockSpec((1, D), _const2)],
        out_specs=pl.BlockSpec((tm, D), _row),
        out_shape=jax.ShapeDtypeStruct((T, D), F32),
        compiler_params=_cp(("parallel",)),
    )(ab, ab, wd, x, gt)


def _ffn_fwd(x, g, sc, sh, gt, wgu, wd, final, *, tm, name):
    T = x.shape[0]
    tm = _tile(T, tm)
    last = final is not None

    def body(x_ref, g_ref, sc_ref, sh_ref, gt_ref, wgu_ref, wd_ref, *rest):
        if last:
            t_ref, gf_ref, h_ref, ab_ref, dx_ref, ls_ref, dgf_ref = rest
        else:
            h_ref, ab_ref, xo_ref = rest
        xv = x_ref[...]
        r = lax.rsqrt(jnp.mean(xv * xv, axis=-1, keepdims=True) + EPS)
        hb = ((xv * r) * g_ref[...] * (1.0 + sc_ref[...]) + sh_ref[...]).astype(BF)
        h_ref[...] = hb
        y = None
        for c0 in range(0, F, MXU_N):
            a = lax.dot_general(hb, wgu_ref[pl.ds(c0, MXU_N), :], NT, preferred_element_type=F32)
            b = lax.dot_general(hb, wgu_ref[pl.ds(F + c0, MXU_N), :], NT, preferred_element_type=F32)
            ab = a.astype(BF)
            bb = b.astype(BF)
            ab_ref[:, pl.ds(c0, MXU_N)] = ab
            ab_ref[:, pl.ds(F + c0, MXU_N)] = bb
            a = ab.astype(F32)
            act = (a * _sigmoid(a) * bb.astype(F32)).astype(BF)
            part = jnp.dot(act, wd_ref[pl.ds(c0, MXU_N), :], preferred_element_type=F32)
            y = part if y is None else y + part
        xo = xv + (0.5 * gt_ref[...]) * y
        if not last:
            xo_ref[...] = xo
            return

        @pl.when(pl.program_id(0) == 0)
        def _():
            ls_ref[...] = jnp.zeros_like(ls_ref)
            dgf_ref[...] = jnp.zeros_like(dgf_ref)
        gv = gf_ref[...]
        r = lax.rsqrt(jnp.mean(xo * xo, axis=-1, keepdims=True) + EPS)
        xh = xo * r
        e = xh * gv - t_ref[...]
        ls_ref[...] += jnp.sum(e * e, axis=0, keepdims=True)
        dy = e * (1.0 / D)
        dgf_ref[...] += jnp.sum(dy * xh, axis=0, keepdims=True)
        dxh = dy * gv
        dx_ref[...] = r * (dxh - xh * jnp.mean(dxh * xh, axis=-1, keepdims=True))

    vec = pl.BlockSpec((1, D), _const2)
    rowspec = pl.BlockSpec((tm, D), _row)
    in_specs = [rowspec, vec, vec, vec, vec, _resident((2 * F, D)), _resident((F, D))]
    out_specs = [rowspec, pl.BlockSpec((tm, 2 * F), _row), rowspec]
    out_shape = [jax.ShapeDtypeStruct((T, D), BF), jax.ShapeDtypeStruct((T, 2 * F), BF),
                 jax.ShapeDtypeStruct((T, D), F32)]
    args = [x, g, sc, sh, gt, wgu, wd]
    if last:
        in_specs += [rowspec, vec]
        out_specs += [vec, vec]
        out_shape += [jax.ShapeDtypeStruct((1, D), F32)] * 2
        args += list(final)
    return pl.pallas_call(
        body, name=name, grid=(T // tm,),
        in_specs=in_specs, out_specs=out_specs, out_shape=out_shape,
        compiler_params=_cp(("arbitrary",) if last else ("parallel",)),
    )(*args)


def _ffn_down_bwd_dw(dxo, gt, ab, wd, *, tm, name):
    T = dxo.shape[0]
    tm = _tile(T, tm)
    nt = T // tm
    nh = DOWN_BWD_PARTS
    hw = F // nh
    chunks = [(c0, min(MXU_N, hw - c0)) for c0 in range(0, hw, MXU_N)]

    def body(dxo_ref, gt_ref, a_ref, b_ref, wd_ref, dab_ref, dgt_ref, dwd_ref, dys, dyt, acc, stage, sem):
        i, j = pl.program_id(0), pl.program_id(1)

        @pl.when(jnp.logical_and(i == 0, j == 0))
        def _():
            dgt_ref[...] = jnp.zeros_like(dgt_ref)

        @pl.when(j == 0)
        def _():
            dxv = dxo_ref[...]
            dys[...] = ((0.5 * gt_ref[...]) * dxv).astype(BF)
            dyt[...] = dxv.T.astype(BF)

        def half(jj):
            @pl.when(i == 0)
            def _():
                acc[jj] = jnp.zeros((D, hw), F32)

            dy = dys[...]
            dy_t = dyt[...]
            for c0, cw in chunks:
                cols = pl.ds(c0, cw)
                dact = lax.dot_general(dy, wd_ref[pl.ds(jj * hw + c0, cw), :], NT, preferred_element_type=F32)
                a = a_ref[:, cols].astype(F32)
                b = b_ref[:, cols].astype(F32)
                s = _sigmoid(a)
                silu = a * s
                dab_ref[0, :, cols] = (dact * b * (s * (1.0 + a * (1.0 - s)))).astype(BF)
                dab_ref[1, :, cols] = (dact * silu).astype(BF)
                acc[jj, :, cols] += jnp.dot(dy_t, (silu * b).astype(BF), preferred_element_type=F32)

            @pl.when(i == nt - 1)
            def _():
                half_gt = 0.5 * gt_ref[...]
                for c0, cw in chunks:
                    g_rows = acc[jj, :, pl.ds(c0, cw)].T
                    w_rows = wd_ref[pl.ds(jj * hw + c0, cw), :].astype(F32)
                    dgt_ref[...] += 0.5 * jnp.sum(g_rows * w_rows, axis=0, keepdims=True)
                    stage[0:cw, :] = (g_rows * half_gt).astype(BF)
                    out = pltpu.make_async_copy(stage.at[pl.ds(0, cw)], dwd_ref.at[pl.ds(jj * hw + c0, cw)], sem)
                    out.start()
                    out.wait()

        for jj in range(nh):
            pl.when(j == jj)(lambda jj=jj: half(jj))

    vec = pl.BlockSpec((1, D), _const2)
    rowspec = pl.BlockSpec((tm, D), lambda i, j: (i, 0))
    return pl.pallas_call(
        body, name=name, grid=(nt, nh),
        in_specs=[rowspec, vec, pl.BlockSpec((tm, hw), lambda i, j: (i, j)),
                  pl.BlockSpec((tm, hw), lambda i, j: (i, j + nh)), _resident((F, D))],
        out_specs=[pl.BlockSpec((2, tm, hw), lambda i, j: (0, i, j)), vec, pl.BlockSpec(memory_space=pl.ANY)],
        out_shape=[jax.ShapeDtypeStruct((2, T, F), BF), jax.ShapeDtypeStruct((1, D), F32),
                   jax.ShapeDtypeStruct((F, D), BF)],
        scratch_shapes=[pltpu.VMEM((tm, D), BF), pltpu.VMEM((D, tm), BF), pltpu.VMEM((nh, D, hw), F32),
                        pltpu.VMEM((MXU_N, D), BF), pltpu.SemaphoreType.DMA(())],
        compiler_params=_cp(("arbitrary", "arbitrary")),
    )(dxo, gt, ab, ab, wd)


def _tn_matmul(a, b, token=None, *, tn, tk, name):
    S, T, Ns = a.shape
    tn, tk = _tile(Ns, tn), _tile(T, tk)
    nk, njs = T // tk, Ns // tn
    deps = [] if token is None else [token]

    def body(a_ref, b_ref, *rest):
        o_ref, acc = rest[len(deps):]
        k = pl.program_id(1)

        @pl.when(k == 0)
        def _():
            acc[...] = jnp.zeros_like(acc)
        acc[...] += lax.dot_general(a_ref[0], b_ref[...], TN, preferred_element_type=F32)

        @pl.when(k == nk - 1)
        def _():
            o_ref[...] = acc[...].astype(BF)

    return pl.pallas_call(
        body, name=name, grid=(S * njs, nk),
        in_specs=[pl.BlockSpec((1, tk, tn), lambda j, k: (j // njs, k, j % njs)),
                  pl.BlockSpec((tk, D), lambda j, k: (k, 0))] + [pl.BlockSpec(memory_space=pl.ANY)] * len(deps),
        out_specs=pl.BlockSpec((tn, D), lambda j, k: (j, 0)),
        out_shape=jax.ShapeDtypeStruct((S * Ns, D), BF),
        scratch_shapes=[pltpu.VMEM((tn, D), F32)],
        compiler_params=_cp(("parallel", "arbitrary")),
    )(a, b, *deps)


def _nn_bwd_norm(da, w, x, g, sc, dxo, *, tm, name):
    S, T, Ks = da.shape
    tm = _tile(T, tm)
    rc = _tile(tm, 256)

    def body(da_ref, w_ref, x_ref, g_ref, sc_ref, dxo_ref, dx_ref, dsh_ref, dsc_ref, dg_ref, acc):
        @pl.when(pl.program_id(0) == 0)
        def _():
            dsh_ref[...] = jnp.zeros_like(dsh_ref)
            dsc_ref[...] = jnp.zeros_like(dsc_ref)
            dg_ref[...] = jnp.zeros_like(dg_ref)

        d = jnp.dot(da_ref[0], w_ref[0:Ks, :], preferred_element_type=F32)
        for s in range(1, S):
            d = d + jnp.dot(da_ref[s], w_ref[s * Ks:(s + 1) * Ks, :], preferred_element_type=F32)
        acc[...] = d
        gv = g_ref[...]
        sc1 = 1.0 + sc_ref[...]
        dsh = jnp.zeros((1, D), F32)
        dsc = jnp.zeros((1, D), F32)
        dg = jnp.zeros((1, D), F32)
        for r0 in range(0, tm, rc):
            rows = pl.ds(r0, rc)
            u = acc[rows, :]
            xv = x_ref[rows, :]
            r = lax.rsqrt(jnp.mean(xv * xv, axis=-1, keepdims=True) + EPS)
            xh = xv * r
            dsh = dsh + jnp.sum(u, axis=0, keepdims=True)
            dsc = dsc + jnp.sum(u * (xh * gv), axis=0, keepdims=True)
            us = u * sc1
            dg = dg + jnp.sum(us * xh, axis=0, keepdims=True)
            dxh = us * gv
            dx_ref[rows, :] = dxo_ref[rows, :] + r * (dxh - xh * jnp.mean(dxh * xh, axis=-1, keepdims=True))
        dsh_ref[...] += dsh
        dsc_ref[...] += dsc
        dg_ref[...] += dg

    vec = pl.BlockSpec((1, D), _const2)
    rowspec = pl.BlockSpec((tm, D), _row)
    return pl.pallas_call(
        body, name=name, grid=(T // tm,),
        in_specs=[pl.BlockSpec((S, tm, Ks), lambda i: (0, i, 0)), _resident((S * Ks, D)), rowspec, vec, vec, rowspec],
        out_specs=[rowspec, vec, vec, vec],
        out_shape=[jax.ShapeDtypeStruct((T, D), F32)] + [jax.ShapeDtypeStruct((1, D), F32)] * 3,
        scratch_shapes=[pltpu.VMEM((tm, D), F32)],
        compiler_params=_cp(("arbitrary",)),
    )(da, w, x, g, sc, dxo)


def _rope(t, cos, sin_signed, lt32, inverse=False):
    sel = jnp.where(lt32, pltpu.roll(t, 96, 1), pltpu.roll(t, 32, 1))
    return t * cos - sel * sin_signed if inverse else t * cos + sel * sin_signed


def _rope_tables(T, token=None):
    inv = 1.0 / (ROPE_THETA ** (jnp.arange(0, HEAD_DIM, 2, dtype=F32) / HEAD_DIM))
    ang = _behind(jnp.arange(T, dtype=F32)[:, None] * inv[None, :], token)
    cos, sin = jnp.cos(ang), jnp.sin(ang)
    cos128 = jnp.tile(cos, (1, 4))
    sin128 = jnp.tile(jnp.concatenate([-sin, sin], axis=1), (1, 2))
    return cos128, sin128


QSCALE = HEAD_DIM ** -0.5


def _lane_masks(rows):
    lane = lax.broadcasted_iota(jnp.int32, (rows, 128), 1)
    return (lane % HEAD_DIM) < (HEAD_DIM // 2), [lane < HEAD_DIM, lane >= HEAD_DIM]


def _attn_bias():
    qi = lax.broadcasted_iota(jnp.int32, (4 * BLK, 2 * BLK), 0) % BLK
    kj = lax.broadcasted_iota(jnp.int32, (4 * BLK, 2 * BLK), 1)
    band = (kj > qi) & (kj <= qi + BLK)
    return jnp.stack([jnp.where(band & (kj >= BLK), 0.0, NEG_INF), jnp.where(band, 0.0, NEG_INF)]).astype(F32)


def _attn_prep_tile(proj_ref, c_ref, s_ref, qs_ref, kr_ref, tm):
    lt32, halves = _lane_masks(BLK)
    for b in range(tm // BLK):
        rows = pl.ds(b * BLK, BLK)
        cc, sc = c_ref[rows, :], s_ref[rows, :]
        qr = [_rope(proj_ref[rows, pl.ds(O_Q + p * 128, 128)].astype(F32), cc, sc, lt32) * QSCALE for p in range(8)]
        for g in range(N_KV):
            qs_ref[g, pl.ds(4 * b * BLK, 4 * BLK), :] = _stack_heads(qr, g, halves).astype(BF)
        kr_ref[rows, :] = jnp.concatenate([_rope(proj_ref[rows, pl.ds(O_K + r * 128, 128)].astype(F32), cc, sc, lt32)
                                           for r in range(2)], axis=1).astype(BF)


ATT_BPS = 4
ATT_ROWS = ATT_BPS * BLK


def _before(n):
    return jnp.maximum(ATT_BPS * n - 1, 0)


def _attn_specs():
    return [pl.BlockSpec((N_KV, 4 * ATT_ROWS, 128), lambda n: (0, n, 0)),
            pl.BlockSpec((ATT_ROWS, 256), _row), pl.BlockSpec((BLK, 256), lambda n: (_before(n), 0)),
            pl.BlockSpec((ATT_ROWS, 256), lambda n: (n, O_V // 256)),
            pl.BlockSpec((BLK, 256), lambda n: (_before(n), O_V // 256)),
            pl.BlockSpec((2, 4 * BLK, 2 * BLK), lambda n: (0, 0, 0)),
            pl.BlockSpec(memory_space=pltpu.SMEM)]


def _bands(sb, kc_ref, kp_ref, vc_ref, vp_ref):
    own = pl.ds(sb * BLK, BLK)
    above = pl.ds((sb - 1) * BLK, BLK)
    kb, vb = [], []
    for r in range(2):
        cols = pl.ds(r * 128, 128)
        kprev = kp_ref[:, cols] if sb == 0 else kc_ref[above, cols]
        vprev = vp_ref[:, cols] if sb == 0 else vc_ref[above, cols]
        kb.append(jnp.concatenate([kprev, kc_ref[own, cols]], axis=0))
        vb.append(jnp.concatenate([vprev, vc_ref[own, cols]], axis=0))
    return kb, vb


def _block_bias(sb, bias_ref):
    return bias_ref[jnp.minimum(pl.program_id(0), 1)] if sb == 0 else bias_ref[1]


def _sink_rows(sink_ref, g):
    return jnp.concatenate([jnp.full((BLK, 128), sink_ref[4 * g + hh], F32) for hh in range(4)], axis=0)


def _both(t):
    return jnp.concatenate([t, t], axis=1)


def _unstack_heads(t, g, halves, acc):
    half = g % 2
    for hh in range(4):
        h = 4 * g + hh
        th = jnp.where(halves[half], t[hh * BLK:(hh + 1) * BLK], 0.0)
        if h % 2 != half:
            th = pltpu.roll(th, HEAD_DIM, 1)
        acc[h // 2] = acc[h // 2] + th


def _stack_heads(chunks, g, halves):
    half = g % 2
    parts = []
    for hh in range(4):
        h = 4 * g + hh
        t = chunks[h // 2]
        if h % 2 != half:
            t = pltpu.roll(t, HEAD_DIM, 1)
        parts.append(jnp.where(halves[half], t, 0.0))
    return jnp.concatenate(parts, axis=0)


def _attn_fwd(qs, kr, proj, bias, sinks, *, name):
    T = proj.shape[0]
    assert T % ATT_ROWS == 0

    def body(qs_ref, kc_ref, kp_ref, vc_ref, vp_ref, bias_ref, sink_ref, o_ref, lse_ref):
        _, h128 = _lane_masks(BLK)
        _, h256 = _lane_masks(2 * BLK)
        _, h512 = _lane_masks(4 * BLK)
        groups = range(N_KV)
        sink = [_sink_rows(sink_ref, g) for g in groups]
        for sb in range(ATT_BPS):
            rows = pl.ds(4 * sb * BLK, 4 * BLK)
            kb, vb = _bands(sb, kc_ref, kp_ref, vc_ref, vp_ref)
            outs = [jnp.zeros((BLK, 128), F32) for _ in range(8)]
            bias = _block_bias(sb, bias_ref)
            s = [lax.dot_general(qs_ref[g, rows, :], kb[g // 2], NT, preferred_element_type=F32) + bias for g in groups]
            m = [jnp.maximum(jnp.broadcast_to(jnp.max(s[g], axis=-1, keepdims=True), (4 * BLK, 128)), sink[g])
                 for g in groups]
            p = [jnp.exp(s[g] - _both(m[g])).astype(BF) for g in groups]
            vg = [jnp.where(h256[g % 2], vb[g // 2].astype(F32), 1.0).astype(BF) for g in groups]
            o = [jnp.dot(p[g], vg[g], preferred_element_type=F32) for g in groups]
            denom = [jnp.where(h512[g % 2], pltpu.roll(o[g], HEAD_DIM, 1), o[g]) + jnp.exp(sink[g] - m[g])
                     for g in groups]
            for g in groups:
                lse_ref[g, rows, :] = m[g] + jnp.log(denom[g])
                _unstack_heads(o[g] * (1.0 / denom[g]), g, h128, outs)
            o_ref[pl.ds(sb * BLK, BLK), :] = jnp.concatenate(outs, axis=1).astype(BF)

    return pl.pallas_call(
        body, name=name, grid=(T // ATT_ROWS,),
        in_specs=_attn_specs(),
        out_specs=[pl.BlockSpec((ATT_ROWS, D), _row), pl.BlockSpec((N_KV, 4 * ATT_ROWS, 128), lambda n: (0, n, 0))],
        out_shape=[jax.ShapeDtypeStruct((T, D), BF), jax.ShapeDtypeStruct((N_KV, 4 * T, 128), F32)],
        compiler_params=_cp(("parallel",)),
    )(qs, kr, kr, proj, proj, bias, sinks)


def _attn_bwd(qs, kr, proj, bias, sinks, lse, o, do, cos, sin, dproj, *, name):
    T = proj.shape[0]
    assert T % ATT_ROWS == 0

    def body(qs_ref, kc_ref, kp_ref, vc_ref, vp_ref, bias_ref, sink_ref, lse_ref, o_ref, do_ref,
             cc_ref, sc_ref, cp_ref, sp_ref, dproj_ref, dq_ref, dkc_ref, dkp_ref, dvc_ref, dvp_ref, dsink_ref):
        @pl.when(pl.program_id(0) == 0)
        def _():
            dsink_ref[...] = jnp.zeros_like(dsink_ref)
        lt32, h128 = _lane_masks(BLK)
        lane1 = lax.broadcasted_iota(jnp.int32, (1, 128), 1)
        dsink = jnp.zeros((1, 128), F32)
        groups = range(N_KV)
        for sb in range(ATT_BPS):
            own = pl.ds(sb * BLK, BLK)
            rows = pl.ds(4 * sb * BLK, 4 * BLK)
            kb, vb = _bands(sb, kc_ref, kp_ref, vc_ref, vp_ref)
            oc = [o_ref[own, pl.ds(p * 128, 128)].astype(F32) for p in range(8)]
            doc = [do_ref[own, pl.ds(p * 128, 128)].astype(F32) for p in range(8)]
            dqs = [jnp.zeros((BLK, 128), F32) for _ in range(8)]
            bias = _block_bias(sb, bias_ref)
            q = [qs_ref[g, rows, :] for g in groups]
            lse_g = [lse_ref[g, rows, :] for g in groups]
            s = [lax.dot_general(q[g], kb[g // 2], NT, preferred_element_type=F32) + bias for g in groups]
            dos = [_stack_heads(doc, g, h128) for g in groups]
            dosb = [t.astype(BF) for t in dos]
            dp = [lax.dot_general(dosb[g], vb[g // 2], NT, preferred_element_type=F32) for g in groups]
            delta = [jnp.broadcast_to(jnp.sum(dos[g] * _stack_heads(oc, g, h128), axis=-1, keepdims=True),
                                      (4 * BLK, 128)) for g in groups]
            p = [jnp.exp(s[g] - _both(lse_g[g])) for g in groups]
            ds = [(p[g] * (dp[g] - _both(delta[g]))).astype(BF) for g in groups]
            pb = [t.astype(BF) for t in p]
            dvg = [lax.dot_general(pb[g], dosb[g], TN, preferred_element_type=F32) for g in groups]
            dkg = [lax.dot_general(ds[g], q[g], TN, preferred_element_type=F32) for g in groups]
            dqg = [jnp.dot(ds[g], kb[g // 2], preferred_element_type=F32) * QSCALE for g in groups]
            dvr = [dvg[0] + dvg[1], dvg[2] + dvg[3]]
            dkr = [dkg[0] + dkg[1], dkg[2] + dkg[3]]
            for g in groups:
                _unstack_heads(dqg[g], g, h128, dqs)
                dsk = -jnp.exp(_sink_rows(sink_ref, g) - lse_g[g]) * delta[g]
                for hh in range(4):
                    val = jnp.sum(dsk[hh * BLK:(hh + 1) * BLK], axis=0, keepdims=True)
                    dsink = dsink + jnp.where(lane1 == 4 * g + hh, val, 0.0)
            cc, sc = cc_ref[own, :], sc_ref[own, :]
            cp, sp = (cp_ref[...], sp_ref[...]) if sb == 0 else (cc_ref[pl.ds((sb - 1) * BLK, BLK), :],
                                                                  sc_ref[pl.ds((sb - 1) * BLK, BLK), :])
            dq_ref[own, :] = jnp.concatenate([_rope(t, cc, sc, lt32, inverse=True) for t in dqs], axis=1).astype(BF)
            dkp_ref[own, :] = jnp.concatenate([_rope(t[:BLK], cp, sp, lt32, inverse=True) for t in dkr],
                                              axis=1).astype(BF)
            dkc_ref[own, :] = jnp.concatenate([_rope(t[BLK:], cc, sc, lt32, inverse=True) for t in dkr],
                                              axis=1).astype(BF)
            dvp_ref[own, :] = jnp.concatenate([t[:BLK] for t in dvr], axis=1).astype(BF)
            dvc_ref[own, :] = jnp.concatenate([t[BLK:] for t in dvr], axis=1).astype(BF)
        dsink_ref[...] += dsink

    kv = pl.BlockSpec((ATT_ROWS, 256), _row)
    tc = pl.BlockSpec((ATT_ROWS, 128), _row)
    tp = pl.BlockSpec((BLK, 128), lambda n: (_before(n), 0))
    return pl.pallas_call(
        body, name=name, grid=(T // ATT_ROWS,),
        in_specs=_attn_specs() + [pl.BlockSpec((N_KV, 4 * ATT_ROWS, 128), lambda n: (0, n, 0)),
                                  pl.BlockSpec((ATT_ROWS, D), _row), pl.BlockSpec((ATT_ROWS, D), _row), tc, tc, tp, tp,
                                  pl.BlockSpec(memory_space=pl.ANY)],
        out_specs=[pl.BlockSpec((ATT_ROWS, D), lambda n: (n, O_Q // D)), kv, kv, kv, kv,
                   pl.BlockSpec((1, 128), _const2)],
        out_shape=[jax.ShapeDtypeStruct(dproj.shape, BF)] + [jax.ShapeDtypeStruct((T, 256), BF)] * 4
        + [jax.ShapeDtypeStruct((1, 128), F32)],
        input_output_aliases={14: 0},
        compiler_params=_cp(("arbitrary",)),
    )(qs, kr, kr, proj, proj, bias, sinks, lse, o, do, cos, sin, cos, sin, dproj)


def _dkv_combine(dkc, dkp, dvc, dvp, dproj, *, name):
    T = dkc.shape[0]
    nb = T // BLK
    tm = _tile(T, 4 * BLK)
    bpt = tm // BLK
    nt = T // tm

    def body(dkc_ref, dkp_ref, dkn_ref, dvc_ref, dvp_ref, dvn_ref, dproj_ref, o_ref):
        keep = jnp.where(pl.program_id(0) == nt - 1, 0.0, 1.0)

        def shifted(prev_ref, next_ref):
            nxt = keep * next_ref[...].astype(F32)
            return nxt if bpt == 1 else jnp.concatenate([prev_ref[BLK:, :].astype(F32), nxt], axis=0)

        o_ref[:, 0:256] = (dkc_ref[...].astype(F32) + shifted(dkp_ref, dkn_ref)).astype(BF)
        o_ref[:, 256:512] = (dvc_ref[...].astype(F32) + shifted(dvp_ref, dvn_ref)).astype(BF)

    cur = pl.BlockSpec((tm, 256), _row)
    nxt = pl.BlockSpec((BLK, 256), lambda i: (jnp.minimum((i + 1) * bpt, nb - 1), 0))
    return pl.pallas_call(
        body, name=name, grid=(nt,),
        in_specs=[cur, cur, nxt, cur, cur, nxt, pl.BlockSpec(memory_space=pl.ANY)],
        out_specs=pl.BlockSpec((tm, 512), lambda i: (i, O_K // 512)),
        out_shape=jax.ShapeDtypeStruct(dproj.shape, BF),
        input_output_aliases={6: 0},
        compiler_params=_cp(("parallel",)),
    )(dkc, dkp, dkp, dvc, dvp, dvp, dproj)


HALO = 16


def _conv_shifts(cu, hprev, tm):
    row = lax.broadcasted_iota(jnp.int32, (8, cu.shape[1]), 0)
    h1 = hprev[HALO - 1:HALO, :]
    h2 = hprev[HALO - 2:HALO - 1, :]
    m1 = pltpu.roll(cu, 1, 0)
    m2 = pltpu.roll(cu, 2, 0)
    m1 = jnp.concatenate([jnp.where(row == 0, h1, m1[0:8]), m1[8:]], axis=0)
    m2 = jnp.concatenate([jnp.where(row == 0, h2, jnp.where(row == 1, h1, m2[0:8])), m2[8:]], axis=0)
    return m1, m2


def _mixer_mid_fwd(proj, attn, wcp, wap, wout, convw, x, gt, *, tm, name):
    T = x.shape[0]
    tm = _tile(T, tm)
    hb = tm // HALO

    def body(bcu_ref, hcg_ref, hu_ref, zc0_ref, zc1_ref, za0_ref, za1_ref, at_ref,
             wcp_ref, wap_ref, wout_ref, cw_ref, x_ref, gt_ref,
             x2_ref, gc_ref, yc_ref, ya_ref, mg_ref, o_ref):
        first = jnp.where(pl.program_id(0) == 0, 0.0, 1.0)
        cu = bcu_ref[:, D:2 * D].astype(F32) * bcu_ref[:, 2 * D:3 * D].astype(F32)
        hprev = first * (hcg_ref[...].astype(F32) * hu_ref[...].astype(F32))
        m1, m2 = _conv_shifts(cu, hprev, tm)
        cv = cw_ref[0:1, :] * m2 + cw_ref[1:2, :] * m1 + cw_ref[2:3, :] * cu
        gc = (bcu_ref[:, 0:D].astype(F32) * cv).astype(BF)
        gc_ref[...] = gc
        yc = jnp.dot(gc, wcp_ref[...], preferred_element_type=F32)
        ya = jnp.dot(at_ref[...], wap_ref[...], preferred_element_type=F32)
        yc_ref[...] = yc.astype(BF)
        ya_ref[...] = ya.astype(BF)
        zc = jnp.concatenate([zc0_ref[...], zc1_ref[...]], axis=1).astype(F32)
        za = jnp.concatenate([za0_ref[...], za1_ref[...]], axis=1).astype(F32)
        mg = (_sigmoid(zc) * yc + _sigmoid(za) * ya).astype(BF)
        mg_ref[...] = mg
        o = jnp.dot(mg, wout_ref[...], preferred_element_type=F32)
        o_ref[...] = o.astype(BF)
        x2_ref[...] = x_ref[...] + gt_ref[...] * o

    wspec = pl.BlockSpec((D, D), _const2)
    rowspec = pl.BlockSpec((tm, D), _row)
    return pl.pallas_call(
        body, name=name, grid=(T // tm,),
        in_specs=[_col(tm, O_BG, 3 * D), _halo_prev(hb, O_CG), _halo_prev(hb, O_U),
                  _col(tm, O_ZC, 512), _col(tm, O_ZC + 512, 512), _col(tm, O_ZA, 512), _col(tm, O_ZA + 512, 512),
                  rowspec, wspec, wspec, wspec, pl.BlockSpec((8, D), _const2), rowspec, pl.BlockSpec((1, D), _const2)],
        out_specs=[rowspec] * 6,
        out_shape=[jax.ShapeDtypeStruct((T, D), F32)] + [jax.ShapeDtypeStruct((T, D), BF)] * 5,
        compiler_params=_cp(("parallel",)),
    )(proj, proj, proj, proj, proj, proj, proj, attn, wcp, wap, wout, convw, x, gt)


def _col(tm, c, w=D):
    assert c % w == 0
    return pl.BlockSpec((tm, w), lambda i: (i, c // w))


def _halo_prev(hb, c):
    return pl.BlockSpec((HALO, D), lambda i: (jnp.maximum(i * hb - 1, 0), c // D))


def _halo_next(hb, nblk, c=0):
    return pl.BlockSpec((HALO, D), lambda i: (jnp.minimum((i + 1) * hb, nblk - 1), c // D))


def _mixer_mid_bwd(dx2, gt, o, proj, yc, ya, wout, wcp, wap, *, tm, name):
    T = dx2.shape[0]
    tm = _tile(T, tm)
    nt = T // tm

    def body(dx_ref, gt_ref, o_ref, zc0_ref, zc1_ref, za0_ref, za1_ref, yc_ref, ya_ref, wout_ref, wcp_ref, wap_ref,
             dout_ref, dyc_ref, dya_ref, dgc_ref, dat_ref, dproj_ref, dgt_ref, dzs, sems):
        i = pl.program_id(0)
        slot = lax.rem(i, 2)

        def slab_copy(step, s):
            return pltpu.make_async_copy(
                dzs.at[s], dproj_ref.at[pl.ds(pl.multiple_of(step * tm, tm), tm), pl.ds(O_ZC, 2 * D)], sems.at[s])

        @pl.when(i == 0)
        def _():
            dgt_ref[...] = jnp.zeros_like(dgt_ref)

        dxv = dx_ref[...]
        dgt_ref[...] += jnp.sum(dxv * o_ref[...].astype(F32), axis=0, keepdims=True)
        dout = (gt_ref[...] * dxv).astype(BF)
        dout_ref[...] = dout
        dmg = lax.dot_general(dout, wout_ref[...], NT, preferred_element_type=F32)
        sc = _sigmoid(jnp.concatenate([zc0_ref[...], zc1_ref[...]], axis=1).astype(F32))
        sa = _sigmoid(jnp.concatenate([za0_ref[...], za1_ref[...]], axis=1).astype(F32))
        dyc = (dmg * sc).astype(BF)
        dya = (dmg * sa).astype(BF)
        dyc_ref[...] = dyc
        dya_ref[...] = dya
        dzs[slot, :, 0:D] = (dmg * yc_ref[...].astype(F32) * (sc * (1.0 - sc))).astype(BF)
        dzs[slot, :, D:2 * D] = (dmg * ya_ref[...].astype(F32) * (sa * (1.0 - sa))).astype(BF)
        slab_copy(i, slot).start()
        dgc_ref[...] = lax.dot_general(dyc, wcp_ref[...], NT, preferred_element_type=F32).astype(BF)
        dat_ref[...] = lax.dot_general(dya, wap_ref[...], NT, preferred_element_type=F32).astype(BF)

        @pl.when(i > 0)
        def _():
            slab_copy(i - 1, 1 - slot).wait()

        @pl.when(i == nt - 1)
        def _():
            slab_copy(i, slot).wait()

    def zcol(c):
        return pl.BlockSpec((tm, 512), lambda i: (i, c // 512))

    wspec = pl.BlockSpec((D, D), _const2)
    rowspec = pl.BlockSpec((tm, D), _row)
    vec = pl.BlockSpec((1, D), _const2)
    return pl.pallas_call(
        body, name=name, grid=(nt,),
        in_specs=[rowspec, vec, rowspec, zcol(O_ZC), zcol(O_ZC + 512), zcol(O_ZA), zcol(O_ZA + 512),
                  rowspec, rowspec, wspec, wspec, wspec],
        out_specs=[rowspec] * 5 + [pl.BlockSpec(memory_space=pl.ANY), vec],
        out_shape=[jax.ShapeDtypeStruct((T, D), BF)] * 5 + [jax.ShapeDtypeStruct((T, NIN), BF),
                                                            jax.ShapeDtypeStruct((1, D), F32)],
        scratch_shapes=[pltpu.VMEM((2, tm, 2 * D), BF), pltpu.SemaphoreType.DMA((2,))],
        compiler_params=_cp(("arbitrary",)),
    )(dx2, gt, o, proj, proj, proj, proj, yc, ya, wout, wcp, wap)


def _conv_bwd(dgc, proj, convw, dproj, *, tm, name):
    T = dgc.shape[0]
    tm = _tile(T, tm)
    hb = tm // HALO
    nblk = T // HALO
    nt = T // tm

    def body(dgc_ref, ndgc_ref, bg_ref, nbg_ref, cg_ref, u_ref, hcg_ref, hu_ref, cw_ref, dproj_ref, dp_ref, dcw_ref):
        i = pl.program_id(0)

        @pl.when(i == 0)
        def _():
            dcw_ref[...] = jnp.zeros_like(dcw_ref)
        first = jnp.where(i == 0, 0.0, 1.0)
        last = jnp.where(i == nt - 1, 0.0, 1.0)
        cg = cg_ref[...].astype(F32)
        u = u_ref[...].astype(F32)
        bg = bg_ref[...].astype(F32)
        dg = dgc_ref[...].astype(F32)
        cu = cg * u
        hprev = first * (hcg_ref[...].astype(F32) * hu_ref[...].astype(F32))
        m1, m2 = _conv_shifts(cu, hprev, tm)
        w0, w1, w2 = cw_ref[0:1, :], cw_ref[1:2, :], cw_ref[2:3, :]
        cv = w0 * m2 + w1 * m1 + w2 * cu
        dcv = dg * bg
        nxt = last * (ndgc_ref[...].astype(F32) * nbg_ref[...].astype(F32))
        n0, n1 = nxt[0:1, :], nxt[1:2, :]
        row = lax.broadcasted_iota(jnp.int32, (8, D), 0)
        p1 = pltpu.roll(dcv, tm - 1, 0)
        p2 = pltpu.roll(dcv, tm - 2, 0)
        p1 = jnp.concatenate([p1[:tm - 8], jnp.where(row == 7, n0, p1[tm - 8:])], axis=0)
        p2 = jnp.concatenate([p2[:tm - 8], jnp.where(row == 7, n1, jnp.where(row == 6, n0, p2[tm - 8:]))], axis=0)
        dcu = w2 * dcv + w1 * p1 + w0 * p2
        dp_ref[:, 0:D] = (dg * cv).astype(BF)
        dp_ref[:, D:2 * D] = (dcu * u).astype(BF)
        dp_ref[:, 2 * D:3 * D] = (dcu * cg).astype(BF)
        dcw_ref[0:1, :] += jnp.sum(dcv * m2, axis=0, keepdims=True)
        dcw_ref[1:2, :] += jnp.sum(dcv * m1, axis=0, keepdims=True)
        dcw_ref[2:3, :] += jnp.sum(dcv * cu, axis=0, keepdims=True)

    rowspec = pl.BlockSpec((tm, D), _row)
    cw = pl.BlockSpec((8, D), _const2)
    return pl.pallas_call(
        body, name=name, grid=(nt,),
        in_specs=[rowspec, _halo_next(hb, nblk), _col(tm, O_BG), _halo_next(hb, nblk, O_BG),
                  _col(tm, O_CG), _col(tm, O_U), _halo_prev(hb, O_CG), _halo_prev(hb, O_U), cw,
                  pl.BlockSpec(memory_space=pl.ANY)],
        out_specs=[pl.BlockSpec((tm, 3 * D), _row), cw],
        out_shape=[jax.ShapeDtypeStruct(dproj.shape, BF), jax.ShapeDtypeStruct((8, D), F32)],
        input_output_aliases={9: 0},
        compiler_params=_cp(("arbitrary",)),
    )(dgc, dgc, proj, proj, proj, proj, proj, proj, convw, dproj)


def _adam_math(w, g, m, v):
    nm = ADAM_B1 * m + (1.0 - ADAM_B1) * g
    nv = ADAM_B2 * v + (1.0 - ADAM_B2) * (g * g)
    m_hat = nm / (1.0 - ADAM_B1 ** ADAM_STEP)
    v_hat = nv / (1.0 - ADAM_B2 ** ADAM_STEP)
    return -ADAM_LR * (m_hat / (jnp.sqrt(v_hat) + ADAM_EPS) + ADAM_WD * w), nm, nv


SMALL = ("b_ada", "g_ffn1", "g_mix", "g_ffn2", "g_final", "conv_w", "sinks")


def _adam_small(gsum, conv_g, w, m, v, *, name):
    nsm = len(SMALL)

    def body(*refs):
        gs_ref, cg_ref = refs[0], refs[1]
        w_refs, m_refs, v_refs = (refs[2 + k * nsm:2 + (k + 1) * nsm] for k in range(3))
        outs = refs[2 + 3 * nsm:]
        for p, n in enumerate(SMALL):
            if n == "b_ada":
                pieces = [(slice(None), slice(r * D, (r + 1) * D), gs_ref[R_MODS + r:R_MODS + r + 1, :])
                          for r in range(N_MOD)]
            elif n == "conv_w":
                pieces = [(slice(None), slice(None), cg_ref[...])]
            elif n == "sinks":
                pieces = [(slice(None), slice(None), gs_ref[R_SINK:R_SINK + 1, 0:N_HEADS])]
            else:
                row = dict(g_ffn1=R_G1, g_mix=R_GM, g_ffn2=R_G2, g_final=R_GF)[n]
                pieces = [(slice(None), slice(None), gs_ref[row:row + 1, :])]
            for rs, cs, g in pieces:
                d, nm, nv = _adam_math(w_refs[p][rs, cs], g, m_refs[p][rs, cs], v_refs[p][rs, cs])
                for k, val in enumerate((g, d, nm, nv)):
                    outs[k * nsm + p][rs, cs] = val

    args = [gsum, conv_g] + [d[n] for d in (w, m, v) for n in SMALL]
    shapes = [jax.ShapeDtypeStruct(w[n].shape, F32) for _ in range(4) for n in SMALL]
    res = pl.pallas_call(body, name=name, out_shape=shapes, compiler_params=_cp())(*args)
    return [dict(zip(SMALL, res[k * nsm:(k + 1) * nsm])) for k in range(4)]


def _adam(w, g, m, v, *, tm, name):
    _, R, C = w.shape
    tm = _tile(R, tm)
    parts = g.ndim == 3

    def body(w_ref, g_ref, m_ref, v_ref, go_ref, d_ref, nm_ref, nv_ref):
        if parts:
            gv = g_ref[0].astype(F32)
            for s in range(1, N_DEV):
                gv = gv + g_ref[s].astype(F32)
        else:
            gv = g_ref[...]
        go_ref[0] = gv
        d_ref[0], nm_ref[0], nv_ref[0] = _adam_math(w_ref[0], gv, m_ref[0], v_ref[0])

    spec = pl.BlockSpec((1, tm, C), lambda i: (0, i, 0))
    gspec = pl.BlockSpec((N_DEV, tm, C), lambda i: (0, i, 0)) if parts else pl.BlockSpec((tm, C), _row)
    return pl.pallas_call(
        body, name=name, grid=(R // tm,),
        in_specs=[spec, gspec, spec, spec], out_specs=[spec] * 4,
        out_shape=[jax.ShapeDtypeStruct((1, R, C), F32)] * 4,
        compiler_params=_cp(("parallel",)),
    )(w, g, m, v)


def _mods_part(c_all, w_ada, b_ada, *, name):
    C = w_ada.shape[1]

    def body(c_ref, w_ref, b_ref, o_ref):
        cv = c_ref[...]
        ca = cv * jax.nn.sigmoid(cv)
        o_ref[...] = jnp.dot(ca, w_ref[...], preferred_element_type=F32,
                             precision=lax.Precision.HIGHEST) + b_ref[...]

    return pl.pallas_call(
        body, name=name,
        out_shape=jax.ShapeDtypeStruct((N_DEV, C), F32),
        compiler_params=_cp(),
    )(c_all, w_ada, b_ada)


def _wada_grad(c_all_t, gm, *, name):
    C = gm.shape[1]

    def body(c_ref, g_ref, o_ref):
        cv = c_ref[...]
        ca = cv * jax.nn.sigmoid(cv)
        acc = ca[:, 0:1] * g_ref[0:1, :]
        for b in range(1, N_DEV):
            acc = acc + ca[:, b:b + 1] * g_ref[b:b + 1, :]
        o_ref[...] = acc

    return pl.pallas_call(
        body, name=name,
        out_shape=jax.ShapeDtypeStruct((D, C), F32),
        compiler_params=_cp(),
    )(c_all_t, gm)


def _peer(x, y, c, d):
    px = lax.rem(x + ((d >> 2) & 1), 2)
    py = lax.rem(y + ((d >> 1) & 1), 2)
    pc = lax.rem(c + (d & 1), 2)
    return (px, py, pc), 4 * px + 2 * py + pc


def _exchange(xs, *, scatter, name):
    n = len(xs)
    nsem = n * (N_DEV - 1)

    def body(*refs):
        ins, outs = refs[:n], refs[n:2 * n]
        token, send_sems, recv_sems, local_sems = refs[2 * n:]
        x, y, c = lax.axis_index("x"), lax.axis_index("y"), lax.axis_index("c")
        me = 4 * x + 2 * y + c
        token[...] = jnp.zeros_like(token)

        def src(t, idx):
            return ins[t].at[idx] if scatter else ins[t]

        local = [pltpu.make_async_copy(src(t, me), outs[t].at[me], local_sems.at[t]) for t in range(n)]
        for cp in local:
            cp.start()
        remote = []
        for t in range(n):
            for d in range(1, N_DEV):
                peer, pidx = _peer(x, y, c, d)
                k = t * (N_DEV - 1) + d - 1
                send = pltpu.make_async_remote_copy(src_ref=src(t, pidx), dst_ref=outs[t].at[me],
                                                    send_sem=send_sems.at[k], recv_sem=recv_sems.at[k],
                                                    device_id=peer, device_id_type=MESH)
                recv = pltpu.make_async_remote_copy(src_ref=src(t, pidx), dst_ref=outs[t].at[pidx],
                                                    send_sem=send_sems.at[k], recv_sem=recv_sems.at[k],
                                                    device_id=peer, device_id_type=MESH)
                send.start()
                remote.append((send, recv))
        for cp in local:
            cp.wait()
        for send, recv in remote:
            send.wait_send()
            recv.wait_recv()

    anyspec = pl.BlockSpec(memory_space=pl.ANY)
    out_shape = [jax.ShapeDtypeStruct(a.shape if scatter else (N_DEV,) + a.shape, a.dtype) for a in xs]
    out_shape.append(jax.ShapeDtypeStruct((8, 128), F32))
    return pl.pallas_call(
        body, name=name,
        in_specs=[anyspec] * n, out_specs=[anyspec] * n + [pl.BlockSpec(memory_space=pltpu.VMEM)],
        out_shape=out_shape,
        scratch_shapes=[pltpu.SemaphoreType.DMA((nsem,)), pltpu.SemaphoreType.DMA((nsem,)),
                        pltpu.SemaphoreType.DMA((n,))],
    )(*xs)


def _sum8(parts, *, name):
    _, R, C = parts.shape

    def body(p_ref, o_ref):
        acc = p_ref[0]
        for s in range(1, N_DEV):
            acc = acc + p_ref[s]
        o_ref[...] = acc

    return pl.pallas_call(body, name=name, out_shape=jax.ShapeDtypeStruct((R, C), F32),
                          compiler_params=_cp())(parts)


HBM_SPEC = pl.BlockSpec(memory_space=pltpu.HBM)
SEM_SPEC = pl.BlockSpec(memory_space=pltpu.SEMAPHORE)
N_PEER = N_DEV - 1


def _split_copies(src_refs, land_refs, send_sems, recv_sems, scatter):
    x, y, c = lax.axis_index("x"), lax.axis_index("y"), lax.axis_index("c")
    me = 4 * x + 2 * y + c
    pairs = []
    for j, (src, land) in enumerate(zip(src_refs, land_refs)):
        for d in range(1, N_DEV):
            peer, pidx = _peer(x, y, c, d)
            k = j * N_PEER + d - 1
            s = src.at[pidx] if scatter else src
            send = pltpu.make_async_remote_copy(src_ref=s, dst_ref=land.at[me], send_sem=send_sems.at[k],
                                                recv_sem=recv_sems.at[k], device_id=peer, device_id_type=MESH)
            recv = pltpu.make_async_remote_copy(src_ref=s, dst_ref=land.at[pidx], send_sem=send_sems.at[k],
                                                recv_sem=recv_sems.at[k], device_id=peer, device_id_type=MESH)
            pairs.append((send, recv))
    return pairs


def _own_slot(block, me):
    land = lax.empty((N_DEV,) + block.shape, block.dtype)
    return lax.dynamic_update_slice(land, block[None], (me, 0, 0))


def _split_start(srcs, lands, groups, *, scatter, name):
    n, ng = len(srcs), len(groups)

    def body(*refs):
        src_refs, land_refs = refs[:n], refs[n:2 * n]
        sems = refs[2 * n:2 * n + 2 * ng]
        token = refs[-1]
        for gi, g in enumerate(groups):
            pairs = _split_copies([src_refs[t] for t in g], [land_refs[t] for t in g], sems[2 * gi],
                                  sems[2 * gi + 1], scatter)
            for send, _ in pairs:
                send.start()
        token[...] = jnp.zeros_like(token)

    sem_shapes = []
    for g in groups:
        sem_shapes += [pltpu.SemaphoreType.DMA((len(g) * N_PEER,))] * 2
    thru = [pltpu.HBM(a.shape, a.dtype) for a in list(srcs) + list(lands)]
    outs = pl.pallas_call(
        body, name=name,
        out_shape=tuple(sem_shapes + thru + [jax.ShapeDtypeStruct((8, 128), F32)]),
        in_specs=[HBM_SPEC] * (2 * n),
        out_specs=tuple([SEM_SPEC] * (2 * ng) + [HBM_SPEC] * (2 * n) + [pl.BlockSpec(memory_space=pltpu.VMEM)]),
        input_output_aliases={i: 2 * ng + i for i in range(2 * n)},
        compiler_params=pltpu.CompilerParams(has_side_effects=pltpu.SideEffectType.DATAFLOW_SIDE_EFFECTING),
    )(*[pltpu.with_memory_space_constraint(a, pltpu.HBM) for a in list(srcs) + list(lands)])
    sems = [(outs[2 * gi], outs[2 * gi + 1]) for gi in range(ng)]
    return sems, outs[2 * ng:2 * ng + n], outs[2 * ng + n:2 * ng + 2 * n], outs[-1]


def _behind(v, token):
    if token is None:
        return v
    return v + token[0, 0].astype(v.dtype)


def _split_wait(srcs, lands, sems, after, *, scatter, name):
    m = len(srcs)

    def body(*refs):
        src_refs, land_refs = refs[:m], refs[m:2 * m]
        send_sems, recv_sems = refs[2 * m], refs[2 * m + 1]
        for send, recv in _split_copies(src_refs, land_refs, send_sems, recv_sems, scatter):
            send.wait_send()
            recv.wait_recv()

    outs = pl.pallas_call(
        body, name=name,
        out_shape=tuple(pltpu.HBM(a.shape, a.dtype) for a in list(srcs) + list(lands)),
        in_specs=[HBM_SPEC] * (2 * m) + [SEM_SPEC, SEM_SPEC, pl.BlockSpec(memory_space=pl.ANY)],
        out_specs=tuple([HBM_SPEC] * (2 * m)),
        input_output_aliases={i: i for i in range(2 * m)},
        compiler_params=pltpu.CompilerParams(has_side_effects=pltpu.SideEffectType.DATAFLOW_SIDE_EFFECTING),
    )(*srcs, *lands, sems[0], sems[1], after)
    return outs[m:]


TL_FIRST = (1, 2, 4, 6)
TL_ICI = (2, 4, 6)
EFFECT = pltpu.SideEffectType.DATAFLOW_SIDE_EFFECTING


def _tl_first(src_refs, land_refs, send_sems, recv_sems):
    x, y, c = lax.axis_index("x"), lax.axis_index("y"), lax.axis_index("c")
    me = 4 * x + 2 * y + c
    out = []
    for j, (src, land) in enumerate(zip(src_refs, land_refs)):
        for i, d in enumerate(TL_FIRST):
            peer, pidx = _peer(x, y, c, d)
            k = len(TL_FIRST) * j + i
            send = pltpu.make_async_remote_copy(src_ref=src, dst_ref=land.at[me], send_sem=send_sems.at[k],
                                                recv_sem=recv_sems.at[k], device_id=peer, device_id_type=MESH)
            recv = pltpu.make_async_remote_copy(src_ref=src, dst_ref=land.at[pidx], send_sem=send_sems.at[k],
                                                recv_sem=recv_sems.at[k], device_id=peer, device_id_type=MESH)
            out.append((d, send, recv))
    return out


def _tl_second(land_refs, send_sems, recv_sems):
    x, y, c = lax.axis_index("x"), lax.axis_index("y"), lax.axis_index("c")
    sibling, _ = _peer(x, y, c, 1)
    out = []
    for j, land in enumerate(land_refs):
        for i, d in enumerate(TL_ICI):
            _, mine = _peer(x, y, c, d)
            _, theirs = _peer(x, y, c, d + 1)
            k = len(TL_ICI) * j + i
            send = pltpu.make_async_remote_copy(src_ref=land.at[mine], dst_ref=land.at[mine], send_sem=send_sems.at[k],
                                                recv_sem=recv_sems.at[k], device_id=sibling, device_id_type=MESH)
            recv = pltpu.make_async_remote_copy(src_ref=land.at[mine], dst_ref=land.at[theirs],
                                                send_sem=send_sems.at[k], recv_sem=recv_sems.at[k],
                                                device_id=sibling, device_id_type=MESH)
            out.append((send, recv))
    return out


def _tl_start(srcs, lands, groups, *, name):
    n, ng = len(srcs), len(groups)

    def body(*refs):
        src_refs, land_refs = refs[:n], refs[n:2 * n]
        sems = refs[2 * n:2 * n + 2 * ng]
        for gi, g in enumerate(groups):
            for _, send, _ in _tl_first([src_refs[t] for t in g], [land_refs[t] for t in g], sems[2 * gi],
                                        sems[2 * gi + 1]):
                send.start()
        refs[-1][...] = jnp.zeros_like(refs[-1])

    sem_shapes = []
    for g in groups:
        sem_shapes += [pltpu.SemaphoreType.DMA((len(g) * len(TL_FIRST),))] * 2
    thru = [pltpu.HBM(a.shape, a.dtype) for a in list(srcs) + list(lands)]
    outs = pl.pallas_call(
        body, name=name,
        out_shape=tuple(sem_shapes + thru + [jax.ShapeDtypeStruct((8, 128), F32)]),
        in_specs=[HBM_SPEC] * (2 * n),
        out_specs=tuple([SEM_SPEC] * (2 * ng) + [HBM_SPEC] * (2 * n) + [pl.BlockSpec(memory_space=pltpu.VMEM)]),
        input_output_aliases={i: 2 * ng + i for i in range(2 * n)},
        compiler_params=pltpu.CompilerParams(has_side_effects=EFFECT),
    )(*[pltpu.with_memory_space_constraint(a, pltpu.HBM) for a in list(srcs) + list(lands)])
    sems = [(outs[2 * gi], outs[2 * gi + 1]) for gi in range(ng)]
    return sems, outs[2 * ng:2 * ng + n], outs[2 * ng + n:2 * ng + 2 * n], outs[-1]


def _tl_forward(srcs, lands, sems1, after, *, name):
    m = len(srcs)

    def body(*refs):
        src_refs, land_refs = refs[:m], refs[m:2 * m]
        send1, recv1 = refs[2 * m], refs[2 * m + 1]
        send2, recv2 = refs[2 * m + 3], refs[2 * m + 4]
        for d, _, recv in _tl_first(src_refs, land_refs, send1, recv1):
            if d in TL_ICI:
                recv.wait_recv()
        for send, _ in _tl_second(land_refs, send2, recv2):
            send.start()

    sem = pltpu.SemaphoreType.DMA((m * len(TL_ICI),))
    outs = pl.pallas_call(
        body, name=name,
        out_shape=tuple([sem, sem] + [pltpu.HBM(a.shape, a.dtype) for a in list(srcs) + list(lands)]),
        in_specs=[HBM_SPEC] * (2 * m) + [SEM_SPEC, SEM_SPEC, pl.BlockSpec(memory_space=pl.ANY)],
        out_specs=tuple([SEM_SPEC, SEM_SPEC] + [HBM_SPEC] * (2 * m)),
        input_output_aliases={i: 2 + i for i in range(2 * m)},
        compiler_params=pltpu.CompilerParams(has_side_effects=EFFECT),
    )(*srcs, *lands, sems1[0], sems1[1], after)
    return (outs[0], outs[1]), outs[2:2 + m], outs[2 + m:2 + 2 * m]


def _tl_wait(srcs, lands, sems1, sems2, after, *, name):
    m = len(srcs)

    def body(*refs):
        src_refs, land_refs = refs[:m], refs[m:2 * m]
        send1, recv1, send2, recv2 = refs[2 * m:2 * m + 4]
        for d, send, recv in _tl_first(src_refs, land_refs, send1, recv1):
            send.wait_send()
            if d not in TL_ICI:
                recv.wait_recv()
        for send, recv in _tl_second(land_refs, send2, recv2):
            send.wait_send()
            recv.wait_recv()

    outs = pl.pallas_call(
        body, name=name,
        out_shape=tuple(pltpu.HBM(a.shape, a.dtype) for a in list(srcs) + list(lands)),
        in_specs=[HBM_SPEC] * (2 * m) + [SEM_SPEC] * 4 + [pl.BlockSpec(memory_space=pl.ANY)],
        out_specs=tuple([HBM_SPEC] * (2 * m)),
        input_output_aliases={i: i for i in range(2 * m)},
        compiler_params=pltpu.CompilerParams(has_side_effects=EFFECT),
    )(*srcs, *lands, sems1[0], sems1[1], sems2[0], sems2[1], after)
    return outs[m:]


TM_PROJ = 512
TN_PROJ = 512
TM_ROW = 512
TM_NN = 512
TK_TN = 2048
TM_ADAM = 208
TN_FFN = F // 2
TN_IN = NIN // 4


def _tn(a, b, name, tn, token=None):
    if a.ndim == 2:
        a = a[None]
    return _tn_matmul(a, b, token, tn=tn, tk=TK_TN, name=name)


def _local_step(x, tgt, mods, g1, gm, g2, gf, convw8, sinks, w_get, g_put, tables=None):
    T = x.shape[0]
    sh1, sc1, gt1, sh2, sc2, gt2, sh3, sc3, gt3 = [mods[i:i + 1] for i in range(N_MOD)]
    cos, sin = _rope_tables(T) if tables is None else tables
    behind = _behind

    w = dict(w_get("gu1", mods))
    h1, ab1 = _norm_proj(x, g1, sc1, sh1, w["gu1"], tm=TM_PROJ, tn=TN_PROJ, name="ffn1_up")
    w.update(w_get("d1", ab1))
    x1 = _ffn_down_fwd(ab1, w["d1"], x, gt1, tm=TM_ROW, name="ffn1_down")
    w.update(w_get("mix", x1))
    h2, proj, qs, kr = _norm_proj(x1, gm, sc2, sh2, w["win"], (cos, sin), tm=TM_PROJ, tn=TN_PROJ, name="mix_in")
    bias = _attn_bias()
    attn, lse = _attn_fwd(qs, kr, proj, bias, sinks, name="attn_fwd")
    x2, gc, yc, ya, mg, o = _mixer_mid_fwd(proj, attn, w["cp"], w["ap"], w["out"], convw8, x1, gt2,
                                           tm=TM_ROW, name="mix_mid")
    w.update(w_get("ffn2", x2))
    h3, ab2, dx3, lsum, dgf = _ffn_fwd(x2, g2, sc3, sh3, gt3, w["gu2"], w["d2"], (tgt, gf), tm=TM_ROW,
                                           name="ffn2_final")

    dab2, dgt3, g_d2 = _ffn_down_bwd_dw(dx3, gt3, ab2, w["d2"], tm=TM_ROW, name="ffn2_down_bwd")
    dx2, dsh3, dsc3, dg2 = _nn_bwd_norm(dab2, w["gu2"], x2, g2, sc3, dx3, tm=TM_NN, name="ffn2_up_bwd")
    g_gu2 = _tn(dab2, h3, "ffn2_up_dw", TN_FFN)
    tok = g_put(dict(gu2=g_gu2, d2=g_d2))

    dout, dyc, dya, dgc, dat, dproj, dgt2 = _mixer_mid_bwd(dx2, behind(gt2, tok), o, proj, yc, ya, w["out"], w["cp"],
                                                           w["ap"], tm=TM_ROW, name="mix_mid_bwd")
    g_out = _tn(mg, dout, "mix_out_dw", D)
    g_cp = _tn(gc, dyc, "mix_cp_dw", D)
    g_ap = _tn(attn, dya, "mix_ap_dw", D)
    dproj, dkc, dkp, dvc, dvp, dsink = _attn_bwd(qs, kr, proj, bias, sinks, lse, attn, dat, cos, sin, dproj,
                                                 name="attn_bwd")
    dproj = _dkv_combine(dkc, dkp, dvc, dvp, dproj, name="attn_dkv")
    dproj, dcw = _conv_bwd(dgc, proj, convw8, dproj, tm=TM_ROW, name="conv_bwd")
    g_in = _tn(dproj, h2, "mix_in_dw", TN_IN)
    tok = g_put(dict(win=g_in, cp=g_cp, ap=g_ap, out=g_out))
    dx1, dsh2, dsc2, dgm = _nn_bwd_norm(dproj[None], w["win"], x1, gm, behind(sc2, tok), dx2, tm=TM_NN,
                                        name="mix_in_bwd")

    dab1, dgt1, g_d1 = _ffn_down_bwd_dw(dx1, gt1, ab1, w["d1"], tm=TM_ROW, name="ffn1_down_bwd")
    tok = g_put(dict(d1=g_d1))
    g_gu1 = _tn(dab1, h1, "ffn1_up_dw", TN_FFN, tok)
    tok = g_put(dict(gu1=g_gu1))
    dx0, dsh1, dsc1, dg1 = _nn_bwd_norm(dab1, w["gu1"], x, g1, behind(sc1, tok), dx1, tm=TM_NN,
                                        name="ffn1_up_bwd")

    small = dict(mods=jnp.concatenate([dsh1, dsc1, dgt1, dsh2, dsc2, dgt2, dsh3, dsc3, dgt3], axis=0),
                 g1=dg1, gm=dgm, g2=dg2, gf=dgf, convw=dcw[0:3], sinks=dsink[:, 0:N_HEADS])
    return lsum, dx0, small


BIG = ("gu1", "d1", "win", "cp", "ap", "out", "gu2", "d2")
TRANSPOSED = ("gu1", "win", "gu2")
SMALL_ROWS = 24
R_MODS, R_G1, R_GM, R_G2, R_GF, R_CONV, R_SINK, R_LOSS = 0, 9, 10, 11, 12, 13, 16, 17


def _pad_to(a, rows, cols):
    return jnp.pad(a, ((0, rows - a.shape[0]), (0, cols - a.shape[1])))


def _pack_small(b_ada, g1, gm, g2, gf, conv, sinks, lsum):
    rows = [b_ada.reshape(N_MOD, D), g1.reshape(1, D), gm.reshape(1, D), g2.reshape(1, D), gf.reshape(1, D),
            _pad_to(conv.reshape(3, -1), 3, D), _pad_to(sinks.reshape(1, N_HEADS), 1, D), lsum.reshape(1, D)]
    return _pad_to(jnp.concatenate(rows, axis=0), SMALL_ROWS, D)


def kernel(x, c, w_ada, b_ada, g_ffn1, w1_gu, w1_down, g_mix, w_in, conv_w, w_conv_proj, w_attn_proj, sinks, w_out, g_ffn2, w2_gu, w2_down, g_final, loss_target, m_w_ada, m_b_ada, m_g_ffn1, m_w1_gu, m_w1_down, m_g_mix, m_w_in, m_conv_w, m_w_conv_proj, m_w_attn_proj, m_sinks, m_w_out, m_g_ffn2, m_w2_gu, m_w2_down, m_g_final, v_w_ada, v_b_ada, v_g_ffn1, v_w1_gu, v_w1_down, v_g_mix, v_w_in, v_conv_w, v_w_conv_proj, v_w_attn_proj, v_sinks, v_w_out, v_g_ffn2, v_w2_gu, v_w2_down, v_g_final):
    me = 4 * lax.axis_index("x") + 2 * lax.axis_index("y") + lax.axis_index("c")
    ada_cols = w_ada.shape[2]
    conv_cols = conv_w.shape[2]

    native = dict(gu1=w1_gu[0], d1=w1_down[0], win=w_in[0], cp=w_conv_proj[0], ap=w_attn_proj[0], out=w_out[0],
                  gu2=w2_gu[0], d2=w2_down[0])

    def shard(n, token):
        a = _behind(native[n], token)
        return (a.T if n in TRANSPOSED else a).astype(BF)

    c_all, conv_all, _ = _exchange([c, _pad_to(conv_w[0], 8, conv_cols)], scatter=False, name="gather_cond")
    c_all = c_all.reshape(N_DEV, D)
    conv_full = conv_all[:, 0:3, :].transpose(1, 0, 2).reshape(3, D)

    b_cols = lax.dynamic_slice(b_ada, (0, me * ada_cols), (1, ada_cols))
    mods_cols = _mods_part(c_all, w_ada[0], b_cols, name="ada_mods")
    mods_all, mods_token = _exchange([mods_cols], scatter=False, name="gather_mods")
    mods = lax.dynamic_index_in_dim(mods_all, me, axis=1, keepdims=False).reshape(N_MOD, D)

    groups = dict(gu1=("gu1",), d1=("d1",), mix=("win", "cp", "ap", "out"), ffn2=("gu2", "d2"))
    in_flight = {}
    first = [shard("gu1", mods_token)]
    sems, srcs, lands, token = _tl_start(first, [_own_slot(s, me) for s in first], [[0]],
                                         name="gather_weights_start_gu1")
    in_flight["gu1"] = [sems[0], srcs, lands, None]
    rest = [n for n in BIG if n != "gu1"]
    shards = [shard(n, token) for n in rest]
    rest_groups = [[rest.index(n) for n in names] for g, names in groups.items() if g != "gu1"]
    sems, srcs, lands, rest_token = _tl_start(shards, [_own_slot(s, me) for s in shards], rest_groups,
                                              name="gather_weights_start_rest")
    for (g, names), gsems, idx in zip([kv for kv in groups.items() if kv[0] != "gu1"], sems, rest_groups):
        in_flight[g] = [gsems, [srcs[t] for t in idx], [lands[t] for t in idx], None]

    def forward(group, after):
        sems1, gsrcs, glands, _ = in_flight[group]
        sems2, gsrcs, glands = _tl_forward(gsrcs, glands, sems1, after, name="gather_weights_forward_" + group)
        in_flight[group] = [sems1, gsrcs, glands, sems2]

    forward_early = dict(d1="mix", mix="ffn2")

    tables = _rope_tables(x.shape[1], rest_token)

    def w_get(group, after):
        if group == "gu1":
            after = tables[0]
        if in_flight[group][3] is None:
            forward(group, after)
        sems1, gsrcs, glands, sems2 = in_flight[group]
        landed = _tl_wait(gsrcs, glands, sems1, sems2, after, name="gather_weights_wait_" + group)
        if group in forward_early:
            forward(forward_early[group], landed[0])
        return {n: a.reshape(-1, D) for n, a in zip(groups[group], landed)}

    pending = []

    def g_put(gs):
        names = tuple(gs)
        srcs = [gs[n].reshape(N_DEV, -1, D) for n in names]
        lands = [_own_slot(lax.dynamic_index_in_dim(s, me, axis=0, keepdims=False), me) for s in srcs]
        sems, srcs, lands, tok = _split_start(srcs, lands, [list(range(len(names)))], scatter=True,
                                              name="scatter_grads_start_" + names[0])
        pending.append((names, sems[0], srcs, lands))
        return tok

    lsum, grad_x, small = _local_step(x[0], loss_target[0], mods, g_ffn1, g_mix, g_ffn2, g_final[None],
                                      _pad_to(conv_full, 8, D), sinks[0], w_get, g_put, tables)

    packed = _pack_small(small["mods"], small["g1"], small["gm"], small["g2"], small["gf"], small["convw"],
                         small["sinks"], lsum)
    sm_sems, sm_srcs, sm_lands, sm_token = _split_start([packed], [_own_slot(packed, me)], [[0]], scatter=False,
                                                        name="gather_small_start")

    w_of = dict(ada=w_ada, gu1=w1_gu, d1=w1_down, win=w_in, cp=w_conv_proj, ap=w_attn_proj, out=w_out, gu2=w2_gu,
                d2=w2_down)
    m_of = dict(ada=m_w_ada, gu1=m_w1_gu, d1=m_w1_down, win=m_w_in, cp=m_w_conv_proj, ap=m_w_attn_proj, out=m_w_out,
                gu2=m_w2_gu, d2=m_w2_down)
    v_of = dict(ada=v_w_ada, gu1=v_w1_gu, d1=v_w1_down, win=v_w_in, cp=v_w_conv_proj, ap=v_w_attn_proj, out=v_w_out,
                gu2=v_w2_gu, d2=v_w2_down)
    upd = {}
    after = sm_token
    for k, (names, sems, srcs, lands) in enumerate(pending):
        if k == 2:
            (packed_all,) = _split_wait(sm_srcs, sm_lands, sm_sems[0], after, scatter=False, name="gather_small_wait")
            gsmall = _sum8(packed_all, name="sum_small")
            loss = (0.5 / D) * jnp.sum(gsmall[R_LOSS])
            after = gsmall
        parts = _split_wait(srcs, lands, sems, after, scatter=True, name="scatter_grads_wait_" + names[0])
        for n, p in zip(names, parts):
            if n in TRANSPOSED:
                res = _adam(jnp.swapaxes(w_of[n], 1, 2), p, jnp.swapaxes(m_of[n], 1, 2), jnp.swapaxes(v_of[n], 1, 2),
                            tm=TM_ADAM, name="adam_" + n)
                upd[n] = [jnp.swapaxes(t, 1, 2) for t in res]
            else:
                upd[n] = _adam(w_of[n], p, m_of[n], v_of[n], tm=TM_ADAM, name="adam_" + n)
        after = upd[names[-1]][1]

    gm_cols = lax.dynamic_slice(packed_all[:, R_MODS:R_MODS + N_MOD, :].reshape(N_DEV, N_MOD * D),
                                (0, me * ada_cols), (N_DEV, ada_cols))
    upd["ada"] = _adam(w_ada, _wada_grad(c_all.T, gm_cols, name="ada_dw"), m_w_ada, v_w_ada, tm=256, name="adam_ada")
    conv_g = lax.dynamic_slice(gsmall, (R_CONV, me * conv_cols), (3, conv_cols))

    def natural(b, g1, gm, g2, gf, cw, sk):
        return dict(b_ada=b, g_ffn1=g1, g_mix=gm, g_ffn2=g2, g_final=gf[None], conv_w=cw[0], sinks=sk)

    small_out = _adam_small(gsmall, conv_g, natural(b_ada, g_ffn1, g_mix, g_ffn2, g_final, conv_w, sinks),
                            natural(m_b_ada, m_g_ffn1, m_g_mix, m_g_ffn2, m_g_final, m_conv_w, m_sinks),
                            natural(v_b_ada, v_g_ffn1, v_g_mix, v_g_ffn2, v_g_final, v_conv_w, v_sinks),
                            name="adam_small")
    for res in small_out:
        res["g_final"] = res["g_final"][0]
        res["conv_w"] = res["conv_w"][None]

    big_name = dict(w_ada="ada", w1_gu="gu1", w1_down="d1", w_in="win", w_conv_proj="cp", w_attn_proj="ap",
                    w_out="out", w2_gu="gu2", w2_down="d2")
    order = ("w_ada", "b_ada", "g_ffn1", "w1_gu", "w1_down", "g_mix", "w_in", "conv_w", "w_conv_proj", "w_attn_proj",
             "sinks", "w_out", "g_ffn2", "w2_gu", "w2_down", "g_final")
    outs = [loss, grad_x[None]]
    for kind in range(4):
        for n in order:
            outs.append(upd[big_name[n]][kind] if n in big_name else small_out[kind][n])
    return tuple(outs)
```

```python
import jax
import jax.numpy as jnp
from jax import lax
from jax.experimental import pallas as pl
from jax.experimental.pallas import tpu as pltpu

D = 1024
F = 2816
NIN = 6656
N_HEADS = 16
N_KV = 4
HEAD_DIM = 64
BLK = 128
N_MOD = 9
N_DEV = 8
EPS = 1e-6
NEG_INF = -1e30
ROPE_THETA = 10000.0
O_BG, O_CG, O_U, O_Q, O_K, O_V, O_ZC, O_ZA = 0, 1024, 2048, 3072, 4096, 4352, 4608, 5632

ADAM_LR = 0.001
ADAM_B1 = 0.9
ADAM_B2 = 0.999
ADAM_EPS = 1e-08
ADAM_WD = 0.01
ADAM_STEP = 10

BF = jnp.bfloat16
F32 = jnp.float32
VMEM_LIMIT = 56 * 1024 * 1024
MXU_N = 256
DOWN_BWD_PARTS = 1
MESH = pl.DeviceIdType.MESH

NT = (((1,), (1,)), ((), ()))
TN = (((0,), (0,)), ((), ()))


def _cp(sem=None):
    return pltpu.CompilerParams(dimension_semantics=sem, vmem_limit_bytes=VMEM_LIMIT)


def _tile(n, pref):
    if n <= pref:
        return n
    for t in range(pref - pref % 16, 15, -16):
        if n % t == 0:
            return t
    raise ValueError((n, pref))


def _sigmoid(v):
    return 0.5 * jnp.tanh(0.5 * v) + 0.5


def _row(i):
    return (i, 0)


def _const2(*_):
    return (0, 0)


def _resident(shape):
    return pl.BlockSpec(shape, lambda *_: (0,) * len(shape), pipeline_mode=pl.Buffered(1))


def _norm_proj(x, g, sc, sh, wt, rope=None, *, tm, tn, name):
    T, N = x.shape[0], wt.shape[0]
    tm = _tile(T, tm)

    def body(x_ref, g_ref, sc_ref, sh_ref, w_ref, *rest):
        if rope is None:
            h_ref, o_ref = rest
        else:
            c_ref, s_ref, h_ref, o_ref, qs_ref, kr_ref = rest
        xv = x_ref[...]
        r = lax.rsqrt(jnp.mean(xv * xv, axis=-1, keepdims=True) + EPS)
        hb = ((xv * r) * g_ref[...] * (1.0 + sc_ref[...]) + sh_ref[...]).astype(BF)
        h_ref[...] = hb
        for c0 in range(0, N, tn):
            cols = pl.ds(c0, tn)
            o_ref[:, cols] = lax.dot_general(hb, w_ref[cols, :], NT, preferred_element_type=F32).astype(BF)
            if rope is not None and c0 < O_V <= c0 + tn:
                _attn_prep_tile(o_ref, c_ref, s_ref, qs_ref, kr_ref, tm)

    vec = pl.BlockSpec((1, D), _const2)
    rowspec = pl.BlockSpec((tm, D), _row)
    in_specs = [rowspec, vec, vec, vec, _resident((N, D))]
    out_specs = [rowspec, pl.BlockSpec((tm, N), _row)]
    out_shape = [jax.ShapeDtypeStruct((T, D), BF), jax.ShapeDtypeStruct((T, N), BF)]
    args = [x, g, sc, sh, wt]
    if rope is not None:
        in_specs += [pl.BlockSpec((tm, 128), _row)] * 2
        out_specs += [pl.BlockSpec((N_KV, 4 * tm, 128), lambda i: (0, i, 0)), pl.BlockSpec((tm, 256), _row)]
        out_shape += [jax.ShapeDtypeStruct((N_KV, 4 * T, 128), BF), jax.ShapeDtypeStruct((T, 256), BF)]
        args += list(rope)
    return pl.pallas_call(
        body, name=name, grid=(T // tm,),
        in_specs=in_specs, out_specs=out_specs, out_shape=out_shape,
        compiler_params=_cp(("parallel",)),
    )(*args)


def _ffn_down_fwd(ab, wd, x, gt, *, tm, name):
    T = x.shape[0]
    tm = _tile(T, tm)

    def body(a_ref, b_ref, wd_ref, x_ref, gt_ref, xo_ref):
        y = None
        for c0 in range(0, F, MXU_N):
            cols = pl.ds(c0, MXU_N)
            a = a_ref[:, cols].astype(F32)
            act = (a * _sigmoid(a) * b_ref[:, cols].astype(F32)).astype(BF)
            part = jnp.dot(act, wd_ref[cols, :], preferred_element_type=F32)
            y = part if y is None else y + part
        xo_ref[...] = x_ref[...] + (0.5 * gt_ref[...]) * y

    return pl.pallas_call(
        body, name=name, grid=(T // tm,),
        in_specs=[pl.BlockSpec((tm, F), lambda i: (i, 0)), pl.BlockSpec((tm, F), lambda i: (i, 1)),
                  _resident((F, D)), pl.BlockSpec((tm, D), _row), pl.BlockSpec((1, D), _const2)],
        out_specs=pl.BlockSpec((tm, D), _row),
        out_shape=jax.ShapeDtypeStruct((T, D), F32),
        compiler_params=_cp(("parallel",)),
    )(ab, ab, wd, x, gt)


def _ffn_fwd(x, g, sc, sh, gt, wgu, wd, final, *, tm, name):
    T = x.shape[0]
    tm = _tile(T, tm)
    last = final is not None

    def body(x_ref, g_ref, sc_ref, sh_ref, gt_ref, wgu_ref, wd_ref, *rest):
        if last:
            t_ref, gf_ref, h_ref, ab_ref, dx_ref, ls_ref, dgf_ref = rest
        else:
            h_ref, ab_ref, xo_ref = rest
        xv = x_ref[...]
        r = lax.rsqrt(jnp.mean(xv * xv, axis=-1, keepdims=True) + EPS)
        hb = ((xv * r) * g_ref[...] * (1.0 + sc_ref[...]) + sh_ref[...]).astype(BF)
        h_ref[...] = hb
        y = None
        for c0 in range(0, F, MXU_N):
            a = lax.dot_general(hb, wgu_ref[pl.ds(c0, MXU_N), :], NT, preferred_element_type=F32)
            b = lax.dot_general(hb, wgu_ref[pl.ds(F + c0, MXU_N), :], NT, preferred_element_type=F32)
            ab = a.astype(BF)
            bb = b.astype(BF)
            ab_ref[:, pl.ds(c0, MXU_N)] = ab
            ab_ref[:, pl.ds(F + c0, MXU_N)] = bb
            a = ab.astype(F32)
            act = (a * _sigmoid(a) * bb.astype(F32)).astype(BF)
            part = jnp.dot(act, wd_ref[pl.ds(c0, MXU_N), :], preferred_element_type=F32)
            y = part if y is None else y + part
        xo = xv + (0.5 * gt_ref[...]) * y
        if not last:
            xo_ref[...] = xo
            return

        @pl.when(pl.program_id(0) == 0)
        def _():
            ls_ref[...] = jnp.zeros_like(ls_ref)
            dgf_ref[...] = jnp.zeros_like(dgf_ref)
        gv = gf_ref[...]
        r = lax.rsqrt(jnp.mean(xo * xo, axis=-1, keepdims=True) + EPS)
        xh = xo * r
        e = xh * gv - t_ref[...]
        ls_ref[...] += jnp.sum(e * e, axis=0, keepdims=True)
        dy = e * (1.0 / D)
        dgf_ref[...] += jnp.sum(dy * xh, axis=0, keepdims=True)
        dxh = dy * gv
        dx_ref[...] = r * (dxh - xh * jnp.mean(dxh * xh, axis=-1, keepdims=True))

    vec = pl.BlockSpec((1, D), _const2)
    rowspec = pl.BlockSpec((tm, D), _row)
    in_specs = [rowspec, vec, vec, vec, vec, _resident((2 * F, D)), _resident((F, D))]
    out_specs = [rowspec, pl.BlockSpec((tm, 2 * F), _row), rowspec]
    out_shape = [jax.ShapeDtypeStruct((T, D), BF), jax.ShapeDtypeStruct((T, 2 * F), BF),
                 jax.ShapeDtypeStruct((T, D), F32)]
    args = [x, g, sc, sh, gt, wgu, wd]
    if last:
        in_specs += [rowspec, vec]
        out_specs += [vec, vec]
        out_shape += [jax.ShapeDtypeStruct((1, D), F32)] * 2
        args += list(final)
    return pl.pallas_call(
        body, name=name, grid=(T // tm,),
        in_specs=in_specs, out_specs=out_specs, out_shape=out_shape,
        compiler_params=_cp(("arbitrary",) if last else ("parallel",)),
    )(*args)


def _ffn_down_bwd_dw(dxo, gt, ab, wd, *, tm, name):
    T = dxo.shape[0]
    tm = _tile(T, tm)
    nt = T // tm
    nh = DOWN_BWD_PARTS
    hw = F // nh
    chunks = [(c0, min(MXU_N, hw - c0)) for c0 in range(0, hw, MXU_N)]

    def body(dxo_ref, gt_ref, a_ref, b_ref, wd_ref, dab_ref, dgt_ref, dwd_ref, dys, dyt, acc, stage, sem):
        i, j = pl.program_id(0), pl.program_id(1)

        @pl.when(jnp.logical_and(i == 0, j == 0))
        def _():
            dgt_ref[...] = jnp.zeros_like(dgt_ref)

        @pl.when(j == 0)
        def _():
            dxv = dxo_ref[...]
            dys[...] = ((0.5 * gt_ref[...]) * dxv).astype(BF)
            dyt[...] = dxv.T.astype(BF)

        def half(jj):
            @pl.when(i == 0)
            def _():
                acc[jj] = jnp.zeros((D, hw), F32)

            dy = dys[...]
            dy_t = dyt[...]
            for c0, cw in chunks:
                cols = pl.ds(c0, cw)
                dact = lax.dot_general(dy, wd_ref[pl.ds(jj * hw + c0, cw), :], NT, preferred_element_type=F32)
                a = a_ref[:, cols].astype(F32)
                b = b_ref[:, cols].astype(F32)
                s = _sigmoid(a)
                silu = a * s
                dab_ref[0, :, cols] = (dact * b * (s * (1.0 + a * (1.0 - s)))).astype(BF)
                dab_ref[1, :, cols] = (dact * silu).astype(BF)
                acc[jj, :, cols] += jnp.dot(dy_t, (silu * b).astype(BF), preferred_element_type=F32)

            @pl.when(i == nt - 1)
            def _():
                half_gt = 0.5 * gt_ref[...]
                for c0, cw in chunks:
                    g_rows = acc[jj, :, pl.ds(c0, cw)].T
                    w_rows = wd_ref[pl.ds(jj * hw + c0, cw), :].astype(F32)
                    dgt_ref[...] += 0.5 * jnp.sum(g_rows * w_rows, axis=0, keepdims=True)
                    stage[0:cw, :] = (g_rows * half_gt).astype(BF)
                    out = pltpu.make_async_copy(stage.at[pl.ds(0, cw)], dwd_ref.at[pl.ds(jj * hw + c0, cw)], sem)
                    out.start()
                    out.wait()

        for jj in range(nh):
            pl.when(j == jj)(lambda jj=jj: half(jj))

    vec = pl.BlockSpec((1, D), _const2)
    rowspec = pl.BlockSpec((tm, D), lambda i, j: (i, 0))
    return pl.pallas_call(
        body, name=name, grid=(nt, nh),
        in_specs=[rowspec, vec, pl.BlockSpec((tm, hw), lambda i, j: (i, j)),
                  pl.BlockSpec((tm, hw), lambda i, j: (i, j + nh)), _resident((F, D))],
        out_specs=[pl.BlockSpec((2, tm, hw), lambda i, j: (0, i, j)), vec, pl.BlockSpec(memory_space=pl.ANY)],
        out_shape=[jax.ShapeDtypeStruct((2, T, F), BF), jax.ShapeDtypeStruct((1, D), F32),
                   jax.ShapeDtypeStruct((F, D), BF)],
        scratch_shapes=[pltpu.VMEM((tm, D), BF), pltpu.VMEM((D, tm), BF), pltpu.VMEM((nh, D, hw), F32),
                        pltpu.VMEM((MXU_N, D), BF), pltpu.SemaphoreType.DMA(())],
        compiler_params=_cp(("arbitrary", "arbitrary")),
    )(dxo, gt, ab, ab, wd)


def _tn_matmul(a, b, token=None, *, tn, tk, name):
    S, T, Ns = a.shape
    tn, tk = _tile(Ns, tn), _tile(T, tk)
    nk, njs = T // tk, Ns // tn
    deps = [] if token is None else [token]

    def body(a_ref, b_ref, *rest):
        o_ref, acc = rest[len(deps):]
        k = pl.program_id(1)

        @pl.when(k == 0)
        def _():
            acc[...] = jnp.zeros_like(acc)
        acc[...] += lax.dot_general(a_ref[0], b_ref[...], TN, preferred_element_type=F32)

        @pl.when(k == nk - 1)
        def _():
            o_ref[...] = acc[...].astype(BF)

    return pl.pallas_call(
        body, name=name, grid=(S * njs, nk),
        in_specs=[pl.BlockSpec((1, tk, tn), lambda j, k: (j // njs, k, j % njs)),
                  pl.BlockSpec((tk, D), lambda j, k: (k, 0))] + [pl.BlockSpec(memory_space=pl.ANY)] * len(deps),
        out_specs=pl.BlockSpec((tn, D), lambda j, k: (j, 0)),
        out_shape=jax.ShapeDtypeStruct((S * Ns, D), BF),
        scratch_shapes=[pltpu.VMEM((tn, D), F32)],
        compiler_params=_cp(("parallel", "arbitrary")),
    )(a, b, *deps)


def _nn_bwd_norm(da, w, x, g, sc, dxo, *, tm, name):
    S, T, Ks = da.shape
    tm = _tile(T, tm)
    rc = _tile(tm, 256)

    def body(da_ref, w_ref, x_ref, g_ref, sc_ref, dxo_ref, dx_ref, dsh_ref, dsc_ref, dg_ref, acc):
        @pl.when(pl.program_id(0) == 0)
        def _():
            dsh_ref[...] = jnp.zeros_like(dsh_ref)
            dsc_ref[...] = jnp.zeros_like(dsc_ref)
            dg_ref[...] = jnp.zeros_like(dg_ref)

        d = jnp.dot(da_ref[0], w_ref[0:Ks, :], preferred_element_type=F32)
        for s in range(1, S):
            d = d + jnp.dot(da_ref[s], w_ref[s * Ks:(s + 1) * Ks, :], preferred_element_type=F32)
        acc[...] = d
        gv = g_ref[...]
        sc1 = 1.0 + sc_ref[...]
        dsh = jnp.zeros((1, D), F32)
        dsc = jnp.zeros((1, D), F32)
        dg = jnp.zeros((1, D), F32)
        for r0 in range(0, tm, rc):
            rows = pl.ds(r0, rc)
            u = acc[rows, :]
            xv = x_ref[rows, :]
            r = lax.rsqrt(jnp.mean(xv * xv, axis=-1, keepdims=True) + EPS)
            xh = xv * r
            dsh = dsh + jnp.sum(u, axis=0, keepdims=True)
            dsc = dsc + jnp.sum(u * (xh * gv), axis=0, keepdims=True)
            us = u * sc1
            dg = dg + jnp.sum(us * xh, axis=0, keepdims=True)
            dxh = us * gv
            dx_ref[rows, :] = dxo_ref[rows, :] + r * (dxh - xh * jnp.mean(dxh * xh, axis=-1, keepdims=True))
        dsh_ref[...] += dsh
        dsc_ref[...] += dsc
        dg_ref[...] += dg

    vec = pl.BlockSpec((1, D), _const2)
    rowspec = pl.BlockSpec((tm, D), _row)
    return pl.pallas_call(
        body, name=name, grid=(T // tm,),
        in_specs=[pl.BlockSpec((S, tm, Ks), lambda i: (0, i, 0)), _resident((S * Ks, D)), rowspec, vec, vec, rowspec],
        out_specs=[rowspec, vec, vec, vec],
        out_shape=[jax.ShapeDtypeStruct((T, D), F32)] + [jax.ShapeDtypeStruct((1, D), F32)] * 3,
        scratch_shapes=[pltpu.VMEM((tm, D), F32)],
        compiler_params=_cp(("arbitrary",)),
    )(da, w, x, g, sc, dxo)


def _rope(t, cos, sin_signed, lt32, inverse=False):
    sel = jnp.where(lt32, pltpu.roll(t, 96, 1), pltpu.roll(t, 32, 1))
    return t * cos - sel * sin_signed if inverse else t * cos + sel * sin_signed


def _rope_tables(T, token=None):
    inv = 1.0 / (ROPE_THETA ** (jnp.arange(0, HEAD_DIM, 2, dtype=F32) / HEAD_DIM))
    ang = _behind(jnp.arange(T, dtype=F32)[:, None] * inv[None, :], token)
    cos, sin = jnp.cos(ang), jnp.sin(ang)
    cos128 = jnp.tile(cos, (1, 4))
    sin128 = jnp.tile(jnp.concatenate([-sin, sin], axis=1), (1, 2))
    return cos128, sin128


QSCALE = HEAD_DIM ** -0.5


def _lane_masks(rows):
    lane = lax.broadcasted_iota(jnp.int32, (rows, 128), 1)
    return (lane % HEAD_DIM) < (HEAD_DIM // 2), [lane < HEAD_DIM, lane >= HEAD_DIM]


def _attn_bias():
    qi = lax.broadcasted_iota(jnp.int32, (4 * BLK, 2 * BLK), 0) % BLK
    kj = lax.broadcasted_iota(jnp.int32, (4 * BLK, 2 * BLK), 1)
    band = (kj > qi) & (kj <= qi + BLK)
    return jnp.stack([jnp.where(band & (kj >= BLK), 0.0, NEG_INF), jnp.where(band, 0.0, NEG_INF)]).astype(F32)


def _attn_prep_tile(proj_ref, c_ref, s_ref, qs_ref, kr_ref, tm):
    lt32, halves = _lane_masks(BLK)
    for b in range(tm // BLK):
        rows = pl.ds(b * BLK, BLK)
        cc, sc = c_ref[rows, :], s_ref[rows, :]
        qr = [_rope(proj_ref[rows, pl.ds(O_Q + p * 128, 128)].astype(F32), cc, sc, lt32) * QSCALE for p in range(8)]
        for g in range(N_KV):
            qs_ref[g, pl.ds(4 * b * BLK, 4 * BLK), :] = _stack_heads(qr, g, halves).astype(BF)
        kr_ref[rows, :] = jnp.concatenate([_rope(proj_ref[rows, pl.ds(O_K + r * 128, 128)].astype(F32), cc, sc, lt32)
                                           for r in range(2)], axis=1).astype(BF)


ATT_BPS = 4
ATT_ROWS = ATT_BPS * BLK


def _before(n):
    return jnp.maximum(ATT_BPS * n - 1, 0)


def _attn_specs():
    return [pl.BlockSpec((N_KV, 4 * ATT_ROWS, 128), lambda n: (0, n, 0)),
            pl.BlockSpec((ATT_ROWS, 256), _row), pl.BlockSpec((BLK, 256), lambda n: (_before(n), 0)),
            pl.BlockSpec((ATT_ROWS, 256), lambda n: (n, O_V // 256)),
            pl.BlockSpec((BLK, 256), lambda n: (_before(n), O_V // 256)),
            pl.BlockSpec((2, 4 * BLK, 2 * BLK), lambda n: (0, 0, 0)),
            pl.BlockSpec(memory_space=pltpu.SMEM)]


def _bands(sb, kc_ref, kp_ref, vc_ref, vp_ref):
    own = pl.ds(sb * BLK, BLK)
    above = pl.ds((sb - 1) * BLK, BLK)
    kb, vb = [], []
    for r in range(2):
        cols = pl.ds(r * 128, 128)
        kprev = kp_ref[:, cols] if sb == 0 else kc_ref[above, cols]
        vprev = vp_ref[:, cols] if sb == 0 else vc_ref[above, cols]
        kb.append(jnp.concatenate([kprev, kc_ref[own, cols]], axis=0))
        vb.append(jnp.concatenate([vprev, vc_ref[own, cols]], axis=0))
    return kb, vb


def _block_bias(sb, bias_ref):
    return bias_ref[jnp.minimum(pl.program_id(0), 1)] if sb == 0 else bias_ref[1]


def _sink_rows(sink_ref, g):
    return jnp.concatenate([jnp.full((BLK, 128), sink_ref[4 * g + hh], F32) for hh in range(4)], axis=0)


def _both(t):
    return jnp.concatenate([t, t], axis=1)


def _unstack_heads(t, g, halves, acc):
    half = g % 2
    for hh in range(4):
        h = 4 * g + hh
        th = jnp.where(halves[half], t[hh * BLK:(hh + 1) * BLK], 0.0)
        if h % 2 != half:
            th = pltpu.roll(th, HEAD_DIM, 1)
        acc[h // 2] = acc[h // 2] + th


def _stack_heads(chunks, g, halves):
    half = g % 2
    parts = []
    for hh in range(4):
        h = 4 * g + hh
        t = chunks[h // 2]
        if h % 2 != half:
            t = pltpu.roll(t, HEAD_DIM, 1)
        parts.append(jnp.where(halves[half], t, 0.0))
    return jnp.concatenate(parts, axis=0)


def _attn_fwd(qs, kr, proj, bias, sinks, *, name):
    T = proj.shape[0]
    assert T % ATT_ROWS == 0

    def body(qs_ref, kc_ref, kp_ref, vc_ref, vp_ref, bias_ref, sink_ref, o_ref, lse_ref):
        _, h128 = _lane_masks(BLK)
        _, h256 = _lane_masks(2 * BLK)
        _, h512 = _lane_masks(4 * BLK)
        groups = range(N_KV)
        sink = [_sink_rows(sink_ref, g) for g in groups]
        for sb in range(ATT_BPS):
            rows = pl.ds(4 * sb * BLK, 4 * BLK)
            kb, vb = _bands(sb, kc_ref, kp_ref, vc_ref, vp_ref)
            outs = [jnp.zeros((BLK, 128), F32) for _ in range(8)]
            bias = _block_bias(sb, bias_ref)
            s = [lax.dot_general(qs_ref[g, rows, :], kb[g // 2], NT, preferred_element_type=F32) + bias for g in groups]
            m = [jnp.maximum(jnp.broadcast_to(jnp.max(s[g], axis=-1, keepdims=True), (4 * BLK, 128)), sink[g])
                 for g in groups]
            p = [jnp.exp(s[g] - _both(m[g])).astype(BF) for g in groups]
            vg = [jnp.where(h256[g % 2], vb[g // 2].astype(F32), 1.0).astype(BF) for g in groups]
            o = [jnp.dot(p[g], vg[g], preferred_element_type=F32) for g in groups]
            denom = [jnp.where(h512[g % 2], pltpu.roll(o[g], HEAD_DIM, 1), o[g]) + jnp.exp(sink[g] - m[g])
                     for g in groups]
            for g in groups:
                lse_ref[g, rows, :] = m[g] + jnp.log(denom[g])
                _unstack_heads(o[g] * (1.0 / denom[g]), g, h128, outs)
            o_ref[pl.ds(sb * BLK, BLK), :] = jnp.concatenate(outs, axis=1).astype(BF)

    return pl.pallas_call(
        body, name=name, grid=(T // ATT_ROWS,),
        in_specs=_attn_specs(),
        out_specs=[pl.BlockSpec((ATT_ROWS, D), _row), pl.BlockSpec((N_KV, 4 * ATT_ROWS, 128), lambda n: (0, n, 0))],
        out_shape=[jax.ShapeDtypeStruct((T, D), BF), jax.ShapeDtypeStruct((N_KV, 4 * T, 128), F32)],
        compiler_params=_cp(("parallel",)),
    )(qs, kr, kr, proj, proj, bias, sinks)


def _attn_bwd(qs, kr, proj, bias, sinks, lse, o, do, cos, sin, dproj, *, name):
    T = proj.shape[0]
    assert T % ATT_ROWS == 0

    def body(qs_ref, kc_ref, kp_ref, vc_ref, vp_ref, bias_ref, sink_ref, lse_ref, o_ref, do_ref,
             cc_ref, sc_ref, cp_ref, sp_ref, dproj_ref, dq_ref, dkc_ref, dkp_ref, dvc_ref, dvp_ref, dsink_ref):
        @pl.when(pl.program_id(0) == 0)
        def _():
            dsink_ref[...] = jnp.zeros_like(dsink_ref)
        lt32, h128 = _lane_masks(BLK)
        lane1 = lax.broadcasted_iota(jnp.int32, (1, 128), 1)
        dsink = jnp.zeros((1, 128), F32)
        groups = range(N_KV)
        for sb in range(ATT_BPS):
            own = pl.ds(sb * BLK, BLK)
            rows = pl.ds(4 * sb * BLK, 4 * BLK)
            kb, vb = _bands(sb, kc_ref, kp_ref, vc_ref, vp_ref)
            oc = [o_ref[own, pl.ds(p * 128, 128)].astype(F32) for p in range(8)]
            doc = [do_ref[own, pl.ds(p * 128, 128)].astype(F32) for p in range(8)]
            dqs = [jnp.zeros((BLK, 128), F32) for _ in range(8)]
            bias = _block_bias(sb, bias_ref)
            q = [qs_ref[g, rows, :] for g in groups]
            lse_g = [lse_ref[g, rows, :] for g in groups]
            s = [lax.dot_general(q[g], kb[g // 2], NT, preferred_element_type=F32) + bias for g in groups]
            dos = [_stack_heads(doc, g, h128) for g in groups]
            dosb = [t.astype(BF) for t in dos]
            dp = [lax.dot_general(dosb[g], vb[g // 2], NT, preferred_element_type=F32) for g in groups]
            delta = [jnp.broadcast_to(jnp.sum(dos[g] * _stack_heads(oc, g, h128), axis=-1, keepdims=True),
                                      (4 * BLK, 128)) for g in groups]
            p = [jnp.exp(s[g] - _both(lse_g[g])) for g in groups]
            ds = [(p[g] * (dp[g] - _both(delta[g]))).astype(BF) for g in groups]
            pb = [t.astype(BF) for t in p]
            dvg = [lax.dot_general(pb[g], dosb[g], TN, preferred_element_type=F32) for g in groups]
            dkg = [lax.dot_general(ds[g], q[g], TN, preferred_element_type=F32) for g in groups]
            dqg = [jnp.dot(ds[g], kb[g // 2], preferred_element_type=F32) * QSCALE for g in groups]
            dvr = [dvg[0] + dvg[1], dvg[2] + dvg[3]]
            dkr = [dkg[0] + dkg[1], dkg[2] + dkg[3]]
            for g in groups:
                _unstack_heads(dqg[g], g, h128, dqs)
                dsk = -jnp.exp(_sink_rows(sink_ref, g) - lse_g[g]) * delta[g]
                for hh in range(4):
                    val = jnp.sum(dsk[hh * BLK:(hh + 1) * BLK], axis=0, keepdims=True)
                    dsink = dsink + jnp.where(lane1 == 4 * g + hh, val, 0.0)
            cc, sc = cc_ref[own, :], sc_ref[own, :]
            cp, sp = (cp_ref[...], sp_ref[...]) if sb == 0 else (cc_ref[pl.ds((sb - 1) * BLK, BLK), :],
                                                                  sc_ref[pl.ds((sb - 1) * BLK, BLK), :])
            dq_ref[own, :] = jnp.concatenate([_rope(t, cc, sc, lt32, inverse=True) for t in dqs], axis=1).astype(BF)
            dkp_ref[own, :] = jnp.concatenate([_rope(t[:BLK], cp, sp, lt32, inverse=True) for t in dkr],
                                              axis=1).astype(BF)
            dkc_ref[own, :] = jnp.concatenate([_rope(t[BLK:], cc, sc, lt32, inverse=True) for t in dkr],
                                              axis=1).astype(BF)
            dvp_ref[own, :] = jnp.concatenate([t[:BLK] for t in dvr], axis=1).astype(BF)
            dvc_ref[own, :] = jnp.concatenate([t[BLK:] for t in dvr], axis=1).astype(BF)
        dsink_ref[...] += dsink

    kv = pl.BlockSpec((ATT_ROWS, 256), _row)
    tc = pl.BlockSpec((ATT_ROWS, 128), _row)
    tp = pl.BlockSpec((BLK, 128), lambda n: (_before(n), 0))
    return pl.pallas_call(
        body, name=name, grid=(T // ATT_ROWS,),
        in_specs=_attn_specs() + [pl.BlockSpec((N_KV, 4 * ATT_ROWS, 128), lambda n: (0, n, 0)),
                                  pl.BlockSpec((ATT_ROWS, D), _row), pl.BlockSpec((ATT_ROWS, D), _row), tc, tc, tp, tp,
                                  pl.BlockSpec(memory_space=pl.ANY)],
        out_specs=[pl.BlockSpec((ATT_ROWS, D), lambda n: (n, O_Q // D)), kv, kv, kv, kv,
                   pl.BlockSpec((1, 128), _const2)],
        out_shape=[jax.ShapeDtypeStruct(dproj.shape, BF)] + [jax.ShapeDtypeStruct((T, 256), BF)] * 4
        + [jax.ShapeDtypeStruct((1, 128), F32)],
        input_output_aliases={14: 0},
        compiler_params=_cp(("arbitrary",)),
    )(qs, kr, kr, proj, proj, bias, sinks, lse, o, do, cos, sin, cos, sin, dproj)


def _dkv_combine(dkc, dkp, dvc, dvp, dproj, *, name):
    T = dkc.shape[0]
    nb = T // BLK
    tm = _tile(T, 4 * BLK)
    bpt = tm // BLK
    nt = T // tm

    def body(dkc_ref, dkp_ref, dkn_ref, dvc_ref, dvp_ref, dvn_ref, dproj_ref, o_ref):
        keep = jnp.where(pl.program_id(0) == nt - 1, 0.0, 1.0)

        def shifted(prev_ref, next_ref):
            nxt = keep * next_ref[...].astype(F32)
            return nxt if bpt == 1 else jnp.concatenate([prev_ref[BLK:, :].astype(F32), nxt], axis=0)

        o_ref[:, 0:256] = (dkc_ref[...].astype(F32) + shifted(dkp_ref, dkn_ref)).astype(BF)
        o_ref[:, 256:512] = (dvc_ref[...].astype(F32) + shifted(dvp_ref, dvn_ref)).astype(BF)

    cur = pl.BlockSpec((tm, 256), _row)
    nxt = pl.BlockSpec((BLK, 256), lambda i: (jnp.minimum((i + 1) * bpt, nb - 1), 0))
    return pl.pallas_call(
        body, name=name, grid=(nt,),
        in_specs=[cur, cur, nxt, cur, cur, nxt, pl.BlockSpec(memory_space=pl.ANY)],
        out_specs=pl.BlockSpec((tm, 512), lambda i: (i, O_K // 512)),
        out_shape=jax.ShapeDtypeStruct(dproj.shape, BF),
        input_output_aliases={6: 0},
        compiler_params=_cp(("parallel",)),
    )(dkc, dkp, dkp, dvc, dvp, dvp, dproj)


HALO = 16


def _conv_shifts(cu, hprev, tm):
    row = lax.broadcasted_iota(jnp.int32, (8, cu.shape[1]), 0)
    h1 = hprev[HALO - 1:HALO, :]
    h2 = hprev[HALO - 2:HALO - 1, :]
    m1 = pltpu.roll(cu, 1, 0)
    m2 = pltpu.roll(cu, 2, 0)
    m1 = jnp.concatenate([jnp.where(row == 0, h1, m1[0:8]), m1[8:]], axis=0)
    m2 = jnp.concatenate([jnp.where(row == 0, h2, jnp.where(row == 1, h1, m2[0:8])), m2[8:]], axis=0)
    return m1, m2


def _mixer_mid_fwd(proj, attn, wcp, wap, wout, convw, x, gt, *, tm, name):
    T = x.shape[0]
    tm = _tile(T, tm)
    hb = tm // HALO

    def body(bg_ref, cg_ref, u_ref, hcg_ref, hu_ref, zc0_ref, zc1_ref, za0_ref, za1_ref, at_ref,
             wcp_ref, wap_ref, wout_ref, cw_ref, x_ref, gt_ref,
             x2_ref, gc_ref, yc_ref, ya_ref, mg_ref, o_ref):
        first = jnp.where(pl.program_id(0) == 0, 0.0, 1.0)
        cu = cg_ref[...].astype(F32) * u_ref[...].astype(F32)
        hprev = first * (hcg_ref[...].astype(F32) * hu_ref[...].astype(F32))
        m1, m2 = _conv_shifts(cu, hprev, tm)
        cv = cw_ref[0:1, :] * m2 + cw_ref[1:2, :] * m1 + cw_ref[2:3, :] * cu
        gc = (bg_ref[...].astype(F32) * cv).astype(BF)
        gc_ref[...] = gc
        yc = jnp.dot(gc, wcp_ref[...], preferred_element_type=F32)
        ya = jnp.dot(at_ref[...], wap_ref[...], preferred_element_type=F32)
        yc_ref[...] = yc.astype(BF)
        ya_ref[...] = ya.astype(BF)
        zc = jnp.concatenate([zc0_ref[...], zc1_ref[...]], axis=1).astype(F32)
        za = jnp.concatenate([za0_ref[...], za1_ref[...]], axis=1).astype(F32)
        mg = (_sigmoid(zc) * yc + _sigmoid(za) * ya).astype(BF)
        mg_ref[...] = mg
        o = jnp.dot(mg, wout_ref[...], preferred_element_type=F32)
        o_ref[...] = o.astype(BF)
        x2_ref[...] = x_ref[...] + gt_ref[...] * o

    wspec = pl.BlockSpec((D, D), _const2)
    rowspec = pl.BlockSpec((tm, D), _row)
    return pl.pallas_call(
        body, name=name, grid=(T // tm,),
        in_specs=[_col(tm, O_BG), _col(tm, O_CG), _col(tm, O_U), _halo_prev(hb, O_CG), _halo_prev(hb, O_U),
                  _col(tm, O_ZC, 512), _col(tm, O_ZC + 512, 512), _col(tm, O_ZA, 512), _col(tm, O_ZA + 512, 512),
                  rowspec, wspec, wspec, wspec, pl.BlockSpec((8, D), _const2), rowspec, pl.BlockSpec((1, D), _const2)],
        out_specs=[rowspec] * 6,
        out_shape=[jax.ShapeDtypeStruct((T, D), F32)] + [jax.ShapeDtypeStruct((T, D), BF)] * 5,
        compiler_params=_cp(("parallel",)),
    )(proj, proj, proj, proj, proj, proj, proj, proj, proj, attn, wcp, wap, wout, convw, x, gt)


def _col(tm, c, w=D):
    assert c % w == 0
    return pl.BlockSpec((tm, w), lambda i: (i, c // w))


def _halo_prev(hb, c):
    return pl.BlockSpec((HALO, D), lambda i: (jnp.maximum(i * hb - 1, 0), c // D))


def _halo_next(hb, nblk, c=0):
    return pl.BlockSpec((HALO, D), lambda i: (jnp.minimum((i + 1) * hb, nblk - 1), c // D))


def _mixer_mid_bwd(dx2, gt, o, proj, yc, ya, wout, wcp, wap, *, tm, name):
    T = dx2.shape[0]
    tm = _tile(T, tm)
    nt = T // tm

    def body(dx_ref, gt_ref, o_ref, proj_ref, yc_ref, ya_ref, wout_ref, wcp_ref, wap_ref,
             dout_ref, dyc_ref, dya_ref, dgc_ref, dat_ref, dproj_ref, dgt_ref, dzs, sems, zin, zsems):
        i = pl.program_id(0)
        slot = lax.rem(i, 2)

        def slab_copy(step, s):
            return pltpu.make_async_copy(
                dzs.at[s], dproj_ref.at[pl.ds(pl.multiple_of(step * tm, tm), tm), pl.ds(O_ZC, 2 * D)], sems.at[s])

        def z_copy(step, s):
            return pltpu.make_async_copy(
                proj_ref.at[pl.ds(pl.multiple_of(step * tm, tm), tm), pl.ds(O_ZC, 2 * D)], zin.at[s], zsems.at[s])

        @pl.when(i == 0)
        def _():
            dgt_ref[...] = jnp.zeros_like(dgt_ref)
            z_copy(i, slot).start()

        @pl.when(i + 1 < nt)
        def _():
            z_copy(i + 1, 1 - slot).start()

        dxv = dx_ref[...]
        dgt_ref[...] += jnp.sum(dxv * o_ref[...].astype(F32), axis=0, keepdims=True)
        dout = (gt_ref[...] * dxv).astype(BF)
        dout_ref[...] = dout
        dmg = lax.dot_general(dout, wout_ref[...], NT, preferred_element_type=F32)
        z_copy(i, slot).wait()
        sc = _sigmoid(zin[slot, :, 0:D].astype(F32))
        sa = _sigmoid(zin[slot, :, D:2 * D].astype(F32))
        dyc = (dmg * sc).astype(BF)
        dya = (dmg * sa).astype(BF)
        dyc_ref[...] = dyc
        dya_ref[...] = dya
        dzs[slot, :, 0:D] = (dmg * yc_ref[...].astype(F32) * (sc * (1.0 - sc))).astype(BF)
        dzs[slot, :, D:2 * D] = (dmg * ya_ref[...].astype(F32) * (sa * (1.0 - sa))).astype(BF)
        slab_copy(i, slot).start()
        dgc_ref[...] = lax.dot_general(dyc, wcp_ref[...], NT, preferred_element_type=F32).astype(BF)
        dat_ref[...] = lax.dot_general(dya, wap_ref[...], NT, preferred_element_type=F32).astype(BF)

        @pl.when(i > 0)
        def _():
            slab_copy(i - 1, 1 - slot).wait()

        @pl.when(i == nt - 1)
        def _():
            slab_copy(i, slot).wait()

    wspec = pl.BlockSpec((D, D), _const2)
    rowspec = pl.BlockSpec((tm, D), _row)
    vec = pl.BlockSpec((1, D), _const2)
    return pl.pallas_call(
        body, name=name, grid=(nt,),
        in_specs=[rowspec, vec, rowspec, pl.BlockSpec(memory_space=pl.ANY), rowspec, rowspec, wspec, wspec, wspec],
        out_specs=[rowspec] * 5 + [pl.BlockSpec(memory_space=pl.ANY), vec],
        out_shape=[jax.ShapeDtypeStruct((T, D), BF)] * 5 + [jax.ShapeDtypeStruct((T, NIN), BF),
                                                            jax.ShapeDtypeStruct((1, D), F32)],
        scratch_shapes=[pltpu.VMEM((2, tm, 2 * D), BF), pltpu.SemaphoreType.DMA((2,)),
                        pltpu.VMEM((2, tm, 2 * D), BF), pltpu.SemaphoreType.DMA((2,))],
        compiler_params=_cp(("arbitrary",)),
    )(dx2, gt, o, proj, yc, ya, wout, wcp, wap)


def _conv_bwd(dgc, proj, convw, dproj, *, tm, name):
    T = dgc.shape[0]
    tm = _tile(T, tm)
    hb = tm // HALO
    nblk = T // HALO
    nt = T // tm

    def body(dgc_ref, ndgc_ref, bg_ref, nbg_ref, cg_ref, u_ref, hcg_ref, hu_ref, cw_ref, dproj_ref, dp_ref, dcw_ref):
        i = pl.program_id(0)

        @pl.when(i == 0)
        def _():
            dcw_ref[...] = jnp.zeros_like(dcw_ref)
        first = jnp.where(i == 0, 0.0, 1.0)
        last = jnp.where(i == nt - 1, 0.0, 1.0)
        cg = cg_ref[...].astype(F32)
        u = u_ref[...].astype(F32)
        bg = bg_ref[...].astype(F32)
        dg = dgc_ref[...].astype(F32)
        cu = cg * u
        hprev = first * (hcg_ref[...].astype(F32) * hu_ref[...].astype(F32))
        m1, m2 = _conv_shifts(cu, hprev, tm)
        w0, w1, w2 = cw_ref[0:1, :], cw_ref[1:2, :], cw_ref[2:3, :]
        cv = w0 * m2 + w1 * m1 + w2 * cu
        dcv = dg * bg
        nxt = last * (ndgc_ref[...].astype(F32) * nbg_ref[...].astype(F32))
        n0, n1 = nxt[0:1, :], nxt[1:2, :]
        row = lax.broadcasted_iota(jnp.int32, (8, D), 0)
        p1 = pltpu.roll(dcv, tm - 1, 0)
        p2 = pltpu.roll(dcv, tm - 2, 0)
        p1 = jnp.concatenate([p1[:tm - 8], jnp.where(row == 7, n0, p1[tm - 8:])], axis=0)
        p2 = jnp.concatenate([p2[:tm - 8], jnp.where(row == 7, n1, jnp.where(row == 6, n0, p2[tm - 8:]))], axis=0)
        dcu = w2 * dcv + w1 * p1 + w0 * p2
        dp_ref[:, 0:D] = (dg * cv).astype(BF)
        dp_ref[:, D:2 * D] = (dcu * u).astype(BF)
        dp_ref[:, 2 * D:3 * D] = (dcu * cg).astype(BF)
        dcw_ref[0:1, :] += jnp.sum(dcv * m2, axis=0, keepdims=True)
        dcw_ref[1:2, :] += jnp.sum(dcv * m1, axis=0, keepdims=True)
        dcw_ref[2:3, :] += jnp.sum(dcv * cu, axis=0, keepdims=True)

    rowspec = pl.BlockSpec((tm, D), _row)
    cw = pl.BlockSpec((8, D), _const2)
    return pl.pallas_call(
        body, name=name, grid=(nt,),
        in_specs=[rowspec, _halo_next(hb, nblk), _col(tm, O_BG), _halo_next(hb, nblk, O_BG),
                  _col(tm, O_CG), _col(tm, O_U), _halo_prev(hb, O_CG), _halo_prev(hb, O_U), cw,
                  pl.BlockSpec(memory_space=pl.ANY)],
        out_specs=[pl.BlockSpec((tm, 3 * D), _row), cw],
        out_shape=[jax.ShapeDtypeStruct(dproj.shape, BF), jax.ShapeDtypeStruct((8, D), F32)],
        input_output_aliases={9: 0},
        compiler_params=_cp(("arbitrary",)),
    )(dgc, dgc, proj, proj, proj, proj, proj, proj, convw, dproj)


def _adam_math(w, g, m, v):
    nm = ADAM_B1 * m + (1.0 - ADAM_B1) * g
    nv = ADAM_B2 * v + (1.0 - ADAM_B2) * (g * g)
    m_hat = nm / (1.0 - ADAM_B1 ** ADAM_STEP)
    v_hat = nv / (1.0 - ADAM_B2 ** ADAM_STEP)
    return -ADAM_LR * (m_hat / (jnp.sqrt(v_hat) + ADAM_EPS) + ADAM_WD * w), nm, nv


SMALL = ("b_ada", "g_ffn1", "g_mix", "g_ffn2", "g_final", "conv_w", "sinks")


def _adam_small(gsum, conv_g, w, m, v, *, name):
    nsm = len(SMALL)

    def body(*refs):
        gs_ref, cg_ref = refs[0], refs[1]
        w_refs, m_refs, v_refs = (refs[2 + k * nsm:2 + (k + 1) * nsm] for k in range(3))
        outs = refs[2 + 3 * nsm:]
        for p, n in enumerate(SMALL):
            if n == "b_ada":
                pieces = [(slice(None), slice(r * D, (r + 1) * D), gs_ref[R_MODS + r:R_MODS + r + 1, :])
                          for r in range(N_MOD)]
            elif n == "conv_w":
                pieces = [(slice(None), slice(None), cg_ref[...])]
            elif n == "sinks":
                pieces = [(slice(None), slice(None), gs_ref[R_SINK:R_SINK + 1, 0:N_HEADS])]
            else:
                row = dict(g_ffn1=R_G1, g_mix=R_GM, g_ffn2=R_G2, g_final=R_GF)[n]
                pieces = [(slice(None), slice(None), gs_ref[row:row + 1, :])]
            for rs, cs, g in pieces:
                d, nm, nv = _adam_math(w_refs[p][rs, cs], g, m_refs[p][rs, cs], v_refs[p][rs, cs])
                for k, val in enumerate((g, d, nm, nv)):
                    outs[k * nsm + p][rs, cs] = val

    args = [gsum, conv_g] + [d[n] for d in (w, m, v) for n in SMALL]
    shapes = [jax.ShapeDtypeStruct(w[n].shape, F32) for _ in range(4) for n in SMALL]
    res = pl.pallas_call(body, name=name, out_shape=shapes, compiler_params=_cp())(*args)
    return [dict(zip(SMALL, res[k * nsm:(k + 1) * nsm])) for k in range(4)]


def _adam(w, g, m, v, *, tm, name):
    _, R, C = w.shape
    tm = _tile(R, tm)
    parts = g.ndim == 3

    def body(w_ref, g_ref, m_ref, v_ref, go_ref, d_ref, nm_ref, nv_ref):
        if parts:
            gv = g_ref[0].astype(F32)
            for s in range(1, N_DEV):
                gv = gv + g_ref[s].astype(F32)
        else:
            gv = g_ref[...]
        go_ref[0] = gv
        d_ref[0], nm_ref[0], nv_ref[0] = _adam_math(w_ref[0], gv, m_ref[0], v_ref[0])

    spec = pl.BlockSpec((1, tm, C), lambda i: (0, i, 0))
    gspec = pl.BlockSpec((N_DEV, tm, C), lambda i: (0, i, 0)) if parts else pl.BlockSpec((tm, C), _row)
    return pl.pallas_call(
        body, name=name, grid=(R // tm,),
        in_specs=[spec, gspec, spec, spec], out_specs=[spec] * 4,
        out_shape=[jax.ShapeDtypeStruct((1, R, C), F32)] * 4,
        compiler_params=_cp(("parallel",)),
    )(w, g, m, v)


def _mods_part(c_all, w_ada, b_ada, *, name):
    C = w_ada.shape[1]

    def body(c_ref, w_ref, b_ref, o_ref):
        cv = c_ref[...]
        ca = cv * jax.nn.sigmoid(cv)
        o_ref[...] = jnp.dot(ca, w_ref[...], preferred_element_type=F32,
                             precision=lax.Precision.HIGHEST) + b_ref[...]

    return pl.pallas_call(
        body, name=name,
        out_shape=jax.ShapeDtypeStruct((N_DEV, C), F32),
        compiler_params=_cp(),
    )(c_all, w_ada, b_ada)


def _wada_grad(c_all_t, gm, *, name):
    C = gm.shape[1]

    def body(c_ref, g_ref, o_ref):
        cv = c_ref[...]
        ca = cv * jax.nn.sigmoid(cv)
        acc = ca[:, 0:1] * g_ref[0:1, :]
        for b in range(1, N_DEV):
            acc = acc + ca[:, b:b + 1] * g_ref[b:b + 1, :]
        o_ref[...] = acc

    return pl.pallas_call(
        body, name=name,
        out_shape=jax.ShapeDtypeStruct((D, C), F32),
        compiler_params=_cp(),
    )(c_all_t, gm)


def _peer(x, y, c, d):
    px = lax.rem(x + ((d >> 2) & 1), 2)
    py = lax.rem(y + ((d >> 1) & 1), 2)
    pc = lax.rem(c + (d & 1), 2)
    return (px, py, pc), 4 * px + 2 * py + pc


def _exchange(xs, *, scatter, name):
    n = len(xs)
    nsem = n * (N_DEV - 1)

    def body(*refs):
        ins, outs = refs[:n], refs[n:2 * n]
        token, send_sems, recv_sems, local_sems = refs[2 * n:]
        x, y, c = lax.axis_index("x"), lax.axis_index("y"), lax.axis_index("c")
        me = 4 * x + 2 * y + c
        token[...] = jnp.zeros_like(token)

        def src(t, idx):
            return ins[t].at[idx] if scatter else ins[t]

        local = [pltpu.make_async_copy(src(t, me), outs[t].at[me], local_sems.at[t]) for t in range(n)]
        for cp in local:
            cp.start()
        remote = []
        for t in range(n):
            for d in range(1, N_DEV):
                peer, pidx = _peer(x, y, c, d)
                k = t * (N_DEV - 1) + d - 1
                send = pltpu.make_async_remote_copy(src_ref=src(t, pidx), dst_ref=outs[t].at[me],
                                                    send_sem=send_sems.at[k], recv_sem=recv_sems.at[k],
                                                    device_id=peer, device_id_type=MESH)
                recv = pltpu.make_async_remote_copy(src_ref=src(t, pidx), dst_ref=outs[t].at[pidx],
                                                    send_sem=send_sems.at[k], recv_sem=recv_sems.at[k],
                                                    device_id=peer, device_id_type=MESH)
                send.start()
                remote.append((send, recv))
        for cp in local:
            cp.wait()
        for send, recv in remote:
            send.wait_send()
            recv.wait_recv()

    anyspec = pl.BlockSpec(memory_space=pl.ANY)
    out_shape = [jax.ShapeDtypeStruct(a.shape if scatter else (N_DEV,) + a.shape, a.dtype) for a in xs]
    out_shape.append(jax.ShapeDtypeStruct((8, 128), F32))
    return pl.pallas_call(
        body, name=name,
        in_specs=[anyspec] * n, out_specs=[anyspec] * n + [pl.BlockSpec(memory_space=pltpu.VMEM)],
        out_shape=out_shape,
        scratch_shapes=[pltpu.SemaphoreType.DMA((nsem,)), pltpu.SemaphoreType.DMA((nsem,)),
                        pltpu.SemaphoreType.DMA((n,))],
    )(*xs)


def _sum8(parts, *, name):
    _, R, C = parts.shape

    def body(p_ref, o_ref):
        acc = p_ref[0]
        for s in range(1, N_DEV):
            acc = acc + p_ref[s]
        o_ref[...] = acc

    return pl.pallas_call(body, name=name, out_shape=jax.ShapeDtypeStruct((R, C), F32),
                          compiler_params=_cp())(parts)


HBM_SPEC = pl.BlockSpec(memory_space=pltpu.HBM)
SEM_SPEC = pl.BlockSpec(memory_space=pltpu.SEMAPHORE)
N_PEER = N_DEV - 1


def _split_copies(src_refs, land_refs, send_sems, recv_sems, scatter):
    x, y, c = lax.axis_index("x"), lax.axis_index("y"), lax.axis_index("c")
    me = 4 * x + 2 * y + c
    pairs = []
    for j, (src, land) in enumerate(zip(src_refs, land_refs)):
        for d in range(1, N_DEV):
            peer, pidx = _peer(x, y, c, d)
            k = j * N_PEER + d - 1
            s = src.at[pidx] if scatter else src
            send = pltpu.make_async_remote_copy(src_ref=s, dst_ref=land.at[me], send_sem=send_sems.at[k],
                                                recv_sem=recv_sems.at[k], device_id=peer, device_id_type=MESH)
            recv = pltpu.make_async_remote_copy(src_ref=s, dst_ref=land.at[pidx], send_sem=send_sems.at[k],
                                                recv_sem=recv_sems.at[k], device_id=peer, device_id_type=MESH)
            pairs.append((send, recv))
    return pairs


def _own_slot(block, me):
    land = lax.empty((N_DEV,) + block.shape, block.dtype)
    return lax.dynamic_update_slice(land, block[None], (me, 0, 0))


def _split_start(srcs, lands, groups, *, scatter, name):
    n, ng = len(srcs), len(groups)

    def body(*refs):
        src_refs, land_refs = refs[:n], refs[n:2 * n]
        sems = refs[2 * n:2 * n + 2 * ng]
        token = refs[-1]
        for gi, g in enumerate(groups):
            pairs = _split_copies([src_refs[t] for t in g], [land_refs[t] for t in g], sems[2 * gi],
                                  sems[2 * gi + 1], scatter)
            for send, _ in pairs:
                send.start()
        token[...] = jnp.zeros_like(token)

    sem_shapes = []
    for g in groups:
        sem_shapes += [pltpu.SemaphoreType.DMA((len(g) * N_PEER,))] * 2
    thru = [pltpu.HBM(a.shape, a.dtype) for a in list(srcs) + list(lands)]
    outs = pl.pallas_call(
        body, name=name,
        out_shape=tuple(sem_shapes + thru + [jax.ShapeDtypeStruct((8, 128), F32)]),
        in_specs=[HBM_SPEC] * (2 * n),
        out_specs=tuple([SEM_SPEC] * (2 * ng) + [HBM_SPEC] * (2 * n) + [pl.BlockSpec(memory_space=pltpu.VMEM)]),
        input_output_aliases={i: 2 * ng + i for i in range(2 * n)},
        compiler_params=pltpu.CompilerParams(has_side_effects=pltpu.SideEffectType.DATAFLOW_SIDE_EFFECTING),
    )(*[pltpu.with_memory_space_constraint(a, pltpu.HBM) for a in list(srcs) + list(lands)])
    sems = [(outs[2 * gi], outs[2 * gi + 1]) for gi in range(ng)]
    return sems, outs[2 * ng:2 * ng + n], outs[2 * ng + n:2 * ng + 2 * n], outs[-1]


def _behind(v, token):
    if token is None:
        return v
    return v + token[0, 0].astype(v.dtype)


def _split_wait(srcs, lands, sems, after, *, scatter, name):
    m = len(srcs)

    def body(*refs):
        src_refs, land_refs = refs[:m], refs[m:2 * m]
        send_sems, recv_sems = refs[2 * m], refs[2 * m + 1]
        for send, recv in _split_copies(src_refs, land_refs, send_sems, recv_sems, scatter):
            send.wait_send()
            recv.wait_recv()

    outs = pl.pallas_call(
        body, name=name,
        out_shape=tuple(pltpu.HBM(a.shape, a.dtype) for a in list(srcs) + list(lands)),
        in_specs=[HBM_SPEC] * (2 * m) + [SEM_SPEC, SEM_SPEC, pl.BlockSpec(memory_space=pl.ANY)],
        out_specs=tuple([HBM_SPEC] * (2 * m)),
        input_output_aliases={i: i for i in range(2 * m)},
        compiler_params=pltpu.CompilerParams(has_side_effects=pltpu.SideEffectType.DATAFLOW_SIDE_EFFECTING),
    )(*srcs, *lands, sems[0], sems[1], after)
    return outs[m:]


TL_FIRST = (1, 2, 4, 6)
TL_ICI = (2, 4, 6)
EFFECT = pltpu.SideEffectType.DATAFLOW_SIDE_EFFECTING


def _tl_first(src_refs, land_refs, send_sems, recv_sems):
    x, y, c = lax.axis_index("x"), lax.axis_index("y"), lax.axis_index("c")
    me = 4 * x + 2 * y + c
    out = []
    for j, (src, land) in enumerate(zip(src_refs, land_refs)):
        for i, d in enumerate(TL_FIRST):
            peer, pidx = _peer(x, y, c, d)
            k = len(TL_FIRST) * j + i
            send = pltpu.make_async_remote_copy(src_ref=src, dst_ref=land.at[me], send_sem=send_sems.at[k],
                                                recv_sem=recv_sems.at[k], device_id=peer, device_id_type=MESH)
            recv = pltpu.make_async_remote_copy(src_ref=src, dst_ref=land.at[pidx], send_sem=send_sems.at[k],
                                                recv_sem=recv_sems.at[k], device_id=peer, device_id_type=MESH)
            out.append((d, send, recv))
    return out


def _tl_second(land_refs, send_sems, recv_sems):
    x, y, c = lax.axis_index("x"), lax.axis_index("y"), lax.axis_index("c")
    sibling, _ = _peer(x, y, c, 1)
    out = []
    for j, land in enumerate(land_refs):
        for i, d in enumerate(TL_ICI):
            _, mine = _peer(x, y, c, d)
            _, theirs = _peer(x, y, c, d + 1)
            k = len(TL_ICI) * j + i
            send = pltpu.make_async_remote_copy(src_ref=land.at[mine], dst_ref=land.at[mine], send_sem=send_sems.at[k],
                                                recv_sem=recv_sems.at[k], device_id=sibling, device_id_type=MESH)
            recv = pltpu.make_async_remote_copy(src_ref=land.at[mine], dst_ref=land.at[theirs],
                                                send_sem=send_sems.at[k], recv_sem=recv_sems.at[k],
                                                device_id=sibling, device_id_type=MESH)
            out.append((send, recv))
    return out


def _tl_start(srcs, lands, groups, *, name):
    n, ng = len(srcs), len(groups)

    def body(*refs):
        src_refs, land_refs = refs[:n], refs[n:2 * n]
        sems = refs[2 * n:2 * n + 2 * ng]
        for gi, g in enumerate(groups):
            for _, send, _ in _tl_first([src_refs[t] for t in g], [land_refs[t] for t in g], sems[2 * gi],
                                        sems[2 * gi + 1]):
                send.start()
        refs[-1][...] = jnp.zeros_like(refs[-1])

    sem_shapes = []
    for g in groups:
        sem_shapes += [pltpu.SemaphoreType.DMA((len(g) * len(TL_FIRST),))] * 2
    thru = [pltpu.HBM(a.shape, a.dtype) for a in list(srcs) + list(lands)]
    outs = pl.pallas_call(
        body, name=name,
        out_shape=tuple(sem_shapes + thru + [jax.ShapeDtypeStruct((8, 128), F32)]),
        in_specs=[HBM_SPEC] * (2 * n),
        out_specs=tuple([SEM_SPEC] * (2 * ng) + [HBM_SPEC] * (2 * n) + [pl.BlockSpec(memory_space=pltpu.VMEM)]),
        input_output_aliases={i: 2 * ng + i for i in range(2 * n)},
        compiler_params=pltpu.CompilerParams(has_side_effects=EFFECT),
    )(*[pltpu.with_memory_space_constraint(a, pltpu.HBM) for a in list(srcs) + list(lands)])
    sems = [(outs[2 * gi], outs[2 * gi + 1]) for gi in range(ng)]
    return sems, outs[2 * ng:2 * ng + n], outs[2 * ng + n:2 * ng + 2 * n], outs[-1]


def _tl_forward(srcs, lands, sems1, after, *, name):
    m = len(srcs)

    def body(*refs):
        src_refs, land_refs = refs[:m], refs[m:2 * m]
        send1, recv1 = refs[2 * m], refs[2 * m + 1]
        send2, recv2 = refs[2 * m + 3], refs[2 * m + 4]
        for d, _, recv in _tl_first(src_refs, land_refs, send1, recv1):
            if d in TL_ICI:
                recv.wait_recv()
        for send, _ in _tl_second(land_refs, send2, recv2):
            send.start()

    sem = pltpu.SemaphoreType.DMA((m * len(TL_ICI),))
    outs = pl.pallas_call(
        body, name=name,
        out_shape=tuple([sem, sem] + [pltpu.HBM(a.shape, a.dtype) for a in list(srcs) + list(lands)]),
        in_specs=[HBM_SPEC] * (2 * m) + [SEM_SPEC, SEM_SPEC, pl.BlockSpec(memory_space=pl.ANY)],
        out_specs=tuple([SEM_SPEC, SEM_SPEC] + [HBM_SPEC] * (2 * m)),
        input_output_aliases={i: 2 + i for i in range(2 * m)},
        compiler_params=pltpu.CompilerParams(has_side_effects=EFFECT),
    )(*srcs, *lands, sems1[0], sems1[1], after)
    return (outs[0], outs[1]), outs[2:2 + m], outs[2 + m:2 + 2 * m]


def _tl_wait(srcs, lands, sems1, sems2, after, *, name):
    m = len(srcs)

    def body(*refs):
        src_refs, land_refs = refs[:m], refs[m:2 * m]
        send1, recv1, send2, recv2 = refs[2 * m:2 * m + 4]
        for d, send, recv in _tl_first(src_refs, land_refs, send1, recv1):
            send.wait_send()
            if d not in TL_ICI:
                recv.wait_recv()
        for send, recv in _tl_second(land_refs, send2, recv2):
            send.wait_send()
            recv.wait_recv()

    outs = pl.pallas_call(
        body, name=name,
        out_shape=tuple(pltpu.HBM(a.shape, a.dtype) for a in list(srcs) + list(lands)),
        in_specs=[HBM_SPEC] * (2 * m) + [SEM_SPEC] * 4 + [pl.BlockSpec(memory_space=pl.ANY)],
        out_specs=tuple([HBM_SPEC] * (2 * m)),
        input_output_aliases={i: i for i in range(2 * m)},
        compiler_params=pltpu.CompilerParams(has_side_effects=EFFECT),
    )(*srcs, *lands, sems1[0], sems1[1], sems2[0], sems2[1], after)
    return outs[m:]


TM_PROJ = 512
TN_PROJ = 512
TM_ROW = 512
TM_NN = 512
TK_TN = 2048
TM_ADAM = 208
TN_FFN = F // 2
TN_IN = NIN // 4


def _tn(a, b, name, tn, token=None):
    if a.ndim == 2:
        a = a[None]
    return _tn_matmul(a, b, token, tn=tn, tk=TK_TN, name=name)


def _local_step(x, tgt, mods, g1, gm, g2, gf, convw8, sinks, w_get, g_put, tables=None):
    T = x.shape[0]
    sh1, sc1, gt1, sh2, sc2, gt2, sh3, sc3, gt3 = [mods[i:i + 1] for i in range(N_MOD)]
    cos, sin = _rope_tables(T) if tables is None else tables
    behind = _behind

    w = dict(w_get("gu1", mods))
    h1, ab1 = _norm_proj(x, g1, sc1, sh1, w["gu1"], tm=TM_PROJ, tn=TN_PROJ, name="ffn1_up")
    w.update(w_get("d1", ab1))
    x1 = _ffn_down_fwd(ab1, w["d1"], x, gt1, tm=TM_ROW, name="ffn1_down")
    w.update(w_get("mix", x1))
    h2, proj, qs, kr = _norm_proj(x1, gm, sc2, sh2, w["win"], (cos, sin), tm=TM_PROJ, tn=TN_PROJ, name="mix_in")
    bias = _attn_bias()
    attn, lse = _attn_fwd(qs, kr, proj, bias, sinks, name="attn_fwd")
    x2, gc, yc, ya, mg, o = _mixer_mid_fwd(proj, attn, w["cp"], w["ap"], w["out"], convw8, x1, gt2,
                                           tm=TM_ROW, name="mix_mid")
    w.update(w_get("ffn2", x2))
    h3, ab2, dx3, lsum, dgf = _ffn_fwd(x2, g2, sc3, sh3, gt3, w["gu2"], w["d2"], (tgt, gf), tm=TM_ROW,
                                           name="ffn2_final")

    dab2, dgt3, g_d2 = _ffn_down_bwd_dw(dx3, gt3, ab2, w["d2"], tm=TM_ROW, name="ffn2_down_bwd")
    dx2, dsh3, dsc3, dg2 = _nn_bwd_norm(dab2, w["gu2"], x2, g2, sc3, dx3, tm=TM_NN, name="ffn2_up_bwd")
    g_gu2 = _tn(dab2, h3, "ffn2_up_dw", TN_FFN)
    tok = g_put(dict(gu2=g_gu2, d2=g_d2))

    dout, dyc, dya, dgc, dat, dproj, dgt2 = _mixer_mid_bwd(dx2, behind(gt2, tok), o, proj, yc, ya, w["out"], w["cp"],
                                                           w["ap"], tm=TM_ROW, name="mix_mid_bwd")
    g_out = _tn(mg, dout, "mix_out_dw", D)
    g_cp = _tn(gc, dyc, "mix_cp_dw", D)
    g_ap = _tn(attn, dya, "mix_ap_dw", D)
    dproj, dkc, dkp, dvc, dvp, dsink = _attn_bwd(qs, kr, proj, bias, sinks, lse, attn, dat, cos, sin, dproj,
                                                 name="attn_bwd")
    dproj = _dkv_combine(dkc, dkp, dvc, dvp, dproj, name="attn_dkv")
    dproj, dcw = _conv_bwd(dgc, proj, convw8, dproj, tm=TM_ROW, name="conv_bwd")
    g_in = _tn(dproj, h2, "mix_in_dw", TN_IN)
    tok = g_put(dict(win=g_in, cp=g_cp, ap=g_ap, out=g_out))
    dx1, dsh2, dsc2, dgm = _nn_bwd_norm(dproj[None], w["win"], x1, gm, behind(sc2, tok), dx2, tm=TM_NN,
                                        name="mix_in_bwd")

    dab1, dgt1, g_d1 = _ffn_down_bwd_dw(dx1, gt1, ab1, w["d1"], tm=TM_ROW, name="ffn1_down_bwd")
    tok = g_put(dict(d1=g_d1))
    g_gu1 = _tn(dab1, h1, "ffn1_up_dw", TN_FFN, tok)
    tok = g_put(dict(gu1=g_gu1))
    dx0, dsh1, dsc1, dg1 = _nn_bwd_norm(dab1, w["gu1"], x, g1, behind(sc1, tok), dx1, tm=TM_NN,
                                        name="ffn1_up_bwd")

    small = dict(mods=jnp.concatenate([dsh1, dsc1, dgt1, dsh2, dsc2, dgt2, dsh3, dsc3, dgt3], axis=0),
                 g1=dg1, gm=dgm, g2=dg2, gf=dgf, convw=dcw[0:3], sinks=dsink[:, 0:N_HEADS])
    return lsum, dx0, small


BIG = ("gu1", "d1", "win", "cp", "ap", "out", "gu2", "d2")
TRANSPOSED = ("gu1", "win", "gu2")
SMALL_ROWS = 24
R_MODS, R_G1, R_GM, R_G2, R_GF, R_CONV, R_SINK, R_LOSS = 0, 9, 10, 11, 12, 13, 16, 17


def _pad_to(a, rows, cols):
    return jnp.pad(a, ((0, rows - a.shape[0]), (0, cols - a.shape[1])))


def _pack_small(b_ada, g1, gm, g2, gf, conv, sinks, lsum):
    rows = [b_ada.reshape(N_MOD, D), g1.reshape(1, D), gm.reshape(1, D), g2.reshape(1, D), gf.reshape(1, D),
            _pad_to(conv.reshape(3, -1), 3, D), _pad_to(sinks.reshape(1, N_HEADS), 1, D), lsum.reshape(1, D)]
    return _pad_to(jnp.concatenate(rows, axis=0), SMALL_ROWS, D)


def kernel(x, c, w_ada, b_ada, g_ffn1, w1_gu, w1_down, g_mix, w_in, conv_w, w_conv_proj, w_attn_proj, sinks, w_out, g_ffn2, w2_gu, w2_down, g_final, loss_target, m_w_ada, m_b_ada, m_g_ffn1, m_w1_gu, m_w1_down, m_g_mix, m_w_in, m_conv_w, m_w_conv_proj, m_w_attn_proj, m_sinks, m_w_out, m_g_ffn2, m_w2_gu, m_w2_down, m_g_final, v_w_ada, v_b_ada, v_g_ffn1, v_w1_gu, v_w1_down, v_g_mix, v_w_in, v_conv_w, v_w_conv_proj, v_w_attn_proj, v_sinks, v_w_out, v_g_ffn2, v_w2_gu, v_w2_down, v_g_final):
    me = 4 * lax.axis_index("x") + 2 * lax.axis_index("y") + lax.axis_index("c")
    ada_cols = w_ada.shape[2]
    conv_cols = conv_w.shape[2]

    native = dict(gu1=w1_gu[0], d1=w1_down[0], win=w_in[0], cp=w_conv_proj[0], ap=w_attn_proj[0], out=w_out[0],
                  gu2=w2_gu[0], d2=w2_down[0])

    def shard(n, token):
        a = _behind(native[n], token)
        return (a.T if n in TRANSPOSED else a).astype(BF)

    c_all, conv_all, _ = _exchange([c, _pad_to(conv_w[0], 8, conv_cols)], scatter=False, name="gather_cond")
    c_all = c_all.reshape(N_DEV, D)
    conv_full = conv_all[:, 0:3, :].transpose(1, 0, 2).reshape(3, D)

    b_cols = lax.dynamic_slice(b_ada, (0, me * ada_cols), (1, ada_cols))
    mods_cols = _mods_part(c_all, w_ada[0], b_cols, name="ada_mods")
    mods_all, mods_token = _exchange([mods_cols], scatter=False, name="gather_mods")
    mods = lax.dynamic_index_in_dim(mods_all, me, axis=1, keepdims=False).reshape(N_MOD, D)

    groups = dict(gu1=("gu1",), d1=("d1",), mix=("win", "cp", "ap", "out"), ffn2=("gu2", "d2"))
    in_flight = {}
    first = [shard("gu1", mods_token)]
    sems, srcs, lands, token = _tl_start(first, [_own_slot(s, me) for s in first], [[0]],
                                         name="gather_weights_start_gu1")
    in_flight["gu1"] = [sems[0], srcs, lands, None]
    rest = [n for n in BIG if n != "gu1"]
    shards = [shard(n, token) for n in rest]
    rest_groups = [[rest.index(n) for n in names] for g, names in groups.items() if g != "gu1"]
    sems, srcs, lands, rest_token = _tl_start(shards, [_own_slot(s, me) for s in shards], rest_groups,
                                              name="gather_weights_start_rest")
    for (g, names), gsems, idx in zip([kv for kv in groups.items() if kv[0] != "gu1"], sems, rest_groups):
        in_flight[g] = [gsems, [srcs[t] for t in idx], [lands[t] for t in idx], None]

    def forward(group, after):
        sems1, gsrcs, glands, _ = in_flight[group]
        sems2, gsrcs, glands = _tl_forward(gsrcs, glands, sems1, after, name="gather_weights_forward_" + group)
        in_flight[group] = [sems1, gsrcs, glands, sems2]

    forward_early = dict(d1="mix", mix="ffn2")

    tables = _rope_tables(x.shape[1], rest_token)

    def w_get(group, after):
        if group == "gu1":
            after = tables[0]
        if in_flight[group][3] is None:
            forward(group, after)
        sems1, gsrcs, glands, sems2 = in_flight[group]
        landed = _tl_wait(gsrcs, glands, sems1, sems2, after, name="gather_weights_wait_" + group)
        if group in forward_early:
            forward(forward_early[group], landed[0])
        return {n: a.reshape(-1, D) for n, a in zip(groups[group], landed)}

    pending = []

    def g_put(gs):
        names = tuple(gs)
        srcs = [gs[n].reshape(N_DEV, -1, D) for n in names]
        lands = [_own_slot(lax.dynamic_index_in_dim(s, me, axis=0, keepdims=False), me) for s in srcs]
        sems, srcs, lands, tok = _split_start(srcs, lands, [list(range(len(names)))], scatter=True,
                                              name="scatter_grads_start_" + names[0])
        pending.append((names, sems[0], srcs, lands))
        return tok

    lsum, grad_x, small = _local_step(x[0], loss_target[0], mods, g_ffn1, g_mix, g_ffn2, g_final[None],
                                      _pad_to(conv_full, 8, D), sinks[0], w_get, g_put, tables)

    packed = _pack_small(small["mods"], small["g1"], small["gm"], small["g2"], small["gf"], small["convw"],
                         small["sinks"], lsum)
    sm_sems, sm_srcs, sm_lands, sm_token = _split_start([packed], [_own_slot(packed, me)], [[0]], scatter=False,
                                                        name="gather_small_start")

    w_of = dict(ada=w_ada, gu1=w1_gu, d1=w1_down, win=w_in, cp=w_conv_proj, ap=w_attn_proj, out=w_out, gu2=w2_gu,
                d2=w2_down)
    m_of = dict(ada=m_w_ada, gu1=m_w1_gu, d1=m_w1_down, win=m_w_in, cp=m_w_conv_proj, ap=m_w_attn_proj, out=m_w_out,
                gu2=m_w2_gu, d2=m_w2_down)
    v_of = dict(ada=v_w_ada, gu1=v_w1_gu, d1=v_w1_down, win=v_w_in, cp=v_w_conv_proj, ap=v_w_attn_proj, out=v_w_out,
                gu2=v_w2_gu, d2=v_w2_down)
    upd = {}
    after = sm_token
    for k, (names, sems, srcs, lands) in enumerate(pending):
        if k == 2:
            (packed_all,) = _split_wait(sm_srcs, sm_lands, sm_sems[0], after, scatter=False, name="gather_small_wait")
            gsmall = _sum8(packed_all, name="sum_small")
            loss = (0.5 / D) * jnp.sum(gsmall[R_LOSS])
            after = gsmall
        parts = _split_wait(srcs, lands, sems, after, scatter=True, name="scatter_grads_wait_" + names[0])
        for n, p in zip(names, parts):
            if n in TRANSPOSED:
                res = _adam(jnp.swapaxes(w_of[n], 1, 2), p, jnp.swapaxes(m_of[n], 1, 2), jnp.swapaxes(v_of[n], 1, 2),
                            tm=TM_ADAM, name="adam_" + n)
                upd[n] = [jnp.swapaxes(t, 1, 2) for t in res]
            else:
                upd[n] = _adam(w_of[n], p, m_of[n], v_of[n], tm=TM_ADAM, name="adam_" + n)
        after = upd[names[-1]][1]

    gm_cols = lax.dynamic_slice(packed_all[:, R_MODS:R_MODS + N_MOD, :].reshape(N_DEV, N_MOD * D),
                                (0, me * ada_cols), (N_DEV, ada_cols))
    upd["ada"] = _adam(w_ada, _wada_grad(c_all.T, gm_cols, name="ada_dw"), m_w_ada, v_w_ada, tm=256, name="adam_ada")
    conv_g = lax.dynamic_slice(gsmall, (R_CONV, me * conv_cols), (3, conv_cols))

    def natural(b, g1, gm, g2, gf, cw, sk):
        return dict(b_ada=b, g_ffn1=g1, g_mix=gm, g_ffn2=g2, g_final=gf[None], conv_w=cw[0], sinks=sk)

    small_out = _adam_small(gsmall, conv_g, natural(b_ada, g_ffn1, g_mix, g_ffn2, g_final, conv_w, sinks),
                            natural(m_b_ada, m_g_ffn1, m_g_mix, m_g_ffn2, m_g_final, m_conv_w, m_sinks),
                            natural(v_b_ada, v_g_ffn1, v_g_mix, v_g_ffn2, v_g_final, v_conv_w, v_sinks),
                            name="adam_small")
    for res in small_out:
        res["g_final"] = res["g_final"][0]
        res["conv_w"] = res["conv_w"][None]

    big_name = dict(w_ada="ada", w1_gu="gu1", w1_down="d1", w_in="win", w_conv_proj="cp", w_attn_proj="ap",
                    w_out="out", w2_gu="gu2", w2_down="d2")
    order = ("w_ada", "b_ada", "g_ffn1", "w1_gu", "w1_down", "g_mix", "w_in", "conv_w", "w_conv_proj", "w_attn_proj",
             "sinks", "w_out", "g_ffn2", "w2_gu", "w2_down", "g_final")
    outs = [loss, grad_x[None]]
    for kind in range(4):
        for n in order:
            outs.append(upd[big_name[n]][kind] if n in big_name else small_out[kind][n])
    return tuple(outs)
```
